```python
import jax, jax.numpy as jnp
from jax import lax
import numpy as np

D_MODEL = 1024
BATCH = 8
SEQ = 2048
DEPTH = 1

GDN_HEADS = 8
GDN_HEAD_DIM = 64
FOX_HEADS = 8
FOX_HEAD_DIM = 64
GDN_WIDTH = GDN_HEADS * GDN_HEAD_DIM
FOX_WIDTH = FOX_HEADS * FOX_HEAD_DIM
D_MIX = GDN_WIDTH + FOX_WIDTH
CONV_K = 4
CHUNK = 64
Q_BLOCK = 128
D_FF = -(-8 * D_MODEL // (3 * 256)) * 256
EPS = 1e-6

SPLIT_SIZES = [
    GDN_WIDTH, GDN_WIDTH, GDN_WIDTH,
    GDN_WIDTH,
    GDN_HEADS, GDN_HEADS,
    FOX_WIDTH, FOX_WIDTH, FOX_WIDTH,
    FOX_WIDTH,
    FOX_HEADS,
]
D_IN = sum(SPLIT_SIZES)
SPLIT_POINTS = list(np.cumsum(SPLIT_SIZES)[:-1])

kernel_name = "hymba_gdn_fox_swiglu"


def rms_norm(x, w):
    xf = x.astype(jnp.float32)
    out = xf * lax.rsqrt(jnp.mean(xf * xf, axis=-1, keepdims=True) + EPS)
    return (out * w.astype(jnp.float32)).astype(x.dtype)


def l2_norm(x):
    xf = x.astype(jnp.float32)
    return xf * lax.rsqrt(jnp.sum(xf * xf, axis=-1, keepdims=True) + EPS)


def causal_depthwise_conv(x, w):
    c = x.shape[-1]
    return lax.conv_general_dilated(
        x, w.reshape(CONV_K, 1, c).astype(x.dtype), window_strides=(1,),
        padding=[(CONV_K - 1, 0)], dimension_numbers=("NWC", "WIO", "NWC"),
        feature_group_count=c)


def gated_delta_rule(q, k, v, beta, g):
    B, T, H, Dk = q.shape
    Dv = v.shape[-1]
    N = T // CHUNK

    def chunks(t):
        return t.reshape(B, N, CHUNK, H, -1).transpose(0, 3, 1, 2, 4)

    q = chunks(q.astype(jnp.float32)) * (Dk ** -0.5)
    k = chunks(k.astype(jnp.float32))
    v = chunks(v.astype(jnp.float32))
    beta = beta.astype(jnp.float32).reshape(B, N, CHUNK, H).transpose(0, 3, 1, 2)
    g = jnp.cumsum(g.astype(jnp.float32).reshape(B, N, CHUNK, H).transpose(0, 3, 1, 2), axis=-1)

    causal = jnp.tril(jnp.ones((CHUNK, CHUNK), dtype=bool))
    strict = jnp.tril(jnp.ones((CHUNK, CHUNK), dtype=bool), k=-1)
    decay = jnp.exp(jnp.where(causal, g[..., :, None] - g[..., None, :], -jnp.inf))

    k_beta = k * beta[..., None]
    v_beta = v * beta[..., None]
    L = jnp.where(strict, jnp.einsum("bhncd,bhnmd->bhncm", k_beta, k) * decay, 0.0)
    eye = jnp.eye(CHUNK, dtype=jnp.float32)
    Tm = lax.linalg.triangular_solve(eye + L, jnp.broadcast_to(eye, L.shape),
                                     left_side=True, lower=True, unit_diagonal=True)
    u = jnp.einsum("bhncm,bhnmd->bhncd", Tm, v_beta)
    w = jnp.einsum("bhncm,bhnmd->bhncd", Tm, k_beta * jnp.exp(g)[..., None])
    intra = jnp.where(causal, jnp.einsum("bhncd,bhnmd->bhncm", q, k) * decay, 0.0)

    def to_scan(t):
        return jnp.moveaxis(t, 2, 0)

    def step(S, xs):
        q_c, k_c, u_c, w_c, A_c, g_c = xs
        v_new = u_c - jnp.einsum("bhcd,bhde->bhce", w_c, S)
        o = (jnp.einsum("bhcd,bhde->bhce", q_c * jnp.exp(g_c)[..., None], S)
             + jnp.einsum("bhcm,bhme->bhce", A_c, v_new))
        g_last = g_c[..., -1]
        S = (S * jnp.exp(g_last)[..., None, None]
             + jnp.einsum("bhcd,bhce->bhde", k_c * jnp.exp(g_last[..., None] - g_c)[..., None], v_new))
        return S, o

    S0 = jnp.zeros((B, H, Dk, Dv), jnp.float32)
    _, o = lax.scan(step, S0, (to_scan(q), to_scan(k), to_scan(u), to_scan(w),
                               to_scan(intra), to_scan(g)))
    return o.transpose(1, 0, 3, 2, 4).reshape(B, T, H, Dv)


def forgetting_attention(q, k, v, log_f):
    B, T, H, D = q.shape
    nb = T // Q_BLOCK
    F = jnp.cumsum(log_f.astype(jnp.float32), axis=1).transpose(0, 2, 1)
    qb = q.reshape(B, nb, Q_BLOCK, H, D).transpose(1, 0, 2, 3, 4)
    Fq = F.reshape(B, H, nb, Q_BLOCK).transpose(2, 0, 1, 3)
    pos_k = jnp.arange(T)
    scale = D ** -0.5

    def block(args):
        i, q_i, F_i = args
        s = jnp.einsum("bqhd,bkhd->bhqk", q_i, k, preferred_element_type=jnp.float32) * scale
        s = s + (F_i[..., :, None] - F[:, :, None, :])
        pos_q = i * Q_BLOCK + jnp.arange(Q_BLOCK)
        s = jnp.where(pos_q[:, None] >= pos_k[None, :], s, -jnp.inf)
        p = jax.nn.softmax(s, axis=-1)
        return jnp.einsum("bhqk,bkhd->bqhd", p.astype(v.dtype), v)

    o = lax.map(block, (jnp.arange(nb), qb, Fq))
    return o.transpose(1, 0, 2, 3, 4).reshape(B, T, H, D)


def _fwd_setup_inputs(seed: int = 0) -> dict:
    key = jax.random.key(seed)
    ks = jax.random.split(key, 20)
    f32 = jnp.float32

    def normal(k, shape, fan_in):
        return jax.random.normal(k, shape, f32) * (fan_in ** -0.5)

    def gain(k, shape):
        return 1.0 + 0.02 * jax.random.normal(k, shape, f32)

    x = jax.random.normal(ks[0], (BATCH, SEQ, D_MODEL), f32)
    norm1_w = gain(ks[1], (DEPTH, D_MODEL))
    w_in = normal(ks[2], (DEPTH, D_MODEL, D_IN), D_MODEL)
    gdn_conv_w = normal(ks[3], (DEPTH, CONV_K, 3 * GDN_WIDTH), CONV_K)
    gdn_A_log = jnp.log(jax.random.uniform(ks[4], (DEPTH, GDN_HEADS), f32, 1.0, 16.0))
    dt = jnp.exp(jax.random.uniform(ks[5], (DEPTH, GDN_HEADS), f32, np.log(1e-3), np.log(1e-1)))
    gdn_dt_bias = dt + jnp.log(-jnp.expm1(-dt))
    gdn_out_norm_w = gain(ks[6], (DEPTH, GDN_HEAD_DIM))
    fox_f_bias = jax.random.uniform(ks[7], (DEPTH, FOX_HEADS), f32, 1.0, 5.0)
    fox_q_norm_w = gain(ks[8], (DEPTH, FOX_HEAD_DIM))
    fox_k_norm_w = gain(ks[9], (DEPTH, FOX_HEAD_DIM))
    w_out = normal(ks[10], (DEPTH, D_MIX, D_MODEL), D_MIX)
    norm2_w = gain(ks[11], (DEPTH, D_MODEL))
    w_ffn_gate = normal(ks[12], (DEPTH, D_MODEL, D_FF), D_MODEL)
    w_ffn_up = normal(ks[13], (DEPTH, D_MODEL, D_FF), D_MODEL)
    w_ffn_down = normal(ks[14], (DEPTH, D_FF, D_MODEL), D_FF)
    final_norm_w = gain(ks[15], (D_MODEL,))
    return {"x": x, "norm1_w": norm1_w, "w_in": w_in, "gdn_conv_w": gdn_conv_w,
            "gdn_A_log": gdn_A_log, "gdn_dt_bias": gdn_dt_bias,
            "gdn_out_norm_w": gdn_out_norm_w, "fox_f_bias": fox_f_bias,
            "fox_q_norm_w": fox_q_norm_w, "fox_k_norm_w": fox_k_norm_w,
            "w_out": w_out, "norm2_w": norm2_w, "w_ffn_gate": w_ffn_gate,
            "w_ffn_up": w_ffn_up, "w_ffn_down": w_ffn_down, "final_norm_w": final_norm_w}


def _fwd_reference(x, norm1_w, w_in, gdn_conv_w, gdn_A_log, gdn_dt_bias, gdn_out_norm_w,
              fox_f_bias, fox_q_norm_w, fox_k_norm_w, w_out, norm2_w, w_ffn_gate,
              w_ffn_up, w_ffn_down, final_norm_w):
    B, T, _ = x.shape
    for l in range(DEPTH):
        h = rms_norm(x, norm1_w[l])
        proj = h @ w_in[l]
        (g_q, g_k, g_v, g_z, g_b, g_a,
         f_q, f_k, f_v, f_gate, f_f) = jnp.split(proj, SPLIT_POINTS, axis=-1)

        qkv = jax.nn.silu(causal_depthwise_conv(jnp.concatenate([g_q, g_k, g_v], -1), gdn_conv_w[l]))
        g_q, g_k, g_v = jnp.split(qkv, 3, axis=-1)
        gq = l2_norm(g_q.reshape(B, T, GDN_HEADS, GDN_HEAD_DIM))
        gk = l2_norm(g_k.reshape(B, T, GDN_HEADS, GDN_HEAD_DIM))
        gv = g_v.reshape(B, T, GDN_HEADS, GDN_HEAD_DIM)
        beta = jax.nn.sigmoid(g_b.astype(jnp.float32))
        g_log = -jnp.exp(gdn_A_log[l].astype(jnp.float32)) * jax.nn.softplus(
            g_a.astype(jnp.float32) + gdn_dt_bias[l].astype(jnp.float32))
        o_gdn = gated_delta_rule(gq, gk, gv, beta, g_log)
        z = g_z.reshape(B, T, GDN_HEADS, GDN_HEAD_DIM).astype(jnp.float32)
        o_gdn = rms_norm(o_gdn, gdn_out_norm_w[l]) * jax.nn.silu(z)
        o_gdn = o_gdn.astype(x.dtype).reshape(B, T, GDN_WIDTH)

        fq = rms_norm(f_q.reshape(B, T, FOX_HEADS, FOX_HEAD_DIM), fox_q_norm_w[l])
        fk = rms_norm(f_k.reshape(B, T, FOX_HEADS, FOX_HEAD_DIM), fox_k_norm_w[l])
        fv = f_v.reshape(B, T, FOX_HEADS, FOX_HEAD_DIM)
        log_f = jax.nn.log_sigmoid(f_f.astype(jnp.float32) + fox_f_bias[l].astype(jnp.float32))
        o_fox = forgetting_attention(fq, fk, fv, log_f).reshape(B, T, FOX_WIDTH)
        o_fox = o_fox * jax.nn.sigmoid(f_gate)

        mix = jnp.concatenate([o_gdn, o_fox.astype(x.dtype)], axis=-1)
        x = x + mix @ w_out[l]

        h = rms_norm(x, norm2_w[l])
        x = x + (jax.nn.silu(h @ w_ffn_gate[l]) * (h @ w_ffn_up[l])) @ w_ffn_down[l]
    return rms_norm(x, final_norm_w)


import jax as _jax
import jax.numpy as _jnp

TWIN_FORMAT = 'train_step'
FWD_PARAMS = ['x', 'norm1_w', 'w_in', 'gdn_conv_w', 'gdn_A_log', 'gdn_dt_bias', 'gdn_out_norm_w', 'fox_f_bias', 'fox_q_norm_w', 'fox_k_norm_w', 'w_out', 'norm2_w', 'w_ffn_gate', 'w_ffn_up', 'w_ffn_down', 'final_norm_w']
TWIN_WEIGHTS = ['norm1_w', 'w_in', 'gdn_conv_w', 'gdn_A_log', 'gdn_dt_bias', 'gdn_out_norm_w', 'fox_f_bias', 'fox_q_norm_w', 'fox_k_norm_w', 'w_out', 'norm2_w', 'w_ffn_gate', 'w_ffn_up', 'w_ffn_down', 'final_norm_w']
TWIN_DIFF_INPUT = 'x'
TWIN_INPUTS = ['x', 'norm1_w', 'w_in', 'gdn_conv_w', 'gdn_A_log', 'gdn_dt_bias', 'gdn_out_norm_w', 'fox_f_bias', 'fox_q_norm_w', 'fox_k_norm_w', 'w_out', 'norm2_w', 'w_ffn_gate', 'w_ffn_up', 'w_ffn_down', 'final_norm_w', 'loss_target', 'm_norm1_w', 'm_w_in', 'm_gdn_conv_w', 'm_gdn_A_log', 'm_gdn_dt_bias', 'm_gdn_out_norm_w', 'm_fox_f_bias', 'm_fox_q_norm_w', 'm_fox_k_norm_w', 'm_w_out', 'm_norm2_w', 'm_w_ffn_gate', 'm_w_ffn_up', 'm_w_ffn_down', 'm_final_norm_w', 'v_norm1_w', 'v_w_in', 'v_gdn_conv_w', 'v_gdn_A_log', 'v_gdn_dt_bias', 'v_gdn_out_norm_w', 'v_fox_f_bias', 'v_fox_q_norm_w', 'v_fox_k_norm_w', 'v_w_out', 'v_norm2_w', 'v_w_ffn_gate', 'v_w_ffn_up', 'v_w_ffn_down', 'v_final_norm_w']
TWIN_OUTPUTS = ['loss', 'grad_x', 'grad_norm1_w', 'grad_w_in', 'grad_gdn_conv_w', 'grad_gdn_A_log', 'grad_gdn_dt_bias', 'grad_gdn_out_norm_w', 'grad_fox_f_bias', 'grad_fox_q_norm_w', 'grad_fox_k_norm_w', 'grad_w_out', 'grad_norm2_w', 'grad_w_ffn_gate', 'grad_w_ffn_up', 'grad_w_ffn_down', 'grad_final_norm_w', 'delta_norm1_w', 'delta_w_in', 'delta_gdn_conv_w', 'delta_gdn_A_log', 'delta_gdn_dt_bias', 'delta_gdn_out_norm_w', 'delta_fox_f_bias', 'delta_fox_q_norm_w', 'delta_fox_k_norm_w', 'delta_w_out', 'delta_norm2_w', 'delta_w_ffn_gate', 'delta_w_ffn_up', 'delta_w_ffn_down', 'delta_final_norm_w', 'new_m_norm1_w', 'new_m_w_in', 'new_m_gdn_conv_w', 'new_m_gdn_A_log', 'new_m_gdn_dt_bias', 'new_m_gdn_out_norm_w', 'new_m_fox_f_bias', 'new_m_fox_q_norm_w', 'new_m_fox_k_norm_w', 'new_m_w_out', 'new_m_norm2_w', 'new_m_w_ffn_gate', 'new_m_w_ffn_up', 'new_m_w_ffn_down', 'new_m_final_norm_w', 'new_v_norm1_w', 'new_v_w_in', 'new_v_gdn_conv_w', 'new_v_gdn_A_log', 'new_v_gdn_dt_bias', 'new_v_gdn_out_norm_w', 'new_v_fox_f_bias', 'new_v_fox_q_norm_w', 'new_v_fox_k_norm_w', 'new_v_w_out', 'new_v_norm2_w', 'new_v_w_ffn_gate', 'new_v_w_ffn_up', 'new_v_w_ffn_down', 'new_v_final_norm_w']
TWIN_LEAF_KINDS = {'loss': 'loss', 'grad_x': 'grad_x', 'grad_norm1_w': 'grad_w', 'grad_w_in': 'grad_w', 'grad_gdn_conv_w': 'grad_w', 'grad_gdn_A_log': 'grad_w', 'grad_gdn_dt_bias': 'grad_w', 'grad_gdn_out_norm_w': 'grad_w', 'grad_fox_f_bias': 'grad_w', 'grad_fox_q_norm_w': 'grad_w', 'grad_fox_k_norm_w': 'grad_w', 'grad_w_out': 'grad_w', 'grad_norm2_w': 'grad_w', 'grad_w_ffn_gate': 'grad_w', 'grad_w_ffn_up': 'grad_w', 'grad_w_ffn_down': 'grad_w', 'grad_final_norm_w': 'grad_w', 'delta_norm1_w': 'delta_w', 'delta_w_in': 'delta_w', 'delta_gdn_conv_w': 'delta_w', 'delta_gdn_A_log': 'delta_w', 'delta_gdn_dt_bias': 'delta_w', 'delta_gdn_out_norm_w': 'delta_w', 'delta_fox_f_bias': 'delta_w', 'delta_fox_q_norm_w': 'delta_w', 'delta_fox_k_norm_w': 'delta_w', 'delta_w_out': 'delta_w', 'delta_norm2_w': 'delta_w', 'delta_w_ffn_gate': 'delta_w', 'delta_w_ffn_up': 'delta_w', 'delta_w_ffn_down': 'delta_w', 'delta_final_norm_w': 'delta_w', 'new_m_norm1_w': 'new_m', 'new_m_w_in': 'new_m', 'new_m_gdn_conv_w': 'new_m', 'new_m_gdn_A_log': 'new_m', 'new_m_gdn_dt_bias': 'new_m', 'new_m_gdn_out_norm_w': 'new_m', 'new_m_fox_f_bias': 'new_m', 'new_m_fox_q_norm_w': 'new_m', 'new_m_fox_k_norm_w': 'new_m', 'new_m_w_out': 'new_m', 'new_m_norm2_w': 'new_m', 'new_m_w_ffn_gate': 'new_m', 'new_m_w_ffn_up': 'new_m', 'new_m_w_ffn_down': 'new_m', 'new_m_final_norm_w': 'new_m', 'new_v_norm1_w': 'new_v', 'new_v_w_in': 'new_v', 'new_v_gdn_conv_w': 'new_v', 'new_v_gdn_A_log': 'new_v', 'new_v_gdn_dt_bias': 'new_v', 'new_v_gdn_out_norm_w': 'new_v', 'new_v_fox_f_bias': 'new_v', 'new_v_fox_q_norm_w': 'new_v', 'new_v_fox_k_norm_w': 'new_v', 'new_v_w_out': 'new_v', 'new_v_norm2_w': 'new_v', 'new_v_w_ffn_gate': 'new_v', 'new_v_w_ffn_up': 'new_v', 'new_v_w_ffn_down': 'new_v', 'new_v_final_norm_w': 'new_v'}


def _forward(args):
    return _fwd_reference(*[args[k] for k in FWD_PARAMS])


def _output_shape():
    out = _jax.eval_shape(lambda: _forward(_fwd_setup_inputs(0)))
    return out.shape, out.dtype

N_MICROBATCH = 1
ADAM_LR = 0.001
ADAM_B1 = 0.9
ADAM_B2 = 0.999
ADAM_EPS = 1e-08
ADAM_WD = 0.01
ADAM_STEP = 10
PER_EXAMPLE_BATCH_AXIS = {'x': 0, 'loss_target': 0}
SHARED_INPUTS = []
_WEIGHT_DTYPES = {'norm1_w': _jnp.float32, 'w_in': _jnp.float32, 'gdn_conv_w': _jnp.float32, 'gdn_A_log': _jnp.float32, 'gdn_dt_bias': _jnp.float32, 'gdn_out_norm_w': _jnp.float32, 'fox_f_bias': _jnp.float32, 'fox_q_norm_w': _jnp.float32, 'fox_k_norm_w': _jnp.float32, 'w_out': _jnp.float32, 'norm2_w': _jnp.float32, 'w_ffn_gate': _jnp.float32, 'w_ffn_up': _jnp.float32, 'w_ffn_down': _jnp.float32, 'final_norm_w': _jnp.float32}
MOMENT_SCALE = {'norm1_w': 1.090987e-01, 'w_in': 5.322052e-02, 'gdn_conv_w': 7.066346e-02, 'gdn_A_log': 2.016608e-01, 'gdn_dt_bias': 1.922967e-01, 'gdn_out_norm_w': 2.559787e-01, 'fox_f_bias': 1.692015e-01, 'fox_q_norm_w': 5.981101e-02, 'fox_k_norm_w': 5.967275e-02, 'w_out': 5.640387e-02, 'norm2_w': 8.582479e-02, 'w_ffn_gate': 3.738191e-02, 'w_ffn_up': 3.618391e-02, 'w_ffn_down': 5.987456e-02, 'final_norm_w': 1.600599e+01}


def _to_microbatches(a, axis):
    t = _jnp.moveaxis(a, axis, 0)
    t = t.reshape((N_MICROBATCH, t.shape[0] // N_MICROBATCH) + t.shape[1:])
    return _jnp.moveaxis(t, 1, axis + 1)


def setup_inputs(seed: int = 0) -> dict:
    inp = _fwd_setup_inputs(seed)
    key = _jax.random.fold_in(_jax.random.key(seed), 7919)
    shape, _ = _output_shape()
    out = dict(inp)
    out["loss_target"] = _jax.random.normal(_jax.random.fold_in(key, 0), shape, _jnp.float32)
    for i, name in enumerate(TWIN_WEIGHTS):
        w = inp[name].astype(_jnp.float32)
        if MOMENT_SCALE is None:
            s = _jnp.sqrt(_jnp.mean(_jnp.square(w)) + 1e-30)
        else:
            s = MOMENT_SCALE[name]
        km, kv = _jax.random.split(_jax.random.fold_in(key, i + 1))
        out[name] = w
        out["m_" + name] = s * _jax.random.normal(km, w.shape, _jnp.float32)
        out["v_" + name] = (s * s) * _jax.random.uniform(kv, w.shape, _jnp.float32, 0.5, 1.5)
    if N_MICROBATCH > 1:
        for name, axis in PER_EXAMPLE_BATCH_AXIS.items():
            out[name] = _to_microbatches(out[name], axis)
    return {'x': out['x'], 'norm1_w': out['norm1_w'], 'w_in': out['w_in'], 'gdn_conv_w': out['gdn_conv_w'], 'gdn_A_log': out['gdn_A_log'], 'gdn_dt_bias': out['gdn_dt_bias'], 'gdn_out_norm_w': out['gdn_out_norm_w'], 'fox_f_bias': out['fox_f_bias'], 'fox_q_norm_w': out['fox_q_norm_w'], 'fox_k_norm_w': out['fox_k_norm_w'], 'w_out': out['w_out'], 'norm2_w': out['norm2_w'], 'w_ffn_gate': out['w_ffn_gate'], 'w_ffn_up': out['w_ffn_up'], 'w_ffn_down': out['w_ffn_down'], 'final_norm_w': out['final_norm_w'], 'loss_target': out['loss_target'], 'm_norm1_w': out['m_norm1_w'], 'm_w_in': out['m_w_in'], 'm_gdn_conv_w': out['m_gdn_conv_w'], 'm_gdn_A_log': out['m_gdn_A_log'], 'm_gdn_dt_bias': out['m_gdn_dt_bias'], 'm_gdn_out_norm_w': out['m_gdn_out_norm_w'], 'm_fox_f_bias': out['m_fox_f_bias'], 'm_fox_q_norm_w': out['m_fox_q_norm_w'], 'm_fox_k_norm_w': out['m_fox_k_norm_w'], 'm_w_out': out['m_w_out'], 'm_norm2_w': out['m_norm2_w'], 'm_w_ffn_gate': out['m_w_ffn_gate'], 'm_w_ffn_up': out['m_w_ffn_up'], 'm_w_ffn_down': out['m_w_ffn_down'], 'm_final_norm_w': out['m_final_norm_w'], 'v_norm1_w': out['v_norm1_w'], 'v_w_in': out['v_w_in'], 'v_gdn_conv_w': out['v_gdn_conv_w'], 'v_gdn_A_log': out['v_gdn_A_log'], 'v_gdn_dt_bias': out['v_gdn_dt_bias'], 'v_gdn_out_norm_w': out['v_gdn_out_norm_w'], 'v_fox_f_bias': out['v_fox_f_bias'], 'v_fox_q_norm_w': out['v_fox_q_norm_w'], 'v_fox_k_norm_w': out['v_fox_k_norm_w'], 'v_w_out': out['v_w_out'], 'v_norm2_w': out['v_norm2_w'], 'v_w_ffn_gate': out['v_w_ffn_gate'], 'v_w_ffn_up': out['v_w_ffn_up'], 'v_w_ffn_down': out['v_w_ffn_down'], 'v_final_norm_w': out['v_final_norm_w']}


def _loss(weights, diff, rest, loss_target):
    with _jax.named_scope("forward"):
        args = {**rest, TWIN_DIFF_INPUT: diff, **{k: w.astype(_WEIGHT_DTYPES[k]) for k, w in weights.items()}}
        y = _forward(args)
    with _jax.named_scope("loss_head"):
        err = _jnp.square(y.astype(_jnp.float32) - loss_target)
        return 0.5 * _jnp.sum(_jnp.mean(err, axis=-1)) if err.ndim else 0.5 * err


def _adamw(w, g, m, v):
    m = ADAM_B1 * m + (1.0 - ADAM_B1) * g
    v = ADAM_B2 * v + (1.0 - ADAM_B2) * _jnp.square(g)
    m_hat = m / (1.0 - ADAM_B1 ** ADAM_STEP)
    v_hat = v / (1.0 - ADAM_B2 ** ADAM_STEP)
    delta = -ADAM_LR * (m_hat / (_jnp.sqrt(v_hat) + ADAM_EPS) + ADAM_WD * w)
    return delta, m, v


def reference(x, norm1_w, w_in, gdn_conv_w, gdn_A_log, gdn_dt_bias, gdn_out_norm_w, fox_f_bias, fox_q_norm_w, fox_k_norm_w, w_out, norm2_w, w_ffn_gate, w_ffn_up, w_ffn_down, final_norm_w, loss_target, m_norm1_w, m_w_in, m_gdn_conv_w, m_gdn_A_log, m_gdn_dt_bias, m_gdn_out_norm_w, m_fox_f_bias, m_fox_q_norm_w, m_fox_k_norm_w, m_w_out, m_norm2_w, m_w_ffn_gate, m_w_ffn_up, m_w_ffn_down, m_final_norm_w, v_norm1_w, v_w_in, v_gdn_conv_w, v_gdn_A_log, v_gdn_dt_bias, v_gdn_out_norm_w, v_fox_f_bias, v_fox_q_norm_w, v_fox_k_norm_w, v_w_out, v_norm2_w, v_w_ffn_gate, v_w_ffn_up, v_w_ffn_down, v_final_norm_w):
    given = dict(x=x, norm1_w=norm1_w, w_in=w_in, gdn_conv_w=gdn_conv_w, gdn_A_log=gdn_A_log, gdn_dt_bias=gdn_dt_bias, gdn_out_norm_w=gdn_out_norm_w, fox_f_bias=fox_f_bias, fox_q_norm_w=fox_q_norm_w, fox_k_norm_w=fox_k_norm_w, w_out=w_out, norm2_w=norm2_w, w_ffn_gate=w_ffn_gate, w_ffn_up=w_ffn_up, w_ffn_down=w_ffn_down, final_norm_w=final_norm_w, loss_target=loss_target, m_norm1_w=m_norm1_w, m_w_in=m_w_in, m_gdn_conv_w=m_gdn_conv_w, m_gdn_A_log=m_gdn_A_log, m_gdn_dt_bias=m_gdn_dt_bias, m_gdn_out_norm_w=m_gdn_out_norm_w, m_fox_f_bias=m_fox_f_bias, m_fox_q_norm_w=m_fox_q_norm_w, m_fox_k_norm_w=m_fox_k_norm_w, m_w_out=m_w_out, m_norm2_w=m_norm2_w, m_w_ffn_gate=m_w_ffn_gate, m_w_ffn_up=m_w_ffn_up, m_w_ffn_down=m_w_ffn_down, m_final_norm_w=m_final_norm_w, v_norm1_w=v_norm1_w, v_w_in=v_w_in, v_gdn_conv_w=v_gdn_conv_w, v_gdn_A_log=v_gdn_A_log, v_gdn_dt_bias=v_gdn_dt_bias, v_gdn_out_norm_w=v_gdn_out_norm_w, v_fox_f_bias=v_fox_f_bias, v_fox_q_norm_w=v_fox_q_norm_w, v_fox_k_norm_w=v_fox_k_norm_w, v_w_out=v_w_out, v_norm2_w=v_norm2_w, v_w_ffn_gate=v_w_ffn_gate, v_w_ffn_up=v_w_ffn_up, v_w_ffn_down=v_w_ffn_down, v_final_norm_w=v_final_norm_w)
    weights = {n: given[n] for n in TWIN_WEIGHTS}
    shared = {n: given[n] for n in SHARED_INPUTS}
    per_example = {n: given[n] for n in ['x']}
    grad_fn = _jax.value_and_grad(_loss, argnums=(0, 1))

    def one_microbatch(ex, loss_target):
        ex = dict(ex)
        diff = ex.pop(TWIN_DIFF_INPUT)
        return grad_fn(weights, diff, {**shared, **ex}, loss_target)

    if N_MICROBATCH == 1:
        loss, (grad_w, grad_x) = one_microbatch(per_example, given["loss_target"])
    else:
        def body(carry, xs):
            loss_sum, grad_sum = carry
            l_k, (gw_k, gx_k) = one_microbatch(xs[0], xs[1])
            with _jax.named_scope("update"):
                return (loss_sum + l_k, _jax.tree.map(_jnp.add, grad_sum, gw_k)), gx_k

        init = (_jnp.zeros((), _jnp.float32), _jax.tree.map(_jnp.zeros_like, weights))
        (loss, grad_w), grad_x = _jax.lax.scan(body, init, (per_example, given["loss_target"]))
    with _jax.named_scope("update"):
        delta_w, new_m, new_v = {}, {}, {}
        for n in TWIN_WEIGHTS:
            delta_w[n], new_m[n], new_v[n] = _adamw(weights[n], grad_w[n], given["m_" + n], given["v_" + n])
    return (loss, grad_x, *[grad_w[n] for n in TWIN_WEIGHTS], *[delta_w[n] for n in TWIN_WEIGHTS],
            *[new_m[n] for n in TWIN_WEIGHTS], *[new_v[n] for n in TWIN_WEIGHTS])
```

```python
import jax
import jax.numpy as jnp
import numpy as np
from jax import lax
from jax.experimental import pallas as pl
from jax.experimental.pallas import tpu as pltpu

F32 = jnp.float32
BF16 = jnp.bfloat16

D_MODEL = 1024
HEADS = 8
HEAD_DIM = 64
PAIRS = HEADS // 2
WIDTH = HEADS * HEAD_DIM
CHUNK = 64
CONV_K = 4
D_FF = 2816
EPS = 1e-6
SCALE = HEAD_DIM ** -0.5
LANES = 128
N_CHIPS = 4
D_IN = 4120
D_CAT = 4224
COL_SMALL = 4096 // LANES

ADAM_LR = 0.001
ADAM_B1 = 0.9
ADAM_B2 = 0.999
ADAM_EPS = 1e-08
ADAM_WD = 0.01
ADAM_STEP = 10

VMEM_LIMIT = 56 * 1024 * 1024
MESH = pl.DeviceIdType.MESH
HIGHEST = lax.Precision.HIGHEST


def _params(sem):
    return pltpu.CompilerParams(dimension_semantics=sem, vmem_limit_bytes=VMEM_LIMIT)


_CONTRACT = {"nn": ((1,), (0,)), "nt": ((1,), (1,)), "tn": ((0,), (0,))}


def _mm(a, b, *, dims, name, out_dtype=F32, add=None, tm=1024, tn=512, tk=512):
    if dims == "nn":
        (m, k), (k2, n) = a.shape, b.shape
    elif dims == "nt":
        (m, k), (n, k2) = a.shape, b.shape
    else:
        (k, m), (k2, n) = a.shape, b.shape
    assert k == k2, (a.shape, b.shape, dims)
    tm, tn, tk = min(tm, m), min(tn, n), min(tk, k)
    assert m % tm == 0 and n % tn == 0 and k % tk == 0, (m, n, k, tm, tn, tk)
    nk = k // tk
    a_spec = (pl.BlockSpec((tk, tm), lambda i, j, kk: (kk, i)) if dims == "tn"
              else pl.BlockSpec((tm, tk), lambda i, j, kk: (i, kk)))
    b_spec = (pl.BlockSpec((tn, tk), lambda i, j, kk: (j, kk)) if dims == "nt"
              else pl.BlockSpec((tk, tn), lambda i, j, kk: (kk, j)))
    o_spec = pl.BlockSpec((tm, tn), lambda i, j, kk: (i, j))
    contract = (_CONTRACT[dims], ((), ()))
    has_add = add is not None

    def body(*refs):
        if has_add:
            a_ref, b_ref, add_ref, o_ref, acc = refs
        else:
            a_ref, b_ref, o_ref, acc = refs
        kk = pl.program_id(2)

        @pl.when(kk == 0)
        def _():
            acc[...] = jnp.zeros_like(acc)

        acc[...] += lax.dot_general(a_ref[...].astype(BF16), b_ref[...].astype(BF16), contract,
                                    preferred_element_type=F32)

        @pl.when(kk == nk - 1)
        def _():
            r = acc[...]
            if has_add:
                r = r + add_ref[...].astype(F32)
            o_ref[...] = r.astype(out_dtype)

    ins = [a, b] + ([add] if has_add else [])
    in_specs = [a_spec, b_spec] + ([o_spec] if has_add else [])
    return pl.pallas_call(
        body, name=name, grid=(m // tm, n // tn, nk),
        in_specs=in_specs, out_specs=o_spec,
        out_shape=jax.ShapeDtypeStruct((m, n), out_dtype),
        scratch_shapes=[pltpu.VMEM((tm, tn), F32)],
        compiler_params=_params(("parallel", "parallel", "arbitrary")),
    )(*ins)


def _tiles(fn, *, name, rows, tm, ncol=1, row_ins=(), col_consts=(), full_consts=(),
           row_outs=(), acc_outs=()):
    nt = rows // tm
    assert rows % tm == 0
    n_full, n_col, n_row = len(full_consts), len(col_consts), len(row_ins)
    n_ro, n_acc = len(row_outs), len(acc_outs)

    def body(*refs):
        ins = refs[:n_full + n_col + n_row]
        outs = refs[n_full + n_col + n_row:]
        i = pl.program_id(1)
        res = fn(pl.program_id(0), *[r[...] for r in ins])
        for r, v in zip(outs[:n_ro], res[:n_ro]):
            r[...] = v.astype(r.dtype)
        if n_acc:
            @pl.when(i == 0)
            def _():
                for r in outs[n_ro:]:
                    r[...] = jnp.zeros_like(r)
            for r, v in zip(outs[n_ro:], res[n_ro:]):
                r[...] += v

    in_specs = [pl.BlockSpec(a.shape, lambda j, i, nd=a.ndim: (0,) * nd) for a in full_consts]
    in_specs += [pl.BlockSpec((nr, w), lambda j, i, o=o: (0, o + j)) for (_, nr, w, o) in col_consts]
    in_specs += [pl.BlockSpec((tm, w), lambda j, i, o=o: (i, o + j)) for (_, w, o) in row_ins]
    out_specs = [pl.BlockSpec((tm, w), lambda j, i: (i, j)) for (w, _) in row_outs]
    out_specs += [pl.BlockSpec((nr, w), lambda j, i: (0, j)) for (nr, w) in acc_outs]
    out_shape = [jax.ShapeDtypeStruct((rows, w * ncol), dt) for (w, dt) in row_outs]
    out_shape += [jax.ShapeDtypeStruct((nr, w * ncol), F32) for (nr, w) in acc_outs]
    args = list(full_consts) + [c[0] for c in col_consts] + [r[0] for r in row_ins]
    out = pl.pallas_call(
        body, name=name, grid=(ncol, nt), in_specs=in_specs, out_specs=out_specs, out_shape=out_shape,
        compiler_params=_params(("parallel", "arbitrary")),
    )(*args)
    return out


def _rms(x, w):
    return x * lax.rsqrt(jnp.mean(x * x, axis=-1, keepdims=True) + EPS) * w


def _lane_lo(shape):
    return lax.broadcasted_iota(jnp.int32, shape, len(shape) - 1) < HEAD_DIM


def _pair_sum(x):
    lo = _lane_lo(x.shape)
    s0 = jnp.sum(jnp.where(lo, x, 0.0), axis=-1, keepdims=True)
    s1 = jnp.sum(jnp.where(lo, 0.0, x), axis=-1, keepdims=True)
    return jnp.where(lo, s0, s1)


def _head_col(x, lo, h):
    keep = lo if h == 0 else jnp.logical_not(lo)
    return jnp.max(jnp.where(keep, x, -jnp.inf), axis=-1, keepdims=True)


def _softplus(x):
    return jnp.maximum(x, 0.0) + jnp.log1p(jnp.exp(-jnp.abs(x)))


def _silu(x):
    return x * jax.nn.sigmoid(x)


def _dot(a, b, contract):
    return lax.dot_general(a.astype(BF16), b.astype(BF16), (contract, ((), ())),
                           preferred_element_type=F32)


def _dot32(a, b, contract):
    return lax.dot_general(a, b, (contract, ((), ())), precision=HIGHEST, preferred_element_type=F32)


def _bd(y):
    yy = jnp.concatenate([y, y], axis=0)
    r = lax.broadcasted_iota(jnp.int32, yy.shape, 0) < HEAD_DIM
    c = lax.broadcasted_iota(jnp.int32, yy.shape, 1) < HEAD_DIM
    return jnp.where(r == c, yy, 0.0)


def _pp(x, y):
    return _dot(x, _bd(y), _CONTRACT["nn"])


def _pp_nt(x, y):
    return _dot(x, _bd(y), _CONTRACT["nt"])


def _pp_tn(x, y):
    full = _dot(x, y, _CONTRACT["tn"])
    return jnp.where(_lane_lo((HEAD_DIM, LANES)), full[:HEAD_DIM], full[HEAD_DIM:])


def _gdn_masks():
    row = lax.broadcasted_iota(jnp.int32, (CHUNK, LANES), 0)
    col = lax.broadcasted_iota(jnp.int32, (CHUNK, LANES), 1) % HEAD_DIM
    return row, col


def _gdn_chunk(q, k, v, bx, gx, gr):
    row, col = _gdn_masks()
    incl, strict = col <= row, col < row
    dm = jnp.where(incl, jnp.exp(jnp.minimum(gx - gr, 0.0)), 0.0)
    kb = k * bx
    vb = v * bx
    big_g = _pp_nt(kb, k)
    low = jnp.where(strict, big_g * dm, 0.0)
    eg = jnp.exp(gx)
    qs = q * SCALE
    big_p = _pp_nt(qs, k)
    att = jnp.where(incl, big_p * dm, 0.0)
    return incl, strict, dm, kb, vb, low, eg, qs, att


def _gdn_forward(qkv, betax, gcx, grow, rows):
    nchunk = rows // CHUNK

    def body(q_ref, k_ref, v_ref, bx_ref, gx_ref, gr_ref, o_ref, ss_ref, ts_ref, state):
        n = pl.program_id(1)

        @pl.when(n == 0)
        def _():
            state[...] = jnp.zeros_like(state)

        q, k, v, bx, gx = q_ref[...], k_ref[...], v_ref[...], bx_ref[...], gx_ref[...]
        gr = gr_ref[0, 0]
        glast = gx_ref[pl.ds(CHUNK - 1, 1), :]
        incl, strict, dm, kb, vb, low, eg, qs, att = _gdn_chunk(q, k, v, bx, gx, gr)
        row, col = _gdn_masks()
        x = -low
        tm = jnp.where(row == col, 1.0, 0.0) + x
        for _ in range(5):
            x = _pp(x, x)
            tm = tm + _pp(tm, x)
        u = _pp(tm, vb)
        w = _pp(tm, kb * eg)
        s = state[...]
        ss_ref[0, 0] = s
        ts_ref[0, 0] = tm
        vn = u - _pp(w, s)
        o_ref[...] = _pp(qs * eg, s) + _pp(att, vn)
        kd = k * jnp.exp(glast - gx)
        state[...] = s * jnp.exp(glast) + _pp_tn(kd, vn)

    blk = lambda off: pl.BlockSpec((CHUNK, LANES), lambda p, n, off=off: (n, off + p))
    sv = pl.BlockSpec((1, 1, CHUNK, LANES), lambda p, n: (n, p, 0, 0))
    return pl.pallas_call(
        body, name="gdn_fwd", grid=(PAIRS, nchunk),
        in_specs=[blk(0), blk(PAIRS), blk(2 * PAIRS), blk(0), blk(0),
                  pl.BlockSpec((1, 1, 1, LANES), lambda p, n: (n, p, 0, 0))],
        out_specs=[blk(0), sv, sv],
        out_shape=[jax.ShapeDtypeStruct((rows, WIDTH), F32),
                   jax.ShapeDtypeStruct((nchunk, PAIRS, CHUNK, LANES), F32),
                   jax.ShapeDtypeStruct((nchunk, PAIRS, CHUNK, LANES), F32)],
        scratch_shapes=[pltpu.VMEM((CHUNK, LANES), F32)],
        compiler_params=_params(("parallel", "arbitrary")),
    )(qkv, qkv, qkv, betax, gcx, grow)


def _gdn_backward(qkv, betax, gcx, grow, ssave, tsave, do, rows):
    nchunk = rows // CHUNK

    def body(q_ref, k_ref, v_ref, bx_ref, gx_ref, gr_ref, ss_ref, ts_ref, do_ref,
             dq_ref, dk_ref, dv_ref, dbx_ref, dgx_ref, dgr_ref, dstate):
        n = pl.program_id(1)

        @pl.when(n == 0)
        def _():
            dstate[...] = jnp.zeros_like(dstate)

        q, k, v, bx, gx = q_ref[...], k_ref[...], v_ref[...], bx_ref[...], gx_ref[...]
        gr = gr_ref[0, 0]
        glast = gx_ref[pl.ds(CHUNK - 1, 1), :]
        s, tm, d_o = ss_ref[0, 0], ts_ref[0, 0], do_ref[...]
        ds_out = dstate[...]
        incl, strict, dm, kb, vb, low, eg, qs, att = _gdn_chunk(q, k, v, bx, gx, gr)
        row, col = _gdn_masks()
        kbg = kb * eg
        u = _pp(tm, vb)
        w = _pp(tm, kbg)
        vn = u - _pp(w, s)
        qg = qs * eg
        ed = jnp.exp(glast - gx)
        kd = k * ed
        eglast = jnp.exp(glast)

        dkd = _pp_nt(vn, ds_out)
        dvn = _pp(kd, ds_out) + _pp_tn(att, d_o)
        dqg = _pp_nt(d_o, s)
        datt = jnp.where(incl, _pp_nt(d_o, vn), 0.0)
        dw = -_pp_nt(dvn, s)
        dtm = _pp_nt(dvn, vb) + _pp_nt(dw, kbg)
        dvb = _pp_tn(tm, dvn)
        dkbg = _pp_tn(tm, dw)
        dlow = jnp.where(strict, -_pp_nt(_pp_tn(tm, dtm), tm), 0.0)
        dbig_g = dlow * dm
        dbig_p = datt * dm
        dkb = _pp(dbig_g, k) + dkbg * eg
        dqs = _pp(dbig_p, k) + dqg * eg
        dk = _pp_tn(dbig_g, kb) + _pp_tn(dbig_p, qs) + dkd * ed + dkb * bx
        z = dlow * low + datt * att
        kdterm = dkd * kd
        dglast = (jnp.sum(ds_out * s, axis=0, keepdims=True) * eglast
                  + jnp.sum(kdterm, axis=0, keepdims=True))
        dgx = dqg * qg + dkbg * kbg - kdterm
        dgx = dgx + jnp.where(col == 0, _pair_sum(z), 0.0)
        dgx = dgx + jnp.where(row == CHUNK - 1, dglast, 0.0)

        dq_ref[...] = dqs * SCALE
        dk_ref[...] = dk
        dv_ref[...] = dvb * bx
        dbx_ref[...] = dkb * k + dvb * v
        dgx_ref[...] = dgx
        dgr_ref[0, 0] = -jnp.sum(z, axis=0, keepdims=True)
        dstate[...] = ds_out * eglast + _pp_tn(qg, d_o) - _pp_tn(w, dvn)

    last = nchunk - 1
    blk = lambda off: pl.BlockSpec((CHUNK, LANES), lambda p, n, off=off: (last - n, off + p))
    sv = pl.BlockSpec((1, 1, CHUNK, LANES), lambda p, n: (last - n, p, 0, 0))
    gr_spec = pl.BlockSpec((1, 1, 1, LANES), lambda p, n: (last - n, p, 0, 0))
    wide = jax.ShapeDtypeStruct((rows, WIDTH), F32)
    return pl.pallas_call(
        body, name="gdn_bwd", grid=(PAIRS, nchunk),
        in_specs=[blk(0), blk(PAIRS), blk(2 * PAIRS), blk(0), blk(0), gr_spec, sv, sv, blk(0)],
        out_specs=[blk(0)] * 5 + [gr_spec],
        out_shape=[wide] * 5 + [jax.ShapeDtypeStruct((nchunk, PAIRS, 1, LANES), F32)],
        scratch_shapes=[pltpu.VMEM((CHUNK, LANES), F32)],
        compiler_params=_params(("parallel", "arbitrary")),
    )(qkv, qkv, qkv, betax, gcx, grow, ssave, tsave, do)


ATT_TQ = 256


def _att_scores(qh, kt, fq, fk, q0, k0, tq, tk):
    s = _dot(qh, kt, _CONTRACT["nt"]) * SCALE + fq - fk
    rq = q0 + lax.broadcasted_iota(jnp.int32, (tq, tk), 0)
    ck = k0 + lax.broadcasted_iota(jnp.int32, (tq, tk), 1)
    return jnp.where(rq >= ck, s, -jnp.inf)


def _attention_forward(fqk, proj, fx, frow, rows):
    tq = tk = min(ATT_TQ, rows)
    nq = rows // tq
    v_off = 3072 // LANES

    def body(q_ref, k_ref, v_ref, fx_ref, fr_ref, o_ref, lse_ref):
        qi = pl.program_id(1)
        q0 = qi * tq
        q = q_ref[...]
        fx_t = fx_ref[...]
        lo_q = _lane_lo((tq, LANES))
        lo_k = _lane_lo((tk, LANES))
        outs, lses = [], []
        for h in range(2):
            keep_q = lo_q if h == 0 else jnp.logical_not(lo_q)
            keep_k = lo_k if h == 0 else jnp.logical_not(lo_k)
            qh = jnp.where(keep_q, q, 0.0).astype(BF16)
            fq = _head_col(fx_t, lo_q, h)

            def step(ki, carry, qh=qh, fq=fq, keep_k=keep_k, h=h):
                m, l, acc = carry
                k0 = pl.multiple_of(ki * tk, tk)
                kt = k_ref[pl.ds(k0, tk), :]
                vt = jnp.where(keep_k, v_ref[pl.ds(k0, tk), :], 0.0)
                fk = fr_ref[0, pl.ds(h, 1), pl.ds(k0, tk)]
                s = _att_scores(qh, kt, fq, fk, q0, k0, tq, tk)
                m_new = jnp.maximum(m, jnp.max(s, axis=-1, keepdims=True))
                p = jnp.exp(s - m_new)
                alpha = jnp.exp(m - m_new)
                l = alpha * l + jnp.sum(p, axis=-1, keepdims=True)
                acc = alpha * acc + _dot(p, vt, _CONTRACT["nn"])
                return m_new, l, acc

            init = (jnp.full((tq, 1), -jnp.inf, F32), jnp.zeros((tq, 1), F32), jnp.zeros((tq, LANES), F32))
            m, l, acc = lax.fori_loop(0, qi + 1, step, init)
            outs.append(acc / l)
            lses.append(m + jnp.log(l))
        o_ref[...] = outs[0] + outs[1]
        lse_ref[...] = jnp.where(lo_q, lses[0], lses[1])

    whole = lambda off: pl.BlockSpec((rows, LANES), lambda p, i, off=off: (0, off + p))
    qblk = lambda off: pl.BlockSpec((tq, LANES), lambda p, i, off=off: (i, off + p))
    wide = jax.ShapeDtypeStruct((rows, WIDTH), F32)
    return pl.pallas_call(
        body, name="fox_fwd", grid=(PAIRS, nq),
        in_specs=[qblk(0), whole(PAIRS), whole(v_off), qblk(0),
                  pl.BlockSpec((1, 2, rows), lambda p, i: (p, 0, 0))],
        out_specs=[qblk(0), qblk(0)], out_shape=[wide, wide],
        compiler_params=_params(("parallel", "arbitrary")),
    )(fqk, fqk, proj, fx, frow)


def _attention_delta(fqk, proj, fx, frow, lse, dao, rows):
    tq = tk = min(ATT_TQ, rows)
    nq = rows // tq
    v_off = 3072 // LANES

    def body(q_ref, k_ref, v_ref, fx_ref, fr_ref, lse_ref, do_ref, delta_ref):
        qi = pl.program_id(1)
        q0 = qi * tq
        q, d_o, fx_t, lse_t = q_ref[...], do_ref[...], fx_ref[...], lse_ref[...]
        lo_q = _lane_lo((tq, LANES))
        deltas = []
        for h in range(2):
            keep_q = lo_q if h == 0 else jnp.logical_not(lo_q)
            qh = jnp.where(keep_q, q, 0.0).astype(BF16)
            doh = jnp.where(keep_q, d_o, 0.0).astype(BF16)
            fq = _head_col(fx_t, lo_q, h)
            lse_h = _head_col(lse_t, lo_q, h)

            def step(ki, acc, qh=qh, doh=doh, fq=fq, lse_h=lse_h, h=h):
                k0 = pl.multiple_of(ki * tk, tk)
                kt = k_ref[pl.ds(k0, tk), :]
                vt = v_ref[pl.ds(k0, tk), :]
                fk = fr_ref[0, pl.ds(h, 1), pl.ds(k0, tk)]
                p = jnp.exp(_att_scores(qh, kt, fq, fk, q0, k0, tq, tk) - lse_h)
                dp = _dot(doh, vt, _CONTRACT["nt"])
                return acc + jnp.sum(p * dp, axis=-1, keepdims=True)

            deltas.append(lax.fori_loop(0, qi + 1, step, jnp.zeros((tq, 1), F32)))
        delta_ref[...] = jnp.where(lo_q, deltas[0], deltas[1])

    whole = lambda off: pl.BlockSpec((rows, LANES), lambda p, i, off=off: (0, off + p))
    qblk = lambda off: pl.BlockSpec((tq, LANES), lambda p, i, off=off: (i, off + p))
    return pl.pallas_call(
        body, name="fox_delta", grid=(PAIRS, nq),
        in_specs=[qblk(0), whole(PAIRS), whole(v_off), qblk(0),
                  pl.BlockSpec((1, 2, rows), lambda p, i: (p, 0, 0)), qblk(0), qblk(0)],
        out_specs=qblk(0), out_shape=jax.ShapeDtypeStruct((rows, WIDTH), F32),
        compiler_params=_params(("parallel", "arbitrary")),
    )(fqk, fqk, proj, fx, frow, lse, dao)


def _attention_backward(fqk, proj, fx, frow, delta, lse, dao, rows):
    tq = tk = min(ATT_TQ, rows)
    nq = rows // tq
    v_off = 3072 // LANES

    def body(q_ref, k_ref, v_ref, fx_ref, fr_ref, delta_ref, lse_ref, do_ref,
             dq_ref, dk_ref, dv_ref, dfr_ref):
        ki = pl.program_id(1)
        k0 = ki * tk

        @pl.when(ki == 0)
        def _():
            dq_ref[...] = jnp.zeros_like(dq_ref)

        lo_q = _lane_lo((tq, LANES))
        lo_k = _lane_lo((tk, LANES))
        kt = k_ref[...]
        vt = v_ref[...]

        def step(qi, carry):
            dk, dv, df0, df1 = carry
            q0 = pl.multiple_of(qi * tq, tq)
            rows_q = pl.ds(q0, tq)
            q, d_o, delta_x = q_ref[rows_q, :], do_ref[rows_q, :], delta_ref[rows_q, :]
            lse_t, fx_t = lse_ref[rows_q, :], fx_ref[rows_q, :]
            dq = jnp.zeros((tq, LANES), F32)
            dfs = []
            for h in range(2):
                keep_q = lo_q if h == 0 else jnp.logical_not(lo_q)
                keep_k = lo_k if h == 0 else jnp.logical_not(lo_k)
                qh = jnp.where(keep_q, q, 0.0).astype(BF16)
                doh = jnp.where(keep_q, d_o, 0.0).astype(BF16)
                fq = _head_col(fx_t, lo_q, h)
                fk = fr_ref[0, pl.ds(h, 1), :]
                s = _att_scores(qh, kt, fq, fk, q0, k0, tq, tk)
                p = jnp.exp(s - _head_col(lse_t, lo_q, h))
                dp = _dot(doh, vt, _CONTRACT["nt"])
                ds = p * (dp - _head_col(delta_x, lo_q, h))
                dv = dv + _dot(p, doh, _CONTRACT["tn"])
                dk = dk + _dot(ds, qh, _CONTRACT["tn"]) * SCALE
                dq = dq + _dot(ds, jnp.where(keep_k, kt, 0.0), _CONTRACT["nn"]) * SCALE
                dfs.append(-jnp.sum(ds, axis=0, keepdims=True))
            dq_ref[rows_q, :] += dq
            return dk, dv, df0 + dfs[0], df1 + dfs[1]

        zero_kv = jnp.zeros((tk, LANES), F32)
        zero_f = jnp.zeros((1, tk), F32)
        dk, dv, df0, df1 = lax.fori_loop(ki, nq, step, (zero_kv, zero_kv, zero_f, zero_f))
        dk_ref[...] = dk
        dv_ref[...] = dv
        dfr_ref[0, pl.ds(0, 1), :] = df0
        dfr_ref[0, pl.ds(1, 1), :] = df1

    whole = lambda off: pl.BlockSpec((rows, LANES), lambda p, i, off=off: (0, off + p))
    kblk = lambda off: pl.BlockSpec((tk, LANES), lambda p, i, off=off: (i, off + p))
    fr_spec = pl.BlockSpec((1, 2, tk), lambda p, i: (p, 0, i))
    wide = jax.ShapeDtypeStruct((rows, WIDTH), F32)
    return pl.pallas_call(
        body, name="fox_bwd", grid=(PAIRS, nq),
        in_specs=[whole(0), kblk(PAIRS), kblk(v_off), whole(0), fr_spec, whole(0), whole(0), whole(0)],
        out_specs=[whole(0), kblk(0), kblk(0), fr_spec],
        out_shape=[wide, wide, wide, jax.ShapeDtypeStruct((PAIRS, 2, rows), F32)],
        compiler_params=_params(("parallel", "arbitrary")),
    )(fqk, fqk, proj, fx, frow, delta, lse, dao)


def _lane_ids(shape):
    return lax.broadcasted_iota(jnp.int32, shape, len(shape) - 1)


def _gates_elem(a_log, dt_bias, f_bias, pre):
    lane = _lane_ids(pre.shape)
    beta = jax.nn.sigmoid(pre)
    g = -jnp.exp(a_log) * _softplus(pre + dt_bias)
    lf = -_softplus(-(pre + f_bias))
    return jnp.where(lane < 8, beta, jnp.where(lane < 16, g, jnp.where(lane < 24, lf, 0.0)))


def _tri_consts():
    r = np.arange(LANES)[:, None]
    c = np.arange(LANES)[None, :]
    full = (c <= r).astype(np.float32)
    chunked = full * ((r // CHUNK) == (c // CHUNK))
    return jnp.asarray(chunked), jnp.asarray(full)


def _cums_fwd(lc, lf, gates):
    rows = gates.shape[0]
    lane = _lane_ids((LANES, LANES))
    carry = jnp.zeros((1, LANES), F32)
    out = []
    for r in range(rows // LANES):
        blk = gates[r * LANES:(r + 1) * LANES]
        gc = _dot32(lc, blk, _CONTRACT["nn"])
        f = _dot32(lf, blk, _CONTRACT["nn"]) + carry
        carry = carry + jnp.sum(blk, axis=0, keepdims=True)
        out.append(jnp.where((lane >= 8) & (lane < 16), gc, jnp.where((lane >= 16) & (lane < 24), f, 0.0)))
    return jnp.concatenate(out, axis=0)


def _cums_bwd(lc, lf, dcums):
    rows = dcums.shape[0]
    lane = _lane_ids((LANES, LANES))
    is_g = (lane >= 8) & (lane < 16)
    is_f = (lane >= 16) & (lane < 24)
    carry = jnp.zeros((1, LANES), F32)
    out = [None] * (rows // LANES)
    for r in reversed(range(rows // LANES)):
        blk = dcums[r * LANES:(r + 1) * LANES]
        dg = jnp.where(is_g, blk, 0.0)
        df = jnp.where(is_f, blk, 0.0)
        out[r] = _dot32(lc, dg, _CONTRACT["tn"]) + _dot32(lf, df, _CONTRACT["tn"]) + carry
        carry = carry + jnp.sum(df, axis=0, keepdims=True)
    return jnp.concatenate(out, axis=0)


def _expand_consts():
    xb = np.zeros((LANES, WIDTH), np.float32)
    xg = np.zeros((LANES, WIDTH), np.float32)
    xf = np.zeros((LANES, WIDTH), np.float32)
    for h in range(HEADS):
        xb[h, h * HEAD_DIM:(h + 1) * HEAD_DIM] = 1.0
        xg[8 + h, h * HEAD_DIM:(h + 1) * HEAD_DIM] = 1.0
        xf[16 + h, h * HEAD_DIM:(h + 1) * HEAD_DIM] = 1.0
    return jnp.asarray(xb), jnp.asarray(xg), jnp.asarray(xf)


def _shift_down(x, s):
    if s == 0:
        return x
    row = lax.broadcasted_iota(jnp.int32, x.shape, 0)
    return jnp.where(row >= s, pltpu.roll(x, s, 0), 0.0)


def _shift_up(x, s):
    if s == 0:
        return x
    n = x.shape[0]
    row = lax.broadcasted_iota(jnp.int32, x.shape, 0)
    return jnp.where(row < n - s, pltpu.roll(x, n - s, 0), 0.0)


def _row_of(cw, i):
    row = lax.broadcasted_iota(jnp.int32, cw.shape, 0)
    return jnp.sum(jnp.where(row == i, cw, 0.0), axis=0, keepdims=True)


def _conv(cw, x):
    c = jnp.zeros_like(x)
    for i in range(CONV_K):
        c = c + _row_of(cw, i) * _shift_down(x, CONV_K - 1 - i)
    return c


def _post_conv(is_qk, c):
    s = _silu(c)
    n = s * lax.rsqrt(_pair_sum(s * s) + EPS)
    return jnp.where(is_qk, n, s)


def _gdn_prep_fwd(col, cw, x):
    return (_post_conv(col < 2 * PAIRS, _conv(cw, x)),)


def _gdn_prep_bwd(col, cw, x, dy):
    c = _conv(cw, x)
    _, vjp = jax.vjp(lambda cc: _post_conv(col < 2 * PAIRS, cc), c)
    (dc,) = vjp(dy)
    dx = jnp.zeros_like(x)
    row = lax.broadcasted_iota(jnp.int32, cw.shape, 0)
    dcw = jnp.zeros(cw.shape, F32)
    for i in range(CONV_K):
        s = CONV_K - 1 - i
        dx = dx + _row_of(cw, i) * _shift_up(dc, s)
        dcw = dcw + jnp.where(row == i, jnp.sum(dc * _shift_down(x, s), axis=0, keepdims=True), 0.0)
    return dx, dcw


def _head_rms(w, x):
    return x * lax.rsqrt(_pair_sum(x * x) / HEAD_DIM + EPS) * w


def _cat_weights(w_in):
    pad = jnp.zeros((w_in.shape[0], D_CAT - D_IN), w_in.dtype)
    return jnp.concatenate([w_in[:, :2048], w_in[:, 2064:4112], w_in[:, 2048:2064], w_in[:, 4112:4120], pad], axis=1)


def _uncat_grad(g):
    return jnp.concatenate([g[:, :2048], g[:, 4096:4112], g[:, 2048:4096], g[:, 4112:4120]], axis=1)


def _lanes_to_rowform(v8, rows):
    return v8.reshape(rows // CHUNK, CHUNK, HEADS).transpose(0, 2, 1).reshape(rows // CHUNK, PAIRS, 1, LANES)


def _rowform_to_lanes(v, rows):
    return v.reshape(rows // CHUNK, HEADS, CHUNK).transpose(0, 2, 1).reshape(rows, HEADS)


def _local_step(x, target, norm1_w, w_cat, conv_w, a_log, dt_bias, out_norm_w, f_bias, q_norm_w, k_norm_w,
                w_out, norm2_w, w_gate, w_up, w_down, final_w):
    rows = x.shape[0]
    tm = min(256, rows)
    lc, lf = _tri_consts()
    xb, xg, xf = _expand_consts()

    (h1,) = _tiles(lambda col, w, xx: (_rms(xx, w),), name="norm1", rows=rows, tm=tm,
                   full_consts=[norm1_w], row_ins=[(x, D_MODEL, 0)], row_outs=[(D_MODEL, BF16)])
    proj = _mm(h1, w_cat, dims="nn", name="in_proj", tn=384, tk=1024)

    lane_pad = lambda v, off: jnp.pad(v.reshape(1, -1), ((0, 0), (off, LANES - off - v.size)))
    p_a, p_dt, p_fb = lane_pad(a_log, 8), lane_pad(dt_bias, 8), lane_pad(f_bias, 16)

    def gates_fwd(col, lcv, lfv, a, dt, fb, pre):
        gates = _gates_elem(a, dt, fb, pre)
        return gates, _cums_fwd(lcv, lfv, gates)

    gates, cums = _tiles(gates_fwd, name="gates", rows=rows, tm=rows,
                         full_consts=[lc, lf, p_a, p_dt, p_fb], row_ins=[(proj, LANES, COL_SMALL)],
                         row_outs=[(LANES, F32), (LANES, F32)])

    def expand_fwd(col, b, g, f, gt, cm):
        return (_dot32(gt, b, _CONTRACT["nn"]), _dot32(cm, g, _CONTRACT["nn"]), _dot32(cm, f, _CONTRACT["nn"]))

    betax, gcx, fx = _tiles(expand_fwd, name="expand", rows=rows, tm=tm, full_consts=[xb, xg, xf],
                            row_ins=[(gates, LANES, 0), (cums, LANES, 0)],
                            row_outs=[(WIDTH, F32)] * 3)
    grow = _lanes_to_rowform(cums[:, 8:16], rows)
    frow = cums[:, 16:24].T.reshape(PAIRS, 2, rows)

    (qkv,) = _tiles(_gdn_prep_fwd, name="gdn_prep", rows=rows, tm=rows, ncol=3 * PAIRS,
                    col_consts=[(conv_w, CONV_K, LANES, 0)], row_ins=[(proj, LANES, 0)],
                    row_outs=[(LANES, F32)])
    o_gdn, ssave, tsave = _gdn_forward(qkv, betax, gcx, grow, rows)

    w_qk = jnp.concatenate([jnp.tile(q_norm_w.reshape(1, -1), (1, HEADS)),
                            jnp.tile(k_norm_w.reshape(1, -1), (1, HEADS))], axis=1)
    fox_off = 2048 // LANES
    (fqk,) = _tiles(lambda col, w, xx: (_head_rms(w, xx),), name="fox_prep", rows=rows, tm=tm, ncol=2 * PAIRS,
                    col_consts=[(w_qk, 1, LANES, 0)], row_ins=[(proj, LANES, fox_off)],
                    row_outs=[(LANES, F32)])
    ao, lse = _attention_forward(fqk, proj, fx, frow, rows)

    w_on = jnp.tile(out_norm_w.reshape(1, -1), (1, 2))
    z_off, fg_off = 1536 // LANES, 3584 // LANES
    mix_g_fn = lambda w, o, z: _head_rms(w, o) * _silu(z)
    mix_f_fn = lambda a, g: a * jax.nn.sigmoid(g)
    (mix_g,) = _tiles(lambda col, w, o, z: (mix_g_fn(w, o, z),), name="mix_gdn", rows=rows, tm=tm, ncol=PAIRS,
                      full_consts=[w_on], row_ins=[(o_gdn, LANES, 0), (proj, LANES, z_off)],
                      row_outs=[(LANES, BF16)])
    (mix_f,) = _tiles(lambda col, a, g: (mix_f_fn(a, g),), name="mix_fox", rows=rows, tm=tm, ncol=PAIRS,
                      row_ins=[(ao, LANES, 0), (proj, LANES, fg_off)], row_outs=[(LANES, BF16)])
    mix = jnp.concatenate([mix_g, mix_f], axis=1)
    x1 = _mm(mix, w_out, dims="nn", name="out_proj", add=x, tk=1024)

    (h2,) = _tiles(lambda col, w, xx: (_rms(xx, w),), name="norm2", rows=rows, tm=tm,
                   full_consts=[norm2_w], row_ins=[(x1, D_MODEL, 0)], row_outs=[(D_MODEL, BF16)])
    gate = _mm(h2, w_gate, dims="nn", name="ffn_gate", tn=256, tk=1024)
    up = _mm(h2, w_up, dims="nn", name="ffn_up", tn=256, tk=1024)
    act_fn = lambda g, u: _silu(g) * u
    (act,) = _tiles(lambda col, g, u: (act_fn(g, u),), name="ffn_act", rows=rows, tm=tm,
                    row_ins=[(gate, D_FF, 0), (up, D_FF, 0)], row_outs=[(D_FF, BF16)])
    x2 = _mm(act, w_down, dims="nn", name="ffn_down", add=x1, tk=256)

    def final_fn(col, w, xx, tgt):
        y, vjp = jax.vjp(_rms, xx, w)
        err = y - tgt
        loss = 0.5 * jnp.sum(err * err) / D_MODEL
        dx, dw = vjp(err / D_MODEL)
        return dx, jnp.full((1, LANES), loss, F32), dw

    dx2, loss, d_final_w = _tiles(final_fn, name="final_loss", rows=rows, tm=tm, full_consts=[final_w],
                                  row_ins=[(x2, D_MODEL, 0), (target, D_MODEL, 0)],
                                  row_outs=[(D_MODEL, F32)], acc_outs=[(1, LANES), (1, D_MODEL)])

    dact = _mm(dx2, w_down, dims="nt", name="d_act", tn=256, tk=1024)
    g_down = _mm(act, dx2, dims="tn", name="g_down", tm=1408)

    def act_bwd(col, g, u, d):
        _, vjp = jax.vjp(act_fn, g, u)
        return vjp(d)

    dgate, dup = _tiles(act_bwd, name="ffn_act_bwd", rows=rows, tm=tm,
                        row_ins=[(gate, D_FF, 0), (up, D_FF, 0), (dact, D_FF, 0)],
                        row_outs=[(D_FF, BF16), (D_FF, BF16)])
    dh2 = _mm(dgate, w_gate, dims="nt", name="d_h2_gate", tk=256)
    dh2 = _mm(dup, w_up, dims="nt", name="d_h2_up", tk=256, add=dh2)
    g_gate = _mm(h2, dgate, dims="tn", name="g_gate", tn=256)
    g_up = _mm(h2, dup, dims="tn", name="g_up", tn=256)

    def norm_bwd(col, w, xx, dh, dres):
        _, vjp = jax.vjp(_rms, xx, w)
        dx, dw = vjp(dh)
        return dx + dres, dw

    dx1, d_norm2_w = _tiles(norm_bwd, name="norm2_bwd", rows=rows, tm=tm, full_consts=[norm2_w],
                            row_ins=[(x1, D_MODEL, 0), (dh2, D_MODEL, 0), (dx2, D_MODEL, 0)],
                            row_outs=[(D_MODEL, F32)], acc_outs=[(1, D_MODEL)])
    dmix = _mm(dx1, w_out, dims="nt", name="d_mix", tk=1024)
    g_out = _mm(mix, dx1, dims="tn", name="g_out")

    def mix_g_bwd(col, w, o, z, d):
        _, vjp = jax.vjp(mix_g_fn, w, o, z)
        dw, do_, dz = vjp(d)
        return do_, dz, dw

    do_gdn, dz, d_on = _tiles(mix_g_bwd, name="mix_gdn_bwd", rows=rows, tm=tm, ncol=PAIRS, full_consts=[w_on],
                              row_ins=[(o_gdn, LANES, 0), (proj, LANES, z_off), (dmix, LANES, 0)],
                              row_outs=[(LANES, F32), (LANES, F32)], acc_outs=[(1, LANES)])

    def mix_f_bwd(col, a, g, d):
        _, vjp = jax.vjp(mix_f_fn, a, g)
        return vjp(d)

    dao, dfgate = _tiles(mix_f_bwd, name="mix_fox_bwd", rows=rows, tm=tm, ncol=PAIRS,
                         row_ins=[(ao, LANES, 0), (proj, LANES, fg_off), (dmix, LANES, PAIRS)],
                         row_outs=[(LANES, F32), (LANES, F32)])

    delta = _attention_delta(fqk, proj, fx, frow, lse, dao, rows)
    dfq, dfk, dfv, dfrow = _attention_backward(fqk, proj, fx, frow, delta, lse, dao, rows)
    dfqk_n = jnp.concatenate([dfq, dfk], axis=1)

    def fox_prep_bwd(col, w, xx, d):
        _, vjp = jax.vjp(_head_rms, w, xx)
        dw, dx = vjp(d)
        return dx, dw

    dfqk, d_wqk = _tiles(fox_prep_bwd, name="fox_prep_bwd", rows=rows, tm=tm, ncol=2 * PAIRS,
                         col_consts=[(w_qk, 1, LANES, 0)],
                         row_ins=[(proj, LANES, fox_off), (dfqk_n, LANES, 0)],
                         row_outs=[(LANES, F32)], acc_outs=[(1, LANES)])

    dq, dk, dv, dbetax, dgcx, dgrow = _gdn_backward(qkv, betax, gcx, grow, ssave, tsave, do_gdn, rows)
    dqkv_n = jnp.concatenate([dq, dk, dv], axis=1)
    dqkv, d_conv = _tiles(_gdn_prep_bwd, name="gdn_prep_bwd", rows=rows, tm=rows, ncol=3 * PAIRS,
                          col_consts=[(conv_w, CONV_K, LANES, 0)],
                          row_ins=[(proj, LANES, 0), (dqkv_n, LANES, 0)],
                          row_outs=[(LANES, F32)], acc_outs=[(CONV_K, LANES)])

    def expand_bwd(col, b, g, db, dg):
        return (_dot32(db, b, _CONTRACT["nt"]), _dot32(dg, g, _CONTRACT["nt"]))

    dgates_b, dcums_g = _tiles(expand_bwd, name="expand_bwd", rows=rows, tm=tm, full_consts=[xb, xg],
                               row_ins=[(dbetax, WIDTH, 0), (dgcx, WIDTH, 0)],
                               row_outs=[(LANES, F32), (LANES, F32)])
    dcums_row = jnp.concatenate([jnp.zeros((rows, 8), F32), _rowform_to_lanes(dgrow, rows),
                                 dfrow.reshape(HEADS, rows).T, jnp.zeros((rows, LANES - 24), F32)], axis=1)

    def gates_bwd(col, lcv, lfv, a, dt, fb, pre, dgb, dcg, dcr):
        lane = _lane_ids(pre.shape)
        dgates = jnp.where(lane < 8, dgb, _cums_bwd(lcv, lfv, dcg + dcr))
        _, vjp = jax.vjp(_gates_elem, a, dt, fb, pre)
        da, ddt, dfb, dpre = vjp(dgates)
        return dpre, da, ddt, dfb

    dpre, d_a, d_dt, d_fb = _tiles(gates_bwd, name="gates_bwd", rows=rows, tm=rows,
                                   full_consts=[lc, lf, p_a, p_dt, p_fb],
                                   row_ins=[(proj, LANES, COL_SMALL), (dgates_b, LANES, 0), (dcums_g, LANES, 0),
                                            (dcums_row, LANES, 0)],
                                   row_outs=[(LANES, F32)], acc_outs=[(1, LANES)] * 3)

    dproj = jnp.concatenate([dqkv.astype(BF16), dz.astype(BF16), dfqk.astype(BF16), dfv.astype(BF16),
                             dfgate.astype(BF16), dpre.astype(BF16)], axis=1)
    dh1 = _mm(dproj, w_cat, dims="nt", name="d_h1", tk=384)
    g_cat = _mm(h1, dproj, dims="tn", name="g_in", tn=384)
    grad_x, d_norm1_w = _tiles(norm_bwd, name="norm1_bwd", rows=rows, tm=tm, full_consts=[norm1_w],
                               row_ins=[(x, D_MODEL, 0), (dh1, D_MODEL, 0), (dx1, D_MODEL, 0)],
                               row_outs=[(D_MODEL, F32)], acc_outs=[(1, D_MODEL)])

    fold = lambda v: v.reshape(-1, HEAD_DIM).sum(axis=0)
    small = dict(
        loss=loss[0, 0],
        norm1_w=d_norm1_w, conv_w=d_conv, a_log=d_a[0, 8:16], dt_bias=d_dt[0, 8:16],
        out_norm_w=fold(d_on), f_bias=d_fb[0, 16:24], q_norm_w=fold(d_wqk[:, :WIDTH]),
        k_norm_w=fold(d_wqk[:, WIDTH:]), norm2_w=d_norm2_w, final_w=d_final_w)
    return grad_x, g_cat, g_out, g_gate, g_up, g_down, small


HBM_SPEC = pl.BlockSpec(memory_space=pltpu.HBM)


def _place():
    x, y, c = lax.axis_index("x"), lax.axis_index("y"), lax.axis_index("c")
    chips = [(1 - x, y), (x, 1 - y), (1 - x, 1 - y)]
    return x, y, c, 2 * x + y, (x, y, 1 - c), chips, [2 * cx + cy for cx, cy in chips]


def _remote(src, dst, send_sem, recv_sem, to):
    return pltpu.make_async_remote_copy(src_ref=src, dst_ref=dst, send_sem=send_sem, recv_sem=recv_sem,
                                        device_id=to, device_id_type=MESH)


def _allgather_weights(shards, conv):
    n = len(shards)
    halves = [s.shape[0] // 2 for s in shards]
    per = 6

    def body(*refs):
        ins, conv_in = refs[:n], refs[n]
        outs, conv_out = refs[n + 1:2 * n + 1], refs[2 * n + 1]
        send_sems, recv_sems, local_sems = refs[2 * n + 2:]
        x, y, c, own, sib, chips, chip_idx = _place()

        def half(i, ref, hc):
            return ref.at[pl.ds(pl.multiple_of(hc * halves[i], 16), halves[i]), :]

        locals_ = [pltpu.make_async_copy(ins[i], outs[i].at[own], local_sems.at[i]) for i in range(n)]
        locals_.append(pltpu.make_async_copy(conv_in, conv_out.at[own], local_sems.at[n]))
        for cp in locals_:
            cp.start()
        sent = []
        for i in range(n):
            for j, chip in enumerate(chips):
                k = i * per + j
                sent.append(_remote(half(i, ins[i], c), half(i, outs[i].at[own], c),
                                    send_sems.at[k], recv_sems.at[k], (*chip, c)))
        for j, chip in enumerate(chips):
            k = n * per + j
            sent.append(_remote(conv_in, conv_out.at[own], send_sems.at[k], recv_sems.at[k], (*chip, c)))
        for cp in sent:
            cp.start()
        for i in range(n):
            for j in range(len(chips)):
                k = i * per + j
                landed = half(i, outs[i].at[chip_idx[j]], c)
                _remote(landed, landed, send_sems.at[k], recv_sems.at[k], sib).wait_recv()
                fwd = _remote(landed, landed, send_sems.at[k + 3], recv_sems.at[k + 3], sib)
                fwd.start()
                sent.append(fwd)
        for i in range(n):
            for j in range(len(chips)):
                k = i * per + 3 + j
                landed = half(i, outs[i].at[chip_idx[j]], 1 - c)
                _remote(landed, landed, send_sems.at[k], recv_sems.at[k], sib).wait_recv()
        for j in range(len(chips)):
            k = n * per + j
            landed = conv_out.at[chip_idx[j]]
            _remote(landed, landed, send_sems.at[k], recv_sems.at[k], sib).wait_recv()
        for cp in sent:
            cp.wait_send()
        for cp in locals_:
            cp.wait()

    n_sem = n * per + 3
    out_shape = [jax.ShapeDtypeStruct((N_CHIPS,) + s.shape, s.dtype) for s in shards]
    out_shape.append(jax.ShapeDtypeStruct((N_CHIPS,) + conv.shape, conv.dtype))
    res = pl.pallas_call(
        body, name="allgather_weights", out_shape=out_shape,
        in_specs=[HBM_SPEC] * (n + 1), out_specs=[HBM_SPEC] * (n + 1),
        scratch_shapes=[pltpu.SemaphoreType.DMA((n_sem,)), pltpu.SemaphoreType.DMA((n_sem,)),
                        pltpu.SemaphoreType.DMA((n + 1,))],
    )(*shards, conv)
    return res[:n], res[n]


def _swap_halves(stacks):
    n = len(stacks)

    def body(*refs):
        ins, outs = refs[:n], refs[n:2 * n]
        send_sems, recv_sems = refs[2 * n:]
        x, y, c, own, sib, chips, chip_idx = _place()
        cps = []
        for i in range(n):
            h = stacks[i].shape[1] // 2
            src = ins[i].at[:, pl.ds(pl.multiple_of((1 - c) * h, 8), h), :]
            cps.append(_remote(src, outs[i], send_sems.at[i], recv_sems.at[i], sib))
        for cp in cps:
            cp.start()
        for cp in cps:
            cp.wait()

    out_shape = [jax.ShapeDtypeStruct((N_CHIPS, s.shape[1] // 2, s.shape[2]), s.dtype) for s in stacks]
    return pl.pallas_call(
        body, name="rs_swap_halves", out_shape=out_shape,
        in_specs=[HBM_SPEC] * n, out_specs=[HBM_SPEC] * n,
        scratch_shapes=[pltpu.SemaphoreType.DMA((n,)), pltpu.SemaphoreType.DMA((n,))],
    )(*stacks)


def _add_half(stack, landed, c_arr, name):
    _, h, cols = landed.shape

    def body(c_ref, a_ref, b_ref, o_ref):
        o_ref[...] = (a_ref[...] + b_ref[...]).astype(o_ref.dtype)

    return pl.pallas_call(
        body, name=name, out_shape=jax.ShapeDtypeStruct(landed.shape, BF16),
        grid_spec=pltpu.PrefetchScalarGridSpec(
            num_scalar_prefetch=1, grid=(N_CHIPS,),
            in_specs=[pl.BlockSpec((1, h, cols), lambda j, c_ref: (j, c_ref[0], 0)),
                      pl.BlockSpec((1, h, cols), lambda j, c_ref: (j, 0, 0))],
            out_specs=pl.BlockSpec((1, h, cols), lambda j, c_ref: (j, 0, 0))),
        compiler_params=_params(("arbitrary",)),
    )(c_arr, stack, landed)


def _exchange_partials(parts):
    n = len(parts)

    def body(*refs):
        ins, outs = refs[:n], refs[n:2 * n]
        send_sems, recv_sems, local_sems = refs[2 * n:]
        x, y, c, own, sib, chips, chip_idx = _place()
        locals_ = [pltpu.make_async_copy(ins[i].at[own], outs[i].at[own], local_sems.at[i]) for i in range(n)]
        for cp in locals_:
            cp.start()
        sent = []
        for i in range(n):
            for j, chip in enumerate(chips):
                k = i * 3 + j
                sent.append(_remote(ins[i].at[chip_idx[j]], outs[i].at[own], send_sems.at[k], recv_sems.at[k],
                                    (*chip, c)))
        for cp in sent:
            cp.start()
        for i in range(n):
            for j in range(len(chips)):
                k = i * 3 + j
                landed = outs[i].at[chip_idx[j]]
                _remote(landed, landed, send_sems.at[k], recv_sems.at[k], sib).wait_recv()
        for cp in sent:
            cp.wait_send()
        for cp in locals_:
            cp.wait()

    return pl.pallas_call(
        body, name="rs_exchange_partials",
        out_shape=[jax.ShapeDtypeStruct(p.shape, p.dtype) for p in parts],
        in_specs=[HBM_SPEC] * n, out_specs=[HBM_SPEC] * n,
        scratch_shapes=[pltpu.SemaphoreType.DMA((3 * n,)), pltpu.SemaphoreType.DMA((3 * n,)),
                        pltpu.SemaphoreType.DMA((n,))],
    )(*parts)


def _sum_partials(landed, name):
    _, h, cols = landed.shape

    def body(a_ref, o_ref):
        acc = a_ref[0].astype(F32)
        for s in range(1, N_CHIPS):
            acc = acc + a_ref[s].astype(F32)
        o_ref[...] = acc

    return pl.pallas_call(
        body, name=name, out_shape=jax.ShapeDtypeStruct((h, cols), F32), grid=(1,),
        in_specs=[pl.BlockSpec(landed.shape, lambda i: (0, 0, 0))],
        out_specs=pl.BlockSpec((h, cols), lambda i: (0, 0)),
        compiler_params=_params(("arbitrary",)),
    )(landed)


def _share_halves(halves):
    n = len(halves)

    def body(*refs):
        ins, outs = refs[:n], refs[n:2 * n]
        send_sems, recv_sems, local_sems = refs[2 * n:]
        x, y, c, own, sib, chips, chip_idx = _place()
        cps, locals_ = [], []
        for i in range(n):
            h = halves[i].shape[0]
            mine = outs[i].at[pl.ds(pl.multiple_of(c * h, 8), h), :]
            locals_.append(pltpu.make_async_copy(ins[i], mine, local_sems.at[i]))
            cps.append(_remote(ins[i], mine, send_sems.at[i], recv_sems.at[i], sib))
        for cp in locals_ + cps:
            cp.start()
        for i in range(n):
            h = halves[i].shape[0]
            other = outs[i].at[pl.ds(pl.multiple_of((1 - c) * h, 8), h), :]
            _remote(other, other, send_sems.at[i], recv_sems.at[i], sib).wait_recv()
        for cp in cps:
            cp.wait_send()
        for cp in locals_:
            cp.wait()

    return pl.pallas_call(
        body, name="rs_share_halves",
        out_shape=[jax.ShapeDtypeStruct((2 * p.shape[0], p.shape[1]), p.dtype) for p in halves],
        in_specs=[HBM_SPEC] * n, out_specs=[HBM_SPEC] * n,
        scratch_shapes=[pltpu.SemaphoreType.DMA((n,)), pltpu.SemaphoreType.DMA((n,)),
                        pltpu.SemaphoreType.DMA((n,))],
    )(*halves)


def _allreduce_small(packed):
    rows = packed.shape[0]
    n_dev = 8

    def body(in_ref, out_ref, gath, send_sems, recv_sems):
        x, y, c = lax.axis_index("x"), lax.axis_index("y"), lax.axis_index("c")
        me = 4 * x + 2 * y + c
        gath[me] = in_ref[...]
        cps = []
        for k in range(1, n_dev):
            fx, fy, fc = (k >> 2) & 1, (k >> 1) & 1, k & 1
            to = (x ^ fx, y ^ fy, c ^ fc)
            cps.append(_remote(in_ref, gath.at[me], send_sems.at[k - 1], recv_sems.at[k - 1], to))
        for cp in cps:
            cp.start()
        for k in range(1, n_dev):
            fx, fy, fc = (k >> 2) & 1, (k >> 1) & 1, k & 1
            src = 4 * (x ^ fx) + 2 * (y ^ fy) + (c ^ fc)
            slot = gath.at[src]
            _remote(slot, slot, send_sems.at[k - 1], recv_sems.at[k - 1], (x, y, c)).wait_recv()
        for cp in cps:
            cp.wait_send()
        acc = gath[0]
        for d in range(1, n_dev):
            acc = acc + gath[d]
        out_ref[...] = acc

    vm = pl.BlockSpec(memory_space=pltpu.VMEM)
    return pl.pallas_call(
        body, name="allreduce_small", out_shape=jax.ShapeDtypeStruct(packed.shape, F32),
        in_specs=[vm], out_specs=vm,
        scratch_shapes=[pltpu.VMEM((n_dev, rows, LANES), F32),
                        pltpu.SemaphoreType.DMA((n_dev - 1,)), pltpu.SemaphoreType.DMA((n_dev - 1,))],
    )(packed)


def _adam(col, w, g, m, v):
    m2 = ADAM_B1 * m + (1.0 - ADAM_B1) * g
    v2 = ADAM_B2 * v + (1.0 - ADAM_B2) * (g * g)
    m_hat = m2 / (1.0 - ADAM_B1 ** ADAM_STEP)
    v_hat = v2 / (1.0 - ADAM_B2 ** ADAM_STEP)
    delta = -ADAM_LR * (m_hat / (jnp.sqrt(v_hat) + ADAM_EPS) + ADAM_WD * w)
    return delta, m2, v2


def _adam_call(w, g, m, v, name):
    rows, cols = w.shape
    tm = rows
    for cand in (256, 352, 176, 128, 64, 48, 16, 8):
        if rows % cand == 0:
            tm = cand
            break
    return _tiles(_adam, name=name, rows=rows, tm=tm,
                  row_ins=[(w, cols, 0), (g, cols, 0), (m, cols, 0), (v, cols, 0)],
                  row_outs=[(cols, F32)] * 3)


def _pack(arrays):
    flat = []
    for a in arrays:
        a = a.reshape(-1).astype(F32)
        flat.append(jnp.pad(a, (0, (-a.size) % LANES)))
    out = jnp.concatenate(flat)
    out = jnp.pad(out, (0, (-out.size) % (8 * LANES)))
    return out.reshape(-1, LANES)


def _unpack(packed, shapes):
    flat = packed.reshape(-1)
    out, off = [], 0
    for s in shapes:
        size = int(np.prod(s))
        out.append(flat[off:off + size].reshape(s))
        off += size + (-size) % LANES
    return out


def kernel(x, norm1_w, w_in, gdn_conv_w, gdn_A_log, gdn_dt_bias, gdn_out_norm_w, fox_f_bias, fox_q_norm_w, fox_k_norm_w, w_out, norm2_w, w_ffn_gate, w_ffn_up, w_ffn_down, final_norm_w, loss_target, m_norm1_w, m_w_in, m_gdn_conv_w, m_gdn_A_log, m_gdn_dt_bias, m_gdn_out_norm_w, m_fox_f_bias, m_fox_q_norm_w, m_fox_k_norm_w, m_w_out, m_norm2_w, m_w_ffn_gate, m_w_ffn_up, m_w_ffn_down, m_final_norm_w, v_norm1_w, v_w_in, v_gdn_conv_w, v_gdn_A_log, v_gdn_dt_bias, v_gdn_out_norm_w, v_fox_f_bias, v_fox_q_norm_w, v_fox_k_norm_w, v_w_out, v_norm2_w, v_w_ffn_gate, v_w_ffn_up, v_w_ffn_down, v_final_norm_w):
    cx, cy, cc = lax.axis_index("x"), lax.axis_index("y"), lax.axis_index("c")
    own = 2 * cx + cy
    c_arr = jnp.reshape(cc, (1,)).astype(jnp.int32)

    big_w = [w_in[0], w_out[0], w_ffn_gate[0], w_ffn_up[0], w_ffn_down[0]]
    gathered, conv_g = _allgather_weights([w.astype(BF16) for w in big_w], gdn_conv_w[0])
    by_cols = lambda g: g.transpose(1, 0, 2).reshape(g.shape[1], N_CHIPS * g.shape[2])
    by_rows = lambda g: g.reshape(N_CHIPS * g.shape[1], g.shape[2])
    w_cat = _cat_weights(by_cols(gathered[0]))
    conv_full = by_cols(conv_g)

    grad_x, g_cat, g_out, g_gate, g_up, g_down, small = _local_step(
        x[0], loss_target[0], norm1_w, w_cat, conv_full, gdn_A_log[0], gdn_dt_bias[0], gdn_out_norm_w[0],
        fox_f_bias[0], fox_q_norm_w[0], fox_k_norm_w[0], by_rows(gathered[1]), norm2_w,
        by_cols(gathered[2]), by_cols(gathered[3]), by_rows(gathered[4]), final_norm_w.reshape(1, -1))

    col_stack = lambda g: g.reshape(g.shape[0], N_CHIPS, g.shape[1] // N_CHIPS).transpose(1, 0, 2)
    row_stack = lambda g: g.reshape(N_CHIPS, g.shape[0] // N_CHIPS, g.shape[1])
    stacks = [col_stack(_uncat_grad(g_cat)), row_stack(g_out), col_stack(g_gate), col_stack(g_up),
              row_stack(g_down)]
    landed = _swap_halves(stacks)
    names = ["w_in", "w_out", "w_gate", "w_up", "w_down"]
    parts = [_add_half(s, l, c_arr, "rs_add_" + nm) for s, l, nm in zip(stacks, landed, names)]
    from_chips = _exchange_partials(parts)
    halves = [_sum_partials(p, "rs_sum_" + nm) for p, nm in zip(from_chips, names)]
    big_g = _share_halves(halves)
    big_m = [m_w_in[0], m_w_out[0], m_w_ffn_gate[0], m_w_ffn_up[0], m_w_ffn_down[0]]
    big_v = [v_w_in[0], v_w_out[0], v_w_ffn_gate[0], v_w_ffn_up[0], v_w_ffn_down[0]]
    big_upd = [_adam_call(w, g, m, v, "adam_" + nm) for w, g, m, v, nm in zip(big_w, big_g, big_m, big_v, names)]

    order = ["norm1_w", "conv_w", "a_log", "dt_bias", "out_norm_w", "f_bias", "q_norm_w", "k_norm_w",
             "norm2_w", "final_w"]
    red = _allreduce_small(_pack([small[k] for k in order] + [small["loss"]]))
    red_shapes = [(1, D_MODEL), (CONV_K, 3 * WIDTH), (1, HEADS), (1, HEADS), (1, HEAD_DIM), (1, HEADS),
                  (1, HEAD_DIM), (1, HEAD_DIM), (1, D_MODEL), (D_MODEL,), ()]
    red_list = _unpack(red, red_shapes)
    loss = red_list[-1]
    small_g = dict(zip(order, red_list[:-1]))
    shard_cols = 3 * WIDTH // N_CHIPS
    small_g["conv_w"] = lax.dynamic_slice_in_dim(small_g["conv_w"], own * shard_cols, shard_cols, axis=1)[None]
    small_w = [norm1_w, gdn_conv_w, gdn_A_log, gdn_dt_bias, gdn_out_norm_w, fox_f_bias, fox_q_norm_w,
               fox_k_norm_w, norm2_w, final_norm_w]
    small_m = [m_norm1_w, m_gdn_conv_w, m_gdn_A_log, m_gdn_dt_bias, m_gdn_out_norm_w, m_fox_f_bias,
               m_fox_q_norm_w, m_fox_k_norm_w, m_norm2_w, m_final_norm_w]
    small_v = [v_norm1_w, v_gdn_conv_w, v_gdn_A_log, v_gdn_dt_bias, v_gdn_out_norm_w, v_fox_f_bias,
               v_fox_q_norm_w, v_fox_k_norm_w, v_norm2_w, v_final_norm_w]
    small_gl = [small_g[k].reshape(w.shape) for k, w in zip(order, small_w)]
    s_delta, s_m, s_v = _adam_call(_pack(small_w), _pack(small_gl), _pack(small_m), _pack(small_v), "adam_small")
    shapes = [w.shape for w in small_w]
    s_delta, s_m, s_v = _unpack(s_delta, shapes), _unpack(s_m, shapes), _unpack(s_v, shapes)

    big_pos = {1: 0, 9: 1, 11: 2, 12: 3, 13: 4}
    small_pos = {0: 0, 2: 1, 3: 2, 4: 3, 5: 4, 6: 5, 7: 6, 8: 7, 10: 8, 14: 9}
    grads, deltas, new_m, new_v = [], [], [], []
    for pos in range(15):
        if pos in big_pos:
            b = big_pos[pos]
            d, m2, v2 = big_upd[b]
            grads.append(big_g[b][None])
            deltas.append(d[None])
            new_m.append(m2[None])
            new_v.append(v2[None])
        else:
            s = small_pos[pos]
            grads.append(small_gl[s])
            deltas.append(s_delta[s])
            new_m.append(s_m[s])
            new_v.append(s_v[s])
    return (loss, grad_x[None], *grads, *deltas, *new_m, *new_v)
```

```python
import jax
import jax.numpy as jnp
import numpy as np
from jax import lax
from jax.experimental import pallas as pl
from jax.experimental.pallas import tpu as pltpu

F32 = jnp.float32
BF16 = jnp.bfloat16

D_MODEL = 1024
HEADS = 8
HEAD_DIM = 64
PAIRS = HEADS // 2
WIDTH = HEADS * HEAD_DIM
CHUNK = 64
CONV_K = 4
D_FF = 2816
EPS = 1e-6
SCALE = HEAD_DIM ** -0.5
LANES = 128
N_CHIPS = 4
D_IN = 4120
D_CAT = 4224
COL_SMALL = 4096 // LANES

ADAM_LR = 0.001
ADAM_B1 = 0.9
ADAM_B2 = 0.999
ADAM_EPS = 1e-08
ADAM_WD = 0.01
ADAM_STEP = 10

VMEM_LIMIT = 56 * 1024 * 1024
MESH = pl.DeviceIdType.MESH
HIGHEST = lax.Precision.HIGHEST


def _params(sem):
    return pltpu.CompilerParams(dimension_semantics=sem, vmem_limit_bytes=VMEM_LIMIT)


_CONTRACT = {"nn": ((1,), (0,)), "nt": ((1,), (1,)), "tn": ((0,), (0,))}


def _mm(a, b, *, dims, name, out_dtype=F32, add=None, tm=1024, tn=512, tk=512):
    if dims == "nn":
        (m, k), (k2, n) = a.shape, b.shape
    elif dims == "nt":
        (m, k), (n, k2) = a.shape, b.shape
    else:
        (k, m), (k2, n) = a.shape, b.shape
    assert k == k2, (a.shape, b.shape, dims)
    tm, tn, tk = min(tm, m), min(tn, n), min(tk, k)
    assert m % tm == 0 and n % tn == 0 and k % tk == 0, (m, n, k, tm, tn, tk)
    nk = k // tk
    a_spec = (pl.BlockSpec((tk, tm), lambda i, j, kk: (kk, i)) if dims == "tn"
              else pl.BlockSpec((tm, tk), lambda i, j, kk: (i, kk)))
    b_spec = (pl.BlockSpec((tn, tk), lambda i, j, kk: (j, kk)) if dims == "nt"
              else pl.BlockSpec((tk, tn), lambda i, j, kk: (kk, j)))
    o_spec = pl.BlockSpec((tm, tn), lambda i, j, kk: (i, j))
    contract = (_CONTRACT[dims], ((), ()))
    has_add = add is not None

    def body(*refs):
        a_ref, b_ref = refs[:2]
        add_ref = refs[2] if has_add else None
        o_ref = refs[3] if has_add else refs[2]
        part = lax.dot_general(a_ref[...].astype(BF16), b_ref[...].astype(BF16), contract,
                               preferred_element_type=F32)

        def finish(r):
            if has_add:
                r = r + add_ref[...].astype(F32)
            o_ref[...] = r.astype(out_dtype)

        if nk == 1:
            finish(part)
            return
        acc = refs[-1]
        kk = pl.program_id(2)

        @pl.when(kk == 0)
        def _():
            acc[...] = part

        @pl.when(kk > 0)
        def _():
            acc[...] += part

        @pl.when(kk == nk - 1)
        def _():
            finish(acc[...])

    ins = [a, b] + ([add] if has_add else [])
    in_specs = [a_spec, b_spec] + ([o_spec] if has_add else [])
    return pl.pallas_call(
        body, name=name, grid=(m // tm, n // tn, nk),
        in_specs=in_specs, out_specs=o_spec,
        out_shape=jax.ShapeDtypeStruct((m, n), out_dtype),
        scratch_shapes=[pltpu.VMEM((tm, tn), F32)] if nk > 1 else [],
        compiler_params=_params(("parallel", "parallel", "arbitrary")),
    )(*ins)


def _tiles(fn, *, name, rows, tm, ncol=1, row_ins=(), col_consts=(), full_consts=(),
           row_outs=(), acc_outs=()):
    nt = rows // tm
    assert rows % tm == 0
    n_full, n_col, n_row = len(full_consts), len(col_consts), len(row_ins)
    n_ro, n_acc = len(row_outs), len(acc_outs)

    def body(*refs):
        ins = refs[:n_full + n_col + n_row]
        outs = refs[n_full + n_col + n_row:]
        i = pl.program_id(1)
        res = fn(pl.program_id(0), *[r[...] for r in ins])
        for r, v in zip(outs[:n_ro], res[:n_ro]):
            r[...] = v.astype(r.dtype)
        if n_acc:
            @pl.when(i == 0)
            def _():
                for r in outs[n_ro:]:
                    r[...] = jnp.zeros_like(r)
            for r, v in zip(outs[n_ro:], res[n_ro:]):
                r[...] += v

    in_specs = [pl.BlockSpec(a.shape, lambda j, i, nd=a.ndim: (0,) * nd) for a in full_consts]
    in_specs += [pl.BlockSpec((nr, w), lambda j, i, o=o: (0, o + j)) for (_, nr, w, o) in col_consts]
    in_specs += [pl.BlockSpec((tm, w), lambda j, i, o=o: (i, o + j)) for (_, w, o) in row_ins]
    out_specs = [pl.BlockSpec((tm, w), lambda j, i: (i, j)) for (w, _) in row_outs]
    out_specs += [pl.BlockSpec((nr, w), lambda j, i: (0, j)) for (nr, w) in acc_outs]
    out_shape = [jax.ShapeDtypeStruct((rows, w * ncol), dt) for (w, dt) in row_outs]
    out_shape += [jax.ShapeDtypeStruct((nr, w * ncol), F32) for (nr, w) in acc_outs]
    args = list(full_consts) + [c[0] for c in col_consts] + [r[0] for r in row_ins]
    out = pl.pallas_call(
        body, name=name, grid=(ncol, nt), in_specs=in_specs, out_specs=out_specs, out_shape=out_shape,
        compiler_params=_params(("parallel", "arbitrary")),
    )(*args)
    return out


def _rms(x, w):
    return x * lax.rsqrt(jnp.mean(x * x, axis=-1, keepdims=True) + EPS) * w


def _lane_lo(shape):
    return lax.broadcasted_iota(jnp.int32, shape, len(shape) - 1) < HEAD_DIM


def _pair_sum(x):
    lo = _lane_lo(x.shape)
    s0 = jnp.sum(jnp.where(lo, x, 0.0), axis=-1, keepdims=True)
    s1 = jnp.sum(jnp.where(lo, 0.0, x), axis=-1, keepdims=True)
    return jnp.where(lo, s0, s1)


def _head_col(x, lo, h):
    keep = lo if h == 0 else jnp.logical_not(lo)
    return jnp.max(jnp.where(keep, x, -jnp.inf), axis=-1, keepdims=True)


def _softplus(x):
    return jnp.maximum(x, 0.0) + jnp.log1p(jnp.exp(-jnp.abs(x)))


def _silu(x):
    return x * jax.nn.sigmoid(x)


def _dot(a, b, contract):
    return lax.dot_general(a.astype(BF16), b.astype(BF16), (contract, ((), ())),
                           preferred_element_type=F32)


def _dot32(a, b, contract):
    return lax.dot_general(a, b, (contract, ((), ())), precision=HIGHEST, preferred_element_type=F32)


def _bd(y):
    yy = jnp.concatenate([y, y], axis=0)
    r = lax.broadcasted_iota(jnp.int32, yy.shape, 0) < HEAD_DIM
    c = lax.broadcasted_iota(jnp.int32, yy.shape, 1) < HEAD_DIM
    return jnp.where(r == c, yy, 0.0)


def _pp(x, y):
    return _dot(x, _bd(y), _CONTRACT["nn"])


def _pp_nt(x, y):
    return _dot(x, _bd(y), _CONTRACT["nt"])


def _pp_tn(x, y):
    full = _dot(x, y, _CONTRACT["tn"])
    return jnp.where(_lane_lo((HEAD_DIM, LANES)), full[:HEAD_DIM], full[HEAD_DIM:])


def _gdn_masks():
    row = lax.broadcasted_iota(jnp.int32, (CHUNK, LANES), 0)
    col = lax.broadcasted_iota(jnp.int32, (CHUNK, LANES), 1) % HEAD_DIM
    return row, col


def _gdn_chunk(q, k, v, bx, gx, gr):
    row, col = _gdn_masks()
    incl, strict = col <= row, col < row
    dm = jnp.where(incl, jnp.exp(jnp.minimum(gx - gr, 0.0)), 0.0)
    kb = k * bx
    vb = v * bx
    big_g = _pp_nt(kb, k)
    low = jnp.where(strict, big_g * dm, 0.0)
    eg = jnp.exp(gx)
    qs = q * SCALE
    big_p = _pp_nt(qs, k)
    att = jnp.where(incl, big_p * dm, 0.0)
    return incl, strict, dm, kb, vb, low, eg, qs, att


def _gdn_forward(qkv, betax, gcx, grow, rows):
    nchunk = rows // CHUNK

    def body(q_ref, k_ref, v_ref, bx_ref, gx_ref, gr_ref, o_ref, ss_ref, ts_ref, state):
        n = pl.program_id(1)

        @pl.when(n == 0)
        def _():
            state[...] = jnp.zeros_like(state)

        q, k, v, bx, gx = q_ref[...], k_ref[...], v_ref[...], bx_ref[...], gx_ref[...]
        gr = gr_ref[0, 0]
        glast = gx_ref[pl.ds(CHUNK - 1, 1), :]
        incl, strict, dm, kb, vb, low, eg, qs, att = _gdn_chunk(q, k, v, bx, gx, gr)
        row, col = _gdn_masks()
        x = -low
        tm = jnp.where(row == col, 1.0, 0.0) + x
        for _ in range(5):
            x = _pp(x, x)
            tm = tm + _pp(tm, x)
        u = _pp(tm, vb)
        w = _pp(tm, kb * eg)
        s = state[...]
        ss_ref[0, 0] = s
        ts_ref[0, 0] = tm
        vn = u - _pp(w, s)
        o_ref[...] = _pp(qs * eg, s) + _pp(att, vn)
        kd = k * jnp.exp(glast - gx)
        state[...] = s * jnp.exp(glast) + _pp_tn(kd, vn)

    blk = lambda off: pl.BlockSpec((CHUNK, LANES), lambda p, n, off=off: (n, off + p))
    sv = pl.BlockSpec((1, 1, CHUNK, LANES), lambda p, n: (n, p, 0, 0))
    return pl.pallas_call(
        body, name="gdn_fwd", grid=(PAIRS, nchunk),
        in_specs=[blk(0), blk(PAIRS), blk(2 * PAIRS), blk(0), blk(0),
                  pl.BlockSpec((1, 1, 1, LANES), lambda p, n: (n, p, 0, 0))],
        out_specs=[blk(0), sv, sv],
        out_shape=[jax.ShapeDtypeStruct((rows, WIDTH), F32),
                   jax.ShapeDtypeStruct((nchunk, PAIRS, CHUNK, LANES), F32),
                   jax.ShapeDtypeStruct((nchunk, PAIRS, CHUNK, LANES), F32)],
        scratch_shapes=[pltpu.VMEM((CHUNK, LANES), F32)],
        compiler_params=_params(("parallel", "arbitrary")),
    )(qkv, qkv, qkv, betax, gcx, grow)


def _gdn_backward(qkv, betax, gcx, grow, ssave, tsave, do, rows):
    nchunk = rows // CHUNK

    def body(q_ref, k_ref, v_ref, bx_ref, gx_ref, gr_ref, ss_ref, ts_ref, do_ref,
             dq_ref, dk_ref, dv_ref, dbx_ref, dgx_ref, dgr_ref, dstate):
        n = pl.program_id(1)

        @pl.when(n == 0)
        def _():
            dstate[...] = jnp.zeros_like(dstate)

        q, k, v, bx, gx = q_ref[...], k_ref[...], v_ref[...], bx_ref[...], gx_ref[...]
        gr = gr_ref[0, 0]
        glast = gx_ref[pl.ds(CHUNK - 1, 1), :]
        s, tm, d_o = ss_ref[0, 0], ts_ref[0, 0], do_ref[...]
        ds_out = dstate[...]
        incl, strict, dm, kb, vb, low, eg, qs, att = _gdn_chunk(q, k, v, bx, gx, gr)
        row, col = _gdn_masks()
        kbg = kb * eg
        u = _pp(tm, vb)
        w = _pp(tm, kbg)
        vn = u - _pp(w, s)
        qg = qs * eg
        ed = jnp.exp(glast - gx)
        kd = k * ed
        eglast = jnp.exp(glast)

        dkd = _pp_nt(vn, ds_out)
        dvn = _pp(kd, ds_out) + _pp_tn(att, d_o)
        dqg = _pp_nt(d_o, s)
        datt = jnp.where(incl, _pp_nt(d_o, vn), 0.0)
        dw = -_pp_nt(dvn, s)
        dtm = _pp_nt(dvn, vb) + _pp_nt(dw, kbg)
        dvb = _pp_tn(tm, dvn)
        dkbg = _pp_tn(tm, dw)
        dlow = jnp.where(strict, -_pp_nt(_pp_tn(tm, dtm), tm), 0.0)
        dbig_g = dlow * dm
        dbig_p = datt * dm
        dkb = _pp(dbig_g, k) + dkbg * eg
        dqs = _pp(dbig_p, k) + dqg * eg
        dk = _pp_tn(dbig_g, kb) + _pp_tn(dbig_p, qs) + dkd * ed + dkb * bx
        z = dlow * low + datt * att
        kdterm = dkd * kd
        dglast = (jnp.sum(ds_out * s, axis=0, keepdims=True) * eglast
                  + jnp.sum(kdterm, axis=0, keepdims=True))
        dgx = dqg * qg + dkbg * kbg - kdterm
        dgx = dgx + jnp.where(col == 0, _pair_sum(z), 0.0)
        dgx = dgx + jnp.where(row == CHUNK - 1, dglast, 0.0)

        dq_ref[...] = dqs * SCALE
        dk_ref[...] = dk
        dv_ref[...] = dvb * bx
        dbx_ref[...] = dkb * k + dvb * v
        dgx_ref[...] = dgx
        dgr_ref[0, 0] = -jnp.sum(z, axis=0, keepdims=True)
        dstate[...] = ds_out * eglast + _pp_tn(qg, d_o) - _pp_tn(w, dvn)

    last = nchunk - 1
    blk = lambda off: pl.BlockSpec((CHUNK, LANES), lambda p, n, off=off: (last - n, off + p))
    sv = pl.BlockSpec((1, 1, CHUNK, LANES), lambda p, n: (last - n, p, 0, 0))
    gr_spec = pl.BlockSpec((1, 1, 1, LANES), lambda p, n: (last - n, p, 0, 0))
    wide = jax.ShapeDtypeStruct((rows, WIDTH), F32)
    return pl.pallas_call(
        body, name="gdn_bwd", grid=(PAIRS, nchunk),
        in_specs=[blk(0), blk(PAIRS), blk(2 * PAIRS), blk(0), blk(0), gr_spec, sv, sv, blk(0)],
        out_specs=[blk(0)] * 5 + [gr_spec],
        out_shape=[wide] * 5 + [jax.ShapeDtypeStruct((nchunk, PAIRS, 1, LANES), F32)],
        scratch_shapes=[pltpu.VMEM((CHUNK, LANES), F32)],
        compiler_params=_params(("parallel", "arbitrary")),
    )(qkv, qkv, qkv, betax, gcx, grow, ssave, tsave, do)


ATT_TQ = 256


def _att_scores(qh, kt, fq, fk, q0, k0, tq, tk):
    s = _dot(qh, kt, _CONTRACT["nt"]) * SCALE + fq - fk
    rq = q0 + lax.broadcasted_iota(jnp.int32, (tq, tk), 0)
    ck = k0 + lax.broadcasted_iota(jnp.int32, (tq, tk), 1)
    return jnp.where(rq >= ck, s, -jnp.inf)


def _attention_forward(fqk, proj, fx, frow, rows):
    tq = tk = min(ATT_TQ, rows)
    nq = rows // tq
    v_off = 3072 // LANES

    def body(q_ref, k_ref, v_ref, fx_ref, fr_ref, o_ref, lse_ref):
        qi = pl.program_id(1)
        q0 = qi * tq
        q = q_ref[...]
        fx_t = fx_ref[...]
        lo_q = _lane_lo((tq, LANES))
        lo_k = _lane_lo((tk, LANES))
        outs, lses = [], []
        for h in range(2):
            keep_q = lo_q if h == 0 else jnp.logical_not(lo_q)
            keep_k = lo_k if h == 0 else jnp.logical_not(lo_k)
            qh = jnp.where(keep_q, q, 0.0).astype(BF16)
            fq = _head_col(fx_t, lo_q, h)

            def step(ki, carry, qh=qh, fq=fq, keep_k=keep_k, h=h):
                m, l, acc = carry
                k0 = pl.multiple_of(ki * tk, tk)
                kt = k_ref[pl.ds(k0, tk), :]
                vt = jnp.where(keep_k, v_ref[pl.ds(k0, tk), :], 0.0)
                fk = fr_ref[0, pl.ds(h, 1), pl.ds(k0, tk)]
                s = _att_scores(qh, kt, fq, fk, q0, k0, tq, tk)
                m_new = jnp.maximum(m, jnp.max(s, axis=-1, keepdims=True))
                p = jnp.exp(s - m_new)
                alpha = jnp.exp(m - m_new)
                l = alpha * l + jnp.sum(p, axis=-1, keepdims=True)
                acc = alpha * acc + _dot(p, vt, _CONTRACT["nn"])
                return m_new, l, acc

            init = (jnp.full((tq, 1), -jnp.inf, F32), jnp.zeros((tq, 1), F32), jnp.zeros((tq, LANES), F32))
            m, l, acc = lax.fori_loop(0, qi + 1, step, init)
            outs.append(acc / l)
            lses.append(m + jnp.log(l))
        o_ref[...] = outs[0] + outs[1]
        lse_ref[...] = jnp.where(lo_q, lses[0], lses[1])

    whole = lambda off: pl.BlockSpec((rows, LANES), lambda p, i, off=off: (0, off + p))
    qblk = lambda off: pl.BlockSpec((tq, LANES), lambda p, i, off=off: (i, off + p))
    wide = jax.ShapeDtypeStruct((rows, WIDTH), F32)
    return pl.pallas_call(
        body, name="fox_fwd", grid=(PAIRS, nq),
        in_specs=[qblk(0), whole(PAIRS), whole(v_off), qblk(0),
                  pl.BlockSpec((1, 2, rows), lambda p, i: (p, 0, 0))],
        out_specs=[qblk(0), qblk(0)], out_shape=[wide, wide],
        compiler_params=_params(("parallel", "arbitrary")),
    )(fqk, fqk, proj, fx, frow)


def _attention_delta(fqk, proj, fx, frow, lse, dao, rows):
    tq = tk = min(ATT_TQ, rows)
    nq = rows // tq
    v_off = 3072 // LANES

    def body(q_ref, k_ref, v_ref, fx_ref, fr_ref, lse_ref, do_ref, delta_ref):
        qi = pl.program_id(1)
        q0 = qi * tq
        q, d_o, fx_t, lse_t = q_ref[...], do_ref[...], fx_ref[...], lse_ref[...]
        lo_q = _lane_lo((tq, LANES))
        deltas = []
        for h in range(2):
            keep_q = lo_q if h == 0 else jnp.logical_not(lo_q)
            qh = jnp.where(keep_q, q, 0.0).astype(BF16)
            doh = jnp.where(keep_q, d_o, 0.0).astype(BF16)
            fq = _head_col(fx_t, lo_q, h)
            lse_h = _head_col(lse_t, lo_q, h)

            def step(ki, acc, qh=qh, doh=doh, fq=fq, lse_h=lse_h, h=h):
                k0 = pl.multiple_of(ki * tk, tk)
                kt = k_ref[pl.ds(k0, tk), :]
                vt = v_ref[pl.ds(k0, tk), :]
                fk = fr_ref[0, pl.ds(h, 1), pl.ds(k0, tk)]
                p = jnp.exp(_att_scores(qh, kt, fq, fk, q0, k0, tq, tk) - lse_h)
                dp = _dot(doh, vt, _CONTRACT["nt"])
                return acc + jnp.sum(p * dp, axis=-1, keepdims=True)

            deltas.append(lax.fori_loop(0, qi + 1, step, jnp.zeros((tq, 1), F32)))
        delta_ref[...] = jnp.where(lo_q, deltas[0], deltas[1])

    whole = lambda off: pl.BlockSpec((rows, LANES), lambda p, i, off=off: (0, off + p))
    qblk = lambda off: pl.BlockSpec((tq, LANES), lambda p, i, off=off: (i, off + p))
    return pl.pallas_call(
        body, name="fox_delta", grid=(PAIRS, nq),
        in_specs=[qblk(0), whole(PAIRS), whole(v_off), qblk(0),
                  pl.BlockSpec((1, 2, rows), lambda p, i: (p, 0, 0)), qblk(0), qblk(0)],
        out_specs=qblk(0), out_shape=jax.ShapeDtypeStruct((rows, WIDTH), F32),
        compiler_params=_params(("parallel", "arbitrary")),
    )(fqk, fqk, proj, fx, frow, lse, dao)


def _attention_backward(fqk, proj, fx, frow, delta, lse, dao, rows):
    tq = tk = min(ATT_TQ, rows)
    nq = rows // tq
    v_off = 3072 // LANES

    def body(q_ref, k_ref, v_ref, fx_ref, fr_ref, delta_ref, lse_ref, do_ref,
             dq_ref, dk_ref, dv_ref, dfr_ref):
        ki = pl.program_id(1)
        k0 = ki * tk

        @pl.when(ki == 0)
        def _():
            dq_ref[...] = jnp.zeros_like(dq_ref)

        lo_q = _lane_lo((tq, LANES))
        lo_k = _lane_lo((tk, LANES))
        kt = k_ref[...]
        vt = v_ref[...]

        def step(qi, carry):
            dk, dv, df0, df1 = carry
            q0 = pl.multiple_of(qi * tq, tq)
            rows_q = pl.ds(q0, tq)
            q, d_o, delta_x = q_ref[rows_q, :], do_ref[rows_q, :], delta_ref[rows_q, :]
            lse_t, fx_t = lse_ref[rows_q, :], fx_ref[rows_q, :]
            dq = jnp.zeros((tq, LANES), F32)
            dfs = []
            for h in range(2):
                keep_q = lo_q if h == 0 else jnp.logical_not(lo_q)
                keep_k = lo_k if h == 0 else jnp.logical_not(lo_k)
                qh = jnp.where(keep_q, q, 0.0).astype(BF16)
                doh = jnp.where(keep_q, d_o, 0.0).astype(BF16)
                fq = _head_col(fx_t, lo_q, h)
                fk = fr_ref[0, pl.ds(h, 1), :]
                s = _att_scores(qh, kt, fq, fk, q0, k0, tq, tk)
                p = jnp.exp(s - _head_col(lse_t, lo_q, h))
                dp = _dot(doh, vt, _CONTRACT["nt"])
                ds = p * (dp - _head_col(delta_x, lo_q, h))
                dv = dv + _dot(p, doh, _CONTRACT["tn"])
                dk = dk + _dot(ds, qh, _CONTRACT["tn"]) * SCALE
                dq = dq + _dot(ds, jnp.where(keep_k, kt, 0.0), _CONTRACT["nn"]) * SCALE
                dfs.append(-jnp.sum(ds, axis=0, keepdims=True))
            dq_ref[rows_q, :] += dq
            return dk, dv, df0 + dfs[0], df1 + dfs[1]

        zero_kv = jnp.zeros((tk, LANES), F32)
        zero_f = jnp.zeros((1, tk), F32)
        dk, dv, df0, df1 = lax.fori_loop(ki, nq, step, (zero_kv, zero_kv, zero_f, zero_f))
        dk_ref[...] = dk
        dv_ref[...] = dv
        dfr_ref[0, pl.ds(0, 1), :] = df0
        dfr_ref[0, pl.ds(1, 1), :] = df1

    whole = lambda off: pl.BlockSpec((rows, LANES), lambda p, i, off=off: (0, off + p))
    kblk = lambda off: pl.BlockSpec((tk, LANES), lambda p, i, off=off: (i, off + p))
    fr_spec = pl.BlockSpec((1, 2, tk), lambda p, i: (p, 0, i))
    wide = jax.ShapeDtypeStruct((rows, WIDTH), F32)
    return pl.pallas_call(
        body, name="fox_bwd", grid=(PAIRS, nq),
        in_specs=[whole(0), kblk(PAIRS), kblk(v_off), whole(0), fr_spec, whole(0), whole(0), whole(0)],
        out_specs=[whole(0), kblk(0), kblk(0), fr_spec],
        out_shape=[wide, wide, wide, jax.ShapeDtypeStruct((PAIRS, 2, rows), F32)],
        compiler_params=_params(("parallel", "arbitrary")),
    )(fqk, fqk, proj, fx, frow, delta, lse, dao)


def _lane_ids(shape):
    return lax.broadcasted_iota(jnp.int32, shape, len(shape) - 1)


def _gates_elem(a_log, dt_bias, f_bias, pre):
    lane = _lane_ids(pre.shape)
    beta = jax.nn.sigmoid(pre)
    g = -jnp.exp(a_log) * _softplus(pre + dt_bias)
    lf = -_softplus(-(pre + f_bias))
    return jnp.where(lane < 8, beta, jnp.where(lane < 16, g, jnp.where(lane < 24, lf, 0.0)))


def _tri_consts():
    r = np.arange(LANES)[:, None]
    c = np.arange(LANES)[None, :]
    full = (c <= r).astype(np.float32)
    chunked = full * ((r // CHUNK) == (c // CHUNK))
    return jnp.asarray(chunked), jnp.asarray(full)


def _cums_fwd(lc, lf, gates):
    rows = gates.shape[0]
    lane = _lane_ids((LANES, LANES))
    carry = jnp.zeros((1, LANES), F32)
    out = []
    for r in range(rows // LANES):
        blk = gates[r * LANES:(r + 1) * LANES]
        gc = _dot32(lc, blk, _CONTRACT["nn"])
        f = _dot32(lf, blk, _CONTRACT["nn"]) + carry
        carry = carry + jnp.sum(blk, axis=0, keepdims=True)
        out.append(jnp.where((lane >= 8) & (lane < 16), gc, jnp.where((lane >= 16) & (lane < 24), f, 0.0)))
    return jnp.concatenate(out, axis=0)


def _cums_bwd(lc, lf, dcums):
    rows = dcums.shape[0]
    lane = _lane_ids((LANES, LANES))
    is_g = (lane >= 8) & (lane < 16)
    is_f = (lane >= 16) & (lane < 24)
    carry = jnp.zeros((1, LANES), F32)
    out = [None] * (rows // LANES)
    for r in reversed(range(rows // LANES)):
        blk = dcums[r * LANES:(r + 1) * LANES]
        dg = jnp.where(is_g, blk, 0.0)
        df = jnp.where(is_f, blk, 0.0)
        out[r] = _dot32(lc, dg, _CONTRACT["tn"]) + _dot32(lf, df, _CONTRACT["tn"]) + carry
        carry = carry + jnp.sum(df, axis=0, keepdims=True)
    return jnp.concatenate(out, axis=0)


def _expand_consts():
    xb = np.zeros((LANES, WIDTH), np.float32)
    xg = np.zeros((LANES, WIDTH), np.float32)
    xf = np.zeros((LANES, WIDTH), np.float32)
    for h in range(HEADS):
        xb[h, h * HEAD_DIM:(h + 1) * HEAD_DIM] = 1.0
        xg[8 + h, h * HEAD_DIM:(h + 1) * HEAD_DIM] = 1.0
        xf[16 + h, h * HEAD_DIM:(h + 1) * HEAD_DIM] = 1.0
    return jnp.asarray(xb), jnp.asarray(xg), jnp.asarray(xf)


def _shift_down(x, s):
    if s == 0:
        return x
    row = lax.broadcasted_iota(jnp.int32, x.shape, 0)
    return jnp.where(row >= s, pltpu.roll(x, s, 0), 0.0)


def _shift_up(x, s):
    if s == 0:
        return x
    n = x.shape[0]
    row = lax.broadcasted_iota(jnp.int32, x.shape, 0)
    return jnp.where(row < n - s, pltpu.roll(x, n - s, 0), 0.0)


def _row_of(cw, i):
    row = lax.broadcasted_iota(jnp.int32, cw.shape, 0)
    return jnp.sum(jnp.where(row == i, cw, 0.0), axis=0, keepdims=True)


def _conv(cw, x):
    c = jnp.zeros_like(x)
    for i in range(CONV_K):
        c = c + _row_of(cw, i) * _shift_down(x, CONV_K - 1 - i)
    return c


def _post_conv(is_qk, c):
    s = _silu(c)
    n = s * lax.rsqrt(_pair_sum(s * s) + EPS)
    return jnp.where(is_qk, n, s)


def _gdn_prep_fwd(col, cw, x):
    return (_post_conv(col < 2 * PAIRS, _conv(cw, x)),)


def _gdn_prep_bwd(col, cw, x, dy):
    c = _conv(cw, x)
    _, vjp = jax.vjp(lambda cc: _post_conv(col < 2 * PAIRS, cc), c)
    (dc,) = vjp(dy)
    dx = jnp.zeros_like(x)
    row = lax.broadcasted_iota(jnp.int32, cw.shape, 0)
    dcw = jnp.zeros(cw.shape, F32)
    for i in range(CONV_K):
        s = CONV_K - 1 - i
        dx = dx + _row_of(cw, i) * _shift_up(dc, s)
        dcw = dcw + jnp.where(row == i, jnp.sum(dc * _shift_down(x, s), axis=0, keepdims=True), 0.0)
    return dx, dcw


def _head_rms(w, x):
    return x * lax.rsqrt(_pair_sum(x * x) / HEAD_DIM + EPS) * w


def _cat_weights(w_in):
    pad = jnp.zeros((w_in.shape[0], D_CAT - D_IN), w_in.dtype)
    return jnp.concatenate([w_in[:, :2048], w_in[:, 2064:4112], w_in[:, 2048:2064], w_in[:, 4112:4120], pad], axis=1)


def _uncat_grad(g):
    return jnp.concatenate([g[:, :2048], g[:, 4096:4112], g[:, 2048:4096], g[:, 4112:4120]], axis=1)


def _lanes_to_rowform(v8, rows):
    return v8.reshape(rows // CHUNK, CHUNK, HEADS).transpose(0, 2, 1).reshape(rows // CHUNK, PAIRS, 1, LANES)


def _rowform_to_lanes(v, rows):
    return v.reshape(rows // CHUNK, HEADS, CHUNK).transpose(0, 2, 1).reshape(rows, HEADS)


def _local_step(x, target, norm1_w, w_cat, conv_w, a_log, dt_bias, out_norm_w, f_bias, q_norm_w, k_norm_w,
                w_out, norm2_w, w_gate, w_up, w_down, final_w):
    rows = x.shape[0]
    tm = min(256, rows)
    lc, lf = _tri_consts()
    xb, xg, xf = _expand_consts()

    (h1,) = _tiles(lambda col, w, xx: (_rms(xx, w),), name="norm1", rows=rows, tm=tm,
                   full_consts=[norm1_w], row_ins=[(x, D_MODEL, 0)], row_outs=[(D_MODEL, BF16)])
    proj = _mm(h1, w_cat, dims="nn", name="in_proj", tn=384, tk=1024)

    lane_pad = lambda v, off: jnp.pad(v.reshape(1, -1), ((0, 0), (off, LANES - off - v.size)))
    p_a, p_dt, p_fb = lane_pad(a_log, 8), lane_pad(dt_bias, 8), lane_pad(f_bias, 16)

    def gates_fwd(col, lcv, lfv, a, dt, fb, pre):
        gates = _gates_elem(a, dt, fb, pre)
        return gates, _cums_fwd(lcv, lfv, gates)

    gates, cums = _tiles(gates_fwd, name="gates", rows=rows, tm=rows,
                         full_consts=[lc, lf, p_a, p_dt, p_fb], row_ins=[(proj, LANES, COL_SMALL)],
                         row_outs=[(LANES, F32), (LANES, F32)])

    def expand_fwd(col, b, g, f, gt, cm):
        return (_dot32(gt, b, _CONTRACT["nn"]), _dot32(cm, g, _CONTRACT["nn"]), _dot32(cm, f, _CONTRACT["nn"]))

    betax, gcx, fx = _tiles(expand_fwd, name="expand", rows=rows, tm=tm, full_consts=[xb, xg, xf],
                            row_ins=[(gates, LANES, 0), (cums, LANES, 0)],
                            row_outs=[(WIDTH, F32)] * 3)
    grow = _lanes_to_rowform(cums[:, 8:16], rows)
    frow = cums[:, 16:24].T.reshape(PAIRS, 2, rows)

    (qkv,) = _tiles(_gdn_prep_fwd, name="gdn_prep", rows=rows, tm=rows, ncol=3 * PAIRS,
                    col_consts=[(conv_w, CONV_K, LANES, 0)], row_ins=[(proj, LANES, 0)],
                    row_outs=[(LANES, F32)])
    o_gdn, ssave, tsave = _gdn_forward(qkv, betax, gcx, grow, rows)

    w_qk = jnp.concatenate([jnp.tile(q_norm_w.reshape(1, -1), (1, HEADS)),
                            jnp.tile(k_norm_w.reshape(1, -1), (1, HEADS))], axis=1)
    fox_off = 2048 // LANES
    (fqk,) = _tiles(lambda col, w, xx: (_head_rms(w, xx),), name="fox_prep", rows=rows, tm=rows, ncol=2 * PAIRS,
                    col_consts=[(w_qk, 1, LANES, 0)], row_ins=[(proj, LANES, fox_off)],
                    row_outs=[(LANES, F32)])
    ao, lse = _attention_forward(fqk, proj, fx, frow, rows)

    w_on = jnp.tile(out_norm_w.reshape(1, -1), (1, 2))
    z_off, fg_off = 1536 // LANES, 3584 // LANES
    mix_g_fn = lambda w, o, z: _head_rms(w, o) * _silu(z)
    mix_f_fn = lambda a, g: a * jax.nn.sigmoid(g)
    (mix_g,) = _tiles(lambda col, w, o, z: (mix_g_fn(w, o, z),), name="mix_gdn", rows=rows, tm=rows, ncol=PAIRS,
                      full_consts=[w_on], row_ins=[(o_gdn, LANES, 0), (proj, LANES, z_off)],
                      row_outs=[(LANES, BF16)])
    (mix_f,) = _tiles(lambda col, a, g: (mix_f_fn(a, g),), name="mix_fox", rows=rows, tm=rows, ncol=PAIRS,
                      row_ins=[(ao, LANES, 0), (proj, LANES, fg_off)], row_outs=[(LANES, BF16)])
    mix = jnp.concatenate([mix_g, mix_f], axis=1)
    x1 = _mm(mix, w_out, dims="nn", name="out_proj", add=x, tk=1024)

    (h2,) = _tiles(lambda col, w, xx: (_rms(xx, w),), name="norm2", rows=rows, tm=tm,
                   full_consts=[norm2_w], row_ins=[(x1, D_MODEL, 0)], row_outs=[(D_MODEL, BF16)])
    gate = _mm(h2, w_gate, dims="nn", name="ffn_gate", tn=256, tk=1024)
    up = _mm(h2, w_up, dims="nn", name="ffn_up", tn=256, tk=1024)
    act_fn = lambda g, u: _silu(g) * u
    (act,) = _tiles(lambda col, g, u: (act_fn(g, u),), name="ffn_act", rows=rows, tm=tm,
                    row_ins=[(gate, D_FF, 0), (up, D_FF, 0)], row_outs=[(D_FF, BF16)])
    x2 = _mm(act, w_down, dims="nn", name="ffn_down", add=x1, tk=D_FF)

    def final_fn(col, w, xx, tgt):
        y, vjp = jax.vjp(_rms, xx, w)
        err = y - tgt
        loss = 0.5 * jnp.sum(err * err) / D_MODEL
        dx, dw = vjp(err / D_MODEL)
        return dx, dx, jnp.full((1, LANES), loss, F32), dw

    dx2, dx2_b, loss, d_final_w = _tiles(final_fn, name="final_loss", rows=rows, tm=tm, full_consts=[final_w],
                                         row_ins=[(x2, D_MODEL, 0), (target, D_MODEL, 0)],
                                         row_outs=[(D_MODEL, F32), (D_MODEL, BF16)],
                                         acc_outs=[(1, LANES), (1, D_MODEL)])

    dact = _mm(dx2_b, w_down, dims="nt", name="d_act", tn=256, tk=1024)
    g_down = _mm(act, dx2_b, dims="tn", name="g_down", tm=1408, tk=rows)

    def act_bwd(col, g, u, d):
        _, vjp = jax.vjp(act_fn, g, u)
        return vjp(d)

    dgate, dup = _tiles(act_bwd, name="ffn_act_bwd", rows=rows, tm=tm,
                        row_ins=[(gate, D_FF, 0), (up, D_FF, 0), (dact, D_FF, 0)],
                        row_outs=[(D_FF, BF16), (D_FF, BF16)])
    dh2 = _mm(dgate, w_gate, dims="nt", name="d_h2_gate", tk=D_FF)
    dh2 = _mm(dup, w_up, dims="nt", name="d_h2_up", tk=D_FF, add=dh2)
    g_gate = _mm(h2, dgate, dims="tn", name="g_gate", tn=1408, tk=rows)
    g_up = _mm(h2, dup, dims="tn", name="g_up", tn=1408, tk=rows)

    def norm_bwd(col, w, xx, dh, dres):
        _, vjp = jax.vjp(_rms, xx, w)
        dx, dw = vjp(dh)
        return dx + dres, dx + dres, dw

    dx1, dx1_b, d_norm2_w = _tiles(norm_bwd, name="norm2_bwd", rows=rows, tm=tm, full_consts=[norm2_w],
                                   row_ins=[(x1, D_MODEL, 0), (dh2, D_MODEL, 0), (dx2, D_MODEL, 0)],
                                   row_outs=[(D_MODEL, F32), (D_MODEL, BF16)], acc_outs=[(1, D_MODEL)])
    dmix = _mm(dx1_b, w_out, dims="nt", name="d_mix", tk=1024)
    g_out = _mm(mix, dx1_b, dims="tn", name="g_out", tk=rows)

    def mix_g_bwd(col, w, o, z, d):
        _, vjp = jax.vjp(mix_g_fn, w, o, z)
        dw, do_, dz = vjp(d)
        return do_, dz, dw

    do_gdn, dz, d_on = _tiles(mix_g_bwd, name="mix_gdn_bwd", rows=rows, tm=rows, ncol=PAIRS, full_consts=[w_on],
                              row_ins=[(o_gdn, LANES, 0), (proj, LANES, z_off), (dmix, LANES, 0)],
                              row_outs=[(LANES, F32), (LANES, F32)], acc_outs=[(1, LANES)])

    def mix_f_bwd(col, a, g, d):
        _, vjp = jax.vjp(mix_f_fn, a, g)
        return vjp(d)

    dao, dfgate = _tiles(mix_f_bwd, name="mix_fox_bwd", rows=rows, tm=rows, ncol=PAIRS,
                         row_ins=[(ao, LANES, 0), (proj, LANES, fg_off), (dmix, LANES, PAIRS)],
                         row_outs=[(LANES, F32), (LANES, F32)])

    delta = _attention_delta(fqk, proj, fx, frow, lse, dao, rows)
    dfq, dfk, dfv, dfrow = _attention_backward(fqk, proj, fx, frow, delta, lse, dao, rows)
    dfqk_n = jnp.concatenate([dfq, dfk], axis=1)

    def fox_prep_bwd(col, w, xx, d):
        _, vjp = jax.vjp(_head_rms, w, xx)
        dw, dx = vjp(d)
        return dx, dw

    dfqk, d_wqk = _tiles(fox_prep_bwd, name="fox_prep_bwd", rows=rows, tm=rows, ncol=2 * PAIRS,
                         col_consts=[(w_qk, 1, LANES, 0)],
                         row_ins=[(proj, LANES, fox_off), (dfqk_n, LANES, 0)],
                         row_outs=[(LANES, F32)], acc_outs=[(1, LANES)])

    dq, dk, dv, dbetax, dgcx, dgrow = _gdn_backward(qkv, betax, gcx, grow, ssave, tsave, do_gdn, rows)
    dqkv_n = jnp.concatenate([dq, dk, dv], axis=1)
    dqkv, d_conv = _tiles(_gdn_prep_bwd, name="gdn_prep_bwd", rows=rows, tm=rows, ncol=3 * PAIRS,
                          col_consts=[(conv_w, CONV_K, LANES, 0)],
                          row_ins=[(proj, LANES, 0), (dqkv_n, LANES, 0)],
                          row_outs=[(LANES, F32)], acc_outs=[(CONV_K, LANES)])

    def expand_bwd(col, b, g, db, dg):
        return (_dot32(db, b, _CONTRACT["nt"]), _dot32(dg, g, _CONTRACT["nt"]))

    dgates_b, dcums_g = _tiles(expand_bwd, name="expand_bwd", rows=rows, tm=tm, full_consts=[xb, xg],
                               row_ins=[(dbetax, WIDTH, 0), (dgcx, WIDTH, 0)],
                               row_outs=[(LANES, F32), (LANES, F32)])
    dcums_row = jnp.concatenate([jnp.zeros((rows, 8), F32), _rowform_to_lanes(dgrow, rows),
                                 dfrow.reshape(HEADS, rows).T, jnp.zeros((rows, LANES - 24), F32)], axis=1)

    def gates_bwd(col, lcv, lfv, a, dt, fb, pre, dgb, dcg, dcr):
        lane = _lane_ids(pre.shape)
        dgates = jnp.where(lane < 8, dgb, _cums_bwd(lcv, lfv, dcg + dcr))
        _, vjp = jax.vjp(_gates_elem, a, dt, fb, pre)
        da, ddt, dfb, dpre = vjp(dgates)
        return dpre, da, ddt, dfb

    dpre, d_a, d_dt, d_fb = _tiles(gates_bwd, name="gates_bwd", rows=rows, tm=rows,
                                   full_consts=[lc, lf, p_a, p_dt, p_fb],
                                   row_ins=[(proj, LANES, COL_SMALL), (dgates_b, LANES, 0), (dcums_g, LANES, 0),
                                            (dcums_row, LANES, 0)],
                                   row_outs=[(LANES, F32)], acc_outs=[(1, LANES)] * 3)

    dproj = jnp.concatenate([dqkv.astype(BF16), dz.astype(BF16), dfqk.astype(BF16), dfv.astype(BF16),
                             dfgate.astype(BF16), dpre.astype(BF16)], axis=1)
    dh1 = _mm(dproj, w_cat, dims="nt", name="d_h1", tk=D_CAT)
    g_cat = _mm(h1, dproj, dims="tn", name="g_in", tn=384, tk=rows)

    def norm1_bwd(col, w, xx, dh, dres):
        _, vjp = jax.vjp(_rms, xx, w)
        dx, dw = vjp(dh)
        return dx + dres, dw

    grad_x, d_norm1_w = _tiles(norm1_bwd, name="norm1_bwd", rows=rows, tm=tm, full_consts=[norm1_w],
                               row_ins=[(x, D_MODEL, 0), (dh1, D_MODEL, 0), (dx1, D_MODEL, 0)],
                               row_outs=[(D_MODEL, F32)], acc_outs=[(1, D_MODEL)])

    fold = lambda v: v.reshape(-1, HEAD_DIM).sum(axis=0)
    small = dict(
        loss=loss[0, 0],
        norm1_w=d_norm1_w, conv_w=d_conv, a_log=d_a[0, 8:16], dt_bias=d_dt[0, 8:16],
        out_norm_w=fold(d_on), f_bias=d_fb[0, 16:24], q_norm_w=fold(d_wqk[:, :WIDTH]),
        k_norm_w=fold(d_wqk[:, WIDTH:]), norm2_w=d_norm2_w, final_w=d_final_w)
    return grad_x, g_cat, g_out, g_gate, g_up, g_down, small


HBM_SPEC = pl.BlockSpec(memory_space=pltpu.HBM)


def _place():
    x, y, c = lax.axis_index("x"), lax.axis_index("y"), lax.axis_index("c")
    chips = [(1 - x, y), (x, 1 - y), (1 - x, 1 - y)]
    return x, y, c, 2 * x + y, (x, y, 1 - c), chips, [2 * cx + cy for cx, cy in chips]


def _remote(src, dst, send_sem, recv_sem, to):
    return pltpu.make_async_remote_copy(src_ref=src, dst_ref=dst, send_sem=send_sem, recv_sem=recv_sem,
                                        device_id=to, device_id_type=MESH)


def _allgather_weights(shards, conv):
    n = len(shards)
    halves = [s.shape[0] // 2 for s in shards]
    per = 6
    own_base = n * per + 3

    def body(*refs):
        ins, conv_in = refs[:n], refs[n]
        outs, conv_out = refs[n + 1:2 * n + 1], refs[2 * n + 1]
        send_sems, recv_sems = refs[2 * n + 2:]
        x, y, c, own, sib, chips, chip_idx = _place()

        def half(i, ref, hc):
            return ref.at[pl.ds(pl.multiple_of(hc * halves[i], 16), halves[i]), :]

        sent = []
        for i, (src, dst) in enumerate(zip(list(ins) + [conv_in], list(outs) + [conv_out])):
            k = own_base + i
            sent.append(_remote(src, dst.at[own], send_sems.at[k], recv_sems.at[k], sib))
        for i in range(n):
            for j, chip in enumerate(chips):
                k = i * per + j
                sent.append(_remote(half(i, ins[i], c), half(i, outs[i].at[own], c),
                                    send_sems.at[k], recv_sems.at[k], (*chip, c)))
        for j, chip in enumerate(chips):
            k = n * per + j
            sent.append(_remote(conv_in, conv_out.at[own], send_sems.at[k], recv_sems.at[k], (*chip, c)))
        for cp in sent:
            cp.start()
        for i in range(n):
            for j in range(len(chips)):
                k = i * per + j
                landed = half(i, outs[i].at[chip_idx[j]], c)
                _remote(landed, landed, send_sems.at[k], recv_sems.at[k], sib).wait_recv()
                fwd = _remote(landed, landed, send_sems.at[k + 3], recv_sems.at[k + 3], sib)
                fwd.start()
                sent.append(fwd)
        for i in range(n):
            for j in range(len(chips)):
                k = i * per + 3 + j
                landed = half(i, outs[i].at[chip_idx[j]], 1 - c)
                _remote(landed, landed, send_sems.at[k], recv_sems.at[k], sib).wait_recv()
        for j in range(len(chips)):
            k = n * per + j
            landed = conv_out.at[chip_idx[j]]
            _remote(landed, landed, send_sems.at[k], recv_sems.at[k], sib).wait_recv()
        for i, dst in enumerate(list(outs) + [conv_out]):
            k = own_base + i
            landed = dst.at[own]
            _remote(landed, landed, send_sems.at[k], recv_sems.at[k], sib).wait_recv()
        for cp in sent:
            cp.wait_send()

    n_sem = own_base + n + 1
    out_shape = [jax.ShapeDtypeStruct((N_CHIPS,) + s.shape, s.dtype) for s in shards]
    out_shape.append(jax.ShapeDtypeStruct((N_CHIPS,) + conv.shape, conv.dtype))
    res = pl.pallas_call(
        body, name="allgather_weights", out_shape=out_shape,
        in_specs=[HBM_SPEC] * (n + 1), out_specs=[HBM_SPEC] * (n + 1),
        scratch_shapes=[pltpu.SemaphoreType.DMA((n_sem,)), pltpu.SemaphoreType.DMA((n_sem,))],
    )(*shards, conv)
    return res[:n], res[n]


def _swap_halves(stacks):
    n = len(stacks)

    def body(*refs):
        ins, outs = refs[:n], refs[n:2 * n]
        send_sems, recv_sems = refs[2 * n:]
        x, y, c, own, sib, chips, chip_idx = _place()
        cps = []
        for i in range(n):
            h = stacks[i].shape[1] // 2
            src = ins[i].at[:, pl.ds(pl.multiple_of((1 - c) * h, 8), h), :]
            cps.append(_remote(src, outs[i], send_sems.at[i], recv_sems.at[i], sib))
        for cp in cps:
            cp.start()
        for cp in cps:
            cp.wait()

    out_shape = [jax.ShapeDtypeStruct((N_CHIPS, s.shape[1] // 2, s.shape[2]), s.dtype) for s in stacks]
    return pl.pallas_call(
        body, name="rs_swap_halves", out_shape=out_shape,
        in_specs=[HBM_SPEC] * n, out_specs=[HBM_SPEC] * n,
        scratch_shapes=[pltpu.SemaphoreType.DMA((n,)), pltpu.SemaphoreType.DMA((n,))],
    )(*stacks)


def _add_half(stack, landed, place, name):
    _, h, cols = landed.shape

    def body(place_ref, a_ref, b_ref, o_ref, own_ref):
        part = (a_ref[...] + b_ref[...]).astype(o_ref.dtype)
        o_ref[...] = part

        @pl.when(pl.program_id(0) == place_ref[1])
        def _():
            own_ref[...] = part[0]

    return pl.pallas_call(
        body, name=name,
        out_shape=[jax.ShapeDtypeStruct(landed.shape, BF16), jax.ShapeDtypeStruct((h, cols), BF16)],
        grid_spec=pltpu.PrefetchScalarGridSpec(
            num_scalar_prefetch=1, grid=(N_CHIPS,),
            in_specs=[pl.BlockSpec((1, h, cols), lambda j, p: (j, p[0], 0)),
                      pl.BlockSpec((1, h, cols), lambda j, p: (j, 0, 0))],
            out_specs=[pl.BlockSpec((1, h, cols), lambda j, p: (j, 0, 0)),
                       pl.BlockSpec((h, cols), lambda j, p: (0, 0))]),
        compiler_params=_params(("arbitrary",)),
    )(place, stack, landed)


def _exchange_partials(parts):
    n = len(parts)

    def body(*refs):
        ins, outs = refs[:n], refs[n:2 * n]
        send_sems, recv_sems = refs[2 * n:]
        x, y, c, own, sib, chips, chip_idx = _place()
        sent = []
        for i in range(n):
            for j, chip in enumerate(chips):
                k = i * 3 + j
                sent.append(_remote(ins[i].at[chip_idx[j]], outs[i].at[j], send_sems.at[k], recv_sems.at[k],
                                    (*chip, c)))
        for cp in sent:
            cp.start()
        for i in range(n):
            for j in range(len(chips)):
                k = i * 3 + j
                landed = outs[i].at[j]
                _remote(landed, landed, send_sems.at[k], recv_sems.at[k], sib).wait_recv()
        for cp in sent:
            cp.wait_send()

    return pl.pallas_call(
        body, name="rs_exchange_partials",
        out_shape=[jax.ShapeDtypeStruct((3,) + p.shape[1:], p.dtype) for p in parts],
        in_specs=[HBM_SPEC] * n, out_specs=[HBM_SPEC] * n,
        scratch_shapes=[pltpu.SemaphoreType.DMA((3 * n,)), pltpu.SemaphoreType.DMA((3 * n,))],
    )(*parts)


def _sum_partials(own_part, landed, name):
    _, h, cols = landed.shape

    def body(own_ref, a_ref, o_ref):
        acc = own_ref[...].astype(F32)
        for s in range(3):
            acc = acc + a_ref[s].astype(F32)
        o_ref[...] = acc

    return pl.pallas_call(
        body, name=name, out_shape=jax.ShapeDtypeStruct((h, cols), F32), grid=(1,),
        in_specs=[pl.BlockSpec((h, cols), lambda i: (0, 0)), pl.BlockSpec(landed.shape, lambda i: (0, 0, 0))],
        out_specs=pl.BlockSpec((h, cols), lambda i: (0, 0)),
        compiler_params=_params(("arbitrary",)),
    )(own_part, landed)


def _share_halves(halves):
    n = len(halves)

    def body(*refs):
        ins, outs = refs[:n], refs[n:2 * n]
        send_sems, recv_sems = refs[2 * n:]
        x, y, c, own, sib, chips, chip_idx = _place()
        cps = [_remote(ins[i], outs[i], send_sems.at[i], recv_sems.at[i], sib) for i in range(n)]
        for cp in cps:
            cp.start()
        for cp in cps:
            cp.wait()

    return pl.pallas_call(
        body, name="rs_share_halves",
        out_shape=[jax.ShapeDtypeStruct(p.shape, p.dtype) for p in halves],
        in_specs=[HBM_SPEC] * n, out_specs=[HBM_SPEC] * n,
        scratch_shapes=[pltpu.SemaphoreType.DMA((n,)), pltpu.SemaphoreType.DMA((n,))],
    )(*halves)


def _allreduce_small(packed):
    rows = packed.shape[0]
    n_dev = 8

    def body(in_ref, out_ref, gath, send_sems, recv_sems):
        x, y, c = lax.axis_index("x"), lax.axis_index("y"), lax.axis_index("c")
        me = 4 * x + 2 * y + c
        gath[me] = in_ref[...]
        cps = []
        for k in range(1, n_dev):
            fx, fy, fc = (k >> 2) & 1, (k >> 1) & 1, k & 1
            to = (x ^ fx, y ^ fy, c ^ fc)
            cps.append(_remote(in_ref, gath.at[me], send_sems.at[k - 1], recv_sems.at[k - 1], to))
        for cp in cps:
            cp.start()
        for k in range(1, n_dev):
            fx, fy, fc = (k >> 2) & 1, (k >> 1) & 1, k & 1
            src = 4 * (x ^ fx) + 2 * (y ^ fy) + (c ^ fc)
            slot = gath.at[src]
            _remote(slot, slot, send_sems.at[k - 1], recv_sems.at[k - 1], (x, y, c)).wait_recv()
        for cp in cps:
            cp.wait_send()
        acc = gath[0]
        for d in range(1, n_dev):
            acc = acc + gath[d]
        out_ref[...] = acc

    vm = pl.BlockSpec(memory_space=pltpu.VMEM)
    return pl.pallas_call(
        body, name="allreduce_small", out_shape=jax.ShapeDtypeStruct(packed.shape, F32),
        in_specs=[vm], out_specs=vm,
        scratch_shapes=[pltpu.VMEM((n_dev, rows, LANES), F32),
                        pltpu.SemaphoreType.DMA((n_dev - 1,)), pltpu.SemaphoreType.DMA((n_dev - 1,))],
    )(packed)


def _adam(col, w, g, m, v):
    m2 = ADAM_B1 * m + (1.0 - ADAM_B1) * g
    v2 = ADAM_B2 * v + (1.0 - ADAM_B2) * (g * g)
    m_hat = m2 / (1.0 - ADAM_B1 ** ADAM_STEP)
    v_hat = v2 / (1.0 - ADAM_B2 ** ADAM_STEP)
    delta = -ADAM_LR * (m_hat / (jnp.sqrt(v_hat) + ADAM_EPS) + ADAM_WD * w)
    return delta, m2, v2


def _adam_call(w, g, m, v, name):
    rows, cols = w.shape
    tm = rows
    for cand in (256, 352, 176, 128, 64, 48, 16, 8):
        if rows % cand == 0:
            tm = cand
            break
    return _tiles(_adam, name=name, rows=rows, tm=tm,
                  row_ins=[(w, cols, 0), (g, cols, 0), (m, cols, 0), (v, cols, 0)],
                  row_outs=[(cols, F32)] * 3)


def _adam_big(w, g_mine, g_other, m, v, place, name):
    rows, cols = w.shape
    h = rows // 2
    tm = next(t for t in (256, 176, 128) if h % t == 0)
    nt = h // tm

    def body(place_ref, w_ref, gm_ref, go_ref, m_ref, v_ref, g_out, d_out, m_out, v_out):
        g = jnp.where(pl.program_id(0) == place_ref[0], gm_ref[...], go_ref[...])
        d, m2, v2 = _adam(None, w_ref[...], g, m_ref[...], v_ref[...])
        g_out[...] = g
        d_out[...] = d
        m_out[...] = m2
        v_out[...] = v2

    full = pl.BlockSpec((tm, cols), lambda hh, i, p: (hh * nt + i, 0))
    half = pl.BlockSpec((tm, cols), lambda hh, i, p: (i, 0))
    return pl.pallas_call(
        body, name=name, out_shape=[jax.ShapeDtypeStruct(w.shape, F32)] * 4,
        grid_spec=pltpu.PrefetchScalarGridSpec(
            num_scalar_prefetch=1, grid=(2, nt),
            in_specs=[full, half, half, full, full], out_specs=[full] * 4),
        compiler_params=_params(("arbitrary", "arbitrary")),
    )(place, w, g_mine, g_other, m, v)


def _pack(arrays):
    flat = []
    for a in arrays:
        a = a.reshape(-1).astype(F32)
        flat.append(jnp.pad(a, (0, (-a.size) % LANES)))
    out = jnp.concatenate(flat)
    out = jnp.pad(out, (0, (-out.size) % (8 * LANES)))
    return out.reshape(-1, LANES)


def _unpack(packed, shapes):
    flat = packed.reshape(-1)
    out, off = [], 0
    for s in shapes:
        size = int(np.prod(s))
        out.append(flat[off:off + size].reshape(s))
        off += size + (-size) % LANES
    return out


def kernel(x, norm1_w, w_in, gdn_conv_w, gdn_A_log, gdn_dt_bias, gdn_out_norm_w, fox_f_bias, fox_q_norm_w, fox_k_norm_w, w_out, norm2_w, w_ffn_gate, w_ffn_up, w_ffn_down, final_norm_w, loss_target, m_norm1_w, m_w_in, m_gdn_conv_w, m_gdn_A_log, m_gdn_dt_bias, m_gdn_out_norm_w, m_fox_f_bias, m_fox_q_norm_w, m_fox_k_norm_w, m_w_out, m_norm2_w, m_w_ffn_gate, m_w_ffn_up, m_w_ffn_down, m_final_norm_w, v_norm1_w, v_w_in, v_gdn_conv_w, v_gdn_A_log, v_gdn_dt_bias, v_gdn_out_norm_w, v_fox_f_bias, v_fox_q_norm_w, v_fox_k_norm_w, v_w_out, v_norm2_w, v_w_ffn_gate, v_w_ffn_up, v_w_ffn_down, v_final_norm_w):
    cx, cy, cc = lax.axis_index("x"), lax.axis_index("y"), lax.axis_index("c")
    own = 2 * cx + cy
    place = jnp.stack([cc, own]).astype(jnp.int32)

    big_w = [w_in[0], w_out[0], w_ffn_gate[0], w_ffn_up[0], w_ffn_down[0]]
    gathered, conv_g = _allgather_weights([w.astype(BF16) for w in big_w], gdn_conv_w[0])
    by_cols = lambda g: g.transpose(1, 0, 2).reshape(g.shape[1], N_CHIPS * g.shape[2])
    by_rows = lambda g: g.reshape(N_CHIPS * g.shape[1], g.shape[2])
    w_cat = _cat_weights(by_cols(gathered[0]))
    conv_full = by_cols(conv_g)

    grad_x, g_cat, g_out, g_gate, g_up, g_down, small = _local_step(
        x[0], loss_target[0], norm1_w, w_cat, conv_full, gdn_A_log[0], gdn_dt_bias[0], gdn_out_norm_w[0],
        fox_f_bias[0], fox_q_norm_w[0], fox_k_norm_w[0], by_rows(gathered[1]), norm2_w,
        by_cols(gathered[2]), by_cols(gathered[3]), by_rows(gathered[4]), final_norm_w.reshape(1, -1))

    col_stack = lambda g: g.reshape(g.shape[0], N_CHIPS, g.shape[1] // N_CHIPS).transpose(1, 0, 2)
    row_stack = lambda g: g.reshape(N_CHIPS, g.shape[0] // N_CHIPS, g.shape[1])
    stacks = [col_stack(_uncat_grad(g_cat)), row_stack(g_out), col_stack(g_gate), col_stack(g_up),
              row_stack(g_down)]
    landed = _swap_halves(stacks)
    names = ["w_in", "w_out", "w_gate", "w_up", "w_down"]
    added = [_add_half(s, l, place, "rs_add_" + nm) for s, l, nm in zip(stacks, landed, names)]
    from_chips = _exchange_partials([a[0] for a in added])
    halves = [_sum_partials(a[1], p, "rs_sum_" + nm) for a, p, nm in zip(added, from_chips, names)]
    others = _share_halves(halves)
    big_m = [m_w_in[0], m_w_out[0], m_w_ffn_gate[0], m_w_ffn_up[0], m_w_ffn_down[0]]
    big_v = [v_w_in[0], v_w_out[0], v_w_ffn_gate[0], v_w_ffn_up[0], v_w_ffn_down[0]]
    big_upd = [_adam_big(w, gm, go, m, v, place, "adam_" + nm)
               for w, gm, go, m, v, nm in zip(big_w, halves, others, big_m, big_v, names)]

    order = ["norm1_w", "conv_w", "a_log", "dt_bias", "out_norm_w", "f_bias", "q_norm_w", "k_norm_w",
             "norm2_w", "final_w"]
    red = _allreduce_small(_pack([small[k] for k in order] + [small["loss"]]))
    red_shapes = [(1, D_MODEL), (CONV_K, 3 * WIDTH), (1, HEADS), (1, HEADS), (1, HEAD_DIM), (1, HEADS),
                  (1, HEAD_DIM), (1, HEAD_DIM), (1, D_MODEL), (D_MODEL,), ()]
    red_list = _unpack(red, red_shapes)
    loss = red_list[-1]
    small_g = dict(zip(order, red_list[:-1]))
    shard_cols = 3 * WIDTH // N_CHIPS
    small_g["conv_w"] = lax.dynamic_slice_in_dim(small_g["conv_w"], own * shard_cols, shard_cols, axis=1)[None]
    small_w = [norm1_w, gdn_conv_w, gdn_A_log, gdn_dt_bias, gdn_out_norm_w, fox_f_bias, fox_q_norm_w,
               fox_k_norm_w, norm2_w, final_norm_w]
    small_m = [m_norm1_w, m_gdn_conv_w, m_gdn_A_log, m_gdn_dt_bias, m_gdn_out_norm_w, m_fox_f_bias,
               m_fox_q_norm_w, m_fox_k_norm_w, m_norm2_w, m_final_norm_w]
    small_v = [v_norm1_w, v_gdn_conv_w, v_gdn_A_log, v_gdn_dt_bias, v_gdn_out_norm_w, v_fox_f_bias,
               v_fox_q_norm_w, v_fox_k_norm_w, v_norm2_w, v_final_norm_w]
    small_gl = [small_g[k].reshape(w.shape) for k, w in zip(order, small_w)]
    s_delta, s_m, s_v = _adam_call(_pack(small_w), _pack(small_gl), _pack(small_m), _pack(small_v), "adam_small")
    shapes = [w.shape for w in small_w]
    s_delta, s_m, s_v = _unpack(s_delta, shapes), _unpack(s_m, shapes), _unpack(s_v, shapes)

    big_pos = {1: 0, 9: 1, 11: 2, 12: 3, 13: 4}
    small_pos = {0: 0, 2: 1, 3: 2, 4: 3, 5: 4, 6: 5, 7: 6, 8: 7, 10: 8, 14: 9}
    grads, deltas, new_m, new_v = [], [], [], []
    for pos in range(15):
        if pos in big_pos:
            b = big_pos[pos]
            g, d, m2, v2 = big_upd[b]
            grads.append(g[None])
            deltas.append(d[None])
            new_m.append(m2[None])
            new_v.append(v2[None])
        else:
            s = small_pos[pos]
            grads.append(small_gl[s])
            deltas.append(s_delta[s])
            new_m.append(s_m[s])
            new_v.append(s_v[s])
    return (loss, grad_x[None], *grads, *deltas, *new_m, *new_v)
```

```python
import jax
import jax.numpy as jnp
import numpy as np
from jax import lax
from jax.experimental import pallas as pl
from jax.experimental.pallas import tpu as pltpu

F32 = jnp.float32
BF16 = jnp.bfloat16

D_MODEL = 1024
HEADS = 8
HEAD_DIM = 64
PAIRS = HEADS // 2
WIDTH = HEADS * HEAD_DIM
CHUNK = 64
CONV_K = 4
D_FF = 2816
EPS = 1e-6
SCALE = HEAD_DIM ** -0.5
LANES = 128
N_CHIPS = 4
D_IN = 4120
D_CAT = 4224
COL_SMALL = 4096 // LANES

ADAM_LR = 0.001
ADAM_B1 = 0.9
ADAM_B2 = 0.999
ADAM_EPS = 1e-08
ADAM_WD = 0.01
ADAM_STEP = 10

VMEM_LIMIT = 56 * 1024 * 1024
MESH = pl.DeviceIdType.MESH
HIGHEST = lax.Precision.HIGHEST


def _params(sem):
    return pltpu.CompilerParams(dimension_semantics=sem, vmem_limit_bytes=VMEM_LIMIT)


_CONTRACT = {"nn": ((1,), (0,)), "nt": ((1,), (1,)), "tn": ((0,), (0,))}


def _mm(a, b, *, dims, name, out_dtype=F32, add=None, tm=1024, tn=512, tk=512):
    if dims == "nn":
        (m, k), (k2, n) = a.shape, b.shape
    elif dims == "nt":
        (m, k), (n, k2) = a.shape, b.shape
    else:
        (k, m), (k2, n) = a.shape, b.shape
    assert k == k2, (a.shape, b.shape, dims)
    tm, tn, tk = min(tm, m), min(tn, n), min(tk, k)
    assert m % tm == 0 and n % tn == 0 and k % tk == 0, (m, n, k, tm, tn, tk)
    nk = k // tk
    a_spec = (pl.BlockSpec((tk, tm), lambda i, j, kk: (kk, i)) if dims == "tn"
              else pl.BlockSpec((tm, tk), lambda i, j, kk: (i, kk)))
    b_spec = (pl.BlockSpec((tn, tk), lambda i, j, kk: (j, kk)) if dims == "nt"
              else pl.BlockSpec((tk, tn), lambda i, j, kk: (kk, j)))
    o_spec = pl.BlockSpec((tm, tn), lambda i, j, kk: (i, j))
    contract = (_CONTRACT[dims], ((), ()))
    has_add = add is not None

    def body(*refs):
        a_ref, b_ref = refs[:2]
        add_ref = refs[2] if has_add else None
        o_ref = refs[3] if has_add else refs[2]
        part = lax.dot_general(a_ref[...].astype(BF16), b_ref[...].astype(BF16), contract,
                               preferred_element_type=F32)

        def finish(r):
            if has_add:
                r = r + add_ref[...].astype(F32)
            o_ref[...] = r.astype(out_dtype)

        if nk == 1:
            finish(part)
            return
        acc = refs[-1]
        kk = pl.program_id(2)

        @pl.when(kk == 0)
        def _():
            acc[...] = part

        @pl.when(kk > 0)
        def _():
            acc[...] += part

        @pl.when(kk == nk - 1)
        def _():
            finish(acc[...])

    ins = [a, b] + ([add] if has_add else [])
    in_specs = [a_spec, b_spec] + ([o_spec] if has_add else [])
    return pl.pallas_call(
        body, name=name, grid=(m // tm, n // tn, nk),
        in_specs=in_specs, out_specs=o_spec,
        out_shape=jax.ShapeDtypeStruct((m, n), out_dtype),
        scratch_shapes=[pltpu.VMEM((tm, tn), F32)] if nk > 1 else [],
        compiler_params=_params(("parallel", "parallel", "arbitrary")),
    )(*ins)


def _tiles(fn, *, name, rows, tm, ncol=1, row_ins=(), col_consts=(), full_consts=(),
           row_outs=(), acc_outs=()):
    nt = rows // tm
    assert rows % tm == 0
    n_full, n_col, n_row = len(full_consts), len(col_consts), len(row_ins)
    n_ro, n_acc = len(row_outs), len(acc_outs)

    def body(*refs):
        ins = refs[:n_full + n_col + n_row]
        outs = refs[n_full + n_col + n_row:]
        i = pl.program_id(1)
        res = fn(pl.program_id(0), *[r[...] for r in ins])
        for r, v in zip(outs[:n_ro], res[:n_ro]):
            r[...] = v.astype(r.dtype)
        if n_acc:
            @pl.when(i == 0)
            def _():
                for r in outs[n_ro:]:
                    r[...] = jnp.zeros_like(r)
            for r, v in zip(outs[n_ro:], res[n_ro:]):
                r[...] += v

    in_specs = [pl.BlockSpec(a.shape, lambda j, i, nd=a.ndim: (0,) * nd) for a in full_consts]
    in_specs += [pl.BlockSpec((nr, w), lambda j, i, o=o: (0, o + j)) for (_, nr, w, o) in col_consts]
    in_specs += [pl.BlockSpec((tm, w), lambda j, i, o=o: (i, o + j)) for (_, w, o) in row_ins]
    out_specs = [pl.BlockSpec((tm, w), lambda j, i: (i, j)) for (w, _) in row_outs]
    out_specs += [pl.BlockSpec((nr, w), lambda j, i: (0, j)) for (nr, w) in acc_outs]
    out_shape = [jax.ShapeDtypeStruct((rows, w * ncol), dt) for (w, dt) in row_outs]
    out_shape += [jax.ShapeDtypeStruct((nr, w * ncol), F32) for (nr, w) in acc_outs]
    args = list(full_consts) + [c[0] for c in col_consts] + [r[0] for r in row_ins]
    out = pl.pallas_call(
        body, name=name, grid=(ncol, nt), in_specs=in_specs, out_specs=out_specs, out_shape=out_shape,
        compiler_params=_params(("parallel", "arbitrary")),
    )(*args)
    return out


def _rms(x, w):
    return x * lax.rsqrt(jnp.mean(x * x, axis=-1, keepdims=True) + EPS) * w


def _lane_lo(shape):
    return lax.broadcasted_iota(jnp.int32, shape, len(shape) - 1) < HEAD_DIM


def _pair_sum(x):
    lo = _lane_lo(x.shape)
    s0 = jnp.sum(jnp.where(lo, x, 0.0), axis=-1, keepdims=True)
    s1 = jnp.sum(jnp.where(lo, 0.0, x), axis=-1, keepdims=True)
    return jnp.where(lo, s0, s1)


def _head_col(x, lo, h):
    keep = lo if h == 0 else jnp.logical_not(lo)
    return jnp.max(jnp.where(keep, x, -jnp.inf), axis=-1, keepdims=True)


def _softplus(x):
    return jnp.maximum(x, 0.0) + jnp.log1p(jnp.exp(-jnp.abs(x)))


def _silu(x):
    return x * jax.nn.sigmoid(x)


def _dot(a, b, contract):
    return lax.dot_general(a.astype(BF16), b.astype(BF16), (contract, ((), ())),
                           preferred_element_type=F32)


def _dot32(a, b, contract):
    return lax.dot_general(a, b, (contract, ((), ())), precision=HIGHEST, preferred_element_type=F32)


def _bd(y):
    yy = jnp.concatenate([y, y], axis=0)
    r = lax.broadcasted_iota(jnp.int32, yy.shape, 0) < HEAD_DIM
    c = lax.broadcasted_iota(jnp.int32, yy.shape, 1) < HEAD_DIM
    return jnp.where(r == c, yy, 0.0)


def _pp(x, y):
    return _dot(x, _bd(y), _CONTRACT["nn"])


def _pp_nt(x, y):
    return _dot(x, _bd(y), _CONTRACT["nt"])


def _pp_tn(x, y):
    full = _dot(x, y, _CONTRACT["tn"])
    return jnp.where(_lane_lo((HEAD_DIM, LANES)), full[:HEAD_DIM], full[HEAD_DIM:])


def _gdn_masks():
    row = lax.broadcasted_iota(jnp.int32, (CHUNK, LANES), 0)
    col = lax.broadcasted_iota(jnp.int32, (CHUNK, LANES), 1) % HEAD_DIM
    return row, col


def _interleave(chains):
    live = list(chains)
    while live:
        for g in list(live):
            try:
                next(g)
            except StopIteration:
                live.remove(g)


def _gdn_forward(qkv, betax, gcx, grow, rows):
    nchunk = rows // CHUNK

    def body(q_ref, k_ref, v_ref, bx_ref, gx_ref, gr_ref, o_ref, ss_ref, ts_ref, state):
        n = pl.program_id(0)

        @pl.when(n == 0)
        def _():
            state[...] = jnp.zeros_like(state)

        row, col = _gdn_masks()
        incl, strict = col <= row, col < row

        def chain(p):
            lanes = pl.ds(p * LANES, LANES)
            q, k, v, bx, gx = q_ref[:, lanes], k_ref[:, lanes], v_ref[:, lanes], bx_ref[:, lanes], gx_ref[:, lanes]
            gr = gr_ref[0, p]
            glast = gx_ref[pl.ds(CHUNK - 1, 1), lanes]
            s = state[p]
            dm = jnp.where(incl, jnp.exp(jnp.minimum(gx - gr, 0.0)), 0.0)
            kb, vb, eg, qs = k * bx, v * bx, jnp.exp(gx), q * SCALE
            yield
            big_g, big_p = _pp_nt(kb, k), _pp_nt(qs, k)
            yield
            x = -jnp.where(strict, big_g * dm, 0.0)
            att = jnp.where(incl, big_p * dm, 0.0)
            tm = jnp.where(row == col, 1.0, 0.0) + x
            x = _pp(x, x)
            yield
            for _ in range(4):
                step, x = _pp(tm, x), _pp(x, x)
                yield
                tm = tm + step
            tm = tm + _pp(tm, x)
            yield
            u, w = _pp(tm, vb), _pp(tm, kb * eg)
            yield
            ws, qgs = _pp(w, s), _pp(qs * eg, s)
            yield
            vn = u - ws
            kd = k * jnp.exp(glast - gx)
            avn, upd = _pp(att, vn), _pp_tn(kd, vn)
            yield
            ss_ref[0, p] = s
            ts_ref[0, p] = tm
            o_ref[:, lanes] = qgs + avn
            state[p] = s * jnp.exp(glast) + upd

        _interleave([chain(p) for p in range(PAIRS)])

    blk = lambda j: pl.BlockSpec((CHUNK, WIDTH), lambda n, j=j: (n, j))
    sv = pl.BlockSpec((1, PAIRS, CHUNK, LANES), lambda n: (n, 0, 0, 0))
    return pl.pallas_call(
        body, name="gdn_fwd", grid=(nchunk,),
        in_specs=[blk(0), blk(1), blk(2), blk(0), blk(0),
                  pl.BlockSpec((1, PAIRS, 1, LANES), lambda n: (n, 0, 0, 0))],
        out_specs=[blk(0), sv, sv],
        out_shape=[jax.ShapeDtypeStruct((rows, WIDTH), F32),
                   jax.ShapeDtypeStruct((nchunk, PAIRS, CHUNK, LANES), F32),
                   jax.ShapeDtypeStruct((nchunk, PAIRS, CHUNK, LANES), F32)],
        scratch_shapes=[pltpu.VMEM((PAIRS, CHUNK, LANES), F32)],
        compiler_params=_params(("arbitrary",)),
    )(qkv, qkv, qkv, betax, gcx, grow)


def _gdn_backward(qkv, betax, gcx, grow, ssave, tsave, do, rows):
    nchunk = rows // CHUNK

    def body(q_ref, k_ref, v_ref, bx_ref, gx_ref, gr_ref, ss_ref, ts_ref, do_ref,
             dq_ref, dk_ref, dv_ref, dbx_ref, dgx_ref, dgr_ref, dstate):
        n = pl.program_id(0)

        @pl.when(n == 0)
        def _():
            dstate[...] = jnp.zeros_like(dstate)

        row, col = _gdn_masks()
        incl, strict = col <= row, col < row

        def chain(p):
            lanes = pl.ds(p * LANES, LANES)
            q, k, v, bx, gx = q_ref[:, lanes], k_ref[:, lanes], v_ref[:, lanes], bx_ref[:, lanes], gx_ref[:, lanes]
            gr = gr_ref[0, p]
            glast = gx_ref[pl.ds(CHUNK - 1, 1), lanes]
            s, tm, d_o = ss_ref[0, p], ts_ref[0, p], do_ref[:, lanes]
            ds_out = dstate[p]
            dm = jnp.where(incl, jnp.exp(jnp.minimum(gx - gr, 0.0)), 0.0)
            kb, vb, eg, qs = k * bx, v * bx, jnp.exp(gx), q * SCALE
            kbg, qg = kb * eg, qs * eg
            ed = jnp.exp(glast - gx)
            kd = k * ed
            eglast = jnp.exp(glast)
            yield
            big_g, big_p = _pp_nt(kb, k), _pp_nt(qs, k)
            u, w = _pp(tm, vb), _pp(tm, kbg)
            dqg, kds = _pp_nt(d_o, s), _pp(kd, ds_out)
            yield
            low = jnp.where(strict, big_g * dm, 0.0)
            att = jnp.where(incl, big_p * dm, 0.0)
            ws, atd = _pp(w, s), _pp_tn(att, d_o)
            yield
            vn = u - ws
            dvn = kds + atd
            dkd, datt_raw = _pp_nt(vn, ds_out), _pp_nt(d_o, vn)
            dw_neg, dvb = _pp_nt(dvn, s), _pp_tn(tm, dvn)
            dtm_a, wdv = _pp_nt(dvn, vb), _pp_tn(w, dvn)
            qgd = _pp_tn(qg, d_o)
            yield
            datt = jnp.where(incl, datt_raw, 0.0)
            dw = -dw_neg
            dtm_b, dkbg = _pp_nt(dw, kbg), _pp_tn(tm, dw)
            dbig_p = datt * dm
            dqs_a, dk_p = _pp(dbig_p, k), _pp_tn(dbig_p, qs)
            yield
            inner = _pp_tn(tm, dtm_a + dtm_b)
            yield
            dlow = jnp.where(strict, -_pp_nt(inner, tm), 0.0)
            yield
            dbig_g = dlow * dm
            dkb_a, dk_g = _pp(dbig_g, k), _pp_tn(dbig_g, kb)
            yield
            dkb = dkb_a + dkbg * eg
            dqs = dqs_a + dqg * eg
            dk = dk_g + dk_p + dkd * ed + dkb * bx
            z = dlow * low + datt * att
            kdterm = dkd * kd
            dglast = (jnp.sum(ds_out * s, axis=0, keepdims=True) * eglast
                      + jnp.sum(kdterm, axis=0, keepdims=True))
            dgx = dqg * qg + dkbg * kbg - kdterm
            dgx = dgx + jnp.where(col == 0, _pair_sum(z), 0.0)
            dgx = dgx + jnp.where(row == CHUNK - 1, dglast, 0.0)
            dq_ref[:, lanes] = dqs * SCALE
            dk_ref[:, lanes] = dk
            dv_ref[:, lanes] = dvb * bx
            dbx_ref[:, lanes] = dkb * k + dvb * v
            dgx_ref[:, lanes] = dgx
            dgr_ref[0, p] = -jnp.sum(z, axis=0, keepdims=True)
            dstate[p] = ds_out * eglast + qgd - wdv

        _interleave([chain(p) for p in range(PAIRS)])

    last = nchunk - 1
    blk = lambda j: pl.BlockSpec((CHUNK, WIDTH), lambda n, j=j: (last - n, j))
    sv = pl.BlockSpec((1, PAIRS, CHUNK, LANES), lambda n: (last - n, 0, 0, 0))
    gr_spec = pl.BlockSpec((1, PAIRS, 1, LANES), lambda n: (last - n, 0, 0, 0))
    wide = jax.ShapeDtypeStruct((rows, WIDTH), F32)
    return pl.pallas_call(
        body, name="gdn_bwd", grid=(nchunk,),
        in_specs=[blk(0), blk(1), blk(2), blk(0), blk(0), gr_spec, sv, sv, blk(0)],
        out_specs=[blk(0)] * 5 + [gr_spec],
        out_shape=[wide] * 5 + [jax.ShapeDtypeStruct((nchunk, PAIRS, 1, LANES), F32)],
        scratch_shapes=[pltpu.VMEM((PAIRS, CHUNK, LANES), F32)],
        compiler_params=_params(("arbitrary",)),
    )(qkv, qkv, qkv, betax, gcx, grow, ssave, tsave, do)


ATT_TQ = 256


def _att_scores(qh, kt, fk, diag):
    s = _dot(qh, kt, _CONTRACT["nt"]) - fk
    if diag:
        r = lax.broadcasted_iota(jnp.int32, s.shape, 0)
        c = lax.broadcasted_iota(jnp.int32, s.shape, 1)
        s = jnp.where(r >= c, s, -jnp.inf)
    return s


def _head_masks(n):
    lo = _lane_lo((n, LANES))
    return [lo, jnp.logical_not(lo)]


def _attention_forward(fqk, proj, frow, rows):
    tq = tk = min(ATT_TQ, rows)
    nq = rows // tq
    v_off = 3072 // LANES

    def body(q_ref, k_ref, v_ref, fr_ref, o_ref, lse_ref):
        qi = pl.program_id(1)
        q = q_ref[...] * SCALE
        keep_q, keep_k = _head_masks(tq), _head_masks(tk)
        qh = [jnp.where(keep_q[h], q, 0.0).astype(BF16) for h in range(2)]

        def tile(ki, carry, diag):
            k0 = pl.multiple_of(ki * tk, tk)
            kt = k_ref[pl.ds(k0, tk), :].astype(BF16)
            v_t = v_ref[pl.ds(k0, tk), :]
            out = []
            for h in range(2):
                m, l, acc = carry[h]
                vt = jnp.where(keep_k[h], v_t, 0.0).astype(BF16)
                s = _att_scores(qh[h], kt, fr_ref[0, pl.ds(h, 1), pl.ds(k0, tk)], diag)
                m_new = jnp.maximum(m, jnp.max(s, axis=-1, keepdims=True))
                p = jnp.exp(s - m_new)
                alpha = jnp.exp(m - m_new)
                l = alpha * l + jnp.sum(p, axis=-1, keepdims=True)
                acc = alpha * acc + _dot(p, vt, _CONTRACT["nn"])
                out.append((m_new, l, acc))
            return tuple(out)

        one = (jnp.full((tq, 1), -jnp.inf, F32), jnp.zeros((tq, 1), F32), jnp.zeros((tq, LANES), F32))
        carry = lax.fori_loop(0, qi, lambda ki, c: tile(ki, c, False), (one, one))
        (m0, l0, acc0), (m1, l1, acc1) = tile(qi, carry, True)
        o_ref[...] = acc0 / l0 + acc1 / l1
        lse_ref[...] = jnp.where(keep_q[0], m0 + jnp.log(l0), m1 + jnp.log(l1))

    whole = lambda off: pl.BlockSpec((rows, LANES), lambda p, i, off=off: (0, off + p))
    qblk = lambda off: pl.BlockSpec((tq, LANES), lambda p, i, off=off: (i, off + p))
    wide = jax.ShapeDtypeStruct((rows, WIDTH), F32)
    return pl.pallas_call(
        body, name="fox_fwd", grid=(PAIRS, nq),
        in_specs=[qblk(0), whole(PAIRS), whole(v_off), pl.BlockSpec((1, 2, rows), lambda p, i: (p, 0, 0))],
        out_specs=[qblk(0), qblk(0)], out_shape=[wide, wide],
        compiler_params=_params(("parallel", "arbitrary")),
    )(fqk, fqk, proj, frow)


def _attention_delta(fqk, proj, frow, lse, dao, rows):
    tq = tk = min(ATT_TQ, rows)
    nq = rows // tq
    v_off = 3072 // LANES

    def body(q_ref, k_ref, v_ref, fr_ref, lse_ref, do_ref, delta_ref):
        qi = pl.program_id(1)
        q, d_o, lse_t = q_ref[...] * SCALE, do_ref[...], lse_ref[...]
        keep_q = _head_masks(tq)
        qh = [jnp.where(keep_q[h], q, 0.0).astype(BF16) for h in range(2)]
        doh = [jnp.where(keep_q[h], d_o, 0.0).astype(BF16) for h in range(2)]
        lse_h = [_head_col(lse_t, keep_q[0], h) for h in range(2)]

        def tile(ki, carry, diag):
            k0 = pl.multiple_of(ki * tk, tk)
            kt = k_ref[pl.ds(k0, tk), :].astype(BF16)
            vt = v_ref[pl.ds(k0, tk), :].astype(BF16)
            out = []
            for h in range(2):
                s = _att_scores(qh[h], kt, fr_ref[0, pl.ds(h, 1), pl.ds(k0, tk)], diag)
                dp = _dot(doh[h], vt, _CONTRACT["nt"])
                out.append(carry[h] + jnp.sum(jnp.exp(s - lse_h[h]) * dp, axis=-1, keepdims=True))
            return tuple(out)

        zero = jnp.zeros((tq, 1), F32)
        carry = lax.fori_loop(0, qi, lambda ki, c: tile(ki, c, False), (zero, zero))
        d0, d1 = tile(qi, carry, True)
        delta_ref[...] = jnp.where(keep_q[0], d0, d1)

    whole = lambda off: pl.BlockSpec((rows, LANES), lambda p, i, off=off: (0, off + p))
    qblk = lambda off: pl.BlockSpec((tq, LANES), lambda p, i, off=off: (i, off + p))
    return pl.pallas_call(
        body, name="fox_delta", grid=(PAIRS, nq),
        in_specs=[qblk(0), whole(PAIRS), whole(v_off),
                  pl.BlockSpec((1, 2, rows), lambda p, i: (p, 0, 0)), qblk(0), qblk(0)],
        out_specs=qblk(0), out_shape=jax.ShapeDtypeStruct((rows, WIDTH), F32),
        compiler_params=_params(("parallel", "arbitrary")),
    )(fqk, fqk, proj, frow, lse, dao)


def _attention_backward(fqk, proj, frow, delta, lse, dao, rows):
    tq = tk = min(ATT_TQ, rows)
    nq = rows // tq
    v_off = 3072 // LANES

    def body(q_ref, k_ref, v_ref, fr_ref, delta_ref, lse_ref, do_ref, dq_ref, dk_ref, dv_ref, dfr_ref):
        ki = pl.program_id(1)

        @pl.when(ki == 0)
        def _():
            dq_ref[...] = jnp.zeros_like(dq_ref)

        keep_q, keep_k = _head_masks(tq), _head_masks(tk)
        k_t = k_ref[...]
        kt = k_t.astype(BF16)
        vt = v_ref[...].astype(BF16)
        kh = [jnp.where(keep_k[h], k_t, 0.0).astype(BF16) for h in range(2)]
        fk = [fr_ref[0, pl.ds(h, 1), :] for h in range(2)]

        def tile(qi, carry, diag):
            dk, dv, df0, df1 = carry
            rows_q = pl.ds(pl.multiple_of(qi * tq, tq), tq)
            q, d_o, delta_x, lse_t = q_ref[rows_q, :] * SCALE, do_ref[rows_q, :], delta_ref[rows_q, :], lse_ref[rows_q, :]
            dq = jnp.zeros((tq, LANES), F32)
            dfs = []
            for h in range(2):
                qh = jnp.where(keep_q[h], q, 0.0).astype(BF16)
                doh = jnp.where(keep_q[h], d_o, 0.0).astype(BF16)
                s = _att_scores(qh, kt, fk[h], diag)
                p = jnp.exp(s - _head_col(lse_t, keep_q[0], h))
                dp = _dot(doh, vt, _CONTRACT["nt"])
                ds = p * (dp - _head_col(delta_x, keep_q[0], h))
                dv = dv + _dot(p, doh, _CONTRACT["tn"])
                dk = dk + _dot(ds, qh, _CONTRACT["tn"])
                dq = dq + _dot(ds, kh[h], _CONTRACT["nn"])
                dfs.append(-jnp.sum(ds, axis=0, keepdims=True))
            dq_ref[rows_q, :] += dq * SCALE
            return dk, dv, df0 + dfs[0], df1 + dfs[1]

        zero_kv = jnp.zeros((tk, LANES), F32)
        zero_f = jnp.zeros((1, tk), F32)
        carry = tile(ki, (zero_kv, zero_kv, zero_f, zero_f), True)
        dk, dv, df0, df1 = lax.fori_loop(ki + 1, nq, lambda qi, c: tile(qi, c, False), carry)
        dk_ref[...] = dk
        dv_ref[...] = dv
        dfr_ref[0, pl.ds(0, 1), :] = df0
        dfr_ref[0, pl.ds(1, 1), :] = df1

    whole = lambda off: pl.BlockSpec((rows, LANES), lambda p, i, off=off: (0, off + p))
    kblk = lambda off: pl.BlockSpec((tk, LANES), lambda p, i, off=off: (i, off + p))
    fr_spec = pl.BlockSpec((1, 2, tk), lambda p, i: (p, 0, i))
    wide = jax.ShapeDtypeStruct((rows, WIDTH), F32)
    return pl.pallas_call(
        body, name="fox_bwd", grid=(PAIRS, nq),
        in_specs=[whole(0), kblk(PAIRS), kblk(v_off), fr_spec, whole(0), whole(0), whole(0)],
        out_specs=[whole(0), kblk(0), kblk(0), fr_spec],
        out_shape=[wide, wide, wide, jax.ShapeDtypeStruct((PAIRS, 2, rows), F32)],
        compiler_params=_params(("parallel", "arbitrary")),
    )(fqk, fqk, proj, frow, delta, lse, dao)


def _lane_ids(shape):
    return lax.broadcasted_iota(jnp.int32, shape, len(shape) - 1)


def _gates_elem(a_log, dt_bias, f_bias, pre):
    lane = _lane_ids(pre.shape)
    beta = jax.nn.sigmoid(pre)
    g = -jnp.exp(a_log) * _softplus(pre + dt_bias)
    lf = -_softplus(-(pre + f_bias))
    return jnp.where(lane < 8, beta, jnp.where(lane < 16, g, jnp.where(lane < 24, lf, 0.0)))


def _tri_consts():
    r = np.arange(LANES)[:, None]
    c = np.arange(LANES)[None, :]
    full = (c <= r).astype(np.float32)
    chunked = full * ((r // CHUNK) == (c // CHUNK))
    return jnp.asarray(chunked), jnp.asarray(full)


def _cums_fwd(lc, lf, gates):
    rows = gates.shape[0]
    lane = _lane_ids((LANES, LANES))
    carry = jnp.zeros((1, LANES), F32)
    out = []
    for r in range(rows // LANES):
        blk = gates[r * LANES:(r + 1) * LANES]
        gc = _dot32(lc, blk, _CONTRACT["nn"])
        f = _dot32(lf, blk, _CONTRACT["nn"]) + carry
        carry = carry + jnp.sum(blk, axis=0, keepdims=True)
        out.append(jnp.where((lane >= 8) & (lane < 16), gc, jnp.where((lane >= 16) & (lane < 24), f, 0.0)))
    return jnp.concatenate(out, axis=0)


def _cums_bwd(lc, lf, dcums):
    rows = dcums.shape[0]
    lane = _lane_ids((LANES, LANES))
    is_g = (lane >= 8) & (lane < 16)
    is_f = (lane >= 16) & (lane < 24)
    carry = jnp.zeros((1, LANES), F32)
    out = [None] * (rows // LANES)
    for r in reversed(range(rows // LANES)):
        blk = dcums[r * LANES:(r + 1) * LANES]
        dg = jnp.where(is_g, blk, 0.0)
        df = jnp.where(is_f, blk, 0.0)
        out[r] = _dot32(lc, dg, _CONTRACT["tn"]) + _dot32(lf, df, _CONTRACT["tn"]) + carry
        carry = carry + jnp.sum(df, axis=0, keepdims=True)
    return jnp.concatenate(out, axis=0)


def _expand_consts():
    xb = np.zeros((LANES, WIDTH), np.float32)
    xg = np.zeros((LANES, WIDTH), np.float32)
    for h in range(HEADS):
        xb[h, h * HEAD_DIM:(h + 1) * HEAD_DIM] = 1.0
        xg[8 + h, h * HEAD_DIM:(h + 1) * HEAD_DIM] = 1.0
    return jnp.asarray(xb), jnp.asarray(xg)


def _shift_down(x, s):
    if s == 0:
        return x
    row = lax.broadcasted_iota(jnp.int32, x.shape, 0)
    return jnp.where(row >= s, pltpu.roll(x, s, 0), 0.0)


def _shift_up(x, s):
    if s == 0:
        return x
    n = x.shape[0]
    row = lax.broadcasted_iota(jnp.int32, x.shape, 0)
    return jnp.where(row < n - s, pltpu.roll(x, n - s, 0), 0.0)


def _row_of(cw, i):
    row = lax.broadcasted_iota(jnp.int32, cw.shape, 0)
    return jnp.sum(jnp.where(row == i, cw, 0.0), axis=0, keepdims=True)


def _conv(cw, x):
    c = jnp.zeros_like(x)
    for i in range(CONV_K):
        c = c + _row_of(cw, i) * _shift_down(x, CONV_K - 1 - i)
    return c


def _post_conv(is_qk, c):
    s = _silu(c)
    n = s * lax.rsqrt(_pair_sum(s * s) + EPS)
    return jnp.where(is_qk, n, s)


def _gdn_prep_fwd(col, cw, x):
    return (_post_conv(col < 2 * PAIRS, _conv(cw, x)),)


def _gdn_prep_bwd(col, cw, x, dy):
    c = _conv(cw, x)
    _, vjp = jax.vjp(lambda cc: _post_conv(col < 2 * PAIRS, cc), c)
    (dc,) = vjp(dy)
    dx = jnp.zeros_like(x)
    row = lax.broadcasted_iota(jnp.int32, cw.shape, 0)
    dcw = jnp.zeros(cw.shape, F32)
    for i in range(CONV_K):
        s = CONV_K - 1 - i
        dx = dx + _row_of(cw, i) * _shift_up(dc, s)
        dcw = dcw + jnp.where(row == i, jnp.sum(dc * _shift_down(x, s), axis=0, keepdims=True), 0.0)
    return dx, dcw


def _head_rms(w, x):
    return x * lax.rsqrt(_pair_sum(x * x) / HEAD_DIM + EPS) * w


def _cat_weights(w_in):
    pad = jnp.zeros((w_in.shape[0], D_CAT - D_IN), w_in.dtype)
    return jnp.concatenate([w_in[:, :2048], w_in[:, 2064:4112], w_in[:, 2048:2064], w_in[:, 4112:4120], pad], axis=1)


def _uncat_grad(g):
    return jnp.concatenate([g[:, :2048], g[:, 4096:4112], g[:, 2048:4096], g[:, 4112:4120]], axis=1)


def _lanes_to_rowform(v8, rows):
    return v8.reshape(rows // CHUNK, CHUNK, HEADS).transpose(0, 2, 1).reshape(rows // CHUNK, PAIRS, 1, LANES)


def _rowform_to_lanes(v, rows):
    return v.reshape(rows // CHUNK, HEADS, CHUNK).transpose(0, 2, 1).reshape(rows, HEADS)


def _local_step(x, target, norm1_w, w_cat, conv_w, a_log, dt_bias, out_norm_w, f_bias, q_norm_w, k_norm_w,
                w_out, norm2_w, w_gate, w_up, w_down, final_w):
    rows = x.shape[0]
    tm = min(256, rows)
    lc, lf = _tri_consts()
    xb, xg = _expand_consts()

    (h1,) = _tiles(lambda col, w, xx: (_rms(xx, w),), name="norm1", rows=rows, tm=tm,
                   full_consts=[norm1_w], row_ins=[(x, D_MODEL, 0)], row_outs=[(D_MODEL, BF16)])
    proj = _mm(h1, w_cat, dims="nn", name="in_proj", tn=384, tk=1024)

    lane_pad = lambda v, off: jnp.pad(v.reshape(1, -1), ((0, 0), (off, LANES - off - v.size)))
    p_a, p_dt, p_fb = lane_pad(a_log, 8), lane_pad(dt_bias, 8), lane_pad(f_bias, 16)

    def gates_fwd(col, lcv, lfv, a, dt, fb, pre):
        gates = _gates_elem(a, dt, fb, pre)
        return gates, _cums_fwd(lcv, lfv, gates)

    gates, cums = _tiles(gates_fwd, name="gates", rows=rows, tm=rows,
                         full_consts=[lc, lf, p_a, p_dt, p_fb], row_ins=[(proj, LANES, COL_SMALL)],
                         row_outs=[(LANES, F32), (LANES, F32)])

    def expand_fwd(col, b, g, gt, cm):
        return (_dot32(gt, b, _CONTRACT["nn"]), _dot32(cm, g, _CONTRACT["nn"]))

    betax, gcx = _tiles(expand_fwd, name="expand", rows=rows, tm=tm, full_consts=[xb, xg],
                        row_ins=[(gates, LANES, 0), (cums, LANES, 0)],
                        row_outs=[(WIDTH, F32)] * 2)
    grow = _lanes_to_rowform(cums[:, 8:16], rows)
    frow = cums[:, 16:24].T.reshape(PAIRS, 2, rows)

    (qkv,) = _tiles(_gdn_prep_fwd, name="gdn_prep", rows=rows, tm=rows, ncol=3 * PAIRS,
                    col_consts=[(conv_w, CONV_K, LANES, 0)], row_ins=[(proj, LANES, 0)],
                    row_outs=[(LANES, F32)])
    o_gdn, ssave, tsave = _gdn_forward(qkv, betax, gcx, grow, rows)

    w_qk = jnp.concatenate([jnp.tile(q_norm_w.reshape(1, -1), (1, HEADS)),
                            jnp.tile(k_norm_w.reshape(1, -1), (1, HEADS))], axis=1)
    fox_off = 2048 // LANES
    (fqk,) = _tiles(lambda col, w, xx: (_head_rms(w, xx),), name="fox_prep", rows=rows, tm=rows, ncol=2 * PAIRS,
                    col_consts=[(w_qk, 1, LANES, 0)], row_ins=[(proj, LANES, fox_off)],
                    row_outs=[(LANES, F32)])
    ao, lse = _attention_forward(fqk, proj, frow, rows)

    w_on = jnp.tile(out_norm_w.reshape(1, -1), (1, 2))
    z_off, fg_off = 1536 // LANES, 3584 // LANES
    mix_g_fn = lambda w, o, z: _head_rms(w, o) * _silu(z)
    mix_f_fn = lambda a, g: a * jax.nn.sigmoid(g)
    (mix_g,) = _tiles(lambda col, w, o, z: (mix_g_fn(w, o, z),), name="mix_gdn", rows=rows, tm=rows, ncol=PAIRS,
                      full_consts=[w_on], row_ins=[(o_gdn, LANES, 0), (proj, LANES, z_off)],
                      row_outs=[(LANES, BF16)])
    (mix_f,) = _tiles(lambda col, a, g: (mix_f_fn(a, g),), name="mix_fox", rows=rows, tm=rows, ncol=PAIRS,
                      row_ins=[(ao, LANES, 0), (proj, LANES, fg_off)], row_outs=[(LANES, BF16)])
    mix = jnp.concatenate([mix_g, mix_f], axis=1)
    x1 = _mm(mix, w_out, dims="nn", name="out_proj", add=x, tk=1024)

    (h2,) = _tiles(lambda col, w, xx: (_rms(xx, w),), name="norm2", rows=rows, tm=tm,
                   full_consts=[norm2_w], row_ins=[(x1, D_MODEL, 0)], row_outs=[(D_MODEL, BF16)])
    gate = _mm(h2, w_gate, dims="nn", name="ffn_gate", tn=256, tk=1024)
    up = _mm(h2, w_up, dims="nn", name="ffn_up", tn=256, tk=1024)
    act_fn = lambda g, u: _silu(g) * u
    (act,) = _tiles(lambda col, g, u: (act_fn(g, u),), name="ffn_act", rows=rows, tm=tm,
                    row_ins=[(gate, D_FF, 0), (up, D_FF, 0)], row_outs=[(D_FF, BF16)])
    x2 = _mm(act, w_down, dims="nn", name="ffn_down", add=x1, tk=D_FF)

    def final_fn(col, w, xx, tgt):
        y, vjp = jax.vjp(_rms, xx, w)
        err = y - tgt
        loss = 0.5 * jnp.sum(err * err) / D_MODEL
        dx, dw = vjp(err / D_MODEL)
        return dx, dx, jnp.full((1, LANES), loss, F32), dw

    dx2, dx2_b, loss, d_final_w = _tiles(final_fn, name="final_loss", rows=rows, tm=tm, full_consts=[final_w],
                                         row_ins=[(x2, D_MODEL, 0), (target, D_MODEL, 0)],
                                         row_outs=[(D_MODEL, F32), (D_MODEL, BF16)],
                                         acc_outs=[(1, LANES), (1, D_MODEL)])

    dact = _mm(dx2_b, w_down, dims="nt", name="d_act", tn=256, tk=1024)
    g_down = _mm(act, dx2_b, dims="tn", name="g_down", tm=1408, tk=rows)

    def act_bwd(col, g, u, d):
        _, vjp = jax.vjp(act_fn, g, u)
        return vjp(d)

    dgate, dup = _tiles(act_bwd, name="ffn_act_bwd", rows=rows, tm=tm,
                        row_ins=[(gate, D_FF, 0), (up, D_FF, 0), (dact, D_FF, 0)],
                        row_outs=[(D_FF, BF16), (D_FF, BF16)])
    dh2 = _mm(dgate, w_gate, dims="nt", name="d_h2_gate", tk=D_FF)
    dh2 = _mm(dup, w_up, dims="nt", name="d_h2_up", tk=D_FF, add=dh2)
    g_gate = _mm(h2, dgate, dims="tn", name="g_gate", tn=1408, tk=rows)
    g_up = _mm(h2, dup, dims="tn", name="g_up", tn=1408, tk=rows)

    def norm_bwd(col, w, xx, dh, dres):
        _, vjp = jax.vjp(_rms, xx, w)
        dx, dw = vjp(dh)
        return dx + dres, dx + dres, dw

    dx1, dx1_b, d_norm2_w = _tiles(norm_bwd, name="norm2_bwd", rows=rows, tm=tm, full_consts=[norm2_w],
                                   row_ins=[(x1, D_MODEL, 0), (dh2, D_MODEL, 0), (dx2, D_MODEL, 0)],
                                   row_outs=[(D_MODEL, F32), (D_MODEL, BF16)], acc_outs=[(1, D_MODEL)])
    dmix = _mm(dx1_b, w_out, dims="nt", name="d_mix", tk=1024)
    g_out = _mm(mix, dx1_b, dims="tn", name="g_out", tk=rows)

    def mix_g_bwd(col, w, o, z, d):
        _, vjp = jax.vjp(mix_g_fn, w, o, z)
        dw, do_, dz = vjp(d)
        return do_, dz, dw

    do_gdn, dz, d_on = _tiles(mix_g_bwd, name="mix_gdn_bwd", rows=rows, tm=rows, ncol=PAIRS, full_consts=[w_on],
                              row_ins=[(o_gdn, LANES, 0), (proj, LANES, z_off), (dmix, LANES, 0)],
                              row_outs=[(LANES, F32), (LANES, F32)], acc_outs=[(1, LANES)])

    def mix_f_bwd(col, a, g, d):
        _, vjp = jax.vjp(mix_f_fn, a, g)
        return vjp(d)

    dao, dfgate = _tiles(mix_f_bwd, name="mix_fox_bwd", rows=rows, tm=rows, ncol=PAIRS,
                         row_ins=[(ao, LANES, 0), (proj, LANES, fg_off), (dmix, LANES, PAIRS)],
                         row_outs=[(LANES, F32), (LANES, F32)])

    delta = _attention_delta(fqk, proj, frow, lse, dao, rows)
    dfq, dfk, dfv, dfrow = _attention_backward(fqk, proj, frow, delta, lse, dao, rows)
    dfqk_n = jnp.concatenate([dfq, dfk], axis=1)

    def fox_prep_bwd(col, w, xx, d):
        _, vjp = jax.vjp(_head_rms, w, xx)
        dw, dx = vjp(d)
        return dx, dw

    dfqk, d_wqk = _tiles(fox_prep_bwd, name="fox_prep_bwd", rows=rows, tm=rows, ncol=2 * PAIRS,
                         col_consts=[(w_qk, 1, LANES, 0)],
                         row_ins=[(proj, LANES, fox_off), (dfqk_n, LANES, 0)],
                         row_outs=[(LANES, F32)], acc_outs=[(1, LANES)])

    dq, dk, dv, dbetax, dgcx, dgrow = _gdn_backward(qkv, betax, gcx, grow, ssave, tsave, do_gdn, rows)
    dqkv_n = jnp.concatenate([dq, dk, dv], axis=1)
    dqkv, d_conv = _tiles(_gdn_prep_bwd, name="gdn_prep_bwd", rows=rows, tm=rows, ncol=3 * PAIRS,
                          col_consts=[(conv_w, CONV_K, LANES, 0)],
                          row_ins=[(proj, LANES, 0), (dqkv_n, LANES, 0)],
                          row_outs=[(LANES, F32)], acc_outs=[(CONV_K, LANES)])

    def expand_bwd(col, b, g, db, dg):
        return (_dot32(db, b, _CONTRACT["nt"]), _dot32(dg, g, _CONTRACT["nt"]))

    dgates_b, dcums_g = _tiles(expand_bwd, name="expand_bwd", rows=rows, tm=tm, full_consts=[xb, xg],
                               row_ins=[(dbetax, WIDTH, 0), (dgcx, WIDTH, 0)],
                               row_outs=[(LANES, F32), (LANES, F32)])
    dcums_row = jnp.concatenate([jnp.zeros((rows, 8), F32), _rowform_to_lanes(dgrow, rows),
                                 dfrow.reshape(HEADS, rows).T, jnp.zeros((rows, LANES - 24), F32)], axis=1)

    def gates_bwd(col, lcv, lfv, a, dt, fb, pre, dgb, dcg, dcr):
        lane = _lane_ids(pre.shape)
        dgates = jnp.where(lane < 8, dgb, _cums_bwd(lcv, lfv, dcg + dcr))
        _, vjp = jax.vjp(_gates_elem, a, dt, fb, pre)
        da, ddt, dfb, dpre = vjp(dgates)
        return dpre, da, ddt, dfb

    dpre, d_a, d_dt, d_fb = _tiles(gates_bwd, name="gates_bwd", rows=rows, tm=rows,
                                   full_consts=[lc, lf, p_a, p_dt, p_fb],
                                   row_ins=[(proj, LANES, COL_SMALL), (dgates_b, LANES, 0), (dcums_g, LANES, 0),
                                            (dcums_row, LANES, 0)],
                                   row_outs=[(LANES, F32)], acc_outs=[(1, LANES)] * 3)

    dproj = jnp.concatenate([dqkv.astype(BF16), dz.astype(BF16), dfqk.astype(BF16), dfv.astype(BF16),
                             dfgate.astype(BF16), dpre.astype(BF16)], axis=1)
    dh1 = _mm(dproj, w_cat, dims="nt", name="d_h1", tk=D_CAT)
    g_cat = _mm(h1, dproj, dims="tn", name="g_in", tn=384, tk=rows)

    def norm1_bwd(col, w, xx, dh, dres):
        _, vjp = jax.vjp(_rms, xx, w)
        dx, dw = vjp(dh)
        return dx + dres, dw

    grad_x, d_norm1_w = _tiles(norm1_bwd, name="norm1_bwd", rows=rows, tm=tm, full_consts=[norm1_w],
                               row_ins=[(x, D_MODEL, 0), (dh1, D_MODEL, 0), (dx1, D_MODEL, 0)],
                               row_outs=[(D_MODEL, F32)], acc_outs=[(1, D_MODEL)])

    fold = lambda v: v.reshape(-1, HEAD_DIM).sum(axis=0)
    small = dict(
        loss=loss[0, 0],
        norm1_w=d_norm1_w, conv_w=d_conv, a_log=d_a[0, 8:16], dt_bias=d_dt[0, 8:16],
        out_norm_w=fold(d_on), f_bias=d_fb[0, 16:24], q_norm_w=fold(d_wqk[:, :WIDTH]),
        k_norm_w=fold(d_wqk[:, WIDTH:]), norm2_w=d_norm2_w, final_w=d_final_w)
    return grad_x, g_cat, g_out, g_gate, g_up, g_down, small


HBM_SPEC = pl.BlockSpec(memory_space=pltpu.HBM)


def _place():
    x, y, c = lax.axis_index("x"), lax.axis_index("y"), lax.axis_index("c")
    chips = [(1 - x, y), (x, 1 - y), (1 - x, 1 - y)]
    return x, y, c, 2 * x + y, (x, y, 1 - c), chips, [2 * cx + cy for cx, cy in chips]


def _remote(src, dst, send_sem, recv_sem, to):
    return pltpu.make_async_remote_copy(src_ref=src, dst_ref=dst, send_sem=send_sem, recv_sem=recv_sem,
                                        device_id=to, device_id_type=MESH)


def _allgather_weights(shards, conv):
    n = len(shards)
    halves = [s.shape[0] // 2 for s in shards]
    per = 6
    own_base = n * per + 3

    def body(*refs):
        ins, conv_in = refs[:n], refs[n]
        outs, conv_out = refs[n + 1:2 * n + 1], refs[2 * n + 1]
        send_sems, recv_sems = refs[2 * n + 2:]
        x, y, c, own, sib, chips, chip_idx = _place()

        def half(i, ref, hc):
            return ref.at[pl.ds(pl.multiple_of(hc * halves[i], 16), halves[i]), :]

        sent = []
        for i, (src, dst) in enumerate(zip(list(ins) + [conv_in], list(outs) + [conv_out])):
            k = own_base + i
            sent.append(_remote(src, dst.at[own], send_sems.at[k], recv_sems.at[k], sib))
        for i in range(n):
            for j, chip in enumerate(chips):
                k = i * per + j
                sent.append(_remote(half(i, ins[i], c), half(i, outs[i].at[own], c),
                                    send_sems.at[k], recv_sems.at[k], (*chip, c)))
        for j, chip in enumerate(chips):
            k = n * per + j
            sent.append(_remote(conv_in, conv_out.at[own], send_sems.at[k], recv_sems.at[k], (*chip, c)))
        for cp in sent:
            cp.start()
        for i in range(n):
            for j in range(len(chips)):
                k = i * per + j
                landed = half(i, outs[i].at[chip_idx[j]], c)
                _remote(landed, landed, send_sems.at[k], recv_sems.at[k], sib).wait_recv()
                fwd = _remote(landed, landed, send_sems.at[k + 3], recv_sems.at[k + 3], sib)
                fwd.start()
                sent.append(fwd)
        for i in range(n):
            for j in range(len(chips)):
                k = i * per + 3 + j
                landed = half(i, outs[i].at[chip_idx[j]], 1 - c)
                _remote(landed, landed, send_sems.at[k], recv_sems.at[k], sib).wait_recv()
        for j in range(len(chips)):
            k = n * per + j
            landed = conv_out.at[chip_idx[j]]
            _remote(landed, landed, send_sems.at[k], recv_sems.at[k], sib).wait_recv()
        for i, dst in enumerate(list(outs) + [conv_out]):
            k = own_base + i
            landed = dst.at[own]
            _remote(landed, landed, send_sems.at[k], recv_sems.at[k], sib).wait_recv()
        for cp in sent:
            cp.wait_send()

    n_sem = own_base + n + 1
    out_shape = [jax.ShapeDtypeStruct((N_CHIPS,) + s.shape, s.dtype) for s in shards]
    out_shape.append(jax.ShapeDtypeStruct((N_CHIPS,) + conv.shape, conv.dtype))
    res = pl.pallas_call(
        body, name="allgather_weights", out_shape=out_shape,
        in_specs=[HBM_SPEC] * (n + 1), out_specs=[HBM_SPEC] * (n + 1),
        scratch_shapes=[pltpu.SemaphoreType.DMA((n_sem,)), pltpu.SemaphoreType.DMA((n_sem,))],
    )(*shards, conv)
    return res[:n], res[n]


def _swap_halves(stacks):
    n = len(stacks)

    def body(*refs):
        ins, outs = refs[:n], refs[n:2 * n]
        send_sems, recv_sems = refs[2 * n:]
        x, y, c, own, sib, chips, chip_idx = _place()
        cps = []
        for i in range(n):
            h = stacks[i].shape[1] // 2
            src = ins[i].at[:, pl.ds(pl.multiple_of((1 - c) * h, 8), h), :]
            cps.append(_remote(src, outs[i], send_sems.at[i], recv_sems.at[i], sib))
        for cp in cps:
            cp.start()
        for cp in cps:
            cp.wait()

    out_shape = [jax.ShapeDtypeStruct((N_CHIPS, s.shape[1] // 2, s.shape[2]), s.dtype) for s in stacks]
    return pl.pallas_call(
        body, name="rs_swap_halves", out_shape=out_shape,
        in_specs=[HBM_SPEC] * n, out_specs=[HBM_SPEC] * n,
        scratch_shapes=[pltpu.SemaphoreType.DMA((n,)), pltpu.SemaphoreType.DMA((n,))],
    )(*stacks)


def _add_half(stack, landed, place, name):
    _, h, cols = landed.shape

    def body(place_ref, a_ref, b_ref, o_ref, own_ref):
        part = (a_ref[...] + b_ref[...]).astype(o_ref.dtype)
        o_ref[...] = part

        @pl.when(pl.program_id(0) == place_ref[1])
        def _():
            own_ref[...] = part[0]

    return pl.pallas_call(
        body, name=name,
        out_shape=[jax.ShapeDtypeStruct(landed.shape, BF16), jax.ShapeDtypeStruct((h, cols), BF16)],
        grid_spec=pltpu.PrefetchScalarGridSpec(
            num_scalar_prefetch=1, grid=(N_CHIPS,),
            in_specs=[pl.BlockSpec((1, h, cols), lambda j, p: (j, p[0], 0)),
                      pl.BlockSpec((1, h, cols), lambda j, p: (j, 0, 0))],
            out_specs=[pl.BlockSpec((1, h, cols), lambda j, p: (j, 0, 0)),
                       pl.BlockSpec((h, cols), lambda j, p: (0, 0))]),
        compiler_params=_params(("arbitrary",)),
    )(place, stack, landed)


def _exchange_partials(parts):
    n = len(parts)

    def body(*refs):
        ins, outs = refs[:n], refs[n:2 * n]
        send_sems, recv_sems = refs[2 * n:]
        x, y, c, own, sib, chips, chip_idx = _place()
        sent = []
        for i in range(n):
            for j, chip in enumerate(chips):
                k = i * 3 + j
                sent.append(_remote(ins[i].at[chip_idx[j]], outs[i].at[j], send_sems.at[k], recv_sems.at[k],
                                    (*chip, c)))
        for cp in sent:
            cp.start()
        for i in range(n):
            for j in range(len(chips)):
                k = i * 3 + j
                landed = outs[i].at[j]
                _remote(landed, landed, send_sems.at[k], recv_sems.at[k], sib).wait_recv()
        for cp in sent:
            cp.wait_send()

    return pl.pallas_call(
        body, name="rs_exchange_partials",
        out_shape=[jax.ShapeDtypeStruct((3,) + p.shape[1:], p.dtype) for p in parts],
        in_specs=[HBM_SPEC] * n, out_specs=[HBM_SPEC] * n,
        scratch_shapes=[pltpu.SemaphoreType.DMA((3 * n,)), pltpu.SemaphoreType.DMA((3 * n,))],
    )(*parts)


def _sum_partials(own_part, landed, name):
    _, h, cols = landed.shape

    def body(own_ref, a_ref, o_ref):
        acc = own_ref[...].astype(F32)
        for s in range(3):
            acc = acc + a_ref[s].astype(F32)
        o_ref[...] = acc

    return pl.pallas_call(
        body, name=name, out_shape=jax.ShapeDtypeStruct((h, cols), F32), grid=(1,),
        in_specs=[pl.BlockSpec((h, cols), lambda i: (0, 0)), pl.BlockSpec(landed.shape, lambda i: (0, 0, 0))],
        out_specs=pl.BlockSpec((h, cols), lambda i: (0, 0)),
        compiler_params=_params(("arbitrary",)),
    )(own_part, landed)


def _share_halves(halves):
    n = len(halves)

    def body(*refs):
        ins, outs = refs[:n], refs[n:2 * n]
        send_sems, recv_sems = refs[2 * n:]
        x, y, c, own, sib, chips, chip_idx = _place()
        cps = [_remote(ins[i], outs[i], send_sems.at[i], recv_sems.at[i], sib) for i in range(n)]
        for cp in cps:
            cp.start()
        for cp in cps:
            cp.wait()

    return pl.pallas_call(
        body, name="rs_share_halves",
        out_shape=[jax.ShapeDtypeStruct(p.shape, p.dtype) for p in halves],
        in_specs=[HBM_SPEC] * n, out_specs=[HBM_SPEC] * n,
        scratch_shapes=[pltpu.SemaphoreType.DMA((n,)), pltpu.SemaphoreType.DMA((n,))],
    )(*halves)


def _allreduce_small(packed):
    rows = packed.shape[0]
    n_dev = 8

    def body(in_ref, out_ref, gath, send_sems, recv_sems):
        x, y, c = lax.axis_index("x"), lax.axis_index("y"), lax.axis_index("c")
        me = 4 * x + 2 * y + c
        gath[me] = in_ref[...]
        cps = []
        for k in range(1, n_dev):
            fx, fy, fc = (k >> 2) & 1, (k >> 1) & 1, k & 1
            to = (x ^ fx, y ^ fy, c ^ fc)
            cps.append(_remote(in_ref, gath.at[me], send_sems.at[k - 1], recv_sems.at[k - 1], to))
        for cp in cps:
            cp.start()
        for k in range(1, n_dev):
            fx, fy, fc = (k >> 2) & 1, (k >> 1) & 1, k & 1
            src = 4 * (x ^ fx) + 2 * (y ^ fy) + (c ^ fc)
            slot = gath.at[src]
            _remote(slot, slot, send_sems.at[k - 1], recv_sems.at[k - 1], (x, y, c)).wait_recv()
        for cp in cps:
            cp.wait_send()
        acc = gath[0]
        for d in range(1, n_dev):
            acc = acc + gath[d]
        out_ref[...] = acc

    vm = pl.BlockSpec(memory_space=pltpu.VMEM)
    return pl.pallas_call(
        body, name="allreduce_small", out_shape=jax.ShapeDtypeStruct(packed.shape, F32),
        in_specs=[vm], out_specs=vm,
        scratch_shapes=[pltpu.VMEM((n_dev, rows, LANES), F32),
                        pltpu.SemaphoreType.DMA((n_dev - 1,)), pltpu.SemaphoreType.DMA((n_dev - 1,))],
    )(packed)


def _adam(col, w, g, m, v):
    m2 = ADAM_B1 * m + (1.0 - ADAM_B1) * g
    v2 = ADAM_B2 * v + (1.0 - ADAM_B2) * (g * g)
    m_hat = m2 / (1.0 - ADAM_B1 ** ADAM_STEP)
    v_hat = v2 / (1.0 - ADAM_B2 ** ADAM_STEP)
    delta = -ADAM_LR * (m_hat / (jnp.sqrt(v_hat) + ADAM_EPS) + ADAM_WD * w)
    return delta, m2, v2


def _adam_call(w, g, m, v, name):
    rows, cols = w.shape
    tm = rows
    for cand in (256, 352, 176, 128, 64, 48, 16, 8):
        if rows % cand == 0:
            tm = cand
            break
    return _tiles(_adam, name=name, rows=rows, tm=tm,
                  row_ins=[(w, cols, 0), (g, cols, 0), (m, cols, 0), (v, cols, 0)],
                  row_outs=[(cols, F32)] * 3)


def _adam_big(w, g_mine, g_other, m, v, place, name):
    rows, cols = w.shape
    h = rows // 2
    tm = next(t for t in (256, 176, 128) if h % t == 0)
    nt = h // tm

    def body(place_ref, w_ref, gm_ref, go_ref, m_ref, v_ref, g_out, d_out, m_out, v_out):
        g = jnp.where(pl.program_id(0) == place_ref[0], gm_ref[...], go_ref[...])
        d, m2, v2 = _adam(None, w_ref[...], g, m_ref[...], v_ref[...])
        g_out[...] = g
        d_out[...] = d
        m_out[...] = m2
        v_out[...] = v2

    full = pl.BlockSpec((tm, cols), lambda hh, i, p: (hh * nt + i, 0))
    half = pl.BlockSpec((tm, cols), lambda hh, i, p: (i, 0))
    return pl.pallas_call(
        body, name=name, out_shape=[jax.ShapeDtypeStruct(w.shape, F32)] * 4,
        grid_spec=pltpu.PrefetchScalarGridSpec(
            num_scalar_prefetch=1, grid=(2, nt),
            in_specs=[full, half, half, full, full], out_specs=[full] * 4),
        compiler_params=_params(("arbitrary", "arbitrary")),
    )(place, w, g_mine, g_other, m, v)


def _pack(arrays):
    flat = []
    for a in arrays:
        a = a.reshape(-1).astype(F32)
        flat.append(jnp.pad(a, (0, (-a.size) % LANES)))
    out = jnp.concatenate(flat)
    out = jnp.pad(out, (0, (-out.size) % (8 * LANES)))
    return out.reshape(-1, LANES)


def _unpack(packed, shapes):
    flat = packed.reshape(-1)
    out, off = [], 0
    for s in shapes:
        size = int(np.prod(s))
        out.append(flat[off:off + size].reshape(s))
        off += size + (-size) % LANES
    return out


def kernel(x, norm1_w, w_in, gdn_conv_w, gdn_A_log, gdn_dt_bias, gdn_out_norm_w, fox_f_bias, fox_q_norm_w, fox_k_norm_w, w_out, norm2_w, w_ffn_gate, w_ffn_up, w_ffn_down, final_norm_w, loss_target, m_norm1_w, m_w_in, m_gdn_conv_w, m_gdn_A_log, m_gdn_dt_bias, m_gdn_out_norm_w, m_fox_f_bias, m_fox_q_norm_w, m_fox_k_norm_w, m_w_out, m_norm2_w, m_w_ffn_gate, m_w_ffn_up, m_w_ffn_down, m_final_norm_w, v_norm1_w, v_w_in, v_gdn_conv_w, v_gdn_A_log, v_gdn_dt_bias, v_gdn_out_norm_w, v_fox_f_bias, v_fox_q_norm_w, v_fox_k_norm_w, v_w_out, v_norm2_w, v_w_ffn_gate, v_w_ffn_up, v_w_ffn_down, v_final_norm_w):
    cx, cy, cc = lax.axis_index("x"), lax.axis_index("y"), lax.axis_index("c")
    own = 2 * cx + cy
    place = jnp.stack([cc, own]).astype(jnp.int32)

    big_w = [w_in[0], w_out[0], w_ffn_gate[0], w_ffn_up[0], w_ffn_down[0]]
    gathered, conv_g = _allgather_weights([w.astype(BF16) for w in big_w], gdn_conv_w[0])
    by_cols = lambda g: g.transpose(1, 0, 2).reshape(g.shape[1], N_CHIPS * g.shape[2])
    by_rows = lambda g: g.reshape(N_CHIPS * g.shape[1], g.shape[2])
    w_cat = _cat_weights(by_cols(gathered[0]))
    conv_full = by_cols(conv_g)

    grad_x, g_cat, g_out, g_gate, g_up, g_down, small = _local_step(
        x[0], loss_target[0], norm1_w, w_cat, conv_full, gdn_A_log[0], gdn_dt_bias[0], gdn_out_norm_w[0],
        fox_f_bias[0], fox_q_norm_w[0], fox_k_norm_w[0], by_rows(gathered[1]), norm2_w,
        by_cols(gathered[2]), by_cols(gathered[3]), by_rows(gathered[4]), final_norm_w.reshape(1, -1))

    col_stack = lambda g: g.reshape(g.shape[0], N_CHIPS, g.shape[1] // N_CHIPS).transpose(1, 0, 2)
    row_stack = lambda g: g.reshape(N_CHIPS, g.shape[0] // N_CHIPS, g.shape[1])
    stacks = [col_stack(_uncat_grad(g_cat)), row_stack(g_out), col_stack(g_gate), col_stack(g_up),
              row_stack(g_down)]
    landed = _swap_halves(stacks)
    names = ["w_in", "w_out", "w_gate", "w_up", "w_down"]
    added = [_add_half(s, l, place, "rs_add_" + nm) for s, l, nm in zip(stacks, landed, names)]
    from_chips = _exchange_partials([a[0] for a in added])
    halves = [_sum_partials(a[1], p, "rs_sum_" + nm) for a, p, nm in zip(added, from_chips, names)]
    others = _share_halves(halves)
    big_m = [m_w_in[0], m_w_out[0], m_w_ffn_gate[0], m_w_ffn_up[0], m_w_ffn_down[0]]
    big_v = [v_w_in[0], v_w_out[0], v_w_ffn_gate[0], v_w_ffn_up[0], v_w_ffn_down[0]]
    big_upd = [_adam_big(w, gm, go, m, v, place, "adam_" + nm)
               for w, gm, go, m, v, nm in zip(big_w, halves, others, big_m, big_v, names)]

    order = ["norm1_w", "conv_w", "a_log", "dt_bias", "out_norm_w", "f_bias", "q_norm_w", "k_norm_w",
             "norm2_w", "final_w"]
    red = _allreduce_small(_pack([small[k] for k in order] + [small["loss"]]))
    red_shapes = [(1, D_MODEL), (CONV_K, 3 * WIDTH), (1, HEADS), (1, HEADS), (1, HEAD_DIM), (1, HEADS),
                  (1, HEAD_DIM), (1, HEAD_DIM), (1, D_MODEL), (D_MODEL,), ()]
    red_list = _unpack(red, red_shapes)
    loss = red_list[-1]
    small_g = dict(zip(order, red_list[:-1]))
    shard_cols = 3 * WIDTH // N_CHIPS
    small_g["conv_w"] = lax.dynamic_slice_in_dim(small_g["conv_w"], own * shard_cols, shard_cols, axis=1)[None]
    small_w = [norm1_w, gdn_conv_w, gdn_A_log, gdn_dt_bias, gdn_out_norm_w, fox_f_bias, fox_q_norm_w,
               fox_k_norm_w, norm2_w, final_norm_w]
    small_m = [m_norm1_w, m_gdn_conv_w, m_gdn_A_log, m_gdn_dt_bias, m_gdn_out_norm_w, m_fox_f_bias,
               m_fox_q_norm_w, m_fox_k_norm_w, m_norm2_w, m_final_norm_w]
    small_v = [v_norm1_w, v_gdn_conv_w, v_gdn_A_log, v_gdn_dt_bias, v_gdn_out_norm_w, v_fox_f_bias,
               v_fox_q_norm_w, v_fox_k_norm_w, v_norm2_w, v_final_norm_w]
    small_gl = [small_g[k].reshape(w.shape) for k, w in zip(order, small_w)]
    s_delta, s_m, s_v = _adam_call(_pack(small_w), _pack(small_gl), _pack(small_m), _pack(small_v), "adam_small")
    shapes = [w.shape for w in small_w]
    s_delta, s_m, s_v = _unpack(s_delta, shapes), _unpack(s_m, shapes), _unpack(s_v, shapes)

    big_pos = {1: 0, 9: 1, 11: 2, 12: 3, 13: 4}
    small_pos = {0: 0, 2: 1, 3: 2, 4: 3, 5: 4, 6: 5, 7: 6, 8: 7, 10: 8, 14: 9}
    grads, deltas, new_m, new_v = [], [], [], []
    for pos in range(15):
        if pos in big_pos:
            b = big_pos[pos]
            g, d, m2, v2 = big_upd[b]
            grads.append(g[None])
            deltas.append(d[None])
            new_m.append(m2[None])
            new_v.append(v2[None])
        else:
            s = small_pos[pos]
            grads.append(small_gl[s])
            deltas.append(s_delta[s])
            new_m.append(s_m[s])
            new_v.append(s_v[s])
    return (loss, grad_x[None], *grads, *deltas, *new_m, *new_v)
```

```python
import jax
import jax.numpy as jnp
import numpy as np
from jax import lax
from jax.experimental import pallas as pl
from jax.experimental.pallas import tpu as pltpu

F32 = jnp.float32
BF16 = jnp.bfloat16

D_MODEL = 1024
HEADS = 8
HEAD_DIM = 64
PAIRS = HEADS // 2
WIDTH = HEADS * HEAD_DIM
CHUNK = 64
CONV_K = 4
D_FF = 2816
FF_SHARD = D_FF // 4
EPS = 1e-6
SCALE = HEAD_DIM ** -0.5
LANES = 128
N_CHIPS = 4
D_IN = 4120
D_CAT = 4224
COL_SMALL = 4096 // LANES

ADAM_LR = 0.001
ADAM_B1 = 0.9
ADAM_B2 = 0.999
ADAM_EPS = 1e-08
ADAM_WD = 0.01
ADAM_STEP = 10

VMEM_LIMIT = 56 * 1024 * 1024
MESH = pl.DeviceIdType.MESH
HIGHEST = lax.Precision.HIGHEST


def _params(sem):
    return pltpu.CompilerParams(dimension_semantics=sem, vmem_limit_bytes=VMEM_LIMIT)


_CONTRACT = {"nn": ((1,), (0,)), "nt": ((1,), (1,)), "tn": ((0,), (0,))}


def _mm(a, b, *, dims, name, out_dtype=F32, add=None, tm=1024, tn=512, tk=512):
    if dims == "nn":
        (m, k), (k2, n) = a.shape, b.shape
    elif dims == "nt":
        (m, k), (n, k2) = a.shape, b.shape
    else:
        (k, m), (k2, n) = a.shape, b.shape
    assert k == k2, (a.shape, b.shape, dims)
    tm, tn, tk = min(tm, m), min(tn, n), min(tk, k)
    assert m % tm == 0 and n % tn == 0 and k % tk == 0, (m, n, k, tm, tn, tk)
    nk = k // tk
    a_spec = (pl.BlockSpec((tk, tm), lambda i, j, kk: (kk, i)) if dims == "tn"
              else pl.BlockSpec((tm, tk), lambda i, j, kk: (i, kk)))
    b_spec = (pl.BlockSpec((tn, tk), lambda i, j, kk: (j, kk)) if dims == "nt"
              else pl.BlockSpec((tk, tn), lambda i, j, kk: (kk, j)))
    o_spec = pl.BlockSpec((tm, tn), lambda i, j, kk: (i, j))
    contract = (_CONTRACT[dims], ((), ()))
    has_add = add is not None

    def body(*refs):
        a_ref, b_ref = refs[:2]
        add_ref = refs[2] if has_add else None
        o_ref = refs[3] if has_add else refs[2]
        part = lax.dot_general(a_ref[...].astype(BF16), b_ref[...].astype(BF16), contract,
                               preferred_element_type=F32)

        def finish(r):
            if has_add:
                r = r + add_ref[...].astype(F32)
            o_ref[...] = r.astype(out_dtype)

        if nk == 1:
            finish(part)
            return
        acc = refs[-1]
        kk = pl.program_id(2)

        @pl.when(kk == 0)
        def _():
            acc[...] = part

        @pl.when(kk > 0)
        def _():
            acc[...] += part

        @pl.when(kk == nk - 1)
        def _():
            finish(acc[...])

    ins = [a, b] + ([add] if has_add else [])
    in_specs = [a_spec, b_spec] + ([o_spec] if has_add else [])
    return pl.pallas_call(
        body, name=name, grid=(m // tm, n // tn, nk),
        in_specs=in_specs, out_specs=o_spec,
        out_shape=jax.ShapeDtypeStruct((m, n), out_dtype),
        scratch_shapes=[pltpu.VMEM((tm, tn), F32)] if nk > 1 else [],
        compiler_params=_params(("parallel", "parallel", "arbitrary")),
    )(*ins)


def _mm_blocks(a, b, *, name, grid, a_spec, b_spec, o_spec, out_shape, dims, n_sum=0, add=None, add_spec=None):
    contract = (_CONTRACT[dims], ((), ()))
    has_add = add is not None

    def body(*refs):
        a_ref, b_ref = refs[:2]
        o_ref = refs[-1]
        dot = lambda x, y: lax.dot_general(x.astype(BF16), y.astype(BF16), contract, preferred_element_type=F32)
        if n_sum:
            r = dot(a_ref[0], b_ref[0])
            for s in range(1, n_sum):
                r = r + dot(a_ref[s], b_ref[s])
        else:
            r = dot(a_ref[...], b_ref[...])
        if has_add:
            r = r + refs[2][...].astype(F32)
        o_ref[...] = r.astype(o_ref.dtype)

    return pl.pallas_call(
        body, name=name, grid=grid,
        in_specs=[a_spec, b_spec] + ([add_spec] if has_add else []), out_specs=o_spec, out_shape=out_shape,
        compiler_params=_params(("parallel",) * len(grid)),
    )(*([a, b] + ([add] if has_add else [])))


def _tiles(fn, *, name, rows, tm, ncol=1, row_ins=(), col_consts=(), full_consts=(),
           row_outs=(), acc_outs=()):
    nt = rows // tm
    assert rows % tm == 0
    n_full, n_col, n_row = len(full_consts), len(col_consts), len(row_ins)
    n_ro, n_acc = len(row_outs), len(acc_outs)

    def body(*refs):
        ins = refs[:n_full + n_col + n_row]
        outs = refs[n_full + n_col + n_row:]
        i = pl.program_id(1)
        res = fn(pl.program_id(0), *[r[...] for r in ins])
        for r, v in zip(outs[:n_ro], res[:n_ro]):
            r[...] = v.astype(r.dtype)
        if n_acc:
            @pl.when(i == 0)
            def _():
                for r in outs[n_ro:]:
                    r[...] = jnp.zeros_like(r)
            for r, v in zip(outs[n_ro:], res[n_ro:]):
                r[...] += v

    in_specs = [pl.BlockSpec(a.shape, lambda j, i, nd=a.ndim: (0,) * nd) for a in full_consts]
    in_specs += [pl.BlockSpec((nr, w), lambda j, i, o=o: (0, o + j)) for (_, nr, w, o) in col_consts]
    in_specs += [pl.BlockSpec((tm, w), lambda j, i, o=o: (i, o + j)) for (_, w, o) in row_ins]
    out_specs = [pl.BlockSpec((tm, w), lambda j, i: (i, j)) for (w, _) in row_outs]
    out_specs += [pl.BlockSpec((nr, w), lambda j, i: (0, j)) for (nr, w) in acc_outs]
    out_shape = [jax.ShapeDtypeStruct((rows, w * ncol), dt) for (w, dt) in row_outs]
    out_shape += [jax.ShapeDtypeStruct((nr, w * ncol), F32) for (nr, w) in acc_outs]
    args = list(full_consts) + [c[0] for c in col_consts] + [r[0] for r in row_ins]
    out = pl.pallas_call(
        body, name=name, grid=(ncol, nt), in_specs=in_specs, out_specs=out_specs, out_shape=out_shape,
        compiler_params=_params(("parallel", "arbitrary")),
    )(*args)
    return out


def _rms(x, w):
    return x * lax.rsqrt(jnp.mean(x * x, axis=-1, keepdims=True) + EPS) * w


def _lane_lo(shape):
    return lax.broadcasted_iota(jnp.int32, shape, len(shape) - 1) < HEAD_DIM


def _pair_sum(x):
    lo = _lane_lo(x.shape)
    s0 = jnp.sum(jnp.where(lo, x, 0.0), axis=-1, keepdims=True)
    s1 = jnp.sum(jnp.where(lo, 0.0, x), axis=-1, keepdims=True)
    return jnp.where(lo, s0, s1)


def _head_col(x, lo, h):
    keep = lo if h == 0 else jnp.logical_not(lo)
    return jnp.max(jnp.where(keep, x, -jnp.inf), axis=-1, keepdims=True)


def _softplus(x):
    return jnp.maximum(x, 0.0) + jnp.log1p(jnp.exp(-jnp.abs(x)))


def _silu(x):
    return x * jax.nn.sigmoid(x)


def _dot(a, b, contract):
    return lax.dot_general(a.astype(BF16), b.astype(BF16), (contract, ((), ())),
                           preferred_element_type=F32)


def _dot32(a, b, contract):
    return lax.dot_general(a, b, (contract, ((), ())), precision=HIGHEST, preferred_element_type=F32)


def _bd(y):
    yy = jnp.concatenate([y, y], axis=0)
    r = lax.broadcasted_iota(jnp.int32, yy.shape, 0) < HEAD_DIM
    c = lax.broadcasted_iota(jnp.int32, yy.shape, 1) < HEAD_DIM
    return jnp.where(r == c, yy, 0.0)


def _pp(x, y):
    return _dot(x, _bd(y), _CONTRACT["nn"])


def _pp_nt(x, y):
    return _dot(x, _bd(y), _CONTRACT["nt"])


def _pp_tn(x, y):
    full = _dot(x, y, _CONTRACT["tn"])
    return jnp.where(_lane_lo((HEAD_DIM, LANES)), full[:HEAD_DIM], full[HEAD_DIM:])


def _gdn_masks():
    row = lax.broadcasted_iota(jnp.int32, (CHUNK, LANES), 0)
    col = lax.broadcasted_iota(jnp.int32, (CHUNK, LANES), 1) % HEAD_DIM
    return row, col


def _interleave(chains):
    live = list(chains)
    while live:
        for g in list(live):
            try:
                next(g)
            except StopIteration:
                live.remove(g)


def _gdn_forward(qkv, betax, gcx, grow, rows):
    nchunk = rows // CHUNK

    def body(q_ref, k_ref, v_ref, bx_ref, gx_ref, gr_ref, o_ref, ss_ref, ts_ref, state):
        n = pl.program_id(0)

        @pl.when(n == 0)
        def _():
            state[...] = jnp.zeros_like(state)

        row, col = _gdn_masks()
        incl, strict = col <= row, col < row

        def chain(p):
            lanes = pl.ds(p * LANES, LANES)
            q, k, v, bx, gx = q_ref[:, lanes], k_ref[:, lanes], v_ref[:, lanes], bx_ref[:, lanes], gx_ref[:, lanes]
            gr = gr_ref[0, p]
            glast = gx_ref[pl.ds(CHUNK - 1, 1), lanes]
            s = state[p]
            dm = jnp.where(incl, jnp.exp(jnp.minimum(gx - gr, 0.0)), 0.0)
            kb, vb, eg, qs = k * bx, v * bx, jnp.exp(gx), q * SCALE
            yield
            big_g, big_p = _pp_nt(kb, k), _pp_nt(qs, k)
            yield
            x = -jnp.where(strict, big_g * dm, 0.0)
            att = jnp.where(incl, big_p * dm, 0.0)
            tm = jnp.where(row == col, 1.0, 0.0) + x
            x = _pp(x, x)
            yield
            for _ in range(4):
                step, x = _pp(tm, x), _pp(x, x)
                yield
                tm = tm + step
            tm = tm + _pp(tm, x)
            yield
            u, w = _pp(tm, vb), _pp(tm, kb * eg)
            yield
            ws, qgs = _pp(w, s), _pp(qs * eg, s)
            yield
            vn = u - ws
            kd = k * jnp.exp(glast - gx)
            avn, upd = _pp(att, vn), _pp_tn(kd, vn)
            yield
            ss_ref[0, p] = s
            ts_ref[0, p] = tm
            o_ref[:, lanes] = qgs + avn
            state[p] = s * jnp.exp(glast) + upd

        _interleave([chain(p) for p in range(PAIRS)])

    blk = lambda j: pl.BlockSpec((CHUNK, WIDTH), lambda n, j=j: (n, j))
    sv = pl.BlockSpec((1, PAIRS, CHUNK, LANES), lambda n: (n, 0, 0, 0))
    return pl.pallas_call(
        body, name="gdn_fwd", grid=(nchunk,),
        in_specs=[blk(0), blk(1), blk(2), blk(0), blk(0),
                  pl.BlockSpec((1, PAIRS, 1, LANES), lambda n: (n, 0, 0, 0))],
        out_specs=[blk(0), sv, sv],
        out_shape=[jax.ShapeDtypeStruct((rows, WIDTH), F32),
                   jax.ShapeDtypeStruct((nchunk, PAIRS, CHUNK, LANES), F32),
                   jax.ShapeDtypeStruct((nchunk, PAIRS, CHUNK, LANES), F32)],
        scratch_shapes=[pltpu.VMEM((PAIRS, CHUNK, LANES), F32)],
        compiler_params=_params(("arbitrary",)),
    )(qkv, qkv, qkv, betax, gcx, grow)


def _gdn_backward(qkv, betax, gcx, grow, ssave, tsave, do, rows):
    nchunk = rows // CHUNK

    def body(q_ref, k_ref, v_ref, bx_ref, gx_ref, gr_ref, ss_ref, ts_ref, do_ref,
             dq_ref, dk_ref, dv_ref, dbx_ref, dgx_ref, dgr_ref, dstate):
        n = pl.program_id(0)

        @pl.when(n == 0)
        def _():
            dstate[...] = jnp.zeros_like(dstate)

        row, col = _gdn_masks()
        incl, strict = col <= row, col < row

        def chain(p):
            lanes = pl.ds(p * LANES, LANES)
            q, k, v, bx, gx = q_ref[:, lanes], k_ref[:, lanes], v_ref[:, lanes], bx_ref[:, lanes], gx_ref[:, lanes]
            gr = gr_ref[0, p]
            glast = gx_ref[pl.ds(CHUNK - 1, 1), lanes]
            s, tm, d_o = ss_ref[0, p], ts_ref[0, p], do_ref[:, lanes]
            ds_out = dstate[p]
            dm = jnp.where(incl, jnp.exp(jnp.minimum(gx - gr, 0.0)), 0.0)
            kb, vb, eg, qs = k * bx, v * bx, jnp.exp(gx), q * SCALE
            kbg, qg = kb * eg, qs * eg
            ed = jnp.exp(glast - gx)
            kd = k * ed
            eglast = jnp.exp(glast)
            yield
            big_g, big_p = _pp_nt(kb, k), _pp_nt(qs, k)
            u, w = _pp(tm, vb), _pp(tm, kbg)
            dqg, kds = _pp_nt(d_o, s), _pp(kd, ds_out)
            yield
            low = jnp.where(strict, big_g * dm, 0.0)
            att = jnp.where(incl, big_p * dm, 0.0)
            ws, atd = _pp(w, s), _pp_tn(att, d_o)
            yield
            vn = u - ws
            dvn = kds + atd
            dkd, datt_raw = _pp_nt(vn, ds_out), _pp_nt(d_o, vn)
            dw_neg, dvb = _pp_nt(dvn, s), _pp_tn(tm, dvn)
            dtm_a, wdv = _pp_nt(dvn, vb), _pp_tn(w, dvn)
            qgd = _pp_tn(qg, d_o)
            yield
            datt = jnp.where(incl, datt_raw, 0.0)
            dw = -dw_neg
            dtm_b, dkbg = _pp_nt(dw, kbg), _pp_tn(tm, dw)
            dbig_p = datt * dm
            dqs_a, dk_p = _pp(dbig_p, k), _pp_tn(dbig_p, qs)
            yield
            inner = _pp_tn(tm, dtm_a + dtm_b)
            yield
            dlow = jnp.where(strict, -_pp_nt(inner, tm), 0.0)
            yield
            dbig_g = dlow * dm
            dkb_a, dk_g = _pp(dbig_g, k), _pp_tn(dbig_g, kb)
            yield
            dkb = dkb_a + dkbg * eg
            dqs = dqs_a + dqg * eg
            dk = dk_g + dk_p + dkd * ed + dkb * bx
            z = dlow * low + datt * att
            kdterm = dkd * kd
            dglast = (jnp.sum(ds_out * s, axis=0, keepdims=True) * eglast
                      + jnp.sum(kdterm, axis=0, keepdims=True))
            dgx = dqg * qg + dkbg * kbg - kdterm
            dgx = dgx + jnp.where(col == 0, _pair_sum(z), 0.0)
            dgx = dgx + jnp.where(row == CHUNK - 1, dglast, 0.0)
            dq_ref[:, lanes] = dqs * SCALE
            dk_ref[:, lanes] = dk
            dv_ref[:, lanes] = dvb * bx
            dbx_ref[:, lanes] = dkb * k + dvb * v
            dgx_ref[:, lanes] = dgx
            dgr_ref[0, p] = -jnp.sum(z, axis=0, keepdims=True)
            dstate[p] = ds_out * eglast + qgd - wdv

        _interleave([chain(p) for p in range(PAIRS)])

    last = nchunk - 1
    blk = lambda j: pl.BlockSpec((CHUNK, WIDTH), lambda n, j=j: (last - n, j))
    sv = pl.BlockSpec((1, PAIRS, CHUNK, LANES), lambda n: (last - n, 0, 0, 0))
    gr_spec = pl.BlockSpec((1, PAIRS, 1, LANES), lambda n: (last - n, 0, 0, 0))
    wide = jax.ShapeDtypeStruct((rows, WIDTH), F32)
    return pl.pallas_call(
        body, name="gdn_bwd", grid=(nchunk,),
        in_specs=[blk(0), blk(1), blk(2), blk(0), blk(0), gr_spec, sv, sv, blk(0)],
        out_specs=[blk(0)] * 5 + [gr_spec],
        out_shape=[wide] * 5 + [jax.ShapeDtypeStruct((nchunk, PAIRS, 1, LANES), F32)],
        scratch_shapes=[pltpu.VMEM((PAIRS, CHUNK, LANES), F32)],
        compiler_params=_params(("arbitrary",)),
    )(qkv, qkv, qkv, betax, gcx, grow, ssave, tsave, do)


ATT_TQ = 256


def _att_scores(qh, kt, fk, diag):
    s = _dot(qh, kt, _CONTRACT["nt"]) - fk
    if diag:
        r = lax.broadcasted_iota(jnp.int32, s.shape, 0)
        c = lax.broadcasted_iota(jnp.int32, s.shape, 1)
        s = jnp.where(r >= c, s, -jnp.inf)
    return s


def _head_masks(n):
    lo = _lane_lo((n, LANES))
    return [lo, jnp.logical_not(lo)]


def _attention_forward(fqk, proj, frow, rows):
    tq = tk = min(ATT_TQ, rows)
    nq = rows // tq
    v_off = 3072 // LANES

    def body(q_ref, k_ref, v_ref, fr_ref, o_ref, lse_ref):
        qi = pl.program_id(1)
        q = q_ref[...] * SCALE
        keep_q, keep_k = _head_masks(tq), _head_masks(tk)
        qh = [jnp.where(keep_q[h], q, 0.0).astype(BF16) for h in range(2)]

        def tile(ki, carry, diag):
            k0 = pl.multiple_of(ki * tk, tk)
            kt = k_ref[pl.ds(k0, tk), :].astype(BF16)
            v_t = v_ref[pl.ds(k0, tk), :]
            out = []
            for h in range(2):
                m, l, acc = carry[h]
                vt = jnp.where(keep_k[h], v_t, 0.0).astype(BF16)
                s = _att_scores(qh[h], kt, fr_ref[0, pl.ds(h, 1), pl.ds(k0, tk)], diag)
                m_new = jnp.maximum(m, jnp.max(s, axis=-1, keepdims=True))
                p = jnp.exp(s - m_new)
                alpha = jnp.exp(m - m_new)
                l = alpha * l + jnp.sum(p, axis=-1, keepdims=True)
                acc = alpha * acc + _dot(p, vt, _CONTRACT["nn"])
                out.append((m_new, l, acc))
            return tuple(out)

        one = (jnp.full((tq, 1), -jnp.inf, F32), jnp.zeros((tq, 1), F32), jnp.zeros((tq, LANES), F32))
        carry = lax.fori_loop(0, qi, lambda ki, c: tile(ki, c, False), (one, one))
        (m0, l0, acc0), (m1, l1, acc1) = tile(qi, carry, True)
        o_ref[...] = acc0 / l0 + acc1 / l1
        lse_ref[...] = jnp.where(keep_q[0], m0 + jnp.log(l0), m1 + jnp.log(l1))

    whole = lambda off: pl.BlockSpec((rows, LANES), lambda p, i, off=off: (0, off + p))
    qblk = lambda off: pl.BlockSpec((tq, LANES), lambda p, i, off=off: (i, off + p))
    wide = jax.ShapeDtypeStruct((rows, WIDTH), F32)
    return pl.pallas_call(
        body, name="fox_fwd", grid=(PAIRS, nq),
        in_specs=[qblk(0), whole(PAIRS), whole(v_off), pl.BlockSpec((1, 2, rows), lambda p, i: (p, 0, 0))],
        out_specs=[qblk(0), qblk(0)], out_shape=[wide, wide],
        compiler_params=_params(("parallel", "arbitrary")),
    )(fqk, fqk, proj, frow)


def _attention_delta(fqk, proj, frow, lse, dao, rows):
    tq = tk = min(ATT_TQ, rows)
    nq = rows // tq
    v_off = 3072 // LANES

    def body(q_ref, k_ref, v_ref, fr_ref, lse_ref, do_ref, delta_ref):
        qi = pl.program_id(1)
        q, d_o, lse_t = q_ref[...] * SCALE, do_ref[...], lse_ref[...]
        keep_q = _head_masks(tq)
        qh = [jnp.where(keep_q[h], q, 0.0).astype(BF16) for h in range(2)]
        doh = [jnp.where(keep_q[h], d_o, 0.0).astype(BF16) for h in range(2)]
        lse_h = [_head_col(lse_t, keep_q[0], h) for h in range(2)]

        def tile(ki, carry, diag):
            k0 = pl.multiple_of(ki * tk, tk)
            kt = k_ref[pl.ds(k0, tk), :].astype(BF16)
            vt = v_ref[pl.ds(k0, tk), :].astype(BF16)
            out = []
            for h in range(2):
                s = _att_scores(qh[h], kt, fr_ref[0, pl.ds(h, 1), pl.ds(k0, tk)], diag)
                dp = _dot(doh[h], vt, _CONTRACT["nt"])
                out.append(carry[h] + jnp.sum(jnp.exp(s - lse_h[h]) * dp, axis=-1, keepdims=True))
            return tuple(out)

        zero = jnp.zeros((tq, 1), F32)
        carry = lax.fori_loop(0, qi, lambda ki, c: tile(ki, c, False), (zero, zero))
        d0, d1 = tile(qi, carry, True)
        delta_ref[...] = jnp.where(keep_q[0], d0, d1)

    whole = lambda off: pl.BlockSpec((rows, LANES), lambda p, i, off=off: (0, off + p))
    qblk = lambda off: pl.BlockSpec((tq, LANES), lambda p, i, off=off: (i, off + p))
    return pl.pallas_call(
        body, name="fox_delta", grid=(PAIRS, nq),
        in_specs=[qblk(0), whole(PAIRS), whole(v_off),
                  pl.BlockSpec((1, 2, rows), lambda p, i: (p, 0, 0)), qblk(0), qblk(0)],
        out_specs=qblk(0), out_shape=jax.ShapeDtypeStruct((rows, WIDTH), F32),
        compiler_params=_params(("parallel", "arbitrary")),
    )(fqk, fqk, proj, frow, lse, dao)


def _attention_backward(fqk, proj, frow, delta, lse, dao, rows):
    tq = tk = min(ATT_TQ, rows)
    nq = rows // tq
    v_off = 3072 // LANES

    def body(q_ref, k_ref, v_ref, fr_ref, delta_ref, lse_ref, do_ref, dq_ref, dk_ref, dv_ref, dfr_ref):
        ki = pl.program_id(1)

        @pl.when(ki == 0)
        def _():
            dq_ref[...] = jnp.zeros_like(dq_ref)

        keep_q, keep_k = _head_masks(tq), _head_masks(tk)
        k_t = k_ref[...]
        kt = k_t.astype(BF16)
        vt = v_ref[...].astype(BF16)
        kh = [jnp.where(keep_k[h], k_t, 0.0).astype(BF16) for h in range(2)]
        fk = [fr_ref[0, pl.ds(h, 1), :] for h in range(2)]

        def tile(qi, carry, diag):
            dk, dv, df0, df1 = carry
            rows_q = pl.ds(pl.multiple_of(qi * tq, tq), tq)
            q, d_o, delta_x, lse_t = q_ref[rows_q, :] * SCALE, do_ref[rows_q, :], delta_ref[rows_q, :], lse_ref[rows_q, :]
            dq = jnp.zeros((tq, LANES), F32)
            dfs = []
            for h in range(2):
                qh = jnp.where(keep_q[h], q, 0.0).astype(BF16)
                doh = jnp.where(keep_q[h], d_o, 0.0).astype(BF16)
                s = _att_scores(qh, kt, fk[h], diag)
                p = jnp.exp(s - _head_col(lse_t, keep_q[0], h))
                dp = _dot(doh, vt, _CONTRACT["nt"])
                ds = p * (dp - _head_col(delta_x, keep_q[0], h))
                dv = dv + _dot(p, doh, _CONTRACT["tn"])
                dk = dk + _dot(ds, qh, _CONTRACT["tn"])
                dq = dq + _dot(ds, kh[h], _CONTRACT["nn"])
                dfs.append(-jnp.sum(ds, axis=0, keepdims=True))
            dq_ref[rows_q, :] += dq * SCALE
            return dk, dv, df0 + dfs[0], df1 + dfs[1]

        zero_kv = jnp.zeros((tk, LANES), F32)
        zero_f = jnp.zeros((1, tk), F32)
        carry = tile(ki, (zero_kv, zero_kv, zero_f, zero_f), True)
        dk, dv, df0, df1 = lax.fori_loop(ki + 1, nq, lambda qi, c: tile(qi, c, False), carry)
        dk_ref[...] = dk
        dv_ref[...] = dv.astype(dv_ref.dtype)
        dfr_ref[0, pl.ds(0, 1), :] = df0
        dfr_ref[0, pl.ds(1, 1), :] = df1

    whole = lambda off: pl.BlockSpec((rows, LANES), lambda p, i, off=off: (0, off + p))
    kblk = lambda off: pl.BlockSpec((tk, LANES), lambda p, i, off=off: (i, off + p))
    fr_spec = pl.BlockSpec((1, 2, tk), lambda p, i: (p, 0, i))
    wide = jax.ShapeDtypeStruct((rows, WIDTH), F32)
    return pl.pallas_call(
        body, name="fox_bwd", grid=(PAIRS, nq),
        in_specs=[whole(0), kblk(PAIRS), kblk(v_off), fr_spec, whole(0), whole(0), whole(0)],
        out_specs=[whole(0), kblk(0), kblk(0), fr_spec],
        out_shape=[wide, wide, jax.ShapeDtypeStruct((rows, WIDTH), BF16),
                   jax.ShapeDtypeStruct((PAIRS, 2, rows), F32)],
        compiler_params=_params(("parallel", "arbitrary")),
    )(fqk, fqk, proj, frow, delta, lse, dao)


def _lane_ids(shape):
    return lax.broadcasted_iota(jnp.int32, shape, len(shape) - 1)


def _gates_elem(a_log, dt_bias, f_bias, pre):
    lane = _lane_ids(pre.shape)
    beta = jax.nn.sigmoid(pre)
    g = -jnp.exp(a_log) * _softplus(pre + dt_bias)
    lf = -_softplus(-(pre + f_bias))
    return jnp.where(lane < 8, beta, jnp.where(lane < 16, g, jnp.where(lane < 24, lf, 0.0)))


def _tri_consts():
    r = np.arange(LANES)[:, None]
    c = np.arange(LANES)[None, :]
    full = (c <= r).astype(np.float32)
    chunked = full * ((r // CHUNK) == (c // CHUNK))
    return jnp.asarray(chunked), jnp.asarray(full)


def _cums_fwd(lc, lf, gates):
    rows = gates.shape[0]
    lane = _lane_ids((LANES, LANES))
    carry = jnp.zeros((1, LANES), F32)
    out = []
    for r in range(rows // LANES):
        blk = gates[r * LANES:(r + 1) * LANES]
        gc = _dot32(lc, blk, _CONTRACT["nn"])
        f = _dot32(lf, blk, _CONTRACT["nn"]) + carry
        carry = carry + jnp.sum(blk, axis=0, keepdims=True)
        out.append(jnp.where((lane >= 8) & (lane < 16), gc, jnp.where((lane >= 16) & (lane < 24), f, 0.0)))
    return jnp.concatenate(out, axis=0)


def _cums_bwd(lc, lf, dcums):
    rows = dcums.shape[0]
    lane = _lane_ids((LANES, LANES))
    is_g = (lane >= 8) & (lane < 16)
    is_f = (lane >= 16) & (lane < 24)
    carry = jnp.zeros((1, LANES), F32)
    out = [None] * (rows // LANES)
    for r in reversed(range(rows // LANES)):
        blk = dcums[r * LANES:(r + 1) * LANES]
        dg = jnp.where(is_g, blk, 0.0)
        df = jnp.where(is_f, blk, 0.0)
        out[r] = _dot32(lc, dg, _CONTRACT["tn"]) + _dot32(lf, df, _CONTRACT["tn"]) + carry
        carry = carry + jnp.sum(df, axis=0, keepdims=True)
    return jnp.concatenate(out, axis=0)


def _expand_consts():
    xb = np.zeros((LANES, WIDTH), np.float32)
    xg = np.zeros((LANES, WIDTH), np.float32)
    for h in range(HEADS):
        xb[h, h * HEAD_DIM:(h + 1) * HEAD_DIM] = 1.0
        xg[8 + h, h * HEAD_DIM:(h + 1) * HEAD_DIM] = 1.0
    return jnp.asarray(xb), jnp.asarray(xg)


def _shift_down(x, s):
    if s == 0:
        return x
    row = lax.broadcasted_iota(jnp.int32, x.shape, 0)
    return jnp.where(row >= s, pltpu.roll(x, s, 0), 0.0)


def _shift_up(x, s):
    if s == 0:
        return x
    n = x.shape[0]
    row = lax.broadcasted_iota(jnp.int32, x.shape, 0)
    return jnp.where(row < n - s, pltpu.roll(x, n - s, 0), 0.0)


def _row_of(cw, i):
    row = lax.broadcasted_iota(jnp.int32, cw.shape, 0)
    return jnp.sum(jnp.where(row == i, cw, 0.0), axis=0, keepdims=True)


def _conv(cw, x):
    c = jnp.zeros_like(x)
    for i in range(CONV_K):
        c = c + _row_of(cw, i) * _shift_down(x, CONV_K - 1 - i)
    return c


def _post_conv(is_qk, c):
    s = _silu(c)
    n = s * lax.rsqrt(_pair_sum(s * s) + EPS)
    return jnp.where(is_qk, n, s)


def _gdn_prep_fwd(col, cw, x):
    return (_post_conv(col < 2 * PAIRS, _conv(cw, x)),)


def _gdn_prep_bwd(is_qk, cw, x, dy):
    c = _conv(cw, x)
    _, vjp = jax.vjp(lambda cc: _post_conv(is_qk, cc), c)
    (dc,) = vjp(dy)
    dx = jnp.zeros_like(x)
    row = lax.broadcasted_iota(jnp.int32, cw.shape, 0)
    dcw = jnp.zeros(cw.shape, F32)
    for i in range(CONV_K):
        s = CONV_K - 1 - i
        dx = dx + _row_of(cw, i) * _shift_up(dc, s)
        dcw = dcw + jnp.where(row == i, jnp.sum(dc * _shift_down(x, s), axis=0, keepdims=True), 0.0)
    return dx, dcw


def _head_rms(w, x):
    return x * lax.rsqrt(_pair_sum(x * x) / HEAD_DIM + EPS) * w


def _cat_weights(w_in):
    pad = jnp.zeros((w_in.shape[0], D_CAT - D_IN), w_in.dtype)
    return jnp.concatenate([w_in[:, :2048], w_in[:, 2064:4112], w_in[:, 2048:2064], w_in[:, 4112:4120], pad], axis=1)


def _uncat_grad(g):
    return jnp.concatenate([g[:, :2048], g[:, 4096:4112], g[:, 2048:4096], g[:, 4112:4120]], axis=1)


def _lanes_to_rowform(v8, rows):
    return v8.reshape(rows // CHUNK, CHUNK, HEADS).transpose(0, 2, 1).reshape(rows // CHUNK, PAIRS, 1, LANES)


def _rowform_to_lanes(v, rows):
    return v.reshape(rows // CHUNK, HEADS, CHUNK).transpose(0, 2, 1).reshape(rows, HEADS)


def _local_step(x, target, norm1_w, w_cat, conv_w, a_log, dt_bias, out_norm_w, f_bias, q_norm_w, k_norm_w,
                w_out, norm2_w, w_gate, w_up, w_down, final_w):
    rows = x.shape[0]
    tm = min(256, rows)
    lc, lf = _tri_consts()
    xb, xg = _expand_consts()

    (h1,) = _tiles(lambda col, w, xx: (_rms(xx, w),), name="norm1", rows=rows, tm=tm,
                   full_consts=[norm1_w], row_ins=[(x, D_MODEL, 0)], row_outs=[(D_MODEL, BF16)])
    proj = _mm(h1, w_cat, dims="nn", name="in_proj", tn=384, tk=1024)

    lane_pad = lambda v, off: jnp.pad(v.reshape(1, -1), ((0, 0), (off, LANES - off - v.size)))
    p_a, p_dt, p_fb = lane_pad(a_log, 8), lane_pad(dt_bias, 8), lane_pad(f_bias, 16)

    def gates_fwd(col, lcv, lfv, a, dt, fb, pre):
        gates = _gates_elem(a, dt, fb, pre)
        return gates, _cums_fwd(lcv, lfv, gates)

    gates, cums = _tiles(gates_fwd, name="gates", rows=rows, tm=rows,
                         full_consts=[lc, lf, p_a, p_dt, p_fb], row_ins=[(proj, LANES, COL_SMALL)],
                         row_outs=[(LANES, F32), (LANES, F32)])

    def expand_fwd(col, b, g, gt, cm):
        return (_dot32(gt, b, _CONTRACT["nn"]), _dot32(cm, g, _CONTRACT["nn"]))

    betax, gcx = _tiles(expand_fwd, name="expand", rows=rows, tm=tm, full_consts=[xb, xg],
                        row_ins=[(gates, LANES, 0), (cums, LANES, 0)],
                        row_outs=[(WIDTH, F32)] * 2)
    grow = _lanes_to_rowform(cums[:, 8:16], rows)
    frow = cums[:, 16:24].T.reshape(PAIRS, 2, rows)

    (qkv,) = _tiles(_gdn_prep_fwd, name="gdn_prep", rows=rows, tm=rows, ncol=3 * PAIRS,
                    col_consts=[(conv_w, CONV_K, LANES, 0)], row_ins=[(proj, LANES, 0)],
                    row_outs=[(LANES, F32)])
    o_gdn, ssave, tsave = _gdn_forward(qkv, betax, gcx, grow, rows)

    w_qk = jnp.concatenate([jnp.tile(q_norm_w.reshape(1, -1), (1, HEADS)),
                            jnp.tile(k_norm_w.reshape(1, -1), (1, HEADS))], axis=1)
    fox_off = 2048 // LANES
    (fqk,) = _tiles(lambda col, w, xx: (_head_rms(w, xx),), name="fox_prep", rows=rows, tm=rows, ncol=2 * PAIRS,
                    col_consts=[(w_qk, 1, LANES, 0)], row_ins=[(proj, LANES, fox_off)],
                    row_outs=[(LANES, F32)])
    ao, lse = _attention_forward(fqk, proj, frow, rows)

    w_on = jnp.tile(out_norm_w.reshape(1, -1), (1, 2))
    z_off, fg_off = 1536 // LANES, 3584 // LANES
    mix_g_fn = lambda w, o, z: _head_rms(w, o) * _silu(z)
    mix_f_fn = lambda a, g: a * jax.nn.sigmoid(g)
    (mix_g,) = _tiles(lambda col, w, o, z: (mix_g_fn(w, o, z),), name="mix_gdn", rows=rows, tm=rows, ncol=PAIRS,
                      full_consts=[w_on], row_ins=[(o_gdn, LANES, 0), (proj, LANES, z_off)],
                      row_outs=[(LANES, BF16)])
    (mix_f,) = _tiles(lambda col, a, g: (mix_f_fn(a, g),), name="mix_fox", rows=rows, tm=rows, ncol=PAIRS,
                      row_ins=[(ao, LANES, 0), (proj, LANES, fg_off)], row_outs=[(LANES, BF16)])
    mix = jnp.concatenate([mix_g, mix_f], axis=1)
    x1 = _mm(mix, w_out, dims="nn", name="out_proj", add=x, tk=1024)

    (h2,) = _tiles(lambda col, w, xx: (_rms(xx, w),), name="norm2", rows=rows, tm=tm,
                   full_consts=[norm2_w], row_ins=[(x1, D_MODEL, 0)], row_outs=[(D_MODEL, BF16)])
    t_rows, t_cols, t_act = min(1024, rows), 512, min(512, rows)
    n_rt = rows // t_rows
    st_act = jax.ShapeDtypeStruct((N_CHIPS, rows, FF_SHARD), F32)
    st_rows = pl.BlockSpec((None, t_rows, FF_SHARD), lambda i, j: (j, i, 0))
    out_rows = pl.BlockSpec((t_rows, t_cols), lambda i, n: (i, n))
    flat = lambda t: t.reshape(N_CHIPS * rows, FF_SHARD)

    def ffn_in(w_st, name):
        return _mm_blocks(h2, w_st, name=name, grid=(n_rt, N_CHIPS), dims="nn",
                          a_spec=pl.BlockSpec((t_rows, D_MODEL), lambda i, j: (i, 0)),
                          b_spec=pl.BlockSpec((None, D_MODEL, FF_SHARD), lambda i, j: (j, 0, 0)),
                          o_spec=st_rows, out_shape=st_act)

    gate, up = ffn_in(w_gate, "ffn_gate"), ffn_in(w_up, "ffn_up")
    act_fn = lambda g, u: _silu(g) * u
    (act,) = _tiles(lambda col, g, u: (act_fn(g, u),), name="ffn_act", rows=N_CHIPS * rows, tm=t_act,
                    row_ins=[(flat(gate), FF_SHARD, 0), (flat(up), FF_SHARD, 0)], row_outs=[(FF_SHARD, BF16)])
    act = act.reshape(st_act.shape)
    x2 = _mm_blocks(act, w_down, name="ffn_down", grid=(n_rt, D_MODEL // t_cols), dims="nn", n_sum=N_CHIPS,
                    a_spec=pl.BlockSpec((N_CHIPS, t_rows, FF_SHARD), lambda i, n: (0, i, 0)),
                    b_spec=pl.BlockSpec((N_CHIPS, FF_SHARD, t_cols), lambda i, n: (0, 0, n)),
                    o_spec=out_rows, out_shape=jax.ShapeDtypeStruct((rows, D_MODEL), F32),
                    add=x1, add_spec=out_rows)

    def final_fn(col, w, xx, tgt):
        y, vjp = jax.vjp(_rms, xx, w)
        err = y - tgt
        loss = 0.5 * jnp.sum(err * err) / D_MODEL
        dx, dw = vjp(err / D_MODEL)
        return dx, dx, jnp.full((1, LANES), loss, F32), dw

    dx2, dx2_b, loss, d_final_w = _tiles(final_fn, name="final_loss", rows=rows, tm=tm, full_consts=[final_w],
                                         row_ins=[(x2, D_MODEL, 0), (target, D_MODEL, 0)],
                                         row_outs=[(D_MODEL, F32), (D_MODEL, BF16)],
                                         acc_outs=[(1, LANES), (1, D_MODEL)])

    dact = _mm_blocks(dx2_b, w_down, name="d_act", grid=(n_rt, N_CHIPS), dims="nt",
                      a_spec=pl.BlockSpec((t_rows, D_MODEL), lambda i, j: (i, 0)),
                      b_spec=pl.BlockSpec((None, FF_SHARD, D_MODEL), lambda i, j: (j, 0, 0)),
                      o_spec=st_rows, out_shape=st_act)
    g_down = _mm_blocks(act, dx2_b, name="g_down", grid=(N_CHIPS, D_MODEL // t_cols), dims="tn",
                        a_spec=pl.BlockSpec((None, rows, FF_SHARD), lambda j, n: (j, 0, 0)),
                        b_spec=pl.BlockSpec((rows, t_cols), lambda j, n: (0, n)),
                        o_spec=pl.BlockSpec((None, FF_SHARD, t_cols), lambda j, n: (j, 0, n)),
                        out_shape=jax.ShapeDtypeStruct((N_CHIPS, FF_SHARD, D_MODEL), F32))

    def act_bwd(col, g, u, d):
        _, vjp = jax.vjp(act_fn, g, u)
        return vjp(d)

    dgate, dup = _tiles(act_bwd, name="ffn_act_bwd", rows=N_CHIPS * rows, tm=t_act,
                        row_ins=[(flat(gate), FF_SHARD, 0), (flat(up), FF_SHARD, 0), (flat(dact), FF_SHARD, 0)],
                        row_outs=[(FF_SHARD, BF16), (FF_SHARD, BF16)])
    dgate, dup = dgate.reshape(st_act.shape), dup.reshape(st_act.shape)

    def d_h2(d_st, w_st, name, add):
        return _mm_blocks(d_st, w_st, name=name, grid=(n_rt, D_MODEL // t_cols), dims="nt", n_sum=N_CHIPS,
                          a_spec=pl.BlockSpec((N_CHIPS, t_rows, FF_SHARD), lambda i, n: (0, i, 0)),
                          b_spec=pl.BlockSpec((N_CHIPS, t_cols, FF_SHARD), lambda i, n: (0, n, 0)),
                          o_spec=out_rows, out_shape=jax.ShapeDtypeStruct((rows, D_MODEL), F32),
                          add=add, add_spec=out_rows)

    dh2 = d_h2(dup, w_up, "d_h2_up", d_h2(dgate, w_gate, "d_h2_gate", None))

    def g_ffn_in(d_st, name):
        return _mm_blocks(h2, d_st, name=name, grid=(N_CHIPS,), dims="tn",
                          a_spec=pl.BlockSpec((rows, D_MODEL), lambda j: (0, 0)),
                          b_spec=pl.BlockSpec((None, rows, FF_SHARD), lambda j: (j, 0, 0)),
                          o_spec=pl.BlockSpec((None, D_MODEL, FF_SHARD), lambda j: (j, 0, 0)),
                          out_shape=jax.ShapeDtypeStruct((N_CHIPS, D_MODEL, FF_SHARD), F32))

    g_gate, g_up = g_ffn_in(dgate, "g_gate"), g_ffn_in(dup, "g_up")

    def norm_bwd(col, w, xx, dh, dres):
        _, vjp = jax.vjp(_rms, xx, w)
        dx, dw = vjp(dh)
        return dx + dres, dx + dres, dw

    dx1, dx1_b, d_norm2_w = _tiles(norm_bwd, name="norm2_bwd", rows=rows, tm=tm, full_consts=[norm2_w],
                                   row_ins=[(x1, D_MODEL, 0), (dh2, D_MODEL, 0), (dx2, D_MODEL, 0)],
                                   row_outs=[(D_MODEL, F32), (D_MODEL, BF16)], acc_outs=[(1, D_MODEL)])
    dmix = _mm(dx1_b, w_out, dims="nt", name="d_mix", tk=1024)
    g_out = _mm(mix, dx1_b, dims="tn", name="g_out", tk=rows)

    def mix_g_bwd(col, w, o, z, d):
        _, vjp = jax.vjp(mix_g_fn, w, o, z)
        dw, do_, dz = vjp(d)
        return do_, dz, dw

    do_gdn, dz, d_on = _tiles(mix_g_bwd, name="mix_gdn_bwd", rows=rows, tm=rows, ncol=PAIRS, full_consts=[w_on],
                              row_ins=[(o_gdn, LANES, 0), (proj, LANES, z_off), (dmix, LANES, 0)],
                              row_outs=[(LANES, F32), (LANES, BF16)], acc_outs=[(1, LANES)])

    def mix_f_bwd(col, a, g, d):
        _, vjp = jax.vjp(mix_f_fn, a, g)
        return vjp(d)

    dao, dfgate = _tiles(mix_f_bwd, name="mix_fox_bwd", rows=rows, tm=rows, ncol=PAIRS,
                         row_ins=[(ao, LANES, 0), (proj, LANES, fg_off), (dmix, LANES, PAIRS)],
                         row_outs=[(LANES, F32), (LANES, BF16)])

    delta = _attention_delta(fqk, proj, frow, lse, dao, rows)
    dfq, dfk, dfv, dfrow = _attention_backward(fqk, proj, frow, delta, lse, dao, rows)

    def fox_prep_bwd(col, w, xx, d):
        _, vjp = jax.vjp(_head_rms, w, xx)
        dw, dx = vjp(d)
        return dx, dw

    dfqk, d_wqk = [], []
    for part, d_n in enumerate((dfq, dfk)):
        dx_p, dw_p = _tiles(fox_prep_bwd, name="fox_prep_bwd_" + "qk"[part], rows=rows, tm=rows, ncol=PAIRS,
                            col_consts=[(w_qk, 1, LANES, part * PAIRS)],
                            row_ins=[(proj, LANES, fox_off + part * PAIRS), (d_n, LANES, 0)],
                            row_outs=[(LANES, BF16)], acc_outs=[(1, LANES)])
        dfqk.append(dx_p)
        d_wqk.append(dw_p)

    dq, dk, dv, dbetax, dgcx, dgrow = _gdn_backward(qkv, betax, gcx, grow, ssave, tsave, do_gdn, rows)
    dqkv, d_conv = [], []
    for part, d_n in enumerate((dq, dk, dv)):
        prep_bwd = lambda col, cw, xx, dy, is_qk=(part < 2): _gdn_prep_bwd(is_qk, cw, xx, dy)
        dx_p, dw_p = _tiles(prep_bwd, name="gdn_prep_bwd_" + "qkv"[part], rows=rows, tm=rows, ncol=PAIRS,
                            col_consts=[(conv_w, CONV_K, LANES, part * PAIRS)],
                            row_ins=[(proj, LANES, part * PAIRS), (d_n, LANES, 0)],
                            row_outs=[(LANES, BF16)], acc_outs=[(CONV_K, LANES)])
        dqkv.append(dx_p)
        d_conv.append(dw_p)
    d_conv = jnp.concatenate(d_conv, axis=1)

    def expand_bwd(col, b, g, db, dg):
        return (_dot32(db, b, _CONTRACT["nt"]), _dot32(dg, g, _CONTRACT["nt"]))

    dgates_b, dcums_g = _tiles(expand_bwd, name="expand_bwd", rows=rows, tm=tm, full_consts=[xb, xg],
                               row_ins=[(dbetax, WIDTH, 0), (dgcx, WIDTH, 0)],
                               row_outs=[(LANES, F32), (LANES, F32)])
    dcums_row = jnp.concatenate([jnp.zeros((rows, 8), F32), _rowform_to_lanes(dgrow, rows),
                                 dfrow.reshape(HEADS, rows).T, jnp.zeros((rows, LANES - 24), F32)], axis=1)

    def gates_bwd(col, lcv, lfv, a, dt, fb, pre, dgb, dcg, dcr):
        lane = _lane_ids(pre.shape)
        dgates = jnp.where(lane < 8, dgb, _cums_bwd(lcv, lfv, dcg + dcr))
        _, vjp = jax.vjp(_gates_elem, a, dt, fb, pre)
        da, ddt, dfb, dpre = vjp(dgates)
        return dpre, da, ddt, dfb

    dpre, d_a, d_dt, d_fb = _tiles(gates_bwd, name="gates_bwd", rows=rows, tm=rows,
                                   full_consts=[lc, lf, p_a, p_dt, p_fb],
                                   row_ins=[(proj, LANES, COL_SMALL), (dgates_b, LANES, 0), (dcums_g, LANES, 0),
                                            (dcums_row, LANES, 0)],
                                   row_outs=[(LANES, BF16)], acc_outs=[(1, LANES)] * 3)

    dproj = jnp.concatenate(dqkv + [dz] + dfqk + [dfv, dfgate, dpre], axis=1)
    dh1 = _mm(dproj, w_cat, dims="nt", name="d_h1", tk=D_CAT)
    g_cat = _mm(h1, dproj, dims="tn", name="g_in", tn=384, tk=rows)

    def norm1_bwd(col, w, xx, dh, dres):
        _, vjp = jax.vjp(_rms, xx, w)
        dx, dw = vjp(dh)
        return dx + dres, dw

    grad_x, d_norm1_w = _tiles(norm1_bwd, name="norm1_bwd", rows=rows, tm=tm, full_consts=[norm1_w],
                               row_ins=[(x, D_MODEL, 0), (dh1, D_MODEL, 0), (dx1, D_MODEL, 0)],
                               row_outs=[(D_MODEL, F32)], acc_outs=[(1, D_MODEL)])

    fold = lambda v: v.reshape(-1, HEAD_DIM).sum(axis=0)
    small = dict(
        loss=loss[0, 0],
        norm1_w=d_norm1_w, conv_w=d_conv, a_log=d_a[0, 8:16], dt_bias=d_dt[0, 8:16],
        out_norm_w=fold(d_on), f_bias=d_fb[0, 16:24], q_norm_w=fold(d_wqk[0]),
        k_norm_w=fold(d_wqk[1]), norm2_w=d_norm2_w, final_w=d_final_w)
    return grad_x, g_cat, g_out, g_gate, g_up, g_down, small


HBM_SPEC = pl.BlockSpec(memory_space=pltpu.HBM)


def _place():
    x, y, c = lax.axis_index("x"), lax.axis_index("y"), lax.axis_index("c")
    chips = [(1 - x, y), (x, 1 - y), (1 - x, 1 - y)]
    return x, y, c, 2 * x + y, (x, y, 1 - c), chips, [2 * cx + cy for cx, cy in chips]


def _remote(src, dst, send_sem, recv_sem, to):
    return pltpu.make_async_remote_copy(src_ref=src, dst_ref=dst, send_sem=send_sem, recv_sem=recv_sem,
                                        device_id=to, device_id_type=MESH)


def _allgather_weights(shards, conv):
    n = len(shards)
    halves = [s.shape[0] // 2 for s in shards]
    per = 6
    own_base = n * per + 3

    def body(*refs):
        ins, conv_in = refs[:n], refs[n]
        outs, conv_out = refs[n + 1:2 * n + 1], refs[2 * n + 1]
        send_sems, recv_sems = refs[2 * n + 2:]
        x, y, c, own, sib, chips, chip_idx = _place()

        def half(i, ref, hc):
            return ref.at[pl.ds(pl.multiple_of(hc * halves[i], 16), halves[i]), :]

        sent = []
        for i, (src, dst) in enumerate(zip(list(ins) + [conv_in], list(outs) + [conv_out])):
            k = own_base + i
            sent.append(_remote(src, dst.at[own], send_sems.at[k], recv_sems.at[k], sib))
        for i in range(n):
            for j, chip in enumerate(chips):
                k = i * per + j
                sent.append(_remote(half(i, ins[i], c), half(i, outs[i].at[own], c),
                                    send_sems.at[k], recv_sems.at[k], (*chip, c)))
        for j, chip in enumerate(chips):
            k = n * per + j
            sent.append(_remote(conv_in, conv_out.at[own], send_sems.at[k], recv_sems.at[k], (*chip, c)))
        for cp in sent:
            cp.start()
        for i in range(n):
            for j in range(len(chips)):
                k = i * per + j
                landed = half(i, outs[i].at[chip_idx[j]], c)
                _remote(landed, landed, send_sems.at[k], recv_sems.at[k], sib).wait_recv()
                fwd = _remote(landed, landed, send_sems.at[k + 3], recv_sems.at[k + 3], sib)
                fwd.start()
                sent.append(fwd)
        for i in range(n):
            for j in range(len(chips)):
                k = i * per + 3 + j
                landed = half(i, outs[i].at[chip_idx[j]], 1 - c)
                _remote(landed, landed, send_sems.at[k], recv_sems.at[k], sib).wait_recv()
        for j in range(len(chips)):
            k = n * per + j
            landed = conv_out.at[chip_idx[j]]
            _remote(landed, landed, send_sems.at[k], recv_sems.at[k], sib).wait_recv()
        for i, dst in enumerate(list(outs) + [conv_out]):
            k = own_base + i
            landed = dst.at[own]
            _remote(landed, landed, send_sems.at[k], recv_sems.at[k], sib).wait_recv()
        for cp in sent:
            cp.wait_send()

    n_sem = own_base + n + 1
    out_shape = [jax.ShapeDtypeStruct((N_CHIPS,) + s.shape, s.dtype) for s in shards]
    out_shape.append(jax.ShapeDtypeStruct((N_CHIPS,) + conv.shape, conv.dtype))
    res = pl.pallas_call(
        body, name="allgather_weights", out_shape=out_shape,
        in_specs=[HBM_SPEC] * (n + 1), out_specs=[HBM_SPEC] * (n + 1),
        scratch_shapes=[pltpu.SemaphoreType.DMA((n_sem,)), pltpu.SemaphoreType.DMA((n_sem,))],
    )(*shards, conv)
    return res[:n], res[n]


def _swap_halves(stacks):
    n = len(stacks)

    def body(*refs):
        ins, outs = refs[:n], refs[n:2 * n]
        send_sems, recv_sems = refs[2 * n:]
        x, y, c, own, sib, chips, chip_idx = _place()
        cps = []
        for i in range(n):
            h = stacks[i].shape[1] // 2
            src = ins[i].at[:, pl.ds(pl.multiple_of((1 - c) * h, 8), h), :]
            cps.append(_remote(src, outs[i], send_sems.at[i], recv_sems.at[i], sib))
        for cp in cps:
            cp.start()
        for cp in cps:
            cp.wait()

    out_shape = [jax.ShapeDtypeStruct((N_CHIPS, s.shape[1] // 2, s.shape[2]), s.dtype) for s in stacks]
    return pl.pallas_call(
        body, name="rs_swap_halves", out_shape=out_shape,
        in_specs=[HBM_SPEC] * n, out_specs=[HBM_SPEC] * n,
        scratch_shapes=[pltpu.SemaphoreType.DMA((n,)), pltpu.SemaphoreType.DMA((n,))],
    )(*stacks)


def _add_half(stack, landed, place, name):
    _, h, cols = landed.shape

    def body(place_ref, a_ref, b_ref, o_ref, own_ref):
        part = (a_ref[...] + b_ref[...]).astype(o_ref.dtype)
        o_ref[...] = part

        @pl.when(pl.program_id(0) == place_ref[1])
        def _():
            own_ref[...] = part[0]

    return pl.pallas_call(
        body, name=name,
        out_shape=[jax.ShapeDtypeStruct(landed.shape, BF16), jax.ShapeDtypeStruct((h, cols), BF16)],
        grid_spec=pltpu.PrefetchScalarGridSpec(
            num_scalar_prefetch=1, grid=(N_CHIPS,),
            in_specs=[pl.BlockSpec((1, h, cols), lambda j, p: (j, p[0], 0)),
                      pl.BlockSpec((1, h, cols), lambda j, p: (j, 0, 0))],
            out_specs=[pl.BlockSpec((1, h, cols), lambda j, p: (j, 0, 0)),
                       pl.BlockSpec((h, cols), lambda j, p: (0, 0))]),
        compiler_params=_params(("arbitrary",)),
    )(place, stack, landed)


def _exchange_partials(parts):
    n = len(parts)

    def body(*refs):
        ins, outs = refs[:n], refs[n:2 * n]
        send_sems, recv_sems = refs[2 * n:]
        x, y, c, own, sib, chips, chip_idx = _place()
        sent = []
        for i in range(n):
            for j, chip in enumerate(chips):
                k = i * 3 + j
                sent.append(_remote(ins[i].at[chip_idx[j]], outs[i].at[j], send_sems.at[k], recv_sems.at[k],
                                    (*chip, c)))
        for cp in sent:
            cp.start()
        for i in range(n):
            for j in range(len(chips)):
                k = i * 3 + j
                landed = outs[i].at[j]
                _remote(landed, landed, send_sems.at[k], recv_sems.at[k], sib).wait_recv()
        for cp in sent:
            cp.wait_send()

    return pl.pallas_call(
        body, name="rs_exchange_partials",
        out_shape=[jax.ShapeDtypeStruct((3,) + p.shape[1:], p.dtype) for p in parts],
        in_specs=[HBM_SPEC] * n, out_specs=[HBM_SPEC] * n,
        scratch_shapes=[pltpu.SemaphoreType.DMA((3 * n,)), pltpu.SemaphoreType.DMA((3 * n,))],
    )(*parts)


def _sum_partials(own_part, landed, name):
    _, h, cols = landed.shape

    def body(own_ref, a_ref, o_ref):
        acc = own_ref[...].astype(F32)
        for s in range(3):
            acc = acc + a_ref[s].astype(F32)
        o_ref[...] = acc

    return pl.pallas_call(
        body, name=name, out_shape=jax.ShapeDtypeStruct((h, cols), F32), grid=(1,),
        in_specs=[pl.BlockSpec((h, cols), lambda i: (0, 0)), pl.BlockSpec(landed.shape, lambda i: (0, 0, 0))],
        out_specs=pl.BlockSpec((h, cols), lambda i: (0, 0)),
        compiler_params=_params(("arbitrary",)),
    )(own_part, landed)


def _share_halves(halves):
    n = len(halves)

    def body(*refs):
        ins, outs = refs[:n], refs[n:2 * n]
        send_sems, recv_sems = refs[2 * n:]
        x, y, c, own, sib, chips, chip_idx = _place()
        cps = [_remote(ins[i], outs[i], send_sems.at[i], recv_sems.at[i], sib) for i in range(n)]
        for cp in cps:
            cp.start()
        for cp in cps:
            cp.wait()

    return pl.pallas_call(
        body, name="rs_share_halves",
        out_shape=[jax.ShapeDtypeStruct(p.shape, p.dtype) for p in halves],
        in_specs=[HBM_SPEC] * n, out_specs=[HBM_SPEC] * n,
        scratch_shapes=[pltpu.SemaphoreType.DMA((n,)), pltpu.SemaphoreType.DMA((n,))],
    )(*halves)


def _allreduce_small(packed):
    rows = packed.shape[0]
    n_dev = 8

    def body(in_ref, out_ref, gath, send_sems, recv_sems):
        x, y, c = lax.axis_index("x"), lax.axis_index("y"), lax.axis_index("c")
        me = 4 * x + 2 * y + c
        gath[me] = in_ref[...]
        cps = []
        for k in range(1, n_dev):
            fx, fy, fc = (k >> 2) & 1, (k >> 1) & 1, k & 1
            to = (x ^ fx, y ^ fy, c ^ fc)
            cps.append(_remote(in_ref, gath.at[me], send_sems.at[k - 1], recv_sems.at[k - 1], to))
        for cp in cps:
            cp.start()
        for k in range(1, n_dev):
            fx, fy, fc = (k >> 2) & 1, (k >> 1) & 1, k & 1
            src = 4 * (x ^ fx) + 2 * (y ^ fy) + (c ^ fc)
            slot = gath.at[src]
            _remote(slot, slot, send_sems.at[k - 1], recv_sems.at[k - 1], (x, y, c)).wait_recv()
        for cp in cps:
            cp.wait_send()
        acc = gath[0]
        for d in range(1, n_dev):
            acc = acc + gath[d]
        out_ref[...] = acc

    vm = pl.BlockSpec(memory_space=pltpu.VMEM)
    return pl.pallas_call(
        body, name="allreduce_small", out_shape=jax.ShapeDtypeStruct(packed.shape, F32),
        in_specs=[vm], out_specs=vm,
        scratch_shapes=[pltpu.VMEM((n_dev, rows, LANES), F32),
                        pltpu.SemaphoreType.DMA((n_dev - 1,)), pltpu.SemaphoreType.DMA((n_dev - 1,))],
    )(packed)


def _adam(col, w, g, m, v):
    m2 = ADAM_B1 * m + (1.0 - ADAM_B1) * g
    v2 = ADAM_B2 * v + (1.0 - ADAM_B2) * (g * g)
    m_hat = m2 / (1.0 - ADAM_B1 ** ADAM_STEP)
    v_hat = v2 / (1.0 - ADAM_B2 ** ADAM_STEP)
    delta = -ADAM_LR * (m_hat / (jnp.sqrt(v_hat) + ADAM_EPS) + ADAM_WD * w)
    return delta, m2, v2


def _adam_call(w, g, m, v, name):
    rows, cols = w.shape
    tm = rows
    for cand in (256, 352, 176, 128, 64, 48, 16, 8):
        if rows % cand == 0:
            tm = cand
            break
    return _tiles(_adam, name=name, rows=rows, tm=tm,
                  row_ins=[(w, cols, 0), (g, cols, 0), (m, cols, 0), (v, cols, 0)],
                  row_outs=[(cols, F32)] * 3)


def _adam_big(w, g_mine, g_other, m, v, place, name):
    _, rows, cols = w.shape
    h = rows // 2
    tm = next(t for t in (256, 176, 128) if h % t == 0)
    nt = h // tm

    def body(place_ref, w_ref, gm_ref, go_ref, m_ref, v_ref, g_out, d_out, m_out, v_out):
        g = jnp.where(pl.program_id(0) == place_ref[0], gm_ref[...], go_ref[...])
        d, m2, v2 = _adam(None, w_ref[...], g, m_ref[...], v_ref[...])
        g_out[...] = g
        d_out[...] = d
        m_out[...] = m2
        v_out[...] = v2

    full = pl.BlockSpec((None, tm, cols), lambda hh, i, p: (0, hh * nt + i, 0))
    half = pl.BlockSpec((tm, cols), lambda hh, i, p: (i, 0))
    return pl.pallas_call(
        body, name=name, out_shape=[jax.ShapeDtypeStruct(w.shape, F32)] * 4,
        grid_spec=pltpu.PrefetchScalarGridSpec(
            num_scalar_prefetch=1, grid=(2, nt),
            in_specs=[full, half, half, full, full], out_specs=[full] * 4),
        compiler_params=_params(("arbitrary", "arbitrary")),
    )(place, w, g_mine, g_other, m, v)


def _pack(arrays):
    flat = []
    for a in arrays:
        a = a.reshape(-1).astype(F32)
        flat.append(jnp.pad(a, (0, (-a.size) % LANES)))
    out = jnp.concatenate(flat)
    out = jnp.pad(out, (0, (-out.size) % (8 * LANES)))
    return out.reshape(-1, LANES)


def _unpack(packed, shapes):
    flat = packed.reshape(-1)
    out, off = [], 0
    for s in shapes:
        size = int(np.prod(s))
        out.append(flat[off:off + size].reshape(s))
        off += size + (-size) % LANES
    return out


def kernel(x, norm1_w, w_in, gdn_conv_w, gdn_A_log, gdn_dt_bias, gdn_out_norm_w, fox_f_bias, fox_q_norm_w, fox_k_norm_w, w_out, norm2_w, w_ffn_gate, w_ffn_up, w_ffn_down, final_norm_w, loss_target, m_norm1_w, m_w_in, m_gdn_conv_w, m_gdn_A_log, m_gdn_dt_bias, m_gdn_out_norm_w, m_fox_f_bias, m_fox_q_norm_w, m_fox_k_norm_w, m_w_out, m_norm2_w, m_w_ffn_gate, m_w_ffn_up, m_w_ffn_down, m_final_norm_w, v_norm1_w, v_w_in, v_gdn_conv_w, v_gdn_A_log, v_gdn_dt_bias, v_gdn_out_norm_w, v_fox_f_bias, v_fox_q_norm_w, v_fox_k_norm_w, v_w_out, v_norm2_w, v_w_ffn_gate, v_w_ffn_up, v_w_ffn_down, v_final_norm_w):
    cx, cy, cc = lax.axis_index("x"), lax.axis_index("y"), lax.axis_index("c")
    own = 2 * cx + cy
    place = jnp.stack([cc, own]).astype(jnp.int32)

    big_w = [w_in, w_out, w_ffn_gate, w_ffn_up, w_ffn_down]
    gathered, conv_g = _allgather_weights([w[0].astype(BF16) for w in big_w], gdn_conv_w[0])
    by_cols = lambda g: g.transpose(1, 0, 2).reshape(g.shape[1], N_CHIPS * g.shape[2])
    by_rows = lambda g: g.reshape(N_CHIPS * g.shape[1], g.shape[2])
    w_cat = _cat_weights(by_cols(gathered[0]))
    conv_full = by_cols(conv_g)

    grad_x, g_cat, g_out, g_gate, g_up, g_down, small = _local_step(
        x[0], loss_target[0], norm1_w, w_cat, conv_full, gdn_A_log[0], gdn_dt_bias[0], gdn_out_norm_w[0],
        fox_f_bias[0], fox_q_norm_w[0], fox_k_norm_w[0], by_rows(gathered[1]), norm2_w,
        gathered[2], gathered[3], gathered[4], final_norm_w.reshape(1, -1))

    col_stack = lambda g: g.reshape(g.shape[0], N_CHIPS, g.shape[1] // N_CHIPS).transpose(1, 0, 2)
    row_stack = lambda g: g.reshape(N_CHIPS, g.shape[0] // N_CHIPS, g.shape[1])
    stacks = [col_stack(_uncat_grad(g_cat)), row_stack(g_out), g_gate, g_up, g_down]
    landed = _swap_halves(stacks)
    names = ["w_in", "w_out", "w_gate", "w_up", "w_down"]
    added = [_add_half(s, l, place, "rs_add_" + nm) for s, l, nm in zip(stacks, landed, names)]
    from_chips = _exchange_partials([a[0] for a in added])
    halves = [_sum_partials(a[1], p, "rs_sum_" + nm) for a, p, nm in zip(added, from_chips, names)]
    others = _share_halves(halves)
    big_m = [m_w_in, m_w_out, m_w_ffn_gate, m_w_ffn_up, m_w_ffn_down]
    big_v = [v_w_in, v_w_out, v_w_ffn_gate, v_w_ffn_up, v_w_ffn_down]
    big_upd = [_adam_big(w, gm, go, m, v, place, "adam_" + nm)
               for w, gm, go, m, v, nm in zip(big_w, halves, others, big_m, big_v, names)]

    order = ["norm1_w", "conv_w", "a_log", "dt_bias", "out_norm_w", "f_bias", "q_norm_w", "k_norm_w",
             "norm2_w", "final_w"]
    red = _allreduce_small(_pack([small[k] for k in order] + [small["loss"]]))
    red_shapes = [(1, D_MODEL), (CONV_K, 3 * WIDTH), (1, HEADS), (1, HEADS), (1, HEAD_DIM), (1, HEADS),
                  (1, HEAD_DIM), (1, HEAD_DIM), (1, D_MODEL), (D_MODEL,), ()]
    red_list = _unpack(red, red_shapes)
    loss = red_list[-1]
    small_g = dict(zip(order, red_list[:-1]))
    shard_cols = 3 * WIDTH // N_CHIPS
    small_g["conv_w"] = lax.dynamic_slice_in_dim(small_g["conv_w"], own * shard_cols, shard_cols, axis=1)[None]
    small_w = [norm1_w, gdn_conv_w, gdn_A_log, gdn_dt_bias, gdn_out_norm_w, fox_f_bias, fox_q_norm_w,
               fox_k_norm_w, norm2_w, final_norm_w]
    small_m = [m_norm1_w, m_gdn_conv_w, m_gdn_A_log, m_gdn_dt_bias, m_gdn_out_norm_w, m_fox_f_bias,
               m_fox_q_norm_w, m_fox_k_norm_w, m_norm2_w, m_final_norm_w]
    small_v = [v_norm1_w, v_gdn_conv_w, v_gdn_A_log, v_gdn_dt_bias, v_gdn_out_norm_w, v_fox_f_bias,
               v_fox_q_norm_w, v_fox_k_norm_w, v_norm2_w, v_final_norm_w]
    small_gl = [small_g[k].reshape(w.shape) for k, w in zip(order, small_w)]
    s_delta, s_m, s_v = _adam_call(_pack(small_w), _pack(small_gl), _pack(small_m), _pack(small_v), "adam_small")
    shapes = [w.shape for w in small_w]
    s_delta, s_m, s_v = _unpack(s_delta, shapes), _unpack(s_m, shapes), _unpack(s_v, shapes)

    big_pos = {1: 0, 9: 1, 11: 2, 12: 3, 13: 4}
    small_pos = {0: 0, 2: 1, 3: 2, 4: 3, 5: 4, 6: 5, 7: 6, 8: 7, 10: 8, 14: 9}
    grads, deltas, new_m, new_v = [], [], [], []
    for pos in range(15):
        if pos in big_pos:
            b = big_pos[pos]
            g, d, m2, v2 = big_upd[b]
            grads.append(g)
            deltas.append(d)
            new_m.append(m2)
            new_v.append(v2)
        else:
            s = small_pos[pos]
            grads.append(small_gl[s])
            deltas.append(s_delta[s])
            new_m.append(s_m[s])
            new_v.append(s_v[s])
    return (loss, grad_x[None], *grads, *deltas, *new_m, *new_v)
```

```python
import jax
import jax.numpy as jnp
import numpy as np
from jax import lax
from jax.experimental import pallas as pl
from jax.experimental.pallas import tpu as pltpu

F32 = jnp.float32
BF16 = jnp.bfloat16

D_MODEL = 1024
HEADS = 8
HEAD_DIM = 64
PAIRS = HEADS // 2
WIDTH = HEADS * HEAD_DIM
CHUNK = 64
CONV_K = 4
D_FF = 2816
FF_SHARD = D_FF // 4
EPS = 1e-6
SCALE = HEAD_DIM ** -0.5
LANES = 128
N_CHIPS = 4
D_IN = 4120
D_CAT = 4224
COL_SMALL = 4096 // LANES

ADAM_LR = 0.001
ADAM_B1 = 0.9
ADAM_B2 = 0.999
ADAM_EPS = 1e-08
ADAM_WD = 0.01
ADAM_STEP = 10

VMEM_LIMIT = 56 * 1024 * 1024
MESH = pl.DeviceIdType.MESH
HIGHEST = lax.Precision.HIGHEST


def _params(sem):
    return pltpu.CompilerParams(dimension_semantics=sem, vmem_limit_bytes=VMEM_LIMIT)


_CONTRACT = {"nn": ((1,), (0,)), "nt": ((1,), (1,)), "tn": ((0,), (0,))}


def _mm(a, b, *, dims, name, out_dtype=F32, add=None, tm=1024, tn=512, tk=512):
    if dims == "nn":
        (m, k), (k2, n) = a.shape, b.shape
    elif dims == "nt":
        (m, k), (n, k2) = a.shape, b.shape
    else:
        (k, m), (k2, n) = a.shape, b.shape
    assert k == k2, (a.shape, b.shape, dims)
    tm, tn, tk = min(tm, m), min(tn, n), min(tk, k)
    assert m % tm == 0 and n % tn == 0 and k % tk == 0, (m, n, k, tm, tn, tk)
    nk = k // tk
    a_spec = (pl.BlockSpec((tk, tm), lambda i, j, kk: (kk, i)) if dims == "tn"
              else pl.BlockSpec((tm, tk), lambda i, j, kk: (i, kk)))
    b_spec = (pl.BlockSpec((tn, tk), lambda i, j, kk: (j, kk)) if dims == "nt"
              else pl.BlockSpec((tk, tn), lambda i, j, kk: (kk, j)))
    o_spec = pl.BlockSpec((tm, tn), lambda i, j, kk: (i, j))
    contract = (_CONTRACT[dims], ((), ()))
    has_add = add is not None

    def body(*refs):
        a_ref, b_ref = refs[:2]
        add_ref = refs[2] if has_add else None
        o_ref = refs[3] if has_add else refs[2]
        part = lax.dot_general(a_ref[...].astype(BF16), b_ref[...].astype(BF16), contract,
                               preferred_element_type=F32)

        def finish(r):
            if has_add:
                r = r + add_ref[...].astype(F32)
            o_ref[...] = r.astype(out_dtype)

        if nk == 1:
            finish(part)
            return
        acc = refs[-1]
        kk = pl.program_id(2)

        @pl.when(kk == 0)
        def _():
            acc[...] = part

        @pl.when(kk > 0)
        def _():
            acc[...] += part

        @pl.when(kk == nk - 1)
        def _():
            finish(acc[...])

    ins = [a, b] + ([add] if has_add else [])
    in_specs = [a_spec, b_spec] + ([o_spec] if has_add else [])
    return pl.pallas_call(
        body, name=name, grid=(m // tm, n // tn, nk),
        in_specs=in_specs, out_specs=o_spec,
        out_shape=jax.ShapeDtypeStruct((m, n), out_dtype),
        scratch_shapes=[pltpu.VMEM((tm, tn), F32)] if nk > 1 else [],
        compiler_params=_params(("parallel", "parallel", "arbitrary")),
    )(*ins)


def _mm_blocks(a, b, *, name, grid, a_spec, b_spec, o_spec, out_shape, dims, n_sum=0, add=None, add_spec=None):
    contract = (_CONTRACT[dims], ((), ()))
    has_add = add is not None

    def body(*refs):
        a_ref, b_ref = refs[:2]
        o_ref = refs[-1]
        dot = lambda x, y: lax.dot_general(x.astype(BF16), y.astype(BF16), contract, preferred_element_type=F32)
        if n_sum:
            r = dot(a_ref[0], b_ref[0])
            for s in range(1, n_sum):
                r = r + dot(a_ref[s], b_ref[s])
        else:
            r = dot(a_ref[...], b_ref[...])
        if has_add:
            r = r + refs[2][...].astype(F32)
        o_ref[...] = r.astype(o_ref.dtype)

    return pl.pallas_call(
        body, name=name, grid=grid,
        in_specs=[a_spec, b_spec] + ([add_spec] if has_add else []), out_specs=o_spec, out_shape=out_shape,
        compiler_params=_params(("parallel",) * len(grid)),
    )(*([a, b] + ([add] if has_add else [])))


def _tiles(fn, *, name, rows, tm, ncol=1, row_ins=(), col_consts=(), full_consts=(),
           row_outs=(), acc_outs=()):
    nt = rows // tm
    assert rows % tm == 0
    n_full, n_col, n_row = len(full_consts), len(col_consts), len(row_ins)
    n_ro, n_acc = len(row_outs), len(acc_outs)

    def body(*refs):
        ins = refs[:n_full + n_col + n_row]
        outs = refs[n_full + n_col + n_row:]
        i = pl.program_id(1)
        res = fn(pl.program_id(0), *[r[...] for r in ins])
        for r, v in zip(outs[:n_ro], res[:n_ro]):
            r[...] = v.astype(r.dtype)
        if n_acc:
            @pl.when(i == 0)
            def _():
                for r in outs[n_ro:]:
                    r[...] = jnp.zeros_like(r)
            for r, v in zip(outs[n_ro:], res[n_ro:]):
                r[...] += v

    in_specs = [pl.BlockSpec(a.shape, lambda j, i, nd=a.ndim: (0,) * nd) for a in full_consts]
    in_specs += [pl.BlockSpec((nr, w), lambda j, i, o=o: (0, o + j)) for (_, nr, w, o) in col_consts]
    in_specs += [pl.BlockSpec((tm, w), lambda j, i, o=o: (i, o + j)) for (_, w, o) in row_ins]
    out_specs = [pl.BlockSpec((tm, w), lambda j, i: (i, j)) for (w, _) in row_outs]
    out_specs += [pl.BlockSpec((nr, w), lambda j, i: (0, j)) for (nr, w) in acc_outs]
    out_shape = [jax.ShapeDtypeStruct((rows, w * ncol), dt) for (w, dt) in row_outs]
    out_shape += [jax.ShapeDtypeStruct((nr, w * ncol), F32) for (nr, w) in acc_outs]
    args = list(full_consts) + [c[0] for c in col_consts] + [r[0] for r in row_ins]
    out = pl.pallas_call(
        body, name=name, grid=(ncol, nt), in_specs=in_specs, out_specs=out_specs, out_shape=out_shape,
        compiler_params=_params(("parallel", "arbitrary")),
    )(*args)
    return out


def _rms(x, w):
    return x * lax.rsqrt(jnp.mean(x * x, axis=-1, keepdims=True) + EPS) * w


def _lane_lo(shape):
    return lax.broadcasted_iota(jnp.int32, shape, len(shape) - 1) < HEAD_DIM


def _pair_sum(x):
    lo = _lane_lo(x.shape)
    s0 = jnp.sum(jnp.where(lo, x, 0.0), axis=-1, keepdims=True)
    s1 = jnp.sum(jnp.where(lo, 0.0, x), axis=-1, keepdims=True)
    return jnp.where(lo, s0, s1)


def _head_col(x, lo, h):
    keep = lo if h == 0 else jnp.logical_not(lo)
    return jnp.max(jnp.where(keep, x, -jnp.inf), axis=-1, keepdims=True)


def _softplus(x):
    return jnp.maximum(x, 0.0) + jnp.log1p(jnp.exp(-jnp.abs(x)))


def _silu(x):
    return x * jax.nn.sigmoid(x)


def _dot(a, b, contract):
    return lax.dot_general(a.astype(BF16), b.astype(BF16), (contract, ((), ())),
                           preferred_element_type=F32)


def _dot32(a, b, contract):
    return lax.dot_general(a, b, (contract, ((), ())), precision=HIGHEST, preferred_element_type=F32)


def _bd(y):
    yy = jnp.concatenate([y, y], axis=0)
    r = lax.broadcasted_iota(jnp.int32, yy.shape, 0) < HEAD_DIM
    c = lax.broadcasted_iota(jnp.int32, yy.shape, 1) < HEAD_DIM
    return jnp.where(r == c, yy, 0.0)


def _pp(x, y):
    return _dot(x, _bd(y), _CONTRACT["nn"])


def _pp_nt(x, y):
    return _dot(x, _bd(y), _CONTRACT["nt"])


def _pp_tn(x, y):
    full = _dot(x, y, _CONTRACT["tn"])
    return jnp.where(_lane_lo((HEAD_DIM, LANES)), full[:HEAD_DIM], full[HEAD_DIM:])


def _gdn_masks():
    row = lax.broadcasted_iota(jnp.int32, (CHUNK, LANES), 0)
    col = lax.broadcasted_iota(jnp.int32, (CHUNK, LANES), 1) % HEAD_DIM
    return row, col


def _interleave(chains):
    live = list(chains)
    while live:
        for g in list(live):
            try:
                next(g)
            except StopIteration:
                live.remove(g)


def _gdn_forward(qkv, betax, gcx, grow, rows):
    nchunk = rows // CHUNK

    def body(q_ref, k_ref, v_ref, bx_ref, gx_ref, gr_ref, o_ref, ss_ref, ts_ref, state):
        n = pl.program_id(0)

        @pl.when(n == 0)
        def _():
            state[...] = jnp.zeros_like(state)

        row, col = _gdn_masks()
        incl, strict = col <= row, col < row

        def chain(p):
            lanes = pl.ds(p * LANES, LANES)
            q, k, v, bx, gx = q_ref[:, lanes], k_ref[:, lanes], v_ref[:, lanes], bx_ref[:, lanes], gx_ref[:, lanes]
            gr = gr_ref[0, p]
            glast = gx_ref[pl.ds(CHUNK - 1, 1), lanes]
            s = state[p]
            dm = jnp.where(incl, jnp.exp(jnp.minimum(gx - gr, 0.0)), 0.0)
            kb, vb, eg, qs = k * bx, v * bx, jnp.exp(gx), q * SCALE
            yield
            big_g, big_p = _pp_nt(kb, k), _pp_nt(qs, k)
            yield
            x = -jnp.where(strict, big_g * dm, 0.0)
            att = jnp.where(incl, big_p * dm, 0.0)
            tm = jnp.where(row == col, 1.0, 0.0) + x
            x = _pp(x, x)
            yield
            for _ in range(4):
                step, x = _pp(tm, x), _pp(x, x)
                yield
                tm = tm + step
            tm = tm + _pp(tm, x)
            yield
            u, w = _pp(tm, vb), _pp(tm, kb * eg)
            yield
            ws, qgs = _pp(w, s), _pp(qs * eg, s)
            yield
            vn = u - ws
            kd = k * jnp.exp(glast - gx)
            avn, upd = _pp(att, vn), _pp_tn(kd, vn)
            yield
            ss_ref[0, p] = s
            ts_ref[0, p] = tm
            o_ref[:, lanes] = qgs + avn
            state[p] = s * jnp.exp(glast) + upd

        _interleave([chain(p) for p in range(PAIRS)])

    blk = lambda j: pl.BlockSpec((CHUNK, WIDTH), lambda n, j=j: (n, j))
    sv = pl.BlockSpec((1, PAIRS, CHUNK, LANES), lambda n: (n, 0, 0, 0))
    return pl.pallas_call(
        body, name="gdn_fwd", grid=(nchunk,),
        in_specs=[blk(0), blk(1), blk(2), blk(0), blk(0),
                  pl.BlockSpec((1, PAIRS, 1, LANES), lambda n: (n, 0, 0, 0))],
        out_specs=[blk(0), sv, sv],
        out_shape=[jax.ShapeDtypeStruct((rows, WIDTH), F32),
                   jax.ShapeDtypeStruct((nchunk, PAIRS, CHUNK, LANES), F32),
                   jax.ShapeDtypeStruct((nchunk, PAIRS, CHUNK, LANES), F32)],
        scratch_shapes=[pltpu.VMEM((PAIRS, CHUNK, LANES), F32)],
        compiler_params=_params(("arbitrary",)),
    )(qkv, qkv, qkv, betax, gcx, grow)


def _gdn_backward(qkv, betax, gcx, grow, ssave, tsave, do, rows):
    nchunk = rows // CHUNK

    def body(q_ref, k_ref, v_ref, bx_ref, gx_ref, gr_ref, ss_ref, ts_ref, do_ref,
             dq_ref, dk_ref, dv_ref, dbx_ref, dgx_ref, dgr_ref, dstate):
        n = pl.program_id(0)

        @pl.when(n == 0)
        def _():
            dstate[...] = jnp.zeros_like(dstate)

        row, col = _gdn_masks()
        incl, strict = col <= row, col < row

        def chain(p):
            lanes = pl.ds(p * LANES, LANES)
            q, k, v, bx, gx = q_ref[:, lanes], k_ref[:, lanes], v_ref[:, lanes], bx_ref[:, lanes], gx_ref[:, lanes]
            gr = gr_ref[0, p]
            glast = gx_ref[pl.ds(CHUNK - 1, 1), lanes]
            s, tm, d_o = ss_ref[0, p], ts_ref[0, p], do_ref[:, lanes]
            ds_out = dstate[p]
            dm = jnp.where(incl, jnp.exp(jnp.minimum(gx - gr, 0.0)), 0.0)
            kb, vb, eg, qs = k * bx, v * bx, jnp.exp(gx), q * SCALE
            kbg, qg = kb * eg, qs * eg
            ed = jnp.exp(glast - gx)
            kd = k * ed
            eglast = jnp.exp(glast)
            yield
            big_g, big_p = _pp_nt(kb, k), _pp_nt(qs, k)
            u, w = _pp(tm, vb), _pp(tm, kbg)
            dqg, kds = _pp_nt(d_o, s), _pp(kd, ds_out)
            yield
            low = jnp.where(strict, big_g * dm, 0.0)
            att = jnp.where(incl, big_p * dm, 0.0)
            ws, atd = _pp(w, s), _pp_tn(att, d_o)
            yield
            vn = u - ws
            dvn = kds + atd
            dkd, datt_raw = _pp_nt(vn, ds_out), _pp_nt(d_o, vn)
            dw_neg, dvb = _pp_nt(dvn, s), _pp_tn(tm, dvn)
            dtm_a, wdv = _pp_nt(dvn, vb), _pp_tn(w, dvn)
            qgd = _pp_tn(qg, d_o)
            yield
            datt = jnp.where(incl, datt_raw, 0.0)
            dw = -dw_neg
            dtm_b, dkbg = _pp_nt(dw, kbg), _pp_tn(tm, dw)
            dbig_p = datt * dm
            dqs_a, dk_p = _pp(dbig_p, k), _pp_tn(dbig_p, qs)
            yield
            inner = _pp_tn(tm, dtm_a + dtm_b)
            yield
            dlow = jnp.where(strict, -_pp_nt(inner, tm), 0.0)
            yield
            dbig_g = dlow * dm
            dkb_a, dk_g = _pp(dbig_g, k), _pp_tn(dbig_g, kb)
            yield
            dkb = dkb_a + dkbg * eg
            dqs = dqs_a + dqg * eg
            dk = dk_g + dk_p + dkd * ed + dkb * bx
            z = dlow * low + datt * att
            kdterm = dkd * kd
            dglast = (jnp.sum(ds_out * s, axis=0, keepdims=True) * eglast
                      + jnp.sum(kdterm, axis=0, keepdims=True))
            dgx = dqg * qg + dkbg * kbg - kdterm
            dgx = dgx + jnp.where(col == 0, _pair_sum(z), 0.0)
            dgx = dgx + jnp.where(row == CHUNK - 1, dglast, 0.0)
            dq_ref[:, lanes] = dqs * SCALE
            dk_ref[:, lanes] = dk
            dv_ref[:, lanes] = dvb * bx
            dbx_ref[:, lanes] = dkb * k + dvb * v
            dgx_ref[:, lanes] = dgx
            dgr_ref[0, p] = -jnp.sum(z, axis=0, keepdims=True)
            dstate[p] = ds_out * eglast + qgd - wdv

        _interleave([chain(p) for p in range(PAIRS)])

    last = nchunk - 1
    blk = lambda j: pl.BlockSpec((CHUNK, WIDTH), lambda n, j=j: (last - n, j))
    sv = pl.BlockSpec((1, PAIRS, CHUNK, LANES), lambda n: (last - n, 0, 0, 0))
    gr_spec = pl.BlockSpec((1, PAIRS, 1, LANES), lambda n: (last - n, 0, 0, 0))
    wide = jax.ShapeDtypeStruct((rows, WIDTH), F32)
    return pl.pallas_call(
        body, name="gdn_bwd", grid=(nchunk,),
        in_specs=[blk(0), blk(1), blk(2), blk(0), blk(0), gr_spec, sv, sv, blk(0)],
        out_specs=[blk(0)] * 5 + [gr_spec],
        out_shape=[wide] * 5 + [jax.ShapeDtypeStruct((nchunk, PAIRS, 1, LANES), F32)],
        scratch_shapes=[pltpu.VMEM((PAIRS, CHUNK, LANES), F32)],
        compiler_params=_params(("arbitrary",)),
    )(qkv, qkv, qkv, betax, gcx, grow, ssave, tsave, do)


ATT_TQ = 256


def _att_scores(qh, kt, fk, diag):
    s = _dot(qh, kt, _CONTRACT["nt"]) - fk
    if diag:
        r = lax.broadcasted_iota(jnp.int32, s.shape, 0)
        c = lax.broadcasted_iota(jnp.int32, s.shape, 1)
        s = jnp.where(r >= c, s, -jnp.inf)
    return s


def _head_masks(n):
    lo = _lane_lo((n, LANES))
    return [lo, jnp.logical_not(lo)]


def _attention_forward(fqk, proj, frow, rows):
    tq = tk = min(ATT_TQ, rows)
    nq = rows // tq
    v_off = 3072 // LANES

    def body(q_ref, k_ref, v_ref, fr_ref, o_ref, lse_ref):
        qi = pl.program_id(1)
        q = q_ref[...] * SCALE
        keep_q, keep_k = _head_masks(tq), _head_masks(tk)
        qh = [jnp.where(keep_q[h], q, 0.0).astype(BF16) for h in range(2)]

        def tile(ki, carry, diag):
            k0 = pl.multiple_of(ki * tk, tk)
            kt = k_ref[pl.ds(k0, tk), :].astype(BF16)
            v_t = v_ref[pl.ds(k0, tk), :]
            out = []
            for h in range(2):
                m, l, acc = carry[h]
                vt = jnp.where(keep_k[h], v_t, 0.0).astype(BF16)
                s = _att_scores(qh[h], kt, fr_ref[0, pl.ds(h, 1), pl.ds(k0, tk)], diag)
                m_new = jnp.maximum(m, jnp.max(s, axis=-1, keepdims=True))
                p = jnp.exp(s - m_new)
                alpha = jnp.exp(m - m_new)
                l = alpha * l + jnp.sum(p, axis=-1, keepdims=True)
                acc = alpha * acc + _dot(p, vt, _CONTRACT["nn"])
                out.append((m_new, l, acc))
            return tuple(out)

        one = (jnp.full((tq, 1), -jnp.inf, F32), jnp.zeros((tq, 1), F32), jnp.zeros((tq, LANES), F32))
        carry = lax.fori_loop(0, qi, lambda ki, c: tile(ki, c, False), (one, one))
        (m0, l0, acc0), (m1, l1, acc1) = tile(qi, carry, True)
        o_ref[...] = acc0 / l0 + acc1 / l1
        lse_ref[...] = jnp.where(keep_q[0], m0 + jnp.log(l0), m1 + jnp.log(l1))

    whole = lambda off: pl.BlockSpec((rows, LANES), lambda p, i, off=off: (0, off + p))
    qblk = lambda off: pl.BlockSpec((tq, LANES), lambda p, i, off=off: (i, off + p))
    wide = jax.ShapeDtypeStruct((rows, WIDTH), F32)
    return pl.pallas_call(
        body, name="fox_fwd", grid=(PAIRS, nq),
        in_specs=[qblk(0), whole(PAIRS), whole(v_off), pl.BlockSpec((1, 2, rows), lambda p, i: (p, 0, 0))],
        out_specs=[qblk(0), qblk(0)], out_shape=[wide, wide],
        compiler_params=_params(("parallel", "arbitrary")),
    )(fqk, fqk, proj, frow)


def _attention_delta(fqk, proj, frow, lse, dao, rows):
    tq = tk = min(ATT_TQ, rows)
    nq = rows // tq
    v_off = 3072 // LANES

    def body(q_ref, k_ref, v_ref, fr_ref, lse_ref, do_ref, delta_ref):
        qi = pl.program_id(1)
        q, d_o, lse_t = q_ref[...] * SCALE, do_ref[...], lse_ref[...]
        keep_q = _head_masks(tq)
        qh = [jnp.where(keep_q[h], q, 0.0).astype(BF16) for h in range(2)]
        doh = [jnp.where(keep_q[h], d_o, 0.0).astype(BF16) for h in range(2)]
        lse_h = [_head_col(lse_t, keep_q[0], h) for h in range(2)]

        def tile(ki, carry, diag):
            k0 = pl.multiple_of(ki * tk, tk)
            kt = k_ref[pl.ds(k0, tk), :].astype(BF16)
            vt = v_ref[pl.ds(k0, tk), :].astype(BF16)
            out = []
            for h in range(2):
                s = _att_scores(qh[h], kt, fr_ref[0, pl.ds(h, 1), pl.ds(k0, tk)], diag)
                dp = _dot(doh[h], vt, _CONTRACT["nt"])
                out.append(carry[h] + jnp.sum(jnp.exp(s - lse_h[h]) * dp, axis=-1, keepdims=True))
            return tuple(out)

        zero = jnp.zeros((tq, 1), F32)
        carry = lax.fori_loop(0, qi, lambda ki, c: tile(ki, c, False), (zero, zero))
        d0, d1 = tile(qi, carry, True)
        delta_ref[...] = jnp.where(keep_q[0], d0, d1)

    whole = lambda off: pl.BlockSpec((rows, LANES), lambda p, i, off=off: (0, off + p))
    qblk = lambda off: pl.BlockSpec((tq, LANES), lambda p, i, off=off: (i, off + p))
    return pl.pallas_call(
        body, name="fox_delta", grid=(PAIRS, nq),
        in_specs=[qblk(0), whole(PAIRS), whole(v_off),
                  pl.BlockSpec((1, 2, rows), lambda p, i: (p, 0, 0)), qblk(0), qblk(0)],
        out_specs=qblk(0), out_shape=jax.ShapeDtypeStruct((rows, WIDTH), F32),
        compiler_params=_params(("parallel", "arbitrary")),
    )(fqk, fqk, proj, frow, lse, dao)


def _attention_backward(fqk, proj, frow, delta, lse, dao, rows):
    tq = tk = min(ATT_TQ, rows)
    nq = rows // tq
    v_off = 3072 // LANES

    def body(q_ref, k_ref, v_ref, fr_ref, delta_ref, lse_ref, do_ref, dq_ref, dk_ref, dv_ref, dfr_ref):
        ki = pl.program_id(1)

        @pl.when(ki == 0)
        def _():
            dq_ref[...] = jnp.zeros_like(dq_ref)

        keep_q, keep_k = _head_masks(tq), _head_masks(tk)
        k_t = k_ref[...]
        kt = k_t.astype(BF16)
        vt = v_ref[...].astype(BF16)
        kh = [jnp.where(keep_k[h], k_t, 0.0).astype(BF16) for h in range(2)]
        fk = [fr_ref[0, pl.ds(h, 1), :] for h in range(2)]

        def tile(qi, carry, diag):
            dk, dv, df0, df1 = carry
            rows_q = pl.ds(pl.multiple_of(qi * tq, tq), tq)
            q, d_o, delta_x, lse_t = q_ref[rows_q, :] * SCALE, do_ref[rows_q, :], delta_ref[rows_q, :], lse_ref[rows_q, :]
            dq = jnp.zeros((tq, LANES), F32)
            dfs = []
            for h in range(2):
                qh = jnp.where(keep_q[h], q, 0.0).astype(BF16)
                doh = jnp.where(keep_q[h], d_o, 0.0).astype(BF16)
                s = _att_scores(qh, kt, fk[h], diag)
                p = jnp.exp(s - _head_col(lse_t, keep_q[0], h))
                dp = _dot(doh, vt, _CONTRACT["nt"])
                ds = p * (dp - _head_col(delta_x, keep_q[0], h))
                dv = dv + _dot(p, doh, _CONTRACT["tn"])
                dk = dk + _dot(ds, qh, _CONTRACT["tn"])
                dq = dq + _dot(ds, kh[h], _CONTRACT["nn"])
                dfs.append(-jnp.sum(ds, axis=0, keepdims=True))
            dq_ref[rows_q, :] += dq * SCALE
            return dk, dv, df0 + dfs[0], df1 + dfs[1]

        zero_kv = jnp.zeros((tk, LANES), F32)
        zero_f = jnp.zeros((1, tk), F32)
        carry = tile(ki, (zero_kv, zero_kv, zero_f, zero_f), True)
        dk, dv, df0, df1 = lax.fori_loop(ki + 1, nq, lambda qi, c: tile(qi, c, False), carry)
        dk_ref[...] = dk
        dv_ref[...] = dv.astype(dv_ref.dtype)
        dfr_ref[0, pl.ds(0, 1), :] = df0
        dfr_ref[0, pl.ds(1, 1), :] = df1

    whole = lambda off: pl.BlockSpec((rows, LANES), lambda p, i, off=off: (0, off + p))
    kblk = lambda off: pl.BlockSpec((tk, LANES), lambda p, i, off=off: (i, off + p))
    fr_spec = pl.BlockSpec((1, 2, tk), lambda p, i: (p, 0, i))
    wide = jax.ShapeDtypeStruct((rows, WIDTH), F32)
    return pl.pallas_call(
        body, name="fox_bwd", grid=(PAIRS, nq),
        in_specs=[whole(0), kblk(PAIRS), kblk(v_off), fr_spec, whole(0), whole(0), whole(0)],
        out_specs=[whole(0), kblk(0), kblk(0), fr_spec],
        out_shape=[wide, wide, jax.ShapeDtypeStruct((rows, WIDTH), BF16),
                   jax.ShapeDtypeStruct((PAIRS, 2, rows), F32)],
        compiler_params=_params(("parallel", "arbitrary")),
    )(fqk, fqk, proj, frow, delta, lse, dao)


def _lane_ids(shape):
    return lax.broadcasted_iota(jnp.int32, shape, len(shape) - 1)


def _gates_elem(a_log, dt_bias, f_bias, pre):
    lane = _lane_ids(pre.shape)
    beta = jax.nn.sigmoid(pre)
    g = -jnp.exp(a_log) * _softplus(pre + dt_bias)
    lf = -_softplus(-(pre + f_bias))
    return jnp.where(lane < 8, beta, jnp.where(lane < 16, g, jnp.where(lane < 24, lf, 0.0)))


def _tri_consts():
    r = np.arange(LANES)[:, None]
    c = np.arange(LANES)[None, :]
    full = (c <= r).astype(np.float32)
    chunked = full * ((r // CHUNK) == (c // CHUNK))
    return jnp.asarray(chunked), jnp.asarray(full)


def _cums_fwd(lc, lf, gates):
    rows = gates.shape[0]
    lane = _lane_ids((LANES, LANES))
    carry = jnp.zeros((1, LANES), F32)
    out = []
    for r in range(rows // LANES):
        blk = gates[r * LANES:(r + 1) * LANES]
        gc = _dot32(lc, blk, _CONTRACT["nn"])
        f = _dot32(lf, blk, _CONTRACT["nn"]) + carry
        carry = carry + jnp.sum(blk, axis=0, keepdims=True)
        out.append(jnp.where((lane >= 8) & (lane < 16), gc, jnp.where((lane >= 16) & (lane < 24), f, 0.0)))
    return jnp.concatenate(out, axis=0)


def _cums_bwd(lc, lf, dcums):
    rows = dcums.shape[0]
    lane = _lane_ids((LANES, LANES))
    is_g = (lane >= 8) & (lane < 16)
    is_f = (lane >= 16) & (lane < 24)
    carry = jnp.zeros((1, LANES), F32)
    out = [None] * (rows // LANES)
    for r in reversed(range(rows // LANES)):
        blk = dcums[r * LANES:(r + 1) * LANES]
        dg = jnp.where(is_g, blk, 0.0)
        df = jnp.where(is_f, blk, 0.0)
        out[r] = _dot32(lc, dg, _CONTRACT["tn"]) + _dot32(lf, df, _CONTRACT["tn"]) + carry
        carry = carry + jnp.sum(df, axis=0, keepdims=True)
    return jnp.concatenate(out, axis=0)


def _expand_consts():
    xb = np.zeros((LANES, WIDTH), np.float32)
    xg = np.zeros((LANES, WIDTH), np.float32)
    for h in range(HEADS):
        xb[h, h * HEAD_DIM:(h + 1) * HEAD_DIM] = 1.0
        xg[8 + h, h * HEAD_DIM:(h + 1) * HEAD_DIM] = 1.0
    return jnp.asarray(xb), jnp.asarray(xg)


def _shift_down(x, s):
    if s == 0:
        return x
    row = lax.broadcasted_iota(jnp.int32, x.shape, 0)
    return jnp.where(row >= s, pltpu.roll(x, s, 0), 0.0)


def _shift_up(x, s):
    if s == 0:
        return x
    n = x.shape[0]
    row = lax.broadcasted_iota(jnp.int32, x.shape, 0)
    return jnp.where(row < n - s, pltpu.roll(x, n - s, 0), 0.0)


def _row_of(cw, i):
    row = lax.broadcasted_iota(jnp.int32, cw.shape, 0)
    return jnp.sum(jnp.where(row == i, cw, 0.0), axis=0, keepdims=True)


def _conv(cw, x):
    c = jnp.zeros_like(x)
    for i in range(CONV_K):
        c = c + _row_of(cw, i) * _shift_down(x, CONV_K - 1 - i)
    return c


def _post_conv(is_qk, c):
    s = _silu(c)
    n = s * lax.rsqrt(_pair_sum(s * s) + EPS)
    return jnp.where(is_qk, n, s)


def _gdn_prep_fwd(col, cw, x):
    return (_post_conv(col < 2 * PAIRS, _conv(cw, x)),)


def _gdn_prep_bwd(is_qk, cw, x, dy):
    c = _conv(cw, x)
    _, vjp = jax.vjp(lambda cc: _post_conv(is_qk, cc), c)
    (dc,) = vjp(dy)
    dx = jnp.zeros_like(x)
    row = lax.broadcasted_iota(jnp.int32, cw.shape, 0)
    dcw = jnp.zeros(cw.shape, F32)
    for i in range(CONV_K):
        s = CONV_K - 1 - i
        dx = dx + _row_of(cw, i) * _shift_up(dc, s)
        dcw = dcw + jnp.where(row == i, jnp.sum(dc * _shift_down(x, s), axis=0, keepdims=True), 0.0)
    return dx, dcw


def _head_rms(w, x):
    return x * lax.rsqrt(_pair_sum(x * x) / HEAD_DIM + EPS) * w


def _cat_weights(w_in):
    pad = jnp.zeros((w_in.shape[0], D_CAT - D_IN), w_in.dtype)
    return jnp.concatenate([w_in[:, :2048], w_in[:, 2064:4112], w_in[:, 2048:2064], w_in[:, 4112:4120], pad], axis=1)


def _uncat_grad(g):
    return jnp.concatenate([g[:, :2048], g[:, 4096:4112], g[:, 2048:4096], g[:, 4112:4120]], axis=1)


def _lanes_to_rowform(v8, rows):
    return v8.reshape(rows // CHUNK, CHUNK, HEADS).transpose(0, 2, 1).reshape(rows // CHUNK, PAIRS, 1, LANES)


def _rowform_to_lanes(v, rows):
    return v.reshape(rows // CHUNK, HEADS, CHUNK).transpose(0, 2, 1).reshape(rows, HEADS)


def _local_step(x, target, norm1_w, w_cat, conv_w, a_log, dt_bias, out_norm_w, f_bias, q_norm_w, k_norm_w,
                norm2_w, final_w, late_weights, ffn_grads_ready):
    rows = x.shape[0]
    tm = min(256, rows)
    lc, lf = _tri_consts()
    xb, xg = _expand_consts()

    (h1,) = _tiles(lambda col, w, xx: (_rms(xx, w),), name="norm1", rows=rows, tm=tm,
                   full_consts=[norm1_w], row_ins=[(x, D_MODEL, 0)], row_outs=[(D_MODEL, BF16)])
    proj = _mm(h1, w_cat, dims="nn", name="in_proj", tn=384, tk=1024)

    lane_pad = lambda v, off: jnp.pad(v.reshape(1, -1), ((0, 0), (off, LANES - off - v.size)))
    p_a, p_dt, p_fb = lane_pad(a_log, 8), lane_pad(dt_bias, 8), lane_pad(f_bias, 16)

    def gates_fwd(col, lcv, lfv, a, dt, fb, pre):
        gates = _gates_elem(a, dt, fb, pre)
        return gates, _cums_fwd(lcv, lfv, gates)

    gates, cums = _tiles(gates_fwd, name="gates", rows=rows, tm=rows,
                         full_consts=[lc, lf, p_a, p_dt, p_fb], row_ins=[(proj, LANES, COL_SMALL)],
                         row_outs=[(LANES, F32), (LANES, F32)])

    def expand_fwd(col, b, g, gt, cm):
        return (_dot32(gt, b, _CONTRACT["nn"]), _dot32(cm, g, _CONTRACT["nn"]))

    betax, gcx = _tiles(expand_fwd, name="expand", rows=rows, tm=tm, full_consts=[xb, xg],
                        row_ins=[(gates, LANES, 0), (cums, LANES, 0)],
                        row_outs=[(WIDTH, F32)] * 2)
    grow = _lanes_to_rowform(cums[:, 8:16], rows)
    frow = cums[:, 16:24].T.reshape(PAIRS, 2, rows)

    (qkv,) = _tiles(_gdn_prep_fwd, name="gdn_prep", rows=rows, tm=rows, ncol=3 * PAIRS,
                    col_consts=[(conv_w, CONV_K, LANES, 0)], row_ins=[(proj, LANES, 0)],
                    row_outs=[(LANES, F32)])
    o_gdn, ssave, tsave = _gdn_forward(qkv, betax, gcx, grow, rows)

    w_qk = jnp.concatenate([jnp.tile(q_norm_w.reshape(1, -1), (1, HEADS)),
                            jnp.tile(k_norm_w.reshape(1, -1), (1, HEADS))], axis=1)
    fox_off = 2048 // LANES
    (fqk,) = _tiles(lambda col, w, xx: (_head_rms(w, xx),), name="fox_prep", rows=rows, tm=rows, ncol=2 * PAIRS,
                    col_consts=[(w_qk, 1, LANES, 0)], row_ins=[(proj, LANES, fox_off)],
                    row_outs=[(LANES, F32)])
    ao, lse = _attention_forward(fqk, proj, frow, rows)

    w_on = jnp.tile(out_norm_w.reshape(1, -1), (1, 2))
    z_off, fg_off = 1536 // LANES, 3584 // LANES
    mix_g_fn = lambda w, o, z: _head_rms(w, o) * _silu(z)
    mix_f_fn = lambda a, g: a * jax.nn.sigmoid(g)
    (mix_g,) = _tiles(lambda col, w, o, z: (mix_g_fn(w, o, z),), name="mix_gdn", rows=rows, tm=rows, ncol=PAIRS,
                      full_consts=[w_on], row_ins=[(o_gdn, LANES, 0), (proj, LANES, z_off)],
                      row_outs=[(LANES, BF16)])
    (mix_f,) = _tiles(lambda col, a, g: (mix_f_fn(a, g),), name="mix_fox", rows=rows, tm=rows, ncol=PAIRS,
                      row_ins=[(ao, LANES, 0), (proj, LANES, fg_off)], row_outs=[(LANES, BF16)])
    mix = jnp.concatenate([mix_g, mix_f], axis=1)
    w_out, w_gate, w_up, w_down = late_weights(mix)
    x1 = _mm(mix, w_out, dims="nn", name="out_proj", add=x, tk=1024)

    (h2,) = _tiles(lambda col, w, xx: (_rms(xx, w),), name="norm2", rows=rows, tm=tm,
                   full_consts=[norm2_w], row_ins=[(x1, D_MODEL, 0)], row_outs=[(D_MODEL, BF16)])
    t_rows, t_cols, t_act = min(1024, rows), 512, min(512, rows)
    n_rt = rows // t_rows
    st_act = jax.ShapeDtypeStruct((N_CHIPS, rows, FF_SHARD), F32)
    st_rows = pl.BlockSpec((None, t_rows, FF_SHARD), lambda i, j: (j, i, 0))
    out_rows = pl.BlockSpec((t_rows, t_cols), lambda i, n: (i, n))
    flat = lambda t: t.reshape(N_CHIPS * rows, FF_SHARD)

    def ffn_in(w_st, name):
        return _mm_blocks(h2, w_st, name=name, grid=(n_rt, N_CHIPS), dims="nn",
                          a_spec=pl.BlockSpec((t_rows, D_MODEL), lambda i, j: (i, 0)),
                          b_spec=pl.BlockSpec((None, D_MODEL, FF_SHARD), lambda i, j: (j, 0, 0)),
                          o_spec=st_rows, out_shape=st_act)

    gate, up = ffn_in(w_gate, "ffn_gate"), ffn_in(w_up, "ffn_up")
    act_fn = lambda g, u: _silu(g) * u
    (act,) = _tiles(lambda col, g, u: (act_fn(g, u),), name="ffn_act", rows=N_CHIPS * rows, tm=t_act,
                    row_ins=[(flat(gate), FF_SHARD, 0), (flat(up), FF_SHARD, 0)], row_outs=[(FF_SHARD, BF16)])
    act = act.reshape(st_act.shape)
    x2 = _mm_blocks(act, w_down, name="ffn_down", grid=(n_rt, D_MODEL // t_cols), dims="nn", n_sum=N_CHIPS,
                    a_spec=pl.BlockSpec((N_CHIPS, t_rows, FF_SHARD), lambda i, n: (0, i, 0)),
                    b_spec=pl.BlockSpec((N_CHIPS, FF_SHARD, t_cols), lambda i, n: (0, 0, n)),
                    o_spec=out_rows, out_shape=jax.ShapeDtypeStruct((rows, D_MODEL), F32),
                    add=x1, add_spec=out_rows)

    def final_fn(col, w, xx, tgt):
        y, vjp = jax.vjp(_rms, xx, w)
        err = y - tgt
        loss = 0.5 * jnp.sum(err * err) / D_MODEL
        dx, dw = vjp(err / D_MODEL)
        return dx, dx, jnp.full((1, LANES), loss, F32), dw

    dx2, dx2_b, loss, d_final_w = _tiles(final_fn, name="final_loss", rows=rows, tm=tm, full_consts=[final_w],
                                         row_ins=[(x2, D_MODEL, 0), (target, D_MODEL, 0)],
                                         row_outs=[(D_MODEL, F32), (D_MODEL, BF16)],
                                         acc_outs=[(1, LANES), (1, D_MODEL)])

    dact = _mm_blocks(dx2_b, w_down, name="d_act", grid=(n_rt, N_CHIPS), dims="nt",
                      a_spec=pl.BlockSpec((t_rows, D_MODEL), lambda i, j: (i, 0)),
                      b_spec=pl.BlockSpec((None, FF_SHARD, D_MODEL), lambda i, j: (j, 0, 0)),
                      o_spec=st_rows, out_shape=st_act)
    g_down = _mm_blocks(act, dx2_b, name="g_down", grid=(N_CHIPS, D_MODEL // t_cols), dims="tn",
                        a_spec=pl.BlockSpec((None, rows, FF_SHARD), lambda j, n: (j, 0, 0)),
                        b_spec=pl.BlockSpec((rows, t_cols), lambda j, n: (0, n)),
                        o_spec=pl.BlockSpec((None, FF_SHARD, t_cols), lambda j, n: (j, 0, n)),
                        out_shape=jax.ShapeDtypeStruct((N_CHIPS, FF_SHARD, D_MODEL), F32))

    def act_bwd(col, g, u, d):
        _, vjp = jax.vjp(act_fn, g, u)
        return vjp(d)

    dgate, dup = _tiles(act_bwd, name="ffn_act_bwd", rows=N_CHIPS * rows, tm=t_act,
                        row_ins=[(flat(gate), FF_SHARD, 0), (flat(up), FF_SHARD, 0), (flat(dact), FF_SHARD, 0)],
                        row_outs=[(FF_SHARD, BF16), (FF_SHARD, BF16)])
    dgate, dup = dgate.reshape(st_act.shape), dup.reshape(st_act.shape)

    def d_h2(d_st, w_st, name, add):
        return _mm_blocks(d_st, w_st, name=name, grid=(n_rt, D_MODEL // t_cols), dims="nt", n_sum=N_CHIPS,
                          a_spec=pl.BlockSpec((N_CHIPS, t_rows, FF_SHARD), lambda i, n: (0, i, 0)),
                          b_spec=pl.BlockSpec((N_CHIPS, t_cols, FF_SHARD), lambda i, n: (0, n, 0)),
                          o_spec=out_rows, out_shape=jax.ShapeDtypeStruct((rows, D_MODEL), F32),
                          add=add, add_spec=out_rows)

    dh2 = d_h2(dup, w_up, "d_h2_up", d_h2(dgate, w_gate, "d_h2_gate", None))

    def g_ffn_in(d_st, name):
        return _mm_blocks(h2, d_st, name=name, grid=(N_CHIPS,), dims="tn",
                          a_spec=pl.BlockSpec((rows, D_MODEL), lambda j: (0, 0)),
                          b_spec=pl.BlockSpec((None, rows, FF_SHARD), lambda j: (j, 0, 0)),
                          o_spec=pl.BlockSpec((None, D_MODEL, FF_SHARD), lambda j: (j, 0, 0)),
                          out_shape=jax.ShapeDtypeStruct((N_CHIPS, D_MODEL, FF_SHARD), F32))

    g_gate, g_up = g_ffn_in(dgate, "g_gate"), g_ffn_in(dup, "g_up")
    norm2_w = norm2_w + ffn_grads_ready(g_gate, g_up, g_down)

    def norm_bwd(col, w, xx, dh, dres):
        _, vjp = jax.vjp(_rms, xx, w)
        dx, dw = vjp(dh)
        return dx + dres, dx + dres, dw

    dx1, dx1_b, d_norm2_w = _tiles(norm_bwd, name="norm2_bwd", rows=rows, tm=tm, full_consts=[norm2_w],
                                   row_ins=[(x1, D_MODEL, 0), (dh2, D_MODEL, 0), (dx2, D_MODEL, 0)],
                                   row_outs=[(D_MODEL, F32), (D_MODEL, BF16)], acc_outs=[(1, D_MODEL)])
    dmix = _mm(dx1_b, w_out, dims="nt", name="d_mix", tk=1024)
    g_out = _mm(mix, dx1_b, dims="tn", name="g_out", tk=rows)

    def mix_g_bwd(col, w, o, z, d):
        _, vjp = jax.vjp(mix_g_fn, w, o, z)
        dw, do_, dz = vjp(d)
        return do_, dz, dw

    do_gdn, dz, d_on = _tiles(mix_g_bwd, name="mix_gdn_bwd", rows=rows, tm=rows, ncol=PAIRS, full_consts=[w_on],
                              row_ins=[(o_gdn, LANES, 0), (proj, LANES, z_off), (dmix, LANES, 0)],
                              row_outs=[(LANES, F32), (LANES, BF16)], acc_outs=[(1, LANES)])

    def mix_f_bwd(col, a, g, d):
        _, vjp = jax.vjp(mix_f_fn, a, g)
        return vjp(d)

    dao, dfgate = _tiles(mix_f_bwd, name="mix_fox_bwd", rows=rows, tm=rows, ncol=PAIRS,
                         row_ins=[(ao, LANES, 0), (proj, LANES, fg_off), (dmix, LANES, PAIRS)],
                         row_outs=[(LANES, F32), (LANES, BF16)])

    delta = _attention_delta(fqk, proj, frow, lse, dao, rows)
    dfq, dfk, dfv, dfrow = _attention_backward(fqk, proj, frow, delta, lse, dao, rows)

    def fox_prep_bwd(col, w, xx, d):
        _, vjp = jax.vjp(_head_rms, w, xx)
        dw, dx = vjp(d)
        return dx, dw

    dfqk, d_wqk = [], []
    for part, d_n in enumerate((dfq, dfk)):
        dx_p, dw_p = _tiles(fox_prep_bwd, name="fox_prep_bwd_" + "qk"[part], rows=rows, tm=rows, ncol=PAIRS,
                            col_consts=[(w_qk, 1, LANES, part * PAIRS)],
                            row_ins=[(proj, LANES, fox_off + part * PAIRS), (d_n, LANES, 0)],
                            row_outs=[(LANES, BF16)], acc_outs=[(1, LANES)])
        dfqk.append(dx_p)
        d_wqk.append(dw_p)

    dq, dk, dv, dbetax, dgcx, dgrow = _gdn_backward(qkv, betax, gcx, grow, ssave, tsave, do_gdn, rows)
    dqkv, d_conv = [], []
    for part, d_n in enumerate((dq, dk, dv)):
        prep_bwd = lambda col, cw, xx, dy, is_qk=(part < 2): _gdn_prep_bwd(is_qk, cw, xx, dy)
        dx_p, dw_p = _tiles(prep_bwd, name="gdn_prep_bwd_" + "qkv"[part], rows=rows, tm=rows, ncol=PAIRS,
                            col_consts=[(conv_w, CONV_K, LANES, part * PAIRS)],
                            row_ins=[(proj, LANES, part * PAIRS), (d_n, LANES, 0)],
                            row_outs=[(LANES, BF16)], acc_outs=[(CONV_K, LANES)])
        dqkv.append(dx_p)
        d_conv.append(dw_p)
    d_conv = jnp.concatenate(d_conv, axis=1)

    def expand_bwd(col, b, g, db, dg):
        return (_dot32(db, b, _CONTRACT["nt"]), _dot32(dg, g, _CONTRACT["nt"]))

    dgates_b, dcums_g = _tiles(expand_bwd, name="expand_bwd", rows=rows, tm=tm, full_consts=[xb, xg],
                               row_ins=[(dbetax, WIDTH, 0), (dgcx, WIDTH, 0)],
                               row_outs=[(LANES, F32), (LANES, F32)])
    dcums_row = jnp.concatenate([jnp.zeros((rows, 8), F32), _rowform_to_lanes(dgrow, rows),
                                 dfrow.reshape(HEADS, rows).T, jnp.zeros((rows, LANES - 24), F32)], axis=1)

    def gates_bwd(col, lcv, lfv, a, dt, fb, pre, dgb, dcg, dcr):
        lane = _lane_ids(pre.shape)
        dgates = jnp.where(lane < 8, dgb, _cums_bwd(lcv, lfv, dcg + dcr))
        _, vjp = jax.vjp(_gates_elem, a, dt, fb, pre)
        da, ddt, dfb, dpre = vjp(dgates)
        return dpre, da, ddt, dfb

    dpre, d_a, d_dt, d_fb = _tiles(gates_bwd, name="gates_bwd", rows=rows, tm=rows,
                                   full_consts=[lc, lf, p_a, p_dt, p_fb],
                                   row_ins=[(proj, LANES, COL_SMALL), (dgates_b, LANES, 0), (dcums_g, LANES, 0),
                                            (dcums_row, LANES, 0)],
                                   row_outs=[(LANES, BF16)], acc_outs=[(1, LANES)] * 3)

    dproj = jnp.concatenate(dqkv + [dz] + dfqk + [dfv, dfgate, dpre], axis=1)
    dh1 = _mm(dproj, w_cat, dims="nt", name="d_h1", tk=D_CAT)
    g_cat = _mm(h1, dproj, dims="tn", name="g_in", tn=384, tk=rows)

    def norm1_bwd(col, w, xx, dh, dres):
        _, vjp = jax.vjp(_rms, xx, w)
        dx, dw = vjp(dh)
        return dx + dres, dw

    grad_x, d_norm1_w = _tiles(norm1_bwd, name="norm1_bwd", rows=rows, tm=tm, full_consts=[norm1_w],
                               row_ins=[(x, D_MODEL, 0), (dh1, D_MODEL, 0), (dx1, D_MODEL, 0)],
                               row_outs=[(D_MODEL, F32)], acc_outs=[(1, D_MODEL)])

    fold = lambda v: v.reshape(-1, HEAD_DIM).sum(axis=0)
    small = dict(
        loss=loss[0, 0],
        norm1_w=d_norm1_w, conv_w=d_conv, a_log=d_a[0, 8:16], dt_bias=d_dt[0, 8:16],
        out_norm_w=fold(d_on), f_bias=d_fb[0, 16:24], q_norm_w=fold(d_wqk[0]),
        k_norm_w=fold(d_wqk[1]), norm2_w=d_norm2_w, final_w=d_final_w)
    return grad_x, g_cat, g_out, g_gate, g_up, g_down, small


HBM_SPEC = pl.BlockSpec(memory_space=pltpu.HBM)


def _place():
    x, y, c = lax.axis_index("x"), lax.axis_index("y"), lax.axis_index("c")
    chips = [(1 - x, y), (x, 1 - y), (1 - x, 1 - y)]
    return x, y, c, 2 * x + y, (x, y, 1 - c), chips, [2 * cx + cy for cx, cy in chips]


def _remote(src, dst, send_sem, recv_sem, to):
    return pltpu.make_async_remote_copy(src_ref=src, dst_ref=dst, send_sem=send_sem, recv_sem=recv_sem,
                                        device_id=to, device_id_type=MESH)


def _allgather_weights(shards, conv):
    n = len(shards)
    halves = [s.shape[0] // 2 for s in shards]
    per = 6
    own_base = n * per + 3

    def body(*refs):
        ins, conv_in = refs[:n], refs[n]
        outs, conv_out = refs[n + 1:2 * n + 1], refs[2 * n + 1]
        send_sems, recv_sems = refs[2 * n + 2:]
        x, y, c, own, sib, chips, chip_idx = _place()

        def half(i, ref, hc):
            return ref.at[pl.ds(pl.multiple_of(hc * halves[i], 16), halves[i]), :]

        sent = []
        for i, (src, dst) in enumerate(zip(list(ins) + [conv_in], list(outs) + [conv_out])):
            k = own_base + i
            sent.append(_remote(src, dst.at[own], send_sems.at[k], recv_sems.at[k], sib))
        for i in range(n):
            for j, chip in enumerate(chips):
                k = i * per + j
                sent.append(_remote(half(i, ins[i], c), half(i, outs[i].at[own], c),
                                    send_sems.at[k], recv_sems.at[k], (*chip, c)))
        for j, chip in enumerate(chips):
            k = n * per + j
            sent.append(_remote(conv_in, conv_out.at[own], send_sems.at[k], recv_sems.at[k], (*chip, c)))
        for cp in sent:
            cp.start()
        for i in range(n):
            for j in range(len(chips)):
                k = i * per + j
                landed = half(i, outs[i].at[chip_idx[j]], c)
                _remote(landed, landed, send_sems.at[k], recv_sems.at[k], sib).wait_recv()
                fwd = _remote(landed, landed, send_sems.at[k + 3], recv_sems.at[k + 3], sib)
                fwd.start()
                sent.append(fwd)
        for i in range(n):
            for j in range(len(chips)):
                k = i * per + 3 + j
                landed = half(i, outs[i].at[chip_idx[j]], 1 - c)
                _remote(landed, landed, send_sems.at[k], recv_sems.at[k], sib).wait_recv()
        for j in range(len(chips)):
            k = n * per + j
            landed = conv_out.at[chip_idx[j]]
            _remote(landed, landed, send_sems.at[k], recv_sems.at[k], sib).wait_recv()
        for i, dst in enumerate(list(outs) + [conv_out]):
            k = own_base + i
            landed = dst.at[own]
            _remote(landed, landed, send_sems.at[k], recv_sems.at[k], sib).wait_recv()
        for cp in sent:
            cp.wait_send()

    n_sem = own_base + n + 1
    out_shape = [jax.ShapeDtypeStruct((N_CHIPS,) + s.shape, s.dtype) for s in shards]
    out_shape.append(jax.ShapeDtypeStruct((N_CHIPS,) + conv.shape, conv.dtype))
    res = pl.pallas_call(
        body, name="allgather_weights", out_shape=out_shape,
        in_specs=[HBM_SPEC] * (n + 1), out_specs=[HBM_SPEC] * (n + 1),
        scratch_shapes=[pltpu.SemaphoreType.DMA((n_sem,)), pltpu.SemaphoreType.DMA((n_sem,))],
    )(*shards, conv)
    return res[:n], res[n]


SEM_SPEC = pl.BlockSpec(memory_space=pltpu.SEMAPHORE)
ANY_SPEC = pl.BlockSpec(memory_space=pl.ANY)
DATAFLOW = pltpu.SideEffectType.DATAFLOW_SIDE_EFFECTING


def _gather_plan(srcs, lands):
    x, y, c, own, sib, chips, chip_idx = _place()
    plan = []
    for src, land in zip(srcs, lands):
        for j, chip in enumerate(chips):
            plan.append((src, land.at[own], (*chip, c), land.at[chip_idx[j]]))
        plan.append((src, land.at[own], sib, land.at[own]))
    return plan


def _exchange_plan(srcs, lands):
    x, y, c, own, sib, chips, chip_idx = _place()
    plan = []
    for src, land in zip(srcs, lands):
        for j, chip in enumerate(chips):
            plan.append((src.at[chip_idx[j]], land.at[j], (*chip, c), land.at[j]))
    return plan


def _split_start(name, plan_fn, srcs, land_shapes, n_copies, after):
    n = len(srcs)

    def body(*refs):
        src_refs, land_refs = refs[:n], refs[n:2 * n]
        send_sems, recv_sems = refs[2 * n + 1], refs[2 * n + 2]
        token = refs[-1]
        for k, (src, dst, to, _) in enumerate(plan_fn(src_refs, land_refs)):
            _remote(src, dst, send_sems.at[k], recv_sems.at[k], to).start()
        token[...] = jnp.zeros_like(token)

    lands = [pltpu.with_memory_space_constraint(lax.empty(s.shape, s.dtype), pltpu.HBM) for s in land_shapes]
    srcs = [pltpu.with_memory_space_constraint(s, pltpu.HBM) for s in srcs]
    out_shape = ([pltpu.SemaphoreType.DMA((n_copies,)), pltpu.SemaphoreType.DMA((n_copies,))]
                 + [pltpu.HBM(s.shape, s.dtype) for s in srcs] + [pltpu.HBM(s.shape, s.dtype) for s in land_shapes]
                 + [jax.ShapeDtypeStruct((8, LANES), F32)])
    res = pl.pallas_call(
        body, name=name, out_shape=out_shape,
        in_specs=[HBM_SPEC] * (2 * n) + [ANY_SPEC],
        out_specs=[SEM_SPEC, SEM_SPEC] + [HBM_SPEC] * (2 * n) + [pl.BlockSpec(memory_space=pltpu.VMEM)],
        input_output_aliases={i: 2 + i for i in range(2 * n)},
        compiler_params=pltpu.CompilerParams(has_side_effects=DATAFLOW),
    )(*srcs, *lands, after)
    return dict(sems=res[:2], srcs=res[2:2 + n], lands=res[2 + n:2 + 2 * n], token=res[-1], n=n)


def _split_wait(name, plan_fn, started, after):
    n = started["n"]

    def body(*refs):
        src_refs, land_refs = refs[:n], refs[n:2 * n]
        send_sems, recv_sems = refs[2 * n], refs[2 * n + 1]
        for k, (src, _, to, landed) in enumerate(plan_fn(src_refs, land_refs)):
            copy = _remote(src, landed, send_sems.at[k], recv_sems.at[k], to)
            copy.wait_send()
            copy.wait_recv()

    srcs, lands = started["srcs"], started["lands"]
    res = pl.pallas_call(
        body, name=name,
        out_shape=[pltpu.HBM(s.shape, s.dtype) for s in srcs] + [pltpu.HBM(s.shape, s.dtype) for s in lands],
        in_specs=[HBM_SPEC] * (2 * n) + [SEM_SPEC, SEM_SPEC, ANY_SPEC],
        out_specs=[HBM_SPEC] * (2 * n),
        input_output_aliases={i: i for i in range(2 * n)},
        compiler_params=pltpu.CompilerParams(has_side_effects=DATAFLOW),
    )(*srcs, *lands, *started["sems"], after)
    return res[n:]


def _swap_halves(stacks, name):
    n = len(stacks)

    def body(*refs):
        ins, outs = refs[:n], refs[n:2 * n]
        send_sems, recv_sems = refs[2 * n:]
        x, y, c, own, sib, chips, chip_idx = _place()
        cps = []
        for i in range(n):
            h = stacks[i].shape[1] // 2
            src = ins[i].at[:, pl.ds(pl.multiple_of((1 - c) * h, 8), h), :]
            cps.append(_remote(src, outs[i], send_sems.at[i], recv_sems.at[i], sib))
        for cp in cps:
            cp.start()
        for cp in cps:
            cp.wait()

    out_shape = [jax.ShapeDtypeStruct((N_CHIPS, s.shape[1] // 2, s.shape[2]), s.dtype) for s in stacks]
    return pl.pallas_call(
        body, name=name, out_shape=out_shape,
        in_specs=[HBM_SPEC] * n, out_specs=[HBM_SPEC] * n,
        scratch_shapes=[pltpu.SemaphoreType.DMA((n,)), pltpu.SemaphoreType.DMA((n,))],
    )(*stacks)


def _add_half(stack, landed, place, name):
    _, h, cols = landed.shape

    def body(place_ref, a_ref, b_ref, o_ref, own_ref):
        part = (a_ref[...] + b_ref[...]).astype(o_ref.dtype)
        o_ref[...] = part

        @pl.when(pl.program_id(0) == place_ref[1])
        def _():
            own_ref[...] = part[0]

    return pl.pallas_call(
        body, name=name,
        out_shape=[jax.ShapeDtypeStruct(landed.shape, BF16), jax.ShapeDtypeStruct((h, cols), BF16)],
        grid_spec=pltpu.PrefetchScalarGridSpec(
            num_scalar_prefetch=1, grid=(N_CHIPS,),
            in_specs=[pl.BlockSpec((1, h, cols), lambda j, p: (j, p[0], 0)),
                      pl.BlockSpec((1, h, cols), lambda j, p: (j, 0, 0))],
            out_specs=[pl.BlockSpec((1, h, cols), lambda j, p: (j, 0, 0)),
                       pl.BlockSpec((h, cols), lambda j, p: (0, 0))]),
        compiler_params=_params(("arbitrary",)),
    )(place, stack, landed)


def _exchange_partials(parts):
    n = len(parts)

    def body(*refs):
        ins, outs = refs[:n], refs[n:2 * n]
        send_sems, recv_sems = refs[2 * n:]
        x, y, c, own, sib, chips, chip_idx = _place()
        sent = []
        for i in range(n):
            for j, chip in enumerate(chips):
                k = i * 3 + j
                sent.append(_remote(ins[i].at[chip_idx[j]], outs[i].at[j], send_sems.at[k], recv_sems.at[k],
                                    (*chip, c)))
        for cp in sent:
            cp.start()
        for i in range(n):
            for j in range(len(chips)):
                k = i * 3 + j
                landed = outs[i].at[j]
                _remote(landed, landed, send_sems.at[k], recv_sems.at[k], sib).wait_recv()
        for cp in sent:
            cp.wait_send()

    return pl.pallas_call(
        body, name="rs_exchange_partials",
        out_shape=[jax.ShapeDtypeStruct((3,) + p.shape[1:], p.dtype) for p in parts],
        in_specs=[HBM_SPEC] * n, out_specs=[HBM_SPEC] * n,
        scratch_shapes=[pltpu.SemaphoreType.DMA((3 * n,)), pltpu.SemaphoreType.DMA((3 * n,))],
    )(*parts)


def _sum_partials(own_part, landed, name):
    _, h, cols = landed.shape

    def body(own_ref, a_ref, o_ref):
        acc = own_ref[...].astype(F32)
        for s in range(3):
            acc = acc + a_ref[s].astype(F32)
        o_ref[...] = acc

    return pl.pallas_call(
        body, name=name, out_shape=jax.ShapeDtypeStruct((h, cols), F32), grid=(1,),
        in_specs=[pl.BlockSpec((h, cols), lambda i: (0, 0)), pl.BlockSpec(landed.shape, lambda i: (0, 0, 0))],
        out_specs=pl.BlockSpec((h, cols), lambda i: (0, 0)),
        compiler_params=_params(("arbitrary",)),
    )(own_part, landed)


def _share_halves(halves):
    n = len(halves)

    def body(*refs):
        ins, outs = refs[:n], refs[n:2 * n]
        send_sems, recv_sems = refs[2 * n:]
        x, y, c, own, sib, chips, chip_idx = _place()
        cps = [_remote(ins[i], outs[i], send_sems.at[i], recv_sems.at[i], sib) for i in range(n)]
        for cp in cps:
            cp.start()
        for cp in cps:
            cp.wait()

    return pl.pallas_call(
        body, name="rs_share_halves",
        out_shape=[jax.ShapeDtypeStruct(p.shape, p.dtype) for p in halves],
        in_specs=[HBM_SPEC] * n, out_specs=[HBM_SPEC] * n,
        scratch_shapes=[pltpu.SemaphoreType.DMA((n,)), pltpu.SemaphoreType.DMA((n,))],
    )(*halves)


def _allreduce_small(packed):
    rows = packed.shape[0]
    n_dev = 8

    def body(in_ref, out_ref, gath, send_sems, recv_sems):
        x, y, c = lax.axis_index("x"), lax.axis_index("y"), lax.axis_index("c")
        me = 4 * x + 2 * y + c
        gath[me] = in_ref[...]
        cps = []
        for k in range(1, n_dev):
            fx, fy, fc = (k >> 2) & 1, (k >> 1) & 1, k & 1
            to = (x ^ fx, y ^ fy, c ^ fc)
            cps.append(_remote(in_ref, gath.at[me], send_sems.at[k - 1], recv_sems.at[k - 1], to))
        for cp in cps:
            cp.start()
        for k in range(1, n_dev):
            fx, fy, fc = (k >> 2) & 1, (k >> 1) & 1, k & 1
            src = 4 * (x ^ fx) + 2 * (y ^ fy) + (c ^ fc)
            slot = gath.at[src]
            _remote(slot, slot, send_sems.at[k - 1], recv_sems.at[k - 1], (x, y, c)).wait_recv()
        for cp in cps:
            cp.wait_send()
        acc = gath[0]
        for d in range(1, n_dev):
            acc = acc + gath[d]
        out_ref[...] = acc

    vm = pl.BlockSpec(memory_space=pltpu.VMEM)
    return pl.pallas_call(
        body, name="allreduce_small", out_shape=jax.ShapeDtypeStruct(packed.shape, F32),
        in_specs=[vm], out_specs=vm,
        scratch_shapes=[pltpu.VMEM((n_dev, rows, LANES), F32),
                        pltpu.SemaphoreType.DMA((n_dev - 1,)), pltpu.SemaphoreType.DMA((n_dev - 1,))],
    )(packed)


def _adam(col, w, g, m, v):
    m2 = ADAM_B1 * m + (1.0 - ADAM_B1) * g
    v2 = ADAM_B2 * v + (1.0 - ADAM_B2) * (g * g)
    m_hat = m2 / (1.0 - ADAM_B1 ** ADAM_STEP)
    v_hat = v2 / (1.0 - ADAM_B2 ** ADAM_STEP)
    delta = -ADAM_LR * (m_hat / (jnp.sqrt(v_hat) + ADAM_EPS) + ADAM_WD * w)
    return delta, m2, v2


def _adam_call(w, g, m, v, name):
    rows, cols = w.shape
    tm = rows
    for cand in (256, 352, 176, 128, 64, 48, 16, 8):
        if rows % cand == 0:
            tm = cand
            break
    return _tiles(_adam, name=name, rows=rows, tm=tm,
                  row_ins=[(w, cols, 0), (g, cols, 0), (m, cols, 0), (v, cols, 0)],
                  row_outs=[(cols, F32)] * 3)


def _adam_big(w, g_mine, g_other, m, v, place, name):
    _, rows, cols = w.shape
    h = rows // 2
    tm = next(t for t in (256, 176, 128) if h % t == 0)
    nt = h // tm

    def body(place_ref, w_ref, gm_ref, go_ref, m_ref, v_ref, g_out, d_out, m_out, v_out):
        g = jnp.where(pl.program_id(0) == place_ref[0], gm_ref[...], go_ref[...])
        d, m2, v2 = _adam(None, w_ref[...], g, m_ref[...], v_ref[...])
        g_out[...] = g
        d_out[...] = d
        m_out[...] = m2
        v_out[...] = v2

    full = pl.BlockSpec((None, tm, cols), lambda hh, i, p: (0, hh * nt + i, 0))
    half = pl.BlockSpec((tm, cols), lambda hh, i, p: (i, 0))
    return pl.pallas_call(
        body, name=name, out_shape=[jax.ShapeDtypeStruct(w.shape, F32)] * 4,
        grid_spec=pltpu.PrefetchScalarGridSpec(
            num_scalar_prefetch=1, grid=(2, nt),
            in_specs=[full, half, half, full, full], out_specs=[full] * 4),
        compiler_params=_params(("arbitrary", "arbitrary")),
    )(place, w, g_mine, g_other, m, v)


def _pack(arrays):
    flat = []
    for a in arrays:
        a = a.reshape(-1).astype(F32)
        flat.append(jnp.pad(a, (0, (-a.size) % LANES)))
    out = jnp.concatenate(flat)
    out = jnp.pad(out, (0, (-out.size) % (8 * LANES)))
    return out.reshape(-1, LANES)


def _unpack(packed, shapes):
    flat = packed.reshape(-1)
    out, off = [], 0
    for s in shapes:
        size = int(np.prod(s))
        out.append(flat[off:off + size].reshape(s))
        off += size + (-size) % LANES
    return out


def kernel(x, norm1_w, w_in, gdn_conv_w, gdn_A_log, gdn_dt_bias, gdn_out_norm_w, fox_f_bias, fox_q_norm_w, fox_k_norm_w, w_out, norm2_w, w_ffn_gate, w_ffn_up, w_ffn_down, final_norm_w, loss_target, m_norm1_w, m_w_in, m_gdn_conv_w, m_gdn_A_log, m_gdn_dt_bias, m_gdn_out_norm_w, m_fox_f_bias, m_fox_q_norm_w, m_fox_k_norm_w, m_w_out, m_norm2_w, m_w_ffn_gate, m_w_ffn_up, m_w_ffn_down, m_final_norm_w, v_norm1_w, v_w_in, v_gdn_conv_w, v_gdn_A_log, v_gdn_dt_bias, v_gdn_out_norm_w, v_fox_f_bias, v_fox_q_norm_w, v_fox_k_norm_w, v_w_out, v_norm2_w, v_w_ffn_gate, v_w_ffn_up, v_w_ffn_down, v_final_norm_w):
    cx, cy, cc = lax.axis_index("x"), lax.axis_index("y"), lax.axis_index("c")
    own = 2 * cx + cy
    place = jnp.stack([cc, own]).astype(jnp.int32)

    big_w = [w_in, w_out, w_ffn_gate, w_ffn_up, w_ffn_down]
    shards = [w[0].astype(BF16) for w in big_w]
    (w_in_g,), conv_g = _allgather_weights(shards[:1], gdn_conv_w[0])
    rest = _split_start("gather_rest_start", _gather_plan, shards[1:],
                        [jax.ShapeDtypeStruct((N_CHIPS,) + s.shape, BF16) for s in shards[1:]],
                        n_copies=4 * len(shards[1:]), after=w_in_g)
    by_cols = lambda g: g.transpose(1, 0, 2).reshape(g.shape[1], N_CHIPS * g.shape[2])
    w_cat = _cat_weights(by_cols(w_in_g))
    conv_full = by_cols(conv_g)

    def late_weights(after):
        w_out_g, w_gate_g, w_up_g, w_down_g = _split_wait("gather_rest_wait", _gather_plan, rest, after)
        return w_out_g.reshape(N_CHIPS * w_out_g.shape[1], w_out_g.shape[2]), w_gate_g, w_up_g, w_down_g

    names = ["w_in", "w_out", "w_gate", "w_up", "w_down"]

    def chip_partials(stacks, nms):
        landed = _swap_halves(stacks, "rs_swap_" + nms[0])
        return [_add_half(s, l, place, "rs_add_" + nm) for s, l, nm in zip(stacks, landed, nms)]

    ffn = {}

    def ffn_grads_ready(g_gate, g_up, g_down):
        ffn["added"] = chip_partials([g_gate, g_up, g_down], names[2:])
        parts = [a[0] for a in ffn["added"]]
        ffn["started"] = _split_start(
            "exchange_ffn_start", _exchange_plan, parts,
            [jax.ShapeDtypeStruct((3,) + p.shape[1:], p.dtype) for p in parts], n_copies=3 * len(parts),
            after=parts[0])
        return ffn["started"]["token"][0, 0]

    grad_x, g_cat, g_out, _, _, _, small = _local_step(
        x[0], loss_target[0], norm1_w + rest["token"][0, 0], w_cat, conv_full, gdn_A_log[0], gdn_dt_bias[0],
        gdn_out_norm_w[0], fox_f_bias[0], fox_q_norm_w[0], fox_k_norm_w[0], norm2_w, final_norm_w.reshape(1, -1),
        late_weights, ffn_grads_ready)

    col_stack = lambda g: g.reshape(g.shape[0], N_CHIPS, g.shape[1] // N_CHIPS).transpose(1, 0, 2)
    row_stack = lambda g: g.reshape(N_CHIPS, g.shape[0] // N_CHIPS, g.shape[1])
    added = chip_partials([col_stack(_uncat_grad(g_cat)), row_stack(g_out)], names[:2])
    from_chips = list(_exchange_partials([a[0] for a in added]))
    from_chips += list(_split_wait("exchange_ffn_wait", _exchange_plan, ffn["started"], from_chips[0]))
    added = list(added) + list(ffn["added"])
    halves = [_sum_partials(a[1], p, "rs_sum_" + nm) for a, p, nm in zip(added, from_chips, names)]
    others = _share_halves(halves)
    big_m = [m_w_in, m_w_out, m_w_ffn_gate, m_w_ffn_up, m_w_ffn_down]
    big_v = [v_w_in, v_w_out, v_w_ffn_gate, v_w_ffn_up, v_w_ffn_down]
    big_upd = [_adam_big(w, gm, go, m, v, place, "adam_" + nm)
               for w, gm, go, m, v, nm in zip(big_w, halves, others, big_m, big_v, names)]

    order = ["norm1_w", "conv_w", "a_log", "dt_bias", "out_norm_w", "f_bias", "q_norm_w", "k_norm_w",
             "norm2_w", "final_w"]
    red = _allreduce_small(_pack([small[k] for k in order] + [small["loss"]]))
    red_shapes = [(1, D_MODEL), (CONV_K, 3 * WIDTH), (1, HEADS), (1, HEADS), (1, HEAD_DIM), (1, HEADS),
                  (1, HEAD_DIM), (1, HEAD_DIM), (1, D_MODEL), (D_MODEL,), ()]
    red_list = _unpack(red, red_shapes)
    loss = red_list[-1]
    small_g = dict(zip(order, red_list[:-1]))
    shard_cols = 3 * WIDTH // N_CHIPS
    small_g["conv_w"] = lax.dynamic_slice_in_dim(small_g["conv_w"], own * shard_cols, shard_cols, axis=1)[None]
    small_w = [norm1_w, gdn_conv_w, gdn_A_log, gdn_dt_bias, gdn_out_norm_w, fox_f_bias, fox_q_norm_w,
               fox_k_norm_w, norm2_w, final_norm_w]
    small_m = [m_norm1_w, m_gdn_conv_w, m_gdn_A_log, m_gdn_dt_bias, m_gdn_out_norm_w, m_fox_f_bias,
               m_fox_q_norm_w, m_fox_k_norm_w, m_norm2_w, m_final_norm_w]
    small_v = [v_norm1_w, v_gdn_conv_w, v_gdn_A_log, v_gdn_dt_bias, v_gdn_out_norm_w, v_fox_f_bias,
               v_fox_q_norm_w, v_fox_k_norm_w, v_norm2_w, v_final_norm_w]
    small_gl = [small_g[k].reshape(w.shape) for k, w in zip(order, small_w)]
    s_delta, s_m, s_v = _adam_call(_pack(small_w), _pack(small_gl), _pack(small_m), _pack(small_v), "adam_small")
    shapes = [w.shape for w in small_w]
    s_delta, s_m, s_v = _unpack(s_delta, shapes), _unpack(s_m, shapes), _unpack(s_v, shapes)

    big_pos = {1: 0, 9: 1, 11: 2, 12: 3, 13: 4}
    small_pos = {0: 0, 2: 1, 3: 2, 4: 3, 5: 4, 6: 5, 7: 6, 8: 7, 10: 8, 14: 9}
    grads, deltas, new_m, new_v = [], [], [], []
    for pos in range(15):
        if pos in big_pos:
            b = big_pos[pos]
            g, d, m2, v2 = big_upd[b]
            grads.append(g)
            deltas.append(d)
            new_m.append(m2)
            new_v.append(v2)
        else:
            s = small_pos[pos]
            grads.append(small_gl[s])
            deltas.append(s_delta[s])
            new_m.append(s_m[s])
            new_v.append(s_v[s])
    return (loss, grad_x[None], *grads, *deltas, *new_m, *new_v)
```

```python
import jax
import jax.numpy as jnp
import numpy as np
from jax import lax
from jax.experimental import pallas as pl
from jax.experimental.pallas import tpu as pltpu

F32 = jnp.float32
BF16 = jnp.bfloat16

D_MODEL = 1024
HEADS = 8
HEAD_DIM = 64
PAIRS = HEADS // 2
WIDTH = HEADS * HEAD_DIM
CHUNK = 64
CONV_K = 4
D_FF = 2816
FF_SHARD = D_FF // 4
EPS = 1e-6
SCALE = HEAD_DIM ** -0.5
LANES = 128
N_CHIPS = 4
D_IN = 4120
D_CAT = 4224
COL_SMALL = 4096 // LANES

ADAM_LR = 0.001
ADAM_B1 = 0.9
ADAM_B2 = 0.999
ADAM_EPS = 1e-08
ADAM_WD = 0.01
ADAM_STEP = 10

VMEM_LIMIT = 56 * 1024 * 1024
MESH = pl.DeviceIdType.MESH
HIGHEST = lax.Precision.HIGHEST


def _params(sem):
    return pltpu.CompilerParams(dimension_semantics=sem, vmem_limit_bytes=VMEM_LIMIT)


_CONTRACT = {"nn": ((1,), (0,)), "nt": ((1,), (1,)), "tn": ((0,), (0,))}


def _mm(a, b, *, dims, name, out_dtype=F32, add=None, tm=1024, tn=512, tk=512):
    if dims == "nn":
        (m, k), (k2, n) = a.shape, b.shape
    elif dims == "nt":
        (m, k), (n, k2) = a.shape, b.shape
    else:
        (k, m), (k2, n) = a.shape, b.shape
    assert k == k2, (a.shape, b.shape, dims)
    tm, tn, tk = min(tm, m), min(tn, n), min(tk, k)
    assert m % tm == 0 and n % tn == 0 and k % tk == 0, (m, n, k, tm, tn, tk)
    nk = k // tk
    a_spec = (pl.BlockSpec((tk, tm), lambda i, j, kk: (kk, i)) if dims == "tn"
              else pl.BlockSpec((tm, tk), lambda i, j, kk: (i, kk)))
    b_spec = (pl.BlockSpec((tn, tk), lambda i, j, kk: (j, kk)) if dims == "nt"
              else pl.BlockSpec((tk, tn), lambda i, j, kk: (kk, j)))
    o_spec = pl.BlockSpec((tm, tn), lambda i, j, kk: (i, j))
    contract = (_CONTRACT[dims], ((), ()))
    has_add = add is not None

    def body(*refs):
        a_ref, b_ref = refs[:2]
        add_ref = refs[2] if has_add else None
        o_ref = refs[3] if has_add else refs[2]
        part = lax.dot_general(a_ref[...].astype(BF16), b_ref[...].astype(BF16), contract,
                               preferred_element_type=F32)

        def finish(r):
            if has_add:
                r = r + add_ref[...].astype(F32)
            o_ref[...] = r.astype(out_dtype)

        if nk == 1:
            finish(part)
            return
        acc = refs[-1]
        kk = pl.program_id(2)

        @pl.when(kk == 0)
        def _():
            acc[...] = part

        @pl.when(kk > 0)
        def _():
            acc[...] += part

        @pl.when(kk == nk - 1)
        def _():
            finish(acc[...])

    ins = [a, b] + ([add] if has_add else [])
    in_specs = [a_spec, b_spec] + ([o_spec] if has_add else [])
    return pl.pallas_call(
        body, name=name, grid=(m // tm, n // tn, nk),
        in_specs=in_specs, out_specs=o_spec,
        out_shape=jax.ShapeDtypeStruct((m, n), out_dtype),
        scratch_shapes=[pltpu.VMEM((tm, tn), F32)] if nk > 1 else [],
        compiler_params=_params(("parallel", "parallel", "arbitrary")),
    )(*ins)


def _mm_blocks(a, b, *, name, grid, a_spec, b_spec, o_spec, out_shape, dims, n_sum=0, add=None, add_spec=None):
    contract = (_CONTRACT[dims], ((), ()))
    has_add = add is not None

    def body(*refs):
        a_ref, b_ref = refs[:2]
        o_ref = refs[-1]
        dot = lambda x, y: lax.dot_general(x.astype(BF16), y.astype(BF16), contract, preferred_element_type=F32)
        if n_sum:
            r = dot(a_ref[0], b_ref[0])
            for s in range(1, n_sum):
                r = r + dot(a_ref[s], b_ref[s])
        else:
            r = dot(a_ref[...], b_ref[...])
        if has_add:
            r = r + refs[2][...].astype(F32)
        o_ref[...] = r.astype(o_ref.dtype)

    return pl.pallas_call(
        body, name=name, grid=grid,
        in_specs=[a_spec, b_spec] + ([add_spec] if has_add else []), out_specs=o_spec, out_shape=out_shape,
        compiler_params=_params(("parallel",) * len(grid)),
    )(*([a, b] + ([add] if has_add else [])))


def _tiles(fn, *, name, rows, tm, ncol=1, row_ins=(), col_consts=(), full_consts=(),
           row_outs=(), acc_outs=()):
    nt = rows // tm
    assert rows % tm == 0
    n_full, n_col, n_row = len(full_consts), len(col_consts), len(row_ins)
    n_ro, n_acc = len(row_outs), len(acc_outs)

    def body(*refs):
        ins = refs[:n_full + n_col + n_row]
        outs = refs[n_full + n_col + n_row:]
        i = pl.program_id(1)
        res = fn(pl.program_id(0), *[r[...] for r in ins])
        for r, v in zip(outs[:n_ro], res[:n_ro]):
            r[...] = v.astype(r.dtype)
        if n_acc:
            @pl.when(i == 0)
            def _():
                for r in outs[n_ro:]:
                    r[...] = jnp.zeros_like(r)
            for r, v in zip(outs[n_ro:], res[n_ro:]):
                r[...] += v

    in_specs = [pl.BlockSpec(a.shape, lambda j, i, nd=a.ndim: (0,) * nd) for a in full_consts]
    in_specs += [pl.BlockSpec((nr, w), lambda j, i, o=o: (0, o + j)) for (_, nr, w, o) in col_consts]
    in_specs += [pl.BlockSpec((tm, w), lambda j, i, o=o: (i, o + j)) for (_, w, o) in row_ins]
    out_specs = [pl.BlockSpec((tm, w), lambda j, i: (i, j)) for (w, _) in row_outs]
    out_specs += [pl.BlockSpec((nr, w), lambda j, i: (0, j)) for (nr, w) in acc_outs]
    out_shape = [jax.ShapeDtypeStruct((rows, w * ncol), dt) for (w, dt) in row_outs]
    out_shape += [jax.ShapeDtypeStruct((nr, w * ncol), F32) for (nr, w) in acc_outs]
    args = list(full_consts) + [c[0] for c in col_consts] + [r[0] for r in row_ins]
    out = pl.pallas_call(
        body, name=name, grid=(ncol, nt), in_specs=in_specs, out_specs=out_specs, out_shape=out_shape,
        compiler_params=_params(("parallel", "arbitrary")),
    )(*args)
    return out


def _rms(x, w):
    return x * lax.rsqrt(jnp.mean(x * x, axis=-1, keepdims=True) + EPS) * w


def _lane_lo(shape):
    return lax.broadcasted_iota(jnp.int32, shape, len(shape) - 1) < HEAD_DIM


def _pair_sum(x):
    lo = _lane_lo(x.shape)
    s0 = jnp.sum(jnp.where(lo, x, 0.0), axis=-1, keepdims=True)
    s1 = jnp.sum(jnp.where(lo, 0.0, x), axis=-1, keepdims=True)
    return jnp.where(lo, s0, s1)


def _head_col(x, lo, h):
    keep = lo if h == 0 else jnp.logical_not(lo)
    return jnp.max(jnp.where(keep, x, -jnp.inf), axis=-1, keepdims=True)


def _softplus(x):
    return jnp.maximum(x, 0.0) + jnp.log1p(jnp.exp(-jnp.abs(x)))


def _silu(x):
    return x * jax.nn.sigmoid(x)


def _dot(a, b, contract):
    return lax.dot_general(a.astype(BF16), b.astype(BF16), (contract, ((), ())),
                           preferred_element_type=F32)


def _dot32(a, b, contract):
    return lax.dot_general(a, b, (contract, ((), ())), precision=HIGHEST, preferred_element_type=F32)


def _bd(y):
    yy = jnp.concatenate([y, y], axis=0)
    r = lax.broadcasted_iota(jnp.int32, yy.shape, 0) < HEAD_DIM
    c = lax.broadcasted_iota(jnp.int32, yy.shape, 1) < HEAD_DIM
    return jnp.where(r == c, yy, 0.0)


def _pp(x, y):
    return _dot(x, _bd(y), _CONTRACT["nn"])


def _pp_nt(x, y):
    return _dot(x, _bd(y), _CONTRACT["nt"])


def _pp_tn(x, y):
    full = _dot(x, y, _CONTRACT["tn"])
    return jnp.where(_lane_lo((HEAD_DIM, LANES)), full[:HEAD_DIM], full[HEAD_DIM:])


def _gdn_masks():
    row = lax.broadcasted_iota(jnp.int32, (CHUNK, LANES), 0)
    col = lax.broadcasted_iota(jnp.int32, (CHUNK, LANES), 1) % HEAD_DIM
    return row, col


def _interleave(chains):
    live = list(chains)
    while live:
        for g in list(live):
            try:
                next(g)
            except StopIteration:
                live.remove(g)


def _gdn_forward(qkv, betax, gcx, grow, rows):
    nchunk = rows // CHUNK

    def body(q_ref, k_ref, v_ref, bx_ref, gx_ref, gr_ref, o_ref, ss_ref, ts_ref, state):
        n = pl.program_id(0)

        @pl.when(n == 0)
        def _():
            state[...] = jnp.zeros_like(state)

        row, col = _gdn_masks()
        incl, strict = col <= row, col < row

        def chain(p):
            lanes = pl.ds(p * LANES, LANES)
            q, k, v, bx, gx = q_ref[:, lanes], k_ref[:, lanes], v_ref[:, lanes], bx_ref[:, lanes], gx_ref[:, lanes]
            gr = gr_ref[0, p]
            glast = gx_ref[pl.ds(CHUNK - 1, 1), lanes]
            s = state[p]
            dm = jnp.where(incl, jnp.exp(jnp.minimum(gx - gr, 0.0)), 0.0)
            kb, vb, eg, qs = k * bx, v * bx, jnp.exp(gx), q * SCALE
            yield
            big_g, big_p = _pp_nt(kb, k), _pp_nt(qs, k)
            yield
            x = -jnp.where(strict, big_g * dm, 0.0)
            att = jnp.where(incl, big_p * dm, 0.0)
            tm = jnp.where(row == col, 1.0, 0.0) + x
            x = _pp(x, x)
            yield
            for _ in range(4):
                step, x = _pp(tm, x), _pp(x, x)
                yield
                tm = tm + step
            tm = tm + _pp(tm, x)
            yield
            u, w = _pp(tm, vb), _pp(tm, kb * eg)
            yield
            ws, qgs = _pp(w, s), _pp(qs * eg, s)
            yield
            vn = u - ws
            kd = k * jnp.exp(glast - gx)
            avn, upd = _pp(att, vn), _pp_tn(kd, vn)
            yield
            ss_ref[0, p] = s
            ts_ref[0, p] = tm
            o_ref[:, lanes] = qgs + avn
            state[p] = s * jnp.exp(glast) + upd

        _interleave([chain(p) for p in range(PAIRS)])

    blk = lambda j: pl.BlockSpec((CHUNK, WIDTH), lambda n, j=j: (n, j))
    sv = pl.BlockSpec((1, PAIRS, CHUNK, LANES), lambda n: (n, 0, 0, 0))
    return pl.pallas_call(
        body, name="gdn_fwd", grid=(nchunk,),
        in_specs=[blk(0), blk(1), blk(2), blk(0), blk(0),
                  pl.BlockSpec((1, PAIRS, 1, LANES), lambda n: (n, 0, 0, 0))],
        out_specs=[blk(0), sv, sv],
        out_shape=[jax.ShapeDtypeStruct((rows, WIDTH), F32),
                   jax.ShapeDtypeStruct((nchunk, PAIRS, CHUNK, LANES), F32),
                   jax.ShapeDtypeStruct((nchunk, PAIRS, CHUNK, LANES), F32)],
        scratch_shapes=[pltpu.VMEM((PAIRS, CHUNK, LANES), F32)],
        compiler_params=_params(("arbitrary",)),
    )(qkv, qkv, qkv, betax, gcx, grow)


def _gdn_backward(qkv, betax, gcx, grow, ssave, tsave, do, rows):
    nchunk = rows // CHUNK

    def body(q_ref, k_ref, v_ref, bx_ref, gx_ref, gr_ref, ss_ref, ts_ref, do_ref,
             dq_ref, dk_ref, dv_ref, dbx_ref, dgx_ref, dgr_ref, dstate):
        n = pl.program_id(0)

        @pl.when(n == 0)
        def _():
            dstate[...] = jnp.zeros_like(dstate)

        row, col = _gdn_masks()
        incl, strict = col <= row, col < row

        def chain(p):
            lanes = pl.ds(p * LANES, LANES)
            q, k, v, bx, gx = q_ref[:, lanes], k_ref[:, lanes], v_ref[:, lanes], bx_ref[:, lanes], gx_ref[:, lanes]
            gr = gr_ref[0, p]
            glast = gx_ref[pl.ds(CHUNK - 1, 1), lanes]
            s, tm, d_o = ss_ref[0, p], ts_ref[0, p], do_ref[:, lanes]
            ds_out = dstate[p]
            dm = jnp.where(incl, jnp.exp(jnp.minimum(gx - gr, 0.0)), 0.0)
            kb, vb, eg, qs = k * bx, v * bx, jnp.exp(gx), q * SCALE
            kbg, qg = kb * eg, qs * eg
            ed = jnp.exp(glast - gx)
            kd = k * ed
            eglast = jnp.exp(glast)
            yield
            big_g, big_p = _pp_nt(kb, k), _pp_nt(qs, k)
            u, w = _pp(tm, vb), _pp(tm, kbg)
            dqg, kds = _pp_nt(d_o, s), _pp(kd, ds_out)
            yield
            low = jnp.where(strict, big_g * dm, 0.0)
            att = jnp.where(incl, big_p * dm, 0.0)
            ws, atd = _pp(w, s), _pp_tn(att, d_o)
            yield
            vn = u - ws
            dvn = kds + atd
            dkd, datt_raw = _pp_nt(vn, ds_out), _pp_nt(d_o, vn)
            dw_neg, dvb = _pp_nt(dvn, s), _pp_tn(tm, dvn)
            dtm_a, wdv = _pp_nt(dvn, vb), _pp_tn(w, dvn)
            qgd = _pp_tn(qg, d_o)
            yield
            datt = jnp.where(incl, datt_raw, 0.0)
            dw = -dw_neg
            dtm_b, dkbg = _pp_nt(dw, kbg), _pp_tn(tm, dw)
            dbig_p = datt * dm
            dqs_a, dk_p = _pp(dbig_p, k), _pp_tn(dbig_p, qs)
            yield
            inner = _pp_tn(tm, dtm_a + dtm_b)
            yield
            dlow = jnp.where(strict, -_pp_nt(inner, tm), 0.0)
            yield
            dbig_g = dlow * dm
            dkb_a, dk_g = _pp(dbig_g, k), _pp_tn(dbig_g, kb)
            yield
            dkb = dkb_a + dkbg * eg
            dqs = dqs_a + dqg * eg
            dk = dk_g + dk_p + dkd * ed + dkb * bx
            z = dlow * low + datt * att
            kdterm = dkd * kd
            dglast = (jnp.sum(ds_out * s, axis=0, keepdims=True) * eglast
                      + jnp.sum(kdterm, axis=0, keepdims=True))
            dgx = dqg * qg + dkbg * kbg - kdterm
            dgx = dgx + jnp.where(col == 0, _pair_sum(z), 0.0)
            dgx = dgx + jnp.where(row == CHUNK - 1, dglast, 0.0)
            dq_ref[:, lanes] = dqs * SCALE
            dk_ref[:, lanes] = dk
            dv_ref[:, lanes] = dvb * bx
            dbx_ref[:, lanes] = dkb * k + dvb * v
            dgx_ref[:, lanes] = dgx
            dgr_ref[0, p] = -jnp.sum(z, axis=0, keepdims=True)
            dstate[p] = ds_out * eglast + qgd - wdv

        _interleave([chain(p) for p in range(PAIRS)])

    last = nchunk - 1
    blk = lambda j: pl.BlockSpec((CHUNK, WIDTH), lambda n, j=j: (last - n, j))
    sv = pl.BlockSpec((1, PAIRS, CHUNK, LANES), lambda n: (last - n, 0, 0, 0))
    gr_spec = pl.BlockSpec((1, PAIRS, 1, LANES), lambda n: (last - n, 0, 0, 0))
    wide = jax.ShapeDtypeStruct((rows, WIDTH), F32)
    return pl.pallas_call(
        body, name="gdn_bwd", grid=(nchunk,),
        in_specs=[blk(0), blk(1), blk(2), blk(0), blk(0), gr_spec, sv, sv, blk(0)],
        out_specs=[blk(0)] * 5 + [gr_spec],
        out_shape=[wide] * 5 + [jax.ShapeDtypeStruct((nchunk, PAIRS, 1, LANES), F32)],
        scratch_shapes=[pltpu.VMEM((PAIRS, CHUNK, LANES), F32)],
        compiler_params=_params(("arbitrary",)),
    )(qkv, qkv, qkv, betax, gcx, grow, ssave, tsave, do)


ATT_TQ = 256


def _att_scores(qh, kt, fk, diag):
    s = _dot(qh, kt, _CONTRACT["nt"]) - fk
    if diag:
        r = lax.broadcasted_iota(jnp.int32, s.shape, 0)
        c = lax.broadcasted_iota(jnp.int32, s.shape, 1)
        s = jnp.where(r >= c, s, -jnp.inf)
    return s


def _head_masks(n):
    lo = _lane_lo((n, LANES))
    return [lo, jnp.logical_not(lo)]


def _attention_forward(fqk, proj, frow, rows):
    tq = tk = min(ATT_TQ, rows)
    nq = rows // tq
    v_off = 3072 // LANES

    def body(q_ref, k_ref, v_ref, fr_ref, o_ref, lse_ref):
        qi = pl.program_id(1)
        q = q_ref[...] * SCALE
        keep_q, keep_k = _head_masks(tq), _head_masks(tk)
        qh = [jnp.where(keep_q[h], q, 0.0).astype(BF16) for h in range(2)]

        def tile(ki, carry, diag):
            k0 = pl.multiple_of(ki * tk, tk)
            kt = k_ref[pl.ds(k0, tk), :].astype(BF16)
            v_t = v_ref[pl.ds(k0, tk), :]
            out = []
            for h in range(2):
                m, l, acc = carry[h]
                vt = jnp.where(keep_k[h], v_t, 0.0).astype(BF16)
                s = _att_scores(qh[h], kt, fr_ref[0, pl.ds(h, 1), pl.ds(k0, tk)], diag)
                m_new = jnp.maximum(m, jnp.max(s, axis=-1, keepdims=True))
                p = jnp.exp(s - m_new)
                alpha = jnp.exp(m - m_new)
                l = alpha * l + jnp.sum(p, axis=-1, keepdims=True)
                acc = alpha * acc + _dot(p, vt, _CONTRACT["nn"])
                out.append((m_new, l, acc))
            return tuple(out)

        one = (jnp.full((tq, 1), -jnp.inf, F32), jnp.zeros((tq, 1), F32), jnp.zeros((tq, LANES), F32))
        carry = lax.fori_loop(0, qi, lambda ki, c: tile(ki, c, False), (one, one))
        (m0, l0, acc0), (m1, l1, acc1) = tile(qi, carry, True)
        o_ref[...] = acc0 / l0 + acc1 / l1
        lse_ref[...] = jnp.where(keep_q[0], m0 + jnp.log(l0), m1 + jnp.log(l1))

    whole = lambda off: pl.BlockSpec((rows, LANES), lambda p, i, off=off: (0, off + p))
    qblk = lambda off: pl.BlockSpec((tq, LANES), lambda p, i, off=off: (i, off + p))
    wide = jax.ShapeDtypeStruct((rows, WIDTH), F32)
    return pl.pallas_call(
        body, name="fox_fwd", grid=(PAIRS, nq),
        in_specs=[qblk(0), whole(PAIRS), whole(v_off), pl.BlockSpec((1, 2, rows), lambda p, i: (p, 0, 0))],
        out_specs=[qblk(0), qblk(0)], out_shape=[wide, wide],
        compiler_params=_params(("parallel", "arbitrary")),
    )(fqk, fqk, proj, frow)


def _attention_delta(fqk, proj, frow, lse, dao, rows):
    tq = tk = min(ATT_TQ, rows)
    nq = rows // tq
    v_off = 3072 // LANES

    def body(q_ref, k_ref, v_ref, fr_ref, lse_ref, do_ref, delta_ref):
        qi = pl.program_id(1)
        q, d_o, lse_t = q_ref[...] * SCALE, do_ref[...], lse_ref[...]
        keep_q = _head_masks(tq)
        qh = [jnp.where(keep_q[h], q, 0.0).astype(BF16) for h in range(2)]
        doh = [jnp.where(keep_q[h], d_o, 0.0).astype(BF16) for h in range(2)]
        lse_h = [_head_col(lse_t, keep_q[0], h) for h in range(2)]

        def tile(ki, carry, diag):
            k0 = pl.multiple_of(ki * tk, tk)
            kt = k_ref[pl.ds(k0, tk), :].astype(BF16)
            vt = v_ref[pl.ds(k0, tk), :].astype(BF16)
            out = []
            for h in range(2):
                s = _att_scores(qh[h], kt, fr_ref[0, pl.ds(h, 1), pl.ds(k0, tk)], diag)
                dp = _dot(doh[h], vt, _CONTRACT["nt"])
                out.append(carry[h] + jnp.sum(jnp.exp(s - lse_h[h]) * dp, axis=-1, keepdims=True))
            return tuple(out)

        zero = jnp.zeros((tq, 1), F32)
        carry = lax.fori_loop(0, qi, lambda ki, c: tile(ki, c, False), (zero, zero))
        d0, d1 = tile(qi, carry, True)
        delta_ref[...] = jnp.where(keep_q[0], d0, d1)

    whole = lambda off: pl.BlockSpec((rows, LANES), lambda p, i, off=off: (0, off + p))
    qblk = lambda off: pl.BlockSpec((tq, LANES), lambda p, i, off=off: (i, off + p))
    return pl.pallas_call(
        body, name="fox_delta", grid=(PAIRS, nq),
        in_specs=[qblk(0), whole(PAIRS), whole(v_off),
                  pl.BlockSpec((1, 2, rows), lambda p, i: (p, 0, 0)), qblk(0), qblk(0)],
        out_specs=qblk(0), out_shape=jax.ShapeDtypeStruct((rows, WIDTH), F32),
        compiler_params=_params(("parallel", "arbitrary")),
    )(fqk, fqk, proj, frow, lse, dao)


def _attention_backward(fqk, proj, frow, delta, lse, dao, rows):
    tq = tk = min(ATT_TQ, rows)
    nq = rows // tq
    v_off = 3072 // LANES

    def body(q_ref, k_ref, v_ref, fr_ref, delta_ref, lse_ref, do_ref, dq_ref, dk_ref, dv_ref, dfr_ref):
        ki = pl.program_id(1)

        @pl.when(ki == 0)
        def _():
            dq_ref[...] = jnp.zeros_like(dq_ref)

        keep_q, keep_k = _head_masks(tq), _head_masks(tk)
        k_t = k_ref[...]
        kt = k_t.astype(BF16)
        vt = v_ref[...].astype(BF16)
        kh = [jnp.where(keep_k[h], k_t, 0.0).astype(BF16) for h in range(2)]
        fk = [fr_ref[0, pl.ds(h, 1), :] for h in range(2)]

        def tile(qi, carry, diag):
            dk, dv, df0, df1 = carry
            rows_q = pl.ds(pl.multiple_of(qi * tq, tq), tq)
            q, d_o, delta_x, lse_t = q_ref[rows_q, :] * SCALE, do_ref[rows_q, :], delta_ref[rows_q, :], lse_ref[rows_q, :]
            dq = jnp.zeros((tq, LANES), F32)
            dfs = []
            for h in range(2):
                qh = jnp.where(keep_q[h], q, 0.0).astype(BF16)
                doh = jnp.where(keep_q[h], d_o, 0.0).astype(BF16)
                s = _att_scores(qh, kt, fk[h], diag)
                p = jnp.exp(s - _head_col(lse_t, keep_q[0], h))
                dp = _dot(doh, vt, _CONTRACT["nt"])
                ds = p * (dp - _head_col(delta_x, keep_q[0], h))
                dv = dv + _dot(p, doh, _CONTRACT["tn"])
                dk = dk + _dot(ds, qh, _CONTRACT["tn"])
                dq = dq + _dot(ds, kh[h], _CONTRACT["nn"])
                dfs.append(-jnp.sum(ds, axis=0, keepdims=True))
            dq_ref[rows_q, :] += dq * SCALE
            return dk, dv, df0 + dfs[0], df1 + dfs[1]

        zero_kv = jnp.zeros((tk, LANES), F32)
        zero_f = jnp.zeros((1, tk), F32)
        carry = tile(ki, (zero_kv, zero_kv, zero_f, zero_f), True)
        dk, dv, df0, df1 = lax.fori_loop(ki + 1, nq, lambda qi, c: tile(qi, c, False), carry)
        dk_ref[...] = dk
        dv_ref[...] = dv.astype(dv_ref.dtype)
        dfr_ref[0, pl.ds(0, 1), :] = df0
        dfr_ref[0, pl.ds(1, 1), :] = df1

    whole = lambda off: pl.BlockSpec((rows, LANES), lambda p, i, off=off: (0, off + p))
    kblk = lambda off: pl.BlockSpec((tk, LANES), lambda p, i, off=off: (i, off + p))
    fr_spec = pl.BlockSpec((1, 2, tk), lambda p, i: (p, 0, i))
    wide = jax.ShapeDtypeStruct((rows, WIDTH), F32)
    return pl.pallas_call(
        body, name="fox_bwd", grid=(PAIRS, nq),
        in_specs=[whole(0), kblk(PAIRS), kblk(v_off), fr_spec, whole(0), whole(0), whole(0)],
        out_specs=[whole(0), kblk(0), kblk(0), fr_spec],
        out_shape=[wide, wide, jax.ShapeDtypeStruct((rows, WIDTH), BF16),
                   jax.ShapeDtypeStruct((PAIRS, 2, rows), F32)],
        compiler_params=_params(("parallel", "arbitrary")),
    )(fqk, fqk, proj, frow, delta, lse, dao)


def _lane_ids(shape):
    return lax.broadcasted_iota(jnp.int32, shape, len(shape) - 1)


def _gates_elem(a_log, dt_bias, f_bias, pre):
    lane = _lane_ids(pre.shape)
    beta = jax.nn.sigmoid(pre)
    g = -jnp.exp(a_log) * _softplus(pre + dt_bias)
    lf = -_softplus(-(pre + f_bias))
    return jnp.where(lane < 8, beta, jnp.where(lane < 16, g, jnp.where(lane < 24, lf, 0.0)))


def _tri_consts():
    r = np.arange(LANES)[:, None]
    c = np.arange(LANES)[None, :]
    full = (c <= r).astype(np.float32)
    chunked = full * ((r // CHUNK) == (c // CHUNK))
    return jnp.asarray(chunked), jnp.asarray(full)


def _cums_fwd(lc, lf, gates):
    rows = gates.shape[0]
    lane = _lane_ids((LANES, LANES))
    carry = jnp.zeros((1, LANES), F32)
    out = []
    for r in range(rows // LANES):
        blk = gates[r * LANES:(r + 1) * LANES]
        gc = _dot32(lc, blk, _CONTRACT["nn"])
        f = _dot32(lf, blk, _CONTRACT["nn"]) + carry
        carry = carry + jnp.sum(blk, axis=0, keepdims=True)
        out.append(jnp.where((lane >= 8) & (lane < 16), gc, jnp.where((lane >= 16) & (lane < 24), f, 0.0)))
    return jnp.concatenate(out, axis=0)


def _cums_bwd(lc, lf, dcums):
    rows = dcums.shape[0]
    lane = _lane_ids((LANES, LANES))
    is_g = (lane >= 8) & (lane < 16)
    is_f = (lane >= 16) & (lane < 24)
    carry = jnp.zeros((1, LANES), F32)
    out = [None] * (rows // LANES)
    for r in reversed(range(rows // LANES)):
        blk = dcums[r * LANES:(r + 1) * LANES]
        dg = jnp.where(is_g, blk, 0.0)
        df = jnp.where(is_f, blk, 0.0)
        out[r] = _dot32(lc, dg, _CONTRACT["tn"]) + _dot32(lf, df, _CONTRACT["tn"]) + carry
        carry = carry + jnp.sum(df, axis=0, keepdims=True)
    return jnp.concatenate(out, axis=0)


def _expand_consts():
    xb = np.zeros((LANES, WIDTH), np.float32)
    xg = np.zeros((LANES, WIDTH), np.float32)
    for h in range(HEADS):
        xb[h, h * HEAD_DIM:(h + 1) * HEAD_DIM] = 1.0
        xg[8 + h, h * HEAD_DIM:(h + 1) * HEAD_DIM] = 1.0
    return jnp.asarray(xb), jnp.asarray(xg)


def _shift_down(x, s):
    if s == 0:
        return x
    row = lax.broadcasted_iota(jnp.int32, x.shape, 0)
    return jnp.where(row >= s, pltpu.roll(x, s, 0), 0.0)


def _shift_up(x, s):
    if s == 0:
        return x
    n = x.shape[0]
    row = lax.broadcasted_iota(jnp.int32, x.shape, 0)
    return jnp.where(row < n - s, pltpu.roll(x, n - s, 0), 0.0)


def _row_of(cw, i):
    row = lax.broadcasted_iota(jnp.int32, cw.shape, 0)
    return jnp.sum(jnp.where(row == i, cw, 0.0), axis=0, keepdims=True)


def _conv(cw, x):
    c = jnp.zeros_like(x)
    for i in range(CONV_K):
        c = c + _row_of(cw, i) * _shift_down(x, CONV_K - 1 - i)
    return c


def _post_conv(is_qk, c):
    s = _silu(c)
    n = s * lax.rsqrt(_pair_sum(s * s) + EPS)
    return jnp.where(is_qk, n, s)


def _gdn_prep_fwd(col, cw, x):
    return (_post_conv(col < 2 * PAIRS, _conv(cw, x)),)


def _gdn_prep_bwd(is_qk, cw, x, dy):
    c = _conv(cw, x)
    _, vjp = jax.vjp(lambda cc: _post_conv(is_qk, cc), c)
    (dc,) = vjp(dy)
    dx = jnp.zeros_like(x)
    row = lax.broadcasted_iota(jnp.int32, cw.shape, 0)
    dcw = jnp.zeros(cw.shape, F32)
    for i in range(CONV_K):
        s = CONV_K - 1 - i
        dx = dx + _row_of(cw, i) * _shift_up(dc, s)
        dcw = dcw + jnp.where(row == i, jnp.sum(dc * _shift_down(x, s), axis=0, keepdims=True), 0.0)
    return dx, dcw


def _head_rms(w, x):
    return x * lax.rsqrt(_pair_sum(x * x) / HEAD_DIM + EPS) * w


def _cat_weights(w_in_t):
    tail = jnp.pad(w_in_t[4112:4120], ((0, D_CAT - D_IN), (0, 0)))
    return jnp.concatenate([w_in_t[:2048], w_in_t[2064:4112], w_in_t[2048:2064], tail], axis=0)


def _uncat_grad(g):
    return jnp.concatenate([g[:2048], g[4096:4112], g[2048:4096], g[4112:4120]], axis=0)


def _lanes_to_rowform(v8, rows):
    return v8.reshape(rows // CHUNK, CHUNK, HEADS).transpose(0, 2, 1).reshape(rows // CHUNK, PAIRS, 1, LANES)


def _rowform_to_lanes(v, rows):
    return v.reshape(rows // CHUNK, HEADS, CHUNK).transpose(0, 2, 1).reshape(rows, HEADS)


def _local_step(x, target, norm1_w, w_cat, conv_w, a_log, dt_bias, out_norm_w, f_bias, q_norm_w, k_norm_w,
                norm2_w, final_w, late_weights, early_grads_ready):
    rows = x.shape[0]
    tm = min(256, rows)
    lc, lf = _tri_consts()
    xb, xg = _expand_consts()

    (h1,) = _tiles(lambda col, w, xx: (_rms(xx, w),), name="norm1", rows=rows, tm=tm,
                   full_consts=[norm1_w], row_ins=[(x, D_MODEL, 0)], row_outs=[(D_MODEL, BF16)])
    proj = _mm(h1, w_cat, dims="nt", name="in_proj", tn=384, tk=1024)

    lane_pad = lambda v, off: jnp.pad(v.reshape(1, -1), ((0, 0), (off, LANES - off - v.size)))
    p_a, p_dt, p_fb = lane_pad(a_log, 8), lane_pad(dt_bias, 8), lane_pad(f_bias, 16)

    def gates_fwd(col, lcv, lfv, a, dt, fb, pre):
        gates = _gates_elem(a, dt, fb, pre)
        return gates, _cums_fwd(lcv, lfv, gates)

    gates, cums = _tiles(gates_fwd, name="gates", rows=rows, tm=rows,
                         full_consts=[lc, lf, p_a, p_dt, p_fb], row_ins=[(proj, LANES, COL_SMALL)],
                         row_outs=[(LANES, F32), (LANES, F32)])

    def expand_fwd(col, b, g, gt, cm):
        return (_dot32(gt, b, _CONTRACT["nn"]), _dot32(cm, g, _CONTRACT["nn"]))

    betax, gcx = _tiles(expand_fwd, name="expand", rows=rows, tm=tm, full_consts=[xb, xg],
                        row_ins=[(gates, LANES, 0), (cums, LANES, 0)],
                        row_outs=[(WIDTH, F32)] * 2)
    grow = _lanes_to_rowform(cums[:, 8:16], rows)
    frow = cums[:, 16:24].T.reshape(PAIRS, 2, rows)

    (qkv,) = _tiles(_gdn_prep_fwd, name="gdn_prep", rows=rows, tm=rows, ncol=3 * PAIRS,
                    col_consts=[(conv_w, CONV_K, LANES, 0)], row_ins=[(proj, LANES, 0)],
                    row_outs=[(LANES, F32)])
    o_gdn, ssave, tsave = _gdn_forward(qkv, betax, gcx, grow, rows)

    w_qk = jnp.concatenate([jnp.tile(q_norm_w.reshape(1, -1), (1, HEADS)),
                            jnp.tile(k_norm_w.reshape(1, -1), (1, HEADS))], axis=1)
    fox_off = 2048 // LANES
    (fqk,) = _tiles(lambda col, w, xx: (_head_rms(w, xx),), name="fox_prep", rows=rows, tm=rows, ncol=2 * PAIRS,
                    col_consts=[(w_qk, 1, LANES, 0)], row_ins=[(proj, LANES, fox_off)],
                    row_outs=[(LANES, F32)])
    ao, lse = _attention_forward(fqk, proj, frow, rows)

    w_on = jnp.tile(out_norm_w.reshape(1, -1), (1, 2))
    z_off, fg_off = 1536 // LANES, 3584 // LANES
    mix_g_fn = lambda w, o, z: _head_rms(w, o) * _silu(z)
    mix_f_fn = lambda a, g: a * jax.nn.sigmoid(g)
    (mix_g,) = _tiles(lambda col, w, o, z: (mix_g_fn(w, o, z),), name="mix_gdn", rows=rows, tm=rows, ncol=PAIRS,
                      full_consts=[w_on], row_ins=[(o_gdn, LANES, 0), (proj, LANES, z_off)],
                      row_outs=[(LANES, BF16)])
    (mix_f,) = _tiles(lambda col, a, g: (mix_f_fn(a, g),), name="mix_fox", rows=rows, tm=rows, ncol=PAIRS,
                      row_ins=[(ao, LANES, 0), (proj, LANES, fg_off)], row_outs=[(LANES, BF16)])
    mix = jnp.concatenate([mix_g, mix_f], axis=1)
    w_out, w_gate, w_up, w_down = late_weights(mix)
    x1 = _mm(mix, w_out, dims="nn", name="out_proj", add=x, tk=1024)

    (h2,) = _tiles(lambda col, w, xx: (_rms(xx, w),), name="norm2", rows=rows, tm=tm,
                   full_consts=[norm2_w], row_ins=[(x1, D_MODEL, 0)], row_outs=[(D_MODEL, BF16)])
    t_rows, t_cols, t_act = min(1024, rows), 512, min(512, rows)
    n_rt = rows // t_rows
    st_act = jax.ShapeDtypeStruct((N_CHIPS, rows, FF_SHARD), F32)
    st_rows = pl.BlockSpec((None, t_rows, FF_SHARD), lambda i, j: (j, i, 0))
    out_rows = pl.BlockSpec((t_rows, t_cols), lambda i, n: (i, n))
    flat = lambda t: t.reshape(N_CHIPS * rows, FF_SHARD)

    def ffn_in(w_st, name):
        return _mm_blocks(h2, w_st, name=name, grid=(n_rt, N_CHIPS), dims="nt",
                          a_spec=pl.BlockSpec((t_rows, D_MODEL), lambda i, j: (i, 0)),
                          b_spec=pl.BlockSpec((None, FF_SHARD, D_MODEL), lambda i, j: (j, 0, 0)),
                          o_spec=st_rows, out_shape=st_act)

    gate, up = ffn_in(w_gate, "ffn_gate"), ffn_in(w_up, "ffn_up")
    act_fn = lambda g, u: _silu(g) * u
    (act,) = _tiles(lambda col, g, u: (act_fn(g, u),), name="ffn_act", rows=N_CHIPS * rows, tm=t_act,
                    row_ins=[(flat(gate), FF_SHARD, 0), (flat(up), FF_SHARD, 0)], row_outs=[(FF_SHARD, BF16)])
    act = act.reshape(st_act.shape)
    x2 = _mm_blocks(act, w_down, name="ffn_down", grid=(n_rt, D_MODEL // t_cols), dims="nn", n_sum=N_CHIPS,
                    a_spec=pl.BlockSpec((N_CHIPS, t_rows, FF_SHARD), lambda i, n: (0, i, 0)),
                    b_spec=pl.BlockSpec((N_CHIPS, FF_SHARD, t_cols), lambda i, n: (0, 0, n)),
                    o_spec=out_rows, out_shape=jax.ShapeDtypeStruct((rows, D_MODEL), F32),
                    add=x1, add_spec=out_rows)

    def final_fn(col, w, xx, tgt):
        y, vjp = jax.vjp(_rms, xx, w)
        err = y - tgt
        loss = 0.5 * jnp.sum(err * err) / D_MODEL
        dx, dw = vjp(err / D_MODEL)
        return dx, dx, jnp.full((1, LANES), loss, F32), dw

    dx2, dx2_b, loss, d_final_w = _tiles(final_fn, name="final_loss", rows=rows, tm=tm, full_consts=[final_w],
                                         row_ins=[(x2, D_MODEL, 0), (target, D_MODEL, 0)],
                                         row_outs=[(D_MODEL, F32), (D_MODEL, BF16)],
                                         acc_outs=[(1, LANES), (1, D_MODEL)])

    dact = _mm_blocks(dx2_b, w_down, name="d_act", grid=(n_rt, N_CHIPS), dims="nt",
                      a_spec=pl.BlockSpec((t_rows, D_MODEL), lambda i, j: (i, 0)),
                      b_spec=pl.BlockSpec((None, FF_SHARD, D_MODEL), lambda i, j: (j, 0, 0)),
                      o_spec=st_rows, out_shape=st_act)
    def g_ffn(d_st, other, name):
        return _mm_blocks(d_st, other, name=name, grid=(N_CHIPS, D_MODEL // t_cols), dims="tn",
                          a_spec=pl.BlockSpec((None, rows, FF_SHARD), lambda j, n: (j, 0, 0)),
                          b_spec=pl.BlockSpec((rows, t_cols), lambda j, n: (0, n)),
                          o_spec=pl.BlockSpec((None, FF_SHARD, t_cols), lambda j, n: (j, 0, n)),
                          out_shape=jax.ShapeDtypeStruct((N_CHIPS, FF_SHARD, D_MODEL), F32))

    g_down = g_ffn(act, dx2_b, "g_down")

    def act_bwd(col, g, u, d):
        _, vjp = jax.vjp(act_fn, g, u)
        return vjp(d)

    dgate, dup = _tiles(act_bwd, name="ffn_act_bwd", rows=N_CHIPS * rows, tm=t_act,
                        row_ins=[(flat(gate), FF_SHARD, 0), (flat(up), FF_SHARD, 0), (flat(dact), FF_SHARD, 0)],
                        row_outs=[(FF_SHARD, BF16), (FF_SHARD, BF16)])
    dgate, dup = dgate.reshape(st_act.shape), dup.reshape(st_act.shape)

    def d_h2(d_st, w_st, name, add):
        return _mm_blocks(d_st, w_st, name=name, grid=(n_rt, D_MODEL // t_cols), dims="nn", n_sum=N_CHIPS,
                          a_spec=pl.BlockSpec((N_CHIPS, t_rows, FF_SHARD), lambda i, n: (0, i, 0)),
                          b_spec=pl.BlockSpec((N_CHIPS, FF_SHARD, t_cols), lambda i, n: (0, 0, n)),
                          o_spec=out_rows, out_shape=jax.ShapeDtypeStruct((rows, D_MODEL), F32),
                          add=add, add_spec=out_rows)

    dh2 = d_h2(dup, w_up, "d_h2_up", d_h2(dgate, w_gate, "d_h2_gate", None))
    g_gate, g_up = g_ffn(dgate, h2, "g_gate"), g_ffn(dup, h2, "g_up")

    def norm_bwd(col, w, xx, dh, dres):
        _, vjp = jax.vjp(_rms, xx, w)
        dx, dw = vjp(dh)
        return dx + dres, dx + dres, dw

    dx1, dx1_b, d_norm2_w = _tiles(norm_bwd, name="norm2_bwd", rows=rows, tm=tm, full_consts=[norm2_w],
                                   row_ins=[(x1, D_MODEL, 0), (dh2, D_MODEL, 0), (dx2, D_MODEL, 0)],
                                   row_outs=[(D_MODEL, F32), (D_MODEL, BF16)], acc_outs=[(1, D_MODEL)])
    dmix = _mm(dx1_b, w_out, dims="nt", name="d_mix", tk=1024)
    g_out = _mm(mix, dx1_b, dims="tn", name="g_out", tk=rows)
    w_on = w_on + early_grads_ready(g_out, g_gate, g_up, g_down)

    def mix_g_bwd(col, w, o, z, d):
        _, vjp = jax.vjp(mix_g_fn, w, o, z)
        dw, do_, dz = vjp(d)
        return do_, dz, dw

    do_gdn, dz, d_on = _tiles(mix_g_bwd, name="mix_gdn_bwd", rows=rows, tm=rows, ncol=PAIRS, full_consts=[w_on],
                              row_ins=[(o_gdn, LANES, 0), (proj, LANES, z_off), (dmix, LANES, 0)],
                              row_outs=[(LANES, F32), (LANES, BF16)], acc_outs=[(1, LANES)])

    def mix_f_bwd(col, a, g, d):
        _, vjp = jax.vjp(mix_f_fn, a, g)
        return vjp(d)

    dao, dfgate = _tiles(mix_f_bwd, name="mix_fox_bwd", rows=rows, tm=rows, ncol=PAIRS,
                         row_ins=[(ao, LANES, 0), (proj, LANES, fg_off), (dmix, LANES, PAIRS)],
                         row_outs=[(LANES, F32), (LANES, BF16)])

    delta = _attention_delta(fqk, proj, frow, lse, dao, rows)
    dfq, dfk, dfv, dfrow = _attention_backward(fqk, proj, frow, delta, lse, dao, rows)

    def fox_prep_bwd(col, w, xx, d):
        _, vjp = jax.vjp(_head_rms, w, xx)
        dw, dx = vjp(d)
        return dx, dw

    dfqk, d_wqk = [], []
    for part, d_n in enumerate((dfq, dfk)):
        dx_p, dw_p = _tiles(fox_prep_bwd, name="fox_prep_bwd_" + "qk"[part], rows=rows, tm=rows, ncol=PAIRS,
                            col_consts=[(w_qk, 1, LANES, part * PAIRS)],
                            row_ins=[(proj, LANES, fox_off + part * PAIRS), (d_n, LANES, 0)],
                            row_outs=[(LANES, BF16)], acc_outs=[(1, LANES)])
        dfqk.append(dx_p)
        d_wqk.append(dw_p)

    dq, dk, dv, dbetax, dgcx, dgrow = _gdn_backward(qkv, betax, gcx, grow, ssave, tsave, do_gdn, rows)
    dqkv, d_conv = [], []
    for part, d_n in enumerate((dq, dk, dv)):
        prep_bwd = lambda col, cw, xx, dy, is_qk=(part < 2): _gdn_prep_bwd(is_qk, cw, xx, dy)
        dx_p, dw_p = _tiles(prep_bwd, name="gdn_prep_bwd_" + "qkv"[part], rows=rows, tm=rows, ncol=PAIRS,
                            col_consts=[(conv_w, CONV_K, LANES, part * PAIRS)],
                            row_ins=[(proj, LANES, part * PAIRS), (d_n, LANES, 0)],
                            row_outs=[(LANES, BF16)], acc_outs=[(CONV_K, LANES)])
        dqkv.append(dx_p)
        d_conv.append(dw_p)
    d_conv = jnp.concatenate(d_conv, axis=1)

    def expand_bwd(col, b, g, db, dg):
        return (_dot32(db, b, _CONTRACT["nt"]), _dot32(dg, g, _CONTRACT["nt"]))

    dgates_b, dcums_g = _tiles(expand_bwd, name="expand_bwd", rows=rows, tm=tm, full_consts=[xb, xg],
                               row_ins=[(dbetax, WIDTH, 0), (dgcx, WIDTH, 0)],
                               row_outs=[(LANES, F32), (LANES, F32)])
    dcums_row = jnp.concatenate([jnp.zeros((rows, 8), F32), _rowform_to_lanes(dgrow, rows),
                                 dfrow.reshape(HEADS, rows).T, jnp.zeros((rows, LANES - 24), F32)], axis=1)

    def gates_bwd(col, lcv, lfv, a, dt, fb, pre, dgb, dcg, dcr):
        lane = _lane_ids(pre.shape)
        dgates = jnp.where(lane < 8, dgb, _cums_bwd(lcv, lfv, dcg + dcr))
        _, vjp = jax.vjp(_gates_elem, a, dt, fb, pre)
        da, ddt, dfb, dpre = vjp(dgates)
        return dpre, da, ddt, dfb

    dpre, d_a, d_dt, d_fb = _tiles(gates_bwd, name="gates_bwd", rows=rows, tm=rows,
                                   full_consts=[lc, lf, p_a, p_dt, p_fb],
                                   row_ins=[(proj, LANES, COL_SMALL), (dgates_b, LANES, 0), (dcums_g, LANES, 0),
                                            (dcums_row, LANES, 0)],
                                   row_outs=[(LANES, BF16)], acc_outs=[(1, LANES)] * 3)

    dproj = jnp.concatenate(dqkv + [dz] + dfqk + [dfv, dfgate, dpre], axis=1)
    dh1 = _mm(dproj, w_cat, dims="nn", name="d_h1", tk=D_CAT)
    g_cat = _mm(dproj, h1, dims="tn", name="g_in", tm=384, tn=D_MODEL, tk=rows)

    def norm1_bwd(col, w, xx, dh, dres):
        _, vjp = jax.vjp(_rms, xx, w)
        dx, dw = vjp(dh)
        return dx + dres, dw

    grad_x, d_norm1_w = _tiles(norm1_bwd, name="norm1_bwd", rows=rows, tm=tm, full_consts=[norm1_w],
                               row_ins=[(x, D_MODEL, 0), (dh1, D_MODEL, 0), (dx1, D_MODEL, 0)],
                               row_outs=[(D_MODEL, F32)], acc_outs=[(1, D_MODEL)])

    fold = lambda v: v.reshape(-1, HEAD_DIM).sum(axis=0)
    small = dict(
        loss=loss[0, 0],
        norm1_w=d_norm1_w, conv_w=d_conv, a_log=d_a[0, 8:16], dt_bias=d_dt[0, 8:16],
        out_norm_w=fold(d_on), f_bias=d_fb[0, 16:24], q_norm_w=fold(d_wqk[0]),
        k_norm_w=fold(d_wqk[1]), norm2_w=d_norm2_w, final_w=d_final_w)
    return grad_x, g_cat, g_out, g_gate, g_up, g_down, small


HBM_SPEC = pl.BlockSpec(memory_space=pltpu.HBM)


def _place():
    x, y, c = lax.axis_index("x"), lax.axis_index("y"), lax.axis_index("c")
    chips = [(1 - x, y), (x, 1 - y), (1 - x, 1 - y)]
    return x, y, c, 2 * x + y, (x, y, 1 - c), chips, [2 * cx + cy for cx, cy in chips]


def _remote(src, dst, send_sem, recv_sem, to):
    return pltpu.make_async_remote_copy(src_ref=src, dst_ref=dst, send_sem=send_sem, recv_sem=recv_sem,
                                        device_id=to, device_id_type=MESH)


def _allgather_weights(shards, conv):
    n = len(shards)
    halves = [s.shape[1] // 2 for s in shards]
    per = 6
    own_base = n * per + 3

    def body(*refs):
        ins, conv_in = refs[:n], refs[n]
        outs, conv_out = refs[n + 1:2 * n + 1], refs[2 * n + 1]
        send_sems, recv_sems = refs[2 * n + 2:]
        x, y, c, own, sib, chips, chip_idx = _place()

        def half(i, ref, hc):
            return ref.at[:, pl.ds(pl.multiple_of(hc * halves[i], LANES), halves[i])]

        sent = []
        for i, (src, dst) in enumerate(zip(list(ins) + [conv_in], list(outs) + [conv_out])):
            k = own_base + i
            sent.append(_remote(src, dst.at[own], send_sems.at[k], recv_sems.at[k], sib))
        for i in range(n):
            for j, chip in enumerate(chips):
                k = i * per + j
                sent.append(_remote(half(i, ins[i], c), half(i, outs[i].at[own], c),
                                    send_sems.at[k], recv_sems.at[k], (*chip, c)))
        for j, chip in enumerate(chips):
            k = n * per + j
            sent.append(_remote(conv_in, conv_out.at[own], send_sems.at[k], recv_sems.at[k], (*chip, c)))
        for cp in sent:
            cp.start()
        for i in range(n):
            for j in range(len(chips)):
                k = i * per + j
                landed = half(i, outs[i].at[chip_idx[j]], c)
                _remote(landed, landed, send_sems.at[k], recv_sems.at[k], sib).wait_recv()
                fwd = _remote(landed, landed, send_sems.at[k + 3], recv_sems.at[k + 3], sib)
                fwd.start()
                sent.append(fwd)
        for i in range(n):
            for j in range(len(chips)):
                k = i * per + 3 + j
                landed = half(i, outs[i].at[chip_idx[j]], 1 - c)
                _remote(landed, landed, send_sems.at[k], recv_sems.at[k], sib).wait_recv()
        for j in range(len(chips)):
            k = n * per + j
            landed = conv_out.at[chip_idx[j]]
            _remote(landed, landed, send_sems.at[k], recv_sems.at[k], sib).wait_recv()
        for i, dst in enumerate(list(outs) + [conv_out]):
            k = own_base + i
            landed = dst.at[own]
            _remote(landed, landed, send_sems.at[k], recv_sems.at[k], sib).wait_recv()
        for cp in sent:
            cp.wait_send()

    n_sem = own_base + n + 1
    out_shape = [jax.ShapeDtypeStruct((N_CHIPS,) + s.shape, s.dtype) for s in shards]
    out_shape.append(jax.ShapeDtypeStruct((N_CHIPS,) + conv.shape, conv.dtype))
    res = pl.pallas_call(
        body, name="allgather_weights", out_shape=out_shape,
        in_specs=[HBM_SPEC] * (n + 1), out_specs=[HBM_SPEC] * (n + 1),
        scratch_shapes=[pltpu.SemaphoreType.DMA((n_sem,)), pltpu.SemaphoreType.DMA((n_sem,))],
    )(*shards, conv)
    return res[:n], res[n]


SEM_SPEC = pl.BlockSpec(memory_space=pltpu.SEMAPHORE)
ANY_SPEC = pl.BlockSpec(memory_space=pl.ANY)
DATAFLOW = pltpu.SideEffectType.DATAFLOW_SIDE_EFFECTING


def _gather_plan(srcs, lands):
    x, y, c, own, sib, chips, chip_idx = _place()
    plan = []
    for src, land in zip(srcs, lands):
        for j, chip in enumerate(chips):
            plan.append((src, land.at[own], (*chip, c), land.at[chip_idx[j]]))
        plan.append((src, land.at[own], sib, land.at[own]))
    return plan


def _exchange_plan(srcs, lands):
    x, y, c, own, sib, chips, chip_idx = _place()
    plan = []
    for src, land in zip(srcs, lands):
        for j, chip in enumerate(chips):
            plan.append((src.at[chip_idx[j]], land.at[j], (*chip, c), land.at[j]))
    return plan


def _split_start(name, plan_fn, srcs, land_shapes, n_copies, after):
    n = len(srcs)

    def body(*refs):
        src_refs, land_refs = refs[:n], refs[n:2 * n]
        send_sems, recv_sems = refs[2 * n + 1], refs[2 * n + 2]
        token = refs[-1]
        for k, (src, dst, to, _) in enumerate(plan_fn(src_refs, land_refs)):
            _remote(src, dst, send_sems.at[k], recv_sems.at[k], to).start()
        token[...] = jnp.zeros_like(token)

    lands = [pltpu.with_memory_space_constraint(lax.empty(s.shape, s.dtype), pltpu.HBM) for s in land_shapes]
    srcs = [pltpu.with_memory_space_constraint(s, pltpu.HBM) for s in srcs]
    out_shape = ([pltpu.SemaphoreType.DMA((n_copies,)), pltpu.SemaphoreType.DMA((n_copies,))]
                 + [pltpu.HBM(s.shape, s.dtype) for s in srcs] + [pltpu.HBM(s.shape, s.dtype) for s in land_shapes]
                 + [jax.ShapeDtypeStruct((8, LANES), F32)])
    res = pl.pallas_call(
        body, name=name, out_shape=out_shape,
        in_specs=[HBM_SPEC] * (2 * n) + [ANY_SPEC],
        out_specs=[SEM_SPEC, SEM_SPEC] + [HBM_SPEC] * (2 * n) + [pl.BlockSpec(memory_space=pltpu.VMEM)],
        input_output_aliases={i: 2 + i for i in range(2 * n)},
        compiler_params=pltpu.CompilerParams(has_side_effects=DATAFLOW),
    )(*srcs, *lands, after)
    return dict(sems=res[:2], srcs=res[2:2 + n], lands=res[2 + n:2 + 2 * n], token=res[-1], n=n)


def _split_wait(name, plan_fn, started, after):
    n = started["n"]

    def body(*refs):
        src_refs, land_refs = refs[:n], refs[n:2 * n]
        send_sems, recv_sems = refs[2 * n], refs[2 * n + 1]
        for k, (src, _, to, landed) in enumerate(plan_fn(src_refs, land_refs)):
            copy = _remote(src, landed, send_sems.at[k], recv_sems.at[k], to)
            copy.wait_send()
            copy.wait_recv()

    srcs, lands = started["srcs"], started["lands"]
    res = pl.pallas_call(
        body, name=name,
        out_shape=[pltpu.HBM(s.shape, s.dtype) for s in srcs] + [pltpu.HBM(s.shape, s.dtype) for s in lands],
        in_specs=[HBM_SPEC] * (2 * n) + [SEM_SPEC, SEM_SPEC, ANY_SPEC],
        out_specs=[HBM_SPEC] * (2 * n),
        input_output_aliases={i: i for i in range(2 * n)},
        compiler_params=pltpu.CompilerParams(has_side_effects=DATAFLOW),
    )(*srcs, *lands, *started["sems"], after)
    return res[n:]


def _swap_halves(stacks, name):
    n = len(stacks)

    def body(*refs):
        ins, outs = refs[:n], refs[n:2 * n]
        send_sems, recv_sems = refs[2 * n:]
        x, y, c, own, sib, chips, chip_idx = _place()
        cps = []
        for i in range(n):
            h = stacks[i].shape[2] // 2
            src = ins[i].at[:, :, pl.ds(pl.multiple_of((1 - c) * h, LANES), h)]
            cps.append(_remote(src, outs[i], send_sems.at[i], recv_sems.at[i], sib))
        for cp in cps:
            cp.start()
        for cp in cps:
            cp.wait()

    out_shape = [jax.ShapeDtypeStruct((N_CHIPS, s.shape[1], s.shape[2] // 2), s.dtype) for s in stacks]
    return pl.pallas_call(
        body, name=name, out_shape=out_shape,
        in_specs=[HBM_SPEC] * n, out_specs=[HBM_SPEC] * n,
        scratch_shapes=[pltpu.SemaphoreType.DMA((n,)), pltpu.SemaphoreType.DMA((n,))],
    )(*stacks)


def _add_half(stack, landed, place, name):
    _, rows, h = landed.shape

    def body(place_ref, a_ref, b_ref, o_ref, own_ref):
        part = (a_ref[...] + b_ref[...]).astype(o_ref.dtype)
        o_ref[...] = part

        @pl.when(pl.program_id(0) == place_ref[1])
        def _():
            own_ref[...] = part[0]

    return pl.pallas_call(
        body, name=name,
        out_shape=[jax.ShapeDtypeStruct(landed.shape, BF16), jax.ShapeDtypeStruct((rows, h), BF16)],
        grid_spec=pltpu.PrefetchScalarGridSpec(
            num_scalar_prefetch=1, grid=(N_CHIPS,),
            in_specs=[pl.BlockSpec((1, rows, h), lambda j, p: (j, 0, p[0])),
                      pl.BlockSpec((1, rows, h), lambda j, p: (j, 0, 0))],
            out_specs=[pl.BlockSpec((1, rows, h), lambda j, p: (j, 0, 0)),
                       pl.BlockSpec((rows, h), lambda j, p: (0, 0))]),
        compiler_params=_params(("arbitrary",)),
    )(place, stack, landed)


def _exchange_partials(parts):
    n = len(parts)

    def body(*refs):
        ins, outs = refs[:n], refs[n:2 * n]
        send_sems, recv_sems = refs[2 * n:]
        x, y, c, own, sib, chips, chip_idx = _place()
        sent = []
        for i in range(n):
            for j, chip in enumerate(chips):
                k = i * 3 + j
                sent.append(_remote(ins[i].at[chip_idx[j]], outs[i].at[j], send_sems.at[k], recv_sems.at[k],
                                    (*chip, c)))
        for cp in sent:
            cp.start()
        for i in range(n):
            for j in range(len(chips)):
                k = i * 3 + j
                landed = outs[i].at[j]
                _remote(landed, landed, send_sems.at[k], recv_sems.at[k], sib).wait_recv()
        for cp in sent:
            cp.wait_send()

    return pl.pallas_call(
        body, name="rs_exchange_partials",
        out_shape=[jax.ShapeDtypeStruct((3,) + p.shape[1:], p.dtype) for p in parts],
        in_specs=[HBM_SPEC] * n, out_specs=[HBM_SPEC] * n,
        scratch_shapes=[pltpu.SemaphoreType.DMA((3 * n,)), pltpu.SemaphoreType.DMA((3 * n,))],
    )(*parts)


def _sum_partials(own_part, landed, name):
    _, h, cols = landed.shape

    def body(own_ref, a_ref, o_ref):
        acc = own_ref[...].astype(F32)
        for s in range(3):
            acc = acc + a_ref[s].astype(F32)
        o_ref[...] = acc

    return pl.pallas_call(
        body, name=name, out_shape=jax.ShapeDtypeStruct((h, cols), F32), grid=(1,),
        in_specs=[pl.BlockSpec((h, cols), lambda i: (0, 0)), pl.BlockSpec(landed.shape, lambda i: (0, 0, 0))],
        out_specs=pl.BlockSpec((h, cols), lambda i: (0, 0)),
        compiler_params=_params(("arbitrary",)),
    )(own_part, landed)


def _share_halves(halves, name):
    n = len(halves)

    def body(*refs):
        ins, outs = refs[:n], refs[n:2 * n]
        send_sems, recv_sems = refs[2 * n:]
        x, y, c, own, sib, chips, chip_idx = _place()
        cps = [_remote(ins[i], outs[i], send_sems.at[i], recv_sems.at[i], sib) for i in range(n)]
        for cp in cps:
            cp.start()
        for cp in cps:
            cp.wait()

    return pl.pallas_call(
        body, name=name,
        out_shape=[jax.ShapeDtypeStruct(p.shape, p.dtype) for p in halves],
        in_specs=[HBM_SPEC] * n, out_specs=[HBM_SPEC] * n,
        scratch_shapes=[pltpu.SemaphoreType.DMA((n,)), pltpu.SemaphoreType.DMA((n,))],
    )(*halves)


def _allreduce_small(packed):
    rows = packed.shape[0]
    n_dev = 8

    def body(in_ref, out_ref, gath, send_sems, recv_sems):
        x, y, c = lax.axis_index("x"), lax.axis_index("y"), lax.axis_index("c")
        me = 4 * x + 2 * y + c
        gath[me] = in_ref[...]
        cps = []
        for k in range(1, n_dev):
            fx, fy, fc = (k >> 2) & 1, (k >> 1) & 1, k & 1
            to = (x ^ fx, y ^ fy, c ^ fc)
            cps.append(_remote(in_ref, gath.at[me], send_sems.at[k - 1], recv_sems.at[k - 1], to))
        for cp in cps:
            cp.start()
        for k in range(1, n_dev):
            fx, fy, fc = (k >> 2) & 1, (k >> 1) & 1, k & 1
            src = 4 * (x ^ fx) + 2 * (y ^ fy) + (c ^ fc)
            slot = gath.at[src]
            _remote(slot, slot, send_sems.at[k - 1], recv_sems.at[k - 1], (x, y, c)).wait_recv()
        for cp in cps:
            cp.wait_send()
        acc = gath[0]
        for d in range(1, n_dev):
            acc = acc + gath[d]
        out_ref[...] = acc

    vm = pl.BlockSpec(memory_space=pltpu.VMEM)
    return pl.pallas_call(
        body, name="allreduce_small", out_shape=jax.ShapeDtypeStruct(packed.shape, F32),
        in_specs=[vm], out_specs=vm,
        scratch_shapes=[pltpu.VMEM((n_dev, rows, LANES), F32),
                        pltpu.SemaphoreType.DMA((n_dev - 1,)), pltpu.SemaphoreType.DMA((n_dev - 1,))],
    )(packed)


def _adam(col, w, g, m, v):
    m2 = ADAM_B1 * m + (1.0 - ADAM_B1) * g
    v2 = ADAM_B2 * v + (1.0 - ADAM_B2) * (g * g)
    m_hat = m2 / (1.0 - ADAM_B1 ** ADAM_STEP)
    v_hat = v2 / (1.0 - ADAM_B2 ** ADAM_STEP)
    delta = -ADAM_LR * (m_hat / (jnp.sqrt(v_hat) + ADAM_EPS) + ADAM_WD * w)
    return delta, m2, v2


def _adam_call(w, g, m, v, name):
    rows, cols = w.shape
    tm = rows
    for cand in (256, 352, 176, 128, 64, 48, 16, 8):
        if rows % cand == 0:
            tm = cand
            break
    return _tiles(_adam, name=name, rows=rows, tm=tm,
                  row_ins=[(w, cols, 0), (g, cols, 0), (m, cols, 0), (v, cols, 0)],
                  row_outs=[(cols, F32)] * 3)


def _adam_big(w, g_mine, g_other, m, v, place, name):
    rows, cols = w.shape
    tc = 256
    nt = cols // 2 // tc

    def body(place_ref, w_ref, gm_ref, go_ref, m_ref, v_ref, g_out, d_out, m_out, v_out):
        g = jnp.where(pl.program_id(0) == place_ref[0], gm_ref[...], go_ref[...])
        d, m2, v2 = _adam(None, w_ref[...], g, m_ref[...], v_ref[...])
        g_out[...] = g
        d_out[...] = d
        m_out[...] = m2
        v_out[...] = v2

    full = pl.BlockSpec((rows, tc), lambda hh, i, p: (0, hh * nt + i))
    half = pl.BlockSpec((rows, tc), lambda hh, i, p: (0, i))
    return pl.pallas_call(
        body, name=name, out_shape=[jax.ShapeDtypeStruct(w.shape, F32)] * 4,
        grid_spec=pltpu.PrefetchScalarGridSpec(
            num_scalar_prefetch=1, grid=(2, nt),
            in_specs=[full, half, half, full, full], out_specs=[full] * 4),
        compiler_params=_params(("arbitrary", "arbitrary")),
    )(place, w, g_mine, g_other, m, v)


def _pack(arrays):
    flat = []
    for a in arrays:
        a = a.reshape(-1).astype(F32)
        flat.append(jnp.pad(a, (0, (-a.size) % LANES)))
    out = jnp.concatenate(flat)
    out = jnp.pad(out, (0, (-out.size) % (8 * LANES)))
    return out.reshape(-1, LANES)


def _unpack(packed, shapes):
    flat = packed.reshape(-1)
    out, off = [], 0
    for s in shapes:
        size = int(np.prod(s))
        out.append(flat[off:off + size].reshape(s))
        off += size + (-size) % LANES
    return out


def kernel(x, norm1_w, w_in, gdn_conv_w, gdn_A_log, gdn_dt_bias, gdn_out_norm_w, fox_f_bias, fox_q_norm_w, fox_k_norm_w, w_out, norm2_w, w_ffn_gate, w_ffn_up, w_ffn_down, final_norm_w, loss_target, m_norm1_w, m_w_in, m_gdn_conv_w, m_gdn_A_log, m_gdn_dt_bias, m_gdn_out_norm_w, m_fox_f_bias, m_fox_q_norm_w, m_fox_k_norm_w, m_w_out, m_norm2_w, m_w_ffn_gate, m_w_ffn_up, m_w_ffn_down, m_final_norm_w, v_norm1_w, v_w_in, v_gdn_conv_w, v_gdn_A_log, v_gdn_dt_bias, v_gdn_out_norm_w, v_fox_f_bias, v_fox_q_norm_w, v_fox_k_norm_w, v_w_out, v_norm2_w, v_w_ffn_gate, v_w_ffn_up, v_w_ffn_down, v_final_norm_w):
    cx, cy, cc = lax.axis_index("x"), lax.axis_index("y"), lax.axis_index("c")
    own = 2 * cx + cy
    place = jnp.stack([cc, own]).astype(jnp.int32)

    names = ["w_in", "w_out", "w_gate", "w_up", "w_down"]
    is_t = [True, False, True, True, False]
    to_t = lambda a, t: a[0].T if t else a[0]
    from_t = lambda a, t: (a.T if t else a)[None]
    big_w = [to_t(a, t) for a, t in zip([w_in, w_out, w_ffn_gate, w_ffn_up, w_ffn_down], is_t)]
    big_m = [to_t(a, t) for a, t in zip([m_w_in, m_w_out, m_w_ffn_gate, m_w_ffn_up, m_w_ffn_down], is_t)]
    big_v = [to_t(a, t) for a, t in zip([v_w_in, v_w_out, v_w_ffn_gate, v_w_ffn_up, v_w_ffn_down], is_t)]
    shards = [w.astype(BF16) for w in big_w]
    (w_in_g,), conv_g = _allgather_weights(shards[:1], gdn_conv_w[0])
    rest = _split_start("gather_rest_start", _gather_plan, shards[1:],
                        [jax.ShapeDtypeStruct((N_CHIPS,) + s.shape, BF16) for s in shards[1:]],
                        n_copies=4 * len(shards[1:]), after=w_in_g)
    w_cat = _cat_weights(w_in_g.reshape(D_IN, D_MODEL))
    conv_full = conv_g.transpose(1, 0, 2).reshape(CONV_K, 3 * WIDTH)

    def late_weights(after):
        w_out_g, w_gate_g, w_up_g, w_down_g = _split_wait("gather_rest_wait", _gather_plan, rest, after)
        return w_out_g.reshape(D_MODEL, D_MODEL), w_gate_g, w_up_g, w_down_g

    def start_reduction(stacks, nms, tag):
        landed = _swap_halves(stacks, "rs_swap_" + tag)
        added = [_add_half(s, l, place, "rs_add_" + nm) for s, l, nm in zip(stacks, landed, nms)]
        parts = [a[0] for a in added]
        started = _split_start("exchange_" + tag + "_start", _exchange_plan, parts,
                               [jax.ShapeDtypeStruct((3,) + p.shape[1:], p.dtype) for p in parts],
                               n_copies=3 * len(parts), after=parts[0])
        return dict(own=[a[1] for a in added], started=started, tag=tag, names=nms)

    def finish_reduction(red, after, ws, ms, vs):
        landed = _split_wait("exchange_" + red["tag"] + "_wait", _exchange_plan, red["started"], after)
        halves = [_sum_partials(o, p, "rs_sum_" + nm) for o, p, nm in zip(red["own"], landed, red["names"])]
        others = _share_halves(halves, "rs_share_" + red["tag"])
        return [_adam_big(w, gm, go, m, v, place, "adam_" + nm)
                for w, gm, go, m, v, nm in zip(ws, halves, others, ms, vs, red["names"])]

    early = {}

    def early_grads_ready(g_out, g_gate, g_up, g_down):
        stacks = [g_out.reshape(N_CHIPS, D_MODEL // N_CHIPS, D_MODEL), g_gate, g_up, g_down]
        early.update(start_reduction(stacks, names[1:], "early"))
        return early["started"]["token"][0, 0]

    grad_x, g_cat, _, _, _, _, small = _local_step(
        x[0], loss_target[0], norm1_w + rest["token"][0, 0], w_cat, conv_full, gdn_A_log[0], gdn_dt_bias[0],
        gdn_out_norm_w[0], fox_f_bias[0], fox_q_norm_w[0], fox_k_norm_w[0], norm2_w, final_norm_w.reshape(1, -1),
        late_weights, early_grads_ready)

    late = start_reduction([_uncat_grad(g_cat).reshape(N_CHIPS, D_IN // N_CHIPS, D_MODEL)], names[:1], "w_in")
    big_upd = finish_reduction(early, late["started"]["token"], big_w[1:], big_m[1:], big_v[1:])

    order = ["norm1_w", "conv_w", "a_log", "dt_bias", "out_norm_w", "f_bias", "q_norm_w", "k_norm_w",
             "norm2_w", "final_w"]
    red = _allreduce_small(_pack([small[k] for k in order] + [small["loss"]]))
    red_shapes = [(1, D_MODEL), (CONV_K, 3 * WIDTH), (1, HEADS), (1, HEADS), (1, HEAD_DIM), (1, HEADS),
                  (1, HEAD_DIM), (1, HEAD_DIM), (1, D_MODEL), (D_MODEL,), ()]
    red_list = _unpack(red, red_shapes)
    loss = red_list[-1]
    small_g = dict(zip(order, red_list[:-1]))
    shard_cols = 3 * WIDTH // N_CHIPS
    small_g["conv_w"] = lax.dynamic_slice_in_dim(small_g["conv_w"], own * shard_cols, shard_cols, axis=1)[None]
    small_w = [norm1_w, gdn_conv_w, gdn_A_log, gdn_dt_bias, gdn_out_norm_w, fox_f_bias, fox_q_norm_w,
               fox_k_norm_w, norm2_w, final_norm_w]
    small_m = [m_norm1_w, m_gdn_conv_w, m_gdn_A_log, m_gdn_dt_bias, m_gdn_out_norm_w, m_fox_f_bias,
               m_fox_q_norm_w, m_fox_k_norm_w, m_norm2_w, m_final_norm_w]
    small_v = [v_norm1_w, v_gdn_conv_w, v_gdn_A_log, v_gdn_dt_bias, v_gdn_out_norm_w, v_fox_f_bias,
               v_fox_q_norm_w, v_fox_k_norm_w, v_norm2_w, v_final_norm_w]
    small_gl = [small_g[k].reshape(w.shape) for k, w in zip(order, small_w)]
    s_delta, s_m, s_v = _adam_call(_pack(small_w), _pack(small_gl), _pack(small_m), _pack(small_v), "adam_small")
    big_upd = finish_reduction(late, s_delta, big_w[:1], big_m[:1], big_v[:1]) + big_upd
    shapes = [w.shape for w in small_w]
    s_delta, s_m, s_v = _unpack(s_delta, shapes), _unpack(s_m, shapes), _unpack(s_v, shapes)

    big_pos = {1: 0, 9: 1, 11: 2, 12: 3, 13: 4}
    small_pos = {0: 0, 2: 1, 3: 2, 4: 3, 5: 4, 6: 5, 7: 6, 8: 7, 10: 8, 14: 9}
    grads, deltas, new_m, new_v = [], [], [], []
    for pos in range(15):
        if pos in big_pos:
            b = big_pos[pos]
            g, d, m2, v2 = [from_t(a, is_t[b]) for a in big_upd[b]]
            grads.append(g)
            deltas.append(d)
            new_m.append(m2)
            new_v.append(v2)
        else:
            s = small_pos[pos]
            grads.append(small_gl[s])
            deltas.append(s_delta[s])
            new_m.append(s_m[s])
            new_v.append(s_v[s])
    return (loss, grad_x[None], *grads, *deltas, *new_m, *new_v)
```

```python
import jax
import jax.numpy as jnp
import numpy as np
from jax import lax
from jax.experimental import pallas as pl
from jax.experimental.pallas import tpu as pltpu

F32 = jnp.float32
BF16 = jnp.bfloat16

D_MODEL = 1024
HEADS = 8
HEAD_DIM = 64
PAIRS = HEADS // 2
WIDTH = HEADS * HEAD_DIM
CHUNK = 64
CONV_K = 4
D_FF = 2816
FF_SHARD = D_FF // 4
EPS = 1e-6
SCALE = HEAD_DIM ** -0.5
LANES = 128
N_CHIPS = 4
D_IN = 4120
D_CAT = 4224
COL_SMALL = 4096 // LANES

ADAM_LR = 0.001
ADAM_B1 = 0.9
ADAM_B2 = 0.999
ADAM_EPS = 1e-08
ADAM_WD = 0.01
ADAM_STEP = 10

VMEM_LIMIT = 56 * 1024 * 1024
MESH = pl.DeviceIdType.MESH
HIGHEST = lax.Precision.HIGHEST


def _params(sem):
    return pltpu.CompilerParams(dimension_semantics=sem, vmem_limit_bytes=VMEM_LIMIT)


_CONTRACT = {"nn": ((1,), (0,)), "nt": ((1,), (1,)), "tn": ((0,), (0,))}


def _mm(a, b, *, dims, name, out_dtype=F32, add=None, tm=1024, tn=512, tk=512):
    if dims == "nn":
        (m, k), (k2, n) = a.shape, b.shape
    elif dims == "nt":
        (m, k), (n, k2) = a.shape, b.shape
    else:
        (k, m), (k2, n) = a.shape, b.shape
    assert k == k2, (a.shape, b.shape, dims)
    tm, tn, tk = min(tm, m), min(tn, n), min(tk, k)
    assert m % tm == 0 and n % tn == 0 and k % tk == 0, (m, n, k, tm, tn, tk)
    nk = k // tk
    a_spec = (pl.BlockSpec((tk, tm), lambda i, j, kk: (kk, i)) if dims == "tn"
              else pl.BlockSpec((tm, tk), lambda i, j, kk: (i, kk)))
    b_spec = (pl.BlockSpec((tn, tk), lambda i, j, kk: (j, kk)) if dims == "nt"
              else pl.BlockSpec((tk, tn), lambda i, j, kk: (kk, j)))
    o_spec = pl.BlockSpec((tm, tn), lambda i, j, kk: (i, j))
    contract = (_CONTRACT[dims], ((), ()))
    has_add = add is not None

    def body(*refs):
        a_ref, b_ref = refs[:2]
        add_ref = refs[2] if has_add else None
        o_ref = refs[3] if has_add else refs[2]
        part = lax.dot_general(a_ref[...].astype(BF16), b_ref[...].astype(BF16), contract,
                               preferred_element_type=F32)

        def finish(r):
            if has_add:
                r = r + add_ref[...].astype(F32)
            o_ref[...] = r.astype(out_dtype)

        if nk == 1:
            finish(part)
            return
        acc = refs[-1]
        kk = pl.program_id(2)

        @pl.when(kk == 0)
        def _():
            acc[...] = part

        @pl.when(kk > 0)
        def _():
            acc[...] += part

        @pl.when(kk == nk - 1)
        def _():
            finish(acc[...])

    ins = [a, b] + ([add] if has_add else [])
    in_specs = [a_spec, b_spec] + ([o_spec] if has_add else [])
    return pl.pallas_call(
        body, name=name, grid=(m // tm, n // tn, nk),
        in_specs=in_specs, out_specs=o_spec,
        out_shape=jax.ShapeDtypeStruct((m, n), out_dtype),
        scratch_shapes=[pltpu.VMEM((tm, tn), F32)] if nk > 1 else [],
        compiler_params=_params(("parallel", "parallel", "arbitrary")),
    )(*ins)


def _mm_blocks(a, b, *, name, grid, a_spec, b_spec, o_spec, out_shape, dims, n_sum=0, add=None, add_spec=None):
    contract = (_CONTRACT[dims], ((), ()))
    has_add = add is not None

    def body(*refs):
        a_ref, b_ref = refs[:2]
        o_ref = refs[-1]
        dot = lambda x, y: lax.dot_general(x.astype(BF16), y.astype(BF16), contract, preferred_element_type=F32)
        if n_sum:
            r = dot(a_ref[0], b_ref[0])
            for s in range(1, n_sum):
                r = r + dot(a_ref[s], b_ref[s])
        else:
            r = dot(a_ref[...], b_ref[...])
        if has_add:
            r = r + refs[2][...].astype(F32)
        o_ref[...] = r.astype(o_ref.dtype)

    return pl.pallas_call(
        body, name=name, grid=grid,
        in_specs=[a_spec, b_spec] + ([add_spec] if has_add else []), out_specs=o_spec, out_shape=out_shape,
        compiler_params=_params(("parallel",) * len(grid)),
    )(*([a, b] + ([add] if has_add else [])))


def _tiles(fn, *, name, rows, tm, ncol=1, row_ins=(), col_consts=(), full_consts=(),
           row_outs=(), acc_outs=()):
    nt = rows // tm
    assert rows % tm == 0
    n_full, n_col, n_row = len(full_consts), len(col_consts), len(row_ins)
    n_ro, n_acc = len(row_outs), len(acc_outs)

    def body(*refs):
        ins = refs[:n_full + n_col + n_row]
        outs = refs[n_full + n_col + n_row:]
        i = pl.program_id(1)
        res = fn(pl.program_id(0), *[r[...] for r in ins])
        for r, v in zip(outs[:n_ro], res[:n_ro]):
            r[...] = v.astype(r.dtype)
        if n_acc:
            @pl.when(i == 0)
            def _():
                for r in outs[n_ro:]:
                    r[...] = jnp.zeros_like(r)
            for r, v in zip(outs[n_ro:], res[n_ro:]):
                r[...] += v

    in_specs = [pl.BlockSpec(a.shape, lambda j, i, nd=a.ndim: (0,) * nd) for a in full_consts]
    in_specs += [pl.BlockSpec((nr, w), lambda j, i, o=o: (0, o + j)) for (_, nr, w, o) in col_consts]
    in_specs += [pl.BlockSpec((tm, w), lambda j, i, o=o: (i, o + j)) for (_, w, o) in row_ins]
    out_specs = [pl.BlockSpec((tm, w), lambda j, i: (i, j)) for (w, _) in row_outs]
    out_specs += [pl.BlockSpec((nr, w), lambda j, i: (0, j)) for (nr, w) in acc_outs]
    out_shape = [jax.ShapeDtypeStruct((rows, w * ncol), dt) for (w, dt) in row_outs]
    out_shape += [jax.ShapeDtypeStruct((nr, w * ncol), F32) for (nr, w) in acc_outs]
    args = list(full_consts) + [c[0] for c in col_consts] + [r[0] for r in row_ins]
    out = pl.pallas_call(
        body, name=name, grid=(ncol, nt), in_specs=in_specs, out_specs=out_specs, out_shape=out_shape,
        compiler_params=_params(("parallel", "arbitrary")),
    )(*args)
    return out


def _rms(x, w):
    return x * lax.rsqrt(jnp.mean(x * x, axis=-1, keepdims=True) + EPS) * w


def _lane_lo(shape):
    return lax.broadcasted_iota(jnp.int32, shape, len(shape) - 1) < HEAD_DIM


def _pair_sum(x):
    lo = _lane_lo(x.shape)
    s0 = jnp.sum(jnp.where(lo, x, 0.0), axis=-1, keepdims=True)
    s1 = jnp.sum(jnp.where(lo, 0.0, x), axis=-1, keepdims=True)
    return jnp.where(lo, s0, s1)


def _head_col(x, lo, h):
    keep = lo if h == 0 else jnp.logical_not(lo)
    return jnp.max(jnp.where(keep, x, -jnp.inf), axis=-1, keepdims=True)


def _softplus(x):
    return jnp.maximum(x, 0.0) + jnp.log1p(jnp.exp(-jnp.abs(x)))


def _silu(x):
    return x * jax.nn.sigmoid(x)


def _dot(a, b, contract):
    return lax.dot_general(a.astype(BF16), b.astype(BF16), (contract, ((), ())),
                           preferred_element_type=F32)


def _dot32(a, b, contract):
    return lax.dot_general(a, b, (contract, ((), ())), precision=HIGHEST, preferred_element_type=F32)


def _bd(y):
    yy = jnp.concatenate([y, y], axis=0)
    r = lax.broadcasted_iota(jnp.int32, yy.shape, 0) < HEAD_DIM
    c = lax.broadcasted_iota(jnp.int32, yy.shape, 1) < HEAD_DIM
    return jnp.where(r == c, yy, 0.0)


def _pp(x, y):
    return _dot(x, _bd(y), _CONTRACT["nn"])


def _pp_nt(x, y):
    return _dot(x, _bd(y), _CONTRACT["nt"])


def _pp_tn(x, y):
    full = _dot(x, y, _CONTRACT["tn"])
    return jnp.where(_lane_lo((HEAD_DIM, LANES)), full[:HEAD_DIM], full[HEAD_DIM:])


def _gdn_masks():
    row = lax.broadcasted_iota(jnp.int32, (CHUNK, LANES), 0)
    col = lax.broadcasted_iota(jnp.int32, (CHUNK, LANES), 1) % HEAD_DIM
    return row, col


def _interleave(chains):
    live = list(chains)
    while live:
        for g in list(live):
            try:
                next(g)
            except StopIteration:
                live.remove(g)


def _gdn_forward(qkv, betax, gcx, grow, rows):
    nchunk = rows // CHUNK

    def body(q_ref, k_ref, v_ref, bx_ref, gx_ref, gr_ref, o_ref, ss_ref, ts_ref, state):
        n = pl.program_id(0)

        @pl.when(n == 0)
        def _():
            state[...] = jnp.zeros_like(state)

        row, col = _gdn_masks()
        incl, strict = col <= row, col < row

        def chain(p):
            lanes = pl.ds(p * LANES, LANES)
            q, k, v, bx, gx = q_ref[:, lanes], k_ref[:, lanes], v_ref[:, lanes], bx_ref[:, lanes], gx_ref[:, lanes]
            gr = gr_ref[0, p]
            glast = gx_ref[pl.ds(CHUNK - 1, 1), lanes]
            s = state[p]
            dm = jnp.where(incl, jnp.exp(jnp.minimum(gx - gr, 0.0)), 0.0)
            kb, vb, eg, qs = k * bx, v * bx, jnp.exp(gx), q * SCALE
            yield
            big_g, big_p = _pp_nt(kb, k), _pp_nt(qs, k)
            yield
            x = -jnp.where(strict, big_g * dm, 0.0)
            att = jnp.where(incl, big_p * dm, 0.0)
            tm = jnp.where(row == col, 1.0, 0.0) + x
            x = _pp(x, x)
            yield
            for _ in range(4):
                step, x = _pp(tm, x), _pp(x, x)
                yield
                tm = tm + step
            tm = tm + _pp(tm, x)
            yield
            u, w = _pp(tm, vb), _pp(tm, kb * eg)
            yield
            ws, qgs = _pp(w, s), _pp(qs * eg, s)
            yield
            vn = u - ws
            kd = k * jnp.exp(glast - gx)
            avn, upd = _pp(att, vn), _pp_tn(kd, vn)
            yield
            ss_ref[0, p] = s
            ts_ref[0, p] = tm
            o_ref[:, lanes] = qgs + avn
            state[p] = s * jnp.exp(glast) + upd

        _interleave([chain(p) for p in range(PAIRS)])

    blk = lambda j: pl.BlockSpec((CHUNK, WIDTH), lambda n, j=j: (n, j))
    sv = pl.BlockSpec((1, PAIRS, CHUNK, LANES), lambda n: (n, 0, 0, 0))
    return pl.pallas_call(
        body, name="gdn_fwd", grid=(nchunk,),
        in_specs=[blk(0), blk(1), blk(2), blk(0), blk(0),
                  pl.BlockSpec((1, PAIRS, 1, LANES), lambda n: (n, 0, 0, 0))],
        out_specs=[blk(0), sv, sv],
        out_shape=[jax.ShapeDtypeStruct((rows, WIDTH), F32),
                   jax.ShapeDtypeStruct((nchunk, PAIRS, CHUNK, LANES), F32),
                   jax.ShapeDtypeStruct((nchunk, PAIRS, CHUNK, LANES), F32)],
        scratch_shapes=[pltpu.VMEM((PAIRS, CHUNK, LANES), F32)],
        compiler_params=_params(("arbitrary",)),
    )(qkv, qkv, qkv, betax, gcx, grow)


def _gdn_backward(qkv, betax, gcx, grow, ssave, tsave, do, rows):
    nchunk = rows // CHUNK

    def body(q_ref, k_ref, v_ref, bx_ref, gx_ref, gr_ref, ss_ref, ts_ref, do_ref,
             dq_ref, dk_ref, dv_ref, dbx_ref, dgx_ref, dgr_ref, dstate):
        n = pl.program_id(0)

        @pl.when(n == 0)
        def _():
            dstate[...] = jnp.zeros_like(dstate)

        row, col = _gdn_masks()
        incl, strict = col <= row, col < row

        def chain(p):
            lanes = pl.ds(p * LANES, LANES)
            q, k, v, bx, gx = q_ref[:, lanes], k_ref[:, lanes], v_ref[:, lanes], bx_ref[:, lanes], gx_ref[:, lanes]
            gr = gr_ref[0, p]
            glast = gx_ref[pl.ds(CHUNK - 1, 1), lanes]
            s, tm, d_o = ss_ref[0, p], ts_ref[0, p], do_ref[:, lanes]
            ds_out = dstate[p]
            dm = jnp.where(incl, jnp.exp(jnp.minimum(gx - gr, 0.0)), 0.0)
            kb, vb, eg, qs = k * bx, v * bx, jnp.exp(gx), q * SCALE
            kbg, qg = kb * eg, qs * eg
            ed = jnp.exp(glast - gx)
            kd = k * ed
            eglast = jnp.exp(glast)
            yield
            big_g, big_p = _pp_nt(kb, k), _pp_nt(qs, k)
            u, w = _pp(tm, vb), _pp(tm, kbg)
            dqg, kds = _pp_nt(d_o, s), _pp(kd, ds_out)
            yield
            low = jnp.where(strict, big_g * dm, 0.0)
            att = jnp.where(incl, big_p * dm, 0.0)
            ws, atd = _pp(w, s), _pp_tn(att, d_o)
            yield
            vn = u - ws
            dvn = kds + atd
            dkd, datt_raw = _pp_nt(vn, ds_out), _pp_nt(d_o, vn)
            dw_neg, dvb = _pp_nt(dvn, s), _pp_tn(tm, dvn)
            dtm_a, wdv = _pp_nt(dvn, vb), _pp_tn(w, dvn)
            qgd = _pp_tn(qg, d_o)
            yield
            datt = jnp.where(incl, datt_raw, 0.0)
            dw = -dw_neg
            dtm_b, dkbg = _pp_nt(dw, kbg), _pp_tn(tm, dw)
            dbig_p = datt * dm
            dqs_a, dk_p = _pp(dbig_p, k), _pp_tn(dbig_p, qs)
            yield
            inner = _pp_tn(tm, dtm_a + dtm_b)
            yield
            dlow = jnp.where(strict, -_pp_nt(inner, tm), 0.0)
            yield
            dbig_g = dlow * dm
            dkb_a, dk_g = _pp(dbig_g, k), _pp_tn(dbig_g, kb)
            yield
            dkb = dkb_a + dkbg * eg
            dqs = dqs_a + dqg * eg
            dk = dk_g + dk_p + dkd * ed + dkb * bx
            z = dlow * low + datt * att
            kdterm = dkd * kd
            dglast = (jnp.sum(ds_out * s, axis=0, keepdims=True) * eglast
                      + jnp.sum(kdterm, axis=0, keepdims=True))
            dgx = dqg * qg + dkbg * kbg - kdterm
            dgx = dgx + jnp.where(col == 0, _pair_sum(z), 0.0)
            dgx = dgx + jnp.where(row == CHUNK - 1, dglast, 0.0)
            dq_ref[:, lanes] = dqs * SCALE
            dk_ref[:, lanes] = dk
            dv_ref[:, lanes] = dvb * bx
            dbx_ref[:, lanes] = dkb * k + dvb * v
            dgx_ref[:, lanes] = dgx
            dgr_ref[0, p] = -jnp.sum(z, axis=0, keepdims=True)
            dstate[p] = ds_out * eglast + qgd - wdv

        _interleave([chain(p) for p in range(PAIRS)])

    last = nchunk - 1
    blk = lambda j: pl.BlockSpec((CHUNK, WIDTH), lambda n, j=j: (last - n, j))
    sv = pl.BlockSpec((1, PAIRS, CHUNK, LANES), lambda n: (last - n, 0, 0, 0))
    gr_spec = pl.BlockSpec((1, PAIRS, 1, LANES), lambda n: (last - n, 0, 0, 0))
    wide = jax.ShapeDtypeStruct((rows, WIDTH), F32)
    return pl.pallas_call(
        body, name="gdn_bwd", grid=(nchunk,),
        in_specs=[blk(0), blk(1), blk(2), blk(0), blk(0), gr_spec, sv, sv, blk(0)],
        out_specs=[blk(0)] * 5 + [gr_spec],
        out_shape=[wide] * 5 + [jax.ShapeDtypeStruct((nchunk, PAIRS, 1, LANES), F32)],
        scratch_shapes=[pltpu.VMEM((PAIRS, CHUNK, LANES), F32)],
        compiler_params=_params(("arbitrary",)),
    )(qkv, qkv, qkv, betax, gcx, grow, ssave, tsave, do)


ATT_TQ = 256


def _att_scores(qh, kt, fk, diag):
    s = _dot(qh, kt, _CONTRACT["nt"]) - fk
    if diag:
        r = lax.broadcasted_iota(jnp.int32, s.shape, 0)
        c = lax.broadcasted_iota(jnp.int32, s.shape, 1)
        s = jnp.where(r >= c, s, -jnp.inf)
    return s


def _head_masks(n):
    lo = _lane_lo((n, LANES))
    return [lo, jnp.logical_not(lo)]


def _attention_forward(fqk, proj, frow, rows):
    tq = tk = min(ATT_TQ, rows)
    nq = rows // tq
    v_off = 3072 // LANES

    def body(q_ref, k_ref, v_ref, fr_ref, o_ref, lse_ref):
        qi = pl.program_id(1)
        q = q_ref[...] * SCALE
        keep_q, keep_k = _head_masks(tq), _head_masks(tk)
        qh = [jnp.where(keep_q[h], q, 0.0).astype(BF16) for h in range(2)]

        def tile(ki, carry, diag):
            k0 = pl.multiple_of(ki * tk, tk)
            kt = k_ref[pl.ds(k0, tk), :].astype(BF16)
            v_t = v_ref[pl.ds(k0, tk), :]
            out = [None, None]

            def chain(h):
                m, l, acc = carry[h]
                vt = jnp.where(keep_k[h], v_t, 0.0).astype(BF16)
                yield
                s = _att_scores(qh[h], kt, fr_ref[0, pl.ds(h, 1), pl.ds(k0, tk)], diag)
                yield
                m_new = jnp.maximum(m, jnp.max(s, axis=-1, keepdims=True))
                p = jnp.exp(s - m_new)
                alpha = jnp.exp(m - m_new)
                l = alpha * l + jnp.sum(p, axis=-1, keepdims=True)
                yield
                out[h] = (m_new, l, alpha * acc + _dot(p, vt, _CONTRACT["nn"]))

            _interleave([chain(0), chain(1)])
            return tuple(out)

        one = (jnp.full((tq, 1), -jnp.inf, F32), jnp.zeros((tq, 1), F32), jnp.zeros((tq, LANES), F32))
        carry = lax.fori_loop(0, qi, lambda ki, c: tile(ki, c, False), (one, one))
        (m0, l0, acc0), (m1, l1, acc1) = tile(qi, carry, True)
        o_ref[...] = acc0 / l0 + acc1 / l1
        lse_ref[...] = jnp.where(keep_q[0], m0 + jnp.log(l0), m1 + jnp.log(l1))

    whole = lambda off: pl.BlockSpec((rows, LANES), lambda p, i, off=off: (0, off + p))
    qblk = lambda off: pl.BlockSpec((tq, LANES), lambda p, i, off=off: (i, off + p))
    wide = jax.ShapeDtypeStruct((rows, WIDTH), F32)
    return pl.pallas_call(
        body, name="fox_fwd", grid=(PAIRS, nq),
        in_specs=[qblk(0), whole(PAIRS), whole(v_off), pl.BlockSpec((1, 2, rows), lambda p, i: (p, 0, 0))],
        out_specs=[qblk(0), qblk(0)], out_shape=[wide, wide],
        compiler_params=_params(("parallel", "arbitrary")),
    )(fqk, fqk, proj, frow)


def _attention_delta(fqk, proj, frow, lse, dao, rows):
    tq = tk = min(ATT_TQ, rows)
    nq = rows // tq
    v_off = 3072 // LANES

    def body(q_ref, k_ref, v_ref, fr_ref, lse_ref, do_ref, delta_ref):
        qi = pl.program_id(1)
        q, d_o, lse_t = q_ref[...] * SCALE, do_ref[...], lse_ref[...]
        keep_q = _head_masks(tq)
        qh = [jnp.where(keep_q[h], q, 0.0).astype(BF16) for h in range(2)]
        doh = [jnp.where(keep_q[h], d_o, 0.0).astype(BF16) for h in range(2)]
        lse_h = [_head_col(lse_t, keep_q[0], h) for h in range(2)]

        def tile(ki, carry, diag):
            k0 = pl.multiple_of(ki * tk, tk)
            kt = k_ref[pl.ds(k0, tk), :].astype(BF16)
            vt = v_ref[pl.ds(k0, tk), :].astype(BF16)
            out = [None, None]

            def chain(h):
                s = _att_scores(qh[h], kt, fr_ref[0, pl.ds(h, 1), pl.ds(k0, tk)], diag)
                dp = _dot(doh[h], vt, _CONTRACT["nt"])
                yield
                out[h] = carry[h] + jnp.sum(jnp.exp(s - lse_h[h]) * dp, axis=-1, keepdims=True)

            _interleave([chain(0), chain(1)])
            return tuple(out)

        zero = jnp.zeros((tq, 1), F32)
        carry = lax.fori_loop(0, qi, lambda ki, c: tile(ki, c, False), (zero, zero))
        d0, d1 = tile(qi, carry, True)
        delta_ref[...] = jnp.where(keep_q[0], d0, d1)

    whole = lambda off: pl.BlockSpec((rows, LANES), lambda p, i, off=off: (0, off + p))
    qblk = lambda off: pl.BlockSpec((tq, LANES), lambda p, i, off=off: (i, off + p))
    return pl.pallas_call(
        body, name="fox_delta", grid=(PAIRS, nq),
        in_specs=[qblk(0), whole(PAIRS), whole(v_off),
                  pl.BlockSpec((1, 2, rows), lambda p, i: (p, 0, 0)), qblk(0), qblk(0)],
        out_specs=qblk(0), out_shape=jax.ShapeDtypeStruct((rows, WIDTH), F32),
        compiler_params=_params(("parallel", "arbitrary")),
    )(fqk, fqk, proj, frow, lse, dao)


def _attention_backward(fqk, proj, frow, delta, lse, dao, rows):
    tq = tk = min(ATT_TQ, rows)
    nq = rows // tq
    v_off = 3072 // LANES

    def body(q_ref, k_ref, v_ref, fr_ref, delta_ref, lse_ref, do_ref, dq_ref, dk_ref, dv_ref, dfr_ref):
        ki = pl.program_id(1)

        @pl.when(ki == 0)
        def _():
            dq_ref[...] = jnp.zeros_like(dq_ref)

        keep_q, keep_k = _head_masks(tq), _head_masks(tk)
        k_t = k_ref[...]
        kt = k_t.astype(BF16)
        vt = v_ref[...].astype(BF16)
        kh = [jnp.where(keep_k[h], k_t, 0.0).astype(BF16) for h in range(2)]
        fk = [fr_ref[0, pl.ds(h, 1), :] for h in range(2)]

        def tile(qi, carry, diag):
            dk, dv, df0, df1 = carry
            rows_q = pl.ds(pl.multiple_of(qi * tq, tq), tq)
            q, d_o, delta_x, lse_t = q_ref[rows_q, :] * SCALE, do_ref[rows_q, :], delta_ref[rows_q, :], lse_ref[rows_q, :]
            res = [None, None]

            def chain(h):
                qh = jnp.where(keep_q[h], q, 0.0).astype(BF16)
                doh = jnp.where(keep_q[h], d_o, 0.0).astype(BF16)
                lse_h, delta_h = _head_col(lse_t, keep_q[0], h), _head_col(delta_x, keep_q[0], h)
                yield
                s, dp = _att_scores(qh, kt, fk[h], diag), _dot(doh, vt, _CONTRACT["nt"])
                yield
                p = jnp.exp(s - lse_h)
                ds = p * (dp - delta_h)
                yield
                res[h] = (_dot(p, doh, _CONTRACT["tn"]), _dot(ds, qh, _CONTRACT["tn"]),
                          _dot(ds, kh[h], _CONTRACT["nn"]), jnp.sum(ds, axis=0, keepdims=True))

            _interleave([chain(0), chain(1)])
            (dv0, dk0, dq0, s0), (dv1, dk1, dq1, s1) = res
            dq_ref[rows_q, :] += (dq0 + dq1) * SCALE
            return dk + dk0 + dk1, dv + dv0 + dv1, df0 - s0, df1 - s1

        zero_kv = jnp.zeros((tk, LANES), F32)
        zero_f = jnp.zeros((1, tk), F32)
        carry = tile(ki, (zero_kv, zero_kv, zero_f, zero_f), True)
        dk, dv, df0, df1 = lax.fori_loop(ki + 1, nq, lambda qi, c: tile(qi, c, False), carry)
        dk_ref[...] = dk
        dv_ref[...] = dv.astype(dv_ref.dtype)
        dfr_ref[0, pl.ds(0, 1), :] = df0
        dfr_ref[0, pl.ds(1, 1), :] = df1

    whole = lambda off: pl.BlockSpec((rows, LANES), lambda p, i, off=off: (0, off + p))
    kblk = lambda off: pl.BlockSpec((tk, LANES), lambda p, i, off=off: (i, off + p))
    fr_spec = pl.BlockSpec((1, 2, tk), lambda p, i: (p, 0, i))
    wide = jax.ShapeDtypeStruct((rows, WIDTH), F32)
    return pl.pallas_call(
        body, name="fox_bwd", grid=(PAIRS, nq),
        in_specs=[whole(0), kblk(PAIRS), kblk(v_off), fr_spec, whole(0), whole(0), whole(0)],
        out_specs=[whole(0), kblk(0), kblk(0), fr_spec],
        out_shape=[wide, wide, jax.ShapeDtypeStruct((rows, WIDTH), BF16),
                   jax.ShapeDtypeStruct((PAIRS, 2, rows), F32)],
        compiler_params=_params(("parallel", "arbitrary")),
    )(fqk, fqk, proj, frow, delta, lse, dao)


def _lane_ids(shape):
    return lax.broadcasted_iota(jnp.int32, shape, len(shape) - 1)


def _gates_elem(a_log, dt_bias, f_bias, pre):
    lane = _lane_ids(pre.shape)
    beta = jax.nn.sigmoid(pre)
    g = -jnp.exp(a_log) * _softplus(pre + dt_bias)
    lf = -_softplus(-(pre + f_bias))
    return jnp.where(lane < 8, beta, jnp.where(lane < 16, g, jnp.where(lane < 24, lf, 0.0)))


def _tri_consts():
    r = np.arange(LANES)[:, None]
    c = np.arange(LANES)[None, :]
    full = (c <= r).astype(np.float32)
    chunked = full * ((r // CHUNK) == (c // CHUNK))
    return jnp.asarray(chunked), jnp.asarray(full)


def _cums_fwd(lc, lf, gates):
    rows = gates.shape[0]
    lane = _lane_ids((LANES, LANES))
    carry = jnp.zeros((1, LANES), F32)
    out = []
    for r in range(rows // LANES):
        blk = gates[r * LANES:(r + 1) * LANES]
        gc = _dot32(lc, blk, _CONTRACT["nn"])
        f = _dot32(lf, blk, _CONTRACT["nn"]) + carry
        carry = carry + jnp.sum(blk, axis=0, keepdims=True)
        out.append(jnp.where((lane >= 8) & (lane < 16), gc, jnp.where((lane >= 16) & (lane < 24), f, 0.0)))
    return jnp.concatenate(out, axis=0)


def _cums_bwd(lc, lf, dcums):
    rows = dcums.shape[0]
    lane = _lane_ids((LANES, LANES))
    is_g = (lane >= 8) & (lane < 16)
    is_f = (lane >= 16) & (lane < 24)
    carry = jnp.zeros((1, LANES), F32)
    out = [None] * (rows // LANES)
    for r in reversed(range(rows // LANES)):
        blk = dcums[r * LANES:(r + 1) * LANES]
        dg = jnp.where(is_g, blk, 0.0)
        df = jnp.where(is_f, blk, 0.0)
        out[r] = _dot32(lc, dg, _CONTRACT["tn"]) + _dot32(lf, df, _CONTRACT["tn"]) + carry
        carry = carry + jnp.sum(df, axis=0, keepdims=True)
    return jnp.concatenate(out, axis=0)


def _expand_consts():
    xb = np.zeros((LANES, WIDTH), np.float32)
    xg = np.zeros((LANES, WIDTH), np.float32)
    for h in range(HEADS):
        xb[h, h * HEAD_DIM:(h + 1) * HEAD_DIM] = 1.0
        xg[8 + h, h * HEAD_DIM:(h + 1) * HEAD_DIM] = 1.0
    return jnp.asarray(xb), jnp.asarray(xg)


def _shift_down(x, s):
    if s == 0:
        return x
    row = lax.broadcasted_iota(jnp.int32, x.shape, 0)
    return jnp.where(row >= s, pltpu.roll(x, s, 0), 0.0)


def _shift_up(x, s):
    if s == 0:
        return x
    n = x.shape[0]
    row = lax.broadcasted_iota(jnp.int32, x.shape, 0)
    return jnp.where(row < n - s, pltpu.roll(x, n - s, 0), 0.0)


def _row_of(cw, i):
    row = lax.broadcasted_iota(jnp.int32, cw.shape, 0)
    return jnp.sum(jnp.where(row == i, cw, 0.0), axis=0, keepdims=True)


def _conv(cw, x):
    c = jnp.zeros_like(x)
    for i in range(CONV_K):
        c = c + _row_of(cw, i) * _shift_down(x, CONV_K - 1 - i)
    return c


def _post_conv(is_qk, c):
    s = _silu(c)
    n = s * lax.rsqrt(_pair_sum(s * s) + EPS)
    return jnp.where(is_qk, n, s)


def _gdn_prep_fwd(col, cw, x):
    return (_post_conv(col < 2 * PAIRS, _conv(cw, x)),)


def _gdn_prep_bwd(is_qk, cw, x, dy):
    c = _conv(cw, x)
    _, vjp = jax.vjp(lambda cc: _post_conv(is_qk, cc), c)
    (dc,) = vjp(dy)
    dx = jnp.zeros_like(x)
    row = lax.broadcasted_iota(jnp.int32, cw.shape, 0)
    dcw = jnp.zeros(cw.shape, F32)
    for i in range(CONV_K):
        s = CONV_K - 1 - i
        dx = dx + _row_of(cw, i) * _shift_up(dc, s)
        dcw = dcw + jnp.where(row == i, jnp.sum(dc * _shift_down(x, s), axis=0, keepdims=True), 0.0)
    return dx, dcw


def _head_rms(w, x):
    return x * lax.rsqrt(_pair_sum(x * x) / HEAD_DIM + EPS) * w


def _cat_weights(w_in_t):
    tail = jnp.pad(w_in_t[4112:4120], ((0, D_CAT - D_IN), (0, 0)))
    return jnp.concatenate([w_in_t[:2048], w_in_t[2064:4112], w_in_t[2048:2064], tail], axis=0)


def _uncat_grad(g):
    return jnp.concatenate([g[:2048], g[4096:4112], g[2048:4096], g[4112:4120]], axis=0)


def _lanes_to_rowform(v8, rows):
    return v8.reshape(rows // CHUNK, CHUNK, HEADS).transpose(0, 2, 1).reshape(rows // CHUNK, PAIRS, 1, LANES)


def _rowform_to_lanes(v, rows):
    return v.reshape(rows // CHUNK, HEADS, CHUNK).transpose(0, 2, 1).reshape(rows, HEADS)


def _local_step(x, target, norm1_w, w_cat, conv_w, a_log, dt_bias, out_norm_w, f_bias, q_norm_w, k_norm_w,
                norm2_w, final_w, late_weights, early_grads_ready):
    rows = x.shape[0]
    tm = min(256, rows)
    lc, lf = _tri_consts()
    xb, xg = _expand_consts()

    (h1,) = _tiles(lambda col, w, xx: (_rms(xx, w),), name="norm1", rows=rows, tm=tm,
                   full_consts=[norm1_w], row_ins=[(x, D_MODEL, 0)], row_outs=[(D_MODEL, BF16)])
    proj = _mm(h1, w_cat, dims="nt", name="in_proj", tn=384, tk=1024)

    lane_pad = lambda v, off: jnp.pad(v.reshape(1, -1), ((0, 0), (off, LANES - off - v.size)))
    p_a, p_dt, p_fb = lane_pad(a_log, 8), lane_pad(dt_bias, 8), lane_pad(f_bias, 16)

    def gates_fwd(col, lcv, lfv, a, dt, fb, pre):
        gates = _gates_elem(a, dt, fb, pre)
        return gates, _cums_fwd(lcv, lfv, gates)

    gates, cums = _tiles(gates_fwd, name="gates", rows=rows, tm=rows,
                         full_consts=[lc, lf, p_a, p_dt, p_fb], row_ins=[(proj, LANES, COL_SMALL)],
                         row_outs=[(LANES, F32), (LANES, F32)])

    def expand_fwd(col, b, g, gt, cm):
        return (_dot32(gt, b, _CONTRACT["nn"]), _dot32(cm, g, _CONTRACT["nn"]))

    betax, gcx = _tiles(expand_fwd, name="expand", rows=rows, tm=tm, full_consts=[xb, xg],
                        row_ins=[(gates, LANES, 0), (cums, LANES, 0)],
                        row_outs=[(WIDTH, F32)] * 2)
    grow = _lanes_to_rowform(cums[:, 8:16], rows)
    frow = cums[:, 16:24].T.reshape(PAIRS, 2, rows)

    (qkv,) = _tiles(_gdn_prep_fwd, name="gdn_prep", rows=rows, tm=rows, ncol=3 * PAIRS,
                    col_consts=[(conv_w, CONV_K, LANES, 0)], row_ins=[(proj, LANES, 0)],
                    row_outs=[(LANES, F32)])
    o_gdn, ssave, tsave = _gdn_forward(qkv, betax, gcx, grow, rows)

    w_qk = jnp.concatenate([jnp.tile(q_norm_w.reshape(1, -1), (1, HEADS)),
                            jnp.tile(k_norm_w.reshape(1, -1), (1, HEADS))], axis=1)
    fox_off = 2048 // LANES
    (fqk,) = _tiles(lambda col, w, xx: (_head_rms(w, xx),), name="fox_prep", rows=rows, tm=rows, ncol=2 * PAIRS,
                    col_consts=[(w_qk, 1, LANES, 0)], row_ins=[(proj, LANES, fox_off)],
                    row_outs=[(LANES, F32)])
    ao, lse = _attention_forward(fqk, proj, frow, rows)

    w_on = jnp.tile(out_norm_w.reshape(1, -1), (1, 2))
    z_off, fg_off = 1536 // LANES, 3584 // LANES
    mix_g_fn = lambda w, o, z: _head_rms(w, o) * _silu(z)
    mix_f_fn = lambda a, g: a * jax.nn.sigmoid(g)
    (mix_g,) = _tiles(lambda col, w, o, z: (mix_g_fn(w, o, z),), name="mix_gdn", rows=rows, tm=rows, ncol=PAIRS,
                      full_consts=[w_on], row_ins=[(o_gdn, LANES, 0), (proj, LANES, z_off)],
                      row_outs=[(LANES, BF16)])
    (mix_f,) = _tiles(lambda col, a, g: (mix_f_fn(a, g),), name="mix_fox", rows=rows, tm=rows, ncol=PAIRS,
                      row_ins=[(ao, LANES, 0), (proj, LANES, fg_off)], row_outs=[(LANES, BF16)])
    mix = jnp.concatenate([mix_g, mix_f], axis=1)
    w_out, w_gate, w_up, w_down = late_weights(mix)
    x1 = _mm(mix, w_out, dims="nn", name="out_proj", add=x, tk=1024)

    (h2,) = _tiles(lambda col, w, xx: (_rms(xx, w),), name="norm2", rows=rows, tm=tm,
                   full_consts=[norm2_w], row_ins=[(x1, D_MODEL, 0)], row_outs=[(D_MODEL, BF16)])
    t_rows, t_cols, t_act = min(1024, rows), 512, min(512, rows)
    n_rt = rows // t_rows
    st_act = jax.ShapeDtypeStruct((N_CHIPS, rows, FF_SHARD), F32)
    st_rows = pl.BlockSpec((None, t_rows, FF_SHARD), lambda i, j: (j, i, 0))
    out_rows = pl.BlockSpec((t_rows, t_cols), lambda i, n: (i, n))
    flat = lambda t: t.reshape(N_CHIPS * rows, FF_SHARD)

    def ffn_in(w_st, name):
        return _mm_blocks(h2, w_st, name=name, grid=(n_rt, N_CHIPS), dims="nt",
                          a_spec=pl.BlockSpec((t_rows, D_MODEL), lambda i, j: (i, 0)),
                          b_spec=pl.BlockSpec((None, FF_SHARD, D_MODEL), lambda i, j: (j, 0, 0)),
                          o_spec=st_rows, out_shape=st_act)

    gate, up = ffn_in(w_gate, "ffn_gate"), ffn_in(w_up, "ffn_up")
    act_fn = lambda g, u: _silu(g) * u
    (act,) = _tiles(lambda col, g, u: (act_fn(g, u),), name="ffn_act", rows=N_CHIPS * rows, tm=t_act,
                    row_ins=[(flat(gate), FF_SHARD, 0), (flat(up), FF_SHARD, 0)], row_outs=[(FF_SHARD, BF16)])
    act = act.reshape(st_act.shape)
    x2 = _mm_blocks(act, w_down, name="ffn_down", grid=(n_rt, D_MODEL // t_cols), dims="nn", n_sum=N_CHIPS,
                    a_spec=pl.BlockSpec((N_CHIPS, t_rows, FF_SHARD), lambda i, n: (0, i, 0)),
                    b_spec=pl.BlockSpec((N_CHIPS, FF_SHARD, t_cols), lambda i, n: (0, 0, n)),
                    o_spec=out_rows, out_shape=jax.ShapeDtypeStruct((rows, D_MODEL), F32),
                    add=x1, add_spec=out_rows)

    def final_fn(col, w, xx, tgt):
        y, vjp = jax.vjp(_rms, xx, w)
        err = y - tgt
        loss = 0.5 * jnp.sum(err * err) / D_MODEL
        dx, dw = vjp(err / D_MODEL)
        return dx, dx, jnp.full((1, LANES), loss, F32), dw

    dx2, dx2_b, loss, d_final_w = _tiles(final_fn, name="final_loss", rows=rows, tm=tm, full_consts=[final_w],
                                         row_ins=[(x2, D_MODEL, 0), (target, D_MODEL, 0)],
                                         row_outs=[(D_MODEL, F32), (D_MODEL, BF16)],
                                         acc_outs=[(1, LANES), (1, D_MODEL)])

    dact = _mm_blocks(dx2_b, w_down, name="d_act", grid=(n_rt, N_CHIPS), dims="nt",
                      a_spec=pl.BlockSpec((t_rows, D_MODEL), lambda i, j: (i, 0)),
                      b_spec=pl.BlockSpec((None, FF_SHARD, D_MODEL), lambda i, j: (j, 0, 0)),
                      o_spec=st_rows, out_shape=st_act)
    def g_ffn(d_st, other, name):
        return _mm_blocks(d_st, other, name=name, grid=(N_CHIPS, D_MODEL // t_cols), dims="tn",
                          a_spec=pl.BlockSpec((None, rows, FF_SHARD), lambda j, n: (j, 0, 0)),
                          b_spec=pl.BlockSpec((rows, t_cols), lambda j, n: (0, n)),
                          o_spec=pl.BlockSpec((None, FF_SHARD, t_cols), lambda j, n: (j, 0, n)),
                          out_shape=jax.ShapeDtypeStruct((N_CHIPS, FF_SHARD, D_MODEL), F32))

    g_down = g_ffn(act, dx2_b, "g_down")

    def act_bwd(col, g, u, d):
        _, vjp = jax.vjp(act_fn, g, u)
        return vjp(d)

    dgate, dup = _tiles(act_bwd, name="ffn_act_bwd", rows=N_CHIPS * rows, tm=t_act,
                        row_ins=[(flat(gate), FF_SHARD, 0), (flat(up), FF_SHARD, 0), (flat(dact), FF_SHARD, 0)],
                        row_outs=[(FF_SHARD, BF16), (FF_SHARD, BF16)])
    dgate, dup = dgate.reshape(st_act.shape), dup.reshape(st_act.shape)

    def d_h2(d_st, w_st, name, add):
        return _mm_blocks(d_st, w_st, name=name, grid=(n_rt, D_MODEL // t_cols), dims="nn", n_sum=N_CHIPS,
                          a_spec=pl.BlockSpec((N_CHIPS, t_rows, FF_SHARD), lambda i, n: (0, i, 0)),
                          b_spec=pl.BlockSpec((N_CHIPS, FF_SHARD, t_cols), lambda i, n: (0, 0, n)),
                          o_spec=out_rows, out_shape=jax.ShapeDtypeStruct((rows, D_MODEL), F32),
                          add=add, add_spec=out_rows)

    dh2 = d_h2(dup, w_up, "d_h2_up", d_h2(dgate, w_gate, "d_h2_gate", None))
    g_gate, g_up = g_ffn(dgate, h2, "g_gate"), g_ffn(dup, h2, "g_up")

    def norm_bwd(col, w, xx, dh, dres):
        _, vjp = jax.vjp(_rms, xx, w)
        dx, dw = vjp(dh)
        return dx + dres, dx + dres, dw

    dx1, dx1_b, d_norm2_w = _tiles(norm_bwd, name="norm2_bwd", rows=rows, tm=tm, full_consts=[norm2_w],
                                   row_ins=[(x1, D_MODEL, 0), (dh2, D_MODEL, 0), (dx2, D_MODEL, 0)],
                                   row_outs=[(D_MODEL, F32), (D_MODEL, BF16)], acc_outs=[(1, D_MODEL)])
    dmix = _mm(dx1_b, w_out, dims="nt", name="d_mix", tk=1024)
    g_out = _mm(mix, dx1_b, dims="tn", name="g_out", tk=rows)
    w_on = w_on + early_grads_ready(g_out, g_gate, g_up, g_down)

    def mix_g_bwd(col, w, o, z, d):
        _, vjp = jax.vjp(mix_g_fn, w, o, z)
        dw, do_, dz = vjp(d)
        return do_, dz, dw

    do_gdn, dz, d_on = _tiles(mix_g_bwd, name="mix_gdn_bwd", rows=rows, tm=rows, ncol=PAIRS, full_consts=[w_on],
                              row_ins=[(o_gdn, LANES, 0), (proj, LANES, z_off), (dmix, LANES, 0)],
                              row_outs=[(LANES, F32), (LANES, BF16)], acc_outs=[(1, LANES)])

    def mix_f_bwd(col, a, g, d):
        _, vjp = jax.vjp(mix_f_fn, a, g)
        return vjp(d)

    dao, dfgate = _tiles(mix_f_bwd, name="mix_fox_bwd", rows=rows, tm=rows, ncol=PAIRS,
                         row_ins=[(ao, LANES, 0), (proj, LANES, fg_off), (dmix, LANES, PAIRS)],
                         row_outs=[(LANES, F32), (LANES, BF16)])

    delta = _attention_delta(fqk, proj, frow, lse, dao, rows)
    dfq, dfk, dfv, dfrow = _attention_backward(fqk, proj, frow, delta, lse, dao, rows)

    def fox_prep_bwd(col, w, xx, d):
        _, vjp = jax.vjp(_head_rms, w, xx)
        dw, dx = vjp(d)
        return dx, dw

    dfqk, d_wqk = [], []
    for part, d_n in enumerate((dfq, dfk)):
        dx_p, dw_p = _tiles(fox_prep_bwd, name="fox_prep_bwd_" + "qk"[part], rows=rows, tm=rows, ncol=PAIRS,
                            col_consts=[(w_qk, 1, LANES, part * PAIRS)],
                            row_ins=[(proj, LANES, fox_off + part * PAIRS), (d_n, LANES, 0)],
                            row_outs=[(LANES, BF16)], acc_outs=[(1, LANES)])
        dfqk.append(dx_p)
        d_wqk.append(dw_p)

    dq, dk, dv, dbetax, dgcx, dgrow = _gdn_backward(qkv, betax, gcx, grow, ssave, tsave, do_gdn, rows)
    dqkv, d_conv = [], []
    for part, d_n in enumerate((dq, dk, dv)):
        prep_bwd = lambda col, cw, xx, dy, is_qk=(part < 2): _gdn_prep_bwd(is_qk, cw, xx, dy)
        dx_p, dw_p = _tiles(prep_bwd, name="gdn_prep_bwd_" + "qkv"[part], rows=rows, tm=rows, ncol=PAIRS,
                            col_consts=[(conv_w, CONV_K, LANES, part * PAIRS)],
                            row_ins=[(proj, LANES, part * PAIRS), (d_n, LANES, 0)],
                            row_outs=[(LANES, BF16)], acc_outs=[(CONV_K, LANES)])
        dqkv.append(dx_p)
        d_conv.append(dw_p)
    d_conv = jnp.concatenate(d_conv, axis=1)

    def expand_bwd(col, b, g, db, dg):
        return (_dot32(db, b, _CONTRACT["nt"]), _dot32(dg, g, _CONTRACT["nt"]))

    dgates_b, dcums_g = _tiles(expand_bwd, name="expand_bwd", rows=rows, tm=tm, full_consts=[xb, xg],
                               row_ins=[(dbetax, WIDTH, 0), (dgcx, WIDTH, 0)],
                               row_outs=[(LANES, F32), (LANES, F32)])
    dcums_row = jnp.concatenate([jnp.zeros((rows, 8), F32), _rowform_to_lanes(dgrow, rows),
                                 dfrow.reshape(HEADS, rows).T, jnp.zeros((rows, LANES - 24), F32)], axis=1)

    def gates_bwd(col, lcv, lfv, a, dt, fb, pre, dgb, dcg, dcr):
        lane = _lane_ids(pre.shape)
        dgates = jnp.where(lane < 8, dgb, _cums_bwd(lcv, lfv, dcg + dcr))
        _, vjp = jax.vjp(_gates_elem, a, dt, fb, pre)
        da, ddt, dfb, dpre = vjp(dgates)
        return dpre, da, ddt, dfb

    dpre, d_a, d_dt, d_fb = _tiles(gates_bwd, name="gates_bwd", rows=rows, tm=rows,
                                   full_consts=[lc, lf, p_a, p_dt, p_fb],
                                   row_ins=[(proj, LANES, COL_SMALL), (dgates_b, LANES, 0), (dcums_g, LANES, 0),
                                            (dcums_row, LANES, 0)],
                                   row_outs=[(LANES, BF16)], acc_outs=[(1, LANES)] * 3)

    dproj = jnp.concatenate(dqkv + [dz] + dfqk + [dfv, dfgate, dpre], axis=1)
    dh1 = _mm(dproj, w_cat, dims="nn", name="d_h1", tk=D_CAT)
    g_cat = _mm(dproj, h1, dims="tn", name="g_in", tm=384, tn=D_MODEL, tk=rows)

    def norm1_bwd(col, w, xx, dh, dres):
        _, vjp = jax.vjp(_rms, xx, w)
        dx, dw = vjp(dh)
        return dx + dres, dw

    grad_x, d_norm1_w = _tiles(norm1_bwd, name="norm1_bwd", rows=rows, tm=tm, full_consts=[norm1_w],
                               row_ins=[(x, D_MODEL, 0), (dh1, D_MODEL, 0), (dx1, D_MODEL, 0)],
                               row_outs=[(D_MODEL, F32)], acc_outs=[(1, D_MODEL)])

    fold = lambda v: v.reshape(-1, HEAD_DIM).sum(axis=0)
    small = dict(
        loss=loss[0, 0],
        norm1_w=d_norm1_w, conv_w=d_conv, a_log=d_a[0, 8:16], dt_bias=d_dt[0, 8:16],
        out_norm_w=fold(d_on), f_bias=d_fb[0, 16:24], q_norm_w=fold(d_wqk[0]),
        k_norm_w=fold(d_wqk[1]), norm2_w=d_norm2_w, final_w=d_final_w)
    return grad_x, g_cat, g_out, g_gate, g_up, g_down, small


HBM_SPEC = pl.BlockSpec(memory_space=pltpu.HBM)


def _place():
    x, y, c = lax.axis_index("x"), lax.axis_index("y"), lax.axis_index("c")
    chips = [(1 - x, y), (x, 1 - y), (1 - x, 1 - y)]
    return x, y, c, 2 * x + y, (x, y, 1 - c), chips, [2 * cx + cy for cx, cy in chips]


def _remote(src, dst, send_sem, recv_sem, to):
    return pltpu.make_async_remote_copy(src_ref=src, dst_ref=dst, send_sem=send_sem, recv_sem=recv_sem,
                                        device_id=to, device_id_type=MESH)


def _allgather_weights(shards, conv):
    n = len(shards)
    halves = [s.shape[1] // 2 for s in shards]
    per = 6
    own_base = n * per + 3

    def body(*refs):
        ins, conv_in = refs[:n], refs[n]
        outs, conv_out = refs[n + 1:2 * n + 1], refs[2 * n + 1]
        send_sems, recv_sems = refs[2 * n + 2:]
        x, y, c, own, sib, chips, chip_idx = _place()

        def half(i, ref, hc):
            return ref.at[:, pl.ds(pl.multiple_of(hc * halves[i], LANES), halves[i])]

        sent = []
        for i, (src, dst) in enumerate(zip(list(ins) + [conv_in], list(outs) + [conv_out])):
            k = own_base + i
            sent.append(_remote(src, dst.at[own], send_sems.at[k], recv_sems.at[k], sib))
        for i in range(n):
            for j, chip in enumerate(chips):
                k = i * per + j
                sent.append(_remote(half(i, ins[i], c), half(i, outs[i].at[own], c),
                                    send_sems.at[k], recv_sems.at[k], (*chip, c)))
        for j, chip in enumerate(chips):
            k = n * per + j
            sent.append(_remote(conv_in, conv_out.at[own], send_sems.at[k], recv_sems.at[k], (*chip, c)))
        for cp in sent:
            cp.start()
        for i in range(n):
            for j in range(len(chips)):
                k = i * per + j
                landed = half(i, outs[i].at[chip_idx[j]], c)
                _remote(landed, landed, send_sems.at[k], recv_sems.at[k], sib).wait_recv()
                fwd = _remote(landed, landed, send_sems.at[k + 3], recv_sems.at[k + 3], sib)
                fwd.start()
                sent.append(fwd)
        for i in range(n):
            for j in range(len(chips)):
                k = i * per + 3 + j
                landed = half(i, outs[i].at[chip_idx[j]], 1 - c)
                _remote(landed, landed, send_sems.at[k], recv_sems.at[k], sib).wait_recv()
        for j in range(len(chips)):
            k = n * per + j
            landed = conv_out.at[chip_idx[j]]
            _remote(landed, landed, send_sems.at[k], recv_sems.at[k], sib).wait_recv()
        for i, dst in enumerate(list(outs) + [conv_out]):
            k = own_base + i
            landed = dst.at[own]
            _remote(landed, landed, send_sems.at[k], recv_sems.at[k], sib).wait_recv()
        for cp in sent:
            cp.wait_send()

    n_sem = own_base + n + 1
    out_shape = [jax.ShapeDtypeStruct((N_CHIPS,) + s.shape, s.dtype) for s in shards]
    out_shape.append(jax.ShapeDtypeStruct((N_CHIPS,) + conv.shape, conv.dtype))
    res = pl.pallas_call(
        body, name="allgather_weights", out_shape=out_shape,
        in_specs=[HBM_SPEC] * (n + 1), out_specs=[HBM_SPEC] * (n + 1),
        scratch_shapes=[pltpu.SemaphoreType.DMA((n_sem,)), pltpu.SemaphoreType.DMA((n_sem,))],
    )(*shards, conv)
    return res[:n], res[n]


SEM_SPEC = pl.BlockSpec(memory_space=pltpu.SEMAPHORE)
ANY_SPEC = pl.BlockSpec(memory_space=pl.ANY)
DATAFLOW = pltpu.SideEffectType.DATAFLOW_SIDE_EFFECTING


def _gather_plan(srcs, lands):
    x, y, c, own, sib, chips, chip_idx = _place()
    plan = []
    for src, land in zip(srcs, lands):
        for j, chip in enumerate(chips):
            plan.append((src, land.at[own], (*chip, c), land.at[chip_idx[j]]))
        plan.append((src, land.at[own], sib, land.at[own]))
    return plan


def _exchange_plan(srcs, lands):
    x, y, c, own, sib, chips, chip_idx = _place()
    plan = []
    for src, land in zip(srcs, lands):
        for j, chip in enumerate(chips):
            plan.append((src.at[chip_idx[j]], land.at[j], (*chip, c), land.at[j]))
    return plan


def _split_start(name, plan_fn, srcs, land_shapes, n_copies, after):
    n = len(srcs)

    def body(*refs):
        src_refs, land_refs = refs[:n], refs[n:2 * n]
        send_sems, recv_sems = refs[2 * n + 1], refs[2 * n + 2]
        token = refs[-1]
        for k, (src, dst, to, _) in enumerate(plan_fn(src_refs, land_refs)):
            _remote(src, dst, send_sems.at[k], recv_sems.at[k], to).start()
        token[...] = jnp.zeros_like(token)

    lands = [pltpu.with_memory_space_constraint(lax.empty(s.shape, s.dtype), pltpu.HBM) for s in land_shapes]
    srcs = [pltpu.with_memory_space_constraint(s, pltpu.HBM) for s in srcs]
    out_shape = ([pltpu.SemaphoreType.DMA((n_copies,)), pltpu.SemaphoreType.DMA((n_copies,))]
                 + [pltpu.HBM(s.shape, s.dtype) for s in srcs] + [pltpu.HBM(s.shape, s.dtype) for s in land_shapes]
                 + [jax.ShapeDtypeStruct((8, LANES), F32)])
    res = pl.pallas_call(
        body, name=name, out_shape=out_shape,
        in_specs=[HBM_SPEC] * (2 * n) + [ANY_SPEC],
        out_specs=[SEM_SPEC, SEM_SPEC] + [HBM_SPEC] * (2 * n) + [pl.BlockSpec(memory_space=pltpu.VMEM)],
        input_output_aliases={i: 2 + i for i in range(2 * n)},
        compiler_params=pltpu.CompilerParams(has_side_effects=DATAFLOW),
    )(*srcs, *lands, after)
    return dict(sems=res[:2], srcs=res[2:2 + n], lands=res[2 + n:2 + 2 * n], token=res[-1], n=n)


def _split_wait(name, plan_fn, started, after):
    n = started["n"]

    def body(*refs):
        src_refs, land_refs = refs[:n], refs[n:2 * n]
        send_sems, recv_sems = refs[2 * n], refs[2 * n + 1]
        for k, (src, _, to, landed) in enumerate(plan_fn(src_refs, land_refs)):
            copy = _remote(src, landed, send_sems.at[k], recv_sems.at[k], to)
            copy.wait_send()
            copy.wait_recv()

    srcs, lands = started["srcs"], started["lands"]
    res = pl.pallas_call(
        body, name=name,
        out_shape=[pltpu.HBM(s.shape, s.dtype) for s in srcs] + [pltpu.HBM(s.shape, s.dtype) for s in lands],
        in_specs=[HBM_SPEC] * (2 * n) + [SEM_SPEC, SEM_SPEC, ANY_SPEC],
        out_specs=[HBM_SPEC] * (2 * n),
        input_output_aliases={i: i for i in range(2 * n)},
        compiler_params=pltpu.CompilerParams(has_side_effects=DATAFLOW),
    )(*srcs, *lands, *started["sems"], after)
    return res[n:]


def _swap_halves(stacks, name):
    n = len(stacks)

    def body(*refs):
        ins, outs = refs[:n], refs[n:2 * n]
        send_sems, recv_sems = refs[2 * n:]
        x, y, c, own, sib, chips, chip_idx = _place()
        cps = []
        for i in range(n):
            h = stacks[i].shape[2] // 2
            src = ins[i].at[:, :, pl.ds(pl.multiple_of((1 - c) * h, LANES), h)]
            cps.append(_remote(src, outs[i], send_sems.at[i], recv_sems.at[i], sib))
        for cp in cps:
            cp.start()
        for cp in cps:
            cp.wait()

    out_shape = [jax.ShapeDtypeStruct((N_CHIPS, s.shape[1], s.shape[2] // 2), s.dtype) for s in stacks]
    return pl.pallas_call(
        body, name=name, out_shape=out_shape,
        in_specs=[HBM_SPEC] * n, out_specs=[HBM_SPEC] * n,
        scratch_shapes=[pltpu.SemaphoreType.DMA((n,)), pltpu.SemaphoreType.DMA((n,))],
    )(*stacks)


def _add_half(stack, landed, place, name):
    _, rows, h = landed.shape

    def body(place_ref, a_ref, b_ref, o_ref, own_ref):
        part = (a_ref[...] + b_ref[...]).astype(o_ref.dtype)
        o_ref[...] = part

        @pl.when(pl.program_id(0) == place_ref[1])
        def _():
            own_ref[...] = part[0]

    return pl.pallas_call(
        body, name=name,
        out_shape=[jax.ShapeDtypeStruct(landed.shape, BF16), jax.ShapeDtypeStruct((rows, h), BF16)],
        grid_spec=pltpu.PrefetchScalarGridSpec(
            num_scalar_prefetch=1, grid=(N_CHIPS,),
            in_specs=[pl.BlockSpec((1, rows, h), lambda j, p: (j, 0, p[0])),
                      pl.BlockSpec((1, rows, h), lambda j, p: (j, 0, 0))],
            out_specs=[pl.BlockSpec((1, rows, h), lambda j, p: (j, 0, 0)),
                       pl.BlockSpec((rows, h), lambda j, p: (0, 0))]),
        compiler_params=_params(("arbitrary",)),
    )(place, stack, landed)


def _exchange_partials(parts):
    n = len(parts)

    def body(*refs):
        ins, outs = refs[:n], refs[n:2 * n]
        send_sems, recv_sems = refs[2 * n:]
        x, y, c, own, sib, chips, chip_idx = _place()
        sent = []
        for i in range(n):
            for j, chip in enumerate(chips):
                k = i * 3 + j
                sent.append(_remote(ins[i].at[chip_idx[j]], outs[i].at[j], send_sems.at[k], recv_sems.at[k],
                                    (*chip, c)))
        for cp in sent:
            cp.start()
        for i in range(n):
            for j in range(len(chips)):
                k = i * 3 + j
                landed = outs[i].at[j]
                _remote(landed, landed, send_sems.at[k], recv_sems.at[k], sib).wait_recv()
        for cp in sent:
            cp.wait_send()

    return pl.pallas_call(
        body, name="rs_exchange_partials",
        out_shape=[jax.ShapeDtypeStruct((3,) + p.shape[1:], p.dtype) for p in parts],
        in_specs=[HBM_SPEC] * n, out_specs=[HBM_SPEC] * n,
        scratch_shapes=[pltpu.SemaphoreType.DMA((3 * n,)), pltpu.SemaphoreType.DMA((3 * n,))],
    )(*parts)


def _sum_partials(own_part, landed, name):
    _, h, cols = landed.shape

    def body(own_ref, a_ref, o_ref):
        acc = own_ref[...].astype(F32)
        for s in range(3):
            acc = acc + a_ref[s].astype(F32)
        o_ref[...] = acc

    return pl.pallas_call(
        body, name=name, out_shape=jax.ShapeDtypeStruct((h, cols), F32), grid=(1,),
        in_specs=[pl.BlockSpec((h, cols), lambda i: (0, 0)), pl.BlockSpec(landed.shape, lambda i: (0, 0, 0))],
        out_specs=pl.BlockSpec((h, cols), lambda i: (0, 0)),
        compiler_params=_params(("arbitrary",)),
    )(own_part, landed)


def _share_halves(halves, name):
    n = len(halves)

    def body(*refs):
        ins, outs = refs[:n], refs[n:2 * n]
        send_sems, recv_sems = refs[2 * n:]
        x, y, c, own, sib, chips, chip_idx = _place()
        cps = [_remote(ins[i], outs[i], send_sems.at[i], recv_sems.at[i], sib) for i in range(n)]
        for cp in cps:
            cp.start()
        for cp in cps:
            cp.wait()

    return pl.pallas_call(
        body, name=name,
        out_shape=[jax.ShapeDtypeStruct(p.shape, p.dtype) for p in halves],
        in_specs=[HBM_SPEC] * n, out_specs=[HBM_SPEC] * n,
        scratch_shapes=[pltpu.SemaphoreType.DMA((n,)), pltpu.SemaphoreType.DMA((n,))],
    )(*halves)


def _allreduce_small(packed):
    rows = packed.shape[0]
    n_dev = 8

    def body(in_ref, out_ref, gath, send_sems, recv_sems):
        x, y, c = lax.axis_index("x"), lax.axis_index("y"), lax.axis_index("c")
        me = 4 * x + 2 * y + c
        gath[me] = in_ref[...]
        cps = []
        for k in range(1, n_dev):
            fx, fy, fc = (k >> 2) & 1, (k >> 1) & 1, k & 1
            to = (x ^ fx, y ^ fy, c ^ fc)
            cps.append(_remote(in_ref, gath.at[me], send_sems.at[k - 1], recv_sems.at[k - 1], to))
        for cp in cps:
            cp.start()
        for k in range(1, n_dev):
            fx, fy, fc = (k >> 2) & 1, (k >> 1) & 1, k & 1
            src = 4 * (x ^ fx) + 2 * (y ^ fy) + (c ^ fc)
            slot = gath.at[src]
            _remote(slot, slot, send_sems.at[k - 1], recv_sems.at[k - 1], (x, y, c)).wait_recv()
        for cp in cps:
            cp.wait_send()
        acc = gath[0]
        for d in range(1, n_dev):
            acc = acc + gath[d]
        out_ref[...] = acc

    vm = pl.BlockSpec(memory_space=pltpu.VMEM)
    return pl.pallas_call(
        body, name="allreduce_small", out_shape=jax.ShapeDtypeStruct(packed.shape, F32),
        in_specs=[vm], out_specs=vm,
        scratch_shapes=[pltpu.VMEM((n_dev, rows, LANES), F32),
                        pltpu.SemaphoreType.DMA((n_dev - 1,)), pltpu.SemaphoreType.DMA((n_dev - 1,))],
    )(packed)


def _adam(col, w, g, m, v):
    m2 = ADAM_B1 * m + (1.0 - ADAM_B1) * g
    v2 = ADAM_B2 * v + (1.0 - ADAM_B2) * (g * g)
    m_hat = m2 / (1.0 - ADAM_B1 ** ADAM_STEP)
    v_hat = v2 / (1.0 - ADAM_B2 ** ADAM_STEP)
    delta = -ADAM_LR * (m_hat / (jnp.sqrt(v_hat) + ADAM_EPS) + ADAM_WD * w)
    return delta, m2, v2


def _adam_call(w, g, m, v, name):
    rows, cols = w.shape
    tm = rows
    for cand in (256, 352, 176, 128, 64, 48, 16, 8):
        if rows % cand == 0:
            tm = cand
            break
    return _tiles(_adam, name=name, rows=rows, tm=tm,
                  row_ins=[(w, cols, 0), (g, cols, 0), (m, cols, 0), (v, cols, 0)],
                  row_outs=[(cols, F32)] * 3)


def _adam_big(w, g_mine, g_other, m, v, place, name):
    rows, cols = w.shape
    tc = 256
    nt = cols // 2 // tc

    def body(place_ref, w_ref, gm_ref, go_ref, m_ref, v_ref, g_out, d_out, m_out, v_out):
        g = jnp.where(pl.program_id(0) == place_ref[0], gm_ref[...], go_ref[...])
        d, m2, v2 = _adam(None, w_ref[...], g, m_ref[...], v_ref[...])
        g_out[...] = g
        d_out[...] = d
        m_out[...] = m2
        v_out[...] = v2

    full = pl.BlockSpec((rows, tc), lambda hh, i, p: (0, hh * nt + i))
    half = pl.BlockSpec((rows, tc), lambda hh, i, p: (0, i))
    return pl.pallas_call(
        body, name=name, out_shape=[jax.ShapeDtypeStruct(w.shape, F32)] * 4,
        grid_spec=pltpu.PrefetchScalarGridSpec(
            num_scalar_prefetch=1, grid=(2, nt),
            in_specs=[full, half, half, full, full], out_specs=[full] * 4),
        compiler_params=_params(("arbitrary", "arbitrary")),
    )(place, w, g_mine, g_other, m, v)


def _pack(arrays):
    flat = []
    for a in arrays:
        a = a.reshape(-1).astype(F32)
        flat.append(jnp.pad(a, (0, (-a.size) % LANES)))
    out = jnp.concatenate(flat)
    out = jnp.pad(out, (0, (-out.size) % (8 * LANES)))
    return out.reshape(-1, LANES)


def _unpack(packed, shapes):
    flat = packed.reshape(-1)
    out, off = [], 0
    for s in shapes:
        size = int(np.prod(s))
        out.append(flat[off:off + size].reshape(s))
        off += size + (-size) % LANES
    return out


def kernel(x, norm1_w, w_in, gdn_conv_w, gdn_A_log, gdn_dt_bias, gdn_out_norm_w, fox_f_bias, fox_q_norm_w, fox_k_norm_w, w_out, norm2_w, w_ffn_gate, w_ffn_up, w_ffn_down, final_norm_w, loss_target, m_norm1_w, m_w_in, m_gdn_conv_w, m_gdn_A_log, m_gdn_dt_bias, m_gdn_out_norm_w, m_fox_f_bias, m_fox_q_norm_w, m_fox_k_norm_w, m_w_out, m_norm2_w, m_w_ffn_gate, m_w_ffn_up, m_w_ffn_down, m_final_norm_w, v_norm1_w, v_w_in, v_gdn_conv_w, v_gdn_A_log, v_gdn_dt_bias, v_gdn_out_norm_w, v_fox_f_bias, v_fox_q_norm_w, v_fox_k_norm_w, v_w_out, v_norm2_w, v_w_ffn_gate, v_w_ffn_up, v_w_ffn_down, v_final_norm_w):
    cx, cy, cc = lax.axis_index("x"), lax.axis_index("y"), lax.axis_index("c")
    own = 2 * cx + cy
    place = jnp.stack([cc, own]).astype(jnp.int32)

    names = ["w_in", "w_out", "w_gate", "w_up", "w_down"]
    is_t = [True, False, True, True, False]
    to_t = lambda a, t: a[0].T if t else a[0]
    from_t = lambda a, t: (a.T if t else a)[None]
    big_w = [to_t(a, t) for a, t in zip([w_in, w_out, w_ffn_gate, w_ffn_up, w_ffn_down], is_t)]
    big_m = [to_t(a, t) for a, t in zip([m_w_in, m_w_out, m_w_ffn_gate, m_w_ffn_up, m_w_ffn_down], is_t)]
    big_v = [to_t(a, t) for a, t in zip([v_w_in, v_w_out, v_w_ffn_gate, v_w_ffn_up, v_w_ffn_down], is_t)]
    shards = [w.astype(BF16) for w in big_w]
    (w_in_g,), conv_g = _allgather_weights(shards[:1], gdn_conv_w[0])
    rest = _split_start("gather_rest_start", _gather_plan, shards[1:],
                        [jax.ShapeDtypeStruct((N_CHIPS,) + s.shape, BF16) for s in shards[1:]],
                        n_copies=4 * len(shards[1:]), after=w_in_g)
    w_cat = _cat_weights(w_in_g.reshape(D_IN, D_MODEL))
    conv_full = conv_g.transpose(1, 0, 2).reshape(CONV_K, 3 * WIDTH)

    def late_weights(after):
        w_out_g, w_gate_g, w_up_g, w_down_g = _split_wait("gather_rest_wait", _gather_plan, rest, after)
        return w_out_g.reshape(D_MODEL, D_MODEL), w_gate_g, w_up_g, w_down_g

    def start_reduction(stacks, nms, tag):
        landed = _swap_halves(stacks, "rs_swap_" + tag)
        added = [_add_half(s, l, place, "rs_add_" + nm) for s, l, nm in zip(stacks, landed, nms)]
        parts = [a[0] for a in added]
        started = _split_start("exchange_" + tag + "_start", _exchange_plan, parts,
                               [jax.ShapeDtypeStruct((3,) + p.shape[1:], p.dtype) for p in parts],
                               n_copies=3 * len(parts), after=parts[0])
        return dict(own=[a[1] for a in added], started=started, tag=tag, names=nms)

    def finish_reduction(red, after, ws, ms, vs):
        landed = _split_wait("exchange_" + red["tag"] + "_wait", _exchange_plan, red["started"], after)
        halves = [_sum_partials(o, p, "rs_sum_" + nm) for o, p, nm in zip(red["own"], landed, red["names"])]
        others = _share_halves(halves, "rs_share_" + red["tag"])
        return [_adam_big(w, gm, go, m, v, place, "adam_" + nm)
                for w, gm, go, m, v, nm in zip(ws, halves, others, ms, vs, red["names"])]

    early = {}

    def early_grads_ready(g_out, g_gate, g_up, g_down):
        stacks = [g_out.reshape(N_CHIPS, D_MODEL // N_CHIPS, D_MODEL), g_gate, g_up, g_down]
        early.update(start_reduction(stacks, names[1:], "early"))
        return early["started"]["token"][0, 0]

    grad_x, g_cat, _, _, _, _, small = _local_step(
        x[0], loss_target[0], norm1_w + rest["token"][0, 0], w_cat, conv_full, gdn_A_log[0], gdn_dt_bias[0],
        gdn_out_norm_w[0], fox_f_bias[0], fox_q_norm_w[0], fox_k_norm_w[0], norm2_w, final_norm_w.reshape(1, -1),
        late_weights, early_grads_ready)

    late = start_reduction([_uncat_grad(g_cat).reshape(N_CHIPS, D_IN // N_CHIPS, D_MODEL)], names[:1], "w_in")
    big_upd = finish_reduction(early, late["started"]["token"], big_w[1:], big_m[1:], big_v[1:])

    order = ["norm1_w", "conv_w", "a_log", "dt_bias", "out_norm_w", "f_bias", "q_norm_w", "k_norm_w",
             "norm2_w", "final_w"]
    red = _allreduce_small(_pack([small[k] for k in order] + [small["loss"]]))
    red_shapes = [(1, D_MODEL), (CONV_K, 3 * WIDTH), (1, HEADS), (1, HEADS), (1, HEAD_DIM), (1, HEADS),
                  (1, HEAD_DIM), (1, HEAD_DIM), (1, D_MODEL), (D_MODEL,), ()]
    red_list = _unpack(red, red_shapes)
    loss = red_list[-1]
    small_g = dict(zip(order, red_list[:-1]))
    shard_cols = 3 * WIDTH // N_CHIPS
    small_g["conv_w"] = lax.dynamic_slice_in_dim(small_g["conv_w"], own * shard_cols, shard_cols, axis=1)[None]
    small_w = [norm1_w, gdn_conv_w, gdn_A_log, gdn_dt_bias, gdn_out_norm_w, fox_f_bias, fox_q_norm_w,
               fox_k_norm_w, norm2_w, final_norm_w]
    small_m = [m_norm1_w, m_gdn_conv_w, m_gdn_A_log, m_gdn_dt_bias, m_gdn_out_norm_w, m_fox_f_bias,
               m_fox_q_norm_w, m_fox_k_norm_w, m_norm2_w, m_final_norm_w]
    small_v = [v_norm1_w, v_gdn_conv_w, v_gdn_A_log, v_gdn_dt_bias, v_gdn_out_norm_w, v_fox_f_bias,
               v_fox_q_norm_w, v_fox_k_norm_w, v_norm2_w, v_final_norm_w]
    small_gl = [small_g[k].reshape(w.shape) for k, w in zip(order, small_w)]
    s_delta, s_m, s_v = _adam_call(_pack(small_w), _pack(small_gl), _pack(small_m), _pack(small_v), "adam_small")
    big_upd = finish_reduction(late, s_delta, big_w[:1], big_m[:1], big_v[:1]) + big_upd
    shapes = [w.shape for w in small_w]
    s_delta, s_m, s_v = _unpack(s_delta, shapes), _unpack(s_m, shapes), _unpack(s_v, shapes)

    big_pos = {1: 0, 9: 1, 11: 2, 12: 3, 13: 4}
    small_pos = {0: 0, 2: 1, 3: 2, 4: 3, 5: 4, 6: 5, 7: 6, 8: 7, 10: 8, 14: 9}
    grads, deltas, new_m, new_v = [], [], [], []
    for pos in range(15):
        if pos in big_pos:
            b = big_pos[pos]
            g, d, m2, v2 = [from_t(a, is_t[b]) for a in big_upd[b]]
            grads.append(g)
            deltas.append(d)
            new_m.append(m2)
            new_v.append(v2)
        else:
            s = small_pos[pos]
            grads.append(small_gl[s])
            deltas.append(s_delta[s])
            new_m.append(s_m[s])
            new_v.append(s_v[s])
    return (loss, grad_x[None], *grads, *deltas, *new_m, *new_v)
```

```python
import jax
import jax.numpy as jnp
import numpy as np
from jax import lax
from jax.experimental import pallas as pl
from jax.experimental.pallas import tpu as pltpu

F32 = jnp.float32
BF16 = jnp.bfloat16

D_MODEL = 1024
HEADS = 8
HEAD_DIM = 64
PAIRS = HEADS // 2
WIDTH = HEADS * HEAD_DIM
CHUNK = 64
CONV_K = 4
D_FF = 2816
FF_SHARD = D_FF // 4
EPS = 1e-6
SCALE = HEAD_DIM ** -0.5
LANES = 128
N_CHIPS = 4
D_IN = 4120
D_CAT = 4224
COL_SMALL = 4096 // LANES

ADAM_LR = 0.001
ADAM_B1 = 0.9
ADAM_B2 = 0.999
ADAM_EPS = 1e-08
ADAM_WD = 0.01
ADAM_STEP = 10

VMEM_LIMIT = 56 * 1024 * 1024
MESH = pl.DeviceIdType.MESH
HIGHEST = lax.Precision.HIGHEST


def _params(sem):
    return pltpu.CompilerParams(dimension_semantics=sem, vmem_limit_bytes=VMEM_LIMIT)


_CONTRACT = {"nn": ((1,), (0,)), "nt": ((1,), (1,)), "tn": ((0,), (0,))}


def _mm(a, b, *, dims, name, out_dtype=F32, add=None, tm=1024, tn=512, tk=512):
    if dims == "nn":
        (m, k), (k2, n) = a.shape, b.shape
    elif dims == "nt":
        (m, k), (n, k2) = a.shape, b.shape
    else:
        (k, m), (k2, n) = a.shape, b.shape
    assert k == k2, (a.shape, b.shape, dims)
    tm, tn, tk = min(tm, m), min(tn, n), min(tk, k)
    assert m % tm == 0 and n % tn == 0 and k % tk == 0, (m, n, k, tm, tn, tk)
    nk = k // tk
    a_spec = (pl.BlockSpec((tk, tm), lambda i, j, kk: (kk, i)) if dims == "tn"
              else pl.BlockSpec((tm, tk), lambda i, j, kk: (i, kk)))
    b_spec = (pl.BlockSpec((tn, tk), lambda i, j, kk: (j, kk)) if dims == "nt"
              else pl.BlockSpec((tk, tn), lambda i, j, kk: (kk, j)))
    o_spec = pl.BlockSpec((tm, tn), lambda i, j, kk: (i, j))
    contract = (_CONTRACT[dims], ((), ()))
    has_add = add is not None

    def body(*refs):
        a_ref, b_ref = refs[:2]
        add_ref = refs[2] if has_add else None
        o_ref = refs[3] if has_add else refs[2]
        part = lax.dot_general(a_ref[...].astype(BF16), b_ref[...].astype(BF16), contract,
                               preferred_element_type=F32)

        def finish(r):
            if has_add:
                r = r + add_ref[...].astype(F32)
            o_ref[...] = r.astype(out_dtype)

        if nk == 1:
            finish(part)
            return
        acc = refs[-1]
        kk = pl.program_id(2)

        @pl.when(kk == 0)
        def _():
            acc[...] = part

        @pl.when(kk > 0)
        def _():
            acc[...] += part

        @pl.when(kk == nk - 1)
        def _():
            finish(acc[...])

    ins = [a, b] + ([add] if has_add else [])
    in_specs = [a_spec, b_spec] + ([o_spec] if has_add else [])
    return pl.pallas_call(
        body, name=name, grid=(m // tm, n // tn, nk),
        in_specs=in_specs, out_specs=o_spec,
        out_shape=jax.ShapeDtypeStruct((m, n), out_dtype),
        scratch_shapes=[pltpu.VMEM((tm, tn), F32)] if nk > 1 else [],
        compiler_params=_params(("parallel", "parallel", "arbitrary")),
    )(*ins)


def _mm_blocks(a, b, *, name, grid, a_spec, b_spec, o_spec, out_shape, dims, n_sum=0, add=None, add_spec=None):
    contract = (_CONTRACT[dims], ((), ()))
    has_add = add is not None

    def body(*refs):
        a_ref, b_ref = refs[:2]
        o_ref = refs[-1]
        dot = lambda x, y: lax.dot_general(x.astype(BF16), y.astype(BF16), contract, preferred_element_type=F32)
        if n_sum:
            r = dot(a_ref[0], b_ref[0])
            for s in range(1, n_sum):
                r = r + dot(a_ref[s], b_ref[s])
        else:
            r = dot(a_ref[...], b_ref[...])
        if has_add:
            r = r + refs[2][...].astype(F32)
        o_ref[...] = r.astype(o_ref.dtype)

    return pl.pallas_call(
        body, name=name, grid=grid,
        in_specs=[a_spec, b_spec] + ([add_spec] if has_add else []), out_specs=o_spec, out_shape=out_shape,
        compiler_params=_params(("parallel",) * len(grid)),
    )(*([a, b] + ([add] if has_add else [])))


def _tiles(fn, *, name, rows, tm, ncol=1, row_ins=(), col_consts=(), full_consts=(),
           row_outs=(), acc_outs=()):
    nt = rows // tm
    assert rows % tm == 0
    n_full, n_col, n_row = len(full_consts), len(col_consts), len(row_ins)
    n_ro, n_acc = len(row_outs), len(acc_outs)

    def body(*refs):
        ins = refs[:n_full + n_col + n_row]
        outs = refs[n_full + n_col + n_row:]
        i = pl.program_id(1)
        res = fn(pl.program_id(0), *[r[...] for r in ins])
        for r, v in zip(outs[:n_ro], res[:n_ro]):
            r[...] = v.astype(r.dtype)
        if n_acc:
            @pl.when(i == 0)
            def _():
                for r in outs[n_ro:]:
                    r[...] = jnp.zeros_like(r)
            for r, v in zip(outs[n_ro:], res[n_ro:]):
                r[...] += v

    in_specs = [pl.BlockSpec(a.shape, lambda j, i, nd=a.ndim: (0,) * nd) for a in full_consts]
    in_specs += [pl.BlockSpec((nr, w), lambda j, i, o=o: (0, o + j)) for (_, nr, w, o) in col_consts]
    in_specs += [pl.BlockSpec((tm, w), lambda j, i, o=o: (i, o + j)) for (_, w, o) in row_ins]
    out_specs = [pl.BlockSpec((tm, w), lambda j, i: (i, j)) for (w, _) in row_outs]
    out_specs += [pl.BlockSpec((nr, w), lambda j, i: (0, j)) for (nr, w) in acc_outs]
    out_shape = [jax.ShapeDtypeStruct((rows, w * ncol), dt) for (w, dt) in row_outs]
    out_shape += [jax.ShapeDtypeStruct((nr, w * ncol), F32) for (nr, w) in acc_outs]
    args = list(full_consts) + [c[0] for c in col_consts] + [r[0] for r in row_ins]
    out = pl.pallas_call(
        body, name=name, grid=(ncol, nt), in_specs=in_specs, out_specs=out_specs, out_shape=out_shape,
        compiler_params=_params(("parallel", "arbitrary")),
    )(*args)
    return out


def _rms(x, w):
    return x * lax.rsqrt(jnp.mean(x * x, axis=-1, keepdims=True) + EPS) * w


def _lane_lo(shape):
    return lax.broadcasted_iota(jnp.int32, shape, len(shape) - 1) < HEAD_DIM


def _pair_sum(x):
    lo = _lane_lo(x.shape)
    s0 = jnp.sum(jnp.where(lo, x, 0.0), axis=-1, keepdims=True)
    s1 = jnp.sum(jnp.where(lo, 0.0, x), axis=-1, keepdims=True)
    return jnp.where(lo, s0, s1)


def _head_col(x, lo, h):
    keep = lo if h == 0 else jnp.logical_not(lo)
    return jnp.max(jnp.where(keep, x, -jnp.inf), axis=-1, keepdims=True)


def _softplus(x):
    return jnp.maximum(x, 0.0) + jnp.log1p(jnp.exp(-jnp.abs(x)))


def _silu(x):
    return x * jax.nn.sigmoid(x)


def _dot(a, b, contract):
    return lax.dot_general(a.astype(BF16), b.astype(BF16), (contract, ((), ())),
                           preferred_element_type=F32)


def _dot32(a, b, contract):
    return lax.dot_general(a, b, (contract, ((), ())), precision=HIGHEST, preferred_element_type=F32)


def _bd(y):
    yy = jnp.concatenate([y, y], axis=0)
    r = lax.broadcasted_iota(jnp.int32, yy.shape, 0) < HEAD_DIM
    c = lax.broadcasted_iota(jnp.int32, yy.shape, 1) < HEAD_DIM
    return jnp.where(r == c, yy, 0.0)


def _pp(x, y):
    return _dot(x, _bd(y), _CONTRACT["nn"])


def _pp_nt(x, y):
    return _dot(x, _bd(y), _CONTRACT["nt"])


def _pp_tn(x, y):
    full = _dot(x, y, _CONTRACT["tn"])
    return jnp.where(_lane_lo((HEAD_DIM, LANES)), full[:HEAD_DIM], full[HEAD_DIM:])


def _gdn_masks():
    row = lax.broadcasted_iota(jnp.int32, (CHUNK, LANES), 0)
    col = lax.broadcasted_iota(jnp.int32, (CHUNK, LANES), 1) % HEAD_DIM
    return row, col


def _interleave(chains):
    live = list(chains)
    while live:
        for g in list(live):
            try:
                next(g)
            except StopIteration:
                live.remove(g)


def _gdn_forward(qkv, betax, gcx, grow, rows):
    nchunk = rows // CHUNK

    def body(q_ref, k_ref, v_ref, bx_ref, gx_ref, gr_ref, o_ref, ss_ref, ts_ref, state):
        n = pl.program_id(0)

        @pl.when(n == 0)
        def _():
            state[...] = jnp.zeros_like(state)

        row, col = _gdn_masks()
        incl, strict = col <= row, col < row

        def chain(p):
            lanes = pl.ds(p * LANES, LANES)
            q, k, v, bx, gx = q_ref[:, lanes], k_ref[:, lanes], v_ref[:, lanes], bx_ref[:, lanes], gx_ref[:, lanes]
            gr = gr_ref[0, p]
            glast = gx_ref[pl.ds(CHUNK - 1, 1), lanes]
            s = state[p]
            dm = jnp.where(incl, jnp.exp(jnp.minimum(gx - gr, 0.0)), 0.0)
            kb, vb, eg, qs = k * bx, v * bx, jnp.exp(gx), q * SCALE
            yield
            big_g, big_p = _pp_nt(kb, k), _pp_nt(qs, k)
            yield
            x = -jnp.where(strict, big_g * dm, 0.0)
            att = jnp.where(incl, big_p * dm, 0.0)
            tm = jnp.where(row == col, 1.0, 0.0) + x
            x = _pp(x, x)
            yield
            for _ in range(4):
                step, x = _pp(tm, x), _pp(x, x)
                yield
                tm = tm + step
            tm = tm + _pp(tm, x)
            yield
            u, w = _pp(tm, vb), _pp(tm, kb * eg)
            yield
            ws, qgs = _pp(w, s), _pp(qs * eg, s)
            yield
            vn = u - ws
            kd = k * jnp.exp(glast - gx)
            avn, upd = _pp(att, vn), _pp_tn(kd, vn)
            yield
            ss_ref[0, p] = s
            ts_ref[0, p] = tm
            o_ref[:, lanes] = qgs + avn
            state[p] = s * jnp.exp(glast) + upd

        _interleave([chain(p) for p in range(PAIRS)])

    blk = lambda j: pl.BlockSpec((CHUNK, WIDTH), lambda n, j=j: (n, j))
    sv = pl.BlockSpec((1, PAIRS, CHUNK, LANES), lambda n: (n, 0, 0, 0))
    return pl.pallas_call(
        body, name="gdn_fwd", grid=(nchunk,),
        in_specs=[blk(0), blk(1), blk(2), blk(0), blk(0),
                  pl.BlockSpec((1, PAIRS, 1, LANES), lambda n: (n, 0, 0, 0))],
        out_specs=[blk(0), sv, sv],
        out_shape=[jax.ShapeDtypeStruct((rows, WIDTH), F32),
                   jax.ShapeDtypeStruct((nchunk, PAIRS, CHUNK, LANES), F32),
                   jax.ShapeDtypeStruct((nchunk, PAIRS, CHUNK, LANES), F32)],
        scratch_shapes=[pltpu.VMEM((PAIRS, CHUNK, LANES), F32)],
        compiler_params=_params(("arbitrary",)),
    )(qkv, qkv, qkv, betax, gcx, grow)


def _gdn_backward(qkv, betax, gcx, grow, ssave, tsave, do, rows):
    nchunk = rows // CHUNK

    def body(q_ref, k_ref, v_ref, bx_ref, gx_ref, gr_ref, ss_ref, ts_ref, do_ref,
             dq_ref, dk_ref, dv_ref, dbx_ref, dgx_ref, dgr_ref, dstate):
        n = pl.program_id(0)

        @pl.when(n == 0)
        def _():
            dstate[...] = jnp.zeros_like(dstate)

        row, col = _gdn_masks()
        incl, strict = col <= row, col < row

        def chain(p):
            lanes = pl.ds(p * LANES, LANES)
            q, k, v, bx, gx = q_ref[:, lanes], k_ref[:, lanes], v_ref[:, lanes], bx_ref[:, lanes], gx_ref[:, lanes]
            gr = gr_ref[0, p]
            glast = gx_ref[pl.ds(CHUNK - 1, 1), lanes]
            s, tm, d_o = ss_ref[0, p], ts_ref[0, p], do_ref[:, lanes]
            ds_out = dstate[p]
            dm = jnp.where(incl, jnp.exp(jnp.minimum(gx - gr, 0.0)), 0.0)
            kb, vb, eg, qs = k * bx, v * bx, jnp.exp(gx), q * SCALE
            kbg, qg = kb * eg, qs * eg
            ed = jnp.exp(glast - gx)
            kd = k * ed
            eglast = jnp.exp(glast)
            yield
            big_g, big_p = _pp_nt(kb, k), _pp_nt(qs, k)
            u, w = _pp(tm, vb), _pp(tm, kbg)
            dqg, kds = _pp_nt(d_o, s), _pp(kd, ds_out)
            yield
            low = jnp.where(strict, big_g * dm, 0.0)
            att = jnp.where(incl, big_p * dm, 0.0)
            ws, atd = _pp(w, s), _pp_tn(att, d_o)
            yield
            vn = u - ws
            dvn = kds + atd
            dkd, datt_raw = _pp_nt(vn, ds_out), _pp_nt(d_o, vn)
            dw_neg, dvb = _pp_nt(dvn, s), _pp_tn(tm, dvn)
            dtm_a, wdv = _pp_nt(dvn, vb), _pp_tn(w, dvn)
            qgd = _pp_tn(qg, d_o)
            yield
            datt = jnp.where(incl, datt_raw, 0.0)
            dw = -dw_neg
            dtm_b, dkbg = _pp_nt(dw, kbg), _pp_tn(tm, dw)
            dbig_p = datt * dm
            dqs_a, dk_p = _pp(dbig_p, k), _pp_tn(dbig_p, qs)
            yield
            inner = _pp_tn(tm, dtm_a + dtm_b)
            yield
            dlow = jnp.where(strict, -_pp_nt(inner, tm), 0.0)
            yield
            dbig_g = dlow * dm
            dkb_a, dk_g = _pp(dbig_g, k), _pp_tn(dbig_g, kb)
            yield
            dkb = dkb_a + dkbg * eg
            dqs = dqs_a + dqg * eg
            dk = dk_g + dk_p + dkd * ed + dkb * bx
            z = dlow * low + datt * att
            kdterm = dkd * kd
            dglast = (jnp.sum(ds_out * s, axis=0, keepdims=True) * eglast
                      + jnp.sum(kdterm, axis=0, keepdims=True))
            dgx = dqg * qg + dkbg * kbg - kdterm
            dgx = dgx + jnp.where(col == 0, _pair_sum(z), 0.0)
            dgx = dgx + jnp.where(row == CHUNK - 1, dglast, 0.0)
            dq_ref[:, lanes] = dqs * SCALE
            dk_ref[:, lanes] = dk
            dv_ref[:, lanes] = dvb * bx
            dbx_ref[:, lanes] = dkb * k + dvb * v
            dgx_ref[:, lanes] = dgx
            dgr_ref[0, p] = -jnp.sum(z, axis=0, keepdims=True)
            dstate[p] = ds_out * eglast + qgd - wdv

        _interleave([chain(p) for p in range(PAIRS)])

    last = nchunk - 1
    blk = lambda j: pl.BlockSpec((CHUNK, WIDTH), lambda n, j=j: (last - n, j))
    sv = pl.BlockSpec((1, PAIRS, CHUNK, LANES), lambda n: (last - n, 0, 0, 0))
    gr_spec = pl.BlockSpec((1, PAIRS, 1, LANES), lambda n: (last - n, 0, 0, 0))
    wide = jax.ShapeDtypeStruct((rows, WIDTH), F32)
    return pl.pallas_call(
        body, name="gdn_bwd", grid=(nchunk,),
        in_specs=[blk(0), blk(1), blk(2), blk(0), blk(0), gr_spec, sv, sv, blk(0)],
        out_specs=[blk(0)] * 5 + [gr_spec],
        out_shape=[wide] * 5 + [jax.ShapeDtypeStruct((nchunk, PAIRS, 1, LANES), F32)],
        scratch_shapes=[pltpu.VMEM((PAIRS, CHUNK, LANES), F32)],
        compiler_params=_params(("arbitrary",)),
    )(qkv, qkv, qkv, betax, gcx, grow, ssave, tsave, do)


ATT_TQ = 256


def _att_scores(qh, kt, fk, diag):
    s = _dot(qh, kt, _CONTRACT["nt"]) - fk
    if diag:
        r = lax.broadcasted_iota(jnp.int32, s.shape, 0)
        c = lax.broadcasted_iota(jnp.int32, s.shape, 1)
        s = jnp.where(r >= c, s, -jnp.inf)
    return s


def _head_masks(n):
    lo = _lane_lo((n, LANES))
    return [lo, jnp.logical_not(lo)]


def _attention_forward(fqk, proj, frow, rows):
    tq = tk = min(ATT_TQ, rows)
    nq = rows // tq
    v_off = 3072 // LANES

    def body(q_ref, k_ref, v_ref, fr_ref, o_ref, lse_ref):
        qi = pl.program_id(1)
        q = q_ref[...] * SCALE
        keep_q, keep_k = _head_masks(tq), _head_masks(tk)
        qh = [jnp.where(keep_q[h], q, 0.0).astype(BF16) for h in range(2)]

        def tile(ki, carry, diag):
            k0 = pl.multiple_of(ki * tk, tk)
            kt = k_ref[pl.ds(k0, tk), :].astype(BF16)
            v_t = v_ref[pl.ds(k0, tk), :]
            out = [None, None]

            def chain(h):
                m, l, acc = carry[h]
                vt = jnp.where(keep_k[h], v_t, 0.0).astype(BF16)
                yield
                s = _att_scores(qh[h], kt, fr_ref[0, pl.ds(h, 1), pl.ds(k0, tk)], diag)
                yield
                m_new = jnp.maximum(m, jnp.max(s, axis=-1, keepdims=True))
                p = jnp.exp(s - m_new)
                alpha = jnp.exp(m - m_new)
                l = alpha * l + jnp.sum(p, axis=-1, keepdims=True)
                p_hi = p.astype(BF16)
                p_lo = p - p_hi.astype(F32)
                yield
                out[h] = (m_new, l, alpha * acc + _dot(p_hi, vt, _CONTRACT["nn"]) + _dot(p_lo, vt, _CONTRACT["nn"]))

            _interleave([chain(0), chain(1)])
            return tuple(out)

        one = (jnp.full((tq, 1), -jnp.inf, F32), jnp.zeros((tq, 1), F32), jnp.zeros((tq, LANES), F32))
        carry = lax.fori_loop(0, qi, lambda ki, c: tile(ki, c, False), (one, one))
        (m0, l0, acc0), (m1, l1, acc1) = tile(qi, carry, True)
        o_ref[...] = acc0 / l0 + acc1 / l1
        lse_ref[...] = jnp.where(keep_q[0], m0 + jnp.log(l0), m1 + jnp.log(l1))

    whole = lambda off: pl.BlockSpec((rows, LANES), lambda p, i, off=off: (0, off + p))
    qblk = lambda off: pl.BlockSpec((tq, LANES), lambda p, i, off=off: (i, off + p))
    wide = jax.ShapeDtypeStruct((rows, WIDTH), F32)
    return pl.pallas_call(
        body, name="fox_fwd", grid=(PAIRS, nq),
        in_specs=[qblk(0), whole(PAIRS), whole(v_off), pl.BlockSpec((1, 2, rows), lambda p, i: (p, 0, 0))],
        out_specs=[qblk(0), qblk(0)], out_shape=[wide, wide],
        compiler_params=_params(("parallel", "arbitrary")),
    )(fqk, fqk, proj, frow)


def _attention_delta(fqk, proj, frow, lse, dao, rows):
    tq = tk = min(ATT_TQ, rows)
    nq = rows // tq
    v_off = 3072 // LANES

    def body(q_ref, k_ref, v_ref, fr_ref, lse_ref, do_ref, delta_ref):
        qi = pl.program_id(1)
        q, d_o, lse_t = q_ref[...] * SCALE, do_ref[...], lse_ref[...]
        keep_q = _head_masks(tq)
        qh = [jnp.where(keep_q[h], q, 0.0).astype(BF16) for h in range(2)]
        doh = [jnp.where(keep_q[h], d_o, 0.0).astype(BF16) for h in range(2)]
        lse_h = [_head_col(lse_t, keep_q[0], h) for h in range(2)]

        def tile(ki, carry, diag):
            k0 = pl.multiple_of(ki * tk, tk)
            kt = k_ref[pl.ds(k0, tk), :].astype(BF16)
            vt = v_ref[pl.ds(k0, tk), :].astype(BF16)
            out = [None, None]

            def chain(h):
                s = _att_scores(qh[h], kt, fr_ref[0, pl.ds(h, 1), pl.ds(k0, tk)], diag)
                dp = _dot(doh[h], vt, _CONTRACT["nt"])
                yield
                out[h] = carry[h] + jnp.sum(jnp.exp(s - lse_h[h]) * dp, axis=-1, keepdims=True)

            _interleave([chain(0), chain(1)])
            return tuple(out)

        zero = jnp.zeros((tq, 1), F32)
        carry = lax.fori_loop(0, qi, lambda ki, c: tile(ki, c, False), (zero, zero))
        d0, d1 = tile(qi, carry, True)
        delta_ref[...] = jnp.where(keep_q[0], d0, d1)

    whole = lambda off: pl.BlockSpec((rows, LANES), lambda p, i, off=off: (0, off + p))
    qblk = lambda off: pl.BlockSpec((tq, LANES), lambda p, i, off=off: (i, off + p))
    return pl.pallas_call(
        body, name="fox_delta", grid=(PAIRS, nq),
        in_specs=[qblk(0), whole(PAIRS), whole(v_off),
                  pl.BlockSpec((1, 2, rows), lambda p, i: (p, 0, 0)), qblk(0), qblk(0)],
        out_specs=qblk(0), out_shape=jax.ShapeDtypeStruct((rows, WIDTH), F32),
        compiler_params=_params(("parallel", "arbitrary")),
    )(fqk, fqk, proj, frow, lse, dao)


def _attention_backward(fqk, proj, frow, ao, lse, dao, rows):
    tq = tk = min(ATT_TQ, rows)
    nq = rows // tq
    v_off = 3072 // LANES

    def body(q_ref, k_ref, v_ref, fr_ref, o_ref, lse_ref, do_ref, dq_ref, dk_ref, dv_ref, dfr_ref):
        ki = pl.program_id(1)

        @pl.when(ki == 0)
        def _():
            dq_ref[...] = jnp.zeros_like(dq_ref)

        keep_q, keep_k = _head_masks(tq), _head_masks(tk)
        k_t = k_ref[...]
        kt = k_t.astype(BF16)
        vt = v_ref[...].astype(BF16)
        kh = [jnp.where(keep_k[h], k_t, 0.0).astype(BF16) for h in range(2)]
        fk = [fr_ref[0, pl.ds(h, 1), :] for h in range(2)]

        def tile(qi, carry, diag):
            dk, dv, df0, df1 = carry
            rows_q = pl.ds(pl.multiple_of(qi * tq, tq), tq)
            q, d_o, lse_t = q_ref[rows_q, :] * SCALE, do_ref[rows_q, :], lse_ref[rows_q, :]
            delta_x = _pair_sum(d_o.astype(BF16).astype(F32) * o_ref[rows_q, :])
            res = [None, None]

            def chain(h):
                qh = jnp.where(keep_q[h], q, 0.0).astype(BF16)
                doh = jnp.where(keep_q[h], d_o, 0.0).astype(BF16)
                lse_h, delta_h = _head_col(lse_t, keep_q[0], h), _head_col(delta_x, keep_q[0], h)
                yield
                s, dp = _att_scores(qh, kt, fk[h], diag), _dot(doh, vt, _CONTRACT["nt"])
                yield
                p = jnp.exp(s - lse_h)
                ds = p * (dp - delta_h)
                yield
                res[h] = (_dot(p, doh, _CONTRACT["tn"]), _dot(ds, qh, _CONTRACT["tn"]),
                          _dot(ds, kh[h], _CONTRACT["nn"]), jnp.sum(ds, axis=0, keepdims=True))

            _interleave([chain(0), chain(1)])
            (dv0, dk0, dq0, s0), (dv1, dk1, dq1, s1) = res
            dq_ref[rows_q, :] += (dq0 + dq1) * SCALE
            return dk + dk0 + dk1, dv + dv0 + dv1, df0 - s0, df1 - s1

        zero_kv = jnp.zeros((tk, LANES), F32)
        zero_f = jnp.zeros((1, tk), F32)
        carry = tile(ki, (zero_kv, zero_kv, zero_f, zero_f), True)
        dk, dv, df0, df1 = lax.fori_loop(ki + 1, nq, lambda qi, c: tile(qi, c, False), carry)
        dk_ref[...] = dk
        dv_ref[...] = dv.astype(dv_ref.dtype)
        dfr_ref[0, pl.ds(0, 1), :] = df0
        dfr_ref[0, pl.ds(1, 1), :] = df1

    whole = lambda off: pl.BlockSpec((rows, LANES), lambda p, i, off=off: (0, off + p))
    kblk = lambda off: pl.BlockSpec((tk, LANES), lambda p, i, off=off: (i, off + p))
    fr_spec = pl.BlockSpec((1, 2, tk), lambda p, i: (p, 0, i))
    wide = jax.ShapeDtypeStruct((rows, WIDTH), F32)
    return pl.pallas_call(
        body, name="fox_bwd", grid=(PAIRS, nq),
        in_specs=[whole(0), kblk(PAIRS), kblk(v_off), fr_spec, whole(0), whole(0), whole(0)],
        out_specs=[whole(0), kblk(0), kblk(0), fr_spec],
        out_shape=[wide, wide, jax.ShapeDtypeStruct((rows, WIDTH), BF16),
                   jax.ShapeDtypeStruct((PAIRS, 2, rows), F32)],
        compiler_params=_params(("parallel", "arbitrary")),
    )(fqk, fqk, proj, frow, ao, lse, dao)


def _lane_ids(shape):
    return lax.broadcasted_iota(jnp.int32, shape, len(shape) - 1)


def _gates_elem(a_log, dt_bias, f_bias, pre):
    lane = _lane_ids(pre.shape)
    beta = jax.nn.sigmoid(pre)
    g = -jnp.exp(a_log) * _softplus(pre + dt_bias)
    lf = -_softplus(-(pre + f_bias))
    return jnp.where(lane < 8, beta, jnp.where(lane < 16, g, jnp.where(lane < 24, lf, 0.0)))


def _tri_consts():
    r = np.arange(LANES)[:, None]
    c = np.arange(LANES)[None, :]
    full = (c <= r).astype(np.float32)
    chunked = full * ((r // CHUNK) == (c // CHUNK))
    return jnp.asarray(chunked), jnp.asarray(full)


def _cums_fwd(lc, lf, gates):
    rows = gates.shape[0]
    lane = _lane_ids((LANES, LANES))
    carry = jnp.zeros((1, LANES), F32)
    out = []
    for r in range(rows // LANES):
        blk = gates[r * LANES:(r + 1) * LANES]
        gc = _dot32(lc, blk, _CONTRACT["nn"])
        f = _dot32(lf, blk, _CONTRACT["nn"]) + carry
        carry = carry + jnp.sum(blk, axis=0, keepdims=True)
        out.append(jnp.where((lane >= 8) & (lane < 16), gc, jnp.where((lane >= 16) & (lane < 24), f, 0.0)))
    return jnp.concatenate(out, axis=0)


def _cums_bwd(lc, lf, dcums):
    rows = dcums.shape[0]
    lane = _lane_ids((LANES, LANES))
    is_g = (lane >= 8) & (lane < 16)
    is_f = (lane >= 16) & (lane < 24)
    carry = jnp.zeros((1, LANES), F32)
    out = [None] * (rows // LANES)
    for r in reversed(range(rows // LANES)):
        blk = dcums[r * LANES:(r + 1) * LANES]
        dg = jnp.where(is_g, blk, 0.0)
        df = jnp.where(is_f, blk, 0.0)
        out[r] = _dot32(lc, dg, _CONTRACT["tn"]) + _dot32(lf, df, _CONTRACT["tn"]) + carry
        carry = carry + jnp.sum(df, axis=0, keepdims=True)
    return jnp.concatenate(out, axis=0)


def _expand_consts():
    xb = np.zeros((LANES, WIDTH), np.float32)
    xg = np.zeros((LANES, WIDTH), np.float32)
    for h in range(HEADS):
        xb[h, h * HEAD_DIM:(h + 1) * HEAD_DIM] = 1.0
        xg[8 + h, h * HEAD_DIM:(h + 1) * HEAD_DIM] = 1.0
    return jnp.asarray(xb), jnp.asarray(xg)


def _shift_down(x, s):
    if s == 0:
        return x
    row = lax.broadcasted_iota(jnp.int32, x.shape, 0)
    return jnp.where(row >= s, pltpu.roll(x, s, 0), 0.0)


def _shift_up(x, s):
    if s == 0:
        return x
    n = x.shape[0]
    row = lax.broadcasted_iota(jnp.int32, x.shape, 0)
    return jnp.where(row < n - s, pltpu.roll(x, n - s, 0), 0.0)


def _row_of(cw, i):
    row = lax.broadcasted_iota(jnp.int32, cw.shape, 0)
    return jnp.sum(jnp.where(row == i, cw, 0.0), axis=0, keepdims=True)


def _conv(cw, x):
    c = jnp.zeros_like(x)
    for i in range(CONV_K):
        c = c + _row_of(cw, i) * _shift_down(x, CONV_K - 1 - i)
    return c


def _post_conv(is_qk, c):
    s = _silu(c)
    n = s * lax.rsqrt(_pair_sum(s * s) + EPS)
    return jnp.where(is_qk, n, s)


def _gdn_prep_fwd(col, cw, x):
    return (_post_conv(col < 2 * PAIRS, _conv(cw, x)),)


def _gdn_prep_bwd(is_qk, cw, x, dy):
    c = _conv(cw, x)
    _, vjp = jax.vjp(lambda cc: _post_conv(is_qk, cc), c)
    (dc,) = vjp(dy)
    dx = jnp.zeros_like(x)
    row = lax.broadcasted_iota(jnp.int32, cw.shape, 0)
    dcw = jnp.zeros(cw.shape, F32)
    for i in range(CONV_K):
        s = CONV_K - 1 - i
        dx = dx + _row_of(cw, i) * _shift_up(dc, s)
        dcw = dcw + jnp.where(row == i, jnp.sum(dc * _shift_down(x, s), axis=0, keepdims=True), 0.0)
    return dx, dcw


def _head_rms(w, x):
    return x * lax.rsqrt(_pair_sum(x * x) / HEAD_DIM + EPS) * w


def _cat_weights(w_in_t):
    tail = jnp.pad(w_in_t[4112:4120], ((0, D_CAT - D_IN), (0, 0)))
    return jnp.concatenate([w_in_t[:2048], w_in_t[2064:4112], w_in_t[2048:2064], tail], axis=0)


def _uncat_grad(g):
    return jnp.concatenate([g[:2048], g[4096:4112], g[2048:4096], g[4112:4120]], axis=0)


def _lanes_to_rowform(v8, rows):
    return v8.reshape(rows // CHUNK, CHUNK, HEADS).transpose(0, 2, 1).reshape(rows // CHUNK, PAIRS, 1, LANES)


def _rowform_to_lanes(v, rows):
    return v.reshape(rows // CHUNK, HEADS, CHUNK).transpose(0, 2, 1).reshape(rows, HEADS)


def _local_step(x, target, norm1_w, w_cat, conv_w, a_log, dt_bias, out_norm_w, f_bias, q_norm_w, k_norm_w,
                norm2_w, final_w, late_weights, early_grads_ready):
    rows = x.shape[0]
    tm = min(256, rows)
    lc, lf = _tri_consts()
    xb, xg = _expand_consts()

    (h1,) = _tiles(lambda col, w, xx: (_rms(xx, w),), name="norm1", rows=rows, tm=tm,
                   full_consts=[norm1_w], row_ins=[(x, D_MODEL, 0)], row_outs=[(D_MODEL, BF16)])
    proj = _mm(h1, w_cat, dims="nt", name="in_proj", tn=384, tk=1024)

    lane_pad = lambda v, off: jnp.pad(v.reshape(1, -1), ((0, 0), (off, LANES - off - v.size)))
    p_a, p_dt, p_fb = lane_pad(a_log, 8), lane_pad(dt_bias, 8), lane_pad(f_bias, 16)

    def gates_fwd(col, lcv, lfv, a, dt, fb, pre):
        gates = _gates_elem(a, dt, fb, pre)
        return gates, _cums_fwd(lcv, lfv, gates)

    gates, cums = _tiles(gates_fwd, name="gates", rows=rows, tm=rows,
                         full_consts=[lc, lf, p_a, p_dt, p_fb], row_ins=[(proj, LANES, COL_SMALL)],
                         row_outs=[(LANES, F32), (LANES, F32)])

    def expand_fwd(col, b, g, gt, cm):
        return (_dot32(gt, b, _CONTRACT["nn"]), _dot32(cm, g, _CONTRACT["nn"]))

    betax, gcx = _tiles(expand_fwd, name="expand", rows=rows, tm=tm, full_consts=[xb, xg],
                        row_ins=[(gates, LANES, 0), (cums, LANES, 0)],
                        row_outs=[(WIDTH, F32)] * 2)
    grow = _lanes_to_rowform(cums[:, 8:16], rows)
    frow = cums[:, 16:24].T.reshape(PAIRS, 2, rows)

    (qkv,) = _tiles(_gdn_prep_fwd, name="gdn_prep", rows=rows, tm=rows, ncol=3 * PAIRS,
                    col_consts=[(conv_w, CONV_K, LANES, 0)], row_ins=[(proj, LANES, 0)],
                    row_outs=[(LANES, F32)])
    o_gdn, ssave, tsave = _gdn_forward(qkv, betax, gcx, grow, rows)

    w_qk = jnp.concatenate([jnp.tile(q_norm_w.reshape(1, -1), (1, HEADS)),
                            jnp.tile(k_norm_w.reshape(1, -1), (1, HEADS))], axis=1)
    fox_off = 2048 // LANES
    (fqk,) = _tiles(lambda col, w, xx: (_head_rms(w, xx),), name="fox_prep", rows=rows, tm=rows, ncol=2 * PAIRS,
                    col_consts=[(w_qk, 1, LANES, 0)], row_ins=[(proj, LANES, fox_off)],
                    row_outs=[(LANES, F32)])
    ao, lse = _attention_forward(fqk, proj, frow, rows)

    w_on = jnp.tile(out_norm_w.reshape(1, -1), (1, 2))
    z_off, fg_off = 1536 // LANES, 3584 // LANES
    mix_g_fn = lambda w, o, z: _head_rms(w, o) * _silu(z)
    mix_f_fn = lambda a, g: a * jax.nn.sigmoid(g)
    (mix_g,) = _tiles(lambda col, w, o, z: (mix_g_fn(w, o, z),), name="mix_gdn", rows=rows, tm=rows, ncol=PAIRS,
                      full_consts=[w_on], row_ins=[(o_gdn, LANES, 0), (proj, LANES, z_off)],
                      row_outs=[(LANES, BF16)])
    (mix_f,) = _tiles(lambda col, a, g: (mix_f_fn(a, g),), name="mix_fox", rows=rows, tm=rows, ncol=PAIRS,
                      row_ins=[(ao, LANES, 0), (proj, LANES, fg_off)], row_outs=[(LANES, BF16)])
    mix = jnp.concatenate([mix_g, mix_f], axis=1)
    w_out, w_gate, w_up, w_down = late_weights(mix)
    x1 = _mm(mix, w_out, dims="nn", name="out_proj", add=x, tk=1024)

    (h2,) = _tiles(lambda col, w, xx: (_rms(xx, w),), name="norm2", rows=rows, tm=tm,
                   full_consts=[norm2_w], row_ins=[(x1, D_MODEL, 0)], row_outs=[(D_MODEL, BF16)])
    t_rows, t_cols, t_act = min(1024, rows), 512, min(512, rows)
    n_rt = rows // t_rows
    st_act = jax.ShapeDtypeStruct((N_CHIPS, rows, FF_SHARD), F32)
    st_rows = pl.BlockSpec((None, t_rows, FF_SHARD), lambda i, j: (j, i, 0))
    out_rows = pl.BlockSpec((t_rows, t_cols), lambda i, n: (i, n))
    flat = lambda t: t.reshape(N_CHIPS * rows, FF_SHARD)

    def ffn_in(w_st, name):
        return _mm_blocks(h2, w_st, name=name, grid=(n_rt, N_CHIPS), dims="nt",
                          a_spec=pl.BlockSpec((t_rows, D_MODEL), lambda i, j: (i, 0)),
                          b_spec=pl.BlockSpec((None, FF_SHARD, D_MODEL), lambda i, j: (j, 0, 0)),
                          o_spec=st_rows, out_shape=st_act)

    gate, up = ffn_in(w_gate, "ffn_gate"), ffn_in(w_up, "ffn_up")
    act_fn = lambda g, u: _silu(g) * u
    (act,) = _tiles(lambda col, g, u: (act_fn(g, u),), name="ffn_act", rows=N_CHIPS * rows, tm=t_act,
                    row_ins=[(flat(gate), FF_SHARD, 0), (flat(up), FF_SHARD, 0)], row_outs=[(FF_SHARD, BF16)])
    act = act.reshape(st_act.shape)
    x2 = _mm_blocks(act, w_down, name="ffn_down", grid=(n_rt, D_MODEL // t_cols), dims="nn", n_sum=N_CHIPS,
                    a_spec=pl.BlockSpec((N_CHIPS, t_rows, FF_SHARD), lambda i, n: (0, i, 0)),
                    b_spec=pl.BlockSpec((N_CHIPS, FF_SHARD, t_cols), lambda i, n: (0, 0, n)),
                    o_spec=out_rows, out_shape=jax.ShapeDtypeStruct((rows, D_MODEL), F32),
                    add=x1, add_spec=out_rows)

    def final_fn(col, w, xx, tgt):
        y, vjp = jax.vjp(_rms, xx, w)
        err = y - tgt
        loss = 0.5 * jnp.sum(err * err) / D_MODEL
        dx, dw = vjp(err / D_MODEL)
        return dx, dx, jnp.full((1, LANES), loss, F32), dw

    dx2, dx2_b, loss, d_final_w = _tiles(final_fn, name="final_loss", rows=rows, tm=tm, full_consts=[final_w],
                                         row_ins=[(x2, D_MODEL, 0), (target, D_MODEL, 0)],
                                         row_outs=[(D_MODEL, F32), (D_MODEL, BF16)],
                                         acc_outs=[(1, LANES), (1, D_MODEL)])

    dact = _mm_blocks(dx2_b, w_down, name="d_act", grid=(n_rt, N_CHIPS), dims="nt",
                      a_spec=pl.BlockSpec((t_rows, D_MODEL), lambda i, j: (i, 0)),
                      b_spec=pl.BlockSpec((None, FF_SHARD, D_MODEL), lambda i, j: (j, 0, 0)),
                      o_spec=st_rows, out_shape=st_act)
    def g_ffn(d_st, other, name):
        return _mm_blocks(d_st, other, name=name, grid=(N_CHIPS, D_MODEL // t_cols), dims="tn",
                          a_spec=pl.BlockSpec((None, rows, FF_SHARD), lambda j, n: (j, 0, 0)),
                          b_spec=pl.BlockSpec((rows, t_cols), lambda j, n: (0, n)),
                          o_spec=pl.BlockSpec((None, FF_SHARD, t_cols), lambda j, n: (j, 0, n)),
                          out_shape=jax.ShapeDtypeStruct((N_CHIPS, FF_SHARD, D_MODEL), F32))

    g_down = g_ffn(act, dx2_b, "g_down")

    def act_bwd(col, g, u, d):
        _, vjp = jax.vjp(act_fn, g, u)
        return vjp(d)

    dgate, dup = _tiles(act_bwd, name="ffn_act_bwd", rows=N_CHIPS * rows, tm=t_act,
                        row_ins=[(flat(gate), FF_SHARD, 0), (flat(up), FF_SHARD, 0), (flat(dact), FF_SHARD, 0)],
                        row_outs=[(FF_SHARD, BF16), (FF_SHARD, BF16)])
    dgate, dup = dgate.reshape(st_act.shape), dup.reshape(st_act.shape)

    def d_h2(d_st, w_st, name, add):
        return _mm_blocks(d_st, w_st, name=name, grid=(n_rt, D_MODEL // t_cols), dims="nn", n_sum=N_CHIPS,
                          a_spec=pl.BlockSpec((N_CHIPS, t_rows, FF_SHARD), lambda i, n: (0, i, 0)),
                          b_spec=pl.BlockSpec((N_CHIPS, FF_SHARD, t_cols), lambda i, n: (0, 0, n)),
                          o_spec=out_rows, out_shape=jax.ShapeDtypeStruct((rows, D_MODEL), F32),
                          add=add, add_spec=out_rows)

    dh2 = d_h2(dup, w_up, "d_h2_up", d_h2(dgate, w_gate, "d_h2_gate", None))
    g_gate, g_up = g_ffn(dgate, h2, "g_gate"), g_ffn(dup, h2, "g_up")

    def norm_bwd(col, w, xx, dh, dres):
        _, vjp = jax.vjp(_rms, xx, w)
        dx, dw = vjp(dh)
        return dx + dres, dx + dres, dw

    dx1, dx1_b, d_norm2_w = _tiles(norm_bwd, name="norm2_bwd", rows=rows, tm=tm, full_consts=[norm2_w],
                                   row_ins=[(x1, D_MODEL, 0), (dh2, D_MODEL, 0), (dx2, D_MODEL, 0)],
                                   row_outs=[(D_MODEL, F32), (D_MODEL, BF16)], acc_outs=[(1, D_MODEL)])
    dmix = _mm(dx1_b, w_out, dims="nt", name="d_mix", tk=1024)
    g_out = _mm(mix, dx1_b, dims="tn", name="g_out", tk=rows)
    w_on = w_on + early_grads_ready(g_out, g_gate, g_up, g_down)

    def mix_g_bwd(col, w, o, z, d):
        _, vjp = jax.vjp(mix_g_fn, w, o, z)
        dw, do_, dz = vjp(d)
        return do_, dz, dw

    do_gdn, dz, d_on = _tiles(mix_g_bwd, name="mix_gdn_bwd", rows=rows, tm=rows, ncol=PAIRS, full_consts=[w_on],
                              row_ins=[(o_gdn, LANES, 0), (proj, LANES, z_off), (dmix, LANES, 0)],
                              row_outs=[(LANES, F32), (LANES, BF16)], acc_outs=[(1, LANES)])

    def mix_f_bwd(col, a, g, d):
        _, vjp = jax.vjp(mix_f_fn, a, g)
        return vjp(d)

    dao, dfgate = _tiles(mix_f_bwd, name="mix_fox_bwd", rows=rows, tm=rows, ncol=PAIRS,
                         row_ins=[(ao, LANES, 0), (proj, LANES, fg_off), (dmix, LANES, PAIRS)],
                         row_outs=[(LANES, F32), (LANES, BF16)])

    dfq, dfk, dfv, dfrow = _attention_backward(fqk, proj, frow, ao, lse, dao, rows)

    def fox_prep_bwd(col, w, xx, d):
        _, vjp = jax.vjp(_head_rms, w, xx)
        dw, dx = vjp(d)
        return dx, dw

    dfqk, d_wqk = [], []
    for part, d_n in enumerate((dfq, dfk)):
        dx_p, dw_p = _tiles(fox_prep_bwd, name="fox_prep_bwd_" + "qk"[part], rows=rows, tm=rows, ncol=PAIRS,
                            col_consts=[(w_qk, 1, LANES, part * PAIRS)],
                            row_ins=[(proj, LANES, fox_off + part * PAIRS), (d_n, LANES, 0)],
                            row_outs=[(LANES, BF16)], acc_outs=[(1, LANES)])
        dfqk.append(dx_p)
        d_wqk.append(dw_p)

    dq, dk, dv, dbetax, dgcx, dgrow = _gdn_backward(qkv, betax, gcx, grow, ssave, tsave, do_gdn, rows)
    dqkv, d_conv = [], []
    for part, d_n in enumerate((dq, dk, dv)):
        prep_bwd = lambda col, cw, xx, dy, is_qk=(part < 2): _gdn_prep_bwd(is_qk, cw, xx, dy)
        dx_p, dw_p = _tiles(prep_bwd, name="gdn_prep_bwd_" + "qkv"[part], rows=rows, tm=rows, ncol=PAIRS,
                            col_consts=[(conv_w, CONV_K, LANES, part * PAIRS)],
                            row_ins=[(proj, LANES, part * PAIRS), (d_n, LANES, 0)],
                            row_outs=[(LANES, BF16)], acc_outs=[(CONV_K, LANES)])
        dqkv.append(dx_p)
        d_conv.append(dw_p)
    d_conv = jnp.concatenate(d_conv, axis=1)

    def expand_bwd(col, b, g, db, dg):
        return (_dot32(db, b, _CONTRACT["nt"]), _dot32(dg, g, _CONTRACT["nt"]))

    dgates_b, dcums_g = _tiles(expand_bwd, name="expand_bwd", rows=rows, tm=tm, full_consts=[xb, xg],
                               row_ins=[(dbetax, WIDTH, 0), (dgcx, WIDTH, 0)],
                               row_outs=[(LANES, F32), (LANES, F32)])
    dcums_row = jnp.concatenate([jnp.zeros((rows, 8), F32), _rowform_to_lanes(dgrow, rows),
                                 dfrow.reshape(HEADS, rows).T, jnp.zeros((rows, LANES - 24), F32)], axis=1)

    def gates_bwd(col, lcv, lfv, a, dt, fb, pre, dgb, dcg, dcr):
        lane = _lane_ids(pre.shape)
        dgates = jnp.where(lane < 8, dgb, _cums_bwd(lcv, lfv, dcg + dcr))
        _, vjp = jax.vjp(_gates_elem, a, dt, fb, pre)
        da, ddt, dfb, dpre = vjp(dgates)
        return dpre, da, ddt, dfb

    dpre, d_a, d_dt, d_fb = _tiles(gates_bwd, name="gates_bwd", rows=rows, tm=rows,
                                   full_consts=[lc, lf, p_a, p_dt, p_fb],
                                   row_ins=[(proj, LANES, COL_SMALL), (dgates_b, LANES, 0), (dcums_g, LANES, 0),
                                            (dcums_row, LANES, 0)],
                                   row_outs=[(LANES, BF16)], acc_outs=[(1, LANES)] * 3)

    dproj = jnp.concatenate(dqkv + [dz] + dfqk + [dfv, dfgate, dpre], axis=1)
    dh1 = _mm(dproj, w_cat, dims="nn", name="d_h1", tk=D_CAT)
    g_cat = _mm(dproj, h1, dims="tn", name="g_in", tm=384, tn=D_MODEL, tk=rows)

    def norm1_bwd(col, w, xx, dh, dres):
        _, vjp = jax.vjp(_rms, xx, w)
        dx, dw = vjp(dh)
        return dx + dres, dw

    grad_x, d_norm1_w = _tiles(norm1_bwd, name="norm1_bwd", rows=rows, tm=tm, full_consts=[norm1_w],
                               row_ins=[(x, D_MODEL, 0), (dh1, D_MODEL, 0), (dx1, D_MODEL, 0)],
                               row_outs=[(D_MODEL, F32)], acc_outs=[(1, D_MODEL)])

    fold = lambda v: v.reshape(-1, HEAD_DIM).sum(axis=0)
    small = dict(
        loss=loss[0, 0],
        norm1_w=d_norm1_w, conv_w=d_conv, a_log=d_a[0, 8:16], dt_bias=d_dt[0, 8:16],
        out_norm_w=fold(d_on), f_bias=d_fb[0, 16:24], q_norm_w=fold(d_wqk[0]),
        k_norm_w=fold(d_wqk[1]), norm2_w=d_norm2_w, final_w=d_final_w)
    return grad_x, g_cat, g_out, g_gate, g_up, g_down, small


HBM_SPEC = pl.BlockSpec(memory_space=pltpu.HBM)


def _place():
    x, y, c = lax.axis_index("x"), lax.axis_index("y"), lax.axis_index("c")
    chips = [(1 - x, y), (x, 1 - y), (1 - x, 1 - y)]
    return x, y, c, 2 * x + y, (x, y, 1 - c), chips, [2 * cx + cy for cx, cy in chips]


def _remote(src, dst, send_sem, recv_sem, to):
    return pltpu.make_async_remote_copy(src_ref=src, dst_ref=dst, send_sem=send_sem, recv_sem=recv_sem,
                                        device_id=to, device_id_type=MESH)


def _allgather_weights(shards, conv):
    n = len(shards)
    halves = [s.shape[1] // 2 for s in shards]
    per = 6
    own_base = n * per + 3

    def body(*refs):
        ins, conv_in = refs[:n], refs[n]
        outs, conv_out = refs[n + 1:2 * n + 1], refs[2 * n + 1]
        send_sems, recv_sems = refs[2 * n + 2:]
        x, y, c, own, sib, chips, chip_idx = _place()

        def half(i, ref, hc):
            return ref.at[:, pl.ds(pl.multiple_of(hc * halves[i], LANES), halves[i])]

        sent = []
        for i, (src, dst) in enumerate(zip(list(ins) + [conv_in], list(outs) + [conv_out])):
            k = own_base + i
            sent.append(_remote(src, dst.at[own], send_sems.at[k], recv_sems.at[k], sib))
        for i in range(n):
            for j, chip in enumerate(chips):
                k = i * per + j
                sent.append(_remote(half(i, ins[i], c), half(i, outs[i].at[own], c),
                                    send_sems.at[k], recv_sems.at[k], (*chip, c)))
        for j, chip in enumerate(chips):
            k = n * per + j
            sent.append(_remote(conv_in, conv_out.at[own], send_sems.at[k], recv_sems.at[k], (*chip, c)))
        for cp in sent:
            cp.start()
        for i in range(n):
            for j in range(len(chips)):
                k = i * per + j
                landed = half(i, outs[i].at[chip_idx[j]], c)
                _remote(landed, landed, send_sems.at[k], recv_sems.at[k], sib).wait_recv()
                fwd = _remote(landed, landed, send_sems.at[k + 3], recv_sems.at[k + 3], sib)
                fwd.start()
                sent.append(fwd)
        for i in range(n):
            for j in range(len(chips)):
                k = i * per + 3 + j
                landed = half(i, outs[i].at[chip_idx[j]], 1 - c)
                _remote(landed, landed, send_sems.at[k], recv_sems.at[k], sib).wait_recv()
        for j in range(len(chips)):
            k = n * per + j
            landed = conv_out.at[chip_idx[j]]
            _remote(landed, landed, send_sems.at[k], recv_sems.at[k], sib).wait_recv()
        for i, dst in enumerate(list(outs) + [conv_out]):
            k = own_base + i
            landed = dst.at[own]
            _remote(landed, landed, send_sems.at[k], recv_sems.at[k], sib).wait_recv()
        for cp in sent:
            cp.wait_send()

    n_sem = own_base + n + 1
    out_shape = [jax.ShapeDtypeStruct((N_CHIPS,) + s.shape, s.dtype) for s in shards]
    out_shape.append(jax.ShapeDtypeStruct((N_CHIPS,) + conv.shape, conv.dtype))
    res = pl.pallas_call(
        body, name="allgather_weights", out_shape=out_shape,
        in_specs=[HBM_SPEC] * (n + 1), out_specs=[HBM_SPEC] * (n + 1),
        scratch_shapes=[pltpu.SemaphoreType.DMA((n_sem,)), pltpu.SemaphoreType.DMA((n_sem,))],
    )(*shards, conv)
    return res[:n], res[n]


SEM_SPEC = pl.BlockSpec(memory_space=pltpu.SEMAPHORE)
ANY_SPEC = pl.BlockSpec(memory_space=pl.ANY)
DATAFLOW = pltpu.SideEffectType.DATAFLOW_SIDE_EFFECTING


def _gather_plan(srcs, lands):
    x, y, c, own, sib, chips, chip_idx = _place()
    plan = []
    for src, land in zip(srcs, lands):
        for j, chip in enumerate(chips):
            plan.append((src, land.at[own], (*chip, c), land.at[chip_idx[j]]))
        plan.append((src, land.at[own], sib, land.at[own]))
    return plan


def _exchange_plan(srcs, lands):
    x, y, c, own, sib, chips, chip_idx = _place()
    plan = []
    for src, land in zip(srcs, lands):
        for j, chip in enumerate(chips):
            plan.append((src.at[chip_idx[j]], land.at[j], (*chip, c), land.at[j]))
    return plan


def _split_start(name, plan_fn, srcs, land_shapes, n_copies, after):
    n = len(srcs)

    def body(*refs):
        src_refs, land_refs = refs[:n], refs[n:2 * n]
        send_sems, recv_sems = refs[2 * n + 1], refs[2 * n + 2]
        token = refs[-1]
        for k, (src, dst, to, _) in enumerate(plan_fn(src_refs, land_refs)):
            _remote(src, dst, send_sems.at[k], recv_sems.at[k], to).start()
        token[...] = jnp.zeros_like(token)

    lands = [pltpu.with_memory_space_constraint(lax.empty(s.shape, s.dtype), pltpu.HBM) for s in land_shapes]
    srcs = [pltpu.with_memory_space_constraint(s, pltpu.HBM) for s in srcs]
    out_shape = ([pltpu.SemaphoreType.DMA((n_copies,)), pltpu.SemaphoreType.DMA((n_copies,))]
                 + [pltpu.HBM(s.shape, s.dtype) for s in srcs] + [pltpu.HBM(s.shape, s.dtype) for s in land_shapes]
                 + [jax.ShapeDtypeStruct((8, LANES), F32)])
    res = pl.pallas_call(
        body, name=name, out_shape=out_shape,
        in_specs=[HBM_SPEC] * (2 * n) + [ANY_SPEC],
        out_specs=[SEM_SPEC, SEM_SPEC] + [HBM_SPEC] * (2 * n) + [pl.BlockSpec(memory_space=pltpu.VMEM)],
        input_output_aliases={i: 2 + i for i in range(2 * n)},
        compiler_params=pltpu.CompilerParams(has_side_effects=DATAFLOW),
    )(*srcs, *lands, after)
    return dict(sems=res[:2], srcs=res[2:2 + n], lands=res[2 + n:2 + 2 * n], token=res[-1], n=n)


def _split_wait(name, plan_fn, started, after):
    n = started["n"]

    def body(*refs):
        src_refs, land_refs = refs[:n], refs[n:2 * n]
        send_sems, recv_sems = refs[2 * n], refs[2 * n + 1]
        for k, (src, _, to, landed) in enumerate(plan_fn(src_refs, land_refs)):
            copy = _remote(src, landed, send_sems.at[k], recv_sems.at[k], to)
            copy.wait_send()
            copy.wait_recv()

    srcs, lands = started["srcs"], started["lands"]
    res = pl.pallas_call(
        body, name=name,
        out_shape=[pltpu.HBM(s.shape, s.dtype) for s in srcs] + [pltpu.HBM(s.shape, s.dtype) for s in lands],
        in_specs=[HBM_SPEC] * (2 * n) + [SEM_SPEC, SEM_SPEC, ANY_SPEC],
        out_specs=[HBM_SPEC] * (2 * n),
        input_output_aliases={i: i for i in range(2 * n)},
        compiler_params=pltpu.CompilerParams(has_side_effects=DATAFLOW),
    )(*srcs, *lands, *started["sems"], after)
    return res[n:]


def _swap_halves(stacks, name):
    n = len(stacks)

    def body(*refs):
        ins, outs = refs[:n], refs[n:2 * n]
        send_sems, recv_sems = refs[2 * n:]
        x, y, c, own, sib, chips, chip_idx = _place()
        cps = []
        for i in range(n):
            h = stacks[i].shape[2] // 2
            src = ins[i].at[:, :, pl.ds(pl.multiple_of((1 - c) * h, LANES), h)]
            cps.append(_remote(src, outs[i], send_sems.at[i], recv_sems.at[i], sib))
        for cp in cps:
            cp.start()
        for cp in cps:
            cp.wait()

    out_shape = [jax.ShapeDtypeStruct((N_CHIPS, s.shape[1], s.shape[2] // 2), s.dtype) for s in stacks]
    return pl.pallas_call(
        body, name=name, out_shape=out_shape,
        in_specs=[HBM_SPEC] * n, out_specs=[HBM_SPEC] * n,
        scratch_shapes=[pltpu.SemaphoreType.DMA((n,)), pltpu.SemaphoreType.DMA((n,))],
    )(*stacks)


def _add_half(stack, landed, place, name):
    _, rows, h = landed.shape

    def body(place_ref, a_ref, b_ref, o_ref, own_ref):
        part = (a_ref[...] + b_ref[...]).astype(o_ref.dtype)
        o_ref[...] = part

        @pl.when(pl.program_id(0) == place_ref[1])
        def _():
            own_ref[...] = part[0]

    return pl.pallas_call(
        body, name=name,
        out_shape=[jax.ShapeDtypeStruct(landed.shape, BF16), jax.ShapeDtypeStruct((rows, h), BF16)],
        grid_spec=pltpu.PrefetchScalarGridSpec(
            num_scalar_prefetch=1, grid=(N_CHIPS,),
            in_specs=[pl.BlockSpec((1, rows, h), lambda j, p: (j, 0, p[0])),
                      pl.BlockSpec((1, rows, h), lambda j, p: (j, 0, 0))],
            out_specs=[pl.BlockSpec((1, rows, h), lambda j, p: (j, 0, 0)),
                       pl.BlockSpec((rows, h), lambda j, p: (0, 0))]),
        compiler_params=_params(("arbitrary",)),
    )(place, stack, landed)


def _exchange_partials(parts):
    n = len(parts)

    def body(*refs):
        ins, outs = refs[:n], refs[n:2 * n]
        send_sems, recv_sems = refs[2 * n:]
        x, y, c, own, sib, chips, chip_idx = _place()
        sent = []
        for i in range(n):
            for j, chip in enumerate(chips):
                k = i * 3 + j
                sent.append(_remote(ins[i].at[chip_idx[j]], outs[i].at[j], send_sems.at[k], recv_sems.at[k],
                                    (*chip, c)))
        for cp in sent:
            cp.start()
        for i in range(n):
            for j in range(len(chips)):
                k = i * 3 + j
                landed = outs[i].at[j]
                _remote(landed, landed, send_sems.at[k], recv_sems.at[k], sib).wait_recv()
        for cp in sent:
            cp.wait_send()

    return pl.pallas_call(
        body, name="rs_exchange_partials",
        out_shape=[jax.ShapeDtypeStruct((3,) + p.shape[1:], p.dtype) for p in parts],
        in_specs=[HBM_SPEC] * n, out_specs=[HBM_SPEC] * n,
        scratch_shapes=[pltpu.SemaphoreType.DMA((3 * n,)), pltpu.SemaphoreType.DMA((3 * n,))],
    )(*parts)


def _sum_partials(own_part, landed, name):
    _, h, cols = landed.shape

    def body(own_ref, a_ref, o_ref):
        acc = own_ref[...].astype(F32)
        for s in range(3):
            acc = acc + a_ref[s].astype(F32)
        o_ref[...] = acc

    return pl.pallas_call(
        body, name=name, out_shape=jax.ShapeDtypeStruct((h, cols), F32), grid=(1,),
        in_specs=[pl.BlockSpec((h, cols), lambda i: (0, 0)), pl.BlockSpec(landed.shape, lambda i: (0, 0, 0))],
        out_specs=pl.BlockSpec((h, cols), lambda i: (0, 0)),
        compiler_params=_params(("arbitrary",)),
    )(own_part, landed)


def _share_halves(halves, name):
    n = len(halves)

    def body(*refs):
        ins, outs = refs[:n], refs[n:2 * n]
        send_sems, recv_sems = refs[2 * n:]
        x, y, c, own, sib, chips, chip_idx = _place()
        cps = [_remote(ins[i], outs[i], send_sems.at[i], recv_sems.at[i], sib) for i in range(n)]
        for cp in cps:
            cp.start()
        for cp in cps:
            cp.wait()

    return pl.pallas_call(
        body, name=name,
        out_shape=[jax.ShapeDtypeStruct(p.shape, p.dtype) for p in halves],
        in_specs=[HBM_SPEC] * n, out_specs=[HBM_SPEC] * n,
        scratch_shapes=[pltpu.SemaphoreType.DMA((n,)), pltpu.SemaphoreType.DMA((n,))],
    )(*halves)


def _allreduce_small(packed):
    rows = packed.shape[0]
    n_dev = 8

    def body(in_ref, out_ref, gath, send_sems, recv_sems):
        x, y, c = lax.axis_index("x"), lax.axis_index("y"), lax.axis_index("c")
        me = 4 * x + 2 * y + c
        gath[me] = in_ref[...]
        cps = []
        for k in range(1, n_dev):
            fx, fy, fc = (k >> 2) & 1, (k >> 1) & 1, k & 1
            to = (x ^ fx, y ^ fy, c ^ fc)
            cps.append(_remote(in_ref, gath.at[me], send_sems.at[k - 1], recv_sems.at[k - 1], to))
        for cp in cps:
            cp.start()
        for k in range(1, n_dev):
            fx, fy, fc = (k >> 2) & 1, (k >> 1) & 1, k & 1
            src = 4 * (x ^ fx) + 2 * (y ^ fy) + (c ^ fc)
            slot = gath.at[src]
            _remote(slot, slot, send_sems.at[k - 1], recv_sems.at[k - 1], (x, y, c)).wait_recv()
        for cp in cps:
            cp.wait_send()
        acc = gath[0]
        for d in range(1, n_dev):
            acc = acc + gath[d]
        out_ref[...] = acc

    vm = pl.BlockSpec(memory_space=pltpu.VMEM)
    return pl.pallas_call(
        body, name="allreduce_small", out_shape=jax.ShapeDtypeStruct(packed.shape, F32),
        in_specs=[vm], out_specs=vm,
        scratch_shapes=[pltpu.VMEM((n_dev, rows, LANES), F32),
                        pltpu.SemaphoreType.DMA((n_dev - 1,)), pltpu.SemaphoreType.DMA((n_dev - 1,))],
    )(packed)


def _adam(col, w, g, m, v):
    m2 = ADAM_B1 * m + (1.0 - ADAM_B1) * g
    v2 = ADAM_B2 * v + (1.0 - ADAM_B2) * (g * g)
    m_hat = m2 / (1.0 - ADAM_B1 ** ADAM_STEP)
    v_hat = v2 / (1.0 - ADAM_B2 ** ADAM_STEP)
    delta = -ADAM_LR * (m_hat / (jnp.sqrt(v_hat) + ADAM_EPS) + ADAM_WD * w)
    return delta, m2, v2


def _adam_call(w, g, m, v, name):
    rows, cols = w.shape
    tm = rows
    for cand in (256, 352, 176, 128, 64, 48, 16, 8):
        if rows % cand == 0:
            tm = cand
            break
    return _tiles(_adam, name=name, rows=rows, tm=tm,
                  row_ins=[(w, cols, 0), (g, cols, 0), (m, cols, 0), (v, cols, 0)],
                  row_outs=[(cols, F32)] * 3)


def _adam_big(w, g_mine, g_other, m, v, place, name):
    rows, cols = w.shape
    tc = 256
    nt = cols // 2 // tc

    def body(place_ref, w_ref, gm_ref, go_ref, m_ref, v_ref, g_out, d_out, m_out, v_out):
        g = jnp.where(pl.program_id(0) == place_ref[0], gm_ref[...], go_ref[...])
        d, m2, v2 = _adam(None, w_ref[...], g, m_ref[...], v_ref[...])
        g_out[...] = g
        d_out[...] = d
        m_out[...] = m2
        v_out[...] = v2

    full = pl.BlockSpec((rows, tc), lambda hh, i, p: (0, hh * nt + i))
    half = pl.BlockSpec((rows, tc), lambda hh, i, p: (0, i))
    return pl.pallas_call(
        body, name=name, out_shape=[jax.ShapeDtypeStruct(w.shape, F32)] * 4,
        grid_spec=pltpu.PrefetchScalarGridSpec(
            num_scalar_prefetch=1, grid=(2, nt),
            in_specs=[full, half, half, full, full], out_specs=[full] * 4),
        compiler_params=_params(("arbitrary", "arbitrary")),
    )(place, w, g_mine, g_other, m, v)


def _adam_untiled_rows(w, g_mine, g_other, m, v, place, name):
    rows, _, cols = w.shape
    tc = 256
    nt = cols // 2 // tc
    rb = next(r for r in (206, 128, 103, rows) if rows % r == 0)

    def body(place_ref, w_ref, gm_ref, go_ref, m_ref, v_ref, g_out, d_out, m_out, v_out):
        g = jnp.where(pl.program_id(0) == place_ref[0], gm_ref[...], go_ref[...])
        d, m2, v2 = _adam(None, w_ref[...], g, m_ref[...], v_ref[...])
        g_out[...] = g
        d_out[...] = d
        m_out[...] = m2
        v_out[...] = v2

    full = pl.BlockSpec((rb, 1, tc), lambda hh, i, r, p: (r, 0, hh * nt + i))
    half = pl.BlockSpec((rb, 1, tc), lambda hh, i, r, p: (r, 0, i))
    return pl.pallas_call(
        body, name=name, out_shape=[jax.ShapeDtypeStruct(w.shape, F32)] * 4,
        grid_spec=pltpu.PrefetchScalarGridSpec(
            num_scalar_prefetch=1, grid=(2, nt, rows // rb),
            in_specs=[full, half, half, full, full], out_specs=[full] * 4),
        compiler_params=_params(("arbitrary", "arbitrary", "arbitrary")),
    )(place, w, g_mine, g_other, m, v)


def _pack(arrays):
    flat = []
    for a in arrays:
        a = a.reshape(-1).astype(F32)
        flat.append(jnp.pad(a, (0, (-a.size) % LANES)))
    out = jnp.concatenate(flat)
    out = jnp.pad(out, (0, (-out.size) % (8 * LANES)))
    return out.reshape(-1, LANES)


def _unpack(packed, shapes):
    flat = packed.reshape(-1)
    out, off = [], 0
    for s in shapes:
        size = int(np.prod(s))
        out.append(flat[off:off + size].reshape(s))
        off += size + (-size) % LANES
    return out


def kernel(x, norm1_w, w_in, gdn_conv_w, gdn_A_log, gdn_dt_bias, gdn_out_norm_w, fox_f_bias, fox_q_norm_w, fox_k_norm_w, w_out, norm2_w, w_ffn_gate, w_ffn_up, w_ffn_down, final_norm_w, loss_target, m_norm1_w, m_w_in, m_gdn_conv_w, m_gdn_A_log, m_gdn_dt_bias, m_gdn_out_norm_w, m_fox_f_bias, m_fox_q_norm_w, m_fox_k_norm_w, m_w_out, m_norm2_w, m_w_ffn_gate, m_w_ffn_up, m_w_ffn_down, m_final_norm_w, v_norm1_w, v_w_in, v_gdn_conv_w, v_gdn_A_log, v_gdn_dt_bias, v_gdn_out_norm_w, v_fox_f_bias, v_fox_q_norm_w, v_fox_k_norm_w, v_w_out, v_norm2_w, v_w_ffn_gate, v_w_ffn_up, v_w_ffn_down, v_final_norm_w):
    cx, cy, cc = lax.axis_index("x"), lax.axis_index("y"), lax.axis_index("c")
    own = 2 * cx + cy
    place = jnp.stack([cc, own]).astype(jnp.int32)

    names = ["w_in", "w_out", "w_gate", "w_up", "w_down"]
    is_t = [True, False, True, True, False]
    to_t = lambda a, t: a[0].T if t else a[0]
    from_t = lambda a, t: (a.T if t else a)[None]
    big_w = [to_t(a, t) for a, t in zip([w_in, w_out, w_ffn_gate, w_ffn_up, w_ffn_down], is_t)]
    big_m = [to_t(a, t) for a, t in zip([m_w_in, m_w_out, m_w_ffn_gate, m_w_ffn_up, m_w_ffn_down], is_t)]
    big_v = [to_t(a, t) for a, t in zip([v_w_in, v_w_out, v_w_ffn_gate, v_w_ffn_up, v_w_ffn_down], is_t)]
    shards = [w.astype(BF16) for w in big_w]
    (w_in_g,), conv_g = _allgather_weights(shards[:1], gdn_conv_w[0])
    rest = _split_start("gather_rest_start", _gather_plan, shards[1:],
                        [jax.ShapeDtypeStruct((N_CHIPS,) + s.shape, BF16) for s in shards[1:]],
                        n_copies=4 * len(shards[1:]), after=w_in_g)
    w_cat = _cat_weights(w_in_g.reshape(D_IN, D_MODEL))
    conv_full = conv_g.transpose(1, 0, 2).reshape(CONV_K, 3 * WIDTH)

    def late_weights(after):
        w_out_g, w_gate_g, w_up_g, w_down_g = _split_wait("gather_rest_wait", _gather_plan, rest, after)
        return w_out_g.reshape(D_MODEL, D_MODEL), w_gate_g, w_up_g, w_down_g

    def start_reduction(stacks, nms, tag):
        landed = _swap_halves(stacks, "rs_swap_" + tag)
        added = [_add_half(s, l, place, "rs_add_" + nm) for s, l, nm in zip(stacks, landed, nms)]
        parts = [a[0] for a in added]
        started = _split_start("exchange_" + tag + "_start", _exchange_plan, parts,
                               [jax.ShapeDtypeStruct((3,) + p.shape[1:], p.dtype) for p in parts],
                               n_copies=3 * len(parts), after=parts[0])
        return dict(own=[a[1] for a in added], started=started, tag=tag, names=nms)

    def finish_reduction(red, after, updates):
        landed = _split_wait("exchange_" + red["tag"] + "_wait", _exchange_plan, red["started"], after)
        halves = [_sum_partials(o, p, "rs_sum_" + nm) for o, p, nm in zip(red["own"], landed, red["names"])]
        others = _share_halves(halves, "rs_share_" + red["tag"])
        return [upd(gm, go) for upd, gm, go in zip(updates, halves, others)]

    def transport_update(b):
        def upd(gm, go):
            res = _adam_big(big_w[b], gm, go, big_m[b], big_v[b], place, "adam_" + names[b])
            return [from_t(a, is_t[b]) for a in res]
        return upd

    def w_in_update(gm, go):
        rows3 = lambda a: jnp.transpose(a, (2, 0, 1))
        res = _adam_untiled_rows(rows3(w_in), gm[:, None, :], go[:, None, :], rows3(m_w_in), rows3(v_w_in),
                                 place, "adam_w_in")
        return [jnp.transpose(a, (1, 2, 0)) for a in res]

    early = {}

    def early_grads_ready(g_out, g_gate, g_up, g_down):
        stacks = [g_out.reshape(N_CHIPS, D_MODEL // N_CHIPS, D_MODEL), g_gate, g_up, g_down]
        early.update(start_reduction(stacks, names[1:], "early"))
        return early["started"]["token"][0, 0]

    grad_x, g_cat, _, _, _, _, small = _local_step(
        x[0], loss_target[0], norm1_w + rest["token"][0, 0], w_cat, conv_full, gdn_A_log[0], gdn_dt_bias[0],
        gdn_out_norm_w[0], fox_f_bias[0], fox_q_norm_w[0], fox_k_norm_w[0], norm2_w, final_norm_w.reshape(1, -1),
        late_weights, early_grads_ready)

    late = start_reduction([_uncat_grad(g_cat).reshape(N_CHIPS, D_IN // N_CHIPS, D_MODEL)], names[:1], "w_in")
    big_upd = finish_reduction(early, late["started"]["token"], [transport_update(b) for b in range(1, 5)])

    order = ["norm1_w", "conv_w", "a_log", "dt_bias", "out_norm_w", "f_bias", "q_norm_w", "k_norm_w",
             "norm2_w", "final_w"]
    red = _allreduce_small(_pack([small[k] for k in order] + [small["loss"]]))
    red_shapes = [(1, D_MODEL), (CONV_K, 3 * WIDTH), (1, HEADS), (1, HEADS), (1, HEAD_DIM), (1, HEADS),
                  (1, HEAD_DIM), (1, HEAD_DIM), (1, D_MODEL), (D_MODEL,), ()]
    red_list = _unpack(red, red_shapes)
    loss = red_list[-1]
    small_g = dict(zip(order, red_list[:-1]))
    shard_cols = 3 * WIDTH // N_CHIPS
    small_g["conv_w"] = lax.dynamic_slice_in_dim(small_g["conv_w"], own * shard_cols, shard_cols, axis=1)[None]
    small_w = [norm1_w, gdn_conv_w, gdn_A_log, gdn_dt_bias, gdn_out_norm_w, fox_f_bias, fox_q_norm_w,
               fox_k_norm_w, norm2_w, final_norm_w]
    small_m = [m_norm1_w, m_gdn_conv_w, m_gdn_A_log, m_gdn_dt_bias, m_gdn_out_norm_w, m_fox_f_bias,
               m_fox_q_norm_w, m_fox_k_norm_w, m_norm2_w, m_final_norm_w]
    small_v = [v_norm1_w, v_gdn_conv_w, v_gdn_A_log, v_gdn_dt_bias, v_gdn_out_norm_w, v_fox_f_bias,
               v_fox_q_norm_w, v_fox_k_norm_w, v_norm2_w, v_final_norm_w]
    small_gl = [small_g[k].reshape(w.shape) for k, w in zip(order, small_w)]
    s_delta, s_m, s_v = _adam_call(_pack(small_w), _pack(small_gl), _pack(small_m), _pack(small_v), "adam_small")
    big_upd = finish_reduction(late, s_delta, [w_in_update]) + big_upd
    shapes = [w.shape for w in small_w]
    s_delta, s_m, s_v = _unpack(s_delta, shapes), _unpack(s_m, shapes), _unpack(s_v, shapes)

    big_pos = {1: 0, 9: 1, 11: 2, 12: 3, 13: 4}
    small_pos = {0: 0, 2: 1, 3: 2, 4: 3, 5: 4, 6: 5, 7: 6, 8: 7, 10: 8, 14: 9}
    grads, deltas, new_m, new_v = [], [], [], []
    for pos in range(15):
        if pos in big_pos:
            b = big_pos[pos]
            g, d, m2, v2 = big_upd[b]
            grads.append(g)
            deltas.append(d)
            new_m.append(m2)
            new_v.append(v2)
        else:
            s = small_pos[pos]
            grads.append(small_gl[s])
            deltas.append(s_delta[s])
            new_m.append(s_m[s])
            new_v.append(s_v[s])
    return (loss, grad_x[None], *grads, *deltas, *new_m, *new_v)
```

```python
import jax
import jax.numpy as jnp
import numpy as np
from jax import lax
from jax.experimental import pallas as pl
from jax.experimental.pallas import tpu as pltpu

F32 = jnp.float32
BF16 = jnp.bfloat16

D_MODEL = 1024
HEADS = 8
HEAD_DIM = 64
PAIRS = HEADS // 2
WIDTH = HEADS * HEAD_DIM
CHUNK = 64
CONV_K = 4
D_FF = 2816
FF_SHARD = D_FF // 4
EPS = 1e-6
SCALE = HEAD_DIM ** -0.5
LANES = 128
N_CHIPS = 4
D_IN = 4120
D_CAT = 4224
COL_SMALL = 4096 // LANES

ADAM_LR = 0.001
ADAM_B1 = 0.9
ADAM_B2 = 0.999
ADAM_EPS = 1e-08
ADAM_WD = 0.01
ADAM_STEP = 10

VMEM_LIMIT = 56 * 1024 * 1024
MESH = pl.DeviceIdType.MESH
HIGHEST = lax.Precision.HIGHEST


def _params(sem):
    return pltpu.CompilerParams(dimension_semantics=sem, vmem_limit_bytes=VMEM_LIMIT)


_CONTRACT = {"nn": ((1,), (0,)), "nt": ((1,), (1,)), "tn": ((0,), (0,))}


def _mm(a, b, *, dims, name, out_dtype=F32, add=None, tm=1024, tn=512, tk=512):
    if dims == "nn":
        (m, k), (k2, n) = a.shape, b.shape
    elif dims == "nt":
        (m, k), (n, k2) = a.shape, b.shape
    else:
        (k, m), (k2, n) = a.shape, b.shape
    assert k == k2, (a.shape, b.shape, dims)
    tm, tn, tk = min(tm, m), min(tn, n), min(tk, k)
    assert m % tm == 0 and n % tn == 0 and k % tk == 0, (m, n, k, tm, tn, tk)
    nk = k // tk
    a_spec = (pl.BlockSpec((tk, tm), lambda i, j, kk: (kk, i)) if dims == "tn"
              else pl.BlockSpec((tm, tk), lambda i, j, kk: (i, kk)))
    b_spec = (pl.BlockSpec((tn, tk), lambda i, j, kk: (j, kk)) if dims == "nt"
              else pl.BlockSpec((tk, tn), lambda i, j, kk: (kk, j)))
    o_spec = pl.BlockSpec((tm, tn), lambda i, j, kk: (i, j))
    contract = (_CONTRACT[dims], ((), ()))
    has_add = add is not None

    def body(*refs):
        a_ref, b_ref = refs[:2]
        add_ref = refs[2] if has_add else None
        o_ref = refs[3] if has_add else refs[2]
        part = lax.dot_general(a_ref[...].astype(BF16), b_ref[...].astype(BF16), contract,
                               preferred_element_type=F32)

        def finish(r):
            if has_add:
                r = r + add_ref[...].astype(F32)
            o_ref[...] = r.astype(out_dtype)

        if nk == 1:
            finish(part)
            return
        acc = refs[-1]
        kk = pl.program_id(2)

        @pl.when(kk == 0)
        def _():
            acc[...] = part

        @pl.when(kk > 0)
        def _():
            acc[...] += part

        @pl.when(kk == nk - 1)
        def _():
            finish(acc[...])

    ins = [a, b] + ([add] if has_add else [])
    in_specs = [a_spec, b_spec] + ([o_spec] if has_add else [])
    return pl.pallas_call(
        body, name=name, grid=(m // tm, n // tn, nk),
        in_specs=in_specs, out_specs=o_spec,
        out_shape=jax.ShapeDtypeStruct((m, n), out_dtype),
        scratch_shapes=[pltpu.VMEM((tm, tn), F32)] if nk > 1 else [],
        compiler_params=_params(("parallel", "parallel", "arbitrary")),
    )(*ins)


def _mm_blocks(a, b, *, name, grid, a_spec, b_spec, o_spec, out_shape, dims, n_sum=0, add=None, add_spec=None):
    contract = (_CONTRACT[dims], ((), ()))
    has_add = add is not None

    def body(*refs):
        a_ref, b_ref = refs[:2]
        o_ref = refs[-1]
        dot = lambda x, y: lax.dot_general(x.astype(BF16), y.astype(BF16), contract, preferred_element_type=F32)
        if n_sum:
            r = dot(a_ref[0], b_ref[0])
            for s in range(1, n_sum):
                r = r + dot(a_ref[s], b_ref[s])
        else:
            r = dot(a_ref[...], b_ref[...])
        if has_add:
            r = r + refs[2][...].astype(F32)
        o_ref[...] = r.astype(o_ref.dtype)

    return pl.pallas_call(
        body, name=name, grid=grid,
        in_specs=[a_spec, b_spec] + ([add_spec] if has_add else []), out_specs=o_spec, out_shape=out_shape,
        compiler_params=_params(("parallel",) * len(grid)),
    )(*([a, b] + ([add] if has_add else [])))


def _tiles(fn, *, name, rows, tm, ncol=1, row_ins=(), col_consts=(), full_consts=(),
           row_outs=(), acc_outs=()):
    nt = rows // tm
    assert rows % tm == 0
    n_full, n_col, n_row = len(full_consts), len(col_consts), len(row_ins)
    n_ro, n_acc = len(row_outs), len(acc_outs)

    def body(*refs):
        ins = refs[:n_full + n_col + n_row]
        outs = refs[n_full + n_col + n_row:]
        i = pl.program_id(1)
        res = fn(pl.program_id(0), *[r[...] for r in ins])
        for r, v in zip(outs[:n_ro], res[:n_ro]):
            r[...] = v.astype(r.dtype)
        if n_acc:
            @pl.when(i == 0)
            def _():
                for r in outs[n_ro:]:
                    r[...] = jnp.zeros_like(r)
            for r, v in zip(outs[n_ro:], res[n_ro:]):
                r[...] += v

    in_specs = [pl.BlockSpec(a.shape, lambda j, i, nd=a.ndim: (0,) * nd) for a in full_consts]
    in_specs += [pl.BlockSpec((nr, w), lambda j, i, o=o: (0, o + j)) for (_, nr, w, o) in col_consts]
    in_specs += [pl.BlockSpec((tm, w), lambda j, i, o=o: (i, o + j)) for (_, w, o) in row_ins]
    out_specs = [pl.BlockSpec((tm, w), lambda j, i: (i, j)) for (w, _) in row_outs]
    out_specs += [pl.BlockSpec((nr, w), lambda j, i: (0, j)) for (nr, w) in acc_outs]
    out_shape = [jax.ShapeDtypeStruct((rows, w * ncol), dt) for (w, dt) in row_outs]
    out_shape += [jax.ShapeDtypeStruct((nr, w * ncol), F32) for (nr, w) in acc_outs]
    args = list(full_consts) + [c[0] for c in col_consts] + [r[0] for r in row_ins]
    out = pl.pallas_call(
        body, name=name, grid=(ncol, nt), in_specs=in_specs, out_specs=out_specs, out_shape=out_shape,
        compiler_params=_params(("parallel", "arbitrary")),
    )(*args)
    return out


def _rms(x, w):
    return x * lax.rsqrt(jnp.mean(x * x, axis=-1, keepdims=True) + EPS) * w


def _lane_lo(shape):
    return lax.broadcasted_iota(jnp.int32, shape, len(shape) - 1) < HEAD_DIM


def _pair_sum(x):
    lo = _lane_lo(x.shape)
    s0 = jnp.sum(jnp.where(lo, x, 0.0), axis=-1, keepdims=True)
    s1 = jnp.sum(jnp.where(lo, 0.0, x), axis=-1, keepdims=True)
    return jnp.where(lo, s0, s1)


def _head_col(x, lo, h):
    keep = lo if h == 0 else jnp.logical_not(lo)
    return jnp.max(jnp.where(keep, x, -jnp.inf), axis=-1, keepdims=True)


def _softplus(x):
    return jnp.maximum(x, 0.0) + jnp.log1p(jnp.exp(-jnp.abs(x)))


def _silu(x):
    return x * jax.nn.sigmoid(x)


def _dot(a, b, contract):
    return lax.dot_general(a.astype(BF16), b.astype(BF16), (contract, ((), ())),
                           preferred_element_type=F32)


def _dot32(a, b, contract):
    return lax.dot_general(a, b, (contract, ((), ())), precision=HIGHEST, preferred_element_type=F32)


def _bd(y):
    yy = jnp.concatenate([y, y], axis=0)
    r = lax.broadcasted_iota(jnp.int32, yy.shape, 0) < HEAD_DIM
    c = lax.broadcasted_iota(jnp.int32, yy.shape, 1) < HEAD_DIM
    return jnp.where(r == c, yy, 0.0)


def _pp(x, y):
    return _dot(x, _bd(y), _CONTRACT["nn"])


def _pp_nt(x, y):
    return _dot(x, _bd(y), _CONTRACT["nt"])


def _pp_tn(x, y):
    full = _dot(x, y, _CONTRACT["tn"])
    return jnp.where(_lane_lo((HEAD_DIM, LANES)), full[:HEAD_DIM], full[HEAD_DIM:])


def _gdn_masks():
    row = lax.broadcasted_iota(jnp.int32, (CHUNK, LANES), 0)
    col = lax.broadcasted_iota(jnp.int32, (CHUNK, LANES), 1) % HEAD_DIM
    return row, col


def _interleave(chains):
    live = list(chains)
    while live:
        for g in list(live):
            try:
                next(g)
            except StopIteration:
                live.remove(g)


def _gdn_forward(qkv, betax, gcx, grow, rows):
    nchunk = rows // CHUNK

    def body(q_ref, k_ref, v_ref, bx_ref, gx_ref, gr_ref, o_ref, ss_ref, ts_ref, state):
        n = pl.program_id(0)

        @pl.when(n == 0)
        def _():
            state[...] = jnp.zeros_like(state)

        row, col = _gdn_masks()
        incl, strict = col <= row, col < row

        def chain(p):
            lanes = pl.ds(p * LANES, LANES)
            q, k, v, bx, gx = q_ref[:, lanes], k_ref[:, lanes], v_ref[:, lanes], bx_ref[:, lanes], gx_ref[:, lanes]
            gr = gr_ref[0, p]
            glast = gx_ref[pl.ds(CHUNK - 1, 1), lanes]
            s = state[p]
            dm = jnp.where(incl, jnp.exp(jnp.minimum(gx - gr, 0.0)), 0.0)
            kb, vb, eg, qs = k * bx, v * bx, jnp.exp(gx), q * SCALE
            yield
            big_g, big_p = _pp_nt(kb, k), _pp_nt(qs, k)
            yield
            x = -jnp.where(strict, big_g * dm, 0.0)
            att = jnp.where(incl, big_p * dm, 0.0)
            tm = jnp.where(row == col, 1.0, 0.0) + x
            x = _pp(x, x)
            yield
            for _ in range(4):
                step, x = _pp(tm, x), _pp(x, x)
                yield
                tm = tm + step
            tm = tm + _pp(tm, x)
            yield
            u, w = _pp(tm, vb), _pp(tm, kb * eg)
            yield
            ws, qgs = _pp(w, s), _pp(qs * eg, s)
            yield
            vn = u - ws
            kd = k * jnp.exp(glast - gx)
            avn, upd = _pp(att, vn), _pp_tn(kd, vn)
            yield
            ss_ref[0, p] = s
            ts_ref[0, p] = tm
            o_ref[:, lanes] = qgs + avn
            state[p] = s * jnp.exp(glast) + upd

        _interleave([chain(p) for p in range(PAIRS)])

    blk = lambda j: pl.BlockSpec((CHUNK, WIDTH), lambda n, j=j: (n, j))
    sv = pl.BlockSpec((1, PAIRS, CHUNK, LANES), lambda n: (n, 0, 0, 0))
    return pl.pallas_call(
        body, name="gdn_fwd", grid=(nchunk,),
        in_specs=[blk(0), blk(1), blk(2), blk(0), blk(0),
                  pl.BlockSpec((1, PAIRS, 1, LANES), lambda n: (n, 0, 0, 0))],
        out_specs=[blk(0), sv, sv],
        out_shape=[jax.ShapeDtypeStruct((rows, WIDTH), F32),
                   jax.ShapeDtypeStruct((nchunk, PAIRS, CHUNK, LANES), F32),
                   jax.ShapeDtypeStruct((nchunk, PAIRS, CHUNK, LANES), F32)],
        scratch_shapes=[pltpu.VMEM((PAIRS, CHUNK, LANES), F32)],
        compiler_params=_params(("arbitrary",)),
    )(qkv, qkv, qkv, betax, gcx, grow)


def _gdn_backward(qkv, betax, gcx, grow, ssave, tsave, do, rows):
    nchunk = rows // CHUNK

    def body(q_ref, k_ref, v_ref, bx_ref, gx_ref, gr_ref, ss_ref, ts_ref, do_ref,
             dq_ref, dk_ref, dv_ref, dbx_ref, dgx_ref, dgr_ref, dstate):
        n = pl.program_id(0)

        @pl.when(n == 0)
        def _():
            dstate[...] = jnp.zeros_like(dstate)

        row, col = _gdn_masks()
        incl, strict = col <= row, col < row

        def chain(p):
            lanes = pl.ds(p * LANES, LANES)
            q, k, v, bx, gx = q_ref[:, lanes], k_ref[:, lanes], v_ref[:, lanes], bx_ref[:, lanes], gx_ref[:, lanes]
            gr = gr_ref[0, p]
            glast = gx_ref[pl.ds(CHUNK - 1, 1), lanes]
            s, tm, d_o = ss_ref[0, p], ts_ref[0, p], do_ref[:, lanes]
            ds_out = dstate[p]
            dm = jnp.where(incl, jnp.exp(jnp.minimum(gx - gr, 0.0)), 0.0)
            kb, vb, eg, qs = k * bx, v * bx, jnp.exp(gx), q * SCALE
            kbg, qg = kb * eg, qs * eg
            ed = jnp.exp(glast - gx)
            kd = k * ed
            eglast = jnp.exp(glast)
            yield
            big_g, big_p = _pp_nt(kb, k), _pp_nt(qs, k)
            u, w = _pp(tm, vb), _pp(tm, kbg)
            dqg, kds = _pp_nt(d_o, s), _pp(kd, ds_out)
            yield
            low = jnp.where(strict, big_g * dm, 0.0)
            att = jnp.where(incl, big_p * dm, 0.0)
            ws, atd = _pp(w, s), _pp_tn(att, d_o)
            yield
            vn = u - ws
            dvn = kds + atd
            dkd, datt_raw = _pp_nt(vn, ds_out), _pp_nt(d_o, vn)
            dw_neg, dvb = _pp_nt(dvn, s), _pp_tn(tm, dvn)
            dtm_a, wdv = _pp_nt(dvn, vb), _pp_tn(w, dvn)
            qgd = _pp_tn(qg, d_o)
            yield
            datt = jnp.where(incl, datt_raw, 0.0)
            dw = -dw_neg
            dtm_b, dkbg = _pp_nt(dw, kbg), _pp_tn(tm, dw)
            dbig_p = datt * dm
            dqs_a, dk_p = _pp(dbig_p, k), _pp_tn(dbig_p, qs)
            yield
            inner = _pp_tn(tm, dtm_a + dtm_b)
            yield
            dlow = jnp.where(strict, -_pp_nt(inner, tm), 0.0)
            yield
            dbig_g = dlow * dm
            dkb_a, dk_g = _pp(dbig_g, k), _pp_tn(dbig_g, kb)
            yield
            dkb = dkb_a + dkbg * eg
            dqs = dqs_a + dqg * eg
            dk = dk_g + dk_p + dkd * ed + dkb * bx
            z = dlow * low + datt * att
            kdterm = dkd * kd
            dglast = (jnp.sum(ds_out * s, axis=0, keepdims=True) * eglast
                      + jnp.sum(kdterm, axis=0, keepdims=True))
            dgx = dqg * qg + dkbg * kbg - kdterm
            dgx = dgx + jnp.where(col == 0, _pair_sum(z), 0.0)
            dgx = dgx + jnp.where(row == CHUNK - 1, dglast, 0.0)
            dq_ref[:, lanes] = dqs * SCALE
            dk_ref[:, lanes] = dk
            dv_ref[:, lanes] = dvb * bx
            dbx_ref[:, lanes] = dkb * k + dvb * v
            dgx_ref[:, lanes] = dgx
            dgr_ref[0, p] = -jnp.sum(z, axis=0, keepdims=True)
            dstate[p] = ds_out * eglast + qgd - wdv

        _interleave([chain(p) for p in range(PAIRS)])

    last = nchunk - 1
    blk = lambda j: pl.BlockSpec((CHUNK, WIDTH), lambda n, j=j: (last - n, j))
    sv = pl.BlockSpec((1, PAIRS, CHUNK, LANES), lambda n: (last - n, 0, 0, 0))
    gr_spec = pl.BlockSpec((1, PAIRS, 1, LANES), lambda n: (last - n, 0, 0, 0))
    wide = jax.ShapeDtypeStruct((rows, WIDTH), F32)
    return pl.pallas_call(
        body, name="gdn_bwd", grid=(nchunk,),
        in_specs=[blk(0), blk(1), blk(2), blk(0), blk(0), gr_spec, sv, sv, blk(0)],
        out_specs=[blk(0)] * 5 + [gr_spec],
        out_shape=[wide] * 5 + [jax.ShapeDtypeStruct((nchunk, PAIRS, 1, LANES), F32)],
        scratch_shapes=[pltpu.VMEM((PAIRS, CHUNK, LANES), F32)],
        compiler_params=_params(("arbitrary",)),
    )(qkv, qkv, qkv, betax, gcx, grow, ssave, tsave, do)


ATT_TQ = 256


def _att_scores(qh, kt, fk, diag):
    s = _dot(qh, kt, _CONTRACT["nt"]) - fk
    if diag:
        r = lax.broadcasted_iota(jnp.int32, s.shape, 0)
        c = lax.broadcasted_iota(jnp.int32, s.shape, 1)
        s = jnp.where(r >= c, s, -jnp.inf)
    return s


def _head_masks(n):
    lo = _lane_lo((n, LANES))
    return [lo, jnp.logical_not(lo)]


def _attention_forward(fqk, proj, frow, rows):
    tq = tk = min(ATT_TQ, rows)
    nq = rows // tq
    v_off = 3072 // LANES

    def body(q_ref, k_ref, v_ref, fr_ref, o_ref, lse_ref):
        qi = pl.program_id(1)
        q = q_ref[...] * SCALE
        keep_q, keep_k = _head_masks(tq), _head_masks(tk)
        qh = [jnp.where(keep_q[h], q, 0.0).astype(BF16) for h in range(2)]

        def tile(ki, carry, diag):
            k0 = pl.multiple_of(ki * tk, tk)
            kt = k_ref[pl.ds(k0, tk), :].astype(BF16)
            v_t = v_ref[pl.ds(k0, tk), :]
            out = [None, None]

            def chain(h):
                m, l, acc = carry[h]
                vt = jnp.where(keep_k[h], v_t, 0.0).astype(BF16)
                yield
                s = _att_scores(qh[h], kt, fr_ref[0, pl.ds(h, 1), pl.ds(k0, tk)], diag)
                yield
                m_new = jnp.maximum(m, jnp.max(s, axis=-1, keepdims=True))
                p = jnp.exp(s - m_new)
                alpha = jnp.exp(m - m_new)
                l = alpha * l + jnp.sum(p, axis=-1, keepdims=True)
                p_hi = p.astype(BF16)
                p_lo = p - p_hi.astype(F32)
                yield
                out[h] = (m_new, l, alpha * acc + _dot(p_hi, vt, _CONTRACT["nn"]) + _dot(p_lo, vt, _CONTRACT["nn"]))

            _interleave([chain(0), chain(1)])
            return tuple(out)

        one = (jnp.full((tq, 1), -jnp.inf, F32), jnp.zeros((tq, 1), F32), jnp.zeros((tq, LANES), F32))
        carry = lax.fori_loop(0, qi, lambda ki, c: tile(ki, c, False), (one, one))
        (m0, l0, acc0), (m1, l1, acc1) = tile(qi, carry, True)
        o_ref[...] = acc0 / l0 + acc1 / l1
        lse_ref[...] = jnp.where(keep_q[0], m0 + jnp.log(l0), m1 + jnp.log(l1))

    whole = lambda off: pl.BlockSpec((rows, LANES), lambda p, i, off=off: (0, off + p))
    qblk = lambda off: pl.BlockSpec((tq, LANES), lambda p, i, off=off: (i, off + p))
    wide = jax.ShapeDtypeStruct((rows, WIDTH), F32)
    return pl.pallas_call(
        body, name="fox_fwd", grid=(PAIRS, nq),
        in_specs=[qblk(0), whole(PAIRS), whole(v_off), pl.BlockSpec((1, 2, rows), lambda p, i: (p, 0, 0))],
        out_specs=[qblk(0), qblk(0)], out_shape=[wide, wide],
        compiler_params=_params(("parallel", "arbitrary")),
    )(fqk, fqk, proj, frow)


def _attention_delta(fqk, proj, frow, lse, dao, rows):
    tq = tk = min(ATT_TQ, rows)
    nq = rows // tq
    v_off = 3072 // LANES

    def body(q_ref, k_ref, v_ref, fr_ref, lse_ref, do_ref, delta_ref):
        qi = pl.program_id(1)
        q, d_o, lse_t = q_ref[...] * SCALE, do_ref[...], lse_ref[...]
        keep_q = _head_masks(tq)
        qh = [jnp.where(keep_q[h], q, 0.0).astype(BF16) for h in range(2)]
        doh = [jnp.where(keep_q[h], d_o, 0.0).astype(BF16) for h in range(2)]
        lse_h = [_head_col(lse_t, keep_q[0], h) for h in range(2)]

        def tile(ki, carry, diag):
            k0 = pl.multiple_of(ki * tk, tk)
            kt = k_ref[pl.ds(k0, tk), :].astype(BF16)
            vt = v_ref[pl.ds(k0, tk), :].astype(BF16)
            out = [None, None]

            def chain(h):
                s = _att_scores(qh[h], kt, fr_ref[0, pl.ds(h, 1), pl.ds(k0, tk)], diag)
                dp = _dot(doh[h], vt, _CONTRACT["nt"])
                yield
                out[h] = carry[h] + jnp.sum(jnp.exp(s - lse_h[h]) * dp, axis=-1, keepdims=True)

            _interleave([chain(0), chain(1)])
            return tuple(out)

        zero = jnp.zeros((tq, 1), F32)
        carry = lax.fori_loop(0, qi, lambda ki, c: tile(ki, c, False), (zero, zero))
        d0, d1 = tile(qi, carry, True)
        delta_ref[...] = jnp.where(keep_q[0], d0, d1)

    whole = lambda off: pl.BlockSpec((rows, LANES), lambda p, i, off=off: (0, off + p))
    qblk = lambda off: pl.BlockSpec((tq, LANES), lambda p, i, off=off: (i, off + p))
    return pl.pallas_call(
        body, name="fox_delta", grid=(PAIRS, nq),
        in_specs=[qblk(0), whole(PAIRS), whole(v_off),
                  pl.BlockSpec((1, 2, rows), lambda p, i: (p, 0, 0)), qblk(0), qblk(0)],
        out_specs=qblk(0), out_shape=jax.ShapeDtypeStruct((rows, WIDTH), F32),
        compiler_params=_params(("parallel", "arbitrary")),
    )(fqk, fqk, proj, frow, lse, dao)


def _attention_backward(fqk, proj, frow, ao, lse, dao, rows):
    tq = tk = min(ATT_TQ, rows)
    nq = rows // tq
    v_off = 3072 // LANES

    def body(q_ref, k_ref, v_ref, fr_ref, o_ref, lse_ref, do_ref, dq_ref, dk_ref, dv_ref, dfr_ref):
        ki = pl.program_id(1)

        @pl.when(ki == 0)
        def _():
            dq_ref[...] = jnp.zeros_like(dq_ref)

        keep_q, keep_k = _head_masks(tq), _head_masks(tk)
        k_t = k_ref[...]
        kt = k_t.astype(BF16)
        vt = v_ref[...].astype(BF16)
        kh = [jnp.where(keep_k[h], k_t, 0.0).astype(BF16) for h in range(2)]
        fk = [fr_ref[0, pl.ds(h, 1), :] for h in range(2)]

        def tile(qi, carry, diag):
            dk, dv, df0, df1 = carry
            rows_q = pl.ds(pl.multiple_of(qi * tq, tq), tq)
            q, d_o, lse_t = q_ref[rows_q, :] * SCALE, do_ref[rows_q, :], lse_ref[rows_q, :]
            delta_x = _pair_sum(d_o.astype(BF16).astype(F32) * o_ref[rows_q, :])
            res = [None, None]

            def chain(h):
                qh = jnp.where(keep_q[h], q, 0.0).astype(BF16)
                doh = jnp.where(keep_q[h], d_o, 0.0).astype(BF16)
                lse_h, delta_h = _head_col(lse_t, keep_q[0], h), _head_col(delta_x, keep_q[0], h)
                yield
                s, dp = _att_scores(qh, kt, fk[h], diag), _dot(doh, vt, _CONTRACT["nt"])
                yield
                p = jnp.exp(s - lse_h)
                ds = p * (dp - delta_h)
                yield
                res[h] = (_dot(p, doh, _CONTRACT["tn"]), _dot(ds, qh, _CONTRACT["tn"]),
                          _dot(ds, kh[h], _CONTRACT["nn"]), jnp.sum(ds, axis=0, keepdims=True))

            _interleave([chain(0), chain(1)])
            (dv0, dk0, dq0, s0), (dv1, dk1, dq1, s1) = res
            dq_ref[rows_q, :] += (dq0 + dq1) * SCALE
            return dk + dk0 + dk1, dv + dv0 + dv1, df0 - s0, df1 - s1

        zero_kv = jnp.zeros((tk, LANES), F32)
        zero_f = jnp.zeros((1, tk), F32)
        carry = tile(ki, (zero_kv, zero_kv, zero_f, zero_f), True)
        dk, dv, df0, df1 = lax.fori_loop(ki + 1, nq, lambda qi, c: tile(qi, c, False), carry)
        dk_ref[...] = dk
        dv_ref[...] = dv.astype(dv_ref.dtype)
        dfr_ref[0, pl.ds(0, 1), :] = df0
        dfr_ref[0, pl.ds(1, 1), :] = df1

    whole = lambda off: pl.BlockSpec((rows, LANES), lambda p, i, off=off: (0, off + p))
    kblk = lambda off: pl.BlockSpec((tk, LANES), lambda p, i, off=off: (i, off + p))
    fr_spec = pl.BlockSpec((1, 2, tk), lambda p, i: (p, 0, i))
    wide = jax.ShapeDtypeStruct((rows, WIDTH), F32)
    return pl.pallas_call(
        body, name="fox_bwd", grid=(PAIRS, nq),
        in_specs=[whole(0), kblk(PAIRS), kblk(v_off), fr_spec, whole(0), whole(0), whole(0)],
        out_specs=[whole(0), kblk(0), kblk(0), fr_spec],
        out_shape=[wide, wide, jax.ShapeDtypeStruct((rows, WIDTH), BF16),
                   jax.ShapeDtypeStruct((PAIRS, 2, rows), F32)],
        compiler_params=_params(("parallel", "arbitrary")),
    )(fqk, fqk, proj, frow, ao, lse, dao)


def _lane_ids(shape):
    return lax.broadcasted_iota(jnp.int32, shape, len(shape) - 1)


def _gates_elem(a_log, dt_bias, f_bias, pre):
    lane = _lane_ids(pre.shape)
    beta = jax.nn.sigmoid(pre)
    g = -jnp.exp(a_log) * _softplus(pre + dt_bias)
    lf = -_softplus(-(pre + f_bias))
    return jnp.where(lane < 8, beta, jnp.where(lane < 16, g, jnp.where(lane < 24, lf, 0.0)))


def _tri_consts():
    r = np.arange(LANES)[:, None]
    c = np.arange(LANES)[None, :]
    full = (c <= r).astype(np.float32)
    chunked = full * ((r // CHUNK) == (c // CHUNK))
    return jnp.asarray(chunked), jnp.asarray(full)


def _cums_fwd(lc, lf, gates):
    rows = gates.shape[0]
    lane = _lane_ids((LANES, LANES))
    carry = jnp.zeros((1, LANES), F32)
    out = []
    for r in range(rows // LANES):
        blk = gates[r * LANES:(r + 1) * LANES]
        gc = _dot32(lc, blk, _CONTRACT["nn"])
        f = _dot32(lf, blk, _CONTRACT["nn"]) + carry
        carry = carry + jnp.sum(blk, axis=0, keepdims=True)
        out.append(jnp.where((lane >= 8) & (lane < 16), gc, jnp.where((lane >= 16) & (lane < 24), f, 0.0)))
    return jnp.concatenate(out, axis=0)


def _cums_bwd(lc, lf, dcums):
    rows = dcums.shape[0]
    lane = _lane_ids((LANES, LANES))
    is_g = (lane >= 8) & (lane < 16)
    is_f = (lane >= 16) & (lane < 24)
    carry = jnp.zeros((1, LANES), F32)
    out = [None] * (rows // LANES)
    for r in reversed(range(rows // LANES)):
        blk = dcums[r * LANES:(r + 1) * LANES]
        dg = jnp.where(is_g, blk, 0.0)
        df = jnp.where(is_f, blk, 0.0)
        out[r] = _dot32(lc, dg, _CONTRACT["tn"]) + _dot32(lf, df, _CONTRACT["tn"]) + carry
        carry = carry + jnp.sum(df, axis=0, keepdims=True)
    return jnp.concatenate(out, axis=0)


def _expand_consts():
    xb = np.zeros((LANES, WIDTH), np.float32)
    xg = np.zeros((LANES, WIDTH), np.float32)
    for h in range(HEADS):
        xb[h, h * HEAD_DIM:(h + 1) * HEAD_DIM] = 1.0
        xg[8 + h, h * HEAD_DIM:(h + 1) * HEAD_DIM] = 1.0
    return jnp.asarray(xb), jnp.asarray(xg)


def _shift_down(x, s):
    if s == 0:
        return x
    row = lax.broadcasted_iota(jnp.int32, x.shape, 0)
    return jnp.where(row >= s, pltpu.roll(x, s, 0), 0.0)


def _shift_up(x, s):
    if s == 0:
        return x
    n = x.shape[0]
    row = lax.broadcasted_iota(jnp.int32, x.shape, 0)
    return jnp.where(row < n - s, pltpu.roll(x, n - s, 0), 0.0)


def _row_of(cw, i):
    row = lax.broadcasted_iota(jnp.int32, cw.shape, 0)
    return jnp.sum(jnp.where(row == i, cw, 0.0), axis=0, keepdims=True)


def _conv(cw, x):
    c = jnp.zeros_like(x)
    for i in range(CONV_K):
        c = c + _row_of(cw, i) * _shift_down(x, CONV_K - 1 - i)
    return c


def _post_conv(is_qk, c):
    s = _silu(c)
    n = s * lax.rsqrt(_pair_sum(s * s) + EPS)
    return jnp.where(is_qk, n, s)


def _gdn_prep_fwd(col, cw, x):
    return (_post_conv(col < 2 * PAIRS, _conv(cw, x)),)


def _gdn_prep_bwd(is_qk, cw, x, dy):
    c = _conv(cw, x)
    _, vjp = jax.vjp(lambda cc: _post_conv(is_qk, cc), c)
    (dc,) = vjp(dy)
    dx = jnp.zeros_like(x)
    row = lax.broadcasted_iota(jnp.int32, cw.shape, 0)
    dcw = jnp.zeros(cw.shape, F32)
    for i in range(CONV_K):
        s = CONV_K - 1 - i
        dx = dx + _row_of(cw, i) * _shift_up(dc, s)
        dcw = dcw + jnp.where(row == i, jnp.sum(dc * _shift_down(x, s), axis=0, keepdims=True), 0.0)
    return dx, dcw


def _head_rms(w, x):
    return x * lax.rsqrt(_pair_sum(x * x) / HEAD_DIM + EPS) * w


def _cat_weights(w_in_t):
    tail = jnp.pad(w_in_t[4112:4120], ((0, D_CAT - D_IN), (0, 0)))
    return jnp.concatenate([w_in_t[:2048], w_in_t[2064:4112], w_in_t[2048:2064], tail], axis=0)


def _uncat_grad(g):
    return jnp.concatenate([g[:2048], g[4096:4112], g[2048:4096], g[4112:4120]], axis=0)


def _lanes_to_rowform(v8, rows):
    return v8.reshape(rows // CHUNK, CHUNK, HEADS).transpose(0, 2, 1).reshape(rows // CHUNK, PAIRS, 1, LANES)


def _rowform_to_lanes(v, rows):
    return v.reshape(rows // CHUNK, HEADS, CHUNK).transpose(0, 2, 1).reshape(rows, HEADS)


def _local_step(x, target, norm1_w, w_cat, conv_w, a_log, dt_bias, out_norm_w, f_bias, q_norm_w, k_norm_w,
                norm2_w, final_w, late_weights, early_grads_ready):
    rows = x.shape[0]
    tm = min(512, rows)
    lc, lf = _tri_consts()
    xb, xg = _expand_consts()

    (h1,) = _tiles(lambda col, w, xx: (_rms(xx, w),), name="norm1", rows=rows, tm=tm,
                   full_consts=[norm1_w], row_ins=[(x, D_MODEL, 0)], row_outs=[(D_MODEL, BF16)])
    proj = _mm(h1, w_cat, dims="nt", name="in_proj", tn=384, tk=1024)

    lane_pad = lambda v, off: jnp.pad(v.reshape(1, -1), ((0, 0), (off, LANES - off - v.size)))
    p_a, p_dt, p_fb = lane_pad(a_log, 8), lane_pad(dt_bias, 8), lane_pad(f_bias, 16)

    def gates_fwd(col, lcv, lfv, a, dt, fb, pre):
        gates = _gates_elem(a, dt, fb, pre)
        return gates, _cums_fwd(lcv, lfv, gates)

    gates, cums = _tiles(gates_fwd, name="gates", rows=rows, tm=rows,
                         full_consts=[lc, lf, p_a, p_dt, p_fb], row_ins=[(proj, LANES, COL_SMALL)],
                         row_outs=[(LANES, F32), (LANES, F32)])

    def expand_fwd(col, b, g, gt, cm):
        return (_dot32(gt, b, _CONTRACT["nn"]), _dot32(cm, g, _CONTRACT["nn"]))

    betax, gcx = _tiles(expand_fwd, name="expand", rows=rows, tm=tm, full_consts=[xb, xg],
                        row_ins=[(gates, LANES, 0), (cums, LANES, 0)],
                        row_outs=[(WIDTH, F32)] * 2)
    grow = _lanes_to_rowform(cums[:, 8:16], rows)
    frow = cums[:, 16:24].T.reshape(PAIRS, 2, rows)

    (qkv,) = _tiles(_gdn_prep_fwd, name="gdn_prep", rows=rows, tm=rows, ncol=3 * PAIRS,
                    col_consts=[(conv_w, CONV_K, LANES, 0)], row_ins=[(proj, LANES, 0)],
                    row_outs=[(LANES, F32)])
    o_gdn, ssave, tsave = _gdn_forward(qkv, betax, gcx, grow, rows)

    w_qk = jnp.concatenate([jnp.tile(q_norm_w.reshape(1, -1), (1, HEADS)),
                            jnp.tile(k_norm_w.reshape(1, -1), (1, HEADS))], axis=1)
    fox_off = 2048 // LANES
    (fqk,) = _tiles(lambda col, w, xx: (_head_rms(w, xx),), name="fox_prep", rows=rows, tm=rows, ncol=2 * PAIRS,
                    col_consts=[(w_qk, 1, LANES, 0)], row_ins=[(proj, LANES, fox_off)],
                    row_outs=[(LANES, F32)])
    ao, lse = _attention_forward(fqk, proj, frow, rows)

    w_on = jnp.tile(out_norm_w.reshape(1, -1), (1, 2))
    z_off, fg_off = 1536 // LANES, 3584 // LANES
    mix_g_fn = lambda w, o, z: _head_rms(w, o) * _silu(z)
    mix_f_fn = lambda a, g: a * jax.nn.sigmoid(g)
    (mix_g,) = _tiles(lambda col, w, o, z: (mix_g_fn(w, o, z),), name="mix_gdn", rows=rows, tm=rows, ncol=PAIRS,
                      full_consts=[w_on], row_ins=[(o_gdn, LANES, 0), (proj, LANES, z_off)],
                      row_outs=[(LANES, BF16)])
    (mix_f,) = _tiles(lambda col, a, g: (mix_f_fn(a, g),), name="mix_fox", rows=rows, tm=rows, ncol=PAIRS,
                      row_ins=[(ao, LANES, 0), (proj, LANES, fg_off)], row_outs=[(LANES, BF16)])
    mix = jnp.concatenate([mix_g, mix_f], axis=1)
    w_out, w_gate, w_up, w_down = late_weights(mix)
    x1 = _mm(mix, w_out, dims="nn", name="out_proj", add=x, tk=1024)

    (h2,) = _tiles(lambda col, w, xx: (_rms(xx, w),), name="norm2", rows=rows, tm=tm,
                   full_consts=[norm2_w], row_ins=[(x1, D_MODEL, 0)], row_outs=[(D_MODEL, BF16)])
    t_rows, t_cols, t_act = min(1024, rows), 512, min(512, rows)
    n_rt = rows // t_rows
    st_act = jax.ShapeDtypeStruct((N_CHIPS, rows, FF_SHARD), BF16)
    st_rows = pl.BlockSpec((None, t_rows, FF_SHARD), lambda i, j: (j, i, 0))
    out_rows = pl.BlockSpec((t_rows, t_cols), lambda i, n: (i, n))
    flat = lambda t: t.reshape(N_CHIPS * rows, FF_SHARD)

    def ffn_in(w_st, name):
        return _mm_blocks(h2, w_st, name=name, grid=(n_rt, N_CHIPS), dims="nt",
                          a_spec=pl.BlockSpec((t_rows, D_MODEL), lambda i, j: (i, 0)),
                          b_spec=pl.BlockSpec((None, FF_SHARD, D_MODEL), lambda i, j: (j, 0, 0)),
                          o_spec=st_rows, out_shape=st_act)

    gate, up = ffn_in(w_gate, "ffn_gate"), ffn_in(w_up, "ffn_up")
    act_fn = lambda g, u: _silu(g.astype(F32)) * u.astype(F32)
    (act,) = _tiles(lambda col, g, u: (act_fn(g, u),), name="ffn_act", rows=N_CHIPS * rows, tm=t_act,
                    row_ins=[(flat(gate), FF_SHARD, 0), (flat(up), FF_SHARD, 0)], row_outs=[(FF_SHARD, BF16)])
    act = act.reshape(st_act.shape)
    x2 = _mm_blocks(act, w_down, name="ffn_down", grid=(n_rt, D_MODEL // t_cols), dims="nn", n_sum=N_CHIPS,
                    a_spec=pl.BlockSpec((N_CHIPS, t_rows, FF_SHARD), lambda i, n: (0, i, 0)),
                    b_spec=pl.BlockSpec((N_CHIPS, FF_SHARD, t_cols), lambda i, n: (0, 0, n)),
                    o_spec=out_rows, out_shape=jax.ShapeDtypeStruct((rows, D_MODEL), F32),
                    add=x1, add_spec=out_rows)

    def final_fn(col, w, xx, tgt):
        y, vjp = jax.vjp(_rms, xx, w)
        err = y - tgt
        loss = 0.5 * jnp.sum(err * err) / D_MODEL
        dx, dw = vjp(err / D_MODEL)
        return dx, dx, jnp.full((1, LANES), loss, F32), dw

    dx2, dx2_b, loss, d_final_w = _tiles(final_fn, name="final_loss", rows=rows, tm=tm, full_consts=[final_w],
                                         row_ins=[(x2, D_MODEL, 0), (target, D_MODEL, 0)],
                                         row_outs=[(D_MODEL, F32), (D_MODEL, BF16)],
                                         acc_outs=[(1, LANES), (1, D_MODEL)])

    dact = _mm_blocks(dx2_b, w_down, name="d_act", grid=(n_rt, N_CHIPS), dims="nt",
                      a_spec=pl.BlockSpec((t_rows, D_MODEL), lambda i, j: (i, 0)),
                      b_spec=pl.BlockSpec((None, FF_SHARD, D_MODEL), lambda i, j: (j, 0, 0)),
                      o_spec=st_rows, out_shape=st_act)
    def g_ffn(d_st, other, name):
        return _mm_blocks(d_st, other, name=name, grid=(N_CHIPS, D_MODEL // t_cols), dims="tn",
                          a_spec=pl.BlockSpec((None, rows, FF_SHARD), lambda j, n: (j, 0, 0)),
                          b_spec=pl.BlockSpec((rows, t_cols), lambda j, n: (0, n)),
                          o_spec=pl.BlockSpec((None, FF_SHARD, t_cols), lambda j, n: (j, 0, n)),
                          out_shape=jax.ShapeDtypeStruct((N_CHIPS, FF_SHARD, D_MODEL), BF16))

    g_down = g_ffn(act, dx2_b, "g_down")

    def act_bwd(col, g, u, d):
        _, vjp = jax.vjp(lambda gg, uu: _silu(gg) * uu, g.astype(F32), u.astype(F32))
        return vjp(d.astype(F32))

    dgate, dup = _tiles(act_bwd, name="ffn_act_bwd", rows=N_CHIPS * rows, tm=t_act,
                        row_ins=[(flat(gate), FF_SHARD, 0), (flat(up), FF_SHARD, 0), (flat(dact), FF_SHARD, 0)],
                        row_outs=[(FF_SHARD, BF16), (FF_SHARD, BF16)])
    dgate, dup = dgate.reshape(st_act.shape), dup.reshape(st_act.shape)

    def d_h2(d_st, w_st, name, add):
        return _mm_blocks(d_st, w_st, name=name, grid=(n_rt, D_MODEL // t_cols), dims="nn", n_sum=N_CHIPS,
                          a_spec=pl.BlockSpec((N_CHIPS, t_rows, FF_SHARD), lambda i, n: (0, i, 0)),
                          b_spec=pl.BlockSpec((N_CHIPS, FF_SHARD, t_cols), lambda i, n: (0, 0, n)),
                          o_spec=out_rows, out_shape=jax.ShapeDtypeStruct((rows, D_MODEL), F32),
                          add=add, add_spec=out_rows)

    dh2 = d_h2(dup, w_up, "d_h2_up", d_h2(dgate, w_gate, "d_h2_gate", None))
    g_gate, g_up = g_ffn(dgate, h2, "g_gate"), g_ffn(dup, h2, "g_up")

    def norm_bwd(col, w, xx, dh, dres):
        _, vjp = jax.vjp(_rms, xx, w)
        dx, dw = vjp(dh)
        return dx + dres, dx + dres, dw

    dx1, dx1_b, d_norm2_w = _tiles(norm_bwd, name="norm2_bwd", rows=rows, tm=tm, full_consts=[norm2_w],
                                   row_ins=[(x1, D_MODEL, 0), (dh2, D_MODEL, 0), (dx2, D_MODEL, 0)],
                                   row_outs=[(D_MODEL, F32), (D_MODEL, BF16)], acc_outs=[(1, D_MODEL)])
    dmix = _mm(dx1_b, w_out, dims="nt", name="d_mix", tk=1024)
    g_out = _mm(mix, dx1_b, dims="tn", name="g_out", tk=rows, out_dtype=BF16)
    w_on = w_on + early_grads_ready(g_out, g_gate, g_up, g_down)

    def mix_g_bwd(col, w, o, z, d):
        _, vjp = jax.vjp(mix_g_fn, w, o, z)
        dw, do_, dz = vjp(d)
        return do_, dz, dw

    do_gdn, dz, d_on = _tiles(mix_g_bwd, name="mix_gdn_bwd", rows=rows, tm=rows, ncol=PAIRS, full_consts=[w_on],
                              row_ins=[(o_gdn, LANES, 0), (proj, LANES, z_off), (dmix, LANES, 0)],
                              row_outs=[(LANES, F32), (LANES, BF16)], acc_outs=[(1, LANES)])

    def mix_f_bwd(col, a, g, d):
        _, vjp = jax.vjp(mix_f_fn, a, g)
        return vjp(d)

    dao, dfgate = _tiles(mix_f_bwd, name="mix_fox_bwd", rows=rows, tm=rows, ncol=PAIRS,
                         row_ins=[(ao, LANES, 0), (proj, LANES, fg_off), (dmix, LANES, PAIRS)],
                         row_outs=[(LANES, F32), (LANES, BF16)])

    dfq, dfk, dfv, dfrow = _attention_backward(fqk, proj, frow, ao, lse, dao, rows)

    def fox_prep_bwd(col, w, xx, d):
        _, vjp = jax.vjp(_head_rms, w, xx)
        dw, dx = vjp(d)
        return dx, dw

    dfqk, d_wqk = [], []
    for part, d_n in enumerate((dfq, dfk)):
        dx_p, dw_p = _tiles(fox_prep_bwd, name="fox_prep_bwd_" + "qk"[part], rows=rows, tm=rows, ncol=PAIRS,
                            col_consts=[(w_qk, 1, LANES, part * PAIRS)],
                            row_ins=[(proj, LANES, fox_off + part * PAIRS), (d_n, LANES, 0)],
                            row_outs=[(LANES, BF16)], acc_outs=[(1, LANES)])
        dfqk.append(dx_p)
        d_wqk.append(dw_p)

    dq, dk, dv, dbetax, dgcx, dgrow = _gdn_backward(qkv, betax, gcx, grow, ssave, tsave, do_gdn, rows)
    dqkv, d_conv = [], []
    for part, d_n in enumerate((dq, dk, dv)):
        prep_bwd = lambda col, cw, xx, dy, is_qk=(part < 2): _gdn_prep_bwd(is_qk, cw, xx, dy)
        dx_p, dw_p = _tiles(prep_bwd, name="gdn_prep_bwd_" + "qkv"[part], rows=rows, tm=rows, ncol=PAIRS,
                            col_consts=[(conv_w, CONV_K, LANES, part * PAIRS)],
                            row_ins=[(proj, LANES, part * PAIRS), (d_n, LANES, 0)],
                            row_outs=[(LANES, BF16)], acc_outs=[(CONV_K, LANES)])
        dqkv.append(dx_p)
        d_conv.append(dw_p)
    d_conv = jnp.concatenate(d_conv, axis=1)

    def expand_bwd(col, b, g, db, dg):
        return (_dot32(db, b, _CONTRACT["nt"]), _dot32(dg, g, _CONTRACT["nt"]))

    dgates_b, dcums_g = _tiles(expand_bwd, name="expand_bwd", rows=rows, tm=tm, full_consts=[xb, xg],
                               row_ins=[(dbetax, WIDTH, 0), (dgcx, WIDTH, 0)],
                               row_outs=[(LANES, F32), (LANES, F32)])
    dcums_row = jnp.concatenate([jnp.zeros((rows, 8), F32), _rowform_to_lanes(dgrow, rows),
                                 dfrow.reshape(HEADS, rows).T, jnp.zeros((rows, LANES - 24), F32)], axis=1)

    def gates_bwd(col, lcv, lfv, a, dt, fb, pre, dgb, dcg, dcr):
        lane = _lane_ids(pre.shape)
        dgates = jnp.where(lane < 8, dgb, _cums_bwd(lcv, lfv, dcg + dcr))
        _, vjp = jax.vjp(_gates_elem, a, dt, fb, pre)
        da, ddt, dfb, dpre = vjp(dgates)
        return dpre, da, ddt, dfb

    dpre, d_a, d_dt, d_fb = _tiles(gates_bwd, name="gates_bwd", rows=rows, tm=rows,
                                   full_consts=[lc, lf, p_a, p_dt, p_fb],
                                   row_ins=[(proj, LANES, COL_SMALL), (dgates_b, LANES, 0), (dcums_g, LANES, 0),
                                            (dcums_row, LANES, 0)],
                                   row_outs=[(LANES, BF16)], acc_outs=[(1, LANES)] * 3)

    dproj = jnp.concatenate(dqkv + [dz] + dfqk + [dfv, dfgate, dpre], axis=1)
    dh1 = _mm(dproj, w_cat, dims="nn", name="d_h1", tk=D_CAT)
    g_cat = _mm(dproj, h1, dims="tn", name="g_in", tm=384, tn=D_MODEL, tk=rows)

    def norm1_bwd(col, w, xx, dh, dres):
        _, vjp = jax.vjp(_rms, xx, w)
        dx, dw = vjp(dh)
        return dx + dres, dw

    grad_x, d_norm1_w = _tiles(norm1_bwd, name="norm1_bwd", rows=rows, tm=tm, full_consts=[norm1_w],
                               row_ins=[(x, D_MODEL, 0), (dh1, D_MODEL, 0), (dx1, D_MODEL, 0)],
                               row_outs=[(D_MODEL, F32)], acc_outs=[(1, D_MODEL)])

    fold = lambda v: v.reshape(-1, HEAD_DIM).sum(axis=0)
    small = dict(
        loss=loss[0, 0],
        norm1_w=d_norm1_w, conv_w=d_conv, a_log=d_a[0, 8:16], dt_bias=d_dt[0, 8:16],
        out_norm_w=fold(d_on), f_bias=d_fb[0, 16:24], q_norm_w=fold(d_wqk[0]),
        k_norm_w=fold(d_wqk[1]), norm2_w=d_norm2_w, final_w=d_final_w)
    return grad_x, g_cat, g_out, g_gate, g_up, g_down, small


HBM_SPEC = pl.BlockSpec(memory_space=pltpu.HBM)


def _place():
    x, y, c = lax.axis_index("x"), lax.axis_index("y"), lax.axis_index("c")
    chips = [(1 - x, y), (x, 1 - y), (1 - x, 1 - y)]
    return x, y, c, 2 * x + y, (x, y, 1 - c), chips, [2 * cx + cy for cx, cy in chips]


def _remote(src, dst, send_sem, recv_sem, to):
    return pltpu.make_async_remote_copy(src_ref=src, dst_ref=dst, send_sem=send_sem, recv_sem=recv_sem,
                                        device_id=to, device_id_type=MESH)


def _allgather_weights(shards, conv):
    n = len(shards)
    halves = [s.shape[1] // 2 for s in shards]
    per = 6
    own_base = n * per + 3

    def body(*refs):
        ins, conv_in = refs[:n], refs[n]
        outs, conv_out = refs[n + 1:2 * n + 1], refs[2 * n + 1]
        send_sems, recv_sems = refs[2 * n + 2:]
        x, y, c, own, sib, chips, chip_idx = _place()

        def half(i, ref, hc):
            return ref.at[:, pl.ds(pl.multiple_of(hc * halves[i], LANES), halves[i])]

        sent = []
        for i, (src, dst) in enumerate(zip(list(ins) + [conv_in], list(outs) + [conv_out])):
            k = own_base + i
            sent.append(_remote(src, dst.at[own], send_sems.at[k], recv_sems.at[k], sib))
        for i in range(n):
            for j, chip in enumerate(chips):
                k = i * per + j
                sent.append(_remote(half(i, ins[i], c), half(i, outs[i].at[own], c),
                                    send_sems.at[k], recv_sems.at[k], (*chip, c)))
        for j, chip in enumerate(chips):
            k = n * per + j
            sent.append(_remote(conv_in, conv_out.at[own], send_sems.at[k], recv_sems.at[k], (*chip, c)))
        for cp in sent:
            cp.start()
        for i in range(n):
            for j in range(len(chips)):
                k = i * per + j
                landed = half(i, outs[i].at[chip_idx[j]], c)
                _remote(landed, landed, send_sems.at[k], recv_sems.at[k], sib).wait_recv()
                fwd = _remote(landed, landed, send_sems.at[k + 3], recv_sems.at[k + 3], sib)
                fwd.start()
                sent.append(fwd)
        for i in range(n):
            for j in range(len(chips)):
                k = i * per + 3 + j
                landed = half(i, outs[i].at[chip_idx[j]], 1 - c)
                _remote(landed, landed, send_sems.at[k], recv_sems.at[k], sib).wait_recv()
        for j in range(len(chips)):
            k = n * per + j
            landed = conv_out.at[chip_idx[j]]
            _remote(landed, landed, send_sems.at[k], recv_sems.at[k], sib).wait_recv()
        for i, dst in enumerate(list(outs) + [conv_out]):
            k = own_base + i
            landed = dst.at[own]
            _remote(landed, landed, send_sems.at[k], recv_sems.at[k], sib).wait_recv()
        for cp in sent:
            cp.wait_send()

    n_sem = own_base + n + 1
    out_shape = [jax.ShapeDtypeStruct((N_CHIPS,) + s.shape, s.dtype) for s in shards]
    out_shape.append(jax.ShapeDtypeStruct((N_CHIPS,) + conv.shape, conv.dtype))
    res = pl.pallas_call(
        body, name="allgather_weights", out_shape=out_shape,
        in_specs=[HBM_SPEC] * (n + 1), out_specs=[HBM_SPEC] * (n + 1),
        scratch_shapes=[pltpu.SemaphoreType.DMA((n_sem,)), pltpu.SemaphoreType.DMA((n_sem,))],
    )(*shards, conv)
    return res[:n], res[n]


SEM_SPEC = pl.BlockSpec(memory_space=pltpu.SEMAPHORE)
ANY_SPEC = pl.BlockSpec(memory_space=pl.ANY)
DATAFLOW = pltpu.SideEffectType.DATAFLOW_SIDE_EFFECTING


def _gather_plan(srcs, lands):
    x, y, c, own, sib, chips, chip_idx = _place()
    plan = []
    for src, land in zip(srcs, lands):
        for j, chip in enumerate(chips):
            plan.append((src, land.at[own], (*chip, c), land.at[chip_idx[j]]))
        plan.append((src, land.at[own], sib, land.at[own]))
    return plan


def _exchange_plan(srcs, lands):
    x, y, c, own, sib, chips, chip_idx = _place()
    plan = []
    for src, land in zip(srcs, lands):
        for j, chip in enumerate(chips):
            plan.append((src.at[chip_idx[j]], land.at[j], (*chip, c), land.at[j]))
    return plan


def _split_start(name, plan_fn, srcs, land_shapes, n_copies, after):
    n = len(srcs)

    def body(*refs):
        src_refs, land_refs = refs[:n], refs[n:2 * n]
        send_sems, recv_sems = refs[2 * n + 1], refs[2 * n + 2]
        token = refs[-1]
        for k, (src, dst, to, _) in enumerate(plan_fn(src_refs, land_refs)):
            _remote(src, dst, send_sems.at[k], recv_sems.at[k], to).start()
        token[...] = jnp.zeros_like(token)

    lands = [pltpu.with_memory_space_constraint(lax.empty(s.shape, s.dtype), pltpu.HBM) for s in land_shapes]
    srcs = [pltpu.with_memory_space_constraint(s, pltpu.HBM) for s in srcs]
    out_shape = ([pltpu.SemaphoreType.DMA((n_copies,)), pltpu.SemaphoreType.DMA((n_copies,))]
                 + [pltpu.HBM(s.shape, s.dtype) for s in srcs] + [pltpu.HBM(s.shape, s.dtype) for s in land_shapes]
                 + [jax.ShapeDtypeStruct((8, LANES), F32)])
    res = pl.pallas_call(
        body, name=name, out_shape=out_shape,
        in_specs=[HBM_SPEC] * (2 * n) + [ANY_SPEC],
        out_specs=[SEM_SPEC, SEM_SPEC] + [HBM_SPEC] * (2 * n) + [pl.BlockSpec(memory_space=pltpu.VMEM)],
        input_output_aliases={i: 2 + i for i in range(2 * n)},
        compiler_params=pltpu.CompilerParams(has_side_effects=DATAFLOW),
    )(*srcs, *lands, after)
    return dict(sems=res[:2], srcs=res[2:2 + n], lands=res[2 + n:2 + 2 * n], token=res[-1], n=n)


def _split_wait(name, plan_fn, started, after):
    n = started["n"]

    def body(*refs):
        src_refs, land_refs = refs[:n], refs[n:2 * n]
        send_sems, recv_sems = refs[2 * n], refs[2 * n + 1]
        for k, (src, _, to, landed) in enumerate(plan_fn(src_refs, land_refs)):
            copy = _remote(src, landed, send_sems.at[k], recv_sems.at[k], to)
            copy.wait_send()
            copy.wait_recv()

    srcs, lands = started["srcs"], started["lands"]
    res = pl.pallas_call(
        body, name=name,
        out_shape=[pltpu.HBM(s.shape, s.dtype) for s in srcs] + [pltpu.HBM(s.shape, s.dtype) for s in lands],
        in_specs=[HBM_SPEC] * (2 * n) + [SEM_SPEC, SEM_SPEC, ANY_SPEC],
        out_specs=[HBM_SPEC] * (2 * n),
        input_output_aliases={i: i for i in range(2 * n)},
        compiler_params=pltpu.CompilerParams(has_side_effects=DATAFLOW),
    )(*srcs, *lands, *started["sems"], after)
    return res[n:]


def _swap_halves(stacks, name):
    n = len(stacks)

    def body(*refs):
        ins, outs = refs[:n], refs[n:2 * n]
        send_sems, recv_sems = refs[2 * n:]
        x, y, c, own, sib, chips, chip_idx = _place()
        cps = []
        for i in range(n):
            h = stacks[i].shape[2] // 2
            src = ins[i].at[:, :, pl.ds(pl.multiple_of((1 - c) * h, LANES), h)]
            cps.append(_remote(src, outs[i], send_sems.at[i], recv_sems.at[i], sib))
        for cp in cps:
            cp.start()
        for cp in cps:
            cp.wait()

    out_shape = [jax.ShapeDtypeStruct((N_CHIPS, s.shape[1], s.shape[2] // 2), s.dtype) for s in stacks]
    return pl.pallas_call(
        body, name=name, out_shape=out_shape,
        in_specs=[HBM_SPEC] * n, out_specs=[HBM_SPEC] * n,
        scratch_shapes=[pltpu.SemaphoreType.DMA((n,)), pltpu.SemaphoreType.DMA((n,))],
    )(*stacks)


def _add_half(stack, landed, place, name):
    _, rows, h = landed.shape

    def body(place_ref, a_ref, b_ref, o_ref, own_ref):
        part = (a_ref[...].astype(F32) + b_ref[...].astype(F32)).astype(o_ref.dtype)
        o_ref[...] = part

        @pl.when(pl.program_id(0) == place_ref[1])
        def _():
            own_ref[...] = part[0]

    return pl.pallas_call(
        body, name=name,
        out_shape=[jax.ShapeDtypeStruct(landed.shape, BF16), jax.ShapeDtypeStruct((rows, h), BF16)],
        grid_spec=pltpu.PrefetchScalarGridSpec(
            num_scalar_prefetch=1, grid=(N_CHIPS,),
            in_specs=[pl.BlockSpec((1, rows, h), lambda j, p: (j, 0, p[0])),
                      pl.BlockSpec((1, rows, h), lambda j, p: (j, 0, 0))],
            out_specs=[pl.BlockSpec((1, rows, h), lambda j, p: (j, 0, 0)),
                       pl.BlockSpec((rows, h), lambda j, p: (0, 0))]),
        compiler_params=_params(("arbitrary",)),
    )(place, stack, landed)


def _exchange_partials(parts):
    n = len(parts)

    def body(*refs):
        ins, outs = refs[:n], refs[n:2 * n]
        send_sems, recv_sems = refs[2 * n:]
        x, y, c, own, sib, chips, chip_idx = _place()
        sent = []
        for i in range(n):
            for j, chip in enumerate(chips):
                k = i * 3 + j
                sent.append(_remote(ins[i].at[chip_idx[j]], outs[i].at[j], send_sems.at[k], recv_sems.at[k],
                                    (*chip, c)))
        for cp in sent:
            cp.start()
        for i in range(n):
            for j in range(len(chips)):
                k = i * 3 + j
                landed = outs[i].at[j]
                _remote(landed, landed, send_sems.at[k], recv_sems.at[k], sib).wait_recv()
        for cp in sent:
            cp.wait_send()

    return pl.pallas_call(
        body, name="rs_exchange_partials",
        out_shape=[jax.ShapeDtypeStruct((3,) + p.shape[1:], p.dtype) for p in parts],
        in_specs=[HBM_SPEC] * n, out_specs=[HBM_SPEC] * n,
        scratch_shapes=[pltpu.SemaphoreType.DMA((3 * n,)), pltpu.SemaphoreType.DMA((3 * n,))],
    )(*parts)


def _sum_partials(own_part, landed, name):
    _, h, cols = landed.shape

    def body(own_ref, a_ref, o_ref):
        acc = own_ref[...].astype(F32)
        for s in range(3):
            acc = acc + a_ref[s].astype(F32)
        o_ref[...] = acc

    return pl.pallas_call(
        body, name=name, out_shape=jax.ShapeDtypeStruct((h, cols), F32), grid=(1,),
        in_specs=[pl.BlockSpec((h, cols), lambda i: (0, 0)), pl.BlockSpec(landed.shape, lambda i: (0, 0, 0))],
        out_specs=pl.BlockSpec((h, cols), lambda i: (0, 0)),
        compiler_params=_params(("arbitrary",)),
    )(own_part, landed)


def _share_halves(halves, name):
    n = len(halves)

    def body(*refs):
        ins, outs = refs[:n], refs[n:2 * n]
        send_sems, recv_sems = refs[2 * n:]
        x, y, c, own, sib, chips, chip_idx = _place()
        cps = [_remote(ins[i], outs[i], send_sems.at[i], recv_sems.at[i], sib) for i in range(n)]
        for cp in cps:
            cp.start()
        for cp in cps:
            cp.wait()

    return pl.pallas_call(
        body, name=name,
        out_shape=[jax.ShapeDtypeStruct(p.shape, p.dtype) for p in halves],
        in_specs=[HBM_SPEC] * n, out_specs=[HBM_SPEC] * n,
        scratch_shapes=[pltpu.SemaphoreType.DMA((n,)), pltpu.SemaphoreType.DMA((n,))],
    )(*halves)


def _allreduce_small(packed):
    rows = packed.shape[0]
    n_dev = 8

    def body(in_ref, out_ref, gath, send_sems, recv_sems):
        x, y, c = lax.axis_index("x"), lax.axis_index("y"), lax.axis_index("c")
        me = 4 * x + 2 * y + c
        gath[me] = in_ref[...]
        cps = []
        for k in range(1, n_dev):
            fx, fy, fc = (k >> 2) & 1, (k >> 1) & 1, k & 1
            to = (x ^ fx, y ^ fy, c ^ fc)
            cps.append(_remote(in_ref, gath.at[me], send_sems.at[k - 1], recv_sems.at[k - 1], to))
        for cp in cps:
            cp.start()
        for k in range(1, n_dev):
            fx, fy, fc = (k >> 2) & 1, (k >> 1) & 1, k & 1
            src = 4 * (x ^ fx) + 2 * (y ^ fy) + (c ^ fc)
            slot = gath.at[src]
            _remote(slot, slot, send_sems.at[k - 1], recv_sems.at[k - 1], (x, y, c)).wait_recv()
        for cp in cps:
            cp.wait_send()
        acc = gath[0]
        for d in range(1, n_dev):
            acc = acc + gath[d]
        out_ref[...] = acc

    vm = pl.BlockSpec(memory_space=pltpu.VMEM)
    return pl.pallas_call(
        body, name="allreduce_small", out_shape=jax.ShapeDtypeStruct(packed.shape, F32),
        in_specs=[vm], out_specs=vm,
        scratch_shapes=[pltpu.VMEM((n_dev, rows, LANES), F32),
                        pltpu.SemaphoreType.DMA((n_dev - 1,)), pltpu.SemaphoreType.DMA((n_dev - 1,))],
    )(packed)


def _adam(col, w, g, m, v):
    m2 = ADAM_B1 * m + (1.0 - ADAM_B1) * g
    v2 = ADAM_B2 * v + (1.0 - ADAM_B2) * (g * g)
    m_hat = m2 / (1.0 - ADAM_B1 ** ADAM_STEP)
    v_hat = v2 / (1.0 - ADAM_B2 ** ADAM_STEP)
    delta = -ADAM_LR * (m_hat / (jnp.sqrt(v_hat) + ADAM_EPS) + ADAM_WD * w)
    return delta, m2, v2


def _adam_call(w, g, m, v, name):
    rows, cols = w.shape
    tm = rows
    for cand in (256, 352, 176, 128, 64, 48, 16, 8):
        if rows % cand == 0:
            tm = cand
            break
    return _tiles(_adam, name=name, rows=rows, tm=tm,
                  row_ins=[(w, cols, 0), (g, cols, 0), (m, cols, 0), (v, cols, 0)],
                  row_outs=[(cols, F32)] * 3)


def _adam_big(w, g_mine, g_other, m, v, place, name):
    rows, cols = w.shape
    tc = 256
    nt = cols // 2 // tc

    def body(place_ref, w_ref, gm_ref, go_ref, m_ref, v_ref, g_out, d_out, m_out, v_out):
        g = jnp.where(pl.program_id(0) == place_ref[0], gm_ref[...], go_ref[...])
        d, m2, v2 = _adam(None, w_ref[...], g, m_ref[...], v_ref[...])
        g_out[...] = g
        d_out[...] = d
        m_out[...] = m2
        v_out[...] = v2

    full = pl.BlockSpec((rows, tc), lambda hh, i, p: (0, hh * nt + i))
    half = pl.BlockSpec((rows, tc), lambda hh, i, p: (0, i))
    return pl.pallas_call(
        body, name=name, out_shape=[jax.ShapeDtypeStruct(w.shape, F32)] * 4,
        grid_spec=pltpu.PrefetchScalarGridSpec(
            num_scalar_prefetch=1, grid=(2, nt),
            in_specs=[full, half, half, full, full], out_specs=[full] * 4),
        compiler_params=_params(("arbitrary", "arbitrary")),
    )(place, w, g_mine, g_other, m, v)


def _adam_untiled_rows(w, g_mine, g_other, m, v, place, name):
    rows, _, cols = w.shape
    tc = 256
    nt = cols // 2 // tc
    rb = next(r for r in (206, 128, 103, rows) if rows % r == 0)

    def body(place_ref, w_ref, gm_ref, go_ref, m_ref, v_ref, g_out, d_out, m_out, v_out):
        g = jnp.where(pl.program_id(0) == place_ref[0], gm_ref[...], go_ref[...])
        d, m2, v2 = _adam(None, w_ref[...], g, m_ref[...], v_ref[...])
        g_out[...] = g
        d_out[...] = d
        m_out[...] = m2
        v_out[...] = v2

    full = pl.BlockSpec((rb, 1, tc), lambda hh, i, r, p: (r, 0, hh * nt + i))
    half = pl.BlockSpec((rb, 1, tc), lambda hh, i, r, p: (r, 0, i))
    return pl.pallas_call(
        body, name=name, out_shape=[jax.ShapeDtypeStruct(w.shape, F32)] * 4,
        grid_spec=pltpu.PrefetchScalarGridSpec(
            num_scalar_prefetch=1, grid=(2, nt, rows // rb),
            in_specs=[full, half, half, full, full], out_specs=[full] * 4),
        compiler_params=_params(("arbitrary", "arbitrary", "arbitrary")),
    )(place, w, g_mine, g_other, m, v)


def _pack(arrays):
    flat = []
    for a in arrays:
        a = a.reshape(-1).astype(F32)
        flat.append(jnp.pad(a, (0, (-a.size) % LANES)))
    out = jnp.concatenate(flat)
    out = jnp.pad(out, (0, (-out.size) % (8 * LANES)))
    return out.reshape(-1, LANES)


def _unpack(packed, shapes):
    flat = packed.reshape(-1)
    out, off = [], 0
    for s in shapes:
        size = int(np.prod(s))
        out.append(flat[off:off + size].reshape(s))
        off += size + (-size) % LANES
    return out


def kernel(x, norm1_w, w_in, gdn_conv_w, gdn_A_log, gdn_dt_bias, gdn_out_norm_w, fox_f_bias, fox_q_norm_w, fox_k_norm_w, w_out, norm2_w, w_ffn_gate, w_ffn_up, w_ffn_down, final_norm_w, loss_target, m_norm1_w, m_w_in, m_gdn_conv_w, m_gdn_A_log, m_gdn_dt_bias, m_gdn_out_norm_w, m_fox_f_bias, m_fox_q_norm_w, m_fox_k_norm_w, m_w_out, m_norm2_w, m_w_ffn_gate, m_w_ffn_up, m_w_ffn_down, m_final_norm_w, v_norm1_w, v_w_in, v_gdn_conv_w, v_gdn_A_log, v_gdn_dt_bias, v_gdn_out_norm_w, v_fox_f_bias, v_fox_q_norm_w, v_fox_k_norm_w, v_w_out, v_norm2_w, v_w_ffn_gate, v_w_ffn_up, v_w_ffn_down, v_final_norm_w):
    cx, cy, cc = lax.axis_index("x"), lax.axis_index("y"), lax.axis_index("c")
    own = 2 * cx + cy
    place = jnp.stack([cc, own]).astype(jnp.int32)

    names = ["w_in", "w_out", "w_gate", "w_up", "w_down"]
    is_t = [True, False, True, True, False]
    to_t = lambda a, t: a[0].T if t else a[0]
    from_t = lambda a, t: (a.T if t else a)[None]
    big_w = [to_t(a, t) for a, t in zip([w_in, w_out, w_ffn_gate, w_ffn_up, w_ffn_down], is_t)]
    big_m = [to_t(a, t) for a, t in zip([m_w_in, m_w_out, m_w_ffn_gate, m_w_ffn_up, m_w_ffn_down], is_t)]
    big_v = [to_t(a, t) for a, t in zip([v_w_in, v_w_out, v_w_ffn_gate, v_w_ffn_up, v_w_ffn_down], is_t)]
    shards = [w.astype(BF16) for w in big_w]
    (w_in_g,), conv_g = _allgather_weights(shards[:1], gdn_conv_w[0])
    rest = _split_start("gather_rest_start", _gather_plan, shards[1:],
                        [jax.ShapeDtypeStruct((N_CHIPS,) + s.shape, BF16) for s in shards[1:]],
                        n_copies=4 * len(shards[1:]), after=w_in_g)
    w_cat = _cat_weights(w_in_g.reshape(D_IN, D_MODEL))
    conv_full = conv_g.transpose(1, 0, 2).reshape(CONV_K, 3 * WIDTH)

    def late_weights(after):
        w_out_g, w_gate_g, w_up_g, w_down_g = _split_wait("gather_rest_wait", _gather_plan, rest, after)
        return w_out_g.reshape(D_MODEL, D_MODEL), w_gate_g, w_up_g, w_down_g

    def start_reduction(stacks, nms, tag):
        landed = _swap_halves(stacks, "rs_swap_" + tag)
        added = [_add_half(s, l, place, "rs_add_" + nm) for s, l, nm in zip(stacks, landed, nms)]
        parts = [a[0] for a in added]
        started = _split_start("exchange_" + tag + "_start", _exchange_plan, parts,
                               [jax.ShapeDtypeStruct((3,) + p.shape[1:], p.dtype) for p in parts],
                               n_copies=3 * len(parts), after=parts[0])
        return dict(own=[a[1] for a in added], started=started, tag=tag, names=nms)

    def finish_reduction(red, after, updates):
        landed = _split_wait("exchange_" + red["tag"] + "_wait", _exchange_plan, red["started"], after)
        halves = [_sum_partials(o, p, "rs_sum_" + nm) for o, p, nm in zip(red["own"], landed, red["names"])]
        others = _share_halves(halves, "rs_share_" + red["tag"])
        return [upd(gm, go) for upd, gm, go in zip(updates, halves, others)]

    def transport_update(b):
        def upd(gm, go):
            res = _adam_big(big_w[b], gm, go, big_m[b], big_v[b], place, "adam_" + names[b])
            return [from_t(a, is_t[b]) for a in res]
        return upd

    def w_in_update(gm, go):
        rows3 = lambda a: jnp.transpose(a, (2, 0, 1))
        res = _adam_untiled_rows(rows3(w_in), gm[:, None, :], go[:, None, :], rows3(m_w_in), rows3(v_w_in),
                                 place, "adam_w_in")
        return [jnp.transpose(a, (1, 2, 0)) for a in res]

    early = {}

    def early_grads_ready(g_out, g_gate, g_up, g_down):
        stacks = [g_out.reshape(N_CHIPS, D_MODEL // N_CHIPS, D_MODEL), g_gate, g_up, g_down]
        early.update(start_reduction(stacks, names[1:], "early"))
        return early["started"]["token"][0, 0]

    grad_x, g_cat, _, _, _, _, small = _local_step(
        x[0], loss_target[0], norm1_w + rest["token"][0, 0], w_cat, conv_full, gdn_A_log[0], gdn_dt_bias[0],
        gdn_out_norm_w[0], fox_f_bias[0], fox_q_norm_w[0], fox_k_norm_w[0], norm2_w, final_norm_w.reshape(1, -1),
        late_weights, early_grads_ready)

    late = start_reduction([_uncat_grad(g_cat).reshape(N_CHIPS, D_IN // N_CHIPS, D_MODEL)], names[:1], "w_in")
    big_upd = finish_reduction(early, late["started"]["token"], [transport_update(b) for b in range(1, 5)])

    order = ["norm1_w", "conv_w", "a_log", "dt_bias", "out_norm_w", "f_bias", "q_norm_w", "k_norm_w",
             "norm2_w", "final_w"]
    red = _allreduce_small(_pack([small[k] for k in order] + [small["loss"]]))
    red_shapes = [(1, D_MODEL), (CONV_K, 3 * WIDTH), (1, HEADS), (1, HEADS), (1, HEAD_DIM), (1, HEADS),
                  (1, HEAD_DIM), (1, HEAD_DIM), (1, D_MODEL), (D_MODEL,), ()]
    red_list = _unpack(red, red_shapes)
    loss = red_list[-1]
    small_g = dict(zip(order, red_list[:-1]))
    shard_cols = 3 * WIDTH // N_CHIPS
    small_g["conv_w"] = lax.dynamic_slice_in_dim(small_g["conv_w"], own * shard_cols, shard_cols, axis=1)[None]
    small_w = [norm1_w, gdn_conv_w, gdn_A_log, gdn_dt_bias, gdn_out_norm_w, fox_f_bias, fox_q_norm_w,
               fox_k_norm_w, norm2_w, final_norm_w]
    small_m = [m_norm1_w, m_gdn_conv_w, m_gdn_A_log, m_gdn_dt_bias, m_gdn_out_norm_w, m_fox_f_bias,
               m_fox_q_norm_w, m_fox_k_norm_w, m_norm2_w, m_final_norm_w]
    small_v = [v_norm1_w, v_gdn_conv_w, v_gdn_A_log, v_gdn_dt_bias, v_gdn_out_norm_w, v_fox_f_bias,
               v_fox_q_norm_w, v_fox_k_norm_w, v_norm2_w, v_final_norm_w]
    small_gl = [small_g[k].reshape(w.shape) for k, w in zip(order, small_w)]
    s_delta, s_m, s_v = _adam_call(_pack(small_w), _pack(small_gl), _pack(small_m), _pack(small_v), "adam_small")
    big_upd = finish_reduction(late, s_delta, [w_in_update]) + big_upd
    shapes = [w.shape for w in small_w]
    s_delta, s_m, s_v = _unpack(s_delta, shapes), _unpack(s_m, shapes), _unpack(s_v, shapes)

    big_pos = {1: 0, 9: 1, 11: 2, 12: 3, 13: 4}
    small_pos = {0: 0, 2: 1, 3: 2, 4: 3, 5: 4, 6: 5, 7: 6, 8: 7, 10: 8, 14: 9}
    grads, deltas, new_m, new_v = [], [], [], []
    for pos in range(15):
        if pos in big_pos:
            b = big_pos[pos]
            g, d, m2, v2 = big_upd[b]
            grads.append(g)
            deltas.append(d)
            new_m.append(m2)
            new_v.append(v2)
        else:
            s = small_pos[pos]
            grads.append(small_gl[s])
            deltas.append(s_delta[s])
            new_m.append(s_m[s])
            new_v.append(s_v[s])
    return (loss, grad_x[None], *grads, *deltas, *new_m, *new_v)
```

```python
import jax
import jax.numpy as jnp
import numpy as np
from jax import lax
from jax.experimental import pallas as pl
from jax.experimental.pallas import tpu as pltpu

F32 = jnp.float32
BF16 = jnp.bfloat16

D_MODEL = 1024
HEADS = 8
HEAD_DIM = 64
PAIRS = HEADS // 2
WIDTH = HEADS * HEAD_DIM
CHUNK = 64
CONV_K = 4
D_FF = 2816
FF_SHARD = D_FF // 4
EPS = 1e-6
SCALE = HEAD_DIM ** -0.5
LANES = 128
N_CHIPS = 4
D_IN = 4120
D_CAT = 4224
COL_SMALL = 4096 // LANES

ADAM_LR = 0.001
ADAM_B1 = 0.9
ADAM_B2 = 0.999
ADAM_EPS = 1e-08
ADAM_WD = 0.01
ADAM_STEP = 10

VMEM_LIMIT = 56 * 1024 * 1024
MESH = pl.DeviceIdType.MESH
HIGHEST = lax.Precision.HIGHEST


def _params(sem):
    return pltpu.CompilerParams(dimension_semantics=sem, vmem_limit_bytes=VMEM_LIMIT)


_CONTRACT = {"nn": ((1,), (0,)), "nt": ((1,), (1,)), "tn": ((0,), (0,))}


def _mm(a, b, *, dims, name, out_dtype=F32, add=None, tm=1024, tn=512, tk=512):
    if dims == "nn":
        (m, k), (k2, n) = a.shape, b.shape
    elif dims == "nt":
        (m, k), (n, k2) = a.shape, b.shape
    else:
        (k, m), (k2, n) = a.shape, b.shape
    assert k == k2, (a.shape, b.shape, dims)
    tm, tn, tk = min(tm, m), min(tn, n), min(tk, k)
    assert m % tm == 0 and n % tn == 0 and k % tk == 0, (m, n, k, tm, tn, tk)
    nk = k // tk
    a_spec = (pl.BlockSpec((tk, tm), lambda i, j, kk: (kk, i)) if dims == "tn"
              else pl.BlockSpec((tm, tk), lambda i, j, kk: (i, kk)))
    b_spec = (pl.BlockSpec((tn, tk), lambda i, j, kk: (j, kk)) if dims == "nt"
              else pl.BlockSpec((tk, tn), lambda i, j, kk: (kk, j)))
    o_spec = pl.BlockSpec((tm, tn), lambda i, j, kk: (i, j))
    contract = (_CONTRACT[dims], ((), ()))
    has_add = add is not None

    def body(*refs):
        a_ref, b_ref = refs[:2]
        add_ref = refs[2] if has_add else None
        o_ref = refs[3] if has_add else refs[2]
        part = lax.dot_general(a_ref[...].astype(BF16), b_ref[...].astype(BF16), contract,
                               preferred_element_type=F32)

        def finish(r):
            if has_add:
                r = r + add_ref[...].astype(F32)
            o_ref[...] = r.astype(out_dtype)

        if nk == 1:
            finish(part)
            return
        acc = refs[-1]
        kk = pl.program_id(2)

        @pl.when(kk == 0)
        def _():
            acc[...] = part

        @pl.when(kk > 0)
        def _():
            acc[...] += part

        @pl.when(kk == nk - 1)
        def _():
            finish(acc[...])

    ins = [a, b] + ([add] if has_add else [])
    in_specs = [a_spec, b_spec] + ([o_spec] if has_add else [])
    return pl.pallas_call(
        body, name=name, grid=(m // tm, n // tn, nk),
        in_specs=in_specs, out_specs=o_spec,
        out_shape=jax.ShapeDtypeStruct((m, n), out_dtype),
        scratch_shapes=[pltpu.VMEM((tm, tn), F32)] if nk > 1 else [],
        compiler_params=_params(("parallel", "parallel", "arbitrary")),
    )(*ins)


def _mm_blocks(a, b, *, name, grid, a_spec, b_spec, o_spec, out_shape, dims, n_sum=0, add=None, add_spec=None):
    contract = (_CONTRACT[dims], ((), ()))
    has_add = add is not None

    def body(*refs):
        a_ref, b_ref = refs[:2]
        o_ref = refs[-1]
        dot = lambda x, y: lax.dot_general(x.astype(BF16), y.astype(BF16), contract, preferred_element_type=F32)
        if n_sum:
            r = dot(a_ref[0], b_ref[0])
            for s in range(1, n_sum):
                r = r + dot(a_ref[s], b_ref[s])
        else:
            r = dot(a_ref[...], b_ref[...])
        if has_add:
            r = r + refs[2][...].astype(F32)
        o_ref[...] = r.astype(o_ref.dtype)

    return pl.pallas_call(
        body, name=name, grid=grid,
        in_specs=[a_spec, b_spec] + ([add_spec] if has_add else []), out_specs=o_spec, out_shape=out_shape,
        compiler_params=_params(("parallel",) * len(grid)),
    )(*([a, b] + ([add] if has_add else [])))


def _tiles(fn, *, name, rows, tm, ncol=1, row_ins=(), col_consts=(), full_consts=(),
           row_outs=(), acc_outs=()):
    nt = rows // tm
    assert rows % tm == 0
    n_full, n_col, n_row = len(full_consts), len(col_consts), len(row_ins)
    n_ro, n_acc = len(row_outs), len(acc_outs)

    def body(*refs):
        ins = refs[:n_full + n_col + n_row]
        outs = refs[n_full + n_col + n_row:]
        i = pl.program_id(1)
        res = fn(pl.program_id(0), *[r[...] for r in ins])
        for r, v in zip(outs[:n_ro], res[:n_ro]):
            r[...] = v.astype(r.dtype)
        if n_acc:
            @pl.when(i == 0)
            def _():
                for r in outs[n_ro:]:
                    r[...] = jnp.zeros_like(r)
            for r, v in zip(outs[n_ro:], res[n_ro:]):
                r[...] += v

    in_specs = [pl.BlockSpec(a.shape, lambda j, i, nd=a.ndim: (0,) * nd) for a in full_consts]
    in_specs += [pl.BlockSpec((nr, w), lambda j, i, o=o: (0, o + j)) for (_, nr, w, o) in col_consts]
    in_specs += [pl.BlockSpec((tm, w), lambda j, i, o=o: (i, o + j)) for (_, w, o) in row_ins]
    out_specs = [pl.BlockSpec((tm, w), lambda j, i: (i, j)) for (w, _) in row_outs]
    out_specs += [pl.BlockSpec((nr, w), lambda j, i: (0, j)) for (nr, w) in acc_outs]
    out_shape = [jax.ShapeDtypeStruct((rows, w * ncol), dt) for (w, dt) in row_outs]
    out_shape += [jax.ShapeDtypeStruct((nr, w * ncol), F32) for (nr, w) in acc_outs]
    args = list(full_consts) + [c[0] for c in col_consts] + [r[0] for r in row_ins]
    out = pl.pallas_call(
        body, name=name, grid=(ncol, nt), in_specs=in_specs, out_specs=out_specs, out_shape=out_shape,
        compiler_params=_params(("parallel", "arbitrary")),
    )(*args)
    return out


def _rms(x, w):
    return x * lax.rsqrt(jnp.mean(x * x, axis=-1, keepdims=True) + EPS) * w


def _lane_lo(shape):
    return lax.broadcasted_iota(jnp.int32, shape, len(shape) - 1) < HEAD_DIM


def _pair_sum(x):
    lo = _lane_lo(x.shape)
    s0 = jnp.sum(jnp.where(lo, x, 0.0), axis=-1, keepdims=True)
    s1 = jnp.sum(jnp.where(lo, 0.0, x), axis=-1, keepdims=True)
    return jnp.where(lo, s0, s1)


def _head_col(x, lo, h):
    keep = lo if h == 0 else jnp.logical_not(lo)
    return jnp.max(jnp.where(keep, x, -jnp.inf), axis=-1, keepdims=True)


def _softplus(x):
    return jnp.maximum(x, 0.0) + jnp.log1p(jnp.exp(-jnp.abs(x)))


def _silu(x):
    return x * jax.nn.sigmoid(x)


def _dot(a, b, contract):
    return lax.dot_general(a.astype(BF16), b.astype(BF16), (contract, ((), ())),
                           preferred_element_type=F32)


def _dot32(a, b, contract):
    return lax.dot_general(a, b, (contract, ((), ())), precision=HIGHEST, preferred_element_type=F32)


def _bd(y):
    yy = jnp.concatenate([y, y], axis=0)
    r = lax.broadcasted_iota(jnp.int32, yy.shape, 0) < HEAD_DIM
    c = lax.broadcasted_iota(jnp.int32, yy.shape, 1) < HEAD_DIM
    return jnp.where(r == c, yy, 0.0)


def _pp(x, y):
    return _dot(x, _bd(y), _CONTRACT["nn"])


def _pp_nt(x, y):
    return _dot(x, _bd(y), _CONTRACT["nt"])


def _pp_tn(x, y):
    full = _dot(x, y, _CONTRACT["tn"])
    return jnp.where(_lane_lo((HEAD_DIM, LANES)), full[:HEAD_DIM], full[HEAD_DIM:])


def _gdn_masks():
    row = lax.broadcasted_iota(jnp.int32, (CHUNK, LANES), 0)
    col = lax.broadcasted_iota(jnp.int32, (CHUNK, LANES), 1) % HEAD_DIM
    return row, col


def _interleave(chains):
    live = list(chains)
    while live:
        for g in list(live):
            try:
                next(g)
            except StopIteration:
                live.remove(g)


def _gdn_forward(qkv, betax, gcx, grow, rows):
    nchunk = rows // CHUNK

    def body(q_ref, k_ref, v_ref, bx_ref, gx_ref, gr_ref, o_ref, ss_ref, ts_ref, state):
        n = pl.program_id(0)

        @pl.when(n == 0)
        def _():
            state[...] = jnp.zeros_like(state)

        row, col = _gdn_masks()
        incl, strict = col <= row, col < row

        def chain(p):
            lanes = pl.ds(p * LANES, LANES)
            q, k, v, bx, gx = q_ref[:, lanes], k_ref[:, lanes], v_ref[:, lanes], bx_ref[:, lanes], gx_ref[:, lanes]
            gr = gr_ref[0, p]
            glast = gx_ref[pl.ds(CHUNK - 1, 1), lanes]
            s = state[p]
            dm = jnp.where(incl, jnp.exp(jnp.minimum(gx - gr, 0.0)), 0.0)
            kb, vb, eg, qs = k * bx, v * bx, jnp.exp(gx), q * SCALE
            yield
            big_g, big_p = _pp_nt(kb, k), _pp_nt(qs, k)
            yield
            x = -jnp.where(strict, big_g * dm, 0.0)
            att = jnp.where(incl, big_p * dm, 0.0)
            tm = jnp.where(row == col, 1.0, 0.0) + x
            x = _pp(x, x)
            yield
            for _ in range(4):
                step, x = _pp(tm, x), _pp(x, x)
                yield
                tm = tm + step
            tm = tm + _pp(tm, x)
            yield
            u, w = _pp(tm, vb), _pp(tm, kb * eg)
            yield
            ws, qgs = _pp(w, s), _pp(qs * eg, s)
            yield
            vn = u - ws
            kd = k * jnp.exp(glast - gx)
            avn, upd = _pp(att, vn), _pp_tn(kd, vn)
            yield
            ss_ref[0, p] = s
            ts_ref[0, p] = tm
            o_ref[:, lanes] = qgs + avn
            state[p] = s * jnp.exp(glast) + upd

        _interleave([chain(p) for p in range(PAIRS)])

    blk = lambda j: pl.BlockSpec((CHUNK, WIDTH), lambda n, j=j: (n, j))
    sv = pl.BlockSpec((1, PAIRS, CHUNK, LANES), lambda n: (n, 0, 0, 0))
    return pl.pallas_call(
        body, name="gdn_fwd", grid=(nchunk,),
        in_specs=[blk(0), blk(1), blk(2), blk(0), blk(0),
                  pl.BlockSpec((1, PAIRS, 1, LANES), lambda n: (n, 0, 0, 0))],
        out_specs=[blk(0), sv, sv],
        out_shape=[jax.ShapeDtypeStruct((rows, WIDTH), F32),
                   jax.ShapeDtypeStruct((nchunk, PAIRS, CHUNK, LANES), F32),
                   jax.ShapeDtypeStruct((nchunk, PAIRS, CHUNK, LANES), F32)],
        scratch_shapes=[pltpu.VMEM((PAIRS, CHUNK, LANES), F32)],
        compiler_params=_params(("arbitrary",)),
    )(qkv, qkv, qkv, betax, gcx, grow)


def _gdn_backward(qkv, betax, gcx, grow, ssave, tsave, do, rows):
    nchunk = rows // CHUNK

    def body(q_ref, k_ref, v_ref, bx_ref, gx_ref, gr_ref, ss_ref, ts_ref, do_ref,
             dq_ref, dk_ref, dv_ref, dbx_ref, dgx_ref, dgr_ref, dstate):
        n = pl.program_id(0)

        @pl.when(n == 0)
        def _():
            dstate[...] = jnp.zeros_like(dstate)

        row, col = _gdn_masks()
        incl, strict = col <= row, col < row

        def chain(p):
            lanes = pl.ds(p * LANES, LANES)
            q, k, v, bx, gx = q_ref[:, lanes], k_ref[:, lanes], v_ref[:, lanes], bx_ref[:, lanes], gx_ref[:, lanes]
            gr = gr_ref[0, p]
            glast = gx_ref[pl.ds(CHUNK - 1, 1), lanes]
            s, tm, d_o = ss_ref[0, p], ts_ref[0, p], do_ref[:, lanes]
            ds_out = dstate[p]
            dm = jnp.where(incl, jnp.exp(jnp.minimum(gx - gr, 0.0)), 0.0)
            kb, vb, eg, qs = k * bx, v * bx, jnp.exp(gx), q * SCALE
            kbg, qg = kb * eg, qs * eg
            ed = jnp.exp(glast - gx)
            kd = k * ed
            eglast = jnp.exp(glast)
            yield
            big_g, big_p = _pp_nt(kb, k), _pp_nt(qs, k)
            u, w = _pp(tm, vb), _pp(tm, kbg)
            dqg, kds = _pp_nt(d_o, s), _pp(kd, ds_out)
            yield
            low = jnp.where(strict, big_g * dm, 0.0)
            att = jnp.where(incl, big_p * dm, 0.0)
            ws, atd = _pp(w, s), _pp_tn(att, d_o)
            yield
            vn = u - ws
            dvn = kds + atd
            dkd, datt_raw = _pp_nt(vn, ds_out), _pp_nt(d_o, vn)
            dw_neg, dvb = _pp_nt(dvn, s), _pp_tn(tm, dvn)
            dtm_a, wdv = _pp_nt(dvn, vb), _pp_tn(w, dvn)
            qgd = _pp_tn(qg, d_o)
            yield
            datt = jnp.where(incl, datt_raw, 0.0)
            dw = -dw_neg
            dtm_b, dkbg = _pp_nt(dw, kbg), _pp_tn(tm, dw)
            dbig_p = datt * dm
            dqs_a, dk_p = _pp(dbig_p, k), _pp_tn(dbig_p, qs)
            yield
            inner = _pp_tn(tm, dtm_a + dtm_b)
            yield
            dlow = jnp.where(strict, -_pp_nt(inner, tm), 0.0)
            yield
            dbig_g = dlow * dm
            dkb_a, dk_g = _pp(dbig_g, k), _pp_tn(dbig_g, kb)
            yield
            dkb = dkb_a + dkbg * eg
            dqs = dqs_a + dqg * eg
            dk = dk_g + dk_p + dkd * ed + dkb * bx
            z = dlow * low + datt * att
            kdterm = dkd * kd
            dglast = (jnp.sum(ds_out * s, axis=0, keepdims=True) * eglast
                      + jnp.sum(kdterm, axis=0, keepdims=True))
            dgx = dqg * qg + dkbg * kbg - kdterm
            dgx = dgx + jnp.where(col == 0, _pair_sum(z), 0.0)
            dgx = dgx + jnp.where(row == CHUNK - 1, dglast, 0.0)
            dq_ref[:, lanes] = dqs * SCALE
            dk_ref[:, lanes] = dk
            dv_ref[:, lanes] = dvb * bx
            dbx_ref[:, lanes] = dkb * k + dvb * v
            dgx_ref[:, lanes] = dgx
            dgr_ref[0, p] = -jnp.sum(z, axis=0, keepdims=True)
            dstate[p] = ds_out * eglast + qgd - wdv

        _interleave([chain(p) for p in range(PAIRS)])

    last = nchunk - 1
    blk = lambda j: pl.BlockSpec((CHUNK, WIDTH), lambda n, j=j: (last - n, j))
    sv = pl.BlockSpec((1, PAIRS, CHUNK, LANES), lambda n: (last - n, 0, 0, 0))
    gr_spec = pl.BlockSpec((1, PAIRS, 1, LANES), lambda n: (last - n, 0, 0, 0))
    wide = jax.ShapeDtypeStruct((rows, WIDTH), F32)
    return pl.pallas_call(
        body, name="gdn_bwd", grid=(nchunk,),
        in_specs=[blk(0), blk(1), blk(2), blk(0), blk(0), gr_spec, sv, sv, blk(0)],
        out_specs=[blk(0)] * 5 + [gr_spec],
        out_shape=[wide] * 5 + [jax.ShapeDtypeStruct((nchunk, PAIRS, 1, LANES), F32)],
        scratch_shapes=[pltpu.VMEM((PAIRS, CHUNK, LANES), F32)],
        compiler_params=_params(("arbitrary",)),
    )(qkv, qkv, qkv, betax, gcx, grow, ssave, tsave, do)


ATT_TQ = 256


def _att_scores(qh, kt, fk, diag):
    s = _dot(qh, kt, _CONTRACT["nt"]) - fk
    if diag:
        r = lax.broadcasted_iota(jnp.int32, s.shape, 0)
        c = lax.broadcasted_iota(jnp.int32, s.shape, 1)
        s = jnp.where(r >= c, s, -jnp.inf)
    return s


def _head_masks(n):
    lo = _lane_lo((n, LANES))
    return [lo, jnp.logical_not(lo)]


def _attention_forward(fqk, proj, frow, rows):
    tq = tk = min(ATT_TQ, rows)
    nq = rows // tq
    v_off = 3072 // LANES

    def body(q_ref, k_ref, v_ref, fr_ref, o_ref, lse_ref):
        qi = pl.program_id(1)
        q = q_ref[...] * SCALE
        keep_q, keep_k = _head_masks(tq), _head_masks(tk)
        qh = [jnp.where(keep_q[h], q, 0.0).astype(BF16) for h in range(2)]

        def tile(ki, carry, diag):
            k0 = pl.multiple_of(ki * tk, tk)
            kt = k_ref[pl.ds(k0, tk), :].astype(BF16)
            v_t = v_ref[pl.ds(k0, tk), :]
            out = [None, None]

            def chain(h):
                m, l, acc = carry[h]
                vt = jnp.where(keep_k[h], v_t, 0.0).astype(BF16)
                yield
                s = _att_scores(qh[h], kt, fr_ref[0, pl.ds(h, 1), pl.ds(k0, tk)], diag)
                yield
                m_new = jnp.maximum(m, jnp.max(s, axis=-1, keepdims=True))
                p = jnp.exp(s - m_new)
                alpha = jnp.exp(m - m_new)
                l = alpha * l + jnp.sum(p, axis=-1, keepdims=True)
                p_hi = p.astype(BF16)
                p_lo = p - p_hi.astype(F32)
                yield
                out[h] = (m_new, l, alpha * acc + _dot(p_hi, vt, _CONTRACT["nn"]) + _dot(p_lo, vt, _CONTRACT["nn"]))

            _interleave([chain(0), chain(1)])
            return tuple(out)

        one = (jnp.full((tq, 1), -jnp.inf, F32), jnp.zeros((tq, 1), F32), jnp.zeros((tq, LANES), F32))
        carry = lax.fori_loop(0, qi, lambda ki, c: tile(ki, c, False), (one, one))
        (m0, l0, acc0), (m1, l1, acc1) = tile(qi, carry, True)
        o_ref[...] = acc0 / l0 + acc1 / l1
        lse_ref[...] = jnp.where(keep_q[0], m0 + jnp.log(l0), m1 + jnp.log(l1))

    whole = lambda off: pl.BlockSpec((rows, LANES), lambda p, i, off=off: (0, off + p))
    qblk = lambda off: pl.BlockSpec((tq, LANES), lambda p, i, off=off: (i, off + p))
    wide = jax.ShapeDtypeStruct((rows, WIDTH), F32)
    return pl.pallas_call(
        body, name="fox_fwd", grid=(PAIRS, nq),
        in_specs=[qblk(0), whole(PAIRS), whole(v_off), pl.BlockSpec((1, 2, rows), lambda p, i: (p, 0, 0))],
        out_specs=[qblk(0), qblk(0)], out_shape=[wide, wide],
        compiler_params=_params(("parallel", "arbitrary")),
    )(fqk, fqk, proj, frow)


def _attention_delta(fqk, proj, frow, lse, dao, rows):
    tq = tk = min(ATT_TQ, rows)
    nq = rows // tq
    v_off = 3072 // LANES

    def body(q_ref, k_ref, v_ref, fr_ref, lse_ref, do_ref, delta_ref):
        qi = pl.program_id(1)
        q, d_o, lse_t = q_ref[...] * SCALE, do_ref[...], lse_ref[...]
        keep_q = _head_masks(tq)
        qh = [jnp.where(keep_q[h], q, 0.0).astype(BF16) for h in range(2)]
        doh = [jnp.where(keep_q[h], d_o, 0.0).astype(BF16) for h in range(2)]
        lse_h = [_head_col(lse_t, keep_q[0], h) for h in range(2)]

        def tile(ki, carry, diag):
            k0 = pl.multiple_of(ki * tk, tk)
            kt = k_ref[pl.ds(k0, tk), :].astype(BF16)
            vt = v_ref[pl.ds(k0, tk), :].astype(BF16)
            out = [None, None]

            def chain(h):
                s = _att_scores(qh[h], kt, fr_ref[0, pl.ds(h, 1), pl.ds(k0, tk)], diag)
                dp = _dot(doh[h], vt, _CONTRACT["nt"])
                yield
                out[h] = carry[h] + jnp.sum(jnp.exp(s - lse_h[h]) * dp, axis=-1, keepdims=True)

            _interleave([chain(0), chain(1)])
            return tuple(out)

        zero = jnp.zeros((tq, 1), F32)
        carry = lax.fori_loop(0, qi, lambda ki, c: tile(ki, c, False), (zero, zero))
        d0, d1 = tile(qi, carry, True)
        delta_ref[...] = jnp.where(keep_q[0], d0, d1)

    whole = lambda off: pl.BlockSpec((rows, LANES), lambda p, i, off=off: (0, off + p))
    qblk = lambda off: pl.BlockSpec((tq, LANES), lambda p, i, off=off: (i, off + p))
    return pl.pallas_call(
        body, name="fox_delta", grid=(PAIRS, nq),
        in_specs=[qblk(0), whole(PAIRS), whole(v_off),
                  pl.BlockSpec((1, 2, rows), lambda p, i: (p, 0, 0)), qblk(0), qblk(0)],
        out_specs=qblk(0), out_shape=jax.ShapeDtypeStruct((rows, WIDTH), F32),
        compiler_params=_params(("parallel", "arbitrary")),
    )(fqk, fqk, proj, frow, lse, dao)


def _attention_backward(fqk, proj, frow, ao, lse, dao, rows):
    tq = tk = min(ATT_TQ, rows)
    nq = rows // tq
    v_off = 3072 // LANES

    def body(q_ref, k_ref, v_ref, fr_ref, o_ref, lse_ref, do_ref, dq_ref, dk_ref, dv_ref, dfr_ref):
        ki = pl.program_id(1)

        @pl.when(ki == 0)
        def _():
            dq_ref[...] = jnp.zeros_like(dq_ref)

        keep_q, keep_k = _head_masks(tq), _head_masks(tk)
        k_t = k_ref[...]
        kt = k_t.astype(BF16)
        vt = v_ref[...].astype(BF16)
        kh = [jnp.where(keep_k[h], k_t, 0.0).astype(BF16) for h in range(2)]
        fk = [fr_ref[0, pl.ds(h, 1), :] for h in range(2)]

        def tile(qi, carry, diag):
            dk, dv, df0, df1 = carry
            rows_q = pl.ds(pl.multiple_of(qi * tq, tq), tq)
            q, d_o, lse_t = q_ref[rows_q, :] * SCALE, do_ref[rows_q, :], lse_ref[rows_q, :]
            delta_x = _pair_sum(d_o.astype(BF16).astype(F32) * o_ref[rows_q, :])
            res = [None, None]

            def chain(h):
                qh = jnp.where(keep_q[h], q, 0.0).astype(BF16)
                doh = jnp.where(keep_q[h], d_o, 0.0).astype(BF16)
                lse_h, delta_h = _head_col(lse_t, keep_q[0], h), _head_col(delta_x, keep_q[0], h)
                yield
                s, dp = _att_scores(qh, kt, fk[h], diag), _dot(doh, vt, _CONTRACT["nt"])
                yield
                p = jnp.exp(s - lse_h)
                ds = p * (dp - delta_h)
                yield
                res[h] = (_dot(p, doh, _CONTRACT["tn"]), _dot(ds, qh, _CONTRACT["tn"]),
                          _dot(ds, kh[h], _CONTRACT["nn"]), jnp.sum(ds, axis=0, keepdims=True))

            _interleave([chain(0), chain(1)])
            (dv0, dk0, dq0, s0), (dv1, dk1, dq1, s1) = res
            dq_ref[rows_q, :] += (dq0 + dq1) * SCALE
            return dk + dk0 + dk1, dv + dv0 + dv1, df0 - s0, df1 - s1

        zero_kv = jnp.zeros((tk, LANES), F32)
        zero_f = jnp.zeros((1, tk), F32)
        carry = tile(ki, (zero_kv, zero_kv, zero_f, zero_f), True)
        dk, dv, df0, df1 = lax.fori_loop(ki + 1, nq, lambda qi, c: tile(qi, c, False), carry)
        dk_ref[...] = dk
        dv_ref[...] = dv.astype(dv_ref.dtype)
        dfr_ref[0, pl.ds(0, 1), :] = df0
        dfr_ref[0, pl.ds(1, 1), :] = df1

    whole = lambda off: pl.BlockSpec((rows, LANES), lambda p, i, off=off: (0, off + p))
    kblk = lambda off: pl.BlockSpec((tk, LANES), lambda p, i, off=off: (i, off + p))
    fr_spec = pl.BlockSpec((1, 2, tk), lambda p, i: (p, 0, i))
    wide = jax.ShapeDtypeStruct((rows, WIDTH), F32)
    return pl.pallas_call(
        body, name="fox_bwd", grid=(PAIRS, nq),
        in_specs=[whole(0), kblk(PAIRS), kblk(v_off), fr_spec, whole(0), whole(0), whole(0)],
        out_specs=[whole(0), kblk(0), kblk(0), fr_spec],
        out_shape=[wide, wide, jax.ShapeDtypeStruct((rows, WIDTH), BF16),
                   jax.ShapeDtypeStruct((PAIRS, 2, rows), F32)],
        compiler_params=_params(("parallel", "arbitrary")),
    )(fqk, fqk, proj, frow, ao, lse, dao)


def _lane_ids(shape):
    return lax.broadcasted_iota(jnp.int32, shape, len(shape) - 1)


def _gates_elem(a_log, dt_bias, f_bias, pre):
    lane = _lane_ids(pre.shape)
    beta = jax.nn.sigmoid(pre)
    g = -jnp.exp(a_log) * _softplus(pre + dt_bias)
    lf = -_softplus(-(pre + f_bias))
    return jnp.where(lane < 8, beta, jnp.where(lane < 16, g, jnp.where(lane < 24, lf, 0.0)))


def _tri_consts():
    r = np.arange(LANES)[:, None]
    c = np.arange(LANES)[None, :]
    full = (c <= r).astype(np.float32)
    chunked = full * ((r // CHUNK) == (c // CHUNK))
    return jnp.asarray(chunked), jnp.asarray(full)


def _cums_fwd(lc, lf, gates):
    rows = gates.shape[0]
    lane = _lane_ids((LANES, LANES))
    carry = jnp.zeros((1, LANES), F32)
    out = []
    for r in range(rows // LANES):
        blk = gates[r * LANES:(r + 1) * LANES]
        gc = _dot32(lc, blk, _CONTRACT["nn"])
        f = _dot32(lf, blk, _CONTRACT["nn"]) + carry
        carry = carry + jnp.sum(blk, axis=0, keepdims=True)
        out.append(jnp.where((lane >= 8) & (lane < 16), gc, jnp.where((lane >= 16) & (lane < 24), f, 0.0)))
    return jnp.concatenate(out, axis=0)


def _cums_bwd(lc, lf, dcums):
    rows = dcums.shape[0]
    lane = _lane_ids((LANES, LANES))
    is_g = (lane >= 8) & (lane < 16)
    is_f = (lane >= 16) & (lane < 24)
    carry = jnp.zeros((1, LANES), F32)
    out = [None] * (rows // LANES)
    for r in reversed(range(rows // LANES)):
        blk = dcums[r * LANES:(r + 1) * LANES]
        dg = jnp.where(is_g, blk, 0.0)
        df = jnp.where(is_f, blk, 0.0)
        out[r] = _dot32(lc, dg, _CONTRACT["tn"]) + _dot32(lf, df, _CONTRACT["tn"]) + carry
        carry = carry + jnp.sum(df, axis=0, keepdims=True)
    return jnp.concatenate(out, axis=0)


def _expand_consts():
    xb = np.zeros((LANES, WIDTH), np.float32)
    xg = np.zeros((LANES, WIDTH), np.float32)
    for h in range(HEADS):
        xb[h, h * HEAD_DIM:(h + 1) * HEAD_DIM] = 1.0
        xg[8 + h, h * HEAD_DIM:(h + 1) * HEAD_DIM] = 1.0
    return jnp.asarray(xb), jnp.asarray(xg)


def _shift_down(x, s):
    if s == 0:
        return x
    row = lax.broadcasted_iota(jnp.int32, x.shape, 0)
    return jnp.where(row >= s, pltpu.roll(x, s, 0), 0.0)


def _shift_up(x, s):
    if s == 0:
        return x
    n = x.shape[0]
    row = lax.broadcasted_iota(jnp.int32, x.shape, 0)
    return jnp.where(row < n - s, pltpu.roll(x, n - s, 0), 0.0)


def _row_of(cw, i):
    row = lax.broadcasted_iota(jnp.int32, cw.shape, 0)
    return jnp.sum(jnp.where(row == i, cw, 0.0), axis=0, keepdims=True)


def _conv(cw, x):
    c = jnp.zeros_like(x)
    for i in range(CONV_K):
        c = c + _row_of(cw, i) * _shift_down(x, CONV_K - 1 - i)
    return c


def _post_conv(is_qk, c):
    s = _silu(c)
    n = s * lax.rsqrt(_pair_sum(s * s) + EPS)
    return jnp.where(is_qk, n, s)


def _gdn_prep_fwd(col, cw, x):
    return (_post_conv(col < 2 * PAIRS, _conv(cw, x)),)


def _gdn_prep_bwd(is_qk, cw, x, dy):
    c = _conv(cw, x)
    _, vjp = jax.vjp(lambda cc: _post_conv(is_qk, cc), c)
    (dc,) = vjp(dy)
    dx = jnp.zeros_like(x)
    row = lax.broadcasted_iota(jnp.int32, cw.shape, 0)
    dcw = jnp.zeros(cw.shape, F32)
    for i in range(CONV_K):
        s = CONV_K - 1 - i
        dx = dx + _row_of(cw, i) * _shift_up(dc, s)
        dcw = dcw + jnp.where(row == i, jnp.sum(dc * _shift_down(x, s), axis=0, keepdims=True), 0.0)
    return dx, dcw


def _head_rms(w, x):
    return x * lax.rsqrt(_pair_sum(x * x) / HEAD_DIM + EPS) * w


def _cat_weights(w_in_t):
    tail = jnp.pad(w_in_t[4112:4120], ((0, D_CAT - D_IN), (0, 0)))
    return jnp.concatenate([w_in_t[:2048], w_in_t[2064:4112], w_in_t[2048:2064], tail], axis=0)


def _uncat_grad(g):
    return jnp.concatenate([g[:2048], g[4096:4112], g[2048:4096], g[4112:4120]], axis=0)


def _lanes_to_rowform(v8, rows):
    return v8.reshape(rows // CHUNK, CHUNK, HEADS).transpose(0, 2, 1).reshape(rows // CHUNK, PAIRS, 1, LANES)


def _rowform_to_lanes(v, rows):
    return v.reshape(rows // CHUNK, HEADS, CHUNK).transpose(0, 2, 1).reshape(rows, HEADS)


def _local_step(x, target, norm1_w, w_cat, conv_w, a_log, dt_bias, out_norm_w, f_bias, q_norm_w, k_norm_w,
                norm2_w, final_w, late_weights, early_grads_ready):
    rows = x.shape[0]
    tm = min(512, rows)
    lc, lf = _tri_consts()
    xb, xg = _expand_consts()

    (h1,) = _tiles(lambda col, w, xx: (_rms(xx, w),), name="norm1", rows=rows, tm=tm,
                   full_consts=[norm1_w], row_ins=[(x, D_MODEL, 0)], row_outs=[(D_MODEL, BF16)])
    proj = _mm(h1, w_cat, dims="nt", name="in_proj", tn=384, tk=1024)

    lane_pad = lambda v, off: jnp.pad(v.reshape(1, -1), ((0, 0), (off, LANES - off - v.size)))
    p_a, p_dt, p_fb = lane_pad(a_log, 8), lane_pad(dt_bias, 8), lane_pad(f_bias, 16)

    def gates_fwd(col, lcv, lfv, a, dt, fb, pre):
        gates = _gates_elem(a, dt, fb, pre)
        return gates, _cums_fwd(lcv, lfv, gates)

    gates, cums = _tiles(gates_fwd, name="gates", rows=rows, tm=rows,
                         full_consts=[lc, lf, p_a, p_dt, p_fb], row_ins=[(proj, LANES, COL_SMALL)],
                         row_outs=[(LANES, F32), (LANES, F32)])

    def expand_fwd(col, b, g, gt, cm):
        return (_dot32(gt, b, _CONTRACT["nn"]), _dot32(cm, g, _CONTRACT["nn"]))

    betax, gcx = _tiles(expand_fwd, name="expand", rows=rows, tm=tm, full_consts=[xb, xg],
                        row_ins=[(gates, LANES, 0), (cums, LANES, 0)],
                        row_outs=[(WIDTH, F32)] * 2)
    grow = _lanes_to_rowform(cums[:, 8:16], rows)
    frow = cums[:, 16:24].T.reshape(PAIRS, 2, rows)

    (qkv,) = _tiles(_gdn_prep_fwd, name="gdn_prep", rows=rows, tm=rows, ncol=3 * PAIRS,
                    col_consts=[(conv_w, CONV_K, LANES, 0)], row_ins=[(proj, LANES, 0)],
                    row_outs=[(LANES, F32)])
    o_gdn, ssave, tsave = _gdn_forward(qkv, betax, gcx, grow, rows)

    w_qk = jnp.concatenate([jnp.tile(q_norm_w.reshape(1, -1), (1, HEADS)),
                            jnp.tile(k_norm_w.reshape(1, -1), (1, HEADS))], axis=1)
    fox_off = 2048 // LANES
    (fqk,) = _tiles(lambda col, w, xx: (_head_rms(w, xx),), name="fox_prep", rows=rows, tm=rows, ncol=2 * PAIRS,
                    col_consts=[(w_qk, 1, LANES, 0)], row_ins=[(proj, LANES, fox_off)],
                    row_outs=[(LANES, F32)])
    ao, lse = _attention_forward(fqk, proj, frow, rows)

    w_on = jnp.tile(out_norm_w.reshape(1, -1), (1, 2))
    z_off, fg_off = 1536 // LANES, 3584 // LANES
    mix_g_fn = lambda w, o, z: _head_rms(w, o) * _silu(z)
    mix_f_fn = lambda a, g: a * jax.nn.sigmoid(g)
    (mix_g,) = _tiles(lambda col, w, o, z: (mix_g_fn(w, o, z),), name="mix_gdn", rows=rows, tm=rows, ncol=PAIRS,
                      full_consts=[w_on], row_ins=[(o_gdn, LANES, 0), (proj, LANES, z_off)],
                      row_outs=[(LANES, BF16)])
    (mix_f,) = _tiles(lambda col, a, g: (mix_f_fn(a, g),), name="mix_fox", rows=rows, tm=rows, ncol=PAIRS,
                      row_ins=[(ao, LANES, 0), (proj, LANES, fg_off)], row_outs=[(LANES, BF16)])
    mix = jnp.concatenate([mix_g, mix_f], axis=1)
    w_out, w_gate, w_up, w_down = late_weights(mix)
    x1 = _mm(mix, w_out, dims="nn", name="out_proj", add=x, tk=1024)

    (h2,) = _tiles(lambda col, w, xx: (_rms(xx, w),), name="norm2", rows=rows, tm=tm,
                   full_consts=[norm2_w], row_ins=[(x1, D_MODEL, 0)], row_outs=[(D_MODEL, BF16)])
    t_rows, t_cols, t_act = min(1024, rows), 512, min(512, rows)
    n_rt = rows // t_rows
    st_act = jax.ShapeDtypeStruct((N_CHIPS, rows, FF_SHARD), BF16)
    st_rows = pl.BlockSpec((None, t_rows, FF_SHARD), lambda i, j: (j, i, 0))
    out_rows = pl.BlockSpec((t_rows, t_cols), lambda i, n: (i, n))
    flat = lambda t: t.reshape(N_CHIPS * rows, FF_SHARD)

    def ffn_in(w_st, name):
        return _mm_blocks(h2, w_st, name=name, grid=(n_rt, N_CHIPS), dims="nt",
                          a_spec=pl.BlockSpec((t_rows, D_MODEL), lambda i, j: (i, 0)),
                          b_spec=pl.BlockSpec((None, FF_SHARD, D_MODEL), lambda i, j: (j, 0, 0)),
                          o_spec=st_rows, out_shape=st_act)

    gate, up = ffn_in(w_gate, "ffn_gate"), ffn_in(w_up, "ffn_up")
    act_fn = lambda g, u: _silu(g.astype(F32)) * u.astype(F32)
    (act,) = _tiles(lambda col, g, u: (act_fn(g, u),), name="ffn_act", rows=N_CHIPS * rows, tm=t_act,
                    row_ins=[(flat(gate), FF_SHARD, 0), (flat(up), FF_SHARD, 0)], row_outs=[(FF_SHARD, BF16)])
    act = act.reshape(st_act.shape)
    x2 = _mm_blocks(act, w_down, name="ffn_down", grid=(n_rt, D_MODEL // t_cols), dims="nn", n_sum=N_CHIPS,
                    a_spec=pl.BlockSpec((N_CHIPS, t_rows, FF_SHARD), lambda i, n: (0, i, 0)),
                    b_spec=pl.BlockSpec((N_CHIPS, FF_SHARD, t_cols), lambda i, n: (0, 0, n)),
                    o_spec=out_rows, out_shape=jax.ShapeDtypeStruct((rows, D_MODEL), F32),
                    add=x1, add_spec=out_rows)

    def final_fn(col, w, xx, tgt):
        y, vjp = jax.vjp(_rms, xx, w)
        err = y - tgt
        loss = 0.5 * jnp.sum(err * err) / D_MODEL
        dx, dw = vjp(err / D_MODEL)
        return dx, dx, jnp.full((1, LANES), loss, F32), dw

    dx2, dx2_b, loss, d_final_w = _tiles(final_fn, name="final_loss", rows=rows, tm=tm, full_consts=[final_w],
                                         row_ins=[(x2, D_MODEL, 0), (target, D_MODEL, 0)],
                                         row_outs=[(D_MODEL, F32), (D_MODEL, BF16)],
                                         acc_outs=[(1, LANES), (1, D_MODEL)])

    dact = _mm_blocks(dx2_b, w_down, name="d_act", grid=(n_rt, N_CHIPS), dims="nt",
                      a_spec=pl.BlockSpec((t_rows, D_MODEL), lambda i, j: (i, 0)),
                      b_spec=pl.BlockSpec((None, FF_SHARD, D_MODEL), lambda i, j: (j, 0, 0)),
                      o_spec=st_rows, out_shape=st_act)
    def g_ffn(d_st, other, name):
        return _mm_blocks(d_st, other, name=name, grid=(N_CHIPS, D_MODEL // t_cols), dims="tn",
                          a_spec=pl.BlockSpec((None, rows, FF_SHARD), lambda j, n: (j, 0, 0)),
                          b_spec=pl.BlockSpec((rows, t_cols), lambda j, n: (0, n)),
                          o_spec=pl.BlockSpec((None, FF_SHARD, t_cols), lambda j, n: (j, 0, n)),
                          out_shape=jax.ShapeDtypeStruct((N_CHIPS, FF_SHARD, D_MODEL), BF16))

    g_down = g_ffn(act, dx2_b, "g_down")

    def act_bwd(col, g, u, d):
        _, vjp = jax.vjp(lambda gg, uu: _silu(gg) * uu, g.astype(F32), u.astype(F32))
        return vjp(d.astype(F32))

    dgate, dup = _tiles(act_bwd, name="ffn_act_bwd", rows=N_CHIPS * rows, tm=t_act,
                        row_ins=[(flat(gate), FF_SHARD, 0), (flat(up), FF_SHARD, 0), (flat(dact), FF_SHARD, 0)],
                        row_outs=[(FF_SHARD, BF16), (FF_SHARD, BF16)])
    dgate, dup = dgate.reshape(st_act.shape), dup.reshape(st_act.shape)

    def d_h2(d_st, w_st, name, add):
        return _mm_blocks(d_st, w_st, name=name, grid=(n_rt, D_MODEL // t_cols), dims="nn", n_sum=N_CHIPS,
                          a_spec=pl.BlockSpec((N_CHIPS, t_rows, FF_SHARD), lambda i, n: (0, i, 0)),
                          b_spec=pl.BlockSpec((N_CHIPS, FF_SHARD, t_cols), lambda i, n: (0, 0, n)),
                          o_spec=out_rows, out_shape=jax.ShapeDtypeStruct((rows, D_MODEL), F32),
                          add=add, add_spec=out_rows)

    dh2 = d_h2(dup, w_up, "d_h2_up", d_h2(dgate, w_gate, "d_h2_gate", None))
    g_gate, g_up = g_ffn(dgate, h2, "g_gate"), g_ffn(dup, h2, "g_up")

    def norm_bwd(col, w, xx, dh, dres):
        _, vjp = jax.vjp(_rms, xx, w)
        dx, dw = vjp(dh)
        return dx + dres, dx + dres, dw

    dx1, dx1_b, d_norm2_w = _tiles(norm_bwd, name="norm2_bwd", rows=rows, tm=tm, full_consts=[norm2_w],
                                   row_ins=[(x1, D_MODEL, 0), (dh2, D_MODEL, 0), (dx2, D_MODEL, 0)],
                                   row_outs=[(D_MODEL, F32), (D_MODEL, BF16)], acc_outs=[(1, D_MODEL)])
    dmix = _mm(dx1_b, w_out, dims="nt", name="d_mix", tk=1024)
    g_out = _mm(mix, dx1_b, dims="tn", name="g_out", tk=rows, out_dtype=BF16)
    w_on = w_on + early_grads_ready(g_out, g_gate, g_up, g_down)

    def mix_g_bwd(col, w, o, z, d):
        _, vjp = jax.vjp(mix_g_fn, w, o, z)
        dw, do_, dz = vjp(d)
        return do_, dz, dw

    do_gdn, dz, d_on = _tiles(mix_g_bwd, name="mix_gdn_bwd", rows=rows, tm=rows, ncol=PAIRS, full_consts=[w_on],
                              row_ins=[(o_gdn, LANES, 0), (proj, LANES, z_off), (dmix, LANES, 0)],
                              row_outs=[(LANES, F32), (LANES, BF16)], acc_outs=[(1, LANES)])

    def mix_f_bwd(col, a, g, d):
        _, vjp = jax.vjp(mix_f_fn, a, g)
        return vjp(d)

    dao, dfgate = _tiles(mix_f_bwd, name="mix_fox_bwd", rows=rows, tm=rows, ncol=PAIRS,
                         row_ins=[(ao, LANES, 0), (proj, LANES, fg_off), (dmix, LANES, PAIRS)],
                         row_outs=[(LANES, F32), (LANES, BF16)])

    dfq, dfk, dfv, dfrow = _attention_backward(fqk, proj, frow, ao, lse, dao, rows)

    def fox_prep_bwd(col, w, xx, d):
        _, vjp = jax.vjp(_head_rms, w, xx)
        dw, dx = vjp(d)
        return dx, dw

    dfqk, d_wqk = [], []
    for part, d_n in enumerate((dfq, dfk)):
        dx_p, dw_p = _tiles(fox_prep_bwd, name="fox_prep_bwd_" + "qk"[part], rows=rows, tm=rows, ncol=PAIRS,
                            col_consts=[(w_qk, 1, LANES, part * PAIRS)],
                            row_ins=[(proj, LANES, fox_off + part * PAIRS), (d_n, LANES, 0)],
                            row_outs=[(LANES, BF16)], acc_outs=[(1, LANES)])
        dfqk.append(dx_p)
        d_wqk.append(dw_p)

    dq, dk, dv, dbetax, dgcx, dgrow = _gdn_backward(qkv, betax, gcx, grow, ssave, tsave, do_gdn, rows)
    dqkv, d_conv = [], []
    for part, d_n in enumerate((dq, dk, dv)):
        prep_bwd = lambda col, cw, xx, dy, is_qk=(part < 2): _gdn_prep_bwd(is_qk, cw, xx, dy)
        dx_p, dw_p = _tiles(prep_bwd, name="gdn_prep_bwd_" + "qkv"[part], rows=rows, tm=rows, ncol=PAIRS,
                            col_consts=[(conv_w, CONV_K, LANES, part * PAIRS)],
                            row_ins=[(proj, LANES, part * PAIRS), (d_n, LANES, 0)],
                            row_outs=[(LANES, BF16)], acc_outs=[(CONV_K, LANES)])
        dqkv.append(dx_p)
        d_conv.append(dw_p)
    d_conv = jnp.concatenate(d_conv, axis=1)

    def expand_bwd(col, b, g, db, dg):
        return (_dot32(db, b, _CONTRACT["nt"]), _dot32(dg, g, _CONTRACT["nt"]))

    dgates_b, dcums_g = _tiles(expand_bwd, name="expand_bwd", rows=rows, tm=tm, full_consts=[xb, xg],
                               row_ins=[(dbetax, WIDTH, 0), (dgcx, WIDTH, 0)],
                               row_outs=[(LANES, F32), (LANES, F32)])
    dcums_row = jnp.concatenate([jnp.zeros((rows, 8), F32), _rowform_to_lanes(dgrow, rows),
                                 dfrow.reshape(HEADS, rows).T, jnp.zeros((rows, LANES - 24), F32)], axis=1)

    def gates_bwd(col, lcv, lfv, a, dt, fb, pre, dgb, dcg, dcr):
        lane = _lane_ids(pre.shape)
        dgates = jnp.where(lane < 8, dgb, _cums_bwd(lcv, lfv, dcg + dcr))
        _, vjp = jax.vjp(_gates_elem, a, dt, fb, pre)
        da, ddt, dfb, dpre = vjp(dgates)
        return dpre, da, ddt, dfb

    dpre, d_a, d_dt, d_fb = _tiles(gates_bwd, name="gates_bwd", rows=rows, tm=rows,
                                   full_consts=[lc, lf, p_a, p_dt, p_fb],
                                   row_ins=[(proj, LANES, COL_SMALL), (dgates_b, LANES, 0), (dcums_g, LANES, 0),
                                            (dcums_row, LANES, 0)],
                                   row_outs=[(LANES, BF16)], acc_outs=[(1, LANES)] * 3)

    dproj = jnp.concatenate(dqkv + [dz] + dfqk + [dfv, dfgate, dpre], axis=1)
    dh1 = _mm(dproj, w_cat, dims="nn", name="d_h1", tk=D_CAT)
    g_cat = _mm(dproj, h1, dims="tn", name="g_in", tm=384, tn=D_MODEL, tk=rows)

    def norm1_bwd(col, w, xx, dh, dres):
        _, vjp = jax.vjp(_rms, xx, w)
        dx, dw = vjp(dh)
        return dx + dres, dw

    grad_x, d_norm1_w = _tiles(norm1_bwd, name="norm1_bwd", rows=rows, tm=tm, full_consts=[norm1_w],
                               row_ins=[(x, D_MODEL, 0), (dh1, D_MODEL, 0), (dx1, D_MODEL, 0)],
                               row_outs=[(D_MODEL, F32)], acc_outs=[(1, D_MODEL)])

    fold = lambda v: v.reshape(-1, HEAD_DIM).sum(axis=0)
    small = dict(
        loss=loss[0, 0],
        norm1_w=d_norm1_w, conv_w=d_conv, a_log=d_a[0, 8:16], dt_bias=d_dt[0, 8:16],
        out_norm_w=fold(d_on), f_bias=d_fb[0, 16:24], q_norm_w=fold(d_wqk[0]),
        k_norm_w=fold(d_wqk[1]), norm2_w=d_norm2_w, final_w=d_final_w)
    return grad_x, g_cat, g_out, g_gate, g_up, g_down, small


HBM_SPEC = pl.BlockSpec(memory_space=pltpu.HBM)


def _place():
    x, y, c = lax.axis_index("x"), lax.axis_index("y"), lax.axis_index("c")
    chips = [(1 - x, y), (x, 1 - y), (1 - x, 1 - y)]
    return x, y, c, 2 * x + y, (x, y, 1 - c), chips, [2 * cx + cy for cx, cy in chips]


def _remote(src, dst, send_sem, recv_sem, to):
    return pltpu.make_async_remote_copy(src_ref=src, dst_ref=dst, send_sem=send_sem, recv_sem=recv_sem,
                                        device_id=to, device_id_type=MESH)


def _allgather_weights(shards, conv):
    n = len(shards)
    halves = [s.shape[1] // 2 for s in shards]
    per = 6
    own_base = n * per + 3

    def body(*refs):
        ins, conv_in = refs[:n], refs[n]
        outs, conv_out = refs[n + 1:2 * n + 1], refs[2 * n + 1]
        send_sems, recv_sems = refs[2 * n + 2:]
        x, y, c, own, sib, chips, chip_idx = _place()

        def half(i, ref, hc):
            return ref.at[:, pl.ds(pl.multiple_of(hc * halves[i], LANES), halves[i])]

        sent = []
        for i, (src, dst) in enumerate(zip(list(ins) + [conv_in], list(outs) + [conv_out])):
            k = own_base + i
            sent.append(_remote(src, dst.at[own], send_sems.at[k], recv_sems.at[k], sib))
        for i in range(n):
            for j, chip in enumerate(chips):
                k = i * per + j
                sent.append(_remote(half(i, ins[i], c), half(i, outs[i].at[own], c),
                                    send_sems.at[k], recv_sems.at[k], (*chip, c)))
        for j, chip in enumerate(chips):
            k = n * per + j
            sent.append(_remote(conv_in, conv_out.at[own], send_sems.at[k], recv_sems.at[k], (*chip, c)))
        for cp in sent:
            cp.start()
        for i in range(n):
            for j in range(len(chips)):
                k = i * per + j
                landed = half(i, outs[i].at[chip_idx[j]], c)
                _remote(landed, landed, send_sems.at[k], recv_sems.at[k], sib).wait_recv()
                fwd = _remote(landed, landed, send_sems.at[k + 3], recv_sems.at[k + 3], sib)
                fwd.start()
                sent.append(fwd)
        for i in range(n):
            for j in range(len(chips)):
                k = i * per + 3 + j
                landed = half(i, outs[i].at[chip_idx[j]], 1 - c)
                _remote(landed, landed, send_sems.at[k], recv_sems.at[k], sib).wait_recv()
        for j in range(len(chips)):
            k = n * per + j
            landed = conv_out.at[chip_idx[j]]
            _remote(landed, landed, send_sems.at[k], recv_sems.at[k], sib).wait_recv()
        for i, dst in enumerate(list(outs) + [conv_out]):
            k = own_base + i
            landed = dst.at[own]
            _remote(landed, landed, send_sems.at[k], recv_sems.at[k], sib).wait_recv()
        for cp in sent:
            cp.wait_send()

    n_sem = own_base + n + 1
    out_shape = [jax.ShapeDtypeStruct((N_CHIPS,) + s.shape, s.dtype) for s in shards]
    out_shape.append(jax.ShapeDtypeStruct((N_CHIPS,) + conv.shape, conv.dtype))
    res = pl.pallas_call(
        body, name="allgather_weights", out_shape=out_shape,
        in_specs=[HBM_SPEC] * (n + 1), out_specs=[HBM_SPEC] * (n + 1),
        scratch_shapes=[pltpu.SemaphoreType.DMA((n_sem,)), pltpu.SemaphoreType.DMA((n_sem,))],
    )(*shards, conv)
    return res[:n], res[n]


SEM_SPEC = pl.BlockSpec(memory_space=pltpu.SEMAPHORE)
ANY_SPEC = pl.BlockSpec(memory_space=pl.ANY)
DATAFLOW = pltpu.SideEffectType.DATAFLOW_SIDE_EFFECTING


def _gather_plan(srcs, lands):
    x, y, c, own, sib, chips, chip_idx = _place()
    plan = []
    for src, land in zip(srcs, lands):
        for j, chip in enumerate(chips):
            plan.append((src, land.at[own], (*chip, c), land.at[chip_idx[j]]))
        plan.append((src, land.at[own], sib, land.at[own]))
    return plan


def _exchange_plan(srcs, lands):
    x, y, c, own, sib, chips, chip_idx = _place()
    plan = []
    for src, land in zip(srcs, lands):
        for j, chip in enumerate(chips):
            plan.append((src.at[chip_idx[j]], land.at[j], (*chip, c), land.at[j]))
    return plan


def _split_start(name, plan_fn, srcs, land_shapes, n_copies, after):
    n = len(srcs)

    def body(*refs):
        src_refs, land_refs = refs[:n], refs[n:2 * n]
        send_sems, recv_sems = refs[2 * n + 1], refs[2 * n + 2]
        token = refs[-1]
        for k, (src, dst, to, _) in enumerate(plan_fn(src_refs, land_refs)):
            _remote(src, dst, send_sems.at[k], recv_sems.at[k], to).start()
        token[...] = jnp.zeros_like(token)

    lands = [pltpu.with_memory_space_constraint(lax.empty(s.shape, s.dtype), pltpu.HBM) for s in land_shapes]
    srcs = [pltpu.with_memory_space_constraint(s, pltpu.HBM) for s in srcs]
    out_shape = ([pltpu.SemaphoreType.DMA((n_copies,)), pltpu.SemaphoreType.DMA((n_copies,))]
                 + [pltpu.HBM(s.shape, s.dtype) for s in srcs] + [pltpu.HBM(s.shape, s.dtype) for s in land_shapes]
                 + [jax.ShapeDtypeStruct((8, LANES), F32)])
    res = pl.pallas_call(
        body, name=name, out_shape=out_shape,
        in_specs=[HBM_SPEC] * (2 * n) + [ANY_SPEC],
        out_specs=[SEM_SPEC, SEM_SPEC] + [HBM_SPEC] * (2 * n) + [pl.BlockSpec(memory_space=pltpu.VMEM)],
        input_output_aliases={i: 2 + i for i in range(2 * n)},
        compiler_params=pltpu.CompilerParams(has_side_effects=DATAFLOW),
    )(*srcs, *lands, after)
    return dict(sems=res[:2], srcs=res[2:2 + n], lands=res[2 + n:2 + 2 * n], token=res[-1], n=n)


def _split_wait(name, plan_fn, started, after):
    n = started["n"]

    def body(*refs):
        src_refs, land_refs = refs[:n], refs[n:2 * n]
        send_sems, recv_sems = refs[2 * n], refs[2 * n + 1]
        for k, (src, _, to, landed) in enumerate(plan_fn(src_refs, land_refs)):
            copy = _remote(src, landed, send_sems.at[k], recv_sems.at[k], to)
            copy.wait_send()
            copy.wait_recv()

    srcs, lands = started["srcs"], started["lands"]
    after = list(after) if isinstance(after, (list, tuple)) else [after]
    res = pl.pallas_call(
        body, name=name,
        out_shape=[pltpu.HBM(s.shape, s.dtype) for s in srcs] + [pltpu.HBM(s.shape, s.dtype) for s in lands],
        in_specs=[HBM_SPEC] * (2 * n) + [SEM_SPEC, SEM_SPEC] + [ANY_SPEC] * len(after),
        out_specs=[HBM_SPEC] * (2 * n),
        input_output_aliases={i: i for i in range(2 * n)},
        compiler_params=pltpu.CompilerParams(has_side_effects=DATAFLOW),
    )(*srcs, *lands, *started["sems"], *after)
    return res[n:]


def _swap_halves(stacks, name):
    n = len(stacks)

    def body(*refs):
        ins, outs = refs[:n], refs[n:2 * n]
        send_sems, recv_sems = refs[2 * n:]
        x, y, c, own, sib, chips, chip_idx = _place()
        cps = []
        for i in range(n):
            h = stacks[i].shape[2] // 2
            src = ins[i].at[:, :, pl.ds(pl.multiple_of((1 - c) * h, LANES), h)]
            cps.append(_remote(src, outs[i], send_sems.at[i], recv_sems.at[i], sib))
        for cp in cps:
            cp.start()
        for cp in cps:
            cp.wait()

    out_shape = [jax.ShapeDtypeStruct((N_CHIPS, s.shape[1], s.shape[2] // 2), s.dtype) for s in stacks]
    return pl.pallas_call(
        body, name=name, out_shape=out_shape,
        in_specs=[HBM_SPEC] * n, out_specs=[HBM_SPEC] * n,
        scratch_shapes=[pltpu.SemaphoreType.DMA((n,)), pltpu.SemaphoreType.DMA((n,))],
    )(*stacks)


def _add_half(stack, landed, place, name):
    _, rows, h = landed.shape

    def body(place_ref, a_ref, b_ref, o_ref, own_ref):
        part = (a_ref[...].astype(F32) + b_ref[...].astype(F32)).astype(o_ref.dtype)
        o_ref[...] = part

        @pl.when(pl.program_id(0) == place_ref[1])
        def _():
            own_ref[...] = part[0]

    return pl.pallas_call(
        body, name=name,
        out_shape=[jax.ShapeDtypeStruct(landed.shape, BF16), jax.ShapeDtypeStruct((rows, h), BF16)],
        grid_spec=pltpu.PrefetchScalarGridSpec(
            num_scalar_prefetch=1, grid=(N_CHIPS,),
            in_specs=[pl.BlockSpec((1, rows, h), lambda j, p: (j, 0, p[0])),
                      pl.BlockSpec((1, rows, h), lambda j, p: (j, 0, 0))],
            out_specs=[pl.BlockSpec((1, rows, h), lambda j, p: (j, 0, 0)),
                       pl.BlockSpec((rows, h), lambda j, p: (0, 0))]),
        compiler_params=_params(("arbitrary",)),
    )(place, stack, landed)


def _exchange_partials(parts):
    n = len(parts)

    def body(*refs):
        ins, outs = refs[:n], refs[n:2 * n]
        send_sems, recv_sems = refs[2 * n:]
        x, y, c, own, sib, chips, chip_idx = _place()
        sent = []
        for i in range(n):
            for j, chip in enumerate(chips):
                k = i * 3 + j
                sent.append(_remote(ins[i].at[chip_idx[j]], outs[i].at[j], send_sems.at[k], recv_sems.at[k],
                                    (*chip, c)))
        for cp in sent:
            cp.start()
        for i in range(n):
            for j in range(len(chips)):
                k = i * 3 + j
                landed = outs[i].at[j]
                _remote(landed, landed, send_sems.at[k], recv_sems.at[k], sib).wait_recv()
        for cp in sent:
            cp.wait_send()

    return pl.pallas_call(
        body, name="rs_exchange_partials",
        out_shape=[jax.ShapeDtypeStruct((3,) + p.shape[1:], p.dtype) for p in parts],
        in_specs=[HBM_SPEC] * n, out_specs=[HBM_SPEC] * n,
        scratch_shapes=[pltpu.SemaphoreType.DMA((3 * n,)), pltpu.SemaphoreType.DMA((3 * n,))],
    )(*parts)


def _sum_partials(own_part, landed, name):
    _, h, cols = landed.shape

    def body(own_ref, a_ref, o_ref):
        acc = own_ref[...].astype(F32)
        for s in range(3):
            acc = acc + a_ref[s].astype(F32)
        o_ref[...] = acc

    return pl.pallas_call(
        body, name=name, out_shape=jax.ShapeDtypeStruct((h, cols), F32), grid=(1,),
        in_specs=[pl.BlockSpec((h, cols), lambda i: (0, 0)), pl.BlockSpec(landed.shape, lambda i: (0, 0, 0))],
        out_specs=pl.BlockSpec((h, cols), lambda i: (0, 0)),
        compiler_params=_params(("arbitrary",)),
    )(own_part, landed)


def _share_halves(halves, name):
    n = len(halves)

    def body(*refs):
        ins, outs = refs[:n], refs[n:2 * n]
        send_sems, recv_sems = refs[2 * n:]
        x, y, c, own, sib, chips, chip_idx = _place()
        cps = [_remote(ins[i], outs[i], send_sems.at[i], recv_sems.at[i], sib) for i in range(n)]
        for cp in cps:
            cp.start()
        for cp in cps:
            cp.wait()

    return pl.pallas_call(
        body, name=name,
        out_shape=[jax.ShapeDtypeStruct(p.shape, p.dtype) for p in halves],
        in_specs=[HBM_SPEC] * n, out_specs=[HBM_SPEC] * n,
        scratch_shapes=[pltpu.SemaphoreType.DMA((n,)), pltpu.SemaphoreType.DMA((n,))],
    )(*halves)


def _allreduce_small(packed):
    rows = packed.shape[0]
    n_dev = 8

    def body(in_ref, out_ref, gath, send_sems, recv_sems):
        x, y, c = lax.axis_index("x"), lax.axis_index("y"), lax.axis_index("c")
        me = 4 * x + 2 * y + c
        gath[me] = in_ref[...]
        cps = []
        for k in range(1, n_dev):
            fx, fy, fc = (k >> 2) & 1, (k >> 1) & 1, k & 1
            to = (x ^ fx, y ^ fy, c ^ fc)
            cps.append(_remote(in_ref, gath.at[me], send_sems.at[k - 1], recv_sems.at[k - 1], to))
        for cp in cps:
            cp.start()
        for k in range(1, n_dev):
            fx, fy, fc = (k >> 2) & 1, (k >> 1) & 1, k & 1
            src = 4 * (x ^ fx) + 2 * (y ^ fy) + (c ^ fc)
            slot = gath.at[src]
            _remote(slot, slot, send_sems.at[k - 1], recv_sems.at[k - 1], (x, y, c)).wait_recv()
        for cp in cps:
            cp.wait_send()
        acc = gath[0]
        for d in range(1, n_dev):
            acc = acc + gath[d]
        out_ref[...] = acc

    vm = pl.BlockSpec(memory_space=pltpu.VMEM)
    return pl.pallas_call(
        body, name="allreduce_small", out_shape=jax.ShapeDtypeStruct(packed.shape, F32),
        in_specs=[vm], out_specs=vm,
        scratch_shapes=[pltpu.VMEM((n_dev, rows, LANES), F32),
                        pltpu.SemaphoreType.DMA((n_dev - 1,)), pltpu.SemaphoreType.DMA((n_dev - 1,))],
    )(packed)


def _adam(col, w, g, m, v):
    m2 = ADAM_B1 * m + (1.0 - ADAM_B1) * g
    v2 = ADAM_B2 * v + (1.0 - ADAM_B2) * (g * g)
    m_hat = m2 / (1.0 - ADAM_B1 ** ADAM_STEP)
    v_hat = v2 / (1.0 - ADAM_B2 ** ADAM_STEP)
    delta = -ADAM_LR * (m_hat / (jnp.sqrt(v_hat) + ADAM_EPS) + ADAM_WD * w)
    return delta, m2, v2


def _adam_call(w, g, m, v, name):
    rows, cols = w.shape
    tm = rows
    for cand in (256, 352, 176, 128, 64, 48, 16, 8):
        if rows % cand == 0:
            tm = cand
            break
    return _tiles(_adam, name=name, rows=rows, tm=tm,
                  row_ins=[(w, cols, 0), (g, cols, 0), (m, cols, 0), (v, cols, 0)],
                  row_outs=[(cols, F32)] * 3)


def _adam_big(w, g_mine, g_other, m, v, place, name):
    rows, cols = w.shape
    tc = 256
    nt = cols // 2 // tc

    def body(place_ref, w_ref, gm_ref, go_ref, m_ref, v_ref, g_out, d_out, m_out, v_out):
        g = jnp.where(pl.program_id(0) == place_ref[0], gm_ref[...], go_ref[...])
        d, m2, v2 = _adam(None, w_ref[...], g, m_ref[...], v_ref[...])
        g_out[...] = g
        d_out[...] = d
        m_out[...] = m2
        v_out[...] = v2

    full = pl.BlockSpec((rows, tc), lambda hh, i, p: (0, hh * nt + i))
    half = pl.BlockSpec((rows, tc), lambda hh, i, p: (0, i))
    return pl.pallas_call(
        body, name=name, out_shape=[jax.ShapeDtypeStruct(w.shape, F32)] * 4,
        grid_spec=pltpu.PrefetchScalarGridSpec(
            num_scalar_prefetch=1, grid=(2, nt),
            in_specs=[full, half, half, full, full], out_specs=[full] * 4),
        compiler_params=_params(("arbitrary", "arbitrary")),
    )(place, w, g_mine, g_other, m, v)


def _adam_untiled_rows(w, g_mine, g_other, m, v, place, name):
    rows, _, cols = w.shape
    tc = 256
    nt = cols // 2 // tc
    rb = next(r for r in (206, 128, 103, rows) if rows % r == 0)

    def body(place_ref, w_ref, gm_ref, go_ref, m_ref, v_ref, g_out, d_out, m_out, v_out):
        g = jnp.where(pl.program_id(0) == place_ref[0], gm_ref[...], go_ref[...])
        d, m2, v2 = _adam(None, w_ref[...], g, m_ref[...], v_ref[...])
        g_out[...] = g
        d_out[...] = d
        m_out[...] = m2
        v_out[...] = v2

    full = pl.BlockSpec((rb, 1, tc), lambda hh, i, r, p: (r, 0, hh * nt + i))
    half = pl.BlockSpec((rb, 1, tc), lambda hh, i, r, p: (r, 0, i))
    return pl.pallas_call(
        body, name=name, out_shape=[jax.ShapeDtypeStruct(w.shape, F32)] * 4,
        grid_spec=pltpu.PrefetchScalarGridSpec(
            num_scalar_prefetch=1, grid=(2, nt, rows // rb),
            in_specs=[full, half, half, full, full], out_specs=[full] * 4),
        compiler_params=_params(("arbitrary", "arbitrary", "arbitrary")),
    )(place, w, g_mine, g_other, m, v)


def _pack(arrays):
    flat = []
    for a in arrays:
        a = a.reshape(-1).astype(F32)
        flat.append(jnp.pad(a, (0, (-a.size) % LANES)))
    out = jnp.concatenate(flat)
    out = jnp.pad(out, (0, (-out.size) % (8 * LANES)))
    return out.reshape(-1, LANES)


def _unpack(packed, shapes):
    flat = packed.reshape(-1)
    out, off = [], 0
    for s in shapes:
        size = int(np.prod(s))
        out.append(flat[off:off + size].reshape(s))
        off += size + (-size) % LANES
    return out


def kernel(x, norm1_w, w_in, gdn_conv_w, gdn_A_log, gdn_dt_bias, gdn_out_norm_w, fox_f_bias, fox_q_norm_w, fox_k_norm_w, w_out, norm2_w, w_ffn_gate, w_ffn_up, w_ffn_down, final_norm_w, loss_target, m_norm1_w, m_w_in, m_gdn_conv_w, m_gdn_A_log, m_gdn_dt_bias, m_gdn_out_norm_w, m_fox_f_bias, m_fox_q_norm_w, m_fox_k_norm_w, m_w_out, m_norm2_w, m_w_ffn_gate, m_w_ffn_up, m_w_ffn_down, m_final_norm_w, v_norm1_w, v_w_in, v_gdn_conv_w, v_gdn_A_log, v_gdn_dt_bias, v_gdn_out_norm_w, v_fox_f_bias, v_fox_q_norm_w, v_fox_k_norm_w, v_w_out, v_norm2_w, v_w_ffn_gate, v_w_ffn_up, v_w_ffn_down, v_final_norm_w):
    cx, cy, cc = lax.axis_index("x"), lax.axis_index("y"), lax.axis_index("c")
    own = 2 * cx + cy
    place = jnp.stack([cc, own]).astype(jnp.int32)

    names = ["w_in", "w_out", "w_gate", "w_up", "w_down"]
    is_t = [True, False, True, True, False]
    to_t = lambda a, t: a[0].T if t else a[0]
    from_t = lambda a, t: (a.T if t else a)[None]
    big_w = [to_t(a, t) for a, t in zip([w_in, w_out, w_ffn_gate, w_ffn_up, w_ffn_down], is_t)]
    big_m = [to_t(a, t) for a, t in zip([m_w_in, m_w_out, m_w_ffn_gate, m_w_ffn_up, m_w_ffn_down], is_t)]
    big_v = [to_t(a, t) for a, t in zip([v_w_in, v_w_out, v_w_ffn_gate, v_w_ffn_up, v_w_ffn_down], is_t)]
    shards = [w.astype(BF16) for w in big_w]
    (w_in_g,), conv_g = _allgather_weights(shards[:1], gdn_conv_w[0])
    rest = _split_start("gather_rest_start", _gather_plan, shards[1:],
                        [jax.ShapeDtypeStruct((N_CHIPS,) + s.shape, BF16) for s in shards[1:]],
                        n_copies=4 * len(shards[1:]), after=w_in_g)
    w_cat = _cat_weights(w_in_g.reshape(D_IN, D_MODEL))
    conv_full = conv_g.transpose(1, 0, 2).reshape(CONV_K, 3 * WIDTH)

    def late_weights(after):
        w_out_g, w_gate_g, w_up_g, w_down_g = _split_wait("gather_rest_wait", _gather_plan, rest, after)
        return w_out_g.reshape(D_MODEL, D_MODEL), w_gate_g, w_up_g, w_down_g

    def start_reduction(stacks, nms, tag):
        landed = _swap_halves(stacks, "rs_swap_" + tag)
        added = [_add_half(s, l, place, "rs_add_" + nm) for s, l, nm in zip(stacks, landed, nms)]
        parts = [a[0] for a in added]
        started = _split_start("exchange_" + tag + "_start", _exchange_plan, parts,
                               [jax.ShapeDtypeStruct((3,) + p.shape[1:], p.dtype) for p in parts],
                               n_copies=3 * len(parts), after=parts[0])
        return dict(own=[a[1] for a in added], started=started, tag=tag, names=nms)

    def finish_reduction(red, after, updates):
        landed = _split_wait("exchange_" + red["tag"] + "_wait", _exchange_plan, red["started"], after)
        halves = [_sum_partials(o, p, "rs_sum_" + nm) for o, p, nm in zip(red["own"], landed, red["names"])]
        others = _share_halves(halves, "rs_share_" + red["tag"])
        return [upd(gm, go) for upd, gm, go in zip(updates, halves, others)]

    def transport_update(b):
        def upd(gm, go):
            res = _adam_big(big_w[b], gm, go, big_m[b], big_v[b], place, "adam_" + names[b])
            early_done.append(res[1])
            return [from_t(a, is_t[b]) for a in res]
        return upd

    early_done = []

    def w_in_update(gm, go):
        rows3 = lambda a: jnp.transpose(a, (2, 0, 1))
        res = _adam_untiled_rows(rows3(w_in), gm[:, None, :], go[:, None, :], rows3(m_w_in), rows3(v_w_in),
                                 place, "adam_w_in")
        return [jnp.transpose(a, (1, 2, 0)) for a in res]

    early = {}

    def early_grads_ready(g_out, g_gate, g_up, g_down):
        stacks = [g_out.reshape(N_CHIPS, D_MODEL // N_CHIPS, D_MODEL), g_gate, g_up, g_down]
        early.update(start_reduction(stacks, names[1:], "early"))
        return early["started"]["token"][0, 0]

    grad_x, g_cat, _, _, _, _, small = _local_step(
        x[0], loss_target[0], norm1_w + rest["token"][0, 0], w_cat, conv_full, gdn_A_log[0], gdn_dt_bias[0],
        gdn_out_norm_w[0], fox_f_bias[0], fox_q_norm_w[0], fox_k_norm_w[0], norm2_w, final_norm_w.reshape(1, -1),
        late_weights, early_grads_ready)

    late = start_reduction([_uncat_grad(g_cat).reshape(N_CHIPS, D_IN // N_CHIPS, D_MODEL)], names[:1], "w_in")
    big_upd = finish_reduction(early, late["started"]["token"], [transport_update(b) for b in range(1, 5)])

    order = ["norm1_w", "conv_w", "a_log", "dt_bias", "out_norm_w", "f_bias", "q_norm_w", "k_norm_w",
             "norm2_w", "final_w"]
    red = _allreduce_small(_pack([small[k] for k in order] + [small["loss"]]))
    red_shapes = [(1, D_MODEL), (CONV_K, 3 * WIDTH), (1, HEADS), (1, HEADS), (1, HEAD_DIM), (1, HEADS),
                  (1, HEAD_DIM), (1, HEAD_DIM), (1, D_MODEL), (D_MODEL,), ()]
    red_list = _unpack(red, red_shapes)
    loss = red_list[-1]
    small_g = dict(zip(order, red_list[:-1]))
    shard_cols = 3 * WIDTH // N_CHIPS
    small_g["conv_w"] = lax.dynamic_slice_in_dim(small_g["conv_w"], own * shard_cols, shard_cols, axis=1)[None]
    small_w = [norm1_w, gdn_conv_w, gdn_A_log, gdn_dt_bias, gdn_out_norm_w, fox_f_bias, fox_q_norm_w,
               fox_k_norm_w, norm2_w, final_norm_w]
    small_m = [m_norm1_w, m_gdn_conv_w, m_gdn_A_log, m_gdn_dt_bias, m_gdn_out_norm_w, m_fox_f_bias,
               m_fox_q_norm_w, m_fox_k_norm_w, m_norm2_w, m_final_norm_w]
    small_v = [v_norm1_w, v_gdn_conv_w, v_gdn_A_log, v_gdn_dt_bias, v_gdn_out_norm_w, v_fox_f_bias,
               v_fox_q_norm_w, v_fox_k_norm_w, v_norm2_w, v_final_norm_w]
    small_gl = [small_g[k].reshape(w.shape) for k, w in zip(order, small_w)]
    s_delta, s_m, s_v = _adam_call(_pack(small_w), _pack(small_gl), _pack(small_m), _pack(small_v), "adam_small")
    big_upd = finish_reduction(late, [s_delta] + early_done, [w_in_update]) + big_upd
    shapes = [w.shape for w in small_w]
    s_delta, s_m, s_v = _unpack(s_delta, shapes), _unpack(s_m, shapes), _unpack(s_v, shapes)

    big_pos = {1: 0, 9: 1, 11: 2, 12: 3, 13: 4}
    small_pos = {0: 0, 2: 1, 3: 2, 4: 3, 5: 4, 6: 5, 7: 6, 8: 7, 10: 8, 14: 9}
    grads, deltas, new_m, new_v = [], [], [], []
    for pos in range(15):
        if pos in big_pos:
            b = big_pos[pos]
            g, d, m2, v2 = big_upd[b]
            grads.append(g)
            deltas.append(d)
            new_m.append(m2)
            new_v.append(v2)
        else:
            s = small_pos[pos]
            grads.append(small_gl[s])
            deltas.append(s_delta[s])
            new_m.append(s_m[s])
            new_v.append(s_v[s])
    return (loss, grad_x[None], *grads, *deltas, *new_m, *new_v)
```

```python
import jax
import jax.numpy as jnp
import numpy as np
from jax import lax
from jax.experimental import pallas as pl
from jax.experimental.pallas import tpu as pltpu

F32 = jnp.float32
BF16 = jnp.bfloat16

D_MODEL = 1024
HEADS = 8
HEAD_DIM = 64
PAIRS = HEADS // 2
WIDTH = HEADS * HEAD_DIM
CHUNK = 64
CONV_K = 4
D_FF = 2816
FF_SHARD = D_FF // 4
EPS = 1e-6
SCALE = HEAD_DIM ** -0.5
LANES = 128
N_CHIPS = 4
D_IN = 4120
D_CAT = 4224
COL_SMALL = 4096 // LANES

ADAM_LR = 0.001
ADAM_B1 = 0.9
ADAM_B2 = 0.999
ADAM_EPS = 1e-08
ADAM_WD = 0.01
ADAM_STEP = 10

VMEM_LIMIT = 56 * 1024 * 1024
MESH = pl.DeviceIdType.MESH
HIGHEST = lax.Precision.HIGHEST


def _params(sem):
    return pltpu.CompilerParams(dimension_semantics=sem, vmem_limit_bytes=VMEM_LIMIT)


_CONTRACT = {"nn": ((1,), (0,)), "nt": ((1,), (1,)), "tn": ((0,), (0,))}


def _mm(a, b, *, dims, name, out_dtype=F32, add=None, tm=1024, tn=512, tk=512):
    if dims == "nn":
        (m, k), (k2, n) = a.shape, b.shape
    elif dims == "nt":
        (m, k), (n, k2) = a.shape, b.shape
    else:
        (k, m), (k2, n) = a.shape, b.shape
    assert k == k2, (a.shape, b.shape, dims)
    tm, tn, tk = min(tm, m), min(tn, n), min(tk, k)
    assert m % tm == 0 and n % tn == 0 and k % tk == 0, (m, n, k, tm, tn, tk)
    nk = k // tk
    a_spec = (pl.BlockSpec((tk, tm), lambda i, j, kk: (kk, i)) if dims == "tn"
              else pl.BlockSpec((tm, tk), lambda i, j, kk: (i, kk)))
    b_spec = (pl.BlockSpec((tn, tk), lambda i, j, kk: (j, kk)) if dims == "nt"
              else pl.BlockSpec((tk, tn), lambda i, j, kk: (kk, j)))
    o_spec = pl.BlockSpec((tm, tn), lambda i, j, kk: (i, j))
    contract = (_CONTRACT[dims], ((), ()))
    has_add = add is not None

    def body(*refs):
        a_ref, b_ref = refs[:2]
        add_ref = refs[2] if has_add else None
        o_ref = refs[3] if has_add else refs[2]
        part = lax.dot_general(a_ref[...].astype(BF16), b_ref[...].astype(BF16), contract,
                               preferred_element_type=F32)

        def finish(r):
            if has_add:
                r = r + add_ref[...].astype(F32)
            o_ref[...] = r.astype(out_dtype)

        if nk == 1:
            finish(part)
            return
        acc = refs[-1]
        kk = pl.program_id(2)

        @pl.when(kk == 0)
        def _():
            acc[...] = part

        @pl.when(kk > 0)
        def _():
            acc[...] += part

        @pl.when(kk == nk - 1)
        def _():
            finish(acc[...])

    ins = [a, b] + ([add] if has_add else [])
    in_specs = [a_spec, b_spec] + ([o_spec] if has_add else [])
    return pl.pallas_call(
        body, name=name, grid=(m // tm, n // tn, nk),
        in_specs=in_specs, out_specs=o_spec,
        out_shape=jax.ShapeDtypeStruct((m, n), out_dtype),
        scratch_shapes=[pltpu.VMEM((tm, tn), F32)] if nk > 1 else [],
        compiler_params=_params(("parallel", "parallel", "arbitrary")),
    )(*ins)


def _mm_blocks(a, b, *, name, grid, a_spec, b_spec, o_spec, out_shape, dims, n_sum=0, add=None, add_spec=None):
    contract = (_CONTRACT[dims], ((), ()))
    has_add = add is not None

    def body(*refs):
        a_ref, b_ref = refs[:2]
        o_ref = refs[-1]
        dot = lambda x, y: lax.dot_general(x.astype(BF16), y.astype(BF16), contract, preferred_element_type=F32)
        if n_sum:
            r = dot(a_ref[0], b_ref[0])
            for s in range(1, n_sum):
                r = r + dot(a_ref[s], b_ref[s])
        else:
            r = dot(a_ref[...], b_ref[...])
        if has_add:
            r = r + refs[2][...].astype(F32)
        o_ref[...] = r.astype(o_ref.dtype)

    return pl.pallas_call(
        body, name=name, grid=grid,
        in_specs=[a_spec, b_spec] + ([add_spec] if has_add else []), out_specs=o_spec, out_shape=out_shape,
        compiler_params=_params(("parallel",) * len(grid)),
    )(*([a, b] + ([add] if has_add else [])))


def _tiles(fn, *, name, rows, tm, ncol=1, row_ins=(), col_consts=(), full_consts=(),
           row_outs=(), acc_outs=()):
    nt = rows // tm
    assert rows % tm == 0
    n_full, n_col, n_row = len(full_consts), len(col_consts), len(row_ins)
    n_ro, n_acc = len(row_outs), len(acc_outs)

    def body(*refs):
        ins = refs[:n_full + n_col + n_row]
        outs = refs[n_full + n_col + n_row:]
        i = pl.program_id(1)
        res = fn(pl.program_id(0), *[r[...] for r in ins])
        for r, v in zip(outs[:n_ro], res[:n_ro]):
            r[...] = v.astype(r.dtype)
        if n_acc:
            @pl.when(i == 0)
            def _():
                for r in outs[n_ro:]:
                    r[...] = jnp.zeros_like(r)
            for r, v in zip(outs[n_ro:], res[n_ro:]):
                r[...] += v

    in_specs = [pl.BlockSpec(a.shape, lambda j, i, nd=a.ndim: (0,) * nd) for a in full_consts]
    in_specs += [pl.BlockSpec((nr, w), lambda j, i, o=o: (0, o + j)) for (_, nr, w, o) in col_consts]
    in_specs += [pl.BlockSpec((tm, w), lambda j, i, o=o: (i, o + j)) for (_, w, o) in row_ins]
    out_specs = [pl.BlockSpec((tm, w), lambda j, i: (i, j)) for (w, _) in row_outs]
    out_specs += [pl.BlockSpec((nr, w), lambda j, i: (0, j)) for (nr, w) in acc_outs]
    out_shape = [jax.ShapeDtypeStruct((rows, w * ncol), dt) for (w, dt) in row_outs]
    out_shape += [jax.ShapeDtypeStruct((nr, w * ncol), F32) for (nr, w) in acc_outs]
    args = list(full_consts) + [c[0] for c in col_consts] + [r[0] for r in row_ins]
    out = pl.pallas_call(
        body, name=name, grid=(ncol, nt), in_specs=in_specs, out_specs=out_specs, out_shape=out_shape,
        compiler_params=_params(("parallel", "arbitrary")),
    )(*args)
    return out


def _rms(x, w):
    return x * lax.rsqrt(jnp.mean(x * x, axis=-1, keepdims=True) + EPS) * w


def _lane_lo(shape):
    return lax.broadcasted_iota(jnp.int32, shape, len(shape) - 1) < HEAD_DIM


def _pair_sum(x):
    lo = _lane_lo(x.shape)
    s0 = jnp.sum(jnp.where(lo, x, 0.0), axis=-1, keepdims=True)
    s1 = jnp.sum(jnp.where(lo, 0.0, x), axis=-1, keepdims=True)
    return jnp.where(lo, s0, s1)


def _head_col(x, lo, h):
    keep = lo if h == 0 else jnp.logical_not(lo)
    return jnp.max(jnp.where(keep, x, -jnp.inf), axis=-1, keepdims=True)


def _softplus(x):
    return jnp.maximum(x, 0.0) + jnp.log1p(jnp.exp(-jnp.abs(x)))


def _silu(x):
    return x * jax.nn.sigmoid(x)


def _dot(a, b, contract):
    return lax.dot_general(a.astype(BF16), b.astype(BF16), (contract, ((), ())),
                           preferred_element_type=F32)


def _dot32(a, b, contract):
    return lax.dot_general(a, b, (contract, ((), ())), precision=HIGHEST, preferred_element_type=F32)


def _bd(y):
    yy = jnp.concatenate([y, y], axis=0)
    r = lax.broadcasted_iota(jnp.int32, yy.shape, 0) < HEAD_DIM
    c = lax.broadcasted_iota(jnp.int32, yy.shape, 1) < HEAD_DIM
    return jnp.where(r == c, yy, 0.0)


def _pp(x, y):
    return _dot(x, _bd(y), _CONTRACT["nn"])


def _pp_nt(x, y):
    return _dot(x, _bd(y), _CONTRACT["nt"])


def _pp_tn(x, y):
    full = _dot(x, y, _CONTRACT["tn"])
    return jnp.where(_lane_lo((HEAD_DIM, LANES)), full[:HEAD_DIM], full[HEAD_DIM:])


def _gdn_masks():
    row = lax.broadcasted_iota(jnp.int32, (CHUNK, LANES), 0)
    col = lax.broadcasted_iota(jnp.int32, (CHUNK, LANES), 1) % HEAD_DIM
    return row, col


def _interleave(chains):
    live = list(chains)
    while live:
        for g in list(live):
            try:
                next(g)
            except StopIteration:
                live.remove(g)


def _gdn_forward(qkv, betax, gcx, grow, rows):
    nchunk = rows // CHUNK

    def body(q_ref, k_ref, v_ref, bx_ref, gx_ref, gr_ref, o_ref, ss_ref, ts_ref, state):
        n = pl.program_id(0)

        @pl.when(n == 0)
        def _():
            state[...] = jnp.zeros_like(state)

        row, col = _gdn_masks()
        incl, strict = col <= row, col < row

        def chain(p):
            lanes = pl.ds(p * LANES, LANES)
            q, k, v, bx, gx = q_ref[:, lanes], k_ref[:, lanes], v_ref[:, lanes], bx_ref[:, lanes], gx_ref[:, lanes]
            gr = gr_ref[0, p]
            glast = gx_ref[pl.ds(CHUNK - 1, 1), lanes]
            s = state[p]
            dm = jnp.where(incl, jnp.exp(jnp.minimum(gx - gr, 0.0)), 0.0)
            kb, vb, eg, qs = k * bx, v * bx, jnp.exp(gx), q * SCALE
            yield
            big_g, big_p = _pp_nt(kb, k), _pp_nt(qs, k)
            yield
            x = -jnp.where(strict, big_g * dm, 0.0)
            att = jnp.where(incl, big_p * dm, 0.0)
            tm = jnp.where(row == col, 1.0, 0.0) + x
            x = _pp(x, x)
            yield
            for _ in range(4):
                step, x = _pp(tm, x), _pp(x, x)
                yield
                tm = tm + step
            tm = tm + _pp(tm, x)
            yield
            u, w = _pp(tm, vb), _pp(tm, kb * eg)
            yield
            ws, qgs = _pp(w, s), _pp(qs * eg, s)
            yield
            vn = u - ws
            kd = k * jnp.exp(glast - gx)
            avn, upd = _pp(att, vn), _pp_tn(kd, vn)
            yield
            ss_ref[0, p] = s
            ts_ref[0, p] = tm
            o_ref[:, lanes] = qgs + avn
            state[p] = s * jnp.exp(glast) + upd

        _interleave([chain(p) for p in range(PAIRS)])

    blk = lambda j: pl.BlockSpec((CHUNK, WIDTH), lambda n, j=j: (n, j))
    sv = pl.BlockSpec((1, PAIRS, CHUNK, LANES), lambda n: (n, 0, 0, 0))
    return pl.pallas_call(
        body, name="gdn_fwd", grid=(nchunk,),
        in_specs=[blk(0), blk(1), blk(2), blk(0), blk(0),
                  pl.BlockSpec((1, PAIRS, 1, LANES), lambda n: (n, 0, 0, 0))],
        out_specs=[blk(0), sv, sv],
        out_shape=[jax.ShapeDtypeStruct((rows, WIDTH), F32),
                   jax.ShapeDtypeStruct((nchunk, PAIRS, CHUNK, LANES), F32),
                   jax.ShapeDtypeStruct((nchunk, PAIRS, CHUNK, LANES), F32)],
        scratch_shapes=[pltpu.VMEM((PAIRS, CHUNK, LANES), F32)],
        compiler_params=_params(("arbitrary",)),
    )(qkv, qkv, qkv, betax, gcx, grow)


def _gdn_backward(qkv, betax, gcx, grow, ssave, tsave, do, rows):
    nchunk = rows // CHUNK

    def body(q_ref, k_ref, v_ref, bx_ref, gx_ref, gr_ref, ss_ref, ts_ref, do_ref,
             dq_ref, dk_ref, dv_ref, dbx_ref, dgx_ref, dgr_ref, dstate):
        n = pl.program_id(0)

        @pl.when(n == 0)
        def _():
            dstate[...] = jnp.zeros_like(dstate)

        row, col = _gdn_masks()
        incl, strict = col <= row, col < row

        def chain(p):
            lanes = pl.ds(p * LANES, LANES)
            q, k, v, bx, gx = q_ref[:, lanes], k_ref[:, lanes], v_ref[:, lanes], bx_ref[:, lanes], gx_ref[:, lanes]
            gr = gr_ref[0, p]
            glast = gx_ref[pl.ds(CHUNK - 1, 1), lanes]
            s, tm, d_o = ss_ref[0, p], ts_ref[0, p], do_ref[:, lanes]
            ds_out = dstate[p]
            dm = jnp.where(incl, jnp.exp(jnp.minimum(gx - gr, 0.0)), 0.0)
            kb, vb, eg, qs = k * bx, v * bx, jnp.exp(gx), q * SCALE
            kbg, qg = kb * eg, qs * eg
            ed = jnp.exp(glast - gx)
            kd = k * ed
            eglast = jnp.exp(glast)
            yield
            big_g, big_p = _pp_nt(kb, k), _pp_nt(qs, k)
            u, w = _pp(tm, vb), _pp(tm, kbg)
            dqg, kds = _pp_nt(d_o, s), _pp(kd, ds_out)
            yield
            low = jnp.where(strict, big_g * dm, 0.0)
            att = jnp.where(incl, big_p * dm, 0.0)
            ws, atd = _pp(w, s), _pp_tn(att, d_o)
            yield
            vn = u - ws
            dvn = kds + atd
            dkd, datt_raw = _pp_nt(vn, ds_out), _pp_nt(d_o, vn)
            dw_neg, dvb = _pp_nt(dvn, s), _pp_tn(tm, dvn)
            dtm_a, wdv = _pp_nt(dvn, vb), _pp_tn(w, dvn)
            qgd = _pp_tn(qg, d_o)
            yield
            datt = jnp.where(incl, datt_raw, 0.0)
            dw = -dw_neg
            dtm_b, dkbg = _pp_nt(dw, kbg), _pp_tn(tm, dw)
            dbig_p = datt * dm
            dqs_a, dk_p = _pp(dbig_p, k), _pp_tn(dbig_p, qs)
            yield
            inner = _pp_tn(tm, dtm_a + dtm_b)
            yield
            dlow = jnp.where(strict, -_pp_nt(inner, tm), 0.0)
            yield
            dbig_g = dlow * dm
            dkb_a, dk_g = _pp(dbig_g, k), _pp_tn(dbig_g, kb)
            yield
            dkb = dkb_a + dkbg * eg
            dqs = dqs_a + dqg * eg
            dk = dk_g + dk_p + dkd * ed + dkb * bx
            z = dlow * low + datt * att
            kdterm = dkd * kd
            dglast = (jnp.sum(ds_out * s, axis=0, keepdims=True) * eglast
                      + jnp.sum(kdterm, axis=0, keepdims=True))
            dgx = dqg * qg + dkbg * kbg - kdterm
            dgx = dgx + jnp.where(col == 0, _pair_sum(z), 0.0)
            dgx = dgx + jnp.where(row == CHUNK - 1, dglast, 0.0)
            dq_ref[:, lanes] = dqs * SCALE
            dk_ref[:, lanes] = dk
            dv_ref[:, lanes] = dvb * bx
            dbx_ref[:, lanes] = dkb * k + dvb * v
            dgx_ref[:, lanes] = dgx
            dgr_ref[0, p] = -jnp.sum(z, axis=0, keepdims=True)
            dstate[p] = ds_out * eglast + qgd - wdv

        _interleave([chain(p) for p in range(PAIRS)])

    last = nchunk - 1
    blk = lambda j: pl.BlockSpec((CHUNK, WIDTH), lambda n, j=j: (last - n, j))
    sv = pl.BlockSpec((1, PAIRS, CHUNK, LANES), lambda n: (last - n, 0, 0, 0))
    gr_spec = pl.BlockSpec((1, PAIRS, 1, LANES), lambda n: (last - n, 0, 0, 0))
    wide = jax.ShapeDtypeStruct((rows, WIDTH), F32)
    return pl.pallas_call(
        body, name="gdn_bwd", grid=(nchunk,),
        in_specs=[blk(0), blk(1), blk(2), blk(0), blk(0), gr_spec, sv, sv, blk(0)],
        out_specs=[blk(0)] * 5 + [gr_spec],
        out_shape=[wide] * 5 + [jax.ShapeDtypeStruct((nchunk, PAIRS, 1, LANES), F32)],
        scratch_shapes=[pltpu.VMEM((PAIRS, CHUNK, LANES), F32)],
        compiler_params=_params(("arbitrary",)),
    )(qkv, qkv, qkv, betax, gcx, grow, ssave, tsave, do)


ATT_TQ = 256


def _att_scores(qh, kt, fk, diag):
    s = _dot(qh, kt, _CONTRACT["nt"]) - fk
    if diag:
        r = lax.broadcasted_iota(jnp.int32, s.shape, 0)
        c = lax.broadcasted_iota(jnp.int32, s.shape, 1)
        s = jnp.where(r >= c, s, -jnp.inf)
    return s


def _head_masks(n):
    lo = _lane_lo((n, LANES))
    return [lo, jnp.logical_not(lo)]


def _attention_forward(fqk, proj, frow, rows):
    tq = tk = min(ATT_TQ, rows)
    nq = rows // tq
    v_off = 3072 // LANES

    def body(q_ref, k_ref, v_ref, fr_ref, o_ref, lse_ref):
        qi = pl.program_id(1)
        q = q_ref[...] * SCALE
        keep_q, keep_k = _head_masks(tq), _head_masks(tk)
        qh = [jnp.where(keep_q[h], q, 0.0).astype(BF16) for h in range(2)]

        def tile(ki, carry, diag):
            k0 = pl.multiple_of(ki * tk, tk)
            kt = k_ref[pl.ds(k0, tk), :].astype(BF16)
            v_t = v_ref[pl.ds(k0, tk), :]
            out = [None, None]

            def chain(h):
                m, l, acc = carry[h]
                vt = jnp.where(keep_k[h], v_t, 0.0).astype(BF16)
                yield
                s = _att_scores(qh[h], kt, fr_ref[0, pl.ds(h, 1), pl.ds(k0, tk)], diag)
                yield
                m_new = jnp.maximum(m, jnp.max(s, axis=-1, keepdims=True))
                p = jnp.exp(s - m_new)
                alpha = jnp.exp(m - m_new)
                l = alpha * l + jnp.sum(p, axis=-1, keepdims=True)
                p_hi = p.astype(BF16)
                p_lo = p - p_hi.astype(F32)
                yield
                out[h] = (m_new, l, alpha * acc + _dot(p_hi, vt, _CONTRACT["nn"]) + _dot(p_lo, vt, _CONTRACT["nn"]))

            _interleave([chain(0), chain(1)])
            return tuple(out)

        one = (jnp.full((tq, 1), -jnp.inf, F32), jnp.zeros((tq, 1), F32), jnp.zeros((tq, LANES), F32))
        carry = lax.fori_loop(0, qi, lambda ki, c: tile(ki, c, False), (one, one))
        (m0, l0, acc0), (m1, l1, acc1) = tile(qi, carry, True)
        o_ref[...] = acc0 / l0 + acc1 / l1
        lse_ref[...] = jnp.where(keep_q[0], m0 + jnp.log(l0), m1 + jnp.log(l1))

    whole = lambda off: pl.BlockSpec((rows, LANES), lambda p, i, off=off: (0, off + p))
    qblk = lambda off: pl.BlockSpec((tq, LANES), lambda p, i, off=off: (i, off + p))
    wide = jax.ShapeDtypeStruct((rows, WIDTH), F32)
    return pl.pallas_call(
        body, name="fox_fwd", grid=(PAIRS, nq),
        in_specs=[qblk(0), whole(PAIRS), whole(v_off), pl.BlockSpec((1, 2, rows), lambda p, i: (p, 0, 0))],
        out_specs=[qblk(0), qblk(0)], out_shape=[wide, wide],
        compiler_params=_params(("parallel", "arbitrary")),
    )(fqk, fqk, proj, frow)


def _attention_delta(fqk, proj, frow, lse, dao, rows):
    tq = tk = min(ATT_TQ, rows)
    nq = rows // tq
    v_off = 3072 // LANES

    def body(q_ref, k_ref, v_ref, fr_ref, lse_ref, do_ref, delta_ref):
        qi = pl.program_id(1)
        q, d_o, lse_t = q_ref[...] * SCALE, do_ref[...], lse_ref[...]
        keep_q = _head_masks(tq)
        qh = [jnp.where(keep_q[h], q, 0.0).astype(BF16) for h in range(2)]
        doh = [jnp.where(keep_q[h], d_o, 0.0).astype(BF16) for h in range(2)]
        lse_h = [_head_col(lse_t, keep_q[0], h) for h in range(2)]

        def tile(ki, carry, diag):
            k0 = pl.multiple_of(ki * tk, tk)
            kt = k_ref[pl.ds(k0, tk), :].astype(BF16)
            vt = v_ref[pl.ds(k0, tk), :].astype(BF16)
            out = [None, None]

            def chain(h):
                s = _att_scores(qh[h], kt, fr_ref[0, pl.ds(h, 1), pl.ds(k0, tk)], diag)
                dp = _dot(doh[h], vt, _CONTRACT["nt"])
                yield
                out[h] = carry[h] + jnp.sum(jnp.exp(s - lse_h[h]) * dp, axis=-1, keepdims=True)

            _interleave([chain(0), chain(1)])
            return tuple(out)

        zero = jnp.zeros((tq, 1), F32)
        carry = lax.fori_loop(0, qi, lambda ki, c: tile(ki, c, False), (zero, zero))
        d0, d1 = tile(qi, carry, True)
        delta_ref[...] = jnp.where(keep_q[0], d0, d1)

    whole = lambda off: pl.BlockSpec((rows, LANES), lambda p, i, off=off: (0, off + p))
    qblk = lambda off: pl.BlockSpec((tq, LANES), lambda p, i, off=off: (i, off + p))
    return pl.pallas_call(
        body, name="fox_delta", grid=(PAIRS, nq),
        in_specs=[qblk(0), whole(PAIRS), whole(v_off),
                  pl.BlockSpec((1, 2, rows), lambda p, i: (p, 0, 0)), qblk(0), qblk(0)],
        out_specs=qblk(0), out_shape=jax.ShapeDtypeStruct((rows, WIDTH), F32),
        compiler_params=_params(("parallel", "arbitrary")),
    )(fqk, fqk, proj, frow, lse, dao)


def _attention_backward(fqk, proj, frow, ao, lse, dao, rows):
    tq = tk = min(ATT_TQ, rows)
    nq = rows // tq
    v_off = 3072 // LANES

    def body(q_ref, k_ref, v_ref, fr_ref, o_ref, lse_ref, do_ref, dq_ref, dk_ref, dv_ref, dfr_ref):
        ki = pl.program_id(1)

        @pl.when(ki == 0)
        def _():
            dq_ref[...] = jnp.zeros_like(dq_ref)

        keep_q, keep_k = _head_masks(tq), _head_masks(tk)
        k_t = k_ref[...]
        kt = k_t.astype(BF16)
        vt = v_ref[...].astype(BF16)
        kh = [jnp.where(keep_k[h], k_t, 0.0).astype(BF16) for h in range(2)]
        fk = [fr_ref[0, pl.ds(h, 1), :] for h in range(2)]

        def tile(qi, carry, diag):
            dk, dv, df0, df1 = carry
            rows_q = pl.ds(pl.multiple_of(qi * tq, tq), tq)
            q, d_o, lse_t = q_ref[rows_q, :] * SCALE, do_ref[rows_q, :], lse_ref[rows_q, :]
            delta_x = _pair_sum(d_o.astype(BF16).astype(F32) * o_ref[rows_q, :])
            res = [None, None]

            def chain(h):
                qh = jnp.where(keep_q[h], q, 0.0).astype(BF16)
                doh = jnp.where(keep_q[h], d_o, 0.0).astype(BF16)
                lse_h, delta_h = _head_col(lse_t, keep_q[0], h), _head_col(delta_x, keep_q[0], h)
                yield
                s, dp = _att_scores(qh, kt, fk[h], diag), _dot(doh, vt, _CONTRACT["nt"])
                yield
                p = jnp.exp(s - lse_h)
                ds = p * (dp - delta_h)
                yield
                res[h] = (_dot(p, doh, _CONTRACT["tn"]), _dot(ds, qh, _CONTRACT["tn"]),
                          _dot(ds, kh[h], _CONTRACT["nn"]), jnp.sum(ds, axis=0, keepdims=True))

            _interleave([chain(0), chain(1)])
            (dv0, dk0, dq0, s0), (dv1, dk1, dq1, s1) = res
            dq_ref[rows_q, :] += (dq0 + dq1) * SCALE
            return dk + dk0 + dk1, dv + dv0 + dv1, df0 - s0, df1 - s1

        zero_kv = jnp.zeros((tk, LANES), F32)
        zero_f = jnp.zeros((1, tk), F32)
        carry = tile(ki, (zero_kv, zero_kv, zero_f, zero_f), True)
        dk, dv, df0, df1 = lax.fori_loop(ki + 1, nq, lambda qi, c: tile(qi, c, False), carry)
        dk_ref[...] = dk
        dv_ref[...] = dv.astype(dv_ref.dtype)
        dfr_ref[0, pl.ds(0, 1), :] = df0
        dfr_ref[0, pl.ds(1, 1), :] = df1

    whole = lambda off: pl.BlockSpec((rows, LANES), lambda p, i, off=off: (0, off + p))
    kblk = lambda off: pl.BlockSpec((tk, LANES), lambda p, i, off=off: (i, off + p))
    fr_spec = pl.BlockSpec((1, 2, tk), lambda p, i: (p, 0, i))
    wide = jax.ShapeDtypeStruct((rows, WIDTH), F32)
    return pl.pallas_call(
        body, name="fox_bwd", grid=(PAIRS, nq),
        in_specs=[whole(0), kblk(PAIRS), kblk(v_off), fr_spec, whole(0), whole(0), whole(0)],
        out_specs=[whole(0), kblk(0), kblk(0), fr_spec],
        out_shape=[wide, wide, jax.ShapeDtypeStruct((rows, WIDTH), BF16),
                   jax.ShapeDtypeStruct((PAIRS, 2, rows), F32)],
        compiler_params=_params(("parallel", "arbitrary")),
    )(fqk, fqk, proj, frow, ao, lse, dao)


def _lane_ids(shape):
    return lax.broadcasted_iota(jnp.int32, shape, len(shape) - 1)


def _gates_elem(a_log, dt_bias, f_bias, pre):
    lane = _lane_ids(pre.shape)
    beta = jax.nn.sigmoid(pre)
    g = -jnp.exp(a_log) * _softplus(pre + dt_bias)
    lf = -_softplus(-(pre + f_bias))
    return jnp.where(lane < 8, beta, jnp.where(lane < 16, g, jnp.where(lane < 24, lf, 0.0)))


def _tri_consts():
    r = np.arange(LANES)[:, None]
    c = np.arange(LANES)[None, :]
    full = (c <= r).astype(np.float32)
    chunked = full * ((r // CHUNK) == (c // CHUNK))
    return jnp.asarray(chunked), jnp.asarray(full)


def _cums_fwd(lc, lf, gates):
    rows = gates.shape[0]
    lane = _lane_ids((LANES, LANES))
    carry = jnp.zeros((1, LANES), F32)
    out = []
    for r in range(rows // LANES):
        blk = gates[r * LANES:(r + 1) * LANES]
        gc = _dot32(lc, blk, _CONTRACT["nn"])
        f = _dot32(lf, blk, _CONTRACT["nn"]) + carry
        carry = carry + jnp.sum(blk, axis=0, keepdims=True)
        out.append(jnp.where((lane >= 8) & (lane < 16), gc, jnp.where((lane >= 16) & (lane < 24), f, 0.0)))
    return jnp.concatenate(out, axis=0)


def _cums_bwd(lc, lf, dcums):
    rows = dcums.shape[0]
    lane = _lane_ids((LANES, LANES))
    is_g = (lane >= 8) & (lane < 16)
    is_f = (lane >= 16) & (lane < 24)
    carry = jnp.zeros((1, LANES), F32)
    out = [None] * (rows // LANES)
    for r in reversed(range(rows // LANES)):
        blk = dcums[r * LANES:(r + 1) * LANES]
        dg = jnp.where(is_g, blk, 0.0)
        df = jnp.where(is_f, blk, 0.0)
        out[r] = _dot32(lc, dg, _CONTRACT["tn"]) + _dot32(lf, df, _CONTRACT["tn"]) + carry
        carry = carry + jnp.sum(df, axis=0, keepdims=True)
    return jnp.concatenate(out, axis=0)


def _expand_consts():
    xb = np.zeros((LANES, WIDTH), np.float32)
    xg = np.zeros((LANES, WIDTH), np.float32)
    for h in range(HEADS):
        xb[h, h * HEAD_DIM:(h + 1) * HEAD_DIM] = 1.0
        xg[8 + h, h * HEAD_DIM:(h + 1) * HEAD_DIM] = 1.0
    return jnp.asarray(xb), jnp.asarray(xg)


def _shift_down(x, s):
    if s == 0:
        return x
    row = lax.broadcasted_iota(jnp.int32, x.shape, 0)
    return jnp.where(row >= s, pltpu.roll(x, s, 0), 0.0)


def _shift_up(x, s):
    if s == 0:
        return x
    n = x.shape[0]
    row = lax.broadcasted_iota(jnp.int32, x.shape, 0)
    return jnp.where(row < n - s, pltpu.roll(x, n - s, 0), 0.0)


def _row_of(cw, i):
    row = lax.broadcasted_iota(jnp.int32, cw.shape, 0)
    return jnp.sum(jnp.where(row == i, cw, 0.0), axis=0, keepdims=True)


def _conv(cw, x):
    c = jnp.zeros_like(x)
    for i in range(CONV_K):
        c = c + _row_of(cw, i) * _shift_down(x, CONV_K - 1 - i)
    return c


def _post_conv(is_qk, c):
    s = _silu(c)
    n = s * lax.rsqrt(_pair_sum(s * s) + EPS)
    return jnp.where(is_qk, n, s)


def _gdn_prep_fwd(col, cw, x):
    return (_post_conv(col < 2 * PAIRS, _conv(cw, x)),)


def _gdn_prep_bwd(is_qk, cw, x, dy):
    c = _conv(cw, x)
    _, vjp = jax.vjp(lambda cc: _post_conv(is_qk, cc), c)
    (dc,) = vjp(dy)
    dx = jnp.zeros_like(x)
    row = lax.broadcasted_iota(jnp.int32, cw.shape, 0)
    dcw = jnp.zeros(cw.shape, F32)
    for i in range(CONV_K):
        s = CONV_K - 1 - i
        dx = dx + _row_of(cw, i) * _shift_up(dc, s)
        dcw = dcw + jnp.where(row == i, jnp.sum(dc * _shift_down(x, s), axis=0, keepdims=True), 0.0)
    return dx, dcw


def _head_rms(w, x):
    return x * lax.rsqrt(_pair_sum(x * x) / HEAD_DIM + EPS) * w


def _cat_weights(w_in_t):
    tail = jnp.pad(w_in_t[4112:4120], ((0, D_CAT - D_IN), (0, 0)))
    return jnp.concatenate([w_in_t[:2048], w_in_t[2064:4112], w_in_t[2048:2064], tail], axis=0)


def _uncat_grad(g):
    return jnp.concatenate([g[:2048], g[4096:4112], g[2048:4096], g[4112:4120]], axis=0)


def _lanes_to_rowform(v8, rows):
    return v8.reshape(rows // CHUNK, CHUNK, HEADS).transpose(0, 2, 1).reshape(rows // CHUNK, PAIRS, 1, LANES)


def _rowform_to_lanes(v, rows):
    return v.reshape(rows // CHUNK, HEADS, CHUNK).transpose(0, 2, 1).reshape(rows, HEADS)


def _local_step(x, target, norm1_w, a_log, dt_bias, out_norm_w, f_bias, q_norm_w, k_norm_w,
                norm2_w, final_w, first_weights, late_weights, early_grads_ready):
    rows = x.shape[0]
    tm = min(512, rows)
    lc, lf = _tri_consts()
    xb, xg = _expand_consts()

    (h1,) = _tiles(lambda col, w, xx: (_rms(xx, w),), name="norm1", rows=rows, tm=tm,
                   full_consts=[norm1_w], row_ins=[(x, D_MODEL, 0)], row_outs=[(D_MODEL, BF16)])
    w_cat, conv_w = first_weights(h1)
    proj = _mm(h1, w_cat, dims="nt", name="in_proj", tn=384, tk=1024)

    lane_pad = lambda v, off: jnp.pad(v.reshape(1, -1), ((0, 0), (off, LANES - off - v.size)))
    p_a, p_dt, p_fb = lane_pad(a_log, 8), lane_pad(dt_bias, 8), lane_pad(f_bias, 16)

    def gates_fwd(col, lcv, lfv, a, dt, fb, pre):
        gates = _gates_elem(a, dt, fb, pre)
        return gates, _cums_fwd(lcv, lfv, gates)

    gates, cums = _tiles(gates_fwd, name="gates", rows=rows, tm=rows,
                         full_consts=[lc, lf, p_a, p_dt, p_fb], row_ins=[(proj, LANES, COL_SMALL)],
                         row_outs=[(LANES, F32), (LANES, F32)])

    def expand_fwd(col, b, g, gt, cm):
        return (_dot32(gt, b, _CONTRACT["nn"]), _dot32(cm, g, _CONTRACT["nn"]))

    betax, gcx = _tiles(expand_fwd, name="expand", rows=rows, tm=tm, full_consts=[xb, xg],
                        row_ins=[(gates, LANES, 0), (cums, LANES, 0)],
                        row_outs=[(WIDTH, F32)] * 2)
    grow = _lanes_to_rowform(cums[:, 8:16], rows)
    frow = cums[:, 16:24].T.reshape(PAIRS, 2, rows)

    (qkv,) = _tiles(_gdn_prep_fwd, name="gdn_prep", rows=rows, tm=rows, ncol=3 * PAIRS,
                    col_consts=[(conv_w, CONV_K, LANES, 0)], row_ins=[(proj, LANES, 0)],
                    row_outs=[(LANES, F32)])
    o_gdn, ssave, tsave = _gdn_forward(qkv, betax, gcx, grow, rows)

    w_qk = jnp.concatenate([jnp.tile(q_norm_w.reshape(1, -1), (1, HEADS)),
                            jnp.tile(k_norm_w.reshape(1, -1), (1, HEADS))], axis=1)
    fox_off = 2048 // LANES
    (fqk,) = _tiles(lambda col, w, xx: (_head_rms(w, xx),), name="fox_prep", rows=rows, tm=rows, ncol=2 * PAIRS,
                    col_consts=[(w_qk, 1, LANES, 0)], row_ins=[(proj, LANES, fox_off)],
                    row_outs=[(LANES, F32)])
    ao, lse = _attention_forward(fqk, proj, frow, rows)

    w_on = jnp.tile(out_norm_w.reshape(1, -1), (1, 2))
    z_off, fg_off = 1536 // LANES, 3584 // LANES
    mix_g_fn = lambda w, o, z: _head_rms(w, o) * _silu(z)
    mix_f_fn = lambda a, g: a * jax.nn.sigmoid(g)
    (mix_g,) = _tiles(lambda col, w, o, z: (mix_g_fn(w, o, z),), name="mix_gdn", rows=rows, tm=rows, ncol=PAIRS,
                      full_consts=[w_on], row_ins=[(o_gdn, LANES, 0), (proj, LANES, z_off)],
                      row_outs=[(LANES, BF16)])
    (mix_f,) = _tiles(lambda col, a, g: (mix_f_fn(a, g),), name="mix_fox", rows=rows, tm=rows, ncol=PAIRS,
                      row_ins=[(ao, LANES, 0), (proj, LANES, fg_off)], row_outs=[(LANES, BF16)])
    mix = jnp.concatenate([mix_g, mix_f], axis=1)
    w_out, w_gate, w_up, w_down = late_weights(mix)
    x1 = _mm(mix, w_out, dims="nn", name="out_proj", add=x, tk=1024)

    (h2,) = _tiles(lambda col, w, xx: (_rms(xx, w),), name="norm2", rows=rows, tm=tm,
                   full_consts=[norm2_w], row_ins=[(x1, D_MODEL, 0)], row_outs=[(D_MODEL, BF16)])
    t_rows, t_cols, t_act = min(1024, rows), 512, min(512, rows)
    n_rt = rows // t_rows
    st_act = jax.ShapeDtypeStruct((N_CHIPS, rows, FF_SHARD), BF16)
    st_rows = pl.BlockSpec((None, t_rows, FF_SHARD), lambda i, j: (j, i, 0))
    out_rows = pl.BlockSpec((t_rows, t_cols), lambda i, n: (i, n))
    flat = lambda t: t.reshape(N_CHIPS * rows, FF_SHARD)

    def ffn_in(w_st, name):
        return _mm_blocks(h2, w_st, name=name, grid=(n_rt, N_CHIPS), dims="nt",
                          a_spec=pl.BlockSpec((t_rows, D_MODEL), lambda i, j: (i, 0)),
                          b_spec=pl.BlockSpec((None, FF_SHARD, D_MODEL), lambda i, j: (j, 0, 0)),
                          o_spec=st_rows, out_shape=st_act)

    gate, up = ffn_in(w_gate, "ffn_gate"), ffn_in(w_up, "ffn_up")
    act_fn = lambda g, u: _silu(g.astype(F32)) * u.astype(F32)
    (act,) = _tiles(lambda col, g, u: (act_fn(g, u),), name="ffn_act", rows=N_CHIPS * rows, tm=t_act,
                    row_ins=[(flat(gate), FF_SHARD, 0), (flat(up), FF_SHARD, 0)], row_outs=[(FF_SHARD, BF16)])
    act = act.reshape(st_act.shape)
    x2 = _mm_blocks(act, w_down, name="ffn_down", grid=(n_rt, D_MODEL // t_cols), dims="nn", n_sum=N_CHIPS,
                    a_spec=pl.BlockSpec((N_CHIPS, t_rows, FF_SHARD), lambda i, n: (0, i, 0)),
                    b_spec=pl.BlockSpec((N_CHIPS, FF_SHARD, t_cols), lambda i, n: (0, 0, n)),
                    o_spec=out_rows, out_shape=jax.ShapeDtypeStruct((rows, D_MODEL), F32),
                    add=x1, add_spec=out_rows)

    def final_fn(col, w, xx, tgt):
        y, vjp = jax.vjp(_rms, xx, w)
        err = y - tgt
        loss = 0.5 * jnp.sum(err * err) / D_MODEL
        dx, dw = vjp(err / D_MODEL)
        return dx, dx, jnp.full((1, LANES), loss, F32), dw

    dx2, dx2_b, loss, d_final_w = _tiles(final_fn, name="final_loss", rows=rows, tm=tm, full_consts=[final_w],
                                         row_ins=[(x2, D_MODEL, 0), (target, D_MODEL, 0)],
                                         row_outs=[(D_MODEL, F32), (D_MODEL, BF16)],
                                         acc_outs=[(1, LANES), (1, D_MODEL)])

    dact = _mm_blocks(dx2_b, w_down, name="d_act", grid=(n_rt, N_CHIPS), dims="nt",
                      a_spec=pl.BlockSpec((t_rows, D_MODEL), lambda i, j: (i, 0)),
                      b_spec=pl.BlockSpec((None, FF_SHARD, D_MODEL), lambda i, j: (j, 0, 0)),
                      o_spec=st_rows, out_shape=st_act)
    def g_ffn(d_st, other, name):
        return _mm_blocks(d_st, other, name=name, grid=(N_CHIPS, D_MODEL // t_cols), dims="tn",
                          a_spec=pl.BlockSpec((None, rows, FF_SHARD), lambda j, n: (j, 0, 0)),
                          b_spec=pl.BlockSpec((rows, t_cols), lambda j, n: (0, n)),
                          o_spec=pl.BlockSpec((None, FF_SHARD, t_cols), lambda j, n: (j, 0, n)),
                          out_shape=jax.ShapeDtypeStruct((N_CHIPS, FF_SHARD, D_MODEL), BF16))

    g_down = g_ffn(act, dx2_b, "g_down")

    def act_bwd(col, g, u, d):
        _, vjp = jax.vjp(lambda gg, uu: _silu(gg) * uu, g.astype(F32), u.astype(F32))
        return vjp(d.astype(F32))

    dgate, dup = _tiles(act_bwd, name="ffn_act_bwd", rows=N_CHIPS * rows, tm=t_act,
                        row_ins=[(flat(gate), FF_SHARD, 0), (flat(up), FF_SHARD, 0), (flat(dact), FF_SHARD, 0)],
                        row_outs=[(FF_SHARD, BF16), (FF_SHARD, BF16)])
    dgate, dup = dgate.reshape(st_act.shape), dup.reshape(st_act.shape)

    def d_h2(d_st, w_st, name, add):
        return _mm_blocks(d_st, w_st, name=name, grid=(n_rt, D_MODEL // t_cols), dims="nn", n_sum=N_CHIPS,
                          a_spec=pl.BlockSpec((N_CHIPS, t_rows, FF_SHARD), lambda i, n: (0, i, 0)),
                          b_spec=pl.BlockSpec((N_CHIPS, FF_SHARD, t_cols), lambda i, n: (0, 0, n)),
                          o_spec=out_rows, out_shape=jax.ShapeDtypeStruct((rows, D_MODEL), F32),
                          add=add, add_spec=out_rows)

    dh2 = d_h2(dup, w_up, "d_h2_up", d_h2(dgate, w_gate, "d_h2_gate", None))
    g_gate, g_up = g_ffn(dgate, h2, "g_gate"), g_ffn(dup, h2, "g_up")

    def norm_bwd(col, w, xx, dh, dres):
        _, vjp = jax.vjp(_rms, xx, w)
        dx, dw = vjp(dh)
        return dx + dres, dx + dres, dw

    dx1, dx1_b, d_norm2_w = _tiles(norm_bwd, name="norm2_bwd", rows=rows, tm=tm, full_consts=[norm2_w],
                                   row_ins=[(x1, D_MODEL, 0), (dh2, D_MODEL, 0), (dx2, D_MODEL, 0)],
                                   row_outs=[(D_MODEL, F32), (D_MODEL, BF16)], acc_outs=[(1, D_MODEL)])
    dmix = _mm(dx1_b, w_out, dims="nt", name="d_mix", tk=1024)
    g_out = _mm(mix, dx1_b, dims="tn", name="g_out", tk=rows, out_dtype=BF16)
    w_on = w_on + early_grads_ready(g_out, g_gate, g_up, g_down)

    def mix_g_bwd(col, w, o, z, d):
        _, vjp = jax.vjp(mix_g_fn, w, o, z)
        dw, do_, dz = vjp(d)
        return do_, dz, dw

    do_gdn, dz, d_on = _tiles(mix_g_bwd, name="mix_gdn_bwd", rows=rows, tm=rows, ncol=PAIRS, full_consts=[w_on],
                              row_ins=[(o_gdn, LANES, 0), (proj, LANES, z_off), (dmix, LANES, 0)],
                              row_outs=[(LANES, F32), (LANES, BF16)], acc_outs=[(1, LANES)])

    def mix_f_bwd(col, a, g, d):
        _, vjp = jax.vjp(mix_f_fn, a, g)
        return vjp(d)

    dao, dfgate = _tiles(mix_f_bwd, name="mix_fox_bwd", rows=rows, tm=rows, ncol=PAIRS,
                         row_ins=[(ao, LANES, 0), (proj, LANES, fg_off), (dmix, LANES, PAIRS)],
                         row_outs=[(LANES, F32), (LANES, BF16)])

    dfq, dfk, dfv, dfrow = _attention_backward(fqk, proj, frow, ao, lse, dao, rows)

    def fox_prep_bwd(col, w, xx, d):
        _, vjp = jax.vjp(_head_rms, w, xx)
        dw, dx = vjp(d)
        return dx, dw

    dfqk, d_wqk = [], []
    for part, d_n in enumerate((dfq, dfk)):
        dx_p, dw_p = _tiles(fox_prep_bwd, name="fox_prep_bwd_" + "qk"[part], rows=rows, tm=rows, ncol=PAIRS,
                            col_consts=[(w_qk, 1, LANES, part * PAIRS)],
                            row_ins=[(proj, LANES, fox_off + part * PAIRS), (d_n, LANES, 0)],
                            row_outs=[(LANES, BF16)], acc_outs=[(1, LANES)])
        dfqk.append(dx_p)
        d_wqk.append(dw_p)

    dq, dk, dv, dbetax, dgcx, dgrow = _gdn_backward(qkv, betax, gcx, grow, ssave, tsave, do_gdn, rows)
    dqkv, d_conv = [], []
    for part, d_n in enumerate((dq, dk, dv)):
        prep_bwd = lambda col, cw, xx, dy, is_qk=(part < 2): _gdn_prep_bwd(is_qk, cw, xx, dy)
        dx_p, dw_p = _tiles(prep_bwd, name="gdn_prep_bwd_" + "qkv"[part], rows=rows, tm=rows, ncol=PAIRS,
                            col_consts=[(conv_w, CONV_K, LANES, part * PAIRS)],
                            row_ins=[(proj, LANES, part * PAIRS), (d_n, LANES, 0)],
                            row_outs=[(LANES, BF16)], acc_outs=[(CONV_K, LANES)])
        dqkv.append(dx_p)
        d_conv.append(dw_p)
    d_conv = jnp.concatenate(d_conv, axis=1)

    def expand_bwd(col, b, g, db, dg):
        return (_dot32(db, b, _CONTRACT["nt"]), _dot32(dg, g, _CONTRACT["nt"]))

    dgates_b, dcums_g = _tiles(expand_bwd, name="expand_bwd", rows=rows, tm=tm, full_consts=[xb, xg],
                               row_ins=[(dbetax, WIDTH, 0), (dgcx, WIDTH, 0)],
                               row_outs=[(LANES, F32), (LANES, F32)])
    dcums_row = jnp.concatenate([jnp.zeros((rows, 8), F32), _rowform_to_lanes(dgrow, rows),
                                 dfrow.reshape(HEADS, rows).T, jnp.zeros((rows, LANES - 24), F32)], axis=1)

    def gates_bwd(col, lcv, lfv, a, dt, fb, pre, dgb, dcg, dcr):
        lane = _lane_ids(pre.shape)
        dgates = jnp.where(lane < 8, dgb, _cums_bwd(lcv, lfv, dcg + dcr))
        _, vjp = jax.vjp(_gates_elem, a, dt, fb, pre)
        da, ddt, dfb, dpre = vjp(dgates)
        return dpre, da, ddt, dfb

    dpre, d_a, d_dt, d_fb = _tiles(gates_bwd, name="gates_bwd", rows=rows, tm=rows,
                                   full_consts=[lc, lf, p_a, p_dt, p_fb],
                                   row_ins=[(proj, LANES, COL_SMALL), (dgates_b, LANES, 0), (dcums_g, LANES, 0),
                                            (dcums_row, LANES, 0)],
                                   row_outs=[(LANES, BF16)], acc_outs=[(1, LANES)] * 3)

    dproj = jnp.concatenate(dqkv + [dz] + dfqk + [dfv, dfgate, dpre], axis=1)
    dh1 = _mm(dproj, w_cat, dims="nn", name="d_h1", tk=D_CAT)
    g_cat = _mm(dproj, h1, dims="tn", name="g_in", tm=384, tn=D_MODEL, tk=rows)

    def norm1_bwd(col, w, xx, dh, dres):
        _, vjp = jax.vjp(_rms, xx, w)
        dx, dw = vjp(dh)
        return dx + dres, dw

    grad_x, d_norm1_w = _tiles(norm1_bwd, name="norm1_bwd", rows=rows, tm=tm, full_consts=[norm1_w],
                               row_ins=[(x, D_MODEL, 0), (dh1, D_MODEL, 0), (dx1, D_MODEL, 0)],
                               row_outs=[(D_MODEL, F32)], acc_outs=[(1, D_MODEL)])

    fold = lambda v: v.reshape(-1, HEAD_DIM).sum(axis=0)
    small = dict(
        loss=loss[0, 0],
        norm1_w=d_norm1_w, conv_w=d_conv, a_log=d_a[0, 8:16], dt_bias=d_dt[0, 8:16],
        out_norm_w=fold(d_on), f_bias=d_fb[0, 16:24], q_norm_w=fold(d_wqk[0]),
        k_norm_w=fold(d_wqk[1]), norm2_w=d_norm2_w, final_w=d_final_w)
    return grad_x, g_cat, g_out, g_gate, g_up, g_down, small


HBM_SPEC = pl.BlockSpec(memory_space=pltpu.HBM)


def _place():
    x, y, c = lax.axis_index("x"), lax.axis_index("y"), lax.axis_index("c")
    chips = [(1 - x, y), (x, 1 - y), (1 - x, 1 - y)]
    return x, y, c, 2 * x + y, (x, y, 1 - c), chips, [2 * cx + cy for cx, cy in chips]


def _remote(src, dst, send_sem, recv_sem, to):
    return pltpu.make_async_remote_copy(src_ref=src, dst_ref=dst, send_sem=send_sem, recv_sem=recv_sem,
                                        device_id=to, device_id_type=MESH)


def _allgather_weights(shards, conv):
    n = len(shards)
    halves = [s.shape[1] // 2 for s in shards]
    per = 6
    own_base = n * per + 3

    def body(*refs):
        ins, conv_in = refs[:n], refs[n]
        outs, conv_out = refs[n + 1:2 * n + 1], refs[2 * n + 1]
        send_sems, recv_sems = refs[2 * n + 2:]
        x, y, c, own, sib, chips, chip_idx = _place()

        def half(i, ref, hc):
            return ref.at[:, pl.ds(pl.multiple_of(hc * halves[i], LANES), halves[i])]

        sent = []
        for i, (src, dst) in enumerate(zip(list(ins) + [conv_in], list(outs) + [conv_out])):
            k = own_base + i
            sent.append(_remote(src, dst.at[own], send_sems.at[k], recv_sems.at[k], sib))
        for i in range(n):
            for j, chip in enumerate(chips):
                k = i * per + j
                sent.append(_remote(half(i, ins[i], c), half(i, outs[i].at[own], c),
                                    send_sems.at[k], recv_sems.at[k], (*chip, c)))
        for j, chip in enumerate(chips):
            k = n * per + j
            sent.append(_remote(conv_in, conv_out.at[own], send_sems.at[k], recv_sems.at[k], (*chip, c)))
        for cp in sent:
            cp.start()
        for i in range(n):
            for j in range(len(chips)):
                k = i * per + j
                landed = half(i, outs[i].at[chip_idx[j]], c)
                _remote(landed, landed, send_sems.at[k], recv_sems.at[k], sib).wait_recv()
                fwd = _remote(landed, landed, send_sems.at[k + 3], recv_sems.at[k + 3], sib)
                fwd.start()
                sent.append(fwd)
        for i in range(n):
            for j in range(len(chips)):
                k = i * per + 3 + j
                landed = half(i, outs[i].at[chip_idx[j]], 1 - c)
                _remote(landed, landed, send_sems.at[k], recv_sems.at[k], sib).wait_recv()
        for j in range(len(chips)):
            k = n * per + j
            landed = conv_out.at[chip_idx[j]]
            _remote(landed, landed, send_sems.at[k], recv_sems.at[k], sib).wait_recv()
        for i, dst in enumerate(list(outs) + [conv_out]):
            k = own_base + i
            landed = dst.at[own]
            _remote(landed, landed, send_sems.at[k], recv_sems.at[k], sib).wait_recv()
        for cp in sent:
            cp.wait_send()

    n_sem = own_base + n + 1
    out_shape = [jax.ShapeDtypeStruct((N_CHIPS,) + s.shape, s.dtype) for s in shards]
    out_shape.append(jax.ShapeDtypeStruct((N_CHIPS,) + conv.shape, conv.dtype))
    res = pl.pallas_call(
        body, name="allgather_weights", out_shape=out_shape,
        in_specs=[HBM_SPEC] * (n + 1), out_specs=[HBM_SPEC] * (n + 1),
        scratch_shapes=[pltpu.SemaphoreType.DMA((n_sem,)), pltpu.SemaphoreType.DMA((n_sem,))],
    )(*shards, conv)
    return res[:n], res[n]


SEM_SPEC = pl.BlockSpec(memory_space=pltpu.SEMAPHORE)
ANY_SPEC = pl.BlockSpec(memory_space=pl.ANY)
DATAFLOW = pltpu.SideEffectType.DATAFLOW_SIDE_EFFECTING


def _gather_plan(srcs, lands):
    x, y, c, own, sib, chips, chip_idx = _place()
    plan = []
    for src, land in zip(srcs, lands):
        for j, chip in enumerate(chips):
            plan.append((src, land.at[own], (*chip, c), land.at[chip_idx[j]]))
        plan.append((src, land.at[own], sib, land.at[own]))
    return plan


def _exchange_plan(srcs, lands):
    x, y, c, own, sib, chips, chip_idx = _place()
    plan = []
    for src, land in zip(srcs, lands):
        for j, chip in enumerate(chips):
            plan.append((src.at[chip_idx[j]], land.at[j], (*chip, c), land.at[j]))
    return plan


def _in_proj_plan(srcs, lands):
    x, y, c, own, sib, chips, chip_idx = _place()
    (w, conv), (w_land, conv_land) = srcs, lands
    hw = w.shape[1] // 2
    half = lambda ref: ref.at[:, pl.ds(pl.multiple_of(c * hw, LANES), hw)]
    plan = []
    for j, chip in enumerate(chips):
        plan.append((half(w), half(w_land.at[own]), (*chip, c), half(w_land.at[chip_idx[j]])))
        plan.append((conv, conv_land.at[own], (*chip, c), conv_land.at[chip_idx[j]]))
    plan.append((w, w_land.at[own], sib, w_land.at[own]))
    plan.append((conv, conv_land.at[own], sib, conv_land.at[own]))
    return plan


def _forward_halves(landed):
    hw = landed.shape[2] // 2

    def body(in_ref, out_ref, send_sems, recv_sems):
        x, y, c, own, sib, chips, chip_idx = _place()
        half = lambda ref, hc: ref.at[:, pl.ds(pl.multiple_of(hc * hw, LANES), hw)]
        sent = [_remote(half(out_ref.at[chip_idx[j]], c), half(out_ref.at[chip_idx[j]], c),
                        send_sems.at[j], recv_sems.at[j], sib) for j in range(3)]
        for cp in sent:
            cp.start()
        for j in range(3):
            other = half(out_ref.at[chip_idx[j]], 1 - c)
            _remote(other, other, send_sems.at[j], recv_sems.at[j], sib).wait_recv()
        for cp in sent:
            cp.wait_send()

    return pl.pallas_call(
        body, name="gather_in_forward", out_shape=jax.ShapeDtypeStruct(landed.shape, landed.dtype),
        in_specs=[HBM_SPEC], out_specs=HBM_SPEC, input_output_aliases={0: 0},
        scratch_shapes=[pltpu.SemaphoreType.DMA((3,)), pltpu.SemaphoreType.DMA((3,))],
    )(landed)


def _split_start(name, plan_fn, srcs, land_shapes, n_copies, after):
    n = len(srcs)

    def body(*refs):
        src_refs, land_refs = refs[:n], refs[n:2 * n]
        send_sems, recv_sems = refs[2 * n + 1], refs[2 * n + 2]
        token = refs[-1]
        for k, (src, dst, to, _) in enumerate(plan_fn(src_refs, land_refs)):
            _remote(src, dst, send_sems.at[k], recv_sems.at[k], to).start()
        token[...] = jnp.zeros_like(token)

    lands = [pltpu.with_memory_space_constraint(lax.empty(s.shape, s.dtype), pltpu.HBM) for s in land_shapes]
    srcs = [pltpu.with_memory_space_constraint(s, pltpu.HBM) for s in srcs]
    out_shape = ([pltpu.SemaphoreType.DMA((n_copies,)), pltpu.SemaphoreType.DMA((n_copies,))]
                 + [pltpu.HBM(s.shape, s.dtype) for s in srcs] + [pltpu.HBM(s.shape, s.dtype) for s in land_shapes]
                 + [jax.ShapeDtypeStruct((8, LANES), F32)])
    res = pl.pallas_call(
        body, name=name, out_shape=out_shape,
        in_specs=[HBM_SPEC] * (2 * n) + [ANY_SPEC],
        out_specs=[SEM_SPEC, SEM_SPEC] + [HBM_SPEC] * (2 * n) + [pl.BlockSpec(memory_space=pltpu.VMEM)],
        input_output_aliases={i: 2 + i for i in range(2 * n)},
        compiler_params=pltpu.CompilerParams(has_side_effects=DATAFLOW),
    )(*srcs, *lands, after)
    return dict(sems=res[:2], srcs=res[2:2 + n], lands=res[2 + n:2 + 2 * n], token=res[-1], n=n)


def _split_wait(name, plan_fn, started, after):
    n = started["n"]

    def body(*refs):
        src_refs, land_refs = refs[:n], refs[n:2 * n]
        send_sems, recv_sems = refs[2 * n], refs[2 * n + 1]
        for k, (src, _, to, landed) in enumerate(plan_fn(src_refs, land_refs)):
            copy = _remote(src, landed, send_sems.at[k], recv_sems.at[k], to)
            copy.wait_send()
            copy.wait_recv()

    srcs, lands = started["srcs"], started["lands"]
    after = list(after) if isinstance(after, (list, tuple)) else [after]
    res = pl.pallas_call(
        body, name=name,
        out_shape=[pltpu.HBM(s.shape, s.dtype) for s in srcs] + [pltpu.HBM(s.shape, s.dtype) for s in lands],
        in_specs=[HBM_SPEC] * (2 * n) + [SEM_SPEC, SEM_SPEC] + [ANY_SPEC] * len(after),
        out_specs=[HBM_SPEC] * (2 * n),
        input_output_aliases={i: i for i in range(2 * n)},
        compiler_params=pltpu.CompilerParams(has_side_effects=DATAFLOW),
    )(*srcs, *lands, *started["sems"], *after)
    return res[n:]


def _swap_halves(stacks, name):
    n = len(stacks)

    def body(*refs):
        ins, outs = refs[:n], refs[n:2 * n]
        send_sems, recv_sems = refs[2 * n:]
        x, y, c, own, sib, chips, chip_idx = _place()
        cps = []
        for i in range(n):
            h = stacks[i].shape[2] // 2
            src = ins[i].at[:, :, pl.ds(pl.multiple_of((1 - c) * h, LANES), h)]
            cps.append(_remote(src, outs[i], send_sems.at[i], recv_sems.at[i], sib))
        for cp in cps:
            cp.start()
        for cp in cps:
            cp.wait()

    out_shape = [jax.ShapeDtypeStruct((N_CHIPS, s.shape[1], s.shape[2] // 2), s.dtype) for s in stacks]
    return pl.pallas_call(
        body, name=name, out_shape=out_shape,
        in_specs=[HBM_SPEC] * n, out_specs=[HBM_SPEC] * n,
        scratch_shapes=[pltpu.SemaphoreType.DMA((n,)), pltpu.SemaphoreType.DMA((n,))],
    )(*stacks)


def _add_half(stack, landed, place, name):
    _, rows, h = landed.shape

    def body(place_ref, a_ref, b_ref, o_ref, own_ref):
        part = (a_ref[...].astype(F32) + b_ref[...].astype(F32)).astype(o_ref.dtype)
        o_ref[...] = part

        @pl.when(pl.program_id(0) == place_ref[1])
        def _():
            own_ref[...] = part[0]

    return pl.pallas_call(
        body, name=name,
        out_shape=[jax.ShapeDtypeStruct(landed.shape, BF16), jax.ShapeDtypeStruct((rows, h), BF16)],
        grid_spec=pltpu.PrefetchScalarGridSpec(
            num_scalar_prefetch=1, grid=(N_CHIPS,),
            in_specs=[pl.BlockSpec((1, rows, h), lambda j, p: (j, 0, p[0])),
                      pl.BlockSpec((1, rows, h), lambda j, p: (j, 0, 0))],
            out_specs=[pl.BlockSpec((1, rows, h), lambda j, p: (j, 0, 0)),
                       pl.BlockSpec((rows, h), lambda j, p: (0, 0))]),
        compiler_params=_params(("arbitrary",)),
    )(place, stack, landed)


def _exchange_partials(parts):
    n = len(parts)

    def body(*refs):
        ins, outs = refs[:n], refs[n:2 * n]
        send_sems, recv_sems = refs[2 * n:]
        x, y, c, own, sib, chips, chip_idx = _place()
        sent = []
        for i in range(n):
            for j, chip in enumerate(chips):
                k = i * 3 + j
                sent.append(_remote(ins[i].at[chip_idx[j]], outs[i].at[j], send_sems.at[k], recv_sems.at[k],
                                    (*chip, c)))
        for cp in sent:
            cp.start()
        for i in range(n):
            for j in range(len(chips)):
                k = i * 3 + j
                landed = outs[i].at[j]
                _remote(landed, landed, send_sems.at[k], recv_sems.at[k], sib).wait_recv()
        for cp in sent:
            cp.wait_send()

    return pl.pallas_call(
        body, name="rs_exchange_partials",
        out_shape=[jax.ShapeDtypeStruct((3,) + p.shape[1:], p.dtype) for p in parts],
        in_specs=[HBM_SPEC] * n, out_specs=[HBM_SPEC] * n,
        scratch_shapes=[pltpu.SemaphoreType.DMA((3 * n,)), pltpu.SemaphoreType.DMA((3 * n,))],
    )(*parts)


def _sum_partials(own_part, landed, name):
    _, h, cols = landed.shape

    def body(own_ref, a_ref, o_ref):
        acc = own_ref[...].astype(F32)
        for s in range(3):
            acc = acc + a_ref[s].astype(F32)
        o_ref[...] = acc

    return pl.pallas_call(
        body, name=name, out_shape=jax.ShapeDtypeStruct((h, cols), F32), grid=(1,),
        in_specs=[pl.BlockSpec((h, cols), lambda i: (0, 0)), pl.BlockSpec(landed.shape, lambda i: (0, 0, 0))],
        out_specs=pl.BlockSpec((h, cols), lambda i: (0, 0)),
        compiler_params=_params(("arbitrary",)),
    )(own_part, landed)


def _share_halves(halves, name):
    n = len(halves)

    def body(*refs):
        ins, outs = refs[:n], refs[n:2 * n]
        send_sems, recv_sems = refs[2 * n:]
        x, y, c, own, sib, chips, chip_idx = _place()
        cps = [_remote(ins[i], outs[i], send_sems.at[i], recv_sems.at[i], sib) for i in range(n)]
        for cp in cps:
            cp.start()
        for cp in cps:
            cp.wait()

    return pl.pallas_call(
        body, name=name,
        out_shape=[jax.ShapeDtypeStruct(p.shape, p.dtype) for p in halves],
        in_specs=[HBM_SPEC] * n, out_specs=[HBM_SPEC] * n,
        scratch_shapes=[pltpu.SemaphoreType.DMA((n,)), pltpu.SemaphoreType.DMA((n,))],
    )(*halves)


def _allreduce_small(packed):
    rows = packed.shape[0]
    n_dev = 8

    def body(in_ref, out_ref, gath, send_sems, recv_sems):
        x, y, c = lax.axis_index("x"), lax.axis_index("y"), lax.axis_index("c")
        me = 4 * x + 2 * y + c
        gath[me] = in_ref[...]
        cps = []
        for k in range(1, n_dev):
            fx, fy, fc = (k >> 2) & 1, (k >> 1) & 1, k & 1
            to = (x ^ fx, y ^ fy, c ^ fc)
            cps.append(_remote(in_ref, gath.at[me], send_sems.at[k - 1], recv_sems.at[k - 1], to))
        for cp in cps:
            cp.start()
        for k in range(1, n_dev):
            fx, fy, fc = (k >> 2) & 1, (k >> 1) & 1, k & 1
            src = 4 * (x ^ fx) + 2 * (y ^ fy) + (c ^ fc)
            slot = gath.at[src]
            _remote(slot, slot, send_sems.at[k - 1], recv_sems.at[k - 1], (x, y, c)).wait_recv()
        for cp in cps:
            cp.wait_send()
        acc = gath[0]
        for d in range(1, n_dev):
            acc = acc + gath[d]
        out_ref[...] = acc

    vm = pl.BlockSpec(memory_space=pltpu.VMEM)
    return pl.pallas_call(
        body, name="allreduce_small", out_shape=jax.ShapeDtypeStruct(packed.shape, F32),
        in_specs=[vm], out_specs=vm,
        scratch_shapes=[pltpu.VMEM((n_dev, rows, LANES), F32),
                        pltpu.SemaphoreType.DMA((n_dev - 1,)), pltpu.SemaphoreType.DMA((n_dev - 1,))],
    )(packed)


def _adam(col, w, g, m, v):
    m2 = ADAM_B1 * m + (1.0 - ADAM_B1) * g
    v2 = ADAM_B2 * v + (1.0 - ADAM_B2) * (g * g)
    m_hat = m2 / (1.0 - ADAM_B1 ** ADAM_STEP)
    v_hat = v2 / (1.0 - ADAM_B2 ** ADAM_STEP)
    delta = -ADAM_LR * (m_hat / (jnp.sqrt(v_hat) + ADAM_EPS) + ADAM_WD * w)
    return delta, m2, v2


def _adam_call(w, g, m, v, name):
    rows, cols = w.shape
    tm = rows
    for cand in (256, 352, 176, 128, 64, 48, 16, 8):
        if rows % cand == 0:
            tm = cand
            break
    return _tiles(_adam, name=name, rows=rows, tm=tm,
                  row_ins=[(w, cols, 0), (g, cols, 0), (m, cols, 0), (v, cols, 0)],
                  row_outs=[(cols, F32)] * 3)


def _adam_big(w, g_mine, g_other, m, v, place, name):
    rows, cols = w.shape
    tc = 256
    nt = cols // 2 // tc

    def body(place_ref, w_ref, gm_ref, go_ref, m_ref, v_ref, g_out, d_out, m_out, v_out):
        g = jnp.where(pl.program_id(0) == place_ref[0], gm_ref[...], go_ref[...])
        d, m2, v2 = _adam(None, w_ref[...], g, m_ref[...], v_ref[...])
        g_out[...] = g
        d_out[...] = d
        m_out[...] = m2
        v_out[...] = v2

    full = pl.BlockSpec((rows, tc), lambda hh, i, p: (0, hh * nt + i))
    half = pl.BlockSpec((rows, tc), lambda hh, i, p: (0, i))
    return pl.pallas_call(
        body, name=name, out_shape=[jax.ShapeDtypeStruct(w.shape, F32)] * 4,
        grid_spec=pltpu.PrefetchScalarGridSpec(
            num_scalar_prefetch=1, grid=(2, nt),
            in_specs=[full, half, half, full, full], out_specs=[full] * 4),
        compiler_params=_params(("arbitrary", "arbitrary")),
    )(place, w, g_mine, g_other, m, v)


def _adam_untiled_rows(w, g_mine, g_other, m, v, place, name):
    rows, _, cols = w.shape
    tc = 256
    nt = cols // 2 // tc
    rb = next(r for r in (206, 128, 103, rows) if rows % r == 0)

    def body(place_ref, w_ref, gm_ref, go_ref, m_ref, v_ref, g_out, d_out, m_out, v_out):
        g = jnp.where(pl.program_id(0) == place_ref[0], gm_ref[...], go_ref[...])
        d, m2, v2 = _adam(None, w_ref[...], g, m_ref[...], v_ref[...])
        g_out[...] = g
        d_out[...] = d
        m_out[...] = m2
        v_out[...] = v2

    full = pl.BlockSpec((rb, 1, tc), lambda hh, i, r, p: (r, 0, hh * nt + i))
    half = pl.BlockSpec((rb, 1, tc), lambda hh, i, r, p: (r, 0, i))
    return pl.pallas_call(
        body, name=name, out_shape=[jax.ShapeDtypeStruct(w.shape, F32)] * 4,
        grid_spec=pltpu.PrefetchScalarGridSpec(
            num_scalar_prefetch=1, grid=(2, nt, rows // rb),
            in_specs=[full, half, half, full, full], out_specs=[full] * 4),
        compiler_params=_params(("arbitrary", "arbitrary", "arbitrary")),
    )(place, w, g_mine, g_other, m, v)


def _pack(arrays):
    flat = []
    for a in arrays:
        a = a.reshape(-1).astype(F32)
        flat.append(jnp.pad(a, (0, (-a.size) % LANES)))
    out = jnp.concatenate(flat)
    out = jnp.pad(out, (0, (-out.size) % (8 * LANES)))
    return out.reshape(-1, LANES)


def _unpack(packed, shapes):
    flat = packed.reshape(-1)
    out, off = [], 0
    for s in shapes:
        size = int(np.prod(s))
        out.append(flat[off:off + size].reshape(s))
        off += size + (-size) % LANES
    return out


def kernel(x, norm1_w, w_in, gdn_conv_w, gdn_A_log, gdn_dt_bias, gdn_out_norm_w, fox_f_bias, fox_q_norm_w, fox_k_norm_w, w_out, norm2_w, w_ffn_gate, w_ffn_up, w_ffn_down, final_norm_w, loss_target, m_norm1_w, m_w_in, m_gdn_conv_w, m_gdn_A_log, m_gdn_dt_bias, m_gdn_out_norm_w, m_fox_f_bias, m_fox_q_norm_w, m_fox_k_norm_w, m_w_out, m_norm2_w, m_w_ffn_gate, m_w_ffn_up, m_w_ffn_down, m_final_norm_w, v_norm1_w, v_w_in, v_gdn_conv_w, v_gdn_A_log, v_gdn_dt_bias, v_gdn_out_norm_w, v_fox_f_bias, v_fox_q_norm_w, v_fox_k_norm_w, v_w_out, v_norm2_w, v_w_ffn_gate, v_w_ffn_up, v_w_ffn_down, v_final_norm_w):
    cx, cy, cc = lax.axis_index("x"), lax.axis_index("y"), lax.axis_index("c")
    own = 2 * cx + cy
    place = jnp.stack([cc, own]).astype(jnp.int32)

    names = ["w_in", "w_out", "w_gate", "w_up", "w_down"]
    is_t = [True, False, True, True, False]
    to_t = lambda a, t: a[0].T if t else a[0]
    from_t = lambda a, t: (a.T if t else a)[None]
    big_w = [to_t(a, t) for a, t in zip([w_in, w_out, w_ffn_gate, w_ffn_up, w_ffn_down], is_t)]
    big_m = [to_t(a, t) for a, t in zip([m_w_in, m_w_out, m_w_ffn_gate, m_w_ffn_up, m_w_ffn_down], is_t)]
    big_v = [to_t(a, t) for a, t in zip([v_w_in, v_w_out, v_w_ffn_gate, v_w_ffn_up, v_w_ffn_down], is_t)]
    shards = [w.astype(BF16) for w in big_w]
    small_w = [norm1_w, gdn_conv_w, gdn_A_log, gdn_dt_bias, gdn_out_norm_w, fox_f_bias, fox_q_norm_w,
               fox_k_norm_w, norm2_w, final_norm_w]
    small_m = [m_norm1_w, m_gdn_conv_w, m_gdn_A_log, m_gdn_dt_bias, m_gdn_out_norm_w, m_fox_f_bias,
               m_fox_q_norm_w, m_fox_k_norm_w, m_norm2_w, m_final_norm_w]
    small_v = [v_norm1_w, v_gdn_conv_w, v_gdn_A_log, v_gdn_dt_bias, v_gdn_out_norm_w, v_fox_f_bias,
               v_fox_q_norm_w, v_fox_k_norm_w, v_norm2_w, v_final_norm_w]
    first = _split_start("gather_in_start", _in_proj_plan, [shards[0], gdn_conv_w[0]],
                         [jax.ShapeDtypeStruct((N_CHIPS,) + shards[0].shape, BF16),
                          jax.ShapeDtypeStruct((N_CHIPS, CONV_K, 3 * WIDTH // N_CHIPS), F32)],
                         n_copies=8, after=shards[0])
    small_packed = [_pack(small_w), _pack(small_m), _pack(small_v)]
    rest = {}

    def first_weights(after):
        w_in_g, conv_g = _split_wait("gather_in_wait", _in_proj_plan, first, [after] + small_packed)
        w_in_g = _forward_halves(w_in_g)
        rest.update(_split_start("gather_rest_start", _gather_plan, shards[1:],
                                 [jax.ShapeDtypeStruct((N_CHIPS,) + s.shape, BF16) for s in shards[1:]],
                                 n_copies=4 * len(shards[1:]), after=w_in_g))
        w_cat = _cat_weights(w_in_g.reshape(D_IN, D_MODEL))
        return w_cat + rest["token"][0, 0].astype(BF16), conv_g.transpose(1, 0, 2).reshape(CONV_K, 3 * WIDTH)

    def late_weights(after):
        w_out_g, w_gate_g, w_up_g, w_down_g = _split_wait("gather_rest_wait", _gather_plan, rest, after)
        return w_out_g.reshape(D_MODEL, D_MODEL), w_gate_g, w_up_g, w_down_g

    def start_reduction(stacks, nms, tag):
        landed = _swap_halves(stacks, "rs_swap_" + tag)
        added = [_add_half(s, l, place, "rs_add_" + nm) for s, l, nm in zip(stacks, landed, nms)]
        parts = [a[0] for a in added]
        started = _split_start("exchange_" + tag + "_start", _exchange_plan, parts,
                               [jax.ShapeDtypeStruct((3,) + p.shape[1:], p.dtype) for p in parts],
                               n_copies=3 * len(parts), after=parts[0])
        return dict(own=[a[1] for a in added], started=started, tag=tag, names=nms)

    def finish_reduction(red, after, updates):
        landed = _split_wait("exchange_" + red["tag"] + "_wait", _exchange_plan, red["started"], after)
        halves = [_sum_partials(o, p, "rs_sum_" + nm) for o, p, nm in zip(red["own"], landed, red["names"])]
        others = _share_halves(halves, "rs_share_" + red["tag"])
        return [upd(gm, go) for upd, gm, go in zip(updates, halves, others)]

    def transport_update(b):
        def upd(gm, go):
            res = _adam_big(big_w[b], gm, go, big_m[b], big_v[b], place, "adam_" + names[b])
            early_done.append(res[1])
            return [from_t(a, is_t[b]) for a in res]
        return upd

    early_done = []

    def w_in_update(gm, go):
        rows3 = lambda a: jnp.transpose(a, (2, 0, 1))
        res = _adam_untiled_rows(rows3(w_in), gm[:, None, :], go[:, None, :], rows3(m_w_in), rows3(v_w_in),
                                 place, "adam_w_in")
        return [jnp.transpose(a, (1, 2, 0)) for a in res]

    early = {}

    def early_grads_ready(g_out, g_gate, g_up, g_down):
        stacks = [g_out.reshape(N_CHIPS, D_MODEL // N_CHIPS, D_MODEL), g_gate, g_up, g_down]
        early.update(start_reduction(stacks, names[1:], "early"))
        return early["started"]["token"][0, 0]

    grad_x, g_cat, _, _, _, _, small = _local_step(
        x[0], loss_target[0], norm1_w + first["token"][0, 0], gdn_A_log[0], gdn_dt_bias[0],
        gdn_out_norm_w[0], fox_f_bias[0], fox_q_norm_w[0], fox_k_norm_w[0], norm2_w, final_norm_w.reshape(1, -1),
        first_weights, late_weights, early_grads_ready)

    late = start_reduction([_uncat_grad(g_cat).reshape(N_CHIPS, D_IN // N_CHIPS, D_MODEL)], names[:1], "w_in")
    big_upd = finish_reduction(early, late["started"]["token"], [transport_update(b) for b in range(1, 5)])

    order = ["norm1_w", "conv_w", "a_log", "dt_bias", "out_norm_w", "f_bias", "q_norm_w", "k_norm_w",
             "norm2_w", "final_w"]
    red = _allreduce_small(_pack([small[k] for k in order] + [small["loss"]]))
    red_shapes = [(1, D_MODEL), (CONV_K, 3 * WIDTH), (1, HEADS), (1, HEADS), (1, HEAD_DIM), (1, HEADS),
                  (1, HEAD_DIM), (1, HEAD_DIM), (1, D_MODEL), (D_MODEL,), ()]
    red_list = _unpack(red, red_shapes)
    loss = red_list[-1]
    small_g = dict(zip(order, red_list[:-1]))
    shard_cols = 3 * WIDTH // N_CHIPS
    small_g["conv_w"] = lax.dynamic_slice_in_dim(small_g["conv_w"], own * shard_cols, shard_cols, axis=1)[None]
    small_gl = [small_g[k].reshape(w.shape) for k, w in zip(order, small_w)]
    s_delta, s_m, s_v = _adam_call(small_packed[0], _pack(small_gl), small_packed[1], small_packed[2], "adam_small")
    big_upd = finish_reduction(late, [s_delta] + early_done, [w_in_update]) + big_upd
    shapes = [w.shape for w in small_w]
    s_delta, s_m, s_v = _unpack(s_delta, shapes), _unpack(s_m, shapes), _unpack(s_v, shapes)

    big_pos = {1: 0, 9: 1, 11: 2, 12: 3, 13: 4}
    small_pos = {0: 0, 2: 1, 3: 2, 4: 3, 5: 4, 6: 5, 7: 6, 8: 7, 10: 8, 14: 9}
    grads, deltas, new_m, new_v = [], [], [], []
    for pos in range(15):
        if pos in big_pos:
            b = big_pos[pos]
            g, d, m2, v2 = big_upd[b]
            grads.append(g)
            deltas.append(d)
            new_m.append(m2)
            new_v.append(v2)
        else:
            s = small_pos[pos]
            grads.append(small_gl[s])
            deltas.append(s_delta[s])
            new_m.append(s_m[s])
            new_v.append(s_v[s])
    return (loss, grad_x[None], *grads, *deltas, *new_m, *new_v)
```

```python
import jax
import jax.numpy as jnp
import numpy as np
from jax import lax
from jax.experimental import pallas as pl
from jax.experimental.pallas import tpu as pltpu

F32 = jnp.float32
BF16 = jnp.bfloat16

D_MODEL = 1024
HEADS = 8
HEAD_DIM = 64
PAIRS = HEADS // 2
WIDTH = HEADS * HEAD_DIM
CHUNK = 64
CONV_K = 4
D_FF = 2816
FF_SHARD = D_FF // 4
EPS = 1e-6
SCALE = HEAD_DIM ** -0.5
LANES = 128
N_CHIPS = 4
D_IN = 4120
D_CAT = 4224
COL_SMALL = 4096 // LANES

ADAM_LR = 0.001
ADAM_B1 = 0.9
ADAM_B2 = 0.999
ADAM_EPS = 1e-08
ADAM_WD = 0.01
ADAM_STEP = 10

VMEM_LIMIT = 56 * 1024 * 1024
MESH = pl.DeviceIdType.MESH
HIGHEST = lax.Precision.HIGHEST


def _params(sem):
    return pltpu.CompilerParams(dimension_semantics=sem, vmem_limit_bytes=VMEM_LIMIT)


_CONTRACT = {"nn": ((1,), (0,)), "nt": ((1,), (1,)), "tn": ((0,), (0,))}


def _mm(a, b, *, dims, name, out_dtype=F32, add=None, tm=1024, tn=512, tk=512):
    if dims == "nn":
        (m, k), (k2, n) = a.shape, b.shape
    elif dims == "nt":
        (m, k), (n, k2) = a.shape, b.shape
    else:
        (k, m), (k2, n) = a.shape, b.shape
    assert k == k2, (a.shape, b.shape, dims)
    tm, tn, tk = min(tm, m), min(tn, n), min(tk, k)
    assert m % tm == 0 and n % tn == 0 and k % tk == 0, (m, n, k, tm, tn, tk)
    nk = k // tk
    a_spec = (pl.BlockSpec((tk, tm), lambda i, j, kk: (kk, i)) if dims == "tn"
              else pl.BlockSpec((tm, tk), lambda i, j, kk: (i, kk)))
    b_spec = (pl.BlockSpec((tn, tk), lambda i, j, kk: (j, kk)) if dims == "nt"
              else pl.BlockSpec((tk, tn), lambda i, j, kk: (kk, j)))
    o_spec = pl.BlockSpec((tm, tn), lambda i, j, kk: (i, j))
    contract = (_CONTRACT[dims], ((), ()))
    has_add = add is not None

    def body(*refs):
        a_ref, b_ref = refs[:2]
        add_ref = refs[2] if has_add else None
        o_ref = refs[3] if has_add else refs[2]
        part = lax.dot_general(a_ref[...].astype(BF16), b_ref[...].astype(BF16), contract,
                               preferred_element_type=F32)

        def finish(r):
            if has_add:
                r = r + add_ref[...].astype(F32)
            o_ref[...] = r.astype(out_dtype)

        if nk == 1:
            finish(part)
            return
        acc = refs[-1]
        kk = pl.program_id(2)

        @pl.when(kk == 0)
        def _():
            acc[...] = part

        @pl.when(kk > 0)
        def _():
            acc[...] += part

        @pl.when(kk == nk - 1)
        def _():
            finish(acc[...])

    ins = [a, b] + ([add] if has_add else [])
    in_specs = [a_spec, b_spec] + ([o_spec] if has_add else [])
    return pl.pallas_call(
        body, name=name, grid=(m // tm, n // tn, nk),
        in_specs=in_specs, out_specs=o_spec,
        out_shape=jax.ShapeDtypeStruct((m, n), out_dtype),
        scratch_shapes=[pltpu.VMEM((tm, tn), F32)] if nk > 1 else [],
        compiler_params=_params(("parallel", "parallel", "arbitrary")),
    )(*ins)


def _mm_blocks(a, b, *, name, grid, a_spec, b_spec, o_spec, out_shape, dims, n_sum=0, add=None, add_spec=None):
    contract = (_CONTRACT[dims], ((), ()))
    has_add = add is not None

    def body(*refs):
        a_ref, b_ref = refs[:2]
        o_ref = refs[-1]
        dot = lambda x, y: lax.dot_general(x.astype(BF16), y.astype(BF16), contract, preferred_element_type=F32)
        if n_sum:
            r = dot(a_ref[0], b_ref[0])
            for s in range(1, n_sum):
                r = r + dot(a_ref[s], b_ref[s])
        else:
            r = dot(a_ref[...], b_ref[...])
        if has_add:
            r = r + refs[2][...].astype(F32)
        o_ref[...] = r.astype(o_ref.dtype)

    return pl.pallas_call(
        body, name=name, grid=grid,
        in_specs=[a_spec, b_spec] + ([add_spec] if has_add else []), out_specs=o_spec, out_shape=out_shape,
        compiler_params=_params(("parallel",) * len(grid)),
    )(*([a, b] + ([add] if has_add else [])))


def _tiles(fn, *, name, rows, tm, ncol=1, row_ins=(), col_consts=(), full_consts=(),
           row_outs=(), acc_outs=()):
    nt = rows // tm
    assert rows % tm == 0
    n_full, n_col, n_row = len(full_consts), len(col_consts), len(row_ins)
    n_ro, n_acc = len(row_outs), len(acc_outs)

    def body(*refs):
        ins = refs[:n_full + n_col + n_row]
        outs = refs[n_full + n_col + n_row:]
        i = pl.program_id(1)
        res = fn(pl.program_id(0), *[r[...] for r in ins])
        for r, v in zip(outs[:n_ro], res[:n_ro]):
            r[...] = v.astype(r.dtype)
        if n_acc:
            @pl.when(i == 0)
            def _():
                for r in outs[n_ro:]:
                    r[...] = jnp.zeros_like(r)
            for r, v in zip(outs[n_ro:], res[n_ro:]):
                r[...] += v

    in_specs = [pl.BlockSpec(a.shape, lambda j, i, nd=a.ndim: (0,) * nd) for a in full_consts]
    in_specs += [pl.BlockSpec((nr, w), lambda j, i, o=o: (0, o + j)) for (_, nr, w, o) in col_consts]
    in_specs += [pl.BlockSpec((tm, w), lambda j, i, o=o: (i, o + j)) for (_, w, o) in row_ins]
    out_specs = [pl.BlockSpec((tm, w), lambda j, i: (i, j)) for (w, _) in row_outs]
    out_specs += [pl.BlockSpec((nr, w), lambda j, i: (0, j)) for (nr, w) in acc_outs]
    out_shape = [jax.ShapeDtypeStruct((rows, w * ncol), dt) for (w, dt) in row_outs]
    out_shape += [jax.ShapeDtypeStruct((nr, w * ncol), F32) for (nr, w) in acc_outs]
    args = list(full_consts) + [c[0] for c in col_consts] + [r[0] for r in row_ins]
    out = pl.pallas_call(
        body, name=name, grid=(ncol, nt), in_specs=in_specs, out_specs=out_specs, out_shape=out_shape,
        compiler_params=_params(("parallel", "arbitrary")),
    )(*args)
    return out


def _rms(x, w):
    return x * lax.rsqrt(jnp.mean(x * x, axis=-1, keepdims=True) + EPS) * w


def _lane_lo(shape):
    return lax.broadcasted_iota(jnp.int32, shape, len(shape) - 1) < HEAD_DIM


def _pair_sum(x):
    lo = _lane_lo(x.shape)
    s0 = jnp.sum(jnp.where(lo, x, 0.0), axis=-1, keepdims=True)
    s1 = jnp.sum(jnp.where(lo, 0.0, x), axis=-1, keepdims=True)
    return jnp.where(lo, s0, s1)


def _head_col(x, lo, h):
    keep = lo if h == 0 else jnp.logical_not(lo)
    return jnp.max(jnp.where(keep, x, -jnp.inf), axis=-1, keepdims=True)


def _softplus(x):
    return jnp.maximum(x, 0.0) + jnp.log1p(jnp.exp(-jnp.abs(x)))


def _silu(x):
    return x * jax.nn.sigmoid(x)


def _dot(a, b, contract):
    return lax.dot_general(a.astype(BF16), b.astype(BF16), (contract, ((), ())),
                           preferred_element_type=F32)


def _dot32(a, b, contract):
    return lax.dot_general(a, b, (contract, ((), ())), precision=HIGHEST, preferred_element_type=F32)


def _bd(y):
    yy = jnp.concatenate([y, y], axis=0)
    r = lax.broadcasted_iota(jnp.int32, yy.shape, 0) < HEAD_DIM
    c = lax.broadcasted_iota(jnp.int32, yy.shape, 1) < HEAD_DIM
    return jnp.where(r == c, yy, 0.0)


def _pp(x, y):
    return _dot(x, _bd(y), _CONTRACT["nn"])


def _pp_nt(x, y):
    return _dot(x, _bd(y), _CONTRACT["nt"])


def _pp_tn(x, y):
    full = _dot(x, y, _CONTRACT["tn"])
    return jnp.where(_lane_lo((HEAD_DIM, LANES)), full[:HEAD_DIM], full[HEAD_DIM:])


def _gdn_masks():
    row = lax.broadcasted_iota(jnp.int32, (CHUNK, LANES), 0)
    col = lax.broadcasted_iota(jnp.int32, (CHUNK, LANES), 1) % HEAD_DIM
    return row, col


def _interleave(chains):
    live = list(chains)
    while live:
        for g in list(live):
            try:
                next(g)
            except StopIteration:
                live.remove(g)


def _gdn_forward(qkv, betax, gcx, grow, rows):
    nchunk = rows // CHUNK

    def body(q_ref, k_ref, v_ref, bx_ref, gx_ref, gr_ref, o_ref, ss_ref, ts_ref, state):
        n = pl.program_id(0)

        @pl.when(n == 0)
        def _():
            state[...] = jnp.zeros_like(state)

        row, col = _gdn_masks()
        incl, strict = col <= row, col < row

        def chain(p):
            lanes = pl.ds(p * LANES, LANES)
            q, k, v, bx, gx = q_ref[:, lanes], k_ref[:, lanes], v_ref[:, lanes], bx_ref[:, lanes], gx_ref[:, lanes]
            gr = gr_ref[0, p]
            glast = gx_ref[pl.ds(CHUNK - 1, 1), lanes]
            s = state[p]
            dm = jnp.where(incl, jnp.exp(jnp.minimum(gx - gr, 0.0)), 0.0)
            kb, vb, eg, qs = k * bx, v * bx, jnp.exp(gx), q * SCALE
            yield
            big_g, big_p = _pp_nt(kb, k), _pp_nt(qs, k)
            yield
            x = -jnp.where(strict, big_g * dm, 0.0)
            att = jnp.where(incl, big_p * dm, 0.0)
            tm = jnp.where(row == col, 1.0, 0.0) + x
            x = _pp(x, x)
            yield
            for _ in range(4):
                step, x = _pp(tm, x), _pp(x, x)
                yield
                tm = tm + step
            tm = tm + _pp(tm, x)
            yield
            u, w = _pp(tm, vb), _pp(tm, kb * eg)
            yield
            ws, qgs = _pp(w, s), _pp(qs * eg, s)
            yield
            vn = u - ws
            kd = k * jnp.exp(glast - gx)
            avn, upd = _pp(att, vn), _pp_tn(kd, vn)
            yield
            ss_ref[0, p] = s
            ts_ref[0, p] = tm
            o_ref[:, lanes] = qgs + avn
            state[p] = s * jnp.exp(glast) + upd

        _interleave([chain(p) for p in range(PAIRS)])

    blk = lambda j: pl.BlockSpec((CHUNK, WIDTH), lambda n, j=j: (n, j))
    sv = pl.BlockSpec((1, PAIRS, CHUNK, LANES), lambda n: (n, 0, 0, 0))
    return pl.pallas_call(
        body, name="gdn_fwd", grid=(nchunk,),
        in_specs=[blk(0), blk(1), blk(2), blk(0), blk(0),
                  pl.BlockSpec((1, PAIRS, 1, LANES), lambda n: (n, 0, 0, 0))],
        out_specs=[blk(0), sv, sv],
        out_shape=[jax.ShapeDtypeStruct((rows, WIDTH), F32),
                   jax.ShapeDtypeStruct((nchunk, PAIRS, CHUNK, LANES), F32),
                   jax.ShapeDtypeStruct((nchunk, PAIRS, CHUNK, LANES), F32)],
        scratch_shapes=[pltpu.VMEM((PAIRS, CHUNK, LANES), F32)],
        compiler_params=_params(("arbitrary",)),
    )(qkv, qkv, qkv, betax, gcx, grow)


def _gdn_backward(qkv, betax, gcx, grow, ssave, tsave, do, rows):
    nchunk = rows // CHUNK

    def body(q_ref, k_ref, v_ref, bx_ref, gx_ref, gr_ref, ss_ref, ts_ref, do_ref,
             dq_ref, dk_ref, dv_ref, dbx_ref, dgx_ref, dgr_ref, dstate):
        n = pl.program_id(0)

        @pl.when(n == 0)
        def _():
            dstate[...] = jnp.zeros_like(dstate)

        row, col = _gdn_masks()
        incl, strict = col <= row, col < row

        def chain(p):
            lanes = pl.ds(p * LANES, LANES)
            q, k, v, bx, gx = q_ref[:, lanes], k_ref[:, lanes], v_ref[:, lanes], bx_ref[:, lanes], gx_ref[:, lanes]
            gr = gr_ref[0, p]
            glast = gx_ref[pl.ds(CHUNK - 1, 1), lanes]
            s, tm, d_o = ss_ref[0, p], ts_ref[0, p], do_ref[:, lanes]
            ds_out = dstate[p]
            dm = jnp.where(incl, jnp.exp(jnp.minimum(gx - gr, 0.0)), 0.0)
            kb, vb, eg, qs = k * bx, v * bx, jnp.exp(gx), q * SCALE
            kbg, qg = kb * eg, qs * eg
            ed = jnp.exp(glast - gx)
            kd = k * ed
            eglast = jnp.exp(glast)
            yield
            big_g, big_p = _pp_nt(kb, k), _pp_nt(qs, k)
            u, w = _pp(tm, vb), _pp(tm, kbg)
            dqg, kds = _pp_nt(d_o, s), _pp(kd, ds_out)
            yield
            low = jnp.where(strict, big_g * dm, 0.0)
            att = jnp.where(incl, big_p * dm, 0.0)
            ws, atd = _pp(w, s), _pp_tn(att, d_o)
            yield
            vn = u - ws
            dvn = kds + atd
            dkd, datt_raw = _pp_nt(vn, ds_out), _pp_nt(d_o, vn)
            dw_neg, dvb = _pp_nt(dvn, s), _pp_tn(tm, dvn)
            dtm_a, wdv = _pp_nt(dvn, vb), _pp_tn(w, dvn)
            qgd = _pp_tn(qg, d_o)
            yield
            datt = jnp.where(incl, datt_raw, 0.0)
            dw = -dw_neg
            dtm_b, dkbg = _pp_nt(dw, kbg), _pp_tn(tm, dw)
            dbig_p = datt * dm
            dqs_a, dk_p = _pp(dbig_p, k), _pp_tn(dbig_p, qs)
            yield
            inner = _pp_tn(tm, dtm_a + dtm_b)
            yield
            dlow = jnp.where(strict, -_pp_nt(inner, tm), 0.0)
            yield
            dbig_g = dlow * dm
            dkb_a, dk_g = _pp(dbig_g, k), _pp_tn(dbig_g, kb)
            yield
            dkb = dkb_a + dkbg * eg
            dqs = dqs_a + dqg * eg
            dk = dk_g + dk_p + dkd * ed + dkb * bx
            z = dlow * low + datt * att
            kdterm = dkd * kd
            dglast = (jnp.sum(ds_out * s, axis=0, keepdims=True) * eglast
                      + jnp.sum(kdterm, axis=0, keepdims=True))
            dgx = dqg * qg + dkbg * kbg - kdterm
            dgx = dgx + jnp.where(col == 0, _pair_sum(z), 0.0)
            dgx = dgx + jnp.where(row == CHUNK - 1, dglast, 0.0)
            dq_ref[:, lanes] = dqs * SCALE
            dk_ref[:, lanes] = dk
            dv_ref[:, lanes] = dvb * bx
            dbx_ref[:, lanes] = dkb * k + dvb * v
            dgx_ref[:, lanes] = dgx
            dgr_ref[0, p] = -jnp.sum(z, axis=0, keepdims=True)
            dstate[p] = ds_out * eglast + qgd - wdv

        _interleave([chain(p) for p in range(PAIRS)])

    last = nchunk - 1
    blk = lambda j: pl.BlockSpec((CHUNK, WIDTH), lambda n, j=j: (last - n, j))
    sv = pl.BlockSpec((1, PAIRS, CHUNK, LANES), lambda n: (last - n, 0, 0, 0))
    gr_spec = pl.BlockSpec((1, PAIRS, 1, LANES), lambda n: (last - n, 0, 0, 0))
    wide = jax.ShapeDtypeStruct((rows, WIDTH), F32)
    return pl.pallas_call(
        body, name="gdn_bwd", grid=(nchunk,),
        in_specs=[blk(0), blk(1), blk(2), blk(0), blk(0), gr_spec, sv, sv, blk(0)],
        out_specs=[blk(0)] * 5 + [gr_spec],
        out_shape=[wide] * 5 + [jax.ShapeDtypeStruct((nchunk, PAIRS, 1, LANES), F32)],
        scratch_shapes=[pltpu.VMEM((PAIRS, CHUNK, LANES), F32)],
        compiler_params=_params(("arbitrary",)),
    )(qkv, qkv, qkv, betax, gcx, grow, ssave, tsave, do)


ATT_TQ = 256


def _att_scores(qh, kt, fk, diag):
    s = _dot(qh, kt, _CONTRACT["nt"]) - fk
    if diag:
        r = lax.broadcasted_iota(jnp.int32, s.shape, 0)
        c = lax.broadcasted_iota(jnp.int32, s.shape, 1)
        s = jnp.where(r >= c, s, -jnp.inf)
    return s


def _head_masks(n):
    lo = _lane_lo((n, LANES))
    return [lo, jnp.logical_not(lo)]


def _attention_forward(fqk, proj, frow, rows):
    tq = tk = min(ATT_TQ, rows)
    nq = rows // tq
    v_off = 3072 // LANES

    def body(q_ref, k_ref, v_ref, fr_ref, o_ref, lse_ref):
        qi = pl.program_id(1)
        q = q_ref[...] * SCALE
        keep_q, keep_k = _head_masks(tq), _head_masks(tk)
        qh = [jnp.where(keep_q[h], q, 0.0).astype(BF16) for h in range(2)]

        def tile(ki, carry, diag):
            k0 = pl.multiple_of(ki * tk, tk)
            kt = k_ref[pl.ds(k0, tk), :].astype(BF16)
            v_t = v_ref[pl.ds(k0, tk), :]
            out = [None, None]

            def chain(h):
                m, l, acc = carry[h]
                vt = jnp.where(keep_k[h], v_t, 0.0).astype(BF16)
                yield
                s = _att_scores(qh[h], kt, fr_ref[0, pl.ds(h, 1), pl.ds(k0, tk)], diag)
                yield
                m_new = jnp.maximum(m, jnp.max(s, axis=-1, keepdims=True))
                p = jnp.exp(s - m_new)
                alpha = jnp.exp(m - m_new)
                l = alpha * l + jnp.sum(p, axis=-1, keepdims=True)
                p_hi = p.astype(BF16)
                p_lo = p - p_hi.astype(F32)
                yield
                out[h] = (m_new, l, alpha * acc + _dot(p_hi, vt, _CONTRACT["nn"]) + _dot(p_lo, vt, _CONTRACT["nn"]))

            _interleave([chain(0), chain(1)])
            return tuple(out)

        one = (jnp.full((tq, 1), -jnp.inf, F32), jnp.zeros((tq, 1), F32), jnp.zeros((tq, LANES), F32))
        carry = lax.fori_loop(0, qi, lambda ki, c: tile(ki, c, False), (one, one))
        (m0, l0, acc0), (m1, l1, acc1) = tile(qi, carry, True)
        o_ref[...] = acc0 / l0 + acc1 / l1
        lse_ref[...] = jnp.where(keep_q[0], m0 + jnp.log(l0), m1 + jnp.log(l1))

    whole = lambda off: pl.BlockSpec((rows, LANES), lambda p, i, off=off: (0, off + p))
    qblk = lambda off: pl.BlockSpec((tq, LANES), lambda p, i, off=off: (i, off + p))
    wide = jax.ShapeDtypeStruct((rows, WIDTH), F32)
    return pl.pallas_call(
        body, name="fox_fwd", grid=(PAIRS, nq),
        in_specs=[qblk(0), whole(PAIRS), whole(v_off), pl.BlockSpec((1, 2, rows), lambda p, i: (p, 0, 0))],
        out_specs=[qblk(0), qblk(0)], out_shape=[wide, wide],
        compiler_params=_params(("parallel", "arbitrary")),
    )(fqk, fqk, proj, frow)


def _attention_delta(fqk, proj, frow, lse, dao, rows):
    tq = tk = min(ATT_TQ, rows)
    nq = rows // tq
    v_off = 3072 // LANES

    def body(q_ref, k_ref, v_ref, fr_ref, lse_ref, do_ref, delta_ref):
        qi = pl.program_id(1)
        q, d_o, lse_t = q_ref[...] * SCALE, do_ref[...], lse_ref[...]
        keep_q = _head_masks(tq)
        qh = [jnp.where(keep_q[h], q, 0.0).astype(BF16) for h in range(2)]
        doh = [jnp.where(keep_q[h], d_o, 0.0).astype(BF16) for h in range(2)]
        lse_h = [_head_col(lse_t, keep_q[0], h) for h in range(2)]

        def tile(ki, carry, diag):
            k0 = pl.multiple_of(ki * tk, tk)
            kt = k_ref[pl.ds(k0, tk), :].astype(BF16)
            vt = v_ref[pl.ds(k0, tk), :].astype(BF16)
            out = [None, None]

            def chain(h):
                s = _att_scores(qh[h], kt, fr_ref[0, pl.ds(h, 1), pl.ds(k0, tk)], diag)
                dp = _dot(doh[h], vt, _CONTRACT["nt"])
                yield
                out[h] = carry[h] + jnp.sum(jnp.exp(s - lse_h[h]) * dp, axis=-1, keepdims=True)

            _interleave([chain(0), chain(1)])
            return tuple(out)

        zero = jnp.zeros((tq, 1), F32)
        carry = lax.fori_loop(0, qi, lambda ki, c: tile(ki, c, False), (zero, zero))
        d0, d1 = tile(qi, carry, True)
        delta_ref[...] = jnp.where(keep_q[0], d0, d1)

    whole = lambda off: pl.BlockSpec((rows, LANES), lambda p, i, off=off: (0, off + p))
    qblk = lambda off: pl.BlockSpec((tq, LANES), lambda p, i, off=off: (i, off + p))
    return pl.pallas_call(
        body, name="fox_delta", grid=(PAIRS, nq),
        in_specs=[qblk(0), whole(PAIRS), whole(v_off),
                  pl.BlockSpec((1, 2, rows), lambda p, i: (p, 0, 0)), qblk(0), qblk(0)],
        out_specs=qblk(0), out_shape=jax.ShapeDtypeStruct((rows, WIDTH), F32),
        compiler_params=_params(("parallel", "arbitrary")),
    )(fqk, fqk, proj, frow, lse, dao)


def _attention_backward(fqk, proj, frow, ao, lse, dao, rows):
    tq = tk = min(ATT_TQ, rows)
    nq = rows // tq
    v_off = 3072 // LANES

    def body(q_ref, k_ref, v_ref, fr_ref, o_ref, lse_ref, do_ref, dq_ref, dk_ref, dv_ref, dfr_ref):
        ki = pl.program_id(1)

        @pl.when(ki == 0)
        def _():
            dq_ref[...] = jnp.zeros_like(dq_ref)

        keep_q, keep_k = _head_masks(tq), _head_masks(tk)
        k_t = k_ref[...]
        kt = k_t.astype(BF16)
        vt = v_ref[...].astype(BF16)
        kh = [jnp.where(keep_k[h], k_t, 0.0).astype(BF16) for h in range(2)]
        fk = [fr_ref[0, pl.ds(h, 1), :] for h in range(2)]

        def tile(qi, carry, diag):
            dk, dv, df0, df1 = carry
            rows_q = pl.ds(pl.multiple_of(qi * tq, tq), tq)
            q, d_o, lse_t = q_ref[rows_q, :] * SCALE, do_ref[rows_q, :], lse_ref[rows_q, :]
            delta_x = _pair_sum(d_o.astype(BF16).astype(F32) * o_ref[rows_q, :])
            res = [None, None]

            def chain(h):
                qh = jnp.where(keep_q[h], q, 0.0).astype(BF16)
                doh = jnp.where(keep_q[h], d_o, 0.0).astype(BF16)
                lse_h, delta_h = _head_col(lse_t, keep_q[0], h), _head_col(delta_x, keep_q[0], h)
                yield
                s, dp = _att_scores(qh, kt, fk[h], diag), _dot(doh, vt, _CONTRACT["nt"])
                yield
                p = jnp.exp(s - lse_h)
                ds = p * (dp - delta_h)
                yield
                res[h] = (_dot(p, doh, _CONTRACT["tn"]), _dot(ds, qh, _CONTRACT["tn"]),
                          _dot(ds, kh[h], _CONTRACT["nn"]), jnp.sum(ds, axis=0, keepdims=True))

            _interleave([chain(0), chain(1)])
            (dv0, dk0, dq0, s0), (dv1, dk1, dq1, s1) = res
            dq_ref[rows_q, :] += (dq0 + dq1) * SCALE
            return dk + dk0 + dk1, dv + dv0 + dv1, df0 - s0, df1 - s1

        zero_kv = jnp.zeros((tk, LANES), F32)
        zero_f = jnp.zeros((1, tk), F32)
        carry = tile(ki, (zero_kv, zero_kv, zero_f, zero_f), True)
        dk, dv, df0, df1 = lax.fori_loop(ki + 1, nq, lambda qi, c: tile(qi, c, False), carry)
        dk_ref[...] = dk
        dv_ref[...] = dv.astype(dv_ref.dtype)
        dfr_ref[0, pl.ds(0, 1), :] = df0
        dfr_ref[0, pl.ds(1, 1), :] = df1

    whole = lambda off: pl.BlockSpec((rows, LANES), lambda p, i, off=off: (0, off + p))
    kblk = lambda off: pl.BlockSpec((tk, LANES), lambda p, i, off=off: (i, off + p))
    fr_spec = pl.BlockSpec((1, 2, tk), lambda p, i: (p, 0, i))
    wide = jax.ShapeDtypeStruct((rows, WIDTH), F32)
    return pl.pallas_call(
        body, name="fox_bwd", grid=(PAIRS, nq),
        in_specs=[whole(0), kblk(PAIRS), kblk(v_off), fr_spec, whole(0), whole(0), whole(0)],
        out_specs=[whole(0), kblk(0), kblk(0), fr_spec],
        out_shape=[wide, wide, jax.ShapeDtypeStruct((rows, WIDTH), BF16),
                   jax.ShapeDtypeStruct((PAIRS, 2, rows), F32)],
        compiler_params=_params(("parallel", "arbitrary")),
    )(fqk, fqk, proj, frow, ao, lse, dao)


def _lane_ids(shape):
    return lax.broadcasted_iota(jnp.int32, shape, len(shape) - 1)


def _gates_elem(a_log, dt_bias, f_bias, pre):
    lane = _lane_ids(pre.shape)
    beta = jax.nn.sigmoid(pre)
    g = -jnp.exp(a_log) * _softplus(pre + dt_bias)
    lf = -_softplus(-(pre + f_bias))
    return jnp.where(lane < 8, beta, jnp.where(lane < 16, g, jnp.where(lane < 24, lf, 0.0)))


def _tri_consts():
    r = np.arange(LANES)[:, None]
    c = np.arange(LANES)[None, :]
    full = (c <= r).astype(np.float32)
    chunked = full * ((r // CHUNK) == (c // CHUNK))
    return jnp.asarray(chunked), jnp.asarray(full)


def _cums_fwd(lc, lf, gates):
    rows = gates.shape[0]
    lane = _lane_ids((LANES, LANES))
    carry = jnp.zeros((1, LANES), F32)
    out = []
    for r in range(rows // LANES):
        blk = gates[r * LANES:(r + 1) * LANES]
        gc = _dot32(lc, blk, _CONTRACT["nn"])
        f = _dot32(lf, blk, _CONTRACT["nn"]) + carry
        carry = carry + jnp.sum(blk, axis=0, keepdims=True)
        out.append(jnp.where((lane >= 8) & (lane < 16), gc, jnp.where((lane >= 16) & (lane < 24), f, 0.0)))
    return jnp.concatenate(out, axis=0)


def _cums_bwd(lc, lf, dcums):
    rows = dcums.shape[0]
    lane = _lane_ids((LANES, LANES))
    is_g = (lane >= 8) & (lane < 16)
    is_f = (lane >= 16) & (lane < 24)
    carry = jnp.zeros((1, LANES), F32)
    out = [None] * (rows // LANES)
    for r in reversed(range(rows // LANES)):
        blk = dcums[r * LANES:(r + 1) * LANES]
        dg = jnp.where(is_g, blk, 0.0)
        df = jnp.where(is_f, blk, 0.0)
        out[r] = _dot32(lc, dg, _CONTRACT["tn"]) + _dot32(lf, df, _CONTRACT["tn"]) + carry
        carry = carry + jnp.sum(df, axis=0, keepdims=True)
    return jnp.concatenate(out, axis=0)


def _expand_consts():
    xb = np.zeros((LANES, WIDTH), np.float32)
    xg = np.zeros((LANES, WIDTH), np.float32)
    for h in range(HEADS):
        xb[h, h * HEAD_DIM:(h + 1) * HEAD_DIM] = 1.0
        xg[8 + h, h * HEAD_DIM:(h + 1) * HEAD_DIM] = 1.0
    return jnp.asarray(xb), jnp.asarray(xg)


def _shift_down(x, s):
    if s == 0:
        return x
    row = lax.broadcasted_iota(jnp.int32, x.shape, 0)
    return jnp.where(row >= s, pltpu.roll(x, s, 0), 0.0)


def _shift_up(x, s):
    if s == 0:
        return x
    n = x.shape[0]
    row = lax.broadcasted_iota(jnp.int32, x.shape, 0)
    return jnp.where(row < n - s, pltpu.roll(x, n - s, 0), 0.0)


def _row_of(cw, i):
    row = lax.broadcasted_iota(jnp.int32, cw.shape, 0)
    return jnp.sum(jnp.where(row == i, cw, 0.0), axis=0, keepdims=True)


def _conv(cw, x):
    c = jnp.zeros_like(x)
    for i in range(CONV_K):
        c = c + _row_of(cw, i) * _shift_down(x, CONV_K - 1 - i)
    return c


def _post_conv(is_qk, c):
    s = _silu(c)
    n = s * lax.rsqrt(_pair_sum(s * s) + EPS)
    return jnp.where(is_qk, n, s)


def _gdn_prep_fwd(col, cw, x):
    return (_post_conv(col < 2 * PAIRS, _conv(cw, x)),)


def _gdn_prep_bwd(is_qk, cw, x, dy):
    c = _conv(cw, x)
    _, vjp = jax.vjp(lambda cc: _post_conv(is_qk, cc), c)
    (dc,) = vjp(dy)
    dx = jnp.zeros_like(x)
    row = lax.broadcasted_iota(jnp.int32, cw.shape, 0)
    dcw = jnp.zeros(cw.shape, F32)
    for i in range(CONV_K):
        s = CONV_K - 1 - i
        dx = dx + _row_of(cw, i) * _shift_up(dc, s)
        dcw = dcw + jnp.where(row == i, jnp.sum(dc * _shift_down(x, s), axis=0, keepdims=True), 0.0)
    return dx, dcw


def _head_rms(w, x):
    return x * lax.rsqrt(_pair_sum(x * x) / HEAD_DIM + EPS) * w


def _cat_weights(w_in_t):
    tail = jnp.pad(w_in_t[4112:4120], ((0, D_CAT - D_IN), (0, 0)))
    return jnp.concatenate([w_in_t[:2048], w_in_t[2064:4112], w_in_t[2048:2064], tail], axis=0)


def _uncat_grad(g):
    return jnp.concatenate([g[:2048], g[4096:4112], g[2048:4096], g[4112:4120]], axis=0)


def _lanes_to_rowform(v8, rows):
    return v8.reshape(rows // CHUNK, CHUNK, HEADS).transpose(0, 2, 1).reshape(rows // CHUNK, PAIRS, 1, LANES)


def _rowform_to_lanes(v, rows):
    return v.reshape(rows // CHUNK, HEADS, CHUNK).transpose(0, 2, 1).reshape(rows, HEADS)


def _local_step(x, target, norm1_w, a_log, dt_bias, out_norm_w, f_bias, q_norm_w, k_norm_w,
                norm2_w, final_w, first_weights, late_weights, early_grads_ready):
    rows = x.shape[0]
    tm = min(512, rows)
    lc, lf = _tri_consts()
    xb, xg = _expand_consts()

    (h1,) = _tiles(lambda col, w, xx: (_rms(xx, w),), name="norm1", rows=rows, tm=tm,
                   full_consts=[norm1_w], row_ins=[(x, D_MODEL, 0)], row_outs=[(D_MODEL, BF16)])
    w_cat, conv_w = first_weights(h1)
    proj = _mm(h1, w_cat, dims="nt", name="in_proj", tn=384, tk=1024)

    lane_pad = lambda v, off: jnp.pad(v.reshape(1, -1), ((0, 0), (off, LANES - off - v.size)))
    p_a, p_dt, p_fb = lane_pad(a_log, 8), lane_pad(dt_bias, 8), lane_pad(f_bias, 16)

    def gates_fwd(col, lcv, lfv, a, dt, fb, pre):
        gates = _gates_elem(a, dt, fb, pre)
        return gates, _cums_fwd(lcv, lfv, gates)

    gates, cums = _tiles(gates_fwd, name="gates", rows=rows, tm=rows,
                         full_consts=[lc, lf, p_a, p_dt, p_fb], row_ins=[(proj, LANES, COL_SMALL)],
                         row_outs=[(LANES, F32), (LANES, F32)])

    def expand_fwd(col, b, g, gt, cm):
        return (_dot32(gt, b, _CONTRACT["nn"]), _dot32(cm, g, _CONTRACT["nn"]))

    betax, gcx = _tiles(expand_fwd, name="expand", rows=rows, tm=tm, full_consts=[xb, xg],
                        row_ins=[(gates, LANES, 0), (cums, LANES, 0)],
                        row_outs=[(WIDTH, F32)] * 2)
    grow = _lanes_to_rowform(cums[:, 8:16], rows)
    frow = cums[:, 16:24].T.reshape(PAIRS, 2, rows)

    (qkv,) = _tiles(_gdn_prep_fwd, name="gdn_prep", rows=rows, tm=rows, ncol=3 * PAIRS,
                    col_consts=[(conv_w, CONV_K, LANES, 0)], row_ins=[(proj, LANES, 0)],
                    row_outs=[(LANES, F32)])
    o_gdn, ssave, tsave = _gdn_forward(qkv, betax, gcx, grow, rows)

    w_qk = jnp.concatenate([jnp.tile(q_norm_w.reshape(1, -1), (1, HEADS)),
                            jnp.tile(k_norm_w.reshape(1, -1), (1, HEADS))], axis=1)
    fox_off = 2048 // LANES
    (fqk,) = _tiles(lambda col, w, xx: (_head_rms(w, xx),), name="fox_prep", rows=rows, tm=rows, ncol=2 * PAIRS,
                    col_consts=[(w_qk, 1, LANES, 0)], row_ins=[(proj, LANES, fox_off)],
                    row_outs=[(LANES, F32)])
    ao, lse = _attention_forward(fqk, proj, frow, rows)

    w_on = jnp.tile(out_norm_w.reshape(1, -1), (1, 2))
    z_off, fg_off = 1536 // LANES, 3584 // LANES
    mix_g_fn = lambda w, o, z: _head_rms(w, o) * _silu(z)
    mix_f_fn = lambda a, g: a * jax.nn.sigmoid(g)
    (mix_g,) = _tiles(lambda col, w, o, z: (mix_g_fn(w, o, z),), name="mix_gdn", rows=rows, tm=rows, ncol=PAIRS,
                      full_consts=[w_on], row_ins=[(o_gdn, LANES, 0), (proj, LANES, z_off)],
                      row_outs=[(LANES, BF16)])
    (mix_f,) = _tiles(lambda col, a, g: (mix_f_fn(a, g),), name="mix_fox", rows=rows, tm=rows, ncol=PAIRS,
                      row_ins=[(ao, LANES, 0), (proj, LANES, fg_off)], row_outs=[(LANES, BF16)])
    mix = jnp.concatenate([mix_g, mix_f], axis=1)
    w_out, w_gate, w_up, w_down = late_weights(mix)
    x1 = _mm(mix, w_out, dims="nn", name="out_proj", add=x, tk=1024)

    (h2,) = _tiles(lambda col, w, xx: (_rms(xx, w),), name="norm2", rows=rows, tm=tm,
                   full_consts=[norm2_w], row_ins=[(x1, D_MODEL, 0)], row_outs=[(D_MODEL, BF16)])
    t_rows, t_cols, t_act = min(1024, rows), 512, min(512, rows)
    n_rt = rows // t_rows
    st_act = jax.ShapeDtypeStruct((N_CHIPS, rows, FF_SHARD), BF16)
    st_rows = pl.BlockSpec((None, t_rows, FF_SHARD), lambda i, j: (j, i, 0))
    out_rows = pl.BlockSpec((t_rows, t_cols), lambda i, n: (i, n))
    flat = lambda t: t.reshape(N_CHIPS * rows, FF_SHARD)

    def ffn_in(w_st, name):
        return _mm_blocks(h2, w_st, name=name, grid=(n_rt, N_CHIPS), dims="nt",
                          a_spec=pl.BlockSpec((t_rows, D_MODEL), lambda i, j: (i, 0)),
                          b_spec=pl.BlockSpec((None, FF_SHARD, D_MODEL), lambda i, j: (j, 0, 0)),
                          o_spec=st_rows, out_shape=st_act)

    gate, up = ffn_in(w_gate, "ffn_gate"), ffn_in(w_up, "ffn_up")
    act_fn = lambda g, u: _silu(g.astype(F32)) * u.astype(F32)
    (act,) = _tiles(lambda col, g, u: (act_fn(g, u),), name="ffn_act", rows=N_CHIPS * rows, tm=t_act,
                    row_ins=[(flat(gate), FF_SHARD, 0), (flat(up), FF_SHARD, 0)], row_outs=[(FF_SHARD, BF16)])
    act = act.reshape(st_act.shape)
    x2 = _mm_blocks(act, w_down, name="ffn_down", grid=(n_rt, D_MODEL // t_cols), dims="nn", n_sum=N_CHIPS,
                    a_spec=pl.BlockSpec((N_CHIPS, t_rows, FF_SHARD), lambda i, n: (0, i, 0)),
                    b_spec=pl.BlockSpec((N_CHIPS, FF_SHARD, t_cols), lambda i, n: (0, 0, n)),
                    o_spec=out_rows, out_shape=jax.ShapeDtypeStruct((rows, D_MODEL), F32),
                    add=x1, add_spec=out_rows)

    def final_fn(col, w, xx, tgt):
        y, vjp = jax.vjp(_rms, xx, w)
        err = y - tgt
        loss = 0.5 * jnp.sum(err * err) / D_MODEL
        dx, dw = vjp(err / D_MODEL)
        return dx, dx, jnp.full((1, LANES), loss, F32), dw

    dx2, dx2_b, loss, d_final_w = _tiles(final_fn, name="final_loss", rows=rows, tm=tm, full_consts=[final_w],
                                         row_ins=[(x2, D_MODEL, 0), (target, D_MODEL, 0)],
                                         row_outs=[(D_MODEL, F32), (D_MODEL, BF16)],
                                         acc_outs=[(1, LANES), (1, D_MODEL)])

    dact = _mm_blocks(dx2_b, w_down, name="d_act", grid=(n_rt, N_CHIPS), dims="nt",
                      a_spec=pl.BlockSpec((t_rows, D_MODEL), lambda i, j: (i, 0)),
                      b_spec=pl.BlockSpec((None, FF_SHARD, D_MODEL), lambda i, j: (j, 0, 0)),
                      o_spec=st_rows, out_shape=st_act)
    def g_ffn(d_st, other, name):
        return _mm_blocks(d_st, other, name=name, grid=(N_CHIPS, D_MODEL // t_cols), dims="tn",
                          a_spec=pl.BlockSpec((None, rows, FF_SHARD), lambda j, n: (j, 0, 0)),
                          b_spec=pl.BlockSpec((rows, t_cols), lambda j, n: (0, n)),
                          o_spec=pl.BlockSpec((None, FF_SHARD, t_cols), lambda j, n: (j, 0, n)),
                          out_shape=jax.ShapeDtypeStruct((N_CHIPS, FF_SHARD, D_MODEL), BF16))

    g_down = g_ffn(act, dx2_b, "g_down")

    def act_bwd(col, g, u, d):
        _, vjp = jax.vjp(lambda gg, uu: _silu(gg) * uu, g.astype(F32), u.astype(F32))
        return vjp(d.astype(F32))

    dgate, dup = _tiles(act_bwd, name="ffn_act_bwd", rows=N_CHIPS * rows, tm=t_act,
                        row_ins=[(flat(gate), FF_SHARD, 0), (flat(up), FF_SHARD, 0), (flat(dact), FF_SHARD, 0)],
                        row_outs=[(FF_SHARD, BF16), (FF_SHARD, BF16)])
    dgate, dup = dgate.reshape(st_act.shape), dup.reshape(st_act.shape)

    def d_h2(d_st, w_st, name, add):
        return _mm_blocks(d_st, w_st, name=name, grid=(n_rt, D_MODEL // t_cols), dims="nn", n_sum=N_CHIPS,
                          a_spec=pl.BlockSpec((N_CHIPS, t_rows, FF_SHARD), lambda i, n: (0, i, 0)),
                          b_spec=pl.BlockSpec((N_CHIPS, FF_SHARD, t_cols), lambda i, n: (0, 0, n)),
                          o_spec=out_rows, out_shape=jax.ShapeDtypeStruct((rows, D_MODEL), F32),
                          add=add, add_spec=out_rows)

    dh2 = d_h2(dup, w_up, "d_h2_up", d_h2(dgate, w_gate, "d_h2_gate", None))
    g_gate, g_up = g_ffn(dgate, h2, "g_gate"), g_ffn(dup, h2, "g_up")

    def norm_bwd(col, w, xx, dh, dres):
        _, vjp = jax.vjp(_rms, xx, w)
        dx, dw = vjp(dh)
        return dx + dres, dx + dres, dw

    dx1, dx1_b, d_norm2_w = _tiles(norm_bwd, name="norm2_bwd", rows=rows, tm=tm, full_consts=[norm2_w],
                                   row_ins=[(x1, D_MODEL, 0), (dh2, D_MODEL, 0), (dx2, D_MODEL, 0)],
                                   row_outs=[(D_MODEL, F32), (D_MODEL, BF16)], acc_outs=[(1, D_MODEL)])
    dmix = _mm(dx1_b, w_out, dims="nt", name="d_mix", tk=1024)
    g_out = _mm(mix, dx1_b, dims="tn", name="g_out", tk=rows, out_dtype=BF16)
    w_on = w_on + early_grads_ready(g_out, g_gate, g_up, g_down)

    def mix_g_bwd(col, w, o, z, d):
        _, vjp = jax.vjp(mix_g_fn, w, o, z)
        dw, do_, dz = vjp(d)
        return do_, dz, dw

    do_gdn, dz, d_on = _tiles(mix_g_bwd, name="mix_gdn_bwd", rows=rows, tm=rows, ncol=PAIRS, full_consts=[w_on],
                              row_ins=[(o_gdn, LANES, 0), (proj, LANES, z_off), (dmix, LANES, 0)],
                              row_outs=[(LANES, F32), (LANES, BF16)], acc_outs=[(1, LANES)])

    def mix_f_bwd(col, a, g, d):
        _, vjp = jax.vjp(mix_f_fn, a, g)
        return vjp(d)

    dao, dfgate = _tiles(mix_f_bwd, name="mix_fox_bwd", rows=rows, tm=rows, ncol=PAIRS,
                         row_ins=[(ao, LANES, 0), (proj, LANES, fg_off), (dmix, LANES, PAIRS)],
                         row_outs=[(LANES, F32), (LANES, BF16)])

    dfq, dfk, dfv, dfrow = _attention_backward(fqk, proj, frow, ao, lse, dao, rows)

    def fox_prep_bwd(col, w, xx, d):
        _, vjp = jax.vjp(_head_rms, w, xx)
        dw, dx = vjp(d)
        return dx, dw

    dfqk, d_wqk = [], []
    for part, d_n in enumerate((dfq, dfk)):
        dx_p, dw_p = _tiles(fox_prep_bwd, name="fox_prep_bwd_" + "qk"[part], rows=rows, tm=rows, ncol=PAIRS,
                            col_consts=[(w_qk, 1, LANES, part * PAIRS)],
                            row_ins=[(proj, LANES, fox_off + part * PAIRS), (d_n, LANES, 0)],
                            row_outs=[(LANES, BF16)], acc_outs=[(1, LANES)])
        dfqk.append(dx_p)
        d_wqk.append(dw_p)

    dq, dk, dv, dbetax, dgcx, dgrow = _gdn_backward(qkv, betax, gcx, grow, ssave, tsave, do_gdn, rows)
    dqkv, d_conv = [], []
    for part, d_n in enumerate((dq, dk, dv)):
        prep_bwd = lambda col, cw, xx, dy, is_qk=(part < 2): _gdn_prep_bwd(is_qk, cw, xx, dy)
        dx_p, dw_p = _tiles(prep_bwd, name="gdn_prep_bwd_" + "qkv"[part], rows=rows, tm=rows, ncol=PAIRS,
                            col_consts=[(conv_w, CONV_K, LANES, part * PAIRS)],
                            row_ins=[(proj, LANES, part * PAIRS), (d_n, LANES, 0)],
                            row_outs=[(LANES, BF16)], acc_outs=[(CONV_K, LANES)])
        dqkv.append(dx_p)
        d_conv.append(dw_p)
    d_conv = jnp.concatenate(d_conv, axis=1)

    def expand_bwd(col, b, g, db, dg):
        return (_dot32(db, b, _CONTRACT["nt"]), _dot32(dg, g, _CONTRACT["nt"]))

    dgates_b, dcums_g = _tiles(expand_bwd, name="expand_bwd", rows=rows, tm=tm, full_consts=[xb, xg],
                               row_ins=[(dbetax, WIDTH, 0), (dgcx, WIDTH, 0)],
                               row_outs=[(LANES, F32), (LANES, F32)])
    dcums_row = jnp.concatenate([jnp.zeros((rows, 8), F32), _rowform_to_lanes(dgrow, rows),
                                 dfrow.reshape(HEADS, rows).T, jnp.zeros((rows, LANES - 24), F32)], axis=1)

    def gates_bwd(col, lcv, lfv, a, dt, fb, pre, dgb, dcg, dcr):
        lane = _lane_ids(pre.shape)
        dgates = jnp.where(lane < 8, dgb, _cums_bwd(lcv, lfv, dcg + dcr))
        _, vjp = jax.vjp(_gates_elem, a, dt, fb, pre)
        da, ddt, dfb, dpre = vjp(dgates)
        return dpre, da, ddt, dfb

    dpre, d_a, d_dt, d_fb = _tiles(gates_bwd, name="gates_bwd", rows=rows, tm=rows,
                                   full_consts=[lc, lf, p_a, p_dt, p_fb],
                                   row_ins=[(proj, LANES, COL_SMALL), (dgates_b, LANES, 0), (dcums_g, LANES, 0),
                                            (dcums_row, LANES, 0)],
                                   row_outs=[(LANES, BF16)], acc_outs=[(1, LANES)] * 3)

    dproj = jnp.concatenate(dqkv + [dz] + dfqk + [dfv, dfgate, dpre], axis=1)
    dh1 = _mm(dproj, w_cat, dims="nn", name="d_h1", tk=D_CAT)
    g_cat = _mm(dproj, h1, dims="tn", name="g_in", tm=384, tn=D_MODEL, tk=rows)

    def norm1_bwd(col, w, xx, dh, dres):
        _, vjp = jax.vjp(_rms, xx, w)
        dx, dw = vjp(dh)
        return dx + dres, dw

    grad_x, d_norm1_w = _tiles(norm1_bwd, name="norm1_bwd", rows=rows, tm=tm, full_consts=[norm1_w],
                               row_ins=[(x, D_MODEL, 0), (dh1, D_MODEL, 0), (dx1, D_MODEL, 0)],
                               row_outs=[(D_MODEL, F32)], acc_outs=[(1, D_MODEL)])

    fold = lambda v: v.reshape(-1, HEAD_DIM).sum(axis=0)
    small = dict(
        loss=loss[0, 0],
        norm1_w=d_norm1_w, conv_w=d_conv, a_log=d_a[0, 8:16], dt_bias=d_dt[0, 8:16],
        out_norm_w=fold(d_on), f_bias=d_fb[0, 16:24], q_norm_w=fold(d_wqk[0]),
        k_norm_w=fold(d_wqk[1]), norm2_w=d_norm2_w, final_w=d_final_w)
    return grad_x, g_cat, g_out, g_gate, g_up, g_down, small


HBM_SPEC = pl.BlockSpec(memory_space=pltpu.HBM)


def _place():
    x, y, c = lax.axis_index("x"), lax.axis_index("y"), lax.axis_index("c")
    chips = [(1 - x, y), (x, 1 - y), (1 - x, 1 - y)]
    return x, y, c, 2 * x + y, (x, y, 1 - c), chips, [2 * cx + cy for cx, cy in chips]


def _remote(src, dst, send_sem, recv_sem, to):
    return pltpu.make_async_remote_copy(src_ref=src, dst_ref=dst, send_sem=send_sem, recv_sem=recv_sem,
                                        device_id=to, device_id_type=MESH)


def _allgather_weights(shards, conv):
    n = len(shards)
    halves = [s.shape[1] // 2 for s in shards]
    per = 6
    own_base = n * per + 3

    def body(*refs):
        ins, conv_in = refs[:n], refs[n]
        outs, conv_out = refs[n + 1:2 * n + 1], refs[2 * n + 1]
        send_sems, recv_sems = refs[2 * n + 2:]
        x, y, c, own, sib, chips, chip_idx = _place()

        def half(i, ref, hc):
            return ref.at[:, pl.ds(pl.multiple_of(hc * halves[i], LANES), halves[i])]

        sent = []
        for i, (src, dst) in enumerate(zip(list(ins) + [conv_in], list(outs) + [conv_out])):
            k = own_base + i
            sent.append(_remote(src, dst.at[own], send_sems.at[k], recv_sems.at[k], sib))
        for i in range(n):
            for j, chip in enumerate(chips):
                k = i * per + j
                sent.append(_remote(half(i, ins[i], c), half(i, outs[i].at[own], c),
                                    send_sems.at[k], recv_sems.at[k], (*chip, c)))
        for j, chip in enumerate(chips):
            k = n * per + j
            sent.append(_remote(conv_in, conv_out.at[own], send_sems.at[k], recv_sems.at[k], (*chip, c)))
        for cp in sent:
            cp.start()
        for i in range(n):
            for j in range(len(chips)):
                k = i * per + j
                landed = half(i, outs[i].at[chip_idx[j]], c)
                _remote(landed, landed, send_sems.at[k], recv_sems.at[k], sib).wait_recv()
                fwd = _remote(landed, landed, send_sems.at[k + 3], recv_sems.at[k + 3], sib)
                fwd.start()
                sent.append(fwd)
        for i in range(n):
            for j in range(len(chips)):
                k = i * per + 3 + j
                landed = half(i, outs[i].at[chip_idx[j]], 1 - c)
                _remote(landed, landed, send_sems.at[k], recv_sems.at[k], sib).wait_recv()
        for j in range(len(chips)):
            k = n * per + j
            landed = conv_out.at[chip_idx[j]]
            _remote(landed, landed, send_sems.at[k], recv_sems.at[k], sib).wait_recv()
        for i, dst in enumerate(list(outs) + [conv_out]):
            k = own_base + i
            landed = dst.at[own]
            _remote(landed, landed, send_sems.at[k], recv_sems.at[k], sib).wait_recv()
        for cp in sent:
            cp.wait_send()

    n_sem = own_base + n + 1
    out_shape = [jax.ShapeDtypeStruct((N_CHIPS,) + s.shape, s.dtype) for s in shards]
    out_shape.append(jax.ShapeDtypeStruct((N_CHIPS,) + conv.shape, conv.dtype))
    res = pl.pallas_call(
        body, name="allgather_weights", out_shape=out_shape,
        in_specs=[HBM_SPEC] * (n + 1), out_specs=[HBM_SPEC] * (n + 1),
        scratch_shapes=[pltpu.SemaphoreType.DMA((n_sem,)), pltpu.SemaphoreType.DMA((n_sem,))],
    )(*shards, conv)
    return res[:n], res[n]


SEM_SPEC = pl.BlockSpec(memory_space=pltpu.SEMAPHORE)
ANY_SPEC = pl.BlockSpec(memory_space=pl.ANY)
DATAFLOW = pltpu.SideEffectType.DATAFLOW_SIDE_EFFECTING


def _gather_plan(srcs, lands):
    x, y, c, own, sib, chips, chip_idx = _place()
    plan = []
    for src, land in zip(srcs, lands):
        for j, chip in enumerate(chips):
            plan.append((src, land.at[own], (*chip, c), land.at[chip_idx[j]]))
        plan.append((src, land.at[own], sib, land.at[own]))
    return plan


def _exchange_plan(srcs, lands):
    x, y, c, own, sib, chips, chip_idx = _place()
    plan = []
    for src, land in zip(srcs, lands):
        for j, chip in enumerate(chips):
            plan.append((src.at[chip_idx[j]], land.at[j], (*chip, c), land.at[j]))
    return plan


def _in_proj_plan(srcs, lands):
    x, y, c, own, sib, chips, chip_idx = _place()
    (w, conv), (w_land, conv_land) = srcs, lands
    hw = w.shape[1] // 2
    half = lambda ref: ref.at[:, pl.ds(pl.multiple_of(c * hw, LANES), hw)]
    plan = []
    for j, chip in enumerate(chips):
        plan.append((half(w), half(w_land.at[own]), (*chip, c), half(w_land.at[chip_idx[j]])))
        plan.append((conv, conv_land.at[own], (*chip, c), conv_land.at[chip_idx[j]]))
    plan.append((w, w_land.at[own], sib, w_land.at[own]))
    plan.append((conv, conv_land.at[own], sib, conv_land.at[own]))
    return plan


def _forward_halves(landed):
    hw = landed.shape[2] // 2

    def body(in_ref, out_ref, send_sems, recv_sems):
        x, y, c, own, sib, chips, chip_idx = _place()
        half = lambda ref, hc: ref.at[:, pl.ds(pl.multiple_of(hc * hw, LANES), hw)]
        sent = [_remote(half(out_ref.at[chip_idx[j]], c), half(out_ref.at[chip_idx[j]], c),
                        send_sems.at[j], recv_sems.at[j], sib) for j in range(3)]
        for cp in sent:
            cp.start()
        for j in range(3):
            other = half(out_ref.at[chip_idx[j]], 1 - c)
            _remote(other, other, send_sems.at[j], recv_sems.at[j], sib).wait_recv()
        for cp in sent:
            cp.wait_send()

    return pl.pallas_call(
        body, name="gather_in_forward", out_shape=jax.ShapeDtypeStruct(landed.shape, landed.dtype),
        in_specs=[HBM_SPEC], out_specs=HBM_SPEC, input_output_aliases={0: 0},
        scratch_shapes=[pltpu.SemaphoreType.DMA((3,)), pltpu.SemaphoreType.DMA((3,))],
    )(landed)


def _split_start(name, plan_fn, srcs, land_shapes, n_copies, after):
    n = len(srcs)

    def body(*refs):
        src_refs, land_refs = refs[:n], refs[n:2 * n]
        send_sems, recv_sems = refs[2 * n + 1], refs[2 * n + 2]
        token = refs[-1]
        for k, (src, dst, to, _) in enumerate(plan_fn(src_refs, land_refs)):
            _remote(src, dst, send_sems.at[k], recv_sems.at[k], to).start()
        token[...] = jnp.zeros_like(token)

    lands = [pltpu.with_memory_space_constraint(lax.empty(s.shape, s.dtype), pltpu.HBM) for s in land_shapes]
    srcs = [pltpu.with_memory_space_constraint(s, pltpu.HBM) for s in srcs]
    out_shape = ([pltpu.SemaphoreType.DMA((n_copies,)), pltpu.SemaphoreType.DMA((n_copies,))]
                 + [pltpu.HBM(s.shape, s.dtype) for s in srcs] + [pltpu.HBM(s.shape, s.dtype) for s in land_shapes]
                 + [jax.ShapeDtypeStruct((8, LANES), F32)])
    res = pl.pallas_call(
        body, name=name, out_shape=out_shape,
        in_specs=[HBM_SPEC] * (2 * n) + [ANY_SPEC],
        out_specs=[SEM_SPEC, SEM_SPEC] + [HBM_SPEC] * (2 * n) + [pl.BlockSpec(memory_space=pltpu.VMEM)],
        input_output_aliases={i: 2 + i for i in range(2 * n)},
        compiler_params=pltpu.CompilerParams(has_side_effects=DATAFLOW),
    )(*srcs, *lands, after)
    return dict(sems=res[:2], srcs=res[2:2 + n], lands=res[2 + n:2 + 2 * n], token=res[-1], n=n)


def _split_wait(name, plan_fn, started, after):
    n = started["n"]

    def body(*refs):
        src_refs, land_refs = refs[:n], refs[n:2 * n]
        send_sems, recv_sems = refs[2 * n], refs[2 * n + 1]
        for k, (src, _, to, landed) in enumerate(plan_fn(src_refs, land_refs)):
            copy = _remote(src, landed, send_sems.at[k], recv_sems.at[k], to)
            copy.wait_send()
            copy.wait_recv()

    srcs, lands = started["srcs"], started["lands"]
    after = list(after) if isinstance(after, (list, tuple)) else [after]
    res = pl.pallas_call(
        body, name=name,
        out_shape=[pltpu.HBM(s.shape, s.dtype) for s in srcs] + [pltpu.HBM(s.shape, s.dtype) for s in lands],
        in_specs=[HBM_SPEC] * (2 * n) + [SEM_SPEC, SEM_SPEC] + [ANY_SPEC] * len(after),
        out_specs=[HBM_SPEC] * (2 * n),
        input_output_aliases={i: i for i in range(2 * n)},
        compiler_params=pltpu.CompilerParams(has_side_effects=DATAFLOW),
    )(*srcs, *lands, *started["sems"], *after)
    return res[n:]


def _swap_halves(stacks, name):
    n = len(stacks)

    def body(*refs):
        ins, outs = refs[:n], refs[n:2 * n]
        send_sems, recv_sems = refs[2 * n:]
        x, y, c, own, sib, chips, chip_idx = _place()
        cps = []
        for i in range(n):
            h = stacks[i].shape[2] // 2
            src = ins[i].at[:, :, pl.ds(pl.multiple_of((1 - c) * h, LANES), h)]
            cps.append(_remote(src, outs[i], send_sems.at[i], recv_sems.at[i], sib))
        for cp in cps:
            cp.start()
        for cp in cps:
            cp.wait()

    out_shape = [jax.ShapeDtypeStruct((N_CHIPS, s.shape[1], s.shape[2] // 2), s.dtype) for s in stacks]
    return pl.pallas_call(
        body, name=name, out_shape=out_shape,
        in_specs=[HBM_SPEC] * n, out_specs=[HBM_SPEC] * n,
        scratch_shapes=[pltpu.SemaphoreType.DMA((n,)), pltpu.SemaphoreType.DMA((n,))],
    )(*stacks)


def _add_half(stack, landed, place, name):
    _, rows, h = landed.shape

    def body(place_ref, a_ref, b_ref, o_ref, own_ref):
        part = (a_ref[...].astype(F32) + b_ref[...].astype(F32)).astype(o_ref.dtype)
        o_ref[...] = part

        @pl.when(pl.program_id(0) == place_ref[1])
        def _():
            own_ref[...] = part[0]

    return pl.pallas_call(
        body, name=name,
        out_shape=[jax.ShapeDtypeStruct(landed.shape, BF16), jax.ShapeDtypeStruct((rows, h), BF16)],
        grid_spec=pltpu.PrefetchScalarGridSpec(
            num_scalar_prefetch=1, grid=(N_CHIPS,),
            in_specs=[pl.BlockSpec((1, rows, h), lambda j, p: (j, 0, p[0])),
                      pl.BlockSpec((1, rows, h), lambda j, p: (j, 0, 0))],
            out_specs=[pl.BlockSpec((1, rows, h), lambda j, p: (j, 0, 0)),
                       pl.BlockSpec((rows, h), lambda j, p: (0, 0))]),
        compiler_params=_params(("arbitrary",)),
    )(place, stack, landed)


def _exchange_partials(parts):
    n = len(parts)

    def body(*refs):
        ins, outs = refs[:n], refs[n:2 * n]
        send_sems, recv_sems = refs[2 * n:]
        x, y, c, own, sib, chips, chip_idx = _place()
        sent = []
        for i in range(n):
            for j, chip in enumerate(chips):
                k = i * 3 + j
                sent.append(_remote(ins[i].at[chip_idx[j]], outs[i].at[j], send_sems.at[k], recv_sems.at[k],
                                    (*chip, c)))
        for cp in sent:
            cp.start()
        for i in range(n):
            for j in range(len(chips)):
                k = i * 3 + j
                landed = outs[i].at[j]
                _remote(landed, landed, send_sems.at[k], recv_sems.at[k], sib).wait_recv()
        for cp in sent:
            cp.wait_send()

    return pl.pallas_call(
        body, name="rs_exchange_partials",
        out_shape=[jax.ShapeDtypeStruct((3,) + p.shape[1:], p.dtype) for p in parts],
        in_specs=[HBM_SPEC] * n, out_specs=[HBM_SPEC] * n,
        scratch_shapes=[pltpu.SemaphoreType.DMA((3 * n,)), pltpu.SemaphoreType.DMA((3 * n,))],
    )(*parts)


def _sum_partials(own_part, landed, name):
    _, h, cols = landed.shape

    def body(own_ref, a_ref, o_ref):
        acc = own_ref[...].astype(F32)
        for s in range(3):
            acc = acc + a_ref[s].astype(F32)
        o_ref[...] = acc

    return pl.pallas_call(
        body, name=name, out_shape=jax.ShapeDtypeStruct((h, cols), F32), grid=(1,),
        in_specs=[pl.BlockSpec((h, cols), lambda i: (0, 0)), pl.BlockSpec(landed.shape, lambda i: (0, 0, 0))],
        out_specs=pl.BlockSpec((h, cols), lambda i: (0, 0)),
        compiler_params=_params(("arbitrary",)),
    )(own_part, landed)


def _share_halves(halves, name):
    n = len(halves)

    def body(*refs):
        ins, outs = refs[:n], refs[n:2 * n]
        send_sems, recv_sems = refs[2 * n:]
        x, y, c, own, sib, chips, chip_idx = _place()
        cps = [_remote(ins[i], outs[i], send_sems.at[i], recv_sems.at[i], sib) for i in range(n)]
        for cp in cps:
            cp.start()
        for cp in cps:
            cp.wait()

    return pl.pallas_call(
        body, name=name,
        out_shape=[jax.ShapeDtypeStruct(p.shape, p.dtype) for p in halves],
        in_specs=[HBM_SPEC] * n, out_specs=[HBM_SPEC] * n,
        scratch_shapes=[pltpu.SemaphoreType.DMA((n,)), pltpu.SemaphoreType.DMA((n,))],
    )(*halves)


def _allreduce_small(packed):
    rows = packed.shape[0]
    n_dev = 8

    def body(in_ref, out_ref, gath, send_sems, recv_sems):
        x, y, c = lax.axis_index("x"), lax.axis_index("y"), lax.axis_index("c")
        me = 4 * x + 2 * y + c
        gath[me] = in_ref[...]
        cps = []
        for k in range(1, n_dev):
            fx, fy, fc = (k >> 2) & 1, (k >> 1) & 1, k & 1
            to = (x ^ fx, y ^ fy, c ^ fc)
            cps.append(_remote(in_ref, gath.at[me], send_sems.at[k - 1], recv_sems.at[k - 1], to))
        for cp in cps:
            cp.start()
        for k in range(1, n_dev):
            fx, fy, fc = (k >> 2) & 1, (k >> 1) & 1, k & 1
            src = 4 * (x ^ fx) + 2 * (y ^ fy) + (c ^ fc)
            slot = gath.at[src]
            _remote(slot, slot, send_sems.at[k - 1], recv_sems.at[k - 1], (x, y, c)).wait_recv()
        for cp in cps:
            cp.wait_send()
        acc = gath[0]
        for d in range(1, n_dev):
            acc = acc + gath[d]
        out_ref[...] = acc

    vm = pl.BlockSpec(memory_space=pltpu.VMEM)
    return pl.pallas_call(
        body, name="allreduce_small", out_shape=jax.ShapeDtypeStruct(packed.shape, F32),
        in_specs=[vm], out_specs=vm,
        scratch_shapes=[pltpu.VMEM((n_dev, rows, LANES), F32),
                        pltpu.SemaphoreType.DMA((n_dev - 1,)), pltpu.SemaphoreType.DMA((n_dev - 1,))],
    )(packed)


def _adam(col, w, g, m, v):
    m2 = ADAM_B1 * m + (1.0 - ADAM_B1) * g
    v2 = ADAM_B2 * v + (1.0 - ADAM_B2) * (g * g)
    m_hat = m2 / (1.0 - ADAM_B1 ** ADAM_STEP)
    v_hat = v2 / (1.0 - ADAM_B2 ** ADAM_STEP)
    delta = -ADAM_LR * (m_hat / (jnp.sqrt(v_hat) + ADAM_EPS) + ADAM_WD * w)
    return delta, m2, v2


def _adam_call(w, g, m, v, name):
    rows, cols = w.shape
    tm = rows
    for cand in (256, 352, 176, 128, 64, 48, 16, 8):
        if rows % cand == 0:
            tm = cand
            break
    return _tiles(_adam, name=name, rows=rows, tm=tm,
                  row_ins=[(w, cols, 0), (g, cols, 0), (m, cols, 0), (v, cols, 0)],
                  row_outs=[(cols, F32)] * 3)


def _adam_big(w, g_mine, g_other, m, v, place, name):
    rows, cols = w.shape
    tc = 256
    nt = cols // 2 // tc

    def body(place_ref, w_ref, gm_ref, go_ref, m_ref, v_ref, g_out, d_out, m_out, v_out):
        g = jnp.where(pl.program_id(0) == place_ref[0], gm_ref[...], go_ref[...])
        d, m2, v2 = _adam(None, w_ref[...], g, m_ref[...], v_ref[...])
        g_out[...] = g
        d_out[...] = d
        m_out[...] = m2
        v_out[...] = v2

    full = pl.BlockSpec((rows, tc), lambda hh, i, p: (0, hh * nt + i))
    half = pl.BlockSpec((rows, tc), lambda hh, i, p: (0, i))
    return pl.pallas_call(
        body, name=name, out_shape=[jax.ShapeDtypeStruct(w.shape, F32)] * 4,
        grid_spec=pltpu.PrefetchScalarGridSpec(
            num_scalar_prefetch=1, grid=(2, nt),
            in_specs=[full, half, half, full, full], out_specs=[full] * 4),
        compiler_params=_params(("arbitrary", "arbitrary")),
    )(place, w, g_mine, g_other, m, v)


def _adam_untiled_rows(w, g_mine, g_other, m, v, place, name):
    rows, _, cols = w.shape
    tc = 256
    nt = cols // 2 // tc
    rb = next(r for r in (206, 128, 103, rows) if rows % r == 0)

    def body(place_ref, w_ref, gm_ref, go_ref, m_ref, v_ref, g_out, d_out, m_out, v_out):
        g = jnp.where(pl.program_id(0) == place_ref[0], gm_ref[...], go_ref[...])
        d, m2, v2 = _adam(None, w_ref[...], g, m_ref[...], v_ref[...])
        g_out[...] = g
        d_out[...] = d
        m_out[...] = m2
        v_out[...] = v2

    full = pl.BlockSpec((rb, 1, tc), lambda hh, i, r, p: (r, 0, hh * nt + i))
    half = pl.BlockSpec((rb, 1, tc), lambda hh, i, r, p: (r, 0, i))
    return pl.pallas_call(
        body, name=name, out_shape=[jax.ShapeDtypeStruct(w.shape, F32)] * 4,
        grid_spec=pltpu.PrefetchScalarGridSpec(
            num_scalar_prefetch=1, grid=(2, nt, rows // rb),
            in_specs=[full, half, half, full, full], out_specs=[full] * 4),
        compiler_params=_params(("arbitrary", "arbitrary", "arbitrary")),
    )(place, w, g_mine, g_other, m, v)


def _pack(arrays, zero=None):
    flat = []
    for a in arrays:
        a = a.reshape(-1).astype(F32)
        if zero is not None:
            a = a + zero
        flat.append(jnp.pad(a, (0, (-a.size) % LANES)))
    out = jnp.concatenate(flat)
    out = jnp.pad(out, (0, (-out.size) % (8 * LANES)))
    return out.reshape(-1, LANES)


def _unpack(packed, shapes):
    flat = packed.reshape(-1)
    out, off = [], 0
    for s in shapes:
        size = int(np.prod(s))
        out.append(flat[off:off + size].reshape(s))
        off += size + (-size) % LANES
    return out


def kernel(x, norm1_w, w_in, gdn_conv_w, gdn_A_log, gdn_dt_bias, gdn_out_norm_w, fox_f_bias, fox_q_norm_w, fox_k_norm_w, w_out, norm2_w, w_ffn_gate, w_ffn_up, w_ffn_down, final_norm_w, loss_target, m_norm1_w, m_w_in, m_gdn_conv_w, m_gdn_A_log, m_gdn_dt_bias, m_gdn_out_norm_w, m_fox_f_bias, m_fox_q_norm_w, m_fox_k_norm_w, m_w_out, m_norm2_w, m_w_ffn_gate, m_w_ffn_up, m_w_ffn_down, m_final_norm_w, v_norm1_w, v_w_in, v_gdn_conv_w, v_gdn_A_log, v_gdn_dt_bias, v_gdn_out_norm_w, v_fox_f_bias, v_fox_q_norm_w, v_fox_k_norm_w, v_w_out, v_norm2_w, v_w_ffn_gate, v_w_ffn_up, v_w_ffn_down, v_final_norm_w):
    cx, cy, cc = lax.axis_index("x"), lax.axis_index("y"), lax.axis_index("c")
    own = 2 * cx + cy
    place = jnp.stack([cc, own]).astype(jnp.int32)

    names = ["w_in", "w_out", "w_gate", "w_up", "w_down"]
    is_t = [True, False, True, True, False]
    to_t = lambda a, t: a[0].T if t else a[0]
    from_t = lambda a, t: (a.T if t else a)[None]
    big_w = [to_t(a, t) for a, t in zip([w_in, w_out, w_ffn_gate, w_ffn_up, w_ffn_down], is_t)]
    big_m = [to_t(a, t) for a, t in zip([m_w_in, m_w_out, m_w_ffn_gate, m_w_ffn_up, m_w_ffn_down], is_t)]
    big_v = [to_t(a, t) for a, t in zip([v_w_in, v_w_out, v_w_ffn_gate, v_w_ffn_up, v_w_ffn_down], is_t)]
    shards = [big_w[0].astype(BF16)]
    small_w = [norm1_w, gdn_conv_w, gdn_A_log, gdn_dt_bias, gdn_out_norm_w, fox_f_bias, fox_q_norm_w,
               fox_k_norm_w, norm2_w, final_norm_w]
    small_m = [m_norm1_w, m_gdn_conv_w, m_gdn_A_log, m_gdn_dt_bias, m_gdn_out_norm_w, m_fox_f_bias,
               m_fox_q_norm_w, m_fox_k_norm_w, m_norm2_w, m_final_norm_w]
    small_v = [v_norm1_w, v_gdn_conv_w, v_gdn_A_log, v_gdn_dt_bias, v_gdn_out_norm_w, v_fox_f_bias,
               v_fox_q_norm_w, v_fox_k_norm_w, v_norm2_w, v_final_norm_w]
    first = _split_start("gather_in_start", _in_proj_plan, [shards[0], gdn_conv_w[0]],
                         [jax.ShapeDtypeStruct((N_CHIPS,) + shards[0].shape, BF16),
                          jax.ShapeDtypeStruct((N_CHIPS, CONV_K, 3 * WIDTH // N_CHIPS), F32)],
                         n_copies=8, after=shards[0])
    small_packed = [_pack(p, first["token"][0, 0]) for p in (small_w, small_m, small_v)]
    shards += [(w + first["token"][0, 0]).astype(BF16) for w in big_w[1:]]
    rest = {}

    def first_weights(after):
        w_in_g, conv_g = _split_wait("gather_in_wait", _in_proj_plan, first, [after] + small_packed)
        w_in_g = _forward_halves(w_in_g)
        rest.update(_split_start("gather_rest_start", _gather_plan, shards[1:],
                                 [jax.ShapeDtypeStruct((N_CHIPS,) + s.shape, BF16) for s in shards[1:]],
                                 n_copies=4 * len(shards[1:]), after=w_in_g))
        w_cat = _cat_weights(w_in_g.reshape(D_IN, D_MODEL))
        return w_cat + rest["token"][0, 0].astype(BF16), conv_g.transpose(1, 0, 2).reshape(CONV_K, 3 * WIDTH)

    def late_weights(after):
        w_out_g, w_gate_g, w_up_g, w_down_g = _split_wait("gather_rest_wait", _gather_plan, rest, after)
        return w_out_g.reshape(D_MODEL, D_MODEL), w_gate_g, w_up_g, w_down_g

    def start_reduction(stacks, nms, tag):
        landed = _swap_halves(stacks, "rs_swap_" + tag)
        added = [_add_half(s, l, place, "rs_add_" + nm) for s, l, nm in zip(stacks, landed, nms)]
        parts = [a[0] for a in added]
        started = _split_start("exchange_" + tag + "_start", _exchange_plan, parts,
                               [jax.ShapeDtypeStruct((3,) + p.shape[1:], p.dtype) for p in parts],
                               n_copies=3 * len(parts), after=parts[0])
        return dict(own=[a[1] for a in added], started=started, tag=tag, names=nms)

    def finish_reduction(red, after, updates):
        landed = _split_wait("exchange_" + red["tag"] + "_wait", _exchange_plan, red["started"], after)
        halves = [_sum_partials(o, p, "rs_sum_" + nm) for o, p, nm in zip(red["own"], landed, red["names"])]
        others = _share_halves(halves, "rs_share_" + red["tag"])
        return [upd(gm, go) for upd, gm, go in zip(updates, halves, others)]

    def transport_update(b):
        def upd(gm, go):
            res = _adam_big(big_w[b], gm, go, big_m[b], big_v[b], place, "adam_" + names[b])
            early_done.append(res[1])
            return [from_t(a, is_t[b]) for a in res]
        return upd

    early_done = []

    def w_in_update(gm, go):
        rows3 = lambda a: jnp.transpose(a, (2, 0, 1))
        res = _adam_untiled_rows(rows3(w_in), gm[:, None, :], go[:, None, :], rows3(m_w_in), rows3(v_w_in),
                                 place, "adam_w_in")
        return [jnp.transpose(a, (1, 2, 0)) for a in res]

    early = {}

    def early_grads_ready(g_out, g_gate, g_up, g_down):
        stacks = [g_out.reshape(N_CHIPS, D_MODEL // N_CHIPS, D_MODEL), g_gate, g_up, g_down]
        early.update(start_reduction(stacks, names[1:], "early"))
        return early["started"]["token"][0, 0]

    grad_x, g_cat, _, _, _, _, small = _local_step(
        x[0], loss_target[0], norm1_w + first["token"][0, 0], gdn_A_log[0], gdn_dt_bias[0],
        gdn_out_norm_w[0], fox_f_bias[0], fox_q_norm_w[0], fox_k_norm_w[0], norm2_w, final_norm_w.reshape(1, -1),
        first_weights, late_weights, early_grads_ready)

    late = start_reduction([_uncat_grad(g_cat).reshape(N_CHIPS, D_IN // N_CHIPS, D_MODEL)], names[:1], "w_in")
    big_upd = finish_reduction(early, late["started"]["token"], [transport_update(b) for b in range(1, 5)])

    order = ["norm1_w", "conv_w", "a_log", "dt_bias", "out_norm_w", "f_bias", "q_norm_w", "k_norm_w",
             "norm2_w", "final_w"]
    red = _allreduce_small(_pack([small[k] for k in order] + [small["loss"]]))
    red_shapes = [(1, D_MODEL), (CONV_K, 3 * WIDTH), (1, HEADS), (1, HEADS), (1, HEAD_DIM), (1, HEADS),
                  (1, HEAD_DIM), (1, HEAD_DIM), (1, D_MODEL), (D_MODEL,), ()]
    red_list = _unpack(red, red_shapes)
    loss = red_list[-1]
    small_g = dict(zip(order, red_list[:-1]))
    shard_cols = 3 * WIDTH // N_CHIPS
    small_g["conv_w"] = lax.dynamic_slice_in_dim(small_g["conv_w"], own * shard_cols, shard_cols, axis=1)[None]
    small_gl = [small_g[k].reshape(w.shape) for k, w in zip(order, small_w)]
    s_delta, s_m, s_v = _adam_call(small_packed[0], _pack(small_gl), small_packed[1], small_packed[2], "adam_small")
    big_upd = finish_reduction(late, [s_delta] + early_done, [w_in_update]) + big_upd
    shapes = [w.shape for w in small_w]
    s_delta, s_m, s_v = _unpack(s_delta, shapes), _unpack(s_m, shapes), _unpack(s_v, shapes)

    big_pos = {1: 0, 9: 1, 11: 2, 12: 3, 13: 4}
    small_pos = {0: 0, 2: 1, 3: 2, 4: 3, 5: 4, 6: 5, 7: 6, 8: 7, 10: 8, 14: 9}
    grads, deltas, new_m, new_v = [], [], [], []
    for pos in range(15):
        if pos in big_pos:
            b = big_pos[pos]
            g, d, m2, v2 = big_upd[b]
            grads.append(g)
            deltas.append(d)
            new_m.append(m2)
            new_v.append(v2)
        else:
            s = small_pos[pos]
            grads.append(small_gl[s])
            deltas.append(s_delta[s])
            new_m.append(s_m[s])
            new_v.append(s_v[s])
    return (loss, grad_x[None], *grads, *deltas, *new_m, *new_v)
```

```python
import jax
import jax.numpy as jnp
import numpy as np
from jax import lax
from jax.experimental import pallas as pl
from jax.experimental.pallas import tpu as pltpu

F32 = jnp.float32
BF16 = jnp.bfloat16

D_MODEL = 1024
HEADS = 8
HEAD_DIM = 64
PAIRS = HEADS // 2
WIDTH = HEADS * HEAD_DIM
CHUNK = 64
CONV_K = 4
D_FF = 2816
FF_SHARD = D_FF // 4
EPS = 1e-6
SCALE = HEAD_DIM ** -0.5
LANES = 128
N_CHIPS = 4
D_IN = 4120
D_CAT = 4224
COL_SMALL = 4096 // LANES

ADAM_LR = 0.001
ADAM_B1 = 0.9
ADAM_B2 = 0.999
ADAM_EPS = 1e-08
ADAM_WD = 0.01
ADAM_STEP = 10

VMEM_LIMIT = 56 * 1024 * 1024
MESH = pl.DeviceIdType.MESH
HIGHEST = lax.Precision.HIGHEST


def _params(sem):
    return pltpu.CompilerParams(dimension_semantics=sem, vmem_limit_bytes=VMEM_LIMIT)


_CONTRACT = {"nn": ((1,), (0,)), "nt": ((1,), (1,)), "tn": ((0,), (0,))}


def _mm(a, b, *, dims, name, out_dtype=F32, add=None, tm=1024, tn=512, tk=512):
    if dims == "nn":
        (m, k), (k2, n) = a.shape, b.shape
    elif dims == "nt":
        (m, k), (n, k2) = a.shape, b.shape
    else:
        (k, m), (k2, n) = a.shape, b.shape
    assert k == k2, (a.shape, b.shape, dims)
    tm, tn, tk = min(tm, m), min(tn, n), min(tk, k)
    assert m % tm == 0 and n % tn == 0 and k % tk == 0, (m, n, k, tm, tn, tk)
    nk = k // tk
    a_spec = (pl.BlockSpec((tk, tm), lambda i, j, kk: (kk, i)) if dims == "tn"
              else pl.BlockSpec((tm, tk), lambda i, j, kk: (i, kk)))
    b_spec = (pl.BlockSpec((tn, tk), lambda i, j, kk: (j, kk)) if dims == "nt"
              else pl.BlockSpec((tk, tn), lambda i, j, kk: (kk, j)))
    o_spec = pl.BlockSpec((tm, tn), lambda i, j, kk: (i, j))
    contract = (_CONTRACT[dims], ((), ()))
    has_add = add is not None

    def body(*refs):
        a_ref, b_ref = refs[:2]
        add_ref = refs[2] if has_add else None
        o_ref = refs[3] if has_add else refs[2]
        part = lax.dot_general(a_ref[...].astype(BF16), b_ref[...].astype(BF16), contract,
                               preferred_element_type=F32)

        def finish(r):
            if has_add:
                r = r + add_ref[...].astype(F32)
            o_ref[...] = r.astype(out_dtype)

        if nk == 1:
            finish(part)
            return
        acc = refs[-1]
        kk = pl.program_id(2)

        @pl.when(kk == 0)
        def _():
            acc[...] = part

        @pl.when(kk > 0)
        def _():
            acc[...] += part

        @pl.when(kk == nk - 1)
        def _():
            finish(acc[...])

    ins = [a, b] + ([add] if has_add else [])
    in_specs = [a_spec, b_spec] + ([o_spec] if has_add else [])
    return pl.pallas_call(
        body, name=name, grid=(m // tm, n // tn, nk),
        in_specs=in_specs, out_specs=o_spec,
        out_shape=jax.ShapeDtypeStruct((m, n), out_dtype),
        scratch_shapes=[pltpu.VMEM((tm, tn), F32)] if nk > 1 else [],
        compiler_params=_params(("parallel", "parallel", "arbitrary")),
    )(*ins)


def _mm_blocks(a, b, *, name, grid, a_spec, b_spec, o_spec, out_shape, dims, n_sum=0, add=None, add_spec=None):
    contract = (_CONTRACT[dims], ((), ()))
    has_add = add is not None

    def body(*refs):
        a_ref, b_ref = refs[:2]
        o_ref = refs[-1]
        dot = lambda x, y: lax.dot_general(x.astype(BF16), y.astype(BF16), contract, preferred_element_type=F32)
        if n_sum:
            r = dot(a_ref[0], b_ref[0])
            for s in range(1, n_sum):
                r = r + dot(a_ref[s], b_ref[s])
        else:
            r = dot(a_ref[...], b_ref[...])
        if has_add:
            r = r + refs[2][...].astype(F32)
        o_ref[...] = r.astype(o_ref.dtype)

    return pl.pallas_call(
        body, name=name, grid=grid,
        in_specs=[a_spec, b_spec] + ([add_spec] if has_add else []), out_specs=o_spec, out_shape=out_shape,
        compiler_params=_params(("parallel",) * len(grid)),
    )(*([a, b] + ([add] if has_add else [])))


def _tiles(fn, *, name, rows, tm, ncol=1, row_ins=(), col_consts=(), full_consts=(),
           row_outs=(), acc_outs=()):
    nt = rows // tm
    assert rows % tm == 0
    n_full, n_col, n_row = len(full_consts), len(col_consts), len(row_ins)
    n_ro, n_acc = len(row_outs), len(acc_outs)

    def body(*refs):
        ins = refs[:n_full + n_col + n_row]
        outs = refs[n_full + n_col + n_row:]
        i = pl.program_id(1)
        res = fn(pl.program_id(0), *[r[...] for r in ins])
        for r, v in zip(outs[:n_ro], res[:n_ro]):
            r[...] = v.astype(r.dtype)
        if n_acc:
            @pl.when(i == 0)
            def _():
                for r in outs[n_ro:]:
                    r[...] = jnp.zeros_like(r)
            for r, v in zip(outs[n_ro:], res[n_ro:]):
                r[...] += v

    in_specs = [pl.BlockSpec(a.shape, lambda j, i, nd=a.ndim: (0,) * nd) for a in full_consts]
    in_specs += [pl.BlockSpec((nr, w), lambda j, i, o=o: (0, o + j)) for (_, nr, w, o) in col_consts]
    in_specs += [pl.BlockSpec((tm, w), lambda j, i, o=o: (i, o + j)) for (_, w, o) in row_ins]
    out_specs = [pl.BlockSpec((tm, w), lambda j, i: (i, j)) for (w, _) in row_outs]
    out_specs += [pl.BlockSpec((nr, w), lambda j, i: (0, j)) for (nr, w) in acc_outs]
    out_shape = [jax.ShapeDtypeStruct((rows, w * ncol), dt) for (w, dt) in row_outs]
    out_shape += [jax.ShapeDtypeStruct((nr, w * ncol), F32) for (nr, w) in acc_outs]
    args = list(full_consts) + [c[0] for c in col_consts] + [r[0] for r in row_ins]
    out = pl.pallas_call(
        body, name=name, grid=(ncol, nt), in_specs=in_specs, out_specs=out_specs, out_shape=out_shape,
        compiler_params=_params(("parallel", "arbitrary")),
    )(*args)
    return out


def _rms(x, w):
    return x * lax.rsqrt(jnp.mean(x * x, axis=-1, keepdims=True) + EPS) * w


def _lane_lo(shape):
    return lax.broadcasted_iota(jnp.int32, shape, len(shape) - 1) < HEAD_DIM


def _pair_sum(x):
    lo = _lane_lo(x.shape)
    s0 = jnp.sum(jnp.where(lo, x, 0.0), axis=-1, keepdims=True)
    s1 = jnp.sum(jnp.where(lo, 0.0, x), axis=-1, keepdims=True)
    return jnp.where(lo, s0, s1)


def _head_col(x, lo, h):
    keep = lo if h == 0 else jnp.logical_not(lo)
    return jnp.max(jnp.where(keep, x, -jnp.inf), axis=-1, keepdims=True)


def _softplus(x):
    return jnp.maximum(x, 0.0) + jnp.log1p(jnp.exp(-jnp.abs(x)))


def _silu(x):
    return x * jax.nn.sigmoid(x)


def _dot(a, b, contract):
    return lax.dot_general(a.astype(BF16), b.astype(BF16), (contract, ((), ())),
                           preferred_element_type=F32)


def _dot32(a, b, contract):
    return lax.dot_general(a, b, (contract, ((), ())), precision=HIGHEST, preferred_element_type=F32)


def _bd(y):
    yy = jnp.concatenate([y, y], axis=0)
    r = lax.broadcasted_iota(jnp.int32, yy.shape, 0) < HEAD_DIM
    c = lax.broadcasted_iota(jnp.int32, yy.shape, 1) < HEAD_DIM
    return jnp.where(r == c, yy, 0.0)


def _pp(x, y):
    return _dot(x, _bd(y), _CONTRACT["nn"])


def _pp_nt(x, y):
    return _dot(x, _bd(y), _CONTRACT["nt"])


def _pp_tn(x, y):
    full = _dot(x, y, _CONTRACT["tn"])
    return jnp.where(_lane_lo((HEAD_DIM, LANES)), full[:HEAD_DIM], full[HEAD_DIM:])


def _gdn_masks():
    row = lax.broadcasted_iota(jnp.int32, (CHUNK, LANES), 0)
    col = lax.broadcasted_iota(jnp.int32, (CHUNK, LANES), 1) % HEAD_DIM
    return row, col


def _interleave(chains):
    live = list(chains)
    while live:
        for g in list(live):
            try:
                next(g)
            except StopIteration:
                live.remove(g)


def _gdn_forward(qkv, betax, gcx, grow, rows):
    nchunk = rows // CHUNK

    def body(q_ref, k_ref, v_ref, bx_ref, gx_ref, gr_ref, o_ref, ss_ref, ts_ref, state):
        n = pl.program_id(0)

        @pl.when(n == 0)
        def _():
            state[...] = jnp.zeros_like(state)

        row, col = _gdn_masks()
        incl, strict = col <= row, col < row

        def chain(p):
            lanes = pl.ds(p * LANES, LANES)
            q, k, v, bx, gx = q_ref[:, lanes], k_ref[:, lanes], v_ref[:, lanes], bx_ref[:, lanes], gx_ref[:, lanes]
            gr = gr_ref[0, p]
            glast = gx_ref[pl.ds(CHUNK - 1, 1), lanes]
            s = state[p]
            dm = jnp.where(incl, jnp.exp(jnp.minimum(gx - gr, 0.0)), 0.0)
            kb, vb, eg, qs = k * bx, v * bx, jnp.exp(gx), q * SCALE
            yield
            big_g, big_p = _pp_nt(kb, k), _pp_nt(qs, k)
            yield
            x = -jnp.where(strict, big_g * dm, 0.0)
            att = jnp.where(incl, big_p * dm, 0.0)
            tm = jnp.where(row == col, 1.0, 0.0) + x
            x = _pp(x, x)
            yield
            for _ in range(4):
                step, x = _pp(tm, x), _pp(x, x)
                yield
                tm = tm + step
            tm = tm + _pp(tm, x)
            yield
            u, w = _pp(tm, vb), _pp(tm, kb * eg)
            yield
            ws, qgs = _pp(w, s), _pp(qs * eg, s)
            yield
            vn = u - ws
            kd = k * jnp.exp(glast - gx)
            avn, upd = _pp(att, vn), _pp_tn(kd, vn)
            yield
            ss_ref[0, p] = s
            ts_ref[0, p] = tm
            o_ref[:, lanes] = qgs + avn
            state[p] = s * jnp.exp(glast) + upd

        _interleave([chain(p) for p in range(PAIRS)])

    blk = lambda j: pl.BlockSpec((CHUNK, WIDTH), lambda n, j=j: (n, j))
    sv = pl.BlockSpec((1, PAIRS, CHUNK, LANES), lambda n: (n, 0, 0, 0))
    return pl.pallas_call(
        body, name="gdn_fwd", grid=(nchunk,),
        in_specs=[blk(0), blk(1), blk(2), blk(0), blk(0),
                  pl.BlockSpec((1, PAIRS, 1, LANES), lambda n: (n, 0, 0, 0))],
        out_specs=[blk(0), sv, sv],
        out_shape=[jax.ShapeDtypeStruct((rows, WIDTH), F32),
                   jax.ShapeDtypeStruct((nchunk, PAIRS, CHUNK, LANES), F32),
                   jax.ShapeDtypeStruct((nchunk, PAIRS, CHUNK, LANES), F32)],
        scratch_shapes=[pltpu.VMEM((PAIRS, CHUNK, LANES), F32)],
        compiler_params=_params(("arbitrary",)),
    )(qkv, qkv, qkv, betax, gcx, grow)


def _gdn_backward(qkv, betax, gcx, grow, ssave, tsave, do, rows):
    nchunk = rows // CHUNK

    def body(q_ref, k_ref, v_ref, bx_ref, gx_ref, gr_ref, ss_ref, ts_ref, do_ref,
             dq_ref, dk_ref, dv_ref, dbx_ref, dgx_ref, dgr_ref, dstate):
        n = pl.program_id(0)

        @pl.when(n == 0)
        def _():
            dstate[...] = jnp.zeros_like(dstate)

        row, col = _gdn_masks()
        incl, strict = col <= row, col < row

        def chain(p):
            lanes = pl.ds(p * LANES, LANES)
            q, k, v, bx, gx = q_ref[:, lanes], k_ref[:, lanes], v_ref[:, lanes], bx_ref[:, lanes], gx_ref[:, lanes]
            gr = gr_ref[0, p]
            glast = gx_ref[pl.ds(CHUNK - 1, 1), lanes]
            s, tm, d_o = ss_ref[0, p], ts_ref[0, p], do_ref[:, lanes]
            ds_out = dstate[p]
            dm = jnp.where(incl, jnp.exp(jnp.minimum(gx - gr, 0.0)), 0.0)
            kb, vb, eg, qs = k * bx, v * bx, jnp.exp(gx), q * SCALE
            kbg, qg = kb * eg, qs * eg
            ed = jnp.exp(glast - gx)
            kd = k * ed
            eglast = jnp.exp(glast)
            yield
            big_g, big_p = _pp_nt(kb, k), _pp_nt(qs, k)
            u, w = _pp(tm, vb), _pp(tm, kbg)
            dqg, kds = _pp_nt(d_o, s), _pp(kd, ds_out)
            yield
            low = jnp.where(strict, big_g * dm, 0.0)
            att = jnp.where(incl, big_p * dm, 0.0)
            ws, atd = _pp(w, s), _pp_tn(att, d_o)
            yield
            vn = u - ws
            dvn = kds + atd
            dkd, datt_raw = _pp_nt(vn, ds_out), _pp_nt(d_o, vn)
            dw_neg, dvb = _pp_nt(dvn, s), _pp_tn(tm, dvn)
            dtm_a, wdv = _pp_nt(dvn, vb), _pp_tn(w, dvn)
            qgd = _pp_tn(qg, d_o)
            yield
            datt = jnp.where(incl, datt_raw, 0.0)
            dw = -dw_neg
            dtm_b, dkbg = _pp_nt(dw, kbg), _pp_tn(tm, dw)
            dbig_p = datt * dm
            dqs_a, dk_p = _pp(dbig_p, k), _pp_tn(dbig_p, qs)
            yield
            inner = _pp_tn(tm, dtm_a + dtm_b)
            yield
            dlow = jnp.where(strict, -_pp_nt(inner, tm), 0.0)
            yield
            dbig_g = dlow * dm
            dkb_a, dk_g = _pp(dbig_g, k), _pp_tn(dbig_g, kb)
            yield
            dkb = dkb_a + dkbg * eg
            dqs = dqs_a + dqg * eg
            dk = dk_g + dk_p + dkd * ed + dkb * bx
            z = dlow * low + datt * att
            kdterm = dkd * kd
            dglast = (jnp.sum(ds_out * s, axis=0, keepdims=True) * eglast
                      + jnp.sum(kdterm, axis=0, keepdims=True))
            dgx = dqg * qg + dkbg * kbg - kdterm
            dgx = dgx + jnp.where(col == 0, _pair_sum(z), 0.0)
            dgx = dgx + jnp.where(row == CHUNK - 1, dglast, 0.0)
            dq_ref[:, lanes] = dqs * SCALE
            dk_ref[:, lanes] = dk
            dv_ref[:, lanes] = dvb * bx
            dbx_ref[:, lanes] = dkb * k + dvb * v
            dgx_ref[:, lanes] = dgx
            dgr_ref[0, p] = -jnp.sum(z, axis=0, keepdims=True)
            dstate[p] = ds_out * eglast + qgd - wdv

        _interleave([chain(p) for p in range(PAIRS)])

    last = nchunk - 1
    blk = lambda j: pl.BlockSpec((CHUNK, WIDTH), lambda n, j=j: (last - n, j))
    sv = pl.BlockSpec((1, PAIRS, CHUNK, LANES), lambda n: (last - n, 0, 0, 0))
    gr_spec = pl.BlockSpec((1, PAIRS, 1, LANES), lambda n: (last - n, 0, 0, 0))
    wide = jax.ShapeDtypeStruct((rows, WIDTH), F32)
    return pl.pallas_call(
        body, name="gdn_bwd", grid=(nchunk,),
        in_specs=[blk(0), blk(1), blk(2), blk(0), blk(0), gr_spec, sv, sv, blk(0)],
        out_specs=[blk(0)] * 5 + [gr_spec],
        out_shape=[wide] * 5 + [jax.ShapeDtypeStruct((nchunk, PAIRS, 1, LANES), F32)],
        scratch_shapes=[pltpu.VMEM((PAIRS, CHUNK, LANES), F32)],
        compiler_params=_params(("arbitrary",)),
    )(qkv, qkv, qkv, betax, gcx, grow, ssave, tsave, do)


ATT_TQ = 256


def _att_scores(qh, kt, fk, diag):
    s = _dot(qh, kt, _CONTRACT["nt"]) - fk
    if diag:
        r = lax.broadcasted_iota(jnp.int32, s.shape, 0)
        c = lax.broadcasted_iota(jnp.int32, s.shape, 1)
        s = jnp.where(r >= c, s, -jnp.inf)
    return s


def _head_masks(n):
    lo = _lane_lo((n, LANES))
    return [lo, jnp.logical_not(lo)]


def _attention_forward(fqk, proj, frow, rows):
    tq = tk = min(ATT_TQ, rows)
    nq = rows // tq
    v_off = 3072 // LANES

    def body(q_ref, k_ref, v_ref, fr_ref, o_ref, lse_ref):
        qi = pl.program_id(1)
        q = q_ref[...] * SCALE
        keep_q, keep_k = _head_masks(tq), _head_masks(tk)
        qh = [jnp.where(keep_q[h], q, 0.0).astype(BF16) for h in range(2)]

        def tile(ki, carry, diag):
            k0 = pl.multiple_of(ki * tk, tk)
            kt = k_ref[pl.ds(k0, tk), :].astype(BF16)
            v_t = v_ref[pl.ds(k0, tk), :]
            out = [None, None]

            def chain(h):
                m, l, acc = carry[h]
                vt = jnp.where(keep_k[h], v_t, 0.0).astype(BF16)
                yield
                s = _att_scores(qh[h], kt, fr_ref[0, pl.ds(h, 1), pl.ds(k0, tk)], diag)
                yield
                m_new = jnp.maximum(m, jnp.max(s, axis=-1, keepdims=True))
                p = jnp.exp(s - m_new)
                alpha = jnp.exp(m - m_new)
                l = alpha * l + jnp.sum(p, axis=-1, keepdims=True)
                p_hi = p.astype(BF16)
                p_lo = p - p_hi.astype(F32)
                yield
                out[h] = (m_new, l, alpha * acc + _dot(p_hi, vt, _CONTRACT["nn"]) + _dot(p_lo, vt, _CONTRACT["nn"]))

            _interleave([chain(0), chain(1)])
            return tuple(out)

        one = (jnp.full((tq, 1), -jnp.inf, F32), jnp.zeros((tq, 1), F32), jnp.zeros((tq, LANES), F32))
        carry = lax.fori_loop(0, qi, lambda ki, c: tile(ki, c, False), (one, one))
        (m0, l0, acc0), (m1, l1, acc1) = tile(qi, carry, True)
        o_ref[...] = acc0 / l0 + acc1 / l1
        lse_ref[...] = jnp.where(keep_q[0], m0 + jnp.log(l0), m1 + jnp.log(l1))

    whole = lambda off: pl.BlockSpec((rows, LANES), lambda p, i, off=off: (0, off + p))
    qblk = lambda off: pl.BlockSpec((tq, LANES), lambda p, i, off=off: (i, off + p))
    wide = jax.ShapeDtypeStruct((rows, WIDTH), F32)
    return pl.pallas_call(
        body, name="fox_fwd", grid=(PAIRS, nq),
        in_specs=[qblk(0), whole(PAIRS), whole(v_off), pl.BlockSpec((1, 2, rows), lambda p, i: (p, 0, 0))],
        out_specs=[qblk(0), qblk(0)], out_shape=[wide, wide],
        compiler_params=_params(("parallel", "arbitrary")),
    )(fqk, fqk, proj, frow)


def _attention_delta(fqk, proj, frow, lse, dao, rows):
    tq = tk = min(ATT_TQ, rows)
    nq = rows // tq
    v_off = 3072 // LANES

    def body(q_ref, k_ref, v_ref, fr_ref, lse_ref, do_ref, delta_ref):
        qi = pl.program_id(1)
        q, d_o, lse_t = q_ref[...] * SCALE, do_ref[...], lse_ref[...]
        keep_q = _head_masks(tq)
        qh = [jnp.where(keep_q[h], q, 0.0).astype(BF16) for h in range(2)]
        doh = [jnp.where(keep_q[h], d_o, 0.0).astype(BF16) for h in range(2)]
        lse_h = [_head_col(lse_t, keep_q[0], h) for h in range(2)]

        def tile(ki, carry, diag):
            k0 = pl.multiple_of(ki * tk, tk)
            kt = k_ref[pl.ds(k0, tk), :].astype(BF16)
            vt = v_ref[pl.ds(k0, tk), :].astype(BF16)
            out = [None, None]

            def chain(h):
                s = _att_scores(qh[h], kt, fr_ref[0, pl.ds(h, 1), pl.ds(k0, tk)], diag)
                dp = _dot(doh[h], vt, _CONTRACT["nt"])
                yield
                out[h] = carry[h] + jnp.sum(jnp.exp(s - lse_h[h]) * dp, axis=-1, keepdims=True)

            _interleave([chain(0), chain(1)])
            return tuple(out)

        zero = jnp.zeros((tq, 1), F32)
        carry = lax.fori_loop(0, qi, lambda ki, c: tile(ki, c, False), (zero, zero))
        d0, d1 = tile(qi, carry, True)
        delta_ref[...] = jnp.where(keep_q[0], d0, d1)

    whole = lambda off: pl.BlockSpec((rows, LANES), lambda p, i, off=off: (0, off + p))
    qblk = lambda off: pl.BlockSpec((tq, LANES), lambda p, i, off=off: (i, off + p))
    return pl.pallas_call(
        body, name="fox_delta", grid=(PAIRS, nq),
        in_specs=[qblk(0), whole(PAIRS), whole(v_off),
                  pl.BlockSpec((1, 2, rows), lambda p, i: (p, 0, 0)), qblk(0), qblk(0)],
        out_specs=qblk(0), out_shape=jax.ShapeDtypeStruct((rows, WIDTH), F32),
        compiler_params=_params(("parallel", "arbitrary")),
    )(fqk, fqk, proj, frow, lse, dao)


def _attention_backward(fqk, proj, frow, ao, lse, dao, rows):
    tq = tk = min(ATT_TQ, rows)
    nq = rows // tq
    v_off = 3072 // LANES

    def body(q_ref, k_ref, v_ref, fr_ref, o_ref, lse_ref, do_ref, dq_ref, dk_ref, dv_ref, dfr_ref):
        ki = pl.program_id(1)

        @pl.when(ki == 0)
        def _():
            dq_ref[...] = jnp.zeros_like(dq_ref)

        keep_q, keep_k = _head_masks(tq), _head_masks(tk)
        k_t = k_ref[...]
        kt = k_t.astype(BF16)
        vt = v_ref[...].astype(BF16)
        kh = [jnp.where(keep_k[h], k_t, 0.0).astype(BF16) for h in range(2)]
        fk = [fr_ref[0, pl.ds(h, 1), :] for h in range(2)]

        def tile(qi, carry, diag):
            dk, dv, df0, df1 = carry
            rows_q = pl.ds(pl.multiple_of(qi * tq, tq), tq)
            q, d_o, lse_t = q_ref[rows_q, :] * SCALE, do_ref[rows_q, :], lse_ref[rows_q, :]
            delta_x = _pair_sum(d_o.astype(BF16).astype(F32) * o_ref[rows_q, :])
            res = [None, None]

            def chain(h):
                qh = jnp.where(keep_q[h], q, 0.0).astype(BF16)
                doh = jnp.where(keep_q[h], d_o, 0.0).astype(BF16)
                lse_h, delta_h = _head_col(lse_t, keep_q[0], h), _head_col(delta_x, keep_q[0], h)
                yield
                s, dp = _att_scores(qh, kt, fk[h], diag), _dot(doh, vt, _CONTRACT["nt"])
                yield
                p = jnp.exp(s - lse_h)
                ds = p * (dp - delta_h)
                yield
                res[h] = (_dot(p, doh, _CONTRACT["tn"]), _dot(ds, qh, _CONTRACT["tn"]),
                          _dot(ds, kh[h], _CONTRACT["nn"]), jnp.sum(ds, axis=0, keepdims=True))

            _interleave([chain(0), chain(1)])
            (dv0, dk0, dq0, s0), (dv1, dk1, dq1, s1) = res
            dq_ref[rows_q, :] += (dq0 + dq1) * SCALE
            return dk + dk0 + dk1, dv + dv0 + dv1, df0 - s0, df1 - s1

        zero_kv = jnp.zeros((tk, LANES), F32)
        zero_f = jnp.zeros((1, tk), F32)
        carry = tile(ki, (zero_kv, zero_kv, zero_f, zero_f), True)
        dk, dv, df0, df1 = lax.fori_loop(ki + 1, nq, lambda qi, c: tile(qi, c, False), carry)
        dk_ref[...] = dk
        dv_ref[...] = dv.astype(dv_ref.dtype)
        dfr_ref[0, pl.ds(0, 1), :] = df0
        dfr_ref[0, pl.ds(1, 1), :] = df1

    whole = lambda off: pl.BlockSpec((rows, LANES), lambda p, i, off=off: (0, off + p))
    kblk = lambda off: pl.BlockSpec((tk, LANES), lambda p, i, off=off: (i, off + p))
    fr_spec = pl.BlockSpec((1, 2, tk), lambda p, i: (p, 0, i))
    wide = jax.ShapeDtypeStruct((rows, WIDTH), F32)
    return pl.pallas_call(
        body, name="fox_bwd", grid=(PAIRS, nq),
        in_specs=[whole(0), kblk(PAIRS), kblk(v_off), fr_spec, whole(0), whole(0), whole(0)],
        out_specs=[whole(0), kblk(0), kblk(0), fr_spec],
        out_shape=[wide, wide, jax.ShapeDtypeStruct((rows, WIDTH), BF16),
                   jax.ShapeDtypeStruct((PAIRS, 2, rows), F32)],
        compiler_params=_params(("parallel", "arbitrary")),
    )(fqk, fqk, proj, frow, ao, lse, dao)


def _lane_ids(shape):
    return lax.broadcasted_iota(jnp.int32, shape, len(shape) - 1)


def _gates_elem(a_log, dt_bias, f_bias, pre):
    lane = _lane_ids(pre.shape)
    beta = jax.nn.sigmoid(pre)
    g = -jnp.exp(a_log) * _softplus(pre + dt_bias)
    lf = -_softplus(-(pre + f_bias))
    return jnp.where(lane < 8, beta, jnp.where(lane < 16, g, jnp.where(lane < 24, lf, 0.0)))


def _tri_consts():
    r = np.arange(LANES)[:, None]
    c = np.arange(LANES)[None, :]
    full = (c <= r).astype(np.float32)
    chunked = full * ((r // CHUNK) == (c // CHUNK))
    return jnp.asarray(chunked), jnp.asarray(full)


def _cums_fwd(lc, lf, gates):
    rows = gates.shape[0]
    lane = _lane_ids((LANES, LANES))
    carry = jnp.zeros((1, LANES), F32)
    out = []
    for r in range(rows // LANES):
        blk = gates[r * LANES:(r + 1) * LANES]
        gc = _dot32(lc, blk, _CONTRACT["nn"])
        f = _dot32(lf, blk, _CONTRACT["nn"]) + carry
        carry = carry + jnp.sum(blk, axis=0, keepdims=True)
        out.append(jnp.where((lane >= 8) & (lane < 16), gc, jnp.where((lane >= 16) & (lane < 24), f, 0.0)))
    return jnp.concatenate(out, axis=0)


def _cums_bwd(lc, lf, dcums):
    rows = dcums.shape[0]
    lane = _lane_ids((LANES, LANES))
    is_g = (lane >= 8) & (lane < 16)
    is_f = (lane >= 16) & (lane < 24)
    carry = jnp.zeros((1, LANES), F32)
    out = [None] * (rows // LANES)
    for r in reversed(range(rows // LANES)):
        blk = dcums[r * LANES:(r + 1) * LANES]
        dg = jnp.where(is_g, blk, 0.0)
        df = jnp.where(is_f, blk, 0.0)
        out[r] = _dot32(lc, dg, _CONTRACT["tn"]) + _dot32(lf, df, _CONTRACT["tn"]) + carry
        carry = carry + jnp.sum(df, axis=0, keepdims=True)
    return jnp.concatenate(out, axis=0)


def _expand_consts():
    xb = np.zeros((LANES, WIDTH), np.float32)
    xg = np.zeros((LANES, WIDTH), np.float32)
    for h in range(HEADS):
        xb[h, h * HEAD_DIM:(h + 1) * HEAD_DIM] = 1.0
        xg[8 + h, h * HEAD_DIM:(h + 1) * HEAD_DIM] = 1.0
    return jnp.asarray(xb), jnp.asarray(xg)


def _shift_down(x, s):
    if s == 0:
        return x
    row = lax.broadcasted_iota(jnp.int32, x.shape, 0)
    return jnp.where(row >= s, pltpu.roll(x, s, 0), 0.0)


def _shift_up(x, s):
    if s == 0:
        return x
    n = x.shape[0]
    row = lax.broadcasted_iota(jnp.int32, x.shape, 0)
    return jnp.where(row < n - s, pltpu.roll(x, n - s, 0), 0.0)


def _row_of(cw, i):
    row = lax.broadcasted_iota(jnp.int32, cw.shape, 0)
    return jnp.sum(jnp.where(row == i, cw, 0.0), axis=0, keepdims=True)


def _conv(cw, x):
    c = jnp.zeros_like(x)
    for i in range(CONV_K):
        c = c + _row_of(cw, i) * _shift_down(x, CONV_K - 1 - i)
    return c


def _post_conv(is_qk, c):
    s = _silu(c)
    n = s * lax.rsqrt(_pair_sum(s * s) + EPS)
    return jnp.where(is_qk, n, s)


def _gdn_prep_fwd(col, cw, x):
    return (_post_conv(col < 2 * PAIRS, _conv(cw, x)),)


def _gdn_prep_bwd(is_qk, cw, x, dy):
    c = _conv(cw, x)
    _, vjp = jax.vjp(lambda cc: _post_conv(is_qk, cc), c)
    (dc,) = vjp(dy)
    dx = jnp.zeros_like(x)
    row = lax.broadcasted_iota(jnp.int32, cw.shape, 0)
    dcw = jnp.zeros(cw.shape, F32)
    for i in range(CONV_K):
        s = CONV_K - 1 - i
        dx = dx + _row_of(cw, i) * _shift_up(dc, s)
        dcw = dcw + jnp.where(row == i, jnp.sum(dc * _shift_down(x, s), axis=0, keepdims=True), 0.0)
    return dx, dcw


def _head_rms(w, x):
    return x * lax.rsqrt(_pair_sum(x * x) / HEAD_DIM + EPS) * w


def _cat_weights(w_in_t):
    tail = jnp.pad(w_in_t[4112:4120], ((0, D_CAT - D_IN), (0, 0)))
    return jnp.concatenate([w_in_t[:2048], w_in_t[2064:4112], w_in_t[2048:2064], tail], axis=0)


def _uncat_grad(g):
    return jnp.concatenate([g[:2048], g[4096:4112], g[2048:4096], g[4112:4120]], axis=0)


def _lanes_to_rowform(v8, rows):
    return v8.reshape(rows // CHUNK, CHUNK, HEADS).transpose(0, 2, 1).reshape(rows // CHUNK, PAIRS, 1, LANES)


def _rowform_to_lanes(v, rows):
    return v.reshape(rows // CHUNK, HEADS, CHUNK).transpose(0, 2, 1).reshape(rows, HEADS)


def _local_step(x, target, norm1_w, a_log, dt_bias, out_norm_w, f_bias, q_norm_w, k_norm_w,
                norm2_w, final_w, first_weights, late_weights, early_grads_ready):
    rows = x.shape[0]
    tm = min(512, rows)
    lc, lf = _tri_consts()
    xb, xg = _expand_consts()

    (h1,) = _tiles(lambda col, w, xx: (_rms(xx, w),), name="norm1", rows=rows, tm=tm,
                   full_consts=[norm1_w], row_ins=[(x, D_MODEL, 0)], row_outs=[(D_MODEL, BF16)])
    w_cat, conv_w = first_weights(h1)
    proj = _mm(h1, w_cat, dims="nt", name="in_proj", tn=1408, tk=1024)

    lane_pad = lambda v, off: jnp.pad(v.reshape(1, -1), ((0, 0), (off, LANES - off - v.size)))
    p_a, p_dt, p_fb = lane_pad(a_log, 8), lane_pad(dt_bias, 8), lane_pad(f_bias, 16)

    def gates_fwd(col, lcv, lfv, a, dt, fb, pre):
        gates = _gates_elem(a, dt, fb, pre)
        return gates, _cums_fwd(lcv, lfv, gates)

    gates, cums = _tiles(gates_fwd, name="gates", rows=rows, tm=rows,
                         full_consts=[lc, lf, p_a, p_dt, p_fb], row_ins=[(proj, LANES, COL_SMALL)],
                         row_outs=[(LANES, F32), (LANES, F32)])

    def expand_fwd(col, b, g, gt, cm):
        return (_dot32(gt, b, _CONTRACT["nn"]), _dot32(cm, g, _CONTRACT["nn"]))

    betax, gcx = _tiles(expand_fwd, name="expand", rows=rows, tm=tm, full_consts=[xb, xg],
                        row_ins=[(gates, LANES, 0), (cums, LANES, 0)],
                        row_outs=[(WIDTH, F32)] * 2)
    grow = _lanes_to_rowform(cums[:, 8:16], rows)
    frow = cums[:, 16:24].T.reshape(PAIRS, 2, rows)

    (qkv,) = _tiles(_gdn_prep_fwd, name="gdn_prep", rows=rows, tm=rows, ncol=3 * PAIRS,
                    col_consts=[(conv_w, CONV_K, LANES, 0)], row_ins=[(proj, LANES, 0)],
                    row_outs=[(LANES, F32)])
    o_gdn, ssave, tsave = _gdn_forward(qkv, betax, gcx, grow, rows)

    w_qk = jnp.concatenate([jnp.tile(q_norm_w.reshape(1, -1), (1, HEADS)),
                            jnp.tile(k_norm_w.reshape(1, -1), (1, HEADS))], axis=1)
    fox_off = 2048 // LANES
    (fqk,) = _tiles(lambda col, w, xx: (_head_rms(w, xx),), name="fox_prep", rows=rows, tm=rows, ncol=2 * PAIRS,
                    col_consts=[(w_qk, 1, LANES, 0)], row_ins=[(proj, LANES, fox_off)],
                    row_outs=[(LANES, F32)])
    ao, lse = _attention_forward(fqk, proj, frow, rows)

    w_on = jnp.tile(out_norm_w.reshape(1, -1), (1, 2))
    z_off, fg_off = 1536 // LANES, 3584 // LANES
    mix_g_fn = lambda w, o, z: _head_rms(w, o) * _silu(z)
    mix_f_fn = lambda a, g: a * jax.nn.sigmoid(g)
    (mix_g,) = _tiles(lambda col, w, o, z: (mix_g_fn(w, o, z),), name="mix_gdn", rows=rows, tm=rows, ncol=PAIRS,
                      full_consts=[w_on], row_ins=[(o_gdn, LANES, 0), (proj, LANES, z_off)],
                      row_outs=[(LANES, BF16)])
    (mix_f,) = _tiles(lambda col, a, g: (mix_f_fn(a, g),), name="mix_fox", rows=rows, tm=rows, ncol=PAIRS,
                      row_ins=[(ao, LANES, 0), (proj, LANES, fg_off)], row_outs=[(LANES, BF16)])
    mix = jnp.concatenate([mix_g, mix_f], axis=1)
    w_out, w_gate, w_up, w_down = late_weights(mix)
    x1 = _mm(mix, w_out, dims="nn", name="out_proj", add=x, tk=1024)

    (h2,) = _tiles(lambda col, w, xx: (_rms(xx, w),), name="norm2", rows=rows, tm=tm,
                   full_consts=[norm2_w], row_ins=[(x1, D_MODEL, 0)], row_outs=[(D_MODEL, BF16)])
    t_rows, t_cols, t_act = min(1024, rows), 512, min(512, rows)
    n_rt = rows // t_rows
    st_act = jax.ShapeDtypeStruct((N_CHIPS, rows, FF_SHARD), BF16)
    st_rows = pl.BlockSpec((None, t_rows, FF_SHARD), lambda i, j: (j, i, 0))
    out_rows = pl.BlockSpec((t_rows, t_cols), lambda i, n: (i, n))
    flat = lambda t: t.reshape(N_CHIPS * rows, FF_SHARD)

    def ffn_in(w_st, name):
        return _mm_blocks(h2, w_st, name=name, grid=(n_rt, N_CHIPS), dims="nt",
                          a_spec=pl.BlockSpec((t_rows, D_MODEL), lambda i, j: (i, 0)),
                          b_spec=pl.BlockSpec((None, FF_SHARD, D_MODEL), lambda i, j: (j, 0, 0)),
                          o_spec=st_rows, out_shape=st_act)

    gate, up = ffn_in(w_gate, "ffn_gate"), ffn_in(w_up, "ffn_up")
    act_fn = lambda g, u: _silu(g.astype(F32)) * u.astype(F32)
    (act,) = _tiles(lambda col, g, u: (act_fn(g, u),), name="ffn_act", rows=N_CHIPS * rows, tm=t_act,
                    row_ins=[(flat(gate), FF_SHARD, 0), (flat(up), FF_SHARD, 0)], row_outs=[(FF_SHARD, BF16)])
    act = act.reshape(st_act.shape)
    x2 = _mm_blocks(act, w_down, name="ffn_down", grid=(n_rt, D_MODEL // t_cols), dims="nn", n_sum=N_CHIPS,
                    a_spec=pl.BlockSpec((N_CHIPS, t_rows, FF_SHARD), lambda i, n: (0, i, 0)),
                    b_spec=pl.BlockSpec((N_CHIPS, FF_SHARD, t_cols), lambda i, n: (0, 0, n)),
                    o_spec=out_rows, out_shape=jax.ShapeDtypeStruct((rows, D_MODEL), F32),
                    add=x1, add_spec=out_rows)

    def final_fn(col, w, xx, tgt):
        y, vjp = jax.vjp(_rms, xx, w)
        err = y - tgt
        loss = 0.5 * jnp.sum(err * err) / D_MODEL
        dx, dw = vjp(err / D_MODEL)
        return dx, dx, jnp.full((1, LANES), loss, F32), dw

    dx2, dx2_b, loss, d_final_w = _tiles(final_fn, name="final_loss", rows=rows, tm=tm, full_consts=[final_w],
                                         row_ins=[(x2, D_MODEL, 0), (target, D_MODEL, 0)],
                                         row_outs=[(D_MODEL, F32), (D_MODEL, BF16)],
                                         acc_outs=[(1, LANES), (1, D_MODEL)])

    dact = _mm_blocks(dx2_b, w_down, name="d_act", grid=(n_rt, N_CHIPS), dims="nt",
                      a_spec=pl.BlockSpec((t_rows, D_MODEL), lambda i, j: (i, 0)),
                      b_spec=pl.BlockSpec((None, FF_SHARD, D_MODEL), lambda i, j: (j, 0, 0)),
                      o_spec=st_rows, out_shape=st_act)
    def g_ffn(d_st, other, name):
        return _mm_blocks(d_st, other, name=name, grid=(N_CHIPS, D_MODEL // t_cols), dims="tn",
                          a_spec=pl.BlockSpec((None, rows, FF_SHARD), lambda j, n: (j, 0, 0)),
                          b_spec=pl.BlockSpec((rows, t_cols), lambda j, n: (0, n)),
                          o_spec=pl.BlockSpec((None, FF_SHARD, t_cols), lambda j, n: (j, 0, n)),
                          out_shape=jax.ShapeDtypeStruct((N_CHIPS, FF_SHARD, D_MODEL), BF16))

    g_down = g_ffn(act, dx2_b, "g_down")

    def act_bwd(col, g, u, d):
        _, vjp = jax.vjp(lambda gg, uu: _silu(gg) * uu, g.astype(F32), u.astype(F32))
        return vjp(d.astype(F32))

    dgate, dup = _tiles(act_bwd, name="ffn_act_bwd", rows=N_CHIPS * rows, tm=t_act,
                        row_ins=[(flat(gate), FF_SHARD, 0), (flat(up), FF_SHARD, 0), (flat(dact), FF_SHARD, 0)],
                        row_outs=[(FF_SHARD, BF16), (FF_SHARD, BF16)])
    dgate, dup = dgate.reshape(st_act.shape), dup.reshape(st_act.shape)

    def d_h2(d_st, w_st, name, add):
        return _mm_blocks(d_st, w_st, name=name, grid=(n_rt, D_MODEL // t_cols), dims="nn", n_sum=N_CHIPS,
                          a_spec=pl.BlockSpec((N_CHIPS, t_rows, FF_SHARD), lambda i, n: (0, i, 0)),
                          b_spec=pl.BlockSpec((N_CHIPS, FF_SHARD, t_cols), lambda i, n: (0, 0, n)),
                          o_spec=out_rows, out_shape=jax.ShapeDtypeStruct((rows, D_MODEL), F32),
                          add=add, add_spec=out_rows)

    dh2 = d_h2(dup, w_up, "d_h2_up", d_h2(dgate, w_gate, "d_h2_gate", None))
    g_gate, g_up = g_ffn(dgate, h2, "g_gate"), g_ffn(dup, h2, "g_up")

    def norm_bwd(col, w, xx, dh, dres):
        _, vjp = jax.vjp(_rms, xx, w)
        dx, dw = vjp(dh)
        return dx + dres, dx + dres, dw

    dx1, dx1_b, d_norm2_w = _tiles(norm_bwd, name="norm2_bwd", rows=rows, tm=tm, full_consts=[norm2_w],
                                   row_ins=[(x1, D_MODEL, 0), (dh2, D_MODEL, 0), (dx2, D_MODEL, 0)],
                                   row_outs=[(D_MODEL, F32), (D_MODEL, BF16)], acc_outs=[(1, D_MODEL)])
    dmix = _mm(dx1_b, w_out, dims="nt", name="d_mix", tk=1024)
    g_out = _mm(mix, dx1_b, dims="tn", name="g_out", tk=rows, out_dtype=BF16)
    w_on = w_on + early_grads_ready(g_out, g_gate, g_up, g_down)

    def mix_g_bwd(col, w, o, z, d):
        _, vjp = jax.vjp(mix_g_fn, w, o, z)
        dw, do_, dz = vjp(d)
        return do_, dz, dw

    do_gdn, dz, d_on = _tiles(mix_g_bwd, name="mix_gdn_bwd", rows=rows, tm=rows, ncol=PAIRS, full_consts=[w_on],
                              row_ins=[(o_gdn, LANES, 0), (proj, LANES, z_off), (dmix, LANES, 0)],
                              row_outs=[(LANES, F32), (LANES, BF16)], acc_outs=[(1, LANES)])

    def mix_f_bwd(col, a, g, d):
        _, vjp = jax.vjp(mix_f_fn, a, g)
        return vjp(d)

    dao, dfgate = _tiles(mix_f_bwd, name="mix_fox_bwd", rows=rows, tm=rows, ncol=PAIRS,
                         row_ins=[(ao, LANES, 0), (proj, LANES, fg_off), (dmix, LANES, PAIRS)],
                         row_outs=[(LANES, F32), (LANES, BF16)])

    dfq, dfk, dfv, dfrow = _attention_backward(fqk, proj, frow, ao, lse, dao, rows)

    def fox_prep_bwd(col, w, xx, d):
        _, vjp = jax.vjp(_head_rms, w, xx)
        dw, dx = vjp(d)
        return dx, dw

    dfqk, d_wqk = [], []
    for part, d_n in enumerate((dfq, dfk)):
        dx_p, dw_p = _tiles(fox_prep_bwd, name="fox_prep_bwd_" + "qk"[part], rows=rows, tm=rows, ncol=PAIRS,
                            col_consts=[(w_qk, 1, LANES, part * PAIRS)],
                            row_ins=[(proj, LANES, fox_off + part * PAIRS), (d_n, LANES, 0)],
                            row_outs=[(LANES, BF16)], acc_outs=[(1, LANES)])
        dfqk.append(dx_p)
        d_wqk.append(dw_p)

    dq, dk, dv, dbetax, dgcx, dgrow = _gdn_backward(qkv, betax, gcx, grow, ssave, tsave, do_gdn, rows)
    dqkv, d_conv = [], []
    for part, d_n in enumerate((dq, dk, dv)):
        prep_bwd = lambda col, cw, xx, dy, is_qk=(part < 2): _gdn_prep_bwd(is_qk, cw, xx, dy)
        dx_p, dw_p = _tiles(prep_bwd, name="gdn_prep_bwd_" + "qkv"[part], rows=rows, tm=rows, ncol=PAIRS,
                            col_consts=[(conv_w, CONV_K, LANES, part * PAIRS)],
                            row_ins=[(proj, LANES, part * PAIRS), (d_n, LANES, 0)],
                            row_outs=[(LANES, BF16)], acc_outs=[(CONV_K, LANES)])
        dqkv.append(dx_p)
        d_conv.append(dw_p)
    d_conv = jnp.concatenate(d_conv, axis=1)

    def expand_bwd(col, b, g, db, dg):
        return (_dot32(db, b, _CONTRACT["nt"]), _dot32(dg, g, _CONTRACT["nt"]))

    dgates_b, dcums_g = _tiles(expand_bwd, name="expand_bwd", rows=rows, tm=tm, full_consts=[xb, xg],
                               row_ins=[(dbetax, WIDTH, 0), (dgcx, WIDTH, 0)],
                               row_outs=[(LANES, F32), (LANES, F32)])
    dcums_row = jnp.concatenate([jnp.zeros((rows, 8), F32), _rowform_to_lanes(dgrow, rows),
                                 dfrow.reshape(HEADS, rows).T, jnp.zeros((rows, LANES - 24), F32)], axis=1)

    def gates_bwd(col, lcv, lfv, a, dt, fb, pre, dgb, dcg, dcr):
        lane = _lane_ids(pre.shape)
        dgates = jnp.where(lane < 8, dgb, _cums_bwd(lcv, lfv, dcg + dcr))
        _, vjp = jax.vjp(_gates_elem, a, dt, fb, pre)
        da, ddt, dfb, dpre = vjp(dgates)
        return dpre, da, ddt, dfb

    dpre, d_a, d_dt, d_fb = _tiles(gates_bwd, name="gates_bwd", rows=rows, tm=rows,
                                   full_consts=[lc, lf, p_a, p_dt, p_fb],
                                   row_ins=[(proj, LANES, COL_SMALL), (dgates_b, LANES, 0), (dcums_g, LANES, 0),
                                            (dcums_row, LANES, 0)],
                                   row_outs=[(LANES, BF16)], acc_outs=[(1, LANES)] * 3)

    dproj = jnp.concatenate(dqkv + [dz] + dfqk + [dfv, dfgate, dpre], axis=1)
    dh1 = _mm(dproj, w_cat, dims="nn", name="d_h1", tk=D_CAT)
    g_cat = _mm(dproj, h1, dims="tn", name="g_in", tm=1408, tn=D_MODEL, tk=rows)

    def norm1_bwd(col, w, xx, dh, dres):
        _, vjp = jax.vjp(_rms, xx, w)
        dx, dw = vjp(dh)
        return dx + dres, dw

    grad_x, d_norm1_w = _tiles(norm1_bwd, name="norm1_bwd", rows=rows, tm=tm, full_consts=[norm1_w],
                               row_ins=[(x, D_MODEL, 0), (dh1, D_MODEL, 0), (dx1, D_MODEL, 0)],
                               row_outs=[(D_MODEL, F32)], acc_outs=[(1, D_MODEL)])

    fold = lambda v: v.reshape(-1, HEAD_DIM).sum(axis=0)
    small = dict(
        loss=loss[0, 0],
        norm1_w=d_norm1_w, conv_w=d_conv, a_log=d_a[0, 8:16], dt_bias=d_dt[0, 8:16],
        out_norm_w=fold(d_on), f_bias=d_fb[0, 16:24], q_norm_w=fold(d_wqk[0]),
        k_norm_w=fold(d_wqk[1]), norm2_w=d_norm2_w, final_w=d_final_w)
    return grad_x, g_cat, g_out, g_gate, g_up, g_down, small


HBM_SPEC = pl.BlockSpec(memory_space=pltpu.HBM)


def _place():
    x, y, c = lax.axis_index("x"), lax.axis_index("y"), lax.axis_index("c")
    chips = [(1 - x, y), (x, 1 - y), (1 - x, 1 - y)]
    return x, y, c, 2 * x + y, (x, y, 1 - c), chips, [2 * cx + cy for cx, cy in chips]


def _remote(src, dst, send_sem, recv_sem, to):
    return pltpu.make_async_remote_copy(src_ref=src, dst_ref=dst, send_sem=send_sem, recv_sem=recv_sem,
                                        device_id=to, device_id_type=MESH)


def _allgather_weights(shards, conv):
    n = len(shards)
    halves = [s.shape[1] // 2 for s in shards]
    per = 6
    own_base = n * per + 3

    def body(*refs):
        ins, conv_in = refs[:n], refs[n]
        outs, conv_out = refs[n + 1:2 * n + 1], refs[2 * n + 1]
        send_sems, recv_sems = refs[2 * n + 2:]
        x, y, c, own, sib, chips, chip_idx = _place()

        def half(i, ref, hc):
            return ref.at[:, pl.ds(pl.multiple_of(hc * halves[i], LANES), halves[i])]

        sent = []
        for i, (src, dst) in enumerate(zip(list(ins) + [conv_in], list(outs) + [conv_out])):
            k = own_base + i
            sent.append(_remote(src, dst.at[own], send_sems.at[k], recv_sems.at[k], sib))
        for i in range(n):
            for j, chip in enumerate(chips):
                k = i * per + j
                sent.append(_remote(half(i, ins[i], c), half(i, outs[i].at[own], c),
                                    send_sems.at[k], recv_sems.at[k], (*chip, c)))
        for j, chip in enumerate(chips):
            k = n * per + j
            sent.append(_remote(conv_in, conv_out.at[own], send_sems.at[k], recv_sems.at[k], (*chip, c)))
        for cp in sent:
            cp.start()
        for i in range(n):
            for j in range(len(chips)):
                k = i * per + j
                landed = half(i, outs[i].at[chip_idx[j]], c)
                _remote(landed, landed, send_sems.at[k], recv_sems.at[k], sib).wait_recv()
                fwd = _remote(landed, landed, send_sems.at[k + 3], recv_sems.at[k + 3], sib)
                fwd.start()
                sent.append(fwd)
        for i in range(n):
            for j in range(len(chips)):
                k = i * per + 3 + j
                landed = half(i, outs[i].at[chip_idx[j]], 1 - c)
                _remote(landed, landed, send_sems.at[k], recv_sems.at[k], sib).wait_recv()
        for j in range(len(chips)):
            k = n * per + j
            landed = conv_out.at[chip_idx[j]]
            _remote(landed, landed, send_sems.at[k], recv_sems.at[k], sib).wait_recv()
        for i, dst in enumerate(list(outs) + [conv_out]):
            k = own_base + i
            landed = dst.at[own]
            _remote(landed, landed, send_sems.at[k], recv_sems.at[k], sib).wait_recv()
        for cp in sent:
            cp.wait_send()

    n_sem = own_base + n + 1
    out_shape = [jax.ShapeDtypeStruct((N_CHIPS,) + s.shape, s.dtype) for s in shards]
    out_shape.append(jax.ShapeDtypeStruct((N_CHIPS,) + conv.shape, conv.dtype))
    res = pl.pallas_call(
        body, name="allgather_weights", out_shape=out_shape,
        in_specs=[HBM_SPEC] * (n + 1), out_specs=[HBM_SPEC] * (n + 1),
        scratch_shapes=[pltpu.SemaphoreType.DMA((n_sem,)), pltpu.SemaphoreType.DMA((n_sem,))],
    )(*shards, conv)
    return res[:n], res[n]


SEM_SPEC = pl.BlockSpec(memory_space=pltpu.SEMAPHORE)
ANY_SPEC = pl.BlockSpec(memory_space=pl.ANY)
DATAFLOW = pltpu.SideEffectType.DATAFLOW_SIDE_EFFECTING


def _gather_plan(srcs, lands):
    x, y, c, own, sib, chips, chip_idx = _place()
    plan = []
    for src, land in zip(srcs, lands):
        for j, chip in enumerate(chips):
            plan.append((src, land.at[own], (*chip, c), land.at[chip_idx[j]]))
        plan.append((src, land.at[own], sib, land.at[own]))
    return plan


def _exchange_plan(srcs, lands):
    x, y, c, own, sib, chips, chip_idx = _place()
    plan = []
    for src, land in zip(srcs, lands):
        for j, chip in enumerate(chips):
            plan.append((src.at[chip_idx[j]], land.at[j], (*chip, c), land.at[j]))
    return plan


def _in_proj_plan(srcs, lands):
    x, y, c, own, sib, chips, chip_idx = _place()
    (w, conv), (w_land, conv_land) = srcs, lands
    hw = w.shape[1] // 2
    half = lambda ref: ref.at[:, pl.ds(pl.multiple_of(c * hw, LANES), hw)]
    plan = []
    for j, chip in enumerate(chips):
        plan.append((half(w), half(w_land.at[own]), (*chip, c), half(w_land.at[chip_idx[j]])))
        plan.append((conv, conv_land.at[own], (*chip, c), conv_land.at[chip_idx[j]]))
    plan.append((w, w_land.at[own], sib, w_land.at[own]))
    plan.append((conv, conv_land.at[own], sib, conv_land.at[own]))
    return plan


def _forward_halves(landed):
    hw = landed.shape[2] // 2

    def body(in_ref, out_ref, send_sems, recv_sems):
        x, y, c, own, sib, chips, chip_idx = _place()
        half = lambda ref, hc: ref.at[:, pl.ds(pl.multiple_of(hc * hw, LANES), hw)]
        sent = [_remote(half(out_ref.at[chip_idx[j]], c), half(out_ref.at[chip_idx[j]], c),
                        send_sems.at[j], recv_sems.at[j], sib) for j in range(3)]
        for cp in sent:
            cp.start()
        for j in range(3):
            other = half(out_ref.at[chip_idx[j]], 1 - c)
            _remote(other, other, send_sems.at[j], recv_sems.at[j], sib).wait_recv()
        for cp in sent:
            cp.wait_send()

    return pl.pallas_call(
        body, name="gather_in_forward", out_shape=jax.ShapeDtypeStruct(landed.shape, landed.dtype),
        in_specs=[HBM_SPEC], out_specs=HBM_SPEC, input_output_aliases={0: 0},
        scratch_shapes=[pltpu.SemaphoreType.DMA((3,)), pltpu.SemaphoreType.DMA((3,))],
    )(landed)


def _split_start(name, plan_fn, srcs, land_shapes, n_copies, after):
    n = len(srcs)

    def body(*refs):
        src_refs, land_refs = refs[:n], refs[n:2 * n]
        send_sems, recv_sems = refs[2 * n + 1], refs[2 * n + 2]
        token = refs[-1]
        for k, (src, dst, to, _) in enumerate(plan_fn(src_refs, land_refs)):
            _remote(src, dst, send_sems.at[k], recv_sems.at[k], to).start()
        token[...] = jnp.zeros_like(token)

    lands = [pltpu.with_memory_space_constraint(lax.empty(s.shape, s.dtype), pltpu.HBM) for s in land_shapes]
    srcs = [pltpu.with_memory_space_constraint(s, pltpu.HBM) for s in srcs]
    out_shape = ([pltpu.SemaphoreType.DMA((n_copies,)), pltpu.SemaphoreType.DMA((n_copies,))]
                 + [pltpu.HBM(s.shape, s.dtype) for s in srcs] + [pltpu.HBM(s.shape, s.dtype) for s in land_shapes]
                 + [jax.ShapeDtypeStruct((8, LANES), F32)])
    res = pl.pallas_call(
        body, name=name, out_shape=out_shape,
        in_specs=[HBM_SPEC] * (2 * n) + [ANY_SPEC],
        out_specs=[SEM_SPEC, SEM_SPEC] + [HBM_SPEC] * (2 * n) + [pl.BlockSpec(memory_space=pltpu.VMEM)],
        input_output_aliases={i: 2 + i for i in range(2 * n)},
        compiler_params=pltpu.CompilerParams(has_side_effects=DATAFLOW),
    )(*srcs, *lands, after)
    return dict(sems=res[:2], srcs=res[2:2 + n], lands=res[2 + n:2 + 2 * n], token=res[-1], n=n)


def _split_wait(name, plan_fn, started, after):
    n = started["n"]

    def body(*refs):
        src_refs, land_refs = refs[:n], refs[n:2 * n]
        send_sems, recv_sems = refs[2 * n], refs[2 * n + 1]
        for k, (src, _, to, landed) in enumerate(plan_fn(src_refs, land_refs)):
            copy = _remote(src, landed, send_sems.at[k], recv_sems.at[k], to)
            copy.wait_send()
            copy.wait_recv()

    srcs, lands = started["srcs"], started["lands"]
    after = list(after) if isinstance(after, (list, tuple)) else [after]
    res = pl.pallas_call(
        body, name=name,
        out_shape=[pltpu.HBM(s.shape, s.dtype) for s in srcs] + [pltpu.HBM(s.shape, s.dtype) for s in lands],
        in_specs=[HBM_SPEC] * (2 * n) + [SEM_SPEC, SEM_SPEC] + [ANY_SPEC] * len(after),
        out_specs=[HBM_SPEC] * (2 * n),
        input_output_aliases={i: i for i in range(2 * n)},
        compiler_params=pltpu.CompilerParams(has_side_effects=DATAFLOW),
    )(*srcs, *lands, *started["sems"], *after)
    return res[n:]


def _swap_halves(stacks, name):
    n = len(stacks)

    def body(*refs):
        ins, outs = refs[:n], refs[n:2 * n]
        send_sems, recv_sems = refs[2 * n:]
        x, y, c, own, sib, chips, chip_idx = _place()
        cps = []
        for i in range(n):
            h = stacks[i].shape[2] // 2
            src = ins[i].at[:, :, pl.ds(pl.multiple_of((1 - c) * h, LANES), h)]
            cps.append(_remote(src, outs[i], send_sems.at[i], recv_sems.at[i], sib))
        for cp in cps:
            cp.start()
        for cp in cps:
            cp.wait()

    out_shape = [jax.ShapeDtypeStruct((N_CHIPS, s.shape[1], s.shape[2] // 2), s.dtype) for s in stacks]
    return pl.pallas_call(
        body, name=name, out_shape=out_shape,
        in_specs=[HBM_SPEC] * n, out_specs=[HBM_SPEC] * n,
        scratch_shapes=[pltpu.SemaphoreType.DMA((n,)), pltpu.SemaphoreType.DMA((n,))],
    )(*stacks)


def _add_half(stack, landed, place, name):
    _, rows, h = landed.shape

    def body(place_ref, a_ref, b_ref, o_ref, own_ref):
        part = (a_ref[...].astype(F32) + b_ref[...].astype(F32)).astype(o_ref.dtype)
        o_ref[...] = part

        @pl.when(pl.program_id(0) == place_ref[1])
        def _():
            own_ref[...] = part[0]

    return pl.pallas_call(
        body, name=name,
        out_shape=[jax.ShapeDtypeStruct(landed.shape, BF16), jax.ShapeDtypeStruct((rows, h), BF16)],
        grid_spec=pltpu.PrefetchScalarGridSpec(
            num_scalar_prefetch=1, grid=(N_CHIPS,),
            in_specs=[pl.BlockSpec((1, rows, h), lambda j, p: (j, 0, p[0])),
                      pl.BlockSpec((1, rows, h), lambda j, p: (j, 0, 0))],
            out_specs=[pl.BlockSpec((1, rows, h), lambda j, p: (j, 0, 0)),
                       pl.BlockSpec((rows, h), lambda j, p: (0, 0))]),
        compiler_params=_params(("arbitrary",)),
    )(place, stack, landed)


def _exchange_partials(parts):
    n = len(parts)

    def body(*refs):
        ins, outs = refs[:n], refs[n:2 * n]
        send_sems, recv_sems = refs[2 * n:]
        x, y, c, own, sib, chips, chip_idx = _place()
        sent = []
        for i in range(n):
            for j, chip in enumerate(chips):
                k = i * 3 + j
                sent.append(_remote(ins[i].at[chip_idx[j]], outs[i].at[j], send_sems.at[k], recv_sems.at[k],
                                    (*chip, c)))
        for cp in sent:
            cp.start()
        for i in range(n):
            for j in range(len(chips)):
                k = i * 3 + j
                landed = outs[i].at[j]
                _remote(landed, landed, send_sems.at[k], recv_sems.at[k], sib).wait_recv()
        for cp in sent:
            cp.wait_send()

    return pl.pallas_call(
        body, name="rs_exchange_partials",
        out_shape=[jax.ShapeDtypeStruct((3,) + p.shape[1:], p.dtype) for p in parts],
        in_specs=[HBM_SPEC] * n, out_specs=[HBM_SPEC] * n,
        scratch_shapes=[pltpu.SemaphoreType.DMA((3 * n,)), pltpu.SemaphoreType.DMA((3 * n,))],
    )(*parts)


def _sum_partials(own_part, landed, name, untiled_rows=False):
    _, h, cols = landed.shape
    tc = LANES if untiled_rows else cols

    def body(own_ref, a_ref, o_ref):
        acc = own_ref[...].astype(F32)
        for s in range(3):
            acc = acc + a_ref[s].astype(F32)
        if untiled_rows:
            o_ref[:, 0, :] = acc
        else:
            o_ref[...] = acc

    if untiled_rows:
        out_shape, out_spec = jax.ShapeDtypeStruct((h, 1, cols), F32), pl.BlockSpec((h, 1, tc), lambda i: (0, 0, i))
    else:
        out_shape, out_spec = jax.ShapeDtypeStruct((h, cols), F32), pl.BlockSpec((h, tc), lambda i: (0, i))
    return pl.pallas_call(
        body, name=name, out_shape=out_shape, grid=(cols // tc,),
        in_specs=[pl.BlockSpec((h, tc), lambda i: (0, i)), pl.BlockSpec((3, h, tc), lambda i: (0, 0, i))],
        out_specs=out_spec, compiler_params=_params(("arbitrary",)),
    )(own_part, landed)


def _share_halves(halves, name):
    n = len(halves)

    def body(*refs):
        ins, outs = refs[:n], refs[n:2 * n]
        send_sems, recv_sems = refs[2 * n:]
        x, y, c, own, sib, chips, chip_idx = _place()
        cps = [_remote(ins[i], outs[i], send_sems.at[i], recv_sems.at[i], sib) for i in range(n)]
        for cp in cps:
            cp.start()
        for cp in cps:
            cp.wait()

    return pl.pallas_call(
        body, name=name,
        out_shape=[jax.ShapeDtypeStruct(p.shape, p.dtype) for p in halves],
        in_specs=[HBM_SPEC] * n, out_specs=[HBM_SPEC] * n,
        scratch_shapes=[pltpu.SemaphoreType.DMA((n,)), pltpu.SemaphoreType.DMA((n,))],
    )(*halves)


def _allreduce_small(packed):
    rows = packed.shape[0]
    n_dev = 8

    def body(in_ref, out_ref, gath, send_sems, recv_sems):
        x, y, c = lax.axis_index("x"), lax.axis_index("y"), lax.axis_index("c")
        me = 4 * x + 2 * y + c
        gath[me] = in_ref[...]
        cps = []
        for k in range(1, n_dev):
            fx, fy, fc = (k >> 2) & 1, (k >> 1) & 1, k & 1
            to = (x ^ fx, y ^ fy, c ^ fc)
            cps.append(_remote(in_ref, gath.at[me], send_sems.at[k - 1], recv_sems.at[k - 1], to))
        for cp in cps:
            cp.start()
        for k in range(1, n_dev):
            fx, fy, fc = (k >> 2) & 1, (k >> 1) & 1, k & 1
            src = 4 * (x ^ fx) + 2 * (y ^ fy) + (c ^ fc)
            slot = gath.at[src]
            _remote(slot, slot, send_sems.at[k - 1], recv_sems.at[k - 1], (x, y, c)).wait_recv()
        for cp in cps:
            cp.wait_send()
        acc = gath[0]
        for d in range(1, n_dev):
            acc = acc + gath[d]
        out_ref[...] = acc

    vm = pl.BlockSpec(memory_space=pltpu.VMEM)
    return pl.pallas_call(
        body, name="allreduce_small", out_shape=jax.ShapeDtypeStruct(packed.shape, F32),
        in_specs=[vm], out_specs=vm,
        scratch_shapes=[pltpu.VMEM((n_dev, rows, LANES), F32),
                        pltpu.SemaphoreType.DMA((n_dev - 1,)), pltpu.SemaphoreType.DMA((n_dev - 1,))],
    )(packed)


def _adam(col, w, g, m, v):
    m2 = ADAM_B1 * m + (1.0 - ADAM_B1) * g
    v2 = ADAM_B2 * v + (1.0 - ADAM_B2) * (g * g)
    m_hat = m2 / (1.0 - ADAM_B1 ** ADAM_STEP)
    v_hat = v2 / (1.0 - ADAM_B2 ** ADAM_STEP)
    delta = -ADAM_LR * (m_hat / (jnp.sqrt(v_hat) + ADAM_EPS) + ADAM_WD * w)
    return delta, m2, v2


def _adam_call(w, g, m, v, name):
    rows, cols = w.shape
    tm = rows
    for cand in (256, 352, 176, 128, 64, 48, 16, 8):
        if rows % cand == 0:
            tm = cand
            break
    return _tiles(_adam, name=name, rows=rows, tm=tm,
                  row_ins=[(w, cols, 0), (g, cols, 0), (m, cols, 0), (v, cols, 0)],
                  row_outs=[(cols, F32)] * 3)


def _adam_big(w, g_mine, g_other, m, v, place, name):
    rows, cols = w.shape
    tc = 256
    nt = cols // 2 // tc

    def body(place_ref, w_ref, gm_ref, go_ref, m_ref, v_ref, g_out, d_out, m_out, v_out):
        g = jnp.where(pl.program_id(0) == place_ref[0], gm_ref[...], go_ref[...])
        d, m2, v2 = _adam(None, w_ref[...], g, m_ref[...], v_ref[...])
        g_out[...] = g
        d_out[...] = d
        m_out[...] = m2
        v_out[...] = v2

    full = pl.BlockSpec((rows, tc), lambda hh, i, p: (0, hh * nt + i))
    half = pl.BlockSpec((rows, tc), lambda hh, i, p: (0, i))
    return pl.pallas_call(
        body, name=name, out_shape=[jax.ShapeDtypeStruct(w.shape, F32)] * 4,
        grid_spec=pltpu.PrefetchScalarGridSpec(
            num_scalar_prefetch=1, grid=(2, nt),
            in_specs=[full, half, half, full, full], out_specs=[full] * 4),
        compiler_params=_params(("arbitrary", "arbitrary")),
    )(place, w, g_mine, g_other, m, v)


def _adam_untiled_rows(w, g_mine, g_other, m, v, place, name):
    rows, _, cols = w.shape
    tc = 256
    nt = cols // 2 // tc
    rb = next(r for r in (206, 128, 103, rows) if rows % r == 0)

    def body(place_ref, w_ref, gm_ref, go_ref, m_ref, v_ref, g_out, d_out, m_out, v_out):
        g = jnp.where(pl.program_id(0) == place_ref[0], gm_ref[...], go_ref[...])
        d, m2, v2 = _adam(None, w_ref[...], g, m_ref[...], v_ref[...])
        g_out[...] = g
        d_out[...] = d
        m_out[...] = m2
        v_out[...] = v2

    full = pl.BlockSpec((rb, 1, tc), lambda hh, i, r, p: (r, 0, hh * nt + i))
    half = pl.BlockSpec((rb, 1, tc), lambda hh, i, r, p: (r, 0, i))
    return pl.pallas_call(
        body, name=name, out_shape=[jax.ShapeDtypeStruct(w.shape, F32)] * 4,
        grid_spec=pltpu.PrefetchScalarGridSpec(
            num_scalar_prefetch=1, grid=(2, nt, rows // rb),
            in_specs=[full, half, half, full, full], out_specs=[full] * 4),
        compiler_params=_params(("arbitrary", "arbitrary", "arbitrary")),
    )(place, w, g_mine, g_other, m, v)


def _pack(arrays, zero=None):
    flat = []
    for a in arrays:
        a = a.reshape(-1).astype(F32)
        if zero is not None:
            a = a + zero
        flat.append(jnp.pad(a, (0, (-a.size) % LANES)))
    out = jnp.concatenate(flat)
    out = jnp.pad(out, (0, (-out.size) % (8 * LANES)))
    return out.reshape(-1, LANES)


def _unpack(packed, shapes):
    flat = packed.reshape(-1)
    out, off = [], 0
    for s in shapes:
        size = int(np.prod(s))
        out.append(flat[off:off + size].reshape(s))
        off += size + (-size) % LANES
    return out


def kernel(x, norm1_w, w_in, gdn_conv_w, gdn_A_log, gdn_dt_bias, gdn_out_norm_w, fox_f_bias, fox_q_norm_w, fox_k_norm_w, w_out, norm2_w, w_ffn_gate, w_ffn_up, w_ffn_down, final_norm_w, loss_target, m_norm1_w, m_w_in, m_gdn_conv_w, m_gdn_A_log, m_gdn_dt_bias, m_gdn_out_norm_w, m_fox_f_bias, m_fox_q_norm_w, m_fox_k_norm_w, m_w_out, m_norm2_w, m_w_ffn_gate, m_w_ffn_up, m_w_ffn_down, m_final_norm_w, v_norm1_w, v_w_in, v_gdn_conv_w, v_gdn_A_log, v_gdn_dt_bias, v_gdn_out_norm_w, v_fox_f_bias, v_fox_q_norm_w, v_fox_k_norm_w, v_w_out, v_norm2_w, v_w_ffn_gate, v_w_ffn_up, v_w_ffn_down, v_final_norm_w):
    cx, cy, cc = lax.axis_index("x"), lax.axis_index("y"), lax.axis_index("c")
    own = 2 * cx + cy
    place = jnp.stack([cc, own]).astype(jnp.int32)

    names = ["w_in", "w_out", "w_gate", "w_up", "w_down"]
    is_t = [True, False, True, True, False]
    to_t = lambda a, t: a[0].T if t else a[0]
    from_t = lambda a, t: (a.T if t else a)[None]
    big_w = [to_t(a, t) for a, t in zip([w_in, w_out, w_ffn_gate, w_ffn_up, w_ffn_down], is_t)]
    big_m = [to_t(a, t) for a, t in zip([m_w_in, m_w_out, m_w_ffn_gate, m_w_ffn_up, m_w_ffn_down], is_t)]
    big_v = [to_t(a, t) for a, t in zip([v_w_in, v_w_out, v_w_ffn_gate, v_w_ffn_up, v_w_ffn_down], is_t)]
    shards = [big_w[0].astype(BF16)]
    small_w = [norm1_w, gdn_conv_w, gdn_A_log, gdn_dt_bias, gdn_out_norm_w, fox_f_bias, fox_q_norm_w,
               fox_k_norm_w, norm2_w, final_norm_w]
    small_m = [m_norm1_w, m_gdn_conv_w, m_gdn_A_log, m_gdn_dt_bias, m_gdn_out_norm_w, m_fox_f_bias,
               m_fox_q_norm_w, m_fox_k_norm_w, m_norm2_w, m_final_norm_w]
    small_v = [v_norm1_w, v_gdn_conv_w, v_gdn_A_log, v_gdn_dt_bias, v_gdn_out_norm_w, v_fox_f_bias,
               v_fox_q_norm_w, v_fox_k_norm_w, v_norm2_w, v_final_norm_w]
    first = _split_start("gather_in_start", _in_proj_plan, [shards[0], gdn_conv_w[0]],
                         [jax.ShapeDtypeStruct((N_CHIPS,) + shards[0].shape, BF16),
                          jax.ShapeDtypeStruct((N_CHIPS, CONV_K, 3 * WIDTH // N_CHIPS), F32)],
                         n_copies=8, after=shards[0])
    small_packed = [_pack(p, first["token"][0, 0]) for p in (small_w, small_m, small_v)]
    shards += [(w + first["token"][0, 0]).astype(BF16) for w in big_w[1:]]
    rest = {}

    def first_weights(after):
        w_in_g, conv_g = _split_wait("gather_in_wait", _in_proj_plan, first, [after] + small_packed)
        w_in_g = _forward_halves(w_in_g)
        rest.update(_split_start("gather_rest_start", _gather_plan, shards[1:],
                                 [jax.ShapeDtypeStruct((N_CHIPS,) + s.shape, BF16) for s in shards[1:]],
                                 n_copies=4 * len(shards[1:]), after=w_in_g))
        w_cat = _cat_weights(w_in_g.reshape(D_IN, D_MODEL))
        return w_cat + rest["token"][0, 0].astype(BF16), conv_g.transpose(1, 0, 2).reshape(CONV_K, 3 * WIDTH)

    def late_weights(after):
        w_out_g, w_gate_g, w_up_g, w_down_g = _split_wait("gather_rest_wait", _gather_plan, rest, after)
        return w_out_g.reshape(D_MODEL, D_MODEL), w_gate_g, w_up_g, w_down_g

    def start_reduction(stacks, nms, tag):
        landed = _swap_halves(stacks, "rs_swap_" + tag)
        added = [_add_half(s, l, place, "rs_add_" + nm) for s, l, nm in zip(stacks, landed, nms)]
        parts = [a[0] for a in added]
        started = _split_start("exchange_" + tag + "_start", _exchange_plan, parts,
                               [jax.ShapeDtypeStruct((3,) + p.shape[1:], p.dtype) for p in parts],
                               n_copies=3 * len(parts), after=parts[0])
        return dict(own=[a[1] for a in added], started=started, tag=tag, names=nms)

    def finish_reduction(red, after, updates):
        landed = _split_wait("exchange_" + red["tag"] + "_wait", _exchange_plan, red["started"], after)
        halves = [_sum_partials(o, p, "rs_sum_" + nm, untiled_rows=nm == "w_in")
                  for o, p, nm in zip(red["own"], landed, red["names"])]
        others = _share_halves(halves, "rs_share_" + red["tag"])
        return [upd(gm, go) for upd, gm, go in zip(updates, halves, others)]

    def transport_update(b):
        def upd(gm, go):
            res = _adam_big(big_w[b], gm, go, big_m[b], big_v[b], place, "adam_" + names[b])
            early_done.append(res[1])
            return [from_t(a, is_t[b]) for a in res]
        return upd

    early_done = []

    def w_in_update(gm, go):
        rows3 = lambda a: jnp.transpose(a, (2, 0, 1))
        res = _adam_untiled_rows(rows3(w_in), gm, go, rows3(m_w_in), rows3(v_w_in), place, "adam_w_in")
        return [jnp.transpose(a, (1, 2, 0)) for a in res]

    early = {}

    def early_grads_ready(g_out, g_gate, g_up, g_down):
        stacks = [g_out.reshape(N_CHIPS, D_MODEL // N_CHIPS, D_MODEL), g_gate, g_up, g_down]
        early.update(start_reduction(stacks, names[1:], "early"))
        return early["started"]["token"][0, 0]

    grad_x, g_cat, _, _, _, _, small = _local_step(
        x[0], loss_target[0], norm1_w + first["token"][0, 0], gdn_A_log[0], gdn_dt_bias[0],
        gdn_out_norm_w[0], fox_f_bias[0], fox_q_norm_w[0], fox_k_norm_w[0], norm2_w, final_norm_w.reshape(1, -1),
        first_weights, late_weights, early_grads_ready)

    late = start_reduction([_uncat_grad(g_cat).reshape(N_CHIPS, D_IN // N_CHIPS, D_MODEL)], names[:1], "w_in")
    big_upd = finish_reduction(early, late["started"]["token"], [transport_update(b) for b in range(1, 5)])

    order = ["norm1_w", "conv_w", "a_log", "dt_bias", "out_norm_w", "f_bias", "q_norm_w", "k_norm_w",
             "norm2_w", "final_w"]
    red = _allreduce_small(_pack([small[k] for k in order] + [small["loss"]]))
    red_shapes = [(1, D_MODEL), (CONV_K, 3 * WIDTH), (1, HEADS), (1, HEADS), (1, HEAD_DIM), (1, HEADS),
                  (1, HEAD_DIM), (1, HEAD_DIM), (1, D_MODEL), (D_MODEL,), ()]
    red_list = _unpack(red, red_shapes)
    loss = red_list[-1]
    small_g = dict(zip(order, red_list[:-1]))
    shard_cols = 3 * WIDTH // N_CHIPS
    small_g["conv_w"] = lax.dynamic_slice_in_dim(small_g["conv_w"], own * shard_cols, shard_cols, axis=1)[None]
    small_gl = [small_g[k].reshape(w.shape) for k, w in zip(order, small_w)]
    s_delta, s_m, s_v = _adam_call(small_packed[0], _pack(small_gl), small_packed[1], small_packed[2], "adam_small")
    big_upd = finish_reduction(late, [s_delta] + early_done, [w_in_update]) + big_upd
    shapes = [w.shape for w in small_w]
    s_delta, s_m, s_v = _unpack(s_delta, shapes), _unpack(s_m, shapes), _unpack(s_v, shapes)

    big_pos = {1: 0, 9: 1, 11: 2, 12: 3, 13: 4}
    small_pos = {0: 0, 2: 1, 3: 2, 4: 3, 5: 4, 6: 5, 7: 6, 8: 7, 10: 8, 14: 9}
    grads, deltas, new_m, new_v = [], [], [], []
    for pos in range(15):
        if pos in big_pos:
            b = big_pos[pos]
            g, d, m2, v2 = big_upd[b]
            grads.append(g)
            deltas.append(d)
            new_m.append(m2)
            new_v.append(v2)
        else:
            s = small_pos[pos]
            grads.append(small_gl[s])
            deltas.append(s_delta[s])
            new_m.append(s_m[s])
            new_v.append(s_v[s])
    return (loss, grad_x[None], *grads, *deltas, *new_m, *new_v)
```

```python
import jax
import jax.numpy as jnp
import numpy as np
from jax import lax
from jax.experimental import pallas as pl
from jax.experimental.pallas import tpu as pltpu

F32 = jnp.float32
BF16 = jnp.bfloat16

D_MODEL = 1024
HEADS = 8
HEAD_DIM = 64
PAIRS = HEADS // 2
WIDTH = HEADS * HEAD_DIM
CHUNK = 64
CONV_K = 4
D_FF = 2816
FF_SHARD = D_FF // 4
EPS = 1e-6
SCALE = HEAD_DIM ** -0.5
LANES = 128
N_CHIPS = 4
D_IN = 4120
D_CAT = 4224
COL_SMALL = 4096 // LANES

ADAM_LR = 0.001
ADAM_B1 = 0.9
ADAM_B2 = 0.999
ADAM_EPS = 1e-08
ADAM_WD = 0.01
ADAM_STEP = 10

VMEM_LIMIT = 56 * 1024 * 1024
MESH = pl.DeviceIdType.MESH
HIGHEST = lax.Precision.HIGHEST


def _params(sem):
    return pltpu.CompilerParams(dimension_semantics=sem, vmem_limit_bytes=VMEM_LIMIT)


_CONTRACT = {"nn": ((1,), (0,)), "nt": ((1,), (1,)), "tn": ((0,), (0,))}


def _mm(a, b, *, dims, name, out_dtype=F32, add=None, tm=1024, tn=512, tk=512):
    if dims == "nn":
        (m, k), (k2, n) = a.shape, b.shape
    elif dims == "nt":
        (m, k), (n, k2) = a.shape, b.shape
    else:
        (k, m), (k2, n) = a.shape, b.shape
    assert k == k2, (a.shape, b.shape, dims)
    tm, tn, tk = min(tm, m), min(tn, n), min(tk, k)
    assert m % tm == 0 and n % tn == 0 and k % tk == 0, (m, n, k, tm, tn, tk)
    nk = k // tk
    a_spec = (pl.BlockSpec((tk, tm), lambda i, j, kk: (kk, i)) if dims == "tn"
              else pl.BlockSpec((tm, tk), lambda i, j, kk: (i, kk)))
    b_spec = (pl.BlockSpec((tn, tk), lambda i, j, kk: (j, kk)) if dims == "nt"
              else pl.BlockSpec((tk, tn), lambda i, j, kk: (kk, j)))
    o_spec = pl.BlockSpec((tm, tn), lambda i, j, kk: (i, j))
    contract = (_CONTRACT[dims], ((), ()))
    has_add = add is not None

    def body(*refs):
        a_ref, b_ref = refs[:2]
        add_ref = refs[2] if has_add else None
        o_ref = refs[3] if has_add else refs[2]
        part = lax.dot_general(a_ref[...].astype(BF16), b_ref[...].astype(BF16), contract,
                               preferred_element_type=F32)

        def finish(r):
            if has_add:
                r = r + add_ref[...].astype(F32)
            o_ref[...] = r.astype(out_dtype)

        if nk == 1:
            finish(part)
            return
        acc = refs[-1]
        kk = pl.program_id(2)

        @pl.when(kk == 0)
        def _():
            acc[...] = part

        @pl.when(kk > 0)
        def _():
            acc[...] += part

        @pl.when(kk == nk - 1)
        def _():
            finish(acc[...])

    ins = [a, b] + ([add] if has_add else [])
    in_specs = [a_spec, b_spec] + ([o_spec] if has_add else [])
    return pl.pallas_call(
        body, name=name, grid=(m // tm, n // tn, nk),
        in_specs=in_specs, out_specs=o_spec,
        out_shape=jax.ShapeDtypeStruct((m, n), out_dtype),
        scratch_shapes=[pltpu.VMEM((tm, tn), F32)] if nk > 1 else [],
        compiler_params=_params(("parallel", "parallel", "arbitrary")),
    )(*ins)


def _mm_blocks(a, b, *, name, grid, a_spec, b_spec, o_spec, out_shape, dims, n_sum=0, add=None, add_spec=None):
    contract = (_CONTRACT[dims], ((), ()))
    has_add = add is not None

    def body(*refs):
        a_ref, b_ref = refs[:2]
        o_ref = refs[-1]
        dot = lambda x, y: lax.dot_general(x.astype(BF16), y.astype(BF16), contract, preferred_element_type=F32)
        if n_sum:
            r = dot(a_ref[0], b_ref[0])
            for s in range(1, n_sum):
                r = r + dot(a_ref[s], b_ref[s])
        else:
            r = dot(a_ref[...], b_ref[...])
        if has_add:
            r = r + refs[2][...].astype(F32)
        o_ref[...] = r.astype(o_ref.dtype)

    return pl.pallas_call(
        body, name=name, grid=grid,
        in_specs=[a_spec, b_spec] + ([add_spec] if has_add else []), out_specs=o_spec, out_shape=out_shape,
        compiler_params=_params(("parallel",) * len(grid)),
    )(*([a, b] + ([add] if has_add else [])))


def _tiles(fn, *, name, rows, tm, ncol=1, row_ins=(), col_consts=(), full_consts=(),
           row_outs=(), acc_outs=()):
    nt = rows // tm
    assert rows % tm == 0
    n_full, n_col, n_row = len(full_consts), len(col_consts), len(row_ins)
    n_ro, n_acc = len(row_outs), len(acc_outs)

    def body(*refs):
        ins = refs[:n_full + n_col + n_row]
        outs = refs[n_full + n_col + n_row:]
        i = pl.program_id(1)
        res = fn(pl.program_id(0), *[r[...] for r in ins])
        for r, v in zip(outs[:n_ro], res[:n_ro]):
            r[...] = v.astype(r.dtype)
        if n_acc:
            @pl.when(i == 0)
            def _():
                for r in outs[n_ro:]:
                    r[...] = jnp.zeros_like(r)
            for r, v in zip(outs[n_ro:], res[n_ro:]):
                r[...] += v

    in_specs = [pl.BlockSpec(a.shape, lambda j, i, nd=a.ndim: (0,) * nd) for a in full_consts]
    in_specs += [pl.BlockSpec((nr, w), lambda j, i, o=o: (0, o + j)) for (_, nr, w, o) in col_consts]
    in_specs += [pl.BlockSpec((tm, w), lambda j, i, o=o: (i, o + j)) for (_, w, o) in row_ins]
    out_specs = [pl.BlockSpec((tm, w), lambda j, i: (i, j)) for (w, _) in row_outs]
    out_specs += [pl.BlockSpec((nr, w), lambda j, i: (0, j)) for (nr, w) in acc_outs]
    out_shape = [jax.ShapeDtypeStruct((rows, w * ncol), dt) for (w, dt) in row_outs]
    out_shape += [jax.ShapeDtypeStruct((nr, w * ncol), F32) for (nr, w) in acc_outs]
    args = list(full_consts) + [c[0] for c in col_consts] + [r[0] for r in row_ins]
    out = pl.pallas_call(
        body, name=name, grid=(ncol, nt), in_specs=in_specs, out_specs=out_specs, out_shape=out_shape,
        compiler_params=_params(("parallel", "arbitrary")),
    )(*args)
    return out


def _rms(x, w):
    return x * lax.rsqrt(jnp.mean(x * x, axis=-1, keepdims=True) + EPS) * w


def _lane_lo(shape):
    return lax.broadcasted_iota(jnp.int32, shape, len(shape) - 1) < HEAD_DIM


def _pair_sum(x):
    lo = _lane_lo(x.shape)
    s0 = jnp.sum(jnp.where(lo, x, 0.0), axis=-1, keepdims=True)
    s1 = jnp.sum(jnp.where(lo, 0.0, x), axis=-1, keepdims=True)
    return jnp.where(lo, s0, s1)


def _head_col(x, lo, h):
    keep = lo if h == 0 else jnp.logical_not(lo)
    return jnp.max(jnp.where(keep, x, -jnp.inf), axis=-1, keepdims=True)


def _softplus(x):
    return jnp.maximum(x, 0.0) + jnp.log1p(jnp.exp(-jnp.abs(x)))


def _silu(x):
    return x * jax.nn.sigmoid(x)


def _dot(a, b, contract):
    return lax.dot_general(a.astype(BF16), b.astype(BF16), (contract, ((), ())),
                           preferred_element_type=F32)


def _dot32(a, b, contract):
    return lax.dot_general(a, b, (contract, ((), ())), precision=HIGHEST, preferred_element_type=F32)


def _bd(y):
    yy = jnp.concatenate([y, y], axis=0)
    r = lax.broadcasted_iota(jnp.int32, yy.shape, 0) < HEAD_DIM
    c = lax.broadcasted_iota(jnp.int32, yy.shape, 1) < HEAD_DIM
    return jnp.where(r == c, yy, 0.0)


def _pp(x, y):
    return _dot(x, _bd(y), _CONTRACT["nn"])


def _pp_nt(x, y):
    return _dot(x, _bd(y), _CONTRACT["nt"])


def _pp_tn(x, y):
    full = _dot(x, y, _CONTRACT["tn"])
    return jnp.where(_lane_lo((HEAD_DIM, LANES)), full[:HEAD_DIM], full[HEAD_DIM:])


def _gdn_masks():
    row = lax.broadcasted_iota(jnp.int32, (CHUNK, LANES), 0)
    col = lax.broadcasted_iota(jnp.int32, (CHUNK, LANES), 1) % HEAD_DIM
    return row, col


def _interleave(chains):
    live = list(chains)
    while live:
        for g in list(live):
            try:
                next(g)
            except StopIteration:
                live.remove(g)


def _gdn_forward(qkv, betax, gcx, grow, rows):
    nchunk = rows // CHUNK

    def body(q_ref, k_ref, v_ref, bx_ref, gx_ref, gr_ref, o_ref, ss_ref, ts_ref, state):
        n = pl.program_id(0)

        @pl.when(n == 0)
        def _():
            state[...] = jnp.zeros_like(state)

        row, col = _gdn_masks()
        incl, strict = col <= row, col < row

        def chain(p):
            lanes = pl.ds(p * LANES, LANES)
            q, k, v, bx, gx = q_ref[:, lanes], k_ref[:, lanes], v_ref[:, lanes], bx_ref[:, lanes], gx_ref[:, lanes]
            gr = gr_ref[0, p]
            glast = gx_ref[pl.ds(CHUNK - 1, 1), lanes]
            s = state[p]
            dm = jnp.where(incl, jnp.exp(jnp.minimum(gx - gr, 0.0)), 0.0)
            kb, vb, eg, qs = k * bx, v * bx, jnp.exp(gx), q * SCALE
            yield
            big_g, big_p = _pp_nt(kb, k), _pp_nt(qs, k)
            yield
            x = -jnp.where(strict, big_g * dm, 0.0)
            att = jnp.where(incl, big_p * dm, 0.0)
            tm = jnp.where(row == col, 1.0, 0.0) + x
            x = _pp(x, x)
            yield
            for _ in range(4):
                step, x = _pp(tm, x), _pp(x, x)
                yield
                tm = tm + step
            tm = tm + _pp(tm, x)
            yield
            u, w = _pp(tm, vb), _pp(tm, kb * eg)
            yield
            ws, qgs = _pp(w, s), _pp(qs * eg, s)
            yield
            vn = u - ws
            kd = k * jnp.exp(glast - gx)
            avn, upd = _pp(att, vn), _pp_tn(kd, vn)
            yield
            ss_ref[0, p] = s
            ts_ref[0, p] = tm
            o_ref[:, lanes] = qgs + avn
            state[p] = s * jnp.exp(glast) + upd

        _interleave([chain(p) for p in range(PAIRS)])

    blk = lambda j: pl.BlockSpec((CHUNK, WIDTH), lambda n, j=j: (n, j))
    sv = pl.BlockSpec((1, PAIRS, CHUNK, LANES), lambda n: (n, 0, 0, 0))
    return pl.pallas_call(
        body, name="gdn_fwd", grid=(nchunk,),
        in_specs=[blk(0), blk(1), blk(2), blk(0), blk(0),
                  pl.BlockSpec((1, PAIRS, 1, LANES), lambda n: (n, 0, 0, 0))],
        out_specs=[blk(0), sv, sv],
        out_shape=[jax.ShapeDtypeStruct((rows, WIDTH), F32),
                   jax.ShapeDtypeStruct((nchunk, PAIRS, CHUNK, LANES), F32),
                   jax.ShapeDtypeStruct((nchunk, PAIRS, CHUNK, LANES), F32)],
        scratch_shapes=[pltpu.VMEM((PAIRS, CHUNK, LANES), F32)],
        compiler_params=_params(("arbitrary",)),
    )(qkv, qkv, qkv, betax, gcx, grow)


def _gdn_backward(qkv, betax, gcx, grow, ssave, tsave, do, rows):
    nchunk = rows // CHUNK

    def body(q_ref, k_ref, v_ref, bx_ref, gx_ref, gr_ref, ss_ref, ts_ref, do_ref,
             dq_ref, dk_ref, dv_ref, dbx_ref, dgx_ref, dgr_ref, dstate):
        n = pl.program_id(0)

        @pl.when(n == 0)
        def _():
            dstate[...] = jnp.zeros_like(dstate)

        row, col = _gdn_masks()
        incl, strict = col <= row, col < row

        def chain(p):
            lanes = pl.ds(p * LANES, LANES)
            q, k, v, bx, gx = q_ref[:, lanes], k_ref[:, lanes], v_ref[:, lanes], bx_ref[:, lanes], gx_ref[:, lanes]
            gr = gr_ref[0, p]
            glast = gx_ref[pl.ds(CHUNK - 1, 1), lanes]
            s, tm, d_o = ss_ref[0, p], ts_ref[0, p], do_ref[:, lanes]
            ds_out = dstate[p]
            dm = jnp.where(incl, jnp.exp(jnp.minimum(gx - gr, 0.0)), 0.0)
            kb, vb, eg, qs = k * bx, v * bx, jnp.exp(gx), q * SCALE
            kbg, qg = kb * eg, qs * eg
            ed = jnp.exp(glast - gx)
            kd = k * ed
            eglast = jnp.exp(glast)
            yield
            big_g, big_p = _pp_nt(kb, k), _pp_nt(qs, k)
            u, w = _pp(tm, vb), _pp(tm, kbg)
            dqg, kds = _pp_nt(d_o, s), _pp(kd, ds_out)
            yield
            low = jnp.where(strict, big_g * dm, 0.0)
            att = jnp.where(incl, big_p * dm, 0.0)
            ws, atd = _pp(w, s), _pp_tn(att, d_o)
            yield
            vn = u - ws
            dvn = kds + atd
            dkd, datt_raw = _pp_nt(vn, ds_out), _pp_nt(d_o, vn)
            dw_neg, dvb = _pp_nt(dvn, s), _pp_tn(tm, dvn)
            dtm_a, wdv = _pp_nt(dvn, vb), _pp_tn(w, dvn)
            qgd = _pp_tn(qg, d_o)
            yield
            datt = jnp.where(incl, datt_raw, 0.0)
            dw = -dw_neg
            dtm_b, dkbg = _pp_nt(dw, kbg), _pp_tn(tm, dw)
            dbig_p = datt * dm
            dqs_a, dk_p = _pp(dbig_p, k), _pp_tn(dbig_p, qs)
            yield
            inner = _pp_tn(tm, dtm_a + dtm_b)
            yield
            dlow = jnp.where(strict, -_pp_nt(inner, tm), 0.0)
            yield
            dbig_g = dlow * dm
            dkb_a, dk_g = _pp(dbig_g, k), _pp_tn(dbig_g, kb)
            yield
            dkb = dkb_a + dkbg * eg
            dqs = dqs_a + dqg * eg
            dk = dk_g + dk_p + dkd * ed + dkb * bx
            z = dlow * low + datt * att
            kdterm = dkd * kd
            dglast = (jnp.sum(ds_out * s, axis=0, keepdims=True) * eglast
                      + jnp.sum(kdterm, axis=0, keepdims=True))
            dgx = dqg * qg + dkbg * kbg - kdterm
            dgx = dgx + jnp.where(col == 0, _pair_sum(z), 0.0)
            dgx = dgx + jnp.where(row == CHUNK - 1, dglast, 0.0)
            dq_ref[:, lanes] = dqs * SCALE
            dk_ref[:, lanes] = dk
            dv_ref[:, lanes] = dvb * bx
            dbx_ref[:, lanes] = dkb * k + dvb * v
            dgx_ref[:, lanes] = dgx
            dgr_ref[0, p] = -jnp.sum(z, axis=0, keepdims=True)
            dstate[p] = ds_out * eglast + qgd - wdv

        _interleave([chain(p) for p in range(PAIRS)])

    last = nchunk - 1
    blk = lambda j: pl.BlockSpec((CHUNK, WIDTH), lambda n, j=j: (last - n, j))
    sv = pl.BlockSpec((1, PAIRS, CHUNK, LANES), lambda n: (last - n, 0, 0, 0))
    gr_spec = pl.BlockSpec((1, PAIRS, 1, LANES), lambda n: (last - n, 0, 0, 0))
    wide = jax.ShapeDtypeStruct((rows, WIDTH), F32)
    return pl.pallas_call(
        body, name="gdn_bwd", grid=(nchunk,),
        in_specs=[blk(0), blk(1), blk(2), blk(0), blk(0), gr_spec, sv, sv, blk(0)],
        out_specs=[blk(0)] * 5 + [gr_spec],
        out_shape=[wide] * 5 + [jax.ShapeDtypeStruct((nchunk, PAIRS, 1, LANES), F32)],
        scratch_shapes=[pltpu.VMEM((PAIRS, CHUNK, LANES), F32)],
        compiler_params=_params(("arbitrary",)),
    )(qkv, qkv, qkv, betax, gcx, grow, ssave, tsave, do)


ATT_TQ = 256


def _att_scores(qh, kt, fk, diag):
    s = _dot(qh, kt, _CONTRACT["nt"]) - fk
    if diag:
        r = lax.broadcasted_iota(jnp.int32, s.shape, 0)
        c = lax.broadcasted_iota(jnp.int32, s.shape, 1)
        s = jnp.where(r >= c, s, -jnp.inf)
    return s


def _head_masks(n):
    lo = _lane_lo((n, LANES))
    return [lo, jnp.logical_not(lo)]


def _attention_forward(fqk, proj, frow, rows):
    tq = tk = min(ATT_TQ, rows)
    nq = rows // tq
    v_off = 3072 // LANES

    def body(q_ref, k_ref, v_ref, fr_ref, o_ref, lse_ref):
        qi = pl.program_id(1)
        q = q_ref[...] * SCALE
        keep_q, keep_k = _head_masks(tq), _head_masks(tk)
        qh = [jnp.where(keep_q[h], q, 0.0).astype(BF16) for h in range(2)]

        def tile(ki, carry, diag):
            k0 = pl.multiple_of(ki * tk, tk)
            kt = k_ref[pl.ds(k0, tk), :].astype(BF16)
            v_t = v_ref[pl.ds(k0, tk), :]
            out = [None, None]

            def chain(h):
                m, l, acc = carry[h]
                vt = jnp.where(keep_k[h], v_t, 0.0).astype(BF16)
                yield
                s = _att_scores(qh[h], kt, fr_ref[0, pl.ds(h, 1), pl.ds(k0, tk)], diag)
                yield
                m_new = jnp.maximum(m, jnp.max(s, axis=-1, keepdims=True))
                p = jnp.exp(s - m_new)
                alpha = jnp.exp(m - m_new)
                l = alpha * l + jnp.sum(p, axis=-1, keepdims=True)
                p_hi = p.astype(BF16)
                p_lo = p - p_hi.astype(F32)
                yield
                out[h] = (m_new, l, alpha * acc + _dot(p_hi, vt, _CONTRACT["nn"]) + _dot(p_lo, vt, _CONTRACT["nn"]))

            _interleave([chain(0), chain(1)])
            return tuple(out)

        one = (jnp.full((tq, 1), -jnp.inf, F32), jnp.zeros((tq, 1), F32), jnp.zeros((tq, LANES), F32))
        carry = lax.fori_loop(0, qi, lambda ki, c: tile(ki, c, False), (one, one))
        (m0, l0, acc0), (m1, l1, acc1) = tile(qi, carry, True)
        o_ref[...] = acc0 / l0 + acc1 / l1
        lse_ref[...] = jnp.where(keep_q[0], m0 + jnp.log(l0), m1 + jnp.log(l1))

    whole = lambda off: pl.BlockSpec((rows, LANES), lambda p, i, off=off: (0, off + p))
    qblk = lambda off: pl.BlockSpec((tq, LANES), lambda p, i, off=off: (i, off + p))
    wide = jax.ShapeDtypeStruct((rows, WIDTH), F32)
    return pl.pallas_call(
        body, name="fox_fwd", grid=(PAIRS, nq),
        in_specs=[qblk(0), whole(PAIRS), whole(v_off), pl.BlockSpec((1, 2, rows), lambda p, i: (p, 0, 0))],
        out_specs=[qblk(0), qblk(0)], out_shape=[wide, wide],
        compiler_params=_params(("parallel", "arbitrary")),
    )(fqk, fqk, proj, frow)


def _attention_delta(fqk, proj, frow, lse, dao, rows):
    tq = tk = min(ATT_TQ, rows)
    nq = rows // tq
    v_off = 3072 // LANES

    def body(q_ref, k_ref, v_ref, fr_ref, lse_ref, do_ref, delta_ref):
        qi = pl.program_id(1)
        q, d_o, lse_t = q_ref[...] * SCALE, do_ref[...], lse_ref[...]
        keep_q = _head_masks(tq)
        qh = [jnp.where(keep_q[h], q, 0.0).astype(BF16) for h in range(2)]
        doh = [jnp.where(keep_q[h], d_o, 0.0).astype(BF16) for h in range(2)]
        lse_h = [_head_col(lse_t, keep_q[0], h) for h in range(2)]

        def tile(ki, carry, diag):
            k0 = pl.multiple_of(ki * tk, tk)
            kt = k_ref[pl.ds(k0, tk), :].astype(BF16)
            vt = v_ref[pl.ds(k0, tk), :].astype(BF16)
            out = [None, None]

            def chain(h):
                s = _att_scores(qh[h], kt, fr_ref[0, pl.ds(h, 1), pl.ds(k0, tk)], diag)
                dp = _dot(doh[h], vt, _CONTRACT["nt"])
                yield
                out[h] = carry[h] + jnp.sum(jnp.exp(s - lse_h[h]) * dp, axis=-1, keepdims=True)

            _interleave([chain(0), chain(1)])
            return tuple(out)

        zero = jnp.zeros((tq, 1), F32)
        carry = lax.fori_loop(0, qi, lambda ki, c: tile(ki, c, False), (zero, zero))
        d0, d1 = tile(qi, carry, True)
        delta_ref[...] = jnp.where(keep_q[0], d0, d1)

    whole = lambda off: pl.BlockSpec((rows, LANES), lambda p, i, off=off: (0, off + p))
    qblk = lambda off: pl.BlockSpec((tq, LANES), lambda p, i, off=off: (i, off + p))
    return pl.pallas_call(
        body, name="fox_delta", grid=(PAIRS, nq),
        in_specs=[qblk(0), whole(PAIRS), whole(v_off),
                  pl.BlockSpec((1, 2, rows), lambda p, i: (p, 0, 0)), qblk(0), qblk(0)],
        out_specs=qblk(0), out_shape=jax.ShapeDtypeStruct((rows, WIDTH), F32),
        compiler_params=_params(("parallel", "arbitrary")),
    )(fqk, fqk, proj, frow, lse, dao)


def _attention_backward(fqk, proj, frow, ao, lse, dao, rows):
    tq = tk = min(ATT_TQ, rows)
    nq = rows // tq
    v_off = 3072 // LANES

    def body(q_ref, k_ref, v_ref, fr_ref, o_ref, lse_ref, do_ref, dq_ref, dk_ref, dv_ref, dfr_ref):
        ki = pl.program_id(1)

        @pl.when(ki == 0)
        def _():
            dq_ref[...] = jnp.zeros_like(dq_ref)

        keep_q, keep_k = _head_masks(tq), _head_masks(tk)
        k_t = k_ref[...]
        kt = k_t.astype(BF16)
        vt = v_ref[...].astype(BF16)
        kh = [jnp.where(keep_k[h], k_t, 0.0).astype(BF16) for h in range(2)]
        fk = [fr_ref[0, pl.ds(h, 1), :] for h in range(2)]

        def tile(qi, carry, diag):
            dk, dv, df0, df1 = carry
            rows_q = pl.ds(pl.multiple_of(qi * tq, tq), tq)
            q, d_o, lse_t = q_ref[rows_q, :] * SCALE, do_ref[rows_q, :], lse_ref[rows_q, :]
            delta_x = _pair_sum(d_o.astype(BF16).astype(F32) * o_ref[rows_q, :])
            res = [None, None]

            def chain(h):
                qh = jnp.where(keep_q[h], q, 0.0).astype(BF16)
                doh = jnp.where(keep_q[h], d_o, 0.0).astype(BF16)
                lse_h, delta_h = _head_col(lse_t, keep_q[0], h), _head_col(delta_x, keep_q[0], h)
                yield
                s, dp = _att_scores(qh, kt, fk[h], diag), _dot(doh, vt, _CONTRACT["nt"])
                yield
                p = jnp.exp(s - lse_h)
                ds = p * (dp - delta_h)
                yield
                res[h] = (_dot(p, doh, _CONTRACT["tn"]), _dot(ds, qh, _CONTRACT["tn"]),
                          _dot(ds, kh[h], _CONTRACT["nn"]), jnp.sum(ds, axis=0, keepdims=True))

            _interleave([chain(0), chain(1)])
            (dv0, dk0, dq0, s0), (dv1, dk1, dq1, s1) = res
            dq_ref[rows_q, :] += (dq0 + dq1) * SCALE
            return dk + dk0 + dk1, dv + dv0 + dv1, df0 - s0, df1 - s1

        zero_kv = jnp.zeros((tk, LANES), F32)
        zero_f = jnp.zeros((1, tk), F32)
        carry = tile(ki, (zero_kv, zero_kv, zero_f, zero_f), True)
        dk, dv, df0, df1 = lax.fori_loop(ki + 1, nq, lambda qi, c: tile(qi, c, False), carry)
        dk_ref[...] = dk
        dv_ref[...] = dv.astype(dv_ref.dtype)
        dfr_ref[0, pl.ds(0, 1), :] = df0
        dfr_ref[0, pl.ds(1, 1), :] = df1

    whole = lambda off: pl.BlockSpec((rows, LANES), lambda p, i, off=off: (0, off + p))
    kblk = lambda off: pl.BlockSpec((tk, LANES), lambda p, i, off=off: (i, off + p))
    fr_spec = pl.BlockSpec((1, 2, tk), lambda p, i: (p, 0, i))
    wide = jax.ShapeDtypeStruct((rows, WIDTH), F32)
    return pl.pallas_call(
        body, name="fox_bwd", grid=(PAIRS, nq),
        in_specs=[whole(0), kblk(PAIRS), kblk(v_off), fr_spec, whole(0), whole(0), whole(0)],
        out_specs=[whole(0), kblk(0), kblk(0), fr_spec],
        out_shape=[wide, wide, jax.ShapeDtypeStruct((rows, WIDTH), BF16),
                   jax.ShapeDtypeStruct((PAIRS, 2, rows), F32)],
        compiler_params=_params(("parallel", "arbitrary")),
    )(fqk, fqk, proj, frow, ao, lse, dao)


def _lane_ids(shape):
    return lax.broadcasted_iota(jnp.int32, shape, len(shape) - 1)


def _gates_elem(a_log, dt_bias, f_bias, pre):
    lane = _lane_ids(pre.shape)
    beta = jax.nn.sigmoid(pre)
    g = -jnp.exp(a_log) * _softplus(pre + dt_bias)
    lf = -_softplus(-(pre + f_bias))
    return jnp.where(lane < 8, beta, jnp.where(lane < 16, g, jnp.where(lane < 24, lf, 0.0)))


def _tri_consts():
    r = np.arange(LANES)[:, None]
    c = np.arange(LANES)[None, :]
    full = (c <= r).astype(np.float32)
    chunked = full * ((r // CHUNK) == (c // CHUNK))
    return jnp.asarray(chunked), jnp.asarray(full)


def _cums_fwd(lc, lf, gates):
    rows = gates.shape[0]
    lane = _lane_ids((LANES, LANES))
    carry = jnp.zeros((1, LANES), F32)
    out = []
    for r in range(rows // LANES):
        blk = gates[r * LANES:(r + 1) * LANES]
        gc = _dot32(lc, blk, _CONTRACT["nn"])
        f = _dot32(lf, blk, _CONTRACT["nn"]) + carry
        carry = carry + jnp.sum(blk, axis=0, keepdims=True)
        out.append(jnp.where((lane >= 8) & (lane < 16), gc, jnp.where((lane >= 16) & (lane < 24), f, 0.0)))
    return jnp.concatenate(out, axis=0)


def _cums_bwd(lc, lf, dcums):
    rows = dcums.shape[0]
    lane = _lane_ids((LANES, LANES))
    is_g = (lane >= 8) & (lane < 16)
    is_f = (lane >= 16) & (lane < 24)
    carry = jnp.zeros((1, LANES), F32)
    out = [None] * (rows // LANES)
    for r in reversed(range(rows // LANES)):
        blk = dcums[r * LANES:(r + 1) * LANES]
        dg = jnp.where(is_g, blk, 0.0)
        df = jnp.where(is_f, blk, 0.0)
        out[r] = _dot32(lc, dg, _CONTRACT["tn"]) + _dot32(lf, df, _CONTRACT["tn"]) + carry
        carry = carry + jnp.sum(df, axis=0, keepdims=True)
    return jnp.concatenate(out, axis=0)


def _expand_consts():
    xb = np.zeros((LANES, WIDTH), np.float32)
    xg = np.zeros((LANES, WIDTH), np.float32)
    for h in range(HEADS):
        xb[h, h * HEAD_DIM:(h + 1) * HEAD_DIM] = 1.0
        xg[8 + h, h * HEAD_DIM:(h + 1) * HEAD_DIM] = 1.0
    return jnp.asarray(xb), jnp.asarray(xg)


def _shift_down(x, s):
    if s == 0:
        return x
    row = lax.broadcasted_iota(jnp.int32, x.shape, 0)
    return jnp.where(row >= s, pltpu.roll(x, s, 0), 0.0)


def _shift_up(x, s):
    if s == 0:
        return x
    n = x.shape[0]
    row = lax.broadcasted_iota(jnp.int32, x.shape, 0)
    return jnp.where(row < n - s, pltpu.roll(x, n - s, 0), 0.0)


def _row_of(cw, i):
    row = lax.broadcasted_iota(jnp.int32, cw.shape, 0)
    return jnp.sum(jnp.where(row == i, cw, 0.0), axis=0, keepdims=True)


def _conv(cw, x):
    c = jnp.zeros_like(x)
    for i in range(CONV_K):
        c = c + _row_of(cw, i) * _shift_down(x, CONV_K - 1 - i)
    return c


def _post_conv(is_qk, c):
    s = _silu(c)
    n = s * lax.rsqrt(_pair_sum(s * s) + EPS)
    return jnp.where(is_qk, n, s)


def _gdn_prep_fwd(col, cw, x):
    return (_post_conv(col < 2 * PAIRS, _conv(cw, x)),)


def _gdn_prep_bwd(is_qk, cw, x, dy):
    c = _conv(cw, x)
    _, vjp = jax.vjp(lambda cc: _post_conv(is_qk, cc), c)
    (dc,) = vjp(dy)
    dx = jnp.zeros_like(x)
    row = lax.broadcasted_iota(jnp.int32, cw.shape, 0)
    dcw = jnp.zeros(cw.shape, F32)
    for i in range(CONV_K):
        s = CONV_K - 1 - i
        dx = dx + _row_of(cw, i) * _shift_up(dc, s)
        dcw = dcw + jnp.where(row == i, jnp.sum(dc * _shift_down(x, s), axis=0, keepdims=True), 0.0)
    return dx, dcw


def _head_rms(w, x):
    return x * lax.rsqrt(_pair_sum(x * x) / HEAD_DIM + EPS) * w


def _cat_weights(w_in_t):
    tail = jnp.pad(w_in_t[4112:4120], ((0, D_CAT - D_IN), (0, 0)))
    return jnp.concatenate([w_in_t[:2048], w_in_t[2064:4112], w_in_t[2048:2064], tail], axis=0)


def _uncat_grad(g):
    return jnp.concatenate([g[:2048], g[4096:4112], g[2048:4096], g[4112:4120]], axis=0)


def _lanes_to_rowform(v8, rows):
    return v8.reshape(rows // CHUNK, CHUNK, HEADS).transpose(0, 2, 1).reshape(rows // CHUNK, PAIRS, 1, LANES)


def _rowform_to_lanes(v, rows):
    return v.reshape(rows // CHUNK, HEADS, CHUNK).transpose(0, 2, 1).reshape(rows, HEADS)


def _local_step(x, target, norm1_w, a_log, dt_bias, out_norm_w, f_bias, q_norm_w, k_norm_w,
                norm2_w, final_w, first_weights, late_weights, early_grads_ready):
    rows = x.shape[0]
    tm = min(512, rows)
    lc, lf = _tri_consts()
    xb, xg = _expand_consts()

    (h1,) = _tiles(lambda col, w, xx: (_rms(xx, w),), name="norm1", rows=rows, tm=tm,
                   full_consts=[norm1_w], row_ins=[(x, D_MODEL, 0)], row_outs=[(D_MODEL, BF16)])
    w_cat, conv_w = first_weights(h1)
    proj = _mm(h1, w_cat, dims="nt", name="in_proj", tn=1408, tk=1024)

    lane_pad = lambda v, off: jnp.pad(v.reshape(1, -1), ((0, 0), (off, LANES - off - v.size)))
    p_a, p_dt, p_fb = lane_pad(a_log, 8), lane_pad(dt_bias, 8), lane_pad(f_bias, 16)

    def gates_fwd(col, lcv, lfv, a, dt, fb, pre):
        gates = _gates_elem(a, dt, fb, pre)
        return gates, _cums_fwd(lcv, lfv, gates)

    gates, cums = _tiles(gates_fwd, name="gates", rows=rows, tm=rows,
                         full_consts=[lc, lf, p_a, p_dt, p_fb], row_ins=[(proj, LANES, COL_SMALL)],
                         row_outs=[(LANES, F32), (LANES, F32)])

    def expand_fwd(col, b, g, gt, cm):
        return (_dot32(gt, b, _CONTRACT["nn"]), _dot32(cm, g, _CONTRACT["nn"]))

    betax, gcx = _tiles(expand_fwd, name="expand", rows=rows, tm=tm, full_consts=[xb, xg],
                        row_ins=[(gates, LANES, 0), (cums, LANES, 0)],
                        row_outs=[(WIDTH, F32)] * 2)
    grow = _lanes_to_rowform(cums[:, 8:16], rows)
    frow = cums[:, 16:24].T.reshape(PAIRS, 2, rows)

    (qkv,) = _tiles(_gdn_prep_fwd, name="gdn_prep", rows=rows, tm=rows, ncol=3 * PAIRS,
                    col_consts=[(conv_w, CONV_K, LANES, 0)], row_ins=[(proj, LANES, 0)],
                    row_outs=[(LANES, F32)])
    o_gdn, ssave, tsave = _gdn_forward(qkv, betax, gcx, grow, rows)

    w_qk = jnp.concatenate([jnp.tile(q_norm_w.reshape(1, -1), (1, HEADS)),
                            jnp.tile(k_norm_w.reshape(1, -1), (1, HEADS))], axis=1)
    fox_off = 2048 // LANES
    (fqk,) = _tiles(lambda col, w, xx: (_head_rms(w, xx),), name="fox_prep", rows=rows, tm=rows, ncol=2 * PAIRS,
                    col_consts=[(w_qk, 1, LANES, 0)], row_ins=[(proj, LANES, fox_off)],
                    row_outs=[(LANES, F32)])
    ao, lse = _attention_forward(fqk, proj, frow, rows)

    w_on = jnp.tile(out_norm_w.reshape(1, -1), (1, 2))
    z_off, fg_off = 1536 // LANES, 3584 // LANES
    mix_g_fn = lambda w, o, z: _head_rms(w, o) * _silu(z)
    mix_f_fn = lambda a, g: a * jax.nn.sigmoid(g)
    (mix_g,) = _tiles(lambda col, w, o, z: (mix_g_fn(w, o, z),), name="mix_gdn", rows=rows, tm=rows, ncol=PAIRS,
                      full_consts=[w_on], row_ins=[(o_gdn, LANES, 0), (proj, LANES, z_off)],
                      row_outs=[(LANES, BF16)])
    (mix_f,) = _tiles(lambda col, a, g: (mix_f_fn(a, g),), name="mix_fox", rows=rows, tm=rows, ncol=PAIRS,
                      row_ins=[(ao, LANES, 0), (proj, LANES, fg_off)], row_outs=[(LANES, BF16)])
    mix = jnp.concatenate([mix_g, mix_f], axis=1)
    w_out, w_gate, w_up, w_down = late_weights(mix)
    x1 = _mm(mix, w_out, dims="nn", name="out_proj", add=x, tn=D_MODEL, tk=1024)

    (h2,) = _tiles(lambda col, w, xx: (_rms(xx, w),), name="norm2", rows=rows, tm=tm,
                   full_consts=[norm2_w], row_ins=[(x1, D_MODEL, 0)], row_outs=[(D_MODEL, BF16)])
    t_rows, t_cols, t_act = min(1024, rows), D_MODEL, min(512, rows)
    n_rt = rows // t_rows
    st_act = jax.ShapeDtypeStruct((N_CHIPS, rows, FF_SHARD), BF16)
    st_rows = pl.BlockSpec((None, rows, FF_SHARD), lambda i, j: (j, i, 0))
    out_rows = pl.BlockSpec((t_rows, t_cols), lambda i, n: (i, n))
    flat = lambda t: t.reshape(N_CHIPS * rows, FF_SHARD)

    def ffn_in(w_st, name):
        return _mm_blocks(h2, w_st, name=name, grid=(1, N_CHIPS), dims="nt",
                          a_spec=pl.BlockSpec((rows, D_MODEL), lambda i, j: (i, 0)),
                          b_spec=pl.BlockSpec((None, FF_SHARD, D_MODEL), lambda i, j: (j, 0, 0)),
                          o_spec=st_rows, out_shape=st_act)

    gate, up = ffn_in(w_gate, "ffn_gate"), ffn_in(w_up, "ffn_up")
    act_fn = lambda g, u: _silu(g.astype(F32)) * u.astype(F32)
    (act,) = _tiles(lambda col, g, u: (act_fn(g, u),), name="ffn_act", rows=N_CHIPS * rows, tm=t_act,
                    row_ins=[(flat(gate), FF_SHARD, 0), (flat(up), FF_SHARD, 0)], row_outs=[(FF_SHARD, BF16)])
    act = act.reshape(st_act.shape)
    x2 = _mm_blocks(act, w_down, name="ffn_down", grid=(n_rt, D_MODEL // t_cols), dims="nn", n_sum=N_CHIPS,
                    a_spec=pl.BlockSpec((N_CHIPS, t_rows, FF_SHARD), lambda i, n: (0, i, 0)),
                    b_spec=pl.BlockSpec((N_CHIPS, FF_SHARD, t_cols), lambda i, n: (0, 0, n)),
                    o_spec=out_rows, out_shape=jax.ShapeDtypeStruct((rows, D_MODEL), F32),
                    add=x1, add_spec=out_rows)

    def final_fn(col, w, xx, tgt):
        y, vjp = jax.vjp(_rms, xx, w)
        err = y - tgt
        loss = 0.5 * jnp.sum(err * err) / D_MODEL
        dx, dw = vjp(err / D_MODEL)
        return dx, dx, jnp.full((1, LANES), loss, F32), dw

    dx2, dx2_b, loss, d_final_w = _tiles(final_fn, name="final_loss", rows=rows, tm=tm, full_consts=[final_w],
                                         row_ins=[(x2, D_MODEL, 0), (target, D_MODEL, 0)],
                                         row_outs=[(D_MODEL, F32), (D_MODEL, BF16)],
                                         acc_outs=[(1, LANES), (1, D_MODEL)])

    dact = _mm_blocks(dx2_b, w_down, name="d_act", grid=(1, N_CHIPS), dims="nt",
                      a_spec=pl.BlockSpec((rows, D_MODEL), lambda i, j: (i, 0)),
                      b_spec=pl.BlockSpec((None, FF_SHARD, D_MODEL), lambda i, j: (j, 0, 0)),
                      o_spec=st_rows, out_shape=st_act)
    def g_ffn(d_st, other, name):
        return _mm_blocks(d_st, other, name=name, grid=(N_CHIPS, D_MODEL // t_cols), dims="tn",
                          a_spec=pl.BlockSpec((None, rows, FF_SHARD), lambda j, n: (j, 0, 0)),
                          b_spec=pl.BlockSpec((rows, t_cols), lambda j, n: (0, n)),
                          o_spec=pl.BlockSpec((None, FF_SHARD, t_cols), lambda j, n: (j, 0, n)),
                          out_shape=jax.ShapeDtypeStruct((N_CHIPS, FF_SHARD, D_MODEL), BF16))

    g_down = g_ffn(act, dx2_b, "g_down")

    def act_bwd(col, g, u, d):
        _, vjp = jax.vjp(lambda gg, uu: _silu(gg) * uu, g.astype(F32), u.astype(F32))
        return vjp(d.astype(F32))

    dgate, dup = _tiles(act_bwd, name="ffn_act_bwd", rows=N_CHIPS * rows, tm=t_act,
                        row_ins=[(flat(gate), FF_SHARD, 0), (flat(up), FF_SHARD, 0), (flat(dact), FF_SHARD, 0)],
                        row_outs=[(FF_SHARD, BF16), (FF_SHARD, BF16)])
    dgate, dup = dgate.reshape(st_act.shape), dup.reshape(st_act.shape)

    def d_h2(d_st, w_st, name, add):
        return _mm_blocks(d_st, w_st, name=name, grid=(n_rt, D_MODEL // t_cols), dims="nn", n_sum=N_CHIPS,
                          a_spec=pl.BlockSpec((N_CHIPS, t_rows, FF_SHARD), lambda i, n: (0, i, 0)),
                          b_spec=pl.BlockSpec((N_CHIPS, FF_SHARD, t_cols), lambda i, n: (0, 0, n)),
                          o_spec=out_rows, out_shape=jax.ShapeDtypeStruct((rows, D_MODEL), F32),
                          add=add, add_spec=out_rows)

    dh2 = d_h2(dup, w_up, "d_h2_up", d_h2(dgate, w_gate, "d_h2_gate", None))
    g_gate, g_up = g_ffn(dgate, h2, "g_gate"), g_ffn(dup, h2, "g_up")

    def norm_bwd(col, w, xx, dh, dres):
        _, vjp = jax.vjp(_rms, xx, w)
        dx, dw = vjp(dh)
        return dx + dres, dx + dres, dw

    dx1, dx1_b, d_norm2_w = _tiles(norm_bwd, name="norm2_bwd", rows=rows, tm=tm, full_consts=[norm2_w],
                                   row_ins=[(x1, D_MODEL, 0), (dh2, D_MODEL, 0), (dx2, D_MODEL, 0)],
                                   row_outs=[(D_MODEL, F32), (D_MODEL, BF16)], acc_outs=[(1, D_MODEL)])
    dmix = _mm(dx1_b, w_out, dims="nt", name="d_mix", tn=D_MODEL, tk=1024)
    g_out = _mm(mix, dx1_b, dims="tn", name="g_out", tn=D_MODEL, tk=rows, out_dtype=BF16)
    w_on = w_on + early_grads_ready(g_out, g_gate, g_up, g_down)

    def mix_g_bwd(col, w, o, z, d):
        _, vjp = jax.vjp(mix_g_fn, w, o, z)
        dw, do_, dz = vjp(d)
        return do_, dz, dw

    do_gdn, dz, d_on = _tiles(mix_g_bwd, name="mix_gdn_bwd", rows=rows, tm=rows, ncol=PAIRS, full_consts=[w_on],
                              row_ins=[(o_gdn, LANES, 0), (proj, LANES, z_off), (dmix, LANES, 0)],
                              row_outs=[(LANES, F32), (LANES, BF16)], acc_outs=[(1, LANES)])

    def mix_f_bwd(col, a, g, d):
        _, vjp = jax.vjp(mix_f_fn, a, g)
        return vjp(d)

    dao, dfgate = _tiles(mix_f_bwd, name="mix_fox_bwd", rows=rows, tm=rows, ncol=PAIRS,
                         row_ins=[(ao, LANES, 0), (proj, LANES, fg_off), (dmix, LANES, PAIRS)],
                         row_outs=[(LANES, F32), (LANES, BF16)])

    dfq, dfk, dfv, dfrow = _attention_backward(fqk, proj, frow, ao, lse, dao, rows)

    def fox_prep_bwd(col, w, xx, d):
        _, vjp = jax.vjp(_head_rms, w, xx)
        dw, dx = vjp(d)
        return dx, dw

    dfqk, d_wqk = [], []
    for part, d_n in enumerate((dfq, dfk)):
        dx_p, dw_p = _tiles(fox_prep_bwd, name="fox_prep_bwd_" + "qk"[part], rows=rows, tm=rows, ncol=PAIRS,
                            col_consts=[(w_qk, 1, LANES, part * PAIRS)],
                            row_ins=[(proj, LANES, fox_off + part * PAIRS), (d_n, LANES, 0)],
                            row_outs=[(LANES, BF16)], acc_outs=[(1, LANES)])
        dfqk.append(dx_p)
        d_wqk.append(dw_p)

    dq, dk, dv, dbetax, dgcx, dgrow = _gdn_backward(qkv, betax, gcx, grow, ssave, tsave, do_gdn, rows)
    dqkv, d_conv = [], []
    for part, d_n in enumerate((dq, dk, dv)):
        prep_bwd = lambda col, cw, xx, dy, is_qk=(part < 2): _gdn_prep_bwd(is_qk, cw, xx, dy)
        dx_p, dw_p = _tiles(prep_bwd, name="gdn_prep_bwd_" + "qkv"[part], rows=rows, tm=rows, ncol=PAIRS,
                            col_consts=[(conv_w, CONV_K, LANES, part * PAIRS)],
                            row_ins=[(proj, LANES, part * PAIRS), (d_n, LANES, 0)],
                            row_outs=[(LANES, BF16)], acc_outs=[(CONV_K, LANES)])
        dqkv.append(dx_p)
        d_conv.append(dw_p)
    d_conv = jnp.concatenate(d_conv, axis=1)

    def expand_bwd(col, b, g, db, dg):
        return (_dot32(db, b, _CONTRACT["nt"]), _dot32(dg, g, _CONTRACT["nt"]))

    dgates_b, dcums_g = _tiles(expand_bwd, name="expand_bwd", rows=rows, tm=tm, full_consts=[xb, xg],
                               row_ins=[(dbetax, WIDTH, 0), (dgcx, WIDTH, 0)],
                               row_outs=[(LANES, F32), (LANES, F32)])
    dcums_row = jnp.concatenate([jnp.zeros((rows, 8), F32), _rowform_to_lanes(dgrow, rows),
                                 dfrow.reshape(HEADS, rows).T, jnp.zeros((rows, LANES - 24), F32)], axis=1)

    def gates_bwd(col, lcv, lfv, a, dt, fb, pre, dgb, dcg, dcr):
        lane = _lane_ids(pre.shape)
        dgates = jnp.where(lane < 8, dgb, _cums_bwd(lcv, lfv, dcg + dcr))
        _, vjp = jax.vjp(_gates_elem, a, dt, fb, pre)
        da, ddt, dfb, dpre = vjp(dgates)
        return dpre, da, ddt, dfb

    dpre, d_a, d_dt, d_fb = _tiles(gates_bwd, name="gates_bwd", rows=rows, tm=rows,
                                   full_consts=[lc, lf, p_a, p_dt, p_fb],
                                   row_ins=[(proj, LANES, COL_SMALL), (dgates_b, LANES, 0), (dcums_g, LANES, 0),
                                            (dcums_row, LANES, 0)],
                                   row_outs=[(LANES, BF16)], acc_outs=[(1, LANES)] * 3)

    dproj = jnp.concatenate(dqkv + [dz] + dfqk + [dfv, dfgate, dpre], axis=1)
    dh1 = _mm(dproj, w_cat, dims="nn", name="d_h1", tm=512, tn=D_MODEL, tk=D_CAT)
    g_cat = _mm(dproj, h1, dims="tn", name="g_in", tm=1408, tn=D_MODEL, tk=rows)

    def norm1_bwd(col, w, xx, dh, dres):
        _, vjp = jax.vjp(_rms, xx, w)
        dx, dw = vjp(dh)
        return dx + dres, dw

    grad_x, d_norm1_w = _tiles(norm1_bwd, name="norm1_bwd", rows=rows, tm=tm, full_consts=[norm1_w],
                               row_ins=[(x, D_MODEL, 0), (dh1, D_MODEL, 0), (dx1, D_MODEL, 0)],
                               row_outs=[(D_MODEL, F32)], acc_outs=[(1, D_MODEL)])

    fold = lambda v: v.reshape(-1, HEAD_DIM).sum(axis=0)
    small = dict(
        loss=loss[0, 0],
        norm1_w=d_norm1_w, conv_w=d_conv, a_log=d_a[0, 8:16], dt_bias=d_dt[0, 8:16],
        out_norm_w=fold(d_on), f_bias=d_fb[0, 16:24], q_norm_w=fold(d_wqk[0]),
        k_norm_w=fold(d_wqk[1]), norm2_w=d_norm2_w, final_w=d_final_w)
    return grad_x, g_cat, g_out, g_gate, g_up, g_down, small


HBM_SPEC = pl.BlockSpec(memory_space=pltpu.HBM)


def _place():
    x, y, c = lax.axis_index("x"), lax.axis_index("y"), lax.axis_index("c")
    chips = [(1 - x, y), (x, 1 - y), (1 - x, 1 - y)]
    return x, y, c, 2 * x + y, (x, y, 1 - c), chips, [2 * cx + cy for cx, cy in chips]


def _remote(src, dst, send_sem, recv_sem, to):
    return pltpu.make_async_remote_copy(src_ref=src, dst_ref=dst, send_sem=send_sem, recv_sem=recv_sem,
                                        device_id=to, device_id_type=MESH)


def _allgather_weights(shards, conv):
    n = len(shards)
    halves = [s.shape[1] // 2 for s in shards]
    per = 6
    own_base = n * per + 3

    def body(*refs):
        ins, conv_in = refs[:n], refs[n]
        outs, conv_out = refs[n + 1:2 * n + 1], refs[2 * n + 1]
        send_sems, recv_sems = refs[2 * n + 2:]
        x, y, c, own, sib, chips, chip_idx = _place()

        def half(i, ref, hc):
            return ref.at[:, pl.ds(pl.multiple_of(hc * halves[i], LANES), halves[i])]

        sent = []
        for i, (src, dst) in enumerate(zip(list(ins) + [conv_in], list(outs) + [conv_out])):
            k = own_base + i
            sent.append(_remote(src, dst.at[own], send_sems.at[k], recv_sems.at[k], sib))
        for i in range(n):
            for j, chip in enumerate(chips):
                k = i * per + j
                sent.append(_remote(half(i, ins[i], c), half(i, outs[i].at[own], c),
                                    send_sems.at[k], recv_sems.at[k], (*chip, c)))
        for j, chip in enumerate(chips):
            k = n * per + j
            sent.append(_remote(conv_in, conv_out.at[own], send_sems.at[k], recv_sems.at[k], (*chip, c)))
        for cp in sent:
            cp.start()
        for i in range(n):
            for j in range(len(chips)):
                k = i * per + j
                landed = half(i, outs[i].at[chip_idx[j]], c)
                _remote(landed, landed, send_sems.at[k], recv_sems.at[k], sib).wait_recv()
                fwd = _remote(landed, landed, send_sems.at[k + 3], recv_sems.at[k + 3], sib)
                fwd.start()
                sent.append(fwd)
        for i in range(n):
            for j in range(len(chips)):
                k = i * per + 3 + j
                landed = half(i, outs[i].at[chip_idx[j]], 1 - c)
                _remote(landed, landed, send_sems.at[k], recv_sems.at[k], sib).wait_recv()
        for j in range(len(chips)):
            k = n * per + j
            landed = conv_out.at[chip_idx[j]]
            _remote(landed, landed, send_sems.at[k], recv_sems.at[k], sib).wait_recv()
        for i, dst in enumerate(list(outs) + [conv_out]):
            k = own_base + i
            landed = dst.at[own]
            _remote(landed, landed, send_sems.at[k], recv_sems.at[k], sib).wait_recv()
        for cp in sent:
            cp.wait_send()

    n_sem = own_base + n + 1
    out_shape = [jax.ShapeDtypeStruct((N_CHIPS,) + s.shape, s.dtype) for s in shards]
    out_shape.append(jax.ShapeDtypeStruct((N_CHIPS,) + conv.shape, conv.dtype))
    res = pl.pallas_call(
        body, name="allgather_weights", out_shape=out_shape,
        in_specs=[HBM_SPEC] * (n + 1), out_specs=[HBM_SPEC] * (n + 1),
        scratch_shapes=[pltpu.SemaphoreType.DMA((n_sem,)), pltpu.SemaphoreType.DMA((n_sem,))],
    )(*shards, conv)
    return res[:n], res[n]


SEM_SPEC = pl.BlockSpec(memory_space=pltpu.SEMAPHORE)
ANY_SPEC = pl.BlockSpec(memory_space=pl.ANY)
DATAFLOW = pltpu.SideEffectType.DATAFLOW_SIDE_EFFECTING


def _gather_plan(srcs, lands):
    x, y, c, own, sib, chips, chip_idx = _place()
    plan = []
    for src, land in zip(srcs, lands):
        for j, chip in enumerate(chips):
            plan.append((src, land.at[own], (*chip, c), land.at[chip_idx[j]]))
        plan.append((src, land.at[own], sib, land.at[own]))
    return plan


def _exchange_plan(srcs, lands):
    x, y, c, own, sib, chips, chip_idx = _place()
    plan = []
    for src, land in zip(srcs, lands):
        for j, chip in enumerate(chips):
            plan.append((src.at[chip_idx[j]], land.at[j], (*chip, c), land.at[j]))
    return plan


def _in_proj_plan(srcs, lands):
    x, y, c, own, sib, chips, chip_idx = _place()
    (w, conv), (w_land, conv_land) = srcs, lands
    hw = w.shape[1] // 2
    half = lambda ref: ref.at[:, pl.ds(pl.multiple_of(c * hw, LANES), hw)]
    plan = []
    for j, chip in enumerate(chips):
        plan.append((half(w), half(w_land.at[own]), (*chip, c), half(w_land.at[chip_idx[j]])))
        plan.append((conv, conv_land.at[own], (*chip, c), conv_land.at[chip_idx[j]]))
    plan.append((w, w_land.at[own], sib, w_land.at[own]))
    plan.append((conv, conv_land.at[own], sib, conv_land.at[own]))
    return plan


def _forward_halves(landed):
    hw = landed.shape[2] // 2

    def body(in_ref, out_ref, send_sems, recv_sems):
        x, y, c, own, sib, chips, chip_idx = _place()
        half = lambda ref, hc: ref.at[:, pl.ds(pl.multiple_of(hc * hw, LANES), hw)]
        sent = [_remote(half(out_ref.at[chip_idx[j]], c), half(out_ref.at[chip_idx[j]], c),
                        send_sems.at[j], recv_sems.at[j], sib) for j in range(3)]
        for cp in sent:
            cp.start()
        for j in range(3):
            other = half(out_ref.at[chip_idx[j]], 1 - c)
            _remote(other, other, send_sems.at[j], recv_sems.at[j], sib).wait_recv()
        for cp in sent:
            cp.wait_send()

    return pl.pallas_call(
        body, name="gather_in_forward", out_shape=jax.ShapeDtypeStruct(landed.shape, landed.dtype),
        in_specs=[HBM_SPEC], out_specs=HBM_SPEC, input_output_aliases={0: 0},
        scratch_shapes=[pltpu.SemaphoreType.DMA((3,)), pltpu.SemaphoreType.DMA((3,))],
    )(landed)


def _split_start(name, plan_fn, srcs, land_shapes, n_copies, after):
    n = len(srcs)

    def body(*refs):
        src_refs, land_refs = refs[:n], refs[n:2 * n]
        send_sems, recv_sems = refs[2 * n + 1], refs[2 * n + 2]
        token = refs[-1]
        for k, (src, dst, to, _) in enumerate(plan_fn(src_refs, land_refs)):
            _remote(src, dst, send_sems.at[k], recv_sems.at[k], to).start()
        token[...] = jnp.zeros_like(token)

    lands = [pltpu.with_memory_space_constraint(lax.empty(s.shape, s.dtype), pltpu.HBM) for s in land_shapes]
    srcs = [pltpu.with_memory_space_constraint(s, pltpu.HBM) for s in srcs]
    out_shape = ([pltpu.SemaphoreType.DMA((n_copies,)), pltpu.SemaphoreType.DMA((n_copies,))]
                 + [pltpu.HBM(s.shape, s.dtype) for s in srcs] + [pltpu.HBM(s.shape, s.dtype) for s in land_shapes]
                 + [jax.ShapeDtypeStruct((8, LANES), F32)])
    res = pl.pallas_call(
        body, name=name, out_shape=out_shape,
        in_specs=[HBM_SPEC] * (2 * n) + [ANY_SPEC],
        out_specs=[SEM_SPEC, SEM_SPEC] + [HBM_SPEC] * (2 * n) + [pl.BlockSpec(memory_space=pltpu.VMEM)],
        input_output_aliases={i: 2 + i for i in range(2 * n)},
        compiler_params=pltpu.CompilerParams(has_side_effects=DATAFLOW),
    )(*srcs, *lands, after)
    return dict(sems=res[:2], srcs=res[2:2 + n], lands=res[2 + n:2 + 2 * n], token=res[-1], n=n)


def _split_wait(name, plan_fn, started, after):
    n = started["n"]

    def body(*refs):
        src_refs, land_refs = refs[:n], refs[n:2 * n]
        send_sems, recv_sems = refs[2 * n], refs[2 * n + 1]
        for k, (src, _, to, landed) in enumerate(plan_fn(src_refs, land_refs)):
            copy = _remote(src, landed, send_sems.at[k], recv_sems.at[k], to)
            copy.wait_send()
            copy.wait_recv()

    srcs, lands = started["srcs"], started["lands"]
    after = list(after) if isinstance(after, (list, tuple)) else [after]
    res = pl.pallas_call(
        body, name=name,
        out_shape=[pltpu.HBM(s.shape, s.dtype) for s in srcs] + [pltpu.HBM(s.shape, s.dtype) for s in lands],
        in_specs=[HBM_SPEC] * (2 * n) + [SEM_SPEC, SEM_SPEC] + [ANY_SPEC] * len(after),
        out_specs=[HBM_SPEC] * (2 * n),
        input_output_aliases={i: i for i in range(2 * n)},
        compiler_params=pltpu.CompilerParams(has_side_effects=DATAFLOW),
    )(*srcs, *lands, *started["sems"], *after)
    return res[n:]


def _swap_halves(stacks, name):
    n = len(stacks)

    def body(*refs):
        ins, outs = refs[:n], refs[n:2 * n]
        send_sems, recv_sems = refs[2 * n:]
        x, y, c, own, sib, chips, chip_idx = _place()
        cps = []
        for i in range(n):
            h = stacks[i].shape[2] // 2
            src = ins[i].at[:, :, pl.ds(pl.multiple_of((1 - c) * h, LANES), h)]
            cps.append(_remote(src, outs[i], send_sems.at[i], recv_sems.at[i], sib))
        for cp in cps:
            cp.start()
        for cp in cps:
            cp.wait()

    out_shape = [jax.ShapeDtypeStruct((N_CHIPS, s.shape[1], s.shape[2] // 2), s.dtype) for s in stacks]
    return pl.pallas_call(
        body, name=name, out_shape=out_shape,
        in_specs=[HBM_SPEC] * n, out_specs=[HBM_SPEC] * n,
        scratch_shapes=[pltpu.SemaphoreType.DMA((n,)), pltpu.SemaphoreType.DMA((n,))],
    )(*stacks)


def _add_half(stack, landed, place, name):
    _, rows, h = landed.shape

    def body(place_ref, a_ref, b_ref, o_ref, own_ref):
        part = (a_ref[...].astype(F32) + b_ref[...].astype(F32)).astype(o_ref.dtype)
        o_ref[...] = part

        @pl.when(pl.program_id(0) == place_ref[1])
        def _():
            own_ref[...] = part[0]

    return pl.pallas_call(
        body, name=name,
        out_shape=[jax.ShapeDtypeStruct(landed.shape, BF16), jax.ShapeDtypeStruct((rows, h), BF16)],
        grid_spec=pltpu.PrefetchScalarGridSpec(
            num_scalar_prefetch=1, grid=(N_CHIPS,),
            in_specs=[pl.BlockSpec((1, rows, h), lambda j, p: (j, 0, p[0])),
                      pl.BlockSpec((1, rows, h), lambda j, p: (j, 0, 0))],
            out_specs=[pl.BlockSpec((1, rows, h), lambda j, p: (j, 0, 0)),
                       pl.BlockSpec((rows, h), lambda j, p: (0, 0))]),
        compiler_params=_params(("arbitrary",)),
    )(place, stack, landed)


def _exchange_partials(parts):
    n = len(parts)

    def body(*refs):
        ins, outs = refs[:n], refs[n:2 * n]
        send_sems, recv_sems = refs[2 * n:]
        x, y, c, own, sib, chips, chip_idx = _place()
        sent = []
        for i in range(n):
            for j, chip in enumerate(chips):
                k = i * 3 + j
                sent.append(_remote(ins[i].at[chip_idx[j]], outs[i].at[j], send_sems.at[k], recv_sems.at[k],
                                    (*chip, c)))
        for cp in sent:
            cp.start()
        for i in range(n):
            for j in range(len(chips)):
                k = i * 3 + j
                landed = outs[i].at[j]
                _remote(landed, landed, send_sems.at[k], recv_sems.at[k], sib).wait_recv()
        for cp in sent:
            cp.wait_send()

    return pl.pallas_call(
        body, name="rs_exchange_partials",
        out_shape=[jax.ShapeDtypeStruct((3,) + p.shape[1:], p.dtype) for p in parts],
        in_specs=[HBM_SPEC] * n, out_specs=[HBM_SPEC] * n,
        scratch_shapes=[pltpu.SemaphoreType.DMA((3 * n,)), pltpu.SemaphoreType.DMA((3 * n,))],
    )(*parts)


def _sum_partials(own_part, landed, name, untiled_rows=False):
    _, h, cols = landed.shape
    tc = LANES if untiled_rows else cols

    def body(own_ref, a_ref, o_ref):
        acc = own_ref[...].astype(F32)
        for s in range(3):
            acc = acc + a_ref[s].astype(F32)
        if untiled_rows:
            o_ref[:, 0, :] = acc
        else:
            o_ref[...] = acc

    if untiled_rows:
        out_shape, out_spec = jax.ShapeDtypeStruct((h, 1, cols), F32), pl.BlockSpec((h, 1, tc), lambda i: (0, 0, i))
    else:
        out_shape, out_spec = jax.ShapeDtypeStruct((h, cols), F32), pl.BlockSpec((h, tc), lambda i: (0, i))
    return pl.pallas_call(
        body, name=name, out_shape=out_shape, grid=(cols // tc,),
        in_specs=[pl.BlockSpec((h, tc), lambda i: (0, i)), pl.BlockSpec((3, h, tc), lambda i: (0, 0, i))],
        out_specs=out_spec, compiler_params=_params(("arbitrary",)),
    )(own_part, landed)


def _share_halves(halves, name):
    n = len(halves)

    def body(*refs):
        ins, outs = refs[:n], refs[n:2 * n]
        send_sems, recv_sems = refs[2 * n:]
        x, y, c, own, sib, chips, chip_idx = _place()
        cps = [_remote(ins[i], outs[i], send_sems.at[i], recv_sems.at[i], sib) for i in range(n)]
        for cp in cps:
            cp.start()
        for cp in cps:
            cp.wait()

    return pl.pallas_call(
        body, name=name,
        out_shape=[jax.ShapeDtypeStruct(p.shape, p.dtype) for p in halves],
        in_specs=[HBM_SPEC] * n, out_specs=[HBM_SPEC] * n,
        scratch_shapes=[pltpu.SemaphoreType.DMA((n,)), pltpu.SemaphoreType.DMA((n,))],
    )(*halves)


def _allreduce_small(packed):
    rows = packed.shape[0]
    n_dev = 8

    def body(in_ref, out_ref, gath, send_sems, recv_sems):
        x, y, c = lax.axis_index("x"), lax.axis_index("y"), lax.axis_index("c")
        me = 4 * x + 2 * y + c
        gath[me] = in_ref[...]
        cps = []
        for k in range(1, n_dev):
            fx, fy, fc = (k >> 2) & 1, (k >> 1) & 1, k & 1
            to = (x ^ fx, y ^ fy, c ^ fc)
            cps.append(_remote(in_ref, gath.at[me], send_sems.at[k - 1], recv_sems.at[k - 1], to))
        for cp in cps:
            cp.start()
        for k in range(1, n_dev):
            fx, fy, fc = (k >> 2) & 1, (k >> 1) & 1, k & 1
            src = 4 * (x ^ fx) + 2 * (y ^ fy) + (c ^ fc)
            slot = gath.at[src]
            _remote(slot, slot, send_sems.at[k - 1], recv_sems.at[k - 1], (x, y, c)).wait_recv()
        for cp in cps:
            cp.wait_send()
        acc = gath[0]
        for d in range(1, n_dev):
            acc = acc + gath[d]
        out_ref[...] = acc

    vm = pl.BlockSpec(memory_space=pltpu.VMEM)
    return pl.pallas_call(
        body, name="allreduce_small", out_shape=jax.ShapeDtypeStruct(packed.shape, F32),
        in_specs=[vm], out_specs=vm,
        scratch_shapes=[pltpu.VMEM((n_dev, rows, LANES), F32),
                        pltpu.SemaphoreType.DMA((n_dev - 1,)), pltpu.SemaphoreType.DMA((n_dev - 1,))],
    )(packed)


def _adam(col, w, g, m, v):
    m2 = ADAM_B1 * m + (1.0 - ADAM_B1) * g
    v2 = ADAM_B2 * v + (1.0 - ADAM_B2) * (g * g)
    m_hat = m2 / (1.0 - ADAM_B1 ** ADAM_STEP)
    v_hat = v2 / (1.0 - ADAM_B2 ** ADAM_STEP)
    delta = -ADAM_LR * (m_hat / (jnp.sqrt(v_hat) + ADAM_EPS) + ADAM_WD * w)
    return delta, m2, v2


def _adam_call(w, g, m, v, name):
    rows, cols = w.shape
    tm = rows
    for cand in (256, 352, 176, 128, 64, 48, 16, 8):
        if rows % cand == 0:
            tm = cand
            break
    return _tiles(_adam, name=name, rows=rows, tm=tm,
                  row_ins=[(w, cols, 0), (g, cols, 0), (m, cols, 0), (v, cols, 0)],
                  row_outs=[(cols, F32)] * 3)


def _adam_big(w, g_mine, g_other, m, v, place, name):
    rows, cols = w.shape
    tc = 256
    nt = cols // 2 // tc

    def body(place_ref, w_ref, gm_ref, go_ref, m_ref, v_ref, g_out, d_out, m_out, v_out):
        g = jnp.where(pl.program_id(0) == place_ref[0], gm_ref[...], go_ref[...])
        d, m2, v2 = _adam(None, w_ref[...], g, m_ref[...], v_ref[...])
        g_out[...] = g
        d_out[...] = d
        m_out[...] = m2
        v_out[...] = v2

    full = pl.BlockSpec((rows, tc), lambda hh, i, p: (0, hh * nt + i))
    half = pl.BlockSpec((rows, tc), lambda hh, i, p: (0, i))
    return pl.pallas_call(
        body, name=name, out_shape=[jax.ShapeDtypeStruct(w.shape, F32)] * 4,
        grid_spec=pltpu.PrefetchScalarGridSpec(
            num_scalar_prefetch=1, grid=(2, nt),
            in_specs=[full, half, half, full, full], out_specs=[full] * 4),
        compiler_params=_params(("arbitrary", "arbitrary")),
    )(place, w, g_mine, g_other, m, v)


def _adam_untiled_rows(w, g_mine, g_other, m, v, place, name):
    rows, _, cols = w.shape
    tc = 256
    nt = cols // 2 // tc
    rb = next(r for r in (206, 128, 103, rows) if rows % r == 0)

    def body(place_ref, w_ref, gm_ref, go_ref, m_ref, v_ref, g_out, d_out, m_out, v_out):
        g = jnp.where(pl.program_id(0) == place_ref[0], gm_ref[...], go_ref[...])
        d, m2, v2 = _adam(None, w_ref[...], g, m_ref[...], v_ref[...])
        g_out[...] = g
        d_out[...] = d
        m_out[...] = m2
        v_out[...] = v2

    full = pl.BlockSpec((rb, 1, tc), lambda hh, i, r, p: (r, 0, hh * nt + i))
    half = pl.BlockSpec((rb, 1, tc), lambda hh, i, r, p: (r, 0, i))
    return pl.pallas_call(
        body, name=name, out_shape=[jax.ShapeDtypeStruct(w.shape, F32)] * 4,
        grid_spec=pltpu.PrefetchScalarGridSpec(
            num_scalar_prefetch=1, grid=(2, nt, rows // rb),
            in_specs=[full, half, half, full, full], out_specs=[full] * 4),
        compiler_params=_params(("arbitrary", "arbitrary", "arbitrary")),
    )(place, w, g_mine, g_other, m, v)


def _pack(arrays, zero=None):
    flat = []
    for a in arrays:
        a = a.reshape(-1).astype(F32)
        if zero is not None:
            a = a + zero
        flat.append(jnp.pad(a, (0, (-a.size) % LANES)))
    out = jnp.concatenate(flat)
    out = jnp.pad(out, (0, (-out.size) % (8 * LANES)))
    return out.reshape(-1, LANES)


def _unpack(packed, shapes):
    flat = packed.reshape(-1)
    out, off = [], 0
    for s in shapes:
        size = int(np.prod(s))
        out.append(flat[off:off + size].reshape(s))
        off += size + (-size) % LANES
    return out


def kernel(x, norm1_w, w_in, gdn_conv_w, gdn_A_log, gdn_dt_bias, gdn_out_norm_w, fox_f_bias, fox_q_norm_w, fox_k_norm_w, w_out, norm2_w, w_ffn_gate, w_ffn_up, w_ffn_down, final_norm_w, loss_target, m_norm1_w, m_w_in, m_gdn_conv_w, m_gdn_A_log, m_gdn_dt_bias, m_gdn_out_norm_w, m_fox_f_bias, m_fox_q_norm_w, m_fox_k_norm_w, m_w_out, m_norm2_w, m_w_ffn_gate, m_w_ffn_up, m_w_ffn_down, m_final_norm_w, v_norm1_w, v_w_in, v_gdn_conv_w, v_gdn_A_log, v_gdn_dt_bias, v_gdn_out_norm_w, v_fox_f_bias, v_fox_q_norm_w, v_fox_k_norm_w, v_w_out, v_norm2_w, v_w_ffn_gate, v_w_ffn_up, v_w_ffn_down, v_final_norm_w):
    cx, cy, cc = lax.axis_index("x"), lax.axis_index("y"), lax.axis_index("c")
    own = 2 * cx + cy
    place = jnp.stack([cc, own]).astype(jnp.int32)

    names = ["w_in", "w_out", "w_gate", "w_up", "w_down"]
    is_t = [True, False, True, True, False]
    to_t = lambda a, t: a[0].T if t else a[0]
    from_t = lambda a, t: (a.T if t else a)[None]
    big_w = [to_t(a, t) for a, t in zip([w_in, w_out, w_ffn_gate, w_ffn_up, w_ffn_down], is_t)]
    big_m = [to_t(a, t) for a, t in zip([m_w_in, m_w_out, m_w_ffn_gate, m_w_ffn_up, m_w_ffn_down], is_t)]
    big_v = [to_t(a, t) for a, t in zip([v_w_in, v_w_out, v_w_ffn_gate, v_w_ffn_up, v_w_ffn_down], is_t)]
    shards = [big_w[0].astype(BF16)]
    small_w = [norm1_w, gdn_conv_w, gdn_A_log, gdn_dt_bias, gdn_out_norm_w, fox_f_bias, fox_q_norm_w,
               fox_k_norm_w, norm2_w, final_norm_w]
    small_m = [m_norm1_w, m_gdn_conv_w, m_gdn_A_log, m_gdn_dt_bias, m_gdn_out_norm_w, m_fox_f_bias,
               m_fox_q_norm_w, m_fox_k_norm_w, m_norm2_w, m_final_norm_w]
    small_v = [v_norm1_w, v_gdn_conv_w, v_gdn_A_log, v_gdn_dt_bias, v_gdn_out_norm_w, v_fox_f_bias,
               v_fox_q_norm_w, v_fox_k_norm_w, v_norm2_w, v_final_norm_w]
    first = _split_start("gather_in_start", _in_proj_plan, [shards[0], gdn_conv_w[0]],
                         [jax.ShapeDtypeStruct((N_CHIPS,) + shards[0].shape, BF16),
                          jax.ShapeDtypeStruct((N_CHIPS, CONV_K, 3 * WIDTH // N_CHIPS), F32)],
                         n_copies=8, after=shards[0])
    small_packed = [_pack(p, first["token"][0, 0]) for p in (small_w, small_m, small_v)]
    shards += [(w + first["token"][0, 0]).astype(BF16) for w in big_w[1:]]
    rest = {}

    def first_weights(after):
        w_in_g, conv_g = _split_wait("gather_in_wait", _in_proj_plan, first, [after] + small_packed)
        w_in_g = _forward_halves(w_in_g)
        rest.update(_split_start("gather_rest_start", _gather_plan, shards[1:],
                                 [jax.ShapeDtypeStruct((N_CHIPS,) + s.shape, BF16) for s in shards[1:]],
                                 n_copies=4 * len(shards[1:]), after=w_in_g))
        w_cat = _cat_weights(w_in_g.reshape(D_IN, D_MODEL))
        return w_cat + rest["token"][0, 0].astype(BF16), conv_g.transpose(1, 0, 2).reshape(CONV_K, 3 * WIDTH)

    def late_weights(after):
        w_out_g, w_gate_g, w_up_g, w_down_g = _split_wait("gather_rest_wait", _gather_plan, rest, after)
        return w_out_g.reshape(D_MODEL, D_MODEL), w_gate_g, w_up_g, w_down_g

    def start_reduction(stacks, nms, tag):
        landed = _swap_halves(stacks, "rs_swap_" + tag)
        added = [_add_half(s, l, place, "rs_add_" + nm) for s, l, nm in zip(stacks, landed, nms)]
        parts = [a[0] for a in added]
        started = _split_start("exchange_" + tag + "_start", _exchange_plan, parts,
                               [jax.ShapeDtypeStruct((3,) + p.shape[1:], p.dtype) for p in parts],
                               n_copies=3 * len(parts), after=parts[0])
        return dict(own=[a[1] for a in added], started=started, tag=tag, names=nms)

    def finish_reduction(red, after, updates):
        landed = _split_wait("exchange_" + red["tag"] + "_wait", _exchange_plan, red["started"], after)
        halves = [_sum_partials(o, p, "rs_sum_" + nm, untiled_rows=nm == "w_in")
                  for o, p, nm in zip(red["own"], landed, red["names"])]
        others = _share_halves(halves, "rs_share_" + red["tag"])
        return [upd(gm, go) for upd, gm, go in zip(updates, halves, others)]

    def transport_update(b):
        def upd(gm, go):
            res = _adam_big(big_w[b], gm, go, big_m[b], big_v[b], place, "adam_" + names[b])
            early_done.append(res[1])
            return [from_t(a, is_t[b]) for a in res]
        return upd

    early_done = []

    def w_in_update(gm, go):
        rows3 = lambda a: jnp.transpose(a, (2, 0, 1))
        res = _adam_untiled_rows(rows3(w_in), gm, go, rows3(m_w_in), rows3(v_w_in), place, "adam_w_in")
        return [jnp.transpose(a, (1, 2, 0)) for a in res]

    early = {}

    def early_grads_ready(g_out, g_gate, g_up, g_down):
        stacks = [g_out.reshape(N_CHIPS, D_MODEL // N_CHIPS, D_MODEL), g_gate, g_up, g_down]
        early.update(start_reduction(stacks, names[1:], "early"))
        return early["started"]["token"][0, 0]

    grad_x, g_cat, _, _, _, _, small = _local_step(
        x[0], loss_target[0], norm1_w + first["token"][0, 0], gdn_A_log[0], gdn_dt_bias[0],
        gdn_out_norm_w[0], fox_f_bias[0], fox_q_norm_w[0], fox_k_norm_w[0], norm2_w, final_norm_w.reshape(1, -1),
        first_weights, late_weights, early_grads_ready)

    late = start_reduction([_uncat_grad(g_cat).reshape(N_CHIPS, D_IN // N_CHIPS, D_MODEL)], names[:1], "w_in")
    big_upd = finish_reduction(early, late["started"]["token"], [transport_update(b) for b in range(1, 5)])

    order = ["norm1_w", "conv_w", "a_log", "dt_bias", "out_norm_w", "f_bias", "q_norm_w", "k_norm_w",
             "norm2_w", "final_w"]
    red = _allreduce_small(_pack([small[k] for k in order] + [small["loss"]]))
    red_shapes = [(1, D_MODEL), (CONV_K, 3 * WIDTH), (1, HEADS), (1, HEADS), (1, HEAD_DIM), (1, HEADS),
                  (1, HEAD_DIM), (1, HEAD_DIM), (1, D_MODEL), (D_MODEL,), ()]
    red_list = _unpack(red, red_shapes)
    loss = red_list[-1]
    small_g = dict(zip(order, red_list[:-1]))
    shard_cols = 3 * WIDTH // N_CHIPS
    small_g["conv_w"] = lax.dynamic_slice_in_dim(small_g["conv_w"], own * shard_cols, shard_cols, axis=1)[None]
    small_gl = [small_g[k].reshape(w.shape) for k, w in zip(order, small_w)]
    s_delta, s_m, s_v = _adam_call(small_packed[0], _pack(small_gl), small_packed[1], small_packed[2], "adam_small")
    big_upd = finish_reduction(late, [s_delta] + early_done, [w_in_update]) + big_upd
    shapes = [w.shape for w in small_w]
    s_delta, s_m, s_v = _unpack(s_delta, shapes), _unpack(s_m, shapes), _unpack(s_v, shapes)

    big_pos = {1: 0, 9: 1, 11: 2, 12: 3, 13: 4}
    small_pos = {0: 0, 2: 1, 3: 2, 4: 3, 5: 4, 6: 5, 7: 6, 8: 7, 10: 8, 14: 9}
    grads, deltas, new_m, new_v = [], [], [], []
    for pos in range(15):
        if pos in big_pos:
            b = big_pos[pos]
            g, d, m2, v2 = big_upd[b]
            grads.append(g)
            deltas.append(d)
            new_m.append(m2)
            new_v.append(v2)
        else:
            s = small_pos[pos]
            grads.append(small_gl[s])
            deltas.append(s_delta[s])
            new_m.append(s_m[s])
            new_v.append(s_v[s])
    return (loss, grad_x[None], *grads, *deltas, *new_m, *new_v)
```

```python
import jax
import jax.numpy as jnp
import numpy as np
from jax import lax
from jax.experimental import pallas as pl
from jax.experimental.pallas import tpu as pltpu

F32 = jnp.float32
BF16 = jnp.bfloat16

D_MODEL = 1024
HEADS = 8
HEAD_DIM = 64
PAIRS = HEADS // 2
WIDTH = HEADS * HEAD_DIM
CHUNK = 64
CONV_K = 4
D_FF = 2816
FF_SHARD = D_FF // 4
EPS = 1e-6
SCALE = HEAD_DIM ** -0.5
LANES = 128
N_CHIPS = 4
D_IN = 4120
D_CAT = 4224
COL_SMALL = 4096 // LANES

ADAM_LR = 0.001
ADAM_B1 = 0.9
ADAM_B2 = 0.999
ADAM_EPS = 1e-08
ADAM_WD = 0.01
ADAM_STEP = 10

VMEM_LIMIT = 56 * 1024 * 1024
MESH = pl.DeviceIdType.MESH
HIGHEST = lax.Precision.HIGHEST


def _params(sem):
    return pltpu.CompilerParams(dimension_semantics=sem, vmem_limit_bytes=VMEM_LIMIT)


_CONTRACT = {"nn": ((1,), (0,)), "nt": ((1,), (1,)), "tn": ((0,), (0,))}


def _mm(a, b, *, dims, name, out_dtype=F32, add=None, tm=1024, tn=512, tk=512):
    if dims == "nn":
        (m, k), (k2, n) = a.shape, b.shape
    elif dims == "nt":
        (m, k), (n, k2) = a.shape, b.shape
    else:
        (k, m), (k2, n) = a.shape, b.shape
    assert k == k2, (a.shape, b.shape, dims)
    tm, tn, tk = min(tm, m), min(tn, n), min(tk, k)
    assert m % tm == 0 and n % tn == 0 and k % tk == 0, (m, n, k, tm, tn, tk)
    nk = k // tk
    a_spec = (pl.BlockSpec((tk, tm), lambda i, j, kk: (kk, i)) if dims == "tn"
              else pl.BlockSpec((tm, tk), lambda i, j, kk: (i, kk)))
    b_spec = (pl.BlockSpec((tn, tk), lambda i, j, kk: (j, kk)) if dims == "nt"
              else pl.BlockSpec((tk, tn), lambda i, j, kk: (kk, j)))
    o_spec = pl.BlockSpec((tm, tn), lambda i, j, kk: (i, j))
    contract = (_CONTRACT[dims], ((), ()))
    has_add = add is not None

    def body(*refs):
        a_ref, b_ref = refs[:2]
        add_ref = refs[2] if has_add else None
        o_ref = refs[3] if has_add else refs[2]
        part = lax.dot_general(a_ref[...].astype(BF16), b_ref[...].astype(BF16), contract,
                               preferred_element_type=F32)

        def finish(r):
            if has_add:
                r = r + add_ref[...].astype(F32)
            o_ref[...] = r.astype(out_dtype)

        if nk == 1:
            finish(part)
            return
        acc = refs[-1]
        kk = pl.program_id(2)

        @pl.when(kk == 0)
        def _():
            acc[...] = part

        @pl.when(kk > 0)
        def _():
            acc[...] += part

        @pl.when(kk == nk - 1)
        def _():
            finish(acc[...])

    ins = [a, b] + ([add] if has_add else [])
    in_specs = [a_spec, b_spec] + ([o_spec] if has_add else [])
    return pl.pallas_call(
        body, name=name, grid=(m // tm, n // tn, nk),
        in_specs=in_specs, out_specs=o_spec,
        out_shape=jax.ShapeDtypeStruct((m, n), out_dtype),
        scratch_shapes=[pltpu.VMEM((tm, tn), F32)] if nk > 1 else [],
        compiler_params=_params(("parallel", "parallel", "arbitrary")),
    )(*ins)


def _mm_blocks(a, b, *, name, grid, a_spec, b_spec, o_spec, out_shape, dims, n_sum=0, add=None, add_spec=None,
               epilogue=None, extra=()):
    contract = (_CONTRACT[dims], ((), ()))
    has_add = add is not None
    n_in = 2 + has_add + len(extra)

    def body(*refs):
        a_ref, b_ref = refs[:2]
        dot = lambda x, y: lax.dot_general(x.astype(BF16), y.astype(BF16), contract, preferred_element_type=F32)
        if n_sum:
            r = dot(a_ref[0], b_ref[0])
            for s in range(1, n_sum):
                r = r + dot(a_ref[s], b_ref[s])
        else:
            r = dot(a_ref[...], b_ref[...])
        if has_add:
            r = r + refs[2][...].astype(F32)
        if epilogue is None:
            refs[-1][...] = r.astype(refs[-1].dtype)
        else:
            outs = epilogue(r, *[e[...] for e in refs[2 + has_add:n_in]])
            for o_ref, val in zip(refs[n_in:], outs):
                o_ref[...] = val.astype(o_ref.dtype)

    ins = [a, b] + ([add] if has_add else []) + [e[0] for e in extra]
    in_specs = [a_spec, b_spec] + ([add_spec] if has_add else []) + [e[1] for e in extra]
    return pl.pallas_call(
        body, name=name, grid=grid, in_specs=in_specs, out_specs=o_spec, out_shape=out_shape,
        compiler_params=_params(("parallel",) * len(grid)),
    )(*ins)


def _tiles(fn, *, name, rows, tm, ncol=1, row_ins=(), col_consts=(), full_consts=(),
           row_outs=(), acc_outs=()):
    nt = rows // tm
    assert rows % tm == 0
    n_full, n_col, n_row = len(full_consts), len(col_consts), len(row_ins)
    n_ro, n_acc = len(row_outs), len(acc_outs)

    def body(*refs):
        ins = refs[:n_full + n_col + n_row]
        outs = refs[n_full + n_col + n_row:]
        i = pl.program_id(1)
        res = fn(pl.program_id(0), *[r[...] for r in ins])
        for r, v in zip(outs[:n_ro], res[:n_ro]):
            r[...] = v.astype(r.dtype)
        if n_acc:
            @pl.when(i == 0)
            def _():
                for r in outs[n_ro:]:
                    r[...] = jnp.zeros_like(r)
            for r, v in zip(outs[n_ro:], res[n_ro:]):
                r[...] += v

    in_specs = [pl.BlockSpec(a.shape, lambda j, i, nd=a.ndim: (0,) * nd) for a in full_consts]
    in_specs += [pl.BlockSpec((nr, w), lambda j, i, o=o: (0, o + j)) for (_, nr, w, o) in col_consts]
    in_specs += [pl.BlockSpec((tm, w), lambda j, i, o=o: (i, o + j)) for (_, w, o) in row_ins]
    out_specs = [pl.BlockSpec((tm, w), lambda j, i: (i, j)) for (w, _) in row_outs]
    out_specs += [pl.BlockSpec((nr, w), lambda j, i: (0, j)) for (nr, w) in acc_outs]
    out_shape = [jax.ShapeDtypeStruct((rows, w * ncol), dt) for (w, dt) in row_outs]
    out_shape += [jax.ShapeDtypeStruct((nr, w * ncol), F32) for (nr, w) in acc_outs]
    args = list(full_consts) + [c[0] for c in col_consts] + [r[0] for r in row_ins]
    out = pl.pallas_call(
        body, name=name, grid=(ncol, nt), in_specs=in_specs, out_specs=out_specs, out_shape=out_shape,
        compiler_params=_params(("parallel", "arbitrary")),
    )(*args)
    return out


def _rms(x, w):
    return x * lax.rsqrt(jnp.mean(x * x, axis=-1, keepdims=True) + EPS) * w


def _lane_lo(shape):
    return lax.broadcasted_iota(jnp.int32, shape, len(shape) - 1) < HEAD_DIM


def _pair_sum(x):
    lo = _lane_lo(x.shape)
    s0 = jnp.sum(jnp.where(lo, x, 0.0), axis=-1, keepdims=True)
    s1 = jnp.sum(jnp.where(lo, 0.0, x), axis=-1, keepdims=True)
    return jnp.where(lo, s0, s1)


def _head_col(x, lo, h):
    keep = lo if h == 0 else jnp.logical_not(lo)
    return jnp.max(jnp.where(keep, x, -jnp.inf), axis=-1, keepdims=True)


def _softplus(x):
    return jnp.maximum(x, 0.0) + jnp.log1p(jnp.exp(-jnp.abs(x)))


def _silu(x):
    return x * jax.nn.sigmoid(x)


def _dot(a, b, contract):
    return lax.dot_general(a.astype(BF16), b.astype(BF16), (contract, ((), ())),
                           preferred_element_type=F32)


def _dot32(a, b, contract):
    return lax.dot_general(a, b, (contract, ((), ())), precision=HIGHEST, preferred_element_type=F32)


def _bd(y):
    yy = jnp.concatenate([y, y], axis=0)
    r = lax.broadcasted_iota(jnp.int32, yy.shape, 0) < HEAD_DIM
    c = lax.broadcasted_iota(jnp.int32, yy.shape, 1) < HEAD_DIM
    return jnp.where(r == c, yy, 0.0)


def _pp(x, y):
    return _dot(x, _bd(y), _CONTRACT["nn"])


def _pp_nt(x, y):
    return _dot(x, _bd(y), _CONTRACT["nt"])


def _pp_tn(x, y):
    full = _dot(x, y, _CONTRACT["tn"])
    return jnp.where(_lane_lo((HEAD_DIM, LANES)), full[:HEAD_DIM], full[HEAD_DIM:])


def _gdn_masks():
    row = lax.broadcasted_iota(jnp.int32, (CHUNK, LANES), 0)
    col = lax.broadcasted_iota(jnp.int32, (CHUNK, LANES), 1) % HEAD_DIM
    return row, col


def _interleave(chains):
    live = list(chains)
    while live:
        for g in list(live):
            try:
                next(g)
            except StopIteration:
                live.remove(g)


def _gdn_forward(qkv, betax, gcx, grow, rows):
    nchunk = rows // CHUNK

    def body(q_ref, k_ref, v_ref, bx_ref, gx_ref, gr_ref, o_ref, ss_ref, ts_ref, state):
        n = pl.program_id(0)

        @pl.when(n == 0)
        def _():
            state[...] = jnp.zeros_like(state)

        row, col = _gdn_masks()
        incl, strict = col <= row, col < row

        def chain(p):
            lanes = pl.ds(p * LANES, LANES)
            q, k, v, bx, gx = q_ref[:, lanes], k_ref[:, lanes], v_ref[:, lanes], bx_ref[:, lanes], gx_ref[:, lanes]
            gr = gr_ref[0, p]
            glast = gx_ref[pl.ds(CHUNK - 1, 1), lanes]
            s = state[p]
            dm = jnp.where(incl, jnp.exp(jnp.minimum(gx - gr, 0.0)), 0.0)
            kb, vb, eg, qs = k * bx, v * bx, jnp.exp(gx), q * SCALE
            yield
            big_g, big_p = _pp_nt(kb, k), _pp_nt(qs, k)
            yield
            x = -jnp.where(strict, big_g * dm, 0.0)
            att = jnp.where(incl, big_p * dm, 0.0)
            tm = jnp.where(row == col, 1.0, 0.0) + x
            x = _pp(x, x)
            yield
            for _ in range(4):
                step, x = _pp(tm, x), _pp(x, x)
                yield
                tm = tm + step
            tm = tm + _pp(tm, x)
            yield
            u, w = _pp(tm, vb), _pp(tm, kb * eg)
            yield
            ws, qgs = _pp(w, s), _pp(qs * eg, s)
            yield
            vn = u - ws
            kd = k * jnp.exp(glast - gx)
            avn, upd = _pp(att, vn), _pp_tn(kd, vn)
            yield
            ss_ref[0, p] = s
            ts_ref[0, p] = tm
            o_ref[:, lanes] = qgs + avn
            state[p] = s * jnp.exp(glast) + upd

        _interleave([chain(p) for p in range(PAIRS)])

    blk = lambda j: pl.BlockSpec((CHUNK, WIDTH), lambda n, j=j: (n, j))
    sv = pl.BlockSpec((1, PAIRS, CHUNK, LANES), lambda n: (n, 0, 0, 0))
    return pl.pallas_call(
        body, name="gdn_fwd", grid=(nchunk,),
        in_specs=[blk(0), blk(1), blk(2), blk(0), blk(0),
                  pl.BlockSpec((1, PAIRS, 1, LANES), lambda n: (n, 0, 0, 0))],
        out_specs=[blk(0), sv, sv],
        out_shape=[jax.ShapeDtypeStruct((rows, WIDTH), F32),
                   jax.ShapeDtypeStruct((nchunk, PAIRS, CHUNK, LANES), F32),
                   jax.ShapeDtypeStruct((nchunk, PAIRS, CHUNK, LANES), F32)],
        scratch_shapes=[pltpu.VMEM((PAIRS, CHUNK, LANES), F32)],
        compiler_params=_params(("arbitrary",)),
    )(qkv, qkv, qkv, betax, gcx, grow)


def _gdn_backward(qkv, betax, gcx, grow, ssave, tsave, do, rows):
    nchunk = rows // CHUNK

    def body(q_ref, k_ref, v_ref, bx_ref, gx_ref, gr_ref, ss_ref, ts_ref, do_ref,
             dq_ref, dk_ref, dv_ref, dbx_ref, dgx_ref, dgr_ref, dstate):
        n = pl.program_id(0)

        @pl.when(n == 0)
        def _():
            dstate[...] = jnp.zeros_like(dstate)

        row, col = _gdn_masks()
        incl, strict = col <= row, col < row

        def chain(p):
            lanes = pl.ds(p * LANES, LANES)
            q, k, v, bx, gx = q_ref[:, lanes], k_ref[:, lanes], v_ref[:, lanes], bx_ref[:, lanes], gx_ref[:, lanes]
            gr = gr_ref[0, p]
            glast = gx_ref[pl.ds(CHUNK - 1, 1), lanes]
            s, tm, d_o = ss_ref[0, p], ts_ref[0, p], do_ref[:, lanes]
            ds_out = dstate[p]
            dm = jnp.where(incl, jnp.exp(jnp.minimum(gx - gr, 0.0)), 0.0)
            kb, vb, eg, qs = k * bx, v * bx, jnp.exp(gx), q * SCALE
            kbg, qg = kb * eg, qs * eg
            ed = jnp.exp(glast - gx)
            kd = k * ed
            eglast = jnp.exp(glast)
            yield
            big_g, big_p = _pp_nt(kb, k), _pp_nt(qs, k)
            u, w = _pp(tm, vb), _pp(tm, kbg)
            dqg, kds = _pp_nt(d_o, s), _pp(kd, ds_out)
            yield
            low = jnp.where(strict, big_g * dm, 0.0)
            att = jnp.where(incl, big_p * dm, 0.0)
            ws, atd = _pp(w, s), _pp_tn(att, d_o)
            yield
            vn = u - ws
            dvn = kds + atd
            dkd, datt_raw = _pp_nt(vn, ds_out), _pp_nt(d_o, vn)
            dw_neg, dvb = _pp_nt(dvn, s), _pp_tn(tm, dvn)
            dtm_a, wdv = _pp_nt(dvn, vb), _pp_tn(w, dvn)
            qgd = _pp_tn(qg, d_o)
            yield
            datt = jnp.where(incl, datt_raw, 0.0)
            dw = -dw_neg
            dtm_b, dkbg = _pp_nt(dw, kbg), _pp_tn(tm, dw)
            dbig_p = datt * dm
            dqs_a, dk_p = _pp(dbig_p, k), _pp_tn(dbig_p, qs)
            yield
            inner = _pp_tn(tm, dtm_a + dtm_b)
            yield
            dlow = jnp.where(strict, -_pp_nt(inner, tm), 0.0)
            yield
            dbig_g = dlow * dm
            dkb_a, dk_g = _pp(dbig_g, k), _pp_tn(dbig_g, kb)
            yield
            dkb = dkb_a + dkbg * eg
            dqs = dqs_a + dqg * eg
            dk = dk_g + dk_p + dkd * ed + dkb * bx
            z = dlow * low + datt * att
            kdterm = dkd * kd
            dglast = (jnp.sum(ds_out * s, axis=0, keepdims=True) * eglast
                      + jnp.sum(kdterm, axis=0, keepdims=True))
            dgx = dqg * qg + dkbg * kbg - kdterm
            dgx = dgx + jnp.where(col == 0, _pair_sum(z), 0.0)
            dgx = dgx + jnp.where(row == CHUNK - 1, dglast, 0.0)
            dq_ref[:, lanes] = dqs * SCALE
            dk_ref[:, lanes] = dk
            dv_ref[:, lanes] = dvb * bx
            dbx_ref[:, lanes] = dkb * k + dvb * v
            dgx_ref[:, lanes] = dgx
            dgr_ref[0, p] = -jnp.sum(z, axis=0, keepdims=True)
            dstate[p] = ds_out * eglast + qgd - wdv

        _interleave([chain(p) for p in range(PAIRS)])

    last = nchunk - 1
    blk = lambda j: pl.BlockSpec((CHUNK, WIDTH), lambda n, j=j: (last - n, j))
    sv = pl.BlockSpec((1, PAIRS, CHUNK, LANES), lambda n: (last - n, 0, 0, 0))
    gr_spec = pl.BlockSpec((1, PAIRS, 1, LANES), lambda n: (last - n, 0, 0, 0))
    wide = jax.ShapeDtypeStruct((rows, WIDTH), F32)
    return pl.pallas_call(
        body, name="gdn_bwd", grid=(nchunk,),
        in_specs=[blk(0), blk(1), blk(2), blk(0), blk(0), gr_spec, sv, sv, blk(0)],
        out_specs=[blk(0)] * 5 + [gr_spec],
        out_shape=[wide] * 5 + [jax.ShapeDtypeStruct((nchunk, PAIRS, 1, LANES), F32)],
        scratch_shapes=[pltpu.VMEM((PAIRS, CHUNK, LANES), F32)],
        compiler_params=_params(("arbitrary",)),
    )(qkv, qkv, qkv, betax, gcx, grow, ssave, tsave, do)


ATT_TQ = 256


def _att_scores(qh, kt, fk, diag):
    s = _dot(qh, kt, _CONTRACT["nt"]) - fk
    if diag:
        r = lax.broadcasted_iota(jnp.int32, s.shape, 0)
        c = lax.broadcasted_iota(jnp.int32, s.shape, 1)
        s = jnp.where(r >= c, s, -jnp.inf)
    return s


def _head_masks(n):
    lo = _lane_lo((n, LANES))
    return [lo, jnp.logical_not(lo)]


def _attention_forward(fqk, proj, frow, rows):
    tq = tk = min(ATT_TQ, rows)
    nq = rows // tq
    v_off = 3072 // LANES

    def body(q_ref, k_ref, v_ref, fr_ref, o_ref, lse_ref):
        qi = pl.program_id(1)
        q = q_ref[...] * SCALE
        keep_q, keep_k = _head_masks(tq), _head_masks(tk)
        qh = [jnp.where(keep_q[h], q, 0.0).astype(BF16) for h in range(2)]

        def tile(ki, carry, diag):
            k0 = pl.multiple_of(ki * tk, tk)
            kt = k_ref[pl.ds(k0, tk), :].astype(BF16)
            v_t = v_ref[pl.ds(k0, tk), :]
            out = [None, None]

            def chain(h):
                m, l, acc = carry[h]
                vt = jnp.where(keep_k[h], v_t, 0.0).astype(BF16)
                yield
                s = _att_scores(qh[h], kt, fr_ref[0, pl.ds(h, 1), pl.ds(k0, tk)], diag)
                yield
                m_new = jnp.maximum(m, jnp.max(s, axis=-1, keepdims=True))
                p = jnp.exp(s - m_new)
                alpha = jnp.exp(m - m_new)
                l = alpha * l + jnp.sum(p, axis=-1, keepdims=True)
                p_hi = p.astype(BF16)
                p_lo = p - p_hi.astype(F32)
                yield
                out[h] = (m_new, l, alpha * acc + _dot(p_hi, vt, _CONTRACT["nn"]) + _dot(p_lo, vt, _CONTRACT["nn"]))

            _interleave([chain(0), chain(1)])
            return tuple(out)

        one = (jnp.full((tq, 1), -jnp.inf, F32), jnp.zeros((tq, 1), F32), jnp.zeros((tq, LANES), F32))
        carry = lax.fori_loop(0, qi, lambda ki, c: tile(ki, c, False), (one, one))
        (m0, l0, acc0), (m1, l1, acc1) = tile(qi, carry, True)
        o_ref[...] = acc0 / l0 + acc1 / l1
        lse_ref[...] = jnp.where(keep_q[0], m0 + jnp.log(l0), m1 + jnp.log(l1))

    whole = lambda off: pl.BlockSpec((rows, LANES), lambda p, i, off=off: (0, off + p))
    qblk = lambda off: pl.BlockSpec((tq, LANES), lambda p, i, off=off: (i, off + p))
    wide = jax.ShapeDtypeStruct((rows, WIDTH), F32)
    return pl.pallas_call(
        body, name="fox_fwd", grid=(PAIRS, nq),
        in_specs=[qblk(0), whole(PAIRS), whole(v_off), pl.BlockSpec((1, 2, rows), lambda p, i: (p, 0, 0))],
        out_specs=[qblk(0), qblk(0)], out_shape=[wide, wide],
        compiler_params=_params(("parallel", "arbitrary")),
    )(fqk, fqk, proj, frow)


def _attention_delta(fqk, proj, frow, lse, dao, rows):
    tq = tk = min(ATT_TQ, rows)
    nq = rows // tq
    v_off = 3072 // LANES

    def body(q_ref, k_ref, v_ref, fr_ref, lse_ref, do_ref, delta_ref):
        qi = pl.program_id(1)
        q, d_o, lse_t = q_ref[...] * SCALE, do_ref[...], lse_ref[...]
        keep_q = _head_masks(tq)
        qh = [jnp.where(keep_q[h], q, 0.0).astype(BF16) for h in range(2)]
        doh = [jnp.where(keep_q[h], d_o, 0.0).astype(BF16) for h in range(2)]
        lse_h = [_head_col(lse_t, keep_q[0], h) for h in range(2)]

        def tile(ki, carry, diag):
            k0 = pl.multiple_of(ki * tk, tk)
            kt = k_ref[pl.ds(k0, tk), :].astype(BF16)
            vt = v_ref[pl.ds(k0, tk), :].astype(BF16)
            out = [None, None]

            def chain(h):
                s = _att_scores(qh[h], kt, fr_ref[0, pl.ds(h, 1), pl.ds(k0, tk)], diag)
                dp = _dot(doh[h], vt, _CONTRACT["nt"])
                yield
                out[h] = carry[h] + jnp.sum(jnp.exp(s - lse_h[h]) * dp, axis=-1, keepdims=True)

            _interleave([chain(0), chain(1)])
            return tuple(out)

        zero = jnp.zeros((tq, 1), F32)
        carry = lax.fori_loop(0, qi, lambda ki, c: tile(ki, c, False), (zero, zero))
        d0, d1 = tile(qi, carry, True)
        delta_ref[...] = jnp.where(keep_q[0], d0, d1)

    whole = lambda off: pl.BlockSpec((rows, LANES), lambda p, i, off=off: (0, off + p))
    qblk = lambda off: pl.BlockSpec((tq, LANES), lambda p, i, off=off: (i, off + p))
    return pl.pallas_call(
        body, name="fox_delta", grid=(PAIRS, nq),
        in_specs=[qblk(0), whole(PAIRS), whole(v_off),
                  pl.BlockSpec((1, 2, rows), lambda p, i: (p, 0, 0)), qblk(0), qblk(0)],
        out_specs=qblk(0), out_shape=jax.ShapeDtypeStruct((rows, WIDTH), F32),
        compiler_params=_params(("parallel", "arbitrary")),
    )(fqk, fqk, proj, frow, lse, dao)


def _attention_backward(fqk, proj, frow, ao, lse, dao, rows):
    tq = tk = min(ATT_TQ, rows)
    nq = rows // tq
    v_off = 3072 // LANES

    def body(q_ref, k_ref, v_ref, fr_ref, o_ref, lse_ref, do_ref, dq_ref, dk_ref, dv_ref, dfr_ref):
        ki = pl.program_id(1)

        @pl.when(ki == 0)
        def _():
            dq_ref[...] = jnp.zeros_like(dq_ref)

        keep_q, keep_k = _head_masks(tq), _head_masks(tk)
        k_t = k_ref[...]
        kt = k_t.astype(BF16)
        vt = v_ref[...].astype(BF16)
        kh = [jnp.where(keep_k[h], k_t, 0.0).astype(BF16) for h in range(2)]
        fk = [fr_ref[0, pl.ds(h, 1), :] for h in range(2)]

        def tile(qi, carry, diag):
            dk, dv, df0, df1 = carry
            rows_q = pl.ds(pl.multiple_of(qi * tq, tq), tq)
            q, d_o, lse_t = q_ref[rows_q, :] * SCALE, do_ref[rows_q, :], lse_ref[rows_q, :]
            delta_x = _pair_sum(d_o.astype(BF16).astype(F32) * o_ref[rows_q, :])
            res = [None, None]

            def chain(h):
                qh = jnp.where(keep_q[h], q, 0.0).astype(BF16)
                doh = jnp.where(keep_q[h], d_o, 0.0).astype(BF16)
                lse_h, delta_h = _head_col(lse_t, keep_q[0], h), _head_col(delta_x, keep_q[0], h)
                yield
                s, dp = _att_scores(qh, kt, fk[h], diag), _dot(doh, vt, _CONTRACT["nt"])
                yield
                p = jnp.exp(s - lse_h)
                ds = p * (dp - delta_h)
                yield
                res[h] = (_dot(p, doh, _CONTRACT["tn"]), _dot(ds, qh, _CONTRACT["tn"]),
                          _dot(ds, kh[h], _CONTRACT["nn"]), jnp.sum(ds, axis=0, keepdims=True))

            _interleave([chain(0), chain(1)])
            (dv0, dk0, dq0, s0), (dv1, dk1, dq1, s1) = res
            dq_ref[rows_q, :] += (dq0 + dq1) * SCALE
            return dk + dk0 + dk1, dv + dv0 + dv1, df0 - s0, df1 - s1

        zero_kv = jnp.zeros((tk, LANES), F32)
        zero_f = jnp.zeros((1, tk), F32)
        carry = tile(ki, (zero_kv, zero_kv, zero_f, zero_f), True)
        dk, dv, df0, df1 = lax.fori_loop(ki + 1, nq, lambda qi, c: tile(qi, c, False), carry)
        dk_ref[...] = dk
        dv_ref[...] = dv.astype(dv_ref.dtype)
        dfr_ref[0, pl.ds(0, 1), :] = df0
        dfr_ref[0, pl.ds(1, 1), :] = df1

    whole = lambda off: pl.BlockSpec((rows, LANES), lambda p, i, off=off: (0, off + p))
    kblk = lambda off: pl.BlockSpec((tk, LANES), lambda p, i, off=off: (i, off + p))
    fr_spec = pl.BlockSpec((1, 2, tk), lambda p, i: (p, 0, i))
    wide = jax.ShapeDtypeStruct((rows, WIDTH), F32)
    return pl.pallas_call(
        body, name="fox_bwd", grid=(PAIRS, nq),
        in_specs=[whole(0), kblk(PAIRS), kblk(v_off), fr_spec, whole(0), whole(0), whole(0)],
        out_specs=[whole(0), kblk(0), kblk(0), fr_spec],
        out_shape=[wide, wide, jax.ShapeDtypeStruct((rows, WIDTH), BF16),
                   jax.ShapeDtypeStruct((PAIRS, 2, rows), F32)],
        compiler_params=_params(("parallel", "arbitrary")),
    )(fqk, fqk, proj, frow, ao, lse, dao)


def _lane_ids(shape):
    return lax.broadcasted_iota(jnp.int32, shape, len(shape) - 1)


def _gates_elem(a_log, dt_bias, f_bias, pre):
    lane = _lane_ids(pre.shape)
    beta = jax.nn.sigmoid(pre)
    g = -jnp.exp(a_log) * _softplus(pre + dt_bias)
    lf = -_softplus(-(pre + f_bias))
    return jnp.where(lane < 8, beta, jnp.where(lane < 16, g, jnp.where(lane < 24, lf, 0.0)))


def _tri_consts():
    r = np.arange(LANES)[:, None]
    c = np.arange(LANES)[None, :]
    full = (c <= r).astype(np.float32)
    chunked = full * ((r // CHUNK) == (c // CHUNK))
    return jnp.asarray(chunked), jnp.asarray(full)


def _cums_fwd(lc, lf, gates):
    rows = gates.shape[0]
    lane = _lane_ids((LANES, LANES))
    carry = jnp.zeros((1, LANES), F32)
    out = []
    for r in range(rows // LANES):
        blk = gates[r * LANES:(r + 1) * LANES]
        gc = _dot32(lc, blk, _CONTRACT["nn"])
        f = _dot32(lf, blk, _CONTRACT["nn"]) + carry
        carry = carry + jnp.sum(blk, axis=0, keepdims=True)
        out.append(jnp.where((lane >= 8) & (lane < 16), gc, jnp.where((lane >= 16) & (lane < 24), f, 0.0)))
    return jnp.concatenate(out, axis=0)


def _cums_bwd(lc, lf, dcums):
    rows = dcums.shape[0]
    lane = _lane_ids((LANES, LANES))
    is_g = (lane >= 8) & (lane < 16)
    is_f = (lane >= 16) & (lane < 24)
    carry = jnp.zeros((1, LANES), F32)
    out = [None] * (rows // LANES)
    for r in reversed(range(rows // LANES)):
        blk = dcums[r * LANES:(r + 1) * LANES]
        dg = jnp.where(is_g, blk, 0.0)
        df = jnp.where(is_f, blk, 0.0)
        out[r] = _dot32(lc, dg, _CONTRACT["tn"]) + _dot32(lf, df, _CONTRACT["tn"]) + carry
        carry = carry + jnp.sum(df, axis=0, keepdims=True)
    return jnp.concatenate(out, axis=0)


def _expand_consts():
    xb = np.zeros((LANES, WIDTH), np.float32)
    xg = np.zeros((LANES, WIDTH), np.float32)
    for h in range(HEADS):
        xb[h, h * HEAD_DIM:(h + 1) * HEAD_DIM] = 1.0
        xg[8 + h, h * HEAD_DIM:(h + 1) * HEAD_DIM] = 1.0
    return jnp.asarray(xb), jnp.asarray(xg)


def _shift_down(x, s):
    if s == 0:
        return x
    row = lax.broadcasted_iota(jnp.int32, x.shape, 0)
    return jnp.where(row >= s, pltpu.roll(x, s, 0), 0.0)


def _shift_up(x, s):
    if s == 0:
        return x
    n = x.shape[0]
    row = lax.broadcasted_iota(jnp.int32, x.shape, 0)
    return jnp.where(row < n - s, pltpu.roll(x, n - s, 0), 0.0)


def _row_of(cw, i):
    row = lax.broadcasted_iota(jnp.int32, cw.shape, 0)
    return jnp.sum(jnp.where(row == i, cw, 0.0), axis=0, keepdims=True)


def _conv(cw, x):
    c = jnp.zeros_like(x)
    for i in range(CONV_K):
        c = c + _row_of(cw, i) * _shift_down(x, CONV_K - 1 - i)
    return c


def _post_conv(is_qk, c):
    s = _silu(c)
    n = s * lax.rsqrt(_pair_sum(s * s) + EPS)
    return jnp.where(is_qk, n, s)


def _gdn_prep_fwd(col, cw, x):
    return (_post_conv(col < 2 * PAIRS, _conv(cw, x)),)


def _gdn_prep_bwd(is_qk, cw, x, dy):
    c = _conv(cw, x)
    _, vjp = jax.vjp(lambda cc: _post_conv(is_qk, cc), c)
    (dc,) = vjp(dy)
    dx = jnp.zeros_like(x)
    row = lax.broadcasted_iota(jnp.int32, cw.shape, 0)
    dcw = jnp.zeros(cw.shape, F32)
    for i in range(CONV_K):
        s = CONV_K - 1 - i
        dx = dx + _row_of(cw, i) * _shift_up(dc, s)
        dcw = dcw + jnp.where(row == i, jnp.sum(dc * _shift_down(x, s), axis=0, keepdims=True), 0.0)
    return dx, dcw


def _head_rms(w, x):
    return x * lax.rsqrt(_pair_sum(x * x) / HEAD_DIM + EPS) * w


def _cat_weights(w_in_t):
    tail = jnp.pad(w_in_t[4112:4120], ((0, D_CAT - D_IN), (0, 0)))
    return jnp.concatenate([w_in_t[:2048], w_in_t[2064:4112], w_in_t[2048:2064], tail], axis=0)


def _uncat_grad(g):
    return jnp.concatenate([g[:2048], g[4096:4112], g[2048:4096], g[4112:4120]], axis=0)


def _lanes_to_rowform(v8, rows):
    return v8.reshape(rows // CHUNK, CHUNK, HEADS).transpose(0, 2, 1).reshape(rows // CHUNK, PAIRS, 1, LANES)


def _rowform_to_lanes(v, rows):
    return v.reshape(rows // CHUNK, HEADS, CHUNK).transpose(0, 2, 1).reshape(rows, HEADS)


def _local_step(x, target, norm1_w, a_log, dt_bias, out_norm_w, f_bias, q_norm_w, k_norm_w,
                norm2_w, final_w, first_weights, late_weights, early_grads_ready):
    rows = x.shape[0]
    tm = min(512, rows)
    lc, lf = _tri_consts()
    xb, xg = _expand_consts()

    (h1,) = _tiles(lambda col, w, xx: (_rms(xx, w),), name="norm1", rows=rows, tm=tm,
                   full_consts=[norm1_w], row_ins=[(x, D_MODEL, 0)], row_outs=[(D_MODEL, BF16)])
    w_cat, conv_w = first_weights(h1)
    proj = _mm(h1, w_cat, dims="nt", name="in_proj", tn=1408, tk=1024)

    lane_pad = lambda v, off: jnp.pad(v.reshape(1, -1), ((0, 0), (off, LANES - off - v.size)))
    p_a, p_dt, p_fb = lane_pad(a_log, 8), lane_pad(dt_bias, 8), lane_pad(f_bias, 16)

    def gates_fwd(col, lcv, lfv, a, dt, fb, pre):
        gates = _gates_elem(a, dt, fb, pre)
        return gates, _cums_fwd(lcv, lfv, gates)

    gates, cums = _tiles(gates_fwd, name="gates", rows=rows, tm=rows,
                         full_consts=[lc, lf, p_a, p_dt, p_fb], row_ins=[(proj, LANES, COL_SMALL)],
                         row_outs=[(LANES, F32), (LANES, F32)])

    def expand_fwd(col, b, g, gt, cm):
        return (_dot32(gt, b, _CONTRACT["nn"]), _dot32(cm, g, _CONTRACT["nn"]))

    betax, gcx = _tiles(expand_fwd, name="expand", rows=rows, tm=tm, full_consts=[xb, xg],
                        row_ins=[(gates, LANES, 0), (cums, LANES, 0)],
                        row_outs=[(WIDTH, F32)] * 2)
    grow = _lanes_to_rowform(cums[:, 8:16], rows)
    frow = cums[:, 16:24].T.reshape(PAIRS, 2, rows)

    (qkv,) = _tiles(_gdn_prep_fwd, name="gdn_prep", rows=rows, tm=rows, ncol=3 * PAIRS,
                    col_consts=[(conv_w, CONV_K, LANES, 0)], row_ins=[(proj, LANES, 0)],
                    row_outs=[(LANES, F32)])
    o_gdn, ssave, tsave = _gdn_forward(qkv, betax, gcx, grow, rows)

    w_qk = jnp.concatenate([jnp.tile(q_norm_w.reshape(1, -1), (1, HEADS)),
                            jnp.tile(k_norm_w.reshape(1, -1), (1, HEADS))], axis=1)
    fox_off = 2048 // LANES
    (fqk,) = _tiles(lambda col, w, xx: (_head_rms(w, xx),), name="fox_prep", rows=rows, tm=rows, ncol=2 * PAIRS,
                    col_consts=[(w_qk, 1, LANES, 0)], row_ins=[(proj, LANES, fox_off)],
                    row_outs=[(LANES, F32)])
    ao, lse = _attention_forward(fqk, proj, frow, rows)

    w_on = jnp.tile(out_norm_w.reshape(1, -1), (1, 2))
    z_off, fg_off = 1536 // LANES, 3584 // LANES
    mix_g_fn = lambda w, o, z: _head_rms(w, o) * _silu(z)
    mix_f_fn = lambda a, g: a * jax.nn.sigmoid(g)
    (mix_g,) = _tiles(lambda col, w, o, z: (mix_g_fn(w, o, z),), name="mix_gdn", rows=rows, tm=rows, ncol=PAIRS,
                      full_consts=[w_on], row_ins=[(o_gdn, LANES, 0), (proj, LANES, z_off)],
                      row_outs=[(LANES, BF16)])
    (mix_f,) = _tiles(lambda col, a, g: (mix_f_fn(a, g),), name="mix_fox", rows=rows, tm=rows, ncol=PAIRS,
                      row_ins=[(ao, LANES, 0), (proj, LANES, fg_off)], row_outs=[(LANES, BF16)])
    mix = jnp.concatenate([mix_g, mix_f], axis=1)
    w_out, w_gate, w_up, w_down = late_weights(mix)
    x1 = _mm(mix, w_out, dims="nn", name="out_proj", add=x, tn=D_MODEL, tk=1024)

    (h2,) = _tiles(lambda col, w, xx: (_rms(xx, w),), name="norm2", rows=rows, tm=tm,
                   full_consts=[norm2_w], row_ins=[(x1, D_MODEL, 0)], row_outs=[(D_MODEL, BF16)])
    t_rows, t_cols, t_act = min(1024, rows), D_MODEL, min(512, rows)
    n_rt = rows // t_rows
    st_act = jax.ShapeDtypeStruct((N_CHIPS, rows, FF_SHARD), BF16)
    st_rows = pl.BlockSpec((None, rows, FF_SHARD), lambda i, j: (j, i, 0))
    out_rows = pl.BlockSpec((t_rows, t_cols), lambda i, n: (i, n))
    flat = lambda t: t.reshape(N_CHIPS * rows, FF_SHARD)

    def ffn_in(w_st, name):
        return _mm_blocks(h2, w_st, name=name, grid=(1, N_CHIPS), dims="nt",
                          a_spec=pl.BlockSpec((rows, D_MODEL), lambda i, j: (i, 0)),
                          b_spec=pl.BlockSpec((None, FF_SHARD, D_MODEL), lambda i, j: (j, 0, 0)),
                          o_spec=st_rows, out_shape=st_act)

    gate = ffn_in(w_gate, "ffn_gate")
    act_fn = lambda g, u: _silu(g) * u
    st_tile = pl.BlockSpec((None, t_rows, FF_SHARD), lambda i, j: (j, i, 0))
    up, act = _mm_blocks(h2, w_up, name="ffn_up_act", grid=(n_rt, N_CHIPS), dims="nt",
                         a_spec=pl.BlockSpec((t_rows, D_MODEL), lambda i, j: (i, 0)),
                         b_spec=pl.BlockSpec((None, FF_SHARD, D_MODEL), lambda i, j: (j, 0, 0)),
                         o_spec=[st_tile, st_tile], out_shape=[st_act, st_act], extra=[(gate, st_tile)],
                         epilogue=lambda u, g: (u, act_fn(g.astype(F32), u)))
    x2 = _mm_blocks(act, w_down, name="ffn_down", grid=(n_rt, D_MODEL // t_cols), dims="nn", n_sum=N_CHIPS,
                    a_spec=pl.BlockSpec((N_CHIPS, t_rows, FF_SHARD), lambda i, n: (0, i, 0)),
                    b_spec=pl.BlockSpec((N_CHIPS, FF_SHARD, t_cols), lambda i, n: (0, 0, n)),
                    o_spec=out_rows, out_shape=jax.ShapeDtypeStruct((rows, D_MODEL), F32),
                    add=x1, add_spec=out_rows)

    def final_fn(col, w, xx, tgt):
        y, vjp = jax.vjp(_rms, xx, w)
        err = y - tgt
        loss = 0.5 * jnp.sum(err * err) / D_MODEL
        dx, dw = vjp(err / D_MODEL)
        return dx, dx, jnp.full((1, LANES), loss, F32), dw

    dx2, dx2_b, loss, d_final_w = _tiles(final_fn, name="final_loss", rows=rows, tm=tm, full_consts=[final_w],
                                         row_ins=[(x2, D_MODEL, 0), (target, D_MODEL, 0)],
                                         row_outs=[(D_MODEL, F32), (D_MODEL, BF16)],
                                         acc_outs=[(1, LANES), (1, D_MODEL)])

    def act_bwd(d, g, u):
        _, vjp = jax.vjp(act_fn, g.astype(F32), u.astype(F32))
        return vjp(d)

    dgate, dup = _mm_blocks(dx2_b, w_down, name="d_act_gate_up", grid=(n_rt, N_CHIPS), dims="nt",
                            a_spec=pl.BlockSpec((t_rows, D_MODEL), lambda i, j: (i, 0)),
                            b_spec=pl.BlockSpec((None, FF_SHARD, D_MODEL), lambda i, j: (j, 0, 0)),
                            o_spec=[st_tile, st_tile], out_shape=[st_act, st_act],
                            extra=[(gate, st_tile), (up, st_tile)], epilogue=act_bwd)

    def g_ffn(d_st, other, name):
        return _mm_blocks(d_st, other, name=name, grid=(N_CHIPS, D_MODEL // t_cols), dims="tn",
                          a_spec=pl.BlockSpec((None, rows, FF_SHARD), lambda j, n: (j, 0, 0)),
                          b_spec=pl.BlockSpec((rows, t_cols), lambda j, n: (0, n)),
                          o_spec=pl.BlockSpec((None, FF_SHARD, t_cols), lambda j, n: (j, 0, n)),
                          out_shape=jax.ShapeDtypeStruct((N_CHIPS, FF_SHARD, D_MODEL), BF16))

    g_down = g_ffn(act, dx2_b, "g_down")

    def d_h2(d_st, w_st, name, add):
        return _mm_blocks(d_st, w_st, name=name, grid=(n_rt, D_MODEL // t_cols), dims="nn", n_sum=N_CHIPS,
                          a_spec=pl.BlockSpec((N_CHIPS, t_rows, FF_SHARD), lambda i, n: (0, i, 0)),
                          b_spec=pl.BlockSpec((N_CHIPS, FF_SHARD, t_cols), lambda i, n: (0, 0, n)),
                          o_spec=out_rows, out_shape=jax.ShapeDtypeStruct((rows, D_MODEL), F32),
                          add=add, add_spec=out_rows)

    dh2 = d_h2(dup, w_up, "d_h2_up", d_h2(dgate, w_gate, "d_h2_gate", None))
    g_gate, g_up = g_ffn(dgate, h2, "g_gate"), g_ffn(dup, h2, "g_up")

    def norm_bwd(col, w, xx, dh, dres):
        _, vjp = jax.vjp(_rms, xx, w)
        dx, dw = vjp(dh)
        return dx + dres, dx + dres, dw

    dx1, dx1_b, d_norm2_w = _tiles(norm_bwd, name="norm2_bwd", rows=rows, tm=tm, full_consts=[norm2_w],
                                   row_ins=[(x1, D_MODEL, 0), (dh2, D_MODEL, 0), (dx2, D_MODEL, 0)],
                                   row_outs=[(D_MODEL, F32), (D_MODEL, BF16)], acc_outs=[(1, D_MODEL)])
    dmix = _mm(dx1_b, w_out, dims="nt", name="d_mix", tn=D_MODEL, tk=1024)
    g_out = _mm(mix, dx1_b, dims="tn", name="g_out", tn=D_MODEL, tk=rows, out_dtype=BF16)
    w_on = w_on + early_grads_ready(g_out, g_gate, g_up, g_down)

    def mix_g_bwd(col, w, o, z, d):
        _, vjp = jax.vjp(mix_g_fn, w, o, z)
        dw, do_, dz = vjp(d)
        return do_, dz, dw

    do_gdn, dz, d_on = _tiles(mix_g_bwd, name="mix_gdn_bwd", rows=rows, tm=rows, ncol=PAIRS, full_consts=[w_on],
                              row_ins=[(o_gdn, LANES, 0), (proj, LANES, z_off), (dmix, LANES, 0)],
                              row_outs=[(LANES, F32), (LANES, BF16)], acc_outs=[(1, LANES)])

    def mix_f_bwd(col, a, g, d):
        _, vjp = jax.vjp(mix_f_fn, a, g)
        return vjp(d)

    dao, dfgate = _tiles(mix_f_bwd, name="mix_fox_bwd", rows=rows, tm=rows, ncol=PAIRS,
                         row_ins=[(ao, LANES, 0), (proj, LANES, fg_off), (dmix, LANES, PAIRS)],
                         row_outs=[(LANES, F32), (LANES, BF16)])

    dfq, dfk, dfv, dfrow = _attention_backward(fqk, proj, frow, ao, lse, dao, rows)

    def fox_prep_bwd(col, w, xx, d):
        _, vjp = jax.vjp(_head_rms, w, xx)
        dw, dx = vjp(d)
        return dx, dw

    dfqk, d_wqk = [], []
    for part, d_n in enumerate((dfq, dfk)):
        dx_p, dw_p = _tiles(fox_prep_bwd, name="fox_prep_bwd_" + "qk"[part], rows=rows, tm=rows, ncol=PAIRS,
                            col_consts=[(w_qk, 1, LANES, part * PAIRS)],
                            row_ins=[(proj, LANES, fox_off + part * PAIRS), (d_n, LANES, 0)],
                            row_outs=[(LANES, BF16)], acc_outs=[(1, LANES)])
        dfqk.append(dx_p)
        d_wqk.append(dw_p)

    dq, dk, dv, dbetax, dgcx, dgrow = _gdn_backward(qkv, betax, gcx, grow, ssave, tsave, do_gdn, rows)
    dqkv, d_conv = [], []
    for part, d_n in enumerate((dq, dk, dv)):
        prep_bwd = lambda col, cw, xx, dy, is_qk=(part < 2): _gdn_prep_bwd(is_qk, cw, xx, dy)
        dx_p, dw_p = _tiles(prep_bwd, name="gdn_prep_bwd_" + "qkv"[part], rows=rows, tm=rows, ncol=PAIRS,
                            col_consts=[(conv_w, CONV_K, LANES, part * PAIRS)],
                            row_ins=[(proj, LANES, part * PAIRS), (d_n, LANES, 0)],
                            row_outs=[(LANES, BF16)], acc_outs=[(CONV_K, LANES)])
        dqkv.append(dx_p)
        d_conv.append(dw_p)
    d_conv = jnp.concatenate(d_conv, axis=1)

    def expand_bwd(col, b, g, db, dg):
        return (_dot32(db, b, _CONTRACT["nt"]), _dot32(dg, g, _CONTRACT["nt"]))

    dgates_b, dcums_g = _tiles(expand_bwd, name="expand_bwd", rows=rows, tm=tm, full_consts=[xb, xg],
                               row_ins=[(dbetax, WIDTH, 0), (dgcx, WIDTH, 0)],
                               row_outs=[(LANES, F32), (LANES, F32)])
    dcums_row = jnp.concatenate([jnp.zeros((rows, 8), F32), _rowform_to_lanes(dgrow, rows),
                                 dfrow.reshape(HEADS, rows).T, jnp.zeros((rows, LANES - 24), F32)], axis=1)

    def gates_bwd(col, lcv, lfv, a, dt, fb, pre, dgb, dcg, dcr):
        lane = _lane_ids(pre.shape)
        dgates = jnp.where(lane < 8, dgb, _cums_bwd(lcv, lfv, dcg + dcr))
        _, vjp = jax.vjp(_gates_elem, a, dt, fb, pre)
        da, ddt, dfb, dpre = vjp(dgates)
        return dpre, da, ddt, dfb

    dpre, d_a, d_dt, d_fb = _tiles(gates_bwd, name="gates_bwd", rows=rows, tm=rows,
                                   full_consts=[lc, lf, p_a, p_dt, p_fb],
                                   row_ins=[(proj, LANES, COL_SMALL), (dgates_b, LANES, 0), (dcums_g, LANES, 0),
                                            (dcums_row, LANES, 0)],
                                   row_outs=[(LANES, BF16)], acc_outs=[(1, LANES)] * 3)

    dproj = jnp.concatenate(dqkv + [dz] + dfqk + [dfv, dfgate, dpre], axis=1)
    dh1 = _mm(dproj, w_cat, dims="nn", name="d_h1", tm=512, tn=D_MODEL, tk=D_CAT)
    g_cat = _mm(dproj, h1, dims="tn", name="g_in", tm=1408, tn=D_MODEL, tk=rows)

    def norm1_bwd(col, w, xx, dh, dres):
        _, vjp = jax.vjp(_rms, xx, w)
        dx, dw = vjp(dh)
        return dx + dres, dw

    grad_x, d_norm1_w = _tiles(norm1_bwd, name="norm1_bwd", rows=rows, tm=tm, full_consts=[norm1_w],
                               row_ins=[(x, D_MODEL, 0), (dh1, D_MODEL, 0), (dx1, D_MODEL, 0)],
                               row_outs=[(D_MODEL, F32)], acc_outs=[(1, D_MODEL)])

    fold = lambda v: v.reshape(-1, HEAD_DIM).sum(axis=0)
    small = dict(
        loss=loss[0, 0],
        norm1_w=d_norm1_w, conv_w=d_conv, a_log=d_a[0, 8:16], dt_bias=d_dt[0, 8:16],
        out_norm_w=fold(d_on), f_bias=d_fb[0, 16:24], q_norm_w=fold(d_wqk[0]),
        k_norm_w=fold(d_wqk[1]), norm2_w=d_norm2_w, final_w=d_final_w)
    return grad_x, g_cat, g_out, g_gate, g_up, g_down, small


HBM_SPEC = pl.BlockSpec(memory_space=pltpu.HBM)


def _place():
    x, y, c = lax.axis_index("x"), lax.axis_index("y"), lax.axis_index("c")
    chips = [(1 - x, y), (x, 1 - y), (1 - x, 1 - y)]
    return x, y, c, 2 * x + y, (x, y, 1 - c), chips, [2 * cx + cy for cx, cy in chips]


def _remote(src, dst, send_sem, recv_sem, to):
    return pltpu.make_async_remote_copy(src_ref=src, dst_ref=dst, send_sem=send_sem, recv_sem=recv_sem,
                                        device_id=to, device_id_type=MESH)


def _allgather_weights(shards, conv):
    n = len(shards)
    halves = [s.shape[1] // 2 for s in shards]
    per = 6
    own_base = n * per + 3

    def body(*refs):
        ins, conv_in = refs[:n], refs[n]
        outs, conv_out = refs[n + 1:2 * n + 1], refs[2 * n + 1]
        send_sems, recv_sems = refs[2 * n + 2:]
        x, y, c, own, sib, chips, chip_idx = _place()

        def half(i, ref, hc):
            return ref.at[:, pl.ds(pl.multiple_of(hc * halves[i], LANES), halves[i])]

        sent = []
        for i, (src, dst) in enumerate(zip(list(ins) + [conv_in], list(outs) + [conv_out])):
            k = own_base + i
            sent.append(_remote(src, dst.at[own], send_sems.at[k], recv_sems.at[k], sib))
        for i in range(n):
            for j, chip in enumerate(chips):
                k = i * per + j
                sent.append(_remote(half(i, ins[i], c), half(i, outs[i].at[own], c),
                                    send_sems.at[k], recv_sems.at[k], (*chip, c)))
        for j, chip in enumerate(chips):
            k = n * per + j
            sent.append(_remote(conv_in, conv_out.at[own], send_sems.at[k], recv_sems.at[k], (*chip, c)))
        for cp in sent:
            cp.start()
        for i in range(n):
            for j in range(len(chips)):
                k = i * per + j
                landed = half(i, outs[i].at[chip_idx[j]], c)
                _remote(landed, landed, send_sems.at[k], recv_sems.at[k], sib).wait_recv()
                fwd = _remote(landed, landed, send_sems.at[k + 3], recv_sems.at[k + 3], sib)
                fwd.start()
                sent.append(fwd)
        for i in range(n):
            for j in range(len(chips)):
                k = i * per + 3 + j
                landed = half(i, outs[i].at[chip_idx[j]], 1 - c)
                _remote(landed, landed, send_sems.at[k], recv_sems.at[k], sib).wait_recv()
        for j in range(len(chips)):
            k = n * per + j
            landed = conv_out.at[chip_idx[j]]
            _remote(landed, landed, send_sems.at[k], recv_sems.at[k], sib).wait_recv()
        for i, dst in enumerate(list(outs) + [conv_out]):
            k = own_base + i
            landed = dst.at[own]
            _remote(landed, landed, send_sems.at[k], recv_sems.at[k], sib).wait_recv()
        for cp in sent:
            cp.wait_send()

    n_sem = own_base + n + 1
    out_shape = [jax.ShapeDtypeStruct((N_CHIPS,) + s.shape, s.dtype) for s in shards]
    out_shape.append(jax.ShapeDtypeStruct((N_CHIPS,) + conv.shape, conv.dtype))
    res = pl.pallas_call(
        body, name="allgather_weights", out_shape=out_shape,
        in_specs=[HBM_SPEC] * (n + 1), out_specs=[HBM_SPEC] * (n + 1),
        scratch_shapes=[pltpu.SemaphoreType.DMA((n_sem,)), pltpu.SemaphoreType.DMA((n_sem,))],
    )(*shards, conv)
    return res[:n], res[n]


SEM_SPEC = pl.BlockSpec(memory_space=pltpu.SEMAPHORE)
ANY_SPEC = pl.BlockSpec(memory_space=pl.ANY)
DATAFLOW = pltpu.SideEffectType.DATAFLOW_SIDE_EFFECTING


def _gather_plan(srcs, lands):
    x, y, c, own, sib, chips, chip_idx = _place()
    plan = []
    for src, land in zip(srcs, lands):
        for j, chip in enumerate(chips):
            plan.append((src, land.at[own], (*chip, c), land.at[chip_idx[j]]))
        plan.append((src, land.at[own], sib, land.at[own]))
    return plan


def _exchange_plan(srcs, lands):
    x, y, c, own, sib, chips, chip_idx = _place()
    plan = []
    for src, land in zip(srcs, lands):
        for j, chip in enumerate(chips):
            plan.append((src.at[chip_idx[j]], land.at[j], (*chip, c), land.at[j]))
    return plan


def _in_proj_plan(srcs, lands):
    x, y, c, own, sib, chips, chip_idx = _place()
    (w, conv), (w_land, conv_land) = srcs, lands
    hw = w.shape[1] // 2
    half = lambda ref: ref.at[:, pl.ds(pl.multiple_of(c * hw, LANES), hw)]
    plan = []
    for j, chip in enumerate(chips):
        plan.append((half(w), half(w_land.at[own]), (*chip, c), half(w_land.at[chip_idx[j]])))
        plan.append((conv, conv_land.at[own], (*chip, c), conv_land.at[chip_idx[j]]))
    plan.append((w, w_land.at[own], sib, w_land.at[own]))
    plan.append((conv, conv_land.at[own], sib, conv_land.at[own]))
    return plan


def _forward_halves(landed):
    hw = landed.shape[2] // 2

    def body(in_ref, out_ref, send_sems, recv_sems):
        x, y, c, own, sib, chips, chip_idx = _place()
        half = lambda ref, hc: ref.at[:, pl.ds(pl.multiple_of(hc * hw, LANES), hw)]
        sent = [_remote(half(out_ref.at[chip_idx[j]], c), half(out_ref.at[chip_idx[j]], c),
                        send_sems.at[j], recv_sems.at[j], sib) for j in range(3)]
        for cp in sent:
            cp.start()
        for j in range(3):
            other = half(out_ref.at[chip_idx[j]], 1 - c)
            _remote(other, other, send_sems.at[j], recv_sems.at[j], sib).wait_recv()
        for cp in sent:
            cp.wait_send()

    return pl.pallas_call(
        body, name="gather_in_forward", out_shape=jax.ShapeDtypeStruct(landed.shape, landed.dtype),
        in_specs=[HBM_SPEC], out_specs=HBM_SPEC, input_output_aliases={0: 0},
        scratch_shapes=[pltpu.SemaphoreType.DMA((3,)), pltpu.SemaphoreType.DMA((3,))],
    )(landed)


def _split_start(name, plan_fn, srcs, land_shapes, n_copies, after):
    n = len(srcs)

    def body(*refs):
        src_refs, land_refs = refs[:n], refs[n:2 * n]
        send_sems, recv_sems = refs[2 * n + 1], refs[2 * n + 2]
        token = refs[-1]
        for k, (src, dst, to, _) in enumerate(plan_fn(src_refs, land_refs)):
            _remote(src, dst, send_sems.at[k], recv_sems.at[k], to).start()
        token[...] = jnp.zeros_like(token)

    lands = [pltpu.with_memory_space_constraint(lax.empty(s.shape, s.dtype), pltpu.HBM) for s in land_shapes]
    srcs = [pltpu.with_memory_space_constraint(s, pltpu.HBM) for s in srcs]
    out_shape = ([pltpu.SemaphoreType.DMA((n_copies,)), pltpu.SemaphoreType.DMA((n_copies,))]
                 + [pltpu.HBM(s.shape, s.dtype) for s in srcs] + [pltpu.HBM(s.shape, s.dtype) for s in land_shapes]
                 + [jax.ShapeDtypeStruct((8, LANES), F32)])
    res = pl.pallas_call(
        body, name=name, out_shape=out_shape,
        in_specs=[HBM_SPEC] * (2 * n) + [ANY_SPEC],
        out_specs=[SEM_SPEC, SEM_SPEC] + [HBM_SPEC] * (2 * n) + [pl.BlockSpec(memory_space=pltpu.VMEM)],
        input_output_aliases={i: 2 + i for i in range(2 * n)},
        compiler_params=pltpu.CompilerParams(has_side_effects=DATAFLOW),
    )(*srcs, *lands, after)
    return dict(sems=res[:2], srcs=res[2:2 + n], lands=res[2 + n:2 + 2 * n], token=res[-1], n=n)


def _split_wait(name, plan_fn, started, after):
    n = started["n"]

    def body(*refs):
        src_refs, land_refs = refs[:n], refs[n:2 * n]
        send_sems, recv_sems = refs[2 * n], refs[2 * n + 1]
        for k, (src, _, to, landed) in enumerate(plan_fn(src_refs, land_refs)):
            copy = _remote(src, landed, send_sems.at[k], recv_sems.at[k], to)
            copy.wait_send()
            copy.wait_recv()

    srcs, lands = started["srcs"], started["lands"]
    after = list(after) if isinstance(after, (list, tuple)) else [after]
    res = pl.pallas_call(
        body, name=name,
        out_shape=[pltpu.HBM(s.shape, s.dtype) for s in srcs] + [pltpu.HBM(s.shape, s.dtype) for s in lands],
        in_specs=[HBM_SPEC] * (2 * n) + [SEM_SPEC, SEM_SPEC] + [ANY_SPEC] * len(after),
        out_specs=[HBM_SPEC] * (2 * n),
        input_output_aliases={i: i for i in range(2 * n)},
        compiler_params=pltpu.CompilerParams(has_side_effects=DATAFLOW),
    )(*srcs, *lands, *started["sems"], *after)
    return res[n:]


def _swap_halves(stacks, name):
    n = len(stacks)

    def body(*refs):
        ins, outs = refs[:n], refs[n:2 * n]
        send_sems, recv_sems = refs[2 * n:]
        x, y, c, own, sib, chips, chip_idx = _place()
        cps = []
        for i in range(n):
            h = stacks[i].shape[2] // 2
            src = ins[i].at[:, :, pl.ds(pl.multiple_of((1 - c) * h, LANES), h)]
            cps.append(_remote(src, outs[i], send_sems.at[i], recv_sems.at[i], sib))
        for cp in cps:
            cp.start()
        for cp in cps:
            cp.wait()

    out_shape = [jax.ShapeDtypeStruct((N_CHIPS, s.shape[1], s.shape[2] // 2), s.dtype) for s in stacks]
    return pl.pallas_call(
        body, name=name, out_shape=out_shape,
        in_specs=[HBM_SPEC] * n, out_specs=[HBM_SPEC] * n,
        scratch_shapes=[pltpu.SemaphoreType.DMA((n,)), pltpu.SemaphoreType.DMA((n,))],
    )(*stacks)


def _add_half(stack, landed, place, name):
    _, rows, h = landed.shape

    def body(place_ref, a_ref, b_ref, o_ref, own_ref):
        part = (a_ref[...].astype(F32) + b_ref[...].astype(F32)).astype(o_ref.dtype)
        o_ref[...] = part

        @pl.when(pl.program_id(0) == place_ref[1])
        def _():
            own_ref[...] = part[0]

    return pl.pallas_call(
        body, name=name,
        out_shape=[jax.ShapeDtypeStruct(landed.shape, BF16), jax.ShapeDtypeStruct((rows, h), BF16)],
        grid_spec=pltpu.PrefetchScalarGridSpec(
            num_scalar_prefetch=1, grid=(N_CHIPS,),
            in_specs=[pl.BlockSpec((1, rows, h), lambda j, p: (j, 0, p[0])),
                      pl.BlockSpec((1, rows, h), lambda j, p: (j, 0, 0))],
            out_specs=[pl.BlockSpec((1, rows, h), lambda j, p: (j, 0, 0)),
                       pl.BlockSpec((rows, h), lambda j, p: (0, 0))]),
        compiler_params=_params(("arbitrary",)),
    )(place, stack, landed)


def _exchange_partials(parts):
    n = len(parts)

    def body(*refs):
        ins, outs = refs[:n], refs[n:2 * n]
        send_sems, recv_sems = refs[2 * n:]
        x, y, c, own, sib, chips, chip_idx = _place()
        sent = []
        for i in range(n):
            for j, chip in enumerate(chips):
                k = i * 3 + j
                sent.append(_remote(ins[i].at[chip_idx[j]], outs[i].at[j], send_sems.at[k], recv_sems.at[k],
                                    (*chip, c)))
        for cp in sent:
            cp.start()
        for i in range(n):
            for j in range(len(chips)):
                k = i * 3 + j
                landed = outs[i].at[j]
                _remote(landed, landed, send_sems.at[k], recv_sems.at[k], sib).wait_recv()
        for cp in sent:
            cp.wait_send()

    return pl.pallas_call(
        body, name="rs_exchange_partials",
        out_shape=[jax.ShapeDtypeStruct((3,) + p.shape[1:], p.dtype) for p in parts],
        in_specs=[HBM_SPEC] * n, out_specs=[HBM_SPEC] * n,
        scratch_shapes=[pltpu.SemaphoreType.DMA((3 * n,)), pltpu.SemaphoreType.DMA((3 * n,))],
    )(*parts)


def _sum_partials(own_part, landed, name, untiled_rows=False):
    _, h, cols = landed.shape
    tc = LANES if untiled_rows else cols

    def body(own_ref, a_ref, o_ref):
        acc = own_ref[...].astype(F32)
        for s in range(3):
            acc = acc + a_ref[s].astype(F32)
        if untiled_rows:
            o_ref[:, 0, :] = acc
        else:
            o_ref[...] = acc

    if untiled_rows:
        out_shape, out_spec = jax.ShapeDtypeStruct((h, 1, cols), F32), pl.BlockSpec((h, 1, tc), lambda i: (0, 0, i))
    else:
        out_shape, out_spec = jax.ShapeDtypeStruct((h, cols), F32), pl.BlockSpec((h, tc), lambda i: (0, i))
    return pl.pallas_call(
        body, name=name, out_shape=out_shape, grid=(cols // tc,),
        in_specs=[pl.BlockSpec((h, tc), lambda i: (0, i)), pl.BlockSpec((3, h, tc), lambda i: (0, 0, i))],
        out_specs=out_spec, compiler_params=_params(("arbitrary",)),
    )(own_part, landed)


def _share_halves(halves, name):
    n = len(halves)

    def body(*refs):
        ins, outs = refs[:n], refs[n:2 * n]
        send_sems, recv_sems = refs[2 * n:]
        x, y, c, own, sib, chips, chip_idx = _place()
        cps = [_remote(ins[i], outs[i], send_sems.at[i], recv_sems.at[i], sib) for i in range(n)]
        for cp in cps:
            cp.start()
        for cp in cps:
            cp.wait()

    return pl.pallas_call(
        body, name=name,
        out_shape=[jax.ShapeDtypeStruct(p.shape, p.dtype) for p in halves],
        in_specs=[HBM_SPEC] * n, out_specs=[HBM_SPEC] * n,
        scratch_shapes=[pltpu.SemaphoreType.DMA((n,)), pltpu.SemaphoreType.DMA((n,))],
    )(*halves)


def _allreduce_small(packed):
    rows = packed.shape[0]
    n_dev = 8

    def body(in_ref, out_ref, gath, send_sems, recv_sems):
        x, y, c = lax.axis_index("x"), lax.axis_index("y"), lax.axis_index("c")
        me = 4 * x + 2 * y + c
        gath[me] = in_ref[...]
        cps = []
        for k in range(1, n_dev):
            fx, fy, fc = (k >> 2) & 1, (k >> 1) & 1, k & 1
            to = (x ^ fx, y ^ fy, c ^ fc)
            cps.append(_remote(in_ref, gath.at[me], send_sems.at[k - 1], recv_sems.at[k - 1], to))
        for cp in cps:
            cp.start()
        for k in range(1, n_dev):
            fx, fy, fc = (k >> 2) & 1, (k >> 1) & 1, k & 1
            src = 4 * (x ^ fx) + 2 * (y ^ fy) + (c ^ fc)
            slot = gath.at[src]
            _remote(slot, slot, send_sems.at[k - 1], recv_sems.at[k - 1], (x, y, c)).wait_recv()
        for cp in cps:
            cp.wait_send()
        acc = gath[0]
        for d in range(1, n_dev):
            acc = acc + gath[d]
        out_ref[...] = acc

    vm = pl.BlockSpec(memory_space=pltpu.VMEM)
    return pl.pallas_call(
        body, name="allreduce_small", out_shape=jax.ShapeDtypeStruct(packed.shape, F32),
        in_specs=[vm], out_specs=vm,
        scratch_shapes=[pltpu.VMEM((n_dev, rows, LANES), F32),
                        pltpu.SemaphoreType.DMA((n_dev - 1,)), pltpu.SemaphoreType.DMA((n_dev - 1,))],
    )(packed)


def _adam(col, w, g, m, v):
    m2 = ADAM_B1 * m + (1.0 - ADAM_B1) * g
    v2 = ADAM_B2 * v + (1.0 - ADAM_B2) * (g * g)
    m_hat = m2 / (1.0 - ADAM_B1 ** ADAM_STEP)
    v_hat = v2 / (1.0 - ADAM_B2 ** ADAM_STEP)
    delta = -ADAM_LR * (m_hat / (jnp.sqrt(v_hat) + ADAM_EPS) + ADAM_WD * w)
    return delta, m2, v2


def _adam_call(w, g, m, v, name):
    rows, cols = w.shape
    tm = rows
    for cand in (256, 352, 176, 128, 64, 48, 16, 8):
        if rows % cand == 0:
            tm = cand
            break
    return _tiles(_adam, name=name, rows=rows, tm=tm,
                  row_ins=[(w, cols, 0), (g, cols, 0), (m, cols, 0), (v, cols, 0)],
                  row_outs=[(cols, F32)] * 3)


def _adam_big(w, g_mine, g_other, m, v, place, name):
    rows, cols = w.shape
    tc = 256
    nt = cols // 2 // tc

    def body(place_ref, w_ref, gm_ref, go_ref, m_ref, v_ref, g_out, d_out, m_out, v_out):
        g = jnp.where(pl.program_id(0) == place_ref[0], gm_ref[...], go_ref[...])
        d, m2, v2 = _adam(None, w_ref[...], g, m_ref[...], v_ref[...])
        g_out[...] = g
        d_out[...] = d
        m_out[...] = m2
        v_out[...] = v2

    full = pl.BlockSpec((rows, tc), lambda hh, i, p: (0, hh * nt + i))
    half = pl.BlockSpec((rows, tc), lambda hh, i, p: (0, i))
    return pl.pallas_call(
        body, name=name, out_shape=[jax.ShapeDtypeStruct(w.shape, F32)] * 4,
        grid_spec=pltpu.PrefetchScalarGridSpec(
            num_scalar_prefetch=1, grid=(2, nt),
            in_specs=[full, half, half, full, full], out_specs=[full] * 4),
        compiler_params=_params(("arbitrary", "arbitrary")),
    )(place, w, g_mine, g_other, m, v)


def _adam_untiled_rows(w, g_mine, g_other, m, v, place, name):
    rows, _, cols = w.shape
    tc = 256
    nt = cols // 2 // tc
    rb = next(r for r in (206, 128, 103, rows) if rows % r == 0)

    def body(place_ref, w_ref, gm_ref, go_ref, m_ref, v_ref, g_out, d_out, m_out, v_out):
        g = jnp.where(pl.program_id(0) == place_ref[0], gm_ref[...], go_ref[...])
        d, m2, v2 = _adam(None, w_ref[...], g, m_ref[...], v_ref[...])
        g_out[...] = g
        d_out[...] = d
        m_out[...] = m2
        v_out[...] = v2

    full = pl.BlockSpec((rb, 1, tc), lambda hh, i, r, p: (r, 0, hh * nt + i))
    half = pl.BlockSpec((rb, 1, tc), lambda hh, i, r, p: (r, 0, i))
    return pl.pallas_call(
        body, name=name, out_shape=[jax.ShapeDtypeStruct(w.shape, F32)] * 4,
        grid_spec=pltpu.PrefetchScalarGridSpec(
            num_scalar_prefetch=1, grid=(2, nt, rows // rb),
            in_specs=[full, half, half, full, full], out_specs=[full] * 4),
        compiler_params=_params(("arbitrary", "arbitrary", "arbitrary")),
    )(place, w, g_mine, g_other, m, v)


def _pack(arrays, zero=None):
    flat = []
    for a in arrays:
        a = a.reshape(-1).astype(F32)
        if zero is not None:
            a = a + zero
        flat.append(jnp.pad(a, (0, (-a.size) % LANES)))
    out = jnp.concatenate(flat)
    out = jnp.pad(out, (0, (-out.size) % (8 * LANES)))
    return out.reshape(-1, LANES)


def _unpack(packed, shapes):
    flat = packed.reshape(-1)
    out, off = [], 0
    for s in shapes:
        size = int(np.prod(s))
        out.append(flat[off:off + size].reshape(s))
        off += size + (-size) % LANES
    return out


def kernel(x, norm1_w, w_in, gdn_conv_w, gdn_A_log, gdn_dt_bias, gdn_out_norm_w, fox_f_bias, fox_q_norm_w, fox_k_norm_w, w_out, norm2_w, w_ffn_gate, w_ffn_up, w_ffn_down, final_norm_w, loss_target, m_norm1_w, m_w_in, m_gdn_conv_w, m_gdn_A_log, m_gdn_dt_bias, m_gdn_out_norm_w, m_fox_f_bias, m_fox_q_norm_w, m_fox_k_norm_w, m_w_out, m_norm2_w, m_w_ffn_gate, m_w_ffn_up, m_w_ffn_down, m_final_norm_w, v_norm1_w, v_w_in, v_gdn_conv_w, v_gdn_A_log, v_gdn_dt_bias, v_gdn_out_norm_w, v_fox_f_bias, v_fox_q_norm_w, v_fox_k_norm_w, v_w_out, v_norm2_w, v_w_ffn_gate, v_w_ffn_up, v_w_ffn_down, v_final_norm_w):
    cx, cy, cc = lax.axis_index("x"), lax.axis_index("y"), lax.axis_index("c")
    own = 2 * cx + cy
    place = jnp.stack([cc, own]).astype(jnp.int32)

    names = ["w_in", "w_out", "w_gate", "w_up", "w_down"]
    is_t = [True, False, True, True, False]
    to_t = lambda a, t: a[0].T if t else a[0]
    from_t = lambda a, t: (a.T if t else a)[None]
    big_w = [to_t(a, t) for a, t in zip([w_in, w_out, w_ffn_gate, w_ffn_up, w_ffn_down], is_t)]
    big_m = [to_t(a, t) for a, t in zip([m_w_in, m_w_out, m_w_ffn_gate, m_w_ffn_up, m_w_ffn_down], is_t)]
    big_v = [to_t(a, t) for a, t in zip([v_w_in, v_w_out, v_w_ffn_gate, v_w_ffn_up, v_w_ffn_down], is_t)]
    shards = [big_w[0].astype(BF16)]
    small_w = [norm1_w, gdn_conv_w, gdn_A_log, gdn_dt_bias, gdn_out_norm_w, fox_f_bias, fox_q_norm_w,
               fox_k_norm_w, norm2_w, final_norm_w]
    small_m = [m_norm1_w, m_gdn_conv_w, m_gdn_A_log, m_gdn_dt_bias, m_gdn_out_norm_w, m_fox_f_bias,
               m_fox_q_norm_w, m_fox_k_norm_w, m_norm2_w, m_final_norm_w]
    small_v = [v_norm1_w, v_gdn_conv_w, v_gdn_A_log, v_gdn_dt_bias, v_gdn_out_norm_w, v_fox_f_bias,
               v_fox_q_norm_w, v_fox_k_norm_w, v_norm2_w, v_final_norm_w]
    first = _split_start("gather_in_start", _in_proj_plan, [shards[0], gdn_conv_w[0]],
                         [jax.ShapeDtypeStruct((N_CHIPS,) + shards[0].shape, BF16),
                          jax.ShapeDtypeStruct((N_CHIPS, CONV_K, 3 * WIDTH // N_CHIPS), F32)],
                         n_copies=8, after=shards[0])
    small_packed = [_pack(p, first["token"][0, 0]) for p in (small_w, small_m, small_v)]
    shards += [(w + first["token"][0, 0]).astype(BF16) for w in big_w[1:]]
    rest = {}

    def first_weights(after):
        w_in_g, conv_g = _split_wait("gather_in_wait", _in_proj_plan, first, [after] + small_packed)
        w_in_g = _forward_halves(w_in_g)
        rest.update(_split_start("gather_rest_start", _gather_plan, shards[1:],
                                 [jax.ShapeDtypeStruct((N_CHIPS,) + s.shape, BF16) for s in shards[1:]],
                                 n_copies=4 * len(shards[1:]), after=w_in_g))
        w_cat = _cat_weights(w_in_g.reshape(D_IN, D_MODEL))
        return w_cat + rest["token"][0, 0].astype(BF16), conv_g.transpose(1, 0, 2).reshape(CONV_K, 3 * WIDTH)

    def late_weights(after):
        w_out_g, w_gate_g, w_up_g, w_down_g = _split_wait("gather_rest_wait", _gather_plan, rest, after)
        return w_out_g.reshape(D_MODEL, D_MODEL), w_gate_g, w_up_g, w_down_g

    def start_reduction(stacks, nms, tag):
        landed = _swap_halves(stacks, "rs_swap_" + tag)
        added = [_add_half(s, l, place, "rs_add_" + nm) for s, l, nm in zip(stacks, landed, nms)]
        parts = [a[0] for a in added]
        started = _split_start("exchange_" + tag + "_start", _exchange_plan, parts,
                               [jax.ShapeDtypeStruct((3,) + p.shape[1:], p.dtype) for p in parts],
                               n_copies=3 * len(parts), after=parts[0])
        return dict(own=[a[1] for a in added], started=started, tag=tag, names=nms)

    def finish_reduction(red, after, updates):
        landed = _split_wait("exchange_" + red["tag"] + "_wait", _exchange_plan, red["started"], after)
        halves = [_sum_partials(o, p, "rs_sum_" + nm, untiled_rows=nm == "w_in")
                  for o, p, nm in zip(red["own"], landed, red["names"])]
        others = _share_halves(halves, "rs_share_" + red["tag"])
        return [upd(gm, go) for upd, gm, go in zip(updates, halves, others)]

    def transport_update(b):
        def upd(gm, go):
            res = _adam_big(big_w[b], gm, go, big_m[b], big_v[b], place, "adam_" + names[b])
            early_done.append(res[1])
            return [from_t(a, is_t[b]) for a in res]
        return upd

    early_done = []

    def w_in_update(gm, go):
        rows3 = lambda a: jnp.transpose(a, (2, 0, 1))
        res = _adam_untiled_rows(rows3(w_in), gm, go, rows3(m_w_in), rows3(v_w_in), place, "adam_w_in")
        return [jnp.transpose(a, (1, 2, 0)) for a in res]

    early = {}

    def early_grads_ready(g_out, g_gate, g_up, g_down):
        stacks = [g_out.reshape(N_CHIPS, D_MODEL // N_CHIPS, D_MODEL), g_gate, g_up, g_down]
        early.update(start_reduction(stacks, names[1:], "early"))
        return early["started"]["token"][0, 0]

    grad_x, g_cat, _, _, _, _, small = _local_step(
        x[0], loss_target[0], norm1_w + first["token"][0, 0], gdn_A_log[0], gdn_dt_bias[0],
        gdn_out_norm_w[0], fox_f_bias[0], fox_q_norm_w[0], fox_k_norm_w[0], norm2_w, final_norm_w.reshape(1, -1),
        first_weights, late_weights, early_grads_ready)

    late = start_reduction([_uncat_grad(g_cat).reshape(N_CHIPS, D_IN // N_CHIPS, D_MODEL)], names[:1], "w_in")
    big_upd = finish_reduction(early, late["started"]["token"], [transport_update(b) for b in range(1, 5)])

    order = ["norm1_w", "conv_w", "a_log", "dt_bias", "out_norm_w", "f_bias", "q_norm_w", "k_norm_w",
             "norm2_w", "final_w"]
    red = _allreduce_small(_pack([small[k] for k in order] + [small["loss"]]))
    red_shapes = [(1, D_MODEL), (CONV_K, 3 * WIDTH), (1, HEADS), (1, HEADS), (1, HEAD_DIM), (1, HEADS),
                  (1, HEAD_DIM), (1, HEAD_DIM), (1, D_MODEL), (D_MODEL,), ()]
    red_list = _unpack(red, red_shapes)
    loss = red_list[-1]
    small_g = dict(zip(order, red_list[:-1]))
    shard_cols = 3 * WIDTH // N_CHIPS
    small_g["conv_w"] = lax.dynamic_slice_in_dim(small_g["conv_w"], own * shard_cols, shard_cols, axis=1)[None]
    small_gl = [small_g[k].reshape(w.shape) for k, w in zip(order, small_w)]
    s_delta, s_m, s_v = _adam_call(small_packed[0], _pack(small_gl), small_packed[1], small_packed[2], "adam_small")
    big_upd = finish_reduction(late, [s_delta] + early_done, [w_in_update]) + big_upd
    shapes = [w.shape for w in small_w]
    s_delta, s_m, s_v = _unpack(s_delta, shapes), _unpack(s_m, shapes), _unpack(s_v, shapes)

    big_pos = {1: 0, 9: 1, 11: 2, 12: 3, 13: 4}
    small_pos = {0: 0, 2: 1, 3: 2, 4: 3, 5: 4, 6: 5, 7: 6, 8: 7, 10: 8, 14: 9}
    grads, deltas, new_m, new_v = [], [], [], []
    for pos in range(15):
        if pos in big_pos:
            b = big_pos[pos]
            g, d, m2, v2 = big_upd[b]
            grads.append(g)
            deltas.append(d)
            new_m.append(m2)
            new_v.append(v2)
        else:
            s = small_pos[pos]
            grads.append(small_gl[s])
            deltas.append(s_delta[s])
            new_m.append(s_m[s])
            new_v.append(s_v[s])
    return (loss, grad_x[None], *grads, *deltas, *new_m, *new_v)
```

```python
import jax
import jax.numpy as jnp
import numpy as np
from jax import lax
from jax.experimental import pallas as pl
from jax.experimental.pallas import tpu as pltpu

F32 = jnp.float32
BF16 = jnp.bfloat16

D_MODEL = 1024
HEADS = 8
HEAD_DIM = 64
PAIRS = HEADS // 2
WIDTH = HEADS * HEAD_DIM
CHUNK = 64
CONV_K = 4
D_FF = 2816
FF_SHARD = D_FF // 4
EPS = 1e-6
SCALE = HEAD_DIM ** -0.5
LANES = 128
N_CHIPS = 4
D_IN = 4120
D_CAT = 4224
COL_SMALL = 4096 // LANES

ADAM_LR = 0.001
ADAM_B1 = 0.9
ADAM_B2 = 0.999
ADAM_EPS = 1e-08
ADAM_WD = 0.01
ADAM_STEP = 10

VMEM_LIMIT = 56 * 1024 * 1024
MESH = pl.DeviceIdType.MESH
HIGHEST = lax.Precision.HIGHEST


def _params(sem):
    return pltpu.CompilerParams(dimension_semantics=sem, vmem_limit_bytes=VMEM_LIMIT)


_CONTRACT = {"nn": ((1,), (0,)), "nt": ((1,), (1,)), "tn": ((0,), (0,))}


def _mm(a, b, *, dims, name, out_dtype=F32, add=None, tm=1024, tn=512, tk=512):
    if dims == "nn":
        (m, k), (k2, n) = a.shape, b.shape
    elif dims == "nt":
        (m, k), (n, k2) = a.shape, b.shape
    else:
        (k, m), (k2, n) = a.shape, b.shape
    assert k == k2, (a.shape, b.shape, dims)
    tm, tn, tk = min(tm, m), min(tn, n), min(tk, k)
    assert m % tm == 0 and n % tn == 0 and k % tk == 0, (m, n, k, tm, tn, tk)
    nk = k // tk
    a_spec = (pl.BlockSpec((tk, tm), lambda i, j, kk: (kk, i)) if dims == "tn"
              else pl.BlockSpec((tm, tk), lambda i, j, kk: (i, kk)))
    b_spec = (pl.BlockSpec((tn, tk), lambda i, j, kk: (j, kk)) if dims == "nt"
              else pl.BlockSpec((tk, tn), lambda i, j, kk: (kk, j)))
    o_spec = pl.BlockSpec((tm, tn), lambda i, j, kk: (i, j))
    contract = (_CONTRACT[dims], ((), ()))
    has_add = add is not None

    def body(*refs):
        a_ref, b_ref = refs[:2]
        add_ref = refs[2] if has_add else None
        o_ref = refs[3] if has_add else refs[2]
        part = lax.dot_general(a_ref[...].astype(BF16), b_ref[...].astype(BF16), contract,
                               preferred_element_type=F32)

        def finish(r):
            if has_add:
                r = r + add_ref[...].astype(F32)
            o_ref[...] = r.astype(out_dtype)

        if nk == 1:
            finish(part)
            return
        acc = refs[-1]
        kk = pl.program_id(2)

        @pl.when(kk == 0)
        def _():
            acc[...] = part

        @pl.when(kk > 0)
        def _():
            acc[...] += part

        @pl.when(kk == nk - 1)
        def _():
            finish(acc[...])

    ins = [a, b] + ([add] if has_add else [])
    in_specs = [a_spec, b_spec] + ([o_spec] if has_add else [])
    return pl.pallas_call(
        body, name=name, grid=(m // tm, n // tn, nk),
        in_specs=in_specs, out_specs=o_spec,
        out_shape=jax.ShapeDtypeStruct((m, n), out_dtype),
        scratch_shapes=[pltpu.VMEM((tm, tn), F32)] if nk > 1 else [],
        compiler_params=_params(("parallel", "parallel", "arbitrary")),
    )(*ins)


def _mm_blocks(a, b, *, name, grid, a_spec, b_spec, o_spec, out_shape, dims, n_sum=0, add=None, add_spec=None,
               epilogue=None, extra=(), n_acc=0):
    contract = (_CONTRACT[dims], ((), ()))
    has_add = add is not None
    n_in = 2 + has_add + len(extra)

    def body(*refs):
        a_ref, b_ref = refs[:2]
        dot = lambda x, y: lax.dot_general(x.astype(BF16), y.astype(BF16), contract, preferred_element_type=F32)
        if n_sum:
            r = dot(a_ref[0], b_ref[0])
            for s in range(1, n_sum):
                r = r + dot(a_ref[s], b_ref[s])
        else:
            r = dot(a_ref[...], b_ref[...])
        if has_add:
            r = r + refs[2][...].astype(F32)
        if epilogue is None:
            refs[-1][...] = r.astype(refs[-1].dtype)
        else:
            outs = epilogue(r, *[e[...] for e in refs[2 + has_add:n_in]])
            out_refs = refs[n_in:]
            n_plain = len(out_refs) - n_acc
            for o_ref, val in zip(out_refs[:n_plain], outs):
                o_ref[...] = val.astype(o_ref.dtype)
            if n_acc:
                @pl.when(pl.program_id(0) == 0)
                def _():
                    for o_ref in out_refs[n_plain:]:
                        o_ref[...] = jnp.zeros_like(o_ref)
                for o_ref, val in zip(out_refs[n_plain:], outs[n_plain:]):
                    o_ref[...] += val

    ins = [a, b] + ([add] if has_add else []) + [e[0] for e in extra]
    in_specs = [a_spec, b_spec] + ([add_spec] if has_add else []) + [e[1] for e in extra]
    sem = ("arbitrary",) + ("parallel",) * (len(grid) - 1) if n_acc else ("parallel",) * len(grid)
    return pl.pallas_call(
        body, name=name, grid=grid, in_specs=in_specs, out_specs=o_spec, out_shape=out_shape,
        compiler_params=_params(sem),
    )(*ins)


def _tiles(fn, *, name, rows, tm, ncol=1, row_ins=(), col_consts=(), full_consts=(),
           row_outs=(), acc_outs=()):
    nt = rows // tm
    assert rows % tm == 0
    n_full, n_col, n_row = len(full_consts), len(col_consts), len(row_ins)
    n_ro, n_acc = len(row_outs), len(acc_outs)

    def body(*refs):
        ins = refs[:n_full + n_col + n_row]
        outs = refs[n_full + n_col + n_row:]
        i = pl.program_id(1)
        res = fn(pl.program_id(0), *[r[...] for r in ins])
        for r, v in zip(outs[:n_ro], res[:n_ro]):
            r[...] = v.astype(r.dtype)
        if n_acc:
            @pl.when(i == 0)
            def _():
                for r in outs[n_ro:]:
                    r[...] = jnp.zeros_like(r)
            for r, v in zip(outs[n_ro:], res[n_ro:]):
                r[...] += v

    in_specs = [pl.BlockSpec(a.shape, lambda j, i, nd=a.ndim: (0,) * nd) for a in full_consts]
    in_specs += [pl.BlockSpec((nr, w), lambda j, i, o=o: (0, o + j)) for (_, nr, w, o) in col_consts]
    in_specs += [pl.BlockSpec((tm, w), lambda j, i, o=o: (i, o + j)) for (_, w, o) in row_ins]
    out_specs = [pl.BlockSpec((tm, w), lambda j, i: (i, j)) for (w, _) in row_outs]
    out_specs += [pl.BlockSpec((nr, w), lambda j, i: (0, j)) for (nr, w) in acc_outs]
    out_shape = [jax.ShapeDtypeStruct((rows, w * ncol), dt) for (w, dt) in row_outs]
    out_shape += [jax.ShapeDtypeStruct((nr, w * ncol), F32) for (nr, w) in acc_outs]
    args = list(full_consts) + [c[0] for c in col_consts] + [r[0] for r in row_ins]
    out = pl.pallas_call(
        body, name=name, grid=(ncol, nt), in_specs=in_specs, out_specs=out_specs, out_shape=out_shape,
        compiler_params=_params(("parallel", "arbitrary")),
    )(*args)
    return out


def _rms(x, w):
    return x * lax.rsqrt(jnp.mean(x * x, axis=-1, keepdims=True) + EPS) * w


def _lane_lo(shape):
    return lax.broadcasted_iota(jnp.int32, shape, len(shape) - 1) < HEAD_DIM


def _pair_sum(x):
    lo = _lane_lo(x.shape)
    s0 = jnp.sum(jnp.where(lo, x, 0.0), axis=-1, keepdims=True)
    s1 = jnp.sum(jnp.where(lo, 0.0, x), axis=-1, keepdims=True)
    return jnp.where(lo, s0, s1)


def _head_col(x, lo, h):
    keep = lo if h == 0 else jnp.logical_not(lo)
    return jnp.max(jnp.where(keep, x, -jnp.inf), axis=-1, keepdims=True)


def _softplus(x):
    return jnp.maximum(x, 0.0) + jnp.log1p(jnp.exp(-jnp.abs(x)))


def _silu(x):
    return x * jax.nn.sigmoid(x)


def _dot(a, b, contract):
    return lax.dot_general(a.astype(BF16), b.astype(BF16), (contract, ((), ())),
                           preferred_element_type=F32)


def _dot32(a, b, contract):
    return lax.dot_general(a, b, (contract, ((), ())), precision=HIGHEST, preferred_element_type=F32)


def _bd(y):
    yy = jnp.concatenate([y, y], axis=0)
    r = lax.broadcasted_iota(jnp.int32, yy.shape, 0) < HEAD_DIM
    c = lax.broadcasted_iota(jnp.int32, yy.shape, 1) < HEAD_DIM
    return jnp.where(r == c, yy, 0.0)


def _pp(x, y):
    return _dot(x, _bd(y), _CONTRACT["nn"])


def _pp_nt(x, y):
    return _dot(x, _bd(y), _CONTRACT["nt"])


def _pp_tn(x, y):
    full = _dot(x, y, _CONTRACT["tn"])
    return jnp.where(_lane_lo((HEAD_DIM, LANES)), full[:HEAD_DIM], full[HEAD_DIM:])


def _gdn_masks():
    row = lax.broadcasted_iota(jnp.int32, (CHUNK, LANES), 0)
    col = lax.broadcasted_iota(jnp.int32, (CHUNK, LANES), 1) % HEAD_DIM
    return row, col


def _interleave(chains):
    live = list(chains)
    while live:
        for g in list(live):
            try:
                next(g)
            except StopIteration:
                live.remove(g)


def _gdn_forward(qkv, betax, gcx, grow, rows):
    nchunk = rows // CHUNK

    def body(q_ref, k_ref, v_ref, bx_ref, gx_ref, gr_ref, o_ref, ss_ref, ts_ref, state):
        n = pl.program_id(0)

        @pl.when(n == 0)
        def _():
            state[...] = jnp.zeros_like(state)

        row, col = _gdn_masks()
        incl, strict = col <= row, col < row

        def chain(p):
            lanes = pl.ds(p * LANES, LANES)
            q, k, v, bx, gx = q_ref[:, lanes], k_ref[:, lanes], v_ref[:, lanes], bx_ref[:, lanes], gx_ref[:, lanes]
            gr = gr_ref[0, p]
            glast = gx_ref[pl.ds(CHUNK - 1, 1), lanes]
            s = state[p]
            dm = jnp.where(incl, jnp.exp(jnp.minimum(gx - gr, 0.0)), 0.0)
            kb, vb, eg, qs = k * bx, v * bx, jnp.exp(gx), q * SCALE
            yield
            big_g, big_p = _pp_nt(kb, k), _pp_nt(qs, k)
            yield
            x = -jnp.where(strict, big_g * dm, 0.0)
            att = jnp.where(incl, big_p * dm, 0.0)
            tm = jnp.where(row == col, 1.0, 0.0) + x
            x = _pp(x, x)
            yield
            for _ in range(4):
                step, x = _pp(tm, x), _pp(x, x)
                yield
                tm = tm + step
            tm = tm + _pp(tm, x)
            yield
            u, w = _pp(tm, vb), _pp(tm, kb * eg)
            yield
            ws, qgs = _pp(w, s), _pp(qs * eg, s)
            yield
            vn = u - ws
            kd = k * jnp.exp(glast - gx)
            avn, upd = _pp(att, vn), _pp_tn(kd, vn)
            yield
            ss_ref[0, p] = s
            ts_ref[0, p] = tm
            o_ref[:, lanes] = qgs + avn
            state[p] = s * jnp.exp(glast) + upd

        _interleave([chain(p) for p in range(PAIRS)])

    blk = lambda j: pl.BlockSpec((CHUNK, WIDTH), lambda n, j=j: (n, j))
    sv = pl.BlockSpec((1, PAIRS, CHUNK, LANES), lambda n: (n, 0, 0, 0))
    return pl.pallas_call(
        body, name="gdn_fwd", grid=(nchunk,),
        in_specs=[blk(0), blk(1), blk(2), blk(0), blk(0),
                  pl.BlockSpec((1, PAIRS, 1, LANES), lambda n: (n, 0, 0, 0))],
        out_specs=[blk(0), sv, sv],
        out_shape=[jax.ShapeDtypeStruct((rows, WIDTH), F32),
                   jax.ShapeDtypeStruct((nchunk, PAIRS, CHUNK, LANES), F32),
                   jax.ShapeDtypeStruct((nchunk, PAIRS, CHUNK, LANES), F32)],
        scratch_shapes=[pltpu.VMEM((PAIRS, CHUNK, LANES), F32)],
        compiler_params=_params(("arbitrary",)),
    )(qkv, qkv, qkv, betax, gcx, grow)


def _gdn_backward(qkv, betax, gcx, grow, ssave, tsave, do, rows):
    nchunk = rows // CHUNK

    def body(q_ref, k_ref, v_ref, bx_ref, gx_ref, gr_ref, ss_ref, ts_ref, do_ref,
             dq_ref, dk_ref, dv_ref, dbx_ref, dgx_ref, dgr_ref, dstate):
        n = pl.program_id(0)

        @pl.when(n == 0)
        def _():
            dstate[...] = jnp.zeros_like(dstate)

        row, col = _gdn_masks()
        incl, strict = col <= row, col < row

        def chain(p):
            lanes = pl.ds(p * LANES, LANES)
            q, k, v, bx, gx = q_ref[:, lanes], k_ref[:, lanes], v_ref[:, lanes], bx_ref[:, lanes], gx_ref[:, lanes]
            gr = gr_ref[0, p]
            glast = gx_ref[pl.ds(CHUNK - 1, 1), lanes]
            s, tm, d_o = ss_ref[0, p], ts_ref[0, p], do_ref[:, lanes]
            ds_out = dstate[p]
            dm = jnp.where(incl, jnp.exp(jnp.minimum(gx - gr, 0.0)), 0.0)
            kb, vb, eg, qs = k * bx, v * bx, jnp.exp(gx), q * SCALE
            kbg, qg = kb * eg, qs * eg
            ed = jnp.exp(glast - gx)
            kd = k * ed
            eglast = jnp.exp(glast)
            yield
            big_g, big_p = _pp_nt(kb, k), _pp_nt(qs, k)
            u, w = _pp(tm, vb), _pp(tm, kbg)
            dqg, kds = _pp_nt(d_o, s), _pp(kd, ds_out)
            yield
            low = jnp.where(strict, big_g * dm, 0.0)
            att = jnp.where(incl, big_p * dm, 0.0)
            ws, atd = _pp(w, s), _pp_tn(att, d_o)
            yield
            vn = u - ws
            dvn = kds + atd
            dkd, datt_raw = _pp_nt(vn, ds_out), _pp_nt(d_o, vn)
            dw_neg, dvb = _pp_nt(dvn, s), _pp_tn(tm, dvn)
            dtm_a, wdv = _pp_nt(dvn, vb), _pp_tn(w, dvn)
            qgd = _pp_tn(qg, d_o)
            yield
            datt = jnp.where(incl, datt_raw, 0.0)
            dw = -dw_neg
            dtm_b, dkbg = _pp_nt(dw, kbg), _pp_tn(tm, dw)
            dbig_p = datt * dm
            dqs_a, dk_p = _pp(dbig_p, k), _pp_tn(dbig_p, qs)
            yield
            inner = _pp_tn(tm, dtm_a + dtm_b)
            yield
            dlow = jnp.where(strict, -_pp_nt(inner, tm), 0.0)
            yield
            dbig_g = dlow * dm
            dkb_a, dk_g = _pp(dbig_g, k), _pp_tn(dbig_g, kb)
            yield
            dkb = dkb_a + dkbg * eg
            dqs = dqs_a + dqg * eg
            dk = dk_g + dk_p + dkd * ed + dkb * bx
            z = dlow * low + datt * att
            kdterm = dkd * kd
            dglast = (jnp.sum(ds_out * s, axis=0, keepdims=True) * eglast
                      + jnp.sum(kdterm, axis=0, keepdims=True))
            dgx = dqg * qg + dkbg * kbg - kdterm
            dgx = dgx + jnp.where(col == 0, _pair_sum(z), 0.0)
            dgx = dgx + jnp.where(row == CHUNK - 1, dglast, 0.0)
            dq_ref[:, lanes] = dqs * SCALE
            dk_ref[:, lanes] = dk
            dv_ref[:, lanes] = dvb * bx
            dbx_ref[:, lanes] = dkb * k + dvb * v
            dgx_ref[:, lanes] = dgx
            dgr_ref[0, p] = -jnp.sum(z, axis=0, keepdims=True)
            dstate[p] = ds_out * eglast + qgd - wdv

        _interleave([chain(p) for p in range(PAIRS)])

    last = nchunk - 1
    blk = lambda j: pl.BlockSpec((CHUNK, WIDTH), lambda n, j=j: (last - n, j))
    sv = pl.BlockSpec((1, PAIRS, CHUNK, LANES), lambda n: (last - n, 0, 0, 0))
    gr_spec = pl.BlockSpec((1, PAIRS, 1, LANES), lambda n: (last - n, 0, 0, 0))
    wide = jax.ShapeDtypeStruct((rows, WIDTH), F32)
    return pl.pallas_call(
        body, name="gdn_bwd", grid=(nchunk,),
        in_specs=[blk(0), blk(1), blk(2), blk(0), blk(0), gr_spec, sv, sv, blk(0)],
        out_specs=[blk(0)] * 5 + [gr_spec],
        out_shape=[wide] * 5 + [jax.ShapeDtypeStruct((nchunk, PAIRS, 1, LANES), F32)],
        scratch_shapes=[pltpu.VMEM((PAIRS, CHUNK, LANES), F32)],
        compiler_params=_params(("arbitrary",)),
    )(qkv, qkv, qkv, betax, gcx, grow, ssave, tsave, do)


ATT_TQ = 256


def _att_scores(qh, kt, fk, diag):
    s = _dot(qh, kt, _CONTRACT["nt"]) - fk
    if diag:
        r = lax.broadcasted_iota(jnp.int32, s.shape, 0)
        c = lax.broadcasted_iota(jnp.int32, s.shape, 1)
        s = jnp.where(r >= c, s, -jnp.inf)
    return s


def _head_masks(n):
    lo = _lane_lo((n, LANES))
    return [lo, jnp.logical_not(lo)]


def _attention_forward(fqk, proj, frow, rows):
    tq = tk = min(ATT_TQ, rows)
    nq = rows // tq
    v_off = 3072 // LANES

    def body(q_ref, k_ref, v_ref, fr_ref, o_ref, lse_ref):
        qi = pl.program_id(1)
        q = q_ref[...] * SCALE
        keep_q, keep_k = _head_masks(tq), _head_masks(tk)
        qh = [jnp.where(keep_q[h], q, 0.0).astype(BF16) for h in range(2)]

        def tile(ki, carry, diag):
            k0 = pl.multiple_of(ki * tk, tk)
            kt = k_ref[pl.ds(k0, tk), :].astype(BF16)
            v_t = v_ref[pl.ds(k0, tk), :]
            out = [None, None]

            def chain(h):
                m, l, acc = carry[h]
                vt = jnp.where(keep_k[h], v_t, 0.0).astype(BF16)
                yield
                s = _att_scores(qh[h], kt, fr_ref[0, pl.ds(h, 1), pl.ds(k0, tk)], diag)
                yield
                m_new = jnp.maximum(m, jnp.max(s, axis=-1, keepdims=True))
                p = jnp.exp(s - m_new)
                alpha = jnp.exp(m - m_new)
                l = alpha * l + jnp.sum(p, axis=-1, keepdims=True)
                p_hi = p.astype(BF16)
                p_lo = p - p_hi.astype(F32)
                yield
                out[h] = (m_new, l, alpha * acc + _dot(p_hi, vt, _CONTRACT["nn"]) + _dot(p_lo, vt, _CONTRACT["nn"]))

            _interleave([chain(0), chain(1)])
            return tuple(out)

        one = (jnp.full((tq, 1), -jnp.inf, F32), jnp.zeros((tq, 1), F32), jnp.zeros((tq, LANES), F32))
        carry = lax.fori_loop(0, qi, lambda ki, c: tile(ki, c, False), (one, one))
        (m0, l0, acc0), (m1, l1, acc1) = tile(qi, carry, True)
        o_ref[...] = acc0 / l0 + acc1 / l1
        lse_ref[...] = jnp.where(keep_q[0], m0 + jnp.log(l0), m1 + jnp.log(l1))

    whole = lambda off: pl.BlockSpec((rows, LANES), lambda p, i, off=off: (0, off + p))
    qblk = lambda off: pl.BlockSpec((tq, LANES), lambda p, i, off=off: (i, off + p))
    wide = jax.ShapeDtypeStruct((rows, WIDTH), F32)
    return pl.pallas_call(
        body, name="fox_fwd", grid=(PAIRS, nq),
        in_specs=[qblk(0), whole(PAIRS), whole(v_off), pl.BlockSpec((1, 2, rows), lambda p, i: (p, 0, 0))],
        out_specs=[qblk(0), qblk(0)], out_shape=[wide, wide],
        compiler_params=_params(("parallel", "arbitrary")),
    )(fqk, fqk, proj, frow)


def _attention_delta(fqk, proj, frow, lse, dao, rows):
    tq = tk = min(ATT_TQ, rows)
    nq = rows // tq
    v_off = 3072 // LANES

    def body(q_ref, k_ref, v_ref, fr_ref, lse_ref, do_ref, delta_ref):
        qi = pl.program_id(1)
        q, d_o, lse_t = q_ref[...] * SCALE, do_ref[...], lse_ref[...]
        keep_q = _head_masks(tq)
        qh = [jnp.where(keep_q[h], q, 0.0).astype(BF16) for h in range(2)]
        doh = [jnp.where(keep_q[h], d_o, 0.0).astype(BF16) for h in range(2)]
        lse_h = [_head_col(lse_t, keep_q[0], h) for h in range(2)]

        def tile(ki, carry, diag):
            k0 = pl.multiple_of(ki * tk, tk)
            kt = k_ref[pl.ds(k0, tk), :].astype(BF16)
            vt = v_ref[pl.ds(k0, tk), :].astype(BF16)
            out = [None, None]

            def chain(h):
                s = _att_scores(qh[h], kt, fr_ref[0, pl.ds(h, 1), pl.ds(k0, tk)], diag)
                dp = _dot(doh[h], vt, _CONTRACT["nt"])
                yield
                out[h] = carry[h] + jnp.sum(jnp.exp(s - lse_h[h]) * dp, axis=-1, keepdims=True)

            _interleave([chain(0), chain(1)])
            return tuple(out)

        zero = jnp.zeros((tq, 1), F32)
        carry = lax.fori_loop(0, qi, lambda ki, c: tile(ki, c, False), (zero, zero))
        d0, d1 = tile(qi, carry, True)
        delta_ref[...] = jnp.where(keep_q[0], d0, d1)

    whole = lambda off: pl.BlockSpec((rows, LANES), lambda p, i, off=off: (0, off + p))
    qblk = lambda off: pl.BlockSpec((tq, LANES), lambda p, i, off=off: (i, off + p))
    return pl.pallas_call(
        body, name="fox_delta", grid=(PAIRS, nq),
        in_specs=[qblk(0), whole(PAIRS), whole(v_off),
                  pl.BlockSpec((1, 2, rows), lambda p, i: (p, 0, 0)), qblk(0), qblk(0)],
        out_specs=qblk(0), out_shape=jax.ShapeDtypeStruct((rows, WIDTH), F32),
        compiler_params=_params(("parallel", "arbitrary")),
    )(fqk, fqk, proj, frow, lse, dao)


def _attention_backward(fqk, proj, frow, ao, lse, dao, rows):
    tq = tk = min(ATT_TQ, rows)
    nq = rows // tq
    v_off = 3072 // LANES

    def body(q_ref, k_ref, v_ref, fr_ref, o_ref, lse_ref, do_ref, dq_ref, dk_ref, dv_ref, dfr_ref):
        ki = pl.program_id(1)

        @pl.when(ki == 0)
        def _():
            dq_ref[...] = jnp.zeros_like(dq_ref)

        keep_q, keep_k = _head_masks(tq), _head_masks(tk)
        k_t = k_ref[...]
        kt = k_t.astype(BF16)
        vt = v_ref[...].astype(BF16)
        kh = [jnp.where(keep_k[h], k_t, 0.0).astype(BF16) for h in range(2)]
        fk = [fr_ref[0, pl.ds(h, 1), :] for h in range(2)]

        def tile(qi, carry, diag):
            dk, dv, df0, df1 = carry
            rows_q = pl.ds(pl.multiple_of(qi * tq, tq), tq)
            q, d_o, lse_t = q_ref[rows_q, :] * SCALE, do_ref[rows_q, :], lse_ref[rows_q, :]
            delta_x = _pair_sum(d_o.astype(BF16).astype(F32) * o_ref[rows_q, :])
            res = [None, None]

            def chain(h):
                qh = jnp.where(keep_q[h], q, 0.0).astype(BF16)
                doh = jnp.where(keep_q[h], d_o, 0.0).astype(BF16)
                lse_h, delta_h = _head_col(lse_t, keep_q[0], h), _head_col(delta_x, keep_q[0], h)
                yield
                s, dp = _att_scores(qh, kt, fk[h], diag), _dot(doh, vt, _CONTRACT["nt"])
                yield
                p = jnp.exp(s - lse_h)
                ds = p * (dp - delta_h)
                yield
                res[h] = (_dot(p, doh, _CONTRACT["tn"]), _dot(ds, qh, _CONTRACT["tn"]),
                          _dot(ds, kh[h], _CONTRACT["nn"]), jnp.sum(ds, axis=0, keepdims=True))

            _interleave([chain(0), chain(1)])
            (dv0, dk0, dq0, s0), (dv1, dk1, dq1, s1) = res
            dq_ref[rows_q, :] += (dq0 + dq1) * SCALE
            return dk + dk0 + dk1, dv + dv0 + dv1, df0 - s0, df1 - s1

        zero_kv = jnp.zeros((tk, LANES), F32)
        zero_f = jnp.zeros((1, tk), F32)
        carry = tile(ki, (zero_kv, zero_kv, zero_f, zero_f), True)
        dk, dv, df0, df1 = lax.fori_loop(ki + 1, nq, lambda qi, c: tile(qi, c, False), carry)
        dk_ref[...] = dk
        dv_ref[...] = dv.astype(dv_ref.dtype)
        dfr_ref[0, pl.ds(0, 1), :] = df0
        dfr_ref[0, pl.ds(1, 1), :] = df1

    whole = lambda off: pl.BlockSpec((rows, LANES), lambda p, i, off=off: (0, off + p))
    kblk = lambda off: pl.BlockSpec((tk, LANES), lambda p, i, off=off: (i, off + p))
    fr_spec = pl.BlockSpec((1, 2, tk), lambda p, i: (p, 0, i))
    wide = jax.ShapeDtypeStruct((rows, WIDTH), F32)
    return pl.pallas_call(
        body, name="fox_bwd", grid=(PAIRS, nq),
        in_specs=[whole(0), kblk(PAIRS), kblk(v_off), fr_spec, whole(0), whole(0), whole(0)],
        out_specs=[whole(0), kblk(0), kblk(0), fr_spec],
        out_shape=[wide, wide, jax.ShapeDtypeStruct((rows, WIDTH), BF16),
                   jax.ShapeDtypeStruct((PAIRS, 2, rows), F32)],
        compiler_params=_params(("parallel", "arbitrary")),
    )(fqk, fqk, proj, frow, ao, lse, dao)


def _lane_ids(shape):
    return lax.broadcasted_iota(jnp.int32, shape, len(shape) - 1)


def _gates_elem(a_log, dt_bias, f_bias, pre):
    lane = _lane_ids(pre.shape)
    beta = jax.nn.sigmoid(pre)
    g = -jnp.exp(a_log) * _softplus(pre + dt_bias)
    lf = -_softplus(-(pre + f_bias))
    return jnp.where(lane < 8, beta, jnp.where(lane < 16, g, jnp.where(lane < 24, lf, 0.0)))


def _tri_consts():
    r = np.arange(LANES)[:, None]
    c = np.arange(LANES)[None, :]
    full = (c <= r).astype(np.float32)
    chunked = full * ((r // CHUNK) == (c // CHUNK))
    return jnp.asarray(chunked), jnp.asarray(full)


def _cums_fwd(lc, lf, gates):
    rows = gates.shape[0]
    lane = _lane_ids((LANES, LANES))
    carry = jnp.zeros((1, LANES), F32)
    out = []
    for r in range(rows // LANES):
        blk = gates[r * LANES:(r + 1) * LANES]
        gc = _dot32(lc, blk, _CONTRACT["nn"])
        f = _dot32(lf, blk, _CONTRACT["nn"]) + carry
        carry = carry + jnp.sum(blk, axis=0, keepdims=True)
        out.append(jnp.where((lane >= 8) & (lane < 16), gc, jnp.where((lane >= 16) & (lane < 24), f, 0.0)))
    return jnp.concatenate(out, axis=0)


def _cums_bwd(lc, lf, dcums):
    rows = dcums.shape[0]
    lane = _lane_ids((LANES, LANES))
    is_g = (lane >= 8) & (lane < 16)
    is_f = (lane >= 16) & (lane < 24)
    carry = jnp.zeros((1, LANES), F32)
    out = [None] * (rows // LANES)
    for r in reversed(range(rows // LANES)):
        blk = dcums[r * LANES:(r + 1) * LANES]
        dg = jnp.where(is_g, blk, 0.0)
        df = jnp.where(is_f, blk, 0.0)
        out[r] = _dot32(lc, dg, _CONTRACT["tn"]) + _dot32(lf, df, _CONTRACT["tn"]) + carry
        carry = carry + jnp.sum(df, axis=0, keepdims=True)
    return jnp.concatenate(out, axis=0)


def _expand_consts():
    xb = np.zeros((LANES, WIDTH), np.float32)
    xg = np.zeros((LANES, WIDTH), np.float32)
    for h in range(HEADS):
        xb[h, h * HEAD_DIM:(h + 1) * HEAD_DIM] = 1.0
        xg[8 + h, h * HEAD_DIM:(h + 1) * HEAD_DIM] = 1.0
    return jnp.asarray(xb), jnp.asarray(xg)


def _shift_down(x, s):
    if s == 0:
        return x
    row = lax.broadcasted_iota(jnp.int32, x.shape, 0)
    return jnp.where(row >= s, pltpu.roll(x, s, 0), 0.0)


def _shift_up(x, s):
    if s == 0:
        return x
    n = x.shape[0]
    row = lax.broadcasted_iota(jnp.int32, x.shape, 0)
    return jnp.where(row < n - s, pltpu.roll(x, n - s, 0), 0.0)


def _row_of(cw, i):
    row = lax.broadcasted_iota(jnp.int32, cw.shape, 0)
    return jnp.sum(jnp.where(row == i, cw, 0.0), axis=0, keepdims=True)


def _conv(cw, x):
    c = jnp.zeros_like(x)
    for i in range(CONV_K):
        c = c + _row_of(cw, i) * _shift_down(x, CONV_K - 1 - i)
    return c


def _post_conv(is_qk, c):
    s = _silu(c)
    n = s * lax.rsqrt(_pair_sum(s * s) + EPS)
    return jnp.where(is_qk, n, s)


def _gdn_prep_fwd(col, cw, x):
    return (_post_conv(col < 2 * PAIRS, _conv(cw, x)),)


def _gdn_prep_bwd(is_qk, cw, x, dy):
    c = _conv(cw, x)
    _, vjp = jax.vjp(lambda cc: _post_conv(is_qk, cc), c)
    (dc,) = vjp(dy)
    dx = jnp.zeros_like(x)
    row = lax.broadcasted_iota(jnp.int32, cw.shape, 0)
    dcw = jnp.zeros(cw.shape, F32)
    for i in range(CONV_K):
        s = CONV_K - 1 - i
        dx = dx + _row_of(cw, i) * _shift_up(dc, s)
        dcw = dcw + jnp.where(row == i, jnp.sum(dc * _shift_down(x, s), axis=0, keepdims=True), 0.0)
    return dx, dcw


def _head_rms(w, x):
    return x * lax.rsqrt(_pair_sum(x * x) / HEAD_DIM + EPS) * w


def _cat_weights(w_in_t):
    tail = jnp.pad(w_in_t[4112:4120], ((0, D_CAT - D_IN), (0, 0)))
    return jnp.concatenate([w_in_t[:2048], w_in_t[2064:4112], w_in_t[2048:2064], tail], axis=0)


def _uncat_grad(g):
    return jnp.concatenate([g[:2048], g[4096:4112], g[2048:4096], g[4112:4120]], axis=0)


def _lanes_to_rowform(v8, rows):
    return v8.reshape(rows // CHUNK, CHUNK, HEADS).transpose(0, 2, 1).reshape(rows // CHUNK, PAIRS, 1, LANES)


def _rowform_to_lanes(v, rows):
    return v.reshape(rows // CHUNK, HEADS, CHUNK).transpose(0, 2, 1).reshape(rows, HEADS)


def _local_step(x, target, norm1_w, a_log, dt_bias, out_norm_w, f_bias, q_norm_w, k_norm_w,
                norm2_w, final_w, first_weights, late_weights, early_grads_ready):
    rows = x.shape[0]
    tm = min(512, rows)
    lc, lf = _tri_consts()
    xb, xg = _expand_consts()

    (h1,) = _tiles(lambda col, w, xx: (_rms(xx, w),), name="norm1", rows=rows, tm=tm,
                   full_consts=[norm1_w], row_ins=[(x, D_MODEL, 0)], row_outs=[(D_MODEL, BF16)])
    w_cat, conv_w = first_weights(h1)
    proj = _mm(h1, w_cat, dims="nt", name="in_proj", tn=1408, tk=1024)

    lane_pad = lambda v, off: jnp.pad(v.reshape(1, -1), ((0, 0), (off, LANES - off - v.size)))
    p_a, p_dt, p_fb = lane_pad(a_log, 8), lane_pad(dt_bias, 8), lane_pad(f_bias, 16)

    def gates_fwd(col, lcv, lfv, a, dt, fb, pre):
        gates = _gates_elem(a, dt, fb, pre)
        return gates, _cums_fwd(lcv, lfv, gates)

    gates, cums = _tiles(gates_fwd, name="gates", rows=rows, tm=rows,
                         full_consts=[lc, lf, p_a, p_dt, p_fb], row_ins=[(proj, LANES, COL_SMALL)],
                         row_outs=[(LANES, F32), (LANES, F32)])

    def expand_fwd(col, b, g, gt, cm):
        return (_dot32(gt, b, _CONTRACT["nn"]), _dot32(cm, g, _CONTRACT["nn"]))

    betax, gcx = _tiles(expand_fwd, name="expand", rows=rows, tm=tm, full_consts=[xb, xg],
                        row_ins=[(gates, LANES, 0), (cums, LANES, 0)],
                        row_outs=[(WIDTH, F32)] * 2)
    grow = _lanes_to_rowform(cums[:, 8:16], rows)
    frow = cums[:, 16:24].T.reshape(PAIRS, 2, rows)

    (qkv,) = _tiles(_gdn_prep_fwd, name="gdn_prep", rows=rows, tm=rows, ncol=3 * PAIRS,
                    col_consts=[(conv_w, CONV_K, LANES, 0)], row_ins=[(proj, LANES, 0)],
                    row_outs=[(LANES, F32)])
    o_gdn, ssave, tsave = _gdn_forward(qkv, betax, gcx, grow, rows)

    w_qk = jnp.concatenate([jnp.tile(q_norm_w.reshape(1, -1), (1, HEADS)),
                            jnp.tile(k_norm_w.reshape(1, -1), (1, HEADS))], axis=1)
    fox_off = 2048 // LANES
    (fqk,) = _tiles(lambda col, w, xx: (_head_rms(w, xx),), name="fox_prep", rows=rows, tm=rows, ncol=2 * PAIRS,
                    col_consts=[(w_qk, 1, LANES, 0)], row_ins=[(proj, LANES, fox_off)],
                    row_outs=[(LANES, F32)])
    ao, lse = _attention_forward(fqk, proj, frow, rows)

    w_on = jnp.tile(out_norm_w.reshape(1, -1), (1, 2))
    z_off, fg_off = 1536 // LANES, 3584 // LANES
    mix_g_fn = lambda w, o, z: _head_rms(w, o) * _silu(z)
    mix_f_fn = lambda a, g: a * jax.nn.sigmoid(g)
    (mix_g,) = _tiles(lambda col, w, o, z: (mix_g_fn(w, o, z),), name="mix_gdn", rows=rows, tm=rows, ncol=PAIRS,
                      full_consts=[w_on], row_ins=[(o_gdn, LANES, 0), (proj, LANES, z_off)],
                      row_outs=[(LANES, BF16)])
    (mix_f,) = _tiles(lambda col, a, g: (mix_f_fn(a, g),), name="mix_fox", rows=rows, tm=rows, ncol=PAIRS,
                      row_ins=[(ao, LANES, 0), (proj, LANES, fg_off)], row_outs=[(LANES, BF16)])
    mix = jnp.concatenate([mix_g, mix_f], axis=1)
    w_out, w_gate, w_up, w_down = late_weights(mix)
    t_rows = min(1024, rows)
    n_rt = rows // t_rows
    row_blk = pl.BlockSpec((t_rows, D_MODEL), lambda i, n: (i, 0))
    vec_blk = pl.BlockSpec((1, D_MODEL), lambda i, n: (0, 0))
    wide = lambda dt: jax.ShapeDtypeStruct((rows, D_MODEL), dt)
    x1, h2 = _mm_blocks(mix, w_out, name="out_proj_norm2", grid=(n_rt, 1), dims="nn",
                        a_spec=row_blk, b_spec=pl.BlockSpec((D_MODEL, D_MODEL), lambda i, n: (0, 0)),
                        o_spec=[row_blk, row_blk], out_shape=[wide(F32), wide(BF16)], add=x, add_spec=row_blk,
                        extra=[(norm2_w, vec_blk)], epilogue=lambda r, w: (r, _rms(r, w)))
    t_cols = D_MODEL
    st_act = jax.ShapeDtypeStruct((N_CHIPS, rows, FF_SHARD), BF16)
    st_rows = pl.BlockSpec((None, rows, FF_SHARD), lambda i, j: (j, i, 0))
    out_rows = pl.BlockSpec((t_rows, t_cols), lambda i, n: (i, n))
    flat = lambda t: t.reshape(N_CHIPS * rows, FF_SHARD)

    def ffn_in(w_st, name):
        return _mm_blocks(h2, w_st, name=name, grid=(1, N_CHIPS), dims="nt",
                          a_spec=pl.BlockSpec((rows, D_MODEL), lambda i, j: (i, 0)),
                          b_spec=pl.BlockSpec((None, FF_SHARD, D_MODEL), lambda i, j: (j, 0, 0)),
                          o_spec=st_rows, out_shape=st_act)

    gate = ffn_in(w_gate, "ffn_gate")
    act_fn = lambda g, u: _silu(g) * u
    st_tile = pl.BlockSpec((None, t_rows, FF_SHARD), lambda i, j: (j, i, 0))
    up, act = _mm_blocks(h2, w_up, name="ffn_up_act", grid=(n_rt, N_CHIPS), dims="nt",
                         a_spec=pl.BlockSpec((t_rows, D_MODEL), lambda i, j: (i, 0)),
                         b_spec=pl.BlockSpec((None, FF_SHARD, D_MODEL), lambda i, j: (j, 0, 0)),
                         o_spec=[st_tile, st_tile], out_shape=[st_act, st_act], extra=[(gate, st_tile)],
                         epilogue=lambda u, g: (u, act_fn(g.astype(F32), u)))
    x2 = _mm_blocks(act, w_down, name="ffn_down", grid=(n_rt, D_MODEL // t_cols), dims="nn", n_sum=N_CHIPS,
                    a_spec=pl.BlockSpec((N_CHIPS, t_rows, FF_SHARD), lambda i, n: (0, i, 0)),
                    b_spec=pl.BlockSpec((N_CHIPS, FF_SHARD, t_cols), lambda i, n: (0, 0, n)),
                    o_spec=out_rows, out_shape=jax.ShapeDtypeStruct((rows, D_MODEL), F32),
                    add=x1, add_spec=out_rows)

    def final_fn(col, w, xx, tgt):
        y, vjp = jax.vjp(_rms, xx, w)
        err = y - tgt
        loss = 0.5 * jnp.sum(err * err) / D_MODEL
        dx, dw = vjp(err / D_MODEL)
        return dx, dx, jnp.full((1, LANES), loss, F32), dw

    dx2, dx2_b, loss, d_final_w = _tiles(final_fn, name="final_loss", rows=rows, tm=tm, full_consts=[final_w],
                                         row_ins=[(x2, D_MODEL, 0), (target, D_MODEL, 0)],
                                         row_outs=[(D_MODEL, F32), (D_MODEL, BF16)],
                                         acc_outs=[(1, LANES), (1, D_MODEL)])

    def act_bwd(d, g, u):
        _, vjp = jax.vjp(act_fn, g.astype(F32), u.astype(F32))
        return vjp(d)

    dgate, dup = _mm_blocks(dx2_b, w_down, name="d_act_gate_up", grid=(n_rt, N_CHIPS), dims="nt",
                            a_spec=pl.BlockSpec((t_rows, D_MODEL), lambda i, j: (i, 0)),
                            b_spec=pl.BlockSpec((None, FF_SHARD, D_MODEL), lambda i, j: (j, 0, 0)),
                            o_spec=[st_tile, st_tile], out_shape=[st_act, st_act],
                            extra=[(gate, st_tile), (up, st_tile)], epilogue=act_bwd)

    def g_ffn(d_st, other, name):
        return _mm_blocks(d_st, other, name=name, grid=(N_CHIPS, D_MODEL // t_cols), dims="tn",
                          a_spec=pl.BlockSpec((None, rows, FF_SHARD), lambda j, n: (j, 0, 0)),
                          b_spec=pl.BlockSpec((rows, t_cols), lambda j, n: (0, n)),
                          o_spec=pl.BlockSpec((None, FF_SHARD, t_cols), lambda j, n: (j, 0, n)),
                          out_shape=jax.ShapeDtypeStruct((N_CHIPS, FF_SHARD, D_MODEL), BF16))

    g_down = g_ffn(act, dx2_b, "g_down")

    def norm_bwd(dh, xx, dres, w):
        _, vjp = jax.vjp(_rms, xx, w)
        dx, dw = vjp(dh)
        return dx + dres, dx + dres, dw

    t_half = min(512, rows)
    half_blk = pl.BlockSpec((t_half, D_MODEL), lambda i, n: (i, 0))

    def d_h2(d_st, w_st, name, add, **fused):
        return _mm_blocks(d_st, w_st, name=name, grid=(rows // t_half, 1), dims="nn", n_sum=N_CHIPS,
                          a_spec=pl.BlockSpec((N_CHIPS, t_half, FF_SHARD), lambda i, n: (0, i, 0)),
                          b_spec=pl.BlockSpec((N_CHIPS, FF_SHARD, D_MODEL), lambda i, n: (0, 0, 0)),
                          add=add, add_spec=half_blk, **fused)

    dh2_gate = d_h2(dgate, w_gate, "d_h2_gate", None, o_spec=half_blk, out_shape=wide(F32))
    dx1, dx1_b, d_norm2_w = d_h2(
        dup, w_up, "d_h2_up_norm2_bwd", dh2_gate, o_spec=[half_blk, half_blk, vec_blk],
        out_shape=[wide(F32), wide(BF16), jax.ShapeDtypeStruct((1, D_MODEL), F32)],
        extra=[(x1, half_blk), (dx2, half_blk), (norm2_w, vec_blk)], epilogue=norm_bwd, n_acc=1)
    g_gate, g_up = g_ffn(dgate, h2, "g_gate"), g_ffn(dup, h2, "g_up")
    dmix = _mm(dx1_b, w_out, dims="nt", name="d_mix", tn=D_MODEL, tk=1024)
    g_out = _mm(mix, dx1_b, dims="tn", name="g_out", tn=D_MODEL, tk=rows, out_dtype=BF16)
    w_on = w_on + early_grads_ready(g_out, g_gate, g_up, g_down)

    def mix_g_bwd(col, w, o, z, d):
        _, vjp = jax.vjp(mix_g_fn, w, o, z)
        dw, do_, dz = vjp(d)
        return do_, dz, dw

    do_gdn, dz, d_on = _tiles(mix_g_bwd, name="mix_gdn_bwd", rows=rows, tm=rows, ncol=PAIRS, full_consts=[w_on],
                              row_ins=[(o_gdn, LANES, 0), (proj, LANES, z_off), (dmix, LANES, 0)],
                              row_outs=[(LANES, F32), (LANES, BF16)], acc_outs=[(1, LANES)])

    def mix_f_bwd(col, a, g, d):
        _, vjp = jax.vjp(mix_f_fn, a, g)
        return vjp(d)

    dao, dfgate = _tiles(mix_f_bwd, name="mix_fox_bwd", rows=rows, tm=rows, ncol=PAIRS,
                         row_ins=[(ao, LANES, 0), (proj, LANES, fg_off), (dmix, LANES, PAIRS)],
                         row_outs=[(LANES, F32), (LANES, BF16)])

    dfq, dfk, dfv, dfrow = _attention_backward(fqk, proj, frow, ao, lse, dao, rows)

    def fox_prep_bwd(col, w, xx, d):
        _, vjp = jax.vjp(_head_rms, w, xx)
        dw, dx = vjp(d)
        return dx, dw

    dfqk, d_wqk = [], []
    for part, d_n in enumerate((dfq, dfk)):
        dx_p, dw_p = _tiles(fox_prep_bwd, name="fox_prep_bwd_" + "qk"[part], rows=rows, tm=rows, ncol=PAIRS,
                            col_consts=[(w_qk, 1, LANES, part * PAIRS)],
                            row_ins=[(proj, LANES, fox_off + part * PAIRS), (d_n, LANES, 0)],
                            row_outs=[(LANES, BF16)], acc_outs=[(1, LANES)])
        dfqk.append(dx_p)
        d_wqk.append(dw_p)

    dq, dk, dv, dbetax, dgcx, dgrow = _gdn_backward(qkv, betax, gcx, grow, ssave, tsave, do_gdn, rows)
    dqkv, d_conv = [], []
    for part, d_n in enumerate((dq, dk, dv)):
        prep_bwd = lambda col, cw, xx, dy, is_qk=(part < 2): _gdn_prep_bwd(is_qk, cw, xx, dy)
        dx_p, dw_p = _tiles(prep_bwd, name="gdn_prep_bwd_" + "qkv"[part], rows=rows, tm=rows, ncol=PAIRS,
                            col_consts=[(conv_w, CONV_K, LANES, part * PAIRS)],
                            row_ins=[(proj, LANES, part * PAIRS), (d_n, LANES, 0)],
                            row_outs=[(LANES, BF16)], acc_outs=[(CONV_K, LANES)])
        dqkv.append(dx_p)
        d_conv.append(dw_p)
    d_conv = jnp.concatenate(d_conv, axis=1)

    def expand_bwd(col, b, g, db, dg):
        return (_dot32(db, b, _CONTRACT["nt"]), _dot32(dg, g, _CONTRACT["nt"]))

    dgates_b, dcums_g = _tiles(expand_bwd, name="expand_bwd", rows=rows, tm=tm, full_consts=[xb, xg],
                               row_ins=[(dbetax, WIDTH, 0), (dgcx, WIDTH, 0)],
                               row_outs=[(LANES, F32), (LANES, F32)])
    dcums_row = jnp.concatenate([jnp.zeros((rows, 8), F32), _rowform_to_lanes(dgrow, rows),
                                 dfrow.reshape(HEADS, rows).T, jnp.zeros((rows, LANES - 24), F32)], axis=1)

    def gates_bwd(col, lcv, lfv, a, dt, fb, pre, dgb, dcg, dcr):
        lane = _lane_ids(pre.shape)
        dgates = jnp.where(lane < 8, dgb, _cums_bwd(lcv, lfv, dcg + dcr))
        _, vjp = jax.vjp(_gates_elem, a, dt, fb, pre)
        da, ddt, dfb, dpre = vjp(dgates)
        return dpre, da, ddt, dfb

    dpre, d_a, d_dt, d_fb = _tiles(gates_bwd, name="gates_bwd", rows=rows, tm=rows,
                                   full_consts=[lc, lf, p_a, p_dt, p_fb],
                                   row_ins=[(proj, LANES, COL_SMALL), (dgates_b, LANES, 0), (dcums_g, LANES, 0),
                                            (dcums_row, LANES, 0)],
                                   row_outs=[(LANES, BF16)], acc_outs=[(1, LANES)] * 3)

    dproj = jnp.concatenate(dqkv + [dz] + dfqk + [dfv, dfgate, dpre], axis=1)
    grad_x, d_norm1_w = _mm_blocks(
        dproj, w_cat, name="d_h1_norm1_bwd", grid=(rows // t_half, 1), dims="nn",
        a_spec=pl.BlockSpec((t_half, D_CAT), lambda i, n: (i, 0)),
        b_spec=pl.BlockSpec((D_CAT, D_MODEL), lambda i, n: (0, 0)),
        o_spec=[half_blk, vec_blk], out_shape=[wide(F32), jax.ShapeDtypeStruct((1, D_MODEL), F32)],
        extra=[(x, half_blk), (dx1, half_blk), (norm1_w, vec_blk)],
        epilogue=lambda dh, xx, dres, w: norm_bwd(dh, xx, dres, w)[1:], n_acc=1)
    g_cat = _mm(dproj, h1, dims="tn", name="g_in", tm=1408, tn=D_MODEL, tk=rows)

    fold = lambda v: v.reshape(-1, HEAD_DIM).sum(axis=0)
    small = dict(
        loss=loss[0, 0],
        norm1_w=d_norm1_w, conv_w=d_conv, a_log=d_a[0, 8:16], dt_bias=d_dt[0, 8:16],
        out_norm_w=fold(d_on), f_bias=d_fb[0, 16:24], q_norm_w=fold(d_wqk[0]),
        k_norm_w=fold(d_wqk[1]), norm2_w=d_norm2_w, final_w=d_final_w)
    return grad_x, g_cat, g_out, g_gate, g_up, g_down, small


HBM_SPEC = pl.BlockSpec(memory_space=pltpu.HBM)


def _place():
    x, y, c = lax.axis_index("x"), lax.axis_index("y"), lax.axis_index("c")
    chips = [(1 - x, y), (x, 1 - y), (1 - x, 1 - y)]
    return x, y, c, 2 * x + y, (x, y, 1 - c), chips, [2 * cx + cy for cx, cy in chips]


def _remote(src, dst, send_sem, recv_sem, to):
    return pltpu.make_async_remote_copy(src_ref=src, dst_ref=dst, send_sem=send_sem, recv_sem=recv_sem,
                                        device_id=to, device_id_type=MESH)


def _allgather_weights(shards, conv):
    n = len(shards)
    halves = [s.shape[1] // 2 for s in shards]
    per = 6
    own_base = n * per + 3

    def body(*refs):
        ins, conv_in = refs[:n], refs[n]
        outs, conv_out = refs[n + 1:2 * n + 1], refs[2 * n + 1]
        send_sems, recv_sems = refs[2 * n + 2:]
        x, y, c, own, sib, chips, chip_idx = _place()

        def half(i, ref, hc):
            return ref.at[:, pl.ds(pl.multiple_of(hc * halves[i], LANES), halves[i])]

        sent = []
        for i, (src, dst) in enumerate(zip(list(ins) + [conv_in], list(outs) + [conv_out])):
            k = own_base + i
            sent.append(_remote(src, dst.at[own], send_sems.at[k], recv_sems.at[k], sib))
        for i in range(n):
            for j, chip in enumerate(chips):
                k = i * per + j
                sent.append(_remote(half(i, ins[i], c), half(i, outs[i].at[own], c),
                                    send_sems.at[k], recv_sems.at[k], (*chip, c)))
        for j, chip in enumerate(chips):
            k = n * per + j
            sent.append(_remote(conv_in, conv_out.at[own], send_sems.at[k], recv_sems.at[k], (*chip, c)))
        for cp in sent:
            cp.start()
        for i in range(n):
            for j in range(len(chips)):
                k = i * per + j
                landed = half(i, outs[i].at[chip_idx[j]], c)
                _remote(landed, landed, send_sems.at[k], recv_sems.at[k], sib).wait_recv()
                fwd = _remote(landed, landed, send_sems.at[k + 3], recv_sems.at[k + 3], sib)
                fwd.start()
                sent.append(fwd)
        for i in range(n):
            for j in range(len(chips)):
                k = i * per + 3 + j
                landed = half(i, outs[i].at[chip_idx[j]], 1 - c)
                _remote(landed, landed, send_sems.at[k], recv_sems.at[k], sib).wait_recv()
        for j in range(len(chips)):
            k = n * per + j
            landed = conv_out.at[chip_idx[j]]
            _remote(landed, landed, send_sems.at[k], recv_sems.at[k], sib).wait_recv()
        for i, dst in enumerate(list(outs) + [conv_out]):
            k = own_base + i
            landed = dst.at[own]
            _remote(landed, landed, send_sems.at[k], recv_sems.at[k], sib).wait_recv()
        for cp in sent:
            cp.wait_send()

    n_sem = own_base + n + 1
    out_shape = [jax.ShapeDtypeStruct((N_CHIPS,) + s.shape, s.dtype) for s in shards]
    out_shape.append(jax.ShapeDtypeStruct((N_CHIPS,) + conv.shape, conv.dtype))
    res = pl.pallas_call(
        body, name="allgather_weights", out_shape=out_shape,
        in_specs=[HBM_SPEC] * (n + 1), out_specs=[HBM_SPEC] * (n + 1),
        scratch_shapes=[pltpu.SemaphoreType.DMA((n_sem,)), pltpu.SemaphoreType.DMA((n_sem,))],
    )(*shards, conv)
    return res[:n], res[n]


SEM_SPEC = pl.BlockSpec(memory_space=pltpu.SEMAPHORE)
ANY_SPEC = pl.BlockSpec(memory_space=pl.ANY)
DATAFLOW = pltpu.SideEffectType.DATAFLOW_SIDE_EFFECTING


def _gather_plan(srcs, lands):
    x, y, c, own, sib, chips, chip_idx = _place()
    plan = []
    for src, land in zip(srcs, lands):
        for j, chip in enumerate(chips):
            plan.append((src, land.at[own], (*chip, c), land.at[chip_idx[j]]))
        plan.append((src, land.at[own], sib, land.at[own]))
    return plan


def _exchange_plan(srcs, lands):
    x, y, c, own, sib, chips, chip_idx = _place()
    plan = []
    for src, land in zip(srcs, lands):
        for j, chip in enumerate(chips):
            plan.append((src.at[chip_idx[j]], land.at[j], (*chip, c), land.at[j]))
    return plan


def _in_proj_plan(srcs, lands):
    x, y, c, own, sib, chips, chip_idx = _place()
    (w, conv), (w_land, conv_land) = srcs, lands
    hw = w.shape[1] // 2
    half = lambda ref: ref.at[:, pl.ds(pl.multiple_of(c * hw, LANES), hw)]
    plan = []
    for j, chip in enumerate(chips):
        plan.append((half(w), half(w_land.at[own]), (*chip, c), half(w_land.at[chip_idx[j]])))
        plan.append((conv, conv_land.at[own], (*chip, c), conv_land.at[chip_idx[j]]))
    plan.append((w, w_land.at[own], sib, w_land.at[own]))
    plan.append((conv, conv_land.at[own], sib, conv_land.at[own]))
    return plan


def _forward_halves(landed):
    hw = landed.shape[2] // 2

    def body(in_ref, out_ref, send_sems, recv_sems):
        x, y, c, own, sib, chips, chip_idx = _place()
        half = lambda ref, hc: ref.at[:, pl.ds(pl.multiple_of(hc * hw, LANES), hw)]
        sent = [_remote(half(out_ref.at[chip_idx[j]], c), half(out_ref.at[chip_idx[j]], c),
                        send_sems.at[j], recv_sems.at[j], sib) for j in range(3)]
        for cp in sent:
            cp.start()
        for j in range(3):
            other = half(out_ref.at[chip_idx[j]], 1 - c)
            _remote(other, other, send_sems.at[j], recv_sems.at[j], sib).wait_recv()
        for cp in sent:
            cp.wait_send()

    return pl.pallas_call(
        body, name="gather_in_forward", out_shape=jax.ShapeDtypeStruct(landed.shape, landed.dtype),
        in_specs=[HBM_SPEC], out_specs=HBM_SPEC, input_output_aliases={0: 0},
        scratch_shapes=[pltpu.SemaphoreType.DMA((3,)), pltpu.SemaphoreType.DMA((3,))],
    )(landed)


def _split_start(name, plan_fn, srcs, land_shapes, n_copies, after):
    n = len(srcs)

    def body(*refs):
        src_refs, land_refs = refs[:n], refs[n:2 * n]
        send_sems, recv_sems = refs[2 * n + 1], refs[2 * n + 2]
        token = refs[-1]
        for k, (src, dst, to, _) in enumerate(plan_fn(src_refs, land_refs)):
            _remote(src, dst, send_sems.at[k], recv_sems.at[k], to).start()
        token[...] = jnp.zeros_like(token)

    lands = [pltpu.with_memory_space_constraint(lax.empty(s.shape, s.dtype), pltpu.HBM) for s in land_shapes]
    srcs = [pltpu.with_memory_space_constraint(s, pltpu.HBM) for s in srcs]
    out_shape = ([pltpu.SemaphoreType.DMA((n_copies,)), pltpu.SemaphoreType.DMA((n_copies,))]
                 + [pltpu.HBM(s.shape, s.dtype) for s in srcs] + [pltpu.HBM(s.shape, s.dtype) for s in land_shapes]
                 + [jax.ShapeDtypeStruct((8, LANES), F32)])
    res = pl.pallas_call(
        body, name=name, out_shape=out_shape,
        in_specs=[HBM_SPEC] * (2 * n) + [ANY_SPEC],
        out_specs=[SEM_SPEC, SEM_SPEC] + [HBM_SPEC] * (2 * n) + [pl.BlockSpec(memory_space=pltpu.VMEM)],
        input_output_aliases={i: 2 + i for i in range(2 * n)},
        compiler_params=pltpu.CompilerParams(has_side_effects=DATAFLOW),
    )(*srcs, *lands, after)
    return dict(sems=res[:2], srcs=res[2:2 + n], lands=res[2 + n:2 + 2 * n], token=res[-1], n=n)


def _split_wait(name, plan_fn, started, after):
    n = started["n"]

    def body(*refs):
        src_refs, land_refs = refs[:n], refs[n:2 * n]
        send_sems, recv_sems = refs[2 * n], refs[2 * n + 1]
        for k, (src, _, to, landed) in enumerate(plan_fn(src_refs, land_refs)):
            copy = _remote(src, landed, send_sems.at[k], recv_sems.at[k], to)
            copy.wait_send()
            copy.wait_recv()

    srcs, lands = started["srcs"], started["lands"]
    after = list(after) if isinstance(after, (list, tuple)) else [after]
    res = pl.pallas_call(
        body, name=name,
        out_shape=[pltpu.HBM(s.shape, s.dtype) for s in srcs] + [pltpu.HBM(s.shape, s.dtype) for s in lands],
        in_specs=[HBM_SPEC] * (2 * n) + [SEM_SPEC, SEM_SPEC] + [ANY_SPEC] * len(after),
        out_specs=[HBM_SPEC] * (2 * n),
        input_output_aliases={i: i for i in range(2 * n)},
        compiler_params=pltpu.CompilerParams(has_side_effects=DATAFLOW),
    )(*srcs, *lands, *started["sems"], *after)
    return res[n:]


def _swap_halves(stacks, name):
    n = len(stacks)

    def body(*refs):
        ins, outs = refs[:n], refs[n:2 * n]
        send_sems, recv_sems = refs[2 * n:]
        x, y, c, own, sib, chips, chip_idx = _place()
        cps = []
        for i in range(n):
            h = stacks[i].shape[2] // 2
            src = ins[i].at[:, :, pl.ds(pl.multiple_of((1 - c) * h, LANES), h)]
            cps.append(_remote(src, outs[i], send_sems.at[i], recv_sems.at[i], sib))
        for cp in cps:
            cp.start()
        for cp in cps:
            cp.wait()

    out_shape = [jax.ShapeDtypeStruct((N_CHIPS, s.shape[1], s.shape[2] // 2), s.dtype) for s in stacks]
    return pl.pallas_call(
        body, name=name, out_shape=out_shape,
        in_specs=[HBM_SPEC] * n, out_specs=[HBM_SPEC] * n,
        scratch_shapes=[pltpu.SemaphoreType.DMA((n,)), pltpu.SemaphoreType.DMA((n,))],
    )(*stacks)


def _add_half(stack, landed, place, name):
    _, rows, h = landed.shape

    def body(place_ref, a_ref, b_ref, o_ref, own_ref):
        part = (a_ref[...].astype(F32) + b_ref[...].astype(F32)).astype(o_ref.dtype)
        o_ref[...] = part

        @pl.when(pl.program_id(0) == place_ref[1])
        def _():
            own_ref[...] = part[0]

    return pl.pallas_call(
        body, name=name,
        out_shape=[jax.ShapeDtypeStruct(landed.shape, BF16), jax.ShapeDtypeStruct((rows, h), BF16)],
        grid_spec=pltpu.PrefetchScalarGridSpec(
            num_scalar_prefetch=1, grid=(N_CHIPS,),
            in_specs=[pl.BlockSpec((1, rows, h), lambda j, p: (j, 0, p[0])),
                      pl.BlockSpec((1, rows, h), lambda j, p: (j, 0, 0))],
            out_specs=[pl.BlockSpec((1, rows, h), lambda j, p: (j, 0, 0)),
                       pl.BlockSpec((rows, h), lambda j, p: (0, 0))]),
        compiler_params=_params(("arbitrary",)),
    )(place, stack, landed)


def _exchange_partials(parts):
    n = len(parts)

    def body(*refs):
        ins, outs = refs[:n], refs[n:2 * n]
        send_sems, recv_sems = refs[2 * n:]
        x, y, c, own, sib, chips, chip_idx = _place()
        sent = []
        for i in range(n):
            for j, chip in enumerate(chips):
                k = i * 3 + j
                sent.append(_remote(ins[i].at[chip_idx[j]], outs[i].at[j], send_sems.at[k], recv_sems.at[k],
                                    (*chip, c)))
        for cp in sent:
            cp.start()
        for i in range(n):
            for j in range(len(chips)):
                k = i * 3 + j
                landed = outs[i].at[j]
                _remote(landed, landed, send_sems.at[k], recv_sems.at[k], sib).wait_recv()
        for cp in sent:
            cp.wait_send()

    return pl.pallas_call(
        body, name="rs_exchange_partials",
        out_shape=[jax.ShapeDtypeStruct((3,) + p.shape[1:], p.dtype) for p in parts],
        in_specs=[HBM_SPEC] * n, out_specs=[HBM_SPEC] * n,
        scratch_shapes=[pltpu.SemaphoreType.DMA((3 * n,)), pltpu.SemaphoreType.DMA((3 * n,))],
    )(*parts)


def _sum_partials(own_part, landed, name, untiled_rows=False):
    _, h, cols = landed.shape
    tc = LANES if untiled_rows else cols

    def body(own_ref, a_ref, o_ref):
        acc = own_ref[...].astype(F32)
        for s in range(3):
            acc = acc + a_ref[s].astype(F32)
        if untiled_rows:
            o_ref[:, 0, :] = acc
        else:
            o_ref[...] = acc

    if untiled_rows:
        out_shape, out_spec = jax.ShapeDtypeStruct((h, 1, cols), F32), pl.BlockSpec((h, 1, tc), lambda i: (0, 0, i))
    else:
        out_shape, out_spec = jax.ShapeDtypeStruct((h, cols), F32), pl.BlockSpec((h, tc), lambda i: (0, i))
    return pl.pallas_call(
        body, name=name, out_shape=out_shape, grid=(cols // tc,),
        in_specs=[pl.BlockSpec((h, tc), lambda i: (0, i)), pl.BlockSpec((3, h, tc), lambda i: (0, 0, i))],
        out_specs=out_spec, compiler_params=_params(("arbitrary",)),
    )(own_part, landed)


def _share_halves(halves, name):
    n = len(halves)

    def body(*refs):
        ins, outs = refs[:n], refs[n:2 * n]
        send_sems, recv_sems = refs[2 * n:]
        x, y, c, own, sib, chips, chip_idx = _place()
        cps = [_remote(ins[i], outs[i], send_sems.at[i], recv_sems.at[i], sib) for i in range(n)]
        for cp in cps:
            cp.start()
        for cp in cps:
            cp.wait()

    return pl.pallas_call(
        body, name=name,
        out_shape=[jax.ShapeDtypeStruct(p.shape, p.dtype) for p in halves],
        in_specs=[HBM_SPEC] * n, out_specs=[HBM_SPEC] * n,
        scratch_shapes=[pltpu.SemaphoreType.DMA((n,)), pltpu.SemaphoreType.DMA((n,))],
    )(*halves)


def _allreduce_small(packed):
    rows = packed.shape[0]
    n_dev = 8

    def body(in_ref, out_ref, gath, send_sems, recv_sems):
        x, y, c = lax.axis_index("x"), lax.axis_index("y"), lax.axis_index("c")
        me = 4 * x + 2 * y + c
        gath[me] = in_ref[...]
        cps = []
        for k in range(1, n_dev):
            fx, fy, fc = (k >> 2) & 1, (k >> 1) & 1, k & 1
            to = (x ^ fx, y ^ fy, c ^ fc)
            cps.append(_remote(in_ref, gath.at[me], send_sems.at[k - 1], recv_sems.at[k - 1], to))
        for cp in cps:
            cp.start()
        for k in range(1, n_dev):
            fx, fy, fc = (k >> 2) & 1, (k >> 1) & 1, k & 1
            src = 4 * (x ^ fx) + 2 * (y ^ fy) + (c ^ fc)
            slot = gath.at[src]
            _remote(slot, slot, send_sems.at[k - 1], recv_sems.at[k - 1], (x, y, c)).wait_recv()
        for cp in cps:
            cp.wait_send()
        acc = gath[0]
        for d in range(1, n_dev):
            acc = acc + gath[d]
        out_ref[...] = acc

    vm = pl.BlockSpec(memory_space=pltpu.VMEM)
    return pl.pallas_call(
        body, name="allreduce_small", out_shape=jax.ShapeDtypeStruct(packed.shape, F32),
        in_specs=[vm], out_specs=vm,
        scratch_shapes=[pltpu.VMEM((n_dev, rows, LANES), F32),
                        pltpu.SemaphoreType.DMA((n_dev - 1,)), pltpu.SemaphoreType.DMA((n_dev - 1,))],
    )(packed)


def _adam(col, w, g, m, v):
    m2 = ADAM_B1 * m + (1.0 - ADAM_B1) * g
    v2 = ADAM_B2 * v + (1.0 - ADAM_B2) * (g * g)
    m_hat = m2 / (1.0 - ADAM_B1 ** ADAM_STEP)
    v_hat = v2 / (1.0 - ADAM_B2 ** ADAM_STEP)
    delta = -ADAM_LR * (m_hat / (jnp.sqrt(v_hat) + ADAM_EPS) + ADAM_WD * w)
    return delta, m2, v2


def _adam_call(w, g, m, v, name):
    rows, cols = w.shape
    tm = rows
    for cand in (256, 352, 176, 128, 64, 48, 16, 8):
        if rows % cand == 0:
            tm = cand
            break
    return _tiles(_adam, name=name, rows=rows, tm=tm,
                  row_ins=[(w, cols, 0), (g, cols, 0), (m, cols, 0), (v, cols, 0)],
                  row_outs=[(cols, F32)] * 3)


def _adam_big(w, g_mine, g_other, m, v, place, name):
    rows, cols = w.shape
    tc = 256
    nt = cols // 2 // tc

    def body(place_ref, w_ref, gm_ref, go_ref, m_ref, v_ref, g_out, d_out, m_out, v_out):
        g = jnp.where(pl.program_id(0) == place_ref[0], gm_ref[...], go_ref[...])
        d, m2, v2 = _adam(None, w_ref[...], g, m_ref[...], v_ref[...])
        g_out[...] = g
        d_out[...] = d
        m_out[...] = m2
        v_out[...] = v2

    full = pl.BlockSpec((rows, tc), lambda hh, i, p: (0, hh * nt + i))
    half = pl.BlockSpec((rows, tc), lambda hh, i, p: (0, i))
    return pl.pallas_call(
        body, name=name, out_shape=[jax.ShapeDtypeStruct(w.shape, F32)] * 4,
        grid_spec=pltpu.PrefetchScalarGridSpec(
            num_scalar_prefetch=1, grid=(2, nt),
            in_specs=[full, half, half, full, full], out_specs=[full] * 4),
        compiler_params=_params(("arbitrary", "arbitrary")),
    )(place, w, g_mine, g_other, m, v)


def _adam_untiled_rows(w, g_mine, g_other, m, v, place, name):
    rows, _, cols = w.shape
    tc = 256
    nt = cols // 2 // tc
    rb = next(r for r in (206, 128, 103, rows) if rows % r == 0)

    def body(place_ref, w_ref, gm_ref, go_ref, m_ref, v_ref, g_out, d_out, m_out, v_out):
        g = jnp.where(pl.program_id(0) == place_ref[0], gm_ref[...], go_ref[...])
        d, m2, v2 = _adam(None, w_ref[...], g, m_ref[...], v_ref[...])
        g_out[...] = g
        d_out[...] = d
        m_out[...] = m2
        v_out[...] = v2

    full = pl.BlockSpec((rb, 1, tc), lambda hh, i, r, p: (r, 0, hh * nt + i))
    half = pl.BlockSpec((rb, 1, tc), lambda hh, i, r, p: (r, 0, i))
    return pl.pallas_call(
        body, name=name, out_shape=[jax.ShapeDtypeStruct(w.shape, F32)] * 4,
        grid_spec=pltpu.PrefetchScalarGridSpec(
            num_scalar_prefetch=1, grid=(2, nt, rows // rb),
            in_specs=[full, half, half, full, full], out_specs=[full] * 4),
        compiler_params=_params(("arbitrary", "arbitrary", "arbitrary")),
    )(place, w, g_mine, g_other, m, v)


def _pack(arrays, zero=None):
    flat = []
    for a in arrays:
        a = a.reshape(-1).astype(F32)
        if zero is not None:
            a = a + zero
        flat.append(jnp.pad(a, (0, (-a.size) % LANES)))
    out = jnp.concatenate(flat)
    out = jnp.pad(out, (0, (-out.size) % (8 * LANES)))
    return out.reshape(-1, LANES)


def _unpack(packed, shapes):
    flat = packed.reshape(-1)
    out, off = [], 0
    for s in shapes:
        size = int(np.prod(s))
        out.append(flat[off:off + size].reshape(s))
        off += size + (-size) % LANES
    return out


def kernel(x, norm1_w, w_in, gdn_conv_w, gdn_A_log, gdn_dt_bias, gdn_out_norm_w, fox_f_bias, fox_q_norm_w, fox_k_norm_w, w_out, norm2_w, w_ffn_gate, w_ffn_up, w_ffn_down, final_norm_w, loss_target, m_norm1_w, m_w_in, m_gdn_conv_w, m_gdn_A_log, m_gdn_dt_bias, m_gdn_out_norm_w, m_fox_f_bias, m_fox_q_norm_w, m_fox_k_norm_w, m_w_out, m_norm2_w, m_w_ffn_gate, m_w_ffn_up, m_w_ffn_down, m_final_norm_w, v_norm1_w, v_w_in, v_gdn_conv_w, v_gdn_A_log, v_gdn_dt_bias, v_gdn_out_norm_w, v_fox_f_bias, v_fox_q_norm_w, v_fox_k_norm_w, v_w_out, v_norm2_w, v_w_ffn_gate, v_w_ffn_up, v_w_ffn_down, v_final_norm_w):
    cx, cy, cc = lax.axis_index("x"), lax.axis_index("y"), lax.axis_index("c")
    own = 2 * cx + cy
    place = jnp.stack([cc, own]).astype(jnp.int32)

    names = ["w_in", "w_out", "w_gate", "w_up", "w_down"]
    is_t = [True, False, True, True, False]
    to_t = lambda a, t: a[0].T if t else a[0]
    from_t = lambda a, t: (a.T if t else a)[None]
    big_w = [to_t(a, t) for a, t in zip([w_in, w_out, w_ffn_gate, w_ffn_up, w_ffn_down], is_t)]
    big_m = [to_t(a, t) for a, t in zip([m_w_in, m_w_out, m_w_ffn_gate, m_w_ffn_up, m_w_ffn_down], is_t)]
    big_v = [to_t(a, t) for a, t in zip([v_w_in, v_w_out, v_w_ffn_gate, v_w_ffn_up, v_w_ffn_down], is_t)]
    shards = [big_w[0].astype(BF16)]
    small_w = [norm1_w, gdn_conv_w, gdn_A_log, gdn_dt_bias, gdn_out_norm_w, fox_f_bias, fox_q_norm_w,
               fox_k_norm_w, norm2_w, final_norm_w]
    small_m = [m_norm1_w, m_gdn_conv_w, m_gdn_A_log, m_gdn_dt_bias, m_gdn_out_norm_w, m_fox_f_bias,
               m_fox_q_norm_w, m_fox_k_norm_w, m_norm2_w, m_final_norm_w]
    small_v = [v_norm1_w, v_gdn_conv_w, v_gdn_A_log, v_gdn_dt_bias, v_gdn_out_norm_w, v_fox_f_bias,
               v_fox_q_norm_w, v_fox_k_norm_w, v_norm2_w, v_final_norm_w]
    first = _split_start("gather_in_start", _in_proj_plan, [shards[0], gdn_conv_w[0]],
                         [jax.ShapeDtypeStruct((N_CHIPS,) + shards[0].shape, BF16),
                          jax.ShapeDtypeStruct((N_CHIPS, CONV_K, 3 * WIDTH // N_CHIPS), F32)],
                         n_copies=8, after=shards[0])
    small_packed = [_pack(p, first["token"][0, 0]) for p in (small_w, small_m, small_v)]
    shards += [(w + first["token"][0, 0]).astype(BF16) for w in big_w[1:]]
    rest = {}

    def first_weights(after):
        w_in_g, conv_g = _split_wait("gather_in_wait", _in_proj_plan, first, [after] + small_packed)
        w_in_g = _forward_halves(w_in_g)
        rest.update(_split_start("gather_rest_start", _gather_plan, shards[1:],
                                 [jax.ShapeDtypeStruct((N_CHIPS,) + s.shape, BF16) for s in shards[1:]],
                                 n_copies=4 * len(shards[1:]), after=w_in_g))
        w_cat = _cat_weights(w_in_g.reshape(D_IN, D_MODEL))
        return w_cat + rest["token"][0, 0].astype(BF16), conv_g.transpose(1, 0, 2).reshape(CONV_K, 3 * WIDTH)

    def late_weights(after):
        w_out_g, w_gate_g, w_up_g, w_down_g = _split_wait("gather_rest_wait", _gather_plan, rest, after)
        return w_out_g.reshape(D_MODEL, D_MODEL), w_gate_g, w_up_g, w_down_g

    def start_reduction(stacks, nms, tag):
        landed = _swap_halves(stacks, "rs_swap_" + tag)
        added = [_add_half(s, l, place, "rs_add_" + nm) for s, l, nm in zip(stacks, landed, nms)]
        parts = [a[0] for a in added]
        started = _split_start("exchange_" + tag + "_start", _exchange_plan, parts,
                               [jax.ShapeDtypeStruct((3,) + p.shape[1:], p.dtype) for p in parts],
                               n_copies=3 * len(parts), after=parts[0])
        return dict(own=[a[1] for a in added], started=started, tag=tag, names=nms)

    def finish_reduction(red, after, updates):
        landed = _split_wait("exchange_" + red["tag"] + "_wait", _exchange_plan, red["started"], after)
        halves = [_sum_partials(o, p, "rs_sum_" + nm, untiled_rows=nm == "w_in")
                  for o, p, nm in zip(red["own"], landed, red["names"])]
        others = _share_halves(halves, "rs_share_" + red["tag"])
        return [upd(gm, go) for upd, gm, go in zip(updates, halves, others)]

    def transport_update(b):
        def upd(gm, go):
            res = _adam_big(big_w[b], gm, go, big_m[b], big_v[b], place, "adam_" + names[b])
            early_done.append(res[1])
            return [from_t(a, is_t[b]) for a in res]
        return upd

    early_done = []

    def w_in_update(gm, go):
        rows3 = lambda a: jnp.transpose(a, (2, 0, 1))
        res = _adam_untiled_rows(rows3(w_in), gm, go, rows3(m_w_in), rows3(v_w_in), place, "adam_w_in")
        return [jnp.transpose(a, (1, 2, 0)) for a in res]

    early = {}

    def early_grads_ready(g_out, g_gate, g_up, g_down):
        stacks = [g_out.reshape(N_CHIPS, D_MODEL // N_CHIPS, D_MODEL), g_gate, g_up, g_down]
        early.update(start_reduction(stacks, names[1:], "early"))
        return early["started"]["token"][0, 0]

    grad_x, g_cat, _, _, _, _, small = _local_step(
        x[0], loss_target[0], norm1_w + first["token"][0, 0], gdn_A_log[0], gdn_dt_bias[0],
        gdn_out_norm_w[0], fox_f_bias[0], fox_q_norm_w[0], fox_k_norm_w[0], norm2_w, final_norm_w.reshape(1, -1),
        first_weights, late_weights, early_grads_ready)

    late = start_reduction([_uncat_grad(g_cat).reshape(N_CHIPS, D_IN // N_CHIPS, D_MODEL)], names[:1], "w_in")
    big_upd = finish_reduction(early, late["started"]["token"], [transport_update(b) for b in range(1, 5)])

    order = ["norm1_w", "conv_w", "a_log", "dt_bias", "out_norm_w", "f_bias", "q_norm_w", "k_norm_w",
             "norm2_w", "final_w"]
    red = _allreduce_small(_pack([small[k] for k in order] + [small["loss"]]))
    red_shapes = [(1, D_MODEL), (CONV_K, 3 * WIDTH), (1, HEADS), (1, HEADS), (1, HEAD_DIM), (1, HEADS),
                  (1, HEAD_DIM), (1, HEAD_DIM), (1, D_MODEL), (D_MODEL,), ()]
    red_list = _unpack(red, red_shapes)
    loss = red_list[-1]
    small_g = dict(zip(order, red_list[:-1]))
    shard_cols = 3 * WIDTH // N_CHIPS
    small_g["conv_w"] = lax.dynamic_slice_in_dim(small_g["conv_w"], own * shard_cols, shard_cols, axis=1)[None]
    small_gl = [small_g[k].reshape(w.shape) for k, w in zip(order, small_w)]
    s_delta, s_m, s_v = _adam_call(small_packed[0], _pack(small_gl), small_packed[1], small_packed[2], "adam_small")
    big_upd = finish_reduction(late, [s_delta] + early_done, [w_in_update]) + big_upd
    shapes = [w.shape for w in small_w]
    s_delta, s_m, s_v = _unpack(s_delta, shapes), _unpack(s_m, shapes), _unpack(s_v, shapes)

    big_pos = {1: 0, 9: 1, 11: 2, 12: 3, 13: 4}
    small_pos = {0: 0, 2: 1, 3: 2, 4: 3, 5: 4, 6: 5, 7: 6, 8: 7, 10: 8, 14: 9}
    grads, deltas, new_m, new_v = [], [], [], []
    for pos in range(15):
        if pos in big_pos:
            b = big_pos[pos]
            g, d, m2, v2 = big_upd[b]
            grads.append(g)
            deltas.append(d)
            new_m.append(m2)
            new_v.append(v2)
        else:
            s = small_pos[pos]
            grads.append(small_gl[s])
            deltas.append(s_delta[s])
            new_m.append(s_m[s])
            new_v.append(s_v[s])
    return (loss, grad_x[None], *grads, *deltas, *new_m, *new_v)
```

```python
import jax
import jax.numpy as jnp
import numpy as np
from jax import lax
from jax.experimental import pallas as pl
from jax.experimental.pallas import tpu as pltpu

F32 = jnp.float32
BF16 = jnp.bfloat16

D_MODEL = 1024
HEADS = 8
HEAD_DIM = 64
PAIRS = HEADS // 2
WIDTH = HEADS * HEAD_DIM
CHUNK = 64
CONV_K = 4
D_FF = 2816
FF_SHARD = D_FF // 4
EPS = 1e-6
SCALE = HEAD_DIM ** -0.5
LANES = 128
N_CHIPS = 4
D_IN = 4120
D_CAT = 4224
COL_SMALL = 4096 // LANES

ADAM_LR = 0.001
ADAM_B1 = 0.9
ADAM_B2 = 0.999
ADAM_EPS = 1e-08
ADAM_WD = 0.01
ADAM_STEP = 10

VMEM_LIMIT = 56 * 1024 * 1024
MESH = pl.DeviceIdType.MESH
HIGHEST = lax.Precision.HIGHEST


def _params(sem):
    return pltpu.CompilerParams(dimension_semantics=sem, vmem_limit_bytes=VMEM_LIMIT)


_CONTRACT = {"nn": ((1,), (0,)), "nt": ((1,), (1,)), "tn": ((0,), (0,))}


def _mm(a, b, *, dims, name, out_dtype=F32, add=None, tm=1024, tn=512, tk=512):
    if dims == "nn":
        (m, k), (k2, n) = a.shape, b.shape
    elif dims == "nt":
        (m, k), (n, k2) = a.shape, b.shape
    else:
        (k, m), (k2, n) = a.shape, b.shape
    assert k == k2, (a.shape, b.shape, dims)
    tm, tn, tk = min(tm, m), min(tn, n), min(tk, k)
    assert m % tm == 0 and n % tn == 0 and k % tk == 0, (m, n, k, tm, tn, tk)
    nk = k // tk
    a_spec = (pl.BlockSpec((tk, tm), lambda i, j, kk: (kk, i)) if dims == "tn"
              else pl.BlockSpec((tm, tk), lambda i, j, kk: (i, kk)))
    b_spec = (pl.BlockSpec((tn, tk), lambda i, j, kk: (j, kk)) if dims == "nt"
              else pl.BlockSpec((tk, tn), lambda i, j, kk: (kk, j)))
    o_spec = pl.BlockSpec((tm, tn), lambda i, j, kk: (i, j))
    contract = (_CONTRACT[dims], ((), ()))
    has_add = add is not None

    def body(*refs):
        a_ref, b_ref = refs[:2]
        add_ref = refs[2] if has_add else None
        o_ref = refs[3] if has_add else refs[2]
        part = lax.dot_general(a_ref[...].astype(BF16), b_ref[...].astype(BF16), contract,
                               preferred_element_type=F32)

        def finish(r):
            if has_add:
                r = r + add_ref[...].astype(F32)
            o_ref[...] = r.astype(out_dtype)

        if nk == 1:
            finish(part)
            return
        acc = refs[-1]
        kk = pl.program_id(2)

        @pl.when(kk == 0)
        def _():
            acc[...] = part

        @pl.when(kk > 0)
        def _():
            acc[...] += part

        @pl.when(kk == nk - 1)
        def _():
            finish(acc[...])

    ins = [a, b] + ([add] if has_add else [])
    in_specs = [a_spec, b_spec] + ([o_spec] if has_add else [])
    return pl.pallas_call(
        body, name=name, grid=(m // tm, n // tn, nk),
        in_specs=in_specs, out_specs=o_spec,
        out_shape=jax.ShapeDtypeStruct((m, n), out_dtype),
        scratch_shapes=[pltpu.VMEM((tm, tn), F32)] if nk > 1 else [],
        compiler_params=_params(("parallel", "parallel", "arbitrary")),
    )(*ins)


def _mm_blocks(a, b, *, name, grid, a_spec, b_spec, o_spec, out_shape, dims, n_sum=0, add=None, add_spec=None,
               epilogue=None, extra=(), n_acc=0):
    contract = (_CONTRACT[dims], ((), ()))
    has_add = add is not None
    n_in = 2 + has_add + len(extra)

    def body(*refs):
        a_ref, b_ref = refs[:2]
        dot = lambda x, y: lax.dot_general(x.astype(BF16), y.astype(BF16), contract, preferred_element_type=F32)
        if n_sum:
            r = dot(a_ref[0], b_ref[0])
            for s in range(1, n_sum):
                r = r + dot(a_ref[s], b_ref[s])
        else:
            r = dot(a_ref[...], b_ref[...])
        if has_add:
            r = r + refs[2][...].astype(F32)
        if epilogue is None:
            refs[-1][...] = r.astype(refs[-1].dtype)
        else:
            outs = epilogue(r, *[e[...] for e in refs[2 + has_add:n_in]])
            out_refs = refs[n_in:]
            n_plain = len(out_refs) - n_acc
            for o_ref, val in zip(out_refs[:n_plain], outs):
                o_ref[...] = val.astype(o_ref.dtype)
            if n_acc:
                @pl.when(pl.program_id(0) == 0)
                def _():
                    for o_ref in out_refs[n_plain:]:
                        o_ref[...] = jnp.zeros_like(o_ref)
                for o_ref, val in zip(out_refs[n_plain:], outs[n_plain:]):
                    o_ref[...] += val

    ins = [a, b] + ([add] if has_add else []) + [e[0] for e in extra]
    in_specs = [a_spec, b_spec] + ([add_spec] if has_add else []) + [e[1] for e in extra]
    sem = ("arbitrary",) + ("parallel",) * (len(grid) - 1) if n_acc else ("parallel",) * len(grid)
    return pl.pallas_call(
        body, name=name, grid=grid, in_specs=in_specs, out_specs=o_spec, out_shape=out_shape,
        compiler_params=_params(sem),
    )(*ins)


def _tiles(fn, *, name, rows, tm, ncol=1, row_ins=(), col_consts=(), full_consts=(),
           row_outs=(), acc_outs=()):
    nt = rows // tm
    assert rows % tm == 0
    n_full, n_col, n_row = len(full_consts), len(col_consts), len(row_ins)
    n_ro, n_acc = len(row_outs), len(acc_outs)

    def body(*refs):
        ins = refs[:n_full + n_col + n_row]
        outs = refs[n_full + n_col + n_row:]
        i = pl.program_id(1)
        res = fn(pl.program_id(0), *[r[...] for r in ins])
        for r, v in zip(outs[:n_ro], res[:n_ro]):
            r[...] = v.astype(r.dtype)
        if n_acc:
            @pl.when(i == 0)
            def _():
                for r in outs[n_ro:]:
                    r[...] = jnp.zeros_like(r)
            for r, v in zip(outs[n_ro:], res[n_ro:]):
                r[...] += v

    in_specs = [pl.BlockSpec(a.shape, lambda j, i, nd=a.ndim: (0,) * nd) for a in full_consts]
    in_specs += [pl.BlockSpec((nr, w), lambda j, i, o=o: (0, o + j)) for (_, nr, w, o) in col_consts]
    in_specs += [pl.BlockSpec((tm, w), lambda j, i, o=o: (i, o + j)) for (_, w, o) in row_ins]
    out_specs = [pl.BlockSpec((tm, w), lambda j, i: (i, j)) for (w, _) in row_outs]
    out_specs += [pl.BlockSpec((nr, w), lambda j, i: (0, j)) for (nr, w) in acc_outs]
    out_shape = [jax.ShapeDtypeStruct((rows, w * ncol), dt) for (w, dt) in row_outs]
    out_shape += [jax.ShapeDtypeStruct((nr, w * ncol), F32) for (nr, w) in acc_outs]
    args = list(full_consts) + [c[0] for c in col_consts] + [r[0] for r in row_ins]
    out = pl.pallas_call(
        body, name=name, grid=(ncol, nt), in_specs=in_specs, out_specs=out_specs, out_shape=out_shape,
        compiler_params=_params(("parallel", "arbitrary")),
    )(*args)
    return out


def _rms(x, w):
    return x * lax.rsqrt(jnp.mean(x * x, axis=-1, keepdims=True) + EPS) * w


def _lane_lo(shape):
    return lax.broadcasted_iota(jnp.int32, shape, len(shape) - 1) < HEAD_DIM


def _pair_sum(x):
    lo = _lane_lo(x.shape)
    s0 = jnp.sum(jnp.where(lo, x, 0.0), axis=-1, keepdims=True)
    s1 = jnp.sum(jnp.where(lo, 0.0, x), axis=-1, keepdims=True)
    return jnp.where(lo, s0, s1)


def _head_col(x, lo, h):
    keep = lo if h == 0 else jnp.logical_not(lo)
    return jnp.max(jnp.where(keep, x, -jnp.inf), axis=-1, keepdims=True)


def _softplus(x):
    return jnp.maximum(x, 0.0) + jnp.log1p(jnp.exp(-jnp.abs(x)))


def _silu(x):
    return x * jax.nn.sigmoid(x)


def _dot(a, b, contract):
    return lax.dot_general(a.astype(BF16), b.astype(BF16), (contract, ((), ())),
                           preferred_element_type=F32)


def _dot32(a, b, contract):
    return lax.dot_general(a, b, (contract, ((), ())), precision=HIGHEST, preferred_element_type=F32)


def _bd(y):
    yy = jnp.concatenate([y, y], axis=0)
    r = lax.broadcasted_iota(jnp.int32, yy.shape, 0) < HEAD_DIM
    c = lax.broadcasted_iota(jnp.int32, yy.shape, 1) < HEAD_DIM
    return jnp.where(r == c, yy, 0.0)


def _pp(x, y):
    return _dot(x, _bd(y), _CONTRACT["nn"])


def _pp_nt(x, y):
    return _dot(x, _bd(y), _CONTRACT["nt"])


def _pp_tn(x, y):
    full = _dot(x, y, _CONTRACT["tn"])
    return jnp.where(_lane_lo((HEAD_DIM, LANES)), full[:HEAD_DIM], full[HEAD_DIM:])


def _gdn_masks():
    row = lax.broadcasted_iota(jnp.int32, (CHUNK, LANES), 0)
    col = lax.broadcasted_iota(jnp.int32, (CHUNK, LANES), 1) % HEAD_DIM
    return row, col


def _interleave(chains):
    live = list(chains)
    while live:
        for g in list(live):
            try:
                next(g)
            except StopIteration:
                live.remove(g)


def _gdn_forward(qkv, betax, gcx, grow, rows):
    nchunk = rows // CHUNK

    def body(q_ref, k_ref, v_ref, bx_ref, gx_ref, gr_ref, o_ref, ss_ref, ts_ref, state):
        n = pl.program_id(0)

        @pl.when(n == 0)
        def _():
            state[...] = jnp.zeros_like(state)

        row, col = _gdn_masks()
        incl, strict = col <= row, col < row

        def chain(p):
            lanes = pl.ds(p * LANES, LANES)
            q, k, v, bx, gx = q_ref[:, lanes], k_ref[:, lanes], v_ref[:, lanes], bx_ref[:, lanes], gx_ref[:, lanes]
            gr = gr_ref[0, p]
            glast = gx_ref[pl.ds(CHUNK - 1, 1), lanes]
            s = state[p]
            dm = jnp.where(incl, jnp.exp(jnp.minimum(gx - gr, 0.0)), 0.0)
            kb, vb, eg, qs = k * bx, v * bx, jnp.exp(gx), q * SCALE
            yield
            big_g, big_p = _pp_nt(kb, k), _pp_nt(qs, k)
            yield
            x = -jnp.where(strict, big_g * dm, 0.0)
            att = jnp.where(incl, big_p * dm, 0.0)
            tm = jnp.where(row == col, 1.0, 0.0) + x
            x = _pp(x, x)
            yield
            for _ in range(4):
                step, x = _pp(tm, x), _pp(x, x)
                yield
                tm = tm + step
            tm = tm + _pp(tm, x)
            yield
            u, w = _pp(tm, vb), _pp(tm, kb * eg)
            yield
            ws, qgs = _pp(w, s), _pp(qs * eg, s)
            yield
            vn = u - ws
            kd = k * jnp.exp(glast - gx)
            avn, upd = _pp(att, vn), _pp_tn(kd, vn)
            yield
            ss_ref[0, p] = s
            ts_ref[0, p] = tm
            o_ref[:, lanes] = qgs + avn
            state[p] = s * jnp.exp(glast) + upd

        _interleave([chain(p) for p in range(PAIRS)])

    blk = lambda j: pl.BlockSpec((CHUNK, WIDTH), lambda n, j=j: (n, j))
    sv = pl.BlockSpec((1, PAIRS, CHUNK, LANES), lambda n: (n, 0, 0, 0))
    return pl.pallas_call(
        body, name="gdn_fwd", grid=(nchunk,),
        in_specs=[blk(0), blk(1), blk(2), blk(0), blk(0),
                  pl.BlockSpec((1, PAIRS, 1, LANES), lambda n: (n, 0, 0, 0))],
        out_specs=[blk(0), sv, sv],
        out_shape=[jax.ShapeDtypeStruct((rows, WIDTH), F32),
                   jax.ShapeDtypeStruct((nchunk, PAIRS, CHUNK, LANES), F32),
                   jax.ShapeDtypeStruct((nchunk, PAIRS, CHUNK, LANES), F32)],
        scratch_shapes=[pltpu.VMEM((PAIRS, CHUNK, LANES), F32)],
        compiler_params=_params(("arbitrary",)),
    )(qkv, qkv, qkv, betax, gcx, grow)


def _gdn_backward(qkv, betax, gcx, grow, ssave, tsave, do, rows):
    nchunk = rows // CHUNK

    def body(q_ref, k_ref, v_ref, bx_ref, gx_ref, gr_ref, ss_ref, ts_ref, do_ref,
             dq_ref, dk_ref, dv_ref, dbx_ref, dgx_ref, dgr_ref, dstate):
        n = pl.program_id(0)

        @pl.when(n == 0)
        def _():
            dstate[...] = jnp.zeros_like(dstate)

        row, col = _gdn_masks()
        incl, strict = col <= row, col < row

        def chain(p):
            lanes = pl.ds(p * LANES, LANES)
            q, k, v, bx, gx = q_ref[:, lanes], k_ref[:, lanes], v_ref[:, lanes], bx_ref[:, lanes], gx_ref[:, lanes]
            gr = gr_ref[0, p]
            glast = gx_ref[pl.ds(CHUNK - 1, 1), lanes]
            s, tm, d_o = ss_ref[0, p], ts_ref[0, p], do_ref[:, lanes]
            ds_out = dstate[p]
            dm = jnp.where(incl, jnp.exp(jnp.minimum(gx - gr, 0.0)), 0.0)
            kb, vb, eg, qs = k * bx, v * bx, jnp.exp(gx), q * SCALE
            kbg, qg = kb * eg, qs * eg
            ed = jnp.exp(glast - gx)
            kd = k * ed
            eglast = jnp.exp(glast)
            yield
            big_g, big_p = _pp_nt(kb, k), _pp_nt(qs, k)
            u, w = _pp(tm, vb), _pp(tm, kbg)
            dqg, kds = _pp_nt(d_o, s), _pp(kd, ds_out)
            yield
            low = jnp.where(strict, big_g * dm, 0.0)
            att = jnp.where(incl, big_p * dm, 0.0)
            ws, atd = _pp(w, s), _pp_tn(att, d_o)
            yield
            vn = u - ws
            dvn = kds + atd
            dkd, datt_raw = _pp_nt(vn, ds_out), _pp_nt(d_o, vn)
            dw_neg, dvb = _pp_nt(dvn, s), _pp_tn(tm, dvn)
            dtm_a, wdv = _pp_nt(dvn, vb), _pp_tn(w, dvn)
            qgd = _pp_tn(qg, d_o)
            yield
            datt = jnp.where(incl, datt_raw, 0.0)
            dw = -dw_neg
            dtm_b, dkbg = _pp_nt(dw, kbg), _pp_tn(tm, dw)
            dbig_p = datt * dm
            dqs_a, dk_p = _pp(dbig_p, k), _pp_tn(dbig_p, qs)
            yield
            inner = _pp_tn(tm, dtm_a + dtm_b)
            yield
            dlow = jnp.where(strict, -_pp_nt(inner, tm), 0.0)
            yield
            dbig_g = dlow * dm
            dkb_a, dk_g = _pp(dbig_g, k), _pp_tn(dbig_g, kb)
            yield
            dkb = dkb_a + dkbg * eg
            dqs = dqs_a + dqg * eg
            dk = dk_g + dk_p + dkd * ed + dkb * bx
            z = dlow * low + datt * att
            kdterm = dkd * kd
            dglast = (jnp.sum(ds_out * s, axis=0, keepdims=True) * eglast
                      + jnp.sum(kdterm, axis=0, keepdims=True))
            dgx = dqg * qg + dkbg * kbg - kdterm
            dgx = dgx + jnp.where(col == 0, _pair_sum(z), 0.0)
            dgx = dgx + jnp.where(row == CHUNK - 1, dglast, 0.0)
            dq_ref[:, lanes] = dqs * SCALE
            dk_ref[:, lanes] = dk
            dv_ref[:, lanes] = dvb * bx
            dbx_ref[:, lanes] = dkb * k + dvb * v
            dgx_ref[:, lanes] = dgx
            dgr_ref[0, p] = -jnp.sum(z, axis=0, keepdims=True)
            dstate[p] = ds_out * eglast + qgd - wdv

        _interleave([chain(p) for p in range(PAIRS)])

    last = nchunk - 1
    blk = lambda j: pl.BlockSpec((CHUNK, WIDTH), lambda n, j=j: (last - n, j))
    sv = pl.BlockSpec((1, PAIRS, CHUNK, LANES), lambda n: (last - n, 0, 0, 0))
    gr_spec = pl.BlockSpec((1, PAIRS, 1, LANES), lambda n: (last - n, 0, 0, 0))
    wide = jax.ShapeDtypeStruct((rows, WIDTH), F32)
    return pl.pallas_call(
        body, name="gdn_bwd", grid=(nchunk,),
        in_specs=[blk(0), blk(1), blk(2), blk(0), blk(0), gr_spec, sv, sv, blk(0)],
        out_specs=[blk(0)] * 5 + [gr_spec],
        out_shape=[wide] * 5 + [jax.ShapeDtypeStruct((nchunk, PAIRS, 1, LANES), F32)],
        scratch_shapes=[pltpu.VMEM((PAIRS, CHUNK, LANES), F32)],
        compiler_params=_params(("arbitrary",)),
    )(qkv, qkv, qkv, betax, gcx, grow, ssave, tsave, do)


ATT_TQ = 256


def _att_scores(qh, kt, fk, diag):
    s = _dot(qh, kt, _CONTRACT["nt"]) - fk
    if diag:
        r = lax.broadcasted_iota(jnp.int32, s.shape, 0)
        c = lax.broadcasted_iota(jnp.int32, s.shape, 1)
        s = jnp.where(r >= c, s, -jnp.inf)
    return s


def _head_masks(n):
    lo = _lane_lo((n, LANES))
    return [lo, jnp.logical_not(lo)]


def _attention_forward(fqk, proj, frow, rows):
    tq = tk = min(ATT_TQ, rows)
    nq = rows // tq
    v_off = 3072 // LANES

    def body(q_ref, k_ref, v_ref, fr_ref, o_ref, lse_ref):
        qi = pl.program_id(1)
        q = q_ref[...] * SCALE
        keep_q, keep_k = _head_masks(tq), _head_masks(tk)
        qh = [jnp.where(keep_q[h], q, 0.0).astype(BF16) for h in range(2)]

        def tile(ki, carry, diag):
            k0 = pl.multiple_of(ki * tk, tk)
            kt = k_ref[pl.ds(k0, tk), :].astype(BF16)
            v_t = v_ref[pl.ds(k0, tk), :]
            out = [None, None]

            def chain(h):
                m, l, acc = carry[h]
                vt = jnp.where(keep_k[h], v_t, 0.0).astype(BF16)
                yield
                s = _att_scores(qh[h], kt, fr_ref[0, pl.ds(h, 1), pl.ds(k0, tk)], diag)
                yield
                m_new = jnp.maximum(m, jnp.max(s, axis=-1, keepdims=True))
                p = jnp.exp(s - m_new)
                alpha = jnp.exp(m - m_new)
                l = alpha * l + jnp.sum(p, axis=-1, keepdims=True)
                p_hi = p.astype(BF16)
                p_lo = p - p_hi.astype(F32)
                yield
                out[h] = (m_new, l, alpha * acc + _dot(p_hi, vt, _CONTRACT["nn"]) + _dot(p_lo, vt, _CONTRACT["nn"]))

            _interleave([chain(0), chain(1)])
            return tuple(out)

        one = (jnp.full((tq, 1), -jnp.inf, F32), jnp.zeros((tq, 1), F32), jnp.zeros((tq, LANES), F32))
        carry = lax.fori_loop(0, qi, lambda ki, c: tile(ki, c, False), (one, one))
        (m0, l0, acc0), (m1, l1, acc1) = tile(qi, carry, True)
        o_ref[...] = acc0 / l0 + acc1 / l1
        lse_ref[...] = jnp.where(keep_q[0], m0 + jnp.log(l0), m1 + jnp.log(l1))

    whole = lambda off: pl.BlockSpec((rows, LANES), lambda p, i, off=off: (0, off + p))
    qblk = lambda off: pl.BlockSpec((tq, LANES), lambda p, i, off=off: (i, off + p))
    wide = jax.ShapeDtypeStruct((rows, WIDTH), F32)
    return pl.pallas_call(
        body, name="fox_fwd", grid=(PAIRS, nq),
        in_specs=[qblk(0), whole(PAIRS), whole(v_off), pl.BlockSpec((1, 2, rows), lambda p, i: (p, 0, 0))],
        out_specs=[qblk(0), qblk(0)], out_shape=[wide, wide],
        compiler_params=_params(("parallel", "arbitrary")),
    )(fqk, fqk, proj, frow)


def _attention_delta(fqk, proj, frow, lse, dao, rows):
    tq = tk = min(ATT_TQ, rows)
    nq = rows // tq
    v_off = 3072 // LANES

    def body(q_ref, k_ref, v_ref, fr_ref, lse_ref, do_ref, delta_ref):
        qi = pl.program_id(1)
        q, d_o, lse_t = q_ref[...] * SCALE, do_ref[...], lse_ref[...]
        keep_q = _head_masks(tq)
        qh = [jnp.where(keep_q[h], q, 0.0).astype(BF16) for h in range(2)]
        doh = [jnp.where(keep_q[h], d_o, 0.0).astype(BF16) for h in range(2)]
        lse_h = [_head_col(lse_t, keep_q[0], h) for h in range(2)]

        def tile(ki, carry, diag):
            k0 = pl.multiple_of(ki * tk, tk)
            kt = k_ref[pl.ds(k0, tk), :].astype(BF16)
            vt = v_ref[pl.ds(k0, tk), :].astype(BF16)
            out = [None, None]

            def chain(h):
                s = _att_scores(qh[h], kt, fr_ref[0, pl.ds(h, 1), pl.ds(k0, tk)], diag)
                dp = _dot(doh[h], vt, _CONTRACT["nt"])
                yield
                out[h] = carry[h] + jnp.sum(jnp.exp(s - lse_h[h]) * dp, axis=-1, keepdims=True)

            _interleave([chain(0), chain(1)])
            return tuple(out)

        zero = jnp.zeros((tq, 1), F32)
        carry = lax.fori_loop(0, qi, lambda ki, c: tile(ki, c, False), (zero, zero))
        d0, d1 = tile(qi, carry, True)
        delta_ref[...] = jnp.where(keep_q[0], d0, d1)

    whole = lambda off: pl.BlockSpec((rows, LANES), lambda p, i, off=off: (0, off + p))
    qblk = lambda off: pl.BlockSpec((tq, LANES), lambda p, i, off=off: (i, off + p))
    return pl.pallas_call(
        body, name="fox_delta", grid=(PAIRS, nq),
        in_specs=[qblk(0), whole(PAIRS), whole(v_off),
                  pl.BlockSpec((1, 2, rows), lambda p, i: (p, 0, 0)), qblk(0), qblk(0)],
        out_specs=qblk(0), out_shape=jax.ShapeDtypeStruct((rows, WIDTH), F32),
        compiler_params=_params(("parallel", "arbitrary")),
    )(fqk, fqk, proj, frow, lse, dao)


def _attention_backward(fqk, proj, frow, ao, lse, dao, rows):
    tq = tk = min(ATT_TQ, rows)
    nq = rows // tq
    v_off = 3072 // LANES

    def body(q_ref, k_ref, v_ref, fr_ref, o_ref, lse_ref, do_ref, dq_ref, dk_ref, dv_ref, dfr_ref):
        ki = pl.program_id(1)

        @pl.when(ki == 0)
        def _():
            dq_ref[...] = jnp.zeros_like(dq_ref)

        keep_q, keep_k = _head_masks(tq), _head_masks(tk)
        k_t = k_ref[...]
        kt = k_t.astype(BF16)
        vt = v_ref[...].astype(BF16)
        kh = [jnp.where(keep_k[h], k_t, 0.0).astype(BF16) for h in range(2)]
        fk = [fr_ref[0, pl.ds(h, 1), :] for h in range(2)]

        def tile(qi, carry, diag):
            dk, dv, df0, df1 = carry
            rows_q = pl.ds(pl.multiple_of(qi * tq, tq), tq)
            q, d_o, lse_t = q_ref[rows_q, :] * SCALE, do_ref[rows_q, :], lse_ref[rows_q, :]
            delta_x = _pair_sum(d_o.astype(BF16).astype(F32) * o_ref[rows_q, :])
            res = [None, None]

            def chain(h):
                qh = jnp.where(keep_q[h], q, 0.0).astype(BF16)
                doh = jnp.where(keep_q[h], d_o, 0.0).astype(BF16)
                lse_h, delta_h = _head_col(lse_t, keep_q[0], h), _head_col(delta_x, keep_q[0], h)
                yield
                s, dp = _att_scores(qh, kt, fk[h], diag), _dot(doh, vt, _CONTRACT["nt"])
                yield
                p = jnp.exp(s - lse_h)
                ds = p * (dp - delta_h)
                yield
                res[h] = (_dot(p, doh, _CONTRACT["tn"]), _dot(ds, qh, _CONTRACT["tn"]),
                          _dot(ds, kh[h], _CONTRACT["nn"]), jnp.sum(ds, axis=0, keepdims=True))

            _interleave([chain(0), chain(1)])
            (dv0, dk0, dq0, s0), (dv1, dk1, dq1, s1) = res
            dq_ref[rows_q, :] += (dq0 + dq1) * SCALE
            return dk + dk0 + dk1, dv + dv0 + dv1, df0 - s0, df1 - s1

        zero_kv = jnp.zeros((tk, LANES), F32)
        zero_f = jnp.zeros((1, tk), F32)
        carry = tile(ki, (zero_kv, zero_kv, zero_f, zero_f), True)
        dk, dv, df0, df1 = lax.fori_loop(ki + 1, nq, lambda qi, c: tile(qi, c, False), carry)
        dk_ref[...] = dk
        dv_ref[...] = dv.astype(dv_ref.dtype)
        dfr_ref[0, pl.ds(0, 1), :] = df0
        dfr_ref[0, pl.ds(1, 1), :] = df1

    whole = lambda off: pl.BlockSpec((rows, LANES), lambda p, i, off=off: (0, off + p))
    kblk = lambda off: pl.BlockSpec((tk, LANES), lambda p, i, off=off: (i, off + p))
    fr_spec = pl.BlockSpec((1, 2, tk), lambda p, i: (p, 0, i))
    wide = jax.ShapeDtypeStruct((rows, WIDTH), F32)
    return pl.pallas_call(
        body, name="fox_bwd", grid=(PAIRS, nq),
        in_specs=[whole(0), kblk(PAIRS), kblk(v_off), fr_spec, whole(0), whole(0), whole(0)],
        out_specs=[whole(0), kblk(0), kblk(0), fr_spec],
        out_shape=[wide, wide, jax.ShapeDtypeStruct((rows, WIDTH), BF16),
                   jax.ShapeDtypeStruct((PAIRS, 2, rows), F32)],
        compiler_params=_params(("parallel", "arbitrary")),
    )(fqk, fqk, proj, frow, ao, lse, dao)


def _lane_ids(shape):
    return lax.broadcasted_iota(jnp.int32, shape, len(shape) - 1)


def _gates_elem(a_log, dt_bias, f_bias, pre):
    lane = _lane_ids(pre.shape)
    beta = jax.nn.sigmoid(pre)
    g = -jnp.exp(a_log) * _softplus(pre + dt_bias)
    lf = -_softplus(-(pre + f_bias))
    return jnp.where(lane < 8, beta, jnp.where(lane < 16, g, jnp.where(lane < 24, lf, 0.0)))


def _tri_consts():
    r = np.arange(LANES)[:, None]
    c = np.arange(LANES)[None, :]
    full = (c <= r).astype(np.float32)
    chunked = full * ((r // CHUNK) == (c // CHUNK))
    return jnp.asarray(chunked), jnp.asarray(full)


def _cums_fwd(lc, lf, gates):
    rows = gates.shape[0]
    lane = _lane_ids((LANES, LANES))
    carry = jnp.zeros((1, LANES), F32)
    out = []
    for r in range(rows // LANES):
        blk = gates[r * LANES:(r + 1) * LANES]
        gc = _dot32(lc, blk, _CONTRACT["nn"])
        f = _dot32(lf, blk, _CONTRACT["nn"]) + carry
        carry = carry + jnp.sum(blk, axis=0, keepdims=True)
        out.append(jnp.where((lane >= 8) & (lane < 16), gc, jnp.where((lane >= 16) & (lane < 24), f, 0.0)))
    return jnp.concatenate(out, axis=0)


def _cums_bwd(lc, lf, dcums):
    rows = dcums.shape[0]
    lane = _lane_ids((LANES, LANES))
    is_g = (lane >= 8) & (lane < 16)
    is_f = (lane >= 16) & (lane < 24)
    carry = jnp.zeros((1, LANES), F32)
    out = [None] * (rows // LANES)
    for r in reversed(range(rows // LANES)):
        blk = dcums[r * LANES:(r + 1) * LANES]
        dg = jnp.where(is_g, blk, 0.0)
        df = jnp.where(is_f, blk, 0.0)
        out[r] = _dot32(lc, dg, _CONTRACT["tn"]) + _dot32(lf, df, _CONTRACT["tn"]) + carry
        carry = carry + jnp.sum(df, axis=0, keepdims=True)
    return jnp.concatenate(out, axis=0)


def _expand_consts():
    xb = np.zeros((LANES, WIDTH), np.float32)
    xg = np.zeros((LANES, WIDTH), np.float32)
    for h in range(HEADS):
        xb[h, h * HEAD_DIM:(h + 1) * HEAD_DIM] = 1.0
        xg[8 + h, h * HEAD_DIM:(h + 1) * HEAD_DIM] = 1.0
    return jnp.asarray(xb), jnp.asarray(xg)


def _shift_down(x, s):
    if s == 0:
        return x
    row = lax.broadcasted_iota(jnp.int32, x.shape, 0)
    return jnp.where(row >= s, pltpu.roll(x, s, 0), 0.0)


def _shift_up(x, s):
    if s == 0:
        return x
    n = x.shape[0]
    row = lax.broadcasted_iota(jnp.int32, x.shape, 0)
    return jnp.where(row < n - s, pltpu.roll(x, n - s, 0), 0.0)


def _row_of(cw, i):
    row = lax.broadcasted_iota(jnp.int32, cw.shape, 0)
    return jnp.sum(jnp.where(row == i, cw, 0.0), axis=0, keepdims=True)


def _conv(cw, x):
    c = jnp.zeros_like(x)
    for i in range(CONV_K):
        c = c + _row_of(cw, i) * _shift_down(x, CONV_K - 1 - i)
    return c


def _post_conv(is_qk, c):
    s = _silu(c)
    n = s * lax.rsqrt(_pair_sum(s * s) + EPS)
    return jnp.where(is_qk, n, s)


def _gdn_prep_fwd(col, cw, x):
    return (_post_conv(col < 2 * PAIRS, _conv(cw, x)),)


def _gdn_prep_bwd(is_qk, cw, x, dy):
    c = _conv(cw, x)
    _, vjp = jax.vjp(lambda cc: _post_conv(is_qk, cc), c)
    (dc,) = vjp(dy)
    dx = jnp.zeros_like(x)
    row = lax.broadcasted_iota(jnp.int32, cw.shape, 0)
    dcw = jnp.zeros(cw.shape, F32)
    for i in range(CONV_K):
        s = CONV_K - 1 - i
        dx = dx + _row_of(cw, i) * _shift_up(dc, s)
        dcw = dcw + jnp.where(row == i, jnp.sum(dc * _shift_down(x, s), axis=0, keepdims=True), 0.0)
    return dx, dcw


def _head_rms(w, x):
    return x * lax.rsqrt(_pair_sum(x * x) / HEAD_DIM + EPS) * w


def _cat_weights(w_in_t):
    tail = jnp.pad(w_in_t[4112:4120], ((0, D_CAT - D_IN), (0, 0)))
    return jnp.concatenate([w_in_t[:2048], w_in_t[2064:4112], w_in_t[2048:2064], tail], axis=0)


def _uncat_grad(g):
    return jnp.concatenate([g[:2048], g[4096:4112], g[2048:4096], g[4112:4120]], axis=0)


def _lanes_to_rowform(v8, rows):
    return v8.reshape(rows // CHUNK, CHUNK, HEADS).transpose(0, 2, 1).reshape(rows // CHUNK, PAIRS, 1, LANES)


def _rowform_to_lanes(v, rows):
    return v.reshape(rows // CHUNK, HEADS, CHUNK).transpose(0, 2, 1).reshape(rows, HEADS)


def _local_step(x, target, norm1_w, a_log, dt_bias, out_norm_w, f_bias, q_norm_w, k_norm_w,
                norm2_w, final_w, first_weights, late_weights, early_grads_ready):
    rows = x.shape[0]
    tm = min(512, rows)
    lc, lf = _tri_consts()
    xb, xg = _expand_consts()

    (h1,) = _tiles(lambda col, w, xx: (_rms(xx, w),), name="norm1", rows=rows, tm=tm,
                   full_consts=[norm1_w], row_ins=[(x, D_MODEL, 0)], row_outs=[(D_MODEL, BF16)])
    w_cat, conv_w = first_weights(h1)
    proj = _mm(h1, w_cat, dims="nt", name="in_proj", tn=1408, tk=1024)

    lane_pad = lambda v, off: jnp.pad(v.reshape(1, -1), ((0, 0), (off, LANES - off - v.size)))
    p_a, p_dt, p_fb = lane_pad(a_log, 8), lane_pad(dt_bias, 8), lane_pad(f_bias, 16)

    def gates_fwd(col, lcv, lfv, a, dt, fb, pre):
        gates = _gates_elem(a, dt, fb, pre)
        return gates, _cums_fwd(lcv, lfv, gates)

    gates, cums = _tiles(gates_fwd, name="gates", rows=rows, tm=rows,
                         full_consts=[lc, lf, p_a, p_dt, p_fb], row_ins=[(proj, LANES, COL_SMALL)],
                         row_outs=[(LANES, F32), (LANES, F32)])

    def expand_fwd(col, b, g, gt, cm):
        return (_dot32(gt, b, _CONTRACT["nn"]), _dot32(cm, g, _CONTRACT["nn"]))

    betax, gcx = _tiles(expand_fwd, name="expand", rows=rows, tm=tm, full_consts=[xb, xg],
                        row_ins=[(gates, LANES, 0), (cums, LANES, 0)],
                        row_outs=[(WIDTH, F32)] * 2)
    grow = _lanes_to_rowform(cums[:, 8:16], rows)
    frow = cums[:, 16:24].T.reshape(PAIRS, 2, rows)

    (qkv,) = _tiles(_gdn_prep_fwd, name="gdn_prep", rows=rows, tm=rows, ncol=3 * PAIRS,
                    col_consts=[(conv_w, CONV_K, LANES, 0)], row_ins=[(proj, LANES, 0)],
                    row_outs=[(LANES, F32)])
    o_gdn, ssave, tsave = _gdn_forward(qkv, betax, gcx, grow, rows)

    w_qk = jnp.concatenate([jnp.tile(q_norm_w.reshape(1, -1), (1, HEADS)),
                            jnp.tile(k_norm_w.reshape(1, -1), (1, HEADS))], axis=1)
    fox_off = 2048 // LANES
    (fqk,) = _tiles(lambda col, w, xx: (_head_rms(w, xx),), name="fox_prep", rows=rows, tm=rows, ncol=2 * PAIRS,
                    col_consts=[(w_qk, 1, LANES, 0)], row_ins=[(proj, LANES, fox_off)],
                    row_outs=[(LANES, F32)])
    ao, lse = _attention_forward(fqk, proj, frow, rows)

    w_on = jnp.tile(out_norm_w.reshape(1, -1), (1, 2))
    z_off, fg_off = 1536 // LANES, 3584 // LANES
    mix_g_fn = lambda w, o, z: _head_rms(w, o) * _silu(z)
    mix_f_fn = lambda a, g: a * jax.nn.sigmoid(g)
    (mix_g,) = _tiles(lambda col, w, o, z: (mix_g_fn(w, o, z),), name="mix_gdn", rows=rows, tm=rows, ncol=PAIRS,
                      full_consts=[w_on], row_ins=[(o_gdn, LANES, 0), (proj, LANES, z_off)],
                      row_outs=[(LANES, BF16)])
    (mix_f,) = _tiles(lambda col, a, g: (mix_f_fn(a, g),), name="mix_fox", rows=rows, tm=rows, ncol=PAIRS,
                      row_ins=[(ao, LANES, 0), (proj, LANES, fg_off)], row_outs=[(LANES, BF16)])
    mix = jnp.concatenate([mix_g, mix_f], axis=1)
    w_out, w_gate, w_up, w_down = late_weights(mix)
    t_rows = min(1024, rows)
    n_rt = rows // t_rows
    row_blk = pl.BlockSpec((t_rows, D_MODEL), lambda i, n: (i, 0))
    vec_blk = pl.BlockSpec((1, D_MODEL), lambda i, n: (0, 0))
    wide = lambda dt: jax.ShapeDtypeStruct((rows, D_MODEL), dt)
    x1, h2 = _mm_blocks(mix, w_out, name="out_proj_norm2", grid=(n_rt, 1), dims="nn",
                        a_spec=row_blk, b_spec=pl.BlockSpec((D_MODEL, D_MODEL), lambda i, n: (0, 0)),
                        o_spec=[row_blk, row_blk], out_shape=[wide(F32), wide(BF16)], add=x, add_spec=row_blk,
                        extra=[(norm2_w, vec_blk)], epilogue=lambda r, w: (r, _rms(r, w)))
    t_cols = D_MODEL
    st_act = jax.ShapeDtypeStruct((N_CHIPS, rows, FF_SHARD), BF16)
    st_rows = pl.BlockSpec((None, rows, FF_SHARD), lambda i, j: (j, i, 0))
    out_rows = pl.BlockSpec((t_rows, t_cols), lambda i, n: (i, n))
    flat = lambda t: t.reshape(N_CHIPS * rows, FF_SHARD)

    def ffn_in(w_st, name):
        return _mm_blocks(h2, w_st, name=name, grid=(1, N_CHIPS), dims="nt",
                          a_spec=pl.BlockSpec((rows, D_MODEL), lambda i, j: (i, 0)),
                          b_spec=pl.BlockSpec((None, FF_SHARD, D_MODEL), lambda i, j: (j, 0, 0)),
                          o_spec=st_rows, out_shape=st_act)

    gate = ffn_in(w_gate, "ffn_gate")
    act_fn = lambda g, u: _silu(g) * u
    st_tile = pl.BlockSpec((None, t_rows, FF_SHARD), lambda i, j: (j, i, 0))
    up, act = _mm_blocks(h2, w_up, name="ffn_up_act", grid=(n_rt, N_CHIPS), dims="nt",
                         a_spec=pl.BlockSpec((t_rows, D_MODEL), lambda i, j: (i, 0)),
                         b_spec=pl.BlockSpec((None, FF_SHARD, D_MODEL), lambda i, j: (j, 0, 0)),
                         o_spec=[st_tile, st_tile], out_shape=[st_act, st_act], extra=[(gate, st_tile)],
                         epilogue=lambda u, g: (u, act_fn(g.astype(F32), u)))
    def final_fn(xx, tgt, w):
        y, vjp = jax.vjp(_rms, xx, w)
        err = y - tgt
        loss = 0.5 * jnp.sum(err * err) / D_MODEL
        dx, dw = vjp(err / D_MODEL)
        return dx, dx, jnp.full((1, LANES), loss, F32), dw

    t_half = min(512, rows)
    half_blk = pl.BlockSpec((t_half, D_MODEL), lambda i, n: (i, 0))
    dx2, dx2_b, loss, d_final_w = _mm_blocks(
        act, w_down, name="ffn_down_loss", grid=(rows // t_half, 1), dims="nn", n_sum=N_CHIPS,
        a_spec=pl.BlockSpec((N_CHIPS, t_half, FF_SHARD), lambda i, n: (0, i, 0)),
        b_spec=pl.BlockSpec((N_CHIPS, FF_SHARD, D_MODEL), lambda i, n: (0, 0, 0)),
        o_spec=[half_blk, half_blk, pl.BlockSpec((1, LANES), lambda i, n: (0, 0)), vec_blk],
        out_shape=[wide(F32), wide(BF16), jax.ShapeDtypeStruct((1, LANES), F32),
                   jax.ShapeDtypeStruct((1, D_MODEL), F32)],
        add=x1, add_spec=half_blk, extra=[(target, half_blk), (final_w, vec_blk)], epilogue=final_fn, n_acc=2)

    def act_bwd(d, g, u):
        _, vjp = jax.vjp(act_fn, g.astype(F32), u.astype(F32))
        return vjp(d)

    dgate, dup = _mm_blocks(dx2_b, w_down, name="d_act_gate_up", grid=(n_rt, N_CHIPS), dims="nt",
                            a_spec=pl.BlockSpec((t_rows, D_MODEL), lambda i, j: (i, 0)),
                            b_spec=pl.BlockSpec((None, FF_SHARD, D_MODEL), lambda i, j: (j, 0, 0)),
                            o_spec=[st_tile, st_tile], out_shape=[st_act, st_act],
                            extra=[(gate, st_tile), (up, st_tile)], epilogue=act_bwd)

    def g_ffn(d_st, other, name):
        return _mm_blocks(d_st, other, name=name, grid=(N_CHIPS, D_MODEL // t_cols), dims="tn",
                          a_spec=pl.BlockSpec((None, rows, FF_SHARD), lambda j, n: (j, 0, 0)),
                          b_spec=pl.BlockSpec((rows, t_cols), lambda j, n: (0, n)),
                          o_spec=pl.BlockSpec((None, FF_SHARD, t_cols), lambda j, n: (j, 0, n)),
                          out_shape=jax.ShapeDtypeStruct((N_CHIPS, FF_SHARD, D_MODEL), BF16))

    g_down = g_ffn(act, dx2_b, "g_down")

    def norm_bwd(dh, xx, dres, w):
        _, vjp = jax.vjp(_rms, xx, w)
        dx, dw = vjp(dh)
        return dx + dres, dx + dres, dw

    t_half = min(512, rows)
    half_blk = pl.BlockSpec((t_half, D_MODEL), lambda i, n: (i, 0))

    def d_h2(d_st, w_st, name, add, **fused):
        return _mm_blocks(d_st, w_st, name=name, grid=(rows // t_half, 1), dims="nn", n_sum=N_CHIPS,
                          a_spec=pl.BlockSpec((N_CHIPS, t_half, FF_SHARD), lambda i, n: (0, i, 0)),
                          b_spec=pl.BlockSpec((N_CHIPS, FF_SHARD, D_MODEL), lambda i, n: (0, 0, 0)),
                          add=add, add_spec=half_blk, **fused)

    dh2_gate = d_h2(dgate, w_gate, "d_h2_gate", None, o_spec=half_blk, out_shape=wide(F32))
    dx1, dx1_b, d_norm2_w = d_h2(
        dup, w_up, "d_h2_up_norm2_bwd", dh2_gate, o_spec=[half_blk, half_blk, vec_blk],
        out_shape=[wide(F32), wide(BF16), jax.ShapeDtypeStruct((1, D_MODEL), F32)],
        extra=[(x1, half_blk), (dx2, half_blk), (norm2_w, vec_blk)], epilogue=norm_bwd, n_acc=1)
    g_gate, g_up = g_ffn(dgate, h2, "g_gate"), g_ffn(dup, h2, "g_up")
    dmix = _mm(dx1_b, w_out, dims="nt", name="d_mix", tn=D_MODEL, tk=1024)
    g_out = _mm(mix, dx1_b, dims="tn", name="g_out", tn=D_MODEL, tk=rows, out_dtype=BF16)
    w_on = w_on + early_grads_ready(g_out, g_gate, g_up, g_down)

    def mix_g_bwd(col, w, o, z, d):
        _, vjp = jax.vjp(mix_g_fn, w, o, z)
        dw, do_, dz = vjp(d)
        return do_, dz, dw

    do_gdn, dz, d_on = _tiles(mix_g_bwd, name="mix_gdn_bwd", rows=rows, tm=rows, ncol=PAIRS, full_consts=[w_on],
                              row_ins=[(o_gdn, LANES, 0), (proj, LANES, z_off), (dmix, LANES, 0)],
                              row_outs=[(LANES, F32), (LANES, BF16)], acc_outs=[(1, LANES)])

    def mix_f_bwd(col, a, g, d):
        _, vjp = jax.vjp(mix_f_fn, a, g)
        return vjp(d)

    dao, dfgate = _tiles(mix_f_bwd, name="mix_fox_bwd", rows=rows, tm=rows, ncol=PAIRS,
                         row_ins=[(ao, LANES, 0), (proj, LANES, fg_off), (dmix, LANES, PAIRS)],
                         row_outs=[(LANES, F32), (LANES, BF16)])

    dfq, dfk, dfv, dfrow = _attention_backward(fqk, proj, frow, ao, lse, dao, rows)

    def fox_prep_bwd(col, w, xx, d):
        _, vjp = jax.vjp(_head_rms, w, xx)
        dw, dx = vjp(d)
        return dx, dw

    dfqk, d_wqk = [], []
    for part, d_n in enumerate((dfq, dfk)):
        dx_p, dw_p = _tiles(fox_prep_bwd, name="fox_prep_bwd_" + "qk"[part], rows=rows, tm=rows, ncol=PAIRS,
                            col_consts=[(w_qk, 1, LANES, part * PAIRS)],
                            row_ins=[(proj, LANES, fox_off + part * PAIRS), (d_n, LANES, 0)],
                            row_outs=[(LANES, BF16)], acc_outs=[(1, LANES)])
        dfqk.append(dx_p)
        d_wqk.append(dw_p)

    dq, dk, dv, dbetax, dgcx, dgrow = _gdn_backward(qkv, betax, gcx, grow, ssave, tsave, do_gdn, rows)
    dqkv, d_conv = [], []
    for part, d_n in enumerate((dq, dk, dv)):
        prep_bwd = lambda col, cw, xx, dy, is_qk=(part < 2): _gdn_prep_bwd(is_qk, cw, xx, dy)
        dx_p, dw_p = _tiles(prep_bwd, name="gdn_prep_bwd_" + "qkv"[part], rows=rows, tm=rows, ncol=PAIRS,
                            col_consts=[(conv_w, CONV_K, LANES, part * PAIRS)],
                            row_ins=[(proj, LANES, part * PAIRS), (d_n, LANES, 0)],
                            row_outs=[(LANES, BF16)], acc_outs=[(CONV_K, LANES)])
        dqkv.append(dx_p)
        d_conv.append(dw_p)
    d_conv = jnp.concatenate(d_conv, axis=1)

    def expand_bwd(col, b, g, db, dg):
        return (_dot32(db, b, _CONTRACT["nt"]), _dot32(dg, g, _CONTRACT["nt"]))

    dgates_b, dcums_g = _tiles(expand_bwd, name="expand_bwd", rows=rows, tm=tm, full_consts=[xb, xg],
                               row_ins=[(dbetax, WIDTH, 0), (dgcx, WIDTH, 0)],
                               row_outs=[(LANES, F32), (LANES, F32)])
    dcums_row = jnp.concatenate([jnp.zeros((rows, 8), F32), _rowform_to_lanes(dgrow, rows),
                                 dfrow.reshape(HEADS, rows).T, jnp.zeros((rows, LANES - 24), F32)], axis=1)

    def gates_bwd(col, lcv, lfv, a, dt, fb, pre, dgb, dcg, dcr):
        lane = _lane_ids(pre.shape)
        dgates = jnp.where(lane < 8, dgb, _cums_bwd(lcv, lfv, dcg + dcr))
        _, vjp = jax.vjp(_gates_elem, a, dt, fb, pre)
        da, ddt, dfb, dpre = vjp(dgates)
        return dpre, da, ddt, dfb

    dpre, d_a, d_dt, d_fb = _tiles(gates_bwd, name="gates_bwd", rows=rows, tm=rows,
                                   full_consts=[lc, lf, p_a, p_dt, p_fb],
                                   row_ins=[(proj, LANES, COL_SMALL), (dgates_b, LANES, 0), (dcums_g, LANES, 0),
                                            (dcums_row, LANES, 0)],
                                   row_outs=[(LANES, BF16)], acc_outs=[(1, LANES)] * 3)

    dproj = jnp.concatenate(dqkv + [dz] + dfqk + [dfv, dfgate, dpre], axis=1)
    grad_x, d_norm1_w = _mm_blocks(
        dproj, w_cat, name="d_h1_norm1_bwd", grid=(rows // t_half, 1), dims="nn",
        a_spec=pl.BlockSpec((t_half, D_CAT), lambda i, n: (i, 0)),
        b_spec=pl.BlockSpec((D_CAT, D_MODEL), lambda i, n: (0, 0)),
        o_spec=[half_blk, vec_blk], out_shape=[wide(F32), jax.ShapeDtypeStruct((1, D_MODEL), F32)],
        extra=[(x, half_blk), (dx1, half_blk), (norm1_w, vec_blk)],
        epilogue=lambda dh, xx, dres, w: norm_bwd(dh, xx, dres, w)[1:], n_acc=1)
    g_cat = _mm(dproj, h1, dims="tn", name="g_in", tm=1408, tn=D_MODEL, tk=rows)

    fold = lambda v: v.reshape(-1, HEAD_DIM).sum(axis=0)
    small = dict(
        loss=loss[0, 0],
        norm1_w=d_norm1_w, conv_w=d_conv, a_log=d_a[0, 8:16], dt_bias=d_dt[0, 8:16],
        out_norm_w=fold(d_on), f_bias=d_fb[0, 16:24], q_norm_w=fold(d_wqk[0]),
        k_norm_w=fold(d_wqk[1]), norm2_w=d_norm2_w, final_w=d_final_w)
    return grad_x, g_cat, g_out, g_gate, g_up, g_down, small


HBM_SPEC = pl.BlockSpec(memory_space=pltpu.HBM)


def _place():
    x, y, c = lax.axis_index("x"), lax.axis_index("y"), lax.axis_index("c")
    chips = [(1 - x, y), (x, 1 - y), (1 - x, 1 - y)]
    return x, y, c, 2 * x + y, (x, y, 1 - c), chips, [2 * cx + cy for cx, cy in chips]


def _remote(src, dst, send_sem, recv_sem, to):
    return pltpu.make_async_remote_copy(src_ref=src, dst_ref=dst, send_sem=send_sem, recv_sem=recv_sem,
                                        device_id=to, device_id_type=MESH)


def _allgather_weights(shards, conv):
    n = len(shards)
    halves = [s.shape[1] // 2 for s in shards]
    per = 6
    own_base = n * per + 3

    def body(*refs):
        ins, conv_in = refs[:n], refs[n]
        outs, conv_out = refs[n + 1:2 * n + 1], refs[2 * n + 1]
        send_sems, recv_sems = refs[2 * n + 2:]
        x, y, c, own, sib, chips, chip_idx = _place()

        def half(i, ref, hc):
            return ref.at[:, pl.ds(pl.multiple_of(hc * halves[i], LANES), halves[i])]

        sent = []
        for i, (src, dst) in enumerate(zip(list(ins) + [conv_in], list(outs) + [conv_out])):
            k = own_base + i
            sent.append(_remote(src, dst.at[own], send_sems.at[k], recv_sems.at[k], sib))
        for i in range(n):
            for j, chip in enumerate(chips):
                k = i * per + j
                sent.append(_remote(half(i, ins[i], c), half(i, outs[i].at[own], c),
                                    send_sems.at[k], recv_sems.at[k], (*chip, c)))
        for j, chip in enumerate(chips):
            k = n * per + j
            sent.append(_remote(conv_in, conv_out.at[own], send_sems.at[k], recv_sems.at[k], (*chip, c)))
        for cp in sent:
            cp.start()
        for i in range(n):
            for j in range(len(chips)):
                k = i * per + j
                landed = half(i, outs[i].at[chip_idx[j]], c)
                _remote(landed, landed, send_sems.at[k], recv_sems.at[k], sib).wait_recv()
                fwd = _remote(landed, landed, send_sems.at[k + 3], recv_sems.at[k + 3], sib)
                fwd.start()
                sent.append(fwd)
        for i in range(n):
            for j in range(len(chips)):
                k = i * per + 3 + j
                landed = half(i, outs[i].at[chip_idx[j]], 1 - c)
                _remote(landed, landed, send_sems.at[k], recv_sems.at[k], sib).wait_recv()
        for j in range(len(chips)):
            k = n * per + j
            landed = conv_out.at[chip_idx[j]]
            _remote(landed, landed, send_sems.at[k], recv_sems.at[k], sib).wait_recv()
        for i, dst in enumerate(list(outs) + [conv_out]):
            k = own_base + i
            landed = dst.at[own]
            _remote(landed, landed, send_sems.at[k], recv_sems.at[k], sib).wait_recv()
        for cp in sent:
            cp.wait_send()

    n_sem = own_base + n + 1
    out_shape = [jax.ShapeDtypeStruct((N_CHIPS,) + s.shape, s.dtype) for s in shards]
    out_shape.append(jax.ShapeDtypeStruct((N_CHIPS,) + conv.shape, conv.dtype))
    res = pl.pallas_call(
        body, name="allgather_weights", out_shape=out_shape,
        in_specs=[HBM_SPEC] * (n + 1), out_specs=[HBM_SPEC] * (n + 1),
        scratch_shapes=[pltpu.SemaphoreType.DMA((n_sem,)), pltpu.SemaphoreType.DMA((n_sem,))],
    )(*shards, conv)
    return res[:n], res[n]


SEM_SPEC = pl.BlockSpec(memory_space=pltpu.SEMAPHORE)
ANY_SPEC = pl.BlockSpec(memory_space=pl.ANY)
DATAFLOW = pltpu.SideEffectType.DATAFLOW_SIDE_EFFECTING


def _gather_plan(srcs, lands):
    x, y, c, own, sib, chips, chip_idx = _place()
    plan = []
    for src, land in zip(srcs, lands):
        for j, chip in enumerate(chips):
            plan.append((src, land.at[own], (*chip, c), land.at[chip_idx[j]]))
        plan.append((src, land.at[own], sib, land.at[own]))
    return plan


def _exchange_plan(srcs, lands):
    x, y, c, own, sib, chips, chip_idx = _place()
    plan = []
    for src, land in zip(srcs, lands):
        for j, chip in enumerate(chips):
            plan.append((src.at[chip_idx[j]], land.at[j], (*chip, c), land.at[j]))
    return plan


def _in_proj_plan(srcs, lands):
    x, y, c, own, sib, chips, chip_idx = _place()
    (w, conv), (w_land, conv_land) = srcs, lands
    hw = w.shape[1] // 2
    half = lambda ref: ref.at[:, pl.ds(pl.multiple_of(c * hw, LANES), hw)]
    plan = []
    for j, chip in enumerate(chips):
        plan.append((half(w), half(w_land.at[own]), (*chip, c), half(w_land.at[chip_idx[j]])))
        plan.append((conv, conv_land.at[own], (*chip, c), conv_land.at[chip_idx[j]]))
    plan.append((w, w_land.at[own], sib, w_land.at[own]))
    plan.append((conv, conv_land.at[own], sib, conv_land.at[own]))
    return plan


def _forward_halves(landed):
    hw = landed.shape[2] // 2

    def body(in_ref, out_ref, send_sems, recv_sems):
        x, y, c, own, sib, chips, chip_idx = _place()
        half = lambda ref, hc: ref.at[:, pl.ds(pl.multiple_of(hc * hw, LANES), hw)]
        sent = [_remote(half(out_ref.at[chip_idx[j]], c), half(out_ref.at[chip_idx[j]], c),
                        send_sems.at[j], recv_sems.at[j], sib) for j in range(3)]
        for cp in sent:
            cp.start()
        for j in range(3):
            other = half(out_ref.at[chip_idx[j]], 1 - c)
            _remote(other, other, send_sems.at[j], recv_sems.at[j], sib).wait_recv()
        for cp in sent:
            cp.wait_send()

    return pl.pallas_call(
        body, name="gather_in_forward", out_shape=jax.ShapeDtypeStruct(landed.shape, landed.dtype),
        in_specs=[HBM_SPEC], out_specs=HBM_SPEC, input_output_aliases={0: 0},
        scratch_shapes=[pltpu.SemaphoreType.DMA((3,)), pltpu.SemaphoreType.DMA((3,))],
    )(landed)


def _split_start(name, plan_fn, srcs, land_shapes, n_copies, after):
    n = len(srcs)

    def body(*refs):
        src_refs, land_refs = refs[:n], refs[n:2 * n]
        send_sems, recv_sems = refs[2 * n + 1], refs[2 * n + 2]
        token = refs[-1]
        for k, (src, dst, to, _) in enumerate(plan_fn(src_refs, land_refs)):
            _remote(src, dst, send_sems.at[k], recv_sems.at[k], to).start()
        token[...] = jnp.zeros_like(token)

    lands = [pltpu.with_memory_space_constraint(lax.empty(s.shape, s.dtype), pltpu.HBM) for s in land_shapes]
    srcs = [pltpu.with_memory_space_constraint(s, pltpu.HBM) for s in srcs]
    out_shape = ([pltpu.SemaphoreType.DMA((n_copies,)), pltpu.SemaphoreType.DMA((n_copies,))]
                 + [pltpu.HBM(s.shape, s.dtype) for s in srcs] + [pltpu.HBM(s.shape, s.dtype) for s in land_shapes]
                 + [jax.ShapeDtypeStruct((8, LANES), F32)])
    res = pl.pallas_call(
        body, name=name, out_shape=out_shape,
        in_specs=[HBM_SPEC] * (2 * n) + [ANY_SPEC],
        out_specs=[SEM_SPEC, SEM_SPEC] + [HBM_SPEC] * (2 * n) + [pl.BlockSpec(memory_space=pltpu.VMEM)],
        input_output_aliases={i: 2 + i for i in range(2 * n)},
        compiler_params=pltpu.CompilerParams(has_side_effects=DATAFLOW),
    )(*srcs, *lands, after)
    return dict(sems=res[:2], srcs=res[2:2 + n], lands=res[2 + n:2 + 2 * n], token=res[-1], n=n)


def _split_wait(name, plan_fn, started, after):
    n = started["n"]

    def body(*refs):
        src_refs, land_refs = refs[:n], refs[n:2 * n]
        send_sems, recv_sems = refs[2 * n], refs[2 * n + 1]
        for k, (src, _, to, landed) in enumerate(plan_fn(src_refs, land_refs)):
            copy = _remote(src, landed, send_sems.at[k], recv_sems.at[k], to)
            copy.wait_send()
            copy.wait_recv()

    srcs, lands = started["srcs"], started["lands"]
    after = list(after) if isinstance(after, (list, tuple)) else [after]
    res = pl.pallas_call(
        body, name=name,
        out_shape=[pltpu.HBM(s.shape, s.dtype) for s in srcs] + [pltpu.HBM(s.shape, s.dtype) for s in lands],
        in_specs=[HBM_SPEC] * (2 * n) + [SEM_SPEC, SEM_SPEC] + [ANY_SPEC] * len(after),
        out_specs=[HBM_SPEC] * (2 * n),
        input_output_aliases={i: i for i in range(2 * n)},
        compiler_params=pltpu.CompilerParams(has_side_effects=DATAFLOW),
    )(*srcs, *lands, *started["sems"], *after)
    return res[n:]


def _swap_halves(stacks, name):
    n = len(stacks)

    def body(*refs):
        ins, outs = refs[:n], refs[n:2 * n]
        send_sems, recv_sems = refs[2 * n:]
        x, y, c, own, sib, chips, chip_idx = _place()
        cps = []
        for i in range(n):
            h = stacks[i].shape[2] // 2
            src = ins[i].at[:, :, pl.ds(pl.multiple_of((1 - c) * h, LANES), h)]
            cps.append(_remote(src, outs[i], send_sems.at[i], recv_sems.at[i], sib))
        for cp in cps:
            cp.start()
        for cp in cps:
            cp.wait()

    out_shape = [jax.ShapeDtypeStruct((N_CHIPS, s.shape[1], s.shape[2] // 2), s.dtype) for s in stacks]
    return pl.pallas_call(
        body, name=name, out_shape=out_shape,
        in_specs=[HBM_SPEC] * n, out_specs=[HBM_SPEC] * n,
        scratch_shapes=[pltpu.SemaphoreType.DMA((n,)), pltpu.SemaphoreType.DMA((n,))],
    )(*stacks)


def _add_half(stack, landed, place, name):
    _, rows, h = landed.shape

    def body(place_ref, a_ref, b_ref, o_ref, own_ref):
        part = (a_ref[...].astype(F32) + b_ref[...].astype(F32)).astype(o_ref.dtype)
        o_ref[...] = part

        @pl.when(pl.program_id(0) == place_ref[1])
        def _():
            own_ref[...] = part[0]

    return pl.pallas_call(
        body, name=name,
        out_shape=[jax.ShapeDtypeStruct(landed.shape, BF16), jax.ShapeDtypeStruct((rows, h), BF16)],
        grid_spec=pltpu.PrefetchScalarGridSpec(
            num_scalar_prefetch=1, grid=(N_CHIPS,),
            in_specs=[pl.BlockSpec((1, rows, h), lambda j, p: (j, 0, p[0])),
                      pl.BlockSpec((1, rows, h), lambda j, p: (j, 0, 0))],
            out_specs=[pl.BlockSpec((1, rows, h), lambda j, p: (j, 0, 0)),
                       pl.BlockSpec((rows, h), lambda j, p: (0, 0))]),
        compiler_params=_params(("arbitrary",)),
    )(place, stack, landed)


def _exchange_partials(parts):
    n = len(parts)

    def body(*refs):
        ins, outs = refs[:n], refs[n:2 * n]
        send_sems, recv_sems = refs[2 * n:]
        x, y, c, own, sib, chips, chip_idx = _place()
        sent = []
        for i in range(n):
            for j, chip in enumerate(chips):
                k = i * 3 + j
                sent.append(_remote(ins[i].at[chip_idx[j]], outs[i].at[j], send_sems.at[k], recv_sems.at[k],
                                    (*chip, c)))
        for cp in sent:
            cp.start()
        for i in range(n):
            for j in range(len(chips)):
                k = i * 3 + j
                landed = outs[i].at[j]
                _remote(landed, landed, send_sems.at[k], recv_sems.at[k], sib).wait_recv()
        for cp in sent:
            cp.wait_send()

    return pl.pallas_call(
        body, name="rs_exchange_partials",
        out_shape=[jax.ShapeDtypeStruct((3,) + p.shape[1:], p.dtype) for p in parts],
        in_specs=[HBM_SPEC] * n, out_specs=[HBM_SPEC] * n,
        scratch_shapes=[pltpu.SemaphoreType.DMA((3 * n,)), pltpu.SemaphoreType.DMA((3 * n,))],
    )(*parts)


def _sum_partials(own_part, landed, name, untiled_rows=False):
    _, h, cols = landed.shape
    tc = LANES if untiled_rows else cols

    def body(own_ref, a_ref, o_ref):
        acc = own_ref[...].astype(F32)
        for s in range(3):
            acc = acc + a_ref[s].astype(F32)
        if untiled_rows:
            o_ref[:, 0, :] = acc
        else:
            o_ref[...] = acc

    if untiled_rows:
        out_shape, out_spec = jax.ShapeDtypeStruct((h, 1, cols), F32), pl.BlockSpec((h, 1, tc), lambda i: (0, 0, i))
    else:
        out_shape, out_spec = jax.ShapeDtypeStruct((h, cols), F32), pl.BlockSpec((h, tc), lambda i: (0, i))
    return pl.pallas_call(
        body, name=name, out_shape=out_shape, grid=(cols // tc,),
        in_specs=[pl.BlockSpec((h, tc), lambda i: (0, i)), pl.BlockSpec((3, h, tc), lambda i: (0, 0, i))],
        out_specs=out_spec, compiler_params=_params(("arbitrary",)),
    )(own_part, landed)


def _share_halves(halves, name):
    n = len(halves)

    def body(*refs):
        ins, outs = refs[:n], refs[n:2 * n]
        send_sems, recv_sems = refs[2 * n:]
        x, y, c, own, sib, chips, chip_idx = _place()
        cps = [_remote(ins[i], outs[i], send_sems.at[i], recv_sems.at[i], sib) for i in range(n)]
        for cp in cps:
            cp.start()
        for cp in cps:
            cp.wait()

    return pl.pallas_call(
        body, name=name,
        out_shape=[jax.ShapeDtypeStruct(p.shape, p.dtype) for p in halves],
        in_specs=[HBM_SPEC] * n, out_specs=[HBM_SPEC] * n,
        scratch_shapes=[pltpu.SemaphoreType.DMA((n,)), pltpu.SemaphoreType.DMA((n,))],
    )(*halves)


def _allreduce_small(packed):
    rows = packed.shape[0]
    n_dev = 8

    def body(in_ref, out_ref, gath, send_sems, recv_sems):
        x, y, c = lax.axis_index("x"), lax.axis_index("y"), lax.axis_index("c")
        me = 4 * x + 2 * y + c
        gath[me] = in_ref[...]
        cps = []
        for k in range(1, n_dev):
            fx, fy, fc = (k >> 2) & 1, (k >> 1) & 1, k & 1
            to = (x ^ fx, y ^ fy, c ^ fc)
            cps.append(_remote(in_ref, gath.at[me], send_sems.at[k - 1], recv_sems.at[k - 1], to))
        for cp in cps:
            cp.start()
        for k in range(1, n_dev):
            fx, fy, fc = (k >> 2) & 1, (k >> 1) & 1, k & 1
            src = 4 * (x ^ fx) + 2 * (y ^ fy) + (c ^ fc)
            slot = gath.at[src]
            _remote(slot, slot, send_sems.at[k - 1], recv_sems.at[k - 1], (x, y, c)).wait_recv()
        for cp in cps:
            cp.wait_send()
        acc = gath[0]
        for d in range(1, n_dev):
            acc = acc + gath[d]
        out_ref[...] = acc

    vm = pl.BlockSpec(memory_space=pltpu.VMEM)
    return pl.pallas_call(
        body, name="allreduce_small", out_shape=jax.ShapeDtypeStruct(packed.shape, F32),
        in_specs=[vm], out_specs=vm,
        scratch_shapes=[pltpu.VMEM((n_dev, rows, LANES), F32),
                        pltpu.SemaphoreType.DMA((n_dev - 1,)), pltpu.SemaphoreType.DMA((n_dev - 1,))],
    )(packed)


def _adam(col, w, g, m, v):
    m2 = ADAM_B1 * m + (1.0 - ADAM_B1) * g
    v2 = ADAM_B2 * v + (1.0 - ADAM_B2) * (g * g)
    m_hat = m2 / (1.0 - ADAM_B1 ** ADAM_STEP)
    v_hat = v2 / (1.0 - ADAM_B2 ** ADAM_STEP)
    delta = -ADAM_LR * (m_hat / (jnp.sqrt(v_hat) + ADAM_EPS) + ADAM_WD * w)
    return delta, m2, v2


def _adam_call(w, g, m, v, name):
    rows, cols = w.shape
    tm = rows
    for cand in (256, 352, 176, 128, 64, 48, 16, 8):
        if rows % cand == 0:
            tm = cand
            break
    return _tiles(_adam, name=name, rows=rows, tm=tm,
                  row_ins=[(w, cols, 0), (g, cols, 0), (m, cols, 0), (v, cols, 0)],
                  row_outs=[(cols, F32)] * 3)


def _adam_big(w, g_mine, g_other, m, v, place, name):
    rows, cols = w.shape
    tc = 256
    nt = cols // 2 // tc

    def body(place_ref, w_ref, gm_ref, go_ref, m_ref, v_ref, g_out, d_out, m_out, v_out):
        g = jnp.where(pl.program_id(0) == place_ref[0], gm_ref[...], go_ref[...])
        d, m2, v2 = _adam(None, w_ref[...], g, m_ref[...], v_ref[...])
        g_out[...] = g
        d_out[...] = d
        m_out[...] = m2
        v_out[...] = v2

    full = pl.BlockSpec((rows, tc), lambda hh, i, p: (0, hh * nt + i))
    half = pl.BlockSpec((rows, tc), lambda hh, i, p: (0, i))
    return pl.pallas_call(
        body, name=name, out_shape=[jax.ShapeDtypeStruct(w.shape, F32)] * 4,
        grid_spec=pltpu.PrefetchScalarGridSpec(
            num_scalar_prefetch=1, grid=(2, nt),
            in_specs=[full, half, half, full, full], out_specs=[full] * 4),
        compiler_params=_params(("arbitrary", "arbitrary")),
    )(place, w, g_mine, g_other, m, v)


def _adam_untiled_rows(w, g_mine, g_other, m, v, place, name):
    rows, _, cols = w.shape
    tc = 256
    nt = cols // 2 // tc
    rb = next(r for r in (206, 128, 103, rows) if rows % r == 0)

    def body(place_ref, w_ref, gm_ref, go_ref, m_ref, v_ref, g_out, d_out, m_out, v_out):
        g = jnp.where(pl.program_id(0) == place_ref[0], gm_ref[...], go_ref[...])
        d, m2, v2 = _adam(None, w_ref[...], g, m_ref[...], v_ref[...])
        g_out[...] = g
        d_out[...] = d
        m_out[...] = m2
        v_out[...] = v2

    full = pl.BlockSpec((rb, 1, tc), lambda hh, i, r, p: (r, 0, hh * nt + i))
    half = pl.BlockSpec((rb, 1, tc), lambda hh, i, r, p: (r, 0, i))
    return pl.pallas_call(
        body, name=name, out_shape=[jax.ShapeDtypeStruct(w.shape, F32)] * 4,
        grid_spec=pltpu.PrefetchScalarGridSpec(
            num_scalar_prefetch=1, grid=(2, nt, rows // rb),
            in_specs=[full, half, half, full, full], out_specs=[full] * 4),
        compiler_params=_params(("arbitrary", "arbitrary", "arbitrary")),
    )(place, w, g_mine, g_other, m, v)


def _pack(arrays, zero=None):
    flat = []
    for a in arrays:
        a = a.reshape(-1).astype(F32)
        if zero is not None:
            a = a + zero
        flat.append(jnp.pad(a, (0, (-a.size) % LANES)))
    out = jnp.concatenate(flat)
    out = jnp.pad(out, (0, (-out.size) % (8 * LANES)))
    return out.reshape(-1, LANES)


def _unpack(packed, shapes):
    flat = packed.reshape(-1)
    out, off = [], 0
    for s in shapes:
        size = int(np.prod(s))
        out.append(flat[off:off + size].reshape(s))
        off += size + (-size) % LANES
    return out


def kernel(x, norm1_w, w_in, gdn_conv_w, gdn_A_log, gdn_dt_bias, gdn_out_norm_w, fox_f_bias, fox_q_norm_w, fox_k_norm_w, w_out, norm2_w, w_ffn_gate, w_ffn_up, w_ffn_down, final_norm_w, loss_target, m_norm1_w, m_w_in, m_gdn_conv_w, m_gdn_A_log, m_gdn_dt_bias, m_gdn_out_norm_w, m_fox_f_bias, m_fox_q_norm_w, m_fox_k_norm_w, m_w_out, m_norm2_w, m_w_ffn_gate, m_w_ffn_up, m_w_ffn_down, m_final_norm_w, v_norm1_w, v_w_in, v_gdn_conv_w, v_gdn_A_log, v_gdn_dt_bias, v_gdn_out_norm_w, v_fox_f_bias, v_fox_q_norm_w, v_fox_k_norm_w, v_w_out, v_norm2_w, v_w_ffn_gate, v_w_ffn_up, v_w_ffn_down, v_final_norm_w):
    cx, cy, cc = lax.axis_index("x"), lax.axis_index("y"), lax.axis_index("c")
    own = 2 * cx + cy
    place = jnp.stack([cc, own]).astype(jnp.int32)

    names = ["w_in", "w_out", "w_gate", "w_up", "w_down"]
    is_t = [True, False, True, True, False]
    to_t = lambda a, t: a[0].T if t else a[0]
    from_t = lambda a, t: (a.T if t else a)[None]
    big_w = [to_t(a, t) for a, t in zip([w_in, w_out, w_ffn_gate, w_ffn_up, w_ffn_down], is_t)]
    big_m = [to_t(a, t) for a, t in zip([m_w_in, m_w_out, m_w_ffn_gate, m_w_ffn_up, m_w_ffn_down], is_t)]
    big_v = [to_t(a, t) for a, t in zip([v_w_in, v_w_out, v_w_ffn_gate, v_w_ffn_up, v_w_ffn_down], is_t)]
    shards = [big_w[0].astype(BF16)]
    small_w = [norm1_w, gdn_conv_w, gdn_A_log, gdn_dt_bias, gdn_out_norm_w, fox_f_bias, fox_q_norm_w,
               fox_k_norm_w, norm2_w, final_norm_w]
    small_m = [m_norm1_w, m_gdn_conv_w, m_gdn_A_log, m_gdn_dt_bias, m_gdn_out_norm_w, m_fox_f_bias,
               m_fox_q_norm_w, m_fox_k_norm_w, m_norm2_w, m_final_norm_w]
    small_v = [v_norm1_w, v_gdn_conv_w, v_gdn_A_log, v_gdn_dt_bias, v_gdn_out_norm_w, v_fox_f_bias,
               v_fox_q_norm_w, v_fox_k_norm_w, v_norm2_w, v_final_norm_w]
    first = _split_start("gather_in_start", _in_proj_plan, [shards[0], gdn_conv_w[0]],
                         [jax.ShapeDtypeStruct((N_CHIPS,) + shards[0].shape, BF16),
                          jax.ShapeDtypeStruct((N_CHIPS, CONV_K, 3 * WIDTH // N_CHIPS), F32)],
                         n_copies=8, after=shards[0])
    small_packed = [_pack(p, first["token"][0, 0]) for p in (small_w, small_m, small_v)]
    shards += [(w + first["token"][0, 0]).astype(BF16) for w in big_w[1:]]
    rest = {}

    def first_weights(after):
        w_in_g, conv_g = _split_wait("gather_in_wait", _in_proj_plan, first, [after] + small_packed)
        w_in_g = _forward_halves(w_in_g)
        rest.update(_split_start("gather_rest_start", _gather_plan, shards[1:],
                                 [jax.ShapeDtypeStruct((N_CHIPS,) + s.shape, BF16) for s in shards[1:]],
                                 n_copies=4 * len(shards[1:]), after=w_in_g))
        w_cat = _cat_weights(w_in_g.reshape(D_IN, D_MODEL))
        return w_cat + rest["token"][0, 0].astype(BF16), conv_g.transpose(1, 0, 2).reshape(CONV_K, 3 * WIDTH)

    def late_weights(after):
        w_out_g, w_gate_g, w_up_g, w_down_g = _split_wait("gather_rest_wait", _gather_plan, rest, after)
        return w_out_g.reshape(D_MODEL, D_MODEL), w_gate_g, w_up_g, w_down_g

    def start_reduction(stacks, nms, tag):
        landed = _swap_halves(stacks, "rs_swap_" + tag)
        added = [_add_half(s, l, place, "rs_add_" + nm) for s, l, nm in zip(stacks, landed, nms)]
        parts = [a[0] for a in added]
        started = _split_start("exchange_" + tag + "_start", _exchange_plan, parts,
                               [jax.ShapeDtypeStruct((3,) + p.shape[1:], p.dtype) for p in parts],
                               n_copies=3 * len(parts), after=parts[0])
        return dict(own=[a[1] for a in added], started=started, tag=tag, names=nms)

    def finish_reduction(red, after, updates):
        landed = _split_wait("exchange_" + red["tag"] + "_wait", _exchange_plan, red["started"], after)
        halves = [_sum_partials(o, p, "rs_sum_" + nm, untiled_rows=nm == "w_in")
                  for o, p, nm in zip(red["own"], landed, red["names"])]
        others = _share_halves(halves, "rs_share_" + red["tag"])
        return [upd(gm, go) for upd, gm, go in zip(updates, halves, others)]

    def transport_update(b):
        def upd(gm, go):
            res = _adam_big(big_w[b], gm, go, big_m[b], big_v[b], place, "adam_" + names[b])
            early_done.append(res[1])
            return [from_t(a, is_t[b]) for a in res]
        return upd

    early_done = []

    def w_in_update(gm, go):
        rows3 = lambda a: jnp.transpose(a, (2, 0, 1))
        res = _adam_untiled_rows(rows3(w_in), gm, go, rows3(m_w_in), rows3(v_w_in), place, "adam_w_in")
        return [jnp.transpose(a, (1, 2, 0)) for a in res]

    early = {}

    def early_grads_ready(g_out, g_gate, g_up, g_down):
        stacks = [g_out.reshape(N_CHIPS, D_MODEL // N_CHIPS, D_MODEL), g_gate, g_up, g_down]
        early.update(start_reduction(stacks, names[1:], "early"))
        return early["started"]["token"][0, 0]

    grad_x, g_cat, _, _, _, _, small = _local_step(
        x[0], loss_target[0], norm1_w + first["token"][0, 0], gdn_A_log[0], gdn_dt_bias[0],
        gdn_out_norm_w[0], fox_f_bias[0], fox_q_norm_w[0], fox_k_norm_w[0], norm2_w, final_norm_w.reshape(1, -1),
        first_weights, late_weights, early_grads_ready)

    late = start_reduction([_uncat_grad(g_cat).reshape(N_CHIPS, D_IN // N_CHIPS, D_MODEL)], names[:1], "w_in")
    big_upd = finish_reduction(early, late["started"]["token"], [transport_update(b) for b in range(1, 5)])

    order = ["norm1_w", "conv_w", "a_log", "dt_bias", "out_norm_w", "f_bias", "q_norm_w", "k_norm_w",
             "norm2_w", "final_w"]
    red = _allreduce_small(_pack([small[k] for k in order] + [small["loss"]]))
    red_shapes = [(1, D_MODEL), (CONV_K, 3 * WIDTH), (1, HEADS), (1, HEADS), (1, HEAD_DIM), (1, HEADS),
                  (1, HEAD_DIM), (1, HEAD_DIM), (1, D_MODEL), (D_MODEL,), ()]
    red_list = _unpack(red, red_shapes)
    loss = red_list[-1]
    small_g = dict(zip(order, red_list[:-1]))
    shard_cols = 3 * WIDTH // N_CHIPS
    small_g["conv_w"] = lax.dynamic_slice_in_dim(small_g["conv_w"], own * shard_cols, shard_cols, axis=1)[None]
    small_gl = [small_g[k].reshape(w.shape) for k, w in zip(order, small_w)]
    s_delta, s_m, s_v = _adam_call(small_packed[0], _pack(small_gl), small_packed[1], small_packed[2], "adam_small")
    big_upd = finish_reduction(late, [s_delta] + early_done, [w_in_update]) + big_upd
    shapes = [w.shape for w in small_w]
    s_delta, s_m, s_v = _unpack(s_delta, shapes), _unpack(s_m, shapes), _unpack(s_v, shapes)

    big_pos = {1: 0, 9: 1, 11: 2, 12: 3, 13: 4}
    small_pos = {0: 0, 2: 1, 3: 2, 4: 3, 5: 4, 6: 5, 7: 6, 8: 7, 10: 8, 14: 9}
    grads, deltas, new_m, new_v = [], [], [], []
    for pos in range(15):
        if pos in big_pos:
            b = big_pos[pos]
            g, d, m2, v2 = big_upd[b]
            grads.append(g)
            deltas.append(d)
            new_m.append(m2)
            new_v.append(v2)
        else:
            s = small_pos[pos]
            grads.append(small_gl[s])
            deltas.append(s_delta[s])
            new_m.append(s_m[s])
            new_v.append(s_v[s])
    return (loss, grad_x[None], *grads, *deltas, *new_m, *new_v)
```

```python
import jax
import jax.numpy as jnp
import numpy as np
from jax import lax
from jax.experimental import pallas as pl
from jax.experimental.pallas import tpu as pltpu

F32 = jnp.float32
BF16 = jnp.bfloat16

D_MODEL = 1024
HEADS = 8
HEAD_DIM = 64
PAIRS = HEADS // 2
WIDTH = HEADS * HEAD_DIM
CHUNK = 64
CONV_K = 4
D_FF = 2816
FF_SHARD = D_FF // 4
EPS = 1e-6
SCALE = HEAD_DIM ** -0.5
LANES = 128
N_CHIPS = 4
D_IN = 4120
D_CAT = 4224
COL_SMALL = 4096 // LANES

ADAM_LR = 0.001
ADAM_B1 = 0.9
ADAM_B2 = 0.999
ADAM_EPS = 1e-08
ADAM_WD = 0.01
ADAM_STEP = 10

VMEM_LIMIT = 56 * 1024 * 1024
MESH = pl.DeviceIdType.MESH
HIGHEST = lax.Precision.HIGHEST


def _params(sem):
    return pltpu.CompilerParams(dimension_semantics=sem, vmem_limit_bytes=VMEM_LIMIT)


_CONTRACT = {"nn": ((1,), (0,)), "nt": ((1,), (1,)), "tn": ((0,), (0,))}


def _mm(a, b, *, dims, name, out_dtype=F32, add=None, tm=1024, tn=512, tk=512):
    if dims == "nn":
        (m, k), (k2, n) = a.shape, b.shape
    elif dims == "nt":
        (m, k), (n, k2) = a.shape, b.shape
    else:
        (k, m), (k2, n) = a.shape, b.shape
    assert k == k2, (a.shape, b.shape, dims)
    tm, tn, tk = min(tm, m), min(tn, n), min(tk, k)
    assert m % tm == 0 and n % tn == 0 and k % tk == 0, (m, n, k, tm, tn, tk)
    nk = k // tk
    a_spec = (pl.BlockSpec((tk, tm), lambda i, j, kk: (kk, i)) if dims == "tn"
              else pl.BlockSpec((tm, tk), lambda i, j, kk: (i, kk)))
    b_spec = (pl.BlockSpec((tn, tk), lambda i, j, kk: (j, kk)) if dims == "nt"
              else pl.BlockSpec((tk, tn), lambda i, j, kk: (kk, j)))
    o_spec = pl.BlockSpec((tm, tn), lambda i, j, kk: (i, j))
    contract = (_CONTRACT[dims], ((), ()))
    has_add = add is not None

    def body(*refs):
        a_ref, b_ref = refs[:2]
        add_ref = refs[2] if has_add else None
        o_ref = refs[3] if has_add else refs[2]
        part = lax.dot_general(a_ref[...].astype(BF16), b_ref[...].astype(BF16), contract,
                               preferred_element_type=F32)

        def finish(r):
            if has_add:
                r = r + add_ref[...].astype(F32)
            o_ref[...] = r.astype(out_dtype)

        if nk == 1:
            finish(part)
            return
        acc = refs[-1]
        kk = pl.program_id(2)

        @pl.when(kk == 0)
        def _():
            acc[...] = part

        @pl.when(kk > 0)
        def _():
            acc[...] += part

        @pl.when(kk == nk - 1)
        def _():
            finish(acc[...])

    ins = [a, b] + ([add] if has_add else [])
    in_specs = [a_spec, b_spec] + ([o_spec] if has_add else [])
    return pl.pallas_call(
        body, name=name, grid=(m // tm, n // tn, nk),
        in_specs=in_specs, out_specs=o_spec,
        out_shape=jax.ShapeDtypeStruct((m, n), out_dtype),
        scratch_shapes=[pltpu.VMEM((tm, tn), F32)] if nk > 1 else [],
        compiler_params=_params(("parallel", "parallel", "arbitrary")),
    )(*ins)


def _mm_blocks(a, b, *, name, grid, a_spec, b_spec, o_spec, out_shape, dims, n_sum=0, add=None, add_spec=None,
               epilogue=None, extra=(), n_acc=0):
    contract = (_CONTRACT[dims], ((), ()))
    has_add = add is not None
    n_in = 2 + has_add + len(extra)

    def body(*refs):
        a_ref, b_ref = refs[:2]
        dot = lambda x, y: lax.dot_general(x.astype(BF16), y.astype(BF16), contract, preferred_element_type=F32)
        if n_sum:
            r = dot(a_ref[0], b_ref[0])
            for s in range(1, n_sum):
                r = r + dot(a_ref[s], b_ref[s])
        else:
            r = dot(a_ref[...], b_ref[...])
        if has_add:
            r = r + refs[2][...].astype(F32)
        if epilogue is None:
            refs[-1][...] = r.astype(refs[-1].dtype)
        else:
            outs = epilogue(r, *[e[...] for e in refs[2 + has_add:n_in]])
            out_refs = refs[n_in:]
            n_plain = len(out_refs) - n_acc
            for o_ref, val in zip(out_refs[:n_plain], outs):
                o_ref[...] = val.astype(o_ref.dtype)
            if n_acc:
                @pl.when(pl.program_id(0) == 0)
                def _():
                    for o_ref in out_refs[n_plain:]:
                        o_ref[...] = jnp.zeros_like(o_ref)
                for o_ref, val in zip(out_refs[n_plain:], outs[n_plain:]):
                    o_ref[...] += val

    ins = [a, b] + ([add] if has_add else []) + [e[0] for e in extra]
    in_specs = [a_spec, b_spec] + ([add_spec] if has_add else []) + [e[1] for e in extra]
    sem = ("arbitrary" if n_acc else "parallel",) * len(grid)
    return pl.pallas_call(
        body, name=name, grid=grid, in_specs=in_specs, out_specs=o_spec, out_shape=out_shape,
        compiler_params=_params(sem),
    )(*ins)


def _tiles(fn, *, name, rows, tm, ncol=1, row_ins=(), col_consts=(), full_consts=(),
           row_outs=(), acc_outs=()):
    nt = rows // tm
    assert rows % tm == 0
    n_full, n_col, n_row = len(full_consts), len(col_consts), len(row_ins)
    n_ro, n_acc = len(row_outs), len(acc_outs)

    def body(*refs):
        ins = refs[:n_full + n_col + n_row]
        outs = refs[n_full + n_col + n_row:]
        i = pl.program_id(1)
        res = fn(pl.program_id(0), *[r[...] for r in ins])
        for r, v in zip(outs[:n_ro], res[:n_ro]):
            r[...] = v.astype(r.dtype)
        if n_acc:
            @pl.when(i == 0)
            def _():
                for r in outs[n_ro:]:
                    r[...] = jnp.zeros_like(r)
            for r, v in zip(outs[n_ro:], res[n_ro:]):
                r[...] += v

    in_specs = [pl.BlockSpec(a.shape, lambda j, i, nd=a.ndim: (0,) * nd) for a in full_consts]
    in_specs += [pl.BlockSpec((nr, w), lambda j, i, o=o: (0, o + j)) for (_, nr, w, o) in col_consts]
    in_specs += [pl.BlockSpec((tm, w), lambda j, i, o=o: (i, o + j)) for (_, w, o) in row_ins]
    out_specs = [pl.BlockSpec((tm, w), lambda j, i: (i, j)) for (w, _) in row_outs]
    out_specs += [pl.BlockSpec((nr, w), lambda j, i: (0, j)) for (nr, w) in acc_outs]
    out_shape = [jax.ShapeDtypeStruct((rows, w * ncol), dt) for (w, dt) in row_outs]
    out_shape += [jax.ShapeDtypeStruct((nr, w * ncol), F32) for (nr, w) in acc_outs]
    args = list(full_consts) + [c[0] for c in col_consts] + [r[0] for r in row_ins]
    out = pl.pallas_call(
        body, name=name, grid=(ncol, nt), in_specs=in_specs, out_specs=out_specs, out_shape=out_shape,
        compiler_params=_params(("parallel", "arbitrary")),
    )(*args)
    return out


def _rms(x, w):
    return x * lax.rsqrt(jnp.mean(x * x, axis=-1, keepdims=True) + EPS) * w


def _lane_lo(shape):
    return lax.broadcasted_iota(jnp.int32, shape, len(shape) - 1) < HEAD_DIM


def _pair_sum(x):
    lo = _lane_lo(x.shape)
    s0 = jnp.sum(jnp.where(lo, x, 0.0), axis=-1, keepdims=True)
    s1 = jnp.sum(jnp.where(lo, 0.0, x), axis=-1, keepdims=True)
    return jnp.where(lo, s0, s1)


def _head_col(x, lo, h):
    keep = lo if h == 0 else jnp.logical_not(lo)
    return jnp.max(jnp.where(keep, x, -jnp.inf), axis=-1, keepdims=True)


def _softplus(x):
    return jnp.maximum(x, 0.0) + jnp.log1p(jnp.exp(-jnp.abs(x)))


def _silu(x):
    return x * jax.nn.sigmoid(x)


def _dot(a, b, contract):
    return lax.dot_general(a.astype(BF16), b.astype(BF16), (contract, ((), ())),
                           preferred_element_type=F32)


def _dot32(a, b, contract):
    return lax.dot_general(a, b, (contract, ((), ())), precision=HIGHEST, preferred_element_type=F32)


def _bd(y):
    yy = jnp.concatenate([y, y], axis=0)
    r = lax.broadcasted_iota(jnp.int32, yy.shape, 0) < HEAD_DIM
    c = lax.broadcasted_iota(jnp.int32, yy.shape, 1) < HEAD_DIM
    return jnp.where(r == c, yy, 0.0)


def _pp(x, y):
    return _dot(x, _bd(y), _CONTRACT["nn"])


def _pp_nt(x, y):
    return _dot(x, _bd(y), _CONTRACT["nt"])


def _pp_tn(x, y):
    full = _dot(x, y, _CONTRACT["tn"])
    return jnp.where(_lane_lo((HEAD_DIM, LANES)), full[:HEAD_DIM], full[HEAD_DIM:])


def _gdn_masks():
    row = lax.broadcasted_iota(jnp.int32, (CHUNK, LANES), 0)
    col = lax.broadcasted_iota(jnp.int32, (CHUNK, LANES), 1) % HEAD_DIM
    return row, col


def _interleave(chains):
    live = list(chains)
    while live:
        for g in list(live):
            try:
                next(g)
            except StopIteration:
                live.remove(g)


def _gdn_forward(qkv, betax, gcx, grow, rows):
    nchunk = rows // CHUNK

    def body(q_ref, k_ref, v_ref, bx_ref, gx_ref, gr_ref, o_ref, ss_ref, ts_ref, state):
        n = pl.program_id(0)

        @pl.when(n == 0)
        def _():
            state[...] = jnp.zeros_like(state)

        row, col = _gdn_masks()
        incl, strict = col <= row, col < row

        def chain(p):
            lanes = pl.ds(p * LANES, LANES)
            q, k, v, bx, gx = q_ref[:, lanes], k_ref[:, lanes], v_ref[:, lanes], bx_ref[:, lanes], gx_ref[:, lanes]
            gr = gr_ref[0, p]
            glast = gx_ref[pl.ds(CHUNK - 1, 1), lanes]
            s = state[p]
            dm = jnp.where(incl, jnp.exp(jnp.minimum(gx - gr, 0.0)), 0.0)
            kb, vb, eg, qs = k * bx, v * bx, jnp.exp(gx), q * SCALE
            yield
            big_g, big_p = _pp_nt(kb, k), _pp_nt(qs, k)
            yield
            x = -jnp.where(strict, big_g * dm, 0.0)
            att = jnp.where(incl, big_p * dm, 0.0)
            tm = jnp.where(row == col, 1.0, 0.0) + x
            x = _pp(x, x)
            yield
            for _ in range(4):
                step, x = _pp(tm, x), _pp(x, x)
                yield
                tm = tm + step
            tm = tm + _pp(tm, x)
            yield
            u, w = _pp(tm, vb), _pp(tm, kb * eg)
            yield
            ws, qgs = _pp(w, s), _pp(qs * eg, s)
            yield
            vn = u - ws
            kd = k * jnp.exp(glast - gx)
            avn, upd = _pp(att, vn), _pp_tn(kd, vn)
            yield
            ss_ref[0, p] = s
            ts_ref[0, p] = tm
            o_ref[:, lanes] = qgs + avn
            state[p] = s * jnp.exp(glast) + upd

        _interleave([chain(p) for p in range(PAIRS)])

    blk = lambda j: pl.BlockSpec((CHUNK, WIDTH), lambda n, j=j: (n, j))
    sv = pl.BlockSpec((1, PAIRS, CHUNK, LANES), lambda n: (n, 0, 0, 0))
    return pl.pallas_call(
        body, name="gdn_fwd", grid=(nchunk,),
        in_specs=[blk(0), blk(1), blk(2), blk(0), blk(0),
                  pl.BlockSpec((1, PAIRS, 1, LANES), lambda n: (n, 0, 0, 0))],
        out_specs=[blk(0), sv, sv],
        out_shape=[jax.ShapeDtypeStruct((rows, WIDTH), F32),
                   jax.ShapeDtypeStruct((nchunk, PAIRS, CHUNK, LANES), F32),
                   jax.ShapeDtypeStruct((nchunk, PAIRS, CHUNK, LANES), F32)],
        scratch_shapes=[pltpu.VMEM((PAIRS, CHUNK, LANES), F32)],
        compiler_params=_params(("arbitrary",)),
    )(qkv, qkv, qkv, betax, gcx, grow)


def _gdn_backward(qkv, betax, gcx, grow, ssave, tsave, do, rows):
    nchunk = rows // CHUNK

    def body(q_ref, k_ref, v_ref, bx_ref, gx_ref, gr_ref, ss_ref, ts_ref, do_ref,
             dq_ref, dk_ref, dv_ref, dbx_ref, dgx_ref, dgr_ref, dstate):
        n = pl.program_id(0)

        @pl.when(n == 0)
        def _():
            dstate[...] = jnp.zeros_like(dstate)

        row, col = _gdn_masks()
        incl, strict = col <= row, col < row

        def chain(p):
            lanes = pl.ds(p * LANES, LANES)
            q, k, v, bx, gx = q_ref[:, lanes], k_ref[:, lanes], v_ref[:, lanes], bx_ref[:, lanes], gx_ref[:, lanes]
            gr = gr_ref[0, p]
            glast = gx_ref[pl.ds(CHUNK - 1, 1), lanes]
            s, tm, d_o = ss_ref[0, p], ts_ref[0, p], do_ref[:, lanes]
            ds_out = dstate[p]
            dm = jnp.where(incl, jnp.exp(jnp.minimum(gx - gr, 0.0)), 0.0)
            kb, vb, eg, qs = k * bx, v * bx, jnp.exp(gx), q * SCALE
            kbg, qg = kb * eg, qs * eg
            ed = jnp.exp(glast - gx)
            kd = k * ed
            eglast = jnp.exp(glast)
            yield
            big_g, big_p = _pp_nt(kb, k), _pp_nt(qs, k)
            u, w = _pp(tm, vb), _pp(tm, kbg)
            dqg, kds = _pp_nt(d_o, s), _pp(kd, ds_out)
            yield
            low = jnp.where(strict, big_g * dm, 0.0)
            att = jnp.where(incl, big_p * dm, 0.0)
            ws, atd = _pp(w, s), _pp_tn(att, d_o)
            yield
            vn = u - ws
            dvn = kds + atd
            dkd, datt_raw = _pp_nt(vn, ds_out), _pp_nt(d_o, vn)
            dw_neg, dvb = _pp_nt(dvn, s), _pp_tn(tm, dvn)
            dtm_a, wdv = _pp_nt(dvn, vb), _pp_tn(w, dvn)
            qgd = _pp_tn(qg, d_o)
            yield
            datt = jnp.where(incl, datt_raw, 0.0)
            dw = -dw_neg
            dtm_b, dkbg = _pp_nt(dw, kbg), _pp_tn(tm, dw)
            dbig_p = datt * dm
            dqs_a, dk_p = _pp(dbig_p, k), _pp_tn(dbig_p, qs)
            yield
            inner = _pp_tn(tm, dtm_a + dtm_b)
            yield
            dlow = jnp.where(strict, -_pp_nt(inner, tm), 0.0)
            yield
            dbig_g = dlow * dm
            dkb_a, dk_g = _pp(dbig_g, k), _pp_tn(dbig_g, kb)
            yield
            dkb = dkb_a + dkbg * eg
            dqs = dqs_a + dqg * eg
            dk = dk_g + dk_p + dkd * ed + dkb * bx
            z = dlow * low + datt * att
            kdterm = dkd * kd
            dglast = (jnp.sum(ds_out * s, axis=0, keepdims=True) * eglast
                      + jnp.sum(kdterm, axis=0, keepdims=True))
            dgx = dqg * qg + dkbg * kbg - kdterm
            dgx = dgx + jnp.where(col == 0, _pair_sum(z), 0.0)
            dgx = dgx + jnp.where(row == CHUNK - 1, dglast, 0.0)
            dq_ref[:, lanes] = dqs * SCALE
            dk_ref[:, lanes] = dk
            dv_ref[:, lanes] = dvb * bx
            dbx_ref[:, lanes] = dkb * k + dvb * v
            dgx_ref[:, lanes] = dgx
            dgr_ref[0, p] = -jnp.sum(z, axis=0, keepdims=True)
            dstate[p] = ds_out * eglast + qgd - wdv

        _interleave([chain(p) for p in range(PAIRS)])

    last = nchunk - 1
    blk = lambda j: pl.BlockSpec((CHUNK, WIDTH), lambda n, j=j: (last - n, j))
    sv = pl.BlockSpec((1, PAIRS, CHUNK, LANES), lambda n: (last - n, 0, 0, 0))
    gr_spec = pl.BlockSpec((1, PAIRS, 1, LANES), lambda n: (last - n, 0, 0, 0))
    wide = jax.ShapeDtypeStruct((rows, WIDTH), F32)
    return pl.pallas_call(
        body, name="gdn_bwd", grid=(nchunk,),
        in_specs=[blk(0), blk(1), blk(2), blk(0), blk(0), gr_spec, sv, sv, blk(0)],
        out_specs=[blk(0)] * 5 + [gr_spec],
        out_shape=[wide] * 5 + [jax.ShapeDtypeStruct((nchunk, PAIRS, 1, LANES), F32)],
        scratch_shapes=[pltpu.VMEM((PAIRS, CHUNK, LANES), F32)],
        compiler_params=_params(("arbitrary",)),
    )(qkv, qkv, qkv, betax, gcx, grow, ssave, tsave, do)


ATT_TQ = 256


def _att_scores(qh, kt, fk, diag):
    s = _dot(qh, kt, _CONTRACT["nt"]) - fk
    if diag:
        r = lax.broadcasted_iota(jnp.int32, s.shape, 0)
        c = lax.broadcasted_iota(jnp.int32, s.shape, 1)
        s = jnp.where(r >= c, s, -jnp.inf)
    return s


def _head_masks(n):
    lo = _lane_lo((n, LANES))
    return [lo, jnp.logical_not(lo)]


def _attention_forward(fqk, proj, frow, rows):
    tq = tk = min(ATT_TQ, rows)
    nq = rows // tq
    v_off = 3072 // LANES

    def body(q_ref, k_ref, v_ref, fr_ref, o_ref, lse_ref):
        qi = pl.program_id(1)
        q = q_ref[...] * SCALE
        keep_q, keep_k = _head_masks(tq), _head_masks(tk)
        qh = [jnp.where(keep_q[h], q, 0.0).astype(BF16) for h in range(2)]

        def tile(ki, carry, diag):
            k0 = pl.multiple_of(ki * tk, tk)
            kt = k_ref[pl.ds(k0, tk), :].astype(BF16)
            v_t = v_ref[pl.ds(k0, tk), :]
            out = [None, None]

            def chain(h):
                m, l, acc = carry[h]
                vt = jnp.where(keep_k[h], v_t, 0.0).astype(BF16)
                yield
                s = _att_scores(qh[h], kt, fr_ref[0, pl.ds(h, 1), pl.ds(k0, tk)], diag)
                yield
                m_new = jnp.maximum(m, jnp.max(s, axis=-1, keepdims=True))
                p = jnp.exp(s - m_new)
                alpha = jnp.exp(m - m_new)
                l = alpha * l + jnp.sum(p, axis=-1, keepdims=True)
                p_hi = p.astype(BF16)
                p_lo = p - p_hi.astype(F32)
                yield
                out[h] = (m_new, l, alpha * acc + _dot(p_hi, vt, _CONTRACT["nn"]) + _dot(p_lo, vt, _CONTRACT["nn"]))

            _interleave([chain(0), chain(1)])
            return tuple(out)

        one = (jnp.full((tq, 1), -jnp.inf, F32), jnp.zeros((tq, 1), F32), jnp.zeros((tq, LANES), F32))
        carry = lax.fori_loop(0, qi, lambda ki, c: tile(ki, c, False), (one, one))
        (m0, l0, acc0), (m1, l1, acc1) = tile(qi, carry, True)
        o_ref[...] = acc0 / l0 + acc1 / l1
        lse_ref[...] = jnp.where(keep_q[0], m0 + jnp.log(l0), m1 + jnp.log(l1))

    whole = lambda off: pl.BlockSpec((rows, LANES), lambda p, i, off=off: (0, off + p))
    qblk = lambda off: pl.BlockSpec((tq, LANES), lambda p, i, off=off: (i, off + p))
    wide = jax.ShapeDtypeStruct((rows, WIDTH), F32)
    return pl.pallas_call(
        body, name="fox_fwd", grid=(PAIRS, nq),
        in_specs=[qblk(0), whole(PAIRS), whole(v_off), pl.BlockSpec((1, 2, rows), lambda p, i: (p, 0, 0))],
        out_specs=[qblk(0), qblk(0)], out_shape=[wide, wide],
        compiler_params=_params(("parallel", "arbitrary")),
    )(fqk, fqk, proj, frow)


def _attention_backward(fqk, proj, frow, ao, lse, dao, rows):
    tq = tk = min(ATT_TQ, rows)
    nq = rows // tq
    v_off = 3072 // LANES

    def body(q_ref, k_ref, v_ref, fr_ref, o_ref, lse_ref, do_ref, dq_ref, dk_ref, dv_ref, dfr_ref):
        ki = pl.program_id(1)

        @pl.when(ki == 0)
        def _():
            dq_ref[...] = jnp.zeros_like(dq_ref)

        keep_q, keep_k = _head_masks(tq), _head_masks(tk)
        k_t = k_ref[...]
        kt = k_t.astype(BF16)
        vt = v_ref[...].astype(BF16)
        kh = [jnp.where(keep_k[h], k_t, 0.0).astype(BF16) for h in range(2)]
        fk = [fr_ref[0, pl.ds(h, 1), :] for h in range(2)]

        def tile(qi, carry, diag):
            dk, dv, df0, df1 = carry
            rows_q = pl.ds(pl.multiple_of(qi * tq, tq), tq)
            q, d_o, lse_t = q_ref[rows_q, :] * SCALE, do_ref[rows_q, :], lse_ref[rows_q, :]
            delta_x = _pair_sum(d_o.astype(BF16).astype(F32) * o_ref[rows_q, :])
            res = [None, None]

            def chain(h):
                qh = jnp.where(keep_q[h], q, 0.0).astype(BF16)
                doh = jnp.where(keep_q[h], d_o, 0.0).astype(BF16)
                lse_h, delta_h = _head_col(lse_t, keep_q[0], h), _head_col(delta_x, keep_q[0], h)
                yield
                s, dp = _att_scores(qh, kt, fk[h], diag), _dot(doh, vt, _CONTRACT["nt"])
                yield
                p = jnp.exp(s - lse_h)
                ds = p * (dp - delta_h)
                yield
                res[h] = (_dot(p, doh, _CONTRACT["tn"]), _dot(ds, qh, _CONTRACT["tn"]),
                          _dot(ds, kh[h], _CONTRACT["nn"]), jnp.sum(ds, axis=0, keepdims=True))

            _interleave([chain(0), chain(1)])
            (dv0, dk0, dq0, s0), (dv1, dk1, dq1, s1) = res
            dq_ref[rows_q, :] += (dq0 + dq1) * SCALE
            return dk + dk0 + dk1, dv + dv0 + dv1, df0 - s0, df1 - s1

        zero_kv = jnp.zeros((tk, LANES), F32)
        zero_f = jnp.zeros((1, tk), F32)
        carry = tile(ki, (zero_kv, zero_kv, zero_f, zero_f), True)
        dk, dv, df0, df1 = lax.fori_loop(ki + 1, nq, lambda qi, c: tile(qi, c, False), carry)
        dk_ref[...] = dk
        dv_ref[...] = dv.astype(dv_ref.dtype)
        dfr_ref[0, pl.ds(0, 1), :] = df0
        dfr_ref[0, pl.ds(1, 1), :] = df1

    whole = lambda off: pl.BlockSpec((rows, LANES), lambda p, i, off=off: (0, off + p))
    kblk = lambda off: pl.BlockSpec((tk, LANES), lambda p, i, off=off: (i, off + p))
    fr_spec = pl.BlockSpec((1, 2, tk), lambda p, i: (p, 0, i))
    wide = jax.ShapeDtypeStruct((rows, WIDTH), F32)
    return pl.pallas_call(
        body, name="fox_bwd", grid=(PAIRS, nq),
        in_specs=[whole(0), kblk(PAIRS), kblk(v_off), fr_spec, whole(0), whole(0), whole(0)],
        out_specs=[whole(0), kblk(0), kblk(0), fr_spec],
        out_shape=[wide, wide, jax.ShapeDtypeStruct((rows, WIDTH), BF16),
                   jax.ShapeDtypeStruct((PAIRS, 2, rows), F32)],
        compiler_params=_params(("parallel", "arbitrary")),
    )(fqk, fqk, proj, frow, ao, lse, dao)


def _lane_ids(shape):
    return lax.broadcasted_iota(jnp.int32, shape, len(shape) - 1)


def _gates_elem(a_log, dt_bias, f_bias, pre):
    lane = _lane_ids(pre.shape)
    beta = jax.nn.sigmoid(pre)
    g = -jnp.exp(a_log) * _softplus(pre + dt_bias)
    lf = -_softplus(-(pre + f_bias))
    return jnp.where(lane < 8, beta, jnp.where(lane < 16, g, jnp.where(lane < 24, lf, 0.0)))


def _tri_consts():
    r = np.arange(LANES)[:, None]
    c = np.arange(LANES)[None, :]
    full = (c <= r).astype(np.float32)
    chunked = full * ((r // CHUNK) == (c // CHUNK))
    return jnp.asarray(chunked), jnp.asarray(full)


def _cums_fwd(lc, lf, gates):
    rows = gates.shape[0]
    lane = _lane_ids((LANES, LANES))
    carry = jnp.zeros((1, LANES), F32)
    out = []
    for r in range(rows // LANES):
        blk = gates[r * LANES:(r + 1) * LANES]
        gc = _dot32(lc, blk, _CONTRACT["nn"])
        f = _dot32(lf, blk, _CONTRACT["nn"]) + carry
        carry = carry + jnp.sum(blk, axis=0, keepdims=True)
        out.append(jnp.where((lane >= 8) & (lane < 16), gc, jnp.where((lane >= 16) & (lane < 24), f, 0.0)))
    return jnp.concatenate(out, axis=0)


def _cums_bwd(lc, lf, dcums):
    rows = dcums.shape[0]
    lane = _lane_ids((LANES, LANES))
    is_g = (lane >= 8) & (lane < 16)
    is_f = (lane >= 16) & (lane < 24)
    carry = jnp.zeros((1, LANES), F32)
    out = [None] * (rows // LANES)
    for r in reversed(range(rows // LANES)):
        blk = dcums[r * LANES:(r + 1) * LANES]
        dg = jnp.where(is_g, blk, 0.0)
        df = jnp.where(is_f, blk, 0.0)
        out[r] = _dot32(lc, dg, _CONTRACT["tn"]) + _dot32(lf, df, _CONTRACT["tn"]) + carry
        carry = carry + jnp.sum(df, axis=0, keepdims=True)
    return jnp.concatenate(out, axis=0)


def _expand_consts():
    xb = np.zeros((LANES, WIDTH), np.float32)
    xg = np.zeros((LANES, WIDTH), np.float32)
    for h in range(HEADS):
        xb[h, h * HEAD_DIM:(h + 1) * HEAD_DIM] = 1.0
        xg[8 + h, h * HEAD_DIM:(h + 1) * HEAD_DIM] = 1.0
    return jnp.asarray(xb), jnp.asarray(xg)


def _shift_down(x, s):
    if s == 0:
        return x
    row = lax.broadcasted_iota(jnp.int32, x.shape, 0)
    return jnp.where(row >= s, pltpu.roll(x, s, 0), 0.0)


def _shift_up(x, s):
    if s == 0:
        return x
    n = x.shape[0]
    row = lax.broadcasted_iota(jnp.int32, x.shape, 0)
    return jnp.where(row < n - s, pltpu.roll(x, n - s, 0), 0.0)


def _row_of(cw, i):
    row = lax.broadcasted_iota(jnp.int32, cw.shape, 0)
    return jnp.sum(jnp.where(row == i, cw, 0.0), axis=0, keepdims=True)


def _conv(cw, x):
    c = jnp.zeros_like(x)
    for i in range(CONV_K):
        c = c + _row_of(cw, i) * _shift_down(x, CONV_K - 1 - i)
    return c


def _post_conv(is_qk, c):
    s = _silu(c)
    n = s * lax.rsqrt(_pair_sum(s * s) + EPS)
    return jnp.where(is_qk, n, s)


def _gdn_prep_fwd(col, cw, x):
    return (_post_conv(col < 2 * PAIRS, _conv(cw, x)),)


def _gdn_prep_bwd(is_qk, cw, x, dy):
    c = _conv(cw, x)
    _, vjp = jax.vjp(lambda cc: _post_conv(is_qk, cc), c)
    (dc,) = vjp(dy)
    dx = jnp.zeros_like(x)
    row = lax.broadcasted_iota(jnp.int32, cw.shape, 0)
    dcw = jnp.zeros(cw.shape, F32)
    for i in range(CONV_K):
        s = CONV_K - 1 - i
        dx = dx + _row_of(cw, i) * _shift_up(dc, s)
        dcw = dcw + jnp.where(row == i, jnp.sum(dc * _shift_down(x, s), axis=0, keepdims=True), 0.0)
    return dx, dcw


def _head_rms(w, x):
    return x * lax.rsqrt(_pair_sum(x * x) / HEAD_DIM + EPS) * w


def _cat_weights(w_in_t):
    tail = jnp.pad(w_in_t[4112:4120], ((0, D_CAT - D_IN), (0, 0)))
    return jnp.concatenate([w_in_t[:2048], w_in_t[2064:4112], w_in_t[2048:2064], tail], axis=0)


def _uncat_grad(g):
    return jnp.concatenate([g[:2048], g[4096:4112], g[2048:4096], g[4112:4120]], axis=0)


def _lanes_to_rowform(v8, rows):
    return v8.reshape(rows // CHUNK, CHUNK, HEADS).transpose(0, 2, 1).reshape(rows // CHUNK, PAIRS, 1, LANES)


def _rowform_to_lanes(v, rows):
    return v.reshape(rows // CHUNK, HEADS, CHUNK).transpose(0, 2, 1).reshape(rows, HEADS)


def _local_step(x, target, norm1_w, a_log, dt_bias, out_norm_w, f_bias, q_norm_w, k_norm_w,
                norm2_w, final_w, first_weights, late_weights, early_grads_ready, early_grads_continue):
    rows = x.shape[0]
    tm = min(512, rows)
    lc, lf = _tri_consts()
    xb, xg = _expand_consts()

    (h1,) = _tiles(lambda col, w, xx: (_rms(xx, w),), name="norm1", rows=rows, tm=tm,
                   full_consts=[norm1_w], row_ins=[(x, D_MODEL, 0)], row_outs=[(D_MODEL, BF16)])
    w_cat, conv_w = first_weights(h1)
    proj = _mm(h1, w_cat, dims="nt", name="in_proj", tn=1408, tk=1024)

    lane_pad = lambda v, off: jnp.pad(v.reshape(1, -1), ((0, 0), (off, LANES - off - v.size)))
    p_a, p_dt, p_fb = lane_pad(a_log, 8), lane_pad(dt_bias, 8), lane_pad(f_bias, 16)

    def gates_fwd(col, lcv, lfv, a, dt, fb, pre):
        gates = _gates_elem(a, dt, fb, pre)
        return gates, _cums_fwd(lcv, lfv, gates)

    gates, cums = _tiles(gates_fwd, name="gates", rows=rows, tm=rows,
                         full_consts=[lc, lf, p_a, p_dt, p_fb], row_ins=[(proj, LANES, COL_SMALL)],
                         row_outs=[(LANES, F32), (LANES, F32)])

    def expand_fwd(col, b, g, gt, cm):
        return (_dot32(gt, b, _CONTRACT["nn"]), _dot32(cm, g, _CONTRACT["nn"]))

    betax, gcx = _tiles(expand_fwd, name="expand", rows=rows, tm=tm, full_consts=[xb, xg],
                        row_ins=[(gates, LANES, 0), (cums, LANES, 0)],
                        row_outs=[(WIDTH, F32)] * 2)
    grow = _lanes_to_rowform(cums[:, 8:16], rows)
    frow = cums[:, 16:24].T.reshape(PAIRS, 2, rows)

    (qkv,) = _tiles(_gdn_prep_fwd, name="gdn_prep", rows=rows, tm=rows, ncol=3 * PAIRS,
                    col_consts=[(conv_w, CONV_K, LANES, 0)], row_ins=[(proj, LANES, 0)],
                    row_outs=[(LANES, F32)])
    o_gdn, ssave, tsave = _gdn_forward(qkv, betax, gcx, grow, rows)

    w_qk = jnp.concatenate([jnp.tile(q_norm_w.reshape(1, -1), (1, HEADS)),
                            jnp.tile(k_norm_w.reshape(1, -1), (1, HEADS))], axis=1)
    fox_off = 2048 // LANES
    (fqk,) = _tiles(lambda col, w, xx: (_head_rms(w, xx),), name="fox_prep", rows=rows, tm=rows, ncol=2 * PAIRS,
                    col_consts=[(w_qk, 1, LANES, 0)], row_ins=[(proj, LANES, fox_off)],
                    row_outs=[(LANES, F32)])
    ao, lse = _attention_forward(fqk, proj, frow, rows)

    w_on = jnp.tile(out_norm_w.reshape(1, -1), (1, 2))
    z_off, fg_off = 1536 // LANES, 3584 // LANES
    mix_g_fn = lambda w, o, z: _head_rms(w, o) * _silu(z)
    mix_f_fn = lambda a, g: a * jax.nn.sigmoid(g)
    (mix_g,) = _tiles(lambda col, w, o, z: (mix_g_fn(w, o, z),), name="mix_gdn", rows=rows, tm=rows, ncol=PAIRS,
                      full_consts=[w_on], row_ins=[(o_gdn, LANES, 0), (proj, LANES, z_off)],
                      row_outs=[(LANES, BF16)])
    (mix_f,) = _tiles(lambda col, a, g: (mix_f_fn(a, g),), name="mix_fox", rows=rows, tm=rows, ncol=PAIRS,
                      row_ins=[(ao, LANES, 0), (proj, LANES, fg_off)], row_outs=[(LANES, BF16)])
    mix = jnp.concatenate([mix_g, mix_f], axis=1)
    w_out, w_gate, w_up, w_down = late_weights(mix)
    t_rows, t_half = min(1024, rows), min(512, rows)
    n_rt = rows // t_rows
    row_blk = pl.BlockSpec((t_rows, D_MODEL), lambda i, n: (i, 0))
    half_blk = pl.BlockSpec((t_half, D_MODEL), lambda i, n: (i, 0))
    vec_blk = pl.BlockSpec((1, D_MODEL), lambda i, n: (0, 0))
    wide = lambda dt: jax.ShapeDtypeStruct((rows, D_MODEL), dt)
    x1, h2 = _mm_blocks(mix, w_out, name="out_proj_norm2", grid=(n_rt, 1), dims="nn",
                        a_spec=row_blk, b_spec=pl.BlockSpec((D_MODEL, D_MODEL), lambda i, n: (0, 0)),
                        o_spec=[row_blk, row_blk], out_shape=[wide(F32), wide(BF16)], add=x, add_spec=row_blk,
                        extra=[(norm2_w, vec_blk)], epilogue=lambda r, w: (r, _rms(r, w)))
    st_act = jax.ShapeDtypeStruct((N_CHIPS, rows, FF_SHARD), BF16)
    st_rows = pl.BlockSpec((None, rows, FF_SHARD), lambda i, j: (j, i, 0))

    def ffn_in(w_st, name):
        return _mm_blocks(h2, w_st, name=name, grid=(1, N_CHIPS), dims="nt",
                          a_spec=pl.BlockSpec((rows, D_MODEL), lambda i, j: (i, 0)),
                          b_spec=pl.BlockSpec((None, FF_SHARD, D_MODEL), lambda i, j: (j, 0, 0)),
                          o_spec=st_rows, out_shape=st_act)

    gate = ffn_in(w_gate, "ffn_gate")
    act_fn = lambda g, u: _silu(g) * u
    st_tile = pl.BlockSpec((None, t_rows, FF_SHARD), lambda i, j: (j, i, 0))
    up, act = _mm_blocks(h2, w_up, name="ffn_up_act", grid=(n_rt, N_CHIPS), dims="nt",
                         a_spec=pl.BlockSpec((t_rows, D_MODEL), lambda i, j: (i, 0)),
                         b_spec=pl.BlockSpec((None, FF_SHARD, D_MODEL), lambda i, j: (j, 0, 0)),
                         o_spec=[st_tile, st_tile], out_shape=[st_act, st_act], extra=[(gate, st_tile)],
                         epilogue=lambda u, g: (u, act_fn(g.astype(F32), u)))

    def final_fn(xx, tgt, w):
        y, vjp = jax.vjp(_rms, xx, w)
        err = y - tgt
        loss = 0.5 * jnp.sum(err * err) / D_MODEL
        dx, dw = vjp(err / D_MODEL)
        return dx, dx, jnp.full((1, LANES), loss, F32), dw

    dx2, dx2_b, loss, d_final_w = _mm_blocks(
        act, w_down, name="ffn_down_loss", grid=(rows // t_half, 1), dims="nn", n_sum=N_CHIPS,
        a_spec=pl.BlockSpec((N_CHIPS, t_half, FF_SHARD), lambda i, n: (0, i, 0)),
        b_spec=pl.BlockSpec((N_CHIPS, FF_SHARD, D_MODEL), lambda i, n: (0, 0, 0)),
        o_spec=[half_blk, half_blk, pl.BlockSpec((1, LANES), lambda i, n: (0, 0)), vec_blk],
        out_shape=[wide(F32), wide(BF16), jax.ShapeDtypeStruct((1, LANES), F32),
                   jax.ShapeDtypeStruct((1, D_MODEL), F32)],
        add=x1, add_spec=half_blk, extra=[(target, half_blk), (final_w, vec_blk)], epilogue=final_fn, n_acc=2)

    def act_bwd(d, g, u):
        _, vjp = jax.vjp(act_fn, g.astype(F32), u.astype(F32))
        return vjp(d)

    dgate, dup = _mm_blocks(dx2_b, w_down, name="d_act_gate_up", grid=(n_rt, N_CHIPS), dims="nt",
                            a_spec=pl.BlockSpec((t_rows, D_MODEL), lambda i, j: (i, 0)),
                            b_spec=pl.BlockSpec((None, FF_SHARD, D_MODEL), lambda i, j: (j, 0, 0)),
                            o_spec=[st_tile, st_tile], out_shape=[st_act, st_act],
                            extra=[(gate, st_tile), (up, st_tile)], epilogue=act_bwd)

    def g_ffn(d_st, other, name):
        return _mm_blocks(d_st, other, name=name, grid=(N_CHIPS, 1), dims="tn",
                          a_spec=pl.BlockSpec((None, rows, FF_SHARD), lambda j, n: (j, 0, 0)),
                          b_spec=pl.BlockSpec((rows, D_MODEL), lambda j, n: (0, 0)),
                          o_spec=pl.BlockSpec((None, FF_SHARD, D_MODEL), lambda j, n: (j, 0, 0)),
                          out_shape=jax.ShapeDtypeStruct((N_CHIPS, FF_SHARD, D_MODEL), BF16))

    g_down = g_ffn(act, dx2_b, "g_down")

    def norm_bwd(dh, xx, dres, w):
        _, vjp = jax.vjp(_rms, xx, w)
        dx, dw = vjp(dh)
        return dx + dres, dx + dres, dw

    def d_h2(d_st, w_st, name, add, **fused):
        return _mm_blocks(d_st, w_st, name=name, grid=(rows // t_half, 1), dims="nn", n_sum=N_CHIPS,
                          a_spec=pl.BlockSpec((N_CHIPS, t_half, FF_SHARD), lambda i, n: (0, i, 0)),
                          b_spec=pl.BlockSpec((N_CHIPS, FF_SHARD, D_MODEL), lambda i, n: (0, 0, 0)),
                          add=add, add_spec=half_blk, **fused)

    dh2_gate = d_h2(dgate, w_gate, "d_h2_gate", None, o_spec=half_blk, out_shape=wide(F32))
    dx1, dx1_b, d_norm2_w = d_h2(
        dup, w_up, "d_h2_up_norm2_bwd", dh2_gate, o_spec=[half_blk, half_blk, vec_blk],
        out_shape=[wide(F32), wide(BF16), jax.ShapeDtypeStruct((1, D_MODEL), F32)],
        extra=[(x1, half_blk), (dx2, half_blk), (norm2_w, vec_blk)], epilogue=norm_bwd, n_acc=1)
    g_gate, g_up = g_ffn(dgate, h2, "g_gate"), g_ffn(dup, h2, "g_up")
    dmix = _mm(dx1_b, w_out, dims="nt", name="d_mix", tn=D_MODEL, tk=1024)
    g_out = _mm(mix, dx1_b, dims="tn", name="g_out", tn=D_MODEL, tk=rows, out_dtype=BF16)
    w_on = w_on + early_grads_ready(g_out, g_gate, g_up, g_down)

    def mix_g_bwd(col, w, o, z, d):
        _, vjp = jax.vjp(mix_g_fn, w, o, z)
        dw, do_, dz = vjp(d)
        return do_, dz, dw

    do_gdn, dz, d_on = _tiles(mix_g_bwd, name="mix_gdn_bwd", rows=rows, tm=rows, ncol=PAIRS, full_consts=[w_on],
                              row_ins=[(o_gdn, LANES, 0), (proj, LANES, z_off), (dmix, LANES, 0)],
                              row_outs=[(LANES, F32), (LANES, BF16)], acc_outs=[(1, LANES)])

    def mix_f_bwd(col, a, g, d):
        _, vjp = jax.vjp(mix_f_fn, a, g)
        return vjp(d)

    dao, dfgate = _tiles(mix_f_bwd, name="mix_fox_bwd", rows=rows, tm=rows, ncol=PAIRS,
                         row_ins=[(ao, LANES, 0), (proj, LANES, fg_off), (dmix, LANES, PAIRS)],
                         row_outs=[(LANES, F32), (LANES, BF16)])

    dfq, dfk, dfv, dfrow = _attention_backward(fqk, proj, frow + early_grads_continue(dao), ao, lse, dao, rows)

    def fox_prep_bwd(col, w, xx, d):
        _, vjp = jax.vjp(_head_rms, w, xx)
        dw, dx = vjp(d)
        return dx, dw

    dfqk, d_wqk = [], []
    for part, d_n in enumerate((dfq, dfk)):
        dx_p, dw_p = _tiles(fox_prep_bwd, name="fox_prep_bwd_" + "qk"[part], rows=rows, tm=rows, ncol=PAIRS,
                            col_consts=[(w_qk, 1, LANES, part * PAIRS)],
                            row_ins=[(proj, LANES, fox_off + part * PAIRS), (d_n, LANES, 0)],
                            row_outs=[(LANES, BF16)], acc_outs=[(1, LANES)])
        dfqk.append(dx_p)
        d_wqk.append(dw_p)

    dq, dk, dv, dbetax, dgcx, dgrow = _gdn_backward(qkv, betax, gcx, grow, ssave, tsave, do_gdn, rows)
    dqkv, d_conv = [], []
    for part, d_n in enumerate((dq, dk, dv)):
        prep_bwd = lambda col, cw, xx, dy, is_qk=(part < 2): _gdn_prep_bwd(is_qk, cw, xx, dy)
        dx_p, dw_p = _tiles(prep_bwd, name="gdn_prep_bwd_" + "qkv"[part], rows=rows, tm=rows, ncol=PAIRS,
                            col_consts=[(conv_w, CONV_K, LANES, part * PAIRS)],
                            row_ins=[(proj, LANES, part * PAIRS), (d_n, LANES, 0)],
                            row_outs=[(LANES, BF16)], acc_outs=[(CONV_K, LANES)])
        dqkv.append(dx_p)
        d_conv.append(dw_p)
    d_conv = jnp.concatenate(d_conv, axis=1)

    def expand_bwd(col, b, g, db, dg):
        return (_dot32(db, b, _CONTRACT["nt"]), _dot32(dg, g, _CONTRACT["nt"]))

    dgates_b, dcums_g = _tiles(expand_bwd, name="expand_bwd", rows=rows, tm=tm, full_consts=[xb, xg],
                               row_ins=[(dbetax, WIDTH, 0), (dgcx, WIDTH, 0)],
                               row_outs=[(LANES, F32), (LANES, F32)])
    dcums_row = jnp.concatenate([jnp.zeros((rows, 8), F32), _rowform_to_lanes(dgrow, rows),
                                 dfrow.reshape(HEADS, rows).T, jnp.zeros((rows, LANES - 24), F32)], axis=1)

    def gates_bwd(col, lcv, lfv, a, dt, fb, pre, dgb, dcg, dcr):
        lane = _lane_ids(pre.shape)
        dgates = jnp.where(lane < 8, dgb, _cums_bwd(lcv, lfv, dcg + dcr))
        _, vjp = jax.vjp(_gates_elem, a, dt, fb, pre)
        da, ddt, dfb, dpre = vjp(dgates)
        return dpre, da, ddt, dfb

    dpre, d_a, d_dt, d_fb = _tiles(gates_bwd, name="gates_bwd", rows=rows, tm=rows,
                                   full_consts=[lc, lf, p_a, p_dt, p_fb],
                                   row_ins=[(proj, LANES, COL_SMALL), (dgates_b, LANES, 0), (dcums_g, LANES, 0),
                                            (dcums_row, LANES, 0)],
                                   row_outs=[(LANES, BF16)], acc_outs=[(1, LANES)] * 3)

    dproj = jnp.concatenate(dqkv + [dz] + dfqk + [dfv, dfgate, dpre], axis=1)
    grad_x, d_norm1_w = _mm_blocks(
        dproj, w_cat, name="d_h1_norm1_bwd", grid=(rows // t_half, 1), dims="nn",
        a_spec=pl.BlockSpec((t_half, D_CAT), lambda i, n: (i, 0)),
        b_spec=pl.BlockSpec((D_CAT, D_MODEL), lambda i, n: (0, 0)),
        o_spec=[half_blk, vec_blk], out_shape=[wide(F32), jax.ShapeDtypeStruct((1, D_MODEL), F32)],
        extra=[(x, half_blk), (dx1, half_blk), (norm1_w, vec_blk)],
        epilogue=lambda dh, xx, dres, w: norm_bwd(dh, xx, dres, w)[1:], n_acc=1)
    g_cat = _mm(dproj, h1, dims="tn", name="g_in", tm=1408, tn=D_MODEL, tk=rows)

    fold = lambda v: v.reshape(-1, HEAD_DIM).sum(axis=0)
    small = dict(
        loss=loss[0, 0],
        norm1_w=d_norm1_w, conv_w=d_conv, a_log=d_a[0, 8:16], dt_bias=d_dt[0, 8:16],
        out_norm_w=fold(d_on), f_bias=d_fb[0, 16:24], q_norm_w=fold(d_wqk[0]),
        k_norm_w=fold(d_wqk[1]), norm2_w=d_norm2_w, final_w=d_final_w)
    return grad_x, g_cat, g_out, g_gate, g_up, g_down, small


HBM_SPEC = pl.BlockSpec(memory_space=pltpu.HBM)


def _place():
    x, y, c = lax.axis_index("x"), lax.axis_index("y"), lax.axis_index("c")
    chips = [(1 - x, y), (x, 1 - y), (1 - x, 1 - y)]
    return x, y, c, 2 * x + y, (x, y, 1 - c), chips, [2 * cx + cy for cx, cy in chips]


def _remote(src, dst, send_sem, recv_sem, to):
    return pltpu.make_async_remote_copy(src_ref=src, dst_ref=dst, send_sem=send_sem, recv_sem=recv_sem,
                                        device_id=to, device_id_type=MESH)


SEM_SPEC =pl.BlockSpec(memory_space=pltpu.SEMAPHORE)
ANY_SPEC = pl.BlockSpec(memory_space=pl.ANY)
DATAFLOW = pltpu.SideEffectType.DATAFLOW_SIDE_EFFECTING


def _gather_plan(srcs, lands):
    x, y, c, own, sib, chips, chip_idx = _place()
    plan = []
    for src, land in zip(srcs, lands):
        for j, chip in enumerate(chips):
            plan.append((src, land.at[own], (*chip, c), land.at[chip_idx[j]]))
        plan.append((src, land.at[own], sib, land.at[own]))
    return plan


def _exchange_plan(srcs, lands):
    x, y, c, own, sib, chips, chip_idx = _place()
    plan = []
    for src, land in zip(srcs, lands):
        for j, chip in enumerate(chips):
            plan.append((src.at[chip_idx[j]], land.at[j], (*chip, c), land.at[j]))
    return plan


def _swap_plan(srcs, lands):
    x, y, c, own, sib, chips, chip_idx = _place()
    plan = []
    for src, land in zip(srcs, lands):
        h = src.shape[2] // 2
        plan.append((src.at[:, :, pl.ds(pl.multiple_of((1 - c) * h, LANES), h)], land, sib, land))
    return plan


def _in_proj_plan(srcs, lands):
    x, y, c, own, sib, chips, chip_idx = _place()
    (w, conv), (w_land, conv_land) = srcs, lands
    hw = w.shape[1] // 2
    half = lambda ref: ref.at[:, pl.ds(pl.multiple_of(c * hw, LANES), hw)]
    plan = []
    for j, chip in enumerate(chips):
        plan.append((half(w), half(w_land.at[own]), (*chip, c), half(w_land.at[chip_idx[j]])))
        plan.append((conv, conv_land.at[own], (*chip, c), conv_land.at[chip_idx[j]]))
    plan.append((w, w_land.at[own], sib, w_land.at[own]))
    plan.append((conv, conv_land.at[own], sib, conv_land.at[own]))
    return plan


def _forward_halves(landed):
    hw = landed.shape[2] // 2

    def body(in_ref, out_ref, send_sems, recv_sems):
        x, y, c, own, sib, chips, chip_idx = _place()
        half = lambda ref, hc: ref.at[:, pl.ds(pl.multiple_of(hc * hw, LANES), hw)]
        sent = [_remote(half(out_ref.at[chip_idx[j]], c), half(out_ref.at[chip_idx[j]], c),
                        send_sems.at[j], recv_sems.at[j], sib) for j in range(3)]
        for cp in sent:
            cp.start()
        for j in range(3):
            other = half(out_ref.at[chip_idx[j]], 1 - c)
            _remote(other, other, send_sems.at[j], recv_sems.at[j], sib).wait_recv()
        for cp in sent:
            cp.wait_send()

    return pl.pallas_call(
        body, name="gather_in_forward", out_shape=jax.ShapeDtypeStruct(landed.shape, landed.dtype),
        in_specs=[HBM_SPEC], out_specs=HBM_SPEC, input_output_aliases={0: 0},
        scratch_shapes=[pltpu.SemaphoreType.DMA((3,)), pltpu.SemaphoreType.DMA((3,))],
    )(landed)


def _split_start(name, plan_fn, srcs, land_shapes, n_copies, after):
    n = len(srcs)

    def body(*refs):
        src_refs, land_refs = refs[:n], refs[n:2 * n]
        send_sems, recv_sems = refs[2 * n + 1], refs[2 * n + 2]
        token = refs[-1]
        for k, (src, dst, to, _) in enumerate(plan_fn(src_refs, land_refs)):
            _remote(src, dst, send_sems.at[k], recv_sems.at[k], to).start()
        token[...] = jnp.zeros_like(token)

    lands = [pltpu.with_memory_space_constraint(lax.empty(s.shape, s.dtype), pltpu.HBM) for s in land_shapes]
    srcs = [pltpu.with_memory_space_constraint(s, pltpu.HBM) for s in srcs]
    out_shape = ([pltpu.SemaphoreType.DMA((n_copies,)), pltpu.SemaphoreType.DMA((n_copies,))]
                 + [pltpu.HBM(s.shape, s.dtype) for s in srcs] + [pltpu.HBM(s.shape, s.dtype) for s in land_shapes]
                 + [jax.ShapeDtypeStruct((8, LANES), F32)])
    res = pl.pallas_call(
        body, name=name, out_shape=out_shape,
        in_specs=[HBM_SPEC] * (2 * n) + [ANY_SPEC],
        out_specs=[SEM_SPEC, SEM_SPEC] + [HBM_SPEC] * (2 * n) + [pl.BlockSpec(memory_space=pltpu.VMEM)],
        input_output_aliases={i: 2 + i for i in range(2 * n)},
        compiler_params=pltpu.CompilerParams(has_side_effects=DATAFLOW),
    )(*srcs, *lands, after)
    return dict(sems=res[:2], srcs=res[2:2 + n], lands=res[2 + n:2 + 2 * n], token=res[-1], n=n)


def _split_wait(name, plan_fn, started, after):
    n = started["n"]

    def body(*refs):
        src_refs, land_refs = refs[:n], refs[n:2 * n]
        send_sems, recv_sems = refs[2 * n], refs[2 * n + 1]
        for k, (src, _, to, landed) in enumerate(plan_fn(src_refs, land_refs)):
            copy = _remote(src, landed, send_sems.at[k], recv_sems.at[k], to)
            copy.wait_send()
            copy.wait_recv()

    srcs, lands = started["srcs"], started["lands"]
    after = list(after) if isinstance(after, (list, tuple)) else [after]
    res = pl.pallas_call(
        body, name=name,
        out_shape=[pltpu.HBM(s.shape, s.dtype) for s in srcs] + [pltpu.HBM(s.shape, s.dtype) for s in lands],
        in_specs=[HBM_SPEC] * (2 * n) + [SEM_SPEC, SEM_SPEC] + [ANY_SPEC] * len(after),
        out_specs=[HBM_SPEC] * (2 * n),
        input_output_aliases={i: i for i in range(2 * n)},
        compiler_params=pltpu.CompilerParams(has_side_effects=DATAFLOW),
    )(*srcs, *lands, *started["sems"], *after)
    return res[n:]


def _swap_halves(stacks, name):
    n = len(stacks)

    def body(*refs):
        ins, outs = refs[:n], refs[n:2 * n]
        send_sems, recv_sems = refs[2 * n:]
        x, y, c, own, sib, chips, chip_idx = _place()
        cps = []
        for i in range(n):
            h = stacks[i].shape[2] // 2
            src = ins[i].at[:, :, pl.ds(pl.multiple_of((1 - c) * h, LANES), h)]
            cps.append(_remote(src, outs[i], send_sems.at[i], recv_sems.at[i], sib))
        for cp in cps:
            cp.start()
        for cp in cps:
            cp.wait()

    out_shape = [jax.ShapeDtypeStruct((N_CHIPS, s.shape[1], s.shape[2] // 2), s.dtype) for s in stacks]
    return pl.pallas_call(
        body, name=name, out_shape=out_shape,
        in_specs=[HBM_SPEC] * n, out_specs=[HBM_SPEC] * n,
        scratch_shapes=[pltpu.SemaphoreType.DMA((n,)), pltpu.SemaphoreType.DMA((n,))],
    )(*stacks)


def _add_half(stack, landed, place, name):
    _, rows, h = landed.shape

    def body(place_ref, a_ref, b_ref, o_ref, own_ref):
        part = (a_ref[...].astype(F32) + b_ref[...].astype(F32)).astype(o_ref.dtype)
        o_ref[...] = part

        @pl.when(pl.program_id(0) == place_ref[1])
        def _():
            own_ref[...] = part[0]

    return pl.pallas_call(
        body, name=name,
        out_shape=[jax.ShapeDtypeStruct(landed.shape, BF16), jax.ShapeDtypeStruct((rows, h), BF16)],
        grid_spec=pltpu.PrefetchScalarGridSpec(
            num_scalar_prefetch=1, grid=(N_CHIPS,),
            in_specs=[pl.BlockSpec((1, rows, h), lambda j, p: (j, 0, p[0])),
                      pl.BlockSpec((1, rows, h), lambda j, p: (j, 0, 0))],
            out_specs=[pl.BlockSpec((1, rows, h), lambda j, p: (j, 0, 0)),
                       pl.BlockSpec((rows, h), lambda j, p: (0, 0))]),
        compiler_params=_params(("arbitrary",)),
    )(place, stack, landed)


def _sum_partials(own_part, landed, name, untiled_rows=False):
    _, h, cols = landed.shape
    tc = LANES if untiled_rows else cols

    def body(own_ref, a_ref, o_ref):
        acc = own_ref[...].astype(F32)
        for s in range(3):
            acc = acc + a_ref[s].astype(F32)
        if untiled_rows:
            o_ref[:, 0, :] = acc
        else:
            o_ref[...] = acc

    if untiled_rows:
        out_shape, out_spec = jax.ShapeDtypeStruct((h, 1, cols), F32), pl.BlockSpec((h, 1, tc), lambda i: (0, 0, i))
    else:
        out_shape, out_spec = jax.ShapeDtypeStruct((h, cols), F32), pl.BlockSpec((h, tc), lambda i: (0, i))
    return pl.pallas_call(
        body, name=name, out_shape=out_shape, grid=(cols // tc,),
        in_specs=[pl.BlockSpec((h, tc), lambda i: (0, i)), pl.BlockSpec((3, h, tc), lambda i: (0, 0, i))],
        out_specs=out_spec, compiler_params=_params(("arbitrary",)),
    )(own_part, landed)


def _share_halves(halves, name):
    n = len(halves)

    def body(*refs):
        ins, outs = refs[:n], refs[n:2 * n]
        send_sems, recv_sems = refs[2 * n:]
        x, y, c, own, sib, chips, chip_idx = _place()
        cps = [_remote(ins[i], outs[i], send_sems.at[i], recv_sems.at[i], sib) for i in range(n)]
        for cp in cps:
            cp.start()
        for cp in cps:
            cp.wait()

    return pl.pallas_call(
        body, name=name,
        out_shape=[jax.ShapeDtypeStruct(p.shape, p.dtype) for p in halves],
        in_specs=[HBM_SPEC] * n, out_specs=[HBM_SPEC] * n,
        scratch_shapes=[pltpu.SemaphoreType.DMA((n,)), pltpu.SemaphoreType.DMA((n,))],
    )(*halves)


def _allreduce_small(packed):
    rows = packed.shape[0]
    n_dev = 8

    def body(in_ref, out_ref, gath, send_sems, recv_sems):
        x, y, c = lax.axis_index("x"), lax.axis_index("y"), lax.axis_index("c")
        me = 4 * x + 2 * y + c
        gath[me] = in_ref[...]
        cps = []
        for k in range(1, n_dev):
            fx, fy, fc = (k >> 2) & 1, (k >> 1) & 1, k & 1
            to = (x ^ fx, y ^ fy, c ^ fc)
            cps.append(_remote(in_ref, gath.at[me], send_sems.at[k - 1], recv_sems.at[k - 1], to))
        for cp in cps:
            cp.start()
        for k in range(1, n_dev):
            fx, fy, fc = (k >> 2) & 1, (k >> 1) & 1, k & 1
            src = 4 * (x ^ fx) + 2 * (y ^ fy) + (c ^ fc)
            slot = gath.at[src]
            _remote(slot, slot, send_sems.at[k - 1], recv_sems.at[k - 1], (x, y, c)).wait_recv()
        for cp in cps:
            cp.wait_send()
        acc = gath[0]
        for d in range(1, n_dev):
            acc = acc + gath[d]
        out_ref[...] = acc

    vm = pl.BlockSpec(memory_space=pltpu.VMEM)
    return pl.pallas_call(
        body, name="allreduce_small", out_shape=jax.ShapeDtypeStruct(packed.shape, F32),
        in_specs=[vm], out_specs=vm,
        scratch_shapes=[pltpu.VMEM((n_dev, rows, LANES), F32),
                        pltpu.SemaphoreType.DMA((n_dev - 1,)), pltpu.SemaphoreType.DMA((n_dev - 1,))],
    )(packed)


def _adam(col, w, g, m, v):
    m2 = ADAM_B1 * m + (1.0 - ADAM_B1) * g
    v2 = ADAM_B2 * v + (1.0 - ADAM_B2) * (g * g)
    m_hat = m2 / (1.0 - ADAM_B1 ** ADAM_STEP)
    v_hat = v2 / (1.0 - ADAM_B2 ** ADAM_STEP)
    delta = -ADAM_LR * (m_hat / (jnp.sqrt(v_hat) + ADAM_EPS) + ADAM_WD * w)
    return delta, m2, v2


def _adam_call(w, g, m, v, name):
    rows, cols = w.shape
    tm = rows
    for cand in (256, 352, 176, 128, 64, 48, 16, 8):
        if rows % cand == 0:
            tm = cand
            break
    return _tiles(_adam, name=name, rows=rows, tm=tm,
                  row_ins=[(w, cols, 0), (g, cols, 0), (m, cols, 0), (v, cols, 0)],
                  row_outs=[(cols, F32)] * 3)


def _adam_big(w, g_mine, g_other, m, v, place, name):
    rows, cols = w.shape
    tc = 256
    nt = cols // 2 // tc

    def body(place_ref, w_ref, gm_ref, go_ref, m_ref, v_ref, g_out, d_out, m_out, v_out):
        g = jnp.where(pl.program_id(0) == place_ref[0], gm_ref[...], go_ref[...])
        d, m2, v2 = _adam(None, w_ref[...], g, m_ref[...], v_ref[...])
        g_out[...] = g
        d_out[...] = d
        m_out[...] = m2
        v_out[...] = v2

    full = pl.BlockSpec((rows, tc), lambda hh, i, p: (0, hh * nt + i))
    half = pl.BlockSpec((rows, tc), lambda hh, i, p: (0, i))
    return pl.pallas_call(
        body, name=name, out_shape=[jax.ShapeDtypeStruct(w.shape, F32)] * 4,
        grid_spec=pltpu.PrefetchScalarGridSpec(
            num_scalar_prefetch=1, grid=(2, nt),
            in_specs=[full, half, half, full, full], out_specs=[full] * 4),
        compiler_params=_params(("arbitrary", "arbitrary")),
    )(place, w, g_mine, g_other, m, v)


def _adam_untiled_rows(w, g_mine, g_other, m, v, place, name):
    rows, _, cols = w.shape
    tc = 256
    nt = cols // 2 // tc
    rb = next(r for r in (206, 128, 103, rows) if rows % r == 0)

    def body(place_ref, w_ref, gm_ref, go_ref, m_ref, v_ref, g_out, d_out, m_out, v_out):
        g = jnp.where(pl.program_id(0) == place_ref[0], gm_ref[...], go_ref[...])
        d, m2, v2 = _adam(None, w_ref[...], g, m_ref[...], v_ref[...])
        g_out[...] = g
        d_out[...] = d
        m_out[...] = m2
        v_out[...] = v2

    full = pl.BlockSpec((rb, 1, tc), lambda hh, i, r, p: (r, 0, hh * nt + i))
    half = pl.BlockSpec((rb, 1, tc), lambda hh, i, r, p: (r, 0, i))
    return pl.pallas_call(
        body, name=name, out_shape=[jax.ShapeDtypeStruct(w.shape, F32)] * 4,
        grid_spec=pltpu.PrefetchScalarGridSpec(
            num_scalar_prefetch=1, grid=(2, nt, rows // rb),
            in_specs=[full, half, half, full, full], out_specs=[full] * 4),
        compiler_params=_params(("arbitrary", "arbitrary", "arbitrary")),
    )(place, w, g_mine, g_other, m, v)


def _pack(arrays, zero=None):
    flat = []
    for a in arrays:
        a = a.reshape(-1).astype(F32)
        if zero is not None:
            a = a + zero
        flat.append(jnp.pad(a, (0, (-a.size) % LANES)))
    out = jnp.concatenate(flat)
    out = jnp.pad(out, (0, (-out.size) % (8 * LANES)))
    return out.reshape(-1, LANES)


def _unpack(packed, shapes):
    flat = packed.reshape(-1)
    out, off = [], 0
    for s in shapes:
        size = int(np.prod(s))
        out.append(flat[off:off + size].reshape(s))
        off += size + (-size) % LANES
    return out


def kernel(x, norm1_w, w_in, gdn_conv_w, gdn_A_log, gdn_dt_bias, gdn_out_norm_w, fox_f_bias, fox_q_norm_w, fox_k_norm_w, w_out, norm2_w, w_ffn_gate, w_ffn_up, w_ffn_down, final_norm_w, loss_target, m_norm1_w, m_w_in, m_gdn_conv_w, m_gdn_A_log, m_gdn_dt_bias, m_gdn_out_norm_w, m_fox_f_bias, m_fox_q_norm_w, m_fox_k_norm_w, m_w_out, m_norm2_w, m_w_ffn_gate, m_w_ffn_up, m_w_ffn_down, m_final_norm_w, v_norm1_w, v_w_in, v_gdn_conv_w, v_gdn_A_log, v_gdn_dt_bias, v_gdn_out_norm_w, v_fox_f_bias, v_fox_q_norm_w, v_fox_k_norm_w, v_w_out, v_norm2_w, v_w_ffn_gate, v_w_ffn_up, v_w_ffn_down, v_final_norm_w):
    cx, cy, cc = lax.axis_index("x"), lax.axis_index("y"), lax.axis_index("c")
    own = 2 * cx + cy
    place = jnp.stack([cc, own]).astype(jnp.int32)

    names = ["w_in", "w_out", "w_gate", "w_up", "w_down"]
    is_t = [True, False, True, True, False]
    to_t = lambda a, t: a[0].T if t else a[0]
    from_t = lambda a, t: (a.T if t else a)[None]
    big_w = [to_t(a, t) for a, t in zip([w_in, w_out, w_ffn_gate, w_ffn_up, w_ffn_down], is_t)]
    big_m = [to_t(a, t) for a, t in zip([m_w_in, m_w_out, m_w_ffn_gate, m_w_ffn_up, m_w_ffn_down], is_t)]
    big_v = [to_t(a, t) for a, t in zip([v_w_in, v_w_out, v_w_ffn_gate, v_w_ffn_up, v_w_ffn_down], is_t)]
    shards = [big_w[0].astype(BF16)]
    small_w = [norm1_w, gdn_conv_w, gdn_A_log, gdn_dt_bias, gdn_out_norm_w, fox_f_bias, fox_q_norm_w,
               fox_k_norm_w, norm2_w, final_norm_w]
    small_m = [m_norm1_w, m_gdn_conv_w, m_gdn_A_log, m_gdn_dt_bias, m_gdn_out_norm_w, m_fox_f_bias,
               m_fox_q_norm_w, m_fox_k_norm_w, m_norm2_w, m_final_norm_w]
    small_v = [v_norm1_w, v_gdn_conv_w, v_gdn_A_log, v_gdn_dt_bias, v_gdn_out_norm_w, v_fox_f_bias,
               v_fox_q_norm_w, v_fox_k_norm_w, v_norm2_w, v_final_norm_w]
    first = _split_start("gather_in_start", _in_proj_plan, [shards[0], gdn_conv_w[0]],
                         [jax.ShapeDtypeStruct((N_CHIPS,) + shards[0].shape, BF16),
                          jax.ShapeDtypeStruct((N_CHIPS, CONV_K, 3 * WIDTH // N_CHIPS), F32)],
                         n_copies=8, after=shards[0])
    small_packed = [_pack(p, first["token"][0, 0]) for p in (small_w, small_m, small_v)]
    shards += [(w + first["token"][0, 0]).astype(BF16) for w in big_w[1:]]
    rest = {}

    def first_weights(after):
        w_in_g, conv_g = _split_wait("gather_in_wait", _in_proj_plan, first, [after] + small_packed)
        w_in_g = _forward_halves(w_in_g)
        rest.update(_split_start("gather_rest_start", _gather_plan, shards[1:],
                                 [jax.ShapeDtypeStruct((N_CHIPS,) + s.shape, BF16) for s in shards[1:]],
                                 n_copies=4 * len(shards[1:]), after=w_in_g))
        w_cat = _cat_weights(w_in_g.reshape(D_IN, D_MODEL))
        return w_cat + rest["token"][0, 0].astype(BF16), conv_g.transpose(1, 0, 2).reshape(CONV_K, 3 * WIDTH)

    def late_weights(after):
        w_out_g, w_gate_g, w_up_g, w_down_g = _split_wait("gather_rest_wait", _gather_plan, rest, after)
        return w_out_g.reshape(D_MODEL, D_MODEL), w_gate_g, w_up_g, w_down_g

    def start_reduction(stacks, nms, tag, landed=None):
        if landed is None:
            landed = _swap_halves(stacks, "rs_swap_" + tag)
        added = [_add_half(s, l, place, "rs_add_" + nm) for s, l, nm in zip(stacks, landed, nms)]
        parts = [a[0] for a in added]
        started = _split_start("exchange_" + tag + "_start", _exchange_plan, parts,
                               [jax.ShapeDtypeStruct((3,) + p.shape[1:], p.dtype) for p in parts],
                               n_copies=3 * len(parts), after=parts[0])
        return dict(own=[a[1] for a in added], started=started, tag=tag, names=nms)

    def finish_reduction(red, after, updates):
        landed = _split_wait("exchange_" + red["tag"] + "_wait", _exchange_plan, red["started"], after)
        halves = [_sum_partials(o, p, "rs_sum_" + nm, untiled_rows=nm == "w_in")
                  for o, p, nm in zip(red["own"], landed, red["names"])]
        others = _share_halves(halves, "rs_share_" + red["tag"])
        return [upd(gm, go) for upd, gm, go in zip(updates, halves, others)]

    def transport_update(b):
        def upd(gm, go):
            res = _adam_big(big_w[b], gm, go, big_m[b], big_v[b], place, "adam_" + names[b])
            early_done.append(res[1])
            return [from_t(a, is_t[b]) for a in res]
        return upd

    early_done = []

    def w_in_update(gm, go):
        rows3 = lambda a: jnp.transpose(a, (2, 0, 1))
        res = _adam_untiled_rows(rows3(w_in), gm, go, rows3(m_w_in), rows3(v_w_in), place, "adam_w_in")
        return [jnp.transpose(a, (1, 2, 0)) for a in res]

    early = {}

    def early_grads_ready(g_out, g_gate, g_up, g_down):
        stacks = [g_out.reshape(N_CHIPS, D_MODEL // N_CHIPS, D_MODEL), g_gate, g_up, g_down]
        swap = _split_start("swap_early_start", _swap_plan, stacks,
                            [jax.ShapeDtypeStruct(s.shape[:2] + (s.shape[2] // 2,), s.dtype) for s in stacks],
                            n_copies=len(stacks), after=stacks[0])
        early.update(stacks=stacks, swap=swap)
        return swap["token"][0, 0]

    def early_grads_continue(after):
        landed = _split_wait("swap_early_wait", _swap_plan, early["swap"], after)
        early.update(start_reduction(early["stacks"], names[1:], "early", landed))
        return early["started"]["token"][0, 0]

    grad_x, g_cat, _, _, _, _, small = _local_step(
        x[0], loss_target[0], norm1_w + first["token"][0, 0], gdn_A_log[0], gdn_dt_bias[0],
        gdn_out_norm_w[0], fox_f_bias[0], fox_q_norm_w[0], fox_k_norm_w[0], norm2_w, final_norm_w.reshape(1, -1),
        first_weights, late_weights, early_grads_ready, early_grads_continue)

    late = start_reduction([_uncat_grad(g_cat).reshape(N_CHIPS, D_IN // N_CHIPS, D_MODEL)], names[:1], "w_in")
    big_upd = finish_reduction(early, late["started"]["token"], [transport_update(b) for b in range(1, 5)])

    order = ["norm1_w", "conv_w", "a_log", "dt_bias", "out_norm_w", "f_bias", "q_norm_w", "k_norm_w",
             "norm2_w", "final_w"]
    red = _allreduce_small(_pack([small[k] for k in order] + [small["loss"]]))
    red_shapes = [(1, D_MODEL), (CONV_K, 3 * WIDTH), (1, HEADS), (1, HEADS), (1, HEAD_DIM), (1, HEADS),
                  (1, HEAD_DIM), (1, HEAD_DIM), (1, D_MODEL), (D_MODEL,), ()]
    red_list = _unpack(red, red_shapes)
    loss = red_list[-1]
    small_g = dict(zip(order, red_list[:-1]))
    shard_cols = 3 * WIDTH // N_CHIPS
    small_g["conv_w"] = lax.dynamic_slice_in_dim(small_g["conv_w"], own * shard_cols, shard_cols, axis=1)[None]
    small_gl = [small_g[k].reshape(w.shape) for k, w in zip(order, small_w)]
    s_delta, s_m, s_v = _adam_call(small_packed[0], _pack(small_gl), small_packed[1], small_packed[2], "adam_small")
    big_upd = finish_reduction(late, [s_delta] + early_done, [w_in_update]) + big_upd
    shapes = [w.shape for w in small_w]
    s_delta, s_m, s_v = _unpack(s_delta, shapes), _unpack(s_m, shapes), _unpack(s_v, shapes)

    big_pos = {1: 0, 9: 1, 11: 2, 12: 3, 13: 4}
    small_pos = {0: 0, 2: 1, 3: 2, 4: 3, 5: 4, 6: 5, 7: 6, 8: 7, 10: 8, 14: 9}
    grads, deltas, new_m, new_v = [], [], [], []
    for pos in range(15):
        if pos in big_pos:
            b = big_pos[pos]
            g, d, m2, v2 = big_upd[b]
            grads.append(g)
            deltas.append(d)
            new_m.append(m2)
            new_v.append(v2)
        else:
            s = small_pos[pos]
            grads.append(small_gl[s])
            deltas.append(s_delta[s])
            new_m.append(s_m[s])
            new_v.append(s_v[s])
    return (loss, grad_x[None], *grads, *deltas, *new_m, *new_v)
```

```python
import jax
import jax.numpy as jnp
import numpy as np
from jax import lax
from jax.experimental import pallas as pl
from jax.experimental.pallas import tpu as pltpu

F32 = jnp.float32
BF16 = jnp.bfloat16

D_MODEL = 1024
HEADS = 8
HEAD_DIM = 64
PAIRS = HEADS // 2
WIDTH = HEADS * HEAD_DIM
CHUNK = 64
CONV_K = 4
D_FF = 2816
FF_SHARD = D_FF // 4
EPS = 1e-6
SCALE = HEAD_DIM ** -0.5
LANES = 128
N_CHIPS = 4
D_IN = 4120
D_CAT = 4224
COL_SMALL = 4096 // LANES

ADAM_LR = 0.001
ADAM_B1 = 0.9
ADAM_B2 = 0.999
ADAM_EPS = 1e-08
ADAM_WD = 0.01
ADAM_STEP = 10

VMEM_LIMIT = 56 * 1024 * 1024
MESH = pl.DeviceIdType.MESH
HIGHEST = lax.Precision.HIGHEST


def _params(sem):
    return pltpu.CompilerParams(dimension_semantics=sem, vmem_limit_bytes=VMEM_LIMIT)


_CONTRACT = {"nn": ((1,), (0,)), "nt": ((1,), (1,)), "tn": ((0,), (0,))}


def _mm(a, b, *, dims, name, out_dtype=F32, add=None, tm=1024, tn=512, tk=512):
    if dims == "nn":
        (m, k), (k2, n) = a.shape, b.shape
    elif dims == "nt":
        (m, k), (n, k2) = a.shape, b.shape
    else:
        (k, m), (k2, n) = a.shape, b.shape
    assert k == k2, (a.shape, b.shape, dims)
    tm, tn, tk = min(tm, m), min(tn, n), min(tk, k)
    assert m % tm == 0 and n % tn == 0 and k % tk == 0, (m, n, k, tm, tn, tk)
    nk = k // tk
    a_spec = (pl.BlockSpec((tk, tm), lambda i, j, kk: (kk, i)) if dims == "tn"
              else pl.BlockSpec((tm, tk), lambda i, j, kk: (i, kk)))
    b_spec = (pl.BlockSpec((tn, tk), lambda i, j, kk: (j, kk)) if dims == "nt"
              else pl.BlockSpec((tk, tn), lambda i, j, kk: (kk, j)))
    o_spec = pl.BlockSpec((tm, tn), lambda i, j, kk: (i, j))
    contract = (_CONTRACT[dims], ((), ()))
    has_add = add is not None

    def body(*refs):
        a_ref, b_ref = refs[:2]
        add_ref = refs[2] if has_add else None
        o_ref = refs[3] if has_add else refs[2]
        part = lax.dot_general(a_ref[...].astype(BF16), b_ref[...].astype(BF16), contract,
                               preferred_element_type=F32)

        def finish(r):
            if has_add:
                r = r + add_ref[...].astype(F32)
            o_ref[...] = r.astype(out_dtype)

        if nk == 1:
            finish(part)
            return
        acc = refs[-1]
        kk = pl.program_id(2)

        @pl.when(kk == 0)
        def _():
            acc[...] = part

        @pl.when(kk > 0)
        def _():
            acc[...] += part

        @pl.when(kk == nk - 1)
        def _():
            finish(acc[...])

    ins = [a, b] + ([add] if has_add else [])
    in_specs = [a_spec, b_spec] + ([o_spec] if has_add else [])
    return pl.pallas_call(
        body, name=name, grid=(m // tm, n // tn, nk),
        in_specs=in_specs, out_specs=o_spec,
        out_shape=jax.ShapeDtypeStruct((m, n), out_dtype),
        scratch_shapes=[pltpu.VMEM((tm, tn), F32)] if nk > 1 else [],
        compiler_params=_params(("parallel", "parallel", "arbitrary")),
    )(*ins)


def _mm_blocks(a, b, *, name, grid, a_spec, b_spec, o_spec, out_shape, dims, n_sum=0, add=None, add_spec=None,
               epilogue=None, extra=(), n_acc=0):
    contract = (_CONTRACT[dims], ((), ()))
    has_add = add is not None
    n_in = 2 + has_add + len(extra)

    def body(*refs):
        a_ref, b_ref = refs[:2]
        dot = lambda x, y: lax.dot_general(x.astype(BF16), y.astype(BF16), contract, preferred_element_type=F32)
        if n_sum:
            r = dot(a_ref[0], b_ref[0])
            for s in range(1, n_sum):
                r = r + dot(a_ref[s], b_ref[s])
        else:
            r = dot(a_ref[...], b_ref[...])
        if has_add:
            r = r + refs[2][...].astype(F32)
        if epilogue is None:
            refs[-1][...] = r.astype(refs[-1].dtype)
        else:
            outs = epilogue(r, *[e[...] for e in refs[2 + has_add:n_in]])
            out_refs = refs[n_in:]
            n_plain = len(out_refs) - n_acc
            for o_ref, val in zip(out_refs[:n_plain], outs):
                o_ref[...] = val.astype(o_ref.dtype)
            if n_acc:
                @pl.when(pl.program_id(0) == 0)
                def _():
                    for o_ref in out_refs[n_plain:]:
                        o_ref[...] = jnp.zeros_like(o_ref)
                for o_ref, val in zip(out_refs[n_plain:], outs[n_plain:]):
                    o_ref[...] += val

    ins = [a, b] + ([add] if has_add else []) + [e[0] for e in extra]
    in_specs = [a_spec, b_spec] + ([add_spec] if has_add else []) + [e[1] for e in extra]
    sem = ("arbitrary" if n_acc else "parallel",) * len(grid)
    return pl.pallas_call(
        body, name=name, grid=grid, in_specs=in_specs, out_specs=o_spec, out_shape=out_shape,
        compiler_params=_params(sem),
    )(*ins)


def _tiles(fn, *, name, rows, tm, ncol=1, row_ins=(), col_consts=(), full_consts=(),
           row_outs=(), acc_outs=()):
    nt = rows // tm
    assert rows % tm == 0
    n_full, n_col, n_row = len(full_consts), len(col_consts), len(row_ins)
    n_ro, n_acc = len(row_outs), len(acc_outs)

    def body(*refs):
        ins = refs[:n_full + n_col + n_row]
        outs = refs[n_full + n_col + n_row:]
        i = pl.program_id(1)
        res = fn(pl.program_id(0), *[r[...] for r in ins])
        for r, v in zip(outs[:n_ro], res[:n_ro]):
            r[...] = v.astype(r.dtype)
        if n_acc:
            @pl.when(i == 0)
            def _():
                for r in outs[n_ro:]:
                    r[...] = jnp.zeros_like(r)
            for r, v in zip(outs[n_ro:], res[n_ro:]):
                r[...] += v

    in_specs = [pl.BlockSpec(a.shape, lambda j, i, nd=a.ndim: (0,) * nd) for a in full_consts]
    in_specs += [pl.BlockSpec((nr, w), lambda j, i, o=o: (0, o + j)) for (_, nr, w, o) in col_consts]
    in_specs += [pl.BlockSpec((tm, w), lambda j, i, o=o: (i, o + j)) for (_, w, o) in row_ins]
    out_specs = [pl.BlockSpec((tm, w), lambda j, i: (i, j)) for (w, _) in row_outs]
    out_specs += [pl.BlockSpec((nr, w), lambda j, i: (0, j)) for (nr, w) in acc_outs]
    out_shape = [jax.ShapeDtypeStruct((rows, w * ncol), dt) for (w, dt) in row_outs]
    out_shape += [jax.ShapeDtypeStruct((nr, w * ncol), F32) for (nr, w) in acc_outs]
    args = list(full_consts) + [c[0] for c in col_consts] + [r[0] for r in row_ins]
    out = pl.pallas_call(
        body, name=name, grid=(ncol, nt), in_specs=in_specs, out_specs=out_specs, out_shape=out_shape,
        compiler_params=_params(("parallel", "arbitrary")),
    )(*args)
    return out


def _rms(x, w):
    return x * lax.rsqrt(jnp.mean(x * x, axis=-1, keepdims=True) + EPS) * w


def _lane_lo(shape):
    return lax.broadcasted_iota(jnp.int32, shape, len(shape) - 1) < HEAD_DIM


def _pair_sum(x):
    lo = _lane_lo(x.shape)
    s0 = jnp.sum(jnp.where(lo, x, 0.0), axis=-1, keepdims=True)
    s1 = jnp.sum(jnp.where(lo, 0.0, x), axis=-1, keepdims=True)
    return jnp.where(lo, s0, s1)


def _head_col(x, lo, h):
    keep = lo if h == 0 else jnp.logical_not(lo)
    return jnp.max(jnp.where(keep, x, -jnp.inf), axis=-1, keepdims=True)


def _softplus(x):
    return jnp.maximum(x, 0.0) + jnp.log1p(jnp.exp(-jnp.abs(x)))


def _silu(x):
    return x * jax.nn.sigmoid(x)


def _dot(a, b, contract):
    return lax.dot_general(a.astype(BF16), b.astype(BF16), (contract, ((), ())),
                           preferred_element_type=F32)


def _dot32(a, b, contract):
    return lax.dot_general(a, b, (contract, ((), ())), precision=HIGHEST, preferred_element_type=F32)


def _bd(y):
    yy = jnp.concatenate([y, y], axis=0)
    r = lax.broadcasted_iota(jnp.int32, yy.shape, 0) < HEAD_DIM
    c = lax.broadcasted_iota(jnp.int32, yy.shape, 1) < HEAD_DIM
    return jnp.where(r == c, yy, 0.0)


def _pp(x, y):
    return _dot(x, _bd(y), _CONTRACT["nn"])


def _pp_nt(x, y):
    return _dot(x, _bd(y), _CONTRACT["nt"])


def _pp_tn(x, y):
    full = _dot(x, y, _CONTRACT["tn"])
    return jnp.where(_lane_lo((HEAD_DIM, LANES)), full[:HEAD_DIM], full[HEAD_DIM:])


def _gdn_masks():
    row = lax.broadcasted_iota(jnp.int32, (CHUNK, LANES), 0)
    col = lax.broadcasted_iota(jnp.int32, (CHUNK, LANES), 1) % HEAD_DIM
    return row, col


def _interleave(chains):
    live = list(chains)
    while live:
        for g in list(live):
            try:
                next(g)
            except StopIteration:
                live.remove(g)


def _gdn_forward(qkv, betax, gcx, grow, rows):
    nchunk = rows // CHUNK

    def body(q_ref, k_ref, v_ref, bx_ref, gx_ref, gr_ref, o_ref, ss_ref, ts_ref, state):
        n = pl.program_id(0)

        @pl.when(n == 0)
        def _():
            state[...] = jnp.zeros_like(state)

        row, col = _gdn_masks()
        incl, strict = col <= row, col < row

        def chain(p):
            lanes = pl.ds(p * LANES, LANES)
            q, k, v, bx, gx = q_ref[:, lanes], k_ref[:, lanes], v_ref[:, lanes], bx_ref[:, lanes], gx_ref[:, lanes]
            gr = gr_ref[0, p]
            glast = gx_ref[pl.ds(CHUNK - 1, 1), lanes]
            s = state[p]
            dm = jnp.where(incl, jnp.exp(jnp.minimum(gx - gr, 0.0)), 0.0)
            kb, vb, eg, qs = k * bx, v * bx, jnp.exp(gx), q * SCALE
            yield
            big_g, big_p = _pp_nt(kb, k), _pp_nt(qs, k)
            yield
            x = -jnp.where(strict, big_g * dm, 0.0)
            att = jnp.where(incl, big_p * dm, 0.0)
            tm = jnp.where(row == col, 1.0, 0.0) + x
            x = _pp(x, x)
            yield
            for _ in range(4):
                step, x = _pp(tm, x), _pp(x, x)
                yield
                tm = tm + step
            tm = tm + _pp(tm, x)
            yield
            u, w = _pp(tm, vb), _pp(tm, kb * eg)
            yield
            ws, qgs = _pp(w, s), _pp(qs * eg, s)
            yield
            vn = u - ws
            kd = k * jnp.exp(glast - gx)
            avn, upd = _pp(att, vn), _pp_tn(kd, vn)
            yield
            ss_ref[0, p] = s
            ts_ref[0, p] = tm
            o_ref[:, lanes] = qgs + avn
            state[p] = s * jnp.exp(glast) + upd

        _interleave([chain(p) for p in range(PAIRS)])

    blk = lambda j: pl.BlockSpec((CHUNK, WIDTH), lambda n, j=j: (n, j))
    sv = pl.BlockSpec((1, PAIRS, CHUNK, LANES), lambda n: (n, 0, 0, 0))
    return pl.pallas_call(
        body, name="gdn_fwd", grid=(nchunk,),
        in_specs=[blk(0), blk(1), blk(2), blk(0), blk(0),
                  pl.BlockSpec((1, PAIRS, 1, LANES), lambda n: (n, 0, 0, 0))],
        out_specs=[blk(0), sv, sv],
        out_shape=[jax.ShapeDtypeStruct((rows, WIDTH), F32),
                   jax.ShapeDtypeStruct((nchunk, PAIRS, CHUNK, LANES), F32),
                   jax.ShapeDtypeStruct((nchunk, PAIRS, CHUNK, LANES), F32)],
        scratch_shapes=[pltpu.VMEM((PAIRS, CHUNK, LANES), F32)],
        compiler_params=_params(("arbitrary",)),
    )(qkv, qkv, qkv, betax, gcx, grow)


def _gdn_backward(qkv, betax, gcx, grow, ssave, tsave, do, rows):
    nchunk = rows // CHUNK

    def body(q_ref, k_ref, v_ref, bx_ref, gx_ref, gr_ref, ss_ref, ts_ref, do_ref,
             dq_ref, dk_ref, dv_ref, dbx_ref, dgx_ref, dgr_ref, dstate):
        n = pl.program_id(0)

        @pl.when(n == 0)
        def _():
            dstate[...] = jnp.zeros_like(dstate)

        row, col = _gdn_masks()
        incl, strict = col <= row, col < row

        def chain(p):
            lanes = pl.ds(p * LANES, LANES)
            q, k, v, bx, gx = q_ref[:, lanes], k_ref[:, lanes], v_ref[:, lanes], bx_ref[:, lanes], gx_ref[:, lanes]
            gr = gr_ref[0, p]
            glast = gx_ref[pl.ds(CHUNK - 1, 1), lanes]
            s, tm, d_o = ss_ref[0, p], ts_ref[0, p], do_ref[:, lanes]
            ds_out = dstate[p]
            dm = jnp.where(incl, jnp.exp(jnp.minimum(gx - gr, 0.0)), 0.0)
            kb, vb, eg, qs = k * bx, v * bx, jnp.exp(gx), q * SCALE
            kbg, qg = kb * eg, qs * eg
            ed = jnp.exp(glast - gx)
            kd = k * ed
            eglast = jnp.exp(glast)
            yield
            big_g, big_p = _pp_nt(kb, k), _pp_nt(qs, k)
            u, w = _pp(tm, vb), _pp(tm, kbg)
            dqg, kds = _pp_nt(d_o, s), _pp(kd, ds_out)
            yield
            low = jnp.where(strict, big_g * dm, 0.0)
            att = jnp.where(incl, big_p * dm, 0.0)
            ws, atd = _pp(w, s), _pp_tn(att, d_o)
            yield
            vn = u - ws
            dvn = kds + atd
            dkd, datt_raw = _pp_nt(vn, ds_out), _pp_nt(d_o, vn)
            dw_neg, dvb = _pp_nt(dvn, s), _pp_tn(tm, dvn)
            dtm_a, wdv = _pp_nt(dvn, vb), _pp_tn(w, dvn)
            qgd = _pp_tn(qg, d_o)
            yield
            datt = jnp.where(incl, datt_raw, 0.0)
            dw = -dw_neg
            dtm_b, dkbg = _pp_nt(dw, kbg), _pp_tn(tm, dw)
            dbig_p = datt * dm
            dqs_a, dk_p = _pp(dbig_p, k), _pp_tn(dbig_p, qs)
            yield
            inner = _pp_tn(tm, dtm_a + dtm_b)
            yield
            dlow = jnp.where(strict, -_pp_nt(inner, tm), 0.0)
            yield
            dbig_g = dlow * dm
            dkb_a, dk_g = _pp(dbig_g, k), _pp_tn(dbig_g, kb)
            yield
            dkb = dkb_a + dkbg * eg
            dqs = dqs_a + dqg * eg
            dk = dk_g + dk_p + dkd * ed + dkb * bx
            z = dlow * low + datt * att
            kdterm = dkd * kd
            dglast = (jnp.sum(ds_out * s, axis=0, keepdims=True) * eglast
                      + jnp.sum(kdterm, axis=0, keepdims=True))
            dgx = dqg * qg + dkbg * kbg - kdterm
            dgx = dgx + jnp.where(col == 0, _pair_sum(z), 0.0)
            dgx = dgx + jnp.where(row == CHUNK - 1, dglast, 0.0)
            dq_ref[:, lanes] = dqs * SCALE
            dk_ref[:, lanes] = dk
            dv_ref[:, lanes] = dvb * bx
            dbx_ref[:, lanes] = dkb * k + dvb * v
            dgx_ref[:, lanes] = dgx
            dgr_ref[0, p] = -jnp.sum(z, axis=0, keepdims=True)
            dstate[p] = ds_out * eglast + qgd - wdv

        _interleave([chain(p) for p in range(PAIRS)])

    last = nchunk - 1
    blk = lambda j: pl.BlockSpec((CHUNK, WIDTH), lambda n, j=j: (last - n, j))
    sv = pl.BlockSpec((1, PAIRS, CHUNK, LANES), lambda n: (last - n, 0, 0, 0))
    gr_spec = pl.BlockSpec((1, PAIRS, 1, LANES), lambda n: (last - n, 0, 0, 0))
    wide = jax.ShapeDtypeStruct((rows, WIDTH), F32)
    return pl.pallas_call(
        body, name="gdn_bwd", grid=(nchunk,),
        in_specs=[blk(0), blk(1), blk(2), blk(0), blk(0), gr_spec, sv, sv, blk(0)],
        out_specs=[blk(0)] * 5 + [gr_spec],
        out_shape=[wide] * 5 + [jax.ShapeDtypeStruct((nchunk, PAIRS, 1, LANES), F32)],
        scratch_shapes=[pltpu.VMEM((PAIRS, CHUNK, LANES), F32)],
        compiler_params=_params(("arbitrary",)),
    )(qkv, qkv, qkv, betax, gcx, grow, ssave, tsave, do)


ATT_TQ = 256


def _att_scores(qh, kt, fk, diag):
    s = _dot(qh, kt, _CONTRACT["nt"]) - fk
    if diag:
        r = lax.broadcasted_iota(jnp.int32, s.shape, 0)
        c = lax.broadcasted_iota(jnp.int32, s.shape, 1)
        s = jnp.where(r >= c, s, -jnp.inf)
    return s


def _head_masks(n):
    lo = _lane_lo((n, LANES))
    return [lo, jnp.logical_not(lo)]


def _attention_forward(fqk, proj, frow, rows):
    tq = tk = min(ATT_TQ, rows)
    nq = rows // tq
    v_off = 3072 // LANES

    def body(q_ref, k_ref, v_ref, fr_ref, o_ref, lse_ref):
        qi = pl.program_id(1)
        q = q_ref[...] * SCALE
        keep_q, keep_k = _head_masks(tq), _head_masks(tk)
        qh = [jnp.where(keep_q[h], q, 0.0).astype(BF16) for h in range(2)]

        def tile(ki, carry, diag):
            k0 = pl.multiple_of(ki * tk, tk)
            kt = k_ref[pl.ds(k0, tk), :].astype(BF16)
            v_t = v_ref[pl.ds(k0, tk), :]
            out = [None, None]

            def chain(h):
                m, l, acc = carry[h]
                vt = jnp.where(keep_k[h], v_t, 0.0).astype(BF16)
                yield
                s = _att_scores(qh[h], kt, fr_ref[0, pl.ds(h, 1), pl.ds(k0, tk)], diag)
                yield
                m_new = jnp.maximum(m, jnp.max(s, axis=-1, keepdims=True))
                p = jnp.exp(s - m_new)
                alpha = jnp.exp(m - m_new)
                l = alpha * l + jnp.sum(p, axis=-1, keepdims=True)
                p_hi = p.astype(BF16)
                p_lo = p - p_hi.astype(F32)
                yield
                out[h] = (m_new, l, alpha * acc + _dot(p_hi, vt, _CONTRACT["nn"]) + _dot(p_lo, vt, _CONTRACT["nn"]))

            _interleave([chain(0), chain(1)])
            return tuple(out)

        one = (jnp.full((tq, 1), -jnp.inf, F32), jnp.zeros((tq, 1), F32), jnp.zeros((tq, LANES), F32))
        carry = lax.fori_loop(0, qi, lambda ki, c: tile(ki, c, False), (one, one))
        (m0, l0, acc0), (m1, l1, acc1) = tile(qi, carry, True)
        o_ref[...] = acc0 / l0 + acc1 / l1
        lse_ref[...] = jnp.where(keep_q[0], m0 + jnp.log(l0), m1 + jnp.log(l1))

    whole = lambda off: pl.BlockSpec((rows, LANES), lambda p, i, off=off: (0, off + p))
    qblk = lambda off: pl.BlockSpec((tq, LANES), lambda p, i, off=off: (i, off + p))
    wide = jax.ShapeDtypeStruct((rows, WIDTH), F32)
    return pl.pallas_call(
        body, name="fox_fwd", grid=(PAIRS, nq),
        in_specs=[qblk(0), whole(PAIRS), whole(v_off), pl.BlockSpec((1, 2, rows), lambda p, i: (p, 0, 0))],
        out_specs=[qblk(0), qblk(0)], out_shape=[wide, wide],
        compiler_params=_params(("parallel", "arbitrary")),
    )(fqk, fqk, proj, frow)


def _attention_backward(fqk, proj, frow, ao, lse, dao, rows):
    tq = tk = min(ATT_TQ, rows)
    nq = rows // tq
    v_off = 3072 // LANES

    def body(q_ref, k_ref, v_ref, fr_ref, o_ref, lse_ref, do_ref, dq_ref, dk_ref, dv_ref, dfr_ref):
        ki = pl.program_id(1)

        @pl.when(ki == 0)
        def _():
            dq_ref[...] = jnp.zeros_like(dq_ref)

        keep_q, keep_k = _head_masks(tq), _head_masks(tk)
        k_t = k_ref[...]
        kt = k_t.astype(BF16)
        vt = v_ref[...].astype(BF16)
        kh = [jnp.where(keep_k[h], k_t, 0.0).astype(BF16) for h in range(2)]
        fk = [fr_ref[0, pl.ds(h, 1), :] for h in range(2)]

        def tile(qi, carry, diag):
            dk, dv, df0, df1 = carry
            rows_q = pl.ds(pl.multiple_of(qi * tq, tq), tq)
            q, d_o, lse_t = q_ref[rows_q, :] * SCALE, do_ref[rows_q, :], lse_ref[rows_q, :]
            delta_x = _pair_sum(d_o.astype(BF16).astype(F32) * o_ref[rows_q, :])
            res = [None, None]

            def chain(h):
                qh = jnp.where(keep_q[h], q, 0.0).astype(BF16)
                doh = jnp.where(keep_q[h], d_o, 0.0).astype(BF16)
                lse_h, delta_h = _head_col(lse_t, keep_q[0], h), _head_col(delta_x, keep_q[0], h)
                yield
                s, dp = _att_scores(qh, kt, fk[h], diag), _dot(doh, vt, _CONTRACT["nt"])
                yield
                p = jnp.exp(s - lse_h)
                ds = p * (dp - delta_h)
                yield
                res[h] = (_dot(p, doh, _CONTRACT["tn"]), _dot(ds, qh, _CONTRACT["tn"]),
                          _dot(ds, kh[h], _CONTRACT["nn"]), jnp.sum(ds, axis=0, keepdims=True))

            _interleave([chain(0), chain(1)])
            (dv0, dk0, dq0, s0), (dv1, dk1, dq1, s1) = res
            dq_ref[rows_q, :] += (dq0 + dq1) * SCALE
            return dk + dk0 + dk1, dv + dv0 + dv1, df0 - s0, df1 - s1

        zero_kv = jnp.zeros((tk, LANES), F32)
        zero_f = jnp.zeros((1, tk), F32)
        carry = tile(ki, (zero_kv, zero_kv, zero_f, zero_f), True)
        dk, dv, df0, df1 = lax.fori_loop(ki + 1, nq, lambda qi, c: tile(qi, c, False), carry)
        dk_ref[...] = dk
        dv_ref[...] = dv.astype(dv_ref.dtype)
        dfr_ref[0, pl.ds(0, 1), :] = df0
        dfr_ref[0, pl.ds(1, 1), :] = df1

    whole = lambda off: pl.BlockSpec((rows, LANES), lambda p, i, off=off: (0, off + p))
    kblk = lambda off: pl.BlockSpec((tk, LANES), lambda p, i, off=off: (i, off + p))
    fr_spec = pl.BlockSpec((1, 2, tk), lambda p, i: (p, 0, i))
    wide = jax.ShapeDtypeStruct((rows, WIDTH), F32)
    return pl.pallas_call(
        body, name="fox_bwd", grid=(PAIRS, nq),
        in_specs=[whole(0), kblk(PAIRS), kblk(v_off), fr_spec, whole(0), whole(0), whole(0)],
        out_specs=[whole(0), kblk(0), kblk(0), fr_spec],
        out_shape=[wide, wide, jax.ShapeDtypeStruct((rows, WIDTH), BF16),
                   jax.ShapeDtypeStruct((PAIRS, 2, rows), F32)],
        compiler_params=_params(("parallel", "arbitrary")),
    )(fqk, fqk, proj, frow, ao, lse, dao)


def _lane_ids(shape):
    return lax.broadcasted_iota(jnp.int32, shape, len(shape) - 1)


def _gates_elem(a_log, dt_bias, f_bias, pre):
    lane = _lane_ids(pre.shape)
    beta = jax.nn.sigmoid(pre)
    g = -jnp.exp(a_log) * _softplus(pre + dt_bias)
    lf = -_softplus(-(pre + f_bias))
    return jnp.where(lane < 8, beta, jnp.where(lane < 16, g, jnp.where(lane < 24, lf, 0.0)))


def _tri_consts():
    r = np.arange(LANES)[:, None]
    c = np.arange(LANES)[None, :]
    full = (c <= r).astype(np.float32)
    chunked = full * ((r // CHUNK) == (c // CHUNK))
    return jnp.asarray(chunked), jnp.asarray(full)


def _cums_fwd(lc, lf, gates):
    rows = gates.shape[0]
    lane = _lane_ids((LANES, LANES))
    carry = jnp.zeros((1, LANES), F32)
    out = []
    for r in range(rows // LANES):
        blk = gates[r * LANES:(r + 1) * LANES]
        gc = _dot32(lc, blk, _CONTRACT["nn"])
        f = _dot32(lf, blk, _CONTRACT["nn"]) + carry
        carry = carry + jnp.sum(blk, axis=0, keepdims=True)
        out.append(jnp.where((lane >= 8) & (lane < 16), gc, jnp.where((lane >= 16) & (lane < 24), f, 0.0)))
    return jnp.concatenate(out, axis=0)


def _cums_bwd(lc, lf, dcums):
    rows = dcums.shape[0]
    lane = _lane_ids((LANES, LANES))
    is_g = (lane >= 8) & (lane < 16)
    is_f = (lane >= 16) & (lane < 24)
    carry = jnp.zeros((1, LANES), F32)
    out = [None] * (rows // LANES)
    for r in reversed(range(rows // LANES)):
        blk = dcums[r * LANES:(r + 1) * LANES]
        dg = jnp.where(is_g, blk, 0.0)
        df = jnp.where(is_f, blk, 0.0)
        out[r] = _dot32(lc, dg, _CONTRACT["tn"]) + _dot32(lf, df, _CONTRACT["tn"]) + carry
        carry = carry + jnp.sum(df, axis=0, keepdims=True)
    return jnp.concatenate(out, axis=0)


def _expand_consts():
    xb = np.zeros((LANES, WIDTH), np.float32)
    xg = np.zeros((LANES, WIDTH), np.float32)
    for h in range(HEADS):
        xb[h, h * HEAD_DIM:(h + 1) * HEAD_DIM] = 1.0
        xg[8 + h, h * HEAD_DIM:(h + 1) * HEAD_DIM] = 1.0
    return jnp.asarray(xb), jnp.asarray(xg)


def _shift_down(x, s):
    if s == 0:
        return x
    row = lax.broadcasted_iota(jnp.int32, x.shape, 0)
    return jnp.where(row >= s, pltpu.roll(x, s, 0), 0.0)


def _shift_up(x, s):
    if s == 0:
        return x
    n = x.shape[0]
    row = lax.broadcasted_iota(jnp.int32, x.shape, 0)
    return jnp.where(row < n - s, pltpu.roll(x, n - s, 0), 0.0)


def _row_of(cw, i):
    row = lax.broadcasted_iota(jnp.int32, cw.shape, 0)
    return jnp.sum(jnp.where(row == i, cw, 0.0), axis=0, keepdims=True)


def _conv(cw, x):
    c = jnp.zeros_like(x)
    for i in range(CONV_K):
        c = c + _row_of(cw, i) * _shift_down(x, CONV_K - 1 - i)
    return c


def _post_conv(is_qk, c):
    s = _silu(c)
    n = s * lax.rsqrt(_pair_sum(s * s) + EPS)
    return jnp.where(is_qk, n, s)


def _gdn_prep_fwd(col, cw, x):
    return (_post_conv(col < 2 * PAIRS, _conv(cw, x)),)


def _gdn_prep_bwd(is_qk, cw, x, dy):
    c = _conv(cw, x)
    _, vjp = jax.vjp(lambda cc: _post_conv(is_qk, cc), c)
    (dc,) = vjp(dy)
    dx = jnp.zeros_like(x)
    row = lax.broadcasted_iota(jnp.int32, cw.shape, 0)
    dcw = jnp.zeros(cw.shape, F32)
    for i in range(CONV_K):
        s = CONV_K - 1 - i
        dx = dx + _row_of(cw, i) * _shift_up(dc, s)
        dcw = dcw + jnp.where(row == i, jnp.sum(dc * _shift_down(x, s), axis=0, keepdims=True), 0.0)
    return dx, dcw


def _head_rms(w, x):
    return x * lax.rsqrt(_pair_sum(x * x) / HEAD_DIM + EPS) * w


def _cat_weights(w_in_t):
    tail = jnp.pad(w_in_t[4112:4120], ((0, D_CAT - D_IN), (0, 0)))
    return jnp.concatenate([w_in_t[:2048], w_in_t[2064:4112], w_in_t[2048:2064], tail], axis=0)


def _uncat_grad(g):
    return jnp.concatenate([g[:2048], g[4096:4112], g[2048:4096], g[4112:4120]], axis=0)


def _lanes_to_rowform(v8, rows):
    return v8.reshape(rows // CHUNK, CHUNK, HEADS).transpose(0, 2, 1).reshape(rows // CHUNK, PAIRS, 1, LANES)


def _rowform_to_lanes(v, rows):
    return v.reshape(rows // CHUNK, HEADS, CHUNK).transpose(0, 2, 1).reshape(rows, HEADS)


def _local_step(x, target, norm1_w, a_log, dt_bias, out_norm_w, f_bias, q_norm_w, k_norm_w,
                norm2_w, final_w, first_weights, late_weights, early_grads_ready, early_grads_continue):
    rows = x.shape[0]
    tm = min(512, rows)
    lc, lf = _tri_consts()
    xb, xg = _expand_consts()

    (h1,) = _tiles(lambda col, w, xx: (_rms(xx, w),), name="norm1", rows=rows, tm=tm,
                   full_consts=[norm1_w], row_ins=[(x, D_MODEL, 0)], row_outs=[(D_MODEL, BF16)])
    w_cat, conv_w = first_weights(h1)
    proj = _mm(h1, w_cat, dims="nt", name="in_proj", tn=1408, tk=1024)

    lane_pad = lambda v, off: jnp.pad(v.reshape(1, -1), ((0, 0), (off, LANES - off - v.size)))
    p_a, p_dt, p_fb = lane_pad(a_log, 8), lane_pad(dt_bias, 8), lane_pad(f_bias, 16)

    def gates_fwd(col, lcv, lfv, a, dt, fb, pre):
        gates = _gates_elem(a, dt, fb, pre)
        return gates, _cums_fwd(lcv, lfv, gates)

    gates, cums = _tiles(gates_fwd, name="gates", rows=rows, tm=rows,
                         full_consts=[lc, lf, p_a, p_dt, p_fb], row_ins=[(proj, LANES, COL_SMALL)],
                         row_outs=[(LANES, F32), (LANES, F32)])

    def expand_fwd(col, b, g, gt, cm):
        return (_dot32(gt, b, _CONTRACT["nn"]), _dot32(cm, g, _CONTRACT["nn"]))

    betax, gcx = _tiles(expand_fwd, name="expand", rows=rows, tm=tm, full_consts=[xb, xg],
                        row_ins=[(gates, LANES, 0), (cums, LANES, 0)],
                        row_outs=[(WIDTH, F32)] * 2)
    grow = _lanes_to_rowform(cums[:, 8:16], rows)
    frow = cums[:, 16:24].T.reshape(PAIRS, 2, rows)

    (qkv,) = _tiles(_gdn_prep_fwd, name="gdn_prep", rows=rows, tm=rows, ncol=3 * PAIRS,
                    col_consts=[(conv_w, CONV_K, LANES, 0)], row_ins=[(proj, LANES, 0)],
                    row_outs=[(LANES, F32)])
    o_gdn, ssave, tsave = _gdn_forward(qkv, betax, gcx, grow, rows)

    w_qk = jnp.concatenate([jnp.tile(q_norm_w.reshape(1, -1), (1, HEADS)),
                            jnp.tile(k_norm_w.reshape(1, -1), (1, HEADS))], axis=1)
    fox_off = 2048 // LANES
    (fqk,) = _tiles(lambda col, w, xx: (_head_rms(w, xx),), name="fox_prep", rows=rows, tm=rows, ncol=2 * PAIRS,
                    col_consts=[(w_qk, 1, LANES, 0)], row_ins=[(proj, LANES, fox_off)],
                    row_outs=[(LANES, F32)])
    ao, lse = _attention_forward(fqk, proj, frow, rows)

    w_on = jnp.tile(out_norm_w.reshape(1, -1), (1, 2))
    z_off, fg_off = 1536 // LANES, 3584 // LANES
    mix_g_fn = lambda w, o, z: _head_rms(w, o) * _silu(z)
    mix_f_fn = lambda a, g: a * jax.nn.sigmoid(g)
    (mix_g,) = _tiles(lambda col, w, o, z: (mix_g_fn(w, o, z),), name="mix_gdn", rows=rows, tm=rows, ncol=PAIRS,
                      full_consts=[w_on], row_ins=[(o_gdn, LANES, 0), (proj, LANES, z_off)],
                      row_outs=[(LANES, BF16)])
    (mix_f,) = _tiles(lambda col, a, g: (mix_f_fn(a, g),), name="mix_fox", rows=rows, tm=rows, ncol=PAIRS,
                      row_ins=[(ao, LANES, 0), (proj, LANES, fg_off)], row_outs=[(LANES, BF16)])
    mix = jnp.concatenate([mix_g, mix_f], axis=1)
    w_out, w_gate, w_up, w_down = late_weights(mix)
    t_rows, t_half = min(1024, rows), min(512, rows)
    n_rt = rows // t_rows
    row_blk = pl.BlockSpec((t_rows, D_MODEL), lambda i, n: (i, 0))
    half_blk = pl.BlockSpec((t_half, D_MODEL), lambda i, n: (i, 0))
    vec_blk = pl.BlockSpec((1, D_MODEL), lambda i, n: (0, 0))
    wide = lambda dt: jax.ShapeDtypeStruct((rows, D_MODEL), dt)
    x1, h2 = _mm_blocks(mix, w_out, name="out_proj_norm2", grid=(n_rt, 1), dims="nn",
                        a_spec=row_blk, b_spec=pl.BlockSpec((D_MODEL, D_MODEL), lambda i, n: (0, 0)),
                        o_spec=[row_blk, row_blk], out_shape=[wide(F32), wide(BF16)], add=x, add_spec=row_blk,
                        extra=[(norm2_w, vec_blk)], epilogue=lambda r, w: (r, _rms(r, w)))
    st_act = jax.ShapeDtypeStruct((N_CHIPS, rows, FF_SHARD), BF16)
    st_rows = pl.BlockSpec((None, rows, FF_SHARD), lambda i, j: (j, i, 0))

    def ffn_in(w_st, name):
        return _mm_blocks(h2, w_st, name=name, grid=(1, N_CHIPS), dims="nt",
                          a_spec=pl.BlockSpec((rows, D_MODEL), lambda i, j: (i, 0)),
                          b_spec=pl.BlockSpec((None, FF_SHARD, D_MODEL), lambda i, j: (j, 0, 0)),
                          o_spec=st_rows, out_shape=st_act)

    gate = ffn_in(w_gate, "ffn_gate")
    act_fn = lambda g, u: _silu(g) * u
    st_tile = pl.BlockSpec((None, t_rows, FF_SHARD), lambda i, j: (j, i, 0))
    up, act = _mm_blocks(h2, w_up, name="ffn_up_act", grid=(n_rt, N_CHIPS), dims="nt",
                         a_spec=pl.BlockSpec((t_rows, D_MODEL), lambda i, j: (i, 0)),
                         b_spec=pl.BlockSpec((None, FF_SHARD, D_MODEL), lambda i, j: (j, 0, 0)),
                         o_spec=[st_tile, st_tile], out_shape=[st_act, st_act], extra=[(gate, st_tile)],
                         epilogue=lambda u, g: (u, act_fn(g.astype(F32), u)))

    def final_fn(xx, tgt, w):
        y, vjp = jax.vjp(_rms, xx, w)
        err = y - tgt
        loss = 0.5 * jnp.sum(err * err) / D_MODEL
        dx, dw = vjp(err / D_MODEL)
        return dx, dx, jnp.full((1, LANES), loss, F32), dw

    dx2, dx2_b, loss, d_final_w = _mm_blocks(
        act, w_down, name="ffn_down_loss", grid=(rows // t_half, 1), dims="nn", n_sum=N_CHIPS,
        a_spec=pl.BlockSpec((N_CHIPS, t_half, FF_SHARD), lambda i, n: (0, i, 0)),
        b_spec=pl.BlockSpec((N_CHIPS, FF_SHARD, D_MODEL), lambda i, n: (0, 0, 0)),
        o_spec=[half_blk, half_blk, pl.BlockSpec((1, LANES), lambda i, n: (0, 0)), vec_blk],
        out_shape=[wide(F32), wide(BF16), jax.ShapeDtypeStruct((1, LANES), F32),
                   jax.ShapeDtypeStruct((1, D_MODEL), F32)],
        add=x1, add_spec=half_blk, extra=[(target, half_blk), (final_w, vec_blk)], epilogue=final_fn, n_acc=2)

    def act_bwd(d, g, u):
        _, vjp = jax.vjp(act_fn, g.astype(F32), u.astype(F32))
        return vjp(d)

    dgate, dup = _mm_blocks(dx2_b, w_down, name="d_act_gate_up", grid=(n_rt, N_CHIPS), dims="nt",
                            a_spec=pl.BlockSpec((t_rows, D_MODEL), lambda i, j: (i, 0)),
                            b_spec=pl.BlockSpec((None, FF_SHARD, D_MODEL), lambda i, j: (j, 0, 0)),
                            o_spec=[st_tile, st_tile], out_shape=[st_act, st_act],
                            extra=[(gate, st_tile), (up, st_tile)], epilogue=act_bwd)

    def g_ffn(d_st, other, name):
        return _mm_blocks(d_st, other, name=name, grid=(N_CHIPS, 1), dims="tn",
                          a_spec=pl.BlockSpec((None, rows, FF_SHARD), lambda j, n: (j, 0, 0)),
                          b_spec=pl.BlockSpec((rows, D_MODEL), lambda j, n: (0, 0)),
                          o_spec=pl.BlockSpec((None, FF_SHARD, D_MODEL), lambda j, n: (j, 0, 0)),
                          out_shape=jax.ShapeDtypeStruct((N_CHIPS, FF_SHARD, D_MODEL), BF16))

    g_down = g_ffn(act, dx2_b, "g_down")

    def norm_bwd(dh, xx, dres, w):
        _, vjp = jax.vjp(_rms, xx, w)
        dx, dw = vjp(dh)
        return dx + dres, dx + dres, dw

    def d_h2(d_st, w_st, name, add, **fused):
        return _mm_blocks(d_st, w_st, name=name, grid=(rows // t_half, 1), dims="nn", n_sum=N_CHIPS,
                          a_spec=pl.BlockSpec((N_CHIPS, t_half, FF_SHARD), lambda i, n: (0, i, 0)),
                          b_spec=pl.BlockSpec((N_CHIPS, FF_SHARD, D_MODEL), lambda i, n: (0, 0, 0)),
                          add=add, add_spec=half_blk, **fused)

    dh2_gate = d_h2(dgate, w_gate, "d_h2_gate", None, o_spec=half_blk, out_shape=wide(F32))
    dx1, dx1_b, d_norm2_w = d_h2(
        dup, w_up, "d_h2_up_norm2_bwd", dh2_gate, o_spec=[half_blk, half_blk, vec_blk],
        out_shape=[wide(F32), wide(BF16), jax.ShapeDtypeStruct((1, D_MODEL), F32)],
        extra=[(x1, half_blk), (dx2, half_blk), (norm2_w, vec_blk)], epilogue=norm_bwd, n_acc=1)
    g_gate, g_up = g_ffn(dgate, h2, "g_gate"), g_ffn(dup, h2, "g_up")
    dmix = _mm(dx1_b, w_out, dims="nt", name="d_mix", tn=D_MODEL, tk=1024)
    g_out = _mm(mix, dx1_b, dims="tn", name="g_out", tn=D_MODEL, tk=rows, out_dtype=BF16)
    w_on = w_on + early_grads_ready(g_out, g_gate, g_up, g_down)

    def mix_g_bwd(col, w, o, z, d):
        _, vjp = jax.vjp(mix_g_fn, w, o, z)
        dw, do_, dz = vjp(d)
        return do_, dz, dw

    do_gdn, dz, d_on = _tiles(mix_g_bwd, name="mix_gdn_bwd", rows=rows, tm=rows, ncol=PAIRS, full_consts=[w_on],
                              row_ins=[(o_gdn, LANES, 0), (proj, LANES, z_off), (dmix, LANES, 0)],
                              row_outs=[(LANES, F32), (LANES, BF16)], acc_outs=[(1, LANES)])

    def mix_f_bwd(col, a, g, d):
        _, vjp = jax.vjp(mix_f_fn, a, g)
        return vjp(d)

    dao, dfgate = _tiles(mix_f_bwd, name="mix_fox_bwd", rows=rows, tm=rows, ncol=PAIRS,
                         row_ins=[(ao, LANES, 0), (proj, LANES, fg_off), (dmix, LANES, PAIRS)],
                         row_outs=[(LANES, F32), (LANES, BF16)])

    dfq, dfk, dfv, dfrow = _attention_backward(fqk, proj, frow + early_grads_continue(dao), ao, lse, dao, rows)

    def fox_prep_bwd(col, w, xx, d):
        _, vjp = jax.vjp(_head_rms, w, xx)
        dw, dx = vjp(d)
        return dx, dw

    dfqk, d_wqk = [], []
    for part, d_n in enumerate((dfq, dfk)):
        dx_p, dw_p = _tiles(fox_prep_bwd, name="fox_prep_bwd_" + "qk"[part], rows=rows, tm=rows, ncol=PAIRS,
                            col_consts=[(w_qk, 1, LANES, part * PAIRS)],
                            row_ins=[(proj, LANES, fox_off + part * PAIRS), (d_n, LANES, 0)],
                            row_outs=[(LANES, BF16)], acc_outs=[(1, LANES)])
        dfqk.append(dx_p)
        d_wqk.append(dw_p)

    dq, dk, dv, dbetax, dgcx, dgrow = _gdn_backward(qkv, betax, gcx, grow, ssave, tsave, do_gdn, rows)
    dqkv, d_conv = [], []
    for part, d_n in enumerate((dq, dk, dv)):
        prep_bwd = lambda col, cw, xx, dy, is_qk=(part < 2): _gdn_prep_bwd(is_qk, cw, xx, dy)
        dx_p, dw_p = _tiles(prep_bwd, name="gdn_prep_bwd_" + "qkv"[part], rows=rows, tm=rows, ncol=PAIRS,
                            col_consts=[(conv_w, CONV_K, LANES, part * PAIRS)],
                            row_ins=[(proj, LANES, part * PAIRS), (d_n, LANES, 0)],
                            row_outs=[(LANES, BF16)], acc_outs=[(CONV_K, LANES)])
        dqkv.append(dx_p)
        d_conv.append(dw_p)
    d_conv = jnp.concatenate(d_conv, axis=1)

    def expand_bwd(col, b, g, db, dg):
        return (_dot32(db, b, _CONTRACT["nt"]), _dot32(dg, g, _CONTRACT["nt"]))

    dgates_b, dcums_g = _tiles(expand_bwd, name="expand_bwd", rows=rows, tm=tm, full_consts=[xb, xg],
                               row_ins=[(dbetax, WIDTH, 0), (dgcx, WIDTH, 0)],
                               row_outs=[(LANES, F32), (LANES, F32)])
    dcums_row = jnp.concatenate([jnp.zeros((rows, 8), F32), _rowform_to_lanes(dgrow, rows),
                                 dfrow.reshape(HEADS, rows).T, jnp.zeros((rows, LANES - 24), F32)], axis=1)

    def gates_bwd(col, lcv, lfv, a, dt, fb, pre, dgb, dcg, dcr):
        lane = _lane_ids(pre.shape)
        dgates = jnp.where(lane < 8, dgb, _cums_bwd(lcv, lfv, dcg + dcr))
        _, vjp = jax.vjp(_gates_elem, a, dt, fb, pre)
        da, ddt, dfb, dpre = vjp(dgates)
        return dpre, da, ddt, dfb

    dpre, d_a, d_dt, d_fb = _tiles(gates_bwd, name="gates_bwd", rows=rows, tm=rows,
                                   full_consts=[lc, lf, p_a, p_dt, p_fb],
                                   row_ins=[(proj, LANES, COL_SMALL), (dgates_b, LANES, 0), (dcums_g, LANES, 0),
                                            (dcums_row, LANES, 0)],
                                   row_outs=[(LANES, BF16)], acc_outs=[(1, LANES)] * 3)

    dproj = jnp.concatenate(dqkv + [dz] + dfqk + [dfv, dfgate, dpre], axis=1)
    grad_x, d_norm1_w = _mm_blocks(
        dproj, w_cat, name="d_h1_norm1_bwd", grid=(rows // t_half, 1), dims="nn",
        a_spec=pl.BlockSpec((t_half, D_CAT), lambda i, n: (i, 0)),
        b_spec=pl.BlockSpec((D_CAT, D_MODEL), lambda i, n: (0, 0)),
        o_spec=[half_blk, vec_blk], out_shape=[wide(F32), jax.ShapeDtypeStruct((1, D_MODEL), F32)],
        extra=[(x, half_blk), (dx1, half_blk), (norm1_w, vec_blk)],
        epilogue=lambda dh, xx, dres, w: norm_bwd(dh, xx, dres, w)[1:], n_acc=1)
    g_cat = _mm(dproj, h1, dims="tn", name="g_in", tm=1408, tn=D_MODEL, tk=rows)

    fold = lambda v: v.reshape(-1, HEAD_DIM).sum(axis=0)
    small = dict(
        loss=loss[0, 0],
        norm1_w=d_norm1_w, conv_w=d_conv, a_log=d_a[0, 8:16], dt_bias=d_dt[0, 8:16],
        out_norm_w=fold(d_on), f_bias=d_fb[0, 16:24], q_norm_w=fold(d_wqk[0]),
        k_norm_w=fold(d_wqk[1]), norm2_w=d_norm2_w, final_w=d_final_w)
    return grad_x, g_cat, g_out, g_gate, g_up, g_down, small


HBM_SPEC = pl.BlockSpec(memory_space=pltpu.HBM)


def _place():
    x, y, c = lax.axis_index("x"), lax.axis_index("y"), lax.axis_index("c")
    chips = [(1 - x, y), (x, 1 - y), (1 - x, 1 - y)]
    return x, y, c, 2 * x + y, (x, y, 1 - c), chips, [2 * cx + cy for cx, cy in chips]


def _remote(src, dst, send_sem, recv_sem, to):
    return pltpu.make_async_remote_copy(src_ref=src, dst_ref=dst, send_sem=send_sem, recv_sem=recv_sem,
                                        device_id=to, device_id_type=MESH)


SEM_SPEC =pl.BlockSpec(memory_space=pltpu.SEMAPHORE)
ANY_SPEC = pl.BlockSpec(memory_space=pl.ANY)
DATAFLOW = pltpu.SideEffectType.DATAFLOW_SIDE_EFFECTING


def _gather_plan(srcs, lands):
    x, y, c, own, sib, chips, chip_idx = _place()
    plan = []
    for src, land in zip(srcs, lands):
        for j, chip in enumerate(chips):
            plan.append((src, land.at[own], (*chip, c), land.at[chip_idx[j]]))
        plan.append((src, land.at[own], sib, land.at[own]))
    return plan


def _exchange_plan(srcs, lands):
    x, y, c, own, sib, chips, chip_idx = _place()
    plan = []
    for src, land in zip(srcs, lands):
        for j, chip in enumerate(chips):
            plan.append((src.at[chip_idx[j]], land.at[j], (*chip, c), land.at[j]))
    return plan


def _swap_plan(srcs, lands):
    x, y, c, own, sib, chips, chip_idx = _place()
    plan = []
    for src, land in zip(srcs, lands):
        h = src.shape[2] // 2
        plan.append((src.at[:, :, pl.ds(pl.multiple_of((1 - c) * h, LANES), h)], land, sib, land))
    return plan


def _in_proj_plan(srcs, lands):
    x, y, c, own, sib, chips, chip_idx = _place()
    (w, conv), (w_land, conv_land) = srcs, lands
    hw = w.shape[1] // 2
    half = lambda ref: ref.at[:, pl.ds(pl.multiple_of(c * hw, LANES), hw)]
    plan = []
    for j, chip in enumerate(chips):
        plan.append((half(w), half(w_land.at[own]), (*chip, c), half(w_land.at[chip_idx[j]])))
        plan.append((conv, conv_land.at[own], (*chip, c), conv_land.at[chip_idx[j]]))
    plan.append((w, w_land.at[own], sib, w_land.at[own]))
    plan.append((conv, conv_land.at[own], sib, conv_land.at[own]))
    return plan


def _forward_halves(landed):
    hw = landed.shape[2] // 2

    def body(in_ref, out_ref, send_sems, recv_sems):
        x, y, c, own, sib, chips, chip_idx = _place()
        half = lambda ref, hc: ref.at[:, pl.ds(pl.multiple_of(hc * hw, LANES), hw)]
        sent = [_remote(half(out_ref.at[chip_idx[j]], c), half(out_ref.at[chip_idx[j]], c),
                        send_sems.at[j], recv_sems.at[j], sib) for j in range(3)]
        for cp in sent:
            cp.start()
        for j in range(3):
            other = half(out_ref.at[chip_idx[j]], 1 - c)
            _remote(other, other, send_sems.at[j], recv_sems.at[j], sib).wait_recv()
        for cp in sent:
            cp.wait_send()

    return pl.pallas_call(
        body, name="gather_in_forward", out_shape=jax.ShapeDtypeStruct(landed.shape, landed.dtype),
        in_specs=[HBM_SPEC], out_specs=HBM_SPEC, input_output_aliases={0: 0},
        scratch_shapes=[pltpu.SemaphoreType.DMA((3,)), pltpu.SemaphoreType.DMA((3,))],
    )(landed)


def _split_start(name, plan_fn, srcs, land_shapes, n_copies, after):
    n = len(srcs)

    def body(*refs):
        src_refs, land_refs = refs[:n], refs[n:2 * n]
        send_sems, recv_sems = refs[2 * n + 1], refs[2 * n + 2]
        token = refs[-1]
        for k, (src, dst, to, _) in enumerate(plan_fn(src_refs, land_refs)):
            _remote(src, dst, send_sems.at[k], recv_sems.at[k], to).start()
        token[...] = jnp.zeros_like(token)

    lands = [pltpu.with_memory_space_constraint(lax.empty(s.shape, s.dtype), pltpu.HBM) for s in land_shapes]
    srcs = [pltpu.with_memory_space_constraint(s, pltpu.HBM) for s in srcs]
    out_shape = ([pltpu.SemaphoreType.DMA((n_copies,)), pltpu.SemaphoreType.DMA((n_copies,))]
                 + [pltpu.HBM(s.shape, s.dtype) for s in srcs] + [pltpu.HBM(s.shape, s.dtype) for s in land_shapes]
                 + [jax.ShapeDtypeStruct((8, LANES), F32)])
    res = pl.pallas_call(
        body, name=name, out_shape=out_shape,
        in_specs=[HBM_SPEC] * (2 * n) + [ANY_SPEC],
        out_specs=[SEM_SPEC, SEM_SPEC] + [HBM_SPEC] * (2 * n) + [pl.BlockSpec(memory_space=pltpu.VMEM)],
        input_output_aliases={i: 2 + i for i in range(2 * n)},
        compiler_params=pltpu.CompilerParams(has_side_effects=DATAFLOW),
    )(*srcs, *lands, after)
    return dict(sems=res[:2], srcs=res[2:2 + n], lands=res[2 + n:2 + 2 * n], token=res[-1], n=n)


def _split_wait(name, plan_fn, started, after):
    n = started["n"]

    def body(*refs):
        src_refs, land_refs = refs[:n], refs[n:2 * n]
        send_sems, recv_sems = refs[2 * n], refs[2 * n + 1]
        for k, (src, _, to, landed) in enumerate(plan_fn(src_refs, land_refs)):
            copy = _remote(src, landed, send_sems.at[k], recv_sems.at[k], to)
            copy.wait_send()
            copy.wait_recv()

    srcs, lands = started["srcs"], started["lands"]
    after = list(after) if isinstance(after, (list, tuple)) else [after]
    res = pl.pallas_call(
        body, name=name,
        out_shape=[pltpu.HBM(s.shape, s.dtype) for s in srcs] + [pltpu.HBM(s.shape, s.dtype) for s in lands],
        in_specs=[HBM_SPEC] * (2 * n) + [SEM_SPEC, SEM_SPEC] + [ANY_SPEC] * len(after),
        out_specs=[HBM_SPEC] * (2 * n),
        input_output_aliases={i: i for i in range(2 * n)},
        compiler_params=pltpu.CompilerParams(has_side_effects=DATAFLOW),
    )(*srcs, *lands, *started["sems"], *after)
    started["srcs_after"] = res[:n]
    return res[n:]


def _swap_halves(stacks, name):
    n = len(stacks)

    def body(*refs):
        ins, outs = refs[:n], refs[n:2 * n]
        send_sems, recv_sems = refs[2 * n:]
        x, y, c, own, sib, chips, chip_idx = _place()
        cps = []
        for i in range(n):
            h = stacks[i].shape[2] // 2
            src = ins[i].at[:, :, pl.ds(pl.multiple_of((1 - c) * h, LANES), h)]
            cps.append(_remote(src, outs[i], send_sems.at[i], recv_sems.at[i], sib))
        for cp in cps:
            cp.start()
        for cp in cps:
            cp.wait()

    out_shape = [jax.ShapeDtypeStruct((N_CHIPS, s.shape[1], s.shape[2] // 2), s.dtype) for s in stacks]
    return pl.pallas_call(
        body, name=name, out_shape=out_shape,
        in_specs=[HBM_SPEC] * n, out_specs=[HBM_SPEC] * n,
        scratch_shapes=[pltpu.SemaphoreType.DMA((n,)), pltpu.SemaphoreType.DMA((n,))],
    )(*stacks)


def _add_half(stack, landed, place, name):
    _, rows, h = landed.shape

    def body(place_ref, a_ref, b_ref, o_ref, own_ref):
        part = (a_ref[...].astype(F32) + b_ref[...].astype(F32)).astype(o_ref.dtype)
        o_ref[...] = part

        @pl.when(pl.program_id(0) == place_ref[1])
        def _():
            own_ref[...] = part[0]

    return pl.pallas_call(
        body, name=name,
        out_shape=[jax.ShapeDtypeStruct(landed.shape, BF16), jax.ShapeDtypeStruct((rows, h), BF16)],
        grid_spec=pltpu.PrefetchScalarGridSpec(
            num_scalar_prefetch=1, grid=(N_CHIPS,),
            in_specs=[pl.BlockSpec((1, rows, h), lambda j, p: (j, 0, p[0])),
                      pl.BlockSpec((1, rows, h), lambda j, p: (j, 0, 0))],
            out_specs=[pl.BlockSpec((1, rows, h), lambda j, p: (j, 0, 0)),
                       pl.BlockSpec((rows, h), lambda j, p: (0, 0))]),
        compiler_params=_params(("arbitrary",)),
    )(place, stack, landed)


def _sum_partials(own_part, landed, name, untiled_rows=False):
    _, h, cols = landed.shape
    tc = LANES if untiled_rows else cols

    def body(own_ref, a_ref, o_ref):
        acc = own_ref[...].astype(F32)
        for s in range(3):
            acc = acc + a_ref[s].astype(F32)
        if untiled_rows:
            o_ref[:, 0, :] = acc
        else:
            o_ref[...] = acc

    if untiled_rows:
        out_shape, out_spec = jax.ShapeDtypeStruct((h, 1, cols), F32), pl.BlockSpec((h, 1, tc), lambda i: (0, 0, i))
    else:
        out_shape, out_spec = jax.ShapeDtypeStruct((h, cols), F32), pl.BlockSpec((h, tc), lambda i: (0, i))
    return pl.pallas_call(
        body, name=name, out_shape=out_shape, grid=(cols // tc,),
        in_specs=[pl.BlockSpec((h, tc), lambda i: (0, i)), pl.BlockSpec((3, h, tc), lambda i: (0, 0, i))],
        out_specs=out_spec, compiler_params=_params(("arbitrary",)),
    )(own_part, landed)


def _share_halves(halves, name):
    n = len(halves)

    def body(*refs):
        ins, outs = refs[:n], refs[n:2 * n]
        send_sems, recv_sems = refs[2 * n:]
        x, y, c, own, sib, chips, chip_idx = _place()
        cps = [_remote(ins[i], outs[i], send_sems.at[i], recv_sems.at[i], sib) for i in range(n)]
        for cp in cps:
            cp.start()
        for cp in cps:
            cp.wait()

    return pl.pallas_call(
        body, name=name,
        out_shape=[jax.ShapeDtypeStruct(p.shape, p.dtype) for p in halves],
        in_specs=[HBM_SPEC] * n, out_specs=[HBM_SPEC] * n,
        scratch_shapes=[pltpu.SemaphoreType.DMA((n,)), pltpu.SemaphoreType.DMA((n,))],
    )(*halves)


def _allreduce_small(packed):
    rows = packed.shape[0]
    n_dev = 8

    def body(in_ref, out_ref, gath, send_sems, recv_sems):
        x, y, c = lax.axis_index("x"), lax.axis_index("y"), lax.axis_index("c")
        me = 4 * x + 2 * y + c
        gath[me] = in_ref[...]
        cps = []
        for k in range(1, n_dev):
            fx, fy, fc = (k >> 2) & 1, (k >> 1) & 1, k & 1
            to = (x ^ fx, y ^ fy, c ^ fc)
            cps.append(_remote(in_ref, gath.at[me], send_sems.at[k - 1], recv_sems.at[k - 1], to))
        for cp in cps:
            cp.start()
        for k in range(1, n_dev):
            fx, fy, fc = (k >> 2) & 1, (k >> 1) & 1, k & 1
            src = 4 * (x ^ fx) + 2 * (y ^ fy) + (c ^ fc)
            slot = gath.at[src]
            _remote(slot, slot, send_sems.at[k - 1], recv_sems.at[k - 1], (x, y, c)).wait_recv()
        for cp in cps:
            cp.wait_send()
        acc = gath[0]
        for d in range(1, n_dev):
            acc = acc + gath[d]
        out_ref[...] = acc

    vm = pl.BlockSpec(memory_space=pltpu.VMEM)
    return pl.pallas_call(
        body, name="allreduce_small", out_shape=jax.ShapeDtypeStruct(packed.shape, F32),
        in_specs=[vm], out_specs=vm,
        scratch_shapes=[pltpu.VMEM((n_dev, rows, LANES), F32),
                        pltpu.SemaphoreType.DMA((n_dev - 1,)), pltpu.SemaphoreType.DMA((n_dev - 1,))],
    )(packed)


def _adam(col, w, g, m, v):
    m2 = ADAM_B1 * m + (1.0 - ADAM_B1) * g
    v2 = ADAM_B2 * v + (1.0 - ADAM_B2) * (g * g)
    m_hat = m2 / (1.0 - ADAM_B1 ** ADAM_STEP)
    v_hat = v2 / (1.0 - ADAM_B2 ** ADAM_STEP)
    delta = -ADAM_LR * (m_hat / (jnp.sqrt(v_hat) + ADAM_EPS) + ADAM_WD * w)
    return delta, m2, v2


def _adam_call(w, g, m, v, name):
    rows, cols = w.shape
    tm = rows
    for cand in (256, 352, 176, 128, 64, 48, 16, 8):
        if rows % cand == 0:
            tm = cand
            break
    return _tiles(_adam, name=name, rows=rows, tm=tm,
                  row_ins=[(w, cols, 0), (g, cols, 0), (m, cols, 0), (v, cols, 0)],
                  row_outs=[(cols, F32)] * 3)


def _adam_big(w, g_mine, g_other, m, v, place, name):
    rows, cols = w.shape
    tc = 256
    nt = cols // 2 // tc

    def body(place_ref, w_ref, gm_ref, go_ref, m_ref, v_ref, g_out, d_out, m_out, v_out):
        g = jnp.where(pl.program_id(0) == place_ref[0], gm_ref[...], go_ref[...])
        d, m2, v2 = _adam(None, w_ref[...], g, m_ref[...], v_ref[...])
        g_out[...] = g
        d_out[...] = d
        m_out[...] = m2
        v_out[...] = v2

    full = pl.BlockSpec((rows, tc), lambda hh, i, p: (0, hh * nt + i))
    half = pl.BlockSpec((rows, tc), lambda hh, i, p: (0, i))
    return pl.pallas_call(
        body, name=name, out_shape=[jax.ShapeDtypeStruct(w.shape, F32)] * 4,
        grid_spec=pltpu.PrefetchScalarGridSpec(
            num_scalar_prefetch=1, grid=(2, nt),
            in_specs=[full, half, half, full, full], out_specs=[full] * 4),
        compiler_params=_params(("arbitrary", "arbitrary")),
    )(place, w, g_mine, g_other, m, v)


def _adam_untiled_rows(w, g_mine, g_other, m, v, place, name):
    rows, _, cols = w.shape
    tc = 256
    nt = cols // 2 // tc
    rb = next(r for r in (206, 128, 103, rows) if rows % r == 0)

    def body(place_ref, w_ref, gm_ref, go_ref, m_ref, v_ref, g_out, d_out, m_out, v_out):
        g = jnp.where(pl.program_id(0) == place_ref[0], gm_ref[...], go_ref[...])
        d, m2, v2 = _adam(None, w_ref[...], g, m_ref[...], v_ref[...])
        g_out[...] = g
        d_out[...] = d
        m_out[...] = m2
        v_out[...] = v2

    full = pl.BlockSpec((rb, 1, tc), lambda hh, i, r, p: (r, 0, hh * nt + i))
    half = pl.BlockSpec((rb, 1, tc), lambda hh, i, r, p: (r, 0, i))
    return pl.pallas_call(
        body, name=name, out_shape=[jax.ShapeDtypeStruct(w.shape, F32)] * 4,
        grid_spec=pltpu.PrefetchScalarGridSpec(
            num_scalar_prefetch=1, grid=(2, nt, rows // rb),
            in_specs=[full, half, half, full, full], out_specs=[full] * 4),
        compiler_params=_params(("arbitrary", "arbitrary", "arbitrary")),
    )(place, w, g_mine, g_other, m, v)


def _pack(arrays, zero=None):
    flat = []
    for a in arrays:
        a = a.reshape(-1).astype(F32)
        if zero is not None:
            a = a + zero
        flat.append(jnp.pad(a, (0, (-a.size) % LANES)))
    out = jnp.concatenate(flat)
    out = jnp.pad(out, (0, (-out.size) % (8 * LANES)))
    return out.reshape(-1, LANES)


def _unpack(packed, shapes):
    flat = packed.reshape(-1)
    out, off = [], 0
    for s in shapes:
        size = int(np.prod(s))
        out.append(flat[off:off + size].reshape(s))
        off += size + (-size) % LANES
    return out


def kernel(x, norm1_w, w_in, gdn_conv_w, gdn_A_log, gdn_dt_bias, gdn_out_norm_w, fox_f_bias, fox_q_norm_w, fox_k_norm_w, w_out, norm2_w, w_ffn_gate, w_ffn_up, w_ffn_down, final_norm_w, loss_target, m_norm1_w, m_w_in, m_gdn_conv_w, m_gdn_A_log, m_gdn_dt_bias, m_gdn_out_norm_w, m_fox_f_bias, m_fox_q_norm_w, m_fox_k_norm_w, m_w_out, m_norm2_w, m_w_ffn_gate, m_w_ffn_up, m_w_ffn_down, m_final_norm_w, v_norm1_w, v_w_in, v_gdn_conv_w, v_gdn_A_log, v_gdn_dt_bias, v_gdn_out_norm_w, v_fox_f_bias, v_fox_q_norm_w, v_fox_k_norm_w, v_w_out, v_norm2_w, v_w_ffn_gate, v_w_ffn_up, v_w_ffn_down, v_final_norm_w):
    cx, cy, cc = lax.axis_index("x"), lax.axis_index("y"), lax.axis_index("c")
    own = 2 * cx + cy
    place = jnp.stack([cc, own]).astype(jnp.int32)

    names = ["w_in", "w_out", "w_gate", "w_up", "w_down"]
    is_t = [True, False, True, True, False]
    to_t = lambda a, t: a[0].T if t else a[0]
    from_t = lambda a, t: (a.T if t else a)[None]
    big_w = [to_t(a, t) for a, t in zip([w_in, w_out, w_ffn_gate, w_ffn_up, w_ffn_down], is_t)]
    big_m = [to_t(a, t) for a, t in zip([m_w_in, m_w_out, m_w_ffn_gate, m_w_ffn_up, m_w_ffn_down], is_t)]
    big_v = [to_t(a, t) for a, t in zip([v_w_in, v_w_out, v_w_ffn_gate, v_w_ffn_up, v_w_ffn_down], is_t)]
    shards = [big_w[0].astype(BF16)]
    small_w = [norm1_w, gdn_conv_w, gdn_A_log, gdn_dt_bias, gdn_out_norm_w, fox_f_bias, fox_q_norm_w,
               fox_k_norm_w, norm2_w, final_norm_w]
    small_m = [m_norm1_w, m_gdn_conv_w, m_gdn_A_log, m_gdn_dt_bias, m_gdn_out_norm_w, m_fox_f_bias,
               m_fox_q_norm_w, m_fox_k_norm_w, m_norm2_w, m_final_norm_w]
    small_v = [v_norm1_w, v_gdn_conv_w, v_gdn_A_log, v_gdn_dt_bias, v_gdn_out_norm_w, v_fox_f_bias,
               v_fox_q_norm_w, v_fox_k_norm_w, v_norm2_w, v_final_norm_w]
    first = _split_start("gather_in_start", _in_proj_plan, [shards[0], gdn_conv_w[0]],
                         [jax.ShapeDtypeStruct((N_CHIPS,) + shards[0].shape, BF16),
                          jax.ShapeDtypeStruct((N_CHIPS, CONV_K, 3 * WIDTH // N_CHIPS), F32)],
                         n_copies=8, after=shards[0])
    small_packed = [_pack(p, first["token"][0, 0]) for p in (small_w, small_m, small_v)]
    shards += [(w + first["token"][0, 0]).astype(BF16) for w in big_w[1:]]
    rest = {}

    def first_weights(after):
        w_in_g, conv_g = _split_wait("gather_in_wait", _in_proj_plan, first, [after] + small_packed)
        w_in_g = _forward_halves(w_in_g)
        rest.update(_split_start("gather_rest_start", _gather_plan, shards[1:],
                                 [jax.ShapeDtypeStruct((N_CHIPS,) + s.shape, BF16) for s in shards[1:]],
                                 n_copies=4 * len(shards[1:]), after=w_in_g))
        w_cat = _cat_weights(w_in_g.reshape(D_IN, D_MODEL))
        return w_cat + rest["token"][0, 0].astype(BF16), conv_g.transpose(1, 0, 2).reshape(CONV_K, 3 * WIDTH)

    def late_weights(after):
        w_out_g, w_gate_g, w_up_g, w_down_g = _split_wait("gather_rest_wait", _gather_plan, rest, after)
        return w_out_g.reshape(D_MODEL, D_MODEL), w_gate_g, w_up_g, w_down_g

    def start_reduction(stacks, nms, tag, landed=None):
        if landed is None:
            landed = _swap_halves(stacks, "rs_swap_" + tag)
        added = [_add_half(s, l, place, "rs_add_" + nm) for s, l, nm in zip(stacks, landed, nms)]
        parts = [a[0] for a in added]
        started = _split_start("exchange_" + tag + "_start", _exchange_plan, parts,
                               [jax.ShapeDtypeStruct((3,) + p.shape[1:], p.dtype) for p in parts],
                               n_copies=3 * len(parts), after=parts[0])
        return dict(own=[a[1] for a in added], started=started, tag=tag, names=nms)

    def finish_reduction(red, after, updates):
        landed = _split_wait("exchange_" + red["tag"] + "_wait", _exchange_plan, red["started"], after)
        halves = [_sum_partials(o, p, "rs_sum_" + nm, untiled_rows=nm == "w_in")
                  for o, p, nm in zip(red["own"], landed, red["names"])]
        others = _share_halves(halves, "rs_share_" + red["tag"])
        return [upd(gm, go) for upd, gm, go in zip(updates, halves, others)]

    def transport_update(b):
        def upd(gm, go):
            res = _adam_big(big_w[b], gm, go, big_m[b], big_v[b], place, "adam_" + names[b])
            early_done.append(res[1])
            return [from_t(a, is_t[b]) for a in res]
        return upd

    early_done = []

    def w_in_update(gm, go):
        rows3 = lambda a: jnp.transpose(a, (2, 0, 1))
        res = _adam_untiled_rows(rows3(w_in), gm, go, rows3(m_w_in), rows3(v_w_in), place, "adam_w_in")
        return [jnp.transpose(a, (1, 2, 0)) for a in res]

    early = {}

    def early_grads_ready(g_out, g_gate, g_up, g_down):
        stacks = [g_out.reshape(N_CHIPS, D_MODEL // N_CHIPS, D_MODEL), g_gate, g_up, g_down]
        swap = _split_start("swap_early_start", _swap_plan, stacks,
                            [jax.ShapeDtypeStruct(s.shape[:2] + (s.shape[2] // 2,), s.dtype) for s in stacks],
                            n_copies=len(stacks), after=stacks[0])
        early.update(stacks=stacks, swap=swap)
        return swap["token"][0, 0]

    def early_grads_continue(after):
        landed = _split_wait("swap_early_wait", _swap_plan, early["swap"], after)
        early.update(start_reduction(early["swap"]["srcs_after"], names[1:], "early", landed))
        return early["started"]["token"][0, 0]

    grad_x, g_cat, _, _, _, _, small = _local_step(
        x[0], loss_target[0], norm1_w + first["token"][0, 0], gdn_A_log[0], gdn_dt_bias[0],
        gdn_out_norm_w[0], fox_f_bias[0], fox_q_norm_w[0], fox_k_norm_w[0], norm2_w, final_norm_w.reshape(1, -1),
        first_weights, late_weights, early_grads_ready, early_grads_continue)

    late = start_reduction([_uncat_grad(g_cat).reshape(N_CHIPS, D_IN // N_CHIPS, D_MODEL)], names[:1], "w_in")
    big_upd = finish_reduction(early, late["started"]["token"], [transport_update(b) for b in range(1, 5)])

    order = ["norm1_w", "conv_w", "a_log", "dt_bias", "out_norm_w", "f_bias", "q_norm_w", "k_norm_w",
             "norm2_w", "final_w"]
    red = _allreduce_small(_pack([small[k] for k in order] + [small["loss"]]))
    red_shapes = [(1, D_MODEL), (CONV_K, 3 * WIDTH), (1, HEADS), (1, HEADS), (1, HEAD_DIM), (1, HEADS),
                  (1, HEAD_DIM), (1, HEAD_DIM), (1, D_MODEL), (D_MODEL,), ()]
    red_list = _unpack(red, red_shapes)
    loss = red_list[-1]
    small_g = dict(zip(order, red_list[:-1]))
    shard_cols = 3 * WIDTH // N_CHIPS
    small_g["conv_w"] = lax.dynamic_slice_in_dim(small_g["conv_w"], own * shard_cols, shard_cols, axis=1)[None]
    small_gl = [small_g[k].reshape(w.shape) for k, w in zip(order, small_w)]
    s_delta, s_m, s_v = _adam_call(small_packed[0], _pack(small_gl), small_packed[1], small_packed[2], "adam_small")
    big_upd = finish_reduction(late, [s_delta] + early_done, [w_in_update]) + big_upd
    shapes = [w.shape for w in small_w]
    s_delta, s_m, s_v = _unpack(s_delta, shapes), _unpack(s_m, shapes), _unpack(s_v, shapes)

    big_pos = {1: 0, 9: 1, 11: 2, 12: 3, 13: 4}
    small_pos = {0: 0, 2: 1, 3: 2, 4: 3, 5: 4, 6: 5, 7: 6, 8: 7, 10: 8, 14: 9}
    grads, deltas, new_m, new_v = [], [], [], []
    for pos in range(15):
        if pos in big_pos:
            b = big_pos[pos]
            g, d, m2, v2 = big_upd[b]
            grads.append(g)
            deltas.append(d)
            new_m.append(m2)
            new_v.append(v2)
        else:
            s = small_pos[pos]
            grads.append(small_gl[s])
            deltas.append(s_delta[s])
            new_m.append(s_m[s])
            new_v.append(s_v[s])
    return (loss, grad_x[None], *grads, *deltas, *new_m, *new_v)
```

```python
import jax
import jax.numpy as jnp
import numpy as np
from jax import lax
from jax.experimental import pallas as pl
from jax.experimental.pallas import tpu as pltpu

F32 = jnp.float32
BF16 = jnp.bfloat16

D_MODEL = 1024
HEADS = 8
HEAD_DIM = 64
PAIRS = HEADS // 2
WIDTH = HEADS * HEAD_DIM
CHUNK = 64
CONV_K = 4
D_FF = 2816
FF_SHARD = D_FF // 4
EPS = 1e-6
SCALE = HEAD_DIM ** -0.5
LANES = 128
N_CHIPS = 4
D_IN = 4120
D_CAT = 4224
COL_SMALL = 4096 // LANES

ADAM_LR = 0.001
ADAM_B1 = 0.9
ADAM_B2 = 0.999
ADAM_EPS = 1e-08
ADAM_WD = 0.01
ADAM_STEP = 10

VMEM_LIMIT = 56 * 1024 * 1024
MESH = pl.DeviceIdType.MESH
HIGHEST = lax.Precision.HIGHEST


def _params(sem):
    return pltpu.CompilerParams(dimension_semantics=sem, vmem_limit_bytes=VMEM_LIMIT)


_CONTRACT = {"nn": ((1,), (0,)), "nt": ((1,), (1,)), "tn": ((0,), (0,))}


def _mm(a, b, *, dims, name, out_dtype=F32, add=None, tm=1024, tn=512, tk=512):
    if dims == "nn":
        (m, k), (k2, n) = a.shape, b.shape
    elif dims == "nt":
        (m, k), (n, k2) = a.shape, b.shape
    else:
        (k, m), (k2, n) = a.shape, b.shape
    assert k == k2, (a.shape, b.shape, dims)
    tm, tn, tk = min(tm, m), min(tn, n), min(tk, k)
    assert m % tm == 0 and n % tn == 0 and k % tk == 0, (m, n, k, tm, tn, tk)
    nk = k // tk
    a_spec = (pl.BlockSpec((tk, tm), lambda i, j, kk: (kk, i)) if dims == "tn"
              else pl.BlockSpec((tm, tk), lambda i, j, kk: (i, kk)))
    b_spec = (pl.BlockSpec((tn, tk), lambda i, j, kk: (j, kk)) if dims == "nt"
              else pl.BlockSpec((tk, tn), lambda i, j, kk: (kk, j)))
    o_spec = pl.BlockSpec((tm, tn), lambda i, j, kk: (i, j))
    contract = (_CONTRACT[dims], ((), ()))
    has_add = add is not None

    def body(*refs):
        a_ref, b_ref = refs[:2]
        add_ref = refs[2] if has_add else None
        o_ref = refs[3] if has_add else refs[2]
        part = lax.dot_general(a_ref[...].astype(BF16), b_ref[...].astype(BF16), contract,
                               preferred_element_type=F32)

        def finish(r):
            if has_add:
                r = r + add_ref[...].astype(F32)
            o_ref[...] = r.astype(out_dtype)

        if nk == 1:
            finish(part)
            return
        acc = refs[-1]
        kk = pl.program_id(2)

        @pl.when(kk == 0)
        def _():
            acc[...] = part

        @pl.when(kk > 0)
        def _():
            acc[...] += part

        @pl.when(kk == nk - 1)
        def _():
            finish(acc[...])

    ins = [a, b] + ([add] if has_add else [])
    in_specs = [a_spec, b_spec] + ([o_spec] if has_add else [])
    return pl.pallas_call(
        body, name=name, grid=(m // tm, n // tn, nk),
        in_specs=in_specs, out_specs=o_spec,
        out_shape=jax.ShapeDtypeStruct((m, n), out_dtype),
        scratch_shapes=[pltpu.VMEM((tm, tn), F32)] if nk > 1 else [],
        compiler_params=_params(("parallel", "parallel", "arbitrary")),
    )(*ins)


def _mm_blocks(a, b, *, name, grid, a_spec, b_spec, o_spec, out_shape, dims, n_sum=0, add=None, add_spec=None,
               epilogue=None, extra=(), n_acc=0):
    contract = (_CONTRACT[dims], ((), ()))
    has_add = add is not None
    n_in = 2 + has_add + len(extra)

    def body(*refs):
        a_ref, b_ref = refs[:2]
        dot = lambda x, y: lax.dot_general(x.astype(BF16), y.astype(BF16), contract, preferred_element_type=F32)
        if n_sum:
            r = dot(a_ref[0], b_ref[0])
            for s in range(1, n_sum):
                r = r + dot(a_ref[s], b_ref[s])
        else:
            r = dot(a_ref[...], b_ref[...])
        if has_add:
            r = r + refs[2][...].astype(F32)
        if epilogue is None:
            refs[-1][...] = r.astype(refs[-1].dtype)
        else:
            outs = epilogue(r, *[e[...] for e in refs[2 + has_add:n_in]])
            out_refs = refs[n_in:]
            n_plain = len(out_refs) - n_acc
            for o_ref, val in zip(out_refs[:n_plain], outs):
                o_ref[...] = val.astype(o_ref.dtype)
            if n_acc:
                @pl.when(pl.program_id(0) == 0)
                def _():
                    for o_ref in out_refs[n_plain:]:
                        o_ref[...] = jnp.zeros_like(o_ref)
                for o_ref, val in zip(out_refs[n_plain:], outs[n_plain:]):
                    o_ref[...] += val

    ins = [a, b] + ([add] if has_add else []) + [e[0] for e in extra]
    in_specs = [a_spec, b_spec] + ([add_spec] if has_add else []) + [e[1] for e in extra]
    sem = ("arbitrary" if n_acc else "parallel",) * len(grid)
    return pl.pallas_call(
        body, name=name, grid=grid, in_specs=in_specs, out_specs=o_spec, out_shape=out_shape,
        compiler_params=_params(sem),
    )(*ins)


def _tiles(fn, *, name, rows, tm, ncol=1, row_ins=(), col_consts=(), full_consts=(),
           row_outs=(), acc_outs=()):
    nt = rows // tm
    assert rows % tm == 0
    n_full, n_col, n_row = len(full_consts), len(col_consts), len(row_ins)
    n_ro, n_acc = len(row_outs), len(acc_outs)

    def body(*refs):
        ins = refs[:n_full + n_col + n_row]
        outs = refs[n_full + n_col + n_row:]
        i = pl.program_id(1)
        res = fn(pl.program_id(0), *[r[...] for r in ins])
        for r, v in zip(outs[:n_ro], res[:n_ro]):
            r[...] = v.astype(r.dtype)
        if n_acc:
            @pl.when(i == 0)
            def _():
                for r in outs[n_ro:]:
                    r[...] = jnp.zeros_like(r)
            for r, v in zip(outs[n_ro:], res[n_ro:]):
                r[...] += v

    in_specs = [pl.BlockSpec(a.shape, lambda j, i, nd=a.ndim: (0,) * nd) for a in full_consts]
    in_specs += [pl.BlockSpec((nr, w), lambda j, i, o=o: (0, o + j)) for (_, nr, w, o) in col_consts]
    in_specs += [pl.BlockSpec((tm, w), lambda j, i, o=o: (i, o + j)) for (_, w, o) in row_ins]
    out_specs = [pl.BlockSpec((tm, w), lambda j, i: (i, j)) for (w, _) in row_outs]
    out_specs += [pl.BlockSpec((nr, w), lambda j, i: (0, j)) for (nr, w) in acc_outs]
    out_shape = [jax.ShapeDtypeStruct((rows, w * ncol), dt) for (w, dt) in row_outs]
    out_shape += [jax.ShapeDtypeStruct((nr, w * ncol), F32) for (nr, w) in acc_outs]
    args = list(full_consts) + [c[0] for c in col_consts] + [r[0] for r in row_ins]
    out = pl.pallas_call(
        body, name=name, grid=(ncol, nt), in_specs=in_specs, out_specs=out_specs, out_shape=out_shape,
        compiler_params=_params(("parallel", "arbitrary")),
    )(*args)
    return out


def _rms(x, w):
    return x * lax.rsqrt(jnp.mean(x * x, axis=-1, keepdims=True) + EPS) * w


def _lane_lo(shape):
    return lax.broadcasted_iota(jnp.int32, shape, len(shape) - 1) < HEAD_DIM


def _pair_sum(x):
    lo = _lane_lo(x.shape)
    s0 = jnp.sum(jnp.where(lo, x, 0.0), axis=-1, keepdims=True)
    s1 = jnp.sum(jnp.where(lo, 0.0, x), axis=-1, keepdims=True)
    return jnp.where(lo, s0, s1)


def _head_col(x, lo, h):
    keep = lo if h == 0 else jnp.logical_not(lo)
    return jnp.max(jnp.where(keep, x, -jnp.inf), axis=-1, keepdims=True)


def _softplus(x):
    return jnp.maximum(x, 0.0) + jnp.log1p(jnp.exp(-jnp.abs(x)))


def _silu(x):
    return x * jax.nn.sigmoid(x)


def _dot(a, b, contract):
    return lax.dot_general(a.astype(BF16), b.astype(BF16), (contract, ((), ())),
                           preferred_element_type=F32)


def _dot32(a, b, contract):
    return lax.dot_general(a, b, (contract, ((), ())), precision=HIGHEST, preferred_element_type=F32)


def _bd(y):
    yy = jnp.concatenate([y, y], axis=0)
    r = lax.broadcasted_iota(jnp.int32, yy.shape, 0) < HEAD_DIM
    c = lax.broadcasted_iota(jnp.int32, yy.shape, 1) < HEAD_DIM
    return jnp.where(r == c, yy, 0.0)


def _pp(x, y):
    return _dot(x, _bd(y), _CONTRACT["nn"])


def _pp_nt(x, y):
    return _dot(x, _bd(y), _CONTRACT["nt"])


def _pp_tn(x, y):
    full = _dot(x, y, _CONTRACT["tn"])
    return jnp.where(_lane_lo((HEAD_DIM, LANES)), full[:HEAD_DIM], full[HEAD_DIM:])


def _gdn_masks():
    row = lax.broadcasted_iota(jnp.int32, (CHUNK, LANES), 0)
    col = lax.broadcasted_iota(jnp.int32, (CHUNK, LANES), 1) % HEAD_DIM
    return row, col


def _interleave(chains):
    live = list(chains)
    while live:
        for g in list(live):
            try:
                next(g)
            except StopIteration:
                live.remove(g)


def _gdn_forward(qkv, betax, gcx, grow, rows):
    nchunk = rows // CHUNK

    def body(q_ref, k_ref, v_ref, bx_ref, gx_ref, gr_ref, o_ref, ss_ref, ts_ref, state):
        n = pl.program_id(0)

        @pl.when(n == 0)
        def _():
            state[...] = jnp.zeros_like(state)

        row, col = _gdn_masks()
        incl, strict = col <= row, col < row

        def chain(p):
            lanes = pl.ds(p * LANES, LANES)
            q, k, v, bx, gx = q_ref[:, lanes], k_ref[:, lanes], v_ref[:, lanes], bx_ref[:, lanes], gx_ref[:, lanes]
            gr = gr_ref[0, p]
            glast = gx_ref[pl.ds(CHUNK - 1, 1), lanes]
            s = state[p]
            dm = jnp.where(incl, jnp.exp(jnp.minimum(gx - gr, 0.0)), 0.0)
            kb, vb, eg, qs = k * bx, v * bx, jnp.exp(gx), q * SCALE
            yield
            big_g, big_p = _pp_nt(kb, k), _pp_nt(qs, k)
            yield
            x = -jnp.where(strict, big_g * dm, 0.0)
            att = jnp.where(incl, big_p * dm, 0.0)
            tm = jnp.where(row == col, 1.0, 0.0) + x
            x = _pp(x, x)
            yield
            for _ in range(4):
                step, x = _pp(tm, x), _pp(x, x)
                yield
                tm = tm + step
            tm = tm + _pp(tm, x)
            yield
            u, w = _pp(tm, vb), _pp(tm, kb * eg)
            yield
            ws, qgs = _pp(w, s), _pp(qs * eg, s)
            yield
            vn = u - ws
            kd = k * jnp.exp(glast - gx)
            avn, upd = _pp(att, vn), _pp_tn(kd, vn)
            yield
            ss_ref[0, p] = s
            ts_ref[0, p] = tm
            o_ref[:, lanes] = qgs + avn
            state[p] = s * jnp.exp(glast) + upd

        _interleave([chain(p) for p in range(PAIRS)])

    blk = lambda j: pl.BlockSpec((CHUNK, WIDTH), lambda n, j=j: (n, j))
    sv = pl.BlockSpec((1, PAIRS, CHUNK, LANES), lambda n: (n, 0, 0, 0))
    return pl.pallas_call(
        body, name="gdn_fwd", grid=(nchunk,),
        in_specs=[blk(0), blk(1), blk(2), blk(0), blk(0),
                  pl.BlockSpec((1, PAIRS, 1, LANES), lambda n: (n, 0, 0, 0))],
        out_specs=[blk(0), sv, sv],
        out_shape=[jax.ShapeDtypeStruct((rows, WIDTH), F32),
                   jax.ShapeDtypeStruct((nchunk, PAIRS, CHUNK, LANES), F32),
                   jax.ShapeDtypeStruct((nchunk, PAIRS, CHUNK, LANES), F32)],
        scratch_shapes=[pltpu.VMEM((PAIRS, CHUNK, LANES), F32)],
        compiler_params=_params(("arbitrary",)),
    )(qkv, qkv, qkv, betax, gcx, grow)


def _gdn_backward(qkv, betax, gcx, grow, ssave, tsave, do, rows):
    nchunk = rows // CHUNK

    def body(q_ref, k_ref, v_ref, bx_ref, gx_ref, gr_ref, ss_ref, ts_ref, do_ref,
             dq_ref, dk_ref, dv_ref, dbx_ref, dgx_ref, dgr_ref, dstate):
        n = pl.program_id(0)

        @pl.when(n == 0)
        def _():
            dstate[...] = jnp.zeros_like(dstate)

        row, col = _gdn_masks()
        incl, strict = col <= row, col < row

        def chain(p):
            lanes = pl.ds(p * LANES, LANES)
            q, k, v, bx, gx = q_ref[:, lanes], k_ref[:, lanes], v_ref[:, lanes], bx_ref[:, lanes], gx_ref[:, lanes]
            gr = gr_ref[0, p]
            glast = gx_ref[pl.ds(CHUNK - 1, 1), lanes]
            s, tm, d_o = ss_ref[0, p], ts_ref[0, p], do_ref[:, lanes]
            ds_out = dstate[p]
            dm = jnp.where(incl, jnp.exp(jnp.minimum(gx - gr, 0.0)), 0.0)
            kb, vb, eg, qs = k * bx, v * bx, jnp.exp(gx), q * SCALE
            kbg, qg = kb * eg, qs * eg
            ed = jnp.exp(glast - gx)
            kd = k * ed
            eglast = jnp.exp(glast)
            yield
            big_g, big_p = _pp_nt(kb, k), _pp_nt(qs, k)
            u, w = _pp(tm, vb), _pp(tm, kbg)
            dqg, kds = _pp_nt(d_o, s), _pp(kd, ds_out)
            yield
            low = jnp.where(strict, big_g * dm, 0.0)
            att = jnp.where(incl, big_p * dm, 0.0)
            ws, atd = _pp(w, s), _pp_tn(att, d_o)
            yield
            vn = u - ws
            dvn = kds + atd
            dkd, datt_raw = _pp_nt(vn, ds_out), _pp_nt(d_o, vn)
            dw_neg, dvb = _pp_nt(dvn, s), _pp_tn(tm, dvn)
            dtm_a, wdv = _pp_nt(dvn, vb), _pp_tn(w, dvn)
            qgd = _pp_tn(qg, d_o)
            yield
            datt = jnp.where(incl, datt_raw, 0.0)
            dw = -dw_neg
            dtm_b, dkbg = _pp_nt(dw, kbg), _pp_tn(tm, dw)
            dbig_p = datt * dm
            dqs_a, dk_p = _pp(dbig_p, k), _pp_tn(dbig_p, qs)
            yield
            inner = _pp_tn(tm, dtm_a + dtm_b)
            yield
            dlow = jnp.where(strict, -_pp_nt(inner, tm), 0.0)
            yield
            dbig_g = dlow * dm
            dkb_a, dk_g = _pp(dbig_g, k), _pp_tn(dbig_g, kb)
            yield
            dkb = dkb_a + dkbg * eg
            dqs = dqs_a + dqg * eg
            dk = dk_g + dk_p + dkd * ed + dkb * bx
            z = dlow * low + datt * att
            kdterm = dkd * kd
            dglast = (jnp.sum(ds_out * s, axis=0, keepdims=True) * eglast
                      + jnp.sum(kdterm, axis=0, keepdims=True))
            dgx = dqg * qg + dkbg * kbg - kdterm
            dgx = dgx + jnp.where(col == 0, _pair_sum(z), 0.0)
            dgx = dgx + jnp.where(row == CHUNK - 1, dglast, 0.0)
            dq_ref[:, lanes] = dqs * SCALE
            dk_ref[:, lanes] = dk
            dv_ref[:, lanes] = dvb * bx
            dbx_ref[:, lanes] = dkb * k + dvb * v
            dgx_ref[:, lanes] = dgx
            dgr_ref[0, p] = -jnp.sum(z, axis=0, keepdims=True)
            dstate[p] = ds_out * eglast + qgd - wdv

        _interleave([chain(p) for p in range(PAIRS)])

    last = nchunk - 1
    blk = lambda j: pl.BlockSpec((CHUNK, WIDTH), lambda n, j=j: (last - n, j))
    sv = pl.BlockSpec((1, PAIRS, CHUNK, LANES), lambda n: (last - n, 0, 0, 0))
    gr_spec = pl.BlockSpec((1, PAIRS, 1, LANES), lambda n: (last - n, 0, 0, 0))
    wide = jax.ShapeDtypeStruct((rows, WIDTH), F32)
    return pl.pallas_call(
        body, name="gdn_bwd", grid=(nchunk,),
        in_specs=[blk(0), blk(1), blk(2), blk(0), blk(0), gr_spec, sv, sv, blk(0)],
        out_specs=[blk(0)] * 5 + [gr_spec],
        out_shape=[wide] * 5 + [jax.ShapeDtypeStruct((nchunk, PAIRS, 1, LANES), F32)],
        scratch_shapes=[pltpu.VMEM((PAIRS, CHUNK, LANES), F32)],
        compiler_params=_params(("arbitrary",)),
    )(qkv, qkv, qkv, betax, gcx, grow, ssave, tsave, do)


ATT_TQ = 256


def _att_scores(qh, kt, fk, diag):
    s = _dot(qh, kt, _CONTRACT["nt"]) - fk
    if diag:
        r = lax.broadcasted_iota(jnp.int32, s.shape, 0)
        c = lax.broadcasted_iota(jnp.int32, s.shape, 1)
        s = jnp.where(r >= c, s, -jnp.inf)
    return s


def _head_masks(n):
    lo = _lane_lo((n, LANES))
    return [lo, jnp.logical_not(lo)]


def _attention_forward(fqk, proj, frow, rows):
    tq = tk = min(ATT_TQ, rows)
    nq = rows // tq
    v_off = 3072 // LANES

    def body(q_ref, k_ref, v_ref, fr_ref, o_ref, lse_ref):
        qi = pl.program_id(1)
        q = q_ref[...] * SCALE
        keep_q, keep_k = _head_masks(tq), _head_masks(tk)
        qh = [jnp.where(keep_q[h], q, 0.0).astype(BF16) for h in range(2)]

        def tile(ki, carry, diag):
            k0 = pl.multiple_of(ki * tk, tk)
            kt = k_ref[pl.ds(k0, tk), :].astype(BF16)
            v_t = v_ref[pl.ds(k0, tk), :]
            out = [None, None]

            def chain(h):
                m, l, acc = carry[h]
                vt = jnp.where(keep_k[h], v_t, 0.0).astype(BF16)
                yield
                s = _att_scores(qh[h], kt, fr_ref[0, pl.ds(h, 1), pl.ds(k0, tk)], diag)
                yield
                m_new = jnp.maximum(m, jnp.max(s, axis=-1, keepdims=True))
                p = jnp.exp(s - m_new)
                alpha = jnp.exp(m - m_new)
                l = alpha * l + jnp.sum(p, axis=-1, keepdims=True)
                p_hi = p.astype(BF16)
                p_lo = p - p_hi.astype(F32)
                yield
                out[h] = (m_new, l, alpha * acc + _dot(p_hi, vt, _CONTRACT["nn"]) + _dot(p_lo, vt, _CONTRACT["nn"]))

            _interleave([chain(0), chain(1)])
            return tuple(out)

        one = (jnp.full((tq, 1), -jnp.inf, F32), jnp.zeros((tq, 1), F32), jnp.zeros((tq, LANES), F32))
        carry = lax.fori_loop(0, qi, lambda ki, c: tile(ki, c, False), (one, one))
        (m0, l0, acc0), (m1, l1, acc1) = tile(qi, carry, True)
        o_ref[...] = acc0 / l0 + acc1 / l1
        lse_ref[...] = jnp.where(keep_q[0], m0 + jnp.log(l0), m1 + jnp.log(l1))

    whole = lambda off: pl.BlockSpec((rows, LANES), lambda p, i, off=off: (0, off + p))
    qblk = lambda off: pl.BlockSpec((tq, LANES), lambda p, i, off=off: (i, off + p))
    wide = jax.ShapeDtypeStruct((rows, WIDTH), F32)
    return pl.pallas_call(
        body, name="fox_fwd", grid=(PAIRS, nq),
        in_specs=[qblk(0), whole(PAIRS), whole(v_off), pl.BlockSpec((1, 2, rows), lambda p, i: (p, 0, 0))],
        out_specs=[qblk(0), qblk(0)], out_shape=[wide, wide],
        compiler_params=_params(("parallel", "arbitrary")),
    )(fqk, fqk, proj, frow)


def _attention_backward(fqk, proj, frow, ao, lse, dao, rows):
    tq = tk = min(ATT_TQ, rows)
    nq = rows // tq
    v_off = 3072 // LANES

    def body(q_ref, k_ref, v_ref, fr_ref, o_ref, lse_ref, do_ref, dq_ref, dk_ref, dv_ref, dfr_ref):
        ki = pl.program_id(1)

        @pl.when(ki == 0)
        def _():
            dq_ref[...] = jnp.zeros_like(dq_ref)

        keep_q, keep_k = _head_masks(tq), _head_masks(tk)
        k_t = k_ref[...]
        kt = k_t.astype(BF16)
        vt = v_ref[...].astype(BF16)
        kh = [jnp.where(keep_k[h], k_t, 0.0).astype(BF16) for h in range(2)]
        fk = [fr_ref[0, pl.ds(h, 1), :] for h in range(2)]

        def tile(qi, carry, diag):
            dk, dv, df0, df1 = carry
            rows_q = pl.ds(pl.multiple_of(qi * tq, tq), tq)
            q, d_o, lse_t = q_ref[rows_q, :] * SCALE, do_ref[rows_q, :], lse_ref[rows_q, :]
            delta_x = _pair_sum(d_o.astype(BF16).astype(F32) * o_ref[rows_q, :])
            res = [None, None]

            def chain(h):
                qh = jnp.where(keep_q[h], q, 0.0).astype(BF16)
                doh = jnp.where(keep_q[h], d_o, 0.0).astype(BF16)
                lse_h, delta_h = _head_col(lse_t, keep_q[0], h), _head_col(delta_x, keep_q[0], h)
                yield
                s, dp = _att_scores(qh, kt, fk[h], diag), _dot(doh, vt, _CONTRACT["nt"])
                yield
                p = jnp.exp(s - lse_h)
                ds = p * (dp - delta_h)
                yield
                res[h] = (_dot(p, doh, _CONTRACT["tn"]), _dot(ds, qh, _CONTRACT["tn"]),
                          _dot(ds, kh[h], _CONTRACT["nn"]), jnp.sum(ds, axis=0, keepdims=True))

            _interleave([chain(0), chain(1)])
            (dv0, dk0, dq0, s0), (dv1, dk1, dq1, s1) = res
            dq_ref[rows_q, :] += (dq0 + dq1) * SCALE
            return dk + dk0 + dk1, dv + dv0 + dv1, df0 - s0, df1 - s1

        zero_kv = jnp.zeros((tk, LANES), F32)
        zero_f = jnp.zeros((1, tk), F32)
        carry = tile(ki, (zero_kv, zero_kv, zero_f, zero_f), True)
        dk, dv, df0, df1 = lax.fori_loop(ki + 1, nq, lambda qi, c: tile(qi, c, False), carry)
        dk_ref[...] = dk
        dv_ref[...] = dv.astype(dv_ref.dtype)
        dfr_ref[0, pl.ds(0, 1), :] = df0
        dfr_ref[0, pl.ds(1, 1), :] = df1

    whole = lambda off: pl.BlockSpec((rows, LANES), lambda p, i, off=off: (0, off + p))
    kblk = lambda off: pl.BlockSpec((tk, LANES), lambda p, i, off=off: (i, off + p))
    fr_spec = pl.BlockSpec((1, 2, tk), lambda p, i: (p, 0, i))
    wide = jax.ShapeDtypeStruct((rows, WIDTH), F32)
    return pl.pallas_call(
        body, name="fox_bwd", grid=(PAIRS, nq),
        in_specs=[whole(0), kblk(PAIRS), kblk(v_off), fr_spec, whole(0), whole(0), whole(0)],
        out_specs=[whole(0), kblk(0), kblk(0), fr_spec],
        out_shape=[wide, wide, jax.ShapeDtypeStruct((rows, WIDTH), BF16),
                   jax.ShapeDtypeStruct((PAIRS, 2, rows), F32)],
        compiler_params=_params(("parallel", "arbitrary")),
    )(fqk, fqk, proj, frow, ao, lse, dao)


def _lane_ids(shape):
    return lax.broadcasted_iota(jnp.int32, shape, len(shape) - 1)


def _gates_elem(a_log, dt_bias, f_bias, pre):
    lane = _lane_ids(pre.shape)
    beta = jax.nn.sigmoid(pre)
    g = -jnp.exp(a_log) * _softplus(pre + dt_bias)
    lf = -_softplus(-(pre + f_bias))
    return jnp.where(lane < 8, beta, jnp.where(lane < 16, g, jnp.where(lane < 24, lf, 0.0)))


def _tri_consts():
    r = np.arange(LANES)[:, None]
    c = np.arange(LANES)[None, :]
    full = (c <= r).astype(np.float32)
    chunked = full * ((r // CHUNK) == (c // CHUNK))
    return jnp.asarray(chunked), jnp.asarray(full)


def _cums_fwd(lc, lf, gates):
    rows = gates.shape[0]
    lane = _lane_ids((LANES, LANES))
    carry = jnp.zeros((1, LANES), F32)
    out = []
    for r in range(rows // LANES):
        blk = gates[r * LANES:(r + 1) * LANES]
        gc = _dot32(lc, blk, _CONTRACT["nn"])
        f = _dot32(lf, blk, _CONTRACT["nn"]) + carry
        carry = carry + jnp.sum(blk, axis=0, keepdims=True)
        out.append(jnp.where((lane >= 8) & (lane < 16), gc, jnp.where((lane >= 16) & (lane < 24), f, 0.0)))
    return jnp.concatenate(out, axis=0)


def _cums_bwd(lc, lf, dcums):
    rows = dcums.shape[0]
    lane = _lane_ids((LANES, LANES))
    is_g = (lane >= 8) & (lane < 16)
    is_f = (lane >= 16) & (lane < 24)
    carry = jnp.zeros((1, LANES), F32)
    out = [None] * (rows // LANES)
    for r in reversed(range(rows // LANES)):
        blk = dcums[r * LANES:(r + 1) * LANES]
        dg = jnp.where(is_g, blk, 0.0)
        df = jnp.where(is_f, blk, 0.0)
        out[r] = _dot32(lc, dg, _CONTRACT["tn"]) + _dot32(lf, df, _CONTRACT["tn"]) + carry
        carry = carry + jnp.sum(df, axis=0, keepdims=True)
    return jnp.concatenate(out, axis=0)


def _expand_consts():
    xb = np.zeros((LANES, WIDTH), np.float32)
    xg = np.zeros((LANES, WIDTH), np.float32)
    for h in range(HEADS):
        xb[h, h * HEAD_DIM:(h + 1) * HEAD_DIM] = 1.0
        xg[8 + h, h * HEAD_DIM:(h + 1) * HEAD_DIM] = 1.0
    return jnp.asarray(xb), jnp.asarray(xg)


def _shift_down(x, s):
    if s == 0:
        return x
    row = lax.broadcasted_iota(jnp.int32, x.shape, 0)
    return jnp.where(row >= s, pltpu.roll(x, s, 0), 0.0)


def _shift_up(x, s):
    if s == 0:
        return x
    n = x.shape[0]
    row = lax.broadcasted_iota(jnp.int32, x.shape, 0)
    return jnp.where(row < n - s, pltpu.roll(x, n - s, 0), 0.0)


def _row_of(cw, i):
    row = lax.broadcasted_iota(jnp.int32, cw.shape, 0)
    return jnp.sum(jnp.where(row == i, cw, 0.0), axis=0, keepdims=True)


def _conv(cw, x):
    c = jnp.zeros_like(x)
    for i in range(CONV_K):
        c = c + _row_of(cw, i) * _shift_down(x, CONV_K - 1 - i)
    return c


def _post_conv(is_qk, c):
    s = _silu(c)
    n = s * lax.rsqrt(_pair_sum(s * s) + EPS)
    return jnp.where(is_qk, n, s)


def _gdn_prep_fwd(col, cw, x):
    return (_post_conv(col < 2 * PAIRS, _conv(cw, x)),)


def _gdn_prep_bwd(is_qk, cw, x, dy):
    c = _conv(cw, x)
    _, vjp = jax.vjp(lambda cc: _post_conv(is_qk, cc), c)
    (dc,) = vjp(dy)
    dx = jnp.zeros_like(x)
    row = lax.broadcasted_iota(jnp.int32, cw.shape, 0)
    dcw = jnp.zeros(cw.shape, F32)
    for i in range(CONV_K):
        s = CONV_K - 1 - i
        dx = dx + _row_of(cw, i) * _shift_up(dc, s)
        dcw = dcw + jnp.where(row == i, jnp.sum(dc * _shift_down(x, s), axis=0, keepdims=True), 0.0)
    return dx, dcw


def _head_rms(w, x):
    return x * lax.rsqrt(_pair_sum(x * x) / HEAD_DIM + EPS) * w


def _cat_weights(w_in_t):
    tail = jnp.pad(w_in_t[4112:4120], ((0, D_CAT - D_IN), (0, 0)))
    return jnp.concatenate([w_in_t[:2048], w_in_t[2064:4112], w_in_t[2048:2064], tail], axis=0)


def _uncat_grad(g):
    return jnp.concatenate([g[:2048], g[4096:4112], g[2048:4096], g[4112:4120]], axis=0)


def _lanes_to_rowform(v8, rows):
    return v8.reshape(rows // CHUNK, CHUNK, HEADS).transpose(0, 2, 1).reshape(rows // CHUNK, PAIRS, 1, LANES)


def _rowform_to_lanes(v, rows):
    return v.reshape(rows // CHUNK, HEADS, CHUNK).transpose(0, 2, 1).reshape(rows, HEADS)


def _local_step(x, target, norm1_w, a_log, dt_bias, out_norm_w, f_bias, q_norm_w, k_norm_w,
                norm2_w, final_w, first_weights, late_weights, early_grads_ready, early_grads_continue):
    rows = x.shape[0]
    tm = min(512, rows)
    lc, lf = _tri_consts()
    xb, xg = _expand_consts()

    (h1,) = _tiles(lambda col, w, xx: (_rms(xx, w),), name="norm1", rows=rows, tm=tm,
                   full_consts=[norm1_w], row_ins=[(x, D_MODEL, 0)], row_outs=[(D_MODEL, BF16)])
    w_cat, conv_w = first_weights(h1)
    proj = _mm(h1, w_cat, dims="nt", name="in_proj", tn=1408, tk=1024)

    lane_pad = lambda v, off: jnp.pad(v.reshape(1, -1), ((0, 0), (off, LANES - off - v.size)))
    p_a, p_dt, p_fb = lane_pad(a_log, 8), lane_pad(dt_bias, 8), lane_pad(f_bias, 16)

    def gates_fwd(col, lcv, lfv, a, dt, fb, pre):
        gates = _gates_elem(a, dt, fb, pre)
        return gates, _cums_fwd(lcv, lfv, gates)

    gates, cums = _tiles(gates_fwd, name="gates", rows=rows, tm=rows,
                         full_consts=[lc, lf, p_a, p_dt, p_fb], row_ins=[(proj, LANES, COL_SMALL)],
                         row_outs=[(LANES, F32), (LANES, F32)])

    def expand_fwd(col, b, g, gt, cm):
        return (_dot32(gt, b, _CONTRACT["nn"]), _dot32(cm, g, _CONTRACT["nn"]))

    betax, gcx = _tiles(expand_fwd, name="expand", rows=rows, tm=tm, full_consts=[xb, xg],
                        row_ins=[(gates, LANES, 0), (cums, LANES, 0)],
                        row_outs=[(WIDTH, F32)] * 2)
    grow = _lanes_to_rowform(cums[:, 8:16], rows)
    frow = cums[:, 16:24].T.reshape(PAIRS, 2, rows)

    (qkv,) = _tiles(_gdn_prep_fwd, name="gdn_prep", rows=rows, tm=rows, ncol=3 * PAIRS,
                    col_consts=[(conv_w, CONV_K, LANES, 0)], row_ins=[(proj, LANES, 0)],
                    row_outs=[(LANES, F32)])
    o_gdn, ssave, tsave = _gdn_forward(qkv, betax, gcx, grow, rows)

    w_qk = jnp.concatenate([jnp.tile(q_norm_w.reshape(1, -1), (1, HEADS)),
                            jnp.tile(k_norm_w.reshape(1, -1), (1, HEADS))], axis=1)
    fox_off = 2048 // LANES
    (fqk,) = _tiles(lambda col, w, xx: (_head_rms(w, xx),), name="fox_prep", rows=rows, tm=rows, ncol=2 * PAIRS,
                    col_consts=[(w_qk, 1, LANES, 0)], row_ins=[(proj, LANES, fox_off)],
                    row_outs=[(LANES, F32)])
    ao, lse = _attention_forward(fqk, proj, frow, rows)

    w_on = jnp.tile(out_norm_w.reshape(1, -1), (1, 2))
    z_off, fg_off = 1536 // LANES, 3584 // LANES
    mix_g_fn = lambda w, o, z: _head_rms(w, o) * _silu(z)
    mix_f_fn = lambda a, g: a * jax.nn.sigmoid(g)
    (mix_g,) = _tiles(lambda col, w, o, z: (mix_g_fn(w, o, z),), name="mix_gdn", rows=rows, tm=rows, ncol=PAIRS,
                      full_consts=[w_on], row_ins=[(o_gdn, LANES, 0), (proj, LANES, z_off)],
                      row_outs=[(LANES, BF16)])
    (mix_f,) = _tiles(lambda col, a, g: (mix_f_fn(a, g),), name="mix_fox", rows=rows, tm=rows, ncol=PAIRS,
                      row_ins=[(ao, LANES, 0), (proj, LANES, fg_off)], row_outs=[(LANES, BF16)])
    mix = jnp.concatenate([mix_g, mix_f], axis=1)
    w_out, w_gate, w_up, w_down = late_weights(mix)
    t_rows, t_half = min(1024, rows), min(512, rows)
    n_rt = rows // t_rows
    row_blk = pl.BlockSpec((t_rows, D_MODEL), lambda i, n: (i, 0))
    half_blk = pl.BlockSpec((t_half, D_MODEL), lambda i, n: (i, 0))
    vec_blk = pl.BlockSpec((1, D_MODEL), lambda i, n: (0, 0))
    wide = lambda dt: jax.ShapeDtypeStruct((rows, D_MODEL), dt)
    x1, h2 = _mm_blocks(mix, w_out, name="out_proj_norm2", grid=(n_rt, 1), dims="nn",
                        a_spec=row_blk, b_spec=pl.BlockSpec((D_MODEL, D_MODEL), lambda i, n: (0, 0)),
                        o_spec=[row_blk, row_blk], out_shape=[wide(F32), wide(BF16)], add=x, add_spec=row_blk,
                        extra=[(norm2_w, vec_blk)], epilogue=lambda r, w: (r, _rms(r, w)))
    st_act = jax.ShapeDtypeStruct((N_CHIPS, rows, FF_SHARD), BF16)
    st_rows = pl.BlockSpec((None, rows, FF_SHARD), lambda i, j: (j, i, 0))

    def ffn_in(w_st, name):
        return _mm_blocks(h2, w_st, name=name, grid=(1, N_CHIPS), dims="nt",
                          a_spec=pl.BlockSpec((rows, D_MODEL), lambda i, j: (i, 0)),
                          b_spec=pl.BlockSpec((None, FF_SHARD, D_MODEL), lambda i, j: (j, 0, 0)),
                          o_spec=st_rows, out_shape=st_act)

    gate = ffn_in(w_gate, "ffn_gate")
    act_fn = lambda g, u: _silu(g) * u
    st_tile = pl.BlockSpec((None, t_rows, FF_SHARD), lambda i, j: (j, i, 0))
    up, act = _mm_blocks(h2, w_up, name="ffn_up_act", grid=(n_rt, N_CHIPS), dims="nt",
                         a_spec=pl.BlockSpec((t_rows, D_MODEL), lambda i, j: (i, 0)),
                         b_spec=pl.BlockSpec((None, FF_SHARD, D_MODEL), lambda i, j: (j, 0, 0)),
                         o_spec=[st_tile, st_tile], out_shape=[st_act, st_act], extra=[(gate, st_tile)],
                         epilogue=lambda u, g: (u, act_fn(g.astype(F32), u)))

    def final_fn(xx, tgt, w):
        y, vjp = jax.vjp(_rms, xx, w)
        err = y - tgt
        loss = 0.5 * jnp.sum(err * err) / D_MODEL
        dx, dw = vjp(err / D_MODEL)
        return dx, dx, jnp.full((1, LANES), loss, F32), dw

    dx2, dx2_b, loss, d_final_w = _mm_blocks(
        act, w_down, name="ffn_down_loss", grid=(rows // t_half, 1), dims="nn", n_sum=N_CHIPS,
        a_spec=pl.BlockSpec((N_CHIPS, t_half, FF_SHARD), lambda i, n: (0, i, 0)),
        b_spec=pl.BlockSpec((N_CHIPS, FF_SHARD, D_MODEL), lambda i, n: (0, 0, 0)),
        o_spec=[half_blk, half_blk, pl.BlockSpec((1, LANES), lambda i, n: (0, 0)), vec_blk],
        out_shape=[wide(F32), wide(BF16), jax.ShapeDtypeStruct((1, LANES), F32),
                   jax.ShapeDtypeStruct((1, D_MODEL), F32)],
        add=x1, add_spec=half_blk, extra=[(target, half_blk), (final_w, vec_blk)], epilogue=final_fn, n_acc=2)

    def act_bwd(d, g, u):
        _, vjp = jax.vjp(act_fn, g.astype(F32), u.astype(F32))
        return vjp(d)

    dgate, dup = _mm_blocks(dx2_b, w_down, name="d_act_gate_up", grid=(n_rt, N_CHIPS), dims="nt",
                            a_spec=pl.BlockSpec((t_rows, D_MODEL), lambda i, j: (i, 0)),
                            b_spec=pl.BlockSpec((None, FF_SHARD, D_MODEL), lambda i, j: (j, 0, 0)),
                            o_spec=[st_tile, st_tile], out_shape=[st_act, st_act],
                            extra=[(gate, st_tile), (up, st_tile)], epilogue=act_bwd)

    def g_ffn(d_st, other, name):
        return _mm_blocks(d_st, other, name=name, grid=(N_CHIPS, 1), dims="tn",
                          a_spec=pl.BlockSpec((None, rows, FF_SHARD), lambda j, n: (j, 0, 0)),
                          b_spec=pl.BlockSpec((rows, D_MODEL), lambda j, n: (0, 0)),
                          o_spec=pl.BlockSpec((None, FF_SHARD, D_MODEL), lambda j, n: (j, 0, 0)),
                          out_shape=jax.ShapeDtypeStruct((N_CHIPS, FF_SHARD, D_MODEL), BF16))

    g_down = g_ffn(act, dx2_b, "g_down")

    def norm_bwd(dh, xx, dres, w):
        _, vjp = jax.vjp(_rms, xx, w)
        dx, dw = vjp(dh)
        return dx + dres, dx + dres, dw

    def d_h2(d_st, w_st, name, add, **fused):
        return _mm_blocks(d_st, w_st, name=name, grid=(rows // t_half, 1), dims="nn", n_sum=N_CHIPS,
                          a_spec=pl.BlockSpec((N_CHIPS, t_half, FF_SHARD), lambda i, n: (0, i, 0)),
                          b_spec=pl.BlockSpec((N_CHIPS, FF_SHARD, D_MODEL), lambda i, n: (0, 0, 0)),
                          add=add, add_spec=half_blk, **fused)

    dh2_gate = d_h2(dgate, w_gate, "d_h2_gate", None, o_spec=half_blk, out_shape=wide(F32))
    dx1, dx1_b, d_norm2_w = d_h2(
        dup, w_up, "d_h2_up_norm2_bwd", dh2_gate, o_spec=[half_blk, half_blk, vec_blk],
        out_shape=[wide(F32), wide(BF16), jax.ShapeDtypeStruct((1, D_MODEL), F32)],
        extra=[(x1, half_blk), (dx2, half_blk), (norm2_w, vec_blk)], epilogue=norm_bwd, n_acc=1)
    g_gate, g_up = g_ffn(dgate, h2, "g_gate"), g_ffn(dup, h2, "g_up")
    dmix = _mm(dx1_b, w_out, dims="nt", name="d_mix", tn=D_MODEL, tk=1024)
    g_out = _mm(mix, dx1_b, dims="tn", name="g_out", tn=D_MODEL, tk=rows, out_dtype=BF16)
    w_on = w_on + early_grads_ready(g_out, g_gate, g_up, g_down)

    def mix_g_bwd(col, w, o, z, d):
        _, vjp = jax.vjp(mix_g_fn, w, o, z)
        dw, do_, dz = vjp(d)
        return do_, dz, dw

    do_gdn, dz, d_on = _tiles(mix_g_bwd, name="mix_gdn_bwd", rows=rows, tm=rows, ncol=PAIRS, full_consts=[w_on],
                              row_ins=[(o_gdn, LANES, 0), (proj, LANES, z_off), (dmix, LANES, 0)],
                              row_outs=[(LANES, F32), (LANES, BF16)], acc_outs=[(1, LANES)])

    def mix_f_bwd(col, a, g, d):
        _, vjp = jax.vjp(mix_f_fn, a, g)
        return vjp(d)

    dao, dfgate = _tiles(mix_f_bwd, name="mix_fox_bwd", rows=rows, tm=rows, ncol=PAIRS,
                         row_ins=[(ao, LANES, 0), (proj, LANES, fg_off), (dmix, LANES, PAIRS)],
                         row_outs=[(LANES, F32), (LANES, BF16)])

    dfq, dfk, dfv, dfrow = _attention_backward(fqk, proj, frow + early_grads_continue(dao), ao, lse, dao, rows)

    def fox_prep_bwd(col, w, xx, d):
        _, vjp = jax.vjp(_head_rms, w, xx)
        dw, dx = vjp(d)
        return dx, dw

    dfqk, d_wqk = [], []
    for part, d_n in enumerate((dfq, dfk)):
        dx_p, dw_p = _tiles(fox_prep_bwd, name="fox_prep_bwd_" + "qk"[part], rows=rows, tm=rows, ncol=PAIRS,
                            col_consts=[(w_qk, 1, LANES, part * PAIRS)],
                            row_ins=[(proj, LANES, fox_off + part * PAIRS), (d_n, LANES, 0)],
                            row_outs=[(LANES, BF16)], acc_outs=[(1, LANES)])
        dfqk.append(dx_p)
        d_wqk.append(dw_p)

    dq, dk, dv, dbetax, dgcx, dgrow = _gdn_backward(qkv, betax, gcx, grow, ssave, tsave, do_gdn, rows)
    dqkv, d_conv = [], []
    for part, d_n in enumerate((dq, dk, dv)):
        prep_bwd = lambda col, cw, xx, dy, is_qk=(part < 2): _gdn_prep_bwd(is_qk, cw, xx, dy)
        dx_p, dw_p = _tiles(prep_bwd, name="gdn_prep_bwd_" + "qkv"[part], rows=rows, tm=rows, ncol=PAIRS,
                            col_consts=[(conv_w, CONV_K, LANES, part * PAIRS)],
                            row_ins=[(proj, LANES, part * PAIRS), (d_n, LANES, 0)],
                            row_outs=[(LANES, BF16)], acc_outs=[(CONV_K, LANES)])
        dqkv.append(dx_p)
        d_conv.append(dw_p)
    d_conv = jnp.concatenate(d_conv, axis=1)

    def expand_bwd(col, b, g, db, dg):
        return (_dot32(db, b, _CONTRACT["nt"]), _dot32(dg, g, _CONTRACT["nt"]))

    dgates_b, dcums_g = _tiles(expand_bwd, name="expand_bwd", rows=rows, tm=tm, full_consts=[xb, xg],
                               row_ins=[(dbetax, WIDTH, 0), (dgcx, WIDTH, 0)],
                               row_outs=[(LANES, F32), (LANES, F32)])
    dcums_row = jnp.concatenate([jnp.zeros((rows, 8), F32), _rowform_to_lanes(dgrow, rows),
                                 dfrow.reshape(HEADS, rows).T, jnp.zeros((rows, LANES - 24), F32)], axis=1)

    def gates_bwd(col, lcv, lfv, a, dt, fb, pre, dgb, dcg, dcr):
        lane = _lane_ids(pre.shape)
        dgates = jnp.where(lane < 8, dgb, _cums_bwd(lcv, lfv, dcg + dcr))
        _, vjp = jax.vjp(_gates_elem, a, dt, fb, pre)
        da, ddt, dfb, dpre = vjp(dgates)
        return dpre, da, ddt, dfb

    dpre, d_a, d_dt, d_fb = _tiles(gates_bwd, name="gates_bwd", rows=rows, tm=rows,
                                   full_consts=[lc, lf, p_a, p_dt, p_fb],
                                   row_ins=[(proj, LANES, COL_SMALL), (dgates_b, LANES, 0), (dcums_g, LANES, 0),
                                            (dcums_row, LANES, 0)],
                                   row_outs=[(LANES, BF16)], acc_outs=[(1, LANES)] * 3)

    dproj = jnp.concatenate(dqkv + [dz] + dfqk + [dfv, dfgate, dpre], axis=1)
    grad_x, d_norm1_w = _mm_blocks(
        dproj, w_cat, name="d_h1_norm1_bwd", grid=(rows // t_half, 1), dims="nn",
        a_spec=pl.BlockSpec((t_half, D_CAT), lambda i, n: (i, 0)),
        b_spec=pl.BlockSpec((D_CAT, D_MODEL), lambda i, n: (0, 0)),
        o_spec=[half_blk, vec_blk], out_shape=[wide(F32), jax.ShapeDtypeStruct((1, D_MODEL), F32)],
        extra=[(x, half_blk), (dx1, half_blk), (norm1_w, vec_blk)],
        epilogue=lambda dh, xx, dres, w: norm_bwd(dh, xx, dres, w)[1:], n_acc=1)
    g_cat = _mm(dproj, h1, dims="tn", name="g_in", tm=1408, tn=D_MODEL, tk=rows)

    fold = lambda v: v.reshape(-1, HEAD_DIM).sum(axis=0)
    small = dict(
        loss=loss[0, 0],
        norm1_w=d_norm1_w, conv_w=d_conv, a_log=d_a[0, 8:16], dt_bias=d_dt[0, 8:16],
        out_norm_w=fold(d_on), f_bias=d_fb[0, 16:24], q_norm_w=fold(d_wqk[0]),
        k_norm_w=fold(d_wqk[1]), norm2_w=d_norm2_w, final_w=d_final_w)
    return grad_x, g_cat, g_out, g_gate, g_up, g_down, small


HBM_SPEC = pl.BlockSpec(memory_space=pltpu.HBM)


def _place():
    x, y, c = lax.axis_index("x"), lax.axis_index("y"), lax.axis_index("c")
    chips = [(1 - x, y), (x, 1 - y), (1 - x, 1 - y)]
    return x, y, c, 2 * x + y, (x, y, 1 - c), chips, [2 * cx + cy for cx, cy in chips]


def _remote(src, dst, send_sem, recv_sem, to):
    return pltpu.make_async_remote_copy(src_ref=src, dst_ref=dst, send_sem=send_sem, recv_sem=recv_sem,
                                        device_id=to, device_id_type=MESH)


SEM_SPEC =pl.BlockSpec(memory_space=pltpu.SEMAPHORE)
ANY_SPEC = pl.BlockSpec(memory_space=pl.ANY)
DATAFLOW = pltpu.SideEffectType.DATAFLOW_SIDE_EFFECTING


def _gather_plan(srcs, lands):
    x, y, c, own, sib, chips, chip_idx = _place()
    plan = []
    for src, land in zip(srcs, lands):
        for j, chip in enumerate(chips):
            plan.append((src, land.at[own], (*chip, c), land.at[chip_idx[j]]))
        plan.append((src, land.at[own], sib, land.at[own]))
    return plan


def _exchange_plan(srcs, lands):
    x, y, c, own, sib, chips, chip_idx = _place()
    plan = []
    for src, land in zip(srcs, lands):
        for j, chip in enumerate(chips):
            plan.append((src.at[chip_idx[j]], land.at[j], (*chip, c), land.at[j]))
    return plan


def _swap_plan(srcs, lands):
    x, y, c, own, sib, chips, chip_idx = _place()
    plan = []
    for src, land in zip(srcs, lands):
        h = src.shape[2] // 2
        plan.append((src.at[:, :, pl.ds(pl.multiple_of((1 - c) * h, LANES), h)], land, sib, land))
    return plan


def _in_proj_plan(srcs, lands):
    x, y, c, own, sib, chips, chip_idx = _place()
    (w, conv), (w_land, conv_land) = srcs, lands
    hw = w.shape[1] // 2
    half = lambda ref: ref.at[:, pl.ds(pl.multiple_of(c * hw, LANES), hw)]
    plan = []
    for j, chip in enumerate(chips):
        plan.append((half(w), half(w_land.at[own]), (*chip, c), half(w_land.at[chip_idx[j]])))
        plan.append((conv, conv_land.at[own], (*chip, c), conv_land.at[chip_idx[j]]))
    plan.append((w, w_land.at[own], sib, w_land.at[own]))
    plan.append((conv, conv_land.at[own], sib, conv_land.at[own]))
    return plan


def _forward_halves(landed):
    hw = landed.shape[2] // 2

    def body(in_ref, out_ref, send_sems, recv_sems):
        x, y, c, own, sib, chips, chip_idx = _place()
        half = lambda ref, hc: ref.at[:, pl.ds(pl.multiple_of(hc * hw, LANES), hw)]
        sent = [_remote(half(out_ref.at[chip_idx[j]], c), half(out_ref.at[chip_idx[j]], c),
                        send_sems.at[j], recv_sems.at[j], sib) for j in range(3)]
        for cp in sent:
            cp.start()
        for j in range(3):
            other = half(out_ref.at[chip_idx[j]], 1 - c)
            _remote(other, other, send_sems.at[j], recv_sems.at[j], sib).wait_recv()
        for cp in sent:
            cp.wait_send()

    return pl.pallas_call(
        body, name="gather_in_forward", out_shape=jax.ShapeDtypeStruct(landed.shape, landed.dtype),
        in_specs=[HBM_SPEC], out_specs=HBM_SPEC, input_output_aliases={0: 0},
        scratch_shapes=[pltpu.SemaphoreType.DMA((3,)), pltpu.SemaphoreType.DMA((3,))],
    )(landed)


def _split_start(name, plan_fn, srcs, land_shapes, n_copies, after):
    n = len(srcs)

    def body(*refs):
        src_refs, land_refs = refs[:n], refs[n:2 * n]
        send_sems, recv_sems = refs[2 * n + 1], refs[2 * n + 2]
        token = refs[-1]
        for k, (src, dst, to, _) in enumerate(plan_fn(src_refs, land_refs)):
            _remote(src, dst, send_sems.at[k], recv_sems.at[k], to).start()
        token[...] = jnp.zeros_like(token)

    lands = [pltpu.with_memory_space_constraint(lax.empty(s.shape, s.dtype), pltpu.HBM) for s in land_shapes]
    srcs = [pltpu.with_memory_space_constraint(s, pltpu.HBM) for s in srcs]
    out_shape = ([pltpu.SemaphoreType.DMA((n_copies,)), pltpu.SemaphoreType.DMA((n_copies,))]
                 + [pltpu.HBM(s.shape, s.dtype) for s in srcs] + [pltpu.HBM(s.shape, s.dtype) for s in land_shapes]
                 + [jax.ShapeDtypeStruct((8, LANES), F32)])
    res = pl.pallas_call(
        body, name=name, out_shape=out_shape,
        in_specs=[HBM_SPEC] * (2 * n) + [ANY_SPEC],
        out_specs=[SEM_SPEC, SEM_SPEC] + [HBM_SPEC] * (2 * n) + [pl.BlockSpec(memory_space=pltpu.VMEM)],
        input_output_aliases={i: 2 + i for i in range(2 * n)},
        compiler_params=pltpu.CompilerParams(has_side_effects=DATAFLOW),
    )(*srcs, *lands, after)
    return dict(sems=res[:2], srcs=res[2:2 + n], lands=res[2 + n:2 + 2 * n], token=res[-1], n=n)


def _split_wait(name, plan_fn, started, after):
    n = started["n"]

    def body(*refs):
        src_refs, land_refs = refs[:n], refs[n:2 * n]
        send_sems, recv_sems = refs[2 * n], refs[2 * n + 1]
        for k, (src, _, to, landed) in enumerate(plan_fn(src_refs, land_refs)):
            copy = _remote(src, landed, send_sems.at[k], recv_sems.at[k], to)
            copy.wait_send()
            copy.wait_recv()

    srcs, lands = started["srcs"], started["lands"]
    after = list(after) if isinstance(after, (list, tuple)) else [after]
    res = pl.pallas_call(
        body, name=name,
        out_shape=[pltpu.HBM(s.shape, s.dtype) for s in srcs] + [pltpu.HBM(s.shape, s.dtype) for s in lands],
        in_specs=[HBM_SPEC] * (2 * n) + [SEM_SPEC, SEM_SPEC] + [ANY_SPEC] * len(after),
        out_specs=[HBM_SPEC] * (2 * n),
        input_output_aliases={i: i for i in range(2 * n)},
        compiler_params=pltpu.CompilerParams(has_side_effects=DATAFLOW),
    )(*srcs, *lands, *started["sems"], *after)
    started["srcs_after"] = res[:n]
    return res[n:]


def _add_half(stack, landed, place, name):
    _, rows, h = landed.shape

    def body(place_ref, a_ref, b_ref, o_ref, own_ref):
        part = (a_ref[...].astype(F32) + b_ref[...].astype(F32)).astype(o_ref.dtype)
        o_ref[...] = part

        @pl.when(pl.program_id(0) == place_ref[1])
        def _():
            own_ref[...] = part[0]

    return pl.pallas_call(
        body, name=name,
        out_shape=[jax.ShapeDtypeStruct(landed.shape, BF16), jax.ShapeDtypeStruct((rows, h), BF16)],
        grid_spec=pltpu.PrefetchScalarGridSpec(
            num_scalar_prefetch=1, grid=(N_CHIPS,),
            in_specs=[pl.BlockSpec((1, rows, h), lambda j, p: (j, 0, p[0])),
                      pl.BlockSpec((1, rows, h), lambda j, p: (j, 0, 0))],
            out_specs=[pl.BlockSpec((1, rows, h), lambda j, p: (j, 0, 0)),
                       pl.BlockSpec((rows, h), lambda j, p: (0, 0))]),
        compiler_params=_params(("arbitrary",)),
    )(place, stack, landed)


def _sum_partials(own_part, landed, name, untiled_rows=False):
    _, h, cols = landed.shape
    tc = LANES if untiled_rows else cols

    def body(own_ref, a_ref, o_ref):
        acc = own_ref[...].astype(F32)
        for s in range(3):
            acc = acc + a_ref[s].astype(F32)
        if untiled_rows:
            o_ref[:, 0, :] = acc
        else:
            o_ref[...] = acc

    if untiled_rows:
        out_shape, out_spec = jax.ShapeDtypeStruct((h, 1, cols), F32), pl.BlockSpec((h, 1, tc), lambda i: (0, 0, i))
    else:
        out_shape, out_spec = jax.ShapeDtypeStruct((h, cols), F32), pl.BlockSpec((h, tc), lambda i: (0, i))
    return pl.pallas_call(
        body, name=name, out_shape=out_shape, grid=(cols // tc,),
        in_specs=[pl.BlockSpec((h, tc), lambda i: (0, i)), pl.BlockSpec((3, h, tc), lambda i: (0, 0, i))],
        out_specs=out_spec, compiler_params=_params(("arbitrary",)),
    )(own_part, landed)


def _share_halves(halves, name):
    n = len(halves)

    def body(*refs):
        ins, outs = refs[:n], refs[n:2 * n]
        send_sems, recv_sems = refs[2 * n:]
        x, y, c, own, sib, chips, chip_idx = _place()
        cps = [_remote(ins[i], outs[i], send_sems.at[i], recv_sems.at[i], sib) for i in range(n)]
        for cp in cps:
            cp.start()
        for cp in cps:
            cp.wait()

    return pl.pallas_call(
        body, name=name,
        out_shape=[jax.ShapeDtypeStruct(p.shape, p.dtype) for p in halves],
        in_specs=[HBM_SPEC] * n, out_specs=[HBM_SPEC] * n,
        scratch_shapes=[pltpu.SemaphoreType.DMA((n,)), pltpu.SemaphoreType.DMA((n,))],
    )(*halves)


def _allreduce_small(packed):
    rows = packed.shape[0]
    n_dev = 8

    def body(in_ref, out_ref, gath, send_sems, recv_sems):
        x, y, c = lax.axis_index("x"), lax.axis_index("y"), lax.axis_index("c")
        me = 4 * x + 2 * y + c
        gath[me] = in_ref[...]
        cps = []
        for k in range(1, n_dev):
            fx, fy, fc = (k >> 2) & 1, (k >> 1) & 1, k & 1
            to = (x ^ fx, y ^ fy, c ^ fc)
            cps.append(_remote(in_ref, gath.at[me], send_sems.at[k - 1], recv_sems.at[k - 1], to))
        for cp in cps:
            cp.start()
        for k in range(1, n_dev):
            fx, fy, fc = (k >> 2) & 1, (k >> 1) & 1, k & 1
            src = 4 * (x ^ fx) + 2 * (y ^ fy) + (c ^ fc)
            slot = gath.at[src]
            _remote(slot, slot, send_sems.at[k - 1], recv_sems.at[k - 1], (x, y, c)).wait_recv()
        for cp in cps:
            cp.wait_send()
        acc = gath[0]
        for d in range(1, n_dev):
            acc = acc + gath[d]
        out_ref[...] = acc

    vm = pl.BlockSpec(memory_space=pltpu.VMEM)
    return pl.pallas_call(
        body, name="allreduce_small", out_shape=jax.ShapeDtypeStruct(packed.shape, F32),
        in_specs=[vm], out_specs=vm,
        scratch_shapes=[pltpu.VMEM((n_dev, rows, LANES), F32),
                        pltpu.SemaphoreType.DMA((n_dev - 1,)), pltpu.SemaphoreType.DMA((n_dev - 1,))],
    )(packed)


def _adam(col, w, g, m, v):
    m2 = ADAM_B1 * m + (1.0 - ADAM_B1) * g
    v2 = ADAM_B2 * v + (1.0 - ADAM_B2) * (g * g)
    m_hat = m2 / (1.0 - ADAM_B1 ** ADAM_STEP)
    v_hat = v2 / (1.0 - ADAM_B2 ** ADAM_STEP)
    delta = -ADAM_LR * (m_hat / (jnp.sqrt(v_hat) + ADAM_EPS) + ADAM_WD * w)
    return delta, m2, v2


def _adam_call(w, g, m, v, name):
    rows, cols = w.shape
    tm = rows
    for cand in (256, 352, 176, 128, 64, 48, 16, 8):
        if rows % cand == 0:
            tm = cand
            break
    return _tiles(_adam, name=name, rows=rows, tm=tm,
                  row_ins=[(w, cols, 0), (g, cols, 0), (m, cols, 0), (v, cols, 0)],
                  row_outs=[(cols, F32)] * 3)


def _adam_big(w, g_mine, g_other, m, v, place, name):
    rows, cols = w.shape
    tc = 256
    nt = cols // 2 // tc

    def body(place_ref, w_ref, gm_ref, go_ref, m_ref, v_ref, g_out, d_out, m_out, v_out):
        g = jnp.where(pl.program_id(0) == place_ref[0], gm_ref[...], go_ref[...])
        d, m2, v2 = _adam(None, w_ref[...], g, m_ref[...], v_ref[...])
        g_out[...] = g
        d_out[...] = d
        m_out[...] = m2
        v_out[...] = v2

    full = pl.BlockSpec((rows, tc), lambda hh, i, p: (0, hh * nt + i))
    half = pl.BlockSpec((rows, tc), lambda hh, i, p: (0, i))
    return pl.pallas_call(
        body, name=name, out_shape=[jax.ShapeDtypeStruct(w.shape, F32)] * 4,
        grid_spec=pltpu.PrefetchScalarGridSpec(
            num_scalar_prefetch=1, grid=(2, nt),
            in_specs=[full, half, half, full, full], out_specs=[full] * 4),
        compiler_params=_params(("arbitrary", "arbitrary")),
    )(place, w, g_mine, g_other, m, v)


def _adam_untiled_rows(w, g_mine, g_other, m, v, place, name):
    rows, _, cols = w.shape
    tc = 256
    nt = cols // 2 // tc
    rb = next(r for r in (206, 128, 103, rows) if rows % r == 0)

    def body(place_ref, w_ref, gm_ref, go_ref, m_ref, v_ref, g_out, d_out, m_out, v_out):
        g = jnp.where(pl.program_id(0) == place_ref[0], gm_ref[...], go_ref[...])
        d, m2, v2 = _adam(None, w_ref[...], g, m_ref[...], v_ref[...])
        g_out[...] = g
        d_out[...] = d
        m_out[...] = m2
        v_out[...] = v2

    full = pl.BlockSpec((rb, 1, tc), lambda hh, i, r, p: (r, 0, hh * nt + i))
    half = pl.BlockSpec((rb, 1, tc), lambda hh, i, r, p: (r, 0, i))
    return pl.pallas_call(
        body, name=name, out_shape=[jax.ShapeDtypeStruct(w.shape, F32)] * 4,
        grid_spec=pltpu.PrefetchScalarGridSpec(
            num_scalar_prefetch=1, grid=(2, nt, rows // rb),
            in_specs=[full, half, half, full, full], out_specs=[full] * 4),
        compiler_params=_params(("arbitrary", "arbitrary", "arbitrary")),
    )(place, w, g_mine, g_other, m, v)


def _pack(arrays, zero=None):
    flat = []
    for a in arrays:
        a = a.reshape(-1).astype(F32)
        if zero is not None:
            a = a + zero
        flat.append(jnp.pad(a, (0, (-a.size) % LANES)))
    out = jnp.concatenate(flat)
    out = jnp.pad(out, (0, (-out.size) % (8 * LANES)))
    return out.reshape(-1, LANES)


def _unpack(packed, shapes):
    flat = packed.reshape(-1)
    out, off = [], 0
    for s in shapes:
        size = int(np.prod(s))
        out.append(flat[off:off + size].reshape(s))
        off += size + (-size) % LANES
    return out


def kernel(x, norm1_w, w_in, gdn_conv_w, gdn_A_log, gdn_dt_bias, gdn_out_norm_w, fox_f_bias, fox_q_norm_w, fox_k_norm_w, w_out, norm2_w, w_ffn_gate, w_ffn_up, w_ffn_down, final_norm_w, loss_target, m_norm1_w, m_w_in, m_gdn_conv_w, m_gdn_A_log, m_gdn_dt_bias, m_gdn_out_norm_w, m_fox_f_bias, m_fox_q_norm_w, m_fox_k_norm_w, m_w_out, m_norm2_w, m_w_ffn_gate, m_w_ffn_up, m_w_ffn_down, m_final_norm_w, v_norm1_w, v_w_in, v_gdn_conv_w, v_gdn_A_log, v_gdn_dt_bias, v_gdn_out_norm_w, v_fox_f_bias, v_fox_q_norm_w, v_fox_k_norm_w, v_w_out, v_norm2_w, v_w_ffn_gate, v_w_ffn_up, v_w_ffn_down, v_final_norm_w):
    cx, cy, cc = lax.axis_index("x"), lax.axis_index("y"), lax.axis_index("c")
    own = 2 * cx + cy
    place = jnp.stack([cc, own]).astype(jnp.int32)

    names = ["w_in", "w_out", "w_gate", "w_up", "w_down"]
    is_t = [True, False, True, True, False]
    to_t = lambda a, t: a[0].T if t else a[0]
    from_t = lambda a, t: (a.T if t else a)[None]
    big_w = [to_t(a, t) for a, t in zip([w_in, w_out, w_ffn_gate, w_ffn_up, w_ffn_down], is_t)]
    big_m = [to_t(a, t) for a, t in zip([m_w_in, m_w_out, m_w_ffn_gate, m_w_ffn_up, m_w_ffn_down], is_t)]
    big_v = [to_t(a, t) for a, t in zip([v_w_in, v_w_out, v_w_ffn_gate, v_w_ffn_up, v_w_ffn_down], is_t)]
    shards = [big_w[0].astype(BF16)]
    small_w = [norm1_w, gdn_conv_w, gdn_A_log, gdn_dt_bias, gdn_out_norm_w, fox_f_bias, fox_q_norm_w,
               fox_k_norm_w, norm2_w, final_norm_w]
    small_m = [m_norm1_w, m_gdn_conv_w, m_gdn_A_log, m_gdn_dt_bias, m_gdn_out_norm_w, m_fox_f_bias,
               m_fox_q_norm_w, m_fox_k_norm_w, m_norm2_w, m_final_norm_w]
    small_v = [v_norm1_w, v_gdn_conv_w, v_gdn_A_log, v_gdn_dt_bias, v_gdn_out_norm_w, v_fox_f_bias,
               v_fox_q_norm_w, v_fox_k_norm_w, v_norm2_w, v_final_norm_w]
    first = _split_start("gather_in_start", _in_proj_plan, [shards[0], gdn_conv_w[0]],
                         [jax.ShapeDtypeStruct((N_CHIPS,) + shards[0].shape, BF16),
                          jax.ShapeDtypeStruct((N_CHIPS, CONV_K, 3 * WIDTH // N_CHIPS), F32)],
                         n_copies=8, after=shards[0])
    small_packed = [_pack(p, first["token"][0, 0]) for p in (small_w, small_m, small_v)]
    shards += [(w + first["token"][0, 0]).astype(BF16) for w in big_w[1:]]
    rest = {}

    def first_weights(after):
        w_in_g, conv_g = _split_wait("gather_in_wait", _in_proj_plan, first, [after] + small_packed)
        w_in_g = _forward_halves(w_in_g)
        rest.update(_split_start("gather_rest_start", _gather_plan, shards[1:],
                                 [jax.ShapeDtypeStruct((N_CHIPS,) + s.shape, BF16) for s in shards[1:]],
                                 n_copies=4 * len(shards[1:]), after=w_in_g))
        w_cat = _cat_weights(w_in_g.reshape(D_IN, D_MODEL))
        return w_cat + rest["token"][0, 0].astype(BF16), conv_g.transpose(1, 0, 2).reshape(CONV_K, 3 * WIDTH)

    def late_weights(after):
        w_out_g, w_gate_g, w_up_g, w_down_g = _split_wait("gather_rest_wait", _gather_plan, rest, after)
        return w_out_g.reshape(D_MODEL, D_MODEL), w_gate_g, w_up_g, w_down_g

    def start_reduction(stacks, landed, nms, tag):
        added = [_add_half(s, l, place, "rs_add_" + nm) for s, l, nm in zip(stacks, landed, nms)]
        parts = [a[0] for a in added]
        started = _split_start("exchange_" + tag + "_start", _exchange_plan, parts,
                               [jax.ShapeDtypeStruct((3,) + p.shape[1:], p.dtype) for p in parts],
                               n_copies=3 * len(parts), after=parts[0])
        return dict(own=[a[1] for a in added], started=started, tag=tag, names=nms)

    def finish_reduction(red, after, updates):
        landed = _split_wait("exchange_" + red["tag"] + "_wait", _exchange_plan, red["started"], after)
        halves = [_sum_partials(o, p, "rs_sum_" + nm, untiled_rows=nm == "w_in")
                  for o, p, nm in zip(red["own"], landed, red["names"])]
        others = _share_halves(halves, "rs_share_" + red["tag"])
        return [upd(gm, go) for upd, gm, go in zip(updates, halves, others)]

    def transport_update(b):
        def upd(gm, go):
            res = _adam_big(big_w[b], gm, go, big_m[b], big_v[b], place, "adam_" + names[b])
            early_done.append(res[1])
            return [from_t(a, is_t[b]) for a in res]
        return upd

    early_done = []

    def w_in_update(gm, go):
        rows3 = lambda a: jnp.transpose(a, (2, 0, 1))
        res = _adam_untiled_rows(rows3(w_in), gm, go, rows3(m_w_in), rows3(v_w_in), place, "adam_w_in")
        return [jnp.transpose(a, (1, 2, 0)) for a in res]

    early = {}

    def early_grads_ready(g_out, g_gate, g_up, g_down):
        stacks = [g_out.reshape(N_CHIPS, D_MODEL // N_CHIPS, D_MODEL), g_gate, g_up, g_down]
        swap = _split_start("swap_early_start", _swap_plan, stacks,
                            [jax.ShapeDtypeStruct(s.shape[:2] + (s.shape[2] // 2,), s.dtype) for s in stacks],
                            n_copies=len(stacks), after=stacks[0])
        early.update(stacks=stacks, swap=swap)
        return swap["token"][0, 0]

    def early_grads_continue(after):
        landed = _split_wait("swap_early_wait", _swap_plan, early["swap"], after)
        early.update(start_reduction(early["swap"]["srcs_after"], landed, names[1:], "early"))
        return early["started"]["token"][0, 0]

    grad_x, g_cat, _, _, _, _, small = _local_step(
        x[0], loss_target[0], norm1_w + first["token"][0, 0], gdn_A_log[0], gdn_dt_bias[0],
        gdn_out_norm_w[0], fox_f_bias[0], fox_q_norm_w[0], fox_k_norm_w[0], norm2_w, final_norm_w.reshape(1, -1),
        first_weights, late_weights, early_grads_ready, early_grads_continue)

    g_in_stack = _uncat_grad(g_cat).reshape(N_CHIPS, D_IN // N_CHIPS, D_MODEL)
    swap_in = _split_start("swap_w_in_start", _swap_plan, [g_in_stack],
                           [jax.ShapeDtypeStruct((N_CHIPS, D_IN // N_CHIPS, D_MODEL // 2), F32)],
                           n_copies=1, after=g_in_stack)

    order = ["norm1_w", "conv_w", "a_log", "dt_bias", "out_norm_w", "f_bias", "q_norm_w", "k_norm_w",
             "norm2_w", "final_w"]
    red = _allreduce_small(_pack([small[k] for k in order] + [small["loss"]], swap_in["token"][0, 0]))
    red_shapes = [(1, D_MODEL), (CONV_K, 3 * WIDTH), (1, HEADS), (1, HEADS), (1, HEAD_DIM), (1, HEADS),
                  (1, HEAD_DIM), (1, HEAD_DIM), (1, D_MODEL), (D_MODEL,), ()]
    red_list = _unpack(red, red_shapes)
    loss = red_list[-1]
    small_g = dict(zip(order, red_list[:-1]))
    shard_cols = 3 * WIDTH // N_CHIPS
    small_g["conv_w"] = lax.dynamic_slice_in_dim(small_g["conv_w"], own * shard_cols, shard_cols, axis=1)[None]
    small_gl = [small_g[k].reshape(w.shape) for k, w in zip(order, small_w)]
    s_delta, s_m, s_v = _adam_call(small_packed[0], _pack(small_gl), small_packed[1], small_packed[2], "adam_small")
    landed_in = _split_wait("swap_w_in_wait", _swap_plan, swap_in, s_delta)
    late = start_reduction(swap_in["srcs_after"], landed_in, names[:1], "w_in")
    big_upd = finish_reduction(early, late["started"]["token"], [transport_update(b) for b in range(1, 5)])
    big_upd = finish_reduction(late, early_done, [w_in_update]) + big_upd
    shapes = [w.shape for w in small_w]
    s_delta, s_m, s_v = _unpack(s_delta, shapes), _unpack(s_m, shapes), _unpack(s_v, shapes)

    big_pos = {1: 0, 9: 1, 11: 2, 12: 3, 13: 4}
    small_pos = {0: 0, 2: 1, 3: 2, 4: 3, 5: 4, 6: 5, 7: 6, 8: 7, 10: 8, 14: 9}
    grads, deltas, new_m, new_v = [], [], [], []
    for pos in range(15):
        if pos in big_pos:
            b = big_pos[pos]
            g, d, m2, v2 = big_upd[b]
            grads.append(g)
            deltas.append(d)
            new_m.append(m2)
            new_v.append(v2)
        else:
            s = small_pos[pos]
            grads.append(small_gl[s])
            deltas.append(s_delta[s])
            new_m.append(s_m[s])
            new_v.append(s_v[s])
    return (loss, grad_x[None], *grads, *deltas, *new_m, *new_v)
```

```python
import jax
import jax.numpy as jnp
import numpy as np
from jax import lax
from jax.experimental import pallas as pl
from jax.experimental.pallas import tpu as pltpu

F32 = jnp.float32
BF16 = jnp.bfloat16

D_MODEL = 1024
HEADS = 8
HEAD_DIM = 64
PAIRS = HEADS // 2
WIDTH = HEADS * HEAD_DIM
CHUNK = 64
CONV_K = 4
D_FF = 2816
FF_SHARD = D_FF // 4
EPS = 1e-6
SCALE = HEAD_DIM ** -0.5
LANES = 128
N_CHIPS = 4
D_IN = 4120
D_CAT = 4224
COL_SMALL = 4096 // LANES

ADAM_LR = 0.001
ADAM_B1 = 0.9
ADAM_B2 = 0.999
ADAM_EPS = 1e-08
ADAM_WD = 0.01
ADAM_STEP = 10

VMEM_LIMIT = 56 * 1024 * 1024
MESH = pl.DeviceIdType.MESH
HIGHEST = lax.Precision.HIGHEST


def _params(sem):
    return pltpu.CompilerParams(dimension_semantics=sem, vmem_limit_bytes=VMEM_LIMIT)


_CONTRACT = {"nn": ((1,), (0,)), "nt": ((1,), (1,)), "tn": ((0,), (0,))}


def _mm(a, b, *, dims, name, out_dtype=F32, add=None, tm=1024, tn=512, tk=512):
    if dims == "nn":
        (m, k), (k2, n) = a.shape, b.shape
    elif dims == "nt":
        (m, k), (n, k2) = a.shape, b.shape
    else:
        (k, m), (k2, n) = a.shape, b.shape
    assert k == k2, (a.shape, b.shape, dims)
    tm, tn, tk = min(tm, m), min(tn, n), min(tk, k)
    assert m % tm == 0 and n % tn == 0 and k % tk == 0, (m, n, k, tm, tn, tk)
    nk = k // tk
    a_spec = (pl.BlockSpec((tk, tm), lambda i, j, kk: (kk, i)) if dims == "tn"
              else pl.BlockSpec((tm, tk), lambda i, j, kk: (i, kk)))
    b_spec = (pl.BlockSpec((tn, tk), lambda i, j, kk: (j, kk)) if dims == "nt"
              else pl.BlockSpec((tk, tn), lambda i, j, kk: (kk, j)))
    o_spec = pl.BlockSpec((tm, tn), lambda i, j, kk: (i, j))
    contract = (_CONTRACT[dims], ((), ()))
    has_add = add is not None

    def body(*refs):
        a_ref, b_ref = refs[:2]
        add_ref = refs[2] if has_add else None
        o_ref = refs[3] if has_add else refs[2]
        part = lax.dot_general(a_ref[...].astype(BF16), b_ref[...].astype(BF16), contract,
                               preferred_element_type=F32)

        def finish(r):
            if has_add:
                r = r + add_ref[...].astype(F32)
            o_ref[...] = r.astype(out_dtype)

        if nk == 1:
            finish(part)
            return
        acc = refs[-1]
        kk = pl.program_id(2)

        @pl.when(kk == 0)
        def _():
            acc[...] = part

        @pl.when(kk > 0)
        def _():
            acc[...] += part

        @pl.when(kk == nk - 1)
        def _():
            finish(acc[...])

    ins = [a, b] + ([add] if has_add else [])
    in_specs = [a_spec, b_spec] + ([o_spec] if has_add else [])
    return pl.pallas_call(
        body, name=name, grid=(m // tm, n // tn, nk),
        in_specs=in_specs, out_specs=o_spec,
        out_shape=jax.ShapeDtypeStruct((m, n), out_dtype),
        scratch_shapes=[pltpu.VMEM((tm, tn), F32)] if nk > 1 else [],
        compiler_params=_params(("parallel", "parallel", "arbitrary")),
    )(*ins)


def _mm_blocks(a, b, *, name, grid, a_spec, b_spec, o_spec, out_shape, dims, n_sum=0, add=None, add_spec=None,
               epilogue=None, extra=(), n_acc=0):
    contract = (_CONTRACT[dims], ((), ()))
    has_add = add is not None
    n_in = 2 + has_add + len(extra)

    def body(*refs):
        a_ref, b_ref = refs[:2]
        dot = lambda x, y: lax.dot_general(x.astype(BF16), y.astype(BF16), contract, preferred_element_type=F32)
        if n_sum:
            r = dot(a_ref[0], b_ref[0])
            for s in range(1, n_sum):
                r = r + dot(a_ref[s], b_ref[s])
        else:
            r = dot(a_ref[...], b_ref[...])
        if has_add:
            r = r + refs[2][...].astype(F32)
        if epilogue is None:
            refs[-1][...] = r.astype(refs[-1].dtype)
        else:
            outs = epilogue(r, *[e[...] for e in refs[2 + has_add:n_in]])
            out_refs = refs[n_in:]
            n_plain = len(out_refs) - n_acc
            for o_ref, val in zip(out_refs[:n_plain], outs):
                o_ref[...] = val.astype(o_ref.dtype)
            if n_acc:
                @pl.when(pl.program_id(0) == 0)
                def _():
                    for o_ref in out_refs[n_plain:]:
                        o_ref[...] = jnp.zeros_like(o_ref)
                for o_ref, val in zip(out_refs[n_plain:], outs[n_plain:]):
                    o_ref[...] += val

    ins = [a, b] + ([add] if has_add else []) + [e[0] for e in extra]
    in_specs = [a_spec, b_spec] + ([add_spec] if has_add else []) + [e[1] for e in extra]
    sem = ("arbitrary" if n_acc else "parallel",) * len(grid)
    return pl.pallas_call(
        body, name=name, grid=grid, in_specs=in_specs, out_specs=o_spec, out_shape=out_shape,
        compiler_params=_params(sem),
    )(*ins)


def _tiles(fn, *, name, rows, tm, ncol=1, row_ins=(), col_consts=(), full_consts=(),
           row_outs=(), acc_outs=()):
    nt = rows // tm
    assert rows % tm == 0
    n_full, n_col, n_row = len(full_consts), len(col_consts), len(row_ins)
    n_ro, n_acc = len(row_outs), len(acc_outs)

    def body(*refs):
        ins = refs[:n_full + n_col + n_row]
        outs = refs[n_full + n_col + n_row:]
        i = pl.program_id(1)
        res = fn(pl.program_id(0), *[r[...] for r in ins])
        for r, v in zip(outs[:n_ro], res[:n_ro]):
            r[...] = v.astype(r.dtype)
        if n_acc:
            @pl.when(i == 0)
            def _():
                for r in outs[n_ro:]:
                    r[...] = jnp.zeros_like(r)
            for r, v in zip(outs[n_ro:], res[n_ro:]):
                r[...] += v

    in_specs = [pl.BlockSpec(a.shape, lambda j, i, nd=a.ndim: (0,) * nd) for a in full_consts]
    in_specs += [pl.BlockSpec((nr, w), lambda j, i, o=o: (0, o + j)) for (_, nr, w, o) in col_consts]
    in_specs += [pl.BlockSpec((tm, w), lambda j, i, o=o: (i, o + j)) for (_, w, o) in row_ins]
    out_specs = [pl.BlockSpec((tm, w), lambda j, i: (i, j)) for (w, _) in row_outs]
    out_specs += [pl.BlockSpec((nr, w), lambda j, i: (0, j)) for (nr, w) in acc_outs]
    out_shape = [jax.ShapeDtypeStruct((rows, w * ncol), dt) for (w, dt) in row_outs]
    out_shape += [jax.ShapeDtypeStruct((nr, w * ncol), F32) for (nr, w) in acc_outs]
    args = list(full_consts) + [c[0] for c in col_consts] + [r[0] for r in row_ins]
    out = pl.pallas_call(
        body, name=name, grid=(ncol, nt), in_specs=in_specs, out_specs=out_specs, out_shape=out_shape,
        compiler_params=_params(("parallel", "arbitrary")),
    )(*args)
    return out


def _rms(x, w):
    return x * lax.rsqrt(jnp.mean(x * x, axis=-1, keepdims=True) + EPS) * w


def _lane_lo(shape):
    return lax.broadcasted_iota(jnp.int32, shape, len(shape) - 1) < HEAD_DIM


def _pair_sum(x):
    lo = _lane_lo(x.shape)
    s0 = jnp.sum(jnp.where(lo, x, 0.0), axis=-1, keepdims=True)
    s1 = jnp.sum(jnp.where(lo, 0.0, x), axis=-1, keepdims=True)
    return jnp.where(lo, s0, s1)


def _head_col(x, lo, h):
    keep = lo if h == 0 else jnp.logical_not(lo)
    return jnp.max(jnp.where(keep, x, -jnp.inf), axis=-1, keepdims=True)


def _softplus(x):
    return jnp.maximum(x, 0.0) + jnp.log1p(jnp.exp(-jnp.abs(x)))


def _silu(x):
    return x * jax.nn.sigmoid(x)


def _dot(a, b, contract):
    return lax.dot_general(a.astype(BF16), b.astype(BF16), (contract, ((), ())),
                           preferred_element_type=F32)


def _dot32(a, b, contract):
    return lax.dot_general(a, b, (contract, ((), ())), precision=HIGHEST, preferred_element_type=F32)


def _bd(y):
    yy = jnp.concatenate([y, y], axis=0)
    r = lax.broadcasted_iota(jnp.int32, yy.shape, 0) < HEAD_DIM
    c = lax.broadcasted_iota(jnp.int32, yy.shape, 1) < HEAD_DIM
    return jnp.where(r == c, yy, 0.0)


def _pp(x, y):
    return _dot(x, _bd(y), _CONTRACT["nn"])


def _pp_nt(x, y):
    return _dot(x, _bd(y), _CONTRACT["nt"])


def _pp_tn(x, y):
    full = _dot(x, y, _CONTRACT["tn"])
    return jnp.where(_lane_lo((HEAD_DIM, LANES)), full[:HEAD_DIM], full[HEAD_DIM:])


def _gdn_masks():
    row = lax.broadcasted_iota(jnp.int32, (CHUNK, LANES), 0)
    col = lax.broadcasted_iota(jnp.int32, (CHUNK, LANES), 1) % HEAD_DIM
    return row, col


def _interleave(chains):
    live = list(chains)
    while live:
        for g in list(live):
            try:
                next(g)
            except StopIteration:
                live.remove(g)


def _gdn_forward(qkv, betax, gcx, grow, rows):
    nchunk = rows // CHUNK

    def body(q_ref, k_ref, v_ref, bx_ref, gx_ref, gr_ref, o_ref, ss_ref, ts_ref, state):
        n = pl.program_id(0)

        @pl.when(n == 0)
        def _():
            state[...] = jnp.zeros_like(state)

        row, col = _gdn_masks()
        incl, strict = col <= row, col < row

        def chain(p):
            lanes = pl.ds(p * LANES, LANES)
            q, k, v, bx, gx = q_ref[:, lanes], k_ref[:, lanes], v_ref[:, lanes], bx_ref[:, lanes], gx_ref[:, lanes]
            gr = gr_ref[0, p]
            glast = gx_ref[pl.ds(CHUNK - 1, 1), lanes]
            s = state[p]
            dm = jnp.where(incl, jnp.exp(jnp.minimum(gx - gr, 0.0)), 0.0)
            kb, vb, eg, qs = k * bx, v * bx, jnp.exp(gx), q * SCALE
            yield
            big_g, big_p = _pp_nt(kb, k), _pp_nt(qs, k)
            yield
            x = -jnp.where(strict, big_g * dm, 0.0)
            att = jnp.where(incl, big_p * dm, 0.0)
            tm = jnp.where(row == col, 1.0, 0.0) + x
            x = _pp(x, x)
            yield
            for _ in range(4):
                step, x = _pp(tm, x), _pp(x, x)
                yield
                tm = tm + step
            tm = tm + _pp(tm, x)
            yield
            u, w = _pp(tm, vb), _pp(tm, kb * eg)
            yield
            ws, qgs = _pp(w, s), _pp(qs * eg, s)
            yield
            vn = u - ws
            kd = k * jnp.exp(glast - gx)
            avn, upd = _pp(att, vn), _pp_tn(kd, vn)
            yield
            ss_ref[0, p] = s
            ts_ref[0, p] = tm
            o_ref[:, lanes] = qgs + avn
            state[p] = s * jnp.exp(glast) + upd

        _interleave([chain(p) for p in range(PAIRS)])

    blk = lambda j: pl.BlockSpec((CHUNK, WIDTH), lambda n, j=j: (n, j))
    sv = pl.BlockSpec((1, PAIRS, CHUNK, LANES), lambda n: (n, 0, 0, 0))
    return pl.pallas_call(
        body, name="gdn_fwd", grid=(nchunk,),
        in_specs=[blk(0), blk(1), blk(2), blk(0), blk(0),
                  pl.BlockSpec((1, PAIRS, 1, LANES), lambda n: (n, 0, 0, 0))],
        out_specs=[blk(0), sv, sv],
        out_shape=[jax.ShapeDtypeStruct((rows, WIDTH), F32),
                   jax.ShapeDtypeStruct((nchunk, PAIRS, CHUNK, LANES), F32),
                   jax.ShapeDtypeStruct((nchunk, PAIRS, CHUNK, LANES), F32)],
        scratch_shapes=[pltpu.VMEM((PAIRS, CHUNK, LANES), F32)],
        compiler_params=_params(("arbitrary",)),
    )(qkv, qkv, qkv, betax, gcx, grow)


def _gdn_backward(qkv, betax, gcx, grow, ssave, tsave, do, rows):
    nchunk = rows // CHUNK

    def body(q_ref, k_ref, v_ref, bx_ref, gx_ref, gr_ref, ss_ref, ts_ref, do_ref,
             dq_ref, dk_ref, dv_ref, dbx_ref, dgx_ref, dgr_ref, dstate):
        n = pl.program_id(0)

        @pl.when(n == 0)
        def _():
            dstate[...] = jnp.zeros_like(dstate)

        row, col = _gdn_masks()
        incl, strict = col <= row, col < row

        def chain(p):
            lanes = pl.ds(p * LANES, LANES)
            q, k, v, bx, gx = q_ref[:, lanes], k_ref[:, lanes], v_ref[:, lanes], bx_ref[:, lanes], gx_ref[:, lanes]
            gr = gr_ref[0, p]
            glast = gx_ref[pl.ds(CHUNK - 1, 1), lanes]
            s, tm, d_o = ss_ref[0, p], ts_ref[0, p], do_ref[:, lanes]
            ds_out = dstate[p]
            dm = jnp.where(incl, jnp.exp(jnp.minimum(gx - gr, 0.0)), 0.0)
            kb, vb, eg, qs = k * bx, v * bx, jnp.exp(gx), q * SCALE
            kbg, qg = kb * eg, qs * eg
            ed = jnp.exp(glast - gx)
            kd = k * ed
            eglast = jnp.exp(glast)
            yield
            big_g, big_p = _pp_nt(kb, k), _pp_nt(qs, k)
            u, w = _pp(tm, vb), _pp(tm, kbg)
            dqg, kds = _pp_nt(d_o, s), _pp(kd, ds_out)
            yield
            low = jnp.where(strict, big_g * dm, 0.0)
            att = jnp.where(incl, big_p * dm, 0.0)
            ws, atd = _pp(w, s), _pp_tn(att, d_o)
            yield
            vn = u - ws
            dvn = kds + atd
            dkd, datt_raw = _pp_nt(vn, ds_out), _pp_nt(d_o, vn)
            dw_neg, dvb = _pp_nt(dvn, s), _pp_tn(tm, dvn)
            dtm_a, wdv = _pp_nt(dvn, vb), _pp_tn(w, dvn)
            qgd = _pp_tn(qg, d_o)
            yield
            datt = jnp.where(incl, datt_raw, 0.0)
            dw = -dw_neg
            dtm_b, dkbg = _pp_nt(dw, kbg), _pp_tn(tm, dw)
            dbig_p = datt * dm
            dqs_a, dk_p = _pp(dbig_p, k), _pp_tn(dbig_p, qs)
            yield
            inner = _pp_tn(tm, dtm_a + dtm_b)
            yield
            dlow = jnp.where(strict, -_pp_nt(inner, tm), 0.0)
            yield
            dbig_g = dlow * dm
            dkb_a, dk_g = _pp(dbig_g, k), _pp_tn(dbig_g, kb)
            yield
            dkb = dkb_a + dkbg * eg
            dqs = dqs_a + dqg * eg
            dk = dk_g + dk_p + dkd * ed + dkb * bx
            z = dlow * low + datt * att
            kdterm = dkd * kd
            dglast = (jnp.sum(ds_out * s, axis=0, keepdims=True) * eglast
                      + jnp.sum(kdterm, axis=0, keepdims=True))
            dgx = dqg * qg + dkbg * kbg - kdterm
            dgx = dgx + jnp.where(col == 0, _pair_sum(z), 0.0)
            dgx = dgx + jnp.where(row == CHUNK - 1, dglast, 0.0)
            dq_ref[:, lanes] = dqs * SCALE
            dk_ref[:, lanes] = dk
            dv_ref[:, lanes] = dvb * bx
            dbx_ref[:, lanes] = dkb * k + dvb * v
            dgx_ref[:, lanes] = dgx
            dgr_ref[0, p] = -jnp.sum(z, axis=0, keepdims=True)
            dstate[p] = ds_out * eglast + qgd - wdv

        _interleave([chain(p) for p in range(PAIRS)])

    last = nchunk - 1
    blk = lambda j: pl.BlockSpec((CHUNK, WIDTH), lambda n, j=j: (last - n, j))
    sv = pl.BlockSpec((1, PAIRS, CHUNK, LANES), lambda n: (last - n, 0, 0, 0))
    gr_spec = pl.BlockSpec((1, PAIRS, 1, LANES), lambda n: (last - n, 0, 0, 0))
    wide = jax.ShapeDtypeStruct((rows, WIDTH), F32)
    return pl.pallas_call(
        body, name="gdn_bwd", grid=(nchunk,),
        in_specs=[blk(0), blk(1), blk(2), blk(0), blk(0), gr_spec, sv, sv, blk(0)],
        out_specs=[blk(0)] * 5 + [gr_spec],
        out_shape=[wide] * 5 + [jax.ShapeDtypeStruct((nchunk, PAIRS, 1, LANES), F32)],
        scratch_shapes=[pltpu.VMEM((PAIRS, CHUNK, LANES), F32)],
        compiler_params=_params(("arbitrary",)),
    )(qkv, qkv, qkv, betax, gcx, grow, ssave, tsave, do)


ATT_TQ = 256


def _att_scores(qh, kt, fk, diag):
    s = _dot(qh, kt, _CONTRACT["nt"]) - fk
    if diag:
        r = lax.broadcasted_iota(jnp.int32, s.shape, 0)
        c = lax.broadcasted_iota(jnp.int32, s.shape, 1)
        s = jnp.where(r >= c, s, -jnp.inf)
    return s


def _head_masks(n):
    lo = _lane_lo((n, LANES))
    return [lo, jnp.logical_not(lo)]


def _attention_forward(fqk, proj, frow, rows):
    tq = tk = min(ATT_TQ, rows)
    nq = rows // tq
    v_off = 3072 // LANES

    def body(q_ref, k_ref, v_ref, fr_ref, o_ref, lse_ref):
        qi = pl.program_id(1)
        q = q_ref[...] * SCALE
        keep_q, keep_k = _head_masks(tq), _head_masks(tk)
        qh = [jnp.where(keep_q[h], q, 0.0).astype(BF16) for h in range(2)]

        def tile(ki, carry, diag):
            k0 = pl.multiple_of(ki * tk, tk)
            kt = k_ref[pl.ds(k0, tk), :].astype(BF16)
            v_t = v_ref[pl.ds(k0, tk), :]
            out = [None, None]

            def chain(h):
                m, l, acc = carry[h]
                vt = jnp.where(keep_k[h], v_t, 0.0).astype(BF16)
                yield
                s = _att_scores(qh[h], kt, fr_ref[0, pl.ds(h, 1), pl.ds(k0, tk)], diag)
                yield
                m_new = jnp.maximum(m, jnp.max(s, axis=-1, keepdims=True))
                p = jnp.exp(s - m_new)
                alpha = jnp.exp(m - m_new)
                l = alpha * l + jnp.sum(p, axis=-1, keepdims=True)
                p_hi = p.astype(BF16)
                p_lo = p - p_hi.astype(F32)
                yield
                out[h] = (m_new, l, alpha * acc + _dot(p_hi, vt, _CONTRACT["nn"]) + _dot(p_lo, vt, _CONTRACT["nn"]))

            _interleave([chain(0), chain(1)])
            return tuple(out)

        one = (jnp.full((tq, 1), -jnp.inf, F32), jnp.zeros((tq, 1), F32), jnp.zeros((tq, LANES), F32))
        carry = lax.fori_loop(0, qi, lambda ki, c: tile(ki, c, False), (one, one))
        (m0, l0, acc0), (m1, l1, acc1) = tile(qi, carry, True)
        o_ref[...] = acc0 / l0 + acc1 / l1
        lse_ref[...] = jnp.where(keep_q[0], m0 + jnp.log(l0), m1 + jnp.log(l1))

    whole = lambda off: pl.BlockSpec((rows, LANES), lambda p, i, off=off: (0, off + p))
    qblk = lambda off: pl.BlockSpec((tq, LANES), lambda p, i, off=off: (i, off + p))
    wide = jax.ShapeDtypeStruct((rows, WIDTH), F32)
    return pl.pallas_call(
        body, name="fox_fwd", grid=(PAIRS, nq),
        in_specs=[qblk(0), whole(PAIRS), whole(v_off), pl.BlockSpec((1, 2, rows), lambda p, i: (p, 0, 0))],
        out_specs=[qblk(0), qblk(0)], out_shape=[wide, wide],
        compiler_params=_params(("parallel", "arbitrary")),
    )(fqk, fqk, proj, frow)


def _attention_backward(fqk, proj, frow, ao, lse, dao, rows):
    tq = tk = min(ATT_TQ, rows)
    nq = rows // tq
    v_off = 3072 // LANES

    def body(q_ref, k_ref, v_ref, fr_ref, o_ref, lse_ref, do_ref, dq_ref, dk_ref, dv_ref, dfr_ref):
        ki = pl.program_id(1)

        @pl.when(ki == 0)
        def _():
            dq_ref[...] = jnp.zeros_like(dq_ref)

        keep_q, keep_k = _head_masks(tq), _head_masks(tk)
        k_t = k_ref[...]
        kt = k_t.astype(BF16)
        vt = v_ref[...].astype(BF16)
        kh = [jnp.where(keep_k[h], k_t, 0.0).astype(BF16) for h in range(2)]
        fk = [fr_ref[0, pl.ds(h, 1), :] for h in range(2)]

        def tile(qi, carry, diag):
            dk, dv, df0, df1 = carry
            rows_q = pl.ds(pl.multiple_of(qi * tq, tq), tq)
            q, d_o, lse_t = q_ref[rows_q, :] * SCALE, do_ref[rows_q, :], lse_ref[rows_q, :]
            delta_x = _pair_sum(d_o.astype(BF16).astype(F32) * o_ref[rows_q, :])
            res = [None, None]

            def chain(h):
                qh = jnp.where(keep_q[h], q, 0.0).astype(BF16)
                doh = jnp.where(keep_q[h], d_o, 0.0).astype(BF16)
                lse_h, delta_h = _head_col(lse_t, keep_q[0], h), _head_col(delta_x, keep_q[0], h)
                yield
                s, dp = _att_scores(qh, kt, fk[h], diag), _dot(doh, vt, _CONTRACT["nt"])
                yield
                p = jnp.exp(s - lse_h)
                ds = p * (dp - delta_h)
                yield
                res[h] = (_dot(p, doh, _CONTRACT["tn"]), _dot(ds, qh, _CONTRACT["tn"]),
                          _dot(ds, kh[h], _CONTRACT["nn"]), jnp.sum(ds, axis=0, keepdims=True))

            _interleave([chain(0), chain(1)])
            (dv0, dk0, dq0, s0), (dv1, dk1, dq1, s1) = res
            dq_ref[rows_q, :] += (dq0 + dq1) * SCALE
            return dk + dk0 + dk1, dv + dv0 + dv1, df0 - s0, df1 - s1

        zero_kv = jnp.zeros((tk, LANES), F32)
        zero_f = jnp.zeros((1, tk), F32)
        carry = tile(ki, (zero_kv, zero_kv, zero_f, zero_f), True)
        dk, dv, df0, df1 = lax.fori_loop(ki + 1, nq, lambda qi, c: tile(qi, c, False), carry)
        dk_ref[...] = dk
        dv_ref[...] = dv.astype(dv_ref.dtype)
        dfr_ref[0, pl.ds(0, 1), :] = df0
        dfr_ref[0, pl.ds(1, 1), :] = df1

    whole = lambda off: pl.BlockSpec((rows, LANES), lambda p, i, off=off: (0, off + p))
    kblk = lambda off: pl.BlockSpec((tk, LANES), lambda p, i, off=off: (i, off + p))
    fr_spec = pl.BlockSpec((1, 2, tk), lambda p, i: (p, 0, i))
    wide = jax.ShapeDtypeStruct((rows, WIDTH), F32)
    return pl.pallas_call(
        body, name="fox_bwd", grid=(PAIRS, nq),
        in_specs=[whole(0), kblk(PAIRS), kblk(v_off), fr_spec, whole(0), whole(0), whole(0)],
        out_specs=[whole(0), kblk(0), kblk(0), fr_spec],
        out_shape=[wide, wide, jax.ShapeDtypeStruct((rows, WIDTH), BF16),
                   jax.ShapeDtypeStruct((PAIRS, 2, rows), F32)],
        compiler_params=_params(("parallel", "arbitrary")),
    )(fqk, fqk, proj, frow, ao, lse, dao)


def _lane_ids(shape):
    return lax.broadcasted_iota(jnp.int32, shape, len(shape) - 1)


def _gates_elem(a_log, dt_bias, f_bias, pre):
    lane = _lane_ids(pre.shape)
    beta = jax.nn.sigmoid(pre)
    g = -jnp.exp(a_log) * _softplus(pre + dt_bias)
    lf = -_softplus(-(pre + f_bias))
    return jnp.where(lane < 8, beta, jnp.where(lane < 16, g, jnp.where(lane < 24, lf, 0.0)))


def _tri_consts():
    r = np.arange(LANES)[:, None]
    c = np.arange(LANES)[None, :]
    full = (c <= r).astype(np.float32)
    chunked = full * ((r // CHUNK) == (c // CHUNK))
    return jnp.asarray(chunked), jnp.asarray(full)


def _cums_fwd(lc, lf, gates):
    rows = gates.shape[0]
    lane = _lane_ids((LANES, LANES))
    carry = jnp.zeros((1, LANES), F32)
    out = []
    for r in range(rows // LANES):
        blk = gates[r * LANES:(r + 1) * LANES]
        gc = _dot32(lc, blk, _CONTRACT["nn"])
        f = _dot32(lf, blk, _CONTRACT["nn"]) + carry
        carry = carry + jnp.sum(blk, axis=0, keepdims=True)
        out.append(jnp.where((lane >= 8) & (lane < 16), gc, jnp.where((lane >= 16) & (lane < 24), f, 0.0)))
    return jnp.concatenate(out, axis=0)


def _cums_bwd(lc, lf, dcums):
    rows = dcums.shape[0]
    lane = _lane_ids((LANES, LANES))
    is_g = (lane >= 8) & (lane < 16)
    is_f = (lane >= 16) & (lane < 24)
    carry = jnp.zeros((1, LANES), F32)
    out = [None] * (rows // LANES)
    for r in reversed(range(rows // LANES)):
        blk = dcums[r * LANES:(r + 1) * LANES]
        dg = jnp.where(is_g, blk, 0.0)
        df = jnp.where(is_f, blk, 0.0)
        out[r] = _dot32(lc, dg, _CONTRACT["tn"]) + _dot32(lf, df, _CONTRACT["tn"]) + carry
        carry = carry + jnp.sum(df, axis=0, keepdims=True)
    return jnp.concatenate(out, axis=0)


def _expand_consts():
    xb = np.zeros((LANES, WIDTH), np.float32)
    xg = np.zeros((LANES, WIDTH), np.float32)
    for h in range(HEADS):
        xb[h, h * HEAD_DIM:(h + 1) * HEAD_DIM] = 1.0
        xg[8 + h, h * HEAD_DIM:(h + 1) * HEAD_DIM] = 1.0
    return jnp.asarray(xb), jnp.asarray(xg)


def _shift_down(x, s):
    if s == 0:
        return x
    row = lax.broadcasted_iota(jnp.int32, x.shape, 0)
    return jnp.where(row >= s, pltpu.roll(x, s, 0), 0.0)


def _shift_up(x, s):
    if s == 0:
        return x
    n = x.shape[0]
    row = lax.broadcasted_iota(jnp.int32, x.shape, 0)
    return jnp.where(row < n - s, pltpu.roll(x, n - s, 0), 0.0)


def _row_of(cw, i):
    row = lax.broadcasted_iota(jnp.int32, cw.shape, 0)
    return jnp.sum(jnp.where(row == i, cw, 0.0), axis=0, keepdims=True)


def _conv(cw, x):
    c = jnp.zeros_like(x)
    for i in range(CONV_K):
        c = c + _row_of(cw, i) * _shift_down(x, CONV_K - 1 - i)
    return c


def _post_conv(is_qk, c):
    s = _silu(c)
    n = s * lax.rsqrt(_pair_sum(s * s) + EPS)
    return jnp.where(is_qk, n, s)


def _gdn_prep_fwd(col, cw, x):
    return (_post_conv(col < 2 * PAIRS, _conv(cw, x)),)


def _gdn_prep_bwd(is_qk, cw, x, dy):
    c = _conv(cw, x)
    _, vjp = jax.vjp(lambda cc: _post_conv(is_qk, cc), c)
    (dc,) = vjp(dy)
    dx = jnp.zeros_like(x)
    row = lax.broadcasted_iota(jnp.int32, cw.shape, 0)
    dcw = jnp.zeros(cw.shape, F32)
    for i in range(CONV_K):
        s = CONV_K - 1 - i
        dx = dx + _row_of(cw, i) * _shift_up(dc, s)
        dcw = dcw + jnp.where(row == i, jnp.sum(dc * _shift_down(x, s), axis=0, keepdims=True), 0.0)
    return dx, dcw


def _head_rms(w, x):
    return x * lax.rsqrt(_pair_sum(x * x) / HEAD_DIM + EPS) * w


def _cat_weights(w_in_t):
    tail = jnp.pad(w_in_t[4112:4120], ((0, D_CAT - D_IN), (0, 0)))
    return jnp.concatenate([w_in_t[:2048], w_in_t[2064:4112], w_in_t[2048:2064], tail], axis=0)


def _uncat_grad(g):
    return jnp.concatenate([g[:2048], g[4096:4112], g[2048:4096], g[4112:4120]], axis=0)


def _lanes_to_rowform(v8, rows):
    return v8.reshape(rows // CHUNK, CHUNK, HEADS).transpose(0, 2, 1).reshape(rows // CHUNK, PAIRS, 1, LANES)


def _rowform_to_lanes(v, rows):
    return v.reshape(rows // CHUNK, HEADS, CHUNK).transpose(0, 2, 1).reshape(rows, HEADS)


def _local_step(x, target, norm1_w, a_log, dt_bias, out_norm_w, f_bias, q_norm_w, k_norm_w,
                norm2_w, final_w, first_weights, late_weights, early_grads_ready, early_grads_continue):
    rows = x.shape[0]
    tm = min(512, rows)
    lc, lf = _tri_consts()
    xb, xg = _expand_consts()

    (h1,) = _tiles(lambda col, w, xx: (_rms(xx, w),), name="norm1", rows=rows, tm=tm,
                   full_consts=[norm1_w], row_ins=[(x, D_MODEL, 0)], row_outs=[(D_MODEL, BF16)])
    w_cat, conv_w = first_weights(h1)
    proj = _mm(h1, w_cat, dims="nt", name="in_proj", tn=1408, tk=1024)

    lane_pad = lambda v, off: jnp.pad(v.reshape(1, -1), ((0, 0), (off, LANES - off - v.size)))
    p_a, p_dt, p_fb = lane_pad(a_log, 8), lane_pad(dt_bias, 8), lane_pad(f_bias, 16)

    def gates_fwd(col, lcv, lfv, a, dt, fb, pre):
        gates = _gates_elem(a, dt, fb, pre)
        return gates, _cums_fwd(lcv, lfv, gates)

    gates, cums = _tiles(gates_fwd, name="gates", rows=rows, tm=rows,
                         full_consts=[lc, lf, p_a, p_dt, p_fb], row_ins=[(proj, LANES, COL_SMALL)],
                         row_outs=[(LANES, F32), (LANES, F32)])

    def expand_fwd(col, b, g, gt, cm):
        return (_dot32(gt, b, _CONTRACT["nn"]), _dot32(cm, g, _CONTRACT["nn"]))

    betax, gcx = _tiles(expand_fwd, name="expand", rows=rows, tm=tm, full_consts=[xb, xg],
                        row_ins=[(gates, LANES, 0), (cums, LANES, 0)],
                        row_outs=[(WIDTH, F32)] * 2)
    grow = _lanes_to_rowform(cums[:, 8:16], rows)
    frow = cums[:, 16:24].T.reshape(PAIRS, 2, rows)

    (qkv,) = _tiles(_gdn_prep_fwd, name="gdn_prep", rows=rows, tm=rows, ncol=3 * PAIRS,
                    col_consts=[(conv_w, CONV_K, LANES, 0)], row_ins=[(proj, LANES, 0)],
                    row_outs=[(LANES, F32)])
    o_gdn, ssave, tsave = _gdn_forward(qkv, betax, gcx, grow, rows)

    w_qk = jnp.concatenate([jnp.tile(q_norm_w.reshape(1, -1), (1, HEADS)),
                            jnp.tile(k_norm_w.reshape(1, -1), (1, HEADS))], axis=1)
    fox_off = 2048 // LANES
    (fqk,) = _tiles(lambda col, w, xx: (_head_rms(w, xx),), name="fox_prep", rows=rows, tm=rows, ncol=2 * PAIRS,
                    col_consts=[(w_qk, 1, LANES, 0)], row_ins=[(proj, LANES, fox_off)],
                    row_outs=[(LANES, F32)])
    ao, lse = _attention_forward(fqk, proj, frow, rows)

    w_on = jnp.tile(out_norm_w.reshape(1, -1), (1, 2))
    z_off, fg_off = 1536 // LANES, 3584 // LANES
    mix_g_fn = lambda w, o, z: _head_rms(w, o) * _silu(z)
    mix_f_fn = lambda a, g: a * jax.nn.sigmoid(g)
    (mix_g,) = _tiles(lambda col, w, o, z: (mix_g_fn(w, o, z),), name="mix_gdn", rows=rows, tm=rows, ncol=PAIRS,
                      full_consts=[w_on], row_ins=[(o_gdn, LANES, 0), (proj, LANES, z_off)],
                      row_outs=[(LANES, BF16)])
    (mix_f,) = _tiles(lambda col, a, g: (mix_f_fn(a, g),), name="mix_fox", rows=rows, tm=rows, ncol=PAIRS,
                      row_ins=[(ao, LANES, 0), (proj, LANES, fg_off)], row_outs=[(LANES, BF16)])
    mix = jnp.concatenate([mix_g, mix_f], axis=1)
    w_out, w_gate, w_up, w_down = late_weights(mix)
    t_rows, t_half = min(1024, rows), min(512, rows)
    n_rt = rows // t_rows
    row_blk = pl.BlockSpec((t_rows, D_MODEL), lambda i, n: (i, 0))
    half_blk = pl.BlockSpec((t_half, D_MODEL), lambda i, n: (i, 0))
    vec_blk = pl.BlockSpec((1, D_MODEL), lambda i, n: (0, 0))
    wide = lambda dt: jax.ShapeDtypeStruct((rows, D_MODEL), dt)
    x1, h2 = _mm_blocks(mix, w_out, name="out_proj_norm2", grid=(n_rt, 1), dims="nn",
                        a_spec=row_blk, b_spec=pl.BlockSpec((D_MODEL, D_MODEL), lambda i, n: (0, 0)),
                        o_spec=[row_blk, row_blk], out_shape=[wide(F32), wide(BF16)], add=x, add_spec=row_blk,
                        extra=[(norm2_w, vec_blk)], epilogue=lambda r, w: (r, _rms(r, w)))
    st_act = jax.ShapeDtypeStruct((N_CHIPS, rows, FF_SHARD), BF16)
    st_rows = pl.BlockSpec((None, rows, FF_SHARD), lambda i, j: (j, i, 0))

    def ffn_in(w_st, name):
        return _mm_blocks(h2, w_st, name=name, grid=(1, N_CHIPS), dims="nt",
                          a_spec=pl.BlockSpec((rows, D_MODEL), lambda i, j: (i, 0)),
                          b_spec=pl.BlockSpec((None, FF_SHARD, D_MODEL), lambda i, j: (j, 0, 0)),
                          o_spec=st_rows, out_shape=st_act)

    gate = ffn_in(w_gate, "ffn_gate")
    act_fn = lambda g, u: _silu(g) * u
    st_tile = pl.BlockSpec((None, t_rows, FF_SHARD), lambda i, j: (j, i, 0))
    up, act = _mm_blocks(h2, w_up, name="ffn_up_act", grid=(n_rt, N_CHIPS), dims="nt",
                         a_spec=pl.BlockSpec((t_rows, D_MODEL), lambda i, j: (i, 0)),
                         b_spec=pl.BlockSpec((None, FF_SHARD, D_MODEL), lambda i, j: (j, 0, 0)),
                         o_spec=[st_tile, st_tile], out_shape=[st_act, st_act], extra=[(gate, st_tile)],
                         epilogue=lambda u, g: (u, act_fn(g.astype(F32), u)))

    def final_fn(xx, tgt, w):
        y, vjp = jax.vjp(_rms, xx, w)
        err = y - tgt
        loss = 0.5 * jnp.sum(err * err) / D_MODEL
        dx, dw = vjp(err / D_MODEL)
        return dx, dx, jnp.full((1, LANES), loss, F32), dw

    dx2, dx2_b, loss, d_final_w = _mm_blocks(
        act, w_down, name="ffn_down_loss", grid=(rows // t_half, 1), dims="nn", n_sum=N_CHIPS,
        a_spec=pl.BlockSpec((N_CHIPS, t_half, FF_SHARD), lambda i, n: (0, i, 0)),
        b_spec=pl.BlockSpec((N_CHIPS, FF_SHARD, D_MODEL), lambda i, n: (0, 0, 0)),
        o_spec=[half_blk, half_blk, pl.BlockSpec((1, LANES), lambda i, n: (0, 0)), vec_blk],
        out_shape=[wide(F32), wide(BF16), jax.ShapeDtypeStruct((1, LANES), F32),
                   jax.ShapeDtypeStruct((1, D_MODEL), F32)],
        add=x1, add_spec=half_blk, extra=[(target, half_blk), (final_w, vec_blk)], epilogue=final_fn, n_acc=2)

    def act_bwd(d, g, u):
        _, vjp = jax.vjp(act_fn, g.astype(F32), u.astype(F32))
        return vjp(d)

    dgate, dup = _mm_blocks(dx2_b, w_down, name="d_act_gate_up", grid=(n_rt, N_CHIPS), dims="nt",
                            a_spec=pl.BlockSpec((t_rows, D_MODEL), lambda i, j: (i, 0)),
                            b_spec=pl.BlockSpec((None, FF_SHARD, D_MODEL), lambda i, j: (j, 0, 0)),
                            o_spec=[st_tile, st_tile], out_shape=[st_act, st_act],
                            extra=[(gate, st_tile), (up, st_tile)], epilogue=act_bwd)

    def g_ffn(d_st, other, name):
        return _mm_blocks(d_st, other, name=name, grid=(N_CHIPS, 1), dims="tn",
                          a_spec=pl.BlockSpec((None, rows, FF_SHARD), lambda j, n: (j, 0, 0)),
                          b_spec=pl.BlockSpec((rows, D_MODEL), lambda j, n: (0, 0)),
                          o_spec=pl.BlockSpec((None, FF_SHARD, D_MODEL), lambda j, n: (j, 0, 0)),
                          out_shape=jax.ShapeDtypeStruct((N_CHIPS, FF_SHARD, D_MODEL), BF16))

    g_down = g_ffn(act, dx2_b, "g_down")

    def norm_bwd(dh, xx, dres, w):
        _, vjp = jax.vjp(_rms, xx, w)
        dx, dw = vjp(dh)
        return dx + dres, dx + dres, dw

    def d_h2(d_st, w_st, name, add, **fused):
        return _mm_blocks(d_st, w_st, name=name, grid=(rows // t_half, 1), dims="nn", n_sum=N_CHIPS,
                          a_spec=pl.BlockSpec((N_CHIPS, t_half, FF_SHARD), lambda i, n: (0, i, 0)),
                          b_spec=pl.BlockSpec((N_CHIPS, FF_SHARD, D_MODEL), lambda i, n: (0, 0, 0)),
                          add=add, add_spec=half_blk, **fused)

    dh2_gate = d_h2(dgate, w_gate, "d_h2_gate", None, o_spec=half_blk, out_shape=wide(F32))
    dx1, dx1_b, d_norm2_w = d_h2(
        dup, w_up, "d_h2_up_norm2_bwd", dh2_gate, o_spec=[half_blk, half_blk, vec_blk],
        out_shape=[wide(F32), wide(BF16), jax.ShapeDtypeStruct((1, D_MODEL), F32)],
        extra=[(x1, half_blk), (dx2, half_blk), (norm2_w, vec_blk)], epilogue=norm_bwd, n_acc=1)
    g_gate, g_up = g_ffn(dgate, h2, "g_gate"), g_ffn(dup, h2, "g_up")
    dmix = _mm(dx1_b, w_out, dims="nt", name="d_mix", tn=D_MODEL, tk=1024)
    g_out = _mm(mix, dx1_b, dims="tn", name="g_out", tn=D_MODEL, tk=rows, out_dtype=BF16)
    w_on = w_on + early_grads_ready(g_out, g_gate, g_up, g_down)

    def mix_g_bwd(col, w, o, z, d):
        _, vjp = jax.vjp(mix_g_fn, w, o, z)
        dw, do_, dz = vjp(d)
        return do_, dz, dw

    do_gdn, dz, d_on = _tiles(mix_g_bwd, name="mix_gdn_bwd", rows=rows, tm=rows, ncol=PAIRS, full_consts=[w_on],
                              row_ins=[(o_gdn, LANES, 0), (proj, LANES, z_off), (dmix, LANES, 0)],
                              row_outs=[(LANES, F32), (LANES, BF16)], acc_outs=[(1, LANES)])

    def mix_f_bwd(col, a, g, d):
        _, vjp = jax.vjp(mix_f_fn, a, g)
        return vjp(d)

    dao, dfgate = _tiles(mix_f_bwd, name="mix_fox_bwd", rows=rows, tm=rows, ncol=PAIRS,
                         row_ins=[(ao, LANES, 0), (proj, LANES, fg_off), (dmix, LANES, PAIRS)],
                         row_outs=[(LANES, F32), (LANES, BF16)])

    dfq, dfk, dfv, dfrow = _attention_backward(fqk, proj, frow + early_grads_continue(dao), ao, lse, dao, rows)

    def fox_prep_bwd(col, w, xx, d):
        _, vjp = jax.vjp(_head_rms, w, xx)
        dw, dx = vjp(d)
        return dx, dw

    dfqk, d_wqk = [], []
    for part, d_n in enumerate((dfq, dfk)):
        dx_p, dw_p = _tiles(fox_prep_bwd, name="fox_prep_bwd_" + "qk"[part], rows=rows, tm=rows, ncol=PAIRS,
                            col_consts=[(w_qk, 1, LANES, part * PAIRS)],
                            row_ins=[(proj, LANES, fox_off + part * PAIRS), (d_n, LANES, 0)],
                            row_outs=[(LANES, BF16)], acc_outs=[(1, LANES)])
        dfqk.append(dx_p)
        d_wqk.append(dw_p)

    dq, dk, dv, dbetax, dgcx, dgrow = _gdn_backward(qkv, betax, gcx, grow, ssave, tsave, do_gdn, rows)
    dqkv, d_conv = [], []
    for part, d_n in enumerate((dq, dk, dv)):
        prep_bwd = lambda col, cw, xx, dy, is_qk=(part < 2): _gdn_prep_bwd(is_qk, cw, xx, dy)
        dx_p, dw_p = _tiles(prep_bwd, name="gdn_prep_bwd_" + "qkv"[part], rows=rows, tm=rows, ncol=PAIRS,
                            col_consts=[(conv_w, CONV_K, LANES, part * PAIRS)],
                            row_ins=[(proj, LANES, part * PAIRS), (d_n, LANES, 0)],
                            row_outs=[(LANES, BF16)], acc_outs=[(CONV_K, LANES)])
        dqkv.append(dx_p)
        d_conv.append(dw_p)
    d_conv = jnp.concatenate(d_conv, axis=1)

    def expand_bwd(col, b, g, db, dg):
        return (_dot32(db, b, _CONTRACT["nt"]), _dot32(dg, g, _CONTRACT["nt"]))

    dgates_b, dcums_g = _tiles(expand_bwd, name="expand_bwd", rows=rows, tm=tm, full_consts=[xb, xg],
                               row_ins=[(dbetax, WIDTH, 0), (dgcx, WIDTH, 0)],
                               row_outs=[(LANES, F32), (LANES, F32)])
    dcums_row = jnp.concatenate([jnp.zeros((rows, 8), F32), _rowform_to_lanes(dgrow, rows),
                                 dfrow.reshape(HEADS, rows).T, jnp.zeros((rows, LANES - 24), F32)], axis=1)

    def gates_bwd(col, lcv, lfv, a, dt, fb, pre, dgb, dcg, dcr):
        lane = _lane_ids(pre.shape)
        dgates = jnp.where(lane < 8, dgb, _cums_bwd(lcv, lfv, dcg + dcr))
        _, vjp = jax.vjp(_gates_elem, a, dt, fb, pre)
        da, ddt, dfb, dpre = vjp(dgates)
        return dpre, da, ddt, dfb

    dpre, d_a, d_dt, d_fb = _tiles(gates_bwd, name="gates_bwd", rows=rows, tm=rows,
                                   full_consts=[lc, lf, p_a, p_dt, p_fb],
                                   row_ins=[(proj, LANES, COL_SMALL), (dgates_b, LANES, 0), (dcums_g, LANES, 0),
                                            (dcums_row, LANES, 0)],
                                   row_outs=[(LANES, BF16)], acc_outs=[(1, LANES)] * 3)

    dproj = jnp.concatenate(dqkv + [dz] + dfqk + [dfv, dfgate, dpre], axis=1)
    grad_x, d_norm1_w = _mm_blocks(
        dproj, w_cat, name="d_h1_norm1_bwd", grid=(rows // t_half, 1), dims="nn",
        a_spec=pl.BlockSpec((t_half, D_CAT), lambda i, n: (i, 0)),
        b_spec=pl.BlockSpec((D_CAT, D_MODEL), lambda i, n: (0, 0)),
        o_spec=[half_blk, vec_blk], out_shape=[wide(F32), jax.ShapeDtypeStruct((1, D_MODEL), F32)],
        extra=[(x, half_blk), (dx1, half_blk), (norm1_w, vec_blk)],
        epilogue=lambda dh, xx, dres, w: norm_bwd(dh, xx, dres, w)[1:], n_acc=1)
    g_cat = _mm(dproj, h1, dims="tn", name="g_in", tm=1408, tn=D_MODEL, tk=rows)

    fold = lambda v: v.reshape(-1, HEAD_DIM).sum(axis=0)
    small = dict(
        loss=loss[0, 0],
        norm1_w=d_norm1_w, conv_w=d_conv, a_log=d_a[0, 8:16], dt_bias=d_dt[0, 8:16],
        out_norm_w=fold(d_on), f_bias=d_fb[0, 16:24], q_norm_w=fold(d_wqk[0]),
        k_norm_w=fold(d_wqk[1]), norm2_w=d_norm2_w, final_w=d_final_w)
    return grad_x, g_cat, g_out, g_gate, g_up, g_down, small


HBM_SPEC = pl.BlockSpec(memory_space=pltpu.HBM)


def _place():
    x, y, c = lax.axis_index("x"), lax.axis_index("y"), lax.axis_index("c")
    chips = [(1 - x, y), (x, 1 - y), (1 - x, 1 - y)]
    return x, y, c, 2 * x + y, (x, y, 1 - c), chips, [2 * cx + cy for cx, cy in chips]


def _remote(src, dst, send_sem, recv_sem, to):
    return pltpu.make_async_remote_copy(src_ref=src, dst_ref=dst, send_sem=send_sem, recv_sem=recv_sem,
                                        device_id=to, device_id_type=MESH)


SEM_SPEC =pl.BlockSpec(memory_space=pltpu.SEMAPHORE)
ANY_SPEC = pl.BlockSpec(memory_space=pl.ANY)
DATAFLOW = pltpu.SideEffectType.DATAFLOW_SIDE_EFFECTING


def _gather_plan(srcs, lands):
    x, y, c, own, sib, chips, chip_idx = _place()
    plan = []
    for src, land in zip(srcs, lands):
        for j, chip in enumerate(chips):
            plan.append((src, land.at[own], (*chip, c), land.at[chip_idx[j]]))
        plan.append((src, land.at[own], sib, land.at[own]))
    return plan


def _exchange_plan(srcs, lands):
    x, y, c, own, sib, chips, chip_idx = _place()
    plan = []
    for src, land in zip(srcs, lands):
        for j, chip in enumerate(chips):
            plan.append((src.at[chip_idx[j]], land.at[j], (*chip, c), land.at[j]))
    return plan


def _swap_plan(srcs, lands):
    x, y, c, own, sib, chips, chip_idx = _place()
    plan = []
    for src, land in zip(srcs, lands):
        h = src.shape[2] // 2
        plan.append((src.at[:, :, pl.ds(pl.multiple_of((1 - c) * h, LANES), h)], land, sib, land))
    return plan


def _in_proj_plan(srcs, lands):
    x, y, c, own, sib, chips, chip_idx = _place()
    (w, conv), (w_land, conv_land) = srcs, lands
    hw = w.shape[1] // 2
    half = lambda ref: ref.at[:, pl.ds(pl.multiple_of(c * hw, LANES), hw)]
    plan = []
    for j, chip in enumerate(chips):
        plan.append((half(w), half(w_land.at[own]), (*chip, c), half(w_land.at[chip_idx[j]])))
        plan.append((conv, conv_land.at[own], (*chip, c), conv_land.at[chip_idx[j]]))
    plan.append((w, w_land.at[own], sib, w_land.at[own]))
    plan.append((conv, conv_land.at[own], sib, conv_land.at[own]))
    return plan


def _forward_halves(landed):
    hw = landed.shape[2] // 2

    def body(in_ref, out_ref, send_sems, recv_sems):
        x, y, c, own, sib, chips, chip_idx = _place()
        half = lambda ref, hc: ref.at[:, pl.ds(pl.multiple_of(hc * hw, LANES), hw)]
        sent = [_remote(half(out_ref.at[chip_idx[j]], c), half(out_ref.at[chip_idx[j]], c),
                        send_sems.at[j], recv_sems.at[j], sib) for j in range(3)]
        for cp in sent:
            cp.start()
        for j in range(3):
            other = half(out_ref.at[chip_idx[j]], 1 - c)
            _remote(other, other, send_sems.at[j], recv_sems.at[j], sib).wait_recv()
        for cp in sent:
            cp.wait_send()

    return pl.pallas_call(
        body, name="gather_in_forward", out_shape=jax.ShapeDtypeStruct(landed.shape, landed.dtype),
        in_specs=[HBM_SPEC], out_specs=HBM_SPEC, input_output_aliases={0: 0},
        scratch_shapes=[pltpu.SemaphoreType.DMA((3,)), pltpu.SemaphoreType.DMA((3,))],
    )(landed)


def _split_start(name, plan_fn, srcs, land_shapes, n_copies, after=None):
    n = len(srcs)
    extra = [] if after is None else [after]

    def body(*refs):
        src_refs, land_refs = refs[:n], refs[n:2 * n]
        send_sems, recv_sems = refs[2 * n + len(extra)], refs[2 * n + len(extra) + 1]
        token = refs[-1]
        for k, (src, dst, to, _) in enumerate(plan_fn(src_refs, land_refs)):
            _remote(src, dst, send_sems.at[k], recv_sems.at[k], to).start()
        token[...] = jnp.zeros_like(token)

    lands = [pltpu.with_memory_space_constraint(lax.empty(s.shape, s.dtype), pltpu.HBM) for s in land_shapes]
    srcs = [pltpu.with_memory_space_constraint(s, pltpu.HBM) for s in srcs]
    out_shape = ([pltpu.SemaphoreType.DMA((n_copies,)), pltpu.SemaphoreType.DMA((n_copies,))]
                 + [pltpu.HBM(s.shape, s.dtype) for s in srcs] + [pltpu.HBM(s.shape, s.dtype) for s in land_shapes]
                 + [jax.ShapeDtypeStruct((8, LANES), F32)])
    res = pl.pallas_call(
        body, name=name, out_shape=out_shape,
        in_specs=[HBM_SPEC] * (2 * n) + [ANY_SPEC] * len(extra),
        out_specs=[SEM_SPEC, SEM_SPEC] + [HBM_SPEC] * (2 * n) + [pl.BlockSpec(memory_space=pltpu.VMEM)],
        input_output_aliases={i: 2 + i for i in range(2 * n)},
        compiler_params=pltpu.CompilerParams(has_side_effects=DATAFLOW),
    )(*srcs, *lands, *extra)
    return dict(sems=res[:2], srcs=res[2:2 + n], lands=res[2 + n:2 + 2 * n], token=res[-1], n=n)


def _split_wait(name, plan_fn, started, after):
    n = started["n"]

    def body(*refs):
        src_refs, land_refs = refs[:n], refs[n:2 * n]
        send_sems, recv_sems = refs[2 * n], refs[2 * n + 1]
        for k, (src, _, to, landed) in enumerate(plan_fn(src_refs, land_refs)):
            copy = _remote(src, landed, send_sems.at[k], recv_sems.at[k], to)
            copy.wait_send()
            copy.wait_recv()

    srcs, lands = started["srcs"], started["lands"]
    after = list(after) if isinstance(after, (list, tuple)) else [after]
    res = pl.pallas_call(
        body, name=name,
        out_shape=[pltpu.HBM(s.shape, s.dtype) for s in srcs] + [pltpu.HBM(s.shape, s.dtype) for s in lands],
        in_specs=[HBM_SPEC] * (2 * n) + [SEM_SPEC, SEM_SPEC] + [ANY_SPEC] * len(after),
        out_specs=[HBM_SPEC] * (2 * n),
        input_output_aliases={i: i for i in range(2 * n)},
        compiler_params=pltpu.CompilerParams(has_side_effects=DATAFLOW),
    )(*srcs, *lands, *started["sems"], *after)
    started["srcs_after"] = res[:n]
    return res[n:]


def _add_half(stack, landed, place, name):
    _, rows, h = landed.shape

    def body(place_ref, a_ref, b_ref, o_ref, own_ref):
        part = (a_ref[...].astype(F32) + b_ref[...].astype(F32)).astype(o_ref.dtype)
        o_ref[...] = part

        @pl.when(pl.program_id(0) == place_ref[1])
        def _():
            own_ref[...] = part[0]

    return pl.pallas_call(
        body, name=name,
        out_shape=[jax.ShapeDtypeStruct(landed.shape, BF16), jax.ShapeDtypeStruct((rows, h), BF16)],
        grid_spec=pltpu.PrefetchScalarGridSpec(
            num_scalar_prefetch=1, grid=(N_CHIPS,),
            in_specs=[pl.BlockSpec((1, rows, h), lambda j, p: (j, 0, p[0])),
                      pl.BlockSpec((1, rows, h), lambda j, p: (j, 0, 0))],
            out_specs=[pl.BlockSpec((1, rows, h), lambda j, p: (j, 0, 0)),
                       pl.BlockSpec((rows, h), lambda j, p: (0, 0))]),
        compiler_params=_params(("arbitrary",)),
    )(place, stack, landed)


def _sum_partials(own_part, landed, name, untiled_rows=False):
    _, h, cols = landed.shape
    tc = LANES if untiled_rows else cols

    def body(own_ref, a_ref, o_ref):
        acc = own_ref[...].astype(F32)
        for s in range(3):
            acc = acc + a_ref[s].astype(F32)
        if untiled_rows:
            o_ref[:, 0, :] = acc
        else:
            o_ref[...] = acc

    if untiled_rows:
        out_shape, out_spec = jax.ShapeDtypeStruct((h, 1, cols), F32), pl.BlockSpec((h, 1, tc), lambda i: (0, 0, i))
    else:
        out_shape, out_spec = jax.ShapeDtypeStruct((h, cols), F32), pl.BlockSpec((h, tc), lambda i: (0, i))
    return pl.pallas_call(
        body, name=name, out_shape=out_shape, grid=(cols // tc,),
        in_specs=[pl.BlockSpec((h, tc), lambda i: (0, i)), pl.BlockSpec((3, h, tc), lambda i: (0, 0, i))],
        out_specs=out_spec, compiler_params=_params(("arbitrary",)),
    )(own_part, landed)


def _share_halves(halves, name):
    n = len(halves)

    def body(*refs):
        ins, outs = refs[:n], refs[n:2 * n]
        send_sems, recv_sems = refs[2 * n:]
        x, y, c, own, sib, chips, chip_idx = _place()
        cps = [_remote(ins[i], outs[i], send_sems.at[i], recv_sems.at[i], sib) for i in range(n)]
        for cp in cps:
            cp.start()
        for cp in cps:
            cp.wait()

    return pl.pallas_call(
        body, name=name,
        out_shape=[jax.ShapeDtypeStruct(p.shape, p.dtype) for p in halves],
        in_specs=[HBM_SPEC] * n, out_specs=[HBM_SPEC] * n,
        scratch_shapes=[pltpu.SemaphoreType.DMA((n,)), pltpu.SemaphoreType.DMA((n,))],
    )(*halves)


def _allreduce_small(packed):
    rows = packed.shape[0]
    n_dev = 8

    def body(in_ref, out_ref, gath, send_sems, recv_sems):
        x, y, c = lax.axis_index("x"), lax.axis_index("y"), lax.axis_index("c")
        me = 4 * x + 2 * y + c
        gath[me] = in_ref[...]
        cps = []
        for k in range(1, n_dev):
            fx, fy, fc = (k >> 2) & 1, (k >> 1) & 1, k & 1
            to = (x ^ fx, y ^ fy, c ^ fc)
            cps.append(_remote(in_ref, gath.at[me], send_sems.at[k - 1], recv_sems.at[k - 1], to))
        for cp in cps:
            cp.start()
        for k in range(1, n_dev):
            fx, fy, fc = (k >> 2) & 1, (k >> 1) & 1, k & 1
            src = 4 * (x ^ fx) + 2 * (y ^ fy) + (c ^ fc)
            slot = gath.at[src]
            _remote(slot, slot, send_sems.at[k - 1], recv_sems.at[k - 1], (x, y, c)).wait_recv()
        for cp in cps:
            cp.wait_send()
        acc = gath[0]
        for d in range(1, n_dev):
            acc = acc + gath[d]
        out_ref[...] = acc

    vm = pl.BlockSpec(memory_space=pltpu.VMEM)
    return pl.pallas_call(
        body, name="allreduce_small", out_shape=jax.ShapeDtypeStruct(packed.shape, F32),
        in_specs=[vm], out_specs=vm,
        scratch_shapes=[pltpu.VMEM((n_dev, rows, LANES), F32),
                        pltpu.SemaphoreType.DMA((n_dev - 1,)), pltpu.SemaphoreType.DMA((n_dev - 1,))],
    )(packed)


def _adam(col, w, g, m, v):
    m2 = ADAM_B1 * m + (1.0 - ADAM_B1) * g
    v2 = ADAM_B2 * v + (1.0 - ADAM_B2) * (g * g)
    m_hat = m2 / (1.0 - ADAM_B1 ** ADAM_STEP)
    v_hat = v2 / (1.0 - ADAM_B2 ** ADAM_STEP)
    delta = -ADAM_LR * (m_hat / (jnp.sqrt(v_hat) + ADAM_EPS) + ADAM_WD * w)
    return delta, m2, v2


def _adam_call(w, g, m, v, name):
    rows, cols = w.shape
    tm = rows
    for cand in (256, 352, 176, 128, 64, 48, 16, 8):
        if rows % cand == 0:
            tm = cand
            break
    return _tiles(_adam, name=name, rows=rows, tm=tm,
                  row_ins=[(w, cols, 0), (g, cols, 0), (m, cols, 0), (v, cols, 0)],
                  row_outs=[(cols, F32)] * 3)


def _adam_big(w, g_mine, g_other, m, v, place, name):
    rows, cols = w.shape
    tc = 256
    nt = cols // 2 // tc

    def body(place_ref, w_ref, gm_ref, go_ref, m_ref, v_ref, g_out, d_out, m_out, v_out):
        g = jnp.where(pl.program_id(0) == place_ref[0], gm_ref[...], go_ref[...])
        d, m2, v2 = _adam(None, w_ref[...], g, m_ref[...], v_ref[...])
        g_out[...] = g
        d_out[...] = d
        m_out[...] = m2
        v_out[...] = v2

    full = pl.BlockSpec((rows, tc), lambda hh, i, p: (0, hh * nt + i))
    half = pl.BlockSpec((rows, tc), lambda hh, i, p: (0, i))
    return pl.pallas_call(
        body, name=name, out_shape=[jax.ShapeDtypeStruct(w.shape, F32)] * 4,
        grid_spec=pltpu.PrefetchScalarGridSpec(
            num_scalar_prefetch=1, grid=(2, nt),
            in_specs=[full, half, half, full, full], out_specs=[full] * 4),
        compiler_params=_params(("arbitrary", "arbitrary")),
    )(place, w, g_mine, g_other, m, v)


def _adam_untiled_rows(w, g_mine, g_other, m, v, place, name):
    rows, _, cols = w.shape
    tc = 256
    nt = cols // 2 // tc
    rb = next(r for r in (206, 128, 103, rows) if rows % r == 0)

    def body(place_ref, w_ref, gm_ref, go_ref, m_ref, v_ref, g_out, d_out, m_out, v_out):
        g = jnp.where(pl.program_id(0) == place_ref[0], gm_ref[...], go_ref[...])
        d, m2, v2 = _adam(None, w_ref[...], g, m_ref[...], v_ref[...])
        g_out[...] = g
        d_out[...] = d
        m_out[...] = m2
        v_out[...] = v2

    full = pl.BlockSpec((rb, 1, tc), lambda hh, i, r, p: (r, 0, hh * nt + i))
    half = pl.BlockSpec((rb, 1, tc), lambda hh, i, r, p: (r, 0, i))
    return pl.pallas_call(
        body, name=name, out_shape=[jax.ShapeDtypeStruct(w.shape, F32)] * 4,
        grid_spec=pltpu.PrefetchScalarGridSpec(
            num_scalar_prefetch=1, grid=(2, nt, rows // rb),
            in_specs=[full, half, half, full, full], out_specs=[full] * 4),
        compiler_params=_params(("arbitrary", "arbitrary", "arbitrary")),
    )(place, w, g_mine, g_other, m, v)


def _pack(arrays, zero=None):
    flat = []
    for a in arrays:
        a = a.reshape(-1).astype(F32)
        if zero is not None:
            a = a + zero
        flat.append(jnp.pad(a, (0, (-a.size) % LANES)))
    out = jnp.concatenate(flat)
    out = jnp.pad(out, (0, (-out.size) % (8 * LANES)))
    return out.reshape(-1, LANES)


def _unpack(packed, shapes):
    flat = packed.reshape(-1)
    out, off = [], 0
    for s in shapes:
        size = int(np.prod(s))
        out.append(flat[off:off + size].reshape(s))
        off += size + (-size) % LANES
    return out


def kernel(x, norm1_w, w_in, gdn_conv_w, gdn_A_log, gdn_dt_bias, gdn_out_norm_w, fox_f_bias, fox_q_norm_w, fox_k_norm_w, w_out, norm2_w, w_ffn_gate, w_ffn_up, w_ffn_down, final_norm_w, loss_target, m_norm1_w, m_w_in, m_gdn_conv_w, m_gdn_A_log, m_gdn_dt_bias, m_gdn_out_norm_w, m_fox_f_bias, m_fox_q_norm_w, m_fox_k_norm_w, m_w_out, m_norm2_w, m_w_ffn_gate, m_w_ffn_up, m_w_ffn_down, m_final_norm_w, v_norm1_w, v_w_in, v_gdn_conv_w, v_gdn_A_log, v_gdn_dt_bias, v_gdn_out_norm_w, v_fox_f_bias, v_fox_q_norm_w, v_fox_k_norm_w, v_w_out, v_norm2_w, v_w_ffn_gate, v_w_ffn_up, v_w_ffn_down, v_final_norm_w):
    cx, cy, cc = lax.axis_index("x"), lax.axis_index("y"), lax.axis_index("c")
    own = 2 * cx + cy
    place = jnp.stack([cc, own]).astype(jnp.int32)

    names = ["w_in", "w_out", "w_gate", "w_up", "w_down"]
    is_t = [True, False, True, True, False]
    to_t = lambda a, t: a[0].T if t else a[0]
    from_t = lambda a, t: (a.T if t else a)[None]
    big_w = [to_t(a, t) for a, t in zip([w_in, w_out, w_ffn_gate, w_ffn_up, w_ffn_down], is_t)]
    big_m = [to_t(a, t) for a, t in zip([m_w_in, m_w_out, m_w_ffn_gate, m_w_ffn_up, m_w_ffn_down], is_t)]
    big_v = [to_t(a, t) for a, t in zip([v_w_in, v_w_out, v_w_ffn_gate, v_w_ffn_up, v_w_ffn_down], is_t)]
    shards = [big_w[0].astype(BF16)]
    small_w = [norm1_w, gdn_conv_w, gdn_A_log, gdn_dt_bias, gdn_out_norm_w, fox_f_bias, fox_q_norm_w,
               fox_k_norm_w, norm2_w, final_norm_w]
    small_m = [m_norm1_w, m_gdn_conv_w, m_gdn_A_log, m_gdn_dt_bias, m_gdn_out_norm_w, m_fox_f_bias,
               m_fox_q_norm_w, m_fox_k_norm_w, m_norm2_w, m_final_norm_w]
    small_v = [v_norm1_w, v_gdn_conv_w, v_gdn_A_log, v_gdn_dt_bias, v_gdn_out_norm_w, v_fox_f_bias,
               v_fox_q_norm_w, v_fox_k_norm_w, v_norm2_w, v_final_norm_w]
    first = _split_start("gather_in_start", _in_proj_plan, [shards[0], gdn_conv_w[0]],
                         [jax.ShapeDtypeStruct((N_CHIPS,) + shards[0].shape, BF16),
                          jax.ShapeDtypeStruct((N_CHIPS, CONV_K, 3 * WIDTH // N_CHIPS), F32)],
                         n_copies=8)
    small_packed = [_pack(p, first["token"][0, 0]) for p in (small_w, small_m, small_v)]
    shards += [(w + first["token"][0, 0]).astype(BF16) for w in big_w[1:]]
    rest = {}

    def first_weights(after):
        w_in_g, conv_g = _split_wait("gather_in_wait", _in_proj_plan, first, [after] + small_packed)
        w_in_g = _forward_halves(w_in_g)
        rest.update(_split_start("gather_rest_start", _gather_plan, shards[1:],
                                 [jax.ShapeDtypeStruct((N_CHIPS,) + s.shape, BF16) for s in shards[1:]],
                                 n_copies=4 * len(shards[1:]), after=w_in_g))
        w_cat = _cat_weights(w_in_g.reshape(D_IN, D_MODEL))
        return w_cat + rest["token"][0, 0].astype(BF16), conv_g.transpose(1, 0, 2).reshape(CONV_K, 3 * WIDTH)

    def late_weights(after):
        w_out_g, w_gate_g, w_up_g, w_down_g = _split_wait("gather_rest_wait", _gather_plan, rest, after)
        return w_out_g.reshape(D_MODEL, D_MODEL), w_gate_g, w_up_g, w_down_g

    def start_reduction(stacks, landed, nms, tag):
        added = [_add_half(s, l, place, "rs_add_" + nm) for s, l, nm in zip(stacks, landed, nms)]
        parts = [a[0] for a in added]
        started = _split_start("exchange_" + tag + "_start", _exchange_plan, parts,
                               [jax.ShapeDtypeStruct((3,) + p.shape[1:], p.dtype) for p in parts],
                               n_copies=3 * len(parts))
        return dict(own=[a[1] for a in added], started=started, tag=tag, names=nms)

    def finish_reduction(red, after, updates):
        landed = _split_wait("exchange_" + red["tag"] + "_wait", _exchange_plan, red["started"], after)
        halves = [_sum_partials(o, p, "rs_sum_" + nm, untiled_rows=nm == "w_in")
                  for o, p, nm in zip(red["own"], landed, red["names"])]
        others = _share_halves(halves, "rs_share_" + red["tag"])
        return [upd(gm, go) for upd, gm, go in zip(updates, halves, others)]

    def transport_update(b):
        def upd(gm, go):
            res = _adam_big(big_w[b], gm, go, big_m[b], big_v[b], place, "adam_" + names[b])
            early_done.append(res[1])
            return [from_t(a, is_t[b]) for a in res]
        return upd

    early_done = []

    def w_in_update(gm, go):
        rows3 = lambda a: jnp.transpose(a, (2, 0, 1))
        res = _adam_untiled_rows(rows3(w_in), gm, go, rows3(m_w_in), rows3(v_w_in), place, "adam_w_in")
        return [jnp.transpose(a, (1, 2, 0)) for a in res]

    early = {}

    def early_grads_ready(g_out, g_gate, g_up, g_down):
        stacks = [g_out.reshape(N_CHIPS, D_MODEL // N_CHIPS, D_MODEL), g_gate, g_up, g_down]
        swap = _split_start("swap_early_start", _swap_plan, stacks,
                            [jax.ShapeDtypeStruct(s.shape[:2] + (s.shape[2] // 2,), s.dtype) for s in stacks],
                            n_copies=len(stacks))
        early.update(stacks=stacks, swap=swap)
        return swap["token"][0, 0]

    def early_grads_continue(after):
        landed = _split_wait("swap_early_wait", _swap_plan, early["swap"], after)
        early.update(start_reduction(early["swap"]["srcs_after"], landed, names[1:], "early"))
        return early["started"]["token"][0, 0]

    grad_x, g_cat, _, _, _, _, small = _local_step(
        x[0], loss_target[0], norm1_w + first["token"][0, 0], gdn_A_log[0], gdn_dt_bias[0],
        gdn_out_norm_w[0], fox_f_bias[0], fox_q_norm_w[0], fox_k_norm_w[0], norm2_w, final_norm_w.reshape(1, -1),
        first_weights, late_weights, early_grads_ready, early_grads_continue)

    g_in_stack = _uncat_grad(g_cat).reshape(N_CHIPS, D_IN // N_CHIPS, D_MODEL)
    swap_in = _split_start("swap_w_in_start", _swap_plan, [g_in_stack],
                           [jax.ShapeDtypeStruct((N_CHIPS, D_IN // N_CHIPS, D_MODEL // 2), F32)],
                           n_copies=1)

    order = ["norm1_w", "conv_w", "a_log", "dt_bias", "out_norm_w", "f_bias", "q_norm_w", "k_norm_w",
             "norm2_w", "final_w"]
    red = _allreduce_small(_pack([small[k] for k in order] + [small["loss"]], swap_in["token"][0, 0]))
    red_shapes = [(1, D_MODEL), (CONV_K, 3 * WIDTH), (1, HEADS), (1, HEADS), (1, HEAD_DIM), (1, HEADS),
                  (1, HEAD_DIM), (1, HEAD_DIM), (1, D_MODEL), (D_MODEL,), ()]
    red_list = _unpack(red, red_shapes)
    loss = red_list[-1]
    small_g = dict(zip(order, red_list[:-1]))
    shard_cols = 3 * WIDTH // N_CHIPS
    small_g["conv_w"] = lax.dynamic_slice_in_dim(small_g["conv_w"], own * shard_cols, shard_cols, axis=1)[None]
    small_gl = [small_g[k].reshape(w.shape) for k, w in zip(order, small_w)]
    s_delta, s_m, s_v = _adam_call(small_packed[0], _pack(small_gl), small_packed[1], small_packed[2], "adam_small")
    landed_in = _split_wait("swap_w_in_wait", _swap_plan, swap_in, s_delta)
    late = start_reduction(swap_in["srcs_after"], landed_in, names[:1], "w_in")
    big_upd = finish_reduction(early, late["started"]["token"], [transport_update(b) for b in range(1, 5)])
    big_upd = finish_reduction(late, early_done, [w_in_update]) + big_upd
    shapes = [w.shape for w in small_w]
    s_delta, s_m, s_v = _unpack(s_delta, shapes), _unpack(s_m, shapes), _unpack(s_v, shapes)

    big_pos = {1: 0, 9: 1, 11: 2, 12: 3, 13: 4}
    small_pos = {0: 0, 2: 1, 3: 2, 4: 3, 5: 4, 6: 5, 7: 6, 8: 7, 10: 8, 14: 9}
    grads, deltas, new_m, new_v = [], [], [], []
    for pos in range(15):
        if pos in big_pos:
            b = big_pos[pos]
            g, d, m2, v2 = big_upd[b]
            grads.append(g)
            deltas.append(d)
            new_m.append(m2)
            new_v.append(v2)
        else:
            s = small_pos[pos]
            grads.append(small_gl[s])
            deltas.append(s_delta[s])
            new_m.append(s_m[s])
            new_v.append(s_v[s])
    return (loss, grad_x[None], *grads, *deltas, *new_m, *new_v)
```

```python
import jax
import jax.numpy as jnp
import numpy as np
from jax import lax
from jax.experimental import pallas as pl
from jax.experimental.pallas import tpu as pltpu

F32 = jnp.float32
BF16 = jnp.bfloat16

D_MODEL = 1024
HEADS = 8
HEAD_DIM = 64
PAIRS = HEADS // 2
WIDTH = HEADS * HEAD_DIM
CHUNK = 64
CONV_K = 4
D_FF = 2816
FF_SHARD = D_FF // 4
EPS = 1e-6
SCALE = HEAD_DIM ** -0.5
LANES = 128
N_CHIPS = 4
D_IN = 4120
D_CAT = 4224
COL_SMALL = 4096 // LANES

ADAM_LR = 0.001
ADAM_B1 = 0.9
ADAM_B2 = 0.999
ADAM_EPS = 1e-08
ADAM_WD = 0.01
ADAM_STEP = 10

VMEM_LIMIT = 56 * 1024 * 1024
MESH = pl.DeviceIdType.MESH
HIGHEST = lax.Precision.HIGHEST


def _params(sem):
    return pltpu.CompilerParams(dimension_semantics=sem, vmem_limit_bytes=VMEM_LIMIT)


_CONTRACT = {"nn": ((1,), (0,)), "nt": ((1,), (1,)), "tn": ((0,), (0,))}


def _mm(a, b, *, dims, name, out_dtype=F32, add=None, tm=1024, tn=512, tk=512):
    if dims == "nn":
        (m, k), (k2, n) = a.shape, b.shape
    elif dims == "nt":
        (m, k), (n, k2) = a.shape, b.shape
    else:
        (k, m), (k2, n) = a.shape, b.shape
    assert k == k2, (a.shape, b.shape, dims)
    tm, tn, tk = min(tm, m), min(tn, n), min(tk, k)
    assert m % tm == 0 and n % tn == 0 and k % tk == 0, (m, n, k, tm, tn, tk)
    nk = k // tk
    a_spec = (pl.BlockSpec((tk, tm), lambda i, j, kk: (kk, i)) if dims == "tn"
              else pl.BlockSpec((tm, tk), lambda i, j, kk: (i, kk)))
    b_spec = (pl.BlockSpec((tn, tk), lambda i, j, kk: (j, kk)) if dims == "nt"
              else pl.BlockSpec((tk, tn), lambda i, j, kk: (kk, j)))
    o_spec = pl.BlockSpec((tm, tn), lambda i, j, kk: (i, j))
    contract = (_CONTRACT[dims], ((), ()))
    has_add = add is not None

    def body(*refs):
        a_ref, b_ref = refs[:2]
        add_ref = refs[2] if has_add else None
        o_ref = refs[3] if has_add else refs[2]
        part = lax.dot_general(a_ref[...].astype(BF16), b_ref[...].astype(BF16), contract,
                               preferred_element_type=F32)

        def finish(r):
            if has_add:
                r = r + add_ref[...].astype(F32)
            o_ref[...] = r.astype(out_dtype)

        if nk == 1:
            finish(part)
            return
        acc = refs[-1]
        kk = pl.program_id(2)

        @pl.when(kk == 0)
        def _():
            acc[...] = part

        @pl.when(kk > 0)
        def _():
            acc[...] += part

        @pl.when(kk == nk - 1)
        def _():
            finish(acc[...])

    ins = [a, b] + ([add] if has_add else [])
    in_specs = [a_spec, b_spec] + ([o_spec] if has_add else [])
    return pl.pallas_call(
        body, name=name, grid=(m // tm, n // tn, nk),
        in_specs=in_specs, out_specs=o_spec,
        out_shape=jax.ShapeDtypeStruct((m, n), out_dtype),
        scratch_shapes=[pltpu.VMEM((tm, tn), F32)] if nk > 1 else [],
        compiler_params=_params(("parallel", "parallel", "arbitrary")),
    )(*ins)


def _mm_blocks(a, b, *, name, grid, a_spec, b_spec, o_spec, out_shape, dims, n_sum=0, add=None, add_spec=None,
               epilogue=None, extra=(), n_acc=0):
    contract = (_CONTRACT[dims], ((), ()))
    has_add = add is not None
    n_in = 2 + has_add + len(extra)

    def body(*refs):
        a_ref, b_ref = refs[:2]
        dot = lambda x, y: lax.dot_general(x.astype(BF16), y.astype(BF16), contract, preferred_element_type=F32)
        if n_sum:
            r = dot(a_ref[0], b_ref[0])
            for s in range(1, n_sum):
                r = r + dot(a_ref[s], b_ref[s])
        else:
            r = dot(a_ref[...], b_ref[...])
        if has_add:
            r = r + refs[2][...].astype(F32)
        if epilogue is None:
            refs[-1][...] = r.astype(refs[-1].dtype)
        else:
            outs = epilogue(r, *[e[...] for e in refs[2 + has_add:n_in]])
            out_refs = refs[n_in:]
            n_plain = len(out_refs) - n_acc
            for o_ref, val in zip(out_refs[:n_plain], outs):
                o_ref[...] = val.astype(o_ref.dtype)
            if n_acc:
                @pl.when(pl.program_id(0) == 0)
                def _():
                    for o_ref in out_refs[n_plain:]:
                        o_ref[...] = jnp.zeros_like(o_ref)
                for o_ref, val in zip(out_refs[n_plain:], outs[n_plain:]):
                    o_ref[...] += val

    ins = [a, b] + ([add] if has_add else []) + [e[0] for e in extra]
    in_specs = [a_spec, b_spec] + ([add_spec] if has_add else []) + [e[1] for e in extra]
    sem = ("arbitrary" if n_acc else "parallel",) * len(grid)
    return pl.pallas_call(
        body, name=name, grid=grid, in_specs=in_specs, out_specs=o_spec, out_shape=out_shape,
        compiler_params=_params(sem),
    )(*ins)


def _tiles(fn, *, name, rows, tm, ncol=1, row_ins=(), col_consts=(), full_consts=(),
           row_outs=(), acc_outs=()):
    nt = rows // tm
    assert rows % tm == 0
    n_full, n_col, n_row = len(full_consts), len(col_consts), len(row_ins)
    n_ro, n_acc = len(row_outs), len(acc_outs)

    def body(*refs):
        ins = refs[:n_full + n_col + n_row]
        outs = refs[n_full + n_col + n_row:]
        i = pl.program_id(1)
        res = fn(pl.program_id(0), *[r[...] for r in ins])
        for r, v in zip(outs[:n_ro], res[:n_ro]):
            r[...] = v.astype(r.dtype)
        if n_acc:
            @pl.when(i == 0)
            def _():
                for r in outs[n_ro:]:
                    r[...] = jnp.zeros_like(r)
            for r, v in zip(outs[n_ro:], res[n_ro:]):
                r[...] += v

    in_specs = [pl.BlockSpec(a.shape, lambda j, i, nd=a.ndim: (0,) * nd) for a in full_consts]
    in_specs += [pl.BlockSpec((nr, w), lambda j, i, o=o: (0, o + j)) for (_, nr, w, o) in col_consts]
    in_specs += [pl.BlockSpec((tm, w), lambda j, i, o=o: (i, o + j)) for (_, w, o) in row_ins]
    out_specs = [pl.BlockSpec((tm, w), lambda j, i: (i, j)) for (w, _) in row_outs]
    out_specs += [pl.BlockSpec((nr, w), lambda j, i: (0, j)) for (nr, w) in acc_outs]
    out_shape = [jax.ShapeDtypeStruct((rows, w * ncol), dt) for (w, dt) in row_outs]
    out_shape += [jax.ShapeDtypeStruct((nr, w * ncol), F32) for (nr, w) in acc_outs]
    args = list(full_consts) + [c[0] for c in col_consts] + [r[0] for r in row_ins]
    out = pl.pallas_call(
        body, name=name, grid=(ncol, nt), in_specs=in_specs, out_specs=out_specs, out_shape=out_shape,
        compiler_params=_params(("parallel", "arbitrary")),
    )(*args)
    return out


def _rms(x, w):
    return x * lax.rsqrt(jnp.mean(x * x, axis=-1, keepdims=True) + EPS) * w


def _lane_lo(shape):
    return lax.broadcasted_iota(jnp.int32, shape, len(shape) - 1) < HEAD_DIM


def _pair_sum(x):
    lo = _lane_lo(x.shape)
    s0 = jnp.sum(jnp.where(lo, x, 0.0), axis=-1, keepdims=True)
    s1 = jnp.sum(jnp.where(lo, 0.0, x), axis=-1, keepdims=True)
    return jnp.where(lo, s0, s1)


def _head_col(x, lo, h):
    keep = lo if h == 0 else jnp.logical_not(lo)
    return jnp.max(jnp.where(keep, x, -jnp.inf), axis=-1, keepdims=True)


def _softplus(x):
    return jnp.maximum(x, 0.0) + jnp.log1p(jnp.exp(-jnp.abs(x)))


def _silu(x):
    return x * jax.nn.sigmoid(x)


def _dot(a, b, contract):
    return lax.dot_general(a.astype(BF16), b.astype(BF16), (contract, ((), ())),
                           preferred_element_type=F32)


def _dot32(a, b, contract):
    return lax.dot_general(a, b, (contract, ((), ())), precision=HIGHEST, preferred_element_type=F32)


def _bd(y):
    yy = jnp.concatenate([y, y], axis=0)
    r = lax.broadcasted_iota(jnp.int32, yy.shape, 0) < HEAD_DIM
    c = lax.broadcasted_iota(jnp.int32, yy.shape, 1) < HEAD_DIM
    return jnp.where(r == c, yy, 0.0)


def _pp(x, y):
    return _dot(x, _bd(y), _CONTRACT["nn"])


def _pp_nt(x, y):
    return _dot(x, _bd(y), _CONTRACT["nt"])


def _pp_tn(x, y):
    full = _dot(x, y, _CONTRACT["tn"])
    return jnp.where(_lane_lo((HEAD_DIM, LANES)), full[:HEAD_DIM], full[HEAD_DIM:])


def _gdn_masks():
    row = lax.broadcasted_iota(jnp.int32, (CHUNK, LANES), 0)
    col = lax.broadcasted_iota(jnp.int32, (CHUNK, LANES), 1) % HEAD_DIM
    return row, col


def _interleave(chains):
    live = list(chains)
    while live:
        for g in list(live):
            try:
                next(g)
            except StopIteration:
                live.remove(g)


def _gdn_forward(qkv, betax, gcx, grow, rows):
    nchunk = rows // CHUNK

    def body(q_ref, k_ref, v_ref, bx_ref, gx_ref, gr_ref, o_ref, ss_ref, ts_ref, state):
        n = pl.program_id(0)

        @pl.when(n == 0)
        def _():
            state[...] = jnp.zeros_like(state)

        row, col = _gdn_masks()
        incl, strict = col <= row, col < row

        def chain(p):
            lanes = pl.ds(p * LANES, LANES)
            q, k, v, bx, gx = q_ref[:, lanes], k_ref[:, lanes], v_ref[:, lanes], bx_ref[:, lanes], gx_ref[:, lanes]
            gr = gr_ref[0, p]
            glast = gx_ref[pl.ds(CHUNK - 1, 1), lanes]
            s = state[p]
            dm = jnp.where(incl, jnp.exp(jnp.minimum(gx - gr, 0.0)), 0.0)
            kb, vb, eg, qs = k * bx, v * bx, jnp.exp(gx), q * SCALE
            yield
            big_g, big_p = _pp_nt(kb, k), _pp_nt(qs, k)
            yield
            x = -jnp.where(strict, big_g * dm, 0.0)
            att = jnp.where(incl, big_p * dm, 0.0)
            tm = jnp.where(row == col, 1.0, 0.0) + x
            x = _pp(x, x)
            yield
            for _ in range(4):
                step, x = _pp(tm, x), _pp(x, x)
                yield
                tm = tm + step
            tm = tm + _pp(tm, x)
            yield
            u, w = _pp(tm, vb), _pp(tm, kb * eg)
            yield
            ws, qgs = _pp(w, s), _pp(qs * eg, s)
            yield
            vn = u - ws
            kd = k * jnp.exp(glast - gx)
            avn, upd = _pp(att, vn), _pp_tn(kd, vn)
            yield
            ss_ref[0, p] = s
            ts_ref[0, p] = tm
            o_ref[:, lanes] = qgs + avn
            state[p] = s * jnp.exp(glast) + upd

        _interleave([chain(p) for p in range(PAIRS)])

    blk = lambda j: pl.BlockSpec((CHUNK, WIDTH), lambda n, j=j: (n, j))
    sv = pl.BlockSpec((1, PAIRS, CHUNK, LANES), lambda n: (n, 0, 0, 0))
    return pl.pallas_call(
        body, name="gdn_fwd", grid=(nchunk,),
        in_specs=[blk(0), blk(1), blk(2), blk(0), blk(0),
                  pl.BlockSpec((1, PAIRS, 1, LANES), lambda n: (n, 0, 0, 0))],
        out_specs=[blk(0), sv, sv],
        out_shape=[jax.ShapeDtypeStruct((rows, WIDTH), F32),
                   jax.ShapeDtypeStruct((nchunk, PAIRS, CHUNK, LANES), F32),
                   jax.ShapeDtypeStruct((nchunk, PAIRS, CHUNK, LANES), F32)],
        scratch_shapes=[pltpu.VMEM((PAIRS, CHUNK, LANES), F32)],
        compiler_params=_params(("arbitrary",)),
    )(qkv, qkv, qkv, betax, gcx, grow)


def _gdn_backward(qkv, betax, gcx, grow, ssave, tsave, do, rows):
    nchunk = rows // CHUNK

    def body(q_ref, k_ref, v_ref, bx_ref, gx_ref, gr_ref, ss_ref, ts_ref, do_ref,
             dq_ref, dk_ref, dv_ref, dbx_ref, dgx_ref, dgr_ref, dstate):
        n = pl.program_id(0)

        @pl.when(n == 0)
        def _():
            dstate[...] = jnp.zeros_like(dstate)

        row, col = _gdn_masks()
        incl, strict = col <= row, col < row

        def chain(p):
            lanes = pl.ds(p * LANES, LANES)
            q, k, v, bx, gx = q_ref[:, lanes], k_ref[:, lanes], v_ref[:, lanes], bx_ref[:, lanes], gx_ref[:, lanes]
            gr = gr_ref[0, p]
            glast = gx_ref[pl.ds(CHUNK - 1, 1), lanes]
            s, tm, d_o = ss_ref[0, p], ts_ref[0, p], do_ref[:, lanes]
            ds_out = dstate[p]
            dm = jnp.where(incl, jnp.exp(jnp.minimum(gx - gr, 0.0)), 0.0)
            kb, vb, eg, qs = k * bx, v * bx, jnp.exp(gx), q * SCALE
            kbg, qg = kb * eg, qs * eg
            ed = jnp.exp(glast - gx)
            kd = k * ed
            eglast = jnp.exp(glast)
            yield
            big_g, big_p = _pp_nt(kb, k), _pp_nt(qs, k)
            u, w = _pp(tm, vb), _pp(tm, kbg)
            dqg, kds = _pp_nt(d_o, s), _pp(kd, ds_out)
            yield
            low = jnp.where(strict, big_g * dm, 0.0)
            att = jnp.where(incl, big_p * dm, 0.0)
            ws, atd = _pp(w, s), _pp_tn(att, d_o)
            yield
            vn = u - ws
            dvn = kds + atd
            dkd, datt_raw = _pp_nt(vn, ds_out), _pp_nt(d_o, vn)
            dw_neg, dvb = _pp_nt(dvn, s), _pp_tn(tm, dvn)
            dtm_a, wdv = _pp_nt(dvn, vb), _pp_tn(w, dvn)
            qgd = _pp_tn(qg, d_o)
            yield
            datt = jnp.where(incl, datt_raw, 0.0)
            dw = -dw_neg
            dtm_b, dkbg = _pp_nt(dw, kbg), _pp_tn(tm, dw)
            dbig_p = datt * dm
            dqs_a, dk_p = _pp(dbig_p, k), _pp_tn(dbig_p, qs)
            yield
            inner = _pp_tn(tm, dtm_a + dtm_b)
            yield
            dlow = jnp.where(strict, -_pp_nt(inner, tm), 0.0)
            yield
            dbig_g = dlow * dm
            dkb_a, dk_g = _pp(dbig_g, k), _pp_tn(dbig_g, kb)
            yield
            dkb = dkb_a + dkbg * eg
            dqs = dqs_a + dqg * eg
            dk = dk_g + dk_p + dkd * ed + dkb * bx
            z = dlow * low + datt * att
            kdterm = dkd * kd
            dglast = (jnp.sum(ds_out * s, axis=0, keepdims=True) * eglast
                      + jnp.sum(kdterm, axis=0, keepdims=True))
            dgx = dqg * qg + dkbg * kbg - kdterm
            dgx = dgx + jnp.where(col == 0, _pair_sum(z), 0.0)
            dgx = dgx + jnp.where(row == CHUNK - 1, dglast, 0.0)
            dq_ref[:, lanes] = dqs * SCALE
            dk_ref[:, lanes] = dk
            dv_ref[:, lanes] = dvb * bx
            dbx_ref[:, lanes] = dkb * k + dvb * v
            dgx_ref[:, lanes] = dgx
            dgr_ref[0, p] = -jnp.sum(z, axis=0, keepdims=True)
            dstate[p] = ds_out * eglast + qgd - wdv

        _interleave([chain(p) for p in range(PAIRS)])

    last = nchunk - 1
    blk = lambda j: pl.BlockSpec((CHUNK, WIDTH), lambda n, j=j: (last - n, j))
    sv = pl.BlockSpec((1, PAIRS, CHUNK, LANES), lambda n: (last - n, 0, 0, 0))
    gr_spec = pl.BlockSpec((1, PAIRS, 1, LANES), lambda n: (last - n, 0, 0, 0))
    wide = jax.ShapeDtypeStruct((rows, WIDTH), F32)
    return pl.pallas_call(
        body, name="gdn_bwd", grid=(nchunk,),
        in_specs=[blk(0), blk(1), blk(2), blk(0), blk(0), gr_spec, sv, sv, blk(0)],
        out_specs=[blk(0)] * 5 + [gr_spec],
        out_shape=[wide] * 5 + [jax.ShapeDtypeStruct((nchunk, PAIRS, 1, LANES), F32)],
        scratch_shapes=[pltpu.VMEM((PAIRS, CHUNK, LANES), F32)],
        compiler_params=_params(("arbitrary",)),
    )(qkv, qkv, qkv, betax, gcx, grow, ssave, tsave, do)


ATT_TQ = 256


def _att_scores(qh, kt, fk, diag):
    s = _dot(qh, kt, _CONTRACT["nt"]) - fk
    if diag:
        r = lax.broadcasted_iota(jnp.int32, s.shape, 0)
        c = lax.broadcasted_iota(jnp.int32, s.shape, 1)
        s = jnp.where(r >= c, s, -jnp.inf)
    return s


def _head_masks(n):
    lo = _lane_lo((n, LANES))
    return [lo, jnp.logical_not(lo)]


def _attention_forward(fqk, proj, frow, rows):
    tq = tk = min(ATT_TQ, rows)
    nq = rows // tq
    v_off = 3072 // LANES

    def body(q_ref, k_ref, v_ref, fr_ref, o_ref, lse_ref):
        qi = pl.program_id(1)
        q = q_ref[...] * SCALE
        keep_q, keep_k = _head_masks(tq), _head_masks(tk)
        qh = [jnp.where(keep_q[h], q, 0.0).astype(BF16) for h in range(2)]

        def tile(ki, carry, diag):
            k0 = pl.multiple_of(ki * tk, tk)
            kt = k_ref[pl.ds(k0, tk), :].astype(BF16)
            v_t = v_ref[pl.ds(k0, tk), :]
            out = [None, None]

            def chain(h):
                m, l, acc = carry[h]
                vt = jnp.where(keep_k[h], v_t, 0.0).astype(BF16)
                yield
                s = _att_scores(qh[h], kt, fr_ref[0, pl.ds(h, 1), pl.ds(k0, tk)], diag)
                yield
                m_new = jnp.maximum(m, jnp.max(s, axis=-1, keepdims=True))
                p = jnp.exp(s - m_new)
                alpha = jnp.exp(m - m_new)
                l = alpha * l + jnp.sum(p, axis=-1, keepdims=True)
                p_hi = p.astype(BF16)
                p_lo = p - p_hi.astype(F32)
                yield
                out[h] = (m_new, l, alpha * acc + _dot(p_hi, vt, _CONTRACT["nn"]) + _dot(p_lo, vt, _CONTRACT["nn"]))

            _interleave([chain(0), chain(1)])
            return tuple(out)

        one = (jnp.full((tq, 1), -jnp.inf, F32), jnp.zeros((tq, 1), F32), jnp.zeros((tq, LANES), F32))
        carry = lax.fori_loop(0, qi, lambda ki, c: tile(ki, c, False), (one, one))
        (m0, l0, acc0), (m1, l1, acc1) = tile(qi, carry, True)
        o_ref[...] = acc0 / l0 + acc1 / l1
        lse_ref[...] = jnp.where(keep_q[0], m0 + jnp.log(l0), m1 + jnp.log(l1))

    whole = lambda off: pl.BlockSpec((rows, LANES), lambda p, i, off=off: (0, off + p))
    qblk = lambda off: pl.BlockSpec((tq, LANES), lambda p, i, off=off: (i, off + p))
    wide = jax.ShapeDtypeStruct((rows, WIDTH), F32)
    return pl.pallas_call(
        body, name="fox_fwd", grid=(PAIRS, nq),
        in_specs=[qblk(0), whole(PAIRS), whole(v_off), pl.BlockSpec((1, 2, rows), lambda p, i: (p, 0, 0))],
        out_specs=[qblk(0), qblk(0)], out_shape=[wide, wide],
        compiler_params=_params(("parallel", "arbitrary")),
    )(fqk, fqk, proj, frow)


def _attention_backward(fqk, proj, frow, ao, lse, dao, rows):
    tq = tk = min(ATT_TQ, rows)
    nq = rows // tq
    v_off = 3072 // LANES

    def body(q_ref, k_ref, v_ref, fr_ref, o_ref, lse_ref, do_ref, dq_ref, dk_ref, dv_ref, dfr_ref):
        ki = pl.program_id(1)

        @pl.when(ki == 0)
        def _():
            dq_ref[...] = jnp.zeros_like(dq_ref)

        keep_q, keep_k = _head_masks(tq), _head_masks(tk)
        k_t = k_ref[...]
        kt = k_t.astype(BF16)
        vt = v_ref[...].astype(BF16)
        kh = [jnp.where(keep_k[h], k_t, 0.0).astype(BF16) for h in range(2)]
        fk = [fr_ref[0, pl.ds(h, 1), :] for h in range(2)]

        def tile(qi, carry, diag):
            dk, dv, df0, df1 = carry
            rows_q = pl.ds(pl.multiple_of(qi * tq, tq), tq)
            q, d_o, lse_t = q_ref[rows_q, :] * SCALE, do_ref[rows_q, :], lse_ref[rows_q, :]
            delta_x = _pair_sum(d_o.astype(BF16).astype(F32) * o_ref[rows_q, :])
            res = [None, None]

            def chain(h):
                qh = jnp.where(keep_q[h], q, 0.0).astype(BF16)
                doh = jnp.where(keep_q[h], d_o, 0.0).astype(BF16)
                lse_h, delta_h = _head_col(lse_t, keep_q[0], h), _head_col(delta_x, keep_q[0], h)
                yield
                s, dp = _att_scores(qh, kt, fk[h], diag), _dot(doh, vt, _CONTRACT["nt"])
                yield
                p = jnp.exp(s - lse_h)
                ds = p * (dp - delta_h)
                yield
                res[h] = (_dot(p, doh, _CONTRACT["tn"]), _dot(ds, qh, _CONTRACT["tn"]),
                          _dot(ds, kh[h], _CONTRACT["nn"]), jnp.sum(ds, axis=0, keepdims=True))

            _interleave([chain(0), chain(1)])
            (dv0, dk0, dq0, s0), (dv1, dk1, dq1, s1) = res
            dq_ref[rows_q, :] += (dq0 + dq1) * SCALE
            return dk + dk0 + dk1, dv + dv0 + dv1, df0 - s0, df1 - s1

        zero_kv = jnp.zeros((tk, LANES), F32)
        zero_f = jnp.zeros((1, tk), F32)
        carry = tile(ki, (zero_kv, zero_kv, zero_f, zero_f), True)
        dk, dv, df0, df1 = lax.fori_loop(ki + 1, nq, lambda qi, c: tile(qi, c, False), carry)
        dk_ref[...] = dk
        dv_ref[...] = dv.astype(dv_ref.dtype)
        dfr_ref[0, pl.ds(0, 1), :] = df0
        dfr_ref[0, pl.ds(1, 1), :] = df1

    whole = lambda off: pl.BlockSpec((rows, LANES), lambda p, i, off=off: (0, off + p))
    kblk = lambda off: pl.BlockSpec((tk, LANES), lambda p, i, off=off: (i, off + p))
    fr_spec = pl.BlockSpec((1, 2, tk), lambda p, i: (p, 0, i))
    wide = jax.ShapeDtypeStruct((rows, WIDTH), F32)
    return pl.pallas_call(
        body, name="fox_bwd", grid=(PAIRS, nq),
        in_specs=[whole(0), kblk(PAIRS), kblk(v_off), fr_spec, whole(0), whole(0), whole(0)],
        out_specs=[whole(0), kblk(0), kblk(0), fr_spec],
        out_shape=[wide, wide, jax.ShapeDtypeStruct((rows, WIDTH), BF16),
                   jax.ShapeDtypeStruct((PAIRS, 2, rows), F32)],
        compiler_params=_params(("parallel", "arbitrary")),
    )(fqk, fqk, proj, frow, ao, lse, dao)


def _lane_ids(shape):
    return lax.broadcasted_iota(jnp.int32, shape, len(shape) - 1)


def _gates_elem(a_log, dt_bias, f_bias, pre):
    lane = _lane_ids(pre.shape)
    beta = jax.nn.sigmoid(pre)
    g = -jnp.exp(a_log) * _softplus(pre + dt_bias)
    lf = -_softplus(-(pre + f_bias))
    return jnp.where(lane < 8, beta, jnp.where(lane < 16, g, jnp.where(lane < 24, lf, 0.0)))


def _tri_consts():
    r = np.arange(LANES)[:, None]
    c = np.arange(LANES)[None, :]
    full = (c <= r).astype(np.float32)
    chunked = full * ((r // CHUNK) == (c // CHUNK))
    return jnp.asarray(chunked), jnp.asarray(full)


def _cums_fwd(lc, lf, gates):
    rows = gates.shape[0]
    lane = _lane_ids((LANES, LANES))
    carry = jnp.zeros((1, LANES), F32)
    out = []
    for r in range(rows // LANES):
        blk = gates[r * LANES:(r + 1) * LANES]
        gc = _dot32(lc, blk, _CONTRACT["nn"])
        f = _dot32(lf, blk, _CONTRACT["nn"]) + carry
        carry = carry + jnp.sum(blk, axis=0, keepdims=True)
        out.append(jnp.where((lane >= 8) & (lane < 16), gc, jnp.where((lane >= 16) & (lane < 24), f, 0.0)))
    return jnp.concatenate(out, axis=0)


def _cums_bwd(lc, lf, dcums):
    rows = dcums.shape[0]
    lane = _lane_ids((LANES, LANES))
    is_g = (lane >= 8) & (lane < 16)
    is_f = (lane >= 16) & (lane < 24)
    carry = jnp.zeros((1, LANES), F32)
    out = [None] * (rows // LANES)
    for r in reversed(range(rows // LANES)):
        blk = dcums[r * LANES:(r + 1) * LANES]
        dg = jnp.where(is_g, blk, 0.0)
        df = jnp.where(is_f, blk, 0.0)
        out[r] = _dot32(lc, dg, _CONTRACT["tn"]) + _dot32(lf, df, _CONTRACT["tn"]) + carry
        carry = carry + jnp.sum(df, axis=0, keepdims=True)
    return jnp.concatenate(out, axis=0)


def _expand_consts():
    xb = np.zeros((LANES, WIDTH), np.float32)
    xg = np.zeros((LANES, WIDTH), np.float32)
    for h in range(HEADS):
        xb[h, h * HEAD_DIM:(h + 1) * HEAD_DIM] = 1.0
        xg[8 + h, h * HEAD_DIM:(h + 1) * HEAD_DIM] = 1.0
    return jnp.asarray(xb), jnp.asarray(xg)


def _shift_down(x, s):
    if s == 0:
        return x
    row = lax.broadcasted_iota(jnp.int32, x.shape, 0)
    return jnp.where(row >= s, pltpu.roll(x, s, 0), 0.0)


def _shift_up(x, s):
    if s == 0:
        return x
    n = x.shape[0]
    row = lax.broadcasted_iota(jnp.int32, x.shape, 0)
    return jnp.where(row < n - s, pltpu.roll(x, n - s, 0), 0.0)


def _row_of(cw, i):
    row = lax.broadcasted_iota(jnp.int32, cw.shape, 0)
    return jnp.sum(jnp.where(row == i, cw, 0.0), axis=0, keepdims=True)


def _conv(cw, x):
    c = jnp.zeros_like(x)
    for i in range(CONV_K):
        c = c + _row_of(cw, i) * _shift_down(x, CONV_K - 1 - i)
    return c


def _post_conv(is_qk, c):
    s = _silu(c)
    n = s * lax.rsqrt(_pair_sum(s * s) + EPS)
    return jnp.where(is_qk, n, s)


def _gdn_prep_fwd(col, cw, x):
    return (_post_conv(col < 2 * PAIRS, _conv(cw, x)),)


def _gdn_prep_bwd(is_qk, cw, x, dy):
    c = _conv(cw, x)
    _, vjp = jax.vjp(lambda cc: _post_conv(is_qk, cc), c)
    (dc,) = vjp(dy)
    dx = jnp.zeros_like(x)
    row = lax.broadcasted_iota(jnp.int32, cw.shape, 0)
    dcw = jnp.zeros(cw.shape, F32)
    for i in range(CONV_K):
        s = CONV_K - 1 - i
        dx = dx + _row_of(cw, i) * _shift_up(dc, s)
        dcw = dcw + jnp.where(row == i, jnp.sum(dc * _shift_down(x, s), axis=0, keepdims=True), 0.0)
    return dx, dcw


def _head_rms(w, x):
    return x * lax.rsqrt(_pair_sum(x * x) / HEAD_DIM + EPS) * w


def _cat_weights(w_in_t):
    tail = jnp.pad(w_in_t[4112:4120], ((0, D_CAT - D_IN), (0, 0)))
    return jnp.concatenate([w_in_t[:2048], w_in_t[2064:4112], w_in_t[2048:2064], tail], axis=0)


def _uncat_grad(g):
    return jnp.concatenate([g[:2048], g[4096:4112], g[2048:4096], g[4112:4120]], axis=0)


def _lanes_to_rowform(v8, rows):
    return v8.reshape(rows // CHUNK, CHUNK, HEADS).transpose(0, 2, 1).reshape(rows // CHUNK, PAIRS, 1, LANES)


def _rowform_to_lanes(v, rows):
    return v.reshape(rows // CHUNK, HEADS, CHUNK).transpose(0, 2, 1).reshape(rows, HEADS)


def _local_step(x, target, norm1_w, a_log, dt_bias, out_norm_w, f_bias, q_norm_w, k_norm_w,
                norm2_w, final_w, first_weights, late_weights, early_grads_ready, early_grads_continue):
    rows = x.shape[0]
    tm = min(512, rows)
    lc, lf = _tri_consts()
    xb, xg = _expand_consts()

    (h1,) = _tiles(lambda col, w, xx: (_rms(xx, w),), name="norm1", rows=rows, tm=tm,
                   full_consts=[norm1_w], row_ins=[(x, D_MODEL, 0)], row_outs=[(D_MODEL, BF16)])
    w_cat, conv_w = first_weights(h1)
    proj = _mm(h1, w_cat, dims="nt", name="in_proj", tn=1408, tk=1024)

    lane_pad = lambda v, off: jnp.pad(v.reshape(1, -1), ((0, 0), (off, LANES - off - v.size)))
    p_a, p_dt, p_fb = lane_pad(a_log, 8), lane_pad(dt_bias, 8), lane_pad(f_bias, 16)

    def gates_fwd(col, lcv, lfv, a, dt, fb, pre):
        gates = _gates_elem(a, dt, fb, pre)
        return gates, _cums_fwd(lcv, lfv, gates)

    gates, cums = _tiles(gates_fwd, name="gates", rows=rows, tm=rows,
                         full_consts=[lc, lf, p_a, p_dt, p_fb], row_ins=[(proj, LANES, COL_SMALL)],
                         row_outs=[(LANES, F32), (LANES, F32)])

    def expand_fwd(col, b, g, gt, cm):
        return (_dot32(gt, b, _CONTRACT["nn"]), _dot32(cm, g, _CONTRACT["nn"]))

    betax, gcx = _tiles(expand_fwd, name="expand", rows=rows, tm=tm, full_consts=[xb, xg],
                        row_ins=[(gates, LANES, 0), (cums, LANES, 0)],
                        row_outs=[(WIDTH, F32)] * 2)
    grow = _lanes_to_rowform(cums[:, 8:16], rows)
    frow = cums[:, 16:24].T.reshape(PAIRS, 2, rows)

    (qkv,) = _tiles(_gdn_prep_fwd, name="gdn_prep", rows=rows, tm=rows, ncol=3 * PAIRS,
                    col_consts=[(conv_w, CONV_K, LANES, 0)], row_ins=[(proj, LANES, 0)],
                    row_outs=[(LANES, F32)])
    o_gdn, ssave, tsave = _gdn_forward(qkv, betax, gcx, grow, rows)

    w_qk = jnp.concatenate([jnp.tile(q_norm_w.reshape(1, -1), (1, HEADS)),
                            jnp.tile(k_norm_w.reshape(1, -1), (1, HEADS))], axis=1)
    fox_off = 2048 // LANES
    (fqk,) = _tiles(lambda col, w, xx: (_head_rms(w, xx),), name="fox_prep", rows=rows, tm=rows, ncol=2 * PAIRS,
                    col_consts=[(w_qk, 1, LANES, 0)], row_ins=[(proj, LANES, fox_off)],
                    row_outs=[(LANES, F32)])
    ao, lse = _attention_forward(fqk, proj, frow, rows)

    w_on = jnp.tile(out_norm_w.reshape(1, -1), (1, 2))
    z_off, fg_off = 1536 // LANES, 3584 // LANES
    mix_g_fn = lambda w, o, z: _head_rms(w, o) * _silu(z)
    mix_f_fn = lambda a, g: a * jax.nn.sigmoid(g)
    (mix_g,) = _tiles(lambda col, w, o, z: (mix_g_fn(w, o, z),), name="mix_gdn", rows=rows, tm=rows, ncol=PAIRS,
                      full_consts=[w_on], row_ins=[(o_gdn, LANES, 0), (proj, LANES, z_off)],
                      row_outs=[(LANES, BF16)])
    (mix_f,) = _tiles(lambda col, a, g: (mix_f_fn(a, g),), name="mix_fox", rows=rows, tm=rows, ncol=PAIRS,
                      row_ins=[(ao, LANES, 0), (proj, LANES, fg_off)], row_outs=[(LANES, BF16)])
    mix = jnp.concatenate([mix_g, mix_f], axis=1)
    w_out, w_gate, w_up, w_down = late_weights(mix)
    t_rows, t_half = min(1024, rows), min(512, rows)
    n_rt = rows // t_rows
    row_blk = pl.BlockSpec((t_rows, D_MODEL), lambda i, n: (i, 0))
    half_blk = pl.BlockSpec((t_half, D_MODEL), lambda i, n: (i, 0))
    vec_blk = pl.BlockSpec((1, D_MODEL), lambda i, n: (0, 0))
    wide = lambda dt: jax.ShapeDtypeStruct((rows, D_MODEL), dt)
    x1, h2 = _mm_blocks(mix, w_out, name="out_proj_norm2", grid=(n_rt, 1), dims="nn",
                        a_spec=row_blk, b_spec=pl.BlockSpec((D_MODEL, D_MODEL), lambda i, n: (0, 0)),
                        o_spec=[row_blk, row_blk], out_shape=[wide(F32), wide(BF16)], add=x, add_spec=row_blk,
                        extra=[(norm2_w, vec_blk)], epilogue=lambda r, w: (r, _rms(r, w)))
    st_act = jax.ShapeDtypeStruct((N_CHIPS, rows, FF_SHARD), BF16)
    st_rows = pl.BlockSpec((None, rows, FF_SHARD), lambda i, j: (j, i, 0))

    def ffn_in(w_st, name):
        return _mm_blocks(h2, w_st, name=name, grid=(1, N_CHIPS), dims="nt",
                          a_spec=pl.BlockSpec((rows, D_MODEL), lambda i, j: (i, 0)),
                          b_spec=pl.BlockSpec((None, FF_SHARD, D_MODEL), lambda i, j: (j, 0, 0)),
                          o_spec=st_rows, out_shape=st_act)

    gate = ffn_in(w_gate, "ffn_gate")
    act_fn = lambda g, u: _silu(g) * u
    st_tile = pl.BlockSpec((None, t_rows, FF_SHARD), lambda i, j: (j, i, 0))
    up, act = _mm_blocks(h2, w_up, name="ffn_up_act", grid=(n_rt, N_CHIPS), dims="nt",
                         a_spec=pl.BlockSpec((t_rows, D_MODEL), lambda i, j: (i, 0)),
                         b_spec=pl.BlockSpec((None, FF_SHARD, D_MODEL), lambda i, j: (j, 0, 0)),
                         o_spec=[st_tile, st_tile], out_shape=[st_act, st_act], extra=[(gate, st_tile)],
                         epilogue=lambda u, g: (u, act_fn(g.astype(F32), u)))

    def final_fn(xx, tgt, w):
        y, vjp = jax.vjp(_rms, xx, w)
        err = y - tgt
        loss = 0.5 * jnp.sum(err * err) / D_MODEL
        dx, dw = vjp(err / D_MODEL)
        return dx, dx, jnp.full((1, LANES), loss, F32), dw

    dx2, dx2_b, loss, d_final_w = _mm_blocks(
        act, w_down, name="ffn_down_loss", grid=(rows // t_half, 1), dims="nn", n_sum=N_CHIPS,
        a_spec=pl.BlockSpec((N_CHIPS, t_half, FF_SHARD), lambda i, n: (0, i, 0)),
        b_spec=pl.BlockSpec((N_CHIPS, FF_SHARD, D_MODEL), lambda i, n: (0, 0, 0)),
        o_spec=[half_blk, half_blk, pl.BlockSpec((1, LANES), lambda i, n: (0, 0)), vec_blk],
        out_shape=[wide(F32), wide(BF16), jax.ShapeDtypeStruct((1, LANES), F32),
                   jax.ShapeDtypeStruct((1, D_MODEL), F32)],
        add=x1, add_spec=half_blk, extra=[(target, half_blk), (final_w, vec_blk)], epilogue=final_fn, n_acc=2)

    def act_bwd(d, g, u):
        _, vjp = jax.vjp(act_fn, g.astype(F32), u.astype(F32))
        return vjp(d)

    dgate, dup = _mm_blocks(dx2_b, w_down, name="d_act_gate_up", grid=(n_rt, N_CHIPS), dims="nt",
                            a_spec=pl.BlockSpec((t_rows, D_MODEL), lambda i, j: (i, 0)),
                            b_spec=pl.BlockSpec((None, FF_SHARD, D_MODEL), lambda i, j: (j, 0, 0)),
                            o_spec=[st_tile, st_tile], out_shape=[st_act, st_act],
                            extra=[(gate, st_tile), (up, st_tile)], epilogue=act_bwd)

    def g_ffn(d_st, other, name):
        return _mm_blocks(d_st, other, name=name, grid=(N_CHIPS, 1), dims="tn",
                          a_spec=pl.BlockSpec((None, rows, FF_SHARD), lambda j, n: (j, 0, 0)),
                          b_spec=pl.BlockSpec((rows, D_MODEL), lambda j, n: (0, 0)),
                          o_spec=pl.BlockSpec((None, FF_SHARD, D_MODEL), lambda j, n: (j, 0, 0)),
                          out_shape=jax.ShapeDtypeStruct((N_CHIPS, FF_SHARD, D_MODEL), BF16))

    g_down = g_ffn(act, dx2_b, "g_down")

    def norm_bwd(dh, xx, dres, w):
        _, vjp = jax.vjp(_rms, xx, w)
        dx, dw = vjp(dh)
        return dx + dres, dx + dres, dw

    def d_h2(d_st, w_st, name, add, **fused):
        return _mm_blocks(d_st, w_st, name=name, grid=(rows // t_half, 1), dims="nn", n_sum=N_CHIPS,
                          a_spec=pl.BlockSpec((N_CHIPS, t_half, FF_SHARD), lambda i, n: (0, i, 0)),
                          b_spec=pl.BlockSpec((N_CHIPS, FF_SHARD, D_MODEL), lambda i, n: (0, 0, 0)),
                          add=add, add_spec=half_blk, **fused)

    dh2_gate = d_h2(dgate, w_gate, "d_h2_gate", None, o_spec=half_blk, out_shape=wide(F32))
    dx1, dx1_b, d_norm2_w = d_h2(
        dup, w_up, "d_h2_up_norm2_bwd", dh2_gate, o_spec=[half_blk, half_blk, vec_blk],
        out_shape=[wide(F32), wide(BF16), jax.ShapeDtypeStruct((1, D_MODEL), F32)],
        extra=[(x1, half_blk), (dx2, half_blk), (norm2_w, vec_blk)], epilogue=norm_bwd, n_acc=1)
    g_gate, g_up = g_ffn(dgate, h2, "g_gate"), g_ffn(dup, h2, "g_up")
    dmix = _mm(dx1_b, w_out, dims="nt", name="d_mix", tn=D_MODEL, tk=1024)
    g_out = _mm(mix, dx1_b, dims="tn", name="g_out", tn=D_MODEL, tk=rows, out_dtype=BF16)
    w_on = w_on + early_grads_ready(g_out, g_gate, g_up, g_down)

    def mix_g_bwd(col, w, o, z, d):
        _, vjp = jax.vjp(mix_g_fn, w, o, z)
        dw, do_, dz = vjp(d)
        return do_, dz, dw

    do_gdn, dz, d_on = _tiles(mix_g_bwd, name="mix_gdn_bwd", rows=rows, tm=rows, ncol=PAIRS, full_consts=[w_on],
                              row_ins=[(o_gdn, LANES, 0), (proj, LANES, z_off), (dmix, LANES, 0)],
                              row_outs=[(LANES, F32), (LANES, BF16)], acc_outs=[(1, LANES)])

    def mix_f_bwd(col, a, g, d):
        _, vjp = jax.vjp(mix_f_fn, a, g)
        return vjp(d)

    dao, dfgate = _tiles(mix_f_bwd, name="mix_fox_bwd", rows=rows, tm=rows, ncol=PAIRS,
                         row_ins=[(ao, LANES, 0), (proj, LANES, fg_off), (dmix, LANES, PAIRS)],
                         row_outs=[(LANES, F32), (LANES, BF16)])

    dfq, dfk, dfv, dfrow = _attention_backward(fqk, proj, frow + early_grads_continue(dao), ao, lse, dao, rows)

    def fox_prep_bwd(col, w, xx, d):
        _, vjp = jax.vjp(_head_rms, w, xx)
        dw, dx = vjp(d)
        return dx, dw

    dfqk, d_wqk = [], []
    for part, d_n in enumerate((dfq, dfk)):
        dx_p, dw_p = _tiles(fox_prep_bwd, name="fox_prep_bwd_" + "qk"[part], rows=rows, tm=rows, ncol=PAIRS,
                            col_consts=[(w_qk, 1, LANES, part * PAIRS)],
                            row_ins=[(proj, LANES, fox_off + part * PAIRS), (d_n, LANES, 0)],
                            row_outs=[(LANES, BF16)], acc_outs=[(1, LANES)])
        dfqk.append(dx_p)
        d_wqk.append(dw_p)

    dq, dk, dv, dbetax, dgcx, dgrow = _gdn_backward(qkv, betax, gcx, grow, ssave, tsave, do_gdn, rows)
    dqkv, d_conv = [], []
    for part, d_n in enumerate((dq, dk, dv)):
        prep_bwd = lambda col, cw, xx, dy, is_qk=(part < 2): _gdn_prep_bwd(is_qk, cw, xx, dy)
        dx_p, dw_p = _tiles(prep_bwd, name="gdn_prep_bwd_" + "qkv"[part], rows=rows, tm=rows, ncol=PAIRS,
                            col_consts=[(conv_w, CONV_K, LANES, part * PAIRS)],
                            row_ins=[(proj, LANES, part * PAIRS), (d_n, LANES, 0)],
                            row_outs=[(LANES, BF16)], acc_outs=[(CONV_K, LANES)])
        dqkv.append(dx_p)
        d_conv.append(dw_p)
    d_conv = jnp.concatenate(d_conv, axis=1)

    def expand_bwd(col, b, g, db, dg):
        return (_dot32(db, b, _CONTRACT["nt"]), _dot32(dg, g, _CONTRACT["nt"]))

    dgates_b, dcums_g = _tiles(expand_bwd, name="expand_bwd", rows=rows, tm=tm, full_consts=[xb, xg],
                               row_ins=[(dbetax, WIDTH, 0), (dgcx, WIDTH, 0)],
                               row_outs=[(LANES, F32), (LANES, F32)])
    dcums_row = jnp.concatenate([jnp.zeros((rows, 8), F32), _rowform_to_lanes(dgrow, rows),
                                 dfrow.reshape(HEADS, rows).T, jnp.zeros((rows, LANES - 24), F32)], axis=1)

    def gates_bwd(col, lcv, lfv, a, dt, fb, pre, dgb, dcg, dcr):
        lane = _lane_ids(pre.shape)
        dgates = jnp.where(lane < 8, dgb, _cums_bwd(lcv, lfv, dcg + dcr))
        _, vjp = jax.vjp(_gates_elem, a, dt, fb, pre)
        da, ddt, dfb, dpre = vjp(dgates)
        return dpre, da, ddt, dfb

    dpre, d_a, d_dt, d_fb = _tiles(gates_bwd, name="gates_bwd", rows=rows, tm=rows,
                                   full_consts=[lc, lf, p_a, p_dt, p_fb],
                                   row_ins=[(proj, LANES, COL_SMALL), (dgates_b, LANES, 0), (dcums_g, LANES, 0),
                                            (dcums_row, LANES, 0)],
                                   row_outs=[(LANES, BF16)], acc_outs=[(1, LANES)] * 3)

    dproj = jnp.concatenate(dqkv + [dz] + dfqk + [dfv, dfgate, dpre], axis=1)
    grad_x, d_norm1_w = _mm_blocks(
        dproj, w_cat, name="d_h1_norm1_bwd", grid=(rows // t_half, 1), dims="nn",
        a_spec=pl.BlockSpec((t_half, D_CAT), lambda i, n: (i, 0)),
        b_spec=pl.BlockSpec((D_CAT, D_MODEL), lambda i, n: (0, 0)),
        o_spec=[half_blk, vec_blk], out_shape=[wide(F32), jax.ShapeDtypeStruct((1, D_MODEL), F32)],
        extra=[(x, half_blk), (dx1, half_blk), (norm1_w, vec_blk)],
        epilogue=lambda dh, xx, dres, w: norm_bwd(dh, xx, dres, w)[1:], n_acc=1)
    g_cat = _mm(dproj, h1, dims="tn", name="g_in", tm=1408, tn=D_MODEL, tk=rows)

    fold = lambda v: v.reshape(-1, HEAD_DIM).sum(axis=0)
    small = dict(
        loss=loss[0, 0],
        norm1_w=d_norm1_w, conv_w=d_conv, a_log=d_a[0, 8:16], dt_bias=d_dt[0, 8:16],
        out_norm_w=fold(d_on), f_bias=d_fb[0, 16:24], q_norm_w=fold(d_wqk[0]),
        k_norm_w=fold(d_wqk[1]), norm2_w=d_norm2_w, final_w=d_final_w)
    return grad_x, g_cat, g_out, g_gate, g_up, g_down, small


HBM_SPEC = pl.BlockSpec(memory_space=pltpu.HBM)


def _place():
    x, y, c = lax.axis_index("x"), lax.axis_index("y"), lax.axis_index("c")
    chips = [(1 - x, y), (x, 1 - y), (1 - x, 1 - y)]
    return x, y, c, 2 * x + y, (x, y, 1 - c), chips, [2 * cx + cy for cx, cy in chips]


def _remote(src, dst, send_sem, recv_sem, to):
    return pltpu.make_async_remote_copy(src_ref=src, dst_ref=dst, send_sem=send_sem, recv_sem=recv_sem,
                                        device_id=to, device_id_type=MESH)


SEM_SPEC =pl.BlockSpec(memory_space=pltpu.SEMAPHORE)
ANY_SPEC = pl.BlockSpec(memory_space=pl.ANY)
DATAFLOW = pltpu.SideEffectType.DATAFLOW_SIDE_EFFECTING


def _gather_plan(srcs, lands):
    x, y, c, own, sib, chips, chip_idx = _place()
    plan = []
    for src, land in zip(srcs, lands):
        for j, chip in enumerate(chips):
            plan.append((src, land.at[own], (*chip, c), land.at[chip_idx[j]]))
        plan.append((src, land.at[own], sib, land.at[own]))
    return plan


def _exchange_plan(srcs, lands):
    x, y, c, own, sib, chips, chip_idx = _place()
    plan = []
    for src, land in zip(srcs, lands):
        for j, chip in enumerate(chips):
            plan.append((src.at[chip_idx[j]], land.at[j], (*chip, c), land.at[j]))
    return plan


def _swap_plan(srcs, lands):
    x, y, c, own, sib, chips, chip_idx = _place()
    plan = []
    for src, land in zip(srcs, lands):
        h = src.shape[2] // 2
        plan.append((src.at[:, :, pl.ds(pl.multiple_of((1 - c) * h, LANES), h)], land, sib, land))
    return plan


def _in_proj_plan(srcs, lands):
    x, y, c, own, sib, chips, chip_idx = _place()
    (w, conv), (w_land, conv_land) = srcs, lands
    hw = w.shape[1] // 2
    half = lambda ref: ref.at[:, pl.ds(pl.multiple_of(c * hw, LANES), hw)]
    plan = []
    for j, chip in enumerate(chips):
        plan.append((half(w), half(w_land.at[own]), (*chip, c), half(w_land.at[chip_idx[j]])))
        plan.append((conv, conv_land.at[own], (*chip, c), conv_land.at[chip_idx[j]]))
    plan.append((w, w_land.at[own], sib, w_land.at[own]))
    plan.append((conv, conv_land.at[own], sib, conv_land.at[own]))
    return plan


def _forward_halves(landed):
    hw = landed.shape[2] // 2

    def body(in_ref, out_ref, send_sems, recv_sems):
        x, y, c, own, sib, chips, chip_idx = _place()
        half = lambda ref, hc: ref.at[:, pl.ds(pl.multiple_of(hc * hw, LANES), hw)]
        sent = [_remote(half(out_ref.at[chip_idx[j]], c), half(out_ref.at[chip_idx[j]], c),
                        send_sems.at[j], recv_sems.at[j], sib) for j in range(3)]
        for cp in sent:
            cp.start()
        for j in range(3):
            other = half(out_ref.at[chip_idx[j]], 1 - c)
            _remote(other, other, send_sems.at[j], recv_sems.at[j], sib).wait_recv()
        for cp in sent:
            cp.wait_send()

    return pl.pallas_call(
        body, name="gather_in_forward", out_shape=jax.ShapeDtypeStruct(landed.shape, landed.dtype),
        in_specs=[HBM_SPEC], out_specs=HBM_SPEC, input_output_aliases={0: 0},
        scratch_shapes=[pltpu.SemaphoreType.DMA((3,)), pltpu.SemaphoreType.DMA((3,))],
    )(landed)


def _split_start(name, plan_fn, srcs, land_shapes, n_copies, after=None):
    n = len(srcs)
    extra = [] if after is None else [after]

    def body(*refs):
        src_refs, land_refs = refs[:n], refs[n:2 * n]
        send_sems, recv_sems = refs[2 * n + len(extra)], refs[2 * n + len(extra) + 1]
        token = refs[-1]
        for k, (src, dst, to, _) in enumerate(plan_fn(src_refs, land_refs)):
            _remote(src, dst, send_sems.at[k], recv_sems.at[k], to).start()
        token[...] = jnp.zeros_like(token)

    lands = [pltpu.with_memory_space_constraint(lax.empty(s.shape, s.dtype), pltpu.HBM) for s in land_shapes]
    srcs = [pltpu.with_memory_space_constraint(s, pltpu.HBM) for s in srcs]
    out_shape = ([pltpu.SemaphoreType.DMA((n_copies,)), pltpu.SemaphoreType.DMA((n_copies,))]
                 + [pltpu.HBM(s.shape, s.dtype) for s in srcs] + [pltpu.HBM(s.shape, s.dtype) for s in land_shapes]
                 + [jax.ShapeDtypeStruct((8, LANES), F32)])
    res = pl.pallas_call(
        body, name=name, out_shape=out_shape,
        in_specs=[HBM_SPEC] * (2 * n) + [ANY_SPEC] * len(extra),
        out_specs=[SEM_SPEC, SEM_SPEC] + [HBM_SPEC] * (2 * n) + [pl.BlockSpec(memory_space=pltpu.VMEM)],
        input_output_aliases={i: 2 + i for i in range(2 * n)},
        compiler_params=pltpu.CompilerParams(has_side_effects=DATAFLOW),
    )(*srcs, *lands, *extra)
    return dict(sems=res[:2], srcs=res[2:2 + n], lands=res[2 + n:2 + 2 * n], token=res[-1], n=n)


def _split_wait(name, plan_fn, started, after):
    n = started["n"]

    def body(*refs):
        src_refs, land_refs = refs[:n], refs[n:2 * n]
        send_sems, recv_sems = refs[2 * n], refs[2 * n + 1]
        for k, (src, _, to, landed) in enumerate(plan_fn(src_refs, land_refs)):
            copy = _remote(src, landed, send_sems.at[k], recv_sems.at[k], to)
            copy.wait_send()
            copy.wait_recv()

    srcs, lands = started["srcs"], started["lands"]
    after = list(after) if isinstance(after, (list, tuple)) else [after]
    res = pl.pallas_call(
        body, name=name,
        out_shape=[pltpu.HBM(s.shape, s.dtype) for s in srcs] + [pltpu.HBM(s.shape, s.dtype) for s in lands],
        in_specs=[HBM_SPEC] * (2 * n) + [SEM_SPEC, SEM_SPEC] + [ANY_SPEC] * len(after),
        out_specs=[HBM_SPEC] * (2 * n),
        input_output_aliases={i: i for i in range(2 * n)},
        compiler_params=pltpu.CompilerParams(has_side_effects=DATAFLOW),
    )(*srcs, *lands, *started["sems"], *after)
    started["srcs_after"] = res[:n]
    return res[n:]


def _add_half(stack, landed, place, name):
    _, rows, h = landed.shape

    def body(place_ref, a_ref, b_ref, o_ref, own_ref):
        part = (a_ref[...].astype(F32) + b_ref[...].astype(F32)).astype(o_ref.dtype)
        o_ref[...] = part

        @pl.when(pl.program_id(0) == place_ref[1])
        def _():
            own_ref[...] = part[0]

    return pl.pallas_call(
        body, name=name,
        out_shape=[jax.ShapeDtypeStruct(landed.shape, BF16), jax.ShapeDtypeStruct((rows, h), BF16)],
        grid_spec=pltpu.PrefetchScalarGridSpec(
            num_scalar_prefetch=1, grid=(N_CHIPS,),
            in_specs=[pl.BlockSpec((1, rows, h), lambda j, p: (j, 0, p[0])),
                      pl.BlockSpec((1, rows, h), lambda j, p: (j, 0, 0))],
            out_specs=[pl.BlockSpec((1, rows, h), lambda j, p: (j, 0, 0)),
                       pl.BlockSpec((rows, h), lambda j, p: (0, 0))]),
        compiler_params=_params(("arbitrary",)),
    )(place, stack, landed)


def _sum_partials(own_part, landed, name, untiled_rows=False):
    _, h, cols = landed.shape
    tc = LANES if untiled_rows else cols

    def body(own_ref, a_ref, o_ref):
        acc = own_ref[...].astype(F32)
        for s in range(3):
            acc = acc + a_ref[s].astype(F32)
        if untiled_rows:
            o_ref[:, 0, :] = acc
        else:
            o_ref[...] = acc

    if untiled_rows:
        out_shape, out_spec = jax.ShapeDtypeStruct((h, 1, cols), F32), pl.BlockSpec((h, 1, tc), lambda i: (0, 0, i))
    else:
        out_shape, out_spec = jax.ShapeDtypeStruct((h, cols), F32), pl.BlockSpec((h, tc), lambda i: (0, i))
    return pl.pallas_call(
        body, name=name, out_shape=out_shape, grid=(cols // tc,),
        in_specs=[pl.BlockSpec((h, tc), lambda i: (0, i)), pl.BlockSpec((3, h, tc), lambda i: (0, 0, i))],
        out_specs=out_spec, compiler_params=_params(("arbitrary",)),
    )(own_part, landed)


def _share_halves(halves, name):
    n = len(halves)

    def body(*refs):
        ins, outs = refs[:n], refs[n:2 * n]
        send_sems, recv_sems = refs[2 * n:]
        x, y, c, own, sib, chips, chip_idx = _place()
        cps = [_remote(ins[i], outs[i], send_sems.at[i], recv_sems.at[i], sib) for i in range(n)]
        for cp in cps:
            cp.start()
        for cp in cps:
            cp.wait()

    return pl.pallas_call(
        body, name=name,
        out_shape=[jax.ShapeDtypeStruct(p.shape, p.dtype) for p in halves],
        in_specs=[HBM_SPEC] * n, out_specs=[HBM_SPEC] * n,
        scratch_shapes=[pltpu.SemaphoreType.DMA((n,)), pltpu.SemaphoreType.DMA((n,))],
    )(*halves)


def _allreduce_small(packed):
    rows = packed.shape[0]
    n_dev = 8

    def body(in_ref, out_ref, gath, send_sems, recv_sems):
        x, y, c = lax.axis_index("x"), lax.axis_index("y"), lax.axis_index("c")
        me = 4 * x + 2 * y + c
        gath[me] = in_ref[...]
        cps = []
        for k in range(1, n_dev):
            fx, fy, fc = (k >> 2) & 1, (k >> 1) & 1, k & 1
            to = (x ^ fx, y ^ fy, c ^ fc)
            cps.append(_remote(in_ref, gath.at[me], send_sems.at[k - 1], recv_sems.at[k - 1], to))
        for cp in cps:
            cp.start()
        for k in range(1, n_dev):
            fx, fy, fc = (k >> 2) & 1, (k >> 1) & 1, k & 1
            src = 4 * (x ^ fx) + 2 * (y ^ fy) + (c ^ fc)
            slot = gath.at[src]
            _remote(slot, slot, send_sems.at[k - 1], recv_sems.at[k - 1], (x, y, c)).wait_recv()
        for cp in cps:
            cp.wait_send()
        acc = gath[0]
        for d in range(1, n_dev):
            acc = acc + gath[d]
        out_ref[...] = acc

    vm = pl.BlockSpec(memory_space=pltpu.VMEM)
    return pl.pallas_call(
        body, name="allreduce_small", out_shape=jax.ShapeDtypeStruct(packed.shape, F32),
        in_specs=[vm], out_specs=vm,
        scratch_shapes=[pltpu.VMEM((n_dev, rows, LANES), F32),
                        pltpu.SemaphoreType.DMA((n_dev - 1,)), pltpu.SemaphoreType.DMA((n_dev - 1,))],
    )(packed)


def _adam(col, w, g, m, v):
    m2 = ADAM_B1 * m + (1.0 - ADAM_B1) * g
    v2 = ADAM_B2 * v + (1.0 - ADAM_B2) * (g * g)
    m_hat = m2 / (1.0 - ADAM_B1 ** ADAM_STEP)
    v_hat = v2 / (1.0 - ADAM_B2 ** ADAM_STEP)
    delta = -ADAM_LR * (m_hat / (jnp.sqrt(v_hat) + ADAM_EPS) + ADAM_WD * w)
    return delta, m2, v2


def _adam_call(w, g, m, v, name):
    rows, cols = w.shape
    tm = rows
    for cand in (256, 352, 176, 128, 64, 48, 16, 8):
        if rows % cand == 0:
            tm = cand
            break
    return _tiles(_adam, name=name, rows=rows, tm=tm,
                  row_ins=[(w, cols, 0), (g, cols, 0), (m, cols, 0), (v, cols, 0)],
                  row_outs=[(cols, F32)] * 3)


def _adam_big(w, g_mine, g_other, m, v, place, name):
    rows, cols = w.shape
    tc = 256
    nt = cols // 2 // tc

    def body(place_ref, w_ref, gm_ref, go_ref, m_ref, v_ref, g_out, d_out, m_out, v_out):
        g = jnp.where(pl.program_id(0) == place_ref[0], gm_ref[...], go_ref[...])
        d, m2, v2 = _adam(None, w_ref[...], g, m_ref[...], v_ref[...])
        g_out[...] = g
        d_out[...] = d
        m_out[...] = m2
        v_out[...] = v2

    full = pl.BlockSpec((rows, tc), lambda hh, i, p: (0, hh * nt + i))
    half = pl.BlockSpec((rows, tc), lambda hh, i, p: (0, i))
    return pl.pallas_call(
        body, name=name, out_shape=[jax.ShapeDtypeStruct(w.shape, F32)] * 4,
        grid_spec=pltpu.PrefetchScalarGridSpec(
            num_scalar_prefetch=1, grid=(2, nt),
            in_specs=[full, half, half, full, full], out_specs=[full] * 4),
        compiler_params=_params(("arbitrary", "arbitrary")),
    )(place, w, g_mine, g_other, m, v)


def _adam_untiled_rows(w, g_mine, g_other, m, v, place, name):
    rows, _, cols = w.shape
    tc = 256
    nt = cols // 2 // tc
    rb = next(r for r in (206, 128, 103, rows) if rows % r == 0)

    def body(place_ref, w_ref, gm_ref, go_ref, m_ref, v_ref, g_out, d_out, m_out, v_out):
        g = jnp.where(pl.program_id(0) == place_ref[0], gm_ref[:, 0, :], go_ref[:, 0, :])
        d, m2, v2 = _adam(None, w_ref[:, 0, :], g, m_ref[:, 0, :], v_ref[:, 0, :])
        g_out[:, 0, :] = g
        d_out[:, 0, :] = d
        m_out[:, 0, :] = m2
        v_out[:, 0, :] = v2

    full = pl.BlockSpec((rb, 1, tc), lambda hh, i, r, p: (r, 0, hh * nt + i))
    half = pl.BlockSpec((rb, 1, tc), lambda hh, i, r, p: (r, 0, i))
    return pl.pallas_call(
        body, name=name, out_shape=[jax.ShapeDtypeStruct(w.shape, F32)] * 4,
        grid_spec=pltpu.PrefetchScalarGridSpec(
            num_scalar_prefetch=1, grid=(2, nt, rows // rb),
            in_specs=[full, half, half, full, full], out_specs=[full] * 4),
        compiler_params=_params(("arbitrary", "arbitrary", "arbitrary")),
    )(place, w, g_mine, g_other, m, v)


def _pack(arrays, zero=None):
    flat = []
    for a in arrays:
        a = a.reshape(-1).astype(F32)
        if zero is not None:
            a = a + zero
        flat.append(jnp.pad(a, (0, (-a.size) % LANES)))
    out = jnp.concatenate(flat)
    out = jnp.pad(out, (0, (-out.size) % (8 * LANES)))
    return out.reshape(-1, LANES)


def _unpack(packed, shapes):
    flat = packed.reshape(-1)
    out, off = [], 0
    for s in shapes:
        size = int(np.prod(s))
        out.append(flat[off:off + size].reshape(s))
        off += size + (-size) % LANES
    return out


def kernel(x, norm1_w, w_in, gdn_conv_w, gdn_A_log, gdn_dt_bias, gdn_out_norm_w, fox_f_bias, fox_q_norm_w, fox_k_norm_w, w_out, norm2_w, w_ffn_gate, w_ffn_up, w_ffn_down, final_norm_w, loss_target, m_norm1_w, m_w_in, m_gdn_conv_w, m_gdn_A_log, m_gdn_dt_bias, m_gdn_out_norm_w, m_fox_f_bias, m_fox_q_norm_w, m_fox_k_norm_w, m_w_out, m_norm2_w, m_w_ffn_gate, m_w_ffn_up, m_w_ffn_down, m_final_norm_w, v_norm1_w, v_w_in, v_gdn_conv_w, v_gdn_A_log, v_gdn_dt_bias, v_gdn_out_norm_w, v_fox_f_bias, v_fox_q_norm_w, v_fox_k_norm_w, v_w_out, v_norm2_w, v_w_ffn_gate, v_w_ffn_up, v_w_ffn_down, v_final_norm_w):
    cx, cy, cc = lax.axis_index("x"), lax.axis_index("y"), lax.axis_index("c")
    own = 2 * cx + cy
    place = jnp.stack([cc, own]).astype(jnp.int32)

    names = ["w_in", "w_out", "w_gate", "w_up", "w_down"]
    is_t = [True, False, True, True, False]
    to_t = lambda a, t: a[0].T if t else a[0]
    from_t = lambda a, t: (a.T if t else a)[None]
    big_w = [to_t(a, t) for a, t in zip([w_in, w_out, w_ffn_gate, w_ffn_up, w_ffn_down], is_t)]
    big_m = [to_t(a, t) for a, t in zip([m_w_in, m_w_out, m_w_ffn_gate, m_w_ffn_up, m_w_ffn_down], is_t)]
    big_v = [to_t(a, t) for a, t in zip([v_w_in, v_w_out, v_w_ffn_gate, v_w_ffn_up, v_w_ffn_down], is_t)]
    shards = [big_w[0].astype(BF16)]
    small_w = [norm1_w, gdn_conv_w, gdn_A_log, gdn_dt_bias, gdn_out_norm_w, fox_f_bias, fox_q_norm_w,
               fox_k_norm_w, norm2_w, final_norm_w]
    small_m = [m_norm1_w, m_gdn_conv_w, m_gdn_A_log, m_gdn_dt_bias, m_gdn_out_norm_w, m_fox_f_bias,
               m_fox_q_norm_w, m_fox_k_norm_w, m_norm2_w, m_final_norm_w]
    small_v = [v_norm1_w, v_gdn_conv_w, v_gdn_A_log, v_gdn_dt_bias, v_gdn_out_norm_w, v_fox_f_bias,
               v_fox_q_norm_w, v_fox_k_norm_w, v_norm2_w, v_final_norm_w]
    first = _split_start("gather_in_start", _in_proj_plan, [shards[0], gdn_conv_w[0]],
                         [jax.ShapeDtypeStruct((N_CHIPS,) + shards[0].shape, BF16),
                          jax.ShapeDtypeStruct((N_CHIPS, CONV_K, 3 * WIDTH // N_CHIPS), F32)],
                         n_copies=8)
    small_packed = [_pack(p, first["token"][0, 0]) for p in (small_w, small_m, small_v)]
    shards += [(w + first["token"][0, 0]).astype(BF16) for w in big_w[1:]]
    rest = {}

    def first_weights(after):
        w_in_g, conv_g = _split_wait("gather_in_wait", _in_proj_plan, first, [after] + small_packed)
        w_in_g = _forward_halves(w_in_g)
        rest.update(_split_start("gather_rest_start", _gather_plan, shards[1:],
                                 [jax.ShapeDtypeStruct((N_CHIPS,) + s.shape, BF16) for s in shards[1:]],
                                 n_copies=4 * len(shards[1:]), after=w_in_g))
        w_cat = _cat_weights(w_in_g.reshape(D_IN, D_MODEL))
        return w_cat + rest["token"][0, 0].astype(BF16), conv_g.transpose(1, 0, 2).reshape(CONV_K, 3 * WIDTH)

    def late_weights(after):
        w_out_g, w_gate_g, w_up_g, w_down_g = _split_wait("gather_rest_wait", _gather_plan, rest, after)
        return w_out_g.reshape(D_MODEL, D_MODEL), w_gate_g, w_up_g, w_down_g

    def start_reduction(stacks, landed, nms, tag):
        added = [_add_half(s, l, place, "rs_add_" + nm) for s, l, nm in zip(stacks, landed, nms)]
        parts = [a[0] for a in added]
        started = _split_start("exchange_" + tag + "_start", _exchange_plan, parts,
                               [jax.ShapeDtypeStruct((3,) + p.shape[1:], p.dtype) for p in parts],
                               n_copies=3 * len(parts))
        return dict(own=[a[1] for a in added], started=started, tag=tag, names=nms)

    def finish_reduction(red, after, updates):
        landed = _split_wait("exchange_" + red["tag"] + "_wait", _exchange_plan, red["started"], after)
        halves = [_sum_partials(o, p, "rs_sum_" + nm, untiled_rows=nm == "w_in")
                  for o, p, nm in zip(red["own"], landed, red["names"])]
        others = _share_halves(halves, "rs_share_" + red["tag"])
        return [upd(gm, go) for upd, gm, go in zip(updates, halves, others)]

    def transport_update(b):
        def upd(gm, go):
            res = _adam_big(big_w[b], gm, go, big_m[b], big_v[b], place, "adam_" + names[b])
            early_done.append(res[1])
            return [from_t(a, is_t[b]) for a in res]
        return upd

    early_done = []

    def w_in_update(gm, go):
        rows3 = lambda a: jnp.transpose(a, (2, 0, 1))
        res = _adam_untiled_rows(rows3(w_in), gm, go, rows3(m_w_in), rows3(v_w_in), place, "adam_w_in")
        return [jnp.transpose(a, (1, 2, 0)) for a in res]

    early = {}

    def early_grads_ready(g_out, g_gate, g_up, g_down):
        stacks = [g_out.reshape(N_CHIPS, D_MODEL // N_CHIPS, D_MODEL), g_gate, g_up, g_down]
        swap = _split_start("swap_early_start", _swap_plan, stacks,
                            [jax.ShapeDtypeStruct(s.shape[:2] + (s.shape[2] // 2,), s.dtype) for s in stacks],
                            n_copies=len(stacks))
        early.update(stacks=stacks, swap=swap)
        return swap["token"][0, 0]

    def early_grads_continue(after):
        landed = _split_wait("swap_early_wait", _swap_plan, early["swap"], after)
        early.update(start_reduction(early["swap"]["srcs_after"], landed, names[1:], "early"))
        return early["started"]["token"][0, 0]

    grad_x, g_cat, _, _, _, _, small = _local_step(
        x[0], loss_target[0], norm1_w + first["token"][0, 0], gdn_A_log[0], gdn_dt_bias[0],
        gdn_out_norm_w[0], fox_f_bias[0], fox_q_norm_w[0], fox_k_norm_w[0], norm2_w, final_norm_w.reshape(1, -1),
        first_weights, late_weights, early_grads_ready, early_grads_continue)

    g_in_stack = _uncat_grad(g_cat).reshape(N_CHIPS, D_IN // N_CHIPS, D_MODEL)
    swap_in = _split_start("swap_w_in_start", _swap_plan, [g_in_stack],
                           [jax.ShapeDtypeStruct((N_CHIPS, D_IN // N_CHIPS, D_MODEL // 2), F32)],
                           n_copies=1)

    order = ["norm1_w", "conv_w", "a_log", "dt_bias", "out_norm_w", "f_bias", "q_norm_w", "k_norm_w",
             "norm2_w", "final_w"]
    red = _allreduce_small(_pack([small[k] for k in order] + [small["loss"]], swap_in["token"][0, 0]))
    red_shapes = [(1, D_MODEL), (CONV_K, 3 * WIDTH), (1, HEADS), (1, HEADS), (1, HEAD_DIM), (1, HEADS),
                  (1, HEAD_DIM), (1, HEAD_DIM), (1, D_MODEL), (D_MODEL,), ()]
    red_list = _unpack(red, red_shapes)
    loss = red_list[-1]
    small_g = dict(zip(order, red_list[:-1]))
    shard_cols = 3 * WIDTH // N_CHIPS
    small_g["conv_w"] = lax.dynamic_slice_in_dim(small_g["conv_w"], own * shard_cols, shard_cols, axis=1)[None]
    small_gl = [small_g[k].reshape(w.shape) for k, w in zip(order, small_w)]
    s_delta, s_m, s_v = _adam_call(small_packed[0], _pack(small_gl), small_packed[1], small_packed[2], "adam_small")
    landed_in = _split_wait("swap_w_in_wait", _swap_plan, swap_in, s_delta)
    late = start_reduction(swap_in["srcs_after"], landed_in, names[:1], "w_in")
    big_upd = finish_reduction(early, late["started"]["token"], [transport_update(b) for b in range(1, 5)])
    big_upd = finish_reduction(late, early_done, [w_in_update]) + big_upd
    shapes = [w.shape for w in small_w]
    s_delta, s_m, s_v = _unpack(s_delta, shapes), _unpack(s_m, shapes), _unpack(s_v, shapes)

    big_pos = {1: 0, 9: 1, 11: 2, 12: 3, 13: 4}
    small_pos = {0: 0, 2: 1, 3: 2, 4: 3, 5: 4, 6: 5, 7: 6, 8: 7, 10: 8, 14: 9}
    grads, deltas, new_m, new_v = [], [], [], []
    for pos in range(15):
        if pos in big_pos:
            b = big_pos[pos]
            g, d, m2, v2 = big_upd[b]
            grads.append(g)
            deltas.append(d)
            new_m.append(m2)
            new_v.append(v2)
        else:
            s = small_pos[pos]
            grads.append(small_gl[s])
            deltas.append(s_delta[s])
            new_m.append(s_m[s])
            new_v.append(s_v[s])
    return (loss, grad_x[None], *grads, *deltas, *new_m, *new_v)
```

```python
import jax
import jax.numpy as jnp
import numpy as np
from jax import lax
from jax.experimental import pallas as pl
from jax.experimental.pallas import tpu as pltpu

F32 = jnp.float32
BF16 = jnp.bfloat16

D_MODEL = 1024
HEADS = 8
HEAD_DIM = 64
PAIRS = HEADS // 2
WIDTH = HEADS * HEAD_DIM
CHUNK = 64
CONV_K = 4
D_FF = 2816
FF_SHARD = D_FF // 4
EPS = 1e-6
SCALE = HEAD_DIM ** -0.5
LANES = 128
N_CHIPS = 4
D_IN = 4120
D_CAT = 4224
COL_SMALL = 4096 // LANES

ADAM_LR = 0.001
ADAM_B1 = 0.9
ADAM_B2 = 0.999
ADAM_EPS = 1e-08
ADAM_WD = 0.01
ADAM_STEP = 10

VMEM_LIMIT = 56 * 1024 * 1024
MESH = pl.DeviceIdType.MESH
HIGHEST = lax.Precision.HIGHEST


def _params(sem):
    return pltpu.CompilerParams(dimension_semantics=sem, vmem_limit_bytes=VMEM_LIMIT)


_CONTRACT = {"nn": ((1,), (0,)), "nt": ((1,), (1,)), "tn": ((0,), (0,))}


def _mm(a, b, *, dims, name, out_dtype=F32, add=None, tm=1024, tn=512, tk=512):
    if dims == "nn":
        (m, k), (k2, n) = a.shape, b.shape
    elif dims == "nt":
        (m, k), (n, k2) = a.shape, b.shape
    else:
        (k, m), (k2, n) = a.shape, b.shape
    assert k == k2, (a.shape, b.shape, dims)
    tm, tn, tk = min(tm, m), min(tn, n), min(tk, k)
    assert m % tm == 0 and n % tn == 0 and k % tk == 0, (m, n, k, tm, tn, tk)
    nk = k // tk
    a_spec = (pl.BlockSpec((tk, tm), lambda i, j, kk: (kk, i)) if dims == "tn"
              else pl.BlockSpec((tm, tk), lambda i, j, kk: (i, kk)))
    b_spec = (pl.BlockSpec((tn, tk), lambda i, j, kk: (j, kk)) if dims == "nt"
              else pl.BlockSpec((tk, tn), lambda i, j, kk: (kk, j)))
    o_spec = pl.BlockSpec((tm, tn), lambda i, j, kk: (i, j))
    contract = (_CONTRACT[dims], ((), ()))
    has_add = add is not None

    def body(*refs):
        a_ref, b_ref = refs[:2]
        add_ref = refs[2] if has_add else None
        o_ref = refs[3] if has_add else refs[2]
        part = lax.dot_general(a_ref[...].astype(BF16), b_ref[...].astype(BF16), contract,
                               preferred_element_type=F32)

        def finish(r):
            if has_add:
                r = r + add_ref[...].astype(F32)
            o_ref[...] = r.astype(out_dtype)

        if nk == 1:
            finish(part)
            return
        acc = refs[-1]
        kk = pl.program_id(2)

        @pl.when(kk == 0)
        def _():
            acc[...] = part

        @pl.when(kk > 0)
        def _():
            acc[...] += part

        @pl.when(kk == nk - 1)
        def _():
            finish(acc[...])

    ins = [a, b] + ([add] if has_add else [])
    in_specs = [a_spec, b_spec] + ([o_spec] if has_add else [])
    return pl.pallas_call(
        body, name=name, grid=(m // tm, n // tn, nk),
        in_specs=in_specs, out_specs=o_spec,
        out_shape=jax.ShapeDtypeStruct((m, n), out_dtype),
        scratch_shapes=[pltpu.VMEM((tm, tn), F32)] if nk > 1 else [],
        compiler_params=_params(("parallel", "parallel", "arbitrary")),
    )(*ins)


def _mm_blocks(a, b, *, name, grid, a_spec, b_spec, o_spec, out_shape, dims, n_sum=0, add=None, add_spec=None,
               epilogue=None, extra=(), n_acc=0):
    contract = (_CONTRACT[dims], ((), ()))
    has_add = add is not None
    n_in = 2 + has_add + len(extra)

    def body(*refs):
        a_ref, b_ref = refs[:2]
        dot = lambda x, y: lax.dot_general(x.astype(BF16), y.astype(BF16), contract, preferred_element_type=F32)
        if n_sum:
            r = dot(a_ref[0], b_ref[0])
            for s in range(1, n_sum):
                r = r + dot(a_ref[s], b_ref[s])
        else:
            r = dot(a_ref[...], b_ref[...])
        if has_add:
            r = r + refs[2][...].astype(F32)
        if epilogue is None:
            refs[-1][...] = r.astype(refs[-1].dtype)
        else:
            outs = epilogue(r, *[e[...] for e in refs[2 + has_add:n_in]])
            out_refs = refs[n_in:]
            n_plain = len(out_refs) - n_acc
            for o_ref, val in zip(out_refs[:n_plain], outs):
                o_ref[...] = val.astype(o_ref.dtype)
            if n_acc:
                @pl.when(pl.program_id(0) == 0)
                def _():
                    for o_ref in out_refs[n_plain:]:
                        o_ref[...] = jnp.zeros_like(o_ref)
                for o_ref, val in zip(out_refs[n_plain:], outs[n_plain:]):
                    o_ref[...] += val

    ins = [a, b] + ([add] if has_add else []) + [e[0] for e in extra]
    in_specs = [a_spec, b_spec] + ([add_spec] if has_add else []) + [e[1] for e in extra]
    sem = ("arbitrary" if n_acc else "parallel",) * len(grid)
    return pl.pallas_call(
        body, name=name, grid=grid, in_specs=in_specs, out_specs=o_spec, out_shape=out_shape,
        compiler_params=_params(sem),
    )(*ins)


def _tiles(fn, *, name, rows, tm, ncol=1, row_ins=(), col_consts=(), full_consts=(),
           row_outs=(), acc_outs=()):
    nt = rows // tm
    assert rows % tm == 0
    n_full, n_col, n_row = len(full_consts), len(col_consts), len(row_ins)
    n_ro, n_acc = len(row_outs), len(acc_outs)

    def body(*refs):
        ins = refs[:n_full + n_col + n_row]
        outs = refs[n_full + n_col + n_row:]
        i = pl.program_id(1)
        res = fn(pl.program_id(0), *[r[...] for r in ins])
        for r, v in zip(outs[:n_ro], res[:n_ro]):
            r[...] = v.astype(r.dtype)
        if n_acc:
            @pl.when(i == 0)
            def _():
                for r in outs[n_ro:]:
                    r[...] = jnp.zeros_like(r)
            for r, v in zip(outs[n_ro:], res[n_ro:]):
                r[...] += v

    in_specs = [pl.BlockSpec(a.shape, lambda j, i, nd=a.ndim: (0,) * nd) for a in full_consts]
    in_specs += [pl.BlockSpec((nr, w), lambda j, i, o=o: (0, o + j)) for (_, nr, w, o) in col_consts]
    in_specs += [pl.BlockSpec((tm, w), lambda j, i, o=o: (i, o + j)) for (_, w, o) in row_ins]
    out_specs = [pl.BlockSpec((tm, w), lambda j, i: (i, j)) for (w, _) in row_outs]
    out_specs += [pl.BlockSpec((nr, w), lambda j, i: (0, j)) for (nr, w) in acc_outs]
    out_shape = [jax.ShapeDtypeStruct((rows, w * ncol), dt) for (w, dt) in row_outs]
    out_shape += [jax.ShapeDtypeStruct((nr, w * ncol), F32) for (nr, w) in acc_outs]
    args = list(full_consts) + [c[0] for c in col_consts] + [r[0] for r in row_ins]
    out = pl.pallas_call(
        body, name=name, grid=(ncol, nt), in_specs=in_specs, out_specs=out_specs, out_shape=out_shape,
        compiler_params=_params(("parallel", "arbitrary")),
    )(*args)
    return out


def _rms(x, w):
    return x * lax.rsqrt(jnp.mean(x * x, axis=-1, keepdims=True) + EPS) * w


def _lane_lo(shape):
    return lax.broadcasted_iota(jnp.int32, shape, len(shape) - 1) < HEAD_DIM


def _pair_sum(x):
    lo = _lane_lo(x.shape)
    s0 = jnp.sum(jnp.where(lo, x, 0.0), axis=-1, keepdims=True)
    s1 = jnp.sum(jnp.where(lo, 0.0, x), axis=-1, keepdims=True)
    return jnp.where(lo, s0, s1)


def _head_col(x, lo, h):
    keep = lo if h == 0 else jnp.logical_not(lo)
    return jnp.max(jnp.where(keep, x, -jnp.inf), axis=-1, keepdims=True)


def _softplus(x):
    return jnp.maximum(x, 0.0) + jnp.log1p(jnp.exp(-jnp.abs(x)))


def _silu(x):
    return x * jax.nn.sigmoid(x)


def _dot(a, b, contract):
    return lax.dot_general(a.astype(BF16), b.astype(BF16), (contract, ((), ())),
                           preferred_element_type=F32)


def _dot32(a, b, contract):
    return lax.dot_general(a, b, (contract, ((), ())), precision=HIGHEST, preferred_element_type=F32)


def _bd(y):
    yy = jnp.concatenate([y, y], axis=0)
    r = lax.broadcasted_iota(jnp.int32, yy.shape, 0) < HEAD_DIM
    c = lax.broadcasted_iota(jnp.int32, yy.shape, 1) < HEAD_DIM
    return jnp.where(r == c, yy, 0.0)


def _pp(x, y):
    return _dot(x, _bd(y), _CONTRACT["nn"])


def _pp_nt(x, y):
    return _dot(x, _bd(y), _CONTRACT["nt"])


def _pp_tn(x, y):
    full = _dot(x, y, _CONTRACT["tn"])
    return jnp.where(_lane_lo((HEAD_DIM, LANES)), full[:HEAD_DIM], full[HEAD_DIM:])


def _gdn_masks():
    row = lax.broadcasted_iota(jnp.int32, (CHUNK, LANES), 0)
    col = lax.broadcasted_iota(jnp.int32, (CHUNK, LANES), 1) % HEAD_DIM
    return row, col


def _interleave(chains):
    live = list(chains)
    while live:
        for g in list(live):
            try:
                next(g)
            except StopIteration:
                live.remove(g)


def _gdn_forward(qk, v, betax, gcx, grow, rows):
    nchunk = rows // CHUNK

    def body(q_ref, k_ref, v_ref, bx_ref, gx_ref, gr_ref, o_ref, ss_ref, ts_ref, state):
        n = pl.program_id(0)

        @pl.when(n == 0)
        def _():
            state[...] = jnp.zeros_like(state)

        row, col = _gdn_masks()
        incl, strict = col <= row, col < row

        def chain(p):
            lanes = pl.ds(p * LANES, LANES)
            q, k, v, bx, gx = q_ref[:, lanes], k_ref[:, lanes], v_ref[:, lanes], bx_ref[:, lanes], gx_ref[:, lanes]
            gr = gr_ref[0, p]
            glast = gx_ref[pl.ds(CHUNK - 1, 1), lanes]
            s = state[p]
            dm = jnp.where(incl, jnp.exp(jnp.minimum(gx - gr, 0.0)), 0.0)
            kb, vb, eg, qs = k * bx, v * bx, jnp.exp(gx), q * SCALE
            yield
            big_g, big_p = _pp_nt(kb, k), _pp_nt(qs, k)
            yield
            x = -jnp.where(strict, big_g * dm, 0.0)
            att = jnp.where(incl, big_p * dm, 0.0)
            tm = jnp.where(row == col, 1.0, 0.0) + x
            x = _pp(x, x)
            yield
            for _ in range(4):
                step, x = _pp(tm, x), _pp(x, x)
                yield
                tm = tm + step
            tm = tm + _pp(tm, x)
            yield
            u, w = _pp(tm, vb), _pp(tm, kb * eg)
            yield
            ws, qgs = _pp(w, s), _pp(qs * eg, s)
            yield
            vn = u - ws
            kd = k * jnp.exp(glast - gx)
            avn, upd = _pp(att, vn), _pp_tn(kd, vn)
            yield
            ss_ref[0, p] = s
            ts_ref[0, p] = tm
            o_ref[:, lanes] = qgs + avn
            state[p] = s * jnp.exp(glast) + upd

        _interleave([chain(p) for p in range(PAIRS)])

    blk = lambda j: pl.BlockSpec((CHUNK, WIDTH), lambda n, j=j: (n, j))
    sv = pl.BlockSpec((1, PAIRS, CHUNK, LANES), lambda n: (n, 0, 0, 0))
    return pl.pallas_call(
        body, name="gdn_fwd", grid=(nchunk,),
        in_specs=[blk(0), blk(1), blk(0), blk(0), blk(0),
                  pl.BlockSpec((1, PAIRS, 1, LANES), lambda n: (n, 0, 0, 0))],
        out_specs=[blk(0), sv, sv],
        out_shape=[jax.ShapeDtypeStruct((rows, WIDTH), F32),
                   jax.ShapeDtypeStruct((nchunk, PAIRS, CHUNK, LANES), F32),
                   jax.ShapeDtypeStruct((nchunk, PAIRS, CHUNK, LANES), F32)],
        scratch_shapes=[pltpu.VMEM((PAIRS, CHUNK, LANES), F32)],
        compiler_params=_params(("arbitrary",)),
    )(qk, qk, v, betax, gcx, grow)


def _gdn_backward(qk, v, betax, gcx, grow, ssave, tsave, do, rows):
    nchunk = rows // CHUNK

    def body(q_ref, k_ref, v_ref, bx_ref, gx_ref, gr_ref, ss_ref, ts_ref, do_ref,
             dq_ref, dk_ref, dv_ref, dbx_ref, dgx_ref, dgr_ref, dstate):
        n = pl.program_id(0)

        @pl.when(n == 0)
        def _():
            dstate[...] = jnp.zeros_like(dstate)

        row, col = _gdn_masks()
        incl, strict = col <= row, col < row

        def chain(p):
            lanes = pl.ds(p * LANES, LANES)
            q, k, v, bx, gx = q_ref[:, lanes], k_ref[:, lanes], v_ref[:, lanes], bx_ref[:, lanes], gx_ref[:, lanes]
            gr = gr_ref[0, p]
            glast = gx_ref[pl.ds(CHUNK - 1, 1), lanes]
            s, tm, d_o = ss_ref[0, p], ts_ref[0, p], do_ref[:, lanes]
            ds_out = dstate[p]
            dm = jnp.where(incl, jnp.exp(jnp.minimum(gx - gr, 0.0)), 0.0)
            kb, vb, eg, qs = k * bx, v * bx, jnp.exp(gx), q * SCALE
            kbg, qg = kb * eg, qs * eg
            ed = jnp.exp(glast - gx)
            kd = k * ed
            eglast = jnp.exp(glast)
            yield
            big_g, big_p = _pp_nt(kb, k), _pp_nt(qs, k)
            u, w = _pp(tm, vb), _pp(tm, kbg)
            dqg, kds = _pp_nt(d_o, s), _pp(kd, ds_out)
            yield
            low = jnp.where(strict, big_g * dm, 0.0)
            att = jnp.where(incl, big_p * dm, 0.0)
            ws, atd = _pp(w, s), _pp_tn(att, d_o)
            yield
            vn = u - ws
            dvn = kds + atd
            dkd, datt_raw = _pp_nt(vn, ds_out), _pp_nt(d_o, vn)
            dw_neg, dvb = _pp_nt(dvn, s), _pp_tn(tm, dvn)
            dtm_a, wdv = _pp_nt(dvn, vb), _pp_tn(w, dvn)
            qgd = _pp_tn(qg, d_o)
            yield
            datt = jnp.where(incl, datt_raw, 0.0)
            dw = -dw_neg
            dtm_b, dkbg = _pp_nt(dw, kbg), _pp_tn(tm, dw)
            dbig_p = datt * dm
            dqs_a, dk_p = _pp(dbig_p, k), _pp_tn(dbig_p, qs)
            yield
            inner = _pp_tn(tm, dtm_a + dtm_b)
            yield
            dlow = jnp.where(strict, -_pp_nt(inner, tm), 0.0)
            yield
            dbig_g = dlow * dm
            dkb_a, dk_g = _pp(dbig_g, k), _pp_tn(dbig_g, kb)
            yield
            dkb = dkb_a + dkbg * eg
            dqs = dqs_a + dqg * eg
            dk = dk_g + dk_p + dkd * ed + dkb * bx
            z = dlow * low + datt * att
            kdterm = dkd * kd
            dglast = (jnp.sum(ds_out * s, axis=0, keepdims=True) * eglast
                      + jnp.sum(kdterm, axis=0, keepdims=True))
            dgx = dqg * qg + dkbg * kbg - kdterm
            dgx = dgx + jnp.where(col == 0, _pair_sum(z), 0.0)
            dgx = dgx + jnp.where(row == CHUNK - 1, dglast, 0.0)
            dq_ref[:, lanes] = dqs * SCALE
            dk_ref[:, lanes] = dk
            dv_ref[:, lanes] = dvb * bx
            dbx_ref[:, lanes] = dkb * k + dvb * v
            dgx_ref[:, lanes] = dgx
            dgr_ref[0, p] = -jnp.sum(z, axis=0, keepdims=True)
            dstate[p] = ds_out * eglast + qgd - wdv

        _interleave([chain(p) for p in range(PAIRS)])

    last = nchunk - 1
    blk = lambda j: pl.BlockSpec((CHUNK, WIDTH), lambda n, j=j: (last - n, j))
    sv = pl.BlockSpec((1, PAIRS, CHUNK, LANES), lambda n: (last - n, 0, 0, 0))
    gr_spec = pl.BlockSpec((1, PAIRS, 1, LANES), lambda n: (last - n, 0, 0, 0))
    wide = jax.ShapeDtypeStruct((rows, WIDTH), F32)
    return pl.pallas_call(
        body, name="gdn_bwd", grid=(nchunk,),
        in_specs=[blk(0), blk(1), blk(0), blk(0), blk(0), gr_spec, sv, sv, blk(0)],
        out_specs=[blk(0)] * 5 + [gr_spec],
        out_shape=[wide] * 5 + [jax.ShapeDtypeStruct((nchunk, PAIRS, 1, LANES), F32)],
        scratch_shapes=[pltpu.VMEM((PAIRS, CHUNK, LANES), F32)],
        compiler_params=_params(("arbitrary",)),
    )(qk, qk, v, betax, gcx, grow, ssave, tsave, do)


ATT_TQ = 256


def _att_scores(qh, kt, fk, diag):
    s = _dot(qh, kt, _CONTRACT["nt"]) - fk
    if diag:
        r = lax.broadcasted_iota(jnp.int32, s.shape, 0)
        c = lax.broadcasted_iota(jnp.int32, s.shape, 1)
        s = jnp.where(r >= c, s, -jnp.inf)
    return s


def _head_masks(n):
    lo = _lane_lo((n, LANES))
    return [lo, jnp.logical_not(lo)]


def _attention_forward(fqk, proj, frow, rows):
    tq = tk = min(ATT_TQ, rows)
    nq = rows // tq
    v_off = 3072 // LANES

    def body(q_ref, k_ref, v_ref, fr_ref, o_ref, lse_ref):
        qi = pl.program_id(1)
        q = q_ref[...] * SCALE
        keep_q, keep_k = _head_masks(tq), _head_masks(tk)
        qh = [jnp.where(keep_q[h], q, 0.0).astype(BF16) for h in range(2)]

        def tile(ki, carry, diag):
            k0 = pl.multiple_of(ki * tk, tk)
            kt = k_ref[pl.ds(k0, tk), :].astype(BF16)
            v_t = v_ref[pl.ds(k0, tk), :]
            out = [None, None]

            def chain(h):
                m, l, acc = carry[h]
                vt = jnp.where(keep_k[h], v_t, 0.0).astype(BF16)
                yield
                s = _att_scores(qh[h], kt, fr_ref[0, pl.ds(h, 1), pl.ds(k0, tk)], diag)
                yield
                m_new = jnp.maximum(m, jnp.max(s, axis=-1, keepdims=True))
                p = jnp.exp(s - m_new)
                alpha = jnp.exp(m - m_new)
                l = alpha * l + jnp.sum(p, axis=-1, keepdims=True)
                p_hi = p.astype(BF16)
                p_lo = p - p_hi.astype(F32)
                yield
                out[h] = (m_new, l, alpha * acc + _dot(p_hi, vt, _CONTRACT["nn"]) + _dot(p_lo, vt, _CONTRACT["nn"]))

            _interleave([chain(0), chain(1)])
            return tuple(out)

        one = (jnp.full((tq, 1), -jnp.inf, F32), jnp.zeros((tq, 1), F32), jnp.zeros((tq, LANES), F32))
        carry = lax.fori_loop(0, qi, lambda ki, c: tile(ki, c, False), (one, one))
        (m0, l0, acc0), (m1, l1, acc1) = tile(qi, carry, True)
        o_ref[...] = acc0 / l0 + acc1 / l1
        lse_ref[...] = jnp.where(keep_q[0], m0 + jnp.log(l0), m1 + jnp.log(l1))

    whole = lambda off: pl.BlockSpec((rows, LANES), lambda p, i, off=off: (0, off + p))
    qblk = lambda off: pl.BlockSpec((tq, LANES), lambda p, i, off=off: (i, off + p))
    wide = jax.ShapeDtypeStruct((rows, WIDTH), F32)
    return pl.pallas_call(
        body, name="fox_fwd", grid=(PAIRS, nq),
        in_specs=[qblk(0), whole(PAIRS), whole(v_off), pl.BlockSpec((1, 2, rows), lambda p, i: (p, 0, 0))],
        out_specs=[qblk(0), qblk(0)], out_shape=[wide, wide],
        compiler_params=_params(("parallel", "arbitrary")),
    )(fqk, fqk, proj, frow)


def _attention_backward(fqk, proj, frow, ao, lse, dao, rows):
    tq = tk = min(ATT_TQ, rows)
    nq = rows // tq
    v_off = 3072 // LANES

    def body(q_ref, k_ref, v_ref, fr_ref, o_ref, lse_ref, do_ref, dq_ref, dk_ref, dv_ref, dfr_ref):
        ki = pl.program_id(1)

        @pl.when(ki == 0)
        def _():
            dq_ref[...] = jnp.zeros_like(dq_ref)

        keep_q, keep_k = _head_masks(tq), _head_masks(tk)
        k_t = k_ref[...]
        kt = k_t.astype(BF16)
        vt = v_ref[...].astype(BF16)
        kh = [jnp.where(keep_k[h], k_t, 0.0).astype(BF16) for h in range(2)]
        fk = [fr_ref[0, pl.ds(h, 1), :] for h in range(2)]

        def tile(qi, carry, diag):
            dk, dv, df0, df1 = carry
            rows_q = pl.ds(pl.multiple_of(qi * tq, tq), tq)
            q, d_o, lse_t = q_ref[rows_q, :] * SCALE, do_ref[rows_q, :], lse_ref[rows_q, :]
            delta_x = _pair_sum(d_o.astype(BF16).astype(F32) * o_ref[rows_q, :])
            res = [None, None]

            def chain(h):
                qh = jnp.where(keep_q[h], q, 0.0).astype(BF16)
                doh = jnp.where(keep_q[h], d_o, 0.0).astype(BF16)
                lse_h, delta_h = _head_col(lse_t, keep_q[0], h), _head_col(delta_x, keep_q[0], h)
                yield
                s, dp = _att_scores(qh, kt, fk[h], diag), _dot(doh, vt, _CONTRACT["nt"])
                yield
                p = jnp.exp(s - lse_h)
                ds = p * (dp - delta_h)
                yield
                res[h] = (_dot(p, doh, _CONTRACT["tn"]), _dot(ds, qh, _CONTRACT["tn"]),
                          _dot(ds, kh[h], _CONTRACT["nn"]), jnp.sum(ds, axis=0, keepdims=True))

            _interleave([chain(0), chain(1)])
            (dv0, dk0, dq0, s0), (dv1, dk1, dq1, s1) = res
            dq_ref[rows_q, :] += (dq0 + dq1) * SCALE
            return dk + dk0 + dk1, dv + dv0 + dv1, df0 - s0, df1 - s1

        zero_kv = jnp.zeros((tk, LANES), F32)
        zero_f = jnp.zeros((1, tk), F32)
        carry = tile(ki, (zero_kv, zero_kv, zero_f, zero_f), True)
        dk, dv, df0, df1 = lax.fori_loop(ki + 1, nq, lambda qi, c: tile(qi, c, False), carry)
        dk_ref[...] = dk
        dv_ref[...] = dv.astype(dv_ref.dtype)
        dfr_ref[0, pl.ds(0, 1), :] = df0
        dfr_ref[0, pl.ds(1, 1), :] = df1

    whole = lambda off: pl.BlockSpec((rows, LANES), lambda p, i, off=off: (0, off + p))
    kblk = lambda off: pl.BlockSpec((tk, LANES), lambda p, i, off=off: (i, off + p))
    fr_spec = pl.BlockSpec((1, 2, tk), lambda p, i: (p, 0, i))
    wide = jax.ShapeDtypeStruct((rows, WIDTH), F32)
    return pl.pallas_call(
        body, name="fox_bwd", grid=(PAIRS, nq),
        in_specs=[whole(0), kblk(PAIRS), kblk(v_off), fr_spec, whole(0), whole(0), whole(0)],
        out_specs=[whole(0), kblk(0), kblk(0), fr_spec],
        out_shape=[wide, wide, jax.ShapeDtypeStruct((rows, WIDTH), BF16),
                   jax.ShapeDtypeStruct((PAIRS, 2, rows), F32)],
        compiler_params=_params(("parallel", "arbitrary")),
    )(fqk, fqk, proj, frow, ao, lse, dao)


def _lane_ids(shape):
    return lax.broadcasted_iota(jnp.int32, shape, len(shape) - 1)


def _gates_elem(a_log, dt_bias, f_bias, pre):
    lane = _lane_ids(pre.shape)
    beta = jax.nn.sigmoid(pre)
    g = -jnp.exp(a_log) * _softplus(pre + dt_bias)
    lf = -_softplus(-(pre + f_bias))
    return jnp.where(lane < 8, beta, jnp.where(lane < 16, g, jnp.where(lane < 24, lf, 0.0)))


def _tri_consts():
    r = np.arange(LANES)[:, None]
    c = np.arange(LANES)[None, :]
    full = (c <= r).astype(np.float32)
    chunked = full * ((r // CHUNK) == (c // CHUNK))
    return jnp.asarray(chunked), jnp.asarray(full)


def _cums_fwd(lc, lf, gates):
    rows = gates.shape[0]
    lane = _lane_ids((LANES, LANES))
    carry = jnp.zeros((1, LANES), F32)
    out = []
    for r in range(rows // LANES):
        blk = gates[r * LANES:(r + 1) * LANES]
        gc = _dot32(lc, blk, _CONTRACT["nn"])
        f = _dot32(lf, blk, _CONTRACT["nn"]) + carry
        carry = carry + jnp.sum(blk, axis=0, keepdims=True)
        out.append(jnp.where((lane >= 8) & (lane < 16), gc, jnp.where((lane >= 16) & (lane < 24), f, 0.0)))
    return jnp.concatenate(out, axis=0)


def _cums_bwd(lc, lf, dcums):
    rows = dcums.shape[0]
    lane = _lane_ids((LANES, LANES))
    is_g = (lane >= 8) & (lane < 16)
    is_f = (lane >= 16) & (lane < 24)
    carry = jnp.zeros((1, LANES), F32)
    out = [None] * (rows // LANES)
    for r in reversed(range(rows // LANES)):
        blk = dcums[r * LANES:(r + 1) * LANES]
        dg = jnp.where(is_g, blk, 0.0)
        df = jnp.where(is_f, blk, 0.0)
        out[r] = _dot32(lc, dg, _CONTRACT["tn"]) + _dot32(lf, df, _CONTRACT["tn"]) + carry
        carry = carry + jnp.sum(df, axis=0, keepdims=True)
    return jnp.concatenate(out, axis=0)


def _expand_consts():
    xb = np.zeros((LANES, WIDTH), np.float32)
    xg = np.zeros((LANES, WIDTH), np.float32)
    for h in range(HEADS):
        xb[h, h * HEAD_DIM:(h + 1) * HEAD_DIM] = 1.0
        xg[8 + h, h * HEAD_DIM:(h + 1) * HEAD_DIM] = 1.0
    return jnp.asarray(xb), jnp.asarray(xg)


def _shift_down(x, s):
    if s == 0:
        return x
    row = lax.broadcasted_iota(jnp.int32, x.shape, 0)
    return jnp.where(row >= s, pltpu.roll(x, s, 0), 0.0)


def _shift_up(x, s):
    if s == 0:
        return x
    n = x.shape[0]
    row = lax.broadcasted_iota(jnp.int32, x.shape, 0)
    return jnp.where(row < n - s, pltpu.roll(x, n - s, 0), 0.0)


def _row_of(cw, i):
    row = lax.broadcasted_iota(jnp.int32, cw.shape, 0)
    return jnp.sum(jnp.where(row == i, cw, 0.0), axis=0, keepdims=True)


def _conv(cw, x):
    c = jnp.zeros_like(x)
    for i in range(CONV_K):
        c = c + _row_of(cw, i) * _shift_down(x, CONV_K - 1 - i)
    return c


def _post_conv(is_qk, c):
    s = _silu(c)
    return s * lax.rsqrt(_pair_sum(s * s) + EPS) if is_qk else s


def _gdn_prep_bwd(is_qk, cw, x, dy):
    c = _conv(cw, x)
    _, vjp = jax.vjp(lambda cc: _post_conv(is_qk, cc), c)
    (dc,) = vjp(dy)
    dx = jnp.zeros_like(x)
    row = lax.broadcasted_iota(jnp.int32, cw.shape, 0)
    dcw = jnp.zeros(cw.shape, F32)
    for i in range(CONV_K):
        s = CONV_K - 1 - i
        dx = dx + _row_of(cw, i) * _shift_up(dc, s)
        dcw = dcw + jnp.where(row == i, jnp.sum(dc * _shift_down(x, s), axis=0, keepdims=True), 0.0)
    return dx, dcw


def _head_rms(w, x):
    return x * lax.rsqrt(_pair_sum(x * x) / HEAD_DIM + EPS) * w


def _cat_weights(w_in_t):
    tail = jnp.pad(w_in_t[4112:4120], ((0, D_CAT - D_IN), (0, 0)))
    return jnp.concatenate([w_in_t[:2048], w_in_t[2064:4112], w_in_t[2048:2064], tail], axis=0)


def _uncat_grad(g):
    return jnp.concatenate([g[:2048], g[4096:4112], g[2048:4096], g[4112:4120]], axis=0)


def _lanes_to_rowform(v8, rows):
    return v8.reshape(rows // CHUNK, CHUNK, HEADS).transpose(0, 2, 1).reshape(rows // CHUNK, PAIRS, 1, LANES)


def _rowform_to_lanes(v, rows):
    return v.reshape(rows // CHUNK, HEADS, CHUNK).transpose(0, 2, 1).reshape(rows, HEADS)


def _local_step(x, target, norm1_w, a_log, dt_bias, out_norm_w, f_bias, q_norm_w, k_norm_w,
                norm2_w, final_w, first_weights, late_weights, early_grads_ready, early_grads_continue):
    rows = x.shape[0]
    tm = min(512, rows)
    lc, lf = _tri_consts()
    xb, xg = _expand_consts()

    (h1,) = _tiles(lambda col, w, xx: (_rms(xx, w),), name="norm1", rows=rows, tm=tm,
                   full_consts=[norm1_w], row_ins=[(x, D_MODEL, 0)], row_outs=[(D_MODEL, BF16)])
    w_cat, conv_w = first_weights(h1)
    proj = _mm(h1, w_cat, dims="nt", name="in_proj", tn=1408, tk=1024)

    lane_pad = lambda v, off: jnp.pad(v.reshape(1, -1), ((0, 0), (off, LANES - off - v.size)))
    p_a, p_dt, p_fb = lane_pad(a_log, 8), lane_pad(dt_bias, 8), lane_pad(f_bias, 16)

    def gates_fwd(col, lcv, lfv, a, dt, fb, pre):
        gates = _gates_elem(a, dt, fb, pre)
        return gates, _cums_fwd(lcv, lfv, gates)

    gates, cums = _tiles(gates_fwd, name="gates", rows=rows, tm=rows,
                         full_consts=[lc, lf, p_a, p_dt, p_fb], row_ins=[(proj, LANES, COL_SMALL)],
                         row_outs=[(LANES, F32), (LANES, F32)])

    def expand_fwd(col, b, g, gt, cm):
        return (_dot32(gt, b, _CONTRACT["nn"]), _dot32(cm, g, _CONTRACT["nn"]))

    betax, gcx = _tiles(expand_fwd, name="expand", rows=rows, tm=tm, full_consts=[xb, xg],
                        row_ins=[(gates, LANES, 0), (cums, LANES, 0)],
                        row_outs=[(WIDTH, F32)] * 2)
    grow = _lanes_to_rowform(cums[:, 8:16], rows)
    frow = cums[:, 16:24].T.reshape(PAIRS, 2, rows)

    prep = lambda is_qk: (lambda col, cw, xx: (_post_conv(is_qk, _conv(cw, xx)),))
    (g_qk,) = _tiles(prep(True), name="gdn_prep_qk", rows=rows, tm=rows, ncol=2 * PAIRS,
                     col_consts=[(conv_w, CONV_K, LANES, 0)], row_ins=[(proj, LANES, 0)],
                     row_outs=[(LANES, F32)])
    (g_v,) = _tiles(prep(False), name="gdn_prep_v", rows=rows, tm=rows, ncol=PAIRS,
                    col_consts=[(conv_w, CONV_K, LANES, 2 * PAIRS)], row_ins=[(proj, LANES, 2 * PAIRS)],
                    row_outs=[(LANES, F32)])
    o_gdn, ssave, tsave = _gdn_forward(g_qk, g_v, betax, gcx, grow, rows)

    w_qk = jnp.concatenate([jnp.tile(q_norm_w.reshape(1, -1), (1, HEADS)),
                            jnp.tile(k_norm_w.reshape(1, -1), (1, HEADS))], axis=1)
    fox_off = 2048 // LANES
    (fqk,) = _tiles(lambda col, w, xx: (_head_rms(w, xx),), name="fox_prep", rows=rows, tm=rows, ncol=2 * PAIRS,
                    col_consts=[(w_qk, 1, LANES, 0)], row_ins=[(proj, LANES, fox_off)],
                    row_outs=[(LANES, F32)])
    ao, lse = _attention_forward(fqk, proj, frow, rows)

    w_on = jnp.tile(out_norm_w.reshape(1, -1), (1, 2))
    z_off, fg_off = 1536 // LANES, 3584 // LANES
    mix_g_fn = lambda w, o, z: _head_rms(w, o) * _silu(z)
    mix_f_fn = lambda a, g: a * jax.nn.sigmoid(g)
    (mix_g,) = _tiles(lambda col, w, o, z: (mix_g_fn(w, o, z),), name="mix_gdn", rows=rows, tm=rows, ncol=PAIRS,
                      full_consts=[w_on], row_ins=[(o_gdn, LANES, 0), (proj, LANES, z_off)],
                      row_outs=[(LANES, BF16)])
    (mix_f,) = _tiles(lambda col, a, g: (mix_f_fn(a, g),), name="mix_fox", rows=rows, tm=rows, ncol=PAIRS,
                      row_ins=[(ao, LANES, 0), (proj, LANES, fg_off)], row_outs=[(LANES, BF16)])
    mix = jnp.concatenate([mix_g, mix_f], axis=1)
    w_out, w_gate, w_up, w_down = late_weights(mix)
    t_rows, t_half = min(1024, rows), min(512, rows)
    n_rt = rows // t_rows
    row_blk = pl.BlockSpec((t_rows, D_MODEL), lambda i, n: (i, 0))
    half_blk = pl.BlockSpec((t_half, D_MODEL), lambda i, n: (i, 0))
    vec_blk = pl.BlockSpec((1, D_MODEL), lambda i, n: (0, 0))
    wide = lambda dt: jax.ShapeDtypeStruct((rows, D_MODEL), dt)
    x1, h2 = _mm_blocks(mix, w_out, name="out_proj_norm2", grid=(n_rt, 1), dims="nn",
                        a_spec=row_blk, b_spec=pl.BlockSpec((D_MODEL, D_MODEL), lambda i, n: (0, 0)),
                        o_spec=[row_blk, row_blk], out_shape=[wide(F32), wide(BF16)], add=x, add_spec=row_blk,
                        extra=[(norm2_w, vec_blk)], epilogue=lambda r, w: (r, _rms(r, w)))
    st_act = jax.ShapeDtypeStruct((N_CHIPS, rows, FF_SHARD), BF16)
    st_rows = pl.BlockSpec((None, rows, FF_SHARD), lambda i, j: (j, i, 0))

    def ffn_in(w_st, name):
        return _mm_blocks(h2, w_st, name=name, grid=(1, N_CHIPS), dims="nt",
                          a_spec=pl.BlockSpec((rows, D_MODEL), lambda i, j: (i, 0)),
                          b_spec=pl.BlockSpec((None, FF_SHARD, D_MODEL), lambda i, j: (j, 0, 0)),
                          o_spec=st_rows, out_shape=st_act)

    gate = ffn_in(w_gate, "ffn_gate")
    act_fn = lambda g, u: _silu(g) * u
    st_tile = pl.BlockSpec((None, t_rows, FF_SHARD), lambda i, j: (j, i, 0))
    up, act = _mm_blocks(h2, w_up, name="ffn_up_act", grid=(n_rt, N_CHIPS), dims="nt",
                         a_spec=pl.BlockSpec((t_rows, D_MODEL), lambda i, j: (i, 0)),
                         b_spec=pl.BlockSpec((None, FF_SHARD, D_MODEL), lambda i, j: (j, 0, 0)),
                         o_spec=[st_tile, st_tile], out_shape=[st_act, st_act], extra=[(gate, st_tile)],
                         epilogue=lambda u, g: (u, act_fn(g.astype(F32), u)))

    def final_fn(xx, tgt, w):
        y, vjp = jax.vjp(_rms, xx, w)
        err = y - tgt
        loss = 0.5 * jnp.sum(err * err) / D_MODEL
        dx, dw = vjp(err / D_MODEL)
        return dx, dx, jnp.full((1, LANES), loss, F32), dw

    dx2, dx2_b, loss, d_final_w = _mm_blocks(
        act, w_down, name="ffn_down_loss", grid=(rows // t_half, 1), dims="nn", n_sum=N_CHIPS,
        a_spec=pl.BlockSpec((N_CHIPS, t_half, FF_SHARD), lambda i, n: (0, i, 0)),
        b_spec=pl.BlockSpec((N_CHIPS, FF_SHARD, D_MODEL), lambda i, n: (0, 0, 0)),
        o_spec=[half_blk, half_blk, pl.BlockSpec((1, LANES), lambda i, n: (0, 0)), vec_blk],
        out_shape=[wide(F32), wide(BF16), jax.ShapeDtypeStruct((1, LANES), F32),
                   jax.ShapeDtypeStruct((1, D_MODEL), F32)],
        add=x1, add_spec=half_blk, extra=[(target, half_blk), (final_w, vec_blk)], epilogue=final_fn, n_acc=2)

    def act_bwd(d, g, u):
        _, vjp = jax.vjp(act_fn, g.astype(F32), u.astype(F32))
        return vjp(d)

    dgate, dup = _mm_blocks(dx2_b, w_down, name="d_act_gate_up", grid=(n_rt, N_CHIPS), dims="nt",
                            a_spec=pl.BlockSpec((t_rows, D_MODEL), lambda i, j: (i, 0)),
                            b_spec=pl.BlockSpec((None, FF_SHARD, D_MODEL), lambda i, j: (j, 0, 0)),
                            o_spec=[st_tile, st_tile], out_shape=[st_act, st_act],
                            extra=[(gate, st_tile), (up, st_tile)], epilogue=act_bwd)

    def g_ffn(d_st, other, name):
        return _mm_blocks(d_st, other, name=name, grid=(N_CHIPS, 1), dims="tn",
                          a_spec=pl.BlockSpec((None, rows, FF_SHARD), lambda j, n: (j, 0, 0)),
                          b_spec=pl.BlockSpec((rows, D_MODEL), lambda j, n: (0, 0)),
                          o_spec=pl.BlockSpec((None, FF_SHARD, D_MODEL), lambda j, n: (j, 0, 0)),
                          out_shape=jax.ShapeDtypeStruct((N_CHIPS, FF_SHARD, D_MODEL), BF16))

    g_down = g_ffn(act, dx2_b, "g_down")

    def norm_bwd(dh, xx, dres, w):
        _, vjp = jax.vjp(_rms, xx, w)
        dx, dw = vjp(dh)
        return dx + dres, dx + dres, dw

    def d_h2(d_st, w_st, name, add, **fused):
        return _mm_blocks(d_st, w_st, name=name, grid=(rows // t_half, 1), dims="nn", n_sum=N_CHIPS,
                          a_spec=pl.BlockSpec((N_CHIPS, t_half, FF_SHARD), lambda i, n: (0, i, 0)),
                          b_spec=pl.BlockSpec((N_CHIPS, FF_SHARD, D_MODEL), lambda i, n: (0, 0, 0)),
                          add=add, add_spec=half_blk, **fused)

    dh2_gate = d_h2(dgate, w_gate, "d_h2_gate", None, o_spec=half_blk, out_shape=wide(F32))
    dx1, dx1_b, d_norm2_w = d_h2(
        dup, w_up, "d_h2_up_norm2_bwd", dh2_gate, o_spec=[half_blk, half_blk, vec_blk],
        out_shape=[wide(F32), wide(BF16), jax.ShapeDtypeStruct((1, D_MODEL), F32)],
        extra=[(x1, half_blk), (dx2, half_blk), (norm2_w, vec_blk)], epilogue=norm_bwd, n_acc=1)
    g_gate, g_up = g_ffn(dgate, h2, "g_gate"), g_ffn(dup, h2, "g_up")
    dmix = _mm(dx1_b, w_out, dims="nt", name="d_mix", tn=D_MODEL, tk=1024)
    g_out = _mm(mix, dx1_b, dims="tn", name="g_out", tn=D_MODEL, tk=rows, out_dtype=BF16)
    w_on = w_on + early_grads_ready(g_out, g_gate, g_up, g_down)

    def mix_g_bwd(col, w, o, z, d):
        _, vjp = jax.vjp(mix_g_fn, w, o, z)
        dw, do_, dz = vjp(d)
        return do_, dz, dw

    do_gdn, dz, d_on = _tiles(mix_g_bwd, name="mix_gdn_bwd", rows=rows, tm=rows, ncol=PAIRS, full_consts=[w_on],
                              row_ins=[(o_gdn, LANES, 0), (proj, LANES, z_off), (dmix, LANES, 0)],
                              row_outs=[(LANES, F32), (LANES, BF16)], acc_outs=[(1, LANES)])

    def mix_f_bwd(col, a, g, d):
        _, vjp = jax.vjp(mix_f_fn, a, g)
        return vjp(d)

    dao, dfgate = _tiles(mix_f_bwd, name="mix_fox_bwd", rows=rows, tm=rows, ncol=PAIRS,
                         row_ins=[(ao, LANES, 0), (proj, LANES, fg_off), (dmix, LANES, PAIRS)],
                         row_outs=[(LANES, F32), (LANES, BF16)])

    dfq, dfk, dfv, dfrow = _attention_backward(fqk, proj, frow + early_grads_continue(dao), ao, lse, dao, rows)

    def fox_prep_bwd(col, w, xx, d):
        _, vjp = jax.vjp(_head_rms, w, xx)
        dw, dx = vjp(d)
        return dx, dw

    dfqk, d_wqk = [], []
    for part, d_n in enumerate((dfq, dfk)):
        dx_p, dw_p = _tiles(fox_prep_bwd, name="fox_prep_bwd_" + "qk"[part], rows=rows, tm=rows, ncol=PAIRS,
                            col_consts=[(w_qk, 1, LANES, part * PAIRS)],
                            row_ins=[(proj, LANES, fox_off + part * PAIRS), (d_n, LANES, 0)],
                            row_outs=[(LANES, BF16)], acc_outs=[(1, LANES)])
        dfqk.append(dx_p)
        d_wqk.append(dw_p)

    dq, dk, dv, dbetax, dgcx, dgrow = _gdn_backward(g_qk, g_v, betax, gcx, grow, ssave, tsave, do_gdn, rows)
    dqkv, d_conv = [], []
    for part, d_n in enumerate((dq, dk, dv)):
        prep_bwd = lambda col, cw, xx, dy, is_qk=(part < 2): _gdn_prep_bwd(is_qk, cw, xx, dy)
        dx_p, dw_p = _tiles(prep_bwd, name="gdn_prep_bwd_" + "qkv"[part], rows=rows, tm=rows, ncol=PAIRS,
                            col_consts=[(conv_w, CONV_K, LANES, part * PAIRS)],
                            row_ins=[(proj, LANES, part * PAIRS), (d_n, LANES, 0)],
                            row_outs=[(LANES, BF16)], acc_outs=[(CONV_K, LANES)])
        dqkv.append(dx_p)
        d_conv.append(dw_p)
    d_conv = jnp.concatenate(d_conv, axis=1)

    def expand_bwd(col, b, g, db, dg):
        return (_dot32(db, b, _CONTRACT["nt"]), _dot32(dg, g, _CONTRACT["nt"]))

    dgates_b, dcums_g = _tiles(expand_bwd, name="expand_bwd", rows=rows, tm=tm, full_consts=[xb, xg],
                               row_ins=[(dbetax, WIDTH, 0), (dgcx, WIDTH, 0)],
                               row_outs=[(LANES, F32), (LANES, F32)])
    dcums_row = jnp.concatenate([jnp.zeros((rows, 8), F32), _rowform_to_lanes(dgrow, rows),
                                 dfrow.reshape(HEADS, rows).T, jnp.zeros((rows, LANES - 24), F32)], axis=1)

    def gates_bwd(col, lcv, lfv, a, dt, fb, pre, dgb, dcg, dcr):
        lane = _lane_ids(pre.shape)
        dgates = jnp.where(lane < 8, dgb, _cums_bwd(lcv, lfv, dcg + dcr))
        _, vjp = jax.vjp(_gates_elem, a, dt, fb, pre)
        da, ddt, dfb, dpre = vjp(dgates)
        return dpre, da, ddt, dfb

    dpre, d_a, d_dt, d_fb = _tiles(gates_bwd, name="gates_bwd", rows=rows, tm=rows,
                                   full_consts=[lc, lf, p_a, p_dt, p_fb],
                                   row_ins=[(proj, LANES, COL_SMALL), (dgates_b, LANES, 0), (dcums_g, LANES, 0),
                                            (dcums_row, LANES, 0)],
                                   row_outs=[(LANES, BF16)], acc_outs=[(1, LANES)] * 3)

    dproj = jnp.concatenate(dqkv + [dz] + dfqk + [dfv, dfgate, dpre], axis=1)
    grad_x, d_norm1_w = _mm_blocks(
        dproj, w_cat, name="d_h1_norm1_bwd", grid=(rows // t_half, 1), dims="nn",
        a_spec=pl.BlockSpec((t_half, D_CAT), lambda i, n: (i, 0)),
        b_spec=pl.BlockSpec((D_CAT, D_MODEL), lambda i, n: (0, 0)),
        o_spec=[half_blk, vec_blk], out_shape=[wide(F32), jax.ShapeDtypeStruct((1, D_MODEL), F32)],
        extra=[(x, half_blk), (dx1, half_blk), (norm1_w, vec_blk)],
        epilogue=lambda dh, xx, dres, w: norm_bwd(dh, xx, dres, w)[1:], n_acc=1)
    g_cat = _mm(dproj, h1, dims="tn", name="g_in", tm=1408, tn=D_MODEL, tk=rows)

    fold = lambda v: v.reshape(-1, HEAD_DIM).sum(axis=0)
    small = dict(
        loss=loss[0, 0],
        norm1_w=d_norm1_w, conv_w=d_conv, a_log=d_a[0, 8:16], dt_bias=d_dt[0, 8:16],
        out_norm_w=fold(d_on), f_bias=d_fb[0, 16:24], q_norm_w=fold(d_wqk[0]),
        k_norm_w=fold(d_wqk[1]), norm2_w=d_norm2_w, final_w=d_final_w)
    return grad_x, g_cat, g_out, g_gate, g_up, g_down, small


HBM_SPEC = pl.BlockSpec(memory_space=pltpu.HBM)


def _place():
    x, y, c = lax.axis_index("x"), lax.axis_index("y"), lax.axis_index("c")
    chips = [(1 - x, y), (x, 1 - y), (1 - x, 1 - y)]
    return x, y, c, 2 * x + y, (x, y, 1 - c), chips, [2 * cx + cy for cx, cy in chips]


def _remote(src, dst, send_sem, recv_sem, to):
    return pltpu.make_async_remote_copy(src_ref=src, dst_ref=dst, send_sem=send_sem, recv_sem=recv_sem,
                                        device_id=to, device_id_type=MESH)


SEM_SPEC =pl.BlockSpec(memory_space=pltpu.SEMAPHORE)
ANY_SPEC = pl.BlockSpec(memory_space=pl.ANY)
DATAFLOW = pltpu.SideEffectType.DATAFLOW_SIDE_EFFECTING


def _gather_plan(srcs, lands):
    x, y, c, own, sib, chips, chip_idx = _place()
    plan = []
    for src, land in zip(srcs, lands):
        for j, chip in enumerate(chips):
            plan.append((src, land.at[own], (*chip, c), land.at[chip_idx[j]]))
        plan.append((src, land.at[own], sib, land.at[own]))
    return plan


def _exchange_plan(srcs, lands):
    x, y, c, own, sib, chips, chip_idx = _place()
    plan = []
    for src, land in zip(srcs, lands):
        for j, chip in enumerate(chips):
            plan.append((src.at[chip_idx[j]], land.at[j], (*chip, c), land.at[j]))
    return plan


def _swap_plan(srcs, lands):
    x, y, c, own, sib, chips, chip_idx = _place()
    plan = []
    for src, land in zip(srcs, lands):
        h = src.shape[2] // 2
        plan.append((src.at[:, :, pl.ds(pl.multiple_of((1 - c) * h, LANES), h)], land, sib, land))
    return plan


def _in_proj_plan(srcs, lands):
    x, y, c, own, sib, chips, chip_idx = _place()
    (w, conv), (w_land, conv_land) = srcs, lands
    hw = w.shape[1] // 2
    half = lambda ref: ref.at[:, pl.ds(pl.multiple_of(c * hw, LANES), hw)]
    plan = []
    for j, chip in enumerate(chips):
        plan.append((half(w), half(w_land.at[own]), (*chip, c), half(w_land.at[chip_idx[j]])))
        plan.append((conv, conv_land.at[own], (*chip, c), conv_land.at[chip_idx[j]]))
    plan.append((w, w_land.at[own], sib, w_land.at[own]))
    plan.append((conv, conv_land.at[own], sib, conv_land.at[own]))
    return plan


def _forward_halves(landed):
    hw = landed.shape[2] // 2

    def body(in_ref, out_ref, send_sems, recv_sems):
        x, y, c, own, sib, chips, chip_idx = _place()
        half = lambda ref, hc: ref.at[:, pl.ds(pl.multiple_of(hc * hw, LANES), hw)]
        sent = [_remote(half(out_ref.at[chip_idx[j]], c), half(out_ref.at[chip_idx[j]], c),
                        send_sems.at[j], recv_sems.at[j], sib) for j in range(3)]
        for cp in sent:
            cp.start()
        for j in range(3):
            other = half(out_ref.at[chip_idx[j]], 1 - c)
            _remote(other, other, send_sems.at[j], recv_sems.at[j], sib).wait_recv()
        for cp in sent:
            cp.wait_send()

    return pl.pallas_call(
        body, name="gather_in_forward", out_shape=jax.ShapeDtypeStruct(landed.shape, landed.dtype),
        in_specs=[HBM_SPEC], out_specs=HBM_SPEC, input_output_aliases={0: 0},
        scratch_shapes=[pltpu.SemaphoreType.DMA((3,)), pltpu.SemaphoreType.DMA((3,))],
    )(landed)


def _split_start(name, plan_fn, srcs, land_shapes, n_copies, after=None):
    n = len(srcs)
    extra = [] if after is None else [after]

    def body(*refs):
        src_refs, land_refs = refs[:n], refs[n:2 * n]
        send_sems, recv_sems = refs[2 * n + len(extra)], refs[2 * n + len(extra) + 1]
        token = refs[-1]
        for k, (src, dst, to, _) in enumerate(plan_fn(src_refs, land_refs)):
            _remote(src, dst, send_sems.at[k], recv_sems.at[k], to).start()
        token[...] = jnp.zeros_like(token)

    lands = [pltpu.with_memory_space_constraint(lax.empty(s.shape, s.dtype), pltpu.HBM) for s in land_shapes]
    srcs = [pltpu.with_memory_space_constraint(s, pltpu.HBM) for s in srcs]
    out_shape = ([pltpu.SemaphoreType.DMA((n_copies,)), pltpu.SemaphoreType.DMA((n_copies,))]
                 + [pltpu.HBM(s.shape, s.dtype) for s in srcs] + [pltpu.HBM(s.shape, s.dtype) for s in land_shapes]
                 + [jax.ShapeDtypeStruct((8, LANES), F32)])
    res = pl.pallas_call(
        body, name=name, out_shape=out_shape,
        in_specs=[HBM_SPEC] * (2 * n) + [ANY_SPEC] * len(extra),
        out_specs=[SEM_SPEC, SEM_SPEC] + [HBM_SPEC] * (2 * n) + [pl.BlockSpec(memory_space=pltpu.VMEM)],
        input_output_aliases={i: 2 + i for i in range(2 * n)},
        compiler_params=pltpu.CompilerParams(has_side_effects=DATAFLOW),
    )(*srcs, *lands, *extra)
    return dict(sems=res[:2], srcs=res[2:2 + n], lands=res[2 + n:2 + 2 * n], token=res[-1], n=n)


def _split_wait(name, plan_fn, started, after):
    n = started["n"]

    def body(*refs):
        src_refs, land_refs = refs[:n], refs[n:2 * n]
        send_sems, recv_sems = refs[2 * n], refs[2 * n + 1]
        for k, (src, _, to, landed) in enumerate(plan_fn(src_refs, land_refs)):
            copy = _remote(src, landed, send_sems.at[k], recv_sems.at[k], to)
            copy.wait_send()
            copy.wait_recv()

    srcs, lands = started["srcs"], started["lands"]
    after = list(after) if isinstance(after, (list, tuple)) else [after]
    res = pl.pallas_call(
        body, name=name,
        out_shape=[pltpu.HBM(s.shape, s.dtype) for s in srcs] + [pltpu.HBM(s.shape, s.dtype) for s in lands],
        in_specs=[HBM_SPEC] * (2 * n) + [SEM_SPEC, SEM_SPEC] + [ANY_SPEC] * len(after),
        out_specs=[HBM_SPEC] * (2 * n),
        input_output_aliases={i: i for i in range(2 * n)},
        compiler_params=pltpu.CompilerParams(has_side_effects=DATAFLOW),
    )(*srcs, *lands, *started["sems"], *after)
    started["srcs_after"] = res[:n]
    return res[n:]


def _add_half(stack, landed, place, name):
    _, rows, h = landed.shape

    def body(place_ref, a_ref, b_ref, o_ref, own_ref):
        part = (a_ref[...].astype(F32) + b_ref[...].astype(F32)).astype(o_ref.dtype)
        o_ref[...] = part

        @pl.when(pl.program_id(0) == place_ref[1])
        def _():
            own_ref[...] = part[0]

    return pl.pallas_call(
        body, name=name,
        out_shape=[jax.ShapeDtypeStruct(landed.shape, BF16), jax.ShapeDtypeStruct((rows, h), BF16)],
        grid_spec=pltpu.PrefetchScalarGridSpec(
            num_scalar_prefetch=1, grid=(N_CHIPS,),
            in_specs=[pl.BlockSpec((1, rows, h), lambda j, p: (j, 0, p[0])),
                      pl.BlockSpec((1, rows, h), lambda j, p: (j, 0, 0))],
            out_specs=[pl.BlockSpec((1, rows, h), lambda j, p: (j, 0, 0)),
                       pl.BlockSpec((rows, h), lambda j, p: (0, 0))]),
        compiler_params=_params(("arbitrary",)),
    )(place, stack, landed)


def _sum_partials(own_part, landed, name, untiled_rows=False):
    _, h, cols = landed.shape
    tc = LANES if untiled_rows else cols

    def body(own_ref, a_ref, o_ref):
        acc = own_ref[...].astype(F32)
        for s in range(3):
            acc = acc + a_ref[s].astype(F32)
        if untiled_rows:
            o_ref[:, 0, :] = acc
        else:
            o_ref[...] = acc

    if untiled_rows:
        out_shape, out_spec = jax.ShapeDtypeStruct((h, 1, cols), F32), pl.BlockSpec((h, 1, tc), lambda i: (0, 0, i))
    else:
        out_shape, out_spec = jax.ShapeDtypeStruct((h, cols), F32), pl.BlockSpec((h, tc), lambda i: (0, i))
    return pl.pallas_call(
        body, name=name, out_shape=out_shape, grid=(cols // tc,),
        in_specs=[pl.BlockSpec((h, tc), lambda i: (0, i)), pl.BlockSpec((3, h, tc), lambda i: (0, 0, i))],
        out_specs=out_spec, compiler_params=_params(("arbitrary",)),
    )(own_part, landed)


def _share_halves(halves, name):
    n = len(halves)

    def body(*refs):
        ins, outs = refs[:n], refs[n:2 * n]
        send_sems, recv_sems = refs[2 * n:]
        x, y, c, own, sib, chips, chip_idx = _place()
        cps = [_remote(ins[i], outs[i], send_sems.at[i], recv_sems.at[i], sib) for i in range(n)]
        for cp in cps:
            cp.start()
        for cp in cps:
            cp.wait()

    return pl.pallas_call(
        body, name=name,
        out_shape=[jax.ShapeDtypeStruct(p.shape, p.dtype) for p in halves],
        in_specs=[HBM_SPEC] * n, out_specs=[HBM_SPEC] * n,
        scratch_shapes=[pltpu.SemaphoreType.DMA((n,)), pltpu.SemaphoreType.DMA((n,))],
    )(*halves)


def _allreduce_small(packed):
    rows = packed.shape[0]
    n_dev = 8

    def body(in_ref, out_ref, gath, send_sems, recv_sems):
        x, y, c = lax.axis_index("x"), lax.axis_index("y"), lax.axis_index("c")
        me = 4 * x + 2 * y + c
        gath[me] = in_ref[...]
        cps = []
        for k in range(1, n_dev):
            fx, fy, fc = (k >> 2) & 1, (k >> 1) & 1, k & 1
            to = (x ^ fx, y ^ fy, c ^ fc)
            cps.append(_remote(in_ref, gath.at[me], send_sems.at[k - 1], recv_sems.at[k - 1], to))
        for cp in cps:
            cp.start()
        for k in range(1, n_dev):
            fx, fy, fc = (k >> 2) & 1, (k >> 1) & 1, k & 1
            src = 4 * (x ^ fx) + 2 * (y ^ fy) + (c ^ fc)
            slot = gath.at[src]
            _remote(slot, slot, send_sems.at[k - 1], recv_sems.at[k - 1], (x, y, c)).wait_recv()
        for cp in cps:
            cp.wait_send()
        acc = gath[0]
        for d in range(1, n_dev):
            acc = acc + gath[d]
        out_ref[...] = acc

    vm = pl.BlockSpec(memory_space=pltpu.VMEM)
    return pl.pallas_call(
        body, name="allreduce_small", out_shape=jax.ShapeDtypeStruct(packed.shape, F32),
        in_specs=[vm], out_specs=vm,
        scratch_shapes=[pltpu.VMEM((n_dev, rows, LANES), F32),
                        pltpu.SemaphoreType.DMA((n_dev - 1,)), pltpu.SemaphoreType.DMA((n_dev - 1,))],
    )(packed)


def _adam(col, w, g, m, v):
    m2 = ADAM_B1 * m + (1.0 - ADAM_B1) * g
    v2 = ADAM_B2 * v + (1.0 - ADAM_B2) * (g * g)
    m_hat = m2 / (1.0 - ADAM_B1 ** ADAM_STEP)
    v_hat = v2 / (1.0 - ADAM_B2 ** ADAM_STEP)
    delta = -ADAM_LR * (m_hat / (jnp.sqrt(v_hat) + ADAM_EPS) + ADAM_WD * w)
    return delta, m2, v2


def _adam_call(w, g, m, v, name):
    rows, cols = w.shape
    tm = rows
    for cand in (256, 352, 176, 128, 64, 48, 16, 8):
        if rows % cand == 0:
            tm = cand
            break
    return _tiles(_adam, name=name, rows=rows, tm=tm,
                  row_ins=[(w, cols, 0), (g, cols, 0), (m, cols, 0), (v, cols, 0)],
                  row_outs=[(cols, F32)] * 3)


def _adam_big(w, g_mine, g_other, m, v, place, name):
    rows, cols = w.shape
    tc = 256
    nt = cols // 2 // tc

    def body(place_ref, w_ref, gm_ref, go_ref, m_ref, v_ref, g_out, d_out, m_out, v_out):
        g = jnp.where(pl.program_id(0) == place_ref[0], gm_ref[...], go_ref[...])
        d, m2, v2 = _adam(None, w_ref[...], g, m_ref[...], v_ref[...])
        g_out[...] = g
        d_out[...] = d
        m_out[...] = m2
        v_out[...] = v2

    full = pl.BlockSpec((rows, tc), lambda hh, i, p: (0, hh * nt + i))
    half = pl.BlockSpec((rows, tc), lambda hh, i, p: (0, i))
    return pl.pallas_call(
        body, name=name, out_shape=[jax.ShapeDtypeStruct(w.shape, F32)] * 4,
        grid_spec=pltpu.PrefetchScalarGridSpec(
            num_scalar_prefetch=1, grid=(2, nt),
            in_specs=[full, half, half, full, full], out_specs=[full] * 4),
        compiler_params=_params(("arbitrary", "arbitrary")),
    )(place, w, g_mine, g_other, m, v)


def _adam_untiled_rows(w, g_mine, g_other, m, v, place, name):
    rows, _, cols = w.shape
    tc = 256
    nt = cols // 2 // tc
    rb = next(r for r in (206, 128, 103, rows) if rows % r == 0)

    def body(place_ref, w_ref, gm_ref, go_ref, m_ref, v_ref, g_out, d_out, m_out, v_out):
        g = jnp.where(pl.program_id(0) == place_ref[0], gm_ref[...], go_ref[...])
        d, m2, v2 = _adam(None, w_ref[...], g, m_ref[...], v_ref[...])
        g_out[...] = g
        d_out[...] = d
        m_out[...] = m2
        v_out[...] = v2

    full = pl.BlockSpec((rb, 1, tc), lambda hh, i, r, p: (r, 0, hh * nt + i))
    half = pl.BlockSpec((rb, 1, tc), lambda hh, i, r, p: (r, 0, i))
    return pl.pallas_call(
        body, name=name, out_shape=[jax.ShapeDtypeStruct(w.shape, F32)] * 4,
        grid_spec=pltpu.PrefetchScalarGridSpec(
            num_scalar_prefetch=1, grid=(2, nt, rows // rb),
            in_specs=[full, half, half, full, full], out_specs=[full] * 4),
        compiler_params=_params(("arbitrary", "arbitrary", "arbitrary")),
    )(place, w, g_mine, g_other, m, v)


def _pack(arrays, zero=None):
    flat = []
    for a in arrays:
        a = a.reshape(-1).astype(F32)
        if zero is not None:
            a = a + zero
        flat.append(jnp.pad(a, (0, (-a.size) % LANES)))
    out = jnp.concatenate(flat)
    out = jnp.pad(out, (0, (-out.size) % (8 * LANES)))
    return out.reshape(-1, LANES)


def _unpack(packed, shapes):
    flat = packed.reshape(-1)
    out, off = [], 0
    for s in shapes:
        size = int(np.prod(s))
        out.append(flat[off:off + size].reshape(s))
        off += size + (-size) % LANES
    return out


def kernel(x, norm1_w, w_in, gdn_conv_w, gdn_A_log, gdn_dt_bias, gdn_out_norm_w, fox_f_bias, fox_q_norm_w, fox_k_norm_w, w_out, norm2_w, w_ffn_gate, w_ffn_up, w_ffn_down, final_norm_w, loss_target, m_norm1_w, m_w_in, m_gdn_conv_w, m_gdn_A_log, m_gdn_dt_bias, m_gdn_out_norm_w, m_fox_f_bias, m_fox_q_norm_w, m_fox_k_norm_w, m_w_out, m_norm2_w, m_w_ffn_gate, m_w_ffn_up, m_w_ffn_down, m_final_norm_w, v_norm1_w, v_w_in, v_gdn_conv_w, v_gdn_A_log, v_gdn_dt_bias, v_gdn_out_norm_w, v_fox_f_bias, v_fox_q_norm_w, v_fox_k_norm_w, v_w_out, v_norm2_w, v_w_ffn_gate, v_w_ffn_up, v_w_ffn_down, v_final_norm_w):
    cx, cy, cc = lax.axis_index("x"), lax.axis_index("y"), lax.axis_index("c")
    own = 2 * cx + cy
    place = jnp.stack([cc, own]).astype(jnp.int32)

    names = ["w_in", "w_out", "w_gate", "w_up", "w_down"]
    is_t = [True, False, True, True, False]
    to_t = lambda a, t: a[0].T if t else a[0]
    from_t = lambda a, t: (a.T if t else a)[None]
    big_w = [to_t(a, t) for a, t in zip([w_in, w_out, w_ffn_gate, w_ffn_up, w_ffn_down], is_t)]
    big_m = [to_t(a, t) for a, t in zip([m_w_in, m_w_out, m_w_ffn_gate, m_w_ffn_up, m_w_ffn_down], is_t)]
    big_v = [to_t(a, t) for a, t in zip([v_w_in, v_w_out, v_w_ffn_gate, v_w_ffn_up, v_w_ffn_down], is_t)]
    shards = [big_w[0].astype(BF16)]
    small_w = [norm1_w, gdn_conv_w, gdn_A_log, gdn_dt_bias, gdn_out_norm_w, fox_f_bias, fox_q_norm_w,
               fox_k_norm_w, norm2_w, final_norm_w]
    small_m = [m_norm1_w, m_gdn_conv_w, m_gdn_A_log, m_gdn_dt_bias, m_gdn_out_norm_w, m_fox_f_bias,
               m_fox_q_norm_w, m_fox_k_norm_w, m_norm2_w, m_final_norm_w]
    small_v = [v_norm1_w, v_gdn_conv_w, v_gdn_A_log, v_gdn_dt_bias, v_gdn_out_norm_w, v_fox_f_bias,
               v_fox_q_norm_w, v_fox_k_norm_w, v_norm2_w, v_final_norm_w]
    first = _split_start("gather_in_start", _in_proj_plan, [shards[0], gdn_conv_w[0]],
                         [jax.ShapeDtypeStruct((N_CHIPS,) + shards[0].shape, BF16),
                          jax.ShapeDtypeStruct((N_CHIPS, CONV_K, 3 * WIDTH // N_CHIPS), F32)],
                         n_copies=8)
    small_packed = [_pack(p, first["token"][0, 0]) for p in (small_w, small_m, small_v)]
    shards += [(w + first["token"][0, 0]).astype(BF16) for w in big_w[1:]]
    rest = {}

    def first_weights(after):
        w_in_g, conv_g = _split_wait("gather_in_wait", _in_proj_plan, first, [after] + small_packed)
        w_in_g = _forward_halves(w_in_g)
        rest.update(_split_start("gather_rest_start", _gather_plan, shards[1:],
                                 [jax.ShapeDtypeStruct((N_CHIPS,) + s.shape, BF16) for s in shards[1:]],
                                 n_copies=4 * len(shards[1:]), after=w_in_g))
        w_cat = _cat_weights(w_in_g.reshape(D_IN, D_MODEL))
        return w_cat + rest["token"][0, 0].astype(BF16), conv_g.transpose(1, 0, 2).reshape(CONV_K, 3 * WIDTH)

    def late_weights(after):
        w_out_g, w_gate_g, w_up_g, w_down_g = _split_wait("gather_rest_wait", _gather_plan, rest, after)
        return w_out_g.reshape(D_MODEL, D_MODEL), w_gate_g, w_up_g, w_down_g

    def start_reduction(stacks, landed, nms, tag):
        added = [_add_half(s, l, place, "rs_add_" + nm) for s, l, nm in zip(stacks, landed, nms)]
        parts = [a[0] for a in added]
        started = _split_start("exchange_" + tag + "_start", _exchange_plan, parts,
                               [jax.ShapeDtypeStruct((3,) + p.shape[1:], p.dtype) for p in parts],
                               n_copies=3 * len(parts))
        return dict(own=[a[1] for a in added], started=started, tag=tag, names=nms)

    def finish_reduction(red, after, updates):
        landed = _split_wait("exchange_" + red["tag"] + "_wait", _exchange_plan, red["started"], after)
        halves = [_sum_partials(o, p, "rs_sum_" + nm, untiled_rows=nm == "w_in")
                  for o, p, nm in zip(red["own"], landed, red["names"])]
        others = _share_halves(halves, "rs_share_" + red["tag"])
        return [upd(gm, go) for upd, gm, go in zip(updates, halves, others)]

    def transport_update(b):
        def upd(gm, go):
            res = _adam_big(big_w[b], gm, go, big_m[b], big_v[b], place, "adam_" + names[b])
            early_done.append(res[1])
            return [from_t(a, is_t[b]) for a in res]
        return upd

    early_done = []

    def w_in_update(gm, go):
        rows3 = lambda a: jnp.transpose(a, (2, 0, 1))
        res = _adam_untiled_rows(rows3(w_in), gm, go, rows3(m_w_in), rows3(v_w_in), place, "adam_w_in")
        return [jnp.transpose(a, (1, 2, 0)) for a in res]

    early = {}

    def early_grads_ready(g_out, g_gate, g_up, g_down):
        stacks = [g_out.reshape(N_CHIPS, D_MODEL // N_CHIPS, D_MODEL), g_gate, g_up, g_down]
        swap = _split_start("swap_early_start", _swap_plan, stacks,
                            [jax.ShapeDtypeStruct(s.shape[:2] + (s.shape[2] // 2,), s.dtype) for s in stacks],
                            n_copies=len(stacks))
        early.update(stacks=stacks, swap=swap)
        return swap["token"][0, 0]

    def early_grads_continue(after):
        landed = _split_wait("swap_early_wait", _swap_plan, early["swap"], after)
        early.update(start_reduction(early["swap"]["srcs_after"], landed, names[1:], "early"))
        return early["started"]["token"][0, 0]

    grad_x, g_cat, _, _, _, _, small = _local_step(
        x[0], loss_target[0], norm1_w + first["token"][0, 0], gdn_A_log[0], gdn_dt_bias[0],
        gdn_out_norm_w[0], fox_f_bias[0], fox_q_norm_w[0], fox_k_norm_w[0], norm2_w, final_norm_w.reshape(1, -1),
        first_weights, late_weights, early_grads_ready, early_grads_continue)

    g_in_stack = _uncat_grad(g_cat).reshape(N_CHIPS, D_IN // N_CHIPS, D_MODEL)
    swap_in = _split_start("swap_w_in_start", _swap_plan, [g_in_stack],
                           [jax.ShapeDtypeStruct((N_CHIPS, D_IN // N_CHIPS, D_MODEL // 2), F32)],
                           n_copies=1)

    order = ["norm1_w", "conv_w", "a_log", "dt_bias", "out_norm_w", "f_bias", "q_norm_w", "k_norm_w",
             "norm2_w", "final_w"]
    red = _allreduce_small(_pack([small[k] for k in order] + [small["loss"]], swap_in["token"][0, 0]))
    red_shapes = [(1, D_MODEL), (CONV_K, 3 * WIDTH), (1, HEADS), (1, HEADS), (1, HEAD_DIM), (1, HEADS),
                  (1, HEAD_DIM), (1, HEAD_DIM), (1, D_MODEL), (D_MODEL,), ()]
    red_list = _unpack(red, red_shapes)
    loss = red_list[-1]
    small_g = dict(zip(order, red_list[:-1]))
    shard_cols = 3 * WIDTH // N_CHIPS
    small_g["conv_w"] = lax.dynamic_slice_in_dim(small_g["conv_w"], own * shard_cols, shard_cols, axis=1)[None]
    small_gl = [small_g[k].reshape(w.shape) for k, w in zip(order, small_w)]
    s_delta, s_m, s_v = _adam_call(small_packed[0], _pack(small_gl), small_packed[1], small_packed[2], "adam_small")
    landed_in = _split_wait("swap_w_in_wait", _swap_plan, swap_in, s_delta)
    late = start_reduction(swap_in["srcs_after"], landed_in, names[:1], "w_in")
    big_upd = finish_reduction(early, late["started"]["token"], [transport_update(b) for b in range(1, 5)])
    big_upd = finish_reduction(late, early_done, [w_in_update]) + big_upd
    shapes = [w.shape for w in small_w]
    s_delta, s_m, s_v = _unpack(s_delta, shapes), _unpack(s_m, shapes), _unpack(s_v, shapes)

    big_pos = {1: 0, 9: 1, 11: 2, 12: 3, 13: 4}
    small_pos = {0: 0, 2: 1, 3: 2, 4: 3, 5: 4, 6: 5, 7: 6, 8: 7, 10: 8, 14: 9}
    grads, deltas, new_m, new_v = [], [], [], []
    for pos in range(15):
        if pos in big_pos:
            b = big_pos[pos]
            g, d, m2, v2 = big_upd[b]
            grads.append(g)
            deltas.append(d)
            new_m.append(m2)
            new_v.append(v2)
        else:
            s = small_pos[pos]
            grads.append(small_gl[s])
            deltas.append(s_delta[s])
            new_m.append(s_m[s])
            new_v.append(s_v[s])
    return (loss, grad_x[None], *grads, *deltas, *new_m, *new_v)
```

```python
import jax
import jax.numpy as jnp
import numpy as np
from jax import lax
from jax.experimental import pallas as pl
from jax.experimental.pallas import tpu as pltpu

F32 = jnp.float32
BF16 = jnp.bfloat16

D_MODEL = 1024
HEADS = 8
HEAD_DIM = 64
PAIRS = HEADS // 2
WIDTH = HEADS * HEAD_DIM
CHUNK = 64
CONV_K = 4
D_FF = 2816
FF_SHARD = D_FF // 4
EPS = 1e-6
SCALE = HEAD_DIM ** -0.5
LANES = 128
N_CHIPS = 4
D_IN = 4120
D_CAT = 4224
COL_SMALL = 4096 // LANES

ADAM_LR = 0.001
ADAM_B1 = 0.9
ADAM_B2 = 0.999
ADAM_EPS = 1e-08
ADAM_WD = 0.01
ADAM_STEP = 10

VMEM_LIMIT = 56 * 1024 * 1024
MESH = pl.DeviceIdType.MESH
HIGHEST = lax.Precision.HIGHEST


def _params(sem):
    return pltpu.CompilerParams(dimension_semantics=sem, vmem_limit_bytes=VMEM_LIMIT)


_CONTRACT = {"nn": ((1,), (0,)), "nt": ((1,), (1,)), "tn": ((0,), (0,))}


def _mm(a, b, *, dims, name, out_dtype=F32, add=None, tm=1024, tn=512, tk=512):
    if dims == "nn":
        (m, k), (k2, n) = a.shape, b.shape
    elif dims == "nt":
        (m, k), (n, k2) = a.shape, b.shape
    else:
        (k, m), (k2, n) = a.shape, b.shape
    assert k == k2, (a.shape, b.shape, dims)
    tm, tn, tk = min(tm, m), min(tn, n), min(tk, k)
    assert m % tm == 0 and n % tn == 0 and k % tk == 0, (m, n, k, tm, tn, tk)
    nk = k // tk
    a_spec = (pl.BlockSpec((tk, tm), lambda i, j, kk: (kk, i)) if dims == "tn"
              else pl.BlockSpec((tm, tk), lambda i, j, kk: (i, kk)))
    b_spec = (pl.BlockSpec((tn, tk), lambda i, j, kk: (j, kk)) if dims == "nt"
              else pl.BlockSpec((tk, tn), lambda i, j, kk: (kk, j)))
    o_spec = pl.BlockSpec((tm, tn), lambda i, j, kk: (i, j))
    contract = (_CONTRACT[dims], ((), ()))
    has_add = add is not None

    def body(*refs):
        a_ref, b_ref = refs[:2]
        add_ref = refs[2] if has_add else None
        o_ref = refs[3] if has_add else refs[2]
        part = lax.dot_general(a_ref[...].astype(BF16), b_ref[...].astype(BF16), contract,
                               preferred_element_type=F32)

        def finish(r):
            if has_add:
                r = r + add_ref[...].astype(F32)
            o_ref[...] = r.astype(out_dtype)

        if nk == 1:
            finish(part)
            return
        acc = refs[-1]
        kk = pl.program_id(2)

        @pl.when(kk == 0)
        def _():
            acc[...] = part

        @pl.when(kk > 0)
        def _():
            acc[...] += part

        @pl.when(kk == nk - 1)
        def _():
            finish(acc[...])

    ins = [a, b] + ([add] if has_add else [])
    in_specs = [a_spec, b_spec] + ([o_spec] if has_add else [])
    return pl.pallas_call(
        body, name=name, grid=(m // tm, n // tn, nk),
        in_specs=in_specs, out_specs=o_spec,
        out_shape=jax.ShapeDtypeStruct((m, n), out_dtype),
        scratch_shapes=[pltpu.VMEM((tm, tn), F32)] if nk > 1 else [],
        compiler_params=_params(("parallel", "parallel", "arbitrary")),
    )(*ins)


def _mm_blocks(a, b, *, name, grid, a_spec, b_spec, o_spec, out_shape, dims, n_sum=0, add=None, add_spec=None,
               epilogue=None, extra=(), n_acc=0):
    contract = (_CONTRACT[dims], ((), ()))
    has_add = add is not None
    n_in = 2 + has_add + len(extra)

    def body(*refs):
        a_ref, b_ref = refs[:2]
        dot = lambda x, y: lax.dot_general(x.astype(BF16), y.astype(BF16), contract, preferred_element_type=F32)
        if n_sum:
            r = dot(a_ref[0], b_ref[0])
            for s in range(1, n_sum):
                r = r + dot(a_ref[s], b_ref[s])
        else:
            r = dot(a_ref[...], b_ref[...])
        if has_add:
            r = r + refs[2][...].astype(F32)
        if epilogue is None:
            refs[-1][...] = r.astype(refs[-1].dtype)
        else:
            outs = epilogue(r, *[e[...] for e in refs[2 + has_add:n_in]])
            out_refs = refs[n_in:]
            n_plain = len(out_refs) - n_acc
            for o_ref, val in zip(out_refs[:n_plain], outs):
                o_ref[...] = val.astype(o_ref.dtype)
            if n_acc:
                @pl.when(pl.program_id(0) == 0)
                def _():
                    for o_ref in out_refs[n_plain:]:
                        o_ref[...] = jnp.zeros_like(o_ref)
                for o_ref, val in zip(out_refs[n_plain:], outs[n_plain:]):
                    o_ref[...] += val

    ins = [a, b] + ([add] if has_add else []) + [e[0] for e in extra]
    in_specs = [a_spec, b_spec] + ([add_spec] if has_add else []) + [e[1] for e in extra]
    sem = ("arbitrary" if n_acc else "parallel",) * len(grid)
    return pl.pallas_call(
        body, name=name, grid=grid, in_specs=in_specs, out_specs=o_spec, out_shape=out_shape,
        compiler_params=_params(sem),
    )(*ins)


def _tiles(fn, *, name, rows, tm, ncol=1, row_ins=(), col_consts=(), full_consts=(),
           row_outs=(), acc_outs=()):
    nt = rows // tm
    assert rows % tm == 0
    n_full, n_col, n_row = len(full_consts), len(col_consts), len(row_ins)
    n_ro, n_acc = len(row_outs), len(acc_outs)

    def body(*refs):
        ins = refs[:n_full + n_col + n_row]
        outs = refs[n_full + n_col + n_row:]
        i = pl.program_id(1)
        res = fn(pl.program_id(0), *[r[...] for r in ins])
        for r, v in zip(outs[:n_ro], res[:n_ro]):
            r[...] = v.astype(r.dtype)
        if n_acc:
            @pl.when(i == 0)
            def _():
                for r in outs[n_ro:]:
                    r[...] = jnp.zeros_like(r)
            for r, v in zip(outs[n_ro:], res[n_ro:]):
                r[...] += v

    in_specs = [pl.BlockSpec(a.shape, lambda j, i, nd=a.ndim: (0,) * nd) for a in full_consts]
    in_specs += [pl.BlockSpec((nr, w), lambda j, i, o=o: (0, o + j)) for (_, nr, w, o) in col_consts]
    in_specs += [pl.BlockSpec((tm, w), lambda j, i, o=o: (i, o + j)) for (_, w, o) in row_ins]
    out_specs = [pl.BlockSpec((tm, w), lambda j, i: (i, j)) for (w, _) in row_outs]
    out_specs += [pl.BlockSpec((nr, w), lambda j, i: (0, j)) for (nr, w) in acc_outs]
    out_shape = [jax.ShapeDtypeStruct((rows, w * ncol), dt) for (w, dt) in row_outs]
    out_shape += [jax.ShapeDtypeStruct((nr, w * ncol), F32) for (nr, w) in acc_outs]
    args = list(full_consts) + [c[0] for c in col_consts] + [r[0] for r in row_ins]
    out = pl.pallas_call(
        body, name=name, grid=(ncol, nt), in_specs=in_specs, out_specs=out_specs, out_shape=out_shape,
        compiler_params=_params(("parallel", "arbitrary")),
    )(*args)
    return out


def _rms(x, w):
    return x * lax.rsqrt(jnp.mean(x * x, axis=-1, keepdims=True) + EPS) * w


def _lane_lo(shape):
    return lax.broadcasted_iota(jnp.int32, shape, len(shape) - 1) < HEAD_DIM


def _pair_sum(x):
    lo = _lane_lo(x.shape)
    s0 = jnp.sum(jnp.where(lo, x, 0.0), axis=-1, keepdims=True)
    s1 = jnp.sum(jnp.where(lo, 0.0, x), axis=-1, keepdims=True)
    return jnp.where(lo, s0, s1)


def _head_col(x, lo, h):
    keep = lo if h == 0 else jnp.logical_not(lo)
    return jnp.max(jnp.where(keep, x, -jnp.inf), axis=-1, keepdims=True)


def _softplus(x):
    return jnp.maximum(x, 0.0) + jnp.log1p(jnp.exp(-jnp.abs(x)))


def _silu(x):
    return x * jax.nn.sigmoid(x)


def _dot(a, b, contract):
    return lax.dot_general(a.astype(BF16), b.astype(BF16), (contract, ((), ())),
                           preferred_element_type=F32)


def _dot32(a, b, contract):
    return lax.dot_general(a, b, (contract, ((), ())), precision=HIGHEST, preferred_element_type=F32)


def _bd(y):
    yy = jnp.concatenate([y, y], axis=0)
    r = lax.broadcasted_iota(jnp.int32, yy.shape, 0) < HEAD_DIM
    c = lax.broadcasted_iota(jnp.int32, yy.shape, 1) < HEAD_DIM
    return jnp.where(r == c, yy, 0.0)


def _pp(x, y):
    return _dot(x, _bd(y), _CONTRACT["nn"])


def _pp_nt(x, y):
    return _dot(x, _bd(y), _CONTRACT["nt"])


def _pp_tn(x, y):
    full = _dot(x, y, _CONTRACT["tn"])
    return jnp.where(_lane_lo((HEAD_DIM, LANES)), full[:HEAD_DIM], full[HEAD_DIM:])


def _gdn_masks():
    row = lax.broadcasted_iota(jnp.int32, (CHUNK, LANES), 0)
    col = lax.broadcasted_iota(jnp.int32, (CHUNK, LANES), 1) % HEAD_DIM
    return row, col


def _interleave(chains):
    live = list(chains)
    while live:
        for g in list(live):
            try:
                next(g)
            except StopIteration:
                live.remove(g)


def _gdn_forward(qkv, betax, gcx, grow, rows):
    nchunk = rows // CHUNK

    def body(q_ref, k_ref, v_ref, bx_ref, gx_ref, gr_ref, o_ref, ss_ref, ts_ref, state):
        n = pl.program_id(0)

        @pl.when(n == 0)
        def _():
            state[...] = jnp.zeros_like(state)

        row, col = _gdn_masks()
        incl, strict = col <= row, col < row

        def chain(p):
            lanes = pl.ds(p * LANES, LANES)
            q, k, v, bx, gx = q_ref[:, lanes], k_ref[:, lanes], v_ref[:, lanes], bx_ref[:, lanes], gx_ref[:, lanes]
            gr = gr_ref[0, p]
            glast = gx_ref[pl.ds(CHUNK - 1, 1), lanes]
            s = state[p]
            dm = jnp.where(incl, jnp.exp(jnp.minimum(gx - gr, 0.0)), 0.0)
            kb, vb, eg, qs = k * bx, v * bx, jnp.exp(gx), q * SCALE
            yield
            big_g, big_p = _pp_nt(kb, k), _pp_nt(qs, k)
            yield
            x = -jnp.where(strict, big_g * dm, 0.0)
            att = jnp.where(incl, big_p * dm, 0.0)
            tm = jnp.where(row == col, 1.0, 0.0) + x
            x = _pp(x, x)
            yield
            for _ in range(4):
                step, x = _pp(tm, x), _pp(x, x)
                yield
                tm = tm + step
            tm = tm + _pp(tm, x)
            yield
            u, w = _pp(tm, vb), _pp(tm, kb * eg)
            yield
            ws, qgs = _pp(w, s), _pp(qs * eg, s)
            yield
            vn = u - ws
            kd = k * jnp.exp(glast - gx)
            avn, upd = _pp(att, vn), _pp_tn(kd, vn)
            yield
            ss_ref[0, p] = s
            ts_ref[0, p] = tm
            o_ref[:, lanes] = qgs + avn
            state[p] = s * jnp.exp(glast) + upd

        _interleave([chain(p) for p in range(PAIRS)])

    blk = lambda j: pl.BlockSpec((CHUNK, WIDTH), lambda n, j=j: (n, j))
    sv = pl.BlockSpec((1, PAIRS, CHUNK, LANES), lambda n: (n, 0, 0, 0))
    return pl.pallas_call(
        body, name="gdn_fwd", grid=(nchunk,),
        in_specs=[blk(0), blk(1), blk(2), blk(0), blk(0),
                  pl.BlockSpec((1, PAIRS, 1, LANES), lambda n: (n, 0, 0, 0))],
        out_specs=[blk(0), sv, sv],
        out_shape=[jax.ShapeDtypeStruct((rows, WIDTH), F32),
                   jax.ShapeDtypeStruct((nchunk, PAIRS, CHUNK, LANES), F32),
                   jax.ShapeDtypeStruct((nchunk, PAIRS, CHUNK, LANES), F32)],
        scratch_shapes=[pltpu.VMEM((PAIRS, CHUNK, LANES), F32)],
        compiler_params=_params(("arbitrary",)),
    )(qkv, qkv, qkv, betax, gcx, grow)


def _gdn_backward(qkv, betax, gcx, grow, ssave, tsave, do, rows):
    nchunk = rows // CHUNK

    def body(q_ref, k_ref, v_ref, bx_ref, gx_ref, gr_ref, ss_ref, ts_ref, do_ref,
             dq_ref, dk_ref, dv_ref, dbx_ref, dgx_ref, dgr_ref, dstate):
        n = pl.program_id(0)

        @pl.when(n == 0)
        def _():
            dstate[...] = jnp.zeros_like(dstate)

        row, col = _gdn_masks()
        incl, strict = col <= row, col < row

        def chain(p):
            lanes = pl.ds(p * LANES, LANES)
            q, k, v, bx, gx = q_ref[:, lanes], k_ref[:, lanes], v_ref[:, lanes], bx_ref[:, lanes], gx_ref[:, lanes]
            gr = gr_ref[0, p]
            glast = gx_ref[pl.ds(CHUNK - 1, 1), lanes]
            s, tm, d_o = ss_ref[0, p], ts_ref[0, p], do_ref[:, lanes]
            ds_out = dstate[p]
            dm = jnp.where(incl, jnp.exp(jnp.minimum(gx - gr, 0.0)), 0.0)
            kb, vb, eg, qs = k * bx, v * bx, jnp.exp(gx), q * SCALE
            kbg, qg = kb * eg, qs * eg
            ed = jnp.exp(glast - gx)
            kd = k * ed
            eglast = jnp.exp(glast)
            yield
            big_g, big_p = _pp_nt(kb, k), _pp_nt(qs, k)
            u, w = _pp(tm, vb), _pp(tm, kbg)
            dqg, kds = _pp_nt(d_o, s), _pp(kd, ds_out)
            yield
            low = jnp.where(strict, big_g * dm, 0.0)
            att = jnp.where(incl, big_p * dm, 0.0)
            ws, atd = _pp(w, s), _pp_tn(att, d_o)
            yield
            vn = u - ws
            dvn = kds + atd
            dkd, datt_raw = _pp_nt(vn, ds_out), _pp_nt(d_o, vn)
            dw_neg, dvb = _pp_nt(dvn, s), _pp_tn(tm, dvn)
            dtm_a, wdv = _pp_nt(dvn, vb), _pp_tn(w, dvn)
            qgd = _pp_tn(qg, d_o)
            yield
            datt = jnp.where(incl, datt_raw, 0.0)
            dw = -dw_neg
            dtm_b, dkbg = _pp_nt(dw, kbg), _pp_tn(tm, dw)
            dbig_p = datt * dm
            dqs_a, dk_p = _pp(dbig_p, k), _pp_tn(dbig_p, qs)
            yield
            inner = _pp_tn(tm, dtm_a + dtm_b)
            yield
            dlow = jnp.where(strict, -_pp_nt(inner, tm), 0.0)
            yield
            dbig_g = dlow * dm
            dkb_a, dk_g = _pp(dbig_g, k), _pp_tn(dbig_g, kb)
            yield
            dkb = dkb_a + dkbg * eg
            dqs = dqs_a + dqg * eg
            dk = dk_g + dk_p + dkd * ed + dkb * bx
            z = dlow * low + datt * att
            kdterm = dkd * kd
            dglast = (jnp.sum(ds_out * s, axis=0, keepdims=True) * eglast
                      + jnp.sum(kdterm, axis=0, keepdims=True))
            dgx = dqg * qg + dkbg * kbg - kdterm
            dgx = dgx + jnp.where(col == 0, _pair_sum(z), 0.0)
            dgx = dgx + jnp.where(row == CHUNK - 1, dglast, 0.0)
            dq_ref[:, lanes] = dqs * SCALE
            dk_ref[:, lanes] = dk
            dv_ref[:, lanes] = dvb * bx
            dbx_ref[:, lanes] = dkb * k + dvb * v
            dgx_ref[:, lanes] = dgx
            dgr_ref[0, p] = -jnp.sum(z, axis=0, keepdims=True)
            dstate[p] = ds_out * eglast + qgd - wdv

        _interleave([chain(p) for p in range(PAIRS)])

    last = nchunk - 1
    blk = lambda j: pl.BlockSpec((CHUNK, WIDTH), lambda n, j=j: (last - n, j))
    sv = pl.BlockSpec((1, PAIRS, CHUNK, LANES), lambda n: (last - n, 0, 0, 0))
    gr_spec = pl.BlockSpec((1, PAIRS, 1, LANES), lambda n: (last - n, 0, 0, 0))
    wide = jax.ShapeDtypeStruct((rows, WIDTH), F32)
    return pl.pallas_call(
        body, name="gdn_bwd", grid=(nchunk,),
        in_specs=[blk(0), blk(1), blk(2), blk(0), blk(0), gr_spec, sv, sv, blk(0)],
        out_specs=[blk(0)] * 5 + [gr_spec],
        out_shape=[wide] * 5 + [jax.ShapeDtypeStruct((nchunk, PAIRS, 1, LANES), F32)],
        scratch_shapes=[pltpu.VMEM((PAIRS, CHUNK, LANES), F32)],
        compiler_params=_params(("arbitrary",)),
    )(qkv, qkv, qkv, betax, gcx, grow, ssave, tsave, do)


ATT_TQ = 256


def _att_scores(qh, kt, fk, diag):
    s = _dot(qh, kt, _CONTRACT["nt"]) - fk
    if diag:
        r = lax.broadcasted_iota(jnp.int32, s.shape, 0)
        c = lax.broadcasted_iota(jnp.int32, s.shape, 1)
        s = jnp.where(r >= c, s, -jnp.inf)
    return s


def _head_masks(n):
    lo = _lane_lo((n, LANES))
    return [lo, jnp.logical_not(lo)]


def _attention_forward(fqk, proj, frow, rows):
    tq = tk = min(ATT_TQ, rows)
    nq = rows // tq
    v_off = 3072 // LANES

    def body(q_ref, k_ref, v_ref, fr_ref, o_ref, lse_ref):
        qi = pl.program_id(1)
        q = q_ref[...] * SCALE
        keep_q, keep_k = _head_masks(tq), _head_masks(tk)
        qh = [jnp.where(keep_q[h], q, 0.0).astype(BF16) for h in range(2)]

        def tile(ki, carry, diag):
            k0 = pl.multiple_of(ki * tk, tk)
            kt = k_ref[pl.ds(k0, tk), :].astype(BF16)
            v_t = v_ref[pl.ds(k0, tk), :]
            out = [None, None]

            def chain(h):
                m, l, acc = carry[h]
                vt = jnp.where(keep_k[h], v_t, 0.0).astype(BF16)
                yield
                s = _att_scores(qh[h], kt, fr_ref[0, pl.ds(h, 1), pl.ds(k0, tk)], diag)
                yield
                m_new = jnp.maximum(m, jnp.max(s, axis=-1, keepdims=True))
                p = jnp.exp(s - m_new)
                alpha = jnp.exp(m - m_new)
                l = alpha * l + jnp.sum(p, axis=-1, keepdims=True)
                p_hi = p.astype(BF16)
                p_lo = p - p_hi.astype(F32)
                yield
                out[h] = (m_new, l, alpha * acc + _dot(p_hi, vt, _CONTRACT["nn"]) + _dot(p_lo, vt, _CONTRACT["nn"]))

            _interleave([chain(0), chain(1)])
            return tuple(out)

        one = (jnp.full((tq, 1), -jnp.inf, F32), jnp.zeros((tq, 1), F32), jnp.zeros((tq, LANES), F32))
        carry = lax.fori_loop(0, qi, lambda ki, c: tile(ki, c, False), (one, one))
        (m0, l0, acc0), (m1, l1, acc1) = tile(qi, carry, True)
        o_ref[...] = acc0 / l0 + acc1 / l1
        lse_ref[...] = jnp.where(keep_q[0], m0 + jnp.log(l0), m1 + jnp.log(l1))

    whole = lambda off: pl.BlockSpec((rows, LANES), lambda p, i, off=off: (0, off + p))
    qblk = lambda off: pl.BlockSpec((tq, LANES), lambda p, i, off=off: (i, off + p))
    wide = jax.ShapeDtypeStruct((rows, WIDTH), F32)
    return pl.pallas_call(
        body, name="fox_fwd", grid=(PAIRS, nq),
        in_specs=[qblk(0), whole(PAIRS), whole(v_off), pl.BlockSpec((1, 2, rows), lambda p, i: (p, 0, 0))],
        out_specs=[qblk(0), qblk(0)], out_shape=[wide, wide],
        compiler_params=_params(("parallel", "arbitrary")),
    )(fqk, fqk, proj, frow)


def _attention_backward(fqk, proj, frow, ao, lse, dao, rows):
    tq = tk = min(ATT_TQ, rows)
    nq = rows // tq
    v_off = 3072 // LANES

    def body(q_ref, k_ref, v_ref, fr_ref, o_ref, lse_ref, do_ref, dq_ref, dk_ref, dv_ref, dfr_ref):
        ki = pl.program_id(1)

        @pl.when(ki == 0)
        def _():
            dq_ref[...] = jnp.zeros_like(dq_ref)

        keep_q, keep_k = _head_masks(tq), _head_masks(tk)
        k_t = k_ref[...]
        kt = k_t.astype(BF16)
        vt = v_ref[...].astype(BF16)
        kh = [jnp.where(keep_k[h], k_t, 0.0).astype(BF16) for h in range(2)]
        fk = [fr_ref[0, pl.ds(h, 1), :] for h in range(2)]

        def tile(qi, carry, diag):
            dk, dv, df0, df1 = carry
            rows_q = pl.ds(pl.multiple_of(qi * tq, tq), tq)
            q, d_o, lse_t = q_ref[rows_q, :] * SCALE, do_ref[rows_q, :], lse_ref[rows_q, :]
            delta_x = _pair_sum(d_o.astype(BF16).astype(F32) * o_ref[rows_q, :])
            res = [None, None]

            def chain(h):
                qh = jnp.where(keep_q[h], q, 0.0).astype(BF16)
                doh = jnp.where(keep_q[h], d_o, 0.0).astype(BF16)
                lse_h, delta_h = _head_col(lse_t, keep_q[0], h), _head_col(delta_x, keep_q[0], h)
                yield
                s, dp = _att_scores(qh, kt, fk[h], diag), _dot(doh, vt, _CONTRACT["nt"])
                yield
                p = jnp.exp(s - lse_h)
                ds = p * (dp - delta_h)
                yield
                res[h] = (_dot(p, doh, _CONTRACT["tn"]), _dot(ds, qh, _CONTRACT["tn"]),
                          _dot(ds, kh[h], _CONTRACT["nn"]), jnp.sum(ds, axis=0, keepdims=True))

            _interleave([chain(0), chain(1)])
            (dv0, dk0, dq0, s0), (dv1, dk1, dq1, s1) = res
            dq_ref[rows_q, :] += (dq0 + dq1) * SCALE
            return dk + dk0 + dk1, dv + dv0 + dv1, df0 - s0, df1 - s1

        zero_kv = jnp.zeros((tk, LANES), F32)
        zero_f = jnp.zeros((1, tk), F32)
        carry = tile(ki, (zero_kv, zero_kv, zero_f, zero_f), True)
        dk, dv, df0, df1 = lax.fori_loop(ki + 1, nq, lambda qi, c: tile(qi, c, False), carry)
        dk_ref[...] = dk
        dv_ref[...] = dv.astype(dv_ref.dtype)
        dfr_ref[0, pl.ds(0, 1), :] = df0
        dfr_ref[0, pl.ds(1, 1), :] = df1

    whole = lambda off: pl.BlockSpec((rows, LANES), lambda p, i, off=off: (0, off + p))
    kblk = lambda off: pl.BlockSpec((tk, LANES), lambda p, i, off=off: (i, off + p))
    fr_spec = pl.BlockSpec((1, 2, tk), lambda p, i: (p, 0, i))
    wide = jax.ShapeDtypeStruct((rows, WIDTH), F32)
    return pl.pallas_call(
        body, name="fox_bwd", grid=(PAIRS, nq),
        in_specs=[whole(0), kblk(PAIRS), kblk(v_off), fr_spec, whole(0), whole(0), whole(0)],
        out_specs=[whole(0), kblk(0), kblk(0), fr_spec],
        out_shape=[wide, wide, jax.ShapeDtypeStruct((rows, WIDTH), BF16),
                   jax.ShapeDtypeStruct((PAIRS, 2, rows), F32)],
        compiler_params=_params(("parallel", "arbitrary")),
    )(fqk, fqk, proj, frow, ao, lse, dao)


def _lane_ids(shape):
    return lax.broadcasted_iota(jnp.int32, shape, len(shape) - 1)


def _gates_elem(a_log, dt_bias, f_bias, pre):
    lane = _lane_ids(pre.shape)
    beta = jax.nn.sigmoid(pre)
    g = -jnp.exp(a_log) * _softplus(pre + dt_bias)
    lf = -_softplus(-(pre + f_bias))
    return jnp.where(lane < 8, beta, jnp.where(lane < 16, g, jnp.where(lane < 24, lf, 0.0)))


def _tri_consts():
    r = np.arange(LANES)[:, None]
    c = np.arange(LANES)[None, :]
    full = (c <= r).astype(np.float32)
    chunked = full * ((r // CHUNK) == (c // CHUNK))
    return jnp.asarray(chunked), jnp.asarray(full)


def _cums_fwd(lc, lf, gates):
    rows = gates.shape[0]
    lane = _lane_ids((LANES, LANES))
    carry = jnp.zeros((1, LANES), F32)
    out = []
    for r in range(rows // LANES):
        blk = gates[r * LANES:(r + 1) * LANES]
        gc = _dot32(lc, blk, _CONTRACT["nn"])
        f = _dot32(lf, blk, _CONTRACT["nn"]) + carry
        carry = carry + jnp.sum(blk, axis=0, keepdims=True)
        out.append(jnp.where((lane >= 8) & (lane < 16), gc, jnp.where((lane >= 16) & (lane < 24), f, 0.0)))
    return jnp.concatenate(out, axis=0)


def _cums_bwd(lc, lf, dcums):
    rows = dcums.shape[0]
    lane = _lane_ids((LANES, LANES))
    is_g = (lane >= 8) & (lane < 16)
    is_f = (lane >= 16) & (lane < 24)
    carry = jnp.zeros((1, LANES), F32)
    out = [None] * (rows // LANES)
    for r in reversed(range(rows // LANES)):
        blk = dcums[r * LANES:(r + 1) * LANES]
        dg = jnp.where(is_g, blk, 0.0)
        df = jnp.where(is_f, blk, 0.0)
        out[r] = _dot32(lc, dg, _CONTRACT["tn"]) + _dot32(lf, df, _CONTRACT["tn"]) + carry
        carry = carry + jnp.sum(df, axis=0, keepdims=True)
    return jnp.concatenate(out, axis=0)


def _expand_consts():
    xb = np.zeros((LANES, WIDTH), np.float32)
    xg = np.zeros((LANES, WIDTH), np.float32)
    for h in range(HEADS):
        xb[h, h * HEAD_DIM:(h + 1) * HEAD_DIM] = 1.0
        xg[8 + h, h * HEAD_DIM:(h + 1) * HEAD_DIM] = 1.0
    return jnp.asarray(xb), jnp.asarray(xg)


def _shift_down(x, s):
    if s == 0:
        return x
    row = lax.broadcasted_iota(jnp.int32, x.shape, 0)
    return jnp.where(row >= s, pltpu.roll(x, s, 0), 0.0)


def _shift_up(x, s):
    if s == 0:
        return x
    n = x.shape[0]
    row = lax.broadcasted_iota(jnp.int32, x.shape, 0)
    return jnp.where(row < n - s, pltpu.roll(x, n - s, 0), 0.0)


def _row_of(cw, i):
    row = lax.broadcasted_iota(jnp.int32, cw.shape, 0)
    return jnp.sum(jnp.where(row == i, cw, 0.0), axis=0, keepdims=True)


def _conv(cw, x):
    c = jnp.zeros_like(x)
    for i in range(CONV_K):
        c = c + _row_of(cw, i) * _shift_down(x, CONV_K - 1 - i)
    return c


def _post_conv(is_qk, c):
    s = _silu(c)
    n = s * lax.rsqrt(_pair_sum(s * s) + EPS)
    return jnp.where(is_qk, n, s)


def _gdn_prep_fwd(col, cw, x):
    return (_post_conv(col < 2 * PAIRS, _conv(cw, x)),)


def _gdn_prep_bwd(is_qk, cw, x, dy):
    c = _conv(cw, x)
    _, vjp = jax.vjp(lambda cc: _post_conv(is_qk, cc), c)
    (dc,) = vjp(dy)
    dx = jnp.zeros_like(x)
    row = lax.broadcasted_iota(jnp.int32, cw.shape, 0)
    dcw = jnp.zeros(cw.shape, F32)
    for i in range(CONV_K):
        s = CONV_K - 1 - i
        dx = dx + _row_of(cw, i) * _shift_up(dc, s)
        dcw = dcw + jnp.where(row == i, jnp.sum(dc * _shift_down(x, s), axis=0, keepdims=True), 0.0)
    return dx, dcw


def _head_rms(w, x):
    return x * lax.rsqrt(_pair_sum(x * x) / HEAD_DIM + EPS) * w


def _cat_weights(w_in_t):
    tail = jnp.pad(w_in_t[4112:4120], ((0, D_CAT - D_IN), (0, 0)))
    return jnp.concatenate([w_in_t[:2048], w_in_t[2064:4112], w_in_t[2048:2064], tail], axis=0)


def _uncat_grad(g):
    return jnp.concatenate([g[:2048], g[4096:4112], g[2048:4096], g[4112:4120]], axis=0)


def _lanes_to_rowform(v8, rows):
    return v8.reshape(rows // CHUNK, CHUNK, HEADS).transpose(0, 2, 1).reshape(rows // CHUNK, PAIRS, 1, LANES)


def _rowform_to_lanes(v, rows):
    return v.reshape(rows // CHUNK, HEADS, CHUNK).transpose(0, 2, 1).reshape(rows, HEADS)


def _local_step(x, target, norm1_w, a_log, dt_bias, out_norm_w, f_bias, q_norm_w, k_norm_w,
                norm2_w, final_w, first_weights, late_weights, early_grads_ready, early_grads_continue):
    rows = x.shape[0]
    tm = min(512, rows)
    lc, lf = _tri_consts()
    xb, xg = _expand_consts()

    (h1,) = _tiles(lambda col, w, xx: (_rms(xx, w),), name="norm1", rows=rows, tm=tm,
                   full_consts=[norm1_w], row_ins=[(x, D_MODEL, 0)], row_outs=[(D_MODEL, BF16)])
    w_cat, conv_w = first_weights(h1)
    proj = _mm(h1, w_cat, dims="nt", name="in_proj", tn=1408, tk=1024)

    lane_pad = lambda v, off: jnp.pad(v.reshape(1, -1), ((0, 0), (off, LANES - off - v.size)))
    p_a, p_dt, p_fb = lane_pad(a_log, 8), lane_pad(dt_bias, 8), lane_pad(f_bias, 16)

    def gates_fwd(col, lcv, lfv, a, dt, fb, pre):
        gates = _gates_elem(a, dt, fb, pre)
        return gates, _cums_fwd(lcv, lfv, gates)

    gates, cums = _tiles(gates_fwd, name="gates", rows=rows, tm=rows,
                         full_consts=[lc, lf, p_a, p_dt, p_fb], row_ins=[(proj, LANES, COL_SMALL)],
                         row_outs=[(LANES, F32), (LANES, F32)])

    def expand_fwd(col, b, g, gt, cm):
        return (_dot32(gt, b, _CONTRACT["nn"]), _dot32(cm, g, _CONTRACT["nn"]))

    betax, gcx = _tiles(expand_fwd, name="expand", rows=rows, tm=tm, full_consts=[xb, xg],
                        row_ins=[(gates, LANES, 0), (cums, LANES, 0)],
                        row_outs=[(WIDTH, F32)] * 2)
    grow = _lanes_to_rowform(cums[:, 8:16], rows)
    frow = cums[:, 16:24].T.reshape(PAIRS, 2, rows)

    (qkv,) = _tiles(_gdn_prep_fwd, name="gdn_prep", rows=rows, tm=rows, ncol=3 * PAIRS,
                    col_consts=[(conv_w, CONV_K, LANES, 0)], row_ins=[(proj, LANES, 0)],
                    row_outs=[(LANES, F32)])
    o_gdn, ssave, tsave = _gdn_forward(qkv, betax, gcx, grow, rows)

    w_qk = jnp.concatenate([jnp.tile(q_norm_w.reshape(1, -1), (1, HEADS)),
                            jnp.tile(k_norm_w.reshape(1, -1), (1, HEADS))], axis=1)
    fox_off = 2048 // LANES
    (fqk,) = _tiles(lambda col, w, xx: (_head_rms(w, xx),), name="fox_prep", rows=rows, tm=rows, ncol=2 * PAIRS,
                    col_consts=[(w_qk, 1, LANES, 0)], row_ins=[(proj, LANES, fox_off)],
                    row_outs=[(LANES, F32)])
    ao, lse = _attention_forward(fqk, proj, frow, rows)

    w_on = jnp.tile(out_norm_w.reshape(1, -1), (1, 2))
    z_off, fg_off = 1536 // LANES, 3584 // LANES
    mix_g_fn = lambda w, o, z: _head_rms(w, o) * _silu(z)
    mix_f_fn = lambda a, g: a * jax.nn.sigmoid(g)
    (mix_g,) = _tiles(lambda col, w, o, z: (mix_g_fn(w, o, z),), name="mix_gdn", rows=rows, tm=rows, ncol=PAIRS,
                      full_consts=[w_on], row_ins=[(o_gdn, LANES, 0), (proj, LANES, z_off)],
                      row_outs=[(LANES, BF16)])
    (mix_f,) = _tiles(lambda col, a, g: (mix_f_fn(a, g),), name="mix_fox", rows=rows, tm=rows, ncol=PAIRS,
                      row_ins=[(ao, LANES, 0), (proj, LANES, fg_off)], row_outs=[(LANES, BF16)])
    mix = jnp.concatenate([mix_g, mix_f], axis=1)
    w_out, w_gate, w_up, w_down = late_weights(mix)
    t_rows, t_half = min(1024, rows), min(512, rows)
    n_rt = rows // t_rows
    row_blk = pl.BlockSpec((t_rows, D_MODEL), lambda i, n: (i, 0))
    half_blk = pl.BlockSpec((t_half, D_MODEL), lambda i, n: (i, 0))
    vec_blk = pl.BlockSpec((1, D_MODEL), lambda i, n: (0, 0))
    wide = lambda dt: jax.ShapeDtypeStruct((rows, D_MODEL), dt)
    x1, h2 = _mm_blocks(mix, w_out, name="out_proj_norm2", grid=(n_rt, 1), dims="nn",
                        a_spec=row_blk, b_spec=pl.BlockSpec((D_MODEL, D_MODEL), lambda i, n: (0, 0)),
                        o_spec=[row_blk, row_blk], out_shape=[wide(F32), wide(BF16)], add=x, add_spec=row_blk,
                        extra=[(norm2_w, vec_blk)], epilogue=lambda r, w: (r, _rms(r, w)))
    st_act = jax.ShapeDtypeStruct((N_CHIPS, rows, FF_SHARD), BF16)
    st_rows = pl.BlockSpec((None, rows, FF_SHARD), lambda i, j: (j, i, 0))

    def ffn_in(w_st, name):
        return _mm_blocks(h2, w_st, name=name, grid=(1, N_CHIPS), dims="nt",
                          a_spec=pl.BlockSpec((rows, D_MODEL), lambda i, j: (i, 0)),
                          b_spec=pl.BlockSpec((None, FF_SHARD, D_MODEL), lambda i, j: (j, 0, 0)),
                          o_spec=st_rows, out_shape=st_act)

    gate = ffn_in(w_gate, "ffn_gate")
    act_fn = lambda g, u: _silu(g) * u
    st_tile = pl.BlockSpec((None, t_rows, FF_SHARD), lambda i, j: (j, i, 0))
    up, act = _mm_blocks(h2, w_up, name="ffn_up_act", grid=(n_rt, N_CHIPS), dims="nt",
                         a_spec=pl.BlockSpec((t_rows, D_MODEL), lambda i, j: (i, 0)),
                         b_spec=pl.BlockSpec((None, FF_SHARD, D_MODEL), lambda i, j: (j, 0, 0)),
                         o_spec=[st_tile, st_tile], out_shape=[st_act, st_act], extra=[(gate, st_tile)],
                         epilogue=lambda u, g: (u, act_fn(g.astype(F32), u)))

    def final_fn(xx, tgt, w):
        y, vjp = jax.vjp(_rms, xx, w)
        err = y - tgt
        loss = 0.5 * jnp.sum(err * err) / D_MODEL
        dx, dw = vjp(err / D_MODEL)
        return dx, dx, jnp.full((1, LANES), loss, F32), dw

    dx2, dx2_b, loss, d_final_w = _mm_blocks(
        act, w_down, name="ffn_down_loss", grid=(rows // t_half, 1), dims="nn", n_sum=N_CHIPS,
        a_spec=pl.BlockSpec((N_CHIPS, t_half, FF_SHARD), lambda i, n: (0, i, 0)),
        b_spec=pl.BlockSpec((N_CHIPS, FF_SHARD, D_MODEL), lambda i, n: (0, 0, 0)),
        o_spec=[half_blk, half_blk, pl.BlockSpec((1, LANES), lambda i, n: (0, 0)), vec_blk],
        out_shape=[wide(F32), wide(BF16), jax.ShapeDtypeStruct((1, LANES), F32),
                   jax.ShapeDtypeStruct((1, D_MODEL), F32)],
        add=x1, add_spec=half_blk, extra=[(target, half_blk), (final_w, vec_blk)], epilogue=final_fn, n_acc=2)

    def act_bwd(d, g, u):
        _, vjp = jax.vjp(act_fn, g.astype(F32), u.astype(F32))
        return vjp(d)

    dgate, dup = _mm_blocks(dx2_b, w_down, name="d_act_gate_up", grid=(n_rt, N_CHIPS), dims="nt",
                            a_spec=pl.BlockSpec((t_rows, D_MODEL), lambda i, j: (i, 0)),
                            b_spec=pl.BlockSpec((None, FF_SHARD, D_MODEL), lambda i, j: (j, 0, 0)),
                            o_spec=[st_tile, st_tile], out_shape=[st_act, st_act],
                            extra=[(gate, st_tile), (up, st_tile)], epilogue=act_bwd)

    def g_ffn(d_st, other, name):
        return _mm_blocks(d_st, other, name=name, grid=(N_CHIPS, 1), dims="tn",
                          a_spec=pl.BlockSpec((None, rows, FF_SHARD), lambda j, n: (j, 0, 0)),
                          b_spec=pl.BlockSpec((rows, D_MODEL), lambda j, n: (0, 0)),
                          o_spec=pl.BlockSpec((None, FF_SHARD, D_MODEL), lambda j, n: (j, 0, 0)),
                          out_shape=jax.ShapeDtypeStruct((N_CHIPS, FF_SHARD, D_MODEL), BF16))

    g_down = g_ffn(act, dx2_b, "g_down")

    def norm_bwd(dh, xx, dres, w):
        _, vjp = jax.vjp(_rms, xx, w)
        dx, dw = vjp(dh)
        return dx + dres, dx + dres, dw

    def d_h2(d_st, w_st, name, add, **fused):
        return _mm_blocks(d_st, w_st, name=name, grid=(rows // t_half, 1), dims="nn", n_sum=N_CHIPS,
                          a_spec=pl.BlockSpec((N_CHIPS, t_half, FF_SHARD), lambda i, n: (0, i, 0)),
                          b_spec=pl.BlockSpec((N_CHIPS, FF_SHARD, D_MODEL), lambda i, n: (0, 0, 0)),
                          add=add, add_spec=half_blk, **fused)

    dh2_gate = d_h2(dgate, w_gate, "d_h2_gate", None, o_spec=half_blk, out_shape=wide(F32))
    dx1, dx1_b, d_norm2_w = d_h2(
        dup, w_up, "d_h2_up_norm2_bwd", dh2_gate, o_spec=[half_blk, half_blk, vec_blk],
        out_shape=[wide(F32), wide(BF16), jax.ShapeDtypeStruct((1, D_MODEL), F32)],
        extra=[(x1, half_blk), (dx2, half_blk), (norm2_w, vec_blk)], epilogue=norm_bwd, n_acc=1)
    g_gate, g_up = g_ffn(dgate, h2, "g_gate"), g_ffn(dup, h2, "g_up")
    dmix = _mm(dx1_b, w_out, dims="nt", name="d_mix", tn=D_MODEL, tk=1024)
    g_out = _mm(mix, dx1_b, dims="tn", name="g_out", tn=D_MODEL, tk=rows, out_dtype=BF16)
    w_on = w_on + early_grads_ready(g_out, g_gate, g_up, g_down)

    def mix_g_bwd(col, w, o, z, d):
        _, vjp = jax.vjp(mix_g_fn, w, o, z)
        dw, do_, dz = vjp(d)
        return do_, dz, dw

    do_gdn, dz, d_on = _tiles(mix_g_bwd, name="mix_gdn_bwd", rows=rows, tm=rows, ncol=PAIRS, full_consts=[w_on],
                              row_ins=[(o_gdn, LANES, 0), (proj, LANES, z_off), (dmix, LANES, 0)],
                              row_outs=[(LANES, F32), (LANES, BF16)], acc_outs=[(1, LANES)])

    def mix_f_bwd(col, a, g, d):
        _, vjp = jax.vjp(mix_f_fn, a, g)
        return vjp(d)

    dao, dfgate = _tiles(mix_f_bwd, name="mix_fox_bwd", rows=rows, tm=rows, ncol=PAIRS,
                         row_ins=[(ao, LANES, 0), (proj, LANES, fg_off), (dmix, LANES, PAIRS)],
                         row_outs=[(LANES, F32), (LANES, BF16)])

    dfq, dfk, dfv, dfrow = _attention_backward(fqk, proj, frow + early_grads_continue(dao), ao, lse, dao, rows)

    def fox_prep_bwd(col, w, xx, d):
        _, vjp = jax.vjp(_head_rms, w, xx)
        dw, dx = vjp(d)
        return dx, dw

    dfqk, d_wqk = [], []
    for part, d_n in enumerate((dfq, dfk)):
        dx_p, dw_p = _tiles(fox_prep_bwd, name="fox_prep_bwd_" + "qk"[part], rows=rows, tm=rows, ncol=PAIRS,
                            col_consts=[(w_qk, 1, LANES, part * PAIRS)],
                            row_ins=[(proj, LANES, fox_off + part * PAIRS), (d_n, LANES, 0)],
                            row_outs=[(LANES, BF16)], acc_outs=[(1, LANES)])
        dfqk.append(dx_p)
        d_wqk.append(dw_p)

    dq, dk, dv, dbetax, dgcx, dgrow = _gdn_backward(qkv, betax, gcx, grow, ssave, tsave, do_gdn, rows)
    dqkv, d_conv = [], []
    for part, d_n in enumerate((dq, dk, dv)):
        prep_bwd = lambda col, cw, xx, dy, is_qk=(part < 2): _gdn_prep_bwd(is_qk, cw, xx, dy)
        dx_p, dw_p = _tiles(prep_bwd, name="gdn_prep_bwd_" + "qkv"[part], rows=rows, tm=rows, ncol=PAIRS,
                            col_consts=[(conv_w, CONV_K, LANES, part * PAIRS)],
                            row_ins=[(proj, LANES, part * PAIRS), (d_n, LANES, 0)],
                            row_outs=[(LANES, BF16)], acc_outs=[(CONV_K, LANES)])
        dqkv.append(dx_p)
        d_conv.append(dw_p)
    d_conv = jnp.concatenate(d_conv, axis=1)

    def expand_bwd(col, b, g, db, dg):
        return (_dot32(db, b, _CONTRACT["nt"]), _dot32(dg, g, _CONTRACT["nt"]))

    dgates_b, dcums_g = _tiles(expand_bwd, name="expand_bwd", rows=rows, tm=tm, full_consts=[xb, xg],
                               row_ins=[(dbetax, WIDTH, 0), (dgcx, WIDTH, 0)],
                               row_outs=[(LANES, F32), (LANES, F32)])
    dcums_row = jnp.concatenate([jnp.zeros((rows, 8), F32), _rowform_to_lanes(dgrow, rows),
                                 dfrow.reshape(HEADS, rows).T, jnp.zeros((rows, LANES - 24), F32)], axis=1)

    def gates_bwd(col, lcv, lfv, a, dt, fb, pre, dgb, dcg, dcr):
        lane = _lane_ids(pre.shape)
        dgates = jnp.where(lane < 8, dgb, _cums_bwd(lcv, lfv, dcg + dcr))
        _, vjp = jax.vjp(_gates_elem, a, dt, fb, pre)
        da, ddt, dfb, dpre = vjp(dgates)
        return dpre, da, ddt, dfb

    dpre, d_a, d_dt, d_fb = _tiles(gates_bwd, name="gates_bwd", rows=rows, tm=rows,
                                   full_consts=[lc, lf, p_a, p_dt, p_fb],
                                   row_ins=[(proj, LANES, COL_SMALL), (dgates_b, LANES, 0), (dcums_g, LANES, 0),
                                            (dcums_row, LANES, 0)],
                                   row_outs=[(LANES, BF16)], acc_outs=[(1, LANES)] * 3)

    dproj = jnp.concatenate(dqkv + [dz] + dfqk + [dfv, dfgate, dpre], axis=1)
    grad_x, d_norm1_w = _mm_blocks(
        dproj, w_cat, name="d_h1_norm1_bwd", grid=(rows // t_half, 1), dims="nn",
        a_spec=pl.BlockSpec((t_half, D_CAT), lambda i, n: (i, 0)),
        b_spec=pl.BlockSpec((D_CAT, D_MODEL), lambda i, n: (0, 0)),
        o_spec=[half_blk, vec_blk], out_shape=[wide(F32), jax.ShapeDtypeStruct((1, D_MODEL), F32)],
        extra=[(x, half_blk), (dx1, half_blk), (norm1_w, vec_blk)],
        epilogue=lambda dh, xx, dres, w: norm_bwd(dh, xx, dres, w)[1:], n_acc=1)
    g_cat = _mm(dproj, h1, dims="tn", name="g_in", tm=1408, tn=D_MODEL, tk=rows)

    fold = lambda v: v.reshape(-1, HEAD_DIM).sum(axis=0)
    small = dict(
        loss=loss[0, 0],
        norm1_w=d_norm1_w, conv_w=d_conv, a_log=d_a[0, 8:16], dt_bias=d_dt[0, 8:16],
        out_norm_w=fold(d_on), f_bias=d_fb[0, 16:24], q_norm_w=fold(d_wqk[0]),
        k_norm_w=fold(d_wqk[1]), norm2_w=d_norm2_w, final_w=d_final_w)
    return grad_x, g_cat, g_out, g_gate, g_up, g_down, small


HBM_SPEC = pl.BlockSpec(memory_space=pltpu.HBM)


def _place():
    x, y, c = lax.axis_index("x"), lax.axis_index("y"), lax.axis_index("c")
    chips = [(1 - x, y), (x, 1 - y), (1 - x, 1 - y)]
    return x, y, c, 2 * x + y, (x, y, 1 - c), chips, [2 * cx + cy for cx, cy in chips]


def _remote(src, dst, send_sem, recv_sem, to):
    return pltpu.make_async_remote_copy(src_ref=src, dst_ref=dst, send_sem=send_sem, recv_sem=recv_sem,
                                        device_id=to, device_id_type=MESH)


SEM_SPEC =pl.BlockSpec(memory_space=pltpu.SEMAPHORE)
ANY_SPEC = pl.BlockSpec(memory_space=pl.ANY)
DATAFLOW = pltpu.SideEffectType.DATAFLOW_SIDE_EFFECTING


def _gather_plan(srcs, lands):
    x, y, c, own, sib, chips, chip_idx = _place()
    plan = []
    for src, land in zip(srcs, lands):
        for j, chip in enumerate(chips):
            plan.append((src, land.at[own], (*chip, c), land.at[chip_idx[j]]))
        plan.append((src, land.at[own], sib, land.at[own]))
    return plan


def _exchange_plan(srcs, lands):
    x, y, c, own, sib, chips, chip_idx = _place()
    plan = []
    for src, land in zip(srcs, lands):
        for j, chip in enumerate(chips):
            plan.append((src.at[chip_idx[j]], land.at[j], (*chip, c), land.at[j]))
    return plan


def _swap_plan(srcs, lands):
    x, y, c, own, sib, chips, chip_idx = _place()
    plan = []
    for src, land in zip(srcs, lands):
        h = src.shape[2] // 2
        plan.append((src.at[:, :, pl.ds(pl.multiple_of((1 - c) * h, LANES), h)], land, sib, land))
    return plan


def _in_proj_plan(srcs, lands):
    x, y, c, own, sib, chips, chip_idx = _place()
    (w, conv), (w_land, conv_land) = srcs, lands
    hw = w.shape[1] // 2
    half = lambda ref: ref.at[:, pl.ds(pl.multiple_of(c * hw, LANES), hw)]
    plan = []
    for j, chip in enumerate(chips):
        plan.append((half(w), half(w_land.at[own]), (*chip, c), half(w_land.at[chip_idx[j]])))
        plan.append((conv, conv_land.at[own], (*chip, c), conv_land.at[chip_idx[j]]))
    plan.append((w, w_land.at[own], sib, w_land.at[own]))
    plan.append((conv, conv_land.at[own], sib, conv_land.at[own]))
    return plan


def _forward_halves(landed):
    hw = landed.shape[2] // 2

    def body(in_ref, out_ref, send_sems, recv_sems):
        x, y, c, own, sib, chips, chip_idx = _place()
        half = lambda ref, hc: ref.at[:, pl.ds(pl.multiple_of(hc * hw, LANES), hw)]
        sent = [_remote(half(out_ref.at[chip_idx[j]], c), half(out_ref.at[chip_idx[j]], c),
                        send_sems.at[j], recv_sems.at[j], sib) for j in range(3)]
        for cp in sent:
            cp.start()
        for j in range(3):
            other = half(out_ref.at[chip_idx[j]], 1 - c)
            _remote(other, other, send_sems.at[j], recv_sems.at[j], sib).wait_recv()
        for cp in sent:
            cp.wait_send()

    return pl.pallas_call(
        body, name="gather_in_forward", out_shape=jax.ShapeDtypeStruct(landed.shape, landed.dtype),
        in_specs=[HBM_SPEC], out_specs=HBM_SPEC, input_output_aliases={0: 0},
        scratch_shapes=[pltpu.SemaphoreType.DMA((3,)), pltpu.SemaphoreType.DMA((3,))],
    )(landed)


def _split_start(name, plan_fn, srcs, land_shapes, n_copies, after=None):
    n = len(srcs)
    extra = [] if after is None else [after]

    def body(*refs):
        src_refs, land_refs = refs[:n], refs[n:2 * n]
        send_sems, recv_sems = refs[2 * n + len(extra)], refs[2 * n + len(extra) + 1]
        token = refs[-1]
        for k, (src, dst, to, _) in enumerate(plan_fn(src_refs, land_refs)):
            _remote(src, dst, send_sems.at[k], recv_sems.at[k], to).start()
        token[...] = jnp.zeros_like(token)

    lands = [pltpu.with_memory_space_constraint(lax.empty(s.shape, s.dtype), pltpu.HBM) for s in land_shapes]
    srcs = [pltpu.with_memory_space_constraint(s, pltpu.HBM) for s in srcs]
    out_shape = ([pltpu.SemaphoreType.DMA((n_copies,)), pltpu.SemaphoreType.DMA((n_copies,))]
                 + [pltpu.HBM(s.shape, s.dtype) for s in srcs] + [pltpu.HBM(s.shape, s.dtype) for s in land_shapes]
                 + [jax.ShapeDtypeStruct((8, LANES), F32)])
    res = pl.pallas_call(
        body, name=name, out_shape=out_shape,
        in_specs=[HBM_SPEC] * (2 * n) + [ANY_SPEC] * len(extra),
        out_specs=[SEM_SPEC, SEM_SPEC] + [HBM_SPEC] * (2 * n) + [pl.BlockSpec(memory_space=pltpu.VMEM)],
        input_output_aliases={i: 2 + i for i in range(2 * n)},
        compiler_params=pltpu.CompilerParams(has_side_effects=DATAFLOW),
    )(*srcs, *lands, *extra)
    return dict(sems=res[:2], srcs=res[2:2 + n], lands=res[2 + n:2 + 2 * n], token=res[-1], n=n)


def _split_wait(name, plan_fn, started, after):
    n = started["n"]

    def body(*refs):
        src_refs, land_refs = refs[:n], refs[n:2 * n]
        send_sems, recv_sems = refs[2 * n], refs[2 * n + 1]
        for k, (src, _, to, landed) in enumerate(plan_fn(src_refs, land_refs)):
            copy = _remote(src, landed, send_sems.at[k], recv_sems.at[k], to)
            copy.wait_send()
            copy.wait_recv()

    srcs, lands = started["srcs"], started["lands"]
    after = list(after) if isinstance(after, (list, tuple)) else [after]
    res = pl.pallas_call(
        body, name=name,
        out_shape=[pltpu.HBM(s.shape, s.dtype) for s in srcs] + [pltpu.HBM(s.shape, s.dtype) for s in lands],
        in_specs=[HBM_SPEC] * (2 * n) + [SEM_SPEC, SEM_SPEC] + [ANY_SPEC] * len(after),
        out_specs=[HBM_SPEC] * (2 * n),
        input_output_aliases={i: i for i in range(2 * n)},
        compiler_params=pltpu.CompilerParams(has_side_effects=DATAFLOW),
    )(*srcs, *lands, *started["sems"], *after)
    started["srcs_after"] = res[:n]
    return res[n:]


def _add_halves(stacks, landed, place, name):
    n = len(stacks)

    def body(place_ref, *refs):
        for a_ref, b_ref, o_ref, own_ref in zip(refs[:n], refs[n:2 * n], refs[2 * n::2], refs[2 * n + 1::2]):
            part = (a_ref[...].astype(F32) + b_ref[...].astype(F32)).astype(o_ref.dtype)
            o_ref[...] = part

            @pl.when(pl.program_id(0) == place_ref[1])
            def _(own_ref=own_ref, part=part):
                own_ref[...] = part[0]

    shapes = [l.shape[1:] for l in landed]
    slab = lambda s: pl.BlockSpec((1,) + s, lambda j, p: (j, 0, 0))
    out_shape, out_specs = [], []
    for l, s in zip(landed, shapes):
        out_shape += [jax.ShapeDtypeStruct(l.shape, BF16), jax.ShapeDtypeStruct(s, BF16)]
        out_specs += [slab(s), pl.BlockSpec(s, lambda j, p: (0, 0))]
    res = pl.pallas_call(
        body, name=name, out_shape=out_shape,
        grid_spec=pltpu.PrefetchScalarGridSpec(
            num_scalar_prefetch=1, grid=(N_CHIPS,),
            in_specs=[pl.BlockSpec((1,) + s, lambda j, p: (j, 0, p[0])) for s in shapes] + [slab(s) for s in shapes],
            out_specs=out_specs),
        compiler_params=_params(("arbitrary",)),
    )(place, *stacks, *landed)
    return [(res[2 * i], res[2 * i + 1]) for i in range(n)]


def _sum_many(own_parts, landed, name):
    n = len(own_parts)

    def body(*refs):
        for own_ref, a_ref, o_ref in zip(refs[:n], refs[n:2 * n], refs[2 * n:]):
            acc = own_ref[...].astype(F32)
            for s in range(3):
                acc = acc + a_ref[s].astype(F32)
            o_ref[...] = acc

    whole = lambda a: pl.BlockSpec(a.shape, lambda i, nd=a.ndim: (0,) * nd)
    return pl.pallas_call(
        body, name=name, grid=(1,), out_shape=[jax.ShapeDtypeStruct(o.shape, F32) for o in own_parts],
        in_specs=[whole(a) for a in own_parts] + [whole(a) for a in landed],
        out_specs=[whole(a) for a in own_parts], compiler_params=_params(("arbitrary",)),
    )(*own_parts, *landed)


def _sum_partials(own_part, landed, name, untiled_rows=False):
    _, h, cols = landed.shape
    tc = LANES if untiled_rows else cols

    def body(own_ref, a_ref, o_ref):
        acc = own_ref[...].astype(F32)
        for s in range(3):
            acc = acc + a_ref[s].astype(F32)
        if untiled_rows:
            o_ref[:, 0, :] = acc
        else:
            o_ref[...] = acc

    if untiled_rows:
        out_shape, out_spec = jax.ShapeDtypeStruct((h, 1, cols), F32), pl.BlockSpec((h, 1, tc), lambda i: (0, 0, i))
    else:
        out_shape, out_spec = jax.ShapeDtypeStruct((h, cols), F32), pl.BlockSpec((h, tc), lambda i: (0, i))
    return pl.pallas_call(
        body, name=name, out_shape=out_shape, grid=(cols // tc,),
        in_specs=[pl.BlockSpec((h, tc), lambda i: (0, i)), pl.BlockSpec((3, h, tc), lambda i: (0, 0, i))],
        out_specs=out_spec, compiler_params=_params(("arbitrary",)),
    )(own_part, landed)


def _share_halves(halves, name):
    n = len(halves)

    def body(*refs):
        ins, outs = refs[:n], refs[n:2 * n]
        send_sems, recv_sems = refs[2 * n:]
        x, y, c, own, sib, chips, chip_idx = _place()
        cps = [_remote(ins[i], outs[i], send_sems.at[i], recv_sems.at[i], sib) for i in range(n)]
        for cp in cps:
            cp.start()
        for cp in cps:
            cp.wait()

    return pl.pallas_call(
        body, name=name,
        out_shape=[jax.ShapeDtypeStruct(p.shape, p.dtype) for p in halves],
        in_specs=[HBM_SPEC] * n, out_specs=[HBM_SPEC] * n,
        scratch_shapes=[pltpu.SemaphoreType.DMA((n,)), pltpu.SemaphoreType.DMA((n,))],
    )(*halves)


def _allreduce_small(packed):
    rows = packed.shape[0]
    n_dev = 8

    def body(in_ref, out_ref, gath, send_sems, recv_sems):
        x, y, c = lax.axis_index("x"), lax.axis_index("y"), lax.axis_index("c")
        me = 4 * x + 2 * y + c
        gath[me] = in_ref[...]
        cps = []
        for k in range(1, n_dev):
            fx, fy, fc = (k >> 2) & 1, (k >> 1) & 1, k & 1
            to = (x ^ fx, y ^ fy, c ^ fc)
            cps.append(_remote(in_ref, gath.at[me], send_sems.at[k - 1], recv_sems.at[k - 1], to))
        for cp in cps:
            cp.start()
        for k in range(1, n_dev):
            fx, fy, fc = (k >> 2) & 1, (k >> 1) & 1, k & 1
            src = 4 * (x ^ fx) + 2 * (y ^ fy) + (c ^ fc)
            slot = gath.at[src]
            _remote(slot, slot, send_sems.at[k - 1], recv_sems.at[k - 1], (x, y, c)).wait_recv()
        for cp in cps:
            cp.wait_send()
        acc = gath[0]
        for d in range(1, n_dev):
            acc = acc + gath[d]
        out_ref[...] = acc

    vm = pl.BlockSpec(memory_space=pltpu.VMEM)
    return pl.pallas_call(
        body, name="allreduce_small", out_shape=jax.ShapeDtypeStruct(packed.shape, F32),
        in_specs=[vm], out_specs=vm,
        scratch_shapes=[pltpu.VMEM((n_dev, rows, LANES), F32),
                        pltpu.SemaphoreType.DMA((n_dev - 1,)), pltpu.SemaphoreType.DMA((n_dev - 1,))],
    )(packed)


def _adam(col, w, g, m, v):
    m2 = ADAM_B1 * m + (1.0 - ADAM_B1) * g
    v2 = ADAM_B2 * v + (1.0 - ADAM_B2) * (g * g)
    m_hat = m2 / (1.0 - ADAM_B1 ** ADAM_STEP)
    v_hat = v2 / (1.0 - ADAM_B2 ** ADAM_STEP)
    delta = -ADAM_LR * (m_hat / (jnp.sqrt(v_hat) + ADAM_EPS) + ADAM_WD * w)
    return delta, m2, v2


def _adam_call(w, g, m, v, name):
    rows, cols = w.shape
    tm = rows
    for cand in (256, 352, 176, 128, 64, 48, 16, 8):
        if rows % cand == 0:
            tm = cand
            break
    return _tiles(_adam, name=name, rows=rows, tm=tm,
                  row_ins=[(w, cols, 0), (g, cols, 0), (m, cols, 0), (v, cols, 0)],
                  row_outs=[(cols, F32)] * 3)


def _adam_big(w, g_mine, g_other, m, v, place, name):
    rows, cols = w.shape
    tc = 256
    nt = cols // 2 // tc

    def body(place_ref, w_ref, gm_ref, go_ref, m_ref, v_ref, g_out, d_out, m_out, v_out):
        g = jnp.where(pl.program_id(0) == place_ref[0], gm_ref[...], go_ref[...])
        d, m2, v2 = _adam(None, w_ref[...], g, m_ref[...], v_ref[...])
        g_out[...] = g
        d_out[...] = d
        m_out[...] = m2
        v_out[...] = v2

    full = pl.BlockSpec((rows, tc), lambda hh, i, p: (0, hh * nt + i))
    half = pl.BlockSpec((rows, tc), lambda hh, i, p: (0, i))
    return pl.pallas_call(
        body, name=name, out_shape=[jax.ShapeDtypeStruct(w.shape, F32)] * 4,
        grid_spec=pltpu.PrefetchScalarGridSpec(
            num_scalar_prefetch=1, grid=(2, nt),
            in_specs=[full, half, half, full, full], out_specs=[full] * 4),
        compiler_params=_params(("arbitrary", "arbitrary")),
    )(place, w, g_mine, g_other, m, v)


def _adam_untiled_rows(w, g_mine, g_other, m, v, place, name):
    rows, _, cols = w.shape
    tc = 256
    nt = cols // 2 // tc
    rb = next(r for r in (206, 128, 103, rows) if rows % r == 0)

    def body(place_ref, w_ref, gm_ref, go_ref, m_ref, v_ref, g_out, d_out, m_out, v_out):
        g = jnp.where(pl.program_id(0) == place_ref[0], gm_ref[...], go_ref[...])
        d, m2, v2 = _adam(None, w_ref[...], g, m_ref[...], v_ref[...])
        g_out[...] = g
        d_out[...] = d
        m_out[...] = m2
        v_out[...] = v2

    full = pl.BlockSpec((rb, 1, tc), lambda hh, i, r, p: (r, 0, hh * nt + i))
    half = pl.BlockSpec((rb, 1, tc), lambda hh, i, r, p: (r, 0, i))
    return pl.pallas_call(
        body, name=name, out_shape=[jax.ShapeDtypeStruct(w.shape, F32)] * 4,
        grid_spec=pltpu.PrefetchScalarGridSpec(
            num_scalar_prefetch=1, grid=(2, nt, rows // rb),
            in_specs=[full, half, half, full, full], out_specs=[full] * 4),
        compiler_params=_params(("arbitrary", "arbitrary", "arbitrary")),
    )(place, w, g_mine, g_other, m, v)


def _pack(arrays, zero=None):
    flat = []
    for a in arrays:
        a = a.reshape(-1).astype(F32)
        if zero is not None:
            a = a + zero
        flat.append(jnp.pad(a, (0, (-a.size) % LANES)))
    out = jnp.concatenate(flat)
    out = jnp.pad(out, (0, (-out.size) % (8 * LANES)))
    return out.reshape(-1, LANES)


def _unpack(packed, shapes):
    flat = packed.reshape(-1)
    out, off = [], 0
    for s in shapes:
        size = int(np.prod(s))
        out.append(flat[off:off + size].reshape(s))
        off += size + (-size) % LANES
    return out


def kernel(x, norm1_w, w_in, gdn_conv_w, gdn_A_log, gdn_dt_bias, gdn_out_norm_w, fox_f_bias, fox_q_norm_w, fox_k_norm_w, w_out, norm2_w, w_ffn_gate, w_ffn_up, w_ffn_down, final_norm_w, loss_target, m_norm1_w, m_w_in, m_gdn_conv_w, m_gdn_A_log, m_gdn_dt_bias, m_gdn_out_norm_w, m_fox_f_bias, m_fox_q_norm_w, m_fox_k_norm_w, m_w_out, m_norm2_w, m_w_ffn_gate, m_w_ffn_up, m_w_ffn_down, m_final_norm_w, v_norm1_w, v_w_in, v_gdn_conv_w, v_gdn_A_log, v_gdn_dt_bias, v_gdn_out_norm_w, v_fox_f_bias, v_fox_q_norm_w, v_fox_k_norm_w, v_w_out, v_norm2_w, v_w_ffn_gate, v_w_ffn_up, v_w_ffn_down, v_final_norm_w):
    cx, cy, cc = lax.axis_index("x"), lax.axis_index("y"), lax.axis_index("c")
    own = 2 * cx + cy
    place = jnp.stack([cc, own]).astype(jnp.int32)

    names = ["w_in", "w_out", "w_gate", "w_up", "w_down"]
    is_t = [True, False, True, True, False]
    to_t = lambda a, t: a[0].T if t else a[0]
    from_t = lambda a, t: (a.T if t else a)[None]
    big_w = [to_t(a, t) for a, t in zip([w_in, w_out, w_ffn_gate, w_ffn_up, w_ffn_down], is_t)]
    big_m = [to_t(a, t) for a, t in zip([m_w_in, m_w_out, m_w_ffn_gate, m_w_ffn_up, m_w_ffn_down], is_t)]
    big_v = [to_t(a, t) for a, t in zip([v_w_in, v_w_out, v_w_ffn_gate, v_w_ffn_up, v_w_ffn_down], is_t)]
    shards = [big_w[0].astype(BF16)]
    small_w = [norm1_w, gdn_conv_w, gdn_A_log, gdn_dt_bias, gdn_out_norm_w, fox_f_bias, fox_q_norm_w,
               fox_k_norm_w, norm2_w, final_norm_w]
    small_m = [m_norm1_w, m_gdn_conv_w, m_gdn_A_log, m_gdn_dt_bias, m_gdn_out_norm_w, m_fox_f_bias,
               m_fox_q_norm_w, m_fox_k_norm_w, m_norm2_w, m_final_norm_w]
    small_v = [v_norm1_w, v_gdn_conv_w, v_gdn_A_log, v_gdn_dt_bias, v_gdn_out_norm_w, v_fox_f_bias,
               v_fox_q_norm_w, v_fox_k_norm_w, v_norm2_w, v_final_norm_w]
    first = _split_start("gather_in_start", _in_proj_plan, [shards[0], gdn_conv_w[0]],
                         [jax.ShapeDtypeStruct((N_CHIPS,) + shards[0].shape, BF16),
                          jax.ShapeDtypeStruct((N_CHIPS, CONV_K, 3 * WIDTH // N_CHIPS), F32)],
                         n_copies=8)
    small_packed = [_pack(p, first["token"][0, 0]) for p in (small_w, small_m, small_v)]
    shards += [(w + first["token"][0, 0]).astype(BF16) for w in big_w[1:]]
    rest = {}

    def first_weights(after):
        w_in_g, conv_g = _split_wait("gather_in_wait", _in_proj_plan, first, [after] + small_packed)
        w_in_g = _forward_halves(w_in_g)
        rest.update(_split_start("gather_rest_start", _gather_plan, shards[1:],
                                 [jax.ShapeDtypeStruct((N_CHIPS,) + s.shape, BF16) for s in shards[1:]],
                                 n_copies=4 * len(shards[1:]), after=w_in_g))
        w_cat = _cat_weights(w_in_g.reshape(D_IN, D_MODEL))
        return w_cat + rest["token"][0, 0].astype(BF16), conv_g.transpose(1, 0, 2).reshape(CONV_K, 3 * WIDTH)

    def late_weights(after):
        w_out_g, w_gate_g, w_up_g, w_down_g = _split_wait("gather_rest_wait", _gather_plan, rest, after)
        return w_out_g.reshape(D_MODEL, D_MODEL), w_gate_g, w_up_g, w_down_g

    def start_reduction(stacks, landed, nms, tag):
        added = _add_halves(stacks, landed, place, "rs_add_" + tag)
        parts = [a[0] for a in added]
        started = _split_start("exchange_" + tag + "_start", _exchange_plan, parts,
                               [jax.ShapeDtypeStruct((3,) + p.shape[1:], p.dtype) for p in parts],
                               n_copies=3 * len(parts))
        return dict(own=[a[1] for a in added], started=started, tag=tag, names=nms)

    def finish_reduction(red, after, updates):
        landed = _split_wait("exchange_" + red["tag"] + "_wait", _exchange_plan, red["started"], after)
        if red["tag"] == "w_in":
            halves = [_sum_partials(red["own"][0], landed[0], "rs_sum_w_in", untiled_rows=True)]
        else:
            halves = _sum_many(red["own"], landed, "rs_sum_" + red["tag"])
        others = _share_halves(halves, "rs_share_" + red["tag"])
        return [upd(gm, go) for upd, gm, go in zip(updates, halves, others)]

    def transport_update(b):
        def upd(gm, go):
            res = _adam_big(big_w[b], gm, go, big_m[b], big_v[b], place, "adam_" + names[b])
            early_done.append(res[1])
            return [from_t(a, is_t[b]) for a in res]
        return upd

    early_done = []

    def w_in_update(gm, go):
        rows3 = lambda a: jnp.transpose(a, (2, 0, 1))
        res = _adam_untiled_rows(rows3(w_in), gm, go, rows3(m_w_in), rows3(v_w_in), place, "adam_w_in")
        return [jnp.transpose(a, (1, 2, 0)) for a in res]

    early = {}

    def early_grads_ready(g_out, g_gate, g_up, g_down):
        stacks = [g_out.reshape(N_CHIPS, D_MODEL // N_CHIPS, D_MODEL), g_gate, g_up, g_down]
        swap = _split_start("swap_early_start", _swap_plan, stacks,
                            [jax.ShapeDtypeStruct(s.shape[:2] + (s.shape[2] // 2,), s.dtype) for s in stacks],
                            n_copies=len(stacks))
        early.update(stacks=stacks, swap=swap)
        return swap["token"][0, 0]

    def early_grads_continue(after):
        landed = _split_wait("swap_early_wait", _swap_plan, early["swap"], after)
        early.update(start_reduction(early["swap"]["srcs_after"], landed, names[1:], "early"))
        return early["started"]["token"][0, 0]

    grad_x, g_cat, _, _, _, _, small = _local_step(
        x[0], loss_target[0], norm1_w + first["token"][0, 0], gdn_A_log[0], gdn_dt_bias[0],
        gdn_out_norm_w[0], fox_f_bias[0], fox_q_norm_w[0], fox_k_norm_w[0], norm2_w, final_norm_w.reshape(1, -1),
        first_weights, late_weights, early_grads_ready, early_grads_continue)

    g_in_stack = _uncat_grad(g_cat).reshape(N_CHIPS, D_IN // N_CHIPS, D_MODEL)
    swap_in = _split_start("swap_w_in_start", _swap_plan, [g_in_stack],
                           [jax.ShapeDtypeStruct((N_CHIPS, D_IN // N_CHIPS, D_MODEL // 2), F32)],
                           n_copies=1)

    order = ["norm1_w", "conv_w", "a_log", "dt_bias", "out_norm_w", "f_bias", "q_norm_w", "k_norm_w",
             "norm2_w", "final_w"]
    red = _allreduce_small(_pack([small[k] for k in order] + [small["loss"]], swap_in["token"][0, 0]))
    red_shapes = [(1, D_MODEL), (CONV_K, 3 * WIDTH), (1, HEADS), (1, HEADS), (1, HEAD_DIM), (1, HEADS),
                  (1, HEAD_DIM), (1, HEAD_DIM), (1, D_MODEL), (D_MODEL,), ()]
    red_list = _unpack(red, red_shapes)
    loss = red_list[-1]
    small_g = dict(zip(order, red_list[:-1]))
    shard_cols = 3 * WIDTH // N_CHIPS
    small_g["conv_w"] = lax.dynamic_slice_in_dim(small_g["conv_w"], own * shard_cols, shard_cols, axis=1)[None]
    small_gl = [small_g[k].reshape(w.shape) for k, w in zip(order, small_w)]
    s_delta, s_m, s_v = _adam_call(small_packed[0], _pack(small_gl), small_packed[1], small_packed[2], "adam_small")
    landed_in = _split_wait("swap_w_in_wait", _swap_plan, swap_in, s_delta)
    late = start_reduction(swap_in["srcs_after"], landed_in, names[:1], "w_in")
    big_upd = finish_reduction(early, late["started"]["token"], [transport_update(b) for b in range(1, 5)])
    big_upd = finish_reduction(late, early_done, [w_in_update]) + big_upd
    shapes = [w.shape for w in small_w]
    s_delta, s_m, s_v = _unpack(s_delta, shapes), _unpack(s_m, shapes), _unpack(s_v, shapes)

    big_pos = {1: 0, 9: 1, 11: 2, 12: 3, 13: 4}
    small_pos = {0: 0, 2: 1, 3: 2, 4: 3, 5: 4, 6: 5, 7: 6, 8: 7, 10: 8, 14: 9}
    grads, deltas, new_m, new_v = [], [], [], []
    for pos in range(15):
        if pos in big_pos:
            b = big_pos[pos]
            g, d, m2, v2 = big_upd[b]
            grads.append(g)
            deltas.append(d)
            new_m.append(m2)
            new_v.append(v2)
        else:
            s = small_pos[pos]
            grads.append(small_gl[s])
            deltas.append(s_delta[s])
            new_m.append(s_m[s])
            new_v.append(s_v[s])
    return (loss, grad_x[None], *grads, *deltas, *new_m, *new_v)
```

```python
import jax
import jax.numpy as jnp
import numpy as np
from jax import lax
from jax.experimental import pallas as pl
from jax.experimental.pallas import tpu as pltpu

F32 = jnp.float32
BF16 = jnp.bfloat16

D_MODEL = 1024
HEADS = 8
HEAD_DIM = 64
PAIRS = HEADS // 2
WIDTH = HEADS * HEAD_DIM
CHUNK = 64
CONV_K = 4
D_FF = 2816
FF_SHARD = D_FF // 4
EPS = 1e-6
SCALE = HEAD_DIM ** -0.5
LANES = 128
N_CHIPS = 4
D_IN = 4120
D_CAT = 4224
COL_SMALL = 4096 // LANES

ADAM_LR = 0.001
ADAM_B1 = 0.9
ADAM_B2 = 0.999
ADAM_EPS = 1e-08
ADAM_WD = 0.01
ADAM_STEP = 10

VMEM_LIMIT = 56 * 1024 * 1024
MESH = pl.DeviceIdType.MESH
HIGHEST = lax.Precision.HIGHEST


def _params(sem):
    return pltpu.CompilerParams(dimension_semantics=sem, vmem_limit_bytes=VMEM_LIMIT)


_CONTRACT = {"nn": ((1,), (0,)), "nt": ((1,), (1,)), "tn": ((0,), (0,))}


def _mm(a, b, *, dims, name, out_dtype=F32, add=None, tm=1024, tn=512, tk=512):
    if dims == "nn":
        (m, k), (k2, n) = a.shape, b.shape
    elif dims == "nt":
        (m, k), (n, k2) = a.shape, b.shape
    else:
        (k, m), (k2, n) = a.shape, b.shape
    assert k == k2, (a.shape, b.shape, dims)
    tm, tn, tk = min(tm, m), min(tn, n), min(tk, k)
    assert m % tm == 0 and n % tn == 0 and k % tk == 0, (m, n, k, tm, tn, tk)
    nk = k // tk
    a_spec = (pl.BlockSpec((tk, tm), lambda i, j, kk: (kk, i)) if dims == "tn"
              else pl.BlockSpec((tm, tk), lambda i, j, kk: (i, kk)))
    b_spec = (pl.BlockSpec((tn, tk), lambda i, j, kk: (j, kk)) if dims == "nt"
              else pl.BlockSpec((tk, tn), lambda i, j, kk: (kk, j)))
    o_spec = pl.BlockSpec((tm, tn), lambda i, j, kk: (i, j))
    contract = (_CONTRACT[dims], ((), ()))
    has_add = add is not None

    def body(*refs):
        a_ref, b_ref = refs[:2]
        add_ref = refs[2] if has_add else None
        o_ref = refs[3] if has_add else refs[2]
        part = lax.dot_general(a_ref[...].astype(BF16), b_ref[...].astype(BF16), contract,
                               preferred_element_type=F32)

        def finish(r):
            if has_add:
                r = r + add_ref[...].astype(F32)
            o_ref[...] = r.astype(out_dtype)

        if nk == 1:
            finish(part)
            return
        acc = refs[-1]
        kk = pl.program_id(2)

        @pl.when(kk == 0)
        def _():
            acc[...] = part

        @pl.when(kk > 0)
        def _():
            acc[...] += part

        @pl.when(kk == nk - 1)
        def _():
            finish(acc[...])

    ins = [a, b] + ([add] if has_add else [])
    in_specs = [a_spec, b_spec] + ([o_spec] if has_add else [])
    return pl.pallas_call(
        body, name=name, grid=(m // tm, n // tn, nk),
        in_specs=in_specs, out_specs=o_spec,
        out_shape=jax.ShapeDtypeStruct((m, n), out_dtype),
        scratch_shapes=[pltpu.VMEM((tm, tn), F32)] if nk > 1 else [],
        compiler_params=_params(("parallel", "parallel", "arbitrary")),
    )(*ins)


def _mm_blocks(a, b, *, name, grid, a_spec, b_spec, o_spec, out_shape, dims, n_sum=0, add=None, add_spec=None,
               epilogue=None, extra=(), n_acc=0):
    contract = (_CONTRACT[dims], ((), ()))
    has_add = add is not None
    n_in = 2 + has_add + len(extra)

    def body(*refs):
        a_ref, b_ref = refs[:2]
        dot = lambda x, y: lax.dot_general(x.astype(BF16), y.astype(BF16), contract, preferred_element_type=F32)
        if n_sum:
            r = dot(a_ref[0], b_ref[0])
            for s in range(1, n_sum):
                r = r + dot(a_ref[s], b_ref[s])
        else:
            r = dot(a_ref[...], b_ref[...])
        if has_add:
            r = r + refs[2][...].astype(F32)
        if epilogue is None:
            refs[-1][...] = r.astype(refs[-1].dtype)
        else:
            outs = epilogue(r, *[e[...] for e in refs[2 + has_add:n_in]])
            out_refs = refs[n_in:]
            n_plain = len(out_refs) - n_acc
            for o_ref, val in zip(out_refs[:n_plain], outs):
                o_ref[...] = val.astype(o_ref.dtype)
            if n_acc:
                @pl.when(pl.program_id(0) == 0)
                def _():
                    for o_ref in out_refs[n_plain:]:
                        o_ref[...] = jnp.zeros_like(o_ref)
                for o_ref, val in zip(out_refs[n_plain:], outs[n_plain:]):
                    o_ref[...] += val

    ins = [a, b] + ([add] if has_add else []) + [e[0] for e in extra]
    in_specs = [a_spec, b_spec] + ([add_spec] if has_add else []) + [e[1] for e in extra]
    sem = ("arbitrary" if n_acc else "parallel",) * len(grid)
    return pl.pallas_call(
        body, name=name, grid=grid, in_specs=in_specs, out_specs=o_spec, out_shape=out_shape,
        compiler_params=_params(sem),
    )(*ins)


def _tiles(fn, *, name, rows, tm, ncol=1, row_ins=(), col_consts=(), full_consts=(),
           row_outs=(), acc_outs=()):
    nt = rows // tm
    assert rows % tm == 0
    n_full, n_col, n_row = len(full_consts), len(col_consts), len(row_ins)
    n_ro, n_acc = len(row_outs), len(acc_outs)

    def body(*refs):
        ins = refs[:n_full + n_col + n_row]
        outs = refs[n_full + n_col + n_row:]
        i = pl.program_id(1)
        res = fn(pl.program_id(0), *[r[...] for r in ins])
        for r, v in zip(outs[:n_ro], res[:n_ro]):
            r[...] = v.astype(r.dtype)
        if n_acc:
            @pl.when(i == 0)
            def _():
                for r in outs[n_ro:]:
                    r[...] = jnp.zeros_like(r)
            for r, v in zip(outs[n_ro:], res[n_ro:]):
                r[...] += v

    in_specs = [pl.BlockSpec(a.shape, lambda j, i, nd=a.ndim: (0,) * nd) for a in full_consts]
    in_specs += [pl.BlockSpec((nr, w), lambda j, i, o=o: (0, o + j)) for (_, nr, w, o) in col_consts]
    in_specs += [pl.BlockSpec((tm, w), lambda j, i, o=o: (i, o + j)) for (_, w, o) in row_ins]
    out_specs = [pl.BlockSpec((tm, w), lambda j, i: (i, j)) for (w, _) in row_outs]
    out_specs += [pl.BlockSpec((nr, w), lambda j, i: (0, j)) for (nr, w) in acc_outs]
    out_shape = [jax.ShapeDtypeStruct((rows, w * ncol), dt) for (w, dt) in row_outs]
    out_shape += [jax.ShapeDtypeStruct((nr, w * ncol), F32) for (nr, w) in acc_outs]
    args = list(full_consts) + [c[0] for c in col_consts] + [r[0] for r in row_ins]
    out = pl.pallas_call(
        body, name=name, grid=(ncol, nt), in_specs=in_specs, out_specs=out_specs, out_shape=out_shape,
        compiler_params=_params(("parallel", "arbitrary")),
    )(*args)
    return out


def _rms(x, w):
    return x * lax.rsqrt(jnp.mean(x * x, axis=-1, keepdims=True) + EPS) * w


def _lane_lo(shape):
    return lax.broadcasted_iota(jnp.int32, shape, len(shape) - 1) < HEAD_DIM


def _pair_sum(x):
    lo = _lane_lo(x.shape)
    s0 = jnp.sum(jnp.where(lo, x, 0.0), axis=-1, keepdims=True)
    s1 = jnp.sum(jnp.where(lo, 0.0, x), axis=-1, keepdims=True)
    return jnp.where(lo, s0, s1)


def _head_col(x, lo, h):
    keep = lo if h == 0 else jnp.logical_not(lo)
    return jnp.max(jnp.where(keep, x, -jnp.inf), axis=-1, keepdims=True)


def _softplus(x):
    return jnp.maximum(x, 0.0) + jnp.log1p(jnp.exp(-jnp.abs(x)))


def _silu(x):
    return x * jax.nn.sigmoid(x)


def _dot(a, b, contract):
    return lax.dot_general(a.astype(BF16), b.astype(BF16), (contract, ((), ())),
                           preferred_element_type=F32)


def _dot32(a, b, contract):
    return lax.dot_general(a, b, (contract, ((), ())), precision=HIGHEST, preferred_element_type=F32)


def _bd(y):
    yy = jnp.concatenate([y, y], axis=0)
    r = lax.broadcasted_iota(jnp.int32, yy.shape, 0) < HEAD_DIM
    c = lax.broadcasted_iota(jnp.int32, yy.shape, 1) < HEAD_DIM
    return jnp.where(r == c, yy, 0.0)


def _pp(x, y):
    return _dot(x, _bd(y), _CONTRACT["nn"])


def _pp_nt(x, y):
    return _dot(x, _bd(y), _CONTRACT["nt"])


def _pp_tn(x, y):
    full = _dot(x, y, _CONTRACT["tn"])
    return jnp.where(_lane_lo((HEAD_DIM, LANES)), full[:HEAD_DIM], full[HEAD_DIM:])


def _gdn_masks():
    row = lax.broadcasted_iota(jnp.int32, (CHUNK, LANES), 0)
    col = lax.broadcasted_iota(jnp.int32, (CHUNK, LANES), 1) % HEAD_DIM
    return row, col


def _interleave(chains):
    live = list(chains)
    while live:
        for g in list(live):
            try:
                next(g)
            except StopIteration:
                live.remove(g)


def _gdn_forward(qkv, betax, gcx, grow, rows):
    nchunk = rows // CHUNK

    def body(q_ref, k_ref, v_ref, bx_ref, gx_ref, gr_ref, o_ref, ss_ref, ts_ref, state):
        n = pl.program_id(0)

        @pl.when(n == 0)
        def _():
            state[...] = jnp.zeros_like(state)

        row, col = _gdn_masks()
        incl, strict = col <= row, col < row

        def chain(p):
            lanes = pl.ds(p * LANES, LANES)
            q, k, v, bx, gx = q_ref[:, lanes], k_ref[:, lanes], v_ref[:, lanes], bx_ref[:, lanes], gx_ref[:, lanes]
            gr = gr_ref[0, p]
            glast = gx_ref[pl.ds(CHUNK - 1, 1), lanes]
            s = state[p]
            dm = jnp.where(incl, jnp.exp(jnp.minimum(gx - gr, 0.0)), 0.0)
            kb, vb, eg, qs = k * bx, v * bx, jnp.exp(gx), q * SCALE
            yield
            big_g, big_p = _pp_nt(kb, k), _pp_nt(qs, k)
            yield
            x = -jnp.where(strict, big_g * dm, 0.0)
            att = jnp.where(incl, big_p * dm, 0.0)
            tm = jnp.where(row == col, 1.0, 0.0) + x
            x = _pp(x, x)
            yield
            for _ in range(4):
                step, x = _pp(tm, x), _pp(x, x)
                yield
                tm = tm + step
            tm = tm + _pp(tm, x)
            yield
            u, w = _pp(tm, vb), _pp(tm, kb * eg)
            yield
            ws, qgs = _pp(w, s), _pp(qs * eg, s)
            yield
            vn = u - ws
            kd = k * jnp.exp(glast - gx)
            avn, upd = _pp(att, vn), _pp_tn(kd, vn)
            yield
            ss_ref[0, p] = s
            ts_ref[0, p] = tm
            o_ref[:, lanes] = qgs + avn
            state[p] = s * jnp.exp(glast) + upd

        _interleave([chain(p) for p in range(PAIRS)])

    blk = lambda j: pl.BlockSpec((CHUNK, WIDTH), lambda n, j=j: (n, j))
    sv = pl.BlockSpec((1, PAIRS, CHUNK, LANES), lambda n: (n, 0, 0, 0))
    return pl.pallas_call(
        body, name="gdn_fwd", grid=(nchunk,),
        in_specs=[blk(0), blk(1), blk(2), blk(0), blk(0),
                  pl.BlockSpec((1, PAIRS, 1, LANES), lambda n: (n, 0, 0, 0))],
        out_specs=[blk(0), sv, sv],
        out_shape=[jax.ShapeDtypeStruct((rows, WIDTH), F32),
                   jax.ShapeDtypeStruct((nchunk, PAIRS, CHUNK, LANES), F32),
                   jax.ShapeDtypeStruct((nchunk, PAIRS, CHUNK, LANES), F32)],
        scratch_shapes=[pltpu.VMEM((PAIRS, CHUNK, LANES), F32)],
        compiler_params=_params(("arbitrary",)),
    )(qkv, qkv, qkv, betax, gcx, grow)


def _gdn_backward(qkv, betax, gcx, grow, ssave, tsave, do, rows):
    nchunk = rows // CHUNK

    def body(q_ref, k_ref, v_ref, bx_ref, gx_ref, gr_ref, ss_ref, ts_ref, do_ref,
             dq_ref, dk_ref, dv_ref, dbx_ref, dgx_ref, dgr_ref, dstate):
        n = pl.program_id(0)

        @pl.when(n == 0)
        def _():
            dstate[...] = jnp.zeros_like(dstate)

        row, col = _gdn_masks()
        incl, strict = col <= row, col < row

        def chain(p):
            lanes = pl.ds(p * LANES, LANES)
            q, k, v, bx, gx = q_ref[:, lanes], k_ref[:, lanes], v_ref[:, lanes], bx_ref[:, lanes], gx_ref[:, lanes]
            gr = gr_ref[0, p]
            glast = gx_ref[pl.ds(CHUNK - 1, 1), lanes]
            s, tm, d_o = ss_ref[0, p], ts_ref[0, p], do_ref[:, lanes]
            ds_out = dstate[p]
            dm = jnp.where(incl, jnp.exp(jnp.minimum(gx - gr, 0.0)), 0.0)
            kb, vb, eg, qs = k * bx, v * bx, jnp.exp(gx), q * SCALE
            kbg, qg = kb * eg, qs * eg
            ed = jnp.exp(glast - gx)
            kd = k * ed
            eglast = jnp.exp(glast)
            yield
            big_g, big_p = _pp_nt(kb, k), _pp_nt(qs, k)
            u, w = _pp(tm, vb), _pp(tm, kbg)
            dqg, kds = _pp_nt(d_o, s), _pp(kd, ds_out)
            yield
            low = jnp.where(strict, big_g * dm, 0.0)
            att = jnp.where(incl, big_p * dm, 0.0)
            ws, atd = _pp(w, s), _pp_tn(att, d_o)
            yield
            vn = u - ws
            dvn = kds + atd
            dkd, datt_raw = _pp_nt(vn, ds_out), _pp_nt(d_o, vn)
            dw_neg, dvb = _pp_nt(dvn, s), _pp_tn(tm, dvn)
            dtm_a, wdv = _pp_nt(dvn, vb), _pp_tn(w, dvn)
            qgd = _pp_tn(qg, d_o)
            yield
            datt = jnp.where(incl, datt_raw, 0.0)
            dw = -dw_neg
            dtm_b, dkbg = _pp_nt(dw, kbg), _pp_tn(tm, dw)
            dbig_p = datt * dm
            dqs_a, dk_p = _pp(dbig_p, k), _pp_tn(dbig_p, qs)
            yield
            inner = _pp_tn(tm, dtm_a + dtm_b)
            yield
            dlow = jnp.where(strict, -_pp_nt(inner, tm), 0.0)
            yield
            dbig_g = dlow * dm
            dkb_a, dk_g = _pp(dbig_g, k), _pp_tn(dbig_g, kb)
            yield
            dkb = dkb_a + dkbg * eg
            dqs = dqs_a + dqg * eg
            dk = dk_g + dk_p + dkd * ed + dkb * bx
            z = dlow * low + datt * att
            kdterm = dkd * kd
            dglast = (jnp.sum(ds_out * s, axis=0, keepdims=True) * eglast
                      + jnp.sum(kdterm, axis=0, keepdims=True))
            dgx = dqg * qg + dkbg * kbg - kdterm
            dgx = dgx + jnp.where(col == 0, _pair_sum(z), 0.0)
            dgx = dgx + jnp.where(row == CHUNK - 1, dglast, 0.0)
            dq_ref[:, lanes] = dqs * SCALE
            dk_ref[:, lanes] = dk
            dv_ref[:, lanes] = dvb * bx
            dbx_ref[:, lanes] = dkb * k + dvb * v
            dgx_ref[:, lanes] = dgx
            dgr_ref[0, p] = -jnp.sum(z, axis=0, keepdims=True)
            dstate[p] = ds_out * eglast + qgd - wdv

        _interleave([chain(p) for p in range(PAIRS)])

    last = nchunk - 1
    blk = lambda j: pl.BlockSpec((CHUNK, WIDTH), lambda n, j=j: (last - n, j))
    sv = pl.BlockSpec((1, PAIRS, CHUNK, LANES), lambda n: (last - n, 0, 0, 0))
    gr_spec = pl.BlockSpec((1, PAIRS, 1, LANES), lambda n: (last - n, 0, 0, 0))
    wide = jax.ShapeDtypeStruct((rows, WIDTH), F32)
    return pl.pallas_call(
        body, name="gdn_bwd", grid=(nchunk,),
        in_specs=[blk(0), blk(1), blk(2), blk(0), blk(0), gr_spec, sv, sv, blk(0)],
        out_specs=[blk(0)] * 5 + [gr_spec],
        out_shape=[wide] * 5 + [jax.ShapeDtypeStruct((nchunk, PAIRS, 1, LANES), F32)],
        scratch_shapes=[pltpu.VMEM((PAIRS, CHUNK, LANES), F32)],
        compiler_params=_params(("arbitrary",)),
    )(qkv, qkv, qkv, betax, gcx, grow, ssave, tsave, do)


ATT_TQ = 256


def _att_scores(qh, kt, fk, diag):
    s = _dot(qh, kt, _CONTRACT["nt"]) - fk
    if diag:
        r = lax.broadcasted_iota(jnp.int32, s.shape, 0)
        c = lax.broadcasted_iota(jnp.int32, s.shape, 1)
        s = jnp.where(r >= c, s, -jnp.inf)
    return s


def _head_masks(n):
    lo = _lane_lo((n, LANES))
    return [lo, jnp.logical_not(lo)]


def _attention_forward(fqk, proj, frow, rows):
    tq = tk = min(ATT_TQ, rows)
    nq = rows // tq
    v_off = 3072 // LANES

    def body(q_ref, k_ref, v_ref, fr_ref, o_ref, lse_ref):
        qi = pl.program_id(1)
        q = q_ref[...] * SCALE
        keep_q, keep_k = _head_masks(tq), _head_masks(tk)
        qh = [jnp.where(keep_q[h], q, 0.0).astype(BF16) for h in range(2)]

        def tile(ki, carry, diag):
            k0 = pl.multiple_of(ki * tk, tk)
            kt = k_ref[pl.ds(k0, tk), :].astype(BF16)
            v_t = v_ref[pl.ds(k0, tk), :]
            out = [None, None]

            def chain(h):
                m, l, acc = carry[h]
                vt = jnp.where(keep_k[h], v_t, 0.0).astype(BF16)
                yield
                s = _att_scores(qh[h], kt, fr_ref[0, pl.ds(h, 1), pl.ds(k0, tk)], diag)
                yield
                m_new = jnp.maximum(m, jnp.max(s, axis=-1, keepdims=True))
                p = jnp.exp(s - m_new)
                alpha = jnp.exp(m - m_new)
                l = alpha * l + jnp.sum(p, axis=-1, keepdims=True)
                p_hi = p.astype(BF16)
                p_lo = p - p_hi.astype(F32)
                yield
                out[h] = (m_new, l, alpha * acc + _dot(p_hi, vt, _CONTRACT["nn"]) + _dot(p_lo, vt, _CONTRACT["nn"]))

            _interleave([chain(0), chain(1)])
            return tuple(out)

        one = (jnp.full((tq, 1), -jnp.inf, F32), jnp.zeros((tq, 1), F32), jnp.zeros((tq, LANES), F32))
        carry = lax.fori_loop(0, qi, lambda ki, c: tile(ki, c, False), (one, one))
        (m0, l0, acc0), (m1, l1, acc1) = tile(qi, carry, True)
        o_ref[...] = acc0 / l0 + acc1 / l1
        lse_ref[...] = jnp.where(keep_q[0], m0 + jnp.log(l0), m1 + jnp.log(l1))

    whole = lambda off: pl.BlockSpec((rows, LANES), lambda p, i, off=off: (0, off + p))
    qblk = lambda off: pl.BlockSpec((tq, LANES), lambda p, i, off=off: (i, off + p))
    wide = jax.ShapeDtypeStruct((rows, WIDTH), F32)
    return pl.pallas_call(
        body, name="fox_fwd", grid=(PAIRS, nq),
        in_specs=[qblk(0), whole(PAIRS), whole(v_off), pl.BlockSpec((1, 2, rows), lambda p, i: (p, 0, 0))],
        out_specs=[qblk(0), qblk(0)], out_shape=[wide, wide],
        compiler_params=_params(("parallel", "arbitrary")),
    )(fqk, fqk, proj, frow)


def _attention_backward(fqk, proj, frow, ao, lse, dao, rows):
    tq = tk = min(ATT_TQ, rows)
    nq = rows // tq
    v_off = 3072 // LANES

    def body(q_ref, k_ref, v_ref, fr_ref, o_ref, lse_ref, do_ref, dq_ref, dk_ref, dv_ref, dfr_ref):
        ki = pl.program_id(1)

        @pl.when(ki == 0)
        def _():
            dq_ref[...] = jnp.zeros_like(dq_ref)

        keep_q, keep_k = _head_masks(tq), _head_masks(tk)
        k_t = k_ref[...]
        kt = k_t.astype(BF16)
        vt = v_ref[...].astype(BF16)
        kh = [jnp.where(keep_k[h], k_t, 0.0).astype(BF16) for h in range(2)]
        fk = [fr_ref[0, pl.ds(h, 1), :] for h in range(2)]

        def tile(qi, carry, diag):
            dk, dv, df0, df1 = carry
            rows_q = pl.ds(pl.multiple_of(qi * tq, tq), tq)
            q, d_o, lse_t = q_ref[rows_q, :] * SCALE, do_ref[rows_q, :], lse_ref[rows_q, :]
            delta_x = _pair_sum(d_o.astype(BF16).astype(F32) * o_ref[rows_q, :])
            res = [None, None]

            def chain(h):
                qh = jnp.where(keep_q[h], q, 0.0).astype(BF16)
                doh = jnp.where(keep_q[h], d_o, 0.0).astype(BF16)
                lse_h, delta_h = _head_col(lse_t, keep_q[0], h), _head_col(delta_x, keep_q[0], h)
                yield
                s, dp = _att_scores(qh, kt, fk[h], diag), _dot(doh, vt, _CONTRACT["nt"])
                yield
                p = jnp.exp(s - lse_h)
                ds = p * (dp - delta_h)
                yield
                res[h] = (_dot(p, doh, _CONTRACT["tn"]), _dot(ds, qh, _CONTRACT["tn"]),
                          _dot(ds, kh[h], _CONTRACT["nn"]), jnp.sum(ds, axis=0, keepdims=True))

            _interleave([chain(0), chain(1)])
            (dv0, dk0, dq0, s0), (dv1, dk1, dq1, s1) = res
            dq_ref[rows_q, :] += (dq0 + dq1) * SCALE
            return dk + dk0 + dk1, dv + dv0 + dv1, df0 - s0, df1 - s1

        zero_kv = jnp.zeros((tk, LANES), F32)
        zero_f = jnp.zeros((1, tk), F32)
        carry = tile(ki, (zero_kv, zero_kv, zero_f, zero_f), True)
        dk, dv, df0, df1 = lax.fori_loop(ki + 1, nq, lambda qi, c: tile(qi, c, False), carry)
        dk_ref[...] = dk
        dv_ref[...] = dv.astype(dv_ref.dtype)
        dfr_ref[0, pl.ds(0, 1), :] = df0
        dfr_ref[0, pl.ds(1, 1), :] = df1

    whole = lambda off: pl.BlockSpec((rows, LANES), lambda p, i, off=off: (0, off + p))
    kblk = lambda off: pl.BlockSpec((tk, LANES), lambda p, i, off=off: (i, off + p))
    fr_spec = pl.BlockSpec((1, 2, tk), lambda p, i: (p, 0, i))
    wide = jax.ShapeDtypeStruct((rows, WIDTH), F32)
    return pl.pallas_call(
        body, name="fox_bwd", grid=(PAIRS, nq),
        in_specs=[whole(0), kblk(PAIRS), kblk(v_off), fr_spec, whole(0), whole(0), whole(0)],
        out_specs=[whole(0), kblk(0), kblk(0), fr_spec],
        out_shape=[wide, wide, jax.ShapeDtypeStruct((rows, WIDTH), BF16),
                   jax.ShapeDtypeStruct((PAIRS, 2, rows), F32)],
        compiler_params=_params(("parallel", "arbitrary")),
    )(fqk, fqk, proj, frow, ao, lse, dao)


def _lane_ids(shape):
    return lax.broadcasted_iota(jnp.int32, shape, len(shape) - 1)


def _gates_elem(a_log, dt_bias, f_bias, pre):
    lane = _lane_ids(pre.shape)
    beta = jax.nn.sigmoid(pre)
    g = -jnp.exp(a_log) * _softplus(pre + dt_bias)
    lf = -_softplus(-(pre + f_bias))
    return jnp.where(lane < 8, beta, jnp.where(lane < 16, g, jnp.where(lane < 24, lf, 0.0)))


def _tri_consts():
    r = np.arange(LANES)[:, None]
    c = np.arange(LANES)[None, :]
    full = (c <= r).astype(np.float32)
    chunked = full * ((r // CHUNK) == (c // CHUNK))
    return jnp.asarray(chunked), jnp.asarray(full)


def _cums_fwd(lc, lf, gates):
    rows = gates.shape[0]
    lane = _lane_ids((LANES, LANES))
    carry = jnp.zeros((1, LANES), F32)
    out = []
    for r in range(rows // LANES):
        blk = gates[r * LANES:(r + 1) * LANES]
        gc = _dot32(lc, blk, _CONTRACT["nn"])
        f = _dot32(lf, blk, _CONTRACT["nn"]) + carry
        carry = carry + jnp.sum(blk, axis=0, keepdims=True)
        out.append(jnp.where((lane >= 8) & (lane < 16), gc, jnp.where((lane >= 16) & (lane < 24), f, 0.0)))
    return jnp.concatenate(out, axis=0)


def _cums_bwd(lc, lf, dcums):
    rows = dcums.shape[0]
    lane = _lane_ids((LANES, LANES))
    is_g = (lane >= 8) & (lane < 16)
    is_f = (lane >= 16) & (lane < 24)
    carry = jnp.zeros((1, LANES), F32)
    out = [None] * (rows // LANES)
    for r in reversed(range(rows // LANES)):
        blk = dcums[r * LANES:(r + 1) * LANES]
        dg = jnp.where(is_g, blk, 0.0)
        df = jnp.where(is_f, blk, 0.0)
        out[r] = _dot32(lc, dg, _CONTRACT["tn"]) + _dot32(lf, df, _CONTRACT["tn"]) + carry
        carry = carry + jnp.sum(df, axis=0, keepdims=True)
    return jnp.concatenate(out, axis=0)


def _expand_consts():
    xb = np.zeros((LANES, WIDTH), np.float32)
    xg = np.zeros((LANES, WIDTH), np.float32)
    for h in range(HEADS):
        xb[h, h * HEAD_DIM:(h + 1) * HEAD_DIM] = 1.0
        xg[8 + h, h * HEAD_DIM:(h + 1) * HEAD_DIM] = 1.0
    return jnp.asarray(xb), jnp.asarray(xg)


def _shift_down(x, s):
    if s == 0:
        return x
    row = lax.broadcasted_iota(jnp.int32, x.shape, 0)
    return jnp.where(row >= s, pltpu.roll(x, s, 0), 0.0)


def _shift_up(x, s):
    if s == 0:
        return x
    n = x.shape[0]
    row = lax.broadcasted_iota(jnp.int32, x.shape, 0)
    return jnp.where(row < n - s, pltpu.roll(x, n - s, 0), 0.0)


def _row_of(cw, i):
    row = lax.broadcasted_iota(jnp.int32, cw.shape, 0)
    return jnp.sum(jnp.where(row == i, cw, 0.0), axis=0, keepdims=True)


def _conv(cw, x):
    c = jnp.zeros_like(x)
    for i in range(CONV_K):
        c = c + _row_of(cw, i) * _shift_down(x, CONV_K - 1 - i)
    return c


def _post_conv(is_qk, c):
    s = _silu(c)
    n = s * lax.rsqrt(_pair_sum(s * s) + EPS)
    return jnp.where(is_qk, n, s)


def _gdn_prep_fwd(col, cw, x):
    return (_post_conv(col < 2 * PAIRS, _conv(cw, x)),)


def _gdn_prep_bwd(is_qk, cw, x, dy):
    c = _conv(cw, x)
    _, vjp = jax.vjp(lambda cc: _post_conv(is_qk, cc), c)
    (dc,) = vjp(dy)
    dx = jnp.zeros_like(x)
    row = lax.broadcasted_iota(jnp.int32, cw.shape, 0)
    dcw = jnp.zeros(cw.shape, F32)
    for i in range(CONV_K):
        s = CONV_K - 1 - i
        dx = dx + _row_of(cw, i) * _shift_up(dc, s)
        dcw = dcw + jnp.where(row == i, jnp.sum(dc * _shift_down(x, s), axis=0, keepdims=True), 0.0)
    return dx, dcw


def _head_rms(w, x):
    return x * lax.rsqrt(_pair_sum(x * x) / HEAD_DIM + EPS) * w


def _cat_weights(w_in_t):
    tail = jnp.pad(w_in_t[4112:4120], ((0, D_CAT - D_IN), (0, 0)))
    return jnp.concatenate([w_in_t[:2048], w_in_t[2064:4112], w_in_t[2048:2064], tail], axis=0)


def _uncat_grad(g):
    return jnp.concatenate([g[:2048], g[4096:4112], g[2048:4096], g[4112:4120]], axis=0)


def _lanes_to_rowform(v8, rows):
    return v8.reshape(rows // CHUNK, CHUNK, HEADS).transpose(0, 2, 1).reshape(rows // CHUNK, PAIRS, 1, LANES)


def _rowform_to_lanes(v, rows):
    return v.reshape(rows // CHUNK, HEADS, CHUNK).transpose(0, 2, 1).reshape(rows, HEADS)


def _local_step(x, target, norm1_w, a_log, dt_bias, out_norm_w, f_bias, q_norm_w, k_norm_w,
                norm2_w, final_w, first_weights, late_weights, early_grads_ready, early_grads_continue):
    rows = x.shape[0]
    tm = min(512, rows)
    lc, lf = _tri_consts()
    xb, xg = _expand_consts()

    (h1,) = _tiles(lambda col, w, xx: (_rms(xx, w),), name="norm1", rows=rows, tm=tm,
                   full_consts=[norm1_w], row_ins=[(x, D_MODEL, 0)], row_outs=[(D_MODEL, BF16)])
    w_cat, conv_w = first_weights(h1)
    proj = _mm(h1, w_cat, dims="nt", name="in_proj", tn=1408, tk=1024)

    lane_pad = lambda v, off: jnp.pad(v.reshape(1, -1), ((0, 0), (off, LANES - off - v.size)))
    p_a, p_dt, p_fb = lane_pad(a_log, 8), lane_pad(dt_bias, 8), lane_pad(f_bias, 16)

    def gates_fwd(col, lcv, lfv, a, dt, fb, pre):
        gates = _gates_elem(a, dt, fb, pre)
        return gates, _cums_fwd(lcv, lfv, gates)

    gates, cums = _tiles(gates_fwd, name="gates", rows=rows, tm=rows,
                         full_consts=[lc, lf, p_a, p_dt, p_fb], row_ins=[(proj, LANES, COL_SMALL)],
                         row_outs=[(LANES, F32), (LANES, F32)])

    def expand_fwd(col, b, g, gt, cm):
        return (_dot32(gt, b, _CONTRACT["nn"]), _dot32(cm, g, _CONTRACT["nn"]))

    betax, gcx = _tiles(expand_fwd, name="expand", rows=rows, tm=tm, full_consts=[xb, xg],
                        row_ins=[(gates, LANES, 0), (cums, LANES, 0)],
                        row_outs=[(WIDTH, F32)] * 2)
    grow = _lanes_to_rowform(cums[:, 8:16], rows)
    frow = cums[:, 16:24].T.reshape(PAIRS, 2, rows)

    (qkv,) = _tiles(_gdn_prep_fwd, name="gdn_prep", rows=rows, tm=rows, ncol=3 * PAIRS,
                    col_consts=[(conv_w, CONV_K, LANES, 0)], row_ins=[(proj, LANES, 0)],
                    row_outs=[(LANES, F32)])
    o_gdn, ssave, tsave = _gdn_forward(qkv, betax, gcx, grow, rows)

    w_qk = jnp.concatenate([jnp.tile(q_norm_w.reshape(1, -1), (1, HEADS)),
                            jnp.tile(k_norm_w.reshape(1, -1), (1, HEADS))], axis=1)
    fox_off = 2048 // LANES
    (fqk,) = _tiles(lambda col, w, xx: (_head_rms(w, xx),), name="fox_prep", rows=rows, tm=rows, ncol=2 * PAIRS,
                    col_consts=[(w_qk, 1, LANES, 0)], row_ins=[(proj, LANES, fox_off)],
                    row_outs=[(LANES, F32)])
    ao, lse = _attention_forward(fqk, proj, frow, rows)

    w_on = jnp.tile(out_norm_w.reshape(1, -1), (1, 2))
    z_off, fg_off = 1536 // LANES, 3584 // LANES
    mix_g_fn = lambda w, o, z: _head_rms(w, o) * _silu(z)
    mix_f_fn = lambda a, g: a * jax.nn.sigmoid(g)
    (mix_g,) = _tiles(lambda col, w, o, z: (mix_g_fn(w, o, z),), name="mix_gdn", rows=rows, tm=rows, ncol=PAIRS,
                      full_consts=[w_on], row_ins=[(o_gdn, LANES, 0), (proj, LANES, z_off)],
                      row_outs=[(LANES, BF16)])
    (mix_f,) = _tiles(lambda col, a, g: (mix_f_fn(a, g),), name="mix_fox", rows=rows, tm=rows, ncol=PAIRS,
                      row_ins=[(ao, LANES, 0), (proj, LANES, fg_off)], row_outs=[(LANES, BF16)])
    mix = jnp.concatenate([mix_g, mix_f], axis=1)
    w_out, w_gate, w_up, w_down = late_weights(mix)
    t_rows, t_half = min(1024, rows), min(512, rows)
    n_rt = rows // t_rows
    row_blk = pl.BlockSpec((t_rows, D_MODEL), lambda i, n: (i, 0))
    half_blk = pl.BlockSpec((t_half, D_MODEL), lambda i, n: (i, 0))
    vec_blk = pl.BlockSpec((1, D_MODEL), lambda i, n: (0, 0))
    wide = lambda dt: jax.ShapeDtypeStruct((rows, D_MODEL), dt)
    x1, h2 = _mm_blocks(mix, w_out, name="out_proj_norm2", grid=(n_rt, 1), dims="nn",
                        a_spec=row_blk, b_spec=pl.BlockSpec((D_MODEL, D_MODEL), lambda i, n: (0, 0)),
                        o_spec=[row_blk, row_blk], out_shape=[wide(F32), wide(BF16)], add=x, add_spec=row_blk,
                        extra=[(norm2_w, vec_blk)], epilogue=lambda r, w: (r, _rms(r, w)))
    st_act = jax.ShapeDtypeStruct((N_CHIPS, rows, FF_SHARD), BF16)
    st_rows = pl.BlockSpec((None, rows, FF_SHARD), lambda i, j: (j, i, 0))

    def ffn_in(w_st, name):
        return _mm_blocks(h2, w_st, name=name, grid=(1, N_CHIPS), dims="nt",
                          a_spec=pl.BlockSpec((rows, D_MODEL), lambda i, j: (i, 0)),
                          b_spec=pl.BlockSpec((None, FF_SHARD, D_MODEL), lambda i, j: (j, 0, 0)),
                          o_spec=st_rows, out_shape=st_act)

    gate = ffn_in(w_gate, "ffn_gate")
    act_fn = lambda g, u: _silu(g) * u
    st_tile = pl.BlockSpec((None, t_rows, FF_SHARD), lambda i, j: (j, i, 0))
    up, act = _mm_blocks(h2, w_up, name="ffn_up_act", grid=(n_rt, N_CHIPS), dims="nt",
                         a_spec=pl.BlockSpec((t_rows, D_MODEL), lambda i, j: (i, 0)),
                         b_spec=pl.BlockSpec((None, FF_SHARD, D_MODEL), lambda i, j: (j, 0, 0)),
                         o_spec=[st_tile, st_tile], out_shape=[st_act, st_act], extra=[(gate, st_tile)],
                         epilogue=lambda u, g: (u, act_fn(g.astype(F32), u)))

    def final_fn(xx, tgt, w):
        y, vjp = jax.vjp(_rms, xx, w)
        err = y - tgt
        loss = 0.5 * jnp.sum(err * err) / D_MODEL
        dx, dw = vjp(err / D_MODEL)
        return dx, dx, jnp.full((1, LANES), loss, F32), dw

    dx2, dx2_b, loss, d_final_w = _mm_blocks(
        act, w_down, name="ffn_down_loss", grid=(rows // t_half, 1), dims="nn", n_sum=N_CHIPS,
        a_spec=pl.BlockSpec((N_CHIPS, t_half, FF_SHARD), lambda i, n: (0, i, 0)),
        b_spec=pl.BlockSpec((N_CHIPS, FF_SHARD, D_MODEL), lambda i, n: (0, 0, 0)),
        o_spec=[half_blk, half_blk, pl.BlockSpec((1, LANES), lambda i, n: (0, 0)), vec_blk],
        out_shape=[wide(F32), wide(BF16), jax.ShapeDtypeStruct((1, LANES), F32),
                   jax.ShapeDtypeStruct((1, D_MODEL), F32)],
        add=x1, add_spec=half_blk, extra=[(target, half_blk), (final_w, vec_blk)], epilogue=final_fn, n_acc=2)

    def act_bwd(d, g, u):
        _, vjp = jax.vjp(act_fn, g.astype(F32), u.astype(F32))
        return vjp(d)

    dgate, dup = _mm_blocks(dx2_b, w_down, name="d_act_gate_up", grid=(n_rt, N_CHIPS), dims="nt",
                            a_spec=pl.BlockSpec((t_rows, D_MODEL), lambda i, j: (i, 0)),
                            b_spec=pl.BlockSpec((None, FF_SHARD, D_MODEL), lambda i, j: (j, 0, 0)),
                            o_spec=[st_tile, st_tile], out_shape=[st_act, st_act],
                            extra=[(gate, st_tile), (up, st_tile)], epilogue=act_bwd)

    def g_ffn(d_st, other, name):
        return _mm_blocks(d_st, other, name=name, grid=(N_CHIPS, 1), dims="tn",
                          a_spec=pl.BlockSpec((None, rows, FF_SHARD), lambda j, n: (j, 0, 0)),
                          b_spec=pl.BlockSpec((rows, D_MODEL), lambda j, n: (0, 0)),
                          o_spec=pl.BlockSpec((None, FF_SHARD, D_MODEL), lambda j, n: (j, 0, 0)),
                          out_shape=jax.ShapeDtypeStruct((N_CHIPS, FF_SHARD, D_MODEL), BF16))

    g_down = g_ffn(act, dx2_b, "g_down")

    def norm_bwd(dh, xx, dres, w):
        _, vjp = jax.vjp(_rms, xx, w)
        dx, dw = vjp(dh)
        return dx + dres, dx + dres, dw

    def d_h2(d_st, w_st, name, add, **fused):
        return _mm_blocks(d_st, w_st, name=name, grid=(rows // t_half, 1), dims="nn", n_sum=N_CHIPS,
                          a_spec=pl.BlockSpec((N_CHIPS, t_half, FF_SHARD), lambda i, n: (0, i, 0)),
                          b_spec=pl.BlockSpec((N_CHIPS, FF_SHARD, D_MODEL), lambda i, n: (0, 0, 0)),
                          add=add, add_spec=half_blk, **fused)

    dh2_gate = d_h2(dgate, w_gate, "d_h2_gate", None, o_spec=half_blk, out_shape=wide(F32))
    dx1, dx1_b, d_norm2_w = d_h2(
        dup, w_up, "d_h2_up_norm2_bwd", dh2_gate, o_spec=[half_blk, half_blk, vec_blk],
        out_shape=[wide(F32), wide(BF16), jax.ShapeDtypeStruct((1, D_MODEL), F32)],
        extra=[(x1, half_blk), (dx2, half_blk), (norm2_w, vec_blk)], epilogue=norm_bwd, n_acc=1)
    g_gate, g_up = g_ffn(dgate, h2, "g_gate"), g_ffn(dup, h2, "g_up")
    dmix = _mm(dx1_b, w_out, dims="nt", name="d_mix", tn=D_MODEL, tk=1024)
    g_out = _mm(mix, dx1_b, dims="tn", name="g_out", tn=D_MODEL, tk=rows, out_dtype=BF16)
    w_on = w_on + early_grads_ready(g_out, g_gate, g_up, g_down)

    def mix_g_bwd(col, w, o, z, d):
        _, vjp = jax.vjp(mix_g_fn, w, o, z)
        dw, do_, dz = vjp(d)
        return do_, dz, dw

    do_gdn, dz, d_on = _tiles(mix_g_bwd, name="mix_gdn_bwd", rows=rows, tm=rows, ncol=PAIRS, full_consts=[w_on],
                              row_ins=[(o_gdn, LANES, 0), (proj, LANES, z_off), (dmix, LANES, 0)],
                              row_outs=[(LANES, F32), (LANES, BF16)], acc_outs=[(1, LANES)])

    def mix_f_bwd(col, a, g, d):
        _, vjp = jax.vjp(mix_f_fn, a, g)
        return vjp(d)

    dao, dfgate = _tiles(mix_f_bwd, name="mix_fox_bwd", rows=rows, tm=rows, ncol=PAIRS,
                         row_ins=[(ao, LANES, 0), (proj, LANES, fg_off), (dmix, LANES, PAIRS)],
                         row_outs=[(LANES, F32), (LANES, BF16)])

    dfq, dfk, dfv, dfrow = _attention_backward(fqk, proj, frow + early_grads_continue(dao), ao, lse, dao, rows)

    def fox_prep_bwd(col, w, xx, d):
        _, vjp = jax.vjp(_head_rms, w, xx)
        dw, dx = vjp(d)
        return dx, dw

    dfqk, d_wqk = [], []
    for part, d_n in enumerate((dfq, dfk)):
        dx_p, dw_p = _tiles(fox_prep_bwd, name="fox_prep_bwd_" + "qk"[part], rows=rows, tm=rows, ncol=PAIRS,
                            col_consts=[(w_qk, 1, LANES, part * PAIRS)],
                            row_ins=[(proj, LANES, fox_off + part * PAIRS), (d_n, LANES, 0)],
                            row_outs=[(LANES, BF16)], acc_outs=[(1, LANES)])
        dfqk.append(dx_p)
        d_wqk.append(dw_p)

    dq, dk, dv, dbetax, dgcx, dgrow = _gdn_backward(qkv, betax, gcx, grow, ssave, tsave, do_gdn, rows)
    dqkv, d_conv = [], []
    for part, d_n in enumerate((dq, dk, dv)):
        prep_bwd = lambda col, cw, xx, dy, is_qk=(part < 2): _gdn_prep_bwd(is_qk, cw, xx, dy)
        dx_p, dw_p = _tiles(prep_bwd, name="gdn_prep_bwd_" + "qkv"[part], rows=rows, tm=rows, ncol=PAIRS,
                            col_consts=[(conv_w, CONV_K, LANES, part * PAIRS)],
                            row_ins=[(proj, LANES, part * PAIRS), (d_n, LANES, 0)],
                            row_outs=[(LANES, BF16)], acc_outs=[(CONV_K, LANES)])
        dqkv.append(dx_p)
        d_conv.append(dw_p)
    d_conv = jnp.concatenate(d_conv, axis=1)

    def expand_bwd(col, b, g, db, dg):
        return (_dot32(db, b, _CONTRACT["nt"]), _dot32(dg, g, _CONTRACT["nt"]))

    dgates_b, dcums_g = _tiles(expand_bwd, name="expand_bwd", rows=rows, tm=tm, full_consts=[xb, xg],
                               row_ins=[(dbetax, WIDTH, 0), (dgcx, WIDTH, 0)],
                               row_outs=[(LANES, F32), (LANES, F32)])
    dcums_row = jnp.concatenate([jnp.zeros((rows, 8), F32), _rowform_to_lanes(dgrow, rows),
                                 dfrow.reshape(HEADS, rows).T, jnp.zeros((rows, LANES - 24), F32)], axis=1)

    def gates_bwd(col, lcv, lfv, a, dt, fb, pre, dgb, dcg, dcr):
        lane = _lane_ids(pre.shape)
        dgates = jnp.where(lane < 8, dgb, _cums_bwd(lcv, lfv, dcg + dcr))
        _, vjp = jax.vjp(_gates_elem, a, dt, fb, pre)
        da, ddt, dfb, dpre = vjp(dgates)
        return dpre, da, ddt, dfb

    dpre, d_a, d_dt, d_fb = _tiles(gates_bwd, name="gates_bwd", rows=rows, tm=rows,
                                   full_consts=[lc, lf, p_a, p_dt, p_fb],
                                   row_ins=[(proj, LANES, COL_SMALL), (dgates_b, LANES, 0), (dcums_g, LANES, 0),
                                            (dcums_row, LANES, 0)],
                                   row_outs=[(LANES, BF16)], acc_outs=[(1, LANES)] * 3)

    dproj = jnp.concatenate(dqkv + [dz] + dfqk + [dfv, dfgate, dpre], axis=1)
    grad_x, d_norm1_w = _mm_blocks(
        dproj, w_cat, name="d_h1_norm1_bwd", grid=(rows // t_half, 1), dims="nn",
        a_spec=pl.BlockSpec((t_half, D_CAT), lambda i, n: (i, 0)),
        b_spec=pl.BlockSpec((D_CAT, D_MODEL), lambda i, n: (0, 0)),
        o_spec=[half_blk, vec_blk], out_shape=[wide(F32), jax.ShapeDtypeStruct((1, D_MODEL), F32)],
        extra=[(x, half_blk), (dx1, half_blk), (norm1_w, vec_blk)],
        epilogue=lambda dh, xx, dres, w: norm_bwd(dh, xx, dres, w)[1:], n_acc=1)
    g_cat = _mm(dproj, h1, dims="tn", name="g_in", tm=1408, tn=D_MODEL, tk=rows)

    fold = lambda v: v.reshape(-1, HEAD_DIM).sum(axis=0)
    small = dict(
        loss=loss[0, 0],
        norm1_w=d_norm1_w, conv_w=d_conv, a_log=d_a[0, 8:16], dt_bias=d_dt[0, 8:16],
        out_norm_w=fold(d_on), f_bias=d_fb[0, 16:24], q_norm_w=fold(d_wqk[0]),
        k_norm_w=fold(d_wqk[1]), norm2_w=d_norm2_w, final_w=d_final_w)
    return grad_x, g_cat, g_out, g_gate, g_up, g_down, small


HBM_SPEC = pl.BlockSpec(memory_space=pltpu.HBM)


def _place():
    x, y, c = lax.axis_index("x"), lax.axis_index("y"), lax.axis_index("c")
    chips = [(1 - x, y), (x, 1 - y), (1 - x, 1 - y)]
    return x, y, c, 2 * x + y, (x, y, 1 - c), chips, [2 * cx + cy for cx, cy in chips]


def _remote(src, dst, send_sem, recv_sem, to):
    return pltpu.make_async_remote_copy(src_ref=src, dst_ref=dst, send_sem=send_sem, recv_sem=recv_sem,
                                        device_id=to, device_id_type=MESH)


SEM_SPEC =pl.BlockSpec(memory_space=pltpu.SEMAPHORE)
ANY_SPEC = pl.BlockSpec(memory_space=pl.ANY)
DATAFLOW = pltpu.SideEffectType.DATAFLOW_SIDE_EFFECTING


def _gather_plan(srcs, lands):
    x, y, c, own, sib, chips, chip_idx = _place()
    plan = []
    for src, land in zip(srcs, lands):
        for j, chip in enumerate(chips):
            plan.append((src, land.at[own], (*chip, c), land.at[chip_idx[j]]))
        plan.append((src, land.at[own], sib, land.at[own]))
    return plan


def _exchange_plan(srcs, lands):
    x, y, c, own, sib, chips, chip_idx = _place()
    plan = []
    for src, land in zip(srcs, lands):
        for j, chip in enumerate(chips):
            plan.append((src.at[chip_idx[j]], land.at[j], (*chip, c), land.at[j]))
    return plan


def _swap_plan(srcs, lands):
    x, y, c, own, sib, chips, chip_idx = _place()
    plan = []
    for src, land in zip(srcs, lands):
        h = src.shape[2] // 2
        plan.append((src.at[:, :, pl.ds(pl.multiple_of((1 - c) * h, LANES), h)], land, sib, land))
    return plan


def _in_proj_plan(srcs, lands):
    x, y, c, own, sib, chips, chip_idx = _place()
    (w, conv), (w_land, conv_land) = srcs, lands
    hw = w.shape[1] // 2
    half = lambda ref: ref.at[:, pl.ds(pl.multiple_of(c * hw, LANES), hw)]
    plan = []
    for j, chip in enumerate(chips):
        plan.append((half(w), half(w_land.at[own]), (*chip, c), half(w_land.at[chip_idx[j]])))
        plan.append((conv, conv_land.at[own], (*chip, c), conv_land.at[chip_idx[j]]))
    plan.append((w, w_land.at[own], sib, w_land.at[own]))
    plan.append((conv, conv_land.at[own], sib, conv_land.at[own]))
    return plan


def _forward_halves(landed):
    hw = landed.shape[2] // 2

    def body(in_ref, out_ref, send_sems, recv_sems):
        x, y, c, own, sib, chips, chip_idx = _place()
        half = lambda ref, hc: ref.at[:, pl.ds(pl.multiple_of(hc * hw, LANES), hw)]
        sent = [_remote(half(out_ref.at[chip_idx[j]], c), half(out_ref.at[chip_idx[j]], c),
                        send_sems.at[j], recv_sems.at[j], sib) for j in range(3)]
        for cp in sent:
            cp.start()
        for j in range(3):
            other = half(out_ref.at[chip_idx[j]], 1 - c)
            _remote(other, other, send_sems.at[j], recv_sems.at[j], sib).wait_recv()
        for cp in sent:
            cp.wait_send()

    return pl.pallas_call(
        body, name="gather_in_forward", out_shape=jax.ShapeDtypeStruct(landed.shape, landed.dtype),
        in_specs=[HBM_SPEC], out_specs=HBM_SPEC, input_output_aliases={0: 0},
        scratch_shapes=[pltpu.SemaphoreType.DMA((3,)), pltpu.SemaphoreType.DMA((3,))],
    )(landed)


def _split_start(name, plan_fn, srcs, land_shapes, n_copies, after=None):
    n = len(srcs)
    extra = [] if after is None else [after]

    def body(*refs):
        src_refs, land_refs = refs[:n], refs[n:2 * n]
        send_sems, recv_sems = refs[2 * n + len(extra)], refs[2 * n + len(extra) + 1]
        token = refs[-1]
        for k, (src, dst, to, _) in enumerate(plan_fn(src_refs, land_refs)):
            _remote(src, dst, send_sems.at[k], recv_sems.at[k], to).start()
        token[...] = jnp.zeros_like(token)

    lands = [pltpu.with_memory_space_constraint(lax.empty(s.shape, s.dtype), pltpu.HBM) for s in land_shapes]
    srcs = [pltpu.with_memory_space_constraint(s, pltpu.HBM) for s in srcs]
    out_shape = ([pltpu.SemaphoreType.DMA((n_copies,)), pltpu.SemaphoreType.DMA((n_copies,))]
                 + [pltpu.HBM(s.shape, s.dtype) for s in srcs] + [pltpu.HBM(s.shape, s.dtype) for s in land_shapes]
                 + [jax.ShapeDtypeStruct((8, LANES), F32)])
    res = pl.pallas_call(
        body, name=name, out_shape=out_shape,
        in_specs=[HBM_SPEC] * (2 * n) + [ANY_SPEC] * len(extra),
        out_specs=[SEM_SPEC, SEM_SPEC] + [HBM_SPEC] * (2 * n) + [pl.BlockSpec(memory_space=pltpu.VMEM)],
        input_output_aliases={i: 2 + i for i in range(2 * n)},
        compiler_params=pltpu.CompilerParams(has_side_effects=DATAFLOW),
    )(*srcs, *lands, *extra)
    return dict(sems=res[:2], srcs=res[2:2 + n], lands=res[2 + n:2 + 2 * n], token=res[-1], n=n)


def _split_wait(name, plan_fn, started, after):
    n = started["n"]

    def body(*refs):
        src_refs, land_refs = refs[:n], refs[n:2 * n]
        send_sems, recv_sems = refs[2 * n], refs[2 * n + 1]
        for k, (src, _, to, landed) in enumerate(plan_fn(src_refs, land_refs)):
            copy = _remote(src, landed, send_sems.at[k], recv_sems.at[k], to)
            copy.wait_send()
            copy.wait_recv()

    srcs, lands = started["srcs"], started["lands"]
    after = list(after) if isinstance(after, (list, tuple)) else [after]
    res = pl.pallas_call(
        body, name=name,
        out_shape=[pltpu.HBM(s.shape, s.dtype) for s in srcs] + [pltpu.HBM(s.shape, s.dtype) for s in lands],
        in_specs=[HBM_SPEC] * (2 * n) + [SEM_SPEC, SEM_SPEC] + [ANY_SPEC] * len(after),
        out_specs=[HBM_SPEC] * (2 * n),
        input_output_aliases={i: i for i in range(2 * n)},
        compiler_params=pltpu.CompilerParams(has_side_effects=DATAFLOW),
    )(*srcs, *lands, *started["sems"], *after)
    started["srcs_after"] = res[:n]
    return res[n:]


def _add_halves(stacks, landed, place, name):
    n = len(stacks)

    def body(place_ref, *refs):
        for a_ref, b_ref, o_ref, own_ref in zip(refs[:n], refs[n:2 * n], refs[2 * n::2], refs[2 * n + 1::2]):
            part = (a_ref[...].astype(F32) + b_ref[...].astype(F32)).astype(o_ref.dtype)
            o_ref[...] = part

            @pl.when(pl.program_id(0) == place_ref[1])
            def _(own_ref=own_ref, part=part):
                own_ref[...] = part[0]

    shapes = [l.shape[1:] for l in landed]
    slab = lambda s: pl.BlockSpec((1,) + s, lambda j, p: (j, 0, 0))
    out_shape, out_specs = [], []
    for l, s in zip(landed, shapes):
        out_shape += [jax.ShapeDtypeStruct(l.shape, BF16), jax.ShapeDtypeStruct(s, BF16)]
        out_specs += [slab(s), pl.BlockSpec(s, lambda j, p: (0, 0))]
    res = pl.pallas_call(
        body, name=name, out_shape=out_shape,
        grid_spec=pltpu.PrefetchScalarGridSpec(
            num_scalar_prefetch=1, grid=(N_CHIPS,),
            in_specs=[pl.BlockSpec((1,) + s, lambda j, p: (j, 0, p[0])) for s in shapes] + [slab(s) for s in shapes],
            out_specs=out_specs),
        compiler_params=_params(("arbitrary",)),
    )(place, *stacks, *landed)
    return [(res[2 * i], res[2 * i + 1]) for i in range(n)]


def _sum_many(own_parts, landed, name):
    n = len(own_parts)

    def body(*refs):
        for own_ref, a_ref, o_ref in zip(refs[:n], refs[n:2 * n], refs[2 * n:]):
            acc = own_ref[...].astype(F32)
            for s in range(3):
                acc = acc + a_ref[s].astype(F32)
            o_ref[...] = acc

    whole = lambda a: pl.BlockSpec(a.shape, lambda i, nd=a.ndim: (0,) * nd)
    return pl.pallas_call(
        body, name=name, grid=(1,), out_shape=[jax.ShapeDtypeStruct(o.shape, F32) for o in own_parts],
        in_specs=[whole(a) for a in own_parts] + [whole(a) for a in landed],
        out_specs=[whole(a) for a in own_parts], compiler_params=_params(("arbitrary",)),
    )(*own_parts, *landed)


def _sum_partials(own_part, landed, name, untiled_rows=False):
    _, h, cols = landed.shape
    tc = LANES if untiled_rows else cols

    def body(own_ref, a_ref, o_ref):
        acc = own_ref[...].astype(F32)
        for s in range(3):
            acc = acc + a_ref[s].astype(F32)
        if untiled_rows:
            o_ref[:, 0, :] = acc
        else:
            o_ref[...] = acc

    if untiled_rows:
        out_shape, out_spec = jax.ShapeDtypeStruct((h, 1, cols), F32), pl.BlockSpec((h, 1, tc), lambda i: (0, 0, i))
    else:
        out_shape, out_spec = jax.ShapeDtypeStruct((h, cols), F32), pl.BlockSpec((h, tc), lambda i: (0, i))
    return pl.pallas_call(
        body, name=name, out_shape=out_shape, grid=(cols // tc,),
        in_specs=[pl.BlockSpec((h, tc), lambda i: (0, i)), pl.BlockSpec((3, h, tc), lambda i: (0, 0, i))],
        out_specs=out_spec, compiler_params=_params(("arbitrary",)),
    )(own_part, landed)


def _share_halves(halves, name):
    n = len(halves)

    def body(*refs):
        ins, outs = refs[:n], refs[n:2 * n]
        send_sems, recv_sems = refs[2 * n:]
        x, y, c, own, sib, chips, chip_idx = _place()
        cps = [_remote(ins[i], outs[i], send_sems.at[i], recv_sems.at[i], sib) for i in range(n)]
        for cp in cps:
            cp.start()
        for cp in cps:
            cp.wait()

    return pl.pallas_call(
        body, name=name,
        out_shape=[jax.ShapeDtypeStruct(p.shape, p.dtype) for p in halves],
        in_specs=[HBM_SPEC] * n, out_specs=[HBM_SPEC] * n,
        scratch_shapes=[pltpu.SemaphoreType.DMA((n,)), pltpu.SemaphoreType.DMA((n,))],
    )(*halves)


def _allreduce_small(packed):
    rows = packed.shape[0]
    n_dev = 8

    def body(in_ref, out_ref, gath, send_sems, recv_sems):
        x, y, c = lax.axis_index("x"), lax.axis_index("y"), lax.axis_index("c")
        me = 4 * x + 2 * y + c
        gath[me] = in_ref[...]
        cps = []
        for k in range(1, n_dev):
            fx, fy, fc = (k >> 2) & 1, (k >> 1) & 1, k & 1
            to = (x ^ fx, y ^ fy, c ^ fc)
            cps.append(_remote(in_ref, gath.at[me], send_sems.at[k - 1], recv_sems.at[k - 1], to))
        for cp in cps:
            cp.start()
        for k in range(1, n_dev):
            fx, fy, fc = (k >> 2) & 1, (k >> 1) & 1, k & 1
            src = 4 * (x ^ fx) + 2 * (y ^ fy) + (c ^ fc)
            slot = gath.at[src]
            _remote(slot, slot, send_sems.at[k - 1], recv_sems.at[k - 1], (x, y, c)).wait_recv()
        for cp in cps:
            cp.wait_send()
        acc = gath[0]
        for d in range(1, n_dev):
            acc = acc + gath[d]
        out_ref[...] = acc

    vm = pl.BlockSpec(memory_space=pltpu.VMEM)
    return pl.pallas_call(
        body, name="allreduce_small", out_shape=jax.ShapeDtypeStruct(packed.shape, F32),
        in_specs=[vm], out_specs=vm,
        scratch_shapes=[pltpu.VMEM((n_dev, rows, LANES), F32),
                        pltpu.SemaphoreType.DMA((n_dev - 1,)), pltpu.SemaphoreType.DMA((n_dev - 1,))],
    )(packed)


def _adam(col, w, g, m, v):
    m2 = ADAM_B1 * m + (1.0 - ADAM_B1) * g
    v2 = ADAM_B2 * v + (1.0 - ADAM_B2) * (g * g)
    m_hat = m2 / (1.0 - ADAM_B1 ** ADAM_STEP)
    v_hat = v2 / (1.0 - ADAM_B2 ** ADAM_STEP)
    delta = -ADAM_LR * (m_hat / (jnp.sqrt(v_hat) + ADAM_EPS) + ADAM_WD * w)
    return delta, m2, v2


def _adam_call(w, g, m, v, name):
    rows, cols = w.shape
    tm = rows
    for cand in (256, 352, 176, 128, 64, 48, 16, 8):
        if rows % cand == 0:
            tm = cand
            break
    return _tiles(_adam, name=name, rows=rows, tm=tm,
                  row_ins=[(w, cols, 0), (g, cols, 0), (m, cols, 0), (v, cols, 0)],
                  row_outs=[(cols, F32)] * 3)


def _adam_big(w, g_mine, g_other, m, v, place, name):
    rows, cols = w.shape
    tc = cols // 2
    nt = cols // 2 // tc

    def body(place_ref, w_ref, gm_ref, go_ref, m_ref, v_ref, g_out, d_out, m_out, v_out):
        g = jnp.where(pl.program_id(0) == place_ref[0], gm_ref[...], go_ref[...])
        d, m2, v2 = _adam(None, w_ref[...], g, m_ref[...], v_ref[...])
        g_out[...] = g
        d_out[...] = d
        m_out[...] = m2
        v_out[...] = v2

    full = pl.BlockSpec((rows, tc), lambda hh, i, p: (0, hh * nt + i))
    half = pl.BlockSpec((rows, tc), lambda hh, i, p: (0, i))
    return pl.pallas_call(
        body, name=name, out_shape=[jax.ShapeDtypeStruct(w.shape, F32)] * 4,
        grid_spec=pltpu.PrefetchScalarGridSpec(
            num_scalar_prefetch=1, grid=(2, nt),
            in_specs=[full, half, half, full, full], out_specs=[full] * 4),
        compiler_params=_params(("arbitrary", "arbitrary")),
    )(place, w, g_mine, g_other, m, v)


def _adam_untiled_rows(w, g_mine, g_other, m, v, place, name):
    rows, _, cols = w.shape
    tc = 256
    nt = cols // 2 // tc
    rb = next(r for r in (206, 128, 103, rows) if rows % r == 0)

    def body(place_ref, w_ref, gm_ref, go_ref, m_ref, v_ref, g_out, d_out, m_out, v_out):
        g = jnp.where(pl.program_id(0) == place_ref[0], gm_ref[...], go_ref[...])
        d, m2, v2 = _adam(None, w_ref[...], g, m_ref[...], v_ref[...])
        g_out[...] = g
        d_out[...] = d
        m_out[...] = m2
        v_out[...] = v2

    full = pl.BlockSpec((rb, 1, tc), lambda hh, i, r, p: (r, 0, hh * nt + i))
    half = pl.BlockSpec((rb, 1, tc), lambda hh, i, r, p: (r, 0, i))
    return pl.pallas_call(
        body, name=name, out_shape=[jax.ShapeDtypeStruct(w.shape, F32)] * 4,
        grid_spec=pltpu.PrefetchScalarGridSpec(
            num_scalar_prefetch=1, grid=(2, nt, rows // rb),
            in_specs=[full, half, half, full, full], out_specs=[full] * 4),
        compiler_params=_params(("arbitrary", "arbitrary", "arbitrary")),
    )(place, w, g_mine, g_other, m, v)


def _pack(arrays, zero=None):
    flat = []
    for a in arrays:
        a = a.reshape(-1).astype(F32)
        if zero is not None:
            a = a + zero
        flat.append(jnp.pad(a, (0, (-a.size) % LANES)))
    out = jnp.concatenate(flat)
    out = jnp.pad(out, (0, (-out.size) % (8 * LANES)))
    return out.reshape(-1, LANES)


def _unpack(packed, shapes):
    flat = packed.reshape(-1)
    out, off = [], 0
    for s in shapes:
        size = int(np.prod(s))
        out.append(flat[off:off + size].reshape(s))
        off += size + (-size) % LANES
    return out


def kernel(x, norm1_w, w_in, gdn_conv_w, gdn_A_log, gdn_dt_bias, gdn_out_norm_w, fox_f_bias, fox_q_norm_w, fox_k_norm_w, w_out, norm2_w, w_ffn_gate, w_ffn_up, w_ffn_down, final_norm_w, loss_target, m_norm1_w, m_w_in, m_gdn_conv_w, m_gdn_A_log, m_gdn_dt_bias, m_gdn_out_norm_w, m_fox_f_bias, m_fox_q_norm_w, m_fox_k_norm_w, m_w_out, m_norm2_w, m_w_ffn_gate, m_w_ffn_up, m_w_ffn_down, m_final_norm_w, v_norm1_w, v_w_in, v_gdn_conv_w, v_gdn_A_log, v_gdn_dt_bias, v_gdn_out_norm_w, v_fox_f_bias, v_fox_q_norm_w, v_fox_k_norm_w, v_w_out, v_norm2_w, v_w_ffn_gate, v_w_ffn_up, v_w_ffn_down, v_final_norm_w):
    cx, cy, cc = lax.axis_index("x"), lax.axis_index("y"), lax.axis_index("c")
    own = 2 * cx + cy
    place = jnp.stack([cc, own]).astype(jnp.int32)

    names = ["w_in", "w_out", "w_gate", "w_up", "w_down"]
    is_t = [True, False, True, True, False]
    to_t = lambda a, t: a[0].T if t else a[0]
    from_t = lambda a, t: (a.T if t else a)[None]
    big_w = [to_t(a, t) for a, t in zip([w_in, w_out, w_ffn_gate, w_ffn_up, w_ffn_down], is_t)]
    big_m = [to_t(a, t) for a, t in zip([m_w_in, m_w_out, m_w_ffn_gate, m_w_ffn_up, m_w_ffn_down], is_t)]
    big_v = [to_t(a, t) for a, t in zip([v_w_in, v_w_out, v_w_ffn_gate, v_w_ffn_up, v_w_ffn_down], is_t)]
    shards = [big_w[0].astype(BF16)]
    small_w = [norm1_w, gdn_conv_w, gdn_A_log, gdn_dt_bias, gdn_out_norm_w, fox_f_bias, fox_q_norm_w,
               fox_k_norm_w, norm2_w, final_norm_w]
    small_m = [m_norm1_w, m_gdn_conv_w, m_gdn_A_log, m_gdn_dt_bias, m_gdn_out_norm_w, m_fox_f_bias,
               m_fox_q_norm_w, m_fox_k_norm_w, m_norm2_w, m_final_norm_w]
    small_v = [v_norm1_w, v_gdn_conv_w, v_gdn_A_log, v_gdn_dt_bias, v_gdn_out_norm_w, v_fox_f_bias,
               v_fox_q_norm_w, v_fox_k_norm_w, v_norm2_w, v_final_norm_w]
    first = _split_start("gather_in_start", _in_proj_plan, [shards[0], gdn_conv_w[0]],
                         [jax.ShapeDtypeStruct((N_CHIPS,) + shards[0].shape, BF16),
                          jax.ShapeDtypeStruct((N_CHIPS, CONV_K, 3 * WIDTH // N_CHIPS), F32)],
                         n_copies=8)
    small_packed = [_pack(p, first["token"][0, 0]) for p in (small_w, small_m, small_v)]
    shards += [(w + first["token"][0, 0]).astype(BF16) for w in big_w[1:]]
    rest = {}

    def first_weights(after):
        w_in_g, conv_g = _split_wait("gather_in_wait", _in_proj_plan, first, [after] + small_packed)
        w_in_g = _forward_halves(w_in_g)
        rest.update(_split_start("gather_rest_start", _gather_plan, shards[1:],
                                 [jax.ShapeDtypeStruct((N_CHIPS,) + s.shape, BF16) for s in shards[1:]],
                                 n_copies=4 * len(shards[1:]), after=w_in_g))
        w_cat = _cat_weights(w_in_g.reshape(D_IN, D_MODEL))
        return w_cat + rest["token"][0, 0].astype(BF16), conv_g.transpose(1, 0, 2).reshape(CONV_K, 3 * WIDTH)

    def late_weights(after):
        w_out_g, w_gate_g, w_up_g, w_down_g = _split_wait("gather_rest_wait", _gather_plan, rest, after)
        return w_out_g.reshape(D_MODEL, D_MODEL), w_gate_g, w_up_g, w_down_g

    def start_reduction(stacks, landed, nms, tag):
        added = _add_halves(stacks, landed, place, "rs_add_" + tag)
        parts = [a[0] for a in added]
        started = _split_start("exchange_" + tag + "_start", _exchange_plan, parts,
                               [jax.ShapeDtypeStruct((3,) + p.shape[1:], p.dtype) for p in parts],
                               n_copies=3 * len(parts))
        return dict(own=[a[1] for a in added], started=started, tag=tag, names=nms)

    def finish_reduction(red, after, updates):
        landed = _split_wait("exchange_" + red["tag"] + "_wait", _exchange_plan, red["started"], after)
        if red["tag"] == "w_in":
            halves = [_sum_partials(red["own"][0], landed[0], "rs_sum_w_in", untiled_rows=True)]
        else:
            halves = _sum_many(red["own"], landed, "rs_sum_" + red["tag"])
        others = _share_halves(halves, "rs_share_" + red["tag"])
        return [upd(gm, go) for upd, gm, go in zip(updates, halves, others)]

    def transport_update(b):
        def upd(gm, go):
            res = _adam_big(big_w[b], gm, go, big_m[b], big_v[b], place, "adam_" + names[b])
            early_done.append(res[1])
            return [from_t(a, is_t[b]) for a in res]
        return upd

    early_done = []

    def w_in_update(gm, go):
        rows3 = lambda a: jnp.transpose(a, (2, 0, 1))
        res = _adam_untiled_rows(rows3(w_in), gm, go, rows3(m_w_in), rows3(v_w_in), place, "adam_w_in")
        return [jnp.transpose(a, (1, 2, 0)) for a in res]

    early = {}

    def early_grads_ready(g_out, g_gate, g_up, g_down):
        stacks = [g_out.reshape(N_CHIPS, D_MODEL // N_CHIPS, D_MODEL), g_gate, g_up, g_down]
        swap = _split_start("swap_early_start", _swap_plan, stacks,
                            [jax.ShapeDtypeStruct(s.shape[:2] + (s.shape[2] // 2,), s.dtype) for s in stacks],
                            n_copies=len(stacks))
        early.update(stacks=stacks, swap=swap)
        return swap["token"][0, 0]

    def early_grads_continue(after):
        landed = _split_wait("swap_early_wait", _swap_plan, early["swap"], after)
        early.update(start_reduction(early["swap"]["srcs_after"], landed, names[1:], "early"))
        return early["started"]["token"][0, 0]

    grad_x, g_cat, _, _, _, _, small = _local_step(
        x[0], loss_target[0], norm1_w + first["token"][0, 0], gdn_A_log[0], gdn_dt_bias[0],
        gdn_out_norm_w[0], fox_f_bias[0], fox_q_norm_w[0], fox_k_norm_w[0], norm2_w, final_norm_w.reshape(1, -1),
        first_weights, late_weights, early_grads_ready, early_grads_continue)

    g_in_stack = _uncat_grad(g_cat).reshape(N_CHIPS, D_IN // N_CHIPS, D_MODEL)
    swap_in = _split_start("swap_w_in_start", _swap_plan, [g_in_stack],
                           [jax.ShapeDtypeStruct((N_CHIPS, D_IN // N_CHIPS, D_MODEL // 2), F32)],
                           n_copies=1)

    order = ["norm1_w", "conv_w", "a_log", "dt_bias", "out_norm_w", "f_bias", "q_norm_w", "k_norm_w",
             "norm2_w", "final_w"]
    red = _allreduce_small(_pack([small[k] for k in order] + [small["loss"]], swap_in["token"][0, 0]))
    red_shapes = [(1, D_MODEL), (CONV_K, 3 * WIDTH), (1, HEADS), (1, HEADS), (1, HEAD_DIM), (1, HEADS),
                  (1, HEAD_DIM), (1, HEAD_DIM), (1, D_MODEL), (D_MODEL,), ()]
    red_list = _unpack(red, red_shapes)
    loss = red_list[-1]
    small_g = dict(zip(order, red_list[:-1]))
    shard_cols = 3 * WIDTH // N_CHIPS
    small_g["conv_w"] = lax.dynamic_slice_in_dim(small_g["conv_w"], own * shard_cols, shard_cols, axis=1)[None]
    small_gl = [small_g[k].reshape(w.shape) for k, w in zip(order, small_w)]
    s_delta, s_m, s_v = _adam_call(small_packed[0], _pack(small_gl), small_packed[1], small_packed[2], "adam_small")
    landed_in = _split_wait("swap_w_in_wait", _swap_plan, swap_in, s_delta)
    late = start_reduction(swap_in["srcs_after"], landed_in, names[:1], "w_in")
    big_upd = finish_reduction(early, late["started"]["token"], [transport_update(b) for b in range(1, 5)])
    big_upd = finish_reduction(late, early_done, [w_in_update]) + big_upd
    shapes = [w.shape for w in small_w]
    s_delta, s_m, s_v = _unpack(s_delta, shapes), _unpack(s_m, shapes), _unpack(s_v, shapes)

    big_pos = {1: 0, 9: 1, 11: 2, 12: 3, 13: 4}
    small_pos = {0: 0, 2: 1, 3: 2, 4: 3, 5: 4, 6: 5, 7: 6, 8: 7, 10: 8, 14: 9}
    grads, deltas, new_m, new_v = [], [], [], []
    for pos in range(15):
        if pos in big_pos:
            b = big_pos[pos]
            g, d, m2, v2 = big_upd[b]
            grads.append(g)
            deltas.append(d)
            new_m.append(m2)
            new_v.append(v2)
        else:
            s = small_pos[pos]
            grads.append(small_gl[s])
            deltas.append(s_delta[s])
            new_m.append(s_m[s])
            new_v.append(s_v[s])
    return (loss, grad_x[None], *grads, *deltas, *new_m, *new_v)
```

```python
import jax
import jax.numpy as jnp
import numpy as np
from jax import lax
from jax.experimental import pallas as pl
from jax.experimental.pallas import tpu as pltpu

F32 = jnp.float32
BF16 = jnp.bfloat16

D_MODEL = 1024
HEADS = 8
HEAD_DIM = 64
PAIRS = HEADS // 2
WIDTH = HEADS * HEAD_DIM
CHUNK = 64
CONV_K = 4
D_FF = 2816
FF_SHARD = D_FF // 4
EPS = 1e-6
SCALE = HEAD_DIM ** -0.5
LANES = 128
N_CHIPS = 4
D_IN = 4120
D_CAT = 4224
COL_SMALL = 4096 // LANES

ADAM_LR = 0.001
ADAM_B1 = 0.9
ADAM_B2 = 0.999
ADAM_EPS = 1e-08
ADAM_WD = 0.01
ADAM_STEP = 10

VMEM_LIMIT = 56 * 1024 * 1024
MESH = pl.DeviceIdType.MESH
HIGHEST = lax.Precision.HIGHEST


def _params(sem):
    return pltpu.CompilerParams(dimension_semantics=sem, vmem_limit_bytes=VMEM_LIMIT)


_CONTRACT = {"nn": ((1,), (0,)), "nt": ((1,), (1,)), "tn": ((0,), (0,))}


def _mm(a, b, *, dims, name, out_dtype=F32, add=None, tm=1024, tn=512, tk=512):
    if dims == "nn":
        (m, k), (k2, n) = a.shape, b.shape
    elif dims == "nt":
        (m, k), (n, k2) = a.shape, b.shape
    else:
        (k, m), (k2, n) = a.shape, b.shape
    assert k == k2, (a.shape, b.shape, dims)
    tm, tn, tk = min(tm, m), min(tn, n), min(tk, k)
    assert m % tm == 0 and n % tn == 0 and k % tk == 0, (m, n, k, tm, tn, tk)
    nk = k // tk
    a_spec = (pl.BlockSpec((tk, tm), lambda i, j, kk: (kk, i)) if dims == "tn"
              else pl.BlockSpec((tm, tk), lambda i, j, kk: (i, kk)))
    b_spec = (pl.BlockSpec((tn, tk), lambda i, j, kk: (j, kk)) if dims == "nt"
              else pl.BlockSpec((tk, tn), lambda i, j, kk: (kk, j)))
    o_spec = pl.BlockSpec((tm, tn), lambda i, j, kk: (i, j))
    contract = (_CONTRACT[dims], ((), ()))
    has_add = add is not None

    def body(*refs):
        a_ref, b_ref = refs[:2]
        add_ref = refs[2] if has_add else None
        o_ref = refs[3] if has_add else refs[2]
        part = lax.dot_general(a_ref[...].astype(BF16), b_ref[...].astype(BF16), contract,
                               preferred_element_type=F32)

        def finish(r):
            if has_add:
                r = r + add_ref[...].astype(F32)
            o_ref[...] = r.astype(out_dtype)

        if nk == 1:
            finish(part)
            return
        acc = refs[-1]
        kk = pl.program_id(2)

        @pl.when(kk == 0)
        def _():
            acc[...] = part

        @pl.when(kk > 0)
        def _():
            acc[...] += part

        @pl.when(kk == nk - 1)
        def _():
            finish(acc[...])

    ins = [a, b] + ([add] if has_add else [])
    in_specs = [a_spec, b_spec] + ([o_spec] if has_add else [])
    return pl.pallas_call(
        body, name=name, grid=(m // tm, n // tn, nk),
        in_specs=in_specs, out_specs=o_spec,
        out_shape=jax.ShapeDtypeStruct((m, n), out_dtype),
        scratch_shapes=[pltpu.VMEM((tm, tn), F32)] if nk > 1 else [],
        compiler_params=_params(("parallel", "parallel", "arbitrary")),
    )(*ins)


def _mm_blocks(a, b, *, name, grid, a_spec, b_spec, o_spec, out_shape, dims, n_sum=0, add=None, add_spec=None,
               epilogue=None, extra=(), n_acc=0):
    contract = (_CONTRACT[dims], ((), ()))
    has_add = add is not None
    n_in = 2 + has_add + len(extra)

    def body(*refs):
        a_ref, b_ref = refs[:2]
        dot = lambda x, y: lax.dot_general(x.astype(BF16), y.astype(BF16), contract, preferred_element_type=F32)
        if n_sum:
            r = dot(a_ref[0], b_ref[0])
            for s in range(1, n_sum):
                r = r + dot(a_ref[s], b_ref[s])
        else:
            r = dot(a_ref[...], b_ref[...])
        if has_add:
            r = r + refs[2][...].astype(F32)
        if epilogue is None:
            refs[-1][...] = r.astype(refs[-1].dtype)
        else:
            outs = epilogue(r, *[e[...] for e in refs[2 + has_add:n_in]])
            out_refs = refs[n_in:]
            n_plain = len(out_refs) - n_acc
            for o_ref, val in zip(out_refs[:n_plain], outs):
                o_ref[...] = val.astype(o_ref.dtype)
            if n_acc:
                @pl.when(pl.program_id(0) == 0)
                def _():
                    for o_ref in out_refs[n_plain:]:
                        o_ref[...] = jnp.zeros_like(o_ref)
                for o_ref, val in zip(out_refs[n_plain:], outs[n_plain:]):
                    o_ref[...] += val

    ins = [a, b] + ([add] if has_add else []) + [e[0] for e in extra]
    in_specs = [a_spec, b_spec] + ([add_spec] if has_add else []) + [e[1] for e in extra]
    sem = ("arbitrary" if n_acc else "parallel",) * len(grid)
    return pl.pallas_call(
        body, name=name, grid=grid, in_specs=in_specs, out_specs=o_spec, out_shape=out_shape,
        compiler_params=_params(sem),
    )(*ins)


def _tiles(fn, *, name, rows, tm, ncol=1, row_ins=(), col_consts=(), full_consts=(),
           row_outs=(), acc_outs=()):
    nt = rows // tm
    assert rows % tm == 0
    n_full, n_col, n_row = len(full_consts), len(col_consts), len(row_ins)
    n_ro, n_acc = len(row_outs), len(acc_outs)

    def body(*refs):
        ins = refs[:n_full + n_col + n_row]
        outs = refs[n_full + n_col + n_row:]
        i = pl.program_id(1)
        res = fn(pl.program_id(0), *[r[...] for r in ins])
        for r, v in zip(outs[:n_ro], res[:n_ro]):
            r[...] = v.astype(r.dtype)
        if n_acc:
            @pl.when(i == 0)
            def _():
                for r in outs[n_ro:]:
                    r[...] = jnp.zeros_like(r)
            for r, v in zip(outs[n_ro:], res[n_ro:]):
                r[...] += v

    in_specs = [pl.BlockSpec(a.shape, lambda j, i, nd=a.ndim: (0,) * nd) for a in full_consts]
    in_specs += [pl.BlockSpec((nr, w), lambda j, i, o=o: (0, o + j)) for (_, nr, w, o) in col_consts]
    in_specs += [pl.BlockSpec((tm, w), lambda j, i, o=o: (i, o + j)) for (_, w, o) in row_ins]
    out_specs = [pl.BlockSpec((tm, w), lambda j, i: (i, j)) for (w, _) in row_outs]
    out_specs += [pl.BlockSpec((nr, w), lambda j, i: (0, j)) for (nr, w) in acc_outs]
    out_shape = [jax.ShapeDtypeStruct((rows, w * ncol), dt) for (w, dt) in row_outs]
    out_shape += [jax.ShapeDtypeStruct((nr, w * ncol), F32) for (nr, w) in acc_outs]
    args = list(full_consts) + [c[0] for c in col_consts] + [r[0] for r in row_ins]
    out = pl.pallas_call(
        body, name=name, grid=(ncol, nt), in_specs=in_specs, out_specs=out_specs, out_shape=out_shape,
        compiler_params=_params(("parallel", "arbitrary")),
    )(*args)
    return out


def _rms(x, w):
    return x * lax.rsqrt(jnp.mean(x * x, axis=-1, keepdims=True) + EPS) * w


def _lane_lo(shape):
    return lax.broadcasted_iota(jnp.int32, shape, len(shape) - 1) < HEAD_DIM


def _pair_sum(x):
    lo = _lane_lo(x.shape)
    s0 = jnp.sum(jnp.where(lo, x, 0.0), axis=-1, keepdims=True)
    s1 = jnp.sum(jnp.where(lo, 0.0, x), axis=-1, keepdims=True)
    return jnp.where(lo, s0, s1)


def _head_col(x, lo, h):
    keep = lo if h == 0 else jnp.logical_not(lo)
    return jnp.max(jnp.where(keep, x, -jnp.inf), axis=-1, keepdims=True)


def _softplus(x):
    return jnp.maximum(x, 0.0) + jnp.log1p(jnp.exp(-jnp.abs(x)))


def _silu(x):
    return x * jax.nn.sigmoid(x)


def _dot(a, b, contract):
    return lax.dot_general(a.astype(BF16), b.astype(BF16), (contract, ((), ())),
                           preferred_element_type=F32)


def _dot32(a, b, contract):
    return lax.dot_general(a, b, (contract, ((), ())), precision=HIGHEST, preferred_element_type=F32)


def _bd(y):
    yy = jnp.concatenate([y, y], axis=0)
    r = lax.broadcasted_iota(jnp.int32, yy.shape, 0) < HEAD_DIM
    c = lax.broadcasted_iota(jnp.int32, yy.shape, 1) < HEAD_DIM
    return jnp.where(r == c, yy, 0.0)


def _pp(x, y):
    return _dot(x, _bd(y), _CONTRACT["nn"])


def _pp_nt(x, y):
    return _dot(x, _bd(y), _CONTRACT["nt"])


def _pp_tn(x, y):
    full = _dot(x, y, _CONTRACT["tn"])
    return jnp.where(_lane_lo((HEAD_DIM, LANES)), full[:HEAD_DIM], full[HEAD_DIM:])


def _gdn_masks():
    row = lax.broadcasted_iota(jnp.int32, (CHUNK, LANES), 0)
    col = lax.broadcasted_iota(jnp.int32, (CHUNK, LANES), 1) % HEAD_DIM
    return row, col


def _interleave(chains):
    live = list(chains)
    while live:
        for g in list(live):
            try:
                next(g)
            except StopIteration:
                live.remove(g)


def _gdn_forward(qkv, betax, gcx, grow, rows):
    nchunk = rows // CHUNK

    def body(q_ref, k_ref, v_ref, bx_ref, gx_ref, gr_ref, o_ref, ss_ref, ts_ref, state):
        n = pl.program_id(0)

        @pl.when(n == 0)
        def _():
            state[...] = jnp.zeros_like(state)

        row, col = _gdn_masks()
        incl, strict = col <= row, col < row

        def chain(p):
            lanes = pl.ds(p * LANES, LANES)
            q, k, v, bx, gx = q_ref[:, lanes], k_ref[:, lanes], v_ref[:, lanes], bx_ref[:, lanes], gx_ref[:, lanes]
            gr = gr_ref[0, p]
            glast = gx_ref[pl.ds(CHUNK - 1, 1), lanes]
            s = state[p]
            dm = jnp.where(incl, jnp.exp(jnp.minimum(gx - gr, 0.0)), 0.0)
            kb, vb, eg, qs = k * bx, v * bx, jnp.exp(gx), q * SCALE
            yield
            big_g, big_p = _pp_nt(kb, k), _pp_nt(qs, k)
            yield
            x = -jnp.where(strict, big_g * dm, 0.0)
            att = jnp.where(incl, big_p * dm, 0.0)
            tm = jnp.where(row == col, 1.0, 0.0) + x
            x = _pp(x, x)
            yield
            for _ in range(4):
                step, x = _pp(tm, x), _pp(x, x)
                yield
                tm = tm + step
            tm = tm + _pp(tm, x)
            yield
            u, w = _pp(tm, vb), _pp(tm, kb * eg)
            yield
            ws, qgs = _pp(w, s), _pp(qs * eg, s)
            yield
            vn = u - ws
            kd = k * jnp.exp(glast - gx)
            avn, upd = _pp(att, vn), _pp_tn(kd, vn)
            yield
            ss_ref[0, p] = s
            ts_ref[0, p] = tm
            o_ref[:, lanes] = qgs + avn
            state[p] = s * jnp.exp(glast) + upd

        _interleave([chain(p) for p in range(PAIRS)])

    blk = lambda j: pl.BlockSpec((CHUNK, WIDTH), lambda n, j=j: (n, j))
    sv = pl.BlockSpec((1, PAIRS, CHUNK, LANES), lambda n: (n, 0, 0, 0))
    return pl.pallas_call(
        body, name="gdn_fwd", grid=(nchunk,),
        in_specs=[blk(0), blk(1), blk(2), blk(0), blk(0),
                  pl.BlockSpec((1, PAIRS, 1, LANES), lambda n: (n, 0, 0, 0))],
        out_specs=[blk(0), sv, sv],
        out_shape=[jax.ShapeDtypeStruct((rows, WIDTH), F32),
                   jax.ShapeDtypeStruct((nchunk, PAIRS, CHUNK, LANES), F32),
                   jax.ShapeDtypeStruct((nchunk, PAIRS, CHUNK, LANES), F32)],
        scratch_shapes=[pltpu.VMEM((PAIRS, CHUNK, LANES), F32)],
        compiler_params=_params(("arbitrary",)),
    )(qkv, qkv, qkv, betax, gcx, grow)


def _gdn_backward(qkv, betax, gcx, grow, ssave, tsave, do, rows):
    nchunk = rows // CHUNK

    def body(q_ref, k_ref, v_ref, bx_ref, gx_ref, gr_ref, ss_ref, ts_ref, do_ref,
             dq_ref, dk_ref, dv_ref, dbx_ref, dgx_ref, dgr_ref, dstate):
        n = pl.program_id(0)

        @pl.when(n == 0)
        def _():
            dstate[...] = jnp.zeros_like(dstate)

        row, col = _gdn_masks()
        incl, strict = col <= row, col < row

        def chain(p):
            lanes = pl.ds(p * LANES, LANES)
            q, k, v, bx, gx = q_ref[:, lanes], k_ref[:, lanes], v_ref[:, lanes], bx_ref[:, lanes], gx_ref[:, lanes]
            gr = gr_ref[0, p]
            glast = gx_ref[pl.ds(CHUNK - 1, 1), lanes]
            s, tm, d_o = ss_ref[0, p], ts_ref[0, p], do_ref[:, lanes]
            ds_out = dstate[p]
            dm = jnp.where(incl, jnp.exp(jnp.minimum(gx - gr, 0.0)), 0.0)
            kb, vb, eg, qs = k * bx, v * bx, jnp.exp(gx), q * SCALE
            kbg, qg = kb * eg, qs * eg
            ed = jnp.exp(glast - gx)
            kd = k * ed
            eglast = jnp.exp(glast)
            yield
            big_g, big_p = _pp_nt(kb, k), _pp_nt(qs, k)
            u, w = _pp(tm, vb), _pp(tm, kbg)
            dqg, kds = _pp_nt(d_o, s), _pp(kd, ds_out)
            yield
            low = jnp.where(strict, big_g * dm, 0.0)
            att = jnp.where(incl, big_p * dm, 0.0)
            ws, atd = _pp(w, s), _pp_tn(att, d_o)
            yield
            vn = u - ws
            dvn = kds + atd
            dkd, datt_raw = _pp_nt(vn, ds_out), _pp_nt(d_o, vn)
            dw_neg, dvb = _pp_nt(dvn, s), _pp_tn(tm, dvn)
            dtm_a, wdv = _pp_nt(dvn, vb), _pp_tn(w, dvn)
            qgd = _pp_tn(qg, d_o)
            yield
            datt = jnp.where(incl, datt_raw, 0.0)
            dw = -dw_neg
            dtm_b, dkbg = _pp_nt(dw, kbg), _pp_tn(tm, dw)
            dbig_p = datt * dm
            dqs_a, dk_p = _pp(dbig_p, k), _pp_tn(dbig_p, qs)
            yield
            inner = _pp_tn(tm, dtm_a + dtm_b)
            yield
            dlow = jnp.where(strict, -_pp_nt(inner, tm), 0.0)
            yield
            dbig_g = dlow * dm
            dkb_a, dk_g = _pp(dbig_g, k), _pp_tn(dbig_g, kb)
            yield
            dkb = dkb_a + dkbg * eg
            dqs = dqs_a + dqg * eg
            dk = dk_g + dk_p + dkd * ed + dkb * bx
            z = dlow * low + datt * att
            kdterm = dkd * kd
            dglast = (jnp.sum(ds_out * s, axis=0, keepdims=True) * eglast
                      + jnp.sum(kdterm, axis=0, keepdims=True))
            dgx = dqg * qg + dkbg * kbg - kdterm
            dgx = dgx + jnp.where(col == 0, _pair_sum(z), 0.0)
            dgx = dgx + jnp.where(row == CHUNK - 1, dglast, 0.0)
            dq_ref[:, lanes] = dqs * SCALE
            dk_ref[:, lanes] = dk
            dv_ref[:, lanes] = dvb * bx
            dbx_ref[:, lanes] = dkb * k + dvb * v
            dgx_ref[:, lanes] = dgx
            dgr_ref[0, p] = -jnp.sum(z, axis=0, keepdims=True)
            dstate[p] = ds_out * eglast + qgd - wdv

        _interleave([chain(p) for p in range(PAIRS)])

    last = nchunk - 1
    blk = lambda j: pl.BlockSpec((CHUNK, WIDTH), lambda n, j=j: (last - n, j))
    sv = pl.BlockSpec((1, PAIRS, CHUNK, LANES), lambda n: (last - n, 0, 0, 0))
    gr_spec = pl.BlockSpec((1, PAIRS, 1, LANES), lambda n: (last - n, 0, 0, 0))
    wide = jax.ShapeDtypeStruct((rows, WIDTH), F32)
    return pl.pallas_call(
        body, name="gdn_bwd", grid=(nchunk,),
        in_specs=[blk(0), blk(1), blk(2), blk(0), blk(0), gr_spec, sv, sv, blk(0)],
        out_specs=[blk(0)] * 5 + [gr_spec],
        out_shape=[wide] * 5 + [jax.ShapeDtypeStruct((nchunk, PAIRS, 1, LANES), F32)],
        scratch_shapes=[pltpu.VMEM((PAIRS, CHUNK, LANES), F32)],
        compiler_params=_params(("arbitrary",)),
    )(qkv, qkv, qkv, betax, gcx, grow, ssave, tsave, do)


ATT_TQ = 256


def _att_scores(qh, kt, fk, diag):
    s = _dot(qh, kt, _CONTRACT["nt"]) - fk
    if diag:
        r = lax.broadcasted_iota(jnp.int32, s.shape, 0)
        c = lax.broadcasted_iota(jnp.int32, s.shape, 1)
        s = jnp.where(r >= c, s, -jnp.inf)
    return s


def _head_masks(n):
    lo = _lane_lo((n, LANES))
    return [lo, jnp.logical_not(lo)]


def _attention_forward(fqk, proj, frow, rows):
    tq = tk = min(ATT_TQ, rows)
    nq = rows // tq
    v_off = 3072 // LANES

    def body(q_ref, k_ref, v_ref, fr_ref, o_ref, lse_ref):
        qi = pl.program_id(1)
        q = q_ref[...] * SCALE
        keep_q, keep_k = _head_masks(tq), _head_masks(tk)
        qh = [jnp.where(keep_q[h], q, 0.0).astype(BF16) for h in range(2)]

        def tile(ki, carry, diag):
            k0 = pl.multiple_of(ki * tk, tk)
            kt = k_ref[pl.ds(k0, tk), :].astype(BF16)
            v_t = v_ref[pl.ds(k0, tk), :]
            out = [None, None]

            def chain(h):
                m, l, acc = carry[h]
                vt = jnp.where(keep_k[h], v_t, 0.0).astype(BF16)
                yield
                s = _att_scores(qh[h], kt, fr_ref[0, pl.ds(h, 1), pl.ds(k0, tk)], diag)
                yield
                m_new = jnp.maximum(m, jnp.max(s, axis=-1, keepdims=True))
                p = jnp.exp(s - m_new)
                alpha = jnp.exp(m - m_new)
                l = alpha * l + jnp.sum(p, axis=-1, keepdims=True)
                p_hi = p.astype(BF16)
                p_lo = p - p_hi.astype(F32)
                yield
                out[h] = (m_new, l, alpha * acc + _dot(p_hi, vt, _CONTRACT["nn"]) + _dot(p_lo, vt, _CONTRACT["nn"]))

            _interleave([chain(0), chain(1)])
            return tuple(out)

        one = (jnp.full((tq, 1), -jnp.inf, F32), jnp.zeros((tq, 1), F32), jnp.zeros((tq, LANES), F32))
        carry = lax.fori_loop(0, qi, lambda ki, c: tile(ki, c, False), (one, one))
        (m0, l0, acc0), (m1, l1, acc1) = tile(qi, carry, True)
        o_ref[...] = acc0 / l0 + acc1 / l1
        lse_ref[...] = jnp.where(keep_q[0], m0 + jnp.log(l0), m1 + jnp.log(l1))

    whole = lambda off: pl.BlockSpec((rows, LANES), lambda p, i, off=off: (0, off + p))
    qblk = lambda off: pl.BlockSpec((tq, LANES), lambda p, i, off=off: (i, off + p))
    wide = jax.ShapeDtypeStruct((rows, WIDTH), F32)
    return pl.pallas_call(
        body, name="fox_fwd", grid=(PAIRS, nq),
        in_specs=[qblk(0), whole(PAIRS), whole(v_off), pl.BlockSpec((1, 2, rows), lambda p, i: (p, 0, 0))],
        out_specs=[qblk(0), qblk(0)], out_shape=[wide, wide],
        compiler_params=_params(("parallel", "arbitrary")),
    )(fqk, fqk, proj, frow)


def _attention_backward(fqk, proj, frow, ao, lse, dao, rows):
    tq = tk = min(ATT_TQ, rows)
    nq = rows // tq
    v_off = 3072 // LANES

    def body(q_ref, k_ref, v_ref, fr_ref, o_ref, lse_ref, do_ref, dq_ref, dk_ref, dv_ref, dfr_ref):
        ki = pl.program_id(1)

        @pl.when(ki == 0)
        def _():
            dq_ref[...] = jnp.zeros_like(dq_ref)

        keep_q, keep_k = _head_masks(tq), _head_masks(tk)
        k_t = k_ref[...]
        kt = k_t.astype(BF16)
        vt = v_ref[...].astype(BF16)
        kh = [jnp.where(keep_k[h], k_t, 0.0).astype(BF16) for h in range(2)]
        fk = [fr_ref[0, pl.ds(h, 1), :] for h in range(2)]

        def tile(qi, carry, diag):
            dk, dv, df0, df1 = carry
            rows_q = pl.ds(pl.multiple_of(qi * tq, tq), tq)
            q, d_o, lse_t = q_ref[rows_q, :] * SCALE, do_ref[rows_q, :], lse_ref[rows_q, :]
            delta_x = _pair_sum(d_o.astype(BF16).astype(F32) * o_ref[rows_q, :])
            res = [None, None]

            def chain(h):
                qh = jnp.where(keep_q[h], q, 0.0).astype(BF16)
                doh = jnp.where(keep_q[h], d_o, 0.0).astype(BF16)
                lse_h, delta_h = _head_col(lse_t, keep_q[0], h), _head_col(delta_x, keep_q[0], h)
                yield
                s, dp = _att_scores(qh, kt, fk[h], diag), _dot(doh, vt, _CONTRACT["nt"])
                yield
                p = jnp.exp(s - lse_h)
                ds = p * (dp - delta_h)
                yield
                res[h] = (_dot(p, doh, _CONTRACT["tn"]), _dot(ds, qh, _CONTRACT["tn"]),
                          _dot(ds, kh[h], _CONTRACT["nn"]), jnp.sum(ds, axis=0, keepdims=True))

            _interleave([chain(0), chain(1)])
            (dv0, dk0, dq0, s0), (dv1, dk1, dq1, s1) = res
            dq_ref[rows_q, :] += (dq0 + dq1) * SCALE
            return dk + dk0 + dk1, dv + dv0 + dv1, df0 - s0, df1 - s1

        zero_kv = jnp.zeros((tk, LANES), F32)
        zero_f = jnp.zeros((1, tk), F32)
        carry = tile(ki, (zero_kv, zero_kv, zero_f, zero_f), True)
        dk, dv, df0, df1 = lax.fori_loop(ki + 1, nq, lambda qi, c: tile(qi, c, False), carry)
        dk_ref[...] = dk
        dv_ref[...] = dv.astype(dv_ref.dtype)
        dfr_ref[0, pl.ds(0, 1), :] = df0
        dfr_ref[0, pl.ds(1, 1), :] = df1

    whole = lambda off: pl.BlockSpec((rows, LANES), lambda p, i, off=off: (0, off + p))
    kblk = lambda off: pl.BlockSpec((tk, LANES), lambda p, i, off=off: (i, off + p))
    fr_spec = pl.BlockSpec((1, 2, tk), lambda p, i: (p, 0, i))
    wide = jax.ShapeDtypeStruct((rows, WIDTH), F32)
    return pl.pallas_call(
        body, name="fox_bwd", grid=(PAIRS, nq),
        in_specs=[whole(0), kblk(PAIRS), kblk(v_off), fr_spec, whole(0), whole(0), whole(0)],
        out_specs=[whole(0), kblk(0), kblk(0), fr_spec],
        out_shape=[wide, wide, jax.ShapeDtypeStruct((rows, WIDTH), BF16),
                   jax.ShapeDtypeStruct((PAIRS, 2, rows), F32)],
        compiler_params=_params(("parallel", "arbitrary")),
    )(fqk, fqk, proj, frow, ao, lse, dao)


def _lane_ids(shape):
    return lax.broadcasted_iota(jnp.int32, shape, len(shape) - 1)


def _gates_elem(a_log, dt_bias, f_bias, pre):
    lane = _lane_ids(pre.shape)
    beta = jax.nn.sigmoid(pre)
    g = -jnp.exp(a_log) * _softplus(pre + dt_bias)
    lf = -_softplus(-(pre + f_bias))
    return jnp.where(lane < 8, beta, jnp.where(lane < 16, g, jnp.where(lane < 24, lf, 0.0)))


def _tri_consts():
    r = np.arange(LANES)[:, None]
    c = np.arange(LANES)[None, :]
    full = (c <= r).astype(np.float32)
    chunked = full * ((r // CHUNK) == (c // CHUNK))
    return jnp.asarray(chunked), jnp.asarray(full)


def _cums_fwd(lc, lf, gates):
    rows = gates.shape[0]
    lane = _lane_ids((LANES, LANES))
    carry = jnp.zeros((1, LANES), F32)
    out = []
    for r in range(rows // LANES):
        blk = gates[r * LANES:(r + 1) * LANES]
        gc = _dot32(lc, blk, _CONTRACT["nn"])
        f = _dot32(lf, blk, _CONTRACT["nn"]) + carry
        carry = carry + jnp.sum(blk, axis=0, keepdims=True)
        out.append(jnp.where((lane >= 8) & (lane < 16), gc, jnp.where((lane >= 16) & (lane < 24), f, 0.0)))
    return jnp.concatenate(out, axis=0)


def _cums_bwd(lc, lf, dcums):
    rows = dcums.shape[0]
    lane = _lane_ids((LANES, LANES))
    is_g = (lane >= 8) & (lane < 16)
    is_f = (lane >= 16) & (lane < 24)
    carry = jnp.zeros((1, LANES), F32)
    out = [None] * (rows // LANES)
    for r in reversed(range(rows // LANES)):
        blk = dcums[r * LANES:(r + 1) * LANES]
        dg = jnp.where(is_g, blk, 0.0)
        df = jnp.where(is_f, blk, 0.0)
        out[r] = _dot32(lc, dg, _CONTRACT["tn"]) + _dot32(lf, df, _CONTRACT["tn"]) + carry
        carry = carry + jnp.sum(df, axis=0, keepdims=True)
    return jnp.concatenate(out, axis=0)


def _expand_consts():
    xb = np.zeros((LANES, WIDTH), np.float32)
    xg = np.zeros((LANES, WIDTH), np.float32)
    for h in range(HEADS):
        xb[h, h * HEAD_DIM:(h + 1) * HEAD_DIM] = 1.0
        xg[8 + h, h * HEAD_DIM:(h + 1) * HEAD_DIM] = 1.0
    return jnp.asarray(xb), jnp.asarray(xg)


def _shift_down(x, s):
    if s == 0:
        return x
    row = lax.broadcasted_iota(jnp.int32, x.shape, 0)
    return jnp.where(row >= s, pltpu.roll(x, s, 0), 0.0)


def _shift_up(x, s):
    if s == 0:
        return x
    n = x.shape[0]
    row = lax.broadcasted_iota(jnp.int32, x.shape, 0)
    return jnp.where(row < n - s, pltpu.roll(x, n - s, 0), 0.0)


def _row_of(cw, i):
    row = lax.broadcasted_iota(jnp.int32, cw.shape, 0)
    return jnp.sum(jnp.where(row == i, cw, 0.0), axis=0, keepdims=True)


def _conv(cw, x):
    c = jnp.zeros_like(x)
    for i in range(CONV_K):
        c = c + _row_of(cw, i) * _shift_down(x, CONV_K - 1 - i)
    return c


def _post_conv(is_qk, c):
    s = _silu(c)
    n = s * lax.rsqrt(_pair_sum(s * s) + EPS)
    return jnp.where(is_qk, n, s)


def _gdn_prep_fwd(col, cw, x):
    return (_post_conv(col < 2 * PAIRS, _conv(cw, x)),)


def _gdn_prep_bwd(is_qk, cw, x, dy):
    c = _conv(cw, x)
    _, vjp = jax.vjp(lambda cc: _post_conv(is_qk, cc), c)
    (dc,) = vjp(dy)
    dx = jnp.zeros_like(x)
    row = lax.broadcasted_iota(jnp.int32, cw.shape, 0)
    dcw = jnp.zeros(cw.shape, F32)
    for i in range(CONV_K):
        s = CONV_K - 1 - i
        dx = dx + _row_of(cw, i) * _shift_up(dc, s)
        dcw = dcw + jnp.where(row == i, jnp.sum(dc * _shift_down(x, s), axis=0, keepdims=True), 0.0)
    return dx, dcw


def _head_rms(w, x):
    return x * lax.rsqrt(_pair_sum(x * x) / HEAD_DIM + EPS) * w


def _cat_weights(w_in_t):
    tail = jnp.pad(w_in_t[4112:4120], ((0, D_CAT - D_IN), (0, 0)))
    return jnp.concatenate([w_in_t[:2048], w_in_t[2064:4112], w_in_t[2048:2064], tail], axis=0)


def _uncat_grad(g):
    return jnp.concatenate([g[:2048], g[4096:4112], g[2048:4096], g[4112:4120]], axis=0)


def _lanes_to_rowform(v8, rows):
    return v8.reshape(rows // CHUNK, CHUNK, HEADS).transpose(0, 2, 1).reshape(rows // CHUNK, PAIRS, 1, LANES)


def _rowform_to_lanes(v, rows):
    return v.reshape(rows // CHUNK, HEADS, CHUNK).transpose(0, 2, 1).reshape(rows, HEADS)


def _local_step(x, target, norm1_w, a_log, dt_bias, out_norm_w, f_bias, q_norm_w, k_norm_w,
                norm2_w, final_w, first_weights, late_weights, early_grads_ready, early_grads_continue):
    rows = x.shape[0]
    tm = min(512, rows)
    lc, lf = _tri_consts()
    xb, xg = _expand_consts()

    (h1,) = _tiles(lambda col, w, xx: (_rms(xx, w),), name="norm1", rows=rows, tm=tm,
                   full_consts=[norm1_w], row_ins=[(x, D_MODEL, 0)], row_outs=[(D_MODEL, BF16)])
    w_cat, conv_w = first_weights(h1)
    proj = _mm(h1, w_cat, dims="nt", name="in_proj", tn=1408, tk=1024)

    lane_pad = lambda v, off: jnp.pad(v.reshape(1, -1), ((0, 0), (off, LANES - off - v.size)))
    p_a, p_dt, p_fb = lane_pad(a_log, 8), lane_pad(dt_bias, 8), lane_pad(f_bias, 16)

    def gates_fwd(col, lcv, lfv, a, dt, fb, pre):
        gates = _gates_elem(a, dt, fb, pre)
        return gates, _cums_fwd(lcv, lfv, gates)

    gates, cums = _tiles(gates_fwd, name="gates", rows=rows, tm=rows,
                         full_consts=[lc, lf, p_a, p_dt, p_fb], row_ins=[(proj, LANES, COL_SMALL)],
                         row_outs=[(LANES, F32), (LANES, F32)])

    def expand_fwd(col, b, g, gt, cm):
        return (_dot32(gt, b, _CONTRACT["nn"]), _dot32(cm, g, _CONTRACT["nn"]))

    betax, gcx = _tiles(expand_fwd, name="expand", rows=rows, tm=tm, full_consts=[xb, xg],
                        row_ins=[(gates, LANES, 0), (cums, LANES, 0)],
                        row_outs=[(WIDTH, F32)] * 2)
    grow = _lanes_to_rowform(cums[:, 8:16], rows)
    frow = cums[:, 16:24].T.reshape(PAIRS, 2, rows)

    (qkv,) = _tiles(_gdn_prep_fwd, name="gdn_prep", rows=rows, tm=rows, ncol=3 * PAIRS,
                    col_consts=[(conv_w, CONV_K, LANES, 0)], row_ins=[(proj, LANES, 0)],
                    row_outs=[(LANES, F32)])
    o_gdn, ssave, tsave = _gdn_forward(qkv, betax, gcx, grow, rows)

    w_qk = jnp.concatenate([jnp.tile(q_norm_w.reshape(1, -1), (1, HEADS)),
                            jnp.tile(k_norm_w.reshape(1, -1), (1, HEADS))], axis=1)
    fox_off = 2048 // LANES
    (fqk,) = _tiles(lambda col, w, xx: (_head_rms(w, xx),), name="fox_prep", rows=rows, tm=rows, ncol=2 * PAIRS,
                    col_consts=[(w_qk, 1, LANES, 0)], row_ins=[(proj, LANES, fox_off)],
                    row_outs=[(LANES, F32)])
    ao, lse = _attention_forward(fqk, proj, frow, rows)

    w_on = jnp.tile(out_norm_w.reshape(1, -1), (1, 2))
    z_off, fg_off = 1536 // LANES, 3584 // LANES
    mix_g_fn = lambda w, o, z: _head_rms(w, o) * _silu(z)
    mix_f_fn = lambda a, g: a * jax.nn.sigmoid(g)
    (mix_g,) = _tiles(lambda col, w, o, z: (mix_g_fn(w, o, z),), name="mix_gdn", rows=rows, tm=rows, ncol=PAIRS,
                      full_consts=[w_on], row_ins=[(o_gdn, LANES, 0), (proj, LANES, z_off)],
                      row_outs=[(LANES, BF16)])
    (mix_f,) = _tiles(lambda col, a, g: (mix_f_fn(a, g),), name="mix_fox", rows=rows, tm=rows, ncol=PAIRS,
                      row_ins=[(ao, LANES, 0), (proj, LANES, fg_off)], row_outs=[(LANES, BF16)])
    mix = jnp.concatenate([mix_g, mix_f], axis=1)
    w_out, w_gate, w_up, w_down = late_weights(mix)
    t_rows, t_half = min(1024, rows), min(512, rows)
    n_rt = rows // t_rows
    row_blk = pl.BlockSpec((t_rows, D_MODEL), lambda i, n: (i, 0))
    half_blk = pl.BlockSpec((t_half, D_MODEL), lambda i, n: (i, 0))
    vec_blk = pl.BlockSpec((1, D_MODEL), lambda i, n: (0, 0))
    wide = lambda dt: jax.ShapeDtypeStruct((rows, D_MODEL), dt)
    x1, h2 = _mm_blocks(mix, w_out, name="out_proj_norm2", grid=(n_rt, 1), dims="nn",
                        a_spec=row_blk, b_spec=pl.BlockSpec((D_MODEL, D_MODEL), lambda i, n: (0, 0)),
                        o_spec=[row_blk, row_blk], out_shape=[wide(F32), wide(BF16)], add=x, add_spec=row_blk,
                        extra=[(norm2_w, vec_blk)], epilogue=lambda r, w: (r, _rms(r, w)))
    st_act = jax.ShapeDtypeStruct((N_CHIPS, rows, FF_SHARD), BF16)
    act_fn = lambda g, u: _silu(g) * u
    st_tile = pl.BlockSpec((None, t_rows, FF_SHARD), lambda i, j: (j, i, 0))
    h2_blk = pl.BlockSpec((t_rows, D_MODEL), lambda i, j: (i, 0))
    w_blk = pl.BlockSpec((None, FF_SHARD, D_MODEL), lambda i, j: (j, 0, 0))

    def gate_up_act(u, h, wg):
        g = _dot(h, wg, _CONTRACT["nt"])
        return g, u, act_fn(g, u)

    gate, up, act = _mm_blocks(h2, w_up, name="ffn_gate_up_act", grid=(n_rt, N_CHIPS), dims="nt",
                               a_spec=h2_blk, b_spec=w_blk, o_spec=[st_tile] * 3, out_shape=[st_act] * 3,
                               extra=[(h2, h2_blk), (w_gate, w_blk)], epilogue=gate_up_act)

    def final_fn(xx, tgt, w):
        y, vjp = jax.vjp(_rms, xx, w)
        err = y - tgt
        loss = 0.5 * jnp.sum(err * err) / D_MODEL
        dx, dw = vjp(err / D_MODEL)
        return dx, dx, jnp.full((1, LANES), loss, F32), dw

    dx2, dx2_b, loss, d_final_w = _mm_blocks(
        act, w_down, name="ffn_down_loss", grid=(rows // t_half, 1), dims="nn", n_sum=N_CHIPS,
        a_spec=pl.BlockSpec((N_CHIPS, t_half, FF_SHARD), lambda i, n: (0, i, 0)),
        b_spec=pl.BlockSpec((N_CHIPS, FF_SHARD, D_MODEL), lambda i, n: (0, 0, 0)),
        o_spec=[half_blk, half_blk, pl.BlockSpec((1, LANES), lambda i, n: (0, 0)), vec_blk],
        out_shape=[wide(F32), wide(BF16), jax.ShapeDtypeStruct((1, LANES), F32),
                   jax.ShapeDtypeStruct((1, D_MODEL), F32)],
        add=x1, add_spec=half_blk, extra=[(target, half_blk), (final_w, vec_blk)], epilogue=final_fn, n_acc=2)

    def act_bwd(d, g, u):
        _, vjp = jax.vjp(act_fn, g.astype(F32), u.astype(F32))
        return vjp(d)

    dgate, dup = _mm_blocks(dx2_b, w_down, name="d_act_gate_up", grid=(n_rt, N_CHIPS), dims="nt",
                            a_spec=pl.BlockSpec((t_rows, D_MODEL), lambda i, j: (i, 0)),
                            b_spec=pl.BlockSpec((None, FF_SHARD, D_MODEL), lambda i, j: (j, 0, 0)),
                            o_spec=[st_tile, st_tile], out_shape=[st_act, st_act],
                            extra=[(gate, st_tile), (up, st_tile)], epilogue=act_bwd)

    def g_ffn(d_st, other, name):
        return _mm_blocks(d_st, other, name=name, grid=(N_CHIPS, 1), dims="tn",
                          a_spec=pl.BlockSpec((None, rows, FF_SHARD), lambda j, n: (j, 0, 0)),
                          b_spec=pl.BlockSpec((rows, D_MODEL), lambda j, n: (0, 0)),
                          o_spec=pl.BlockSpec((None, FF_SHARD, D_MODEL), lambda j, n: (j, 0, 0)),
                          out_shape=jax.ShapeDtypeStruct((N_CHIPS, FF_SHARD, D_MODEL), BF16))

    g_down = g_ffn(act, dx2_b, "g_down")

    def norm_bwd(dh, xx, dres, w):
        _, vjp = jax.vjp(_rms, xx, w)
        dx, dw = vjp(dh)
        return dx + dres, dx + dres, dw

    def d_h2(d_st, w_st, name, add, **fused):
        return _mm_blocks(d_st, w_st, name=name, grid=(rows // t_half, 1), dims="nn", n_sum=N_CHIPS,
                          a_spec=pl.BlockSpec((N_CHIPS, t_half, FF_SHARD), lambda i, n: (0, i, 0)),
                          b_spec=pl.BlockSpec((N_CHIPS, FF_SHARD, D_MODEL), lambda i, n: (0, 0, 0)),
                          add=add, add_spec=half_blk, **fused)

    dh2_gate = d_h2(dgate, w_gate, "d_h2_gate", None, o_spec=half_blk, out_shape=wide(F32))
    dx1, dx1_b, d_norm2_w = d_h2(
        dup, w_up, "d_h2_up_norm2_bwd", dh2_gate, o_spec=[half_blk, half_blk, vec_blk],
        out_shape=[wide(F32), wide(BF16), jax.ShapeDtypeStruct((1, D_MODEL), F32)],
        extra=[(x1, half_blk), (dx2, half_blk), (norm2_w, vec_blk)], epilogue=norm_bwd, n_acc=1)
    g_gate, g_up = g_ffn(dgate, h2, "g_gate"), g_ffn(dup, h2, "g_up")
    dmix = _mm(dx1_b, w_out, dims="nt", name="d_mix", tn=D_MODEL, tk=1024)
    g_out = _mm(mix, dx1_b, dims="tn", name="g_out", tn=D_MODEL, tk=rows, out_dtype=BF16)
    w_on = w_on + early_grads_ready(g_out, g_gate, g_up, g_down)

    def mix_g_bwd(col, w, o, z, d):
        _, vjp = jax.vjp(mix_g_fn, w, o, z)
        dw, do_, dz = vjp(d)
        return do_, dz, dw

    do_gdn, dz, d_on = _tiles(mix_g_bwd, name="mix_gdn_bwd", rows=rows, tm=rows, ncol=PAIRS, full_consts=[w_on],
                              row_ins=[(o_gdn, LANES, 0), (proj, LANES, z_off), (dmix, LANES, 0)],
                              row_outs=[(LANES, F32), (LANES, BF16)], acc_outs=[(1, LANES)])

    def mix_f_bwd(col, a, g, d):
        _, vjp = jax.vjp(mix_f_fn, a, g)
        return vjp(d)

    dao, dfgate = _tiles(mix_f_bwd, name="mix_fox_bwd", rows=rows, tm=rows, ncol=PAIRS,
                         row_ins=[(ao, LANES, 0), (proj, LANES, fg_off), (dmix, LANES, PAIRS)],
                         row_outs=[(LANES, F32), (LANES, BF16)])

    dfq, dfk, dfv, dfrow = _attention_backward(fqk, proj, frow + early_grads_continue(dao), ao, lse, dao, rows)

    def fox_prep_bwd(col, w, xx, d):
        _, vjp = jax.vjp(_head_rms, w, xx)
        dw, dx = vjp(d)
        return dx, dw

    dfqk, d_wqk = [], []
    for part, d_n in enumerate((dfq, dfk)):
        dx_p, dw_p = _tiles(fox_prep_bwd, name="fox_prep_bwd_" + "qk"[part], rows=rows, tm=rows, ncol=PAIRS,
                            col_consts=[(w_qk, 1, LANES, part * PAIRS)],
                            row_ins=[(proj, LANES, fox_off + part * PAIRS), (d_n, LANES, 0)],
                            row_outs=[(LANES, BF16)], acc_outs=[(1, LANES)])
        dfqk.append(dx_p)
        d_wqk.append(dw_p)

    dq, dk, dv, dbetax, dgcx, dgrow = _gdn_backward(qkv, betax, gcx, grow, ssave, tsave, do_gdn, rows)
    dqkv, d_conv = [], []
    for part, d_n in enumerate((dq, dk, dv)):
        prep_bwd = lambda col, cw, xx, dy, is_qk=(part < 2): _gdn_prep_bwd(is_qk, cw, xx, dy)
        dx_p, dw_p = _tiles(prep_bwd, name="gdn_prep_bwd_" + "qkv"[part], rows=rows, tm=rows, ncol=PAIRS,
                            col_consts=[(conv_w, CONV_K, LANES, part * PAIRS)],
                            row_ins=[(proj, LANES, part * PAIRS), (d_n, LANES, 0)],
                            row_outs=[(LANES, BF16)], acc_outs=[(CONV_K, LANES)])
        dqkv.append(dx_p)
        d_conv.append(dw_p)
    d_conv = jnp.concatenate(d_conv, axis=1)

    def expand_bwd(col, b, g, db, dg):
        return (_dot32(db, b, _CONTRACT["nt"]), _dot32(dg, g, _CONTRACT["nt"]))

    dgates_b, dcums_g = _tiles(expand_bwd, name="expand_bwd", rows=rows, tm=tm, full_consts=[xb, xg],
                               row_ins=[(dbetax, WIDTH, 0), (dgcx, WIDTH, 0)],
                               row_outs=[(LANES, F32), (LANES, F32)])
    dcums_row = jnp.concatenate([jnp.zeros((rows, 8), F32), _rowform_to_lanes(dgrow, rows),
                                 dfrow.reshape(HEADS, rows).T, jnp.zeros((rows, LANES - 24), F32)], axis=1)

    def gates_bwd(col, lcv, lfv, a, dt, fb, pre, dgb, dcg, dcr):
        lane = _lane_ids(pre.shape)
        dgates = jnp.where(lane < 8, dgb, _cums_bwd(lcv, lfv, dcg + dcr))
        _, vjp = jax.vjp(_gates_elem, a, dt, fb, pre)
        da, ddt, dfb, dpre = vjp(dgates)
        return dpre, da, ddt, dfb

    dpre, d_a, d_dt, d_fb = _tiles(gates_bwd, name="gates_bwd", rows=rows, tm=rows,
                                   full_consts=[lc, lf, p_a, p_dt, p_fb],
                                   row_ins=[(proj, LANES, COL_SMALL), (dgates_b, LANES, 0), (dcums_g, LANES, 0),
                                            (dcums_row, LANES, 0)],
                                   row_outs=[(LANES, BF16)], acc_outs=[(1, LANES)] * 3)

    dproj = jnp.concatenate(dqkv + [dz] + dfqk + [dfv, dfgate, dpre], axis=1)
    grad_x, d_norm1_w = _mm_blocks(
        dproj, w_cat, name="d_h1_norm1_bwd", grid=(rows // t_half, 1), dims="nn",
        a_spec=pl.BlockSpec((t_half, D_CAT), lambda i, n: (i, 0)),
        b_spec=pl.BlockSpec((D_CAT, D_MODEL), lambda i, n: (0, 0)),
        o_spec=[half_blk, vec_blk], out_shape=[wide(F32), jax.ShapeDtypeStruct((1, D_MODEL), F32)],
        extra=[(x, half_blk), (dx1, half_blk), (norm1_w, vec_blk)],
        epilogue=lambda dh, xx, dres, w: norm_bwd(dh, xx, dres, w)[1:], n_acc=1)
    g_cat = _mm(dproj, h1, dims="tn", name="g_in", tm=1408, tn=D_MODEL, tk=rows)

    fold = lambda v: v.reshape(-1, HEAD_DIM).sum(axis=0)
    small = dict(
        loss=loss[0, 0],
        norm1_w=d_norm1_w, conv_w=d_conv, a_log=d_a[0, 8:16], dt_bias=d_dt[0, 8:16],
        out_norm_w=fold(d_on), f_bias=d_fb[0, 16:24], q_norm_w=fold(d_wqk[0]),
        k_norm_w=fold(d_wqk[1]), norm2_w=d_norm2_w, final_w=d_final_w)
    return grad_x, g_cat, g_out, g_gate, g_up, g_down, small


HBM_SPEC = pl.BlockSpec(memory_space=pltpu.HBM)


def _place():
    x, y, c = lax.axis_index("x"), lax.axis_index("y"), lax.axis_index("c")
    chips = [(1 - x, y), (x, 1 - y), (1 - x, 1 - y)]
    return x, y, c, 2 * x + y, (x, y, 1 - c), chips, [2 * cx + cy for cx, cy in chips]


def _remote(src, dst, send_sem, recv_sem, to):
    return pltpu.make_async_remote_copy(src_ref=src, dst_ref=dst, send_sem=send_sem, recv_sem=recv_sem,
                                        device_id=to, device_id_type=MESH)


SEM_SPEC =pl.BlockSpec(memory_space=pltpu.SEMAPHORE)
ANY_SPEC = pl.BlockSpec(memory_space=pl.ANY)
DATAFLOW = pltpu.SideEffectType.DATAFLOW_SIDE_EFFECTING


def _gather_plan(srcs, lands):
    x, y, c, own, sib, chips, chip_idx = _place()
    plan = []
    for src, land in zip(srcs, lands):
        for j, chip in enumerate(chips):
            plan.append((src, land.at[own], (*chip, c), land.at[chip_idx[j]]))
        plan.append((src, land.at[own], sib, land.at[own]))
    return plan


def _exchange_plan(srcs, lands):
    x, y, c, own, sib, chips, chip_idx = _place()
    plan = []
    for src, land in zip(srcs, lands):
        for j, chip in enumerate(chips):
            plan.append((src.at[chip_idx[j]], land.at[j], (*chip, c), land.at[j]))
    return plan


def _swap_plan(srcs, lands):
    x, y, c, own, sib, chips, chip_idx = _place()
    plan = []
    for src, land in zip(srcs, lands):
        h = src.shape[2] // 2
        plan.append((src.at[:, :, pl.ds(pl.multiple_of((1 - c) * h, LANES), h)], land, sib, land))
    return plan


def _in_proj_plan(srcs, lands):
    x, y, c, own, sib, chips, chip_idx = _place()
    (w, conv), (w_land, conv_land) = srcs, lands
    hw = w.shape[1] // 2
    half = lambda ref: ref.at[:, pl.ds(pl.multiple_of(c * hw, LANES), hw)]
    plan = []
    for j, chip in enumerate(chips):
        plan.append((half(w), half(w_land.at[own]), (*chip, c), half(w_land.at[chip_idx[j]])))
        plan.append((conv, conv_land.at[own], (*chip, c), conv_land.at[chip_idx[j]]))
    plan.append((w, w_land.at[own], sib, w_land.at[own]))
    plan.append((conv, conv_land.at[own], sib, conv_land.at[own]))
    return plan


def _forward_halves(landed):
    hw = landed.shape[2] // 2

    def body(in_ref, out_ref, send_sems, recv_sems):
        x, y, c, own, sib, chips, chip_idx = _place()
        half = lambda ref, hc: ref.at[:, pl.ds(pl.multiple_of(hc * hw, LANES), hw)]
        sent = [_remote(half(out_ref.at[chip_idx[j]], c), half(out_ref.at[chip_idx[j]], c),
                        send_sems.at[j], recv_sems.at[j], sib) for j in range(3)]
        for cp in sent:
            cp.start()
        for j in range(3):
            other = half(out_ref.at[chip_idx[j]], 1 - c)
            _remote(other, other, send_sems.at[j], recv_sems.at[j], sib).wait_recv()
        for cp in sent:
            cp.wait_send()

    return pl.pallas_call(
        body, name="gather_in_forward", out_shape=jax.ShapeDtypeStruct(landed.shape, landed.dtype),
        in_specs=[HBM_SPEC], out_specs=HBM_SPEC, input_output_aliases={0: 0},
        scratch_shapes=[pltpu.SemaphoreType.DMA((3,)), pltpu.SemaphoreType.DMA((3,))],
    )(landed)


def _split_start(name, plan_fn, srcs, land_shapes, n_copies, after=None):
    n = len(srcs)
    extra = [] if after is None else [after]

    def body(*refs):
        src_refs, land_refs = refs[:n], refs[n:2 * n]
        send_sems, recv_sems = refs[2 * n + len(extra)], refs[2 * n + len(extra) + 1]
        token = refs[-1]
        for k, (src, dst, to, _) in enumerate(plan_fn(src_refs, land_refs)):
            _remote(src, dst, send_sems.at[k], recv_sems.at[k], to).start()
        token[...] = jnp.zeros_like(token)

    lands = [pltpu.with_memory_space_constraint(lax.empty(s.shape, s.dtype), pltpu.HBM) for s in land_shapes]
    srcs = [pltpu.with_memory_space_constraint(s, pltpu.HBM) for s in srcs]
    out_shape = ([pltpu.SemaphoreType.DMA((n_copies,)), pltpu.SemaphoreType.DMA((n_copies,))]
                 + [pltpu.HBM(s.shape, s.dtype) for s in srcs] + [pltpu.HBM(s.shape, s.dtype) for s in land_shapes]
                 + [jax.ShapeDtypeStruct((8, LANES), F32)])
    res = pl.pallas_call(
        body, name=name, out_shape=out_shape,
        in_specs=[HBM_SPEC] * (2 * n) + [ANY_SPEC] * len(extra),
        out_specs=[SEM_SPEC, SEM_SPEC] + [HBM_SPEC] * (2 * n) + [pl.BlockSpec(memory_space=pltpu.VMEM)],
        input_output_aliases={i: 2 + i for i in range(2 * n)},
        compiler_params=pltpu.CompilerParams(has_side_effects=DATAFLOW),
    )(*srcs, *lands, *extra)
    return dict(sems=res[:2], srcs=res[2:2 + n], lands=res[2 + n:2 + 2 * n], token=res[-1], n=n)


def _split_wait(name, plan_fn, started, after):
    n = started["n"]

    def body(*refs):
        src_refs, land_refs = refs[:n], refs[n:2 * n]
        send_sems, recv_sems = refs[2 * n], refs[2 * n + 1]
        for k, (src, _, to, landed) in enumerate(plan_fn(src_refs, land_refs)):
            copy = _remote(src, landed, send_sems.at[k], recv_sems.at[k], to)
            copy.wait_send()
            copy.wait_recv()

    srcs, lands = started["srcs"], started["lands"]
    after = list(after) if isinstance(after, (list, tuple)) else [after]
    res = pl.pallas_call(
        body, name=name,
        out_shape=[pltpu.HBM(s.shape, s.dtype) for s in srcs] + [pltpu.HBM(s.shape, s.dtype) for s in lands],
        in_specs=[HBM_SPEC] * (2 * n) + [SEM_SPEC, SEM_SPEC] + [ANY_SPEC] * len(after),
        out_specs=[HBM_SPEC] * (2 * n),
        input_output_aliases={i: i for i in range(2 * n)},
        compiler_params=pltpu.CompilerParams(has_side_effects=DATAFLOW),
    )(*srcs, *lands, *started["sems"], *after)
    started["srcs_after"] = res[:n]
    return res[n:]


def _add_halves(stacks, landed, place, name):
    n = len(stacks)

    def body(place_ref, *refs):
        for a_ref, b_ref, o_ref, own_ref in zip(refs[:n], refs[n:2 * n], refs[2 * n::2], refs[2 * n + 1::2]):
            part = (a_ref[...].astype(F32) + b_ref[...].astype(F32)).astype(o_ref.dtype)
            o_ref[...] = part

            @pl.when(pl.program_id(0) == place_ref[1])
            def _(own_ref=own_ref, part=part):
                own_ref[...] = part[0]

    shapes = [l.shape[1:] for l in landed]
    slab = lambda s: pl.BlockSpec((1,) + s, lambda j, p: (j, 0, 0))
    out_shape, out_specs = [], []
    for l, s in zip(landed, shapes):
        out_shape += [jax.ShapeDtypeStruct(l.shape, BF16), jax.ShapeDtypeStruct(s, BF16)]
        out_specs += [slab(s), pl.BlockSpec(s, lambda j, p: (0, 0))]
    res = pl.pallas_call(
        body, name=name, out_shape=out_shape,
        grid_spec=pltpu.PrefetchScalarGridSpec(
            num_scalar_prefetch=1, grid=(N_CHIPS,),
            in_specs=[pl.BlockSpec((1,) + s, lambda j, p: (j, 0, p[0])) for s in shapes] + [slab(s) for s in shapes],
            out_specs=out_specs),
        compiler_params=_params(("arbitrary",)),
    )(place, *stacks, *landed)
    return [(res[2 * i], res[2 * i + 1]) for i in range(n)]


def _sum_many(own_parts, landed, name):
    n = len(own_parts)

    def body(*refs):
        for own_ref, a_ref, o_ref in zip(refs[:n], refs[n:2 * n], refs[2 * n:]):
            acc = own_ref[...].astype(F32)
            for s in range(3):
                acc = acc + a_ref[s].astype(F32)
            o_ref[...] = acc

    whole = lambda a: pl.BlockSpec(a.shape, lambda i, nd=a.ndim: (0,) * nd)
    return pl.pallas_call(
        body, name=name, grid=(1,), out_shape=[jax.ShapeDtypeStruct(o.shape, F32) for o in own_parts],
        in_specs=[whole(a) for a in own_parts] + [whole(a) for a in landed],
        out_specs=[whole(a) for a in own_parts], compiler_params=_params(("arbitrary",)),
    )(*own_parts, *landed)


def _sum_partials(own_part, landed, name, untiled_rows=False):
    _, h, cols = landed.shape
    tc = LANES if untiled_rows else cols

    def body(own_ref, a_ref, o_ref):
        acc = own_ref[...].astype(F32)
        for s in range(3):
            acc = acc + a_ref[s].astype(F32)
        if untiled_rows:
            o_ref[:, 0, :] = acc
        else:
            o_ref[...] = acc

    if untiled_rows:
        out_shape, out_spec = jax.ShapeDtypeStruct((h, 1, cols), F32), pl.BlockSpec((h, 1, tc), lambda i: (0, 0, i))
    else:
        out_shape, out_spec = jax.ShapeDtypeStruct((h, cols), F32), pl.BlockSpec((h, tc), lambda i: (0, i))
    return pl.pallas_call(
        body, name=name, out_shape=out_shape, grid=(cols // tc,),
        in_specs=[pl.BlockSpec((h, tc), lambda i: (0, i)), pl.BlockSpec((3, h, tc), lambda i: (0, 0, i))],
        out_specs=out_spec, compiler_params=_params(("arbitrary",)),
    )(own_part, landed)


def _share_halves(halves, name):
    n = len(halves)

    def body(*refs):
        ins, outs = refs[:n], refs[n:2 * n]
        send_sems, recv_sems = refs[2 * n:]
        x, y, c, own, sib, chips, chip_idx = _place()
        cps = [_remote(ins[i], outs[i], send_sems.at[i], recv_sems.at[i], sib) for i in range(n)]
        for cp in cps:
            cp.start()
        for cp in cps:
            cp.wait()

    return pl.pallas_call(
        body, name=name,
        out_shape=[jax.ShapeDtypeStruct(p.shape, p.dtype) for p in halves],
        in_specs=[HBM_SPEC] * n, out_specs=[HBM_SPEC] * n,
        scratch_shapes=[pltpu.SemaphoreType.DMA((n,)), pltpu.SemaphoreType.DMA((n,))],
    )(*halves)


def _allreduce_small(packed):
    rows = packed.shape[0]
    n_dev = 8

    def body(in_ref, out_ref, gath, send_sems, recv_sems):
        x, y, c = lax.axis_index("x"), lax.axis_index("y"), lax.axis_index("c")
        me = 4 * x + 2 * y + c
        gath[me] = in_ref[...]
        cps = []
        for k in range(1, n_dev):
            fx, fy, fc = (k >> 2) & 1, (k >> 1) & 1, k & 1
            to = (x ^ fx, y ^ fy, c ^ fc)
            cps.append(_remote(in_ref, gath.at[me], send_sems.at[k - 1], recv_sems.at[k - 1], to))
        for cp in cps:
            cp.start()
        for k in range(1, n_dev):
            fx, fy, fc = (k >> 2) & 1, (k >> 1) & 1, k & 1
            src = 4 * (x ^ fx) + 2 * (y ^ fy) + (c ^ fc)
            slot = gath.at[src]
            _remote(slot, slot, send_sems.at[k - 1], recv_sems.at[k - 1], (x, y, c)).wait_recv()
        for cp in cps:
            cp.wait_send()
        acc = gath[0]
        for d in range(1, n_dev):
            acc = acc + gath[d]
        out_ref[...] = acc

    vm = pl.BlockSpec(memory_space=pltpu.VMEM)
    return pl.pallas_call(
        body, name="allreduce_small", out_shape=jax.ShapeDtypeStruct(packed.shape, F32),
        in_specs=[vm], out_specs=vm,
        scratch_shapes=[pltpu.VMEM((n_dev, rows, LANES), F32),
                        pltpu.SemaphoreType.DMA((n_dev - 1,)), pltpu.SemaphoreType.DMA((n_dev - 1,))],
    )(packed)


def _adam(col, w, g, m, v):
    m2 = ADAM_B1 * m + (1.0 - ADAM_B1) * g
    v2 = ADAM_B2 * v + (1.0 - ADAM_B2) * (g * g)
    m_hat = m2 / (1.0 - ADAM_B1 ** ADAM_STEP)
    v_hat = v2 / (1.0 - ADAM_B2 ** ADAM_STEP)
    delta = -ADAM_LR * (m_hat / (jnp.sqrt(v_hat) + ADAM_EPS) + ADAM_WD * w)
    return delta, m2, v2


def _adam_call(w, g, m, v, name):
    rows, cols = w.shape
    tm = rows
    for cand in (256, 352, 176, 128, 64, 48, 16, 8):
        if rows % cand == 0:
            tm = cand
            break
    return _tiles(_adam, name=name, rows=rows, tm=tm,
                  row_ins=[(w, cols, 0), (g, cols, 0), (m, cols, 0), (v, cols, 0)],
                  row_outs=[(cols, F32)] * 3)


def _adam_big(w, g_mine, g_other, m, v, place, name):
    rows, cols = w.shape
    tc = cols // 2
    nt = cols // 2 // tc

    def body(place_ref, w_ref, gm_ref, go_ref, m_ref, v_ref, g_out, d_out, m_out, v_out):
        g = jnp.where(pl.program_id(0) == place_ref[0], gm_ref[...], go_ref[...])
        d, m2, v2 = _adam(None, w_ref[...], g, m_ref[...], v_ref[...])
        g_out[...] = g
        d_out[...] = d
        m_out[...] = m2
        v_out[...] = v2

    full = pl.BlockSpec((rows, tc), lambda hh, i, p: (0, hh * nt + i))
    half = pl.BlockSpec((rows, tc), lambda hh, i, p: (0, i))
    return pl.pallas_call(
        body, name=name, out_shape=[jax.ShapeDtypeStruct(w.shape, F32)] * 4,
        grid_spec=pltpu.PrefetchScalarGridSpec(
            num_scalar_prefetch=1, grid=(2, nt),
            in_specs=[full, half, half, full, full], out_specs=[full] * 4),
        compiler_params=_params(("arbitrary", "arbitrary")),
    )(place, w, g_mine, g_other, m, v)


def _adam_untiled_rows(w, g_mine, g_other, m, v, place, name):
    rows, _, cols = w.shape
    tc = 256
    nt = cols // 2 // tc
    rb = next(r for r in (206, 128, 103, rows) if rows % r == 0)

    def body(place_ref, w_ref, gm_ref, go_ref, m_ref, v_ref, g_out, d_out, m_out, v_out):
        g = jnp.where(pl.program_id(0) == place_ref[0], gm_ref[...], go_ref[...])
        d, m2, v2 = _adam(None, w_ref[...], g, m_ref[...], v_ref[...])
        g_out[...] = g
        d_out[...] = d
        m_out[...] = m2
        v_out[...] = v2

    full = pl.BlockSpec((rb, 1, tc), lambda hh, i, r, p: (r, 0, hh * nt + i))
    half = pl.BlockSpec((rb, 1, tc), lambda hh, i, r, p: (r, 0, i))
    return pl.pallas_call(
        body, name=name, out_shape=[jax.ShapeDtypeStruct(w.shape, F32)] * 4,
        grid_spec=pltpu.PrefetchScalarGridSpec(
            num_scalar_prefetch=1, grid=(2, nt, rows // rb),
            in_specs=[full, half, half, full, full], out_specs=[full] * 4),
        compiler_params=_params(("arbitrary", "arbitrary", "arbitrary")),
    )(place, w, g_mine, g_other, m, v)


def _pack(arrays, zero=None):
    flat = []
    for a in arrays:
        a = a.reshape(-1).astype(F32)
        if zero is not None:
            a = a + zero
        flat.append(jnp.pad(a, (0, (-a.size) % LANES)))
    out = jnp.concatenate(flat)
    out = jnp.pad(out, (0, (-out.size) % (8 * LANES)))
    return out.reshape(-1, LANES)


def _unpack(packed, shapes):
    flat = packed.reshape(-1)
    out, off = [], 0
    for s in shapes:
        size = int(np.prod(s))
        out.append(flat[off:off + size].reshape(s))
        off += size + (-size) % LANES
    return out


def kernel(x, norm1_w, w_in, gdn_conv_w, gdn_A_log, gdn_dt_bias, gdn_out_norm_w, fox_f_bias, fox_q_norm_w, fox_k_norm_w, w_out, norm2_w, w_ffn_gate, w_ffn_up, w_ffn_down, final_norm_w, loss_target, m_norm1_w, m_w_in, m_gdn_conv_w, m_gdn_A_log, m_gdn_dt_bias, m_gdn_out_norm_w, m_fox_f_bias, m_fox_q_norm_w, m_fox_k_norm_w, m_w_out, m_norm2_w, m_w_ffn_gate, m_w_ffn_up, m_w_ffn_down, m_final_norm_w, v_norm1_w, v_w_in, v_gdn_conv_w, v_gdn_A_log, v_gdn_dt_bias, v_gdn_out_norm_w, v_fox_f_bias, v_fox_q_norm_w, v_fox_k_norm_w, v_w_out, v_norm2_w, v_w_ffn_gate, v_w_ffn_up, v_w_ffn_down, v_final_norm_w):
    cx, cy, cc = lax.axis_index("x"), lax.axis_index("y"), lax.axis_index("c")
    own = 2 * cx + cy
    place = jnp.stack([cc, own]).astype(jnp.int32)

    names = ["w_in", "w_out", "w_gate", "w_up", "w_down"]
    is_t = [True, False, True, True, False]
    to_t = lambda a, t: a[0].T if t else a[0]
    from_t = lambda a, t: (a.T if t else a)[None]
    big_w = [to_t(a, t) for a, t in zip([w_in, w_out, w_ffn_gate, w_ffn_up, w_ffn_down], is_t)]
    big_m = [to_t(a, t) for a, t in zip([m_w_in, m_w_out, m_w_ffn_gate, m_w_ffn_up, m_w_ffn_down], is_t)]
    big_v = [to_t(a, t) for a, t in zip([v_w_in, v_w_out, v_w_ffn_gate, v_w_ffn_up, v_w_ffn_down], is_t)]
    shards = [big_w[0].astype(BF16)]
    small_w = [norm1_w, gdn_conv_w, gdn_A_log, gdn_dt_bias, gdn_out_norm_w, fox_f_bias, fox_q_norm_w,
               fox_k_norm_w, norm2_w, final_norm_w]
    small_m = [m_norm1_w, m_gdn_conv_w, m_gdn_A_log, m_gdn_dt_bias, m_gdn_out_norm_w, m_fox_f_bias,
               m_fox_q_norm_w, m_fox_k_norm_w, m_norm2_w, m_final_norm_w]
    small_v = [v_norm1_w, v_gdn_conv_w, v_gdn_A_log, v_gdn_dt_bias, v_gdn_out_norm_w, v_fox_f_bias,
               v_fox_q_norm_w, v_fox_k_norm_w, v_norm2_w, v_final_norm_w]
    first = _split_start("gather_in_start", _in_proj_plan, [shards[0], gdn_conv_w[0]],
                         [jax.ShapeDtypeStruct((N_CHIPS,) + shards[0].shape, BF16),
                          jax.ShapeDtypeStruct((N_CHIPS, CONV_K, 3 * WIDTH // N_CHIPS), F32)],
                         n_copies=8)
    small_packed = [_pack(p, first["token"][0, 0]) for p in (small_w, small_m, small_v)]
    shards += [(w + first["token"][0, 0]).astype(BF16) for w in big_w[1:]]
    rest = {}

    def first_weights(after):
        w_in_g, conv_g = _split_wait("gather_in_wait", _in_proj_plan, first, [after] + small_packed)
        w_in_g = _forward_halves(w_in_g)
        rest.update(_split_start("gather_rest_start", _gather_plan, shards[1:],
                                 [jax.ShapeDtypeStruct((N_CHIPS,) + s.shape, BF16) for s in shards[1:]],
                                 n_copies=4 * len(shards[1:]), after=w_in_g))
        w_cat = _cat_weights(w_in_g.reshape(D_IN, D_MODEL))
        return w_cat + rest["token"][0, 0].astype(BF16), conv_g.transpose(1, 0, 2).reshape(CONV_K, 3 * WIDTH)

    def late_weights(after):
        w_out_g, w_gate_g, w_up_g, w_down_g = _split_wait("gather_rest_wait", _gather_plan, rest, after)
        return w_out_g.reshape(D_MODEL, D_MODEL), w_gate_g, w_up_g, w_down_g

    def start_reduction(stacks, landed, nms, tag):
        added = _add_halves(stacks, landed, place, "rs_add_" + tag)
        parts = [a[0] for a in added]
        started = _split_start("exchange_" + tag + "_start", _exchange_plan, parts,
                               [jax.ShapeDtypeStruct((3,) + p.shape[1:], p.dtype) for p in parts],
                               n_copies=3 * len(parts))
        return dict(own=[a[1] for a in added], started=started, tag=tag, names=nms)

    def finish_reduction(red, after, updates):
        landed = _split_wait("exchange_" + red["tag"] + "_wait", _exchange_plan, red["started"], after)
        if red["tag"] == "w_in":
            halves = [_sum_partials(red["own"][0], landed[0], "rs_sum_w_in", untiled_rows=True)]
        else:
            halves = _sum_many(red["own"], landed, "rs_sum_" + red["tag"])
        others = _share_halves(halves, "rs_share_" + red["tag"])
        return [upd(gm, go) for upd, gm, go in zip(updates, halves, others)]

    def transport_update(b):
        def upd(gm, go):
            res = _adam_big(big_w[b], gm, go, big_m[b], big_v[b], place, "adam_" + names[b])
            early_done.append(res[1])
            return [from_t(a, is_t[b]) for a in res]
        return upd

    early_done = []

    def w_in_update(gm, go):
        rows3 = lambda a: jnp.transpose(a, (2, 0, 1))
        res = _adam_untiled_rows(rows3(w_in), gm, go, rows3(m_w_in), rows3(v_w_in), place, "adam_w_in")
        return [jnp.transpose(a, (1, 2, 0)) for a in res]

    early = {}

    def early_grads_ready(g_out, g_gate, g_up, g_down):
        stacks = [g_out.reshape(N_CHIPS, D_MODEL // N_CHIPS, D_MODEL), g_gate, g_up, g_down]
        swap = _split_start("swap_early_start", _swap_plan, stacks,
                            [jax.ShapeDtypeStruct(s.shape[:2] + (s.shape[2] // 2,), s.dtype) for s in stacks],
                            n_copies=len(stacks))
        early.update(stacks=stacks, swap=swap)
        return swap["token"][0, 0]

    def early_grads_continue(after):
        landed = _split_wait("swap_early_wait", _swap_plan, early["swap"], after)
        early.update(start_reduction(early["swap"]["srcs_after"], landed, names[1:], "early"))
        return early["started"]["token"][0, 0]

    grad_x, g_cat, _, _, _, _, small = _local_step(
        x[0], loss_target[0], norm1_w + first["token"][0, 0], gdn_A_log[0], gdn_dt_bias[0],
        gdn_out_norm_w[0], fox_f_bias[0], fox_q_norm_w[0], fox_k_norm_w[0], norm2_w, final_norm_w.reshape(1, -1),
        first_weights, late_weights, early_grads_ready, early_grads_continue)

    g_in_stack = _uncat_grad(g_cat).reshape(N_CHIPS, D_IN // N_CHIPS, D_MODEL)
    swap_in = _split_start("swap_w_in_start", _swap_plan, [g_in_stack],
                           [jax.ShapeDtypeStruct((N_CHIPS, D_IN // N_CHIPS, D_MODEL // 2), F32)],
                           n_copies=1)

    order = ["norm1_w", "conv_w", "a_log", "dt_bias", "out_norm_w", "f_bias", "q_norm_w", "k_norm_w",
             "norm2_w", "final_w"]
    red = _allreduce_small(_pack([small[k] for k in order] + [small["loss"]], swap_in["token"][0, 0]))
    red_shapes = [(1, D_MODEL), (CONV_K, 3 * WIDTH), (1, HEADS), (1, HEADS), (1, HEAD_DIM), (1, HEADS),
                  (1, HEAD_DIM), (1, HEAD_DIM), (1, D_MODEL), (D_MODEL,), ()]
    red_list = _unpack(red, red_shapes)
    loss = red_list[-1]
    small_g = dict(zip(order, red_list[:-1]))
    shard_cols = 3 * WIDTH // N_CHIPS
    small_g["conv_w"] = lax.dynamic_slice_in_dim(small_g["conv_w"], own * shard_cols, shard_cols, axis=1)[None]
    small_gl = [small_g[k].reshape(w.shape) for k, w in zip(order, small_w)]
    s_delta, s_m, s_v = _adam_call(small_packed[0], _pack(small_gl), small_packed[1], small_packed[2], "adam_small")
    landed_in = _split_wait("swap_w_in_wait", _swap_plan, swap_in, s_delta)
    late = start_reduction(swap_in["srcs_after"], landed_in, names[:1], "w_in")
    big_upd = finish_reduction(early, late["started"]["token"], [transport_update(b) for b in range(1, 5)])
    big_upd = finish_reduction(late, early_done, [w_in_update]) + big_upd
    shapes = [w.shape for w in small_w]
    s_delta, s_m, s_v = _unpack(s_delta, shapes), _unpack(s_m, shapes), _unpack(s_v, shapes)

    big_pos = {1: 0, 9: 1, 11: 2, 12: 3, 13: 4}
    small_pos = {0: 0, 2: 1, 3: 2, 4: 3, 5: 4, 6: 5, 7: 6, 8: 7, 10: 8, 14: 9}
    grads, deltas, new_m, new_v = [], [], [], []
    for pos in range(15):
        if pos in big_pos:
            b = big_pos[pos]
            g, d, m2, v2 = big_upd[b]
            grads.append(g)
            deltas.append(d)
            new_m.append(m2)
            new_v.append(v2)
        else:
            s = small_pos[pos]
            grads.append(small_gl[s])
            deltas.append(s_delta[s])
            new_m.append(s_m[s])
            new_v.append(s_v[s])
    return (loss, grad_x[None], *grads, *deltas, *new_m, *new_v)
```

```python
import jax
import jax.numpy as jnp
import numpy as np
from jax import lax
from jax.experimental import pallas as pl
from jax.experimental.pallas import tpu as pltpu

F32 = jnp.float32
BF16 = jnp.bfloat16

D_MODEL = 1024
HEADS = 8
HEAD_DIM = 64
PAIRS = HEADS // 2
WIDTH = HEADS * HEAD_DIM
CHUNK = 64
CONV_K = 4
D_FF = 2816
FF_SHARD = D_FF // 4
EPS = 1e-6
SCALE = HEAD_DIM ** -0.5
LANES = 128
N_CHIPS = 4
D_IN = 4120
D_CAT = 4224
COL_SMALL = 4096 // LANES

ADAM_LR = 0.001
ADAM_B1 = 0.9
ADAM_B2 = 0.999
ADAM_EPS = 1e-08
ADAM_WD = 0.01
ADAM_STEP = 10

VMEM_LIMIT = 56 * 1024 * 1024
MESH = pl.DeviceIdType.MESH
HIGHEST = lax.Precision.HIGHEST


def _params(sem):
    return pltpu.CompilerParams(dimension_semantics=sem, vmem_limit_bytes=VMEM_LIMIT)


_CONTRACT = {"nn": ((1,), (0,)), "nt": ((1,), (1,)), "tn": ((0,), (0,))}


def _mm(a, b, *, dims, name, out_dtype=F32, add=None, tm=1024, tn=512, tk=512):
    if dims == "nn":
        (m, k), (k2, n) = a.shape, b.shape
    elif dims == "nt":
        (m, k), (n, k2) = a.shape, b.shape
    else:
        (k, m), (k2, n) = a.shape, b.shape
    assert k == k2, (a.shape, b.shape, dims)
    tm, tn, tk = min(tm, m), min(tn, n), min(tk, k)
    assert m % tm == 0 and n % tn == 0 and k % tk == 0, (m, n, k, tm, tn, tk)
    nk = k // tk
    a_spec = (pl.BlockSpec((tk, tm), lambda i, j, kk: (kk, i)) if dims == "tn"
              else pl.BlockSpec((tm, tk), lambda i, j, kk: (i, kk)))
    b_spec = (pl.BlockSpec((tn, tk), lambda i, j, kk: (j, kk)) if dims == "nt"
              else pl.BlockSpec((tk, tn), lambda i, j, kk: (kk, j)))
    o_spec = pl.BlockSpec((tm, tn), lambda i, j, kk: (i, j))
    contract = (_CONTRACT[dims], ((), ()))
    has_add = add is not None

    def body(*refs):
        a_ref, b_ref = refs[:2]
        add_ref = refs[2] if has_add else None
        o_ref = refs[3] if has_add else refs[2]
        part = lax.dot_general(a_ref[...].astype(BF16), b_ref[...].astype(BF16), contract,
                               preferred_element_type=F32)

        def finish(r):
            if has_add:
                r = r + add_ref[...].astype(F32)
            o_ref[...] = r.astype(out_dtype)

        if nk == 1:
            finish(part)
            return
        acc = refs[-1]
        kk = pl.program_id(2)

        @pl.when(kk == 0)
        def _():
            acc[...] = part

        @pl.when(kk > 0)
        def _():
            acc[...] += part

        @pl.when(kk == nk - 1)
        def _():
            finish(acc[...])

    ins = [a, b] + ([add] if has_add else [])
    in_specs = [a_spec, b_spec] + ([o_spec] if has_add else [])
    return pl.pallas_call(
        body, name=name, grid=(m // tm, n // tn, nk),
        in_specs=in_specs, out_specs=o_spec,
        out_shape=jax.ShapeDtypeStruct((m, n), out_dtype),
        scratch_shapes=[pltpu.VMEM((tm, tn), F32)] if nk > 1 else [],
        compiler_params=_params(("parallel", "parallel", "arbitrary")),
    )(*ins)


def _mm_blocks(a, b, *, name, grid, a_spec, b_spec, o_spec, out_shape, dims, n_sum=0, add=None, add_spec=None,
               epilogue=None, extra=(), n_acc=0):
    contract = (_CONTRACT[dims], ((), ()))
    has_add = add is not None
    n_in = 2 + has_add + len(extra)

    def body(*refs):
        a_ref, b_ref = refs[:2]
        dot = lambda x, y: lax.dot_general(x.astype(BF16), y.astype(BF16), contract, preferred_element_type=F32)
        if n_sum:
            r = dot(a_ref[0], b_ref[0])
            for s in range(1, n_sum):
                r = r + dot(a_ref[s], b_ref[s])
        else:
            r = dot(a_ref[...], b_ref[...])
        if has_add:
            r = r + refs[2][...].astype(F32)
        if epilogue is None:
            refs[-1][...] = r.astype(refs[-1].dtype)
        else:
            outs = epilogue(r, *[e[...] for e in refs[2 + has_add:n_in]])
            out_refs = refs[n_in:]
            n_plain = len(out_refs) - n_acc
            for o_ref, val in zip(out_refs[:n_plain], outs):
                o_ref[...] = val.astype(o_ref.dtype)
            if n_acc:
                @pl.when(pl.program_id(0) == 0)
                def _():
                    for o_ref in out_refs[n_plain:]:
                        o_ref[...] = jnp.zeros_like(o_ref)
                for o_ref, val in zip(out_refs[n_plain:], outs[n_plain:]):
                    o_ref[...] += val

    ins = [a, b] + ([add] if has_add else []) + [e[0] for e in extra]
    in_specs = [a_spec, b_spec] + ([add_spec] if has_add else []) + [e[1] for e in extra]
    sem = ("arbitrary" if n_acc else "parallel",) * len(grid)
    return pl.pallas_call(
        body, name=name, grid=grid, in_specs=in_specs, out_specs=o_spec, out_shape=out_shape,
        compiler_params=_params(sem),
    )(*ins)


def _tiles(fn, *, name, rows, tm, ncol=1, row_ins=(), col_consts=(), full_consts=(),
           row_outs=(), acc_outs=()):
    nt = rows // tm
    assert rows % tm == 0
    n_full, n_col, n_row = len(full_consts), len(col_consts), len(row_ins)
    n_ro, n_acc = len(row_outs), len(acc_outs)

    def body(*refs):
        ins = refs[:n_full + n_col + n_row]
        outs = refs[n_full + n_col + n_row:]
        i = pl.program_id(1)
        res = fn(pl.program_id(0), *[r[...] for r in ins])
        for r, v in zip(outs[:n_ro], res[:n_ro]):
            r[...] = v.astype(r.dtype)
        if n_acc:
            @pl.when(i == 0)
            def _():
                for r in outs[n_ro:]:
                    r[...] = jnp.zeros_like(r)
            for r, v in zip(outs[n_ro:], res[n_ro:]):
                r[...] += v

    in_specs = [pl.BlockSpec(a.shape, lambda j, i, nd=a.ndim: (0,) * nd) for a in full_consts]
    in_specs += [pl.BlockSpec((nr, w), lambda j, i, o=o: (0, o + j)) for (_, nr, w, o) in col_consts]
    in_specs += [pl.BlockSpec((tm, w), lambda j, i, o=o: (i, o + j)) for (_, w, o) in row_ins]
    out_specs = [pl.BlockSpec((tm, w), lambda j, i: (i, j)) for (w, _) in row_outs]
    out_specs += [pl.BlockSpec((nr, w), lambda j, i: (0, j)) for (nr, w) in acc_outs]
    out_shape = [jax.ShapeDtypeStruct((rows, w * ncol), dt) for (w, dt) in row_outs]
    out_shape += [jax.ShapeDtypeStruct((nr, w * ncol), F32) for (nr, w) in acc_outs]
    args = list(full_consts) + [c[0] for c in col_consts] + [r[0] for r in row_ins]
    out = pl.pallas_call(
        body, name=name, grid=(ncol, nt), in_specs=in_specs, out_specs=out_specs, out_shape=out_shape,
        compiler_params=_params(("parallel", "arbitrary")),
    )(*args)
    return out


def _rms(x, w):
    return x * lax.rsqrt(jnp.mean(x * x, axis=-1, keepdims=True) + EPS) * w


def _lane_lo(shape):
    return lax.broadcasted_iota(jnp.int32, shape, len(shape) - 1) < HEAD_DIM


def _pair_sum(x):
    lo = _lane_lo(x.shape)
    s0 = jnp.sum(jnp.where(lo, x, 0.0), axis=-1, keepdims=True)
    s1 = jnp.sum(jnp.where(lo, 0.0, x), axis=-1, keepdims=True)
    return jnp.where(lo, s0, s1)


def _head_col(x, lo, h):
    keep = lo if h == 0 else jnp.logical_not(lo)
    return jnp.max(jnp.where(keep, x, -jnp.inf), axis=-1, keepdims=True)


def _softplus(x):
    return jnp.maximum(x, 0.0) + jnp.log1p(jnp.exp(-jnp.abs(x)))


def _silu(x):
    return x * jax.nn.sigmoid(x)


def _dot(a, b, contract):
    return lax.dot_general(a.astype(BF16), b.astype(BF16), (contract, ((), ())),
                           preferred_element_type=F32)


def _dot32(a, b, contract):
    return lax.dot_general(a, b, (contract, ((), ())), precision=HIGHEST, preferred_element_type=F32)


def _bd(y):
    yy = jnp.concatenate([y, y], axis=0)
    r = lax.broadcasted_iota(jnp.int32, yy.shape, 0) < HEAD_DIM
    c = lax.broadcasted_iota(jnp.int32, yy.shape, 1) < HEAD_DIM
    return jnp.where(r == c, yy, 0.0)


def _pp(x, y):
    return _dot(x, _bd(y), _CONTRACT["nn"])


def _pp_nt(x, y):
    return _dot(x, _bd(y), _CONTRACT["nt"])


def _pp_tn(x, y):
    full = _dot(x, y, _CONTRACT["tn"])
    return jnp.where(_lane_lo((HEAD_DIM, LANES)), full[:HEAD_DIM], full[HEAD_DIM:])


def _gdn_masks():
    row = lax.broadcasted_iota(jnp.int32, (CHUNK, LANES), 0)
    col = lax.broadcasted_iota(jnp.int32, (CHUNK, LANES), 1) % HEAD_DIM
    return row, col


def _interleave(chains):
    live = list(chains)
    while live:
        for g in list(live):
            try:
                next(g)
            except StopIteration:
                live.remove(g)


def _gdn_forward(qkv, betax, gcx, grow, rows):
    nchunk = rows // CHUNK

    def body(q_ref, k_ref, v_ref, bx_ref, gx_ref, gr_ref, o_ref, ss_ref, ts_ref, state):
        n = pl.program_id(0)

        @pl.when(n == 0)
        def _():
            state[...] = jnp.zeros_like(state)

        row, col = _gdn_masks()
        incl, strict = col <= row, col < row

        def chain(p):
            lanes = pl.ds(p * LANES, LANES)
            q, k, v, bx, gx = q_ref[:, lanes], k_ref[:, lanes], v_ref[:, lanes], bx_ref[:, lanes], gx_ref[:, lanes]
            gr = gr_ref[0, p]
            glast = gx_ref[pl.ds(CHUNK - 1, 1), lanes]
            s = state[p]
            dm = jnp.where(incl, jnp.exp(jnp.minimum(gx - gr, 0.0)), 0.0)
            kb, vb, eg, qs = k * bx, v * bx, jnp.exp(gx), q * SCALE
            yield
            big_g, big_p = _pp_nt(kb, k), _pp_nt(qs, k)
            yield
            x = -jnp.where(strict, big_g * dm, 0.0)
            att = jnp.where(incl, big_p * dm, 0.0)
            tm = jnp.where(row == col, 1.0, 0.0) + x
            x = _pp(x, x)
            yield
            for _ in range(4):
                step, x = _pp(tm, x), _pp(x, x)
                yield
                tm = tm + step
            tm = tm + _pp(tm, x)
            yield
            u, w = _pp(tm, vb), _pp(tm, kb * eg)
            yield
            ws, qgs = _pp(w, s), _pp(qs * eg, s)
            yield
            vn = u - ws
            kd = k * jnp.exp(glast - gx)
            avn, upd = _pp(att, vn), _pp_tn(kd, vn)
            yield
            ss_ref[0, p] = s
            ts_ref[0, p] = tm
            o_ref[:, lanes] = qgs + avn
            state[p] = s * jnp.exp(glast) + upd

        _interleave([chain(p) for p in range(PAIRS)])

    blk = lambda j: pl.BlockSpec((CHUNK, WIDTH), lambda n, j=j: (n, j))
    sv = pl.BlockSpec((1, PAIRS, CHUNK, LANES), lambda n: (n, 0, 0, 0))
    return pl.pallas_call(
        body, name="gdn_fwd", grid=(nchunk,),
        in_specs=[blk(0), blk(1), blk(2), blk(0), blk(0),
                  pl.BlockSpec((1, PAIRS, 1, LANES), lambda n: (n, 0, 0, 0))],
        out_specs=[blk(0), sv, sv],
        out_shape=[jax.ShapeDtypeStruct((rows, WIDTH), F32),
                   jax.ShapeDtypeStruct((nchunk, PAIRS, CHUNK, LANES), F32),
                   jax.ShapeDtypeStruct((nchunk, PAIRS, CHUNK, LANES), F32)],
        scratch_shapes=[pltpu.VMEM((PAIRS, CHUNK, LANES), F32)],
        compiler_params=_params(("arbitrary",)),
    )(qkv, qkv, qkv, betax, gcx, grow)


def _gdn_backward(qkv, betax, gcx, grow, ssave, tsave, do, rows):
    nchunk = rows // CHUNK

    def body(q_ref, k_ref, v_ref, bx_ref, gx_ref, gr_ref, ss_ref, ts_ref, do_ref,
             dq_ref, dk_ref, dv_ref, dbx_ref, dgx_ref, dgr_ref, dstate):
        n = pl.program_id(0)

        @pl.when(n == 0)
        def _():
            dstate[...] = jnp.zeros_like(dstate)

        row, col = _gdn_masks()
        incl, strict = col <= row, col < row

        def chain(p):
            lanes = pl.ds(p * LANES, LANES)
            q, k, v, bx, gx = q_ref[:, lanes], k_ref[:, lanes], v_ref[:, lanes], bx_ref[:, lanes], gx_ref[:, lanes]
            gr = gr_ref[0, p]
            glast = gx_ref[pl.ds(CHUNK - 1, 1), lanes]
            s, tm, d_o = ss_ref[0, p], ts_ref[0, p], do_ref[:, lanes]
            ds_out = dstate[p]
            dm = jnp.where(incl, jnp.exp(jnp.minimum(gx - gr, 0.0)), 0.0)
            kb, vb, eg, qs = k * bx, v * bx, jnp.exp(gx), q * SCALE
            kbg, qg = kb * eg, qs * eg
            ed = jnp.exp(glast - gx)
            kd = k * ed
            eglast = jnp.exp(glast)
            yield
            big_g, big_p = _pp_nt(kb, k), _pp_nt(qs, k)
            u, w = _pp(tm, vb), _pp(tm, kbg)
            dqg, kds = _pp_nt(d_o, s), _pp(kd, ds_out)
            yield
            low = jnp.where(strict, big_g * dm, 0.0)
            att = jnp.where(incl, big_p * dm, 0.0)
            ws, atd = _pp(w, s), _pp_tn(att, d_o)
            yield
            vn = u - ws
            dvn = kds + atd
            dkd, datt_raw = _pp_nt(vn, ds_out), _pp_nt(d_o, vn)
            dw_neg, dvb = _pp_nt(dvn, s), _pp_tn(tm, dvn)
            dtm_a, wdv = _pp_nt(dvn, vb), _pp_tn(w, dvn)
            qgd = _pp_tn(qg, d_o)
            yield
            datt = jnp.where(incl, datt_raw, 0.0)
            dw = -dw_neg
            dtm_b, dkbg = _pp_nt(dw, kbg), _pp_tn(tm, dw)
            dbig_p = datt * dm
            dqs_a, dk_p = _pp(dbig_p, k), _pp_tn(dbig_p, qs)
            yield
            inner = _pp_tn(tm, dtm_a + dtm_b)
            yield
            dlow = jnp.where(strict, -_pp_nt(inner, tm), 0.0)
            yield
            dbig_g = dlow * dm
            dkb_a, dk_g = _pp(dbig_g, k), _pp_tn(dbig_g, kb)
            yield
            dkb = dkb_a + dkbg * eg
            dqs = dqs_a + dqg * eg
            dk = dk_g + dk_p + dkd * ed + dkb * bx
            z = dlow * low + datt * att
            kdterm = dkd * kd
            dglast = (jnp.sum(ds_out * s, axis=0, keepdims=True) * eglast
                      + jnp.sum(kdterm, axis=0, keepdims=True))
            dgx = dqg * qg + dkbg * kbg - kdterm
            dgx = dgx + jnp.where(col == 0, _pair_sum(z), 0.0)
            dgx = dgx + jnp.where(row == CHUNK - 1, dglast, 0.0)
            dq_ref[:, lanes] = dqs * SCALE
            dk_ref[:, lanes] = dk
            dv_ref[:, lanes] = dvb * bx
            dbx_ref[:, lanes] = dkb * k + dvb * v
            dgx_ref[:, lanes] = dgx
            dgr_ref[0, p] = -jnp.sum(z, axis=0, keepdims=True)
            dstate[p] = ds_out * eglast + qgd - wdv

        _interleave([chain(p) for p in range(PAIRS)])

    last = nchunk - 1
    blk = lambda j: pl.BlockSpec((CHUNK, WIDTH), lambda n, j=j: (last - n, j))
    sv = pl.BlockSpec((1, PAIRS, CHUNK, LANES), lambda n: (last - n, 0, 0, 0))
    gr_spec = pl.BlockSpec((1, PAIRS, 1, LANES), lambda n: (last - n, 0, 0, 0))
    wide = jax.ShapeDtypeStruct((rows, WIDTH), F32)
    return pl.pallas_call(
        body, name="gdn_bwd", grid=(nchunk,),
        in_specs=[blk(0), blk(1), blk(2), blk(0), blk(0), gr_spec, sv, sv, blk(0)],
        out_specs=[blk(0)] * 5 + [gr_spec],
        out_shape=[wide] * 5 + [jax.ShapeDtypeStruct((nchunk, PAIRS, 1, LANES), F32)],
        scratch_shapes=[pltpu.VMEM((PAIRS, CHUNK, LANES), F32)],
        compiler_params=_params(("arbitrary",)),
    )(qkv, qkv, qkv, betax, gcx, grow, ssave, tsave, do)


ATT_TQ = 256


def _att_scores(qh, kt, fk, diag):
    s = _dot(qh, kt, _CONTRACT["nt"]) - fk
    if diag:
        r = lax.broadcasted_iota(jnp.int32, s.shape, 0)
        c = lax.broadcasted_iota(jnp.int32, s.shape, 1)
        s = jnp.where(r >= c, s, -jnp.inf)
    return s


def _head_masks(n):
    lo = _lane_lo((n, LANES))
    return [lo, jnp.logical_not(lo)]


def _attention_forward(fqk, proj, frow, rows):
    tq = tk = min(ATT_TQ, rows)
    nq = rows // tq
    v_off = 3072 // LANES

    def body(q_ref, k_ref, v_ref, fr_ref, o_ref, lse_ref):
        qi = pl.program_id(1)
        q = q_ref[...] * SCALE
        keep_q, keep_k = _head_masks(tq), _head_masks(tk)
        qh = [jnp.where(keep_q[h], q, 0.0).astype(BF16) for h in range(2)]

        def tile(ki, carry, diag):
            k0 = pl.multiple_of(ki * tk, tk)
            kt = k_ref[pl.ds(k0, tk), :].astype(BF16)
            v_t = v_ref[pl.ds(k0, tk), :]
            out = [None, None]

            def chain(h):
                m, l, acc = carry[h]
                vt = jnp.where(keep_k[h], v_t, 0.0).astype(BF16)
                yield
                s = _att_scores(qh[h], kt, fr_ref[0, pl.ds(h, 1), pl.ds(k0, tk)], diag)
                yield
                m_new = jnp.maximum(m, jnp.max(s, axis=-1, keepdims=True))
                p = jnp.exp(s - m_new)
                alpha = jnp.exp(m - m_new)
                l = alpha * l + jnp.sum(p, axis=-1, keepdims=True)
                p_hi = p.astype(BF16)
                p_lo = p - p_hi.astype(F32)
                yield
                out[h] = (m_new, l, alpha * acc + _dot(p_hi, vt, _CONTRACT["nn"]) + _dot(p_lo, vt, _CONTRACT["nn"]))

            _interleave([chain(0), chain(1)])
            return tuple(out)

        one = (jnp.full((tq, 1), -jnp.inf, F32), jnp.zeros((tq, 1), F32), jnp.zeros((tq, LANES), F32))
        carry = lax.fori_loop(0, qi, lambda ki, c: tile(ki, c, False), (one, one))
        (m0, l0, acc0), (m1, l1, acc1) = tile(qi, carry, True)
        o_ref[...] = acc0 / l0 + acc1 / l1
        lse_ref[...] = jnp.where(keep_q[0], m0 + jnp.log(l0), m1 + jnp.log(l1))

    whole = lambda off: pl.BlockSpec((rows, LANES), lambda p, i, off=off: (0, off + p))
    qblk = lambda off: pl.BlockSpec((tq, LANES), lambda p, i, off=off: (i, off + p))
    wide = jax.ShapeDtypeStruct((rows, WIDTH), F32)
    return pl.pallas_call(
        body, name="fox_fwd", grid=(PAIRS, nq),
        in_specs=[qblk(0), whole(PAIRS), whole(v_off), pl.BlockSpec((1, 2, rows), lambda p, i: (p, 0, 0))],
        out_specs=[qblk(0), qblk(0)], out_shape=[wide, wide],
        compiler_params=_params(("parallel", "arbitrary")),
    )(fqk, fqk, proj, frow)


def _attention_backward(fqk, proj, frow, ao, lse, dao, rows):
    tq = tk = min(ATT_TQ, rows)
    nq = rows // tq
    v_off = 3072 // LANES

    def body(q_ref, k_ref, v_ref, fr_ref, o_ref, lse_ref, do_ref, dq_ref, dk_ref, dv_ref, dfr_ref):
        ki = pl.program_id(1)

        @pl.when(ki == 0)
        def _():
            dq_ref[...] = jnp.zeros_like(dq_ref)

        keep_q, keep_k = _head_masks(tq), _head_masks(tk)
        k_t = k_ref[...]
        kt = k_t.astype(BF16)
        vt = v_ref[...].astype(BF16)
        kh = [jnp.where(keep_k[h], k_t, 0.0).astype(BF16) for h in range(2)]
        fk = [fr_ref[0, pl.ds(h, 1), :] for h in range(2)]

        def tile(qi, carry, diag):
            dk, dv, df0, df1 = carry
            rows_q = pl.ds(pl.multiple_of(qi * tq, tq), tq)
            q, d_o, lse_t = q_ref[rows_q, :] * SCALE, do_ref[rows_q, :], lse_ref[rows_q, :]
            delta_x = _pair_sum(d_o.astype(BF16).astype(F32) * o_ref[rows_q, :])
            res = [None, None]

            def chain(h):
                qh = jnp.where(keep_q[h], q, 0.0).astype(BF16)
                doh = jnp.where(keep_q[h], d_o, 0.0).astype(BF16)
                lse_h, delta_h = _head_col(lse_t, keep_q[0], h), _head_col(delta_x, keep_q[0], h)
                yield
                s, dp = _att_scores(qh, kt, fk[h], diag), _dot(doh, vt, _CONTRACT["nt"])
                yield
                p = jnp.exp(s - lse_h)
                ds = p * (dp - delta_h)
                yield
                res[h] = (_dot(p, doh, _CONTRACT["tn"]), _dot(ds, qh, _CONTRACT["tn"]),
                          _dot(ds, kh[h], _CONTRACT["nn"]), jnp.sum(ds, axis=0, keepdims=True))

            _interleave([chain(0), chain(1)])
            (dv0, dk0, dq0, s0), (dv1, dk1, dq1, s1) = res
            dq_ref[rows_q, :] += (dq0 + dq1) * SCALE
            return dk + dk0 + dk1, dv + dv0 + dv1, df0 - s0, df1 - s1

        zero_kv = jnp.zeros((tk, LANES), F32)
        zero_f = jnp.zeros((1, tk), F32)
        carry = tile(ki, (zero_kv, zero_kv, zero_f, zero_f), True)
        dk, dv, df0, df1 = lax.fori_loop(ki + 1, nq, lambda qi, c: tile(qi, c, False), carry)
        dk_ref[...] = dk
        dv_ref[...] = dv.astype(dv_ref.dtype)
        dfr_ref[0, pl.ds(0, 1), :] = df0
        dfr_ref[0, pl.ds(1, 1), :] = df1

    whole = lambda off: pl.BlockSpec((rows, LANES), lambda p, i, off=off: (0, off + p))
    kblk = lambda off: pl.BlockSpec((tk, LANES), lambda p, i, off=off: (i, off + p))
    fr_spec = pl.BlockSpec((1, 2, tk), lambda p, i: (p, 0, i))
    wide = jax.ShapeDtypeStruct((rows, WIDTH), F32)
    return pl.pallas_call(
        body, name="fox_bwd", grid=(PAIRS, nq),
        in_specs=[whole(0), kblk(PAIRS), kblk(v_off), fr_spec, whole(0), whole(0), whole(0)],
        out_specs=[whole(0), kblk(0), kblk(0), fr_spec],
        out_shape=[wide, wide, jax.ShapeDtypeStruct((rows, WIDTH), BF16),
                   jax.ShapeDtypeStruct((PAIRS, 2, rows), F32)],
        compiler_params=_params(("parallel", "arbitrary")),
    )(fqk, fqk, proj, frow, ao, lse, dao)


def _lane_ids(shape):
    return lax.broadcasted_iota(jnp.int32, shape, len(shape) - 1)


def _gates_elem(a_log, dt_bias, f_bias, pre):
    lane = _lane_ids(pre.shape)
    beta = jax.nn.sigmoid(pre)
    g = -jnp.exp(a_log) * _softplus(pre + dt_bias)
    lf = -_softplus(-(pre + f_bias))
    return jnp.where(lane < 8, beta, jnp.where(lane < 16, g, jnp.where(lane < 24, lf, 0.0)))


def _tri_consts():
    r = np.arange(LANES)[:, None]
    c = np.arange(LANES)[None, :]
    full = (c <= r).astype(np.float32)
    chunked = full * ((r // CHUNK) == (c // CHUNK))
    return jnp.asarray(chunked), jnp.asarray(full)


def _cums_fwd(lc, lf, gates):
    rows = gates.shape[0]
    lane = _lane_ids((LANES, LANES))
    carry = jnp.zeros((1, LANES), F32)
    out = []
    for r in range(rows // LANES):
        blk = gates[r * LANES:(r + 1) * LANES]
        gc = _dot32(lc, blk, _CONTRACT["nn"])
        f = _dot32(lf, blk, _CONTRACT["nn"]) + carry
        carry = carry + jnp.sum(blk, axis=0, keepdims=True)
        out.append(jnp.where((lane >= 8) & (lane < 16), gc, jnp.where((lane >= 16) & (lane < 24), f, 0.0)))
    return jnp.concatenate(out, axis=0)


def _cums_bwd(lc, lf, dcums):
    rows = dcums.shape[0]
    lane = _lane_ids((LANES, LANES))
    is_g = (lane >= 8) & (lane < 16)
    is_f = (lane >= 16) & (lane < 24)
    carry = jnp.zeros((1, LANES), F32)
    out = [None] * (rows // LANES)
    for r in reversed(range(rows // LANES)):
        blk = dcums[r * LANES:(r + 1) * LANES]
        dg = jnp.where(is_g, blk, 0.0)
        df = jnp.where(is_f, blk, 0.0)
        out[r] = _dot32(lc, dg, _CONTRACT["tn"]) + _dot32(lf, df, _CONTRACT["tn"]) + carry
        carry = carry + jnp.sum(df, axis=0, keepdims=True)
    return jnp.concatenate(out, axis=0)


def _expand_consts():
    xb = np.zeros((LANES, WIDTH), np.float32)
    xg = np.zeros((LANES, WIDTH), np.float32)
    for h in range(HEADS):
        xb[h, h * HEAD_DIM:(h + 1) * HEAD_DIM] = 1.0
        xg[8 + h, h * HEAD_DIM:(h + 1) * HEAD_DIM] = 1.0
    return jnp.asarray(xb), jnp.asarray(xg)


def _shift_down(x, s):
    if s == 0:
        return x
    row = lax.broadcasted_iota(jnp.int32, x.shape, 0)
    return jnp.where(row >= s, pltpu.roll(x, s, 0), 0.0)


def _shift_up(x, s):
    if s == 0:
        return x
    n = x.shape[0]
    row = lax.broadcasted_iota(jnp.int32, x.shape, 0)
    return jnp.where(row < n - s, pltpu.roll(x, n - s, 0), 0.0)


def _row_of(cw, i):
    row = lax.broadcasted_iota(jnp.int32, cw.shape, 0)
    return jnp.sum(jnp.where(row == i, cw, 0.0), axis=0, keepdims=True)


def _conv(cw, x):
    c = jnp.zeros_like(x)
    for i in range(CONV_K):
        c = c + _row_of(cw, i) * _shift_down(x, CONV_K - 1 - i)
    return c


def _post_conv(is_qk, c):
    s = _silu(c)
    n = s * lax.rsqrt(_pair_sum(s * s) + EPS)
    return jnp.where(is_qk, n, s)


def _gdn_prep_fwd(col, cw, x):
    return (_post_conv(col < 2 * PAIRS, _conv(cw, x)),)


def _gdn_prep_bwd(is_qk, cw, x, dy):
    c = _conv(cw, x)
    _, vjp = jax.vjp(lambda cc: _post_conv(is_qk, cc), c)
    (dc,) = vjp(dy)
    dx = jnp.zeros_like(x)
    row = lax.broadcasted_iota(jnp.int32, cw.shape, 0)
    dcw = jnp.zeros(cw.shape, F32)
    for i in range(CONV_K):
        s = CONV_K - 1 - i
        dx = dx + _row_of(cw, i) * _shift_up(dc, s)
        dcw = dcw + jnp.where(row == i, jnp.sum(dc * _shift_down(x, s), axis=0, keepdims=True), 0.0)
    return dx, dcw


def _head_rms(w, x):
    return x * lax.rsqrt(_pair_sum(x * x) / HEAD_DIM + EPS) * w


def _cat_weights(w_in_t):
    tail = jnp.pad(w_in_t[4112:4120], ((0, D_CAT - D_IN), (0, 0)))
    return jnp.concatenate([w_in_t[:2048], w_in_t[2064:4112], w_in_t[2048:2064], tail], axis=0)


def _uncat_grad(g):
    return jnp.concatenate([g[:2048], g[4096:4112], g[2048:4096], g[4112:4120]], axis=0)


def _lanes_to_rowform(v8, rows):
    return v8.reshape(rows // CHUNK, CHUNK, HEADS).transpose(0, 2, 1).reshape(rows // CHUNK, PAIRS, 1, LANES)


def _rowform_to_lanes(v, rows):
    return v.reshape(rows // CHUNK, HEADS, CHUNK).transpose(0, 2, 1).reshape(rows, HEADS)


def _local_step(x, target, norm1_w, a_log, dt_bias, out_norm_w, f_bias, q_norm_w, k_norm_w,
                norm2_w, final_w, first_weights, late_weights, early_grads_ready, early_grads_continue):
    rows = x.shape[0]
    tm = min(512, rows)
    lc, lf = _tri_consts()
    xb, xg = _expand_consts()

    (h1,) = _tiles(lambda col, w, xx: (_rms(xx, w),), name="norm1", rows=rows, tm=tm,
                   full_consts=[norm1_w], row_ins=[(x, D_MODEL, 0)], row_outs=[(D_MODEL, BF16)])
    w_cat, conv_w = first_weights(h1)
    proj = _mm(h1, w_cat, dims="nt", name="in_proj", tn=1408, tk=1024)

    lane_pad = lambda v, off: jnp.pad(v.reshape(1, -1), ((0, 0), (off, LANES - off - v.size)))
    p_a, p_dt, p_fb = lane_pad(a_log, 8), lane_pad(dt_bias, 8), lane_pad(f_bias, 16)

    def gates_fwd(col, lcv, lfv, a, dt, fb, pre):
        gates = _gates_elem(a, dt, fb, pre)
        return gates, _cums_fwd(lcv, lfv, gates)

    gates, cums = _tiles(gates_fwd, name="gates", rows=rows, tm=rows,
                         full_consts=[lc, lf, p_a, p_dt, p_fb], row_ins=[(proj, LANES, COL_SMALL)],
                         row_outs=[(LANES, F32), (LANES, F32)])

    def expand_fwd(col, b, g, gt, cm):
        return (_dot32(gt, b, _CONTRACT["nn"]), _dot32(cm, g, _CONTRACT["nn"]))

    betax, gcx = _tiles(expand_fwd, name="expand", rows=rows, tm=tm, full_consts=[xb, xg],
                        row_ins=[(gates, LANES, 0), (cums, LANES, 0)],
                        row_outs=[(WIDTH, F32)] * 2)
    grow = _lanes_to_rowform(cums[:, 8:16], rows)
    frow = cums[:, 16:24].T.reshape(PAIRS, 2, rows)

    (qkv,) = _tiles(_gdn_prep_fwd, name="gdn_prep", rows=rows, tm=rows, ncol=3 * PAIRS,
                    col_consts=[(conv_w, CONV_K, LANES, 0)], row_ins=[(proj, LANES, 0)],
                    row_outs=[(LANES, F32)])
    o_gdn, ssave, tsave = _gdn_forward(qkv, betax, gcx, grow, rows)

    w_qk = jnp.concatenate([jnp.tile(q_norm_w.reshape(1, -1), (1, HEADS)),
                            jnp.tile(k_norm_w.reshape(1, -1), (1, HEADS))], axis=1)
    fox_off = 2048 // LANES
    (fqk,) = _tiles(lambda col, w, xx: (_head_rms(w, xx),), name="fox_prep", rows=rows, tm=rows, ncol=2 * PAIRS,
                    col_consts=[(w_qk, 1, LANES, 0)], row_ins=[(proj, LANES, fox_off)],
                    row_outs=[(LANES, F32)])
    ao, lse = _attention_forward(fqk, proj, frow, rows)

    w_on = jnp.tile(out_norm_w.reshape(1, -1), (1, 2))
    z_off, fg_off = 1536 // LANES, 3584 // LANES
    mix_g_fn = lambda w, o, z: _head_rms(w, o) * _silu(z)
    mix_f_fn = lambda a, g: a * jax.nn.sigmoid(g)
    (mix_g,) = _tiles(lambda col, w, o, z: (mix_g_fn(w, o, z),), name="mix_gdn", rows=rows, tm=rows, ncol=PAIRS,
                      full_consts=[w_on], row_ins=[(o_gdn, LANES, 0), (proj, LANES, z_off)],
                      row_outs=[(LANES, BF16)])
    (mix_f,) = _tiles(lambda col, a, g: (mix_f_fn(a, g),), name="mix_fox", rows=rows, tm=rows, ncol=PAIRS,
                      row_ins=[(ao, LANES, 0), (proj, LANES, fg_off)], row_outs=[(LANES, BF16)])
    mix = jnp.concatenate([mix_g, mix_f], axis=1)
    w_out, w_gate, w_up, w_down = late_weights(mix)
    t_rows, t_half = min(1024, rows), min(512, rows)
    n_rt = rows // t_rows
    row_blk = pl.BlockSpec((t_rows, D_MODEL), lambda i, n: (i, 0))
    half_blk = pl.BlockSpec((t_half, D_MODEL), lambda i, n: (i, 0))
    vec_blk = pl.BlockSpec((1, D_MODEL), lambda i, n: (0, 0))
    wide = lambda dt: jax.ShapeDtypeStruct((rows, D_MODEL), dt)
    x1, h2 = _mm_blocks(mix, w_out, name="out_proj_norm2", grid=(n_rt, 1), dims="nn",
                        a_spec=row_blk, b_spec=pl.BlockSpec((D_MODEL, D_MODEL), lambda i, n: (0, 0)),
                        o_spec=[row_blk, row_blk], out_shape=[wide(F32), wide(BF16)], add=x, add_spec=row_blk,
                        extra=[(norm2_w, vec_blk)], epilogue=lambda r, w: (r, _rms(r, w)))
    st_act = jax.ShapeDtypeStruct((N_CHIPS, rows, FF_SHARD), BF16)
    act_fn = lambda g, u: _silu(g) * u
    st_tile = pl.BlockSpec((None, t_rows, FF_SHARD), lambda i, j: (j, i, 0))
    h2_blk = pl.BlockSpec((t_rows, D_MODEL), lambda i, j: (i, 0))
    w_blk = pl.BlockSpec((None, FF_SHARD, D_MODEL), lambda i, j: (j, 0, 0))

    def gate_up_act(u, h, wg):
        g = _dot(h, wg, _CONTRACT["nt"])
        return g, u, act_fn(g, u)

    gate, up, act = _mm_blocks(h2, w_up, name="ffn_gate_up_act", grid=(n_rt, N_CHIPS), dims="nt",
                               a_spec=h2_blk, b_spec=w_blk, o_spec=[st_tile] * 3, out_shape=[st_act] * 3,
                               extra=[(h2, h2_blk), (w_gate, w_blk)], epilogue=gate_up_act)

    def final_fn(xx, tgt, w):
        y, vjp = jax.vjp(_rms, xx, w)
        err = y - tgt
        loss = 0.5 * jnp.sum(err * err) / D_MODEL
        dx, dw = vjp(err / D_MODEL)
        return dx, dx, jnp.full((1, LANES), loss, F32), dw

    dx2, dx2_b, loss, d_final_w = _mm_blocks(
        act, w_down, name="ffn_down_loss", grid=(rows // t_half, 1), dims="nn", n_sum=N_CHIPS,
        a_spec=pl.BlockSpec((N_CHIPS, t_half, FF_SHARD), lambda i, n: (0, i, 0)),
        b_spec=pl.BlockSpec((N_CHIPS, FF_SHARD, D_MODEL), lambda i, n: (0, 0, 0)),
        o_spec=[half_blk, half_blk, pl.BlockSpec((1, LANES), lambda i, n: (0, 0)), vec_blk],
        out_shape=[wide(F32), wide(BF16), jax.ShapeDtypeStruct((1, LANES), F32),
                   jax.ShapeDtypeStruct((1, D_MODEL), F32)],
        add=x1, add_spec=half_blk, extra=[(target, half_blk), (final_w, vec_blk)], epilogue=final_fn, n_acc=2)

    def act_bwd(d, g, u):
        _, vjp = jax.vjp(act_fn, g.astype(F32), u.astype(F32))
        return vjp(d)

    dgate, dup = _mm_blocks(dx2_b, w_down, name="d_act_gate_up", grid=(n_rt, N_CHIPS), dims="nt",
                            a_spec=pl.BlockSpec((t_rows, D_MODEL), lambda i, j: (i, 0)),
                            b_spec=pl.BlockSpec((None, FF_SHARD, D_MODEL), lambda i, j: (j, 0, 0)),
                            o_spec=[st_tile, st_tile], out_shape=[st_act, st_act],
                            extra=[(gate, st_tile), (up, st_tile)], epilogue=act_bwd)

    def g_ffn(d_st, other, name):
        return _mm_blocks(d_st, other, name=name, grid=(N_CHIPS, 1), dims="tn",
                          a_spec=pl.BlockSpec((None, rows, FF_SHARD), lambda j, n: (j, 0, 0)),
                          b_spec=pl.BlockSpec((rows, D_MODEL), lambda j, n: (0, 0)),
                          o_spec=pl.BlockSpec((None, FF_SHARD, D_MODEL), lambda j, n: (j, 0, 0)),
                          out_shape=jax.ShapeDtypeStruct((N_CHIPS, FF_SHARD, D_MODEL), BF16))

    g_down = g_ffn(act, dx2_b, "g_down")

    def norm_bwd(dh, xx, dres, w):
        _, vjp = jax.vjp(_rms, xx, w)
        dx, dw = vjp(dh)
        return dx + dres, dx + dres, dw

    def d_h2(d_st, w_st, name, add, **fused):
        return _mm_blocks(d_st, w_st, name=name, grid=(rows // t_half, 1), dims="nn", n_sum=N_CHIPS,
                          a_spec=pl.BlockSpec((N_CHIPS, t_half, FF_SHARD), lambda i, n: (0, i, 0)),
                          b_spec=pl.BlockSpec((N_CHIPS, FF_SHARD, D_MODEL), lambda i, n: (0, 0, 0)),
                          add=add, add_spec=half_blk, **fused)

    dh2_gate = d_h2(dgate, w_gate, "d_h2_gate", None, o_spec=half_blk, out_shape=wide(F32))
    dx1, dx1_b, d_norm2_w = d_h2(
        dup, w_up, "d_h2_up_norm2_bwd", dh2_gate, o_spec=[half_blk, half_blk, vec_blk],
        out_shape=[wide(F32), wide(BF16), jax.ShapeDtypeStruct((1, D_MODEL), F32)],
        extra=[(x1, half_blk), (dx2, half_blk), (norm2_w, vec_blk)], epilogue=norm_bwd, n_acc=1)
    st_blk = pl.BlockSpec((None, rows, FF_SHARD), lambda j, n: (j, 0, 0))
    g_blk = pl.BlockSpec((None, FF_SHARD, D_MODEL), lambda j, n: (j, 0, 0))
    g_shape = jax.ShapeDtypeStruct((N_CHIPS, FF_SHARD, D_MODEL), BF16)
    h2_whole = pl.BlockSpec((rows, D_MODEL), lambda j, n: (0, 0))
    g_gate, g_up = _mm_blocks(dgate, h2, name="g_gate_up", grid=(N_CHIPS, 1), dims="tn",
                              a_spec=st_blk, b_spec=h2_whole, o_spec=[g_blk, g_blk], out_shape=[g_shape, g_shape],
                              extra=[(dup, st_blk), (h2, h2_whole)],
                              epilogue=lambda r, du, h: (r, _dot(du, h, _CONTRACT["tn"])))
    dmix = _mm(dx1_b, w_out, dims="nt", name="d_mix", tn=D_MODEL, tk=1024)
    g_out = _mm(mix, dx1_b, dims="tn", name="g_out", tn=D_MODEL, tk=rows, out_dtype=BF16)
    w_on = w_on + early_grads_ready(g_out, g_gate, g_up, g_down)

    def mix_g_bwd(col, w, o, z, d):
        _, vjp = jax.vjp(mix_g_fn, w, o, z)
        dw, do_, dz = vjp(d)
        return do_, dz, dw

    do_gdn, dz, d_on = _tiles(mix_g_bwd, name="mix_gdn_bwd", rows=rows, tm=rows, ncol=PAIRS, full_consts=[w_on],
                              row_ins=[(o_gdn, LANES, 0), (proj, LANES, z_off), (dmix, LANES, 0)],
                              row_outs=[(LANES, F32), (LANES, BF16)], acc_outs=[(1, LANES)])

    def mix_f_bwd(col, a, g, d):
        _, vjp = jax.vjp(mix_f_fn, a, g)
        return vjp(d)

    dao, dfgate = _tiles(mix_f_bwd, name="mix_fox_bwd", rows=rows, tm=rows, ncol=PAIRS,
                         row_ins=[(ao, LANES, 0), (proj, LANES, fg_off), (dmix, LANES, PAIRS)],
                         row_outs=[(LANES, F32), (LANES, BF16)])

    dfq, dfk, dfv, dfrow = _attention_backward(fqk, proj, frow + early_grads_continue(dao), ao, lse, dao, rows)

    def fox_prep_bwd(col, w, xx, d):
        _, vjp = jax.vjp(_head_rms, w, xx)
        dw, dx = vjp(d)
        return dx, dw

    dfqk, d_wqk = [], []
    for part, d_n in enumerate((dfq, dfk)):
        dx_p, dw_p = _tiles(fox_prep_bwd, name="fox_prep_bwd_" + "qk"[part], rows=rows, tm=rows, ncol=PAIRS,
                            col_consts=[(w_qk, 1, LANES, part * PAIRS)],
                            row_ins=[(proj, LANES, fox_off + part * PAIRS), (d_n, LANES, 0)],
                            row_outs=[(LANES, BF16)], acc_outs=[(1, LANES)])
        dfqk.append(dx_p)
        d_wqk.append(dw_p)

    dq, dk, dv, dbetax, dgcx, dgrow = _gdn_backward(qkv, betax, gcx, grow, ssave, tsave, do_gdn, rows)
    dqkv, d_conv = [], []
    for part, d_n in enumerate((dq, dk, dv)):
        prep_bwd = lambda col, cw, xx, dy, is_qk=(part < 2): _gdn_prep_bwd(is_qk, cw, xx, dy)
        dx_p, dw_p = _tiles(prep_bwd, name="gdn_prep_bwd_" + "qkv"[part], rows=rows, tm=rows, ncol=PAIRS,
                            col_consts=[(conv_w, CONV_K, LANES, part * PAIRS)],
                            row_ins=[(proj, LANES, part * PAIRS), (d_n, LANES, 0)],
                            row_outs=[(LANES, BF16)], acc_outs=[(CONV_K, LANES)])
        dqkv.append(dx_p)
        d_conv.append(dw_p)
    d_conv = jnp.concatenate(d_conv, axis=1)

    def expand_bwd(col, b, g, db, dg):
        return (_dot32(db, b, _CONTRACT["nt"]), _dot32(dg, g, _CONTRACT["nt"]))

    dgates_b, dcums_g = _tiles(expand_bwd, name="expand_bwd", rows=rows, tm=tm, full_consts=[xb, xg],
                               row_ins=[(dbetax, WIDTH, 0), (dgcx, WIDTH, 0)],
                               row_outs=[(LANES, F32), (LANES, F32)])
    dcums_row = jnp.concatenate([jnp.zeros((rows, 8), F32), _rowform_to_lanes(dgrow, rows),
                                 dfrow.reshape(HEADS, rows).T, jnp.zeros((rows, LANES - 24), F32)], axis=1)

    def gates_bwd(col, lcv, lfv, a, dt, fb, pre, dgb, dcg, dcr):
        lane = _lane_ids(pre.shape)
        dgates = jnp.where(lane < 8, dgb, _cums_bwd(lcv, lfv, dcg + dcr))
        _, vjp = jax.vjp(_gates_elem, a, dt, fb, pre)
        da, ddt, dfb, dpre = vjp(dgates)
        return dpre, da, ddt, dfb

    dpre, d_a, d_dt, d_fb = _tiles(gates_bwd, name="gates_bwd", rows=rows, tm=rows,
                                   full_consts=[lc, lf, p_a, p_dt, p_fb],
                                   row_ins=[(proj, LANES, COL_SMALL), (dgates_b, LANES, 0), (dcums_g, LANES, 0),
                                            (dcums_row, LANES, 0)],
                                   row_outs=[(LANES, BF16)], acc_outs=[(1, LANES)] * 3)

    dproj = jnp.concatenate(dqkv + [dz] + dfqk + [dfv, dfgate, dpre], axis=1)
    grad_x, d_norm1_w = _mm_blocks(
        dproj, w_cat, name="d_h1_norm1_bwd", grid=(rows // t_half, 1), dims="nn",
        a_spec=pl.BlockSpec((t_half, D_CAT), lambda i, n: (i, 0)),
        b_spec=pl.BlockSpec((D_CAT, D_MODEL), lambda i, n: (0, 0)),
        o_spec=[half_blk, vec_blk], out_shape=[wide(F32), jax.ShapeDtypeStruct((1, D_MODEL), F32)],
        extra=[(x, half_blk), (dx1, half_blk), (norm1_w, vec_blk)],
        epilogue=lambda dh, xx, dres, w: norm_bwd(dh, xx, dres, w)[1:], n_acc=1)
    g_cat = _mm(dproj, h1, dims="tn", name="g_in", tm=1408, tn=D_MODEL, tk=rows)

    fold = lambda v: v.reshape(-1, HEAD_DIM).sum(axis=0)
    small = dict(
        loss=loss[0, 0],
        norm1_w=d_norm1_w, conv_w=d_conv, a_log=d_a[0, 8:16], dt_bias=d_dt[0, 8:16],
        out_norm_w=fold(d_on), f_bias=d_fb[0, 16:24], q_norm_w=fold(d_wqk[0]),
        k_norm_w=fold(d_wqk[1]), norm2_w=d_norm2_w, final_w=d_final_w)
    return grad_x, g_cat, g_out, g_gate, g_up, g_down, small


HBM_SPEC = pl.BlockSpec(memory_space=pltpu.HBM)


def _place():
    x, y, c = lax.axis_index("x"), lax.axis_index("y"), lax.axis_index("c")
    chips = [(1 - x, y), (x, 1 - y), (1 - x, 1 - y)]
    return x, y, c, 2 * x + y, (x, y, 1 - c), chips, [2 * cx + cy for cx, cy in chips]


def _remote(src, dst, send_sem, recv_sem, to):
    return pltpu.make_async_remote_copy(src_ref=src, dst_ref=dst, send_sem=send_sem, recv_sem=recv_sem,
                                        device_id=to, device_id_type=MESH)


SEM_SPEC =pl.BlockSpec(memory_space=pltpu.SEMAPHORE)
ANY_SPEC = pl.BlockSpec(memory_space=pl.ANY)
DATAFLOW = pltpu.SideEffectType.DATAFLOW_SIDE_EFFECTING


def _gather_plan(srcs, lands):
    x, y, c, own, sib, chips, chip_idx = _place()
    plan = []
    for src, land in zip(srcs, lands):
        for j, chip in enumerate(chips):
            plan.append((src, land.at[own], (*chip, c), land.at[chip_idx[j]]))
        plan.append((src, land.at[own], sib, land.at[own]))
    return plan


def _exchange_plan(srcs, lands):
    x, y, c, own, sib, chips, chip_idx = _place()
    plan = []
    for src, land in zip(srcs, lands):
        for j, chip in enumerate(chips):
            plan.append((src.at[chip_idx[j]], land.at[j], (*chip, c), land.at[j]))
    return plan


def _swap_plan(srcs, lands):
    x, y, c, own, sib, chips, chip_idx = _place()
    plan = []
    for src, land in zip(srcs, lands):
        h = src.shape[2] // 2
        plan.append((src.at[:, :, pl.ds(pl.multiple_of((1 - c) * h, LANES), h)], land, sib, land))
    return plan


def _in_proj_plan(srcs, lands):
    x, y, c, own, sib, chips, chip_idx = _place()
    (w, conv), (w_land, conv_land) = srcs, lands
    hw = w.shape[1] // 2
    half = lambda ref: ref.at[:, pl.ds(pl.multiple_of(c * hw, LANES), hw)]
    plan = []
    for j, chip in enumerate(chips):
        plan.append((half(w), half(w_land.at[own]), (*chip, c), half(w_land.at[chip_idx[j]])))
        plan.append((conv, conv_land.at[own], (*chip, c), conv_land.at[chip_idx[j]]))
    plan.append((w, w_land.at[own], sib, w_land.at[own]))
    plan.append((conv, conv_land.at[own], sib, conv_land.at[own]))
    return plan


def _forward_halves(landed):
    hw = landed.shape[2] // 2

    def body(in_ref, out_ref, send_sems, recv_sems):
        x, y, c, own, sib, chips, chip_idx = _place()
        half = lambda ref, hc: ref.at[:, pl.ds(pl.multiple_of(hc * hw, LANES), hw)]
        sent = [_remote(half(out_ref.at[chip_idx[j]], c), half(out_ref.at[chip_idx[j]], c),
                        send_sems.at[j], recv_sems.at[j], sib) for j in range(3)]
        for cp in sent:
            cp.start()
        for j in range(3):
            other = half(out_ref.at[chip_idx[j]], 1 - c)
            _remote(other, other, send_sems.at[j], recv_sems.at[j], sib).wait_recv()
        for cp in sent:
            cp.wait_send()

    return pl.pallas_call(
        body, name="gather_in_forward", out_shape=jax.ShapeDtypeStruct(landed.shape, landed.dtype),
        in_specs=[HBM_SPEC], out_specs=HBM_SPEC, input_output_aliases={0: 0},
        scratch_shapes=[pltpu.SemaphoreType.DMA((3,)), pltpu.SemaphoreType.DMA((3,))],
    )(landed)


def _split_start(name, plan_fn, srcs, land_shapes, n_copies, after=None):
    n = len(srcs)
    extra = [] if after is None else [after]

    def body(*refs):
        src_refs, land_refs = refs[:n], refs[n:2 * n]
        send_sems, recv_sems = refs[2 * n + len(extra)], refs[2 * n + len(extra) + 1]
        token = refs[-1]
        for k, (src, dst, to, _) in enumerate(plan_fn(src_refs, land_refs)):
            _remote(src, dst, send_sems.at[k], recv_sems.at[k], to).start()
        token[...] = jnp.zeros_like(token)

    lands = [pltpu.with_memory_space_constraint(lax.empty(s.shape, s.dtype), pltpu.HBM) for s in land_shapes]
    srcs = [pltpu.with_memory_space_constraint(s, pltpu.HBM) for s in srcs]
    out_shape = ([pltpu.SemaphoreType.DMA((n_copies,)), pltpu.SemaphoreType.DMA((n_copies,))]
                 + [pltpu.HBM(s.shape, s.dtype) for s in srcs] + [pltpu.HBM(s.shape, s.dtype) for s in land_shapes]
                 + [jax.ShapeDtypeStruct((8, LANES), F32)])
    res = pl.pallas_call(
        body, name=name, out_shape=out_shape,
        in_specs=[HBM_SPEC] * (2 * n) + [ANY_SPEC] * len(extra),
        out_specs=[SEM_SPEC, SEM_SPEC] + [HBM_SPEC] * (2 * n) + [pl.BlockSpec(memory_space=pltpu.VMEM)],
        input_output_aliases={i: 2 + i for i in range(2 * n)},
        compiler_params=pltpu.CompilerParams(has_side_effects=DATAFLOW),
    )(*srcs, *lands, *extra)
    return dict(sems=res[:2], srcs=res[2:2 + n], lands=res[2 + n:2 + 2 * n], token=res[-1], n=n)


def _split_wait(name, plan_fn, started, after):
    n = started["n"]

    def body(*refs):
        src_refs, land_refs = refs[:n], refs[n:2 * n]
        send_sems, recv_sems = refs[2 * n], refs[2 * n + 1]
        for k, (src, _, to, landed) in enumerate(plan_fn(src_refs, land_refs)):
            copy = _remote(src, landed, send_sems.at[k], recv_sems.at[k], to)
            copy.wait_send()
            copy.wait_recv()

    srcs, lands = started["srcs"], started["lands"]
    after = list(after) if isinstance(after, (list, tuple)) else [after]
    res = pl.pallas_call(
        body, name=name,
        out_shape=[pltpu.HBM(s.shape, s.dtype) for s in srcs] + [pltpu.HBM(s.shape, s.dtype) for s in lands],
        in_specs=[HBM_SPEC] * (2 * n) + [SEM_SPEC, SEM_SPEC] + [ANY_SPEC] * len(after),
        out_specs=[HBM_SPEC] * (2 * n),
        input_output_aliases={i: i for i in range(2 * n)},
        compiler_params=pltpu.CompilerParams(has_side_effects=DATAFLOW),
    )(*srcs, *lands, *started["sems"], *after)
    started["srcs_after"] = res[:n]
    return res[n:]


def _add_halves(stacks, landed, place, name):
    n = len(stacks)

    def body(place_ref, *refs):
        for a_ref, b_ref, o_ref, own_ref in zip(refs[:n], refs[n:2 * n], refs[2 * n::2], refs[2 * n + 1::2]):
            part = (a_ref[...].astype(F32) + b_ref[...].astype(F32)).astype(o_ref.dtype)
            o_ref[...] = part

            @pl.when(pl.program_id(0) == place_ref[1])
            def _(own_ref=own_ref, part=part):
                own_ref[...] = part[0]

    shapes = [l.shape[1:] for l in landed]
    slab = lambda s: pl.BlockSpec((1,) + s, lambda j, p: (j, 0, 0))
    out_shape, out_specs = [], []
    for l, s in zip(landed, shapes):
        out_shape += [jax.ShapeDtypeStruct(l.shape, BF16), jax.ShapeDtypeStruct(s, BF16)]
        out_specs += [slab(s), pl.BlockSpec(s, lambda j, p: (0, 0))]
    res = pl.pallas_call(
        body, name=name, out_shape=out_shape,
        grid_spec=pltpu.PrefetchScalarGridSpec(
            num_scalar_prefetch=1, grid=(N_CHIPS,),
            in_specs=[pl.BlockSpec((1,) + s, lambda j, p: (j, 0, p[0])) for s in shapes] + [slab(s) for s in shapes],
            out_specs=out_specs),
        compiler_params=_params(("arbitrary",)),
    )(place, *stacks, *landed)
    return [(res[2 * i], res[2 * i + 1]) for i in range(n)]


def _sum_many(own_parts, landed, name):
    n = len(own_parts)

    def body(*refs):
        for own_ref, a_ref, o_ref in zip(refs[:n], refs[n:2 * n], refs[2 * n:]):
            acc = own_ref[...].astype(F32)
            for s in range(3):
                acc = acc + a_ref[s].astype(F32)
            o_ref[...] = acc

    whole = lambda a: pl.BlockSpec(a.shape, lambda i, nd=a.ndim: (0,) * nd)
    return pl.pallas_call(
        body, name=name, grid=(1,), out_shape=[jax.ShapeDtypeStruct(o.shape, F32) for o in own_parts],
        in_specs=[whole(a) for a in own_parts] + [whole(a) for a in landed],
        out_specs=[whole(a) for a in own_parts], compiler_params=_params(("arbitrary",)),
    )(*own_parts, *landed)


def _sum_partials(own_part, landed, name, untiled_rows=False):
    _, h, cols = landed.shape
    tc = LANES if untiled_rows else cols

    def body(own_ref, a_ref, o_ref):
        acc = own_ref[...].astype(F32)
        for s in range(3):
            acc = acc + a_ref[s].astype(F32)
        if untiled_rows:
            o_ref[:, 0, :] = acc
        else:
            o_ref[...] = acc

    if untiled_rows:
        out_shape, out_spec = jax.ShapeDtypeStruct((h, 1, cols), F32), pl.BlockSpec((h, 1, tc), lambda i: (0, 0, i))
    else:
        out_shape, out_spec = jax.ShapeDtypeStruct((h, cols), F32), pl.BlockSpec((h, tc), lambda i: (0, i))
    return pl.pallas_call(
        body, name=name, out_shape=out_shape, grid=(cols // tc,),
        in_specs=[pl.BlockSpec((h, tc), lambda i: (0, i)), pl.BlockSpec((3, h, tc), lambda i: (0, 0, i))],
        out_specs=out_spec, compiler_params=_params(("arbitrary",)),
    )(own_part, landed)


def _share_halves(halves, name):
    n = len(halves)

    def body(*refs):
        ins, outs = refs[:n], refs[n:2 * n]
        send_sems, recv_sems = refs[2 * n:]
        x, y, c, own, sib, chips, chip_idx = _place()
        cps = [_remote(ins[i], outs[i], send_sems.at[i], recv_sems.at[i], sib) for i in range(n)]
        for cp in cps:
            cp.start()
        for cp in cps:
            cp.wait()

    return pl.pallas_call(
        body, name=name,
        out_shape=[jax.ShapeDtypeStruct(p.shape, p.dtype) for p in halves],
        in_specs=[HBM_SPEC] * n, out_specs=[HBM_SPEC] * n,
        scratch_shapes=[pltpu.SemaphoreType.DMA((n,)), pltpu.SemaphoreType.DMA((n,))],
    )(*halves)


def _allreduce_small(packed):
    rows = packed.shape[0]
    n_dev = 8

    def body(in_ref, out_ref, gath, send_sems, recv_sems):
        x, y, c = lax.axis_index("x"), lax.axis_index("y"), lax.axis_index("c")
        me = 4 * x + 2 * y + c
        gath[me] = in_ref[...]
        cps = []
        for k in range(1, n_dev):
            fx, fy, fc = (k >> 2) & 1, (k >> 1) & 1, k & 1
            to = (x ^ fx, y ^ fy, c ^ fc)
            cps.append(_remote(in_ref, gath.at[me], send_sems.at[k - 1], recv_sems.at[k - 1], to))
        for cp in cps:
            cp.start()
        for k in range(1, n_dev):
            fx, fy, fc = (k >> 2) & 1, (k >> 1) & 1, k & 1
            src = 4 * (x ^ fx) + 2 * (y ^ fy) + (c ^ fc)
            slot = gath.at[src]
            _remote(slot, slot, send_sems.at[k - 1], recv_sems.at[k - 1], (x, y, c)).wait_recv()
        for cp in cps:
            cp.wait_send()
        acc = gath[0]
        for d in range(1, n_dev):
            acc = acc + gath[d]
        out_ref[...] = acc

    vm = pl.BlockSpec(memory_space=pltpu.VMEM)
    return pl.pallas_call(
        body, name="allreduce_small", out_shape=jax.ShapeDtypeStruct(packed.shape, F32),
        in_specs=[vm], out_specs=vm,
        scratch_shapes=[pltpu.VMEM((n_dev, rows, LANES), F32),
                        pltpu.SemaphoreType.DMA((n_dev - 1,)), pltpu.SemaphoreType.DMA((n_dev - 1,))],
    )(packed)


def _adam(col, w, g, m, v):
    m2 = ADAM_B1 * m + (1.0 - ADAM_B1) * g
    v2 = ADAM_B2 * v + (1.0 - ADAM_B2) * (g * g)
    m_hat = m2 / (1.0 - ADAM_B1 ** ADAM_STEP)
    v_hat = v2 / (1.0 - ADAM_B2 ** ADAM_STEP)
    delta = -ADAM_LR * (m_hat / (jnp.sqrt(v_hat) + ADAM_EPS) + ADAM_WD * w)
    return delta, m2, v2


def _adam_call(w, g, m, v, name):
    rows, cols = w.shape
    tm = rows
    for cand in (256, 352, 176, 128, 64, 48, 16, 8):
        if rows % cand == 0:
            tm = cand
            break
    return _tiles(_adam, name=name, rows=rows, tm=tm,
                  row_ins=[(w, cols, 0), (g, cols, 0), (m, cols, 0), (v, cols, 0)],
                  row_outs=[(cols, F32)] * 3)


def _adam_big(w, g_mine, g_other, m, v, place, name):
    rows, cols = w.shape
    tc = cols // 2
    nt = cols // 2 // tc

    def body(place_ref, w_ref, gm_ref, go_ref, m_ref, v_ref, g_out, d_out, m_out, v_out):
        g = jnp.where(pl.program_id(0) == place_ref[0], gm_ref[...], go_ref[...])
        d, m2, v2 = _adam(None, w_ref[...], g, m_ref[...], v_ref[...])
        g_out[...] = g
        d_out[...] = d
        m_out[...] = m2
        v_out[...] = v2

    full = pl.BlockSpec((rows, tc), lambda hh, i, p: (0, hh * nt + i))
    half = pl.BlockSpec((rows, tc), lambda hh, i, p: (0, i))
    return pl.pallas_call(
        body, name=name, out_shape=[jax.ShapeDtypeStruct(w.shape, F32)] * 4,
        grid_spec=pltpu.PrefetchScalarGridSpec(
            num_scalar_prefetch=1, grid=(2, nt),
            in_specs=[full, half, half, full, full], out_specs=[full] * 4),
        compiler_params=_params(("arbitrary", "arbitrary")),
    )(place, w, g_mine, g_other, m, v)


def _adam_untiled_rows(w, g_mine, g_other, m, v, place, name):
    rows, _, cols = w.shape
    tc = 256
    nt = cols // 2 // tc
    rb = next(r for r in (206, 128, 103, rows) if rows % r == 0)

    def body(place_ref, w_ref, gm_ref, go_ref, m_ref, v_ref, g_out, d_out, m_out, v_out):
        g = jnp.where(pl.program_id(0) == place_ref[0], gm_ref[...], go_ref[...])
        d, m2, v2 = _adam(None, w_ref[...], g, m_ref[...], v_ref[...])
        g_out[...] = g
        d_out[...] = d
        m_out[...] = m2
        v_out[...] = v2

    full = pl.BlockSpec((rb, 1, tc), lambda hh, i, r, p: (r, 0, hh * nt + i))
    half = pl.BlockSpec((rb, 1, tc), lambda hh, i, r, p: (r, 0, i))
    return pl.pallas_call(
        body, name=name, out_shape=[jax.ShapeDtypeStruct(w.shape, F32)] * 4,
        grid_spec=pltpu.PrefetchScalarGridSpec(
            num_scalar_prefetch=1, grid=(2, nt, rows // rb),
            in_specs=[full, half, half, full, full], out_specs=[full] * 4),
        compiler_params=_params(("arbitrary", "arbitrary", "arbitrary")),
    )(place, w, g_mine, g_other, m, v)


def _pack(arrays, zero=None):
    flat = []
    for a in arrays:
        a = a.reshape(-1).astype(F32)
        if zero is not None:
            a = a + zero
        flat.append(jnp.pad(a, (0, (-a.size) % LANES)))
    out = jnp.concatenate(flat)
    out = jnp.pad(out, (0, (-out.size) % (8 * LANES)))
    return out.reshape(-1, LANES)


def _unpack(packed, shapes):
    flat = packed.reshape(-1)
    out, off = [], 0
    for s in shapes:
        size = int(np.prod(s))
        out.append(flat[off:off + size].reshape(s))
        off += size + (-size) % LANES
    return out


def kernel(x, norm1_w, w_in, gdn_conv_w, gdn_A_log, gdn_dt_bias, gdn_out_norm_w, fox_f_bias, fox_q_norm_w, fox_k_norm_w, w_out, norm2_w, w_ffn_gate, w_ffn_up, w_ffn_down, final_norm_w, loss_target, m_norm1_w, m_w_in, m_gdn_conv_w, m_gdn_A_log, m_gdn_dt_bias, m_gdn_out_norm_w, m_fox_f_bias, m_fox_q_norm_w, m_fox_k_norm_w, m_w_out, m_norm2_w, m_w_ffn_gate, m_w_ffn_up, m_w_ffn_down, m_final_norm_w, v_norm1_w, v_w_in, v_gdn_conv_w, v_gdn_A_log, v_gdn_dt_bias, v_gdn_out_norm_w, v_fox_f_bias, v_fox_q_norm_w, v_fox_k_norm_w, v_w_out, v_norm2_w, v_w_ffn_gate, v_w_ffn_up, v_w_ffn_down, v_final_norm_w):
    cx, cy, cc = lax.axis_index("x"), lax.axis_index("y"), lax.axis_index("c")
    own = 2 * cx + cy
    place = jnp.stack([cc, own]).astype(jnp.int32)

    names = ["w_in", "w_out", "w_gate", "w_up", "w_down"]
    is_t = [True, False, True, True, False]
    to_t = lambda a, t: a[0].T if t else a[0]
    from_t = lambda a, t: (a.T if t else a)[None]
    big_w = [to_t(a, t) for a, t in zip([w_in, w_out, w_ffn_gate, w_ffn_up, w_ffn_down], is_t)]
    big_m = [to_t(a, t) for a, t in zip([m_w_in, m_w_out, m_w_ffn_gate, m_w_ffn_up, m_w_ffn_down], is_t)]
    big_v = [to_t(a, t) for a, t in zip([v_w_in, v_w_out, v_w_ffn_gate, v_w_ffn_up, v_w_ffn_down], is_t)]
    shards = [big_w[0].astype(BF16)]
    small_w = [norm1_w, gdn_conv_w, gdn_A_log, gdn_dt_bias, gdn_out_norm_w, fox_f_bias, fox_q_norm_w,
               fox_k_norm_w, norm2_w, final_norm_w]
    small_m = [m_norm1_w, m_gdn_conv_w, m_gdn_A_log, m_gdn_dt_bias, m_gdn_out_norm_w, m_fox_f_bias,
               m_fox_q_norm_w, m_fox_k_norm_w, m_norm2_w, m_final_norm_w]
    small_v = [v_norm1_w, v_gdn_conv_w, v_gdn_A_log, v_gdn_dt_bias, v_gdn_out_norm_w, v_fox_f_bias,
               v_fox_q_norm_w, v_fox_k_norm_w, v_norm2_w, v_final_norm_w]
    first = _split_start("gather_in_start", _in_proj_plan, [shards[0], gdn_conv_w[0]],
                         [jax.ShapeDtypeStruct((N_CHIPS,) + shards[0].shape, BF16),
                          jax.ShapeDtypeStruct((N_CHIPS, CONV_K, 3 * WIDTH // N_CHIPS), F32)],
                         n_copies=8)
    small_packed = [_pack(p, first["token"][0, 0]) for p in (small_w, small_m, small_v)]
    shards += [(w + first["token"][0, 0]).astype(BF16) for w in big_w[1:]]
    rest = {}

    def first_weights(after):
        w_in_g, conv_g = _split_wait("gather_in_wait", _in_proj_plan, first, [after] + small_packed)
        w_in_g = _forward_halves(w_in_g)
        rest.update(_split_start("gather_rest_start", _gather_plan, shards[1:],
                                 [jax.ShapeDtypeStruct((N_CHIPS,) + s.shape, BF16) for s in shards[1:]],
                                 n_copies=4 * len(shards[1:]), after=w_in_g))
        w_cat = _cat_weights(w_in_g.reshape(D_IN, D_MODEL))
        return w_cat + rest["token"][0, 0].astype(BF16), conv_g.transpose(1, 0, 2).reshape(CONV_K, 3 * WIDTH)

    def late_weights(after):
        w_out_g, w_gate_g, w_up_g, w_down_g = _split_wait("gather_rest_wait", _gather_plan, rest, after)
        return w_out_g.reshape(D_MODEL, D_MODEL), w_gate_g, w_up_g, w_down_g

    def start_reduction(stacks, landed, nms, tag):
        added = _add_halves(stacks, landed, place, "rs_add_" + tag)
        parts = [a[0] for a in added]
        started = _split_start("exchange_" + tag + "_start", _exchange_plan, parts,
                               [jax.ShapeDtypeStruct((3,) + p.shape[1:], p.dtype) for p in parts],
                               n_copies=3 * len(parts))
        return dict(own=[a[1] for a in added], started=started, tag=tag, names=nms)

    def finish_reduction(red, after, updates):
        landed = _split_wait("exchange_" + red["tag"] + "_wait", _exchange_plan, red["started"], after)
        if red["tag"] == "w_in":
            halves = [_sum_partials(red["own"][0], landed[0], "rs_sum_w_in", untiled_rows=True)]
        else:
            halves = _sum_many(red["own"], landed, "rs_sum_" + red["tag"])
        others = _share_halves(halves, "rs_share_" + red["tag"])
        return [upd(gm, go) for upd, gm, go in zip(updates, halves, others)]

    def transport_update(b):
        def upd(gm, go):
            res = _adam_big(big_w[b], gm, go, big_m[b], big_v[b], place, "adam_" + names[b])
            early_done.append(res[1])
            return [from_t(a, is_t[b]) for a in res]
        return upd

    early_done = []

    def w_in_update(gm, go):
        rows3 = lambda a: jnp.transpose(a, (2, 0, 1))
        res = _adam_untiled_rows(rows3(w_in), gm, go, rows3(m_w_in), rows3(v_w_in), place, "adam_w_in")
        return [jnp.transpose(a, (1, 2, 0)) for a in res]

    early = {}

    def early_grads_ready(g_out, g_gate, g_up, g_down):
        stacks = [g_out.reshape(N_CHIPS, D_MODEL // N_CHIPS, D_MODEL), g_gate, g_up, g_down]
        swap = _split_start("swap_early_start", _swap_plan, stacks,
                            [jax.ShapeDtypeStruct(s.shape[:2] + (s.shape[2] // 2,), s.dtype) for s in stacks],
                            n_copies=len(stacks))
        early.update(stacks=stacks, swap=swap)
        return swap["token"][0, 0]

    def early_grads_continue(after):
        landed = _split_wait("swap_early_wait", _swap_plan, early["swap"], after)
        early.update(start_reduction(early["swap"]["srcs_after"], landed, names[1:], "early"))
        return early["started"]["token"][0, 0]

    grad_x, g_cat, _, _, _, _, small = _local_step(
        x[0], loss_target[0], norm1_w + first["token"][0, 0], gdn_A_log[0], gdn_dt_bias[0],
        gdn_out_norm_w[0], fox_f_bias[0], fox_q_norm_w[0], fox_k_norm_w[0], norm2_w, final_norm_w.reshape(1, -1),
        first_weights, late_weights, early_grads_ready, early_grads_continue)

    g_in_stack = _uncat_grad(g_cat).reshape(N_CHIPS, D_IN // N_CHIPS, D_MODEL)
    swap_in = _split_start("swap_w_in_start", _swap_plan, [g_in_stack],
                           [jax.ShapeDtypeStruct((N_CHIPS, D_IN // N_CHIPS, D_MODEL // 2), F32)],
                           n_copies=1)

    order = ["norm1_w", "conv_w", "a_log", "dt_bias", "out_norm_w", "f_bias", "q_norm_w", "k_norm_w",
             "norm2_w", "final_w"]
    red = _allreduce_small(_pack([small[k] for k in order] + [small["loss"]], swap_in["token"][0, 0]))
    red_shapes = [(1, D_MODEL), (CONV_K, 3 * WIDTH), (1, HEADS), (1, HEADS), (1, HEAD_DIM), (1, HEADS),
                  (1, HEAD_DIM), (1, HEAD_DIM), (1, D_MODEL), (D_MODEL,), ()]
    red_list = _unpack(red, red_shapes)
    loss = red_list[-1]
    small_g = dict(zip(order, red_list[:-1]))
    shard_cols = 3 * WIDTH // N_CHIPS
    small_g["conv_w"] = lax.dynamic_slice_in_dim(small_g["conv_w"], own * shard_cols, shard_cols, axis=1)[None]
    small_gl = [small_g[k].reshape(w.shape) for k, w in zip(order, small_w)]
    s_delta, s_m, s_v = _adam_call(small_packed[0], _pack(small_gl), small_packed[1], small_packed[2], "adam_small")
    landed_in = _split_wait("swap_w_in_wait", _swap_plan, swap_in, s_delta)
    late = start_reduction(swap_in["srcs_after"], landed_in, names[:1], "w_in")
    big_upd = finish_reduction(early, late["started"]["token"], [transport_update(b) for b in range(1, 5)])
    big_upd = finish_reduction(late, early_done, [w_in_update]) + big_upd
    shapes = [w.shape for w in small_w]
    s_delta, s_m, s_v = _unpack(s_delta, shapes), _unpack(s_m, shapes), _unpack(s_v, shapes)

    big_pos = {1: 0, 9: 1, 11: 2, 12: 3, 13: 4}
    small_pos = {0: 0, 2: 1, 3: 2, 4: 3, 5: 4, 6: 5, 7: 6, 8: 7, 10: 8, 14: 9}
    grads, deltas, new_m, new_v = [], [], [], []
    for pos in range(15):
        if pos in big_pos:
            b = big_pos[pos]
            g, d, m2, v2 = big_upd[b]
            grads.append(g)
            deltas.append(d)
            new_m.append(m2)
            new_v.append(v2)
        else:
            s = small_pos[pos]
            grads.append(small_gl[s])
            deltas.append(s_delta[s])
            new_m.append(s_m[s])
            new_v.append(s_v[s])
    return (loss, grad_x[None], *grads, *deltas, *new_m, *new_v)
```
